```python
import jax, jax.numpy as jnp
from jax import lax
import numpy as np

D_MODEL = 1024
BATCH = 8
SEQ = 4096
DEPTH = 1

DN_HEADS = 8
DN_HEAD_DIM = 128
DN_WIDTH = DN_HEADS * DN_HEAD_DIM
DN_CONV = 4
DN_CHUNK = 64
SB_HEADS = 8
SB_HEAD_DIM = 128
SB_WIDTH = SB_HEADS * SB_HEAD_DIM
SB_BLOCK = 128
D_FF = 2816
FFN_CONV = 3
EPS = 1e-6

DN_QKV_END = 3 * DN_WIDTH
DN_A_END = DN_QKV_END + DN_HEADS
DN_B_END = DN_A_END + DN_HEADS
DN_G_END = DN_B_END + DN_WIDTH
SB_QKV_END = DN_G_END + 3 * SB_WIDTH
IN_WIDTH = SB_QKV_END + 2 * D_MODEL
SPLIT_IDX = (DN_QKV_END, DN_A_END, DN_B_END, DN_G_END, SB_QKV_END)

kernel_name = 'hybrid_gdn_stickbreak_convffn'


def rmsnorm(x, w):
    xf = x.astype(jnp.float32)
    y = xf * lax.rsqrt(jnp.mean(xf * xf, axis=-1, keepdims=True) + EPS)
    return (y * w.astype(jnp.float32)).astype(x.dtype)


def l2norm(x):
    xf = x.astype(jnp.float32)
    return (xf * lax.rsqrt(jnp.sum(xf * xf, axis=-1, keepdims=True) + EPS)).astype(x.dtype)


def causal_dwconv(x, w):
    K = w.shape[0]
    T = x.shape[1]
    xp = jnp.pad(x, ((0, 0), (K - 1, 0), (0, 0)))
    y = w[0] * xp[:, 0:T]
    for i in range(1, K):
        y = y + w[i] * xp[:, i:i + T]
    return y


def to_heads(t, n_heads, head_dim):
    B, T, _ = t.shape
    return t.reshape(B, T, n_heads, head_dim).transpose(0, 2, 1, 3)


def gated_delta_rule(q, k, v, g, beta):
    out_dtype = v.dtype
    q, k, v, g, beta = (t.astype(jnp.float32) for t in (q, k, v, g, beta))
    B, H, T, dk = q.shape
    dv = v.shape[-1]
    C = DN_CHUNK
    N = T // C
    q = q * (dk ** -0.5)
    rs = lambda t: t.reshape(B, H, N, C, *t.shape[3:])
    q, k, v, g, beta = rs(q), rs(k), rs(v), rs(g), rs(beta)
    g = jnp.cumsum(g, axis=-1)
    k_beta = k * beta[..., None]
    v_beta = v * beta[..., None]
    lower = jnp.tril(jnp.ones((C, C), dtype=bool))
    strict = jnp.tril(jnp.ones((C, C), dtype=bool), -1)
    diff = g[..., :, None] - g[..., None, :]
    decay = jnp.where(lower, jnp.exp(jnp.where(lower, diff, 0.0)), 0.0)
    L = jnp.where(strict, jnp.einsum('bhncd,bhnsd->bhncs', k_beta, k) * decay, 0.0)
    rhs = jnp.concatenate([v_beta, k_beta * jnp.exp(g)[..., None]], axis=-1)
    sol = lax.linalg.triangular_solve(L, rhs, left_side=True, lower=True, unit_diagonal=True)
    u = sol[..., :dv]
    w = sol[..., dv:]
    a_qk = jnp.where(lower, jnp.einsum('bhncd,bhnsd->bhncs', q, k) * decay, 0.0)

    def step(S, xs):
        q_i, k_i, u_i, w_i, g_i, a_i = xs
        v_new = u_i - jnp.einsum('bhcd,bhde->bhce', w_i, S)
        o_i = (jnp.einsum('bhcd,bhde->bhce', q_i * jnp.exp(g_i)[..., None], S)
               + jnp.einsum('bhcs,bhse->bhce', a_i, v_new))
        g_last = g_i[..., -1]
        S = (S * jnp.exp(g_last)[..., None, None]
             + jnp.einsum('bhcd,bhce->bhde', k_i * jnp.exp(g_last[..., None] - g_i)[..., None], v_new))
        return S, o_i

    xs = tuple(jnp.moveaxis(t, 2, 0) for t in (q, k, u, w, g, a_qk))
    S0 = jnp.zeros((B, H, dk, dv), jnp.float32)
    _, o = lax.scan(step, S0, xs)
    o = jnp.moveaxis(o, 0, 2).reshape(B, H, T, dv)
    return o.astype(out_dtype)


def stick_breaking_attention(q, k, v):
    B, H, T, d = q.shape
    nb = T // SB_BLOCK
    scale = d ** -0.5
    qb = q.reshape(B, H, nb, SB_BLOCK, d).transpose(2, 0, 1, 3, 4)
    key_pos = jnp.arange(T)

    def block(args):
        q_blk, i = args
        z = jnp.einsum('bhqd,bhkd->bhqk', q_blk, k).astype(jnp.float32) * scale
        q_pos = i * SB_BLOCK + jnp.arange(SB_BLOCK)
        mask = key_pos[None, :] < q_pos[:, None]
        log_keep = jnp.where(mask, jax.nn.log_sigmoid(-z), 0.0)
        between = lax.cumsum(log_keep, axis=3, reverse=True) - log_keep
        log_a = jax.nn.log_sigmoid(z) + between
        a = jnp.where(mask, jnp.exp(log_a), 0.0)
        return jnp.einsum('bhqk,bhkd->bhqd', a.astype(v.dtype), v)

    o = lax.map(block, (qb, jnp.arange(nb)))
    return o.transpose(1, 2, 0, 3, 4).reshape(B, H, T, d)


def _fwd_setup_inputs(seed: int = 0) -> dict:
    key = jax.random.key(seed)
    ks = jax.random.split(key, 20)
    L = DEPTH
    nrm = lambda k, shape, fan_in: jax.random.normal(k, shape, jnp.float32) * (fan_in ** -0.5)
    gain = lambda k, shape: 1.0 + 0.02 * jax.random.normal(k, shape, jnp.float32)
    x = jax.random.normal(ks[0], (BATCH, SEQ, D_MODEL), jnp.float32)
    norm1_w = gain(ks[1], (L, D_MODEL))
    w_in = nrm(ks[2], (L, D_MODEL, IN_WIDTH), D_MODEL)
    dn_conv_w = nrm(ks[3], (L, DN_CONV, 3 * DN_WIDTH), DN_CONV)
    dn_A_log = jnp.log(jax.random.uniform(ks[4], (L, DN_HEADS), jnp.float32, 1.0, 16.0))
    dt = jnp.exp(jax.random.uniform(ks[5], (L, DN_HEADS), jnp.float32, np.log(1e-3), np.log(1e-1)))
    dn_dt_bias = dt + jnp.log(-jnp.expm1(-dt))
    dn_norm_w = gain(ks[6], (L, DN_HEAD_DIM))
    w_proj_dn = nrm(ks[7], (L, DN_WIDTH, D_MODEL), DN_WIDTH)
    w_proj_sb = nrm(ks[8], (L, SB_WIDTH, D_MODEL), SB_WIDTH)
    w_out = nrm(ks[9], (L, D_MODEL, D_MODEL), D_MODEL)
    norm2_w = gain(ks[10], (L, D_MODEL))
    ffn_w_up = nrm(ks[11], (L, D_MODEL, 2 * D_FF), D_MODEL)
    ffn_conv_w = nrm(ks[12], (L, FFN_CONV, 2 * D_FF), FFN_CONV)
    ffn_w_down = nrm(ks[13], (L, D_FF, D_MODEL), D_FF)
    norm_f_w = gain(ks[14], (D_MODEL,))
    return {'x': x, 'norm1_w': norm1_w, 'w_in': w_in, 'dn_conv_w': dn_conv_w,
            'dn_A_log': dn_A_log, 'dn_dt_bias': dn_dt_bias, 'dn_norm_w': dn_norm_w,
            'w_proj_dn': w_proj_dn, 'w_proj_sb': w_proj_sb, 'w_out': w_out,
            'norm2_w': norm2_w, 'ffn_w_up': ffn_w_up, 'ffn_conv_w': ffn_conv_w,
            'ffn_w_down': ffn_w_down, 'norm_f_w': norm_f_w}


def _fwd_reference(x, norm1_w, w_in, dn_conv_w, dn_A_log, dn_dt_bias, dn_norm_w,
              w_proj_dn, w_proj_sb, w_out, norm2_w, ffn_w_up, ffn_conv_w,
              ffn_w_down, norm_f_w):
    B, T, _ = x.shape
    for l in range(DEPTH):
        n1 = rmsnorm(x, norm1_w[l])
        h = n1 @ w_in[l]
        dn_qkv, dn_a, dn_b, dn_gate, sb_qkv, gate_logits = jnp.split(h, SPLIT_IDX, axis=-1)

        dn_qkv = jax.nn.silu(causal_dwconv(dn_qkv, dn_conv_w[l]))
        dq, dk, dv = jnp.split(dn_qkv, 3, axis=-1)
        dq = l2norm(to_heads(dq, DN_HEADS, DN_HEAD_DIM))
        dk = l2norm(to_heads(dk, DN_HEADS, DN_HEAD_DIM))
        dv = to_heads(dv, DN_HEADS, DN_HEAD_DIM)
        beta = jax.nn.sigmoid(dn_b.astype(jnp.float32)).transpose(0, 2, 1)
        g = (-jnp.exp(dn_A_log[l].astype(jnp.float32))
             * jax.nn.softplus(dn_a.astype(jnp.float32) + dn_dt_bias[l].astype(jnp.float32))).transpose(0, 2, 1)
        o_dn = gated_delta_rule(dq, dk, dv, g, beta)
        o_dn = rmsnorm(o_dn, dn_norm_w[l]).transpose(0, 2, 1, 3)
        o_dn = o_dn * jax.nn.silu(dn_gate.reshape(B, T, DN_HEADS, DN_HEAD_DIM))
        o_dn = o_dn.reshape(B, T, DN_WIDTH)

        sq, sk, sv = jnp.split(sb_qkv, 3, axis=-1)
        o_sb = stick_breaking_attention(to_heads(sq, SB_HEADS, SB_HEAD_DIM),
                                        to_heads(sk, SB_HEADS, SB_HEAD_DIM),
                                        to_heads(sv, SB_HEADS, SB_HEAD_DIM))
        o_sb = o_sb.transpose(0, 2, 1, 3).reshape(B, T, SB_WIDTH)

        gate_dn, gate_sb = jnp.split(jax.nn.sigmoid(gate_logits), 2, axis=-1)
        mixed = gate_dn * (o_dn @ w_proj_dn[l]) + gate_sb * (o_sb @ w_proj_sb[l])
        x = x + mixed @ w_out[l]

        n2 = rmsnorm(x, norm2_w[l])
        u = causal_dwconv(n2 @ ffn_w_up[l], ffn_conv_w[l])
        gate, up = jnp.split(u, 2, axis=-1)
        x = x + (jax.nn.silu(gate) * up) @ ffn_w_down[l]
    return rmsnorm(x, norm_f_w)


import jax as _jax
import jax.numpy as _jnp

TWIN_FORMAT = 'train_step'
FWD_PARAMS = ['x', 'norm1_w', 'w_in', 'dn_conv_w', 'dn_A_log', 'dn_dt_bias', 'dn_norm_w', 'w_proj_dn', 'w_proj_sb', 'w_out', 'norm2_w', 'ffn_w_up', 'ffn_conv_w', 'ffn_w_down', 'norm_f_w']
TWIN_WEIGHTS = ['norm1_w', 'w_in', 'dn_conv_w', 'dn_A_log', 'dn_dt_bias', 'dn_norm_w', 'w_proj_dn', 'w_proj_sb', 'w_out', 'norm2_w', 'ffn_w_up', 'ffn_conv_w', 'ffn_w_down', 'norm_f_w']
TWIN_DIFF_INPUT = 'x'
TWIN_INPUTS = ['x', 'norm1_w', 'w_in', 'dn_conv_w', 'dn_A_log', 'dn_dt_bias', 'dn_norm_w', 'w_proj_dn', 'w_proj_sb', 'w_out', 'norm2_w', 'ffn_w_up', 'ffn_conv_w', 'ffn_w_down', 'norm_f_w', 'loss_target', 'm_norm1_w', 'm_w_in', 'm_dn_conv_w', 'm_dn_A_log', 'm_dn_dt_bias', 'm_dn_norm_w', 'm_w_proj_dn', 'm_w_proj_sb', 'm_w_out', 'm_norm2_w', 'm_ffn_w_up', 'm_ffn_conv_w', 'm_ffn_w_down', 'm_norm_f_w', 'v_norm1_w', 'v_w_in', 'v_dn_conv_w', 'v_dn_A_log', 'v_dn_dt_bias', 'v_dn_norm_w', 'v_w_proj_dn', 'v_w_proj_sb', 'v_w_out', 'v_norm2_w', 'v_ffn_w_up', 'v_ffn_conv_w', 'v_ffn_w_down', 'v_norm_f_w']
TWIN_OUTPUTS = ['loss', 'grad_x', 'grad_norm1_w', 'grad_w_in', 'grad_dn_conv_w', 'grad_dn_A_log', 'grad_dn_dt_bias', 'grad_dn_norm_w', 'grad_w_proj_dn', 'grad_w_proj_sb', 'grad_w_out', 'grad_norm2_w', 'grad_ffn_w_up', 'grad_ffn_conv_w', 'grad_ffn_w_down', 'grad_norm_f_w', 'delta_norm1_w', 'delta_w_in', 'delta_dn_conv_w', 'delta_dn_A_log', 'delta_dn_dt_bias', 'delta_dn_norm_w', 'delta_w_proj_dn', 'delta_w_proj_sb', 'delta_w_out', 'delta_norm2_w', 'delta_ffn_w_up', 'delta_ffn_conv_w', 'delta_ffn_w_down', 'delta_norm_f_w', 'new_m_norm1_w', 'new_m_w_in', 'new_m_dn_conv_w', 'new_m_dn_A_log', 'new_m_dn_dt_bias', 'new_m_dn_norm_w', 'new_m_w_proj_dn', 'new_m_w_proj_sb', 'new_m_w_out', 'new_m_norm2_w', 'new_m_ffn_w_up', 'new_m_ffn_conv_w', 'new_m_ffn_w_down', 'new_m_norm_f_w', 'new_v_norm1_w', 'new_v_w_in', 'new_v_dn_conv_w', 'new_v_dn_A_log', 'new_v_dn_dt_bias', 'new_v_dn_norm_w', 'new_v_w_proj_dn', 'new_v_w_proj_sb', 'new_v_w_out', 'new_v_norm2_w', 'new_v_ffn_w_up', 'new_v_ffn_conv_w', 'new_v_ffn_w_down', 'new_v_norm_f_w']
TWIN_LEAF_KINDS = {'loss': 'loss', 'grad_x': 'grad_x', 'grad_norm1_w': 'grad_w', 'grad_w_in': 'grad_w', 'grad_dn_conv_w': 'grad_w', 'grad_dn_A_log': 'grad_w', 'grad_dn_dt_bias': 'grad_w', 'grad_dn_norm_w': 'grad_w', 'grad_w_proj_dn': 'grad_w', 'grad_w_proj_sb': 'grad_w', 'grad_w_out': 'grad_w', 'grad_norm2_w': 'grad_w', 'grad_ffn_w_up': 'grad_w', 'grad_ffn_conv_w': 'grad_w', 'grad_ffn_w_down': 'grad_w', 'grad_norm_f_w': 'grad_w', 'delta_norm1_w': 'delta_w', 'delta_w_in': 'delta_w', 'delta_dn_conv_w': 'delta_w', 'delta_dn_A_log': 'delta_w', 'delta_dn_dt_bias': 'delta_w', 'delta_dn_norm_w': 'delta_w', 'delta_w_proj_dn': 'delta_w', 'delta_w_proj_sb': 'delta_w', 'delta_w_out': 'delta_w', 'delta_norm2_w': 'delta_w', 'delta_ffn_w_up': 'delta_w', 'delta_ffn_conv_w': 'delta_w', 'delta_ffn_w_down': 'delta_w', 'delta_norm_f_w': 'delta_w', 'new_m_norm1_w': 'new_m', 'new_m_w_in': 'new_m', 'new_m_dn_conv_w': 'new_m', 'new_m_dn_A_log': 'new_m', 'new_m_dn_dt_bias': 'new_m', 'new_m_dn_norm_w': 'new_m', 'new_m_w_proj_dn': 'new_m', 'new_m_w_proj_sb': 'new_m', 'new_m_w_out': 'new_m', 'new_m_norm2_w': 'new_m', 'new_m_ffn_w_up': 'new_m', 'new_m_ffn_conv_w': 'new_m', 'new_m_ffn_w_down': 'new_m', 'new_m_norm_f_w': 'new_m', 'new_v_norm1_w': 'new_v', 'new_v_w_in': 'new_v', 'new_v_dn_conv_w': 'new_v', 'new_v_dn_A_log': 'new_v', 'new_v_dn_dt_bias': 'new_v', 'new_v_dn_norm_w': 'new_v', 'new_v_w_proj_dn': 'new_v', 'new_v_w_proj_sb': 'new_v', 'new_v_w_out': 'new_v', 'new_v_norm2_w': 'new_v', 'new_v_ffn_w_up': 'new_v', 'new_v_ffn_conv_w': 'new_v', 'new_v_ffn_w_down': 'new_v', 'new_v_norm_f_w': 'new_v'}


def _forward(args):
    return _fwd_reference(*[args[k] for k in FWD_PARAMS])


def _output_shape():
    out = _jax.eval_shape(lambda: _forward(_fwd_setup_inputs(0)))
    return out.shape, out.dtype

N_MICROBATCH = 1
ADAM_LR = 0.001
ADAM_B1 = 0.9
ADAM_B2 = 0.999
ADAM_EPS = 1e-08
ADAM_WD = 0.01
ADAM_STEP = 10
PER_EXAMPLE_BATCH_AXIS = {'x': 0, 'loss_target': 0}
SHARED_INPUTS = []
_WEIGHT_DTYPES = {'norm1_w': _jnp.float32, 'w_in': _jnp.float32, 'dn_conv_w': _jnp.float32, 'dn_A_log': _jnp.float32, 'dn_dt_bias': _jnp.float32, 'dn_norm_w': _jnp.float32, 'w_proj_dn': _jnp.float32, 'w_proj_sb': _jnp.float32, 'w_out': _jnp.float32, 'norm2_w': _jnp.float32, 'ffn_w_up': _jnp.float32, 'ffn_conv_w': _jnp.float32, 'ffn_w_down': _jnp.float32, 'norm_f_w': _jnp.float32}
MOMENT_SCALE = {'norm1_w': 1.368933e-01, 'w_in': 4.185042e-02, 'dn_conv_w': 4.375574e-02, 'dn_A_log': 2.638581e-01, 'dn_dt_bias': 2.496510e-01, 'dn_norm_w': 1.798221e-01, 'w_proj_dn': 5.638491e-02, 'w_proj_sb': 6.217381e-02, 'w_out': 8.433908e-02, 'norm2_w': 1.262048e-01, 'ffn_w_up': 5.217982e-02, 'ffn_conv_w': 5.257503e-02, 'ffn_w_down': 8.519670e-02, 'norm_f_w': 3.202137e+01}


def _to_microbatches(a, axis):
    t = _jnp.moveaxis(a, axis, 0)
    t = t.reshape((N_MICROBATCH, t.shape[0] // N_MICROBATCH) + t.shape[1:])
    return _jnp.moveaxis(t, 1, axis + 1)


def setup_inputs(seed: int = 0) -> dict:
    inp = _fwd_setup_inputs(seed)
    key = _jax.random.fold_in(_jax.random.key(seed), 7919)
    shape, _ = _output_shape()
    out = dict(inp)
    out["loss_target"] = _jax.random.normal(_jax.random.fold_in(key, 0), shape, _jnp.float32)
    for i, name in enumerate(TWIN_WEIGHTS):
        w = inp[name].astype(_jnp.float32)
        if MOMENT_SCALE is None:
            s = _jnp.sqrt(_jnp.mean(_jnp.square(w)) + 1e-30)
        else:
            s = MOMENT_SCALE[name]
        km, kv = _jax.random.split(_jax.random.fold_in(key, i + 1))
        out[name] = w
        out["m_" + name] = s * _jax.random.normal(km, w.shape, _jnp.float32)
        out["v_" + name] = (s * s) * _jax.random.uniform(kv, w.shape, _jnp.float32, 0.5, 1.5)
    if N_MICROBATCH > 1:
        for name, axis in PER_EXAMPLE_BATCH_AXIS.items():
            out[name] = _to_microbatches(out[name], axis)
    return {'x': out['x'], 'norm1_w': out['norm1_w'], 'w_in': out['w_in'], 'dn_conv_w': out['dn_conv_w'], 'dn_A_log': out['dn_A_log'], 'dn_dt_bias': out['dn_dt_bias'], 'dn_norm_w': out['dn_norm_w'], 'w_proj_dn': out['w_proj_dn'], 'w_proj_sb': out['w_proj_sb'], 'w_out': out['w_out'], 'norm2_w': out['norm2_w'], 'ffn_w_up': out['ffn_w_up'], 'ffn_conv_w': out['ffn_conv_w'], 'ffn_w_down': out['ffn_w_down'], 'norm_f_w': out['norm_f_w'], 'loss_target': out['loss_target'], 'm_norm1_w': out['m_norm1_w'], 'm_w_in': out['m_w_in'], 'm_dn_conv_w': out['m_dn_conv_w'], 'm_dn_A_log': out['m_dn_A_log'], 'm_dn_dt_bias': out['m_dn_dt_bias'], 'm_dn_norm_w': out['m_dn_norm_w'], 'm_w_proj_dn': out['m_w_proj_dn'], 'm_w_proj_sb': out['m_w_proj_sb'], 'm_w_out': out['m_w_out'], 'm_norm2_w': out['m_norm2_w'], 'm_ffn_w_up': out['m_ffn_w_up'], 'm_ffn_conv_w': out['m_ffn_conv_w'], 'm_ffn_w_down': out['m_ffn_w_down'], 'm_norm_f_w': out['m_norm_f_w'], 'v_norm1_w': out['v_norm1_w'], 'v_w_in': out['v_w_in'], 'v_dn_conv_w': out['v_dn_conv_w'], 'v_dn_A_log': out['v_dn_A_log'], 'v_dn_dt_bias': out['v_dn_dt_bias'], 'v_dn_norm_w': out['v_dn_norm_w'], 'v_w_proj_dn': out['v_w_proj_dn'], 'v_w_proj_sb': out['v_w_proj_sb'], 'v_w_out': out['v_w_out'], 'v_norm2_w': out['v_norm2_w'], 'v_ffn_w_up': out['v_ffn_w_up'], 'v_ffn_conv_w': out['v_ffn_conv_w'], 'v_ffn_w_down': out['v_ffn_w_down'], 'v_norm_f_w': out['v_norm_f_w']}


def _loss(weights, diff, rest, loss_target):
    with _jax.named_scope("forward"):
        args = {**rest, TWIN_DIFF_INPUT: diff, **{k: w.astype(_WEIGHT_DTYPES[k]) for k, w in weights.items()}}
        y = _forward(args)
    with _jax.named_scope("loss_head"):
        err = _jnp.square(y.astype(_jnp.float32) - loss_target)
        return 0.5 * _jnp.sum(_jnp.mean(err, axis=-1)) if err.ndim else 0.5 * err


def _adamw(w, g, m, v):
    m = ADAM_B1 * m + (1.0 - ADAM_B1) * g
    v = ADAM_B2 * v + (1.0 - ADAM_B2) * _jnp.square(g)
    m_hat = m / (1.0 - ADAM_B1 ** ADAM_STEP)
    v_hat = v / (1.0 - ADAM_B2 ** ADAM_STEP)
    delta = -ADAM_LR * (m_hat / (_jnp.sqrt(v_hat) + ADAM_EPS) + ADAM_WD * w)
    return delta, m, v


def reference(x, norm1_w, w_in, dn_conv_w, dn_A_log, dn_dt_bias, dn_norm_w, w_proj_dn, w_proj_sb, w_out, norm2_w, ffn_w_up, ffn_conv_w, ffn_w_down, norm_f_w, loss_target, m_norm1_w, m_w_in, m_dn_conv_w, m_dn_A_log, m_dn_dt_bias, m_dn_norm_w, m_w_proj_dn, m_w_proj_sb, m_w_out, m_norm2_w, m_ffn_w_up, m_ffn_conv_w, m_ffn_w_down, m_norm_f_w, v_norm1_w, v_w_in, v_dn_conv_w, v_dn_A_log, v_dn_dt_bias, v_dn_norm_w, v_w_proj_dn, v_w_proj_sb, v_w_out, v_norm2_w, v_ffn_w_up, v_ffn_conv_w, v_ffn_w_down, v_norm_f_w):
    given = dict(x=x, norm1_w=norm1_w, w_in=w_in, dn_conv_w=dn_conv_w, dn_A_log=dn_A_log, dn_dt_bias=dn_dt_bias, dn_norm_w=dn_norm_w, w_proj_dn=w_proj_dn, w_proj_sb=w_proj_sb, w_out=w_out, norm2_w=norm2_w, ffn_w_up=ffn_w_up, ffn_conv_w=ffn_conv_w, ffn_w_down=ffn_w_down, norm_f_w=norm_f_w, loss_target=loss_target, m_norm1_w=m_norm1_w, m_w_in=m_w_in, m_dn_conv_w=m_dn_conv_w, m_dn_A_log=m_dn_A_log, m_dn_dt_bias=m_dn_dt_bias, m_dn_norm_w=m_dn_norm_w, m_w_proj_dn=m_w_proj_dn, m_w_proj_sb=m_w_proj_sb, m_w_out=m_w_out, m_norm2_w=m_norm2_w, m_ffn_w_up=m_ffn_w_up, m_ffn_conv_w=m_ffn_conv_w, m_ffn_w_down=m_ffn_w_down, m_norm_f_w=m_norm_f_w, v_norm1_w=v_norm1_w, v_w_in=v_w_in, v_dn_conv_w=v_dn_conv_w, v_dn_A_log=v_dn_A_log, v_dn_dt_bias=v_dn_dt_bias, v_dn_norm_w=v_dn_norm_w, v_w_proj_dn=v_w_proj_dn, v_w_proj_sb=v_w_proj_sb, v_w_out=v_w_out, v_norm2_w=v_norm2_w, v_ffn_w_up=v_ffn_w_up, v_ffn_conv_w=v_ffn_conv_w, v_ffn_w_down=v_ffn_w_down, v_norm_f_w=v_norm_f_w)
    weights = {n: given[n] for n in TWIN_WEIGHTS}
    shared = {n: given[n] for n in SHARED_INPUTS}
    per_example = {n: given[n] for n in ['x']}
    grad_fn = _jax.value_and_grad(_loss, argnums=(0, 1))

    def one_microbatch(ex, loss_target):
        ex = dict(ex)
        diff = ex.pop(TWIN_DIFF_INPUT)
        return grad_fn(weights, diff, {**shared, **ex}, loss_target)

    if N_MICROBATCH == 1:
        loss, (grad_w, grad_x) = one_microbatch(per_example, given["loss_target"])
    else:
        def body(carry, xs):
            loss_sum, grad_sum = carry
            l_k, (gw_k, gx_k) = one_microbatch(xs[0], xs[1])
            with _jax.named_scope("update"):
                return (loss_sum + l_k, _jax.tree.map(_jnp.add, grad_sum, gw_k)), gx_k

        init = (_jnp.zeros((), _jnp.float32), _jax.tree.map(_jnp.zeros_like, weights))
        (loss, grad_w), grad_x = _jax.lax.scan(body, init, (per_example, given["loss_target"]))
    with _jax.named_scope("update"):
        delta_w, new_m, new_v = {}, {}, {}
        for n in TWIN_WEIGHTS:
            delta_w[n], new_m[n], new_v[n] = _adamw(weights[n], grad_w[n], given["m_" + n], given["v_" + n])
    return (loss, grad_x, *[grad_w[n] for n in TWIN_WEIGHTS], *[delta_w[n] for n in TWIN_WEIGHTS],
            *[new_m[n] for n in TWIN_WEIGHTS], *[new_v[n] for n in TWIN_WEIGHTS])
```

```python
import functools

import jax
import jax.numpy as jnp
from jax import lax
from jax.experimental import pallas as pl
from jax.experimental.pallas import tpu as pltpu

F32 = jnp.float32
BF16 = jnp.bfloat16

N_DEV = 8
D_MODEL = 1024
HEADS = 8
HEAD_DIM = 128
WIDTH = HEADS * HEAD_DIM
DN_CONV = 4
DN_CHUNK = 64
D_FF = 2816
FFN_CONV = 3
EPS = 1e-6
ATT_BLOCK = 256
SMALL_ROWS = 72

ADAM_LR = 0.001
ADAM_B1 = 0.9
ADAM_B2 = 0.999
ADAM_EPS = 1e-08
ADAM_WD = 0.01
ADAM_STEP = 10

VMEM_LIMIT = 48 * 1024 * 1024


def _params(sem=None, **kw):
    return pltpu.CompilerParams(dimension_semantics=sem, vmem_limit_bytes=VMEM_LIMIT, **kw)


def _tile(n, cap):
    if n <= cap:
        return n
    best = None
    for t in range(128, cap + 1, 128):
        if n % t == 0:
            best = t
    assert best is not None, (n, cap)
    return best


def _dot(a, b, dims):
    return lax.dot_general(a, b, ((dims[0], dims[1]), ((), ())), preferred_element_type=F32)


NN = ((1,), (0,))
NT = ((1,), (1,))
TN = ((0,), (0,))


def _dotb(a, b, dims):
    return _dot(a.astype(BF16), b.astype(BF16), dims)


def _split3(x):
    h1 = x.astype(BF16)
    r1 = x - h1.astype(F32)
    h2 = r1.astype(BF16)
    r2 = r1 - h2.astype(F32)
    return h1, h2, r2.astype(BF16)


def _dot_xr(a, b_exact, dims):
    a1, a2, a3 = _split3(a)
    return _dot(a1, b_exact, dims) + _dot(a2, b_exact, dims) + _dot(a3, b_exact, dims)


def _dot_xl(a_exact, b, dims):
    b1, b2, b3 = _split3(b)
    return _dot(a_exact, b1, dims) + _dot(a_exact, b2, dims) + _dot(a_exact, b3, dims)


def _dot3(a, b, dims):
    a1 = a.astype(BF16)
    a2 = (a - a1.astype(F32)).astype(BF16)
    b1 = b.astype(BF16)
    b2 = (b - b1.astype(F32)).astype(BF16)
    return _dot(a1, b1, dims) + (_dot(a1, b2, dims) + _dot(a2, b1, dims))


def _sigmoid(x):
    return 1.0 / (1.0 + jnp.exp(-x))


def _log1pexp_neg_abs(x):
    return jnp.log(1.0 + jnp.exp(-jnp.abs(x)))


def _iota(shape, dim):
    return lax.broadcasted_iota(jnp.int32, shape, dim)


def _matmul(a, b, mode, out_dtype, name, add=None):
    if mode == "nn":
        (m, k), (k2, n) = a.shape, b.shape
    elif mode == "nt":
        (m, k), (n, k2) = a.shape, b.shape
    else:
        (k, m), (k2, n) = a.shape, b.shape
    assert k == k2, (a.shape, b.shape, mode)
    tm, tn, tk = _tile(m, 512), _tile(n, 512), _tile(k, 1536)
    nk = k // tk
    dims = {"nn": NN, "nt": NT, "tn": TN}[mode]

    def body(*refs):
        if add is None:
            a_ref, b_ref, o_ref, acc_ref = refs
        else:
            a_ref, b_ref, add_ref, o_ref, acc_ref = refs
        kk = pl.program_id(2)

        @pl.when(kk == 0)
        def _():
            acc_ref[...] = jnp.zeros_like(acc_ref)

        acc_ref[...] += _dotb(a_ref[...], b_ref[...], dims)

        @pl.when(kk == nk - 1)
        def _():
            r = acc_ref[...]
            if add is not None:
                r = r + add_ref[...].astype(F32)
            o_ref[...] = r.astype(out_dtype)

    if mode == "nn":
        specs = [pl.BlockSpec((tm, tk), lambda i, j, l: (i, l)), pl.BlockSpec((tk, tn), lambda i, j, l: (l, j))]
    elif mode == "nt":
        specs = [pl.BlockSpec((tm, tk), lambda i, j, l: (i, l)), pl.BlockSpec((tn, tk), lambda i, j, l: (j, l))]
    else:
        specs = [pl.BlockSpec((tk, tm), lambda i, j, l: (l, i)), pl.BlockSpec((tk, tn), lambda i, j, l: (l, j))]
    args = [a, b]
    if add is not None:
        specs.append(pl.BlockSpec((tm, tn), lambda i, j, l: (i, j)))
        args.append(add)
    return pl.pallas_call(
        body, name=name,
        out_shape=jax.ShapeDtypeStruct((m, n), out_dtype),
        grid=(m // tm, n // tn, nk),
        in_specs=specs,
        out_specs=pl.BlockSpec((tm, tn), lambda i, j, l: (i, j)),
        scratch_shapes=[pltpu.VMEM((tm, tn), F32)],
        compiler_params=_params(("parallel", "parallel", "arbitrary")),
    )(*args)


def _rmsnorm_fwd(x, w, name):
    t, d = x.shape
    tr = _tile(t, 512)

    def body(x_ref, w_ref, o_ref):
        xv = x_ref[...]
        r = lax.rsqrt(jnp.mean(xv * xv, axis=1, keepdims=True) + EPS)
        o_ref[...] = (xv * r * w_ref[...]).astype(BF16)

    return pl.pallas_call(
        body, name=name,
        out_shape=jax.ShapeDtypeStruct((t, d), BF16),
        grid=(t // tr,),
        in_specs=[pl.BlockSpec((tr, d), lambda i: (i, 0)), pl.BlockSpec((1, d), lambda i: (0, 0))],
        out_specs=pl.BlockSpec((tr, d), lambda i: (i, 0)),
        compiler_params=_params(("parallel",)),
    )(x, w)


def _rmsnorm_bwd(dn, x, w, dres, name):
    t, d = x.shape
    tr = _tile(t, 512)

    def body(dn_ref, x_ref, w_ref, dres_ref, dx_ref, dw_ref):
        i = pl.program_id(0)
        xv = x_ref[...]
        g = dn_ref[...].astype(F32)
        r = lax.rsqrt(jnp.mean(xv * xv, axis=1, keepdims=True) + EPS)
        xh = xv * r
        dxh = g * w_ref[...]
        dx = r * (dxh - xh * jnp.mean(dxh * xh, axis=1, keepdims=True))
        dx_ref[...] = dres_ref[...] + dx

        @pl.when(i == 0)
        def _():
            dw_ref[...] = jnp.zeros_like(dw_ref)

        dw_ref[...] += jnp.sum(g * xh, axis=0, keepdims=True)

    return pl.pallas_call(
        body, name=name,
        out_shape=(jax.ShapeDtypeStruct((t, d), F32), jax.ShapeDtypeStruct((1, d), F32)),
        grid=(t // tr,),
        in_specs=[pl.BlockSpec((tr, d), lambda i: (i, 0)), pl.BlockSpec((tr, d), lambda i: (i, 0)),
                  pl.BlockSpec((1, d), lambda i: (0, 0)), pl.BlockSpec((tr, d), lambda i: (i, 0))],
        out_specs=(pl.BlockSpec((tr, d), lambda i: (i, 0)), pl.BlockSpec((1, d), lambda i: (0, 0))),
        compiler_params=_params(("arbitrary",)),
    )(dn, x, w, dres)


def _final_loss(x2, target, w, name):
    t, d = x2.shape
    tr = _tile(t, 512)

    def body(x_ref, t_ref, w_ref, dx_ref, dw_ref, loss_ref):
        i = pl.program_id(0)
        xv = x_ref[...]
        r = lax.rsqrt(jnp.mean(xv * xv, axis=1, keepdims=True) + EPS)
        xh = xv * r
        err = xh * w_ref[...] - t_ref[...]
        dy = err * (1.0 / d)
        dxh = dy * w_ref[...]
        dx_ref[...] = r * (dxh - xh * jnp.mean(dxh * xh, axis=1, keepdims=True))

        @pl.when(i == 0)
        def _():
            dw_ref[...] = jnp.zeros_like(dw_ref)
            loss_ref[...] = jnp.zeros_like(loss_ref)

        dw_ref[...] += jnp.sum(dy * xh, axis=0, keepdims=True)
        row = jnp.sum(err * err, axis=1, keepdims=True) * (0.5 / d)
        loss_ref[...] += jnp.sum(row, axis=0, keepdims=True)

    return pl.pallas_call(
        body, name=name,
        out_shape=(jax.ShapeDtypeStruct((t, d), F32), jax.ShapeDtypeStruct((1, d), F32),
                   jax.ShapeDtypeStruct((1, 1), F32)),
        grid=(t // tr,),
        in_specs=[pl.BlockSpec((tr, d), lambda i: (i, 0)), pl.BlockSpec((tr, d), lambda i: (i, 0)),
                  pl.BlockSpec((1, d), lambda i: (0, 0))],
        out_specs=(pl.BlockSpec((tr, d), lambda i: (i, 0)), pl.BlockSpec((1, d), lambda i: (0, 0)),
                   pl.BlockSpec((1, 1), lambda i: (0, 0))),
        compiler_params=_params(("arbitrary",)),
    )(x2, target, w)


def _shift_down(cur, prev, k, row):
    r = pltpu.roll(cur, k, 0)
    for m in range(k):
        r = jnp.where(row == m, prev[8 - k + m:8 - k + m + 1, :], r)
    return r


def _shift_up(cur, nxt, k, row, tr):
    r = pltpu.roll(cur, tr - k, 0)
    for m in range(k):
        r = jnp.where(row == tr - k + m, nxt[m:m + 1, :], r)
    return r


def _conv_taps(cur, prev, w, ntaps, row):
    taps = [cur if i == ntaps - 1 else _shift_down(cur, prev, ntaps - 1 - i, row) for i in range(ntaps)]
    y = w[0:1, :] * taps[0]
    for i in range(1, ntaps):
        y = y + w[i:i + 1, :] * taps[i]
    return taps, y


def _conv_bwd_data(dc, w, ntaps, out_dtype, name):
    t, ch = dc.shape
    tr, tc = _tile(t, 512), _tile(ch, 512)
    nrow8 = t // 8
    last = t // tr - 1

    def body(cur_ref, nxt_ref, w_ref, o_ref):
        i = pl.program_id(0)
        cur = cur_ref[...]
        nxt = jnp.where(i == last, 0.0, nxt_ref[...])
        row = _iota(cur.shape, 0)
        wv = w_ref[...]
        y = wv[ntaps - 1:ntaps, :] * cur
        for k in range(1, ntaps):
            y = y + wv[ntaps - 1 - k:ntaps - k, :] * _shift_up(cur, nxt, k, row, tr)
        o_ref[...] = y.astype(out_dtype)

    return pl.pallas_call(
        body, name=name,
        out_shape=jax.ShapeDtypeStruct((t, ch), out_dtype),
        grid=(t // tr, ch // tc),
        in_specs=[pl.BlockSpec((tr, tc), lambda i, j: (i, j)),
                  pl.BlockSpec((8, tc), lambda i, j: (jnp.minimum((i + 1) * (tr // 8), nrow8 - 1), j)),
                  pl.BlockSpec((ntaps, tc), lambda i, j: (0, j))],
        out_specs=pl.BlockSpec((tr, tc), lambda i, j: (i, j)),
        compiler_params=_params(("parallel", "parallel")),
    )(dc, dc, w)


def _ffn_act_fwd(upre, cw, name):
    t = upre.shape[0]
    tr, tc = _tile(t, 512), 256
    nj = D_FF // tc

    def body(g_ref, gp_ref, u_ref, up_ref, wg_ref, wu_ref, o_ref):
        i = pl.program_id(0)
        row = _iota((tr, tc), 0)
        gp = jnp.where(i == 0, 0.0, gp_ref[...])
        up = jnp.where(i == 0, 0.0, up_ref[...])
        _, gc = _conv_taps(g_ref[...], gp, wg_ref[...], FFN_CONV, row)
        _, uc = _conv_taps(u_ref[...], up, wu_ref[...], FFN_CONV, row)
        o_ref[...] = (gc * _sigmoid(gc) * uc).astype(BF16)

    prev = lambda off: (lambda i, j: (jnp.maximum(i * (tr // 8) - 1, 0), j + off))
    return pl.pallas_call(
        body, name=name,
        out_shape=jax.ShapeDtypeStruct((t, D_FF), BF16),
        grid=(t // tr, nj),
        in_specs=[pl.BlockSpec((tr, tc), lambda i, j: (i, j)), pl.BlockSpec((8, tc), prev(0)),
                  pl.BlockSpec((tr, tc), lambda i, j: (i, j + nj)), pl.BlockSpec((8, tc), prev(nj)),
                  pl.BlockSpec((FFN_CONV, tc), lambda i, j: (0, j)),
                  pl.BlockSpec((FFN_CONV, tc), lambda i, j: (0, j + nj))],
        out_specs=pl.BlockSpec((tr, tc), lambda i, j: (i, j)),
        compiler_params=_params(("parallel", "parallel")),
    )(upre, upre, upre, upre, cw, cw)


def _ffn_act_bwd(dact, upre, cw, name):
    t = upre.shape[0]
    tr, tc = _tile(t, 512), 256
    nj = D_FF // tc

    def body(da_ref, g_ref, gp_ref, u_ref, up_ref, wg_ref, wu_ref, dg_ref, du_ref, dwg_ref, dwu_ref):
        i = pl.program_id(1)
        row = _iota((tr, tc), 0)
        gp = jnp.where(i == 0, 0.0, gp_ref[...])
        up = jnp.where(i == 0, 0.0, up_ref[...])
        gt, gc = _conv_taps(g_ref[...], gp, wg_ref[...], FFN_CONV, row)
        ut, uc = _conv_taps(u_ref[...], up, wu_ref[...], FFN_CONV, row)
        da = da_ref[...].astype(F32)
        sg = _sigmoid(gc)
        dgc = da * uc * (sg * (1.0 + gc * (1.0 - sg)))
        duc = da * (gc * sg)
        dg_ref[...] = dgc
        du_ref[...] = duc

        @pl.when(i == 0)
        def _():
            dwg_ref[...] = jnp.zeros_like(dwg_ref)
            dwu_ref[...] = jnp.zeros_like(dwu_ref)

        for k in range(FFN_CONV):
            dwg_ref[k:k + 1, :] += jnp.sum(dgc * gt[k], axis=0, keepdims=True)
            dwu_ref[k:k + 1, :] += jnp.sum(duc * ut[k], axis=0, keepdims=True)

    prev = lambda off: (lambda j, i: (jnp.maximum(i * (tr // 8) - 1, 0), j + off))
    blk = lambda off: pl.BlockSpec((tr, tc), lambda j, i: (i, j + off))
    wblk = lambda off: pl.BlockSpec((FFN_CONV, tc), lambda j, i: (0, j + off))
    dgc, duc, dwg, dwu = pl.pallas_call(
        body, name=name,
        out_shape=(jax.ShapeDtypeStruct((t, D_FF), F32), jax.ShapeDtypeStruct((t, D_FF), F32),
                   jax.ShapeDtypeStruct((FFN_CONV, D_FF), F32), jax.ShapeDtypeStruct((FFN_CONV, D_FF), F32)),
        grid=(nj, t // tr),
        in_specs=[blk(0), blk(0), pl.BlockSpec((8, tc), prev(0)), blk(nj), pl.BlockSpec((8, tc), prev(nj)),
                  wblk(0), wblk(nj)],
        out_specs=(blk(0), blk(0), wblk(0), wblk(0)),
        compiler_params=_params(("parallel", "arbitrary")),
    )(dact, upre, upre, upre, upre, cw, cw)
    return dgc, duc, dwg, dwu


def _dn_pre_fwd(qkv_pre, cw, name):
    t = qkv_pre.shape[0]
    tr = _tile(t, 512)
    scale = HEAD_DIM ** -0.5

    def body(x_ref, p_ref, w_ref, o_ref):
        i, j = pl.program_id(0), pl.program_id(1)
        row = _iota((tr, HEAD_DIM), 0)
        prev = jnp.where(i == 0, 0.0, p_ref[...])
        _, c = _conv_taps(x_ref[...], prev, w_ref[...], DN_CONV, row)
        s = c * _sigmoid(c)
        r = lax.rsqrt(jnp.sum(s * s, axis=1, keepdims=True) + EPS)
        mult = jnp.where(j < HEADS, r * scale, jnp.where(j < 2 * HEADS, r, 1.0))
        o_ref[...] = s * mult

    return pl.pallas_call(
        body, name=name,
        out_shape=jax.ShapeDtypeStruct((t, 3 * WIDTH), F32),
        grid=(t // tr, 3 * HEADS),
        in_specs=[pl.BlockSpec((tr, HEAD_DIM), lambda i, j: (i, j)),
                  pl.BlockSpec((8, HEAD_DIM), lambda i, j: (jnp.maximum(i * (tr // 8) - 1, 0), j)),
                  pl.BlockSpec((DN_CONV, HEAD_DIM), lambda i, j: (0, j))],
        out_specs=pl.BlockSpec((tr, HEAD_DIM), lambda i, j: (i, j)),
        compiler_params=_params(("parallel", "parallel")),
    )(qkv_pre, qkv_pre, cw)


def _dn_pre_bwd(dact, qkv_pre, cw, name):
    t = qkv_pre.shape[0]
    tr = _tile(t, 512)
    scale = HEAD_DIM ** -0.5

    def body(d_ref, x_ref, p_ref, w_ref, dc_ref, dw_ref):
        j, i = pl.program_id(0), pl.program_id(1)
        row = _iota((tr, HEAD_DIM), 0)
        prev = jnp.where(i == 0, 0.0, p_ref[...])
        taps, c = _conv_taps(x_ref[...], prev, w_ref[...], DN_CONV, row)
        sg = _sigmoid(c)
        s = c * sg
        r = lax.rsqrt(jnp.sum(s * s, axis=1, keepdims=True) + EPS)
        nh = s * r
        dn = d_ref[...] * jnp.where(j < HEADS, scale, 1.0)
        ds_norm = r * (dn - nh * jnp.sum(nh * dn, axis=1, keepdims=True))
        ds = jnp.where(j < 2 * HEADS, ds_norm, d_ref[...])
        dc = ds * (sg * (1.0 + c * (1.0 - sg)))
        dc_ref[...] = dc

        @pl.when(i == 0)
        def _():
            dw_ref[...] = jnp.zeros_like(dw_ref)

        for k in range(DN_CONV):
            dw_ref[k:k + 1, :] += jnp.sum(dc * taps[k], axis=0, keepdims=True)

    return pl.pallas_call(
        body, name=name,
        out_shape=(jax.ShapeDtypeStruct((t, 3 * WIDTH), F32), jax.ShapeDtypeStruct((DN_CONV, 3 * WIDTH), F32)),
        grid=(3 * HEADS, t // tr),
        in_specs=[pl.BlockSpec((tr, HEAD_DIM), lambda j, i: (i, j)),
                  pl.BlockSpec((tr, HEAD_DIM), lambda j, i: (i, j)),
                  pl.BlockSpec((8, HEAD_DIM), lambda j, i: (jnp.maximum(i * (tr // 8) - 1, 0), j)),
                  pl.BlockSpec((DN_CONV, HEAD_DIM), lambda j, i: (0, j))],
        out_specs=(pl.BlockSpec((tr, HEAD_DIM), lambda j, i: (i, j)),
                   pl.BlockSpec((DN_CONV, HEAD_DIM), lambda j, i: (0, j))),
        compiler_params=_params(("parallel", "arbitrary")),
    )(dact, qkv_pre, qkv_pre, cw)


def _tri(n, kind):
    r, c = _iota((n, n), 0), _iota((n, n), 1)
    m = {"lower": r >= c, "strict": r > c, "upper": r <= c}[kind]
    return m


def _dn_gates_fwd(hab, alog, dtb, name):
    t = hab.shape[0]
    cc = DN_CHUNK

    def body(h_ref, al_ref, dt_ref, o_ref):
        hv = h_ref[...]
        lane = _iota(hv.shape, 1)
        xa = hv + dt_ref[...]
        sp = jnp.maximum(xa, 0.0) + _log1pexp_neg_abs(xa)
        g = jnp.where(lane < HEADS, -jnp.exp(al_ref[...]) * sp, 0.0)
        tril = jnp.where(_tri(cc, "lower"), 1.0, 0.0).astype(BF16)
        gc = _dot_xl(tril, g, NN)
        o_ref[...] = jnp.where(lane < HEADS, gc, jnp.where(lane < 2 * HEADS, _sigmoid(hv), 0.0))

    return pl.pallas_call(
        body, name=name,
        out_shape=jax.ShapeDtypeStruct((t, 128), F32),
        grid=(t // cc,),
        in_specs=[pl.BlockSpec((cc, 128), lambda i: (i, 0)), pl.BlockSpec((1, 128), lambda i: (0, 0)),
                  pl.BlockSpec((1, 128), lambda i: (0, 0))],
        out_specs=pl.BlockSpec((cc, 128), lambda i: (i, 0)),
        compiler_params=_params(("parallel",)),
    )(hab, alog, dtb)


def _dn_gates_bwd(dgates, hab, alog, dtb, name):
    t = hab.shape[0]
    cc = DN_CHUNK

    def body(d_ref, h_ref, al_ref, dt_ref, o_ref, dal_ref, ddt_ref):
        i = pl.program_id(0)
        hv = h_ref[...]
        dv = d_ref[...]
        lane = _iota(hv.shape, 1)
        triu = jnp.where(_tri(cc, "upper"), 1.0, 0.0).astype(BF16)
        dg = _dot_xl(triu, jnp.where(lane < HEADS, dv, 0.0), NN)
        xa = hv + dt_ref[...]
        sp = jnp.maximum(xa, 0.0) + _log1pexp_neg_abs(xa)
        ea = jnp.exp(al_ref[...])
        da = jnp.where(lane < HEADS, dg * (-ea) * _sigmoid(xa), 0.0)
        be = _sigmoid(hv)
        db = dv * be * (1.0 - be)
        o_ref[...] = jnp.where(lane < HEADS, da, jnp.where(lane < 2 * HEADS, db, 0.0))

        @pl.when(i == 0)
        def _():
            dal_ref[...] = jnp.zeros_like(dal_ref)
            ddt_ref[...] = jnp.zeros_like(ddt_ref)

        dal_ref[...] += jnp.sum(jnp.where(lane < HEADS, dg * (-ea) * sp, 0.0), axis=0, keepdims=True)
        ddt_ref[...] += jnp.sum(da, axis=0, keepdims=True)

    return pl.pallas_call(
        body, name=name,
        out_shape=(jax.ShapeDtypeStruct((t, 128), F32), jax.ShapeDtypeStruct((1, 128), F32),
                   jax.ShapeDtypeStruct((1, 128), F32)),
        grid=(t // cc,),
        in_specs=[pl.BlockSpec((cc, 128), lambda i: (i, 0)), pl.BlockSpec((cc, 128), lambda i: (i, 0)),
                  pl.BlockSpec((1, 128), lambda i: (0, 0)), pl.BlockSpec((1, 128), lambda i: (0, 0))],
        out_specs=(pl.BlockSpec((cc, 128), lambda i: (i, 0)), pl.BlockSpec((1, 128), lambda i: (0, 0)),
                   pl.BlockSpec((1, 128), lambda i: (0, 0))),
        compiler_params=_params(("arbitrary",)),
    )(dgates, hab, alog, dtb)


def _dn_chunk_common(gates, h):
    cc = DN_CHUNK
    lane = _iota(gates.shape, 1)
    gh = jnp.where(lane == h, gates, 0.0)
    gc_col = jnp.sum(gh, axis=1, keepdims=True)
    gc_row = _dot_xl(jnp.ones((cc, 128), BF16), gh, NT)
    beta = jnp.sum(jnp.where(lane == h + HEADS, gates, 0.0), axis=1, keepdims=True)
    lower = _tri(cc, "lower")
    decay = jnp.where(lower, jnp.exp(jnp.where(lower, gc_col - gc_row, 0.0)), 0.0)
    gc_last = gc_col[cc - 1:cc, :]
    return gc_col, gc_last, beta, decay


def _dn_local_fwd(act, gates, name):
    t = act.shape[0]
    cc = DN_CHUNK
    nc = t // cc

    def body(q_ref, k_ref, v_ref, g_ref, u_ref, w_ref, kd_ref, qg_ref, ti_ref, p_ref):
        h = pl.program_id(1)
        q, k, v = q_ref[...], k_ref[...], v_ref[...]
        gc_col, gc_last, beta, decay = _dn_chunk_common(g_ref[...], h)
        gam = jnp.exp(gc_col)
        kb = k * beta
        a = jnp.where(_tri(cc, "strict"), _dotb(kb, k, NT) * decay, 0.0)
        eye = jnp.where(_iota((cc, cc), 0) == _iota((cc, cc), 1), 1.0, 0.0)
        npow = -a
        tinv = eye + npow
        for _ in range(5):
            npow = _dot3(npow, npow, NN)
            tinv = tinv + _dot3(tinv, npow, NN)
        u_ref[...] = _dot3(tinv, v * beta, NN)
        w_ref[...] = _dot3(tinv, kb * gam, NN)
        kd_ref[...] = k * jnp.exp(gc_last - gc_col)
        qg_ref[...] = q * gam
        ti_ref[...] = tinv
        p_ref[...] = jnp.where(_tri(cc, "lower"), _dotb(q, k, NT) * decay, 0.0)

    hb = lambda off: pl.BlockSpec((cc, HEAD_DIM), lambda n, h: (n, h + off))
    mat = pl.BlockSpec((None, cc, cc), lambda n, h: (h, n, 0))
    tw = jax.ShapeDtypeStruct((t, WIDTH), F32)
    hm = jax.ShapeDtypeStruct((HEADS, t, cc), F32)
    return pl.pallas_call(
        body, name=name,
        out_shape=(tw, tw, tw, tw, hm, hm),
        grid=(nc, HEADS),
        in_specs=[hb(0), hb(HEADS), hb(2 * HEADS), pl.BlockSpec((cc, 128), lambda n, h: (n, 0))],
        out_specs=(hb(0), hb(0), hb(0), hb(0), mat, mat),
        compiler_params=_params(("parallel", "parallel")),
    )(act, act, act, gates)


def _dn_scan_fwd(u, w, kd, qg, p, gates, name):
    t = u.shape[0]
    cc = DN_CHUNK
    nc = t // cc

    def body(u_ref, w_ref, kd_ref, qg_ref, p_ref, g_ref, o_ref, sh_ref, s_ref):
        n = pl.program_id(0)

        @pl.when(n == 0)
        def _():
            s_ref[...] = jnp.zeros_like(s_ref)

        glast = jnp.exp(g_ref[cc - 1:cc, :])
        for h in range(HEADS):
            sl = slice(h * HEAD_DIM, (h + 1) * HEAD_DIM)
            s = s_ref[h]
            sb = s.astype(BF16)
            vn = u_ref[:, sl] - _dot(w_ref[:, sl].astype(BF16), sb, NN)
            vnb = vn.astype(BF16)
            o_ref[:, sl] = _dot(qg_ref[:, sl].astype(BF16), sb, NN) + _dot(p_ref[h].astype(BF16), vnb, NN)
            sh_ref[0, h] = s
            s_ref[h] = glast[:, h:h + 1] * s + _dot(kd_ref[:, sl].astype(BF16), vnb, TN)

    row = pl.BlockSpec((cc, WIDTH), lambda n: (n, 0))
    return pl.pallas_call(
        body, name=name,
        out_shape=(jax.ShapeDtypeStruct((t, WIDTH), F32),
                   jax.ShapeDtypeStruct((nc, HEADS, HEAD_DIM, HEAD_DIM), F32)),
        grid=(nc,),
        in_specs=[row, row, row, row, pl.BlockSpec((HEADS, cc, cc), lambda n: (0, n, 0)),
                  pl.BlockSpec((cc, 128), lambda n: (n, 0))],
        out_specs=(row, pl.BlockSpec((1, HEADS, HEAD_DIM, HEAD_DIM), lambda n: (n, 0, 0, 0))),
        scratch_shapes=[pltpu.VMEM((HEADS, HEAD_DIM, HEAD_DIM), F32)],
        compiler_params=_params(("arbitrary",)),
    )(u, w, kd, qg, p, gates)


def _dn_scan_bwd(do, w, kd, qg, p, gates, name):
    t = do.shape[0]
    cc = DN_CHUNK
    nc = t // cc

    def body(do_ref, w_ref, kd_ref, qg_ref, p_ref, g_ref, dvn_ref, dsh_ref, ds_ref):
        n = pl.program_id(0)

        @pl.when(n == 0)
        def _():
            ds_ref[...] = jnp.zeros_like(ds_ref)

        glast = jnp.exp(g_ref[cc - 1:cc, :])
        for h in range(HEADS):
            sl = slice(h * HEAD_DIM, (h + 1) * HEAD_DIM)
            ds = ds_ref[h]
            dob = do_ref[:, sl].astype(BF16)
            dvn = _dot(p_ref[h].astype(BF16), dob, TN) + _dot(kd_ref[:, sl].astype(BF16), ds.astype(BF16), NN)
            dvn_ref[:, sl] = dvn
            dsh_ref[0, h] = ds
            ds_ref[h] = (_dot(qg_ref[:, sl].astype(BF16), dob, TN) + glast[:, h:h + 1] * ds
                         - _dot(w_ref[:, sl].astype(BF16), dvn.astype(BF16), TN))

    row = pl.BlockSpec((cc, WIDTH), lambda n: (nc - 1 - n, 0))
    return pl.pallas_call(
        body, name=name,
        out_shape=(jax.ShapeDtypeStruct((t, WIDTH), F32),
                   jax.ShapeDtypeStruct((nc, HEADS, HEAD_DIM, HEAD_DIM), F32)),
        grid=(nc,),
        in_specs=[row, row, row, row, pl.BlockSpec((HEADS, cc, cc), lambda n: (0, nc - 1 - n, 0)),
                  pl.BlockSpec((cc, 128), lambda n: (nc - 1 - n, 0))],
        out_specs=(row, pl.BlockSpec((1, HEADS, HEAD_DIM, HEAD_DIM), lambda n: (nc - 1 - n, 0, 0, 0))),
        scratch_shapes=[pltpu.VMEM((HEADS, HEAD_DIM, HEAD_DIM), F32)],
        compiler_params=_params(("arbitrary",)),
    )(do, w, kd, qg, p, gates)


def _dn_local_bwd(act, gates, u, w, kd, qg, tinv, p, sh, dsh, dvn, do, name):
    t = act.shape[0]
    cc = DN_CHUNK
    nc = t // cc

    def body(q_ref, k_ref, v_ref, g_ref, u_ref, w_ref, kd_ref, qg_ref, ti_ref, p_ref, s_ref, ds_ref,
             dvn_ref, do_ref, dq_ref, dk_ref, dv_ref, dg_ref):
        h = pl.program_id(1)
        q, k, v = q_ref[...], k_ref[...], v_ref[...]
        gates_v = g_ref[...]
        gc_col, gc_last, beta, decay = _dn_chunk_common(gates_v, h)
        gam = jnp.exp(gc_col)
        kb = k * beta
        uu, ww, kd, qg = u_ref[...], w_ref[...], kd_ref[...], qg_ref[...]
        tinv, pp = ti_ref[...], p_ref[...]
        s_in, ds_out = s_ref[0, 0], ds_ref[0, 0]
        dvn, do = dvn_ref[...], do_ref[...]
        lower, strict = _tri(cc, "lower"), _tri(cc, "strict")

        a = jnp.where(strict, _dotb(kb, k, NT) * decay, 0.0)
        vn = uu - _dotb(ww, s_in, NN)
        dp = jnp.where(lower, _dotb(do, vn, NT), 0.0)
        dqg = _dotb(do, s_in, NT)
        dw = -_dotb(dvn, s_in, NT)
        dkd = _dotb(vn, ds_out, NT)
        dru = _dot3(tinv, dvn, TN)
        drw = _dot3(tinv, dw, TN)
        da = -jnp.where(strict, _dotb(dru, uu, NT) + _dotb(drw, ww, NT), 0.0)
        dad = da * decay
        dpd = dp * decay
        dkb = _dotb(dad, k, NN) + gam * drw
        kdec = jnp.exp(gc_last - gc_col)
        dk_ref[...] = _dotb(dad, kb, TN) + _dotb(dpd, q, TN) + beta * dkb + kdec * dkd
        dq_ref[...] = gam * dqg + _dotb(dpd, k, NN)
        dv_ref[...] = beta * dru
        dbeta = jnp.sum(dkb * k, axis=1, keepdims=True) + jnp.sum(dru * v, axis=1, keepdims=True)

        gm = da * a + dp * pp
        ones = jnp.ones((cc, 128), BF16)
        colsum = _dot_xr(gm, ones, TN)[:, 0:1]
        rkd = jnp.sum(dkd * kd, axis=1, keepdims=True)
        dgc = (jnp.sum(gm, axis=1, keepdims=True) - colsum + jnp.sum(dqg * qg, axis=1, keepdims=True)
               + jnp.sum(drw * kb, axis=1, keepdims=True) * gam - rkd)
        tail = jnp.sum(rkd, axis=0, keepdims=True) + jnp.exp(gc_last) * jnp.sum(
            jnp.sum(s_in * ds_out, axis=1, keepdims=True), axis=0, keepdims=True)
        rowc = _iota((cc, 1), 0)
        dgc = dgc + jnp.where(rowc == cc - 1, tail, 0.0)

        lane = _iota((cc, 128), 1)
        contrib = jnp.where(lane == h, dgc, 0.0) + jnp.where(lane == h + HEADS, dbeta, 0.0)

        @pl.when(h == 0)
        def _():
            dg_ref[...] = jnp.zeros_like(dg_ref)

        dg_ref[...] += contrib

    hb = lambda off: pl.BlockSpec((cc, HEAD_DIM), lambda n, h: (n, h + off))
    mat = pl.BlockSpec((None, cc, cc), lambda n, h: (h, n, 0))
    st = pl.BlockSpec((1, 1, HEAD_DIM, HEAD_DIM), lambda n, h: (n, h, 0, 0))
    tw = jax.ShapeDtypeStruct((t, WIDTH), F32)
    return pl.pallas_call(
        body, name=name,
        out_shape=(tw, tw, tw, jax.ShapeDtypeStruct((t, 128), F32)),
        grid=(nc, HEADS),
        in_specs=[hb(0), hb(HEADS), hb(2 * HEADS), pl.BlockSpec((cc, 128), lambda n, h: (n, 0)),
                  hb(0), hb(0), hb(0), hb(0), mat, mat, st, st, hb(0), hb(0)],
        out_specs=(hb(0), hb(0), hb(0), pl.BlockSpec((cc, 128), lambda n, h: (n, 0))),
        compiler_params=_params(("parallel", "arbitrary")),
    )(act, act, act, gates, u, w, kd, qg, tinv, p, sh, dsh, dvn, do)


def _dn_post_fwd(o, gate, w, name):
    t = o.shape[0]
    tr = _tile(t, 512)

    def body(o_ref, g_ref, w_ref, y_ref):
        ov, gv = o_ref[...], g_ref[...]
        r = lax.rsqrt(jnp.mean(ov * ov, axis=1, keepdims=True) + EPS)
        y_ref[...] = (ov * r * w_ref[...] * (gv * _sigmoid(gv))).astype(BF16)

    blk = pl.BlockSpec((tr, HEAD_DIM), lambda i, h: (i, h))
    return pl.pallas_call(
        body, name=name,
        out_shape=jax.ShapeDtypeStruct((t, WIDTH), BF16),
        grid=(t // tr, HEADS),
        in_specs=[blk, blk, pl.BlockSpec((1, HEAD_DIM), lambda i, h: (0, 0))],
        out_specs=blk,
        compiler_params=_params(("parallel", "parallel")),
    )(o, gate, w)


def _dn_post_bwd(dy, o, gate, w, name):
    t = o.shape[0]
    tr = _tile(t, 512)

    def body(dy_ref, o_ref, g_ref, w_ref, do_ref, dg_ref, dw_ref):
        i, h = pl.program_id(0), pl.program_id(1)
        ov, gv, dyv = o_ref[...], g_ref[...], dy_ref[...].astype(F32)
        r = lax.rsqrt(jnp.mean(ov * ov, axis=1, keepdims=True) + EPS)
        oh = ov * r
        sg = _sigmoid(gv)
        act = gv * sg
        dg_ref[...] = (dyv * oh * w_ref[...] * (sg * (1.0 + gv * (1.0 - sg)))).astype(BF16)
        dn = dyv * act
        doh = dn * w_ref[...]
        do_ref[...] = r * (doh - oh * jnp.mean(doh * oh, axis=1, keepdims=True))

        @pl.when(jnp.logical_and(i == 0, h == 0))
        def _():
            dw_ref[...] = jnp.zeros_like(dw_ref)

        dw_ref[...] += jnp.sum(dn * oh, axis=0, keepdims=True)

    blk = pl.BlockSpec((tr, HEAD_DIM), lambda i, h: (i, h))
    return pl.pallas_call(
        body, name=name,
        out_shape=(jax.ShapeDtypeStruct((t, WIDTH), F32), jax.ShapeDtypeStruct((t, WIDTH), BF16),
                   jax.ShapeDtypeStruct((1, HEAD_DIM), F32)),
        grid=(t // tr, HEADS),
        in_specs=[blk, blk, blk, pl.BlockSpec((1, HEAD_DIM), lambda i, h: (0, 0))],
        out_specs=(blk, blk, pl.BlockSpec((1, HEAD_DIM), lambda i, h: (0, 0))),
        compiler_params=_params(("arbitrary", "arbitrary")),
    )(dy, o, gate, w)


def _sb_scores(q, ks, qi, j, carry_b, uincl):
    bk = ATT_BLOCK
    scale = HEAD_DIM ** -0.5
    z = _dot(q, ks, NT) * scale
    qpos = qi * bk + _iota(z.shape, 0)
    kpos = j * bk + _iota(z.shape, 1)
    mask = kpos < qpos
    soft = _log1pexp_neg_abs(z)
    lk_full = -(jnp.maximum(z, 0.0) + soft)
    lk = jnp.where(mask, lk_full, 0.0)
    ls = jnp.minimum(z, 0.0) - soft
    incl = _dot_xr(lk, uincl, NN)
    a = jnp.where(mask, jnp.exp(ls + (carry_b + incl - lk)), 0.0)
    return a, mask, lk_full, ls, carry_b + incl[:, 0:1]


def _sb_fwd(qkv, name):
    t = qkv.shape[0]
    bk = ATT_BLOCK

    def body(q_ref, k_ref, v_ref, o_ref):
        qi = pl.program_id(1)
        q = q_ref[...]
        uincl = jnp.where(_tri(bk, "lower"), 1.0, 0.0).astype(BF16)

        def step(it, carry):
            cb, acc = carry
            j = qi - it
            rows = pl.ds(pl.multiple_of(j * bk, bk), bk)
            a, _, _, _, cb = _sb_scores(q, k_ref[rows, :], qi, j, cb, uincl)
            acc = acc + _dot(a.astype(BF16), v_ref[rows, :], NN)
            return cb, acc

        _, acc = lax.fori_loop(0, qi + 1, step, (jnp.zeros((bk, 1), F32), jnp.zeros((bk, HEAD_DIM), F32)))
        o_ref[...] = acc

    return pl.pallas_call(
        body, name=name,
        out_shape=jax.ShapeDtypeStruct((t, WIDTH), F32),
        grid=(HEADS, t // bk),
        in_specs=[pl.BlockSpec((bk, HEAD_DIM), lambda h, i: (i, h)),
                  pl.BlockSpec((t, HEAD_DIM), lambda h, i: (0, HEADS + h)),
                  pl.BlockSpec((t, HEAD_DIM), lambda h, i: (0, 2 * HEADS + h))],
        out_specs=pl.BlockSpec((bk, HEAD_DIM), lambda h, i: (i, h)),
        compiler_params=_params(("parallel", "arbitrary")),
    )(qkv, qkv, qkv)


def _sb_bwd(qkv, o, do, name):
    t = qkv.shape[0]
    bk = ATT_BLOCK
    scale = HEAD_DIM ** -0.5

    def body(q_ref, k_ref, v_ref, o_ref, do_ref, dq_ref, dk_ref, dv_ref):
        qi = pl.program_id(1)

        @pl.when(qi == 0)
        def _():
            dk_ref[...] = jnp.zeros_like(dk_ref)
            dv_ref[...] = jnp.zeros_like(dv_ref)

        q = q_ref[...]
        dov = do_ref[...]
        dob = dov.astype(BF16)
        do1, do2, do3 = _split3(dov)
        dsum = jnp.sum(dov * o_ref[...], axis=1, keepdims=True)
        uincl = jnp.where(_tri(bk, "lower"), 1.0, 0.0).astype(BF16)

        def step(it, carry):
            cb, ce, dq = carry
            j = qi - it
            rows = pl.ds(pl.multiple_of(j * bk, bk), bk)
            ks = k_ref[rows, :]
            a, mask, lk_full, ls, cb = _sb_scores(q, ks, qi, j, cb, uincl)
            ab = a.astype(BF16)
            vs = v_ref[rows, :]
            dla = ab.astype(F32) * (_dot(do1, vs, NT) + _dot(do2, vs, NT) + _dot(do3, vs, NT))
            suf = _dot_xr(dla, uincl, NN)
            e = dsum - (ce + suf)
            dz = jnp.where(mask, dla * jnp.exp(lk_full) - e * jnp.exp(ls), 0.0)
            dzb = (dz * scale).astype(BF16)
            dq = dq + _dot(dzb, ks, NN)
            dk_ref[rows, :] += _dot(dzb, q, TN)
            dv_ref[rows, :] += _dot(ab, dob, TN)
            return cb, ce + suf[:, 0:1], dq

        zc = jnp.zeros((bk, 1), F32)
        _, _, dq = lax.fori_loop(0, qi + 1, step, (zc, zc, jnp.zeros((bk, HEAD_DIM), F32)))
        dq_ref[...] = dq

    tw = jax.ShapeDtypeStruct((t, WIDTH), F32)
    qb = pl.BlockSpec((bk, HEAD_DIM), lambda h, i: (i, h))
    full = lambda off: pl.BlockSpec((t, HEAD_DIM), lambda h, i: (0, off + h))
    return pl.pallas_call(
        body, name=name,
        out_shape=(tw, tw, tw),
        grid=(HEADS, t // bk),
        in_specs=[qb, full(HEADS), full(2 * HEADS), qb, qb],
        out_specs=(qb, full(0), full(0)),
        compiler_params=_params(("parallel", "arbitrary")),
    )(qkv, qkv, qkv, o, do)


def _merge_fwd(pd, ps, gl, name):
    t = pd.shape[0]
    tr, tc = _tile(t, 512), 512
    nj = D_MODEL // tc

    def body(pd_ref, ps_ref, gd_ref, gs_ref, o_ref):
        o_ref[...] = (_sigmoid(gd_ref[...]) * pd_ref[...] + _sigmoid(gs_ref[...]) * ps_ref[...]).astype(BF16)

    blk = lambda off: pl.BlockSpec((tr, tc), lambda i, j: (i, j + off))
    return pl.pallas_call(
        body, name=name,
        out_shape=jax.ShapeDtypeStruct((t, D_MODEL), BF16),
        grid=(t // tr, nj),
        in_specs=[blk(0), blk(0), blk(0), blk(nj)],
        out_specs=blk(0),
        compiler_params=_params(("parallel", "parallel")),
    )(pd, ps, gl, gl)


def _merge_bwd(dm, pd, ps, gl, name):
    t = pd.shape[0]
    tr, tc = _tile(t, 512), 512
    nj = D_MODEL // tc

    def body(dm_ref, pd_ref, ps_ref, gd_ref, gs_ref, dpd_ref, dps_ref, dgd_ref, dgs_ref):
        dmv = dm_ref[...]
        sd, ss = _sigmoid(gd_ref[...]), _sigmoid(gs_ref[...])
        dpd_ref[...] = (dmv * sd).astype(BF16)
        dps_ref[...] = (dmv * ss).astype(BF16)
        dgd_ref[...] = (dmv * pd_ref[...] * sd * (1.0 - sd)).astype(BF16)
        dgs_ref[...] = (dmv * ps_ref[...] * ss * (1.0 - ss)).astype(BF16)

    blk = lambda off: pl.BlockSpec((tr, tc), lambda i, j: (i, j + off))
    out = jax.ShapeDtypeStruct((t, D_MODEL), BF16)
    return pl.pallas_call(
        body, name=name,
        out_shape=(out, out, out, out),
        grid=(t // tr, nj),
        in_specs=[blk(0), blk(0), blk(0), blk(0), blk(nj)],
        out_specs=(blk(0), blk(0), blk(0), blk(0)),
        compiler_params=_params(("parallel", "parallel")),
    )(dm, pd, ps, gl, gl)


def _local_step(x, target, wts):
    n1 = _rmsnorm_fwd(x, wts["norm1_w"], "norm1_fwd")
    qkv_pre = _matmul(n1, wts["w_dnqkv"], "nn", F32, "in_dnqkv")
    hgate = _matmul(n1, wts["w_dngate"], "nn", F32, "in_dngate")
    sbqkv = _matmul(n1, wts["w_sbqkv"], "nn", BF16, "in_sbqkv")
    gl = _matmul(n1, wts["w_gl"], "nn", F32, "in_gl")
    hab = _matmul(n1, wts["w_ab"], "nn", F32, "in_ab")

    act = _dn_pre_fwd(qkv_pre, wts["dn_conv_w"], "dn_pre_fwd")
    gates = _dn_gates_fwd(hab, wts["alog"], wts["dtb"], "dn_gates_fwd")
    u, w, kd, qg, tinv, p = _dn_local_fwd(act, gates, "dn_local_fwd")
    o_dn, sh = _dn_scan_fwd(u, w, kd, qg, p, gates, "dn_scan_fwd")
    y_dn = _dn_post_fwd(o_dn, hgate, wts["dn_norm_w"], "dn_post_fwd")

    o_sb = _sb_fwd(sbqkv, "sb_fwd")

    pd = _matmul(y_dn, wts["w_proj_dn"], "nn", F32, "proj_dn")
    ps = _matmul(o_sb, wts["w_proj_sb"], "nn", F32, "proj_sb")
    mixed = _merge_fwd(pd, ps, gl, "merge_fwd")
    x1 = _matmul(mixed, wts["w_out"], "nn", F32, "out_proj", add=x)

    n2 = _rmsnorm_fwd(x1, wts["norm2_w"], "norm2_fwd")
    upre = _matmul(n2, wts["ffn_w_up"], "nn", F32, "ffn_up")
    fact = _ffn_act_fwd(upre, wts["ffn_conv_w"], "ffn_act_fwd")
    x2 = _matmul(fact, wts["ffn_w_down"], "nn", F32, "ffn_down", add=x1)

    dx2, g_normf, loss = _final_loss(x2, target, wts["norm_f_w"], "final_loss")

    dfact = _matmul(dx2, wts["ffn_w_down"], "nt", BF16, "ffn_down_dx")
    g_wdown = _matmul(fact, dx2, "tn", F32, "ffn_down_dw")
    dgc, duc, dwg, dwu = _ffn_act_bwd(dfact, upre, wts["ffn_conv_w"], "ffn_act_bwd")
    dconv = jnp.concatenate([dgc, duc], axis=1)
    g_fconv = jnp.concatenate([dwg, dwu], axis=1)
    dupre = _conv_bwd_data(dconv, wts["ffn_conv_w"], FFN_CONV, BF16, "ffn_conv_bwd")
    dn2 = _matmul(dupre, wts["ffn_w_up"], "nt", F32, "ffn_up_dx")
    g_wup = _matmul(n2, dupre, "tn", F32, "ffn_up_dw")
    dx1, g_norm2 = _rmsnorm_bwd(dn2, x1, wts["norm2_w"], dx2, "norm2_bwd")

    dmixed = _matmul(dx1, wts["w_out"], "nt", F32, "out_proj_dx")
    g_wout = _matmul(mixed, dx1, "tn", F32, "out_proj_dw")
    dpd, dps, dgd, dgs = _merge_bwd(dmixed, pd, ps, gl, "merge_bwd")
    dy_dn = _matmul(dpd, wts["w_proj_dn"], "nt", F32, "proj_dn_dx")
    g_wpd = _matmul(y_dn, dpd, "tn", F32, "proj_dn_dw")
    do_sb = _matmul(dps, wts["w_proj_sb"], "nt", F32, "proj_sb_dx")
    g_wps = _matmul(o_sb, dps, "tn", F32, "proj_sb_dw")

    dsq, dsk, dsv = _sb_bwd(sbqkv, o_sb, do_sb, "sb_bwd")

    do_dn, dhgate, g_dnnorm = _dn_post_bwd(dy_dn, o_dn, hgate, wts["dn_norm_w"], "dn_post_bwd")
    dvn, dsh = _dn_scan_bwd(do_dn, w, kd, qg, p, gates, "dn_scan_bwd")
    dq, dk, dv, dgates = _dn_local_bwd(act, gates, u, w, kd, qg, tinv, p, sh, dsh, dvn, do_dn, "dn_local_bwd")
    dhab, g_alog, g_dtb = _dn_gates_bwd(dgates, hab, wts["alog"], wts["dtb"], "dn_gates_bwd")
    dact = jnp.concatenate([dq, dk, dv], axis=1)
    dcv, g_dnconv = _dn_pre_bwd(dact, qkv_pre, wts["dn_conv_w"], "dn_pre_bwd")
    dqkv_pre = _conv_bwd_data(dcv, wts["dn_conv_w"], DN_CONV, BF16, "dn_conv_bwd")

    dh = jnp.concatenate([dqkv_pre, dhgate, dsq.astype(BF16), dsk.astype(BF16), dsv.astype(BF16), dgd, dgs], axis=1)
    w_main = jnp.concatenate([wts["w_dnqkv"], wts["w_dngate"], wts["w_sbqkv"], wts["w_gl"]], axis=1)
    dn1 = _matmul(dh, w_main, "nt", F32, "in_dx_main")
    dn1 = _matmul(dhab, wts["w_ab"], "nt", F32, "in_dx_ab", add=dn1)
    g_wmain = _matmul(n1, dh, "tn", F32, "in_dw_main")
    g_wab = _matmul(n1, dhab, "tn", F32, "in_dw_ab")
    grad_x, g_norm1 = _rmsnorm_bwd(dn1, x, wts["norm1_w"], dx1, "norm1_bwd")

    grads = dict(norm1_w=g_norm1, w_main=g_wmain, w_ab=g_wab, dn_conv_w=g_dnconv, alog=g_alog, dtb=g_dtb,
                 dn_norm_w=g_dnnorm, w_proj_dn=g_wpd, w_proj_sb=g_wps, w_out=g_wout, norm2_w=g_norm2,
                 ffn_w_up=g_wup, ffn_conv_w=g_fconv, ffn_w_down=g_wdown, norm_f_w=g_normf)
    return loss, grad_x, grads


HBM_SPEC = pl.BlockSpec(memory_space=pltpu.HBM)


def _mesh_pos():
    x, y, c = lax.axis_index("x"), lax.axis_index("y"), lax.axis_index("c")
    return x, y, c, 4 * x + 2 * y + c


def _peer(k):
    x, y, c, _ = _mesh_pos()
    px = 1 - x if k & 4 else x
    py = 1 - y if k & 2 else y
    pc = 1 - c if k & 1 else c
    return (px, py, pc), 4 * px + 2 * py + pc


def _all_gather(shards, name):
    n = len(shards)

    def body(*refs):
        ins, outs = refs[:n], refs[n:2 * n]
        send, recv, loc = refs[2 * n:]
        me = _mesh_pos()[3]
        local = [pltpu.make_async_copy(ins[a], outs[a].at[me], loc.at[a]) for a in range(n)]
        for cp in local:
            cp.start()
        sends, recvs = [], []
        for a in range(n):
            for k in range(1, N_DEV):
                peer, pidx = _peer(k)
                sends.append(pltpu.make_async_remote_copy(
                    src_ref=ins[a], dst_ref=outs[a].at[me], send_sem=send.at[a, k - 1], recv_sem=recv.at[a, k - 1],
                    device_id=peer, device_id_type=pl.DeviceIdType.MESH))
                recvs.append(pltpu.make_async_remote_copy(
                    src_ref=ins[a], dst_ref=outs[a].at[pidx], send_sem=send.at[a, k - 1], recv_sem=recv.at[a, k - 1],
                    device_id=peer, device_id_type=pl.DeviceIdType.MESH))
        for cp in sends:
            cp.start()
        for cp in recvs:
            cp.wait_recv()
        for cp in sends:
            cp.wait_send()
        for cp in local:
            cp.wait()

    return pl.pallas_call(
        body, name=name,
        out_shape=[jax.ShapeDtypeStruct((N_DEV,) + s.shape, s.dtype) for s in shards],
        in_specs=[HBM_SPEC] * n,
        out_specs=[HBM_SPEC] * n,
        scratch_shapes=[pltpu.SemaphoreType.DMA((n, N_DEV - 1)), pltpu.SemaphoreType.DMA((n, N_DEV - 1)),
                        pltpu.SemaphoreType.DMA((n,))],
        compiler_params=pltpu.CompilerParams(has_side_effects=True),
    )(*shards)


def _exchange(slabs, small, name):
    n = len(slabs)

    def body(*refs):
        ins, small_in = refs[:n], refs[n]
        outs, small_out = refs[n + 1:2 * n + 1], refs[2 * n + 1]
        send, recv, loc = refs[2 * n + 2:]
        me = _mesh_pos()[3]
        local = [pltpu.make_async_copy(ins[a].at[me], outs[a].at[me], loc.at[a]) for a in range(n)]
        local.append(pltpu.make_async_copy(small_in, small_out.at[me], loc.at[n]))
        for cp in local:
            cp.start()
        sends, recvs = [], []
        for a in range(n + 1):
            for k in range(1, N_DEV):
                peer, pidx = _peer(k)
                src = ins[a].at[pidx] if a < n else small_in
                out = outs[a] if a < n else small_out
                sends.append(pltpu.make_async_remote_copy(
                    src_ref=src, dst_ref=out.at[me], send_sem=send.at[a, k - 1], recv_sem=recv.at[a, k - 1],
                    device_id=peer, device_id_type=pl.DeviceIdType.MESH))
                recvs.append(pltpu.make_async_remote_copy(
                    src_ref=src, dst_ref=out.at[pidx], send_sem=send.at[a, k - 1], recv_sem=recv.at[a, k - 1],
                    device_id=peer, device_id_type=pl.DeviceIdType.MESH))
        for cp in sends:
            cp.start()
        for cp in recvs:
            cp.wait_recv()
        for cp in sends:
            cp.wait_send()
        for cp in local:
            cp.wait()

    return pl.pallas_call(
        body, name=name,
        out_shape=[jax.ShapeDtypeStruct(s.shape, s.dtype) for s in slabs]
        + [jax.ShapeDtypeStruct((N_DEV,) + small.shape, small.dtype)],
        in_specs=[HBM_SPEC] * (n + 1),
        out_specs=[HBM_SPEC] * (n + 1),
        scratch_shapes=[pltpu.SemaphoreType.DMA((n + 1, N_DEV - 1)), pltpu.SemaphoreType.DMA((n + 1, N_DEV - 1)),
                        pltpu.SemaphoreType.DMA((n + 1,))],
        compiler_params=pltpu.CompilerParams(has_side_effects=True),
    )(*slabs, small)


def _adamw(parts, w, m, v, name):
    rows, cols = w.shape
    tr = rows
    for cand in (128, 88):
        if rows > cand and rows % cand == 0:
            tr = cand
            break

    def body(p_ref, w_ref, m_ref, v_ref, g_ref, d_ref, mo_ref, vo_ref):
        g = p_ref[0]
        for s in range(1, N_DEV):
            g = g + p_ref[s]
        mn = ADAM_B1 * m_ref[...] + (1.0 - ADAM_B1) * g
        vn = ADAM_B2 * v_ref[...] + (1.0 - ADAM_B2) * (g * g)
        m_hat = mn / (1.0 - ADAM_B1 ** ADAM_STEP)
        v_hat = vn / (1.0 - ADAM_B2 ** ADAM_STEP)
        g_ref[...] = g
        d_ref[...] = -ADAM_LR * (m_hat / (jnp.sqrt(v_hat) + ADAM_EPS) + ADAM_WD * w_ref[...])
        mo_ref[...] = mn
        vo_ref[...] = vn

    blk = pl.BlockSpec((tr, cols), lambda i: (i, 0))
    out = jax.ShapeDtypeStruct((rows, cols), F32)
    return pl.pallas_call(
        body, name=name,
        out_shape=(out, out, out, out),
        grid=(rows // tr,),
        in_specs=[pl.BlockSpec((N_DEV, tr, cols), lambda i: (0, i, 0)), blk, blk, blk],
        out_specs=(blk, blk, blk, blk),
        compiler_params=_params(("parallel",)),
    )(parts, w, m, v)


CONV_PACK = 8 * 1024
WEIGHT_ORDER = ("norm1_w", "w_in", "dn_conv_w", "dn_A_log", "dn_dt_bias", "dn_norm_w", "w_proj_dn", "w_proj_sb",
                "w_out", "norm2_w", "ffn_w_up", "ffn_conv_w", "ffn_w_down", "norm_f_w")


def _cols_to_slabs(g):
    r, c8 = g.shape
    return g.reshape(r, N_DEV, c8 // N_DEV).transpose(1, 0, 2)


def _slabs_to_cols(s):
    d, r, c = s.shape
    return s.transpose(1, 0, 2).reshape(r, d * c)


def kernel(x, norm1_w, w_in, dn_conv_w, dn_A_log, dn_dt_bias, dn_norm_w, w_proj_dn, w_proj_sb, w_out, norm2_w, ffn_w_up, ffn_conv_w, ffn_w_down, norm_f_w, loss_target, m_norm1_w, m_w_in, m_dn_conv_w, m_dn_A_log, m_dn_dt_bias, m_dn_norm_w, m_w_proj_dn, m_w_proj_sb, m_w_out, m_norm2_w, m_ffn_w_up, m_ffn_conv_w, m_ffn_w_down, m_norm_f_w, v_norm1_w, v_w_in, v_dn_conv_w, v_dn_A_log, v_dn_dt_bias, v_dn_norm_w, v_w_proj_dn, v_w_proj_sb, v_w_out, v_norm2_w, v_ffn_w_up, v_ffn_conv_w, v_ffn_w_down, v_norm_f_w):
    me = _mesh_pos()[3]
    w_loc = dict(norm1_w=norm1_w, w_in=w_in[0], dn_conv_w=dn_conv_w[0], dn_A_log=dn_A_log, dn_dt_bias=dn_dt_bias,
                 dn_norm_w=dn_norm_w, w_proj_dn=w_proj_dn[0], w_proj_sb=w_proj_sb[0], w_out=w_out[0],
                 norm2_w=norm2_w, ffn_w_up=ffn_w_up[0], ffn_conv_w=ffn_conv_w[0], ffn_w_down=ffn_w_down[0],
                 norm_f_w=norm_f_w[None, :])
    m_loc = dict(norm1_w=m_norm1_w, w_in=m_w_in[0], dn_conv_w=m_dn_conv_w[0], dn_A_log=m_dn_A_log,
                 dn_dt_bias=m_dn_dt_bias, dn_norm_w=m_dn_norm_w, w_proj_dn=m_w_proj_dn[0], w_proj_sb=m_w_proj_sb[0],
                 w_out=m_w_out[0], norm2_w=m_norm2_w, ffn_w_up=m_ffn_w_up[0], ffn_conv_w=m_ffn_conv_w[0],
                 ffn_w_down=m_ffn_w_down[0], norm_f_w=m_norm_f_w[None, :])
    v_loc = dict(norm1_w=v_norm1_w, w_in=v_w_in[0], dn_conv_w=v_dn_conv_w[0], dn_A_log=v_dn_A_log,
                 dn_dt_bias=v_dn_dt_bias, dn_norm_w=v_dn_norm_w, w_proj_dn=v_w_proj_dn[0], w_proj_sb=v_w_proj_sb[0],
                 w_out=v_w_out[0], norm2_w=v_norm2_w, ffn_w_up=v_ffn_w_up[0], ffn_conv_w=v_ffn_conv_w[0],
                 ffn_w_down=v_ffn_w_down[0], norm_f_w=v_norm_f_w[None, :])

    big = ("w_in", "w_proj_dn", "w_proj_sb", "w_out", "ffn_w_up", "ffn_w_down")
    conv_flat = jnp.concatenate([w_loc["dn_conv_w"].reshape(-1), w_loc["ffn_conv_w"].reshape(-1)])
    n_dn, n_ffn = DN_CONV * 3 * WIDTH // N_DEV, FFN_CONV * 2 * D_FF // N_DEV
    conv_pack = jnp.pad(conv_flat, (0, CONV_PACK - n_dn - n_ffn)).reshape(8, 1024)
    gathered = _all_gather([w_loc[k].astype(BF16) for k in big] + [conv_pack], "gather_weights")
    g_in, g_pd, g_ps, g_out, g_up, g_down, g_conv = gathered
    w_in_full = _slabs_to_cols(g_in)
    g_conv = g_conv.reshape(N_DEV, CONV_PACK)
    dn_conv_full = _slabs_to_cols(g_conv[:, :n_dn].reshape(N_DEV, DN_CONV, 3 * WIDTH // N_DEV))
    ffn_conv_full = _slabs_to_cols(g_conv[:, n_dn:n_dn + n_ffn].reshape(N_DEV, FFN_CONV, 2 * D_FF // N_DEV))
    q_end = 3 * WIDTH
    ab_end = q_end + 2 * HEADS
    gate_end = ab_end + WIDTH
    sb_end = gate_end + 3 * WIDTH
    pad_lanes = lambda a: jnp.pad(a, ((0, 0), (0, 128 - a.shape[1])))
    wts = dict(
        norm1_w=norm1_w, w_dnqkv=w_in_full[:, :q_end], w_ab=pad_lanes(w_in_full[:, q_end:ab_end]),
        w_dngate=w_in_full[:, ab_end:gate_end], w_sbqkv=w_in_full[:, gate_end:sb_end], w_gl=w_in_full[:, sb_end:],
        dn_conv_w=dn_conv_full, alog=pad_lanes(dn_A_log), dtb=pad_lanes(dn_dt_bias), dn_norm_w=dn_norm_w,
        w_proj_dn=g_pd.reshape(WIDTH, D_MODEL), w_proj_sb=g_ps.reshape(WIDTH, D_MODEL),
        w_out=g_out.reshape(D_MODEL, D_MODEL), norm2_w=norm2_w, ffn_w_up=_slabs_to_cols(g_up),
        ffn_conv_w=ffn_conv_full, ffn_w_down=g_down.reshape(D_FF, D_MODEL), norm_f_w=norm_f_w[None, :])

    loss, grad_x, g = _local_step(x[0], loss_target[0], wts)

    g_win = jnp.concatenate([g["w_main"][:, :q_end], g["w_ab"][:, :2 * HEADS], g["w_main"][:, q_end:]], axis=1)
    slabs = [_cols_to_slabs(g_win), g["w_proj_dn"].reshape(N_DEV, WIDTH // N_DEV, D_MODEL),
             g["w_proj_sb"].reshape(N_DEV, WIDTH // N_DEV, D_MODEL),
             g["w_out"].reshape(N_DEV, D_MODEL // N_DEV, D_MODEL), _cols_to_slabs(g["ffn_w_up"]),
             g["ffn_w_down"].reshape(N_DEV, D_FF // N_DEV, D_MODEL)]
    row3 = jnp.concatenate([g["dn_norm_w"], g["alog"], g["dtb"], jnp.pad(loss, ((0, 0), (0, 127))),
                            jnp.zeros((1, D_MODEL - 512), F32)], axis=1)
    n_fc = FFN_CONV * 2 * D_FF
    fc_rows = -(-n_fc // D_MODEL)
    fconv_rows = jnp.pad(g["ffn_conv_w"].reshape(-1), (0, fc_rows * D_MODEL - n_fc)).reshape(fc_rows, D_MODEL)
    dn_rows = DN_CONV * 3 * WIDTH // D_MODEL
    pad8 = lambda a: jnp.pad(a, ((0, -a.shape[0] % 8), (0, 0)))
    pieces = [g["norm1_w"], g["norm2_w"], g["norm_f_w"], row3, g["dn_conv_w"].reshape(dn_rows, D_MODEL), fconv_rows]
    small = jnp.concatenate([pad8(a) for a in pieces], axis=0)
    assert small.shape[0] == SMALL_ROWS
    r_in, r_pd, r_ps, r_out, r_up, r_down, r_small = _exchange(slabs, small, "exchange_grads")

    parts = dict(w_in=r_in, w_proj_dn=r_pd, w_proj_sb=r_ps, w_out=r_out, ffn_w_up=r_up, ffn_w_down=r_down)
    parts["norm1_w"] = r_small[:, 0:1, :]
    parts["norm2_w"] = r_small[:, 8:9, :]
    parts["norm_f_w"] = r_small[:, 16:17, :]
    parts["dn_norm_w"] = r_small[:, 24:25, 0:HEAD_DIM]
    parts["dn_A_log"] = r_small[:, 24:25, 128:128 + HEADS]
    parts["dn_dt_bias"] = r_small[:, 24:25, 256:256 + HEADS]
    dnc = r_small[:, 32:32 + dn_rows, :].reshape(N_DEV, DN_CONV, 3 * WIDTH)
    parts["dn_conv_w"] = lax.dynamic_slice_in_dim(dnc, me * (3 * WIDTH // N_DEV), 3 * WIDTH // N_DEV, axis=2)
    fc0 = 32 + dn_rows + (-dn_rows % 8)
    fcc = r_small[:, fc0:fc0 + fc_rows, :].reshape(N_DEV, fc_rows * D_MODEL)[:, :n_fc]
    fcc = fcc.reshape(N_DEV, FFN_CONV, 2 * D_FF)
    parts["ffn_conv_w"] = lax.dynamic_slice_in_dim(fcc, me * (2 * D_FF // N_DEV), 2 * D_FF // N_DEV, axis=2)
    loss_total = jnp.sum(r_small[:, 24, 384])

    res = {k: _adamw(parts[k], w_loc[k], m_loc[k], v_loc[k], "adamw_" + k) for k in WEIGHT_ORDER}
    lead = ("w_in", "dn_conv_w", "w_proj_dn", "w_proj_sb", "w_out", "ffn_w_up", "ffn_conv_w", "ffn_w_down")

    def shaped(k, a):
        if k in lead:
            return a[None]
        if k == "norm_f_w":
            return a[0]
        return a

    outs = [loss_total, grad_x[None]]
    for idx in range(4):
        outs += [shaped(k, res[k][idx]) for k in WEIGHT_ORDER]
    return tuple(outs)
```

```python
import functools

import jax
import jax.numpy as jnp
from jax import lax
from jax.experimental import pallas as pl
from jax.experimental.pallas import tpu as pltpu

F32 = jnp.float32
BF16 = jnp.bfloat16

N_DEV = 8
D_MODEL = 1024
HEADS = 8
HEAD_DIM = 128
WIDTH = HEADS * HEAD_DIM
DN_CONV = 4
DN_CHUNK = 64
D_FF = 2816
FFN_CONV = 3
EPS = 1e-6
ATT_BLOCK = 256
SB_LOG_ZERO = -104.0
SMALL_ROWS = 72

ADAM_LR = 0.001
ADAM_B1 = 0.9
ADAM_B2 = 0.999
ADAM_EPS = 1e-08
ADAM_WD = 0.01
ADAM_STEP = 10

VMEM_LIMIT = 48 * 1024 * 1024


def _params(sem=None, **kw):
    return pltpu.CompilerParams(dimension_semantics=sem, vmem_limit_bytes=VMEM_LIMIT, **kw)


def _tile(n, cap):
    if n <= cap:
        return n
    best = None
    for t in range(128, cap + 1, 128):
        if n % t == 0:
            best = t
    assert best is not None, (n, cap)
    return best


def _dot(a, b, dims):
    return lax.dot_general(a, b, ((dims[0], dims[1]), ((), ())), preferred_element_type=F32)


NN = ((1,), (0,))
NT = ((1,), (1,))
TN = ((0,), (0,))


def _dotb(a, b, dims):
    return _dot(a.astype(BF16), b.astype(BF16), dims)


def _split3(x):
    h1 = x.astype(BF16)
    r1 = x - h1.astype(F32)
    h2 = r1.astype(BF16)
    r2 = r1 - h2.astype(F32)
    return h1, h2, r2.astype(BF16)


def _dot_xr(a, b_exact, dims):
    a1, a2, a3 = _split3(a)
    return _dot(a1, b_exact, dims) + _dot(a2, b_exact, dims) + _dot(a3, b_exact, dims)


def _dot_xl(a_exact, b, dims):
    b1, b2, b3 = _split3(b)
    return _dot(a_exact, b1, dims) + _dot(a_exact, b2, dims) + _dot(a_exact, b3, dims)


def _dot3(a, b, dims):
    a1 = a.astype(BF16)
    a2 = (a - a1.astype(F32)).astype(BF16)
    b1 = b.astype(BF16)
    b2 = (b - b1.astype(F32)).astype(BF16)
    return _dot(a1, b1, dims) + (_dot(a1, b2, dims) + _dot(a2, b1, dims))


def _sigmoid(x):
    return 1.0 / (1.0 + jnp.exp(-x))


def _log1pexp_neg_abs(x):
    return jnp.log(1.0 + jnp.exp(-jnp.abs(x)))


def _iota(shape, dim):
    return lax.broadcasted_iota(jnp.int32, shape, dim)


def _matmul(a, b, mode, out_dtype, name, add=None):
    if mode == "nn":
        (m, k), (k2, n) = a.shape, b.shape
    elif mode == "nt":
        (m, k), (n, k2) = a.shape, b.shape
    else:
        (k, m), (k2, n) = a.shape, b.shape
    assert k == k2, (a.shape, b.shape, mode)
    tm, tn, tk = _tile(m, 512), _tile(n, 512), _tile(k, 1536)
    nk = k // tk
    dims = {"nn": NN, "nt": NT, "tn": TN}[mode]

    def body(*refs):
        if add is None:
            a_ref, b_ref, o_ref, acc_ref = refs
        else:
            a_ref, b_ref, add_ref, o_ref, acc_ref = refs
        kk = pl.program_id(2)

        @pl.when(kk == 0)
        def _():
            acc_ref[...] = jnp.zeros_like(acc_ref)

        acc_ref[...] += _dotb(a_ref[...], b_ref[...], dims)

        @pl.when(kk == nk - 1)
        def _():
            r = acc_ref[...]
            if add is not None:
                r = r + add_ref[...].astype(F32)
            o_ref[...] = r.astype(out_dtype)

    if mode == "nn":
        specs = [pl.BlockSpec((tm, tk), lambda i, j, l: (i, l)), pl.BlockSpec((tk, tn), lambda i, j, l: (l, j))]
    elif mode == "nt":
        specs = [pl.BlockSpec((tm, tk), lambda i, j, l: (i, l)), pl.BlockSpec((tn, tk), lambda i, j, l: (j, l))]
    else:
        specs = [pl.BlockSpec((tk, tm), lambda i, j, l: (l, i)), pl.BlockSpec((tk, tn), lambda i, j, l: (l, j))]
    args = [a, b]
    if add is not None:
        specs.append(pl.BlockSpec((tm, tn), lambda i, j, l: (i, j)))
        args.append(add)
    return pl.pallas_call(
        body, name=name,
        out_shape=jax.ShapeDtypeStruct((m, n), out_dtype),
        grid=(m // tm, n // tn, nk),
        in_specs=specs,
        out_specs=pl.BlockSpec((tm, tn), lambda i, j, l: (i, j)),
        scratch_shapes=[pltpu.VMEM((tm, tn), F32)],
        compiler_params=_params(("parallel", "parallel", "arbitrary")),
    )(*args)


def _rmsnorm_fwd(x, w, name):
    t, d = x.shape
    tr = _tile(t, 512)

    def body(x_ref, w_ref, o_ref):
        xv = x_ref[...]
        r = lax.rsqrt(jnp.mean(xv * xv, axis=1, keepdims=True) + EPS)
        o_ref[...] = (xv * r * w_ref[...]).astype(BF16)

    return pl.pallas_call(
        body, name=name,
        out_shape=jax.ShapeDtypeStruct((t, d), BF16),
        grid=(t // tr,),
        in_specs=[pl.BlockSpec((tr, d), lambda i: (i, 0)), pl.BlockSpec((1, d), lambda i: (0, 0))],
        out_specs=pl.BlockSpec((tr, d), lambda i: (i, 0)),
        compiler_params=_params(("parallel",)),
    )(x, w)


def _rmsnorm_bwd(dn, x, w, dres, name):
    t, d = x.shape
    tr = _tile(t, 512)

    def body(dn_ref, x_ref, w_ref, dres_ref, dx_ref, dw_ref):
        i = pl.program_id(0)
        xv = x_ref[...]
        g = dn_ref[...].astype(F32)
        r = lax.rsqrt(jnp.mean(xv * xv, axis=1, keepdims=True) + EPS)
        xh = xv * r
        dxh = g * w_ref[...]
        dx = r * (dxh - xh * jnp.mean(dxh * xh, axis=1, keepdims=True))
        dx_ref[...] = dres_ref[...] + dx

        @pl.when(i == 0)
        def _():
            dw_ref[...] = jnp.zeros_like(dw_ref)

        dw_ref[...] += jnp.sum(g * xh, axis=0, keepdims=True)

    return pl.pallas_call(
        body, name=name,
        out_shape=(jax.ShapeDtypeStruct((t, d), F32), jax.ShapeDtypeStruct((1, d), F32)),
        grid=(t // tr,),
        in_specs=[pl.BlockSpec((tr, d), lambda i: (i, 0)), pl.BlockSpec((tr, d), lambda i: (i, 0)),
                  pl.BlockSpec((1, d), lambda i: (0, 0)), pl.BlockSpec((tr, d), lambda i: (i, 0))],
        out_specs=(pl.BlockSpec((tr, d), lambda i: (i, 0)), pl.BlockSpec((1, d), lambda i: (0, 0))),
        compiler_params=_params(("arbitrary",)),
    )(dn, x, w, dres)


def _final_loss(x2, target, w, name):
    t, d = x2.shape
    tr = _tile(t, 512)

    def body(x_ref, t_ref, w_ref, dx_ref, dw_ref, loss_ref):
        i = pl.program_id(0)
        xv = x_ref[...]
        r = lax.rsqrt(jnp.mean(xv * xv, axis=1, keepdims=True) + EPS)
        xh = xv * r
        err = xh * w_ref[...] - t_ref[...]
        dy = err * (1.0 / d)
        dxh = dy * w_ref[...]
        dx_ref[...] = r * (dxh - xh * jnp.mean(dxh * xh, axis=1, keepdims=True))

        @pl.when(i == 0)
        def _():
            dw_ref[...] = jnp.zeros_like(dw_ref)
            loss_ref[...] = jnp.zeros_like(loss_ref)

        dw_ref[...] += jnp.sum(dy * xh, axis=0, keepdims=True)
        row = jnp.sum(err * err, axis=1, keepdims=True) * (0.5 / d)
        loss_ref[...] += jnp.sum(row, axis=0, keepdims=True)

    return pl.pallas_call(
        body, name=name,
        out_shape=(jax.ShapeDtypeStruct((t, d), F32), jax.ShapeDtypeStruct((1, d), F32),
                   jax.ShapeDtypeStruct((1, 1), F32)),
        grid=(t // tr,),
        in_specs=[pl.BlockSpec((tr, d), lambda i: (i, 0)), pl.BlockSpec((tr, d), lambda i: (i, 0)),
                  pl.BlockSpec((1, d), lambda i: (0, 0))],
        out_specs=(pl.BlockSpec((tr, d), lambda i: (i, 0)), pl.BlockSpec((1, d), lambda i: (0, 0)),
                   pl.BlockSpec((1, 1), lambda i: (0, 0))),
        compiler_params=_params(("arbitrary",)),
    )(x2, target, w)


def _shift_down(cur, prev, k, row):
    r = pltpu.roll(cur, k, 0)
    for m in range(k):
        r = jnp.where(row == m, prev[8 - k + m:8 - k + m + 1, :], r)
    return r


def _shift_up(cur, nxt, k, row, tr):
    r = pltpu.roll(cur, tr - k, 0)
    for m in range(k):
        r = jnp.where(row == tr - k + m, nxt[m:m + 1, :], r)
    return r


def _conv_taps(cur, prev, w, ntaps, row):
    taps = [cur if i == ntaps - 1 else _shift_down(cur, prev, ntaps - 1 - i, row) for i in range(ntaps)]
    y = w[0:1, :] * taps[0]
    for i in range(1, ntaps):
        y = y + w[i:i + 1, :] * taps[i]
    return taps, y


def _conv_bwd_data(dc, w, ntaps, out_dtype, name):
    t, ch = dc.shape
    tr, tc = _tile(t, 512), _tile(ch, 512)
    nrow8 = t // 8
    last = t // tr - 1

    def body(cur_ref, nxt_ref, w_ref, o_ref):
        i = pl.program_id(0)
        cur = cur_ref[...]
        nxt = jnp.where(i == last, 0.0, nxt_ref[...])
        row = _iota(cur.shape, 0)
        wv = w_ref[...]
        y = wv[ntaps - 1:ntaps, :] * cur
        for k in range(1, ntaps):
            y = y + wv[ntaps - 1 - k:ntaps - k, :] * _shift_up(cur, nxt, k, row, tr)
        o_ref[...] = y.astype(out_dtype)

    return pl.pallas_call(
        body, name=name,
        out_shape=jax.ShapeDtypeStruct((t, ch), out_dtype),
        grid=(t // tr, ch // tc),
        in_specs=[pl.BlockSpec((tr, tc), lambda i, j: (i, j)),
                  pl.BlockSpec((8, tc), lambda i, j: (jnp.minimum((i + 1) * (tr // 8), nrow8 - 1), j)),
                  pl.BlockSpec((ntaps, tc), lambda i, j: (0, j))],
        out_specs=pl.BlockSpec((tr, tc), lambda i, j: (i, j)),
        compiler_params=_params(("parallel", "parallel")),
    )(dc, dc, w)


def _ffn_act_fwd(upre, cw, name):
    t = upre.shape[0]
    tr, tc = _tile(t, 512), 256
    nj = D_FF // tc

    def body(g_ref, gp_ref, u_ref, up_ref, wg_ref, wu_ref, o_ref):
        i = pl.program_id(0)
        row = _iota((tr, tc), 0)
        gp = jnp.where(i == 0, 0.0, gp_ref[...])
        up = jnp.where(i == 0, 0.0, up_ref[...])
        _, gc = _conv_taps(g_ref[...], gp, wg_ref[...], FFN_CONV, row)
        _, uc = _conv_taps(u_ref[...], up, wu_ref[...], FFN_CONV, row)
        o_ref[...] = (gc * _sigmoid(gc) * uc).astype(BF16)

    prev = lambda off: (lambda i, j: (jnp.maximum(i * (tr // 8) - 1, 0), j + off))
    return pl.pallas_call(
        body, name=name,
        out_shape=jax.ShapeDtypeStruct((t, D_FF), BF16),
        grid=(t // tr, nj),
        in_specs=[pl.BlockSpec((tr, tc), lambda i, j: (i, j)), pl.BlockSpec((8, tc), prev(0)),
                  pl.BlockSpec((tr, tc), lambda i, j: (i, j + nj)), pl.BlockSpec((8, tc), prev(nj)),
                  pl.BlockSpec((FFN_CONV, tc), lambda i, j: (0, j)),
                  pl.BlockSpec((FFN_CONV, tc), lambda i, j: (0, j + nj))],
        out_specs=pl.BlockSpec((tr, tc), lambda i, j: (i, j)),
        compiler_params=_params(("parallel", "parallel")),
    )(upre, upre, upre, upre, cw, cw)


def _ffn_act_bwd(dact, upre, cw, name):
    t = upre.shape[0]
    tr, tc = _tile(t, 512), 256
    nj = D_FF // tc

    def body(da_ref, g_ref, gp_ref, u_ref, up_ref, wg_ref, wu_ref, dg_ref, du_ref, dwg_ref, dwu_ref):
        i = pl.program_id(1)
        row = _iota((tr, tc), 0)
        gp = jnp.where(i == 0, 0.0, gp_ref[...])
        up = jnp.where(i == 0, 0.0, up_ref[...])
        gt, gc = _conv_taps(g_ref[...], gp, wg_ref[...], FFN_CONV, row)
        ut, uc = _conv_taps(u_ref[...], up, wu_ref[...], FFN_CONV, row)
        da = da_ref[...].astype(F32)
        sg = _sigmoid(gc)
        dgc = da * uc * (sg * (1.0 + gc * (1.0 - sg)))
        duc = da * (gc * sg)
        dg_ref[...] = dgc
        du_ref[...] = duc

        @pl.when(i == 0)
        def _():
            dwg_ref[...] = jnp.zeros_like(dwg_ref)
            dwu_ref[...] = jnp.zeros_like(dwu_ref)

        for k in range(FFN_CONV):
            dwg_ref[k:k + 1, :] += jnp.sum(dgc * gt[k], axis=0, keepdims=True)
            dwu_ref[k:k + 1, :] += jnp.sum(duc * ut[k], axis=0, keepdims=True)

    prev = lambda off: (lambda j, i: (jnp.maximum(i * (tr // 8) - 1, 0), j + off))
    blk = lambda off: pl.BlockSpec((tr, tc), lambda j, i: (i, j + off))
    wblk = lambda off: pl.BlockSpec((FFN_CONV, tc), lambda j, i: (0, j + off))
    dgc, duc, dwg, dwu = pl.pallas_call(
        body, name=name,
        out_shape=(jax.ShapeDtypeStruct((t, D_FF), F32), jax.ShapeDtypeStruct((t, D_FF), F32),
                   jax.ShapeDtypeStruct((FFN_CONV, D_FF), F32), jax.ShapeDtypeStruct((FFN_CONV, D_FF), F32)),
        grid=(nj, t // tr),
        in_specs=[blk(0), blk(0), pl.BlockSpec((8, tc), prev(0)), blk(nj), pl.BlockSpec((8, tc), prev(nj)),
                  wblk(0), wblk(nj)],
        out_specs=(blk(0), blk(0), wblk(0), wblk(0)),
        compiler_params=_params(("parallel", "arbitrary")),
    )(dact, upre, upre, upre, upre, cw, cw)
    return dgc, duc, dwg, dwu


def _dn_pre_fwd(qkv_pre, cw, name):
    t = qkv_pre.shape[0]
    tr = _tile(t, 512)
    scale = HEAD_DIM ** -0.5

    def body(x_ref, p_ref, w_ref, o_ref):
        i, j = pl.program_id(0), pl.program_id(1)
        row = _iota((tr, HEAD_DIM), 0)
        prev = jnp.where(i == 0, 0.0, p_ref[...])
        _, c = _conv_taps(x_ref[...], prev, w_ref[...], DN_CONV, row)
        s = c * _sigmoid(c)
        r = lax.rsqrt(jnp.sum(s * s, axis=1, keepdims=True) + EPS)
        mult = jnp.where(j < HEADS, r * scale, jnp.where(j < 2 * HEADS, r, 1.0))
        o_ref[...] = s * mult

    return pl.pallas_call(
        body, name=name,
        out_shape=jax.ShapeDtypeStruct((t, 3 * WIDTH), F32),
        grid=(t // tr, 3 * HEADS),
        in_specs=[pl.BlockSpec((tr, HEAD_DIM), lambda i, j: (i, j)),
                  pl.BlockSpec((8, HEAD_DIM), lambda i, j: (jnp.maximum(i * (tr // 8) - 1, 0), j)),
                  pl.BlockSpec((DN_CONV, HEAD_DIM), lambda i, j: (0, j))],
        out_specs=pl.BlockSpec((tr, HEAD_DIM), lambda i, j: (i, j)),
        compiler_params=_params(("parallel", "parallel")),
    )(qkv_pre, qkv_pre, cw)


def _dn_pre_bwd(dact, qkv_pre, cw, name):
    t = qkv_pre.shape[0]
    tr = _tile(t, 512)
    scale = HEAD_DIM ** -0.5

    def body(d_ref, x_ref, p_ref, w_ref, dc_ref, dw_ref):
        j, i = pl.program_id(0), pl.program_id(1)
        row = _iota((tr, HEAD_DIM), 0)
        prev = jnp.where(i == 0, 0.0, p_ref[...])
        taps, c = _conv_taps(x_ref[...], prev, w_ref[...], DN_CONV, row)
        sg = _sigmoid(c)
        s = c * sg
        r = lax.rsqrt(jnp.sum(s * s, axis=1, keepdims=True) + EPS)
        nh = s * r
        dn = d_ref[...] * jnp.where(j < HEADS, scale, 1.0)
        ds_norm = r * (dn - nh * jnp.sum(nh * dn, axis=1, keepdims=True))
        ds = jnp.where(j < 2 * HEADS, ds_norm, d_ref[...])
        dc = ds * (sg * (1.0 + c * (1.0 - sg)))
        dc_ref[...] = dc

        @pl.when(i == 0)
        def _():
            dw_ref[...] = jnp.zeros_like(dw_ref)

        for k in range(DN_CONV):
            dw_ref[k:k + 1, :] += jnp.sum(dc * taps[k], axis=0, keepdims=True)

    return pl.pallas_call(
        body, name=name,
        out_shape=(jax.ShapeDtypeStruct((t, 3 * WIDTH), F32), jax.ShapeDtypeStruct((DN_CONV, 3 * WIDTH), F32)),
        grid=(3 * HEADS, t // tr),
        in_specs=[pl.BlockSpec((tr, HEAD_DIM), lambda j, i: (i, j)),
                  pl.BlockSpec((tr, HEAD_DIM), lambda j, i: (i, j)),
                  pl.BlockSpec((8, HEAD_DIM), lambda j, i: (jnp.maximum(i * (tr // 8) - 1, 0), j)),
                  pl.BlockSpec((DN_CONV, HEAD_DIM), lambda j, i: (0, j))],
        out_specs=(pl.BlockSpec((tr, HEAD_DIM), lambda j, i: (i, j)),
                   pl.BlockSpec((DN_CONV, HEAD_DIM), lambda j, i: (0, j))),
        compiler_params=_params(("parallel", "arbitrary")),
    )(dact, qkv_pre, qkv_pre, cw)


def _tri(n, kind):
    r, c = _iota((n, n), 0), _iota((n, n), 1)
    m = {"lower": r >= c, "strict": r > c, "upper": r <= c}[kind]
    return m


def _dn_gates_fwd(hab, alog, dtb, name):
    t = hab.shape[0]
    cc = DN_CHUNK

    def body(h_ref, al_ref, dt_ref, o_ref):
        hv = h_ref[...]
        lane = _iota(hv.shape, 1)
        xa = hv + dt_ref[...]
        sp = jnp.maximum(xa, 0.0) + _log1pexp_neg_abs(xa)
        g = jnp.where(lane < HEADS, -jnp.exp(al_ref[...]) * sp, 0.0)
        tril = jnp.where(_tri(cc, "lower"), 1.0, 0.0).astype(BF16)
        gc = _dot_xl(tril, g, NN)
        o_ref[...] = jnp.where(lane < HEADS, gc, jnp.where(lane < 2 * HEADS, _sigmoid(hv), 0.0))

    return pl.pallas_call(
        body, name=name,
        out_shape=jax.ShapeDtypeStruct((t, 128), F32),
        grid=(t // cc,),
        in_specs=[pl.BlockSpec((cc, 128), lambda i: (i, 0)), pl.BlockSpec((1, 128), lambda i: (0, 0)),
                  pl.BlockSpec((1, 128), lambda i: (0, 0))],
        out_specs=pl.BlockSpec((cc, 128), lambda i: (i, 0)),
        compiler_params=_params(("parallel",)),
    )(hab, alog, dtb)


def _dn_gates_bwd(dgates, hab, alog, dtb, name):
    t = hab.shape[0]
    cc = DN_CHUNK

    def body(d_ref, h_ref, al_ref, dt_ref, o_ref, dal_ref, ddt_ref):
        i = pl.program_id(0)
        hv = h_ref[...]
        dv = d_ref[...]
        lane = _iota(hv.shape, 1)
        triu = jnp.where(_tri(cc, "upper"), 1.0, 0.0).astype(BF16)
        dg = _dot_xl(triu, jnp.where(lane < HEADS, dv, 0.0), NN)
        xa = hv + dt_ref[...]
        sp = jnp.maximum(xa, 0.0) + _log1pexp_neg_abs(xa)
        ea = jnp.exp(al_ref[...])
        da = jnp.where(lane < HEADS, dg * (-ea) * _sigmoid(xa), 0.0)
        be = _sigmoid(hv)
        db = dv * be * (1.0 - be)
        o_ref[...] = jnp.where(lane < HEADS, da, jnp.where(lane < 2 * HEADS, db, 0.0))

        @pl.when(i == 0)
        def _():
            dal_ref[...] = jnp.zeros_like(dal_ref)
            ddt_ref[...] = jnp.zeros_like(ddt_ref)

        dal_ref[...] += jnp.sum(jnp.where(lane < HEADS, dg * (-ea) * sp, 0.0), axis=0, keepdims=True)
        ddt_ref[...] += jnp.sum(da, axis=0, keepdims=True)

    return pl.pallas_call(
        body, name=name,
        out_shape=(jax.ShapeDtypeStruct((t, 128), F32), jax.ShapeDtypeStruct((1, 128), F32),
                   jax.ShapeDtypeStruct((1, 128), F32)),
        grid=(t // cc,),
        in_specs=[pl.BlockSpec((cc, 128), lambda i: (i, 0)), pl.BlockSpec((cc, 128), lambda i: (i, 0)),
                  pl.BlockSpec((1, 128), lambda i: (0, 0)), pl.BlockSpec((1, 128), lambda i: (0, 0))],
        out_specs=(pl.BlockSpec((cc, 128), lambda i: (i, 0)), pl.BlockSpec((1, 128), lambda i: (0, 0)),
                   pl.BlockSpec((1, 128), lambda i: (0, 0))),
        compiler_params=_params(("arbitrary",)),
    )(dgates, hab, alog, dtb)


def _dn_chunk_common(gates, h):
    cc = DN_CHUNK
    lane = _iota(gates.shape, 1)
    gh = jnp.where(lane == h, gates, 0.0)
    gc_col = jnp.sum(gh, axis=1, keepdims=True)
    gc_row = _dot_xl(jnp.ones((cc, 128), BF16), gh, NT)
    beta = jnp.sum(jnp.where(lane == h + HEADS, gates, 0.0), axis=1, keepdims=True)
    lower = _tri(cc, "lower")
    decay = jnp.where(lower, jnp.exp(jnp.where(lower, gc_col - gc_row, 0.0)), 0.0)
    gc_last = gc_col[cc - 1:cc, :]
    return gc_col, gc_last, beta, decay


def _dn_local_fwd(act, gates, name):
    t = act.shape[0]
    cc = DN_CHUNK
    nc = t // cc

    def body(q_ref, k_ref, v_ref, g_ref, u_ref, w_ref, kd_ref, qg_ref, ti_ref, p_ref):
        gates = g_ref[...]
        eye = jnp.where(_iota((cc, cc), 0) == _iota((cc, cc), 1), 1.0, 0.0)
        for h in range(HEADS):
            sl = slice(h * HEAD_DIM, (h + 1) * HEAD_DIM)
            q, k, v = q_ref[:, sl], k_ref[:, sl], v_ref[:, sl]
            gc_col, gc_last, beta, decay = _dn_chunk_common(gates, h)
            gam = jnp.exp(gc_col)
            kb = k * beta
            a = jnp.where(_tri(cc, "strict"), _dotb(kb, k, NT) * decay, 0.0)
            npow = -a
            tinv = eye + npow
            for _ in range(5):
                npow = _dot3(npow, npow, NN)
                tinv = tinv + _dot3(tinv, npow, NN)
            u_ref[:, sl] = _dot3(tinv, v * beta, NN)
            w_ref[:, sl] = _dot3(tinv, kb * gam, NN)
            kd_ref[:, sl] = k * jnp.exp(gc_last - gc_col)
            qg_ref[:, sl] = q * gam
            ti_ref[h] = tinv
            p_ref[h] = jnp.where(_tri(cc, "lower"), _dotb(q, k, NT) * decay, 0.0)

    row = lambda off: pl.BlockSpec((cc, WIDTH), lambda n: (n, off))
    mat = pl.BlockSpec((HEADS, cc, cc), lambda n: (0, n, 0))
    tw = jax.ShapeDtypeStruct((t, WIDTH), F32)
    hm = jax.ShapeDtypeStruct((HEADS, t, cc), F32)
    return pl.pallas_call(
        body, name=name,
        out_shape=(tw, tw, tw, tw, hm, hm),
        grid=(nc,),
        in_specs=[row(0), row(1), row(2), pl.BlockSpec((cc, 128), lambda n: (n, 0))],
        out_specs=(row(0), row(0), row(0), row(0), mat, mat),
        compiler_params=_params(("parallel",)),
    )(act, act, act, gates)


def _dn_scan_fwd(u, w, kd, qg, p, gates, name):
    t = u.shape[0]
    cc = DN_CHUNK
    nc = t // cc

    def body(u_ref, w_ref, kd_ref, qg_ref, p_ref, g_ref, o_ref, sh_ref, s_ref):
        n = pl.program_id(0)

        @pl.when(n == 0)
        def _():
            s_ref[...] = jnp.zeros_like(s_ref)

        glast = jnp.exp(g_ref[cc - 1:cc, :])
        for h in range(HEADS):
            sl = slice(h * HEAD_DIM, (h + 1) * HEAD_DIM)
            s = s_ref[h]
            sb = s.astype(BF16)
            vn = u_ref[:, sl] - _dot(w_ref[:, sl].astype(BF16), sb, NN)
            vnb = vn.astype(BF16)
            o_ref[:, sl] = _dot(qg_ref[:, sl].astype(BF16), sb, NN) + _dot(p_ref[h].astype(BF16), vnb, NN)
            sh_ref[0, h] = s
            s_ref[h] = glast[:, h:h + 1] * s + _dot(kd_ref[:, sl].astype(BF16), vnb, TN)

    row = pl.BlockSpec((cc, WIDTH), lambda n: (n, 0))
    return pl.pallas_call(
        body, name=name,
        out_shape=(jax.ShapeDtypeStruct((t, WIDTH), F32),
                   jax.ShapeDtypeStruct((nc, HEADS, HEAD_DIM, HEAD_DIM), F32)),
        grid=(nc,),
        in_specs=[row, row, row, row, pl.BlockSpec((HEADS, cc, cc), lambda n: (0, n, 0)),
                  pl.BlockSpec((cc, 128), lambda n: (n, 0))],
        out_specs=(row, pl.BlockSpec((1, HEADS, HEAD_DIM, HEAD_DIM), lambda n: (n, 0, 0, 0))),
        scratch_shapes=[pltpu.VMEM((HEADS, HEAD_DIM, HEAD_DIM), F32)],
        compiler_params=_params(("arbitrary",)),
    )(u, w, kd, qg, p, gates)


def _dn_scan_bwd(do, w, kd, qg, p, gates, name):
    t = do.shape[0]
    cc = DN_CHUNK
    nc = t // cc

    def body(do_ref, w_ref, kd_ref, qg_ref, p_ref, g_ref, dvn_ref, dsh_ref, ds_ref):
        n = pl.program_id(0)

        @pl.when(n == 0)
        def _():
            ds_ref[...] = jnp.zeros_like(ds_ref)

        glast = jnp.exp(g_ref[cc - 1:cc, :])
        for h in range(HEADS):
            sl = slice(h * HEAD_DIM, (h + 1) * HEAD_DIM)
            ds = ds_ref[h]
            dob = do_ref[:, sl].astype(BF16)
            dvn = _dot(p_ref[h].astype(BF16), dob, TN) + _dot(kd_ref[:, sl].astype(BF16), ds.astype(BF16), NN)
            dvn_ref[:, sl] = dvn
            dsh_ref[0, h] = ds
            ds_ref[h] = (_dot(qg_ref[:, sl].astype(BF16), dob, TN) + glast[:, h:h + 1] * ds
                         - _dot(w_ref[:, sl].astype(BF16), dvn.astype(BF16), TN))

    row = pl.BlockSpec((cc, WIDTH), lambda n: (nc - 1 - n, 0))
    return pl.pallas_call(
        body, name=name,
        out_shape=(jax.ShapeDtypeStruct((t, WIDTH), F32),
                   jax.ShapeDtypeStruct((nc, HEADS, HEAD_DIM, HEAD_DIM), F32)),
        grid=(nc,),
        in_specs=[row, row, row, row, pl.BlockSpec((HEADS, cc, cc), lambda n: (0, nc - 1 - n, 0)),
                  pl.BlockSpec((cc, 128), lambda n: (nc - 1 - n, 0))],
        out_specs=(row, pl.BlockSpec((1, HEADS, HEAD_DIM, HEAD_DIM), lambda n: (nc - 1 - n, 0, 0, 0))),
        scratch_shapes=[pltpu.VMEM((HEADS, HEAD_DIM, HEAD_DIM), F32)],
        compiler_params=_params(("arbitrary",)),
    )(do, w, kd, qg, p, gates)


def _dn_local_bwd(act, gates, u, w, kd, qg, tinv, p, sh, dsh, dvn, do, name):
    t = act.shape[0]
    cc = DN_CHUNK
    nc = t // cc

    def body(q_ref, k_ref, v_ref, g_ref, u_ref, w_ref, kd_ref, qg_ref, ti_ref, p_ref, s_ref, ds_ref,
             dvn_ref, do_ref, dq_ref, dk_ref, dv_ref, dg_ref):
        gates_v = g_ref[...]
        lower, strict = _tri(cc, "lower"), _tri(cc, "strict")
        ones = jnp.ones((cc, 128), BF16)
        rowc = _iota((cc, 1), 0)
        lane = _iota((cc, 128), 1)
        dgates = jnp.zeros((cc, 128), F32)
        for h in range(HEADS):
            sl = slice(h * HEAD_DIM, (h + 1) * HEAD_DIM)
            q, k, v = q_ref[:, sl], k_ref[:, sl], v_ref[:, sl]
            gc_col, gc_last, beta, decay = _dn_chunk_common(gates_v, h)
            gam = jnp.exp(gc_col)
            kb = k * beta
            uu, ww, kd, qg = u_ref[:, sl], w_ref[:, sl], kd_ref[:, sl], qg_ref[:, sl]
            tinv, pp = ti_ref[h], p_ref[h]
            s_in, ds_out = s_ref[0, h], ds_ref[0, h]
            dvn, do = dvn_ref[:, sl], do_ref[:, sl]

            a = jnp.where(strict, _dotb(kb, k, NT) * decay, 0.0)
            vn = uu - _dotb(ww, s_in, NN)
            dp = jnp.where(lower, _dotb(do, vn, NT), 0.0)
            dqg = _dotb(do, s_in, NT)
            dw = -_dotb(dvn, s_in, NT)
            dkd = _dotb(vn, ds_out, NT)
            dru = _dot3(tinv, dvn, TN)
            drw = _dot3(tinv, dw, TN)
            da = -jnp.where(strict, _dotb(dru, uu, NT) + _dotb(drw, ww, NT), 0.0)
            dad = da * decay
            dpd = dp * decay
            dkb = _dotb(dad, k, NN) + gam * drw
            kdec = jnp.exp(gc_last - gc_col)
            dk_ref[:, sl] = _dotb(dad, kb, TN) + _dotb(dpd, q, TN) + beta * dkb + kdec * dkd
            dq_ref[:, sl] = gam * dqg + _dotb(dpd, k, NN)
            dv_ref[:, sl] = beta * dru
            dbeta = jnp.sum(dkb * k, axis=1, keepdims=True) + jnp.sum(dru * v, axis=1, keepdims=True)

            gm = da * a + dp * pp
            colsum = _dot_xr(gm, ones, TN)[:, 0:1]
            rkd = jnp.sum(dkd * kd, axis=1, keepdims=True)
            dgc = (jnp.sum(gm, axis=1, keepdims=True) - colsum + jnp.sum(dqg * qg, axis=1, keepdims=True)
                   + jnp.sum(drw * kb, axis=1, keepdims=True) * gam - rkd)
            tail = jnp.sum(rkd, axis=0, keepdims=True) + jnp.exp(gc_last) * jnp.sum(
                jnp.sum(s_in * ds_out, axis=1, keepdims=True), axis=0, keepdims=True)
            dgc = dgc + jnp.where(rowc == cc - 1, tail, 0.0)
            dgates = dgates + jnp.where(lane == h, dgc, 0.0) + jnp.where(lane == h + HEADS, dbeta, 0.0)
        dg_ref[...] = dgates

    row = lambda off: pl.BlockSpec((cc, WIDTH), lambda n: (n, off))
    mat = pl.BlockSpec((HEADS, cc, cc), lambda n: (0, n, 0))
    st = pl.BlockSpec((1, HEADS, HEAD_DIM, HEAD_DIM), lambda n: (n, 0, 0, 0))
    gl = pl.BlockSpec((cc, 128), lambda n: (n, 0))
    tw = jax.ShapeDtypeStruct((t, WIDTH), F32)
    return pl.pallas_call(
        body, name=name,
        out_shape=(tw, tw, tw, jax.ShapeDtypeStruct((t, 128), F32)),
        grid=(nc,),
        in_specs=[row(0), row(1), row(2), gl, row(0), row(0), row(0), row(0), mat, mat, st, st, row(0), row(0)],
        out_specs=(row(0), row(0), row(0), gl),
        compiler_params=_params(("parallel",)),
    )(act, act, act, gates, u, w, kd, qg, tinv, p, sh, dsh, dvn, do)


def _dn_post_fwd(o, gate, w, name):
    t = o.shape[0]
    tr = _tile(t, 512)

    def body(o_ref, g_ref, w_ref, y_ref):
        ov, gv = o_ref[...], g_ref[...]
        r = lax.rsqrt(jnp.mean(ov * ov, axis=1, keepdims=True) + EPS)
        y_ref[...] = (ov * r * w_ref[...] * (gv * _sigmoid(gv))).astype(BF16)

    blk = pl.BlockSpec((tr, HEAD_DIM), lambda i, h: (i, h))
    return pl.pallas_call(
        body, name=name,
        out_shape=jax.ShapeDtypeStruct((t, WIDTH), BF16),
        grid=(t // tr, HEADS),
        in_specs=[blk, blk, pl.BlockSpec((1, HEAD_DIM), lambda i, h: (0, 0))],
        out_specs=blk,
        compiler_params=_params(("parallel", "parallel")),
    )(o, gate, w)


def _dn_post_bwd(dy, o, gate, w, name):
    t = o.shape[0]
    tr = _tile(t, 512)

    def body(dy_ref, o_ref, g_ref, w_ref, do_ref, dg_ref, dw_ref):
        i, h = pl.program_id(0), pl.program_id(1)
        ov, gv, dyv = o_ref[...], g_ref[...], dy_ref[...].astype(F32)
        r = lax.rsqrt(jnp.mean(ov * ov, axis=1, keepdims=True) + EPS)
        oh = ov * r
        sg = _sigmoid(gv)
        act = gv * sg
        dg_ref[...] = (dyv * oh * w_ref[...] * (sg * (1.0 + gv * (1.0 - sg)))).astype(BF16)
        dn = dyv * act
        doh = dn * w_ref[...]
        do_ref[...] = r * (doh - oh * jnp.mean(doh * oh, axis=1, keepdims=True))

        @pl.when(jnp.logical_and(i == 0, h == 0))
        def _():
            dw_ref[...] = jnp.zeros_like(dw_ref)

        dw_ref[...] += jnp.sum(dn * oh, axis=0, keepdims=True)

    blk = pl.BlockSpec((tr, HEAD_DIM), lambda i, h: (i, h))
    return pl.pallas_call(
        body, name=name,
        out_shape=(jax.ShapeDtypeStruct((t, WIDTH), F32), jax.ShapeDtypeStruct((t, WIDTH), BF16),
                   jax.ShapeDtypeStruct((1, HEAD_DIM), F32)),
        grid=(t // tr, HEADS),
        in_specs=[blk, blk, blk, pl.BlockSpec((1, HEAD_DIM), lambda i, h: (0, 0))],
        out_specs=(blk, blk, pl.BlockSpec((1, HEAD_DIM), lambda i, h: (0, 0))),
        compiler_params=_params(("arbitrary", "arbitrary")),
    )(dy, o, gate, w)


def _sb_scores(q, ks, qi, j, carry_b, uincl):
    bk = ATT_BLOCK
    scale = HEAD_DIM ** -0.5
    z = _dot(q, ks, NT) * scale
    qpos = qi * bk + _iota(z.shape, 0)
    kpos = j * bk + _iota(z.shape, 1)
    mask = kpos < qpos
    soft = _log1pexp_neg_abs(z)
    lk_full = -(jnp.maximum(z, 0.0) + soft)
    lk = jnp.where(mask, lk_full, 0.0)
    ls = jnp.minimum(z, 0.0) - soft
    incl = _dot_xr(lk, uincl, NN)
    a = jnp.where(mask, jnp.exp(ls + (carry_b + incl - lk)), 0.0)
    return a, mask, lk_full, ls, carry_b + incl[:, 0:1]


def _sb_more(qi, carry):
    it, cb = carry[0], carry[1]
    return jnp.logical_and(it <= qi, jnp.max(cb) > SB_LOG_ZERO)


def _sb_fwd(qkv, name):
    t = qkv.shape[0]
    bk = ATT_BLOCK

    def body(q_ref, k_ref, v_ref, o_ref):
        qi = pl.program_id(1)
        q = q_ref[...]
        uincl = jnp.where(_tri(bk, "lower"), 1.0, 0.0).astype(BF16)

        def step(carry):
            it, cb, acc = carry
            j = qi - it
            rows = pl.ds(pl.multiple_of(j * bk, bk), bk)
            a, _, _, _, cb = _sb_scores(q, k_ref[rows, :], qi, j, cb, uincl)
            acc = acc + _dot(a.astype(BF16), v_ref[rows, :], NN)
            return it + 1, cb, acc

        init = (jnp.int32(0), jnp.zeros((bk, 1), F32), jnp.zeros((bk, HEAD_DIM), F32))
        _, _, acc = lax.while_loop(functools.partial(_sb_more, qi), step, init)
        o_ref[...] = acc

    return pl.pallas_call(
        body, name=name,
        out_shape=jax.ShapeDtypeStruct((t, WIDTH), F32),
        grid=(HEADS, t // bk),
        in_specs=[pl.BlockSpec((bk, HEAD_DIM), lambda h, i: (i, h)),
                  pl.BlockSpec((t, HEAD_DIM), lambda h, i: (0, HEADS + h)),
                  pl.BlockSpec((t, HEAD_DIM), lambda h, i: (0, 2 * HEADS + h))],
        out_specs=pl.BlockSpec((bk, HEAD_DIM), lambda h, i: (i, h)),
        compiler_params=_params(("parallel", "arbitrary")),
    )(qkv, qkv, qkv)


def _sb_bwd(qkv, o, do, name):
    t = qkv.shape[0]
    bk = ATT_BLOCK
    scale = HEAD_DIM ** -0.5

    def body(q_ref, k_ref, v_ref, o_ref, do_ref, dq_ref, dk_ref, dv_ref):
        qi = pl.program_id(1)

        @pl.when(qi == 0)
        def _():
            dk_ref[...] = jnp.zeros_like(dk_ref)
            dv_ref[...] = jnp.zeros_like(dv_ref)

        q = q_ref[...]
        dov = do_ref[...]
        dob = dov.astype(BF16)
        do1, do2, do3 = _split3(dov)
        dsum = jnp.sum(dov * o_ref[...], axis=1, keepdims=True)
        uincl = jnp.where(_tri(bk, "lower"), 1.0, 0.0).astype(BF16)

        def step(carry):
            it, cb, ce, dq = carry
            j = qi - it
            rows = pl.ds(pl.multiple_of(j * bk, bk), bk)
            ks = k_ref[rows, :]
            a, mask, lk_full, ls, cb = _sb_scores(q, ks, qi, j, cb, uincl)
            ab = a.astype(BF16)
            vs = v_ref[rows, :]
            dla = ab.astype(F32) * (_dot(do1, vs, NT) + _dot(do2, vs, NT) + _dot(do3, vs, NT))
            suf = _dot_xr(dla, uincl, NN)
            e = dsum - (ce + suf)
            dz = jnp.where(mask, dla * jnp.exp(lk_full) - e * jnp.exp(ls), 0.0)
            dzb = (dz * scale).astype(BF16)
            dq = dq + _dot(dzb, ks, NN)
            dk_ref[rows, :] += _dot(dzb, q, TN)
            dv_ref[rows, :] += _dot(ab, dob, TN)
            return it + 1, cb, ce + suf[:, 0:1], dq

        zc = jnp.zeros((bk, 1), F32)
        init = (jnp.int32(0), zc, zc, jnp.zeros((bk, HEAD_DIM), F32))
        dq_ref[...] = lax.while_loop(functools.partial(_sb_more, qi), step, init)[3]

    tw = jax.ShapeDtypeStruct((t, WIDTH), F32)
    qb = pl.BlockSpec((bk, HEAD_DIM), lambda h, i: (i, h))
    full = lambda off: pl.BlockSpec((t, HEAD_DIM), lambda h, i: (0, off + h))
    return pl.pallas_call(
        body, name=name,
        out_shape=(tw, tw, tw),
        grid=(HEADS, t // bk),
        in_specs=[qb, full(HEADS), full(2 * HEADS), qb, qb],
        out_specs=(qb, full(0), full(0)),
        compiler_params=_params(("parallel", "arbitrary")),
    )(qkv, qkv, qkv, o, do)


def _merge_fwd(pd, ps, gl, name):
    t = pd.shape[0]
    tr, tc = _tile(t, 512), 512
    nj = D_MODEL // tc

    def body(pd_ref, ps_ref, gd_ref, gs_ref, o_ref):
        o_ref[...] = (_sigmoid(gd_ref[...]) * pd_ref[...] + _sigmoid(gs_ref[...]) * ps_ref[...]).astype(BF16)

    blk = lambda off: pl.BlockSpec((tr, tc), lambda i, j: (i, j + off))
    return pl.pallas_call(
        body, name=name,
        out_shape=jax.ShapeDtypeStruct((t, D_MODEL), BF16),
        grid=(t // tr, nj),
        in_specs=[blk(0), blk(0), blk(0), blk(nj)],
        out_specs=blk(0),
        compiler_params=_params(("parallel", "parallel")),
    )(pd, ps, gl, gl)


def _merge_bwd(dm, pd, ps, gl, name):
    t = pd.shape[0]
    tr, tc = _tile(t, 512), 512
    nj = D_MODEL // tc

    def body(dm_ref, pd_ref, ps_ref, gd_ref, gs_ref, dpd_ref, dps_ref, dgd_ref, dgs_ref):
        dmv = dm_ref[...]
        sd, ss = _sigmoid(gd_ref[...]), _sigmoid(gs_ref[...])
        dpd_ref[...] = (dmv * sd).astype(BF16)
        dps_ref[...] = (dmv * ss).astype(BF16)
        dgd_ref[...] = (dmv * pd_ref[...] * sd * (1.0 - sd)).astype(BF16)
        dgs_ref[...] = (dmv * ps_ref[...] * ss * (1.0 - ss)).astype(BF16)

    blk = lambda off: pl.BlockSpec((tr, tc), lambda i, j: (i, j + off))
    out = jax.ShapeDtypeStruct((t, D_MODEL), BF16)
    return pl.pallas_call(
        body, name=name,
        out_shape=(out, out, out, out),
        grid=(t // tr, nj),
        in_specs=[blk(0), blk(0), blk(0), blk(0), blk(nj)],
        out_specs=(blk(0), blk(0), blk(0), blk(0)),
        compiler_params=_params(("parallel", "parallel")),
    )(dm, pd, ps, gl, gl)


def _local_step(x, target, wts):
    n1 = _rmsnorm_fwd(x, wts["norm1_w"], "norm1_fwd")
    qkv_pre = _matmul(n1, wts["w_dnqkv"], "nn", F32, "in_dnqkv")
    hgate = _matmul(n1, wts["w_dngate"], "nn", F32, "in_dngate")
    sbqkv = _matmul(n1, wts["w_sbqkv"], "nn", BF16, "in_sbqkv")
    gl = _matmul(n1, wts["w_gl"], "nn", F32, "in_gl")
    hab = _matmul(n1, wts["w_ab"], "nn", F32, "in_ab")

    act = _dn_pre_fwd(qkv_pre, wts["dn_conv_w"], "dn_pre_fwd")
    gates = _dn_gates_fwd(hab, wts["alog"], wts["dtb"], "dn_gates_fwd")
    u, w, kd, qg, tinv, p = _dn_local_fwd(act, gates, "dn_local_fwd")
    o_dn, sh = _dn_scan_fwd(u, w, kd, qg, p, gates, "dn_scan_fwd")
    y_dn = _dn_post_fwd(o_dn, hgate, wts["dn_norm_w"], "dn_post_fwd")

    o_sb = _sb_fwd(sbqkv, "sb_fwd")

    pd = _matmul(y_dn, wts["w_proj_dn"], "nn", F32, "proj_dn")
    ps = _matmul(o_sb, wts["w_proj_sb"], "nn", F32, "proj_sb")
    mixed = _merge_fwd(pd, ps, gl, "merge_fwd")
    x1 = _matmul(mixed, wts["w_out"], "nn", F32, "out_proj", add=x)

    n2 = _rmsnorm_fwd(x1, wts["norm2_w"], "norm2_fwd")
    upre = _matmul(n2, wts["ffn_w_up"], "nn", F32, "ffn_up")
    fact = _ffn_act_fwd(upre, wts["ffn_conv_w"], "ffn_act_fwd")
    x2 = _matmul(fact, wts["ffn_w_down"], "nn", F32, "ffn_down", add=x1)

    dx2, g_normf, loss = _final_loss(x2, target, wts["norm_f_w"], "final_loss")

    dfact = _matmul(dx2, wts["ffn_w_down"], "nt", BF16, "ffn_down_dx")
    g_wdown = _matmul(fact, dx2, "tn", F32, "ffn_down_dw")
    dgc, duc, dwg, dwu = _ffn_act_bwd(dfact, upre, wts["ffn_conv_w"], "ffn_act_bwd")
    dconv = jnp.concatenate([dgc, duc], axis=1)
    g_fconv = jnp.concatenate([dwg, dwu], axis=1)
    dupre = _conv_bwd_data(dconv, wts["ffn_conv_w"], FFN_CONV, BF16, "ffn_conv_bwd")
    dn2 = _matmul(dupre, wts["ffn_w_up"], "nt", F32, "ffn_up_dx")
    g_wup = _matmul(n2, dupre, "tn", F32, "ffn_up_dw")
    dx1, g_norm2 = _rmsnorm_bwd(dn2, x1, wts["norm2_w"], dx2, "norm2_bwd")

    dmixed = _matmul(dx1, wts["w_out"], "nt", F32, "out_proj_dx")
    g_wout = _matmul(mixed, dx1, "tn", F32, "out_proj_dw")
    dpd, dps, dgd, dgs = _merge_bwd(dmixed, pd, ps, gl, "merge_bwd")
    dy_dn = _matmul(dpd, wts["w_proj_dn"], "nt", F32, "proj_dn_dx")
    g_wpd = _matmul(y_dn, dpd, "tn", F32, "proj_dn_dw")
    do_sb = _matmul(dps, wts["w_proj_sb"], "nt", F32, "proj_sb_dx")
    g_wps = _matmul(o_sb, dps, "tn", F32, "proj_sb_dw")

    dsq, dsk, dsv = _sb_bwd(sbqkv, o_sb, do_sb, "sb_bwd")

    do_dn, dhgate, g_dnnorm = _dn_post_bwd(dy_dn, o_dn, hgate, wts["dn_norm_w"], "dn_post_bwd")
    dvn, dsh = _dn_scan_bwd(do_dn, w, kd, qg, p, gates, "dn_scan_bwd")
    dq, dk, dv, dgates = _dn_local_bwd(act, gates, u, w, kd, qg, tinv, p, sh, dsh, dvn, do_dn, "dn_local_bwd")
    dhab, g_alog, g_dtb = _dn_gates_bwd(dgates, hab, wts["alog"], wts["dtb"], "dn_gates_bwd")
    dact = jnp.concatenate([dq, dk, dv], axis=1)
    dcv, g_dnconv = _dn_pre_bwd(dact, qkv_pre, wts["dn_conv_w"], "dn_pre_bwd")
    dqkv_pre = _conv_bwd_data(dcv, wts["dn_conv_w"], DN_CONV, BF16, "dn_conv_bwd")

    dh = jnp.concatenate([dqkv_pre, dhgate, dsq.astype(BF16), dsk.astype(BF16), dsv.astype(BF16), dgd, dgs], axis=1)
    w_main = jnp.concatenate([wts["w_dnqkv"], wts["w_dngate"], wts["w_sbqkv"], wts["w_gl"]], axis=1)
    dn1 = _matmul(dh, w_main, "nt", F32, "in_dx_main")
    dn1 = _matmul(dhab, wts["w_ab"], "nt", F32, "in_dx_ab", add=dn1)
    g_wmain = _matmul(n1, dh, "tn", F32, "in_dw_main")
    g_wab = _matmul(n1, dhab, "tn", F32, "in_dw_ab")
    grad_x, g_norm1 = _rmsnorm_bwd(dn1, x, wts["norm1_w"], dx1, "norm1_bwd")

    grads = dict(norm1_w=g_norm1, w_main=g_wmain, w_ab=g_wab, dn_conv_w=g_dnconv, alog=g_alog, dtb=g_dtb,
                 dn_norm_w=g_dnnorm, w_proj_dn=g_wpd, w_proj_sb=g_wps, w_out=g_wout, norm2_w=g_norm2,
                 ffn_w_up=g_wup, ffn_conv_w=g_fconv, ffn_w_down=g_wdown, norm_f_w=g_normf)
    return loss, grad_x, grads


HBM_SPEC = pl.BlockSpec(memory_space=pltpu.HBM)


def _mesh_pos():
    x, y, c = lax.axis_index("x"), lax.axis_index("y"), lax.axis_index("c")
    return x, y, c, 4 * x + 2 * y + c


def _peer(k):
    x, y, c, _ = _mesh_pos()
    px = 1 - x if k & 4 else x
    py = 1 - y if k & 2 else y
    pc = 1 - c if k & 1 else c
    return (px, py, pc), 4 * px + 2 * py + pc


def _all_gather(shards, name):
    n = len(shards)

    def body(*refs):
        ins, outs = refs[:n], refs[n:2 * n]
        send, recv, loc = refs[2 * n:]
        me = _mesh_pos()[3]
        local = [pltpu.make_async_copy(ins[a], outs[a].at[me], loc.at[a]) for a in range(n)]
        for cp in local:
            cp.start()
        sends, recvs = [], []
        for a in range(n):
            for k in range(1, N_DEV):
                peer, pidx = _peer(k)
                sends.append(pltpu.make_async_remote_copy(
                    src_ref=ins[a], dst_ref=outs[a].at[me], send_sem=send.at[a, k - 1], recv_sem=recv.at[a, k - 1],
                    device_id=peer, device_id_type=pl.DeviceIdType.MESH))
                recvs.append(pltpu.make_async_remote_copy(
                    src_ref=ins[a], dst_ref=outs[a].at[pidx], send_sem=send.at[a, k - 1], recv_sem=recv.at[a, k - 1],
                    device_id=peer, device_id_type=pl.DeviceIdType.MESH))
        for cp in sends:
            cp.start()
        for cp in recvs:
            cp.wait_recv()
        for cp in sends:
            cp.wait_send()
        for cp in local:
            cp.wait()

    return pl.pallas_call(
        body, name=name,
        out_shape=[jax.ShapeDtypeStruct((N_DEV,) + s.shape, s.dtype) for s in shards],
        in_specs=[HBM_SPEC] * n,
        out_specs=[HBM_SPEC] * n,
        scratch_shapes=[pltpu.SemaphoreType.DMA((n, N_DEV - 1)), pltpu.SemaphoreType.DMA((n, N_DEV - 1)),
                        pltpu.SemaphoreType.DMA((n,))],
        compiler_params=pltpu.CompilerParams(has_side_effects=True),
    )(*shards)


def _exchange(slabs, small, name):
    n = len(slabs)

    def body(*refs):
        ins, small_in = refs[:n], refs[n]
        outs, small_out = refs[n + 1:2 * n + 1], refs[2 * n + 1]
        send, recv, loc = refs[2 * n + 2:]
        me = _mesh_pos()[3]
        local = [pltpu.make_async_copy(ins[a].at[me], outs[a].at[me], loc.at[a]) for a in range(n)]
        local.append(pltpu.make_async_copy(small_in, small_out.at[me], loc.at[n]))
        for cp in local:
            cp.start()
        sends, recvs = [], []
        for a in range(n + 1):
            for k in range(1, N_DEV):
                peer, pidx = _peer(k)
                src = ins[a].at[pidx] if a < n else small_in
                out = outs[a] if a < n else small_out
                sends.append(pltpu.make_async_remote_copy(
                    src_ref=src, dst_ref=out.at[me], send_sem=send.at[a, k - 1], recv_sem=recv.at[a, k - 1],
                    device_id=peer, device_id_type=pl.DeviceIdType.MESH))
                recvs.append(pltpu.make_async_remote_copy(
                    src_ref=src, dst_ref=out.at[pidx], send_sem=send.at[a, k - 1], recv_sem=recv.at[a, k - 1],
                    device_id=peer, device_id_type=pl.DeviceIdType.MESH))
        for cp in sends:
            cp.start()
        for cp in recvs:
            cp.wait_recv()
        for cp in sends:
            cp.wait_send()
        for cp in local:
            cp.wait()

    return pl.pallas_call(
        body, name=name,
        out_shape=[jax.ShapeDtypeStruct(s.shape, s.dtype) for s in slabs]
        + [jax.ShapeDtypeStruct((N_DEV,) + small.shape, small.dtype)],
        in_specs=[HBM_SPEC] * (n + 1),
        out_specs=[HBM_SPEC] * (n + 1),
        scratch_shapes=[pltpu.SemaphoreType.DMA((n + 1, N_DEV - 1)), pltpu.SemaphoreType.DMA((n + 1, N_DEV - 1)),
                        pltpu.SemaphoreType.DMA((n + 1,))],
        compiler_params=pltpu.CompilerParams(has_side_effects=True),
    )(*slabs, small)


def _adamw(parts, w, m, v, name):
    rows, cols = w.shape
    tr = rows
    for cand in (128, 88):
        if rows > cand and rows % cand == 0:
            tr = cand
            break

    def body(p_ref, w_ref, m_ref, v_ref, g_ref, d_ref, mo_ref, vo_ref):
        g = p_ref[0]
        for s in range(1, N_DEV):
            g = g + p_ref[s]
        mn = ADAM_B1 * m_ref[...] + (1.0 - ADAM_B1) * g
        vn = ADAM_B2 * v_ref[...] + (1.0 - ADAM_B2) * (g * g)
        m_hat = mn / (1.0 - ADAM_B1 ** ADAM_STEP)
        v_hat = vn / (1.0 - ADAM_B2 ** ADAM_STEP)
        g_ref[...] = g
        d_ref[...] = -ADAM_LR * (m_hat / (jnp.sqrt(v_hat) + ADAM_EPS) + ADAM_WD * w_ref[...])
        mo_ref[...] = mn
        vo_ref[...] = vn

    blk = pl.BlockSpec((tr, cols), lambda i: (i, 0))
    out = jax.ShapeDtypeStruct((rows, cols), F32)
    return pl.pallas_call(
        body, name=name,
        out_shape=(out, out, out, out),
        grid=(rows // tr,),
        in_specs=[pl.BlockSpec((N_DEV, tr, cols), lambda i: (0, i, 0)), blk, blk, blk],
        out_specs=(blk, blk, blk, blk),
        compiler_params=_params(("parallel",)),
    )(parts, w, m, v)


CONV_PACK = 8 * 1024
WEIGHT_ORDER = ("norm1_w", "w_in", "dn_conv_w", "dn_A_log", "dn_dt_bias", "dn_norm_w", "w_proj_dn", "w_proj_sb",
                "w_out", "norm2_w", "ffn_w_up", "ffn_conv_w", "ffn_w_down", "norm_f_w")


def _cols_to_slabs(g):
    r, c8 = g.shape
    return g.reshape(r, N_DEV, c8 // N_DEV).transpose(1, 0, 2)


def _slabs_to_cols(s):
    d, r, c = s.shape
    return s.transpose(1, 0, 2).reshape(r, d * c)


def kernel(x, norm1_w, w_in, dn_conv_w, dn_A_log, dn_dt_bias, dn_norm_w, w_proj_dn, w_proj_sb, w_out, norm2_w, ffn_w_up, ffn_conv_w, ffn_w_down, norm_f_w, loss_target, m_norm1_w, m_w_in, m_dn_conv_w, m_dn_A_log, m_dn_dt_bias, m_dn_norm_w, m_w_proj_dn, m_w_proj_sb, m_w_out, m_norm2_w, m_ffn_w_up, m_ffn_conv_w, m_ffn_w_down, m_norm_f_w, v_norm1_w, v_w_in, v_dn_conv_w, v_dn_A_log, v_dn_dt_bias, v_dn_norm_w, v_w_proj_dn, v_w_proj_sb, v_w_out, v_norm2_w, v_ffn_w_up, v_ffn_conv_w, v_ffn_w_down, v_norm_f_w):
    me = _mesh_pos()[3]
    w_loc = dict(norm1_w=norm1_w, w_in=w_in[0], dn_conv_w=dn_conv_w[0], dn_A_log=dn_A_log, dn_dt_bias=dn_dt_bias,
                 dn_norm_w=dn_norm_w, w_proj_dn=w_proj_dn[0], w_proj_sb=w_proj_sb[0], w_out=w_out[0],
                 norm2_w=norm2_w, ffn_w_up=ffn_w_up[0], ffn_conv_w=ffn_conv_w[0], ffn_w_down=ffn_w_down[0],
                 norm_f_w=norm_f_w[None, :])
    m_loc = dict(norm1_w=m_norm1_w, w_in=m_w_in[0], dn_conv_w=m_dn_conv_w[0], dn_A_log=m_dn_A_log,
                 dn_dt_bias=m_dn_dt_bias, dn_norm_w=m_dn_norm_w, w_proj_dn=m_w_proj_dn[0], w_proj_sb=m_w_proj_sb[0],
                 w_out=m_w_out[0], norm2_w=m_norm2_w, ffn_w_up=m_ffn_w_up[0], ffn_conv_w=m_ffn_conv_w[0],
                 ffn_w_down=m_ffn_w_down[0], norm_f_w=m_norm_f_w[None, :])
    v_loc = dict(norm1_w=v_norm1_w, w_in=v_w_in[0], dn_conv_w=v_dn_conv_w[0], dn_A_log=v_dn_A_log,
                 dn_dt_bias=v_dn_dt_bias, dn_norm_w=v_dn_norm_w, w_proj_dn=v_w_proj_dn[0], w_proj_sb=v_w_proj_sb[0],
                 w_out=v_w_out[0], norm2_w=v_norm2_w, ffn_w_up=v_ffn_w_up[0], ffn_conv_w=v_ffn_conv_w[0],
                 ffn_w_down=v_ffn_w_down[0], norm_f_w=v_norm_f_w[None, :])

    big = ("w_in", "w_proj_dn", "w_proj_sb", "w_out", "ffn_w_up", "ffn_w_down")
    conv_flat = jnp.concatenate([w_loc["dn_conv_w"].reshape(-1), w_loc["ffn_conv_w"].reshape(-1)])
    n_dn, n_ffn = DN_CONV * 3 * WIDTH // N_DEV, FFN_CONV * 2 * D_FF // N_DEV
    conv_pack = jnp.pad(conv_flat, (0, CONV_PACK - n_dn - n_ffn)).reshape(8, 1024)
    gathered = _all_gather([w_loc[k].astype(BF16) for k in big] + [conv_pack], "gather_weights")
    g_in, g_pd, g_ps, g_out, g_up, g_down, g_conv = gathered
    w_in_full = _slabs_to_cols(g_in)
    g_conv = g_conv.reshape(N_DEV, CONV_PACK)
    dn_conv_full = _slabs_to_cols(g_conv[:, :n_dn].reshape(N_DEV, DN_CONV, 3 * WIDTH // N_DEV))
    ffn_conv_full = _slabs_to_cols(g_conv[:, n_dn:n_dn + n_ffn].reshape(N_DEV, FFN_CONV, 2 * D_FF // N_DEV))
    q_end = 3 * WIDTH
    ab_end = q_end + 2 * HEADS
    gate_end = ab_end + WIDTH
    sb_end = gate_end + 3 * WIDTH
    pad_lanes = lambda a: jnp.pad(a, ((0, 0), (0, 128 - a.shape[1])))
    wts = dict(
        norm1_w=norm1_w, w_dnqkv=w_in_full[:, :q_end], w_ab=pad_lanes(w_in_full[:, q_end:ab_end]),
        w_dngate=w_in_full[:, ab_end:gate_end], w_sbqkv=w_in_full[:, gate_end:sb_end], w_gl=w_in_full[:, sb_end:],
        dn_conv_w=dn_conv_full, alog=pad_lanes(dn_A_log), dtb=pad_lanes(dn_dt_bias), dn_norm_w=dn_norm_w,
        w_proj_dn=g_pd.reshape(WIDTH, D_MODEL), w_proj_sb=g_ps.reshape(WIDTH, D_MODEL),
        w_out=g_out.reshape(D_MODEL, D_MODEL), norm2_w=norm2_w, ffn_w_up=_slabs_to_cols(g_up),
        ffn_conv_w=ffn_conv_full, ffn_w_down=g_down.reshape(D_FF, D_MODEL), norm_f_w=norm_f_w[None, :])

    loss, grad_x, g = _local_step(x[0], loss_target[0], wts)

    g_win = jnp.concatenate([g["w_main"][:, :q_end], g["w_ab"][:, :2 * HEADS], g["w_main"][:, q_end:]], axis=1)
    slabs = [_cols_to_slabs(g_win), g["w_proj_dn"].reshape(N_DEV, WIDTH // N_DEV, D_MODEL),
             g["w_proj_sb"].reshape(N_DEV, WIDTH // N_DEV, D_MODEL),
             g["w_out"].reshape(N_DEV, D_MODEL // N_DEV, D_MODEL), _cols_to_slabs(g["ffn_w_up"]),
             g["ffn_w_down"].reshape(N_DEV, D_FF // N_DEV, D_MODEL)]
    row3 = jnp.concatenate([g["dn_norm_w"], g["alog"], g["dtb"], jnp.pad(loss, ((0, 0), (0, 127))),
                            jnp.zeros((1, D_MODEL - 512), F32)], axis=1)
    n_fc = FFN_CONV * 2 * D_FF
    fc_rows = -(-n_fc // D_MODEL)
    fconv_rows = jnp.pad(g["ffn_conv_w"].reshape(-1), (0, fc_rows * D_MODEL - n_fc)).reshape(fc_rows, D_MODEL)
    dn_rows = DN_CONV * 3 * WIDTH // D_MODEL
    pad8 = lambda a: jnp.pad(a, ((0, -a.shape[0] % 8), (0, 0)))
    pieces = [g["norm1_w"], g["norm2_w"], g["norm_f_w"], row3, g["dn_conv_w"].reshape(dn_rows, D_MODEL), fconv_rows]
    small = jnp.concatenate([pad8(a) for a in pieces], axis=0)
    assert small.shape[0] == SMALL_ROWS
    r_in, r_pd, r_ps, r_out, r_up, r_down, r_small = _exchange(slabs, small, "exchange_grads")

    parts = dict(w_in=r_in, w_proj_dn=r_pd, w_proj_sb=r_ps, w_out=r_out, ffn_w_up=r_up, ffn_w_down=r_down)
    parts["norm1_w"] = r_small[:, 0:1, :]
    parts["norm2_w"] = r_small[:, 8:9, :]
    parts["norm_f_w"] = r_small[:, 16:17, :]
    parts["dn_norm_w"] = r_small[:, 24:25, 0:HEAD_DIM]
    parts["dn_A_log"] = r_small[:, 24:25, 128:128 + HEADS]
    parts["dn_dt_bias"] = r_small[:, 24:25, 256:256 + HEADS]
    dnc = r_small[:, 32:32 + dn_rows, :].reshape(N_DEV, DN_CONV, 3 * WIDTH)
    parts["dn_conv_w"] = lax.dynamic_slice_in_dim(dnc, me * (3 * WIDTH // N_DEV), 3 * WIDTH // N_DEV, axis=2)
    fc0 = 32 + dn_rows + (-dn_rows % 8)
    fcc = r_small[:, fc0:fc0 + fc_rows, :].reshape(N_DEV, fc_rows * D_MODEL)[:, :n_fc]
    fcc = fcc.reshape(N_DEV, FFN_CONV, 2 * D_FF)
    parts["ffn_conv_w"] = lax.dynamic_slice_in_dim(fcc, me * (2 * D_FF // N_DEV), 2 * D_FF // N_DEV, axis=2)
    loss_total = jnp.sum(r_small[:, 24, 384])

    res = {k: _adamw(parts[k], w_loc[k], m_loc[k], v_loc[k], "adamw_" + k) for k in WEIGHT_ORDER}
    lead = ("w_in", "dn_conv_w", "w_proj_dn", "w_proj_sb", "w_out", "ffn_w_up", "ffn_conv_w", "ffn_w_down")

    def shaped(k, a):
        if k in lead:
            return a[None]
        if k == "norm_f_w":
            return a[0]
        return a

    outs = [loss_total, grad_x[None]]
    for idx in range(4):
        outs += [shaped(k, res[k][idx]) for k in WEIGHT_ORDER]
    return tuple(outs)
```

```python
import functools

import jax
import jax.numpy as jnp
from jax import lax
from jax.experimental import pallas as pl
from jax.experimental.pallas import tpu as pltpu

F32 = jnp.float32
BF16 = jnp.bfloat16

N_DEV = 8
D_MODEL = 1024
HEADS = 8
HEAD_DIM = 128
WIDTH = HEADS * HEAD_DIM
DN_CONV = 4
DN_CHUNK = 64
D_FF = 2816
FFN_CONV = 3
EPS = 1e-6
ATT_BLOCK = 256
SB_LOG_ZERO = -104.0
SMALL_ROWS = 72

ADAM_LR = 0.001
ADAM_B1 = 0.9
ADAM_B2 = 0.999
ADAM_EPS = 1e-08
ADAM_WD = 0.01
ADAM_STEP = 10

VMEM_LIMIT = 48 * 1024 * 1024


def _params(sem=None, **kw):
    return pltpu.CompilerParams(dimension_semantics=sem, vmem_limit_bytes=VMEM_LIMIT, **kw)


def _tile(n, cap):
    if n <= cap:
        return n
    best = None
    for t in range(128, cap + 1, 128):
        if n % t == 0:
            best = t
    assert best is not None, (n, cap)
    return best


def _dot(a, b, dims):
    return lax.dot_general(a, b, ((dims[0], dims[1]), ((), ())), preferred_element_type=F32)


NN = ((1,), (0,))
NT = ((1,), (1,))
TN = ((0,), (0,))


def _dotb(a, b, dims):
    return _dot(a.astype(BF16), b.astype(BF16), dims)


def _split3(x):
    h1 = x.astype(BF16)
    r1 = x - h1.astype(F32)
    h2 = r1.astype(BF16)
    r2 = r1 - h2.astype(F32)
    return h1, h2, r2.astype(BF16)


def _dot_xr(a, b_exact, dims):
    a1, a2, a3 = _split3(a)
    return _dot(a1, b_exact, dims) + _dot(a2, b_exact, dims) + _dot(a3, b_exact, dims)


def _dot_xl(a_exact, b, dims):
    b1, b2, b3 = _split3(b)
    return _dot(a_exact, b1, dims) + _dot(a_exact, b2, dims) + _dot(a_exact, b3, dims)


def _dot3(a, b, dims):
    a1 = a.astype(BF16)
    a2 = (a - a1.astype(F32)).astype(BF16)
    b1 = b.astype(BF16)
    b2 = (b - b1.astype(F32)).astype(BF16)
    return _dot(a1, b1, dims) + (_dot(a1, b2, dims) + _dot(a2, b1, dims))


def _sigmoid(x):
    return 1.0 / (1.0 + jnp.exp(-x))


def _log1pexp_neg_abs(x):
    return jnp.log(1.0 + jnp.exp(-jnp.abs(x)))


def _iota(shape, dim):
    return lax.broadcasted_iota(jnp.int32, shape, dim)


def _matmul(a, b, mode, out_dtype, name, add=None):
    if mode == "nn":
        (m, k), (k2, n) = a.shape, b.shape
    elif mode == "nt":
        (m, k), (n, k2) = a.shape, b.shape
    else:
        (k, m), (k2, n) = a.shape, b.shape
    assert k == k2, (a.shape, b.shape, mode)
    tm, tn, tk = _tile(m, 1024), _tile(n, 1408), _tile(k, 1536)
    nk = k // tk
    dims = {"nn": NN, "nt": NT, "tn": TN}[mode]

    def body(*refs):
        if add is None:
            a_ref, b_ref, o_ref, acc_ref = refs
        else:
            a_ref, b_ref, add_ref, o_ref, acc_ref = refs
        kk = pl.program_id(2)

        @pl.when(kk == 0)
        def _():
            acc_ref[...] = jnp.zeros_like(acc_ref)

        acc_ref[...] += _dotb(a_ref[...], b_ref[...], dims)

        @pl.when(kk == nk - 1)
        def _():
            r = acc_ref[...]
            if add is not None:
                r = r + add_ref[...].astype(F32)
            o_ref[...] = r.astype(out_dtype)

    if mode == "nn":
        specs = [pl.BlockSpec((tm, tk), lambda i, j, l: (i, l)), pl.BlockSpec((tk, tn), lambda i, j, l: (l, j))]
    elif mode == "nt":
        specs = [pl.BlockSpec((tm, tk), lambda i, j, l: (i, l)), pl.BlockSpec((tn, tk), lambda i, j, l: (j, l))]
    else:
        specs = [pl.BlockSpec((tk, tm), lambda i, j, l: (l, i)), pl.BlockSpec((tk, tn), lambda i, j, l: (l, j))]
    args = [a, b]
    if add is not None:
        specs.append(pl.BlockSpec((tm, tn), lambda i, j, l: (i, j)))
        args.append(add)
    return pl.pallas_call(
        body, name=name,
        out_shape=jax.ShapeDtypeStruct((m, n), out_dtype),
        grid=(m // tm, n // tn, nk),
        in_specs=specs,
        out_specs=pl.BlockSpec((tm, tn), lambda i, j, l: (i, j)),
        scratch_shapes=[pltpu.VMEM((tm, tn), F32)],
        compiler_params=_params(("parallel", "parallel", "arbitrary")),
    )(*args)


def _rmsnorm_fwd(x, w, name):
    t, d = x.shape
    tr = _tile(t, 512)

    def body(x_ref, w_ref, o_ref):
        xv = x_ref[...]
        r = lax.rsqrt(jnp.mean(xv * xv, axis=1, keepdims=True) + EPS)
        o_ref[...] = (xv * r * w_ref[...]).astype(BF16)

    return pl.pallas_call(
        body, name=name,
        out_shape=jax.ShapeDtypeStruct((t, d), BF16),
        grid=(t // tr,),
        in_specs=[pl.BlockSpec((tr, d), lambda i: (i, 0)), pl.BlockSpec((1, d), lambda i: (0, 0))],
        out_specs=pl.BlockSpec((tr, d), lambda i: (i, 0)),
        compiler_params=_params(("parallel",)),
    )(x, w)


def _rmsnorm_bwd(dn, x, w, dres, name):
    t, d = x.shape
    tr = _tile(t, 512)

    def body(dn_ref, x_ref, w_ref, dres_ref, dx_ref, dw_ref):
        i = pl.program_id(0)
        xv = x_ref[...]
        g = dn_ref[...].astype(F32)
        r = lax.rsqrt(jnp.mean(xv * xv, axis=1, keepdims=True) + EPS)
        xh = xv * r
        dxh = g * w_ref[...]
        dx = r * (dxh - xh * jnp.mean(dxh * xh, axis=1, keepdims=True))
        dx_ref[...] = dres_ref[...] + dx

        @pl.when(i == 0)
        def _():
            dw_ref[...] = jnp.zeros_like(dw_ref)

        dw_ref[...] += jnp.sum(g * xh, axis=0, keepdims=True)

    return pl.pallas_call(
        body, name=name,
        out_shape=(jax.ShapeDtypeStruct((t, d), F32), jax.ShapeDtypeStruct((1, d), F32)),
        grid=(t // tr,),
        in_specs=[pl.BlockSpec((tr, d), lambda i: (i, 0)), pl.BlockSpec((tr, d), lambda i: (i, 0)),
                  pl.BlockSpec((1, d), lambda i: (0, 0)), pl.BlockSpec((tr, d), lambda i: (i, 0))],
        out_specs=(pl.BlockSpec((tr, d), lambda i: (i, 0)), pl.BlockSpec((1, d), lambda i: (0, 0))),
        compiler_params=_params(("arbitrary",)),
    )(dn, x, w, dres)


def _final_loss(x2, target, w, name):
    t, d = x2.shape
    tr = _tile(t, 512)

    def body(x_ref, t_ref, w_ref, dx_ref, dw_ref, loss_ref):
        i = pl.program_id(0)
        xv = x_ref[...]
        r = lax.rsqrt(jnp.mean(xv * xv, axis=1, keepdims=True) + EPS)
        xh = xv * r
        err = xh * w_ref[...] - t_ref[...]
        dy = err * (1.0 / d)
        dxh = dy * w_ref[...]
        dx_ref[...] = r * (dxh - xh * jnp.mean(dxh * xh, axis=1, keepdims=True))

        @pl.when(i == 0)
        def _():
            dw_ref[...] = jnp.zeros_like(dw_ref)
            loss_ref[...] = jnp.zeros_like(loss_ref)

        dw_ref[...] += jnp.sum(dy * xh, axis=0, keepdims=True)
        row = jnp.sum(err * err, axis=1, keepdims=True) * (0.5 / d)
        loss_ref[...] += jnp.sum(row, axis=0, keepdims=True)

    return pl.pallas_call(
        body, name=name,
        out_shape=(jax.ShapeDtypeStruct((t, d), F32), jax.ShapeDtypeStruct((1, d), F32),
                   jax.ShapeDtypeStruct((1, 1), F32)),
        grid=(t // tr,),
        in_specs=[pl.BlockSpec((tr, d), lambda i: (i, 0)), pl.BlockSpec((tr, d), lambda i: (i, 0)),
                  pl.BlockSpec((1, d), lambda i: (0, 0))],
        out_specs=(pl.BlockSpec((tr, d), lambda i: (i, 0)), pl.BlockSpec((1, d), lambda i: (0, 0)),
                   pl.BlockSpec((1, 1), lambda i: (0, 0))),
        compiler_params=_params(("arbitrary",)),
    )(x2, target, w)


def _shift_down(cur, prev, k, row):
    r = pltpu.roll(cur, k, 0)
    for m in range(k):
        r = jnp.where(row == m, prev[8 - k + m:8 - k + m + 1, :], r)
    return r


def _shift_up(cur, nxt, k, row, tr):
    r = pltpu.roll(cur, tr - k, 0)
    for m in range(k):
        r = jnp.where(row == tr - k + m, nxt[m:m + 1, :], r)
    return r


def _conv_taps(cur, prev, w, ntaps, row):
    taps = [cur if i == ntaps - 1 else _shift_down(cur, prev, ntaps - 1 - i, row) for i in range(ntaps)]
    y = w[0:1, :] * taps[0]
    for i in range(1, ntaps):
        y = y + w[i:i + 1, :] * taps[i]
    return taps, y


def _conv_bwd_data(dc, w, ntaps, out_dtype, name):
    t, ch = dc.shape
    tr, tc = _tile(t, 512), _tile(ch, 512)
    nrow8 = t // 8
    last = t // tr - 1

    def body(cur_ref, nxt_ref, w_ref, o_ref):
        i = pl.program_id(0)
        cur = cur_ref[...]
        nxt = jnp.where(i == last, 0.0, nxt_ref[...])
        row = _iota(cur.shape, 0)
        wv = w_ref[...]
        y = wv[ntaps - 1:ntaps, :] * cur
        for k in range(1, ntaps):
            y = y + wv[ntaps - 1 - k:ntaps - k, :] * _shift_up(cur, nxt, k, row, tr)
        o_ref[...] = y.astype(out_dtype)

    return pl.pallas_call(
        body, name=name,
        out_shape=jax.ShapeDtypeStruct((t, ch), out_dtype),
        grid=(t // tr, ch // tc),
        in_specs=[pl.BlockSpec((tr, tc), lambda i, j: (i, j)),
                  pl.BlockSpec((8, tc), lambda i, j: (jnp.minimum((i + 1) * (tr // 8), nrow8 - 1), j)),
                  pl.BlockSpec((ntaps, tc), lambda i, j: (0, j))],
        out_specs=pl.BlockSpec((tr, tc), lambda i, j: (i, j)),
        compiler_params=_params(("parallel", "parallel")),
    )(dc, dc, w)


def _ffn_act_fwd(upre, cw, name):
    t = upre.shape[0]
    tr, tc = _tile(t, 512), 256
    nj = D_FF // tc

    def body(g_ref, gp_ref, u_ref, up_ref, wg_ref, wu_ref, o_ref):
        i = pl.program_id(0)
        row = _iota((tr, tc), 0)
        gp = jnp.where(i == 0, 0.0, gp_ref[...])
        up = jnp.where(i == 0, 0.0, up_ref[...])
        _, gc = _conv_taps(g_ref[...], gp, wg_ref[...], FFN_CONV, row)
        _, uc = _conv_taps(u_ref[...], up, wu_ref[...], FFN_CONV, row)
        o_ref[...] = (gc * _sigmoid(gc) * uc).astype(BF16)

    prev = lambda off: (lambda i, j: (jnp.maximum(i * (tr // 8) - 1, 0), j + off))
    return pl.pallas_call(
        body, name=name,
        out_shape=jax.ShapeDtypeStruct((t, D_FF), BF16),
        grid=(t // tr, nj),
        in_specs=[pl.BlockSpec((tr, tc), lambda i, j: (i, j)), pl.BlockSpec((8, tc), prev(0)),
                  pl.BlockSpec((tr, tc), lambda i, j: (i, j + nj)), pl.BlockSpec((8, tc), prev(nj)),
                  pl.BlockSpec((FFN_CONV, tc), lambda i, j: (0, j)),
                  pl.BlockSpec((FFN_CONV, tc), lambda i, j: (0, j + nj))],
        out_specs=pl.BlockSpec((tr, tc), lambda i, j: (i, j)),
        compiler_params=_params(("parallel", "parallel")),
    )(upre, upre, upre, upre, cw, cw)


def _ffn_act_bwd(dact, upre, cw, name):
    t = upre.shape[0]
    tr, tc = _tile(t, 512), 256
    nj = D_FF // tc

    def body(da_ref, g_ref, gp_ref, u_ref, up_ref, wg_ref, wu_ref, dg_ref, du_ref, dwg_ref, dwu_ref):
        i = pl.program_id(1)
        row = _iota((tr, tc), 0)
        gp = jnp.where(i == 0, 0.0, gp_ref[...])
        up = jnp.where(i == 0, 0.0, up_ref[...])
        gt, gc = _conv_taps(g_ref[...], gp, wg_ref[...], FFN_CONV, row)
        ut, uc = _conv_taps(u_ref[...], up, wu_ref[...], FFN_CONV, row)
        da = da_ref[...].astype(F32)
        sg = _sigmoid(gc)
        dgc = da * uc * (sg * (1.0 + gc * (1.0 - sg)))
        duc = da * (gc * sg)
        dg_ref[...] = dgc
        du_ref[...] = duc

        @pl.when(i == 0)
        def _():
            dwg_ref[...] = jnp.zeros_like(dwg_ref)
            dwu_ref[...] = jnp.zeros_like(dwu_ref)

        for k in range(FFN_CONV):
            dwg_ref[k:k + 1, :] += jnp.sum(dgc * gt[k], axis=0, keepdims=True)
            dwu_ref[k:k + 1, :] += jnp.sum(duc * ut[k], axis=0, keepdims=True)

    prev = lambda off: (lambda j, i: (jnp.maximum(i * (tr // 8) - 1, 0), j + off))
    blk = lambda off: pl.BlockSpec((tr, tc), lambda j, i: (i, j + off))
    wblk = lambda off: pl.BlockSpec((FFN_CONV, tc), lambda j, i: (0, j + off))
    dgc, duc, dwg, dwu = pl.pallas_call(
        body, name=name,
        out_shape=(jax.ShapeDtypeStruct((t, D_FF), F32), jax.ShapeDtypeStruct((t, D_FF), F32),
                   jax.ShapeDtypeStruct((FFN_CONV, D_FF), F32), jax.ShapeDtypeStruct((FFN_CONV, D_FF), F32)),
        grid=(nj, t // tr),
        in_specs=[blk(0), blk(0), pl.BlockSpec((8, tc), prev(0)), blk(nj), pl.BlockSpec((8, tc), prev(nj)),
                  wblk(0), wblk(nj)],
        out_specs=(blk(0), blk(0), wblk(0), wblk(0)),
        compiler_params=_params(("parallel", "arbitrary")),
    )(dact, upre, upre, upre, upre, cw, cw)
    return dgc, duc, dwg, dwu


def _dn_pre_fwd(qkv_pre, cw, name):
    t = qkv_pre.shape[0]
    tr = _tile(t, 512)
    scale = HEAD_DIM ** -0.5

    def body(x_ref, p_ref, w_ref, o_ref):
        i, j = pl.program_id(0), pl.program_id(1)
        row = _iota((tr, HEAD_DIM), 0)
        prev = jnp.where(i == 0, 0.0, p_ref[...])
        _, c = _conv_taps(x_ref[...], prev, w_ref[...], DN_CONV, row)
        s = c * _sigmoid(c)
        r = lax.rsqrt(jnp.sum(s * s, axis=1, keepdims=True) + EPS)
        mult = jnp.where(j < HEADS, r * scale, jnp.where(j < 2 * HEADS, r, 1.0))
        o_ref[...] = s * mult

    return pl.pallas_call(
        body, name=name,
        out_shape=jax.ShapeDtypeStruct((t, 3 * WIDTH), F32),
        grid=(t // tr, 3 * HEADS),
        in_specs=[pl.BlockSpec((tr, HEAD_DIM), lambda i, j: (i, j)),
                  pl.BlockSpec((8, HEAD_DIM), lambda i, j: (jnp.maximum(i * (tr // 8) - 1, 0), j)),
                  pl.BlockSpec((DN_CONV, HEAD_DIM), lambda i, j: (0, j))],
        out_specs=pl.BlockSpec((tr, HEAD_DIM), lambda i, j: (i, j)),
        compiler_params=_params(("parallel", "parallel")),
    )(qkv_pre, qkv_pre, cw)


def _dn_pre_bwd(dact, qkv_pre, cw, name):
    t = qkv_pre.shape[0]
    tr = _tile(t, 512)
    scale = HEAD_DIM ** -0.5

    def body(d_ref, x_ref, p_ref, w_ref, dc_ref, dw_ref):
        j, i = pl.program_id(0), pl.program_id(1)
        row = _iota((tr, HEAD_DIM), 0)
        prev = jnp.where(i == 0, 0.0, p_ref[...])
        taps, c = _conv_taps(x_ref[...], prev, w_ref[...], DN_CONV, row)
        sg = _sigmoid(c)
        s = c * sg
        r = lax.rsqrt(jnp.sum(s * s, axis=1, keepdims=True) + EPS)
        nh = s * r
        dn = d_ref[...] * jnp.where(j < HEADS, scale, 1.0)
        ds_norm = r * (dn - nh * jnp.sum(nh * dn, axis=1, keepdims=True))
        ds = jnp.where(j < 2 * HEADS, ds_norm, d_ref[...])
        dc = ds * (sg * (1.0 + c * (1.0 - sg)))
        dc_ref[...] = dc

        @pl.when(i == 0)
        def _():
            dw_ref[...] = jnp.zeros_like(dw_ref)

        for k in range(DN_CONV):
            dw_ref[k:k + 1, :] += jnp.sum(dc * taps[k], axis=0, keepdims=True)

    return pl.pallas_call(
        body, name=name,
        out_shape=(jax.ShapeDtypeStruct((t, 3 * WIDTH), F32), jax.ShapeDtypeStruct((DN_CONV, 3 * WIDTH), F32)),
        grid=(3 * HEADS, t // tr),
        in_specs=[pl.BlockSpec((tr, HEAD_DIM), lambda j, i: (i, j)),
                  pl.BlockSpec((tr, HEAD_DIM), lambda j, i: (i, j)),
                  pl.BlockSpec((8, HEAD_DIM), lambda j, i: (jnp.maximum(i * (tr // 8) - 1, 0), j)),
                  pl.BlockSpec((DN_CONV, HEAD_DIM), lambda j, i: (0, j))],
        out_specs=(pl.BlockSpec((tr, HEAD_DIM), lambda j, i: (i, j)),
                   pl.BlockSpec((DN_CONV, HEAD_DIM), lambda j, i: (0, j))),
        compiler_params=_params(("parallel", "arbitrary")),
    )(dact, qkv_pre, qkv_pre, cw)


def _tri(n, kind):
    r, c = _iota((n, n), 0), _iota((n, n), 1)
    m = {"lower": r >= c, "strict": r > c, "upper": r <= c}[kind]
    return m


def _dn_gates_fwd(hab, alog, dtb, name):
    t = hab.shape[0]
    cc = DN_CHUNK

    def body(h_ref, al_ref, dt_ref, o_ref):
        hv = h_ref[...]
        lane = _iota(hv.shape, 1)
        xa = hv + dt_ref[...]
        sp = jnp.maximum(xa, 0.0) + _log1pexp_neg_abs(xa)
        g = jnp.where(lane < HEADS, -jnp.exp(al_ref[...]) * sp, 0.0)
        tril = jnp.where(_tri(cc, "lower"), 1.0, 0.0).astype(BF16)
        gc = _dot_xl(tril, g, NN)
        o_ref[...] = jnp.where(lane < HEADS, gc, jnp.where(lane < 2 * HEADS, _sigmoid(hv), 0.0))

    return pl.pallas_call(
        body, name=name,
        out_shape=jax.ShapeDtypeStruct((t, 128), F32),
        grid=(t // cc,),
        in_specs=[pl.BlockSpec((cc, 128), lambda i: (i, 0)), pl.BlockSpec((1, 128), lambda i: (0, 0)),
                  pl.BlockSpec((1, 128), lambda i: (0, 0))],
        out_specs=pl.BlockSpec((cc, 128), lambda i: (i, 0)),
        compiler_params=_params(("parallel",)),
    )(hab, alog, dtb)


def _dn_gates_bwd(dgates, hab, alog, dtb, name):
    t = hab.shape[0]
    cc = DN_CHUNK

    def body(d_ref, h_ref, al_ref, dt_ref, o_ref, dal_ref, ddt_ref):
        i = pl.program_id(0)
        hv = h_ref[...]
        dv = d_ref[...]
        lane = _iota(hv.shape, 1)
        triu = jnp.where(_tri(cc, "upper"), 1.0, 0.0).astype(BF16)
        dg = _dot_xl(triu, jnp.where(lane < HEADS, dv, 0.0), NN)
        xa = hv + dt_ref[...]
        sp = jnp.maximum(xa, 0.0) + _log1pexp_neg_abs(xa)
        ea = jnp.exp(al_ref[...])
        da = jnp.where(lane < HEADS, dg * (-ea) * _sigmoid(xa), 0.0)
        be = _sigmoid(hv)
        db = dv * be * (1.0 - be)
        o_ref[...] = jnp.where(lane < HEADS, da, jnp.where(lane < 2 * HEADS, db, 0.0))

        @pl.when(i == 0)
        def _():
            dal_ref[...] = jnp.zeros_like(dal_ref)
            ddt_ref[...] = jnp.zeros_like(ddt_ref)

        dal_ref[...] += jnp.sum(jnp.where(lane < HEADS, dg * (-ea) * sp, 0.0), axis=0, keepdims=True)
        ddt_ref[...] += jnp.sum(da, axis=0, keepdims=True)

    return pl.pallas_call(
        body, name=name,
        out_shape=(jax.ShapeDtypeStruct((t, 128), F32), jax.ShapeDtypeStruct((1, 128), F32),
                   jax.ShapeDtypeStruct((1, 128), F32)),
        grid=(t // cc,),
        in_specs=[pl.BlockSpec((cc, 128), lambda i: (i, 0)), pl.BlockSpec((cc, 128), lambda i: (i, 0)),
                  pl.BlockSpec((1, 128), lambda i: (0, 0)), pl.BlockSpec((1, 128), lambda i: (0, 0))],
        out_specs=(pl.BlockSpec((cc, 128), lambda i: (i, 0)), pl.BlockSpec((1, 128), lambda i: (0, 0)),
                   pl.BlockSpec((1, 128), lambda i: (0, 0))),
        compiler_params=_params(("arbitrary",)),
    )(dgates, hab, alog, dtb)


def _dn_chunk_common(gates, h):
    cc = DN_CHUNK
    lane = _iota(gates.shape, 1)
    gh = jnp.where(lane == h, gates, 0.0)
    gc_col = jnp.sum(gh, axis=1, keepdims=True)
    gc_row = _dot_xl(jnp.ones((cc, 128), BF16), gh, NT)
    beta = jnp.sum(jnp.where(lane == h + HEADS, gates, 0.0), axis=1, keepdims=True)
    lower = _tri(cc, "lower")
    decay = jnp.where(lower, jnp.exp(jnp.where(lower, gc_col - gc_row, 0.0)), 0.0)
    gc_last = gc_col[cc - 1:cc, :]
    return gc_col, gc_last, beta, decay


def _dn_local_fwd(act, gates, name):
    t = act.shape[0]
    cc = DN_CHUNK
    nc = t // cc

    def body(q_ref, k_ref, v_ref, g_ref, u_ref, w_ref, kd_ref, qg_ref, ti_ref, p_ref):
        gates = g_ref[...]
        eye = jnp.where(_iota((cc, cc), 0) == _iota((cc, cc), 1), 1.0, 0.0)
        hs = range(HEADS)
        sl = [slice(h * HEAD_DIM, (h + 1) * HEAD_DIM) for h in hs]
        q, k, v = ([r[:, s] for s in sl] for r in (q_ref, k_ref, v_ref))
        gc_col, gc_last, beta, decay = zip(*[_dn_chunk_common(gates, h) for h in hs])
        gam = [jnp.exp(g) for g in gc_col]
        kb = [k[h] * beta[h] for h in hs]
        npow = [-jnp.where(_tri(cc, "strict"), _dotb(kb[h], k[h], NT) * decay[h], 0.0) for h in hs]
        tinv = [eye + n for n in npow]
        for _ in range(5):
            npow = [_dot3(n, n, NN) for n in npow]
            tinv = [t + _dot3(t, n, NN) for t, n in zip(tinv, npow)]
        uu = [_dot3(tinv[h], v[h] * beta[h], NN) for h in hs]
        ww = [_dot3(tinv[h], kb[h] * gam[h], NN) for h in hs]
        pp = [jnp.where(_tri(cc, "lower"), _dotb(q[h], k[h], NT) * decay[h], 0.0) for h in hs]
        for h in hs:
            u_ref[:, sl[h]] = uu[h]
            w_ref[:, sl[h]] = ww[h]
            kd_ref[:, sl[h]] = k[h] * jnp.exp(gc_last[h] - gc_col[h])
            qg_ref[:, sl[h]] = q[h] * gam[h]
            ti_ref[h] = tinv[h]
            p_ref[h] = pp[h]

    row = lambda off: pl.BlockSpec((cc, WIDTH), lambda n: (n, off))
    mat = pl.BlockSpec((HEADS, cc, cc), lambda n: (0, n, 0))
    tw = jax.ShapeDtypeStruct((t, WIDTH), F32)
    hm = jax.ShapeDtypeStruct((HEADS, t, cc), F32)
    return pl.pallas_call(
        body, name=name,
        out_shape=(tw, tw, tw, tw, hm, hm),
        grid=(nc,),
        in_specs=[row(0), row(1), row(2), pl.BlockSpec((cc, 128), lambda n: (n, 0))],
        out_specs=(row(0), row(0), row(0), row(0), mat, mat),
        compiler_params=_params(("parallel",)),
    )(act, act, act, gates)


def _dn_scan_fwd(u, w, kd, qg, p, gates, name):
    t = u.shape[0]
    cc = DN_CHUNK
    nc = t // cc

    def body(u_ref, w_ref, kd_ref, qg_ref, p_ref, g_ref, o_ref, sh_ref, s_ref):
        n = pl.program_id(0)

        @pl.when(n == 0)
        def _():
            s_ref[...] = jnp.zeros_like(s_ref)

        glast = jnp.exp(g_ref[cc - 1:cc, :])
        for h in range(HEADS):
            sl = slice(h * HEAD_DIM, (h + 1) * HEAD_DIM)
            s = s_ref[h]
            sb = s.astype(BF16)
            vn = u_ref[:, sl] - _dot(w_ref[:, sl].astype(BF16), sb, NN)
            vnb = vn.astype(BF16)
            o_ref[:, sl] = _dot(qg_ref[:, sl].astype(BF16), sb, NN) + _dot(p_ref[h].astype(BF16), vnb, NN)
            sh_ref[0, h] = s
            s_ref[h] = glast[:, h:h + 1] * s + _dot(kd_ref[:, sl].astype(BF16), vnb, TN)

    row = pl.BlockSpec((cc, WIDTH), lambda n: (n, 0))
    return pl.pallas_call(
        body, name=name,
        out_shape=(jax.ShapeDtypeStruct((t, WIDTH), F32),
                   jax.ShapeDtypeStruct((nc, HEADS, HEAD_DIM, HEAD_DIM), F32)),
        grid=(nc,),
        in_specs=[row, row, row, row, pl.BlockSpec((HEADS, cc, cc), lambda n: (0, n, 0)),
                  pl.BlockSpec((cc, 128), lambda n: (n, 0))],
        out_specs=(row, pl.BlockSpec((1, HEADS, HEAD_DIM, HEAD_DIM), lambda n: (n, 0, 0, 0))),
        scratch_shapes=[pltpu.VMEM((HEADS, HEAD_DIM, HEAD_DIM), F32)],
        compiler_params=_params(("arbitrary",)),
    )(u, w, kd, qg, p, gates)


def _dn_scan_bwd(do, w, kd, qg, p, gates, name):
    t = do.shape[0]
    cc = DN_CHUNK
    nc = t // cc

    def body(do_ref, w_ref, kd_ref, qg_ref, p_ref, g_ref, dvn_ref, dsh_ref, ds_ref):
        n = pl.program_id(0)

        @pl.when(n == 0)
        def _():
            ds_ref[...] = jnp.zeros_like(ds_ref)

        glast = jnp.exp(g_ref[cc - 1:cc, :])
        for h in range(HEADS):
            sl = slice(h * HEAD_DIM, (h + 1) * HEAD_DIM)
            ds = ds_ref[h]
            dob = do_ref[:, sl].astype(BF16)
            dvn = _dot(p_ref[h].astype(BF16), dob, TN) + _dot(kd_ref[:, sl].astype(BF16), ds.astype(BF16), NN)
            dvn_ref[:, sl] = dvn
            dsh_ref[0, h] = ds
            ds_ref[h] = (_dot(qg_ref[:, sl].astype(BF16), dob, TN) + glast[:, h:h + 1] * ds
                         - _dot(w_ref[:, sl].astype(BF16), dvn.astype(BF16), TN))

    row = pl.BlockSpec((cc, WIDTH), lambda n: (nc - 1 - n, 0))
    return pl.pallas_call(
        body, name=name,
        out_shape=(jax.ShapeDtypeStruct((t, WIDTH), F32),
                   jax.ShapeDtypeStruct((nc, HEADS, HEAD_DIM, HEAD_DIM), F32)),
        grid=(nc,),
        in_specs=[row, row, row, row, pl.BlockSpec((HEADS, cc, cc), lambda n: (0, nc - 1 - n, 0)),
                  pl.BlockSpec((cc, 128), lambda n: (nc - 1 - n, 0))],
        out_specs=(row, pl.BlockSpec((1, HEADS, HEAD_DIM, HEAD_DIM), lambda n: (nc - 1 - n, 0, 0, 0))),
        scratch_shapes=[pltpu.VMEM((HEADS, HEAD_DIM, HEAD_DIM), F32)],
        compiler_params=_params(("arbitrary",)),
    )(do, w, kd, qg, p, gates)


def _dn_local_bwd(act, gates, u, w, kd, qg, tinv, p, sh, dsh, dvn, do, name):
    t = act.shape[0]
    cc = DN_CHUNK
    nc = t // cc

    def body(q_ref, k_ref, v_ref, g_ref, u_ref, w_ref, kd_ref, qg_ref, ti_ref, p_ref, s_ref, ds_ref,
             dvn_ref, do_ref, dq_ref, dk_ref, dv_ref, dg_ref):
        gates_v = g_ref[...]
        lower, strict = _tri(cc, "lower"), _tri(cc, "strict")
        ones = jnp.ones((cc, 128), BF16)
        rowc = _iota((cc, 1), 0)
        lane = _iota((cc, 128), 1)
        hs = range(HEADS)
        sl = [slice(h * HEAD_DIM, (h + 1) * HEAD_DIM) for h in hs]
        q, k, v, uu, ww, kd, qg, dvn, do = ([r[:, s] for s in sl] for r in (
            q_ref, k_ref, v_ref, u_ref, w_ref, kd_ref, qg_ref, dvn_ref, do_ref))
        gc_col, gc_last, beta, decay = zip(*[_dn_chunk_common(gates_v, h) for h in hs])
        gam = [jnp.exp(g) for g in gc_col]
        kb = [k[h] * beta[h] for h in hs]
        s_in = [s_ref[0, h] for h in hs]
        ds_out = [ds_ref[0, h] for h in hs]
        tinv = [ti_ref[h] for h in hs]

        a = [jnp.where(strict, _dotb(kb[h], k[h], NT) * decay[h], 0.0) for h in hs]
        vn = [uu[h] - _dotb(ww[h], s_in[h], NN) for h in hs]
        dqg = [_dotb(do[h], s_in[h], NT) for h in hs]
        dw = [-_dotb(dvn[h], s_in[h], NT) for h in hs]
        dp = [jnp.where(lower, _dotb(do[h], vn[h], NT), 0.0) for h in hs]
        dkd = [_dotb(vn[h], ds_out[h], NT) for h in hs]
        dru = [_dot3(tinv[h], dvn[h], TN) for h in hs]
        drw = [_dot3(tinv[h], dw[h], TN) for h in hs]
        da = [-jnp.where(strict, _dotb(dru[h], uu[h], NT) + _dotb(drw[h], ww[h], NT), 0.0) for h in hs]
        dad = [da[h] * decay[h] for h in hs]
        dpd = [dp[h] * decay[h] for h in hs]
        dkb = [_dotb(dad[h], k[h], NN) + gam[h] * drw[h] for h in hs]
        dk = [_dotb(dad[h], kb[h], TN) + _dotb(dpd[h], q[h], TN) + beta[h] * dkb[h]
              + jnp.exp(gc_last[h] - gc_col[h]) * dkd[h] for h in hs]
        dq = [gam[h] * dqg[h] + _dotb(dpd[h], k[h], NN) for h in hs]
        gm = [da[h] * a[h] + dp[h] * p_ref[h] for h in hs]
        colsum = [_dot_xr(gm[h], ones, TN)[:, 0:1] for h in hs]

        dgates = jnp.zeros((cc, 128), F32)
        for h in hs:
            dk_ref[:, sl[h]] = dk[h]
            dq_ref[:, sl[h]] = dq[h]
            dv_ref[:, sl[h]] = beta[h] * dru[h]
            dbeta = (jnp.sum(dkb[h] * k[h], axis=1, keepdims=True)
                     + jnp.sum(dru[h] * v[h], axis=1, keepdims=True))
            rkd = jnp.sum(dkd[h] * kd[h], axis=1, keepdims=True)
            dgc = (jnp.sum(gm[h], axis=1, keepdims=True) - colsum[h]
                   + jnp.sum(dqg[h] * qg[h], axis=1, keepdims=True)
                   + jnp.sum(drw[h] * kb[h], axis=1, keepdims=True) * gam[h] - rkd)
            tail = jnp.sum(rkd, axis=0, keepdims=True) + jnp.exp(gc_last[h]) * jnp.sum(
                jnp.sum(s_in[h] * ds_out[h], axis=1, keepdims=True), axis=0, keepdims=True)
            dgc = dgc + jnp.where(rowc == cc - 1, tail, 0.0)
            dgates = dgates + jnp.where(lane == h, dgc, 0.0) + jnp.where(lane == h + HEADS, dbeta, 0.0)
        dg_ref[...] = dgates

    row = lambda off: pl.BlockSpec((cc, WIDTH), lambda n: (n, off))
    mat = pl.BlockSpec((HEADS, cc, cc), lambda n: (0, n, 0))
    st = pl.BlockSpec((1, HEADS, HEAD_DIM, HEAD_DIM), lambda n: (n, 0, 0, 0))
    gl = pl.BlockSpec((cc, 128), lambda n: (n, 0))
    tw = jax.ShapeDtypeStruct((t, WIDTH), F32)
    return pl.pallas_call(
        body, name=name,
        out_shape=(tw, tw, tw, jax.ShapeDtypeStruct((t, 128), F32)),
        grid=(nc,),
        in_specs=[row(0), row(1), row(2), gl, row(0), row(0), row(0), row(0), mat, mat, st, st, row(0), row(0)],
        out_specs=(row(0), row(0), row(0), gl),
        compiler_params=_params(("parallel",)),
    )(act, act, act, gates, u, w, kd, qg, tinv, p, sh, dsh, dvn, do)


def _dn_post_fwd(o, gate, w, name):
    t = o.shape[0]
    tr = _tile(t, 512)

    def body(o_ref, g_ref, w_ref, y_ref):
        ov, gv = o_ref[...], g_ref[...]
        r = lax.rsqrt(jnp.mean(ov * ov, axis=1, keepdims=True) + EPS)
        y_ref[...] = (ov * r * w_ref[...] * (gv * _sigmoid(gv))).astype(BF16)

    blk = pl.BlockSpec((tr, HEAD_DIM), lambda i, h: (i, h))
    return pl.pallas_call(
        body, name=name,
        out_shape=jax.ShapeDtypeStruct((t, WIDTH), BF16),
        grid=(t // tr, HEADS),
        in_specs=[blk, blk, pl.BlockSpec((1, HEAD_DIM), lambda i, h: (0, 0))],
        out_specs=blk,
        compiler_params=_params(("parallel", "parallel")),
    )(o, gate, w)


def _dn_post_bwd(dy, o, gate, w, name):
    t = o.shape[0]
    tr = _tile(t, 512)

    def body(dy_ref, o_ref, g_ref, w_ref, do_ref, dg_ref, dw_ref):
        i, h = pl.program_id(0), pl.program_id(1)
        ov, gv, dyv = o_ref[...], g_ref[...], dy_ref[...].astype(F32)
        r = lax.rsqrt(jnp.mean(ov * ov, axis=1, keepdims=True) + EPS)
        oh = ov * r
        sg = _sigmoid(gv)
        act = gv * sg
        dg_ref[...] = (dyv * oh * w_ref[...] * (sg * (1.0 + gv * (1.0 - sg)))).astype(BF16)
        dn = dyv * act
        doh = dn * w_ref[...]
        do_ref[...] = r * (doh - oh * jnp.mean(doh * oh, axis=1, keepdims=True))

        @pl.when(jnp.logical_and(i == 0, h == 0))
        def _():
            dw_ref[...] = jnp.zeros_like(dw_ref)

        dw_ref[...] += jnp.sum(dn * oh, axis=0, keepdims=True)

    blk = pl.BlockSpec((tr, HEAD_DIM), lambda i, h: (i, h))
    return pl.pallas_call(
        body, name=name,
        out_shape=(jax.ShapeDtypeStruct((t, WIDTH), F32), jax.ShapeDtypeStruct((t, WIDTH), BF16),
                   jax.ShapeDtypeStruct((1, HEAD_DIM), F32)),
        grid=(t // tr, HEADS),
        in_specs=[blk, blk, blk, pl.BlockSpec((1, HEAD_DIM), lambda i, h: (0, 0))],
        out_specs=(blk, blk, pl.BlockSpec((1, HEAD_DIM), lambda i, h: (0, 0))),
        compiler_params=_params(("arbitrary", "arbitrary")),
    )(dy, o, gate, w)


def _sb_scores(q, ks, qi, j, carry_b, uincl):
    bk = ATT_BLOCK
    scale = HEAD_DIM ** -0.5
    z = _dot(q, ks, NT) * scale
    qpos = qi * bk + _iota(z.shape, 0)
    kpos = j * bk + _iota(z.shape, 1)
    mask = kpos < qpos
    soft = _log1pexp_neg_abs(z)
    lk_full = -(jnp.maximum(z, 0.0) + soft)
    lk = jnp.where(mask, lk_full, 0.0)
    ls = jnp.minimum(z, 0.0) - soft
    incl = _dot_xr(lk, uincl, NN)
    a = jnp.where(mask, jnp.exp(ls + (carry_b + incl - lk)), 0.0)
    return a, mask, lk_full, ls, carry_b + incl[:, 0:1]


def _sb_more(qi, carry):
    it, cb = carry[0], carry[1]
    return jnp.logical_and(it <= qi, jnp.max(cb) > SB_LOG_ZERO)


def _sb_fwd(qkv, name):
    t = qkv.shape[0]
    bk = ATT_BLOCK

    def body(q_ref, k_ref, v_ref, o_ref):
        qi = pl.program_id(1)
        q = q_ref[...]
        uincl = jnp.where(_tri(bk, "lower"), 1.0, 0.0).astype(BF16)

        def step(carry):
            it, cb, acc = carry
            j = qi - it
            rows = pl.ds(pl.multiple_of(j * bk, bk), bk)
            a, _, _, _, cb = _sb_scores(q, k_ref[rows, :], qi, j, cb, uincl)
            acc = acc + _dot(a.astype(BF16), v_ref[rows, :], NN)
            return it + 1, cb, acc

        init = (jnp.int32(0), jnp.zeros((bk, 1), F32), jnp.zeros((bk, HEAD_DIM), F32))
        _, _, acc = lax.while_loop(functools.partial(_sb_more, qi), step, init)
        o_ref[...] = acc

    return pl.pallas_call(
        body, name=name,
        out_shape=jax.ShapeDtypeStruct((t, WIDTH), F32),
        grid=(HEADS, t // bk),
        in_specs=[pl.BlockSpec((bk, HEAD_DIM), lambda h, i: (i, h)),
                  pl.BlockSpec((t, HEAD_DIM), lambda h, i: (0, HEADS + h)),
                  pl.BlockSpec((t, HEAD_DIM), lambda h, i: (0, 2 * HEADS + h))],
        out_specs=pl.BlockSpec((bk, HEAD_DIM), lambda h, i: (i, h)),
        compiler_params=_params(("parallel", "arbitrary")),
    )(qkv, qkv, qkv)


def _sb_bwd(qkv, o, do, name):
    t = qkv.shape[0]
    bk = ATT_BLOCK
    scale = HEAD_DIM ** -0.5

    def body(q_ref, k_ref, v_ref, o_ref, do_ref, dq_ref, dk_ref, dv_ref):
        qi = pl.program_id(1)

        @pl.when(qi == 0)
        def _():
            dk_ref[...] = jnp.zeros_like(dk_ref)
            dv_ref[...] = jnp.zeros_like(dv_ref)

        q = q_ref[...]
        dov = do_ref[...]
        dob = dov.astype(BF16)
        do1, do2, do3 = _split3(dov)
        dsum = jnp.sum(dov * o_ref[...], axis=1, keepdims=True)
        uincl = jnp.where(_tri(bk, "lower"), 1.0, 0.0).astype(BF16)

        def step(carry):
            it, cb, ce, dq = carry
            j = qi - it
            rows = pl.ds(pl.multiple_of(j * bk, bk), bk)
            ks = k_ref[rows, :]
            a, mask, lk_full, ls, cb = _sb_scores(q, ks, qi, j, cb, uincl)
            ab = a.astype(BF16)
            vs = v_ref[rows, :]
            dla = ab.astype(F32) * (_dot(do1, vs, NT) + _dot(do2, vs, NT) + _dot(do3, vs, NT))
            suf = _dot_xr(dla, uincl, NN)
            e = dsum - (ce + suf)
            dz = jnp.where(mask, dla * jnp.exp(lk_full) - e * jnp.exp(ls), 0.0)
            dzb = (dz * scale).astype(BF16)
            dq = dq + _dot(dzb, ks, NN)
            dk_ref[rows, :] += _dot(dzb, q, TN)
            dv_ref[rows, :] += _dot(ab, dob, TN)
            return it + 1, cb, ce + suf[:, 0:1], dq

        zc = jnp.zeros((bk, 1), F32)
        init = (jnp.int32(0), zc, zc, jnp.zeros((bk, HEAD_DIM), F32))
        dq_ref[...] = lax.while_loop(functools.partial(_sb_more, qi), step, init)[3]

    tw = jax.ShapeDtypeStruct((t, WIDTH), F32)
    qb = pl.BlockSpec((bk, HEAD_DIM), lambda h, i: (i, h))
    full = lambda off: pl.BlockSpec((t, HEAD_DIM), lambda h, i: (0, off + h))
    return pl.pallas_call(
        body, name=name,
        out_shape=(tw, tw, tw),
        grid=(HEADS, t // bk),
        in_specs=[qb, full(HEADS), full(2 * HEADS), qb, qb],
        out_specs=(qb, full(0), full(0)),
        compiler_params=_params(("parallel", "arbitrary")),
    )(qkv, qkv, qkv, o, do)


def _merge_fwd(pd, ps, gl, name):
    t = pd.shape[0]
    tr, tc = _tile(t, 512), 512
    nj = D_MODEL // tc

    def body(pd_ref, ps_ref, gd_ref, gs_ref, o_ref):
        o_ref[...] = (_sigmoid(gd_ref[...]) * pd_ref[...] + _sigmoid(gs_ref[...]) * ps_ref[...]).astype(BF16)

    blk = lambda off: pl.BlockSpec((tr, tc), lambda i, j: (i, j + off))
    return pl.pallas_call(
        body, name=name,
        out_shape=jax.ShapeDtypeStruct((t, D_MODEL), BF16),
        grid=(t // tr, nj),
        in_specs=[blk(0), blk(0), blk(0), blk(nj)],
        out_specs=blk(0),
        compiler_params=_params(("parallel", "parallel")),
    )(pd, ps, gl, gl)


def _merge_bwd(dm, pd, ps, gl, name):
    t = pd.shape[0]
    tr, tc = _tile(t, 512), 512
    nj = D_MODEL // tc

    def body(dm_ref, pd_ref, ps_ref, gd_ref, gs_ref, dpd_ref, dps_ref, dgd_ref, dgs_ref):
        dmv = dm_ref[...]
        sd, ss = _sigmoid(gd_ref[...]), _sigmoid(gs_ref[...])
        dpd_ref[...] = (dmv * sd).astype(BF16)
        dps_ref[...] = (dmv * ss).astype(BF16)
        dgd_ref[...] = (dmv * pd_ref[...] * sd * (1.0 - sd)).astype(BF16)
        dgs_ref[...] = (dmv * ps_ref[...] * ss * (1.0 - ss)).astype(BF16)

    blk = lambda off: pl.BlockSpec((tr, tc), lambda i, j: (i, j + off))
    out = jax.ShapeDtypeStruct((t, D_MODEL), BF16)
    return pl.pallas_call(
        body, name=name,
        out_shape=(out, out, out, out),
        grid=(t // tr, nj),
        in_specs=[blk(0), blk(0), blk(0), blk(0), blk(nj)],
        out_specs=(blk(0), blk(0), blk(0), blk(0)),
        compiler_params=_params(("parallel", "parallel")),
    )(dm, pd, ps, gl, gl)


def _local_step(x, target, wts):
    n1 = _rmsnorm_fwd(x, wts["norm1_w"], "norm1_fwd")
    qkv_pre = _matmul(n1, wts["w_dnqkv"], "nn", F32, "in_dnqkv")
    hgate = _matmul(n1, wts["w_dngate"], "nn", F32, "in_dngate")
    sbqkv = _matmul(n1, wts["w_sbqkv"], "nn", BF16, "in_sbqkv")
    gl = _matmul(n1, wts["w_gl"], "nn", F32, "in_gl")
    hab = _matmul(n1, wts["w_ab"], "nn", F32, "in_ab")

    act = _dn_pre_fwd(qkv_pre, wts["dn_conv_w"], "dn_pre_fwd")
    gates = _dn_gates_fwd(hab, wts["alog"], wts["dtb"], "dn_gates_fwd")
    u, w, kd, qg, tinv, p = _dn_local_fwd(act, gates, "dn_local_fwd")
    o_dn, sh = _dn_scan_fwd(u, w, kd, qg, p, gates, "dn_scan_fwd")
    y_dn = _dn_post_fwd(o_dn, hgate, wts["dn_norm_w"], "dn_post_fwd")

    o_sb = _sb_fwd(sbqkv, "sb_fwd")

    pd = _matmul(y_dn, wts["w_proj_dn"], "nn", F32, "proj_dn")
    ps = _matmul(o_sb, wts["w_proj_sb"], "nn", F32, "proj_sb")
    mixed = _merge_fwd(pd, ps, gl, "merge_fwd")
    x1 = _matmul(mixed, wts["w_out"], "nn", F32, "out_proj", add=x)

    n2 = _rmsnorm_fwd(x1, wts["norm2_w"], "norm2_fwd")
    upre = _matmul(n2, wts["ffn_w_up"], "nn", F32, "ffn_up")
    fact = _ffn_act_fwd(upre, wts["ffn_conv_w"], "ffn_act_fwd")
    x2 = _matmul(fact, wts["ffn_w_down"], "nn", F32, "ffn_down", add=x1)

    dx2, g_normf, loss = _final_loss(x2, target, wts["norm_f_w"], "final_loss")

    dfact = _matmul(dx2, wts["ffn_w_down"], "nt", BF16, "ffn_down_dx")
    g_wdown = _matmul(fact, dx2, "tn", F32, "ffn_down_dw")
    dgc, duc, dwg, dwu = _ffn_act_bwd(dfact, upre, wts["ffn_conv_w"], "ffn_act_bwd")
    dconv = jnp.concatenate([dgc, duc], axis=1)
    g_fconv = jnp.concatenate([dwg, dwu], axis=1)
    dupre = _conv_bwd_data(dconv, wts["ffn_conv_w"], FFN_CONV, BF16, "ffn_conv_bwd")
    dn2 = _matmul(dupre, wts["ffn_w_up"], "nt", F32, "ffn_up_dx")
    g_wup = _matmul(n2, dupre, "tn", F32, "ffn_up_dw")
    dx1, g_norm2 = _rmsnorm_bwd(dn2, x1, wts["norm2_w"], dx2, "norm2_bwd")

    dmixed = _matmul(dx1, wts["w_out"], "nt", F32, "out_proj_dx")
    g_wout = _matmul(mixed, dx1, "tn", F32, "out_proj_dw")
    dpd, dps, dgd, dgs = _merge_bwd(dmixed, pd, ps, gl, "merge_bwd")
    dy_dn = _matmul(dpd, wts["w_proj_dn"], "nt", F32, "proj_dn_dx")
    g_wpd = _matmul(y_dn, dpd, "tn", F32, "proj_dn_dw")
    do_sb = _matmul(dps, wts["w_proj_sb"], "nt", F32, "proj_sb_dx")
    g_wps = _matmul(o_sb, dps, "tn", F32, "proj_sb_dw")

    dsq, dsk, dsv = _sb_bwd(sbqkv, o_sb, do_sb, "sb_bwd")

    do_dn, dhgate, g_dnnorm = _dn_post_bwd(dy_dn, o_dn, hgate, wts["dn_norm_w"], "dn_post_bwd")
    dvn, dsh = _dn_scan_bwd(do_dn, w, kd, qg, p, gates, "dn_scan_bwd")
    dq, dk, dv, dgates = _dn_local_bwd(act, gates, u, w, kd, qg, tinv, p, sh, dsh, dvn, do_dn, "dn_local_bwd")
    dhab, g_alog, g_dtb = _dn_gates_bwd(dgates, hab, wts["alog"], wts["dtb"], "dn_gates_bwd")
    dact = jnp.concatenate([dq, dk, dv], axis=1)
    dcv, g_dnconv = _dn_pre_bwd(dact, qkv_pre, wts["dn_conv_w"], "dn_pre_bwd")
    dqkv_pre = _conv_bwd_data(dcv, wts["dn_conv_w"], DN_CONV, BF16, "dn_conv_bwd")

    dh = jnp.concatenate([dqkv_pre, dhgate, dsq.astype(BF16), dsk.astype(BF16), dsv.astype(BF16), dgd, dgs], axis=1)
    w_main = jnp.concatenate([wts["w_dnqkv"], wts["w_dngate"], wts["w_sbqkv"], wts["w_gl"]], axis=1)
    dn1 = _matmul(dh, w_main, "nt", F32, "in_dx_main")
    dn1 = _matmul(dhab, wts["w_ab"], "nt", F32, "in_dx_ab", add=dn1)
    g_wmain = _matmul(n1, dh, "tn", F32, "in_dw_main")
    g_wab = _matmul(n1, dhab, "tn", F32, "in_dw_ab")
    grad_x, g_norm1 = _rmsnorm_bwd(dn1, x, wts["norm1_w"], dx1, "norm1_bwd")

    grads = dict(norm1_w=g_norm1, w_main=g_wmain, w_ab=g_wab, dn_conv_w=g_dnconv, alog=g_alog, dtb=g_dtb,
                 dn_norm_w=g_dnnorm, w_proj_dn=g_wpd, w_proj_sb=g_wps, w_out=g_wout, norm2_w=g_norm2,
                 ffn_w_up=g_wup, ffn_conv_w=g_fconv, ffn_w_down=g_wdown, norm_f_w=g_normf)
    return loss, grad_x, grads


HBM_SPEC = pl.BlockSpec(memory_space=pltpu.HBM)


def _mesh_pos():
    x, y, c = lax.axis_index("x"), lax.axis_index("y"), lax.axis_index("c")
    return x, y, c, 4 * x + 2 * y + c


def _peer(k):
    x, y, c, _ = _mesh_pos()
    px = 1 - x if k & 4 else x
    py = 1 - y if k & 2 else y
    pc = 1 - c if k & 1 else c
    return (px, py, pc), 4 * px + 2 * py + pc


def _all_gather(shards, name):
    n = len(shards)

    def body(*refs):
        ins, outs = refs[:n], refs[n:2 * n]
        send, recv, loc = refs[2 * n:]
        me = _mesh_pos()[3]
        local = [pltpu.make_async_copy(ins[a], outs[a].at[me], loc.at[a]) for a in range(n)]
        for cp in local:
            cp.start()
        sends, recvs = [], []
        for a in range(n):
            for k in range(1, N_DEV):
                peer, pidx = _peer(k)
                sends.append(pltpu.make_async_remote_copy(
                    src_ref=ins[a], dst_ref=outs[a].at[me], send_sem=send.at[a, k - 1], recv_sem=recv.at[a, k - 1],
                    device_id=peer, device_id_type=pl.DeviceIdType.MESH))
                recvs.append(pltpu.make_async_remote_copy(
                    src_ref=ins[a], dst_ref=outs[a].at[pidx], send_sem=send.at[a, k - 1], recv_sem=recv.at[a, k - 1],
                    device_id=peer, device_id_type=pl.DeviceIdType.MESH))
        for cp in sends:
            cp.start()
        for cp in recvs:
            cp.wait_recv()
        for cp in sends:
            cp.wait_send()
        for cp in local:
            cp.wait()

    return pl.pallas_call(
        body, name=name,
        out_shape=[jax.ShapeDtypeStruct((N_DEV,) + s.shape, s.dtype) for s in shards],
        in_specs=[HBM_SPEC] * n,
        out_specs=[HBM_SPEC] * n,
        scratch_shapes=[pltpu.SemaphoreType.DMA((n, N_DEV - 1)), pltpu.SemaphoreType.DMA((n, N_DEV - 1)),
                        pltpu.SemaphoreType.DMA((n,))],
        compiler_params=pltpu.CompilerParams(has_side_effects=True),
    )(*shards)


def _exchange(slabs, small, name):
    n = len(slabs)

    def body(*refs):
        ins, small_in = refs[:n], refs[n]
        outs, small_out = refs[n + 1:2 * n + 1], refs[2 * n + 1]
        send, recv, loc = refs[2 * n + 2:]
        me = _mesh_pos()[3]
        local = [pltpu.make_async_copy(ins[a].at[me], outs[a].at[me], loc.at[a]) for a in range(n)]
        local.append(pltpu.make_async_copy(small_in, small_out.at[me], loc.at[n]))
        for cp in local:
            cp.start()
        sends, recvs = [], []
        for a in range(n + 1):
            for k in range(1, N_DEV):
                peer, pidx = _peer(k)
                src = ins[a].at[pidx] if a < n else small_in
                out = outs[a] if a < n else small_out
                sends.append(pltpu.make_async_remote_copy(
                    src_ref=src, dst_ref=out.at[me], send_sem=send.at[a, k - 1], recv_sem=recv.at[a, k - 1],
                    device_id=peer, device_id_type=pl.DeviceIdType.MESH))
                recvs.append(pltpu.make_async_remote_copy(
                    src_ref=src, dst_ref=out.at[pidx], send_sem=send.at[a, k - 1], recv_sem=recv.at[a, k - 1],
                    device_id=peer, device_id_type=pl.DeviceIdType.MESH))
        for cp in sends:
            cp.start()
        for cp in recvs:
            cp.wait_recv()
        for cp in sends:
            cp.wait_send()
        for cp in local:
            cp.wait()

    return pl.pallas_call(
        body, name=name,
        out_shape=[jax.ShapeDtypeStruct(s.shape, s.dtype) for s in slabs]
        + [jax.ShapeDtypeStruct((N_DEV,) + small.shape, small.dtype)],
        in_specs=[HBM_SPEC] * (n + 1),
        out_specs=[HBM_SPEC] * (n + 1),
        scratch_shapes=[pltpu.SemaphoreType.DMA((n + 1, N_DEV - 1)), pltpu.SemaphoreType.DMA((n + 1, N_DEV - 1)),
                        pltpu.SemaphoreType.DMA((n + 1,))],
        compiler_params=pltpu.CompilerParams(has_side_effects=True),
    )(*slabs, small)


def _adamw(parts, w, m, v, name):
    rows, cols = w.shape
    tr = rows
    for cand in (128, 88):
        if rows > cand and rows % cand == 0:
            tr = cand
            break

    def body(p_ref, w_ref, m_ref, v_ref, g_ref, d_ref, mo_ref, vo_ref):
        g = p_ref[0]
        for s in range(1, N_DEV):
            g = g + p_ref[s]
        mn = ADAM_B1 * m_ref[...] + (1.0 - ADAM_B1) * g
        vn = ADAM_B2 * v_ref[...] + (1.0 - ADAM_B2) * (g * g)
        m_hat = mn / (1.0 - ADAM_B1 ** ADAM_STEP)
        v_hat = vn / (1.0 - ADAM_B2 ** ADAM_STEP)
        g_ref[...] = g
        d_ref[...] = -ADAM_LR * (m_hat / (jnp.sqrt(v_hat) + ADAM_EPS) + ADAM_WD * w_ref[...])
        mo_ref[...] = mn
        vo_ref[...] = vn

    blk = pl.BlockSpec((tr, cols), lambda i: (i, 0))
    out = jax.ShapeDtypeStruct((rows, cols), F32)
    return pl.pallas_call(
        body, name=name,
        out_shape=(out, out, out, out),
        grid=(rows // tr,),
        in_specs=[pl.BlockSpec((N_DEV, tr, cols), lambda i: (0, i, 0)), blk, blk, blk],
        out_specs=(blk, blk, blk, blk),
        compiler_params=_params(("parallel",)),
    )(parts, w, m, v)


CONV_PACK = 8 * 1024
WEIGHT_ORDER = ("norm1_w", "w_in", "dn_conv_w", "dn_A_log", "dn_dt_bias", "dn_norm_w", "w_proj_dn", "w_proj_sb",
                "w_out", "norm2_w", "ffn_w_up", "ffn_conv_w", "ffn_w_down", "norm_f_w")


def _cols_to_slabs(g):
    r, c8 = g.shape
    return g.reshape(r, N_DEV, c8 // N_DEV).transpose(1, 0, 2)


def _slabs_to_cols(s):
    d, r, c = s.shape
    return s.transpose(1, 0, 2).reshape(r, d * c)


def kernel(x, norm1_w, w_in, dn_conv_w, dn_A_log, dn_dt_bias, dn_norm_w, w_proj_dn, w_proj_sb, w_out, norm2_w, ffn_w_up, ffn_conv_w, ffn_w_down, norm_f_w, loss_target, m_norm1_w, m_w_in, m_dn_conv_w, m_dn_A_log, m_dn_dt_bias, m_dn_norm_w, m_w_proj_dn, m_w_proj_sb, m_w_out, m_norm2_w, m_ffn_w_up, m_ffn_conv_w, m_ffn_w_down, m_norm_f_w, v_norm1_w, v_w_in, v_dn_conv_w, v_dn_A_log, v_dn_dt_bias, v_dn_norm_w, v_w_proj_dn, v_w_proj_sb, v_w_out, v_norm2_w, v_ffn_w_up, v_ffn_conv_w, v_ffn_w_down, v_norm_f_w):
    me = _mesh_pos()[3]
    w_loc = dict(norm1_w=norm1_w, w_in=w_in[0], dn_conv_w=dn_conv_w[0], dn_A_log=dn_A_log, dn_dt_bias=dn_dt_bias,
                 dn_norm_w=dn_norm_w, w_proj_dn=w_proj_dn[0], w_proj_sb=w_proj_sb[0], w_out=w_out[0],
                 norm2_w=norm2_w, ffn_w_up=ffn_w_up[0], ffn_conv_w=ffn_conv_w[0], ffn_w_down=ffn_w_down[0],
                 norm_f_w=norm_f_w[None, :])
    m_loc = dict(norm1_w=m_norm1_w, w_in=m_w_in[0], dn_conv_w=m_dn_conv_w[0], dn_A_log=m_dn_A_log,
                 dn_dt_bias=m_dn_dt_bias, dn_norm_w=m_dn_norm_w, w_proj_dn=m_w_proj_dn[0], w_proj_sb=m_w_proj_sb[0],
                 w_out=m_w_out[0], norm2_w=m_norm2_w, ffn_w_up=m_ffn_w_up[0], ffn_conv_w=m_ffn_conv_w[0],
                 ffn_w_down=m_ffn_w_down[0], norm_f_w=m_norm_f_w[None, :])
    v_loc = dict(norm1_w=v_norm1_w, w_in=v_w_in[0], dn_conv_w=v_dn_conv_w[0], dn_A_log=v_dn_A_log,
                 dn_dt_bias=v_dn_dt_bias, dn_norm_w=v_dn_norm_w, w_proj_dn=v_w_proj_dn[0], w_proj_sb=v_w_proj_sb[0],
                 w_out=v_w_out[0], norm2_w=v_norm2_w, ffn_w_up=v_ffn_w_up[0], ffn_conv_w=v_ffn_conv_w[0],
                 ffn_w_down=v_ffn_w_down[0], norm_f_w=v_norm_f_w[None, :])

    big = ("w_in", "w_proj_dn", "w_proj_sb", "w_out", "ffn_w_up", "ffn_w_down")
    conv_flat = jnp.concatenate([w_loc["dn_conv_w"].reshape(-1), w_loc["ffn_conv_w"].reshape(-1)])
    n_dn, n_ffn = DN_CONV * 3 * WIDTH // N_DEV, FFN_CONV * 2 * D_FF // N_DEV
    conv_pack = jnp.pad(conv_flat, (0, CONV_PACK - n_dn - n_ffn)).reshape(8, 1024)
    gathered = _all_gather([w_loc[k].astype(BF16) for k in big] + [conv_pack], "gather_weights")
    g_in, g_pd, g_ps, g_out, g_up, g_down, g_conv = gathered
    w_in_full = _slabs_to_cols(g_in)
    g_conv = g_conv.reshape(N_DEV, CONV_PACK)
    dn_conv_full = _slabs_to_cols(g_conv[:, :n_dn].reshape(N_DEV, DN_CONV, 3 * WIDTH // N_DEV))
    ffn_conv_full = _slabs_to_cols(g_conv[:, n_dn:n_dn + n_ffn].reshape(N_DEV, FFN_CONV, 2 * D_FF // N_DEV))
    q_end = 3 * WIDTH
    ab_end = q_end + 2 * HEADS
    gate_end = ab_end + WIDTH
    sb_end = gate_end + 3 * WIDTH
    pad_lanes = lambda a: jnp.pad(a, ((0, 0), (0, 128 - a.shape[1])))
    wts = dict(
        norm1_w=norm1_w, w_dnqkv=w_in_full[:, :q_end], w_ab=pad_lanes(w_in_full[:, q_end:ab_end]),
        w_dngate=w_in_full[:, ab_end:gate_end], w_sbqkv=w_in_full[:, gate_end:sb_end], w_gl=w_in_full[:, sb_end:],
        dn_conv_w=dn_conv_full, alog=pad_lanes(dn_A_log), dtb=pad_lanes(dn_dt_bias), dn_norm_w=dn_norm_w,
        w_proj_dn=g_pd.reshape(WIDTH, D_MODEL), w_proj_sb=g_ps.reshape(WIDTH, D_MODEL),
        w_out=g_out.reshape(D_MODEL, D_MODEL), norm2_w=norm2_w, ffn_w_up=_slabs_to_cols(g_up),
        ffn_conv_w=ffn_conv_full, ffn_w_down=g_down.reshape(D_FF, D_MODEL), norm_f_w=norm_f_w[None, :])

    loss, grad_x, g = _local_step(x[0], loss_target[0], wts)

    g_win = jnp.concatenate([g["w_main"][:, :q_end], g["w_ab"][:, :2 * HEADS], g["w_main"][:, q_end:]], axis=1)
    slabs = [_cols_to_slabs(g_win), g["w_proj_dn"].reshape(N_DEV, WIDTH // N_DEV, D_MODEL),
             g["w_proj_sb"].reshape(N_DEV, WIDTH // N_DEV, D_MODEL),
             g["w_out"].reshape(N_DEV, D_MODEL // N_DEV, D_MODEL), _cols_to_slabs(g["ffn_w_up"]),
             g["ffn_w_down"].reshape(N_DEV, D_FF // N_DEV, D_MODEL)]
    row3 = jnp.concatenate([g["dn_norm_w"], g["alog"], g["dtb"], jnp.pad(loss, ((0, 0), (0, 127))),
                            jnp.zeros((1, D_MODEL - 512), F32)], axis=1)
    n_fc = FFN_CONV * 2 * D_FF
    fc_rows = -(-n_fc // D_MODEL)
    fconv_rows = jnp.pad(g["ffn_conv_w"].reshape(-1), (0, fc_rows * D_MODEL - n_fc)).reshape(fc_rows, D_MODEL)
    dn_rows = DN_CONV * 3 * WIDTH // D_MODEL
    pad8 = lambda a: jnp.pad(a, ((0, -a.shape[0] % 8), (0, 0)))
    pieces = [g["norm1_w"], g["norm2_w"], g["norm_f_w"], row3, g["dn_conv_w"].reshape(dn_rows, D_MODEL), fconv_rows]
    small = jnp.concatenate([pad8(a) for a in pieces], axis=0)
    assert small.shape[0] == SMALL_ROWS
    r_in, r_pd, r_ps, r_out, r_up, r_down, r_small = _exchange(slabs, small, "exchange_grads")

    parts = dict(w_in=r_in, w_proj_dn=r_pd, w_proj_sb=r_ps, w_out=r_out, ffn_w_up=r_up, ffn_w_down=r_down)
    parts["norm1_w"] = r_small[:, 0:1, :]
    parts["norm2_w"] = r_small[:, 8:9, :]
    parts["norm_f_w"] = r_small[:, 16:17, :]
    parts["dn_norm_w"] = r_small[:, 24:25, 0:HEAD_DIM]
    parts["dn_A_log"] = r_small[:, 24:25, 128:128 + HEADS]
    parts["dn_dt_bias"] = r_small[:, 24:25, 256:256 + HEADS]
    dnc = r_small[:, 32:32 + dn_rows, :].reshape(N_DEV, DN_CONV, 3 * WIDTH)
    parts["dn_conv_w"] = lax.dynamic_slice_in_dim(dnc, me * (3 * WIDTH // N_DEV), 3 * WIDTH // N_DEV, axis=2)
    fc0 = 32 + dn_rows + (-dn_rows % 8)
    fcc = r_small[:, fc0:fc0 + fc_rows, :].reshape(N_DEV, fc_rows * D_MODEL)[:, :n_fc]
    fcc = fcc.reshape(N_DEV, FFN_CONV, 2 * D_FF)
    parts["ffn_conv_w"] = lax.dynamic_slice_in_dim(fcc, me * (2 * D_FF // N_DEV), 2 * D_FF // N_DEV, axis=2)
    loss_total = jnp.sum(r_small[:, 24, 384])

    res = {k: _adamw(parts[k], w_loc[k], m_loc[k], v_loc[k], "adamw_" + k) for k in WEIGHT_ORDER}
    lead = ("w_in", "dn_conv_w", "w_proj_dn", "w_proj_sb", "w_out", "ffn_w_up", "ffn_conv_w", "ffn_w_down")

    def shaped(k, a):
        if k in lead:
            return a[None]
        if k == "norm_f_w":
            return a[0]
        return a

    outs = [loss_total, grad_x[None]]
    for idx in range(4):
        outs += [shaped(k, res[k][idx]) for k in WEIGHT_ORDER]
    return tuple(outs)
```

```python
import functools

import jax
import jax.numpy as jnp
from jax import lax
from jax.experimental import pallas as pl
from jax.experimental.pallas import tpu as pltpu

F32 = jnp.float32
BF16 = jnp.bfloat16

N_DEV = 8
D_MODEL = 1024
HEADS = 8
HEAD_DIM = 128
WIDTH = HEADS * HEAD_DIM
DN_CONV = 4
DN_CHUNK = 64
D_FF = 2816
FFN_CONV = 3
EPS = 1e-6
ATT_BLOCK = 256
SB_LOG_ZERO = -104.0
SMALL_ROWS = 64

ADAM_LR = 0.001
ADAM_B1 = 0.9
ADAM_B2 = 0.999
ADAM_EPS = 1e-08
ADAM_WD = 0.01
ADAM_STEP = 10

VMEM_LIMIT = 48 * 1024 * 1024


def _params(sem=None, **kw):
    return pltpu.CompilerParams(dimension_semantics=sem, vmem_limit_bytes=VMEM_LIMIT, **kw)


def _tile(n, cap):
    if n <= cap:
        return n
    best = None
    for t in range(128, cap + 1, 128):
        if n % t == 0:
            best = t
    assert best is not None, (n, cap)
    return best


def _dot(a, b, dims):
    return lax.dot_general(a, b, ((dims[0], dims[1]), ((), ())), preferred_element_type=F32)


NN = ((1,), (0,))
NT = ((1,), (1,))
TN = ((0,), (0,))


def _dotb(a, b, dims):
    return _dot(a.astype(BF16), b.astype(BF16), dims)


def _split3(x):
    h1 = x.astype(BF16)
    r1 = x - h1.astype(F32)
    h2 = r1.astype(BF16)
    r2 = r1 - h2.astype(F32)
    return h1, h2, r2.astype(BF16)


def _dot_xr(a, b_exact, dims):
    a1, a2, a3 = _split3(a)
    return _dot(a1, b_exact, dims) + _dot(a2, b_exact, dims) + _dot(a3, b_exact, dims)


def _dot_xl(a_exact, b, dims):
    b1, b2, b3 = _split3(b)
    return _dot(a_exact, b1, dims) + _dot(a_exact, b2, dims) + _dot(a_exact, b3, dims)


def _dot3(a, b, dims):
    a1 = a.astype(BF16)
    a2 = (a - a1.astype(F32)).astype(BF16)
    b1 = b.astype(BF16)
    b2 = (b - b1.astype(F32)).astype(BF16)
    return _dot(a1, b1, dims) + (_dot(a1, b2, dims) + _dot(a2, b1, dims))


def _sigmoid(x):
    return 1.0 / (1.0 + jnp.exp(-x))


def _log1pexp_neg_abs(x):
    return jnp.log(1.0 + jnp.exp(-jnp.abs(x)))


def _iota(shape, dim):
    return lax.broadcasted_iota(jnp.int32, shape, dim)


def _matmul(a, b, mode, out_dtype, name, add=None, comm=None):
    if mode == "nn":
        (m, k), (k2, n) = a.shape, b.shape
    elif mode == "nt":
        (m, k), (n, k2) = a.shape, b.shape
    else:
        (k, m), (k2, n) = a.shape, b.shape
    assert k == k2, (a.shape, b.shape, mode)
    tm, tn, tk = _tile(m, 1024), _tile(n, 1408), _tile(k, 1536)
    nk = k // tk
    dims = {"nn": NN, "nt": NT, "tn": TN}[mode]

    def body(*refs):
        if add is None:
            a_ref, b_ref, o_ref, acc_ref = refs
        else:
            a_ref, b_ref, add_ref, o_ref, acc_ref = refs
        kk = pl.program_id(2)

        @pl.when(kk == 0)
        def _():
            acc_ref[...] = jnp.zeros_like(acc_ref)

        acc_ref[...] += _dotb(a_ref[...], b_ref[...], dims)

        @pl.when(kk == nk - 1)
        def _():
            r = acc_ref[...]
            if add is not None:
                r = r + add_ref[...].astype(F32)
            o_ref[...] = r.astype(out_dtype)

    if mode == "nn":
        specs = [pl.BlockSpec((tm, tk), lambda i, j, l: (i, l)), pl.BlockSpec((tk, tn), lambda i, j, l: (l, j))]
    elif mode == "nt":
        specs = [pl.BlockSpec((tm, tk), lambda i, j, l: (i, l)), pl.BlockSpec((tn, tk), lambda i, j, l: (j, l))]
    else:
        specs = [pl.BlockSpec((tk, tm), lambda i, j, l: (l, i)), pl.BlockSpec((tk, tn), lambda i, j, l: (l, j))]
    args = [a, b]
    if add is not None:
        specs.append(pl.BlockSpec((tm, tn), lambda i, j, l: (i, j)))
        args.append(add)
    grid = (m // tm, n // tn, nk)

    def when():
        i, j, l = pl.program_id(0), pl.program_id(1), pl.program_id(2)
        first = jnp.logical_and(jnp.logical_and(i == 0, j == 0), l == 0)
        last = jnp.logical_and(jnp.logical_and(i == grid[0] - 1, j == grid[1] - 1), l == nk - 1)
        return first, last, last

    (out,), extra = _host_call(
        body, name, comm, when, [jax.ShapeDtypeStruct((m, n), out_dtype)], grid, specs,
        [pl.BlockSpec((tm, tn), lambda i, j, l: (i, j))], [pltpu.VMEM((tm, tn), F32)],
        ("parallel", "parallel", "arbitrary"), args)
    return out if comm is None else (out, extra)


def _rmsnorm_fwd(x, w, name):
    t, d = x.shape
    tr = _tile(t, 512)

    def body(x_ref, w_ref, o_ref):
        xv = x_ref[...]
        r = lax.rsqrt(jnp.mean(xv * xv, axis=1, keepdims=True) + EPS)
        o_ref[...] = (xv * r * w_ref[...]).astype(BF16)

    return pl.pallas_call(
        body, name=name,
        out_shape=jax.ShapeDtypeStruct((t, d), BF16),
        grid=(t // tr,),
        in_specs=[pl.BlockSpec((tr, d), lambda i: (i, 0)), pl.BlockSpec((1, d), lambda i: (0, 0))],
        out_specs=pl.BlockSpec((tr, d), lambda i: (i, 0)),
        compiler_params=_params(("parallel",)),
    )(x, w)


def _rmsnorm_bwd(dn, x, w, dres, name):
    t, d = x.shape
    tr = _tile(t, 512)

    def body(dn_ref, x_ref, w_ref, dres_ref, dx_ref, dw_ref):
        i = pl.program_id(0)
        xv = x_ref[...]
        g = dn_ref[...].astype(F32)
        r = lax.rsqrt(jnp.mean(xv * xv, axis=1, keepdims=True) + EPS)
        xh = xv * r
        dxh = g * w_ref[...]
        dx = r * (dxh - xh * jnp.mean(dxh * xh, axis=1, keepdims=True))
        dx_ref[...] = dres_ref[...] + dx

        @pl.when(i == 0)
        def _():
            dw_ref[...] = jnp.zeros_like(dw_ref)

        dw_ref[...] += jnp.sum(g * xh, axis=0, keepdims=True)

    return pl.pallas_call(
        body, name=name,
        out_shape=(jax.ShapeDtypeStruct((t, d), F32), jax.ShapeDtypeStruct((1, d), F32)),
        grid=(t // tr,),
        in_specs=[pl.BlockSpec((tr, d), lambda i: (i, 0)), pl.BlockSpec((tr, d), lambda i: (i, 0)),
                  pl.BlockSpec((1, d), lambda i: (0, 0)), pl.BlockSpec((tr, d), lambda i: (i, 0))],
        out_specs=(pl.BlockSpec((tr, d), lambda i: (i, 0)), pl.BlockSpec((1, d), lambda i: (0, 0))),
        compiler_params=_params(("arbitrary",)),
    )(dn, x, w, dres)


def _final_loss(x2, target, w, name):
    t, d = x2.shape
    tr = _tile(t, 512)

    def body(x_ref, t_ref, w_ref, dx_ref, dw_ref, loss_ref):
        i = pl.program_id(0)
        xv = x_ref[...]
        r = lax.rsqrt(jnp.mean(xv * xv, axis=1, keepdims=True) + EPS)
        xh = xv * r
        err = xh * w_ref[...] - t_ref[...]
        dy = err * (1.0 / d)
        dxh = dy * w_ref[...]
        dx_ref[...] = r * (dxh - xh * jnp.mean(dxh * xh, axis=1, keepdims=True))

        @pl.when(i == 0)
        def _():
            dw_ref[...] = jnp.zeros_like(dw_ref)
            loss_ref[...] = jnp.zeros_like(loss_ref)

        dw_ref[...] += jnp.sum(dy * xh, axis=0, keepdims=True)
        row = jnp.sum(err * err, axis=1, keepdims=True) * (0.5 / d)
        loss_ref[...] += jnp.sum(row, axis=0, keepdims=True)

    return pl.pallas_call(
        body, name=name,
        out_shape=(jax.ShapeDtypeStruct((t, d), F32), jax.ShapeDtypeStruct((1, d), F32),
                   jax.ShapeDtypeStruct((1, 1), F32)),
        grid=(t // tr,),
        in_specs=[pl.BlockSpec((tr, d), lambda i: (i, 0)), pl.BlockSpec((tr, d), lambda i: (i, 0)),
                  pl.BlockSpec((1, d), lambda i: (0, 0))],
        out_specs=(pl.BlockSpec((tr, d), lambda i: (i, 0)), pl.BlockSpec((1, d), lambda i: (0, 0)),
                   pl.BlockSpec((1, 1), lambda i: (0, 0))),
        compiler_params=_params(("arbitrary",)),
    )(x2, target, w)


def _shift_down(cur, prev, k, row):
    r = pltpu.roll(cur, k, 0)
    for m in range(k):
        r = jnp.where(row == m, prev[8 - k + m:8 - k + m + 1, :], r)
    return r


def _shift_up(cur, nxt, k, row, tr):
    r = pltpu.roll(cur, tr - k, 0)
    for m in range(k):
        r = jnp.where(row == tr - k + m, nxt[m:m + 1, :], r)
    return r


def _conv_taps(cur, prev, w, ntaps, row):
    taps = [cur if i == ntaps - 1 else _shift_down(cur, prev, ntaps - 1 - i, row) for i in range(ntaps)]
    y = w[0:1, :] * taps[0]
    for i in range(1, ntaps):
        y = y + w[i:i + 1, :] * taps[i]
    return taps, y


def _conv_bwd_data(dc, w, ntaps, out_dtype, name):
    t, ch = dc.shape
    tr, tc = _tile(t, 512), _tile(ch, 512)
    nrow8 = t // 8
    last = t // tr - 1

    def body(cur_ref, nxt_ref, w_ref, o_ref):
        i = pl.program_id(0)
        cur = cur_ref[...]
        nxt = jnp.where(i == last, 0.0, nxt_ref[...])
        row = _iota(cur.shape, 0)
        wv = w_ref[...]
        y = wv[ntaps - 1:ntaps, :] * cur
        for k in range(1, ntaps):
            y = y + wv[ntaps - 1 - k:ntaps - k, :] * _shift_up(cur, nxt, k, row, tr)
        o_ref[...] = y.astype(out_dtype)

    return pl.pallas_call(
        body, name=name,
        out_shape=jax.ShapeDtypeStruct((t, ch), out_dtype),
        grid=(t // tr, ch // tc),
        in_specs=[pl.BlockSpec((tr, tc), lambda i, j: (i, j)),
                  pl.BlockSpec((8, tc), lambda i, j: (jnp.minimum((i + 1) * (tr // 8), nrow8 - 1), j)),
                  pl.BlockSpec((ntaps, tc), lambda i, j: (0, j))],
        out_specs=pl.BlockSpec((tr, tc), lambda i, j: (i, j)),
        compiler_params=_params(("parallel", "parallel")),
    )(dc, dc, w)


def _ffn_act_fwd(upre, cw, name):
    t = upre.shape[0]
    tr, tc = _tile(t, 512), 256
    nj = D_FF // tc

    def body(g_ref, gp_ref, u_ref, up_ref, wg_ref, wu_ref, o_ref):
        i = pl.program_id(0)
        row = _iota((tr, tc), 0)
        gp = jnp.where(i == 0, 0.0, gp_ref[...])
        up = jnp.where(i == 0, 0.0, up_ref[...])
        _, gc = _conv_taps(g_ref[...], gp, wg_ref[...], FFN_CONV, row)
        _, uc = _conv_taps(u_ref[...], up, wu_ref[...], FFN_CONV, row)
        o_ref[...] = (gc * _sigmoid(gc) * uc).astype(BF16)

    prev = lambda off: (lambda i, j: (jnp.maximum(i * (tr // 8) - 1, 0), j + off))
    return pl.pallas_call(
        body, name=name,
        out_shape=jax.ShapeDtypeStruct((t, D_FF), BF16),
        grid=(t // tr, nj),
        in_specs=[pl.BlockSpec((tr, tc), lambda i, j: (i, j)), pl.BlockSpec((8, tc), prev(0)),
                  pl.BlockSpec((tr, tc), lambda i, j: (i, j + nj)), pl.BlockSpec((8, tc), prev(nj)),
                  pl.BlockSpec((FFN_CONV, tc), lambda i, j: (0, j)),
                  pl.BlockSpec((FFN_CONV, tc), lambda i, j: (0, j + nj))],
        out_specs=pl.BlockSpec((tr, tc), lambda i, j: (i, j)),
        compiler_params=_params(("parallel", "parallel")),
    )(upre, upre, upre, upre, cw, cw)


def _ffn_act_bwd(dact, upre, cw, name):
    t = upre.shape[0]
    tr, tc = _tile(t, 512), 256
    nj = D_FF // tc

    def body(da_ref, g_ref, gp_ref, u_ref, up_ref, wg_ref, wu_ref, dg_ref, du_ref, dwg_ref, dwu_ref):
        i = pl.program_id(1)
        row = _iota((tr, tc), 0)
        gp = jnp.where(i == 0, 0.0, gp_ref[...])
        up = jnp.where(i == 0, 0.0, up_ref[...])
        gt, gc = _conv_taps(g_ref[...], gp, wg_ref[...], FFN_CONV, row)
        ut, uc = _conv_taps(u_ref[...], up, wu_ref[...], FFN_CONV, row)
        da = da_ref[...].astype(F32)
        sg = _sigmoid(gc)
        dgc = da * uc * (sg * (1.0 + gc * (1.0 - sg)))
        duc = da * (gc * sg)
        dg_ref[...] = dgc
        du_ref[...] = duc

        @pl.when(i == 0)
        def _():
            dwg_ref[...] = jnp.zeros_like(dwg_ref)
            dwu_ref[...] = jnp.zeros_like(dwu_ref)

        for k in range(FFN_CONV):
            dwg_ref[k:k + 1, :] += jnp.sum(dgc * gt[k], axis=0, keepdims=True)
            dwu_ref[k:k + 1, :] += jnp.sum(duc * ut[k], axis=0, keepdims=True)

    prev = lambda off: (lambda j, i: (jnp.maximum(i * (tr // 8) - 1, 0), j + off))
    blk = lambda off: pl.BlockSpec((tr, tc), lambda j, i: (i, j + off))
    wblk = lambda off: pl.BlockSpec((FFN_CONV, tc), lambda j, i: (0, j + off))
    dgc, duc, dwg, dwu = pl.pallas_call(
        body, name=name,
        out_shape=(jax.ShapeDtypeStruct((t, D_FF), F32), jax.ShapeDtypeStruct((t, D_FF), F32),
                   jax.ShapeDtypeStruct((FFN_CONV, D_FF), F32), jax.ShapeDtypeStruct((FFN_CONV, D_FF), F32)),
        grid=(nj, t // tr),
        in_specs=[blk(0), blk(0), pl.BlockSpec((8, tc), prev(0)), blk(nj), pl.BlockSpec((8, tc), prev(nj)),
                  wblk(0), wblk(nj)],
        out_specs=(blk(0), blk(0), wblk(0), wblk(0)),
        compiler_params=_params(("parallel", "arbitrary")),
    )(dact, upre, upre, upre, upre, cw, cw)
    return dgc, duc, dwg, dwu


def _dn_pre_fwd(qkv_pre, cw, name):
    t = qkv_pre.shape[0]
    tr = _tile(t, 512)
    scale = HEAD_DIM ** -0.5

    def body(x_ref, p_ref, w_ref, o_ref):
        i, j = pl.program_id(0), pl.program_id(1)
        row = _iota((tr, HEAD_DIM), 0)
        prev = jnp.where(i == 0, 0.0, p_ref[...])
        _, c = _conv_taps(x_ref[...], prev, w_ref[...], DN_CONV, row)
        s = c * _sigmoid(c)
        r = lax.rsqrt(jnp.sum(s * s, axis=1, keepdims=True) + EPS)
        mult = jnp.where(j < HEADS, r * scale, jnp.where(j < 2 * HEADS, r, 1.0))
        o_ref[...] = s * mult

    return pl.pallas_call(
        body, name=name,
        out_shape=jax.ShapeDtypeStruct((t, 3 * WIDTH), F32),
        grid=(t // tr, 3 * HEADS),
        in_specs=[pl.BlockSpec((tr, HEAD_DIM), lambda i, j: (i, j)),
                  pl.BlockSpec((8, HEAD_DIM), lambda i, j: (jnp.maximum(i * (tr // 8) - 1, 0), j)),
                  pl.BlockSpec((DN_CONV, HEAD_DIM), lambda i, j: (0, j))],
        out_specs=pl.BlockSpec((tr, HEAD_DIM), lambda i, j: (i, j)),
        compiler_params=_params(("parallel", "parallel")),
    )(qkv_pre, qkv_pre, cw)


def _dn_pre_bwd(dact, qkv_pre, cw, name):
    t = qkv_pre.shape[0]
    tr = _tile(t, 512)
    scale = HEAD_DIM ** -0.5

    def body(d_ref, x_ref, p_ref, w_ref, dc_ref, dw_ref):
        j, i = pl.program_id(0), pl.program_id(1)
        row = _iota((tr, HEAD_DIM), 0)
        prev = jnp.where(i == 0, 0.0, p_ref[...])
        taps, c = _conv_taps(x_ref[...], prev, w_ref[...], DN_CONV, row)
        sg = _sigmoid(c)
        s = c * sg
        r = lax.rsqrt(jnp.sum(s * s, axis=1, keepdims=True) + EPS)
        nh = s * r
        dn = d_ref[...] * jnp.where(j < HEADS, scale, 1.0)
        ds_norm = r * (dn - nh * jnp.sum(nh * dn, axis=1, keepdims=True))
        ds = jnp.where(j < 2 * HEADS, ds_norm, d_ref[...])
        dc = ds * (sg * (1.0 + c * (1.0 - sg)))
        dc_ref[...] = dc

        @pl.when(i == 0)
        def _():
            dw_ref[...] = jnp.zeros_like(dw_ref)

        for k in range(DN_CONV):
            dw_ref[k:k + 1, :] += jnp.sum(dc * taps[k], axis=0, keepdims=True)

    return pl.pallas_call(
        body, name=name,
        out_shape=(jax.ShapeDtypeStruct((t, 3 * WIDTH), F32), jax.ShapeDtypeStruct((DN_CONV, 3 * WIDTH), F32)),
        grid=(3 * HEADS, t // tr),
        in_specs=[pl.BlockSpec((tr, HEAD_DIM), lambda j, i: (i, j)),
                  pl.BlockSpec((tr, HEAD_DIM), lambda j, i: (i, j)),
                  pl.BlockSpec((8, HEAD_DIM), lambda j, i: (jnp.maximum(i * (tr // 8) - 1, 0), j)),
                  pl.BlockSpec((DN_CONV, HEAD_DIM), lambda j, i: (0, j))],
        out_specs=(pl.BlockSpec((tr, HEAD_DIM), lambda j, i: (i, j)),
                   pl.BlockSpec((DN_CONV, HEAD_DIM), lambda j, i: (0, j))),
        compiler_params=_params(("parallel", "arbitrary")),
    )(dact, qkv_pre, qkv_pre, cw)


def _tri(n, kind):
    r, c = _iota((n, n), 0), _iota((n, n), 1)
    m = {"lower": r >= c, "strict": r > c, "upper": r <= c}[kind]
    return m


def _dn_gates_fwd(hab, alog, dtb, name):
    t = hab.shape[0]
    cc = DN_CHUNK

    def body(h_ref, al_ref, dt_ref, o_ref):
        hv = h_ref[...]
        lane = _iota(hv.shape, 1)
        xa = hv + dt_ref[...]
        sp = jnp.maximum(xa, 0.0) + _log1pexp_neg_abs(xa)
        g = jnp.where(lane < HEADS, -jnp.exp(al_ref[...]) * sp, 0.0)
        tril = jnp.where(_tri(cc, "lower"), 1.0, 0.0).astype(BF16)
        gc = _dot_xl(tril, g, NN)
        o_ref[...] = jnp.where(lane < HEADS, gc, jnp.where(lane < 2 * HEADS, _sigmoid(hv), 0.0))

    return pl.pallas_call(
        body, name=name,
        out_shape=jax.ShapeDtypeStruct((t, 128), F32),
        grid=(t // cc,),
        in_specs=[pl.BlockSpec((cc, 128), lambda i: (i, 0)), pl.BlockSpec((1, 128), lambda i: (0, 0)),
                  pl.BlockSpec((1, 128), lambda i: (0, 0))],
        out_specs=pl.BlockSpec((cc, 128), lambda i: (i, 0)),
        compiler_params=_params(("parallel",)),
    )(hab, alog, dtb)


def _dn_gates_bwd(dgates, hab, alog, dtb, name):
    t = hab.shape[0]
    cc = DN_CHUNK

    def body(d_ref, h_ref, al_ref, dt_ref, o_ref, dal_ref, ddt_ref):
        i = pl.program_id(0)
        hv = h_ref[...]
        dv = d_ref[...]
        lane = _iota(hv.shape, 1)
        triu = jnp.where(_tri(cc, "upper"), 1.0, 0.0).astype(BF16)
        dg = _dot_xl(triu, jnp.where(lane < HEADS, dv, 0.0), NN)
        xa = hv + dt_ref[...]
        sp = jnp.maximum(xa, 0.0) + _log1pexp_neg_abs(xa)
        ea = jnp.exp(al_ref[...])
        da = jnp.where(lane < HEADS, dg * (-ea) * _sigmoid(xa), 0.0)
        be = _sigmoid(hv)
        db = dv * be * (1.0 - be)
        o_ref[...] = jnp.where(lane < HEADS, da, jnp.where(lane < 2 * HEADS, db, 0.0))

        @pl.when(i == 0)
        def _():
            dal_ref[...] = jnp.zeros_like(dal_ref)
            ddt_ref[...] = jnp.zeros_like(ddt_ref)

        dal_ref[...] += jnp.sum(jnp.where(lane < HEADS, dg * (-ea) * sp, 0.0), axis=0, keepdims=True)
        ddt_ref[...] += jnp.sum(da, axis=0, keepdims=True)

    return pl.pallas_call(
        body, name=name,
        out_shape=(jax.ShapeDtypeStruct((t, 128), F32), jax.ShapeDtypeStruct((1, 128), F32),
                   jax.ShapeDtypeStruct((1, 128), F32)),
        grid=(t // cc,),
        in_specs=[pl.BlockSpec((cc, 128), lambda i: (i, 0)), pl.BlockSpec((cc, 128), lambda i: (i, 0)),
                  pl.BlockSpec((1, 128), lambda i: (0, 0)), pl.BlockSpec((1, 128), lambda i: (0, 0))],
        out_specs=(pl.BlockSpec((cc, 128), lambda i: (i, 0)), pl.BlockSpec((1, 128), lambda i: (0, 0)),
                   pl.BlockSpec((1, 128), lambda i: (0, 0))),
        compiler_params=_params(("arbitrary",)),
    )(dgates, hab, alog, dtb)


def _dn_chunk_common(gates, h):
    cc = DN_CHUNK
    lane = _iota(gates.shape, 1)
    gh = jnp.where(lane == h, gates, 0.0)
    gc_col = jnp.sum(gh, axis=1, keepdims=True)
    gc_row = _dot_xl(jnp.ones((cc, 128), BF16), gh, NT)
    beta = jnp.sum(jnp.where(lane == h + HEADS, gates, 0.0), axis=1, keepdims=True)
    lower = _tri(cc, "lower")
    decay = jnp.where(lower, jnp.exp(jnp.where(lower, gc_col - gc_row, 0.0)), 0.0)
    gc_last = gc_col[cc - 1:cc, :]
    return gc_col, gc_last, beta, decay


def _dn_local_fwd(act, gates, name):
    t = act.shape[0]
    cc = DN_CHUNK
    nc = t // cc

    def body(q_ref, k_ref, v_ref, g_ref, u_ref, w_ref, kd_ref, qg_ref, ti_ref, p_ref):
        gates = g_ref[...]
        eye = jnp.where(_iota((cc, cc), 0) == _iota((cc, cc), 1), 1.0, 0.0)
        hs = range(HEADS)
        sl = [slice(h * HEAD_DIM, (h + 1) * HEAD_DIM) for h in hs]
        q, k, v = ([r[:, s] for s in sl] for r in (q_ref, k_ref, v_ref))
        gc_col, gc_last, beta, decay = zip(*[_dn_chunk_common(gates, h) for h in hs])
        gam = [jnp.exp(g) for g in gc_col]
        kb = [k[h] * beta[h] for h in hs]
        npow = [-jnp.where(_tri(cc, "strict"), _dotb(kb[h], k[h], NT) * decay[h], 0.0) for h in hs]
        tinv = [eye + n for n in npow]
        for _ in range(5):
            npow = [_dot3(n, n, NN) for n in npow]
            tinv = [t + _dot3(t, n, NN) for t, n in zip(tinv, npow)]
        uu = [_dot3(tinv[h], v[h] * beta[h], NN) for h in hs]
        ww = [_dot3(tinv[h], kb[h] * gam[h], NN) for h in hs]
        pp = [jnp.where(_tri(cc, "lower"), _dotb(q[h], k[h], NT) * decay[h], 0.0) for h in hs]
        for h in hs:
            u_ref[:, sl[h]] = uu[h]
            w_ref[:, sl[h]] = ww[h]
            kd_ref[:, sl[h]] = k[h] * jnp.exp(gc_last[h] - gc_col[h])
            qg_ref[:, sl[h]] = q[h] * gam[h]
            ti_ref[h] = tinv[h]
            p_ref[h] = pp[h]

    row = lambda off: pl.BlockSpec((cc, WIDTH), lambda n: (n, off))
    mat = pl.BlockSpec((HEADS, cc, cc), lambda n: (0, n, 0))
    tw = jax.ShapeDtypeStruct((t, WIDTH), F32)
    hm = jax.ShapeDtypeStruct((HEADS, t, cc), F32)
    return pl.pallas_call(
        body, name=name,
        out_shape=(tw, tw, tw, tw, hm, hm),
        grid=(nc,),
        in_specs=[row(0), row(1), row(2), pl.BlockSpec((cc, 128), lambda n: (n, 0))],
        out_specs=(row(0), row(0), row(0), row(0), mat, mat),
        compiler_params=_params(("parallel",)),
    )(act, act, act, gates)


def _dn_scan_fwd(u, w, kd, qg, p, gates, name):
    t = u.shape[0]
    cc = DN_CHUNK
    nc = t // cc

    def body(u_ref, w_ref, kd_ref, qg_ref, p_ref, g_ref, o_ref, sh_ref, s_ref):
        n = pl.program_id(0)

        @pl.when(n == 0)
        def _():
            s_ref[...] = jnp.zeros_like(s_ref)

        glast = jnp.exp(g_ref[cc - 1:cc, :])
        for h in range(HEADS):
            sl = slice(h * HEAD_DIM, (h + 1) * HEAD_DIM)
            s = s_ref[h]
            sb = s.astype(BF16)
            vn = u_ref[:, sl] - _dot(w_ref[:, sl].astype(BF16), sb, NN)
            vnb = vn.astype(BF16)
            o_ref[:, sl] = _dot(qg_ref[:, sl].astype(BF16), sb, NN) + _dot(p_ref[h].astype(BF16), vnb, NN)
            sh_ref[0, h] = s
            s_ref[h] = glast[:, h:h + 1] * s + _dot(kd_ref[:, sl].astype(BF16), vnb, TN)

    row = pl.BlockSpec((cc, WIDTH), lambda n: (n, 0))
    return pl.pallas_call(
        body, name=name,
        out_shape=(jax.ShapeDtypeStruct((t, WIDTH), F32),
                   jax.ShapeDtypeStruct((nc, HEADS, HEAD_DIM, HEAD_DIM), F32)),
        grid=(nc,),
        in_specs=[row, row, row, row, pl.BlockSpec((HEADS, cc, cc), lambda n: (0, n, 0)),
                  pl.BlockSpec((cc, 128), lambda n: (n, 0))],
        out_specs=(row, pl.BlockSpec((1, HEADS, HEAD_DIM, HEAD_DIM), lambda n: (n, 0, 0, 0))),
        scratch_shapes=[pltpu.VMEM((HEADS, HEAD_DIM, HEAD_DIM), F32)],
        compiler_params=_params(("arbitrary",)),
    )(u, w, kd, qg, p, gates)


def _dn_scan_bwd(do, w, kd, qg, p, gates, name):
    t = do.shape[0]
    cc = DN_CHUNK
    nc = t // cc

    def body(do_ref, w_ref, kd_ref, qg_ref, p_ref, g_ref, dvn_ref, dsh_ref, ds_ref):
        n = pl.program_id(0)

        @pl.when(n == 0)
        def _():
            ds_ref[...] = jnp.zeros_like(ds_ref)

        glast = jnp.exp(g_ref[cc - 1:cc, :])
        for h in range(HEADS):
            sl = slice(h * HEAD_DIM, (h + 1) * HEAD_DIM)
            ds = ds_ref[h]
            dob = do_ref[:, sl].astype(BF16)
            dvn = _dot(p_ref[h].astype(BF16), dob, TN) + _dot(kd_ref[:, sl].astype(BF16), ds.astype(BF16), NN)
            dvn_ref[:, sl] = dvn
            dsh_ref[0, h] = ds
            ds_ref[h] = (_dot(qg_ref[:, sl].astype(BF16), dob, TN) + glast[:, h:h + 1] * ds
                         - _dot(w_ref[:, sl].astype(BF16), dvn.astype(BF16), TN))

    row = pl.BlockSpec((cc, WIDTH), lambda n: (nc - 1 - n, 0))
    return pl.pallas_call(
        body, name=name,
        out_shape=(jax.ShapeDtypeStruct((t, WIDTH), F32),
                   jax.ShapeDtypeStruct((nc, HEADS, HEAD_DIM, HEAD_DIM), F32)),
        grid=(nc,),
        in_specs=[row, row, row, row, pl.BlockSpec((HEADS, cc, cc), lambda n: (0, nc - 1 - n, 0)),
                  pl.BlockSpec((cc, 128), lambda n: (nc - 1 - n, 0))],
        out_specs=(row, pl.BlockSpec((1, HEADS, HEAD_DIM, HEAD_DIM), lambda n: (nc - 1 - n, 0, 0, 0))),
        scratch_shapes=[pltpu.VMEM((HEADS, HEAD_DIM, HEAD_DIM), F32)],
        compiler_params=_params(("arbitrary",)),
    )(do, w, kd, qg, p, gates)


def _dn_local_bwd(act, gates, u, w, kd, qg, tinv, p, sh, dsh, dvn, do, name):
    t = act.shape[0]
    cc = DN_CHUNK
    nc = t // cc

    def body(q_ref, k_ref, v_ref, g_ref, u_ref, w_ref, kd_ref, qg_ref, ti_ref, p_ref, s_ref, ds_ref,
             dvn_ref, do_ref, dq_ref, dk_ref, dv_ref, dg_ref):
        gates_v = g_ref[...]
        lower, strict = _tri(cc, "lower"), _tri(cc, "strict")
        ones = jnp.ones((cc, 128), BF16)
        rowc = _iota((cc, 1), 0)
        lane = _iota((cc, 128), 1)
        hs = range(HEADS)
        sl = [slice(h * HEAD_DIM, (h + 1) * HEAD_DIM) for h in hs]
        q, k, v, uu, ww, kd, qg, dvn, do = ([r[:, s] for s in sl] for r in (
            q_ref, k_ref, v_ref, u_ref, w_ref, kd_ref, qg_ref, dvn_ref, do_ref))
        gc_col, gc_last, beta, decay = zip(*[_dn_chunk_common(gates_v, h) for h in hs])
        gam = [jnp.exp(g) for g in gc_col]
        kb = [k[h] * beta[h] for h in hs]
        s_in = [s_ref[0, h] for h in hs]
        ds_out = [ds_ref[0, h] for h in hs]
        tinv = [ti_ref[h] for h in hs]

        a = [jnp.where(strict, _dotb(kb[h], k[h], NT) * decay[h], 0.0) for h in hs]
        vn = [uu[h] - _dotb(ww[h], s_in[h], NN) for h in hs]
        dqg = [_dotb(do[h], s_in[h], NT) for h in hs]
        dw = [-_dotb(dvn[h], s_in[h], NT) for h in hs]
        dp = [jnp.where(lower, _dotb(do[h], vn[h], NT), 0.0) for h in hs]
        dkd = [_dotb(vn[h], ds_out[h], NT) for h in hs]
        dru = [_dot3(tinv[h], dvn[h], TN) for h in hs]
        drw = [_dot3(tinv[h], dw[h], TN) for h in hs]
        da = [-jnp.where(strict, _dotb(dru[h], uu[h], NT) + _dotb(drw[h], ww[h], NT), 0.0) for h in hs]
        dad = [da[h] * decay[h] for h in hs]
        dpd = [dp[h] * decay[h] for h in hs]
        dkb = [_dotb(dad[h], k[h], NN) + gam[h] * drw[h] for h in hs]
        dk = [_dotb(dad[h], kb[h], TN) + _dotb(dpd[h], q[h], TN) + beta[h] * dkb[h]
              + jnp.exp(gc_last[h] - gc_col[h]) * dkd[h] for h in hs]
        dq = [gam[h] * dqg[h] + _dotb(dpd[h], k[h], NN) for h in hs]
        gm = [da[h] * a[h] + dp[h] * p_ref[h] for h in hs]
        colsum = [_dot_xr(gm[h], ones, TN)[:, 0:1] for h in hs]

        dgates = jnp.zeros((cc, 128), F32)
        for h in hs:
            dk_ref[:, sl[h]] = dk[h]
            dq_ref[:, sl[h]] = dq[h]
            dv_ref[:, sl[h]] = beta[h] * dru[h]
            dbeta = (jnp.sum(dkb[h] * k[h], axis=1, keepdims=True)
                     + jnp.sum(dru[h] * v[h], axis=1, keepdims=True))
            rkd = jnp.sum(dkd[h] * kd[h], axis=1, keepdims=True)
            dgc = (jnp.sum(gm[h], axis=1, keepdims=True) - colsum[h]
                   + jnp.sum(dqg[h] * qg[h], axis=1, keepdims=True)
                   + jnp.sum(drw[h] * kb[h], axis=1, keepdims=True) * gam[h] - rkd)
            tail = jnp.sum(rkd, axis=0, keepdims=True) + jnp.exp(gc_last[h]) * jnp.sum(
                jnp.sum(s_in[h] * ds_out[h], axis=1, keepdims=True), axis=0, keepdims=True)
            dgc = dgc + jnp.where(rowc == cc - 1, tail, 0.0)
            dgates = dgates + jnp.where(lane == h, dgc, 0.0) + jnp.where(lane == h + HEADS, dbeta, 0.0)
        dg_ref[...] = dgates

    row = lambda off: pl.BlockSpec((cc, WIDTH), lambda n: (n, off))
    mat = pl.BlockSpec((HEADS, cc, cc), lambda n: (0, n, 0))
    st = pl.BlockSpec((1, HEADS, HEAD_DIM, HEAD_DIM), lambda n: (n, 0, 0, 0))
    gl = pl.BlockSpec((cc, 128), lambda n: (n, 0))
    tw = jax.ShapeDtypeStruct((t, WIDTH), F32)
    return pl.pallas_call(
        body, name=name,
        out_shape=(tw, tw, tw, jax.ShapeDtypeStruct((t, 128), F32)),
        grid=(nc,),
        in_specs=[row(0), row(1), row(2), gl, row(0), row(0), row(0), row(0), mat, mat, st, st, row(0), row(0)],
        out_specs=(row(0), row(0), row(0), gl),
        compiler_params=_params(("parallel",)),
    )(act, act, act, gates, u, w, kd, qg, tinv, p, sh, dsh, dvn, do)


def _dn_post_fwd(o, gate, w, name):
    t = o.shape[0]
    tr = _tile(t, 512)

    def body(o_ref, g_ref, w_ref, y_ref):
        ov, gv = o_ref[...], g_ref[...]
        r = lax.rsqrt(jnp.mean(ov * ov, axis=1, keepdims=True) + EPS)
        y_ref[...] = (ov * r * w_ref[...] * (gv * _sigmoid(gv))).astype(BF16)

    blk = pl.BlockSpec((tr, HEAD_DIM), lambda i, h: (i, h))
    return pl.pallas_call(
        body, name=name,
        out_shape=jax.ShapeDtypeStruct((t, WIDTH), BF16),
        grid=(t // tr, HEADS),
        in_specs=[blk, blk, pl.BlockSpec((1, HEAD_DIM), lambda i, h: (0, 0))],
        out_specs=blk,
        compiler_params=_params(("parallel", "parallel")),
    )(o, gate, w)


def _dn_post_bwd(dy, o, gate, w, name):
    t = o.shape[0]
    tr = _tile(t, 512)

    def body(dy_ref, o_ref, g_ref, w_ref, do_ref, dg_ref, dw_ref):
        i, h = pl.program_id(0), pl.program_id(1)
        ov, gv, dyv = o_ref[...], g_ref[...], dy_ref[...].astype(F32)
        r = lax.rsqrt(jnp.mean(ov * ov, axis=1, keepdims=True) + EPS)
        oh = ov * r
        sg = _sigmoid(gv)
        act = gv * sg
        dg_ref[...] = (dyv * oh * w_ref[...] * (sg * (1.0 + gv * (1.0 - sg)))).astype(BF16)
        dn = dyv * act
        doh = dn * w_ref[...]
        do_ref[...] = r * (doh - oh * jnp.mean(doh * oh, axis=1, keepdims=True))

        @pl.when(jnp.logical_and(i == 0, h == 0))
        def _():
            dw_ref[...] = jnp.zeros_like(dw_ref)

        dw_ref[...] += jnp.sum(dn * oh, axis=0, keepdims=True)

    blk = pl.BlockSpec((tr, HEAD_DIM), lambda i, h: (i, h))
    return pl.pallas_call(
        body, name=name,
        out_shape=(jax.ShapeDtypeStruct((t, WIDTH), F32), jax.ShapeDtypeStruct((t, WIDTH), BF16),
                   jax.ShapeDtypeStruct((1, HEAD_DIM), F32)),
        grid=(t // tr, HEADS),
        in_specs=[blk, blk, blk, pl.BlockSpec((1, HEAD_DIM), lambda i, h: (0, 0))],
        out_specs=(blk, blk, pl.BlockSpec((1, HEAD_DIM), lambda i, h: (0, 0))),
        compiler_params=_params(("arbitrary", "arbitrary")),
    )(dy, o, gate, w)


def _sb_scores(q, ks, qi, j, carry_b, uincl):
    bk = ATT_BLOCK
    scale = HEAD_DIM ** -0.5
    z = _dot(q, ks, NT) * scale
    qpos = qi * bk + _iota(z.shape, 0)
    kpos = j * bk + _iota(z.shape, 1)
    mask = kpos < qpos
    soft = _log1pexp_neg_abs(z)
    lk_full = -(jnp.maximum(z, 0.0) + soft)
    lk = jnp.where(mask, lk_full, 0.0)
    ls = jnp.minimum(z, 0.0) - soft
    incl = _dot_xr(lk, uincl, NN)
    a = jnp.where(mask, jnp.exp(ls + (carry_b + incl - lk)), 0.0)
    return a, mask, lk_full, ls, carry_b + incl[:, 0:1]


def _sb_more(qi, carry):
    it, cb = carry[0], carry[1]
    return jnp.logical_and(it <= qi, jnp.max(cb) > SB_LOG_ZERO)


def _sb_steps(nq):
    def when():
        h, i = pl.program_id(0), pl.program_id(1)
        return (jnp.logical_and(h == 0, i == 0), jnp.logical_and(h == HEADS // 2, i == 0),
                jnp.logical_and(h == HEADS - 1, i == nq - 1))
    return when


def _sb_fwd(qkv, name, comm=None):
    t = qkv.shape[0]
    bk = ATT_BLOCK

    def body(q_ref, k_ref, v_ref, o_ref):
        qi = pl.program_id(1)
        q = q_ref[...]
        uincl = jnp.where(_tri(bk, "lower"), 1.0, 0.0).astype(BF16)

        def step(carry):
            it, cb, acc = carry
            j = qi - it
            rows = pl.ds(pl.multiple_of(j * bk, bk), bk)
            a, _, _, _, cb = _sb_scores(q, k_ref[rows, :], qi, j, cb, uincl)
            acc = acc + _dot(a.astype(BF16), v_ref[rows, :], NN)
            return it + 1, cb, acc

        init = (jnp.int32(0), jnp.zeros((bk, 1), F32), jnp.zeros((bk, HEAD_DIM), F32))
        _, _, acc = lax.while_loop(functools.partial(_sb_more, qi), step, init)
        o_ref[...] = acc

    (o,), extra = _host_call(
        body, name, comm, _sb_steps(t // bk), [jax.ShapeDtypeStruct((t, WIDTH), F32)], (HEADS, t // bk),
        [pl.BlockSpec((bk, HEAD_DIM), lambda h, i: (i, h)),
         pl.BlockSpec((t, HEAD_DIM), lambda h, i: (0, HEADS + h)),
         pl.BlockSpec((t, HEAD_DIM), lambda h, i: (0, 2 * HEADS + h))],
        [pl.BlockSpec((bk, HEAD_DIM), lambda h, i: (i, h))], [], ("parallel", "arbitrary"), (qkv, qkv, qkv))
    return o, extra


def _sb_bwd(qkv, o, do, name, comm=None):
    t = qkv.shape[0]
    bk = ATT_BLOCK
    scale = HEAD_DIM ** -0.5

    def body(q_ref, k_ref, v_ref, o_ref, do_ref, dq_ref, dk_ref, dv_ref):
        qi = pl.program_id(1)

        @pl.when(qi == 0)
        def _():
            dk_ref[...] = jnp.zeros_like(dk_ref)
            dv_ref[...] = jnp.zeros_like(dv_ref)

        q = q_ref[...]
        dov = do_ref[...]
        dob = dov.astype(BF16)
        do1, do2, do3 = _split3(dov)
        dsum = jnp.sum(dov * o_ref[...], axis=1, keepdims=True)
        uincl = jnp.where(_tri(bk, "lower"), 1.0, 0.0).astype(BF16)

        def step(carry):
            it, cb, ce, dq = carry
            j = qi - it
            rows = pl.ds(pl.multiple_of(j * bk, bk), bk)
            ks = k_ref[rows, :]
            a, mask, lk_full, ls, cb = _sb_scores(q, ks, qi, j, cb, uincl)
            ab = a.astype(BF16)
            vs = v_ref[rows, :]
            dla = ab.astype(F32) * (_dot(do1, vs, NT) + _dot(do2, vs, NT) + _dot(do3, vs, NT))
            suf = _dot_xr(dla, uincl, NN)
            e = dsum - (ce + suf)
            dz = jnp.where(mask, dla * jnp.exp(lk_full) - e * jnp.exp(ls), 0.0)
            dzb = (dz * scale).astype(BF16)
            dq = dq + _dot(dzb, ks, NN)
            dk_ref[rows, :] += _dot(dzb, q, TN)
            dv_ref[rows, :] += _dot(ab, dob, TN)
            return it + 1, cb, ce + suf[:, 0:1], dq

        zc = jnp.zeros((bk, 1), F32)
        init = (jnp.int32(0), zc, zc, jnp.zeros((bk, HEAD_DIM), F32))
        dq_ref[...] = lax.while_loop(functools.partial(_sb_more, qi), step, init)[3]

    tw = jax.ShapeDtypeStruct((t, WIDTH), F32)
    qb = pl.BlockSpec((bk, HEAD_DIM), lambda h, i: (i, h))
    full = lambda off: pl.BlockSpec((t, HEAD_DIM), lambda h, i: (0, off + h))
    return _host_call(
        body, name, comm, _sb_steps(t // bk), [tw, tw, tw], (HEADS, t // bk),
        [qb, full(HEADS), full(2 * HEADS), qb, qb], [qb, full(0), full(0)], [], ("parallel", "arbitrary"),
        (qkv, qkv, qkv, o, do))


def _merge_fwd(pd, ps, gl, name):
    t = pd.shape[0]
    tr, tc = _tile(t, 512), 512
    nj = D_MODEL // tc

    def body(pd_ref, ps_ref, gd_ref, gs_ref, o_ref):
        o_ref[...] = (_sigmoid(gd_ref[...]) * pd_ref[...] + _sigmoid(gs_ref[...]) * ps_ref[...]).astype(BF16)

    blk = lambda off: pl.BlockSpec((tr, tc), lambda i, j: (i, j + off))
    return pl.pallas_call(
        body, name=name,
        out_shape=jax.ShapeDtypeStruct((t, D_MODEL), BF16),
        grid=(t // tr, nj),
        in_specs=[blk(0), blk(0), blk(0), blk(nj)],
        out_specs=blk(0),
        compiler_params=_params(("parallel", "parallel")),
    )(pd, ps, gl, gl)


def _merge_bwd(dm, pd, ps, gl, name):
    t = pd.shape[0]
    tr, tc = _tile(t, 512), 512
    nj = D_MODEL // tc

    def body(dm_ref, pd_ref, ps_ref, gd_ref, gs_ref, dpd_ref, dps_ref, dgd_ref, dgs_ref):
        dmv = dm_ref[...]
        sd, ss = _sigmoid(gd_ref[...]), _sigmoid(gs_ref[...])
        dpd_ref[...] = (dmv * sd).astype(BF16)
        dps_ref[...] = (dmv * ss).astype(BF16)
        dgd_ref[...] = (dmv * pd_ref[...] * sd * (1.0 - sd)).astype(BF16)
        dgs_ref[...] = (dmv * ps_ref[...] * ss * (1.0 - ss)).astype(BF16)

    blk = lambda off: pl.BlockSpec((tr, tc), lambda i, j: (i, j + off))
    out = jax.ShapeDtypeStruct((t, D_MODEL), BF16)
    return pl.pallas_call(
        body, name=name,
        out_shape=(out, out, out, out),
        grid=(t // tr, nj),
        in_specs=[blk(0), blk(0), blk(0), blk(0), blk(nj)],
        out_specs=(blk(0), blk(0), blk(0), blk(0)),
        compiler_params=_params(("parallel", "parallel")),
    )(dm, pd, ps, gl, gl)


def _local_step(x, target, wts, plan=None):
    n1 = _rmsnorm_fwd(x, wts["norm1_w"], "norm1_fwd")
    qkv_pre = _matmul(n1, wts["w_dnqkv"], "nn", F32, "in_dnqkv")
    hgate = _matmul(n1, wts["w_dngate"], "nn", F32, "in_dngate")
    sbqkv = _matmul(n1, wts["w_sbqkv"], "nn", BF16, "in_sbqkv")
    gl = _matmul(n1, wts["w_gl"], "nn", F32, "in_gl")
    hab = _matmul(n1, wts["w_ab"], "nn", F32, "in_ab")

    act = _dn_pre_fwd(qkv_pre, wts["dn_conv_w"], "dn_pre_fwd")
    gates = _dn_gates_fwd(hab, wts["alog"], wts["dtb"], "dn_gates_fwd")
    u, w, kd, qg, tinv, p = _dn_local_fwd(act, gates, "dn_local_fwd")
    o_dn, sh = _dn_scan_fwd(u, w, kd, qg, p, gates, "dn_scan_fwd")
    y_dn = _dn_post_fwd(o_dn, hgate, wts["dn_norm_w"], "dn_post_fwd")

    o_sb, late = _sb_fwd(sbqkv, "sb_fwd", comm=plan.late_gather() if plan else None)
    if plan:
        wts = {**wts, **plan.late_weights(late)}

    pd = _matmul(y_dn, wts["w_proj_dn"], "nn", F32, "proj_dn")
    ps = _matmul(o_sb, wts["w_proj_sb"], "nn", F32, "proj_sb")
    mixed = _merge_fwd(pd, ps, gl, "merge_fwd")
    x1 = _matmul(mixed, wts["w_out"], "nn", F32, "out_proj", add=x)

    n2 = _rmsnorm_fwd(x1, wts["norm2_w"], "norm2_fwd")
    upre = _matmul(n2, wts["ffn_w_up"], "nn", F32, "ffn_up")
    fact = _ffn_act_fwd(upre, wts["ffn_conv_w"], "ffn_act_fwd")
    x2 = _matmul(fact, wts["ffn_w_down"], "nn", F32, "ffn_down", add=x1)

    dx2, g_normf, loss = _final_loss(x2, target, wts["norm_f_w"], "final_loss")

    dfact = _matmul(dx2, wts["ffn_w_down"], "nt", BF16, "ffn_down_dx")
    g_wdown = _matmul(fact, dx2, "tn", BF16, "ffn_down_dw")
    dgc, duc, dwg, dwu = _ffn_act_bwd(dfact, upre, wts["ffn_conv_w"], "ffn_act_bwd")
    dconv = jnp.concatenate([dgc, duc], axis=1)
    g_fconv = jnp.concatenate([dwg, dwu], axis=1)
    dupre = _conv_bwd_data(dconv, wts["ffn_conv_w"], FFN_CONV, BF16, "ffn_conv_bwd")
    dn2 = _matmul(dupre, wts["ffn_w_up"], "nt", F32, "ffn_up_dx")
    g_wup = _matmul(n2, dupre, "tn", BF16, "ffn_up_dw")
    dx1, g_norm2 = _rmsnorm_bwd(dn2, x1, wts["norm2_w"], dx2, "norm2_bwd")

    dmixed = _matmul(dx1, wts["w_out"], "nt", F32, "out_proj_dx")
    g_wout = _matmul(mixed, dx1, "tn", BF16, "out_proj_dw")
    dpd, dps, dgd, dgs = _merge_bwd(dmixed, pd, ps, gl, "merge_bwd")
    dy_dn = _matmul(dpd, wts["w_proj_dn"], "nt", F32, "proj_dn_dx")
    g_wpd = _matmul(y_dn, dpd, "tn", BF16, "proj_dn_dw")
    do_sb = _matmul(dps, wts["w_proj_sb"], "nt", F32, "proj_sb_dx")
    g_wps = _matmul(o_sb, dps, "tn", BF16, "proj_sb_dw")
    grads = dict(w_proj_dn=g_wpd, w_proj_sb=g_wps, w_out=g_wout, ffn_w_up=g_wup, ffn_w_down=g_wdown)

    (dsq, dsk, dsv), got_early = _sb_bwd(sbqkv, o_sb, do_sb, "sb_bwd",
                                         comm=plan.early_grads(grads) if plan else None)

    do_dn, dhgate, g_dnnorm = _dn_post_bwd(dy_dn, o_dn, hgate, wts["dn_norm_w"], "dn_post_bwd")
    dvn, dsh = _dn_scan_bwd(do_dn, w, kd, qg, p, gates, "dn_scan_bwd")
    dq, dk, dv, dgates = _dn_local_bwd(act, gates, u, w, kd, qg, tinv, p, sh, dsh, dvn, do_dn, "dn_local_bwd")
    dhab, g_alog, g_dtb = _dn_gates_bwd(dgates, hab, wts["alog"], wts["dtb"], "dn_gates_bwd")
    dact = jnp.concatenate([dq, dk, dv], axis=1)
    dcv, g_dnconv = _dn_pre_bwd(dact, qkv_pre, wts["dn_conv_w"], "dn_pre_bwd")
    dqkv_pre = _conv_bwd_data(dcv, wts["dn_conv_w"], DN_CONV, BF16, "dn_conv_bwd")

    dh = jnp.concatenate([dqkv_pre, dhgate, dsq.astype(BF16), dsk.astype(BF16), dsv.astype(BF16), dgd, dgs], axis=1)
    w_main = jnp.concatenate([wts["w_dnqkv"], wts["w_dngate"], wts["w_sbqkv"], wts["w_gl"]], axis=1)
    g_wmain = _matmul(n1, dh, "tn", BF16, "in_dw_main")
    g_wab = _matmul(n1, dhab, "tn", BF16, "in_dw_ab")
    grads.update(w_main=g_wmain, w_ab=g_wab, dn_conv_w=g_dnconv, alog=g_alog, dtb=g_dtb, dn_norm_w=g_dnnorm,
                 norm2_w=g_norm2, ffn_conv_w=g_fconv, norm_f_w=g_normf)
    got_late = []
    if plan:
        dn1, got_late = _matmul(dh, w_main, "nt", F32, "in_dx_main", comm=plan.late_grads(grads, loss))
    else:
        dn1 = _matmul(dh, w_main, "nt", F32, "in_dx_main")
    dn1 = _matmul(dhab, wts["w_ab"], "nt", F32, "in_dx_ab", add=dn1)
    grad_x, g_norm1 = _rmsnorm_bwd(dn1, x, wts["norm1_w"], dx1, "norm1_bwd")
    grads["norm1_w"] = g_norm1
    return loss, grad_x, grads, got_early, got_late


HBM_SPEC = pl.BlockSpec(memory_space=pltpu.HBM)


def _mesh_pos():
    x, y, c = lax.axis_index("x"), lax.axis_index("y"), lax.axis_index("c")
    return x, y, c, 4 * x + 2 * y + c


def _peer(k):
    x, y, c, _ = _mesh_pos()
    px = 1 - x if k & 4 else x
    py = 1 - y if k & 2 else y
    pc = 1 - c if k & 1 else c
    return (px, py, pc), 4 * px + 2 * py + pc


def _rcopy(src, dst, send, recv, a, s, peer):
    return pltpu.make_async_remote_copy(src_ref=src, dst_ref=dst, send_sem=send.at[a, s], recv_sem=recv.at[a, s],
                                        device_id=peer, device_id_type=pl.DeviceIdType.MESH)


class _Gather:
    ICI = (2, 4, 6)

    def __init__(self, shards):
        self.args = list(shards)
        self.n = len(shards)
        self.out_shape = [jax.ShapeDtypeStruct((N_DEV,) + s.shape, s.dtype) for s in shards]
        self.scratch = [pltpu.SemaphoreType.DMA((self.n, N_DEV - 1)), pltpu.SemaphoreType.DMA((self.n, N_DEV - 1)),
                        pltpu.SemaphoreType.DMA((self.n,))]

    def _first(self, ins, outs, send, recv, a):
        me = _mesh_pos()[3]
        out, got = [], []
        for s, k in enumerate((1,) + self.ICI):
            peer, pidx = _peer(k)
            out.append(_rcopy(ins[a], outs[a].at[me], send, recv, a, s, peer))
            got.append(_rcopy(ins[a], outs[a].at[pidx], send, recv, a, s, peer))
        return out, got

    def _forward(self, ins, outs, send, recv, a):
        sib = _peer(1)[0]
        out, got = [], []
        for s, k in enumerate(self.ICI):
            held = outs[a].at[_peer(k)[1]]
            out.append(_rcopy(held, held, send, recv, a, 4 + s, sib))
            other = outs[a].at[_peer(k | 1)[1]]
            got.append(_rcopy(other, other, send, recv, a, 4 + s, sib))
        return out, got

    def start(self, ins, outs, sems):
        send, recv, loc = sems
        me = _mesh_pos()[3]
        for a in range(self.n):
            pltpu.make_async_copy(ins[a], outs[a].at[me], loc.at[a]).start()
            for cp in self._first(ins, outs, send, recv, a)[0]:
                cp.start()

    def mid(self, ins, outs, sems):
        send, recv, _ = sems
        for a in range(self.n):
            arrivals = self._first(ins, outs, send, recv, a)[1]
            for s, cp in enumerate(self._forward(ins, outs, send, recv, a)[0]):
                arrivals[1 + s].wait_recv()
                cp.start()

    def finish(self, ins, outs, sems):
        send, recv, loc = sems
        me = _mesh_pos()[3]
        for a in range(self.n):
            first_out, first_got = self._first(ins, outs, send, recv, a)
            fwd_out, fwd_got = self._forward(ins, outs, send, recv, a)
            first_got[0].wait_recv()
            for cp in fwd_got:
                cp.wait_recv()
            for cp in first_out + fwd_out:
                cp.wait_send()
            pltpu.make_async_copy(ins[a], outs[a].at[me], loc.at[a]).wait()


class _Exchange:
    def __init__(self, slabs, gathered=()):
        self.args = list(slabs) + list(gathered)
        self.n_slab = len(slabs)
        self.n = len(self.args)
        self.out_shape = ([jax.ShapeDtypeStruct(s.shape, s.dtype) for s in slabs]
                          + [jax.ShapeDtypeStruct((N_DEV,) + s.shape, s.dtype) for s in gathered])
        self.scratch = [pltpu.SemaphoreType.DMA((self.n, N_DEV - 1)), pltpu.SemaphoreType.DMA((self.n, N_DEV - 1)),
                        pltpu.SemaphoreType.DMA((self.n,))]

    def _copies(self, ins, outs, send, recv, a):
        me = _mesh_pos()[3]
        out, got = [], []
        for k in range(1, N_DEV):
            peer, pidx = _peer(k)
            src = ins[a].at[pidx] if a < self.n_slab else ins[a]
            out.append(_rcopy(src, outs[a].at[me], send, recv, a, k - 1, peer))
            got.append(_rcopy(src, outs[a].at[pidx], send, recv, a, k - 1, peer))
        return out, got

    def _local(self, ins, outs, loc, a):
        me = _mesh_pos()[3]
        return pltpu.make_async_copy(ins[a].at[me] if a < self.n_slab else ins[a], outs[a].at[me], loc.at[a])

    def start(self, ins, outs, sems):
        send, recv, loc = sems
        for a in range(self.n):
            self._local(ins, outs, loc, a).start()
            for cp in self._copies(ins, outs, send, recv, a)[0]:
                cp.start()

    def mid(self, ins, outs, sems):
        pass

    def finish(self, ins, outs, sems):
        send, recv, loc = sems
        for a in range(self.n):
            out, got = self._copies(ins, outs, send, recv, a)
            for cp in got:
                cp.wait_recv()
            for cp in out:
                cp.wait_send()
            self._local(ins, outs, loc, a).wait()


def _comm_call(comm, name):
    n = comm.n

    def body(*refs):
        ins, outs, sems = refs[:n], refs[n:2 * n], refs[2 * n:]
        comm.start(ins, outs, sems)
        comm.mid(ins, outs, sems)
        comm.finish(ins, outs, sems)

    return pl.pallas_call(
        body, name=name, out_shape=comm.out_shape, in_specs=[HBM_SPEC] * n, out_specs=[HBM_SPEC] * n,
        scratch_shapes=comm.scratch,
    )(*comm.args)


def _hosted(body, comm, n_in, n_out, when):
    if comm is None:
        return body

    def wrapped(*refs):
        ins, c_ins = refs[:n_in], refs[n_in:n_in + comm.n]
        o0 = n_in + comm.n
        outs, c_outs = refs[o0:o0 + n_out], refs[o0 + n_out:o0 + n_out + comm.n]
        scratch, sems = refs[o0 + n_out + comm.n:len(refs) - 3], refs[len(refs) - 3:]
        first, middle, last = when()

        @pl.when(first)
        def _():
            comm.start(c_ins, c_outs, sems)

        body(*ins, *outs, *scratch)

        @pl.when(middle)
        def _():
            comm.mid(c_ins, c_outs, sems)

        @pl.when(last)
        def _():
            comm.finish(c_ins, c_outs, sems)

    return wrapped


def _host_call(body, name, comm, when, out_shape, grid, in_specs, out_specs, scratch_shapes, sem, args):
    n_in, n_out = len(in_specs), len(out_specs)
    if comm is None:
        res = pl.pallas_call(body, name=name, out_shape=out_shape, grid=grid, in_specs=in_specs, out_specs=out_specs,
                             scratch_shapes=scratch_shapes, compiler_params=_params(sem))(*args)
        return list(res), []
    res = pl.pallas_call(
        _hosted(body, comm, n_in, n_out, when), name=name,
        out_shape=list(out_shape) + comm.out_shape, grid=grid,
        in_specs=list(in_specs) + [HBM_SPEC] * comm.n, out_specs=list(out_specs) + [HBM_SPEC] * comm.n,
        scratch_shapes=list(scratch_shapes) + comm.scratch,
        compiler_params=_params(("arbitrary",) * len(grid)),
    )(*args, *comm.args)
    return list(res[:n_out]), list(res[n_out:])


def _adamw(parts, w, m, v, name):
    rows, cols = w.shape
    tr = rows
    for cand in (128, 176):
        if rows > cand and rows % cand == 0:
            tr = cand
            break

    def body(p_ref, w_ref, m_ref, v_ref, g_ref, d_ref, mo_ref, vo_ref):
        g = p_ref[0].astype(F32)
        for s in range(1, N_DEV):
            g = g + p_ref[s].astype(F32)
        mn = ADAM_B1 * m_ref[...] + (1.0 - ADAM_B1) * g
        vn = ADAM_B2 * v_ref[...] + (1.0 - ADAM_B2) * (g * g)
        m_hat = mn / (1.0 - ADAM_B1 ** ADAM_STEP)
        v_hat = vn / (1.0 - ADAM_B2 ** ADAM_STEP)
        g_ref[...] = g
        d_ref[...] = -ADAM_LR * (m_hat / (jnp.sqrt(v_hat) + ADAM_EPS) + ADAM_WD * w_ref[...])
        mo_ref[...] = mn
        vo_ref[...] = vn

    blk = pl.BlockSpec((tr, cols), lambda i: (i, 0))
    out = jax.ShapeDtypeStruct((rows, cols), F32)
    return pl.pallas_call(
        body, name=name,
        out_shape=(out, out, out, out),
        grid=(rows // tr,),
        in_specs=[pl.BlockSpec((N_DEV, tr, cols), lambda i: (0, i, 0)), blk, blk, blk],
        out_specs=(blk, blk, blk, blk),
        compiler_params=_params(("parallel",)),
    )(parts, w, m, v)


CONV_PACK = 8 * 1024
WEIGHT_ORDER = ("norm1_w", "w_in", "dn_conv_w", "dn_A_log", "dn_dt_bias", "dn_norm_w", "w_proj_dn", "w_proj_sb",
                "w_out", "norm2_w", "ffn_w_up", "ffn_conv_w", "ffn_w_down", "norm_f_w")


def _cols_to_slabs(g):
    r, c8 = g.shape
    return g.reshape(r, N_DEV, c8 // N_DEV).transpose(1, 0, 2)


def _slabs_to_cols(s):
    d, r, c = s.shape
    return s.transpose(1, 0, 2).reshape(r, d * c)


def kernel(x, norm1_w, w_in, dn_conv_w, dn_A_log, dn_dt_bias, dn_norm_w, w_proj_dn, w_proj_sb, w_out, norm2_w, ffn_w_up, ffn_conv_w, ffn_w_down, norm_f_w, loss_target, m_norm1_w, m_w_in, m_dn_conv_w, m_dn_A_log, m_dn_dt_bias, m_dn_norm_w, m_w_proj_dn, m_w_proj_sb, m_w_out, m_norm2_w, m_ffn_w_up, m_ffn_conv_w, m_ffn_w_down, m_norm_f_w, v_norm1_w, v_w_in, v_dn_conv_w, v_dn_A_log, v_dn_dt_bias, v_dn_norm_w, v_w_proj_dn, v_w_proj_sb, v_w_out, v_norm2_w, v_ffn_w_up, v_ffn_conv_w, v_ffn_w_down, v_norm_f_w):
    me = _mesh_pos()[3]
    w_loc = dict(norm1_w=norm1_w, w_in=w_in[0], dn_conv_w=dn_conv_w[0], dn_A_log=dn_A_log, dn_dt_bias=dn_dt_bias,
                 dn_norm_w=dn_norm_w, w_proj_dn=w_proj_dn[0], w_proj_sb=w_proj_sb[0], w_out=w_out[0],
                 norm2_w=norm2_w, ffn_w_up=ffn_w_up[0], ffn_conv_w=ffn_conv_w[0], ffn_w_down=ffn_w_down[0],
                 norm_f_w=norm_f_w[None, :])
    m_loc = dict(norm1_w=m_norm1_w, w_in=m_w_in[0], dn_conv_w=m_dn_conv_w[0], dn_A_log=m_dn_A_log,
                 dn_dt_bias=m_dn_dt_bias, dn_norm_w=m_dn_norm_w, w_proj_dn=m_w_proj_dn[0], w_proj_sb=m_w_proj_sb[0],
                 w_out=m_w_out[0], norm2_w=m_norm2_w, ffn_w_up=m_ffn_w_up[0], ffn_conv_w=m_ffn_conv_w[0],
                 ffn_w_down=m_ffn_w_down[0], norm_f_w=m_norm_f_w[None, :])
    v_loc = dict(norm1_w=v_norm1_w, w_in=v_w_in[0], dn_conv_w=v_dn_conv_w[0], dn_A_log=v_dn_A_log,
                 dn_dt_bias=v_dn_dt_bias, dn_norm_w=v_dn_norm_w, w_proj_dn=v_w_proj_dn[0], w_proj_sb=v_w_proj_sb[0],
                 w_out=v_w_out[0], norm2_w=v_norm2_w, ffn_w_up=v_ffn_w_up[0], ffn_conv_w=v_ffn_conv_w[0],
                 ffn_w_down=v_ffn_w_down[0], norm_f_w=v_norm_f_w[None, :])

    conv_flat = jnp.concatenate([w_loc["dn_conv_w"].reshape(-1), w_loc["ffn_conv_w"].reshape(-1)])
    n_dn, n_ffn = DN_CONV * 3 * WIDTH // N_DEV, FFN_CONV * 2 * D_FF // N_DEV
    conv_pack = jnp.pad(conv_flat, (0, CONV_PACK - n_dn - n_ffn)).reshape(8, 1024)
    g_in, g_conv = _comm_call(_Gather([w_loc["w_in"].astype(BF16), conv_pack]), "gather_first")
    w_in_full = _slabs_to_cols(g_in)
    g_conv = g_conv.reshape(N_DEV, CONV_PACK)
    dn_conv_full = _slabs_to_cols(g_conv[:, :n_dn].reshape(N_DEV, DN_CONV, 3 * WIDTH // N_DEV))
    ffn_conv_full = _slabs_to_cols(g_conv[:, n_dn:n_dn + n_ffn].reshape(N_DEV, FFN_CONV, 2 * D_FF // N_DEV))
    q_end = 3 * WIDTH
    ab_end = q_end + 2 * HEADS
    gate_end = ab_end + WIDTH
    sb_end = gate_end + 3 * WIDTH
    pad_lanes = lambda a: jnp.pad(a, ((0, 0), (0, 128 - a.shape[1])))
    wts = dict(
        norm1_w=norm1_w, w_dnqkv=w_in_full[:, :q_end], w_ab=pad_lanes(w_in_full[:, q_end:ab_end]),
        w_dngate=w_in_full[:, ab_end:gate_end], w_sbqkv=w_in_full[:, gate_end:sb_end], w_gl=w_in_full[:, sb_end:],
        dn_conv_w=dn_conv_full, alog=pad_lanes(dn_A_log), dtb=pad_lanes(dn_dt_bias), dn_norm_w=dn_norm_w,
        norm2_w=norm2_w, ffn_conv_w=ffn_conv_full, norm_f_w=norm_f_w[None, :])

    n_fc = FFN_CONV * 2 * D_FF
    fc_rows = -(-n_fc // D_MODEL)
    dn_rows = DN_CONV * 3 * WIDTH // D_MODEL
    late_names = ("w_proj_dn", "w_proj_sb", "w_out", "ffn_w_up", "ffn_w_down")

    class Plan:
        @staticmethod
        def late_gather():
            return _Gather([w_loc[k].astype(BF16) for k in late_names])

        @staticmethod
        def late_weights(got):
            g_pd, g_ps, g_out, g_up, g_down = got
            return dict(w_proj_dn=g_pd.reshape(WIDTH, D_MODEL), w_proj_sb=g_ps.reshape(WIDTH, D_MODEL),
                        w_out=g_out.reshape(D_MODEL, D_MODEL), ffn_w_up=_slabs_to_cols(g_up),
                        ffn_w_down=g_down.reshape(D_FF, D_MODEL))

        @staticmethod
        def early_grads(g):
            return _Exchange([g["w_proj_dn"].reshape(N_DEV, WIDTH // N_DEV, D_MODEL),
                              g["w_proj_sb"].reshape(N_DEV, WIDTH // N_DEV, D_MODEL),
                              g["w_out"].reshape(N_DEV, D_MODEL // N_DEV, D_MODEL), _cols_to_slabs(g["ffn_w_up"]),
                              g["ffn_w_down"].reshape(N_DEV, D_FF // N_DEV, D_MODEL)])

        @staticmethod
        def late_grads(g, loss):
            g_win = jnp.concatenate([g["w_main"][:, :q_end], g["w_ab"][:, :2 * HEADS], g["w_main"][:, q_end:]],
                                    axis=1)
            row3 = jnp.concatenate([g["dn_norm_w"], g["alog"], g["dtb"], jnp.pad(loss, ((0, 0), (0, 127))),
                                    jnp.zeros((1, D_MODEL - 512), F32)], axis=1)
            fconv_rows = jnp.pad(g["ffn_conv_w"].reshape(-1), (0, fc_rows * D_MODEL - n_fc)).reshape(fc_rows, D_MODEL)
            pad8 = lambda a: jnp.pad(a, ((0, -a.shape[0] % 8), (0, 0)))
            pieces = [g["norm2_w"], g["norm_f_w"], row3, g["dn_conv_w"].reshape(dn_rows, D_MODEL), fconv_rows]
            small = jnp.concatenate([pad8(a) for a in pieces], axis=0)
            assert small.shape[0] == SMALL_ROWS
            return _Exchange([_cols_to_slabs(g_win)], [small])

    loss, grad_x, g, got_early, got_late = _local_step(x[0], loss_target[0], wts, Plan)
    r_pd, r_ps, r_out, r_up, r_down = got_early
    r_in, r_small = got_late
    (r_norm1,) = _comm_call(_Exchange([], [jnp.pad(g["norm1_w"], ((0, 7), (0, 0)))]), "gather_norm1")

    parts = dict(w_in=r_in, w_proj_dn=r_pd, w_proj_sb=r_ps, w_out=r_out, ffn_w_up=r_up, ffn_w_down=r_down)
    parts["norm1_w"] = r_norm1[:, 0:1, :]
    parts["norm2_w"] = r_small[:, 0:1, :]
    parts["norm_f_w"] = r_small[:, 8:9, :]
    parts["dn_norm_w"] = r_small[:, 16:17, 0:HEAD_DIM]
    parts["dn_A_log"] = r_small[:, 16:17, 128:128 + HEADS]
    parts["dn_dt_bias"] = r_small[:, 16:17, 256:256 + HEADS]
    dnc = r_small[:, 24:24 + dn_rows, :].reshape(N_DEV, DN_CONV, 3 * WIDTH)
    parts["dn_conv_w"] = lax.dynamic_slice_in_dim(dnc, me * (3 * WIDTH // N_DEV), 3 * WIDTH // N_DEV, axis=2)
    fc0 = 24 + dn_rows + (-dn_rows % 8)
    fcc = r_small[:, fc0:fc0 + fc_rows, :].reshape(N_DEV, fc_rows * D_MODEL)[:, :n_fc]
    fcc = fcc.reshape(N_DEV, FFN_CONV, 2 * D_FF)
    parts["ffn_conv_w"] = lax.dynamic_slice_in_dim(fcc, me * (2 * D_FF // N_DEV), 2 * D_FF // N_DEV, axis=2)
    loss_total = jnp.sum(r_small[:, 16, 384])

    res = {k: _adamw(parts[k], w_loc[k], m_loc[k], v_loc[k], "adamw_" + k) for k in WEIGHT_ORDER}
    lead = ("w_in", "dn_conv_w", "w_proj_dn", "w_proj_sb", "w_out", "ffn_w_up", "ffn_conv_w", "ffn_w_down")

    def shaped(k, a):
        if k in lead:
            return a[None]
        if k == "norm_f_w":
            return a[0]
        return a

    outs = [loss_total, grad_x[None]]
    for idx in range(4):
        outs += [shaped(k, res[k][idx]) for k in WEIGHT_ORDER]
    return tuple(outs)
```

```python
import functools

import jax
import jax.numpy as jnp
from jax import lax
from jax.experimental import pallas as pl
from jax.experimental.pallas import tpu as pltpu

F32 = jnp.float32
BF16 = jnp.bfloat16

N_DEV = 8
D_MODEL = 1024
HEADS = 8
HEAD_DIM = 128
WIDTH = HEADS * HEAD_DIM
DN_CONV = 4
DN_CHUNK = 64
D_FF = 2816
FFN_CONV = 3
EPS = 1e-6
ATT_BLOCK = 256
SB_LOG_ZERO = -104.0
SMALL_ROWS = 64

ADAM_LR = 0.001
ADAM_B1 = 0.9
ADAM_B2 = 0.999
ADAM_EPS = 1e-08
ADAM_WD = 0.01
ADAM_STEP = 10

VMEM_LIMIT = 48 * 1024 * 1024


def _params(sem=None, **kw):
    return pltpu.CompilerParams(dimension_semantics=sem, vmem_limit_bytes=VMEM_LIMIT, **kw)


def _tile(n, cap):
    if n <= cap:
        return n
    best = None
    for t in range(128, cap + 1, 128):
        if n % t == 0:
            best = t
    assert best is not None, (n, cap)
    return best


def _dot(a, b, dims):
    return lax.dot_general(a, b, ((dims[0], dims[1]), ((), ())), preferred_element_type=F32)


NN = ((1,), (0,))
NT = ((1,), (1,))
TN = ((0,), (0,))


def _dotb(a, b, dims):
    return _dot(a.astype(BF16), b.astype(BF16), dims)


def _split3(x):
    h1 = x.astype(BF16)
    r1 = x - h1.astype(F32)
    h2 = r1.astype(BF16)
    r2 = r1 - h2.astype(F32)
    return h1, h2, r2.astype(BF16)


def _dot_xr(a, b_exact, dims):
    a1, a2, a3 = _split3(a)
    return _dot(a1, b_exact, dims) + _dot(a2, b_exact, dims) + _dot(a3, b_exact, dims)


def _split2(x):
    h1 = x.astype(BF16)
    return h1, (x - h1.astype(F32)).astype(BF16)


def _dot_xr2(a, b_exact, dims):
    a1, a2 = _split2(a)
    return _dot(a1, b_exact, dims) + _dot(a2, b_exact, dims)


def _dot_xl(a_exact, b, dims):
    b1, b2, b3 = _split3(b)
    return _dot(a_exact, b1, dims) + _dot(a_exact, b2, dims) + _dot(a_exact, b3, dims)


def _dot3(a, b, dims):
    a1 = a.astype(BF16)
    a2 = (a - a1.astype(F32)).astype(BF16)
    b1 = b.astype(BF16)
    b2 = (b - b1.astype(F32)).astype(BF16)
    return _dot(a1, b1, dims) + (_dot(a1, b2, dims) + _dot(a2, b1, dims))


def _sigmoid(x):
    return 1.0 / (1.0 + jnp.exp(-x))


def _log1pexp_neg_abs(x):
    return jnp.log(1.0 + jnp.exp(-jnp.abs(x)))


def _iota(shape, dim):
    return lax.broadcasted_iota(jnp.int32, shape, dim)


def _matmul(a, b, mode, out_dtype, name, add=None, comm=None):
    if mode == "nn":
        (m, k), (k2, n) = a.shape, b.shape
    elif mode == "nt":
        (m, k), (n, k2) = a.shape, b.shape
    else:
        (k, m), (k2, n) = a.shape, b.shape
    assert k == k2, (a.shape, b.shape, mode)
    tm, tn, tk = _tile(m, 1024), _tile(n, 1408), _tile(k, 1536)
    nk = k // tk
    dims = {"nn": NN, "nt": NT, "tn": TN}[mode]

    def body(*refs):
        if add is None:
            a_ref, b_ref, o_ref, acc_ref = refs
        else:
            a_ref, b_ref, add_ref, o_ref, acc_ref = refs
        kk = pl.program_id(2)

        @pl.when(kk == 0)
        def _():
            acc_ref[...] = jnp.zeros_like(acc_ref)

        acc_ref[...] += _dotb(a_ref[...], b_ref[...], dims)

        @pl.when(kk == nk - 1)
        def _():
            r = acc_ref[...]
            if add is not None:
                r = r + add_ref[...].astype(F32)
            o_ref[...] = r.astype(out_dtype)

    if mode == "nn":
        specs = [pl.BlockSpec((tm, tk), lambda i, j, l: (i, l)), pl.BlockSpec((tk, tn), lambda i, j, l: (l, j))]
    elif mode == "nt":
        specs = [pl.BlockSpec((tm, tk), lambda i, j, l: (i, l)), pl.BlockSpec((tn, tk), lambda i, j, l: (j, l))]
    else:
        specs = [pl.BlockSpec((tk, tm), lambda i, j, l: (l, i)), pl.BlockSpec((tk, tn), lambda i, j, l: (l, j))]
    args = [a, b]
    if add is not None:
        specs.append(pl.BlockSpec((tm, tn), lambda i, j, l: (i, j)))
        args.append(add)
    grid = (m // tm, n // tn, nk)

    def when():
        i, j, l = pl.program_id(0), pl.program_id(1), pl.program_id(2)
        first = jnp.logical_and(jnp.logical_and(i == 0, j == 0), l == 0)
        last = jnp.logical_and(jnp.logical_and(i == grid[0] - 1, j == grid[1] - 1), l == nk - 1)
        return first, last, last

    (out,), extra = _host_call(
        body, name, comm, when, [jax.ShapeDtypeStruct((m, n), out_dtype)], grid, specs,
        [pl.BlockSpec((tm, tn), lambda i, j, l: (i, j))], [pltpu.VMEM((tm, tn), F32)],
        ("parallel", "parallel", "arbitrary"), args)
    return out if comm is None else (out, extra)


def _rmsnorm_fwd(x, w, name):
    t, d = x.shape
    tr = _tile(t, 512)

    def body(x_ref, w_ref, o_ref):
        xv = x_ref[...]
        r = lax.rsqrt(jnp.mean(xv * xv, axis=1, keepdims=True) + EPS)
        o_ref[...] = (xv * r * w_ref[...]).astype(BF16)

    return pl.pallas_call(
        body, name=name,
        out_shape=jax.ShapeDtypeStruct((t, d), BF16),
        grid=(t // tr,),
        in_specs=[pl.BlockSpec((tr, d), lambda i: (i, 0)), pl.BlockSpec((1, d), lambda i: (0, 0))],
        out_specs=pl.BlockSpec((tr, d), lambda i: (i, 0)),
        compiler_params=_params(("parallel",)),
    )(x, w)


def _rmsnorm_bwd(dn, x, w, dres, name):
    t, d = x.shape
    tr = _tile(t, 512)

    def body(dn_ref, x_ref, w_ref, dres_ref, dx_ref, dw_ref):
        i = pl.program_id(0)
        xv = x_ref[...]
        g = dn_ref[...].astype(F32)
        r = lax.rsqrt(jnp.mean(xv * xv, axis=1, keepdims=True) + EPS)
        xh = xv * r
        dxh = g * w_ref[...]
        dx = r * (dxh - xh * jnp.mean(dxh * xh, axis=1, keepdims=True))
        dx_ref[...] = dres_ref[...] + dx

        @pl.when(i == 0)
        def _():
            dw_ref[...] = jnp.zeros_like(dw_ref)

        dw_ref[...] += jnp.sum(g * xh, axis=0, keepdims=True)

    return pl.pallas_call(
        body, name=name,
        out_shape=(jax.ShapeDtypeStruct((t, d), F32), jax.ShapeDtypeStruct((1, d), F32)),
        grid=(t // tr,),
        in_specs=[pl.BlockSpec((tr, d), lambda i: (i, 0)), pl.BlockSpec((tr, d), lambda i: (i, 0)),
                  pl.BlockSpec((1, d), lambda i: (0, 0)), pl.BlockSpec((tr, d), lambda i: (i, 0))],
        out_specs=(pl.BlockSpec((tr, d), lambda i: (i, 0)), pl.BlockSpec((1, d), lambda i: (0, 0))),
        compiler_params=_params(("arbitrary",)),
    )(dn, x, w, dres)


def _final_loss(x2, target, w, name):
    t, d = x2.shape
    tr = _tile(t, 512)

    def body(x_ref, t_ref, w_ref, dx_ref, dw_ref, loss_ref):
        i = pl.program_id(0)
        xv = x_ref[...]
        r = lax.rsqrt(jnp.mean(xv * xv, axis=1, keepdims=True) + EPS)
        xh = xv * r
        err = xh * w_ref[...] - t_ref[...]
        dy = err * (1.0 / d)
        dxh = dy * w_ref[...]
        dx_ref[...] = r * (dxh - xh * jnp.mean(dxh * xh, axis=1, keepdims=True))

        @pl.when(i == 0)
        def _():
            dw_ref[...] = jnp.zeros_like(dw_ref)
            loss_ref[...] = jnp.zeros_like(loss_ref)

        dw_ref[...] += jnp.sum(dy * xh, axis=0, keepdims=True)
        row = jnp.sum(err * err, axis=1, keepdims=True) * (0.5 / d)
        loss_ref[...] += jnp.sum(row, axis=0, keepdims=True)

    return pl.pallas_call(
        body, name=name,
        out_shape=(jax.ShapeDtypeStruct((t, d), F32), jax.ShapeDtypeStruct((1, d), F32),
                   jax.ShapeDtypeStruct((1, 1), F32)),
        grid=(t // tr,),
        in_specs=[pl.BlockSpec((tr, d), lambda i: (i, 0)), pl.BlockSpec((tr, d), lambda i: (i, 0)),
                  pl.BlockSpec((1, d), lambda i: (0, 0))],
        out_specs=(pl.BlockSpec((tr, d), lambda i: (i, 0)), pl.BlockSpec((1, d), lambda i: (0, 0)),
                   pl.BlockSpec((1, 1), lambda i: (0, 0))),
        compiler_params=_params(("arbitrary",)),
    )(x2, target, w)


def _shift_down(cur, prev, k, row):
    r = pltpu.roll(cur, k, 0)
    for m in range(k):
        r = jnp.where(row == m, prev[8 - k + m:8 - k + m + 1, :], r)
    return r


def _shift_up(cur, nxt, k, row, tr):
    r = pltpu.roll(cur, tr - k, 0)
    for m in range(k):
        r = jnp.where(row == tr - k + m, nxt[m:m + 1, :], r)
    return r


def _conv_taps(cur, prev, w, ntaps, row):
    taps = [cur if i == ntaps - 1 else _shift_down(cur, prev, ntaps - 1 - i, row) for i in range(ntaps)]
    y = w[0:1, :] * taps[0]
    for i in range(1, ntaps):
        y = y + w[i:i + 1, :] * taps[i]
    return taps, y


def _conv_bwd_data(parts, w, ntaps, out_dtype, name):
    t, chp = parts[0].shape
    npart = len(parts)
    tr, tc = _tile(t, 512), _tile(chp, 1408)
    nc = chp // tc
    nrow8 = t // 8
    last = t // tr - 1

    def body(*refs):
        cur_refs, nxt_refs = refs[:npart], refs[npart:2 * npart]
        w_ref, o_ref = refs[2 * npart], refs[2 * npart + 1]
        i, j = pl.program_id(0), pl.program_id(1)
        cur, nxt = cur_refs[0][...], nxt_refs[0][...]
        for p in range(1, npart):
            cur = jnp.where(j >= p * nc, cur_refs[p][...], cur)
            nxt = jnp.where(j >= p * nc, nxt_refs[p][...], nxt)
        nxt = jnp.where(i == last, 0.0, nxt)
        row = _iota(cur.shape, 0)
        wv = w_ref[...]
        y = wv[ntaps - 1:ntaps, :] * cur
        for k in range(1, ntaps):
            y = y + wv[ntaps - 1 - k:ntaps - k, :] * _shift_up(cur, nxt, k, row, tr)
        o_ref[...] = y.astype(out_dtype)

    col = lambda p: (lambda j: jnp.clip(j - p * nc, 0, nc - 1))
    cur_specs = [pl.BlockSpec((tr, tc), lambda i, j, c=col(p): (i, c(j))) for p in range(npart)]
    nxt_specs = [pl.BlockSpec((8, tc), lambda i, j, c=col(p): (jnp.minimum((i + 1) * (tr // 8), nrow8 - 1), c(j)))
                 for p in range(npart)]
    return pl.pallas_call(
        body, name=name,
        out_shape=jax.ShapeDtypeStruct((t, npart * chp), out_dtype),
        grid=(t // tr, npart * nc),
        in_specs=cur_specs + nxt_specs + [pl.BlockSpec((ntaps, tc), lambda i, j: (0, j))],
        out_specs=pl.BlockSpec((tr, tc), lambda i, j: (i, j)),
        compiler_params=_params(("parallel", "parallel")),
    )(*parts, *parts, w)


def _ffn_act_fwd(upre, cw, name):
    t = upre.shape[0]
    tr, tc = _tile(t, 512), _tile(D_FF, 1408)
    nj = D_FF // tc

    def body(g_ref, gp_ref, u_ref, up_ref, wg_ref, wu_ref, o_ref):
        i = pl.program_id(0)
        row = _iota((tr, tc), 0)
        gp = jnp.where(i == 0, 0.0, gp_ref[...])
        up = jnp.where(i == 0, 0.0, up_ref[...])
        _, gc = _conv_taps(g_ref[...], gp, wg_ref[...], FFN_CONV, row)
        _, uc = _conv_taps(u_ref[...], up, wu_ref[...], FFN_CONV, row)
        o_ref[...] = (gc * _sigmoid(gc) * uc).astype(BF16)

    prev = lambda off: (lambda i, j: (jnp.maximum(i * (tr // 8) - 1, 0), j + off))
    return pl.pallas_call(
        body, name=name,
        out_shape=jax.ShapeDtypeStruct((t, D_FF), BF16),
        grid=(t // tr, nj),
        in_specs=[pl.BlockSpec((tr, tc), lambda i, j: (i, j)), pl.BlockSpec((8, tc), prev(0)),
                  pl.BlockSpec((tr, tc), lambda i, j: (i, j + nj)), pl.BlockSpec((8, tc), prev(nj)),
                  pl.BlockSpec((FFN_CONV, tc), lambda i, j: (0, j)),
                  pl.BlockSpec((FFN_CONV, tc), lambda i, j: (0, j + nj))],
        out_specs=pl.BlockSpec((tr, tc), lambda i, j: (i, j)),
        compiler_params=_params(("parallel", "parallel")),
    )(upre, upre, upre, upre, cw, cw)


def _ffn_act_bwd(dact, upre, cw, name):
    t = upre.shape[0]
    tr, tc = _tile(t, 256), _tile(D_FF, 1408)
    nj = D_FF // tc

    def body(da_ref, g_ref, gp_ref, u_ref, up_ref, wg_ref, wu_ref, dg_ref, du_ref, dwg_ref, dwu_ref):
        i = pl.program_id(1)
        row = _iota((tr, tc), 0)
        gp = jnp.where(i == 0, 0.0, gp_ref[...])
        up = jnp.where(i == 0, 0.0, up_ref[...])
        gt, gc = _conv_taps(g_ref[...], gp, wg_ref[...], FFN_CONV, row)
        ut, uc = _conv_taps(u_ref[...], up, wu_ref[...], FFN_CONV, row)
        da = da_ref[...].astype(F32)
        sg = _sigmoid(gc)
        dgc = da * uc * (sg * (1.0 + gc * (1.0 - sg)))
        duc = da * (gc * sg)
        dg_ref[...] = dgc
        du_ref[...] = duc

        @pl.when(i == 0)
        def _():
            dwg_ref[...] = jnp.zeros_like(dwg_ref)
            dwu_ref[...] = jnp.zeros_like(dwu_ref)

        for k in range(FFN_CONV):
            dwg_ref[k:k + 1, :] += jnp.sum(dgc * gt[k], axis=0, keepdims=True)
            dwu_ref[k:k + 1, :] += jnp.sum(duc * ut[k], axis=0, keepdims=True)

    prev = lambda off: (lambda j, i: (jnp.maximum(i * (tr // 8) - 1, 0), j + off))
    blk = lambda off: pl.BlockSpec((tr, tc), lambda j, i: (i, j + off))
    wblk = lambda off: pl.BlockSpec((FFN_CONV, tc), lambda j, i: (0, j + off))
    dgc, duc, dwg, dwu = pl.pallas_call(
        body, name=name,
        out_shape=(jax.ShapeDtypeStruct((t, D_FF), F32), jax.ShapeDtypeStruct((t, D_FF), F32),
                   jax.ShapeDtypeStruct((FFN_CONV, D_FF), F32), jax.ShapeDtypeStruct((FFN_CONV, D_FF), F32)),
        grid=(nj, t // tr),
        in_specs=[blk(0), blk(0), pl.BlockSpec((8, tc), prev(0)), blk(nj), pl.BlockSpec((8, tc), prev(nj)),
                  wblk(0), wblk(nj)],
        out_specs=(blk(0), blk(0), wblk(0), wblk(0)),
        compiler_params=_params(("parallel", "arbitrary")),
    )(dact, upre, upre, upre, upre, cw, cw)
    return dgc, duc, dwg, dwu


def _dn_pre_fwd(qkv_pre, cw, name):
    t = qkv_pre.shape[0]
    tr = _tile(t, 512)
    scale = HEAD_DIM ** -0.5

    def body(x_ref, p_ref, w_ref, o_ref):
        i, j = pl.program_id(0), pl.program_id(1)
        row = _iota((tr, WIDTH), 0)
        prev = jnp.where(i == 0, 0.0, p_ref[...])
        _, c = _conv_taps(x_ref[...], prev, w_ref[...], DN_CONV, row)
        s = c * _sigmoid(c)
        for h in range(HEADS):
            sl = slice(h * HEAD_DIM, (h + 1) * HEAD_DIM)
            sh = s[:, sl]
            r = lax.rsqrt(jnp.sum(sh * sh, axis=1, keepdims=True) + EPS)
            o_ref[:, sl] = sh * jnp.where(j == 0, r * scale, jnp.where(j == 1, r, 1.0))

    return pl.pallas_call(
        body, name=name,
        out_shape=jax.ShapeDtypeStruct((t, 3 * WIDTH), F32),
        grid=(t // tr, 3),
        in_specs=[pl.BlockSpec((tr, WIDTH), lambda i, j: (i, j)),
                  pl.BlockSpec((8, WIDTH), lambda i, j: (jnp.maximum(i * (tr // 8) - 1, 0), j)),
                  pl.BlockSpec((DN_CONV, WIDTH), lambda i, j: (0, j))],
        out_specs=pl.BlockSpec((tr, WIDTH), lambda i, j: (i, j)),
        compiler_params=_params(("parallel", "parallel")),
    )(qkv_pre, qkv_pre, cw)


def _dn_pre_bwd(dq, dk, dv, qkv_pre, cw, name):
    t = qkv_pre.shape[0]
    tr = _tile(t, 256)
    scale = HEAD_DIM ** -0.5

    def body(dq_ref, dk_ref, dv_ref, x_ref, p_ref, w_ref, dc_ref, dw_ref):
        j, i = pl.program_id(0), pl.program_id(1)
        row = _iota((tr, WIDTH), 0)
        prev = jnp.where(i == 0, 0.0, p_ref[...])
        taps, c = _conv_taps(x_ref[...], prev, w_ref[...], DN_CONV, row)
        d = jnp.where(j == 0, dq_ref[...] * scale, jnp.where(j == 1, dk_ref[...], dv_ref[...]))
        sg = _sigmoid(c)
        s = c * sg
        dsilu = sg * (1.0 + c * (1.0 - sg))
        for h in range(HEADS):
            sl = slice(h * HEAD_DIM, (h + 1) * HEAD_DIM)
            sh, dh = s[:, sl], d[:, sl]
            r = lax.rsqrt(jnp.sum(sh * sh, axis=1, keepdims=True) + EPS)
            nh = sh * r
            ds_norm = r * (dh - nh * jnp.sum(nh * dh, axis=1, keepdims=True))
            dc_ref[:, sl] = jnp.where(j < 2, ds_norm, dh) * dsilu[:, sl]

        @pl.when(i == 0)
        def _():
            dw_ref[...] = jnp.zeros_like(dw_ref)

        dc = dc_ref[...]
        for k in range(DN_CONV):
            dw_ref[k:k + 1, :] += jnp.sum(dc * taps[k], axis=0, keepdims=True)

    dspec = lambda p: pl.BlockSpec((tr, WIDTH), lambda j, i: (jnp.where(j == p, i, 0), 0))
    return pl.pallas_call(
        body, name=name,
        out_shape=(jax.ShapeDtypeStruct((t, 3 * WIDTH), F32), jax.ShapeDtypeStruct((DN_CONV, 3 * WIDTH), F32)),
        grid=(3, t // tr),
        in_specs=[dspec(0), dspec(1), dspec(2),
                  pl.BlockSpec((tr, WIDTH), lambda j, i: (i, j)),
                  pl.BlockSpec((8, WIDTH), lambda j, i: (jnp.maximum(i * (tr // 8) - 1, 0), j)),
                  pl.BlockSpec((DN_CONV, WIDTH), lambda j, i: (0, j))],
        out_specs=(pl.BlockSpec((tr, WIDTH), lambda j, i: (i, j)),
                   pl.BlockSpec((DN_CONV, WIDTH), lambda j, i: (0, j))),
        compiler_params=_params(("parallel", "arbitrary")),
    )(dq, dk, dv, qkv_pre, qkv_pre, cw)


def _tri(n, kind):
    r, c = _iota((n, n), 0), _iota((n, n), 1)
    m = {"lower": r >= c, "strict": r > c, "upper": r <= c}[kind]
    return m


def _dn_gates_fwd(hab, alog, dtb, name):
    t = hab.shape[0]
    cc = DN_CHUNK

    def body(h_ref, al_ref, dt_ref, o_ref):
        hv = h_ref[...]
        lane = _iota(hv.shape, 1)
        xa = hv + dt_ref[...]
        sp = jnp.maximum(xa, 0.0) + _log1pexp_neg_abs(xa)
        g = jnp.where(lane < HEADS, -jnp.exp(al_ref[...]) * sp, 0.0)
        tril = jnp.where(_tri(cc, "lower"), 1.0, 0.0).astype(BF16)
        gc = _dot_xl(tril, g, NN)
        o_ref[...] = jnp.where(lane < HEADS, gc, jnp.where(lane < 2 * HEADS, _sigmoid(hv), 0.0))

    return pl.pallas_call(
        body, name=name,
        out_shape=jax.ShapeDtypeStruct((t, 128), F32),
        grid=(t // cc,),
        in_specs=[pl.BlockSpec((cc, 128), lambda i: (i, 0)), pl.BlockSpec((1, 128), lambda i: (0, 0)),
                  pl.BlockSpec((1, 128), lambda i: (0, 0))],
        out_specs=pl.BlockSpec((cc, 128), lambda i: (i, 0)),
        compiler_params=_params(("parallel",)),
    )(hab, alog, dtb)


def _dn_gates_bwd(dgates, hab, alog, dtb, name):
    t = hab.shape[0]
    cc = DN_CHUNK

    def body(d_ref, h_ref, al_ref, dt_ref, o_ref, dal_ref, ddt_ref):
        i = pl.program_id(0)
        hv = h_ref[...]
        dv = d_ref[...]
        lane = _iota(hv.shape, 1)
        triu = jnp.where(_tri(cc, "upper"), 1.0, 0.0).astype(BF16)
        dg = _dot_xl(triu, jnp.where(lane < HEADS, dv, 0.0), NN)
        xa = hv + dt_ref[...]
        sp = jnp.maximum(xa, 0.0) + _log1pexp_neg_abs(xa)
        ea = jnp.exp(al_ref[...])
        da = jnp.where(lane < HEADS, dg * (-ea) * _sigmoid(xa), 0.0)
        be = _sigmoid(hv)
        db = dv * be * (1.0 - be)
        o_ref[...] = jnp.where(lane < HEADS, da, jnp.where(lane < 2 * HEADS, db, 0.0))

        @pl.when(i == 0)
        def _():
            dal_ref[...] = jnp.zeros_like(dal_ref)
            ddt_ref[...] = jnp.zeros_like(ddt_ref)

        dal_ref[...] += jnp.sum(jnp.where(lane < HEADS, dg * (-ea) * sp, 0.0), axis=0, keepdims=True)
        ddt_ref[...] += jnp.sum(da, axis=0, keepdims=True)

    return pl.pallas_call(
        body, name=name,
        out_shape=(jax.ShapeDtypeStruct((t, 128), F32), jax.ShapeDtypeStruct((1, 128), F32),
                   jax.ShapeDtypeStruct((1, 128), F32)),
        grid=(t // cc,),
        in_specs=[pl.BlockSpec((cc, 128), lambda i: (i, 0)), pl.BlockSpec((cc, 128), lambda i: (i, 0)),
                  pl.BlockSpec((1, 128), lambda i: (0, 0)), pl.BlockSpec((1, 128), lambda i: (0, 0))],
        out_specs=(pl.BlockSpec((cc, 128), lambda i: (i, 0)), pl.BlockSpec((1, 128), lambda i: (0, 0)),
                   pl.BlockSpec((1, 128), lambda i: (0, 0))),
        compiler_params=_params(("arbitrary",)),
    )(dgates, hab, alog, dtb)


def _dn_chunk_common(gates, h):
    cc = DN_CHUNK
    lane = _iota(gates.shape, 1)
    gh = jnp.where(lane == h, gates, 0.0)
    gc_col = jnp.sum(gh, axis=1, keepdims=True)
    gc_row = _dot_xl(jnp.ones((cc, 128), BF16), gh, NT)
    beta = jnp.sum(jnp.where(lane == h + HEADS, gates, 0.0), axis=1, keepdims=True)
    lower = _tri(cc, "lower")
    decay = jnp.where(lower, jnp.exp(jnp.where(lower, gc_col - gc_row, 0.0)), 0.0)
    gc_last = gc_col[cc - 1:cc, :]
    return gc_col, gc_last, beta, decay


def _dn_local_fwd(act, gates, name):
    t = act.shape[0]
    cc = DN_CHUNK
    nc = t // cc

    def body(q_ref, k_ref, v_ref, g_ref, u_ref, w_ref, kd_ref, qg_ref, ti_ref, p_ref):
        gates = g_ref[...]
        eye = jnp.where(_iota((cc, cc), 0) == _iota((cc, cc), 1), 1.0, 0.0)
        hs = range(HEADS)
        sl = [slice(h * HEAD_DIM, (h + 1) * HEAD_DIM) for h in hs]
        q, k, v = ([r[:, s] for s in sl] for r in (q_ref, k_ref, v_ref))
        gc_col, gc_last, beta, decay = zip(*[_dn_chunk_common(gates, h) for h in hs])
        gam = [jnp.exp(g) for g in gc_col]
        kb = [k[h] * beta[h] for h in hs]
        npow = [-jnp.where(_tri(cc, "strict"), _dotb(kb[h], k[h], NT) * decay[h], 0.0) for h in hs]
        tinv = [eye + n for n in npow]
        for _ in range(5):
            npow = [_dot3(n, n, NN) for n in npow]
            tinv = [t + _dot3(t, n, NN) for t, n in zip(tinv, npow)]
        uu = [_dot3(tinv[h], v[h] * beta[h], NN) for h in hs]
        ww = [_dot3(tinv[h], kb[h] * gam[h], NN) for h in hs]
        pp = [jnp.where(_tri(cc, "lower"), _dotb(q[h], k[h], NT) * decay[h], 0.0) for h in hs]
        for h in hs:
            u_ref[:, sl[h]] = uu[h]
            w_ref[:, sl[h]] = ww[h]
            kd_ref[:, sl[h]] = k[h] * jnp.exp(gc_last[h] - gc_col[h])
            qg_ref[:, sl[h]] = q[h] * gam[h]
            ti_ref[h] = tinv[h]
            p_ref[h] = pp[h]

    row = lambda off: pl.BlockSpec((cc, WIDTH), lambda n: (n, off))
    mat = pl.BlockSpec((HEADS, cc, cc), lambda n: (0, n, 0))
    tw = jax.ShapeDtypeStruct((t, WIDTH), F32)
    hm = jax.ShapeDtypeStruct((HEADS, t, cc), F32)
    return pl.pallas_call(
        body, name=name,
        out_shape=(tw, tw, tw, tw, hm, hm),
        grid=(nc,),
        in_specs=[row(0), row(1), row(2), pl.BlockSpec((cc, 128), lambda n: (n, 0))],
        out_specs=(row(0), row(0), row(0), row(0), mat, mat),
        compiler_params=_params(("parallel",)),
    )(act, act, act, gates)


def _dn_scan_fwd(u, w, kd, qg, p, gates, name):
    t = u.shape[0]
    cc = DN_CHUNK
    nc = t // cc

    def body(u_ref, w_ref, kd_ref, qg_ref, p_ref, g_ref, o_ref, sh_ref, s_ref):
        n = pl.program_id(0)

        @pl.when(n == 0)
        def _():
            s_ref[...] = jnp.zeros_like(s_ref)

        glast = jnp.exp(g_ref[cc - 1:cc, :])
        hs = range(HEADS)
        sl = [slice(h * HEAD_DIM, (h + 1) * HEAD_DIM) for h in hs]
        s = [s_ref[h] for h in hs]
        sb = [a.astype(BF16) for a in s]
        vn = [u_ref[:, sl[h]] - _dot(w_ref[:, sl[h]].astype(BF16), sb[h], NN) for h in hs]
        vnb = [a.astype(BF16) for a in vn]
        o_state = [_dot(qg_ref[:, sl[h]].astype(BF16), sb[h], NN) for h in hs]
        o_local = [_dot(p_ref[h].astype(BF16), vnb[h], NN) for h in hs]
        s_add = [_dot(kd_ref[:, sl[h]].astype(BF16), vnb[h], TN) for h in hs]
        for h in hs:
            o_ref[:, sl[h]] = o_state[h] + o_local[h]
            sh_ref[0, h] = s[h]
            s_ref[h] = glast[:, h:h + 1] * s[h] + s_add[h]

    row = pl.BlockSpec((cc, WIDTH), lambda n: (n, 0))
    return pl.pallas_call(
        body, name=name,
        out_shape=(jax.ShapeDtypeStruct((t, WIDTH), F32),
                   jax.ShapeDtypeStruct((nc, HEADS, HEAD_DIM, HEAD_DIM), F32)),
        grid=(nc,),
        in_specs=[row, row, row, row, pl.BlockSpec((HEADS, cc, cc), lambda n: (0, n, 0)),
                  pl.BlockSpec((cc, 128), lambda n: (n, 0))],
        out_specs=(row, pl.BlockSpec((1, HEADS, HEAD_DIM, HEAD_DIM), lambda n: (n, 0, 0, 0))),
        scratch_shapes=[pltpu.VMEM((HEADS, HEAD_DIM, HEAD_DIM), F32)],
        compiler_params=_params(("arbitrary",)),
    )(u, w, kd, qg, p, gates)


def _dn_scan_bwd(do, w, kd, qg, p, gates, name):
    t = do.shape[0]
    cc = DN_CHUNK
    nc = t // cc

    def body(do_ref, w_ref, kd_ref, qg_ref, p_ref, g_ref, dvn_ref, dsh_ref, ds_ref):
        n = pl.program_id(0)

        @pl.when(n == 0)
        def _():
            ds_ref[...] = jnp.zeros_like(ds_ref)

        glast = jnp.exp(g_ref[cc - 1:cc, :])
        hs = range(HEADS)
        sl = [slice(h * HEAD_DIM, (h + 1) * HEAD_DIM) for h in hs]
        ds = [ds_ref[h] for h in hs]
        dob = [do_ref[:, sl[h]].astype(BF16) for h in hs]
        dvn = [_dot(p_ref[h].astype(BF16), dob[h], TN) + _dot(kd_ref[:, sl[h]].astype(BF16), ds[h].astype(BF16), NN)
               for h in hs]
        ds_q = [_dot(qg_ref[:, sl[h]].astype(BF16), dob[h], TN) for h in hs]
        ds_w = [_dot(w_ref[:, sl[h]].astype(BF16), dvn[h].astype(BF16), TN) for h in hs]
        for h in hs:
            dvn_ref[:, sl[h]] = dvn[h]
            dsh_ref[0, h] = ds[h]
            ds_ref[h] = ds_q[h] + glast[:, h:h + 1] * ds[h] - ds_w[h]

    row = pl.BlockSpec((cc, WIDTH), lambda n: (nc - 1 - n, 0))
    return pl.pallas_call(
        body, name=name,
        out_shape=(jax.ShapeDtypeStruct((t, WIDTH), F32),
                   jax.ShapeDtypeStruct((nc, HEADS, HEAD_DIM, HEAD_DIM), F32)),
        grid=(nc,),
        in_specs=[row, row, row, row, pl.BlockSpec((HEADS, cc, cc), lambda n: (0, nc - 1 - n, 0)),
                  pl.BlockSpec((cc, 128), lambda n: (nc - 1 - n, 0))],
        out_specs=(row, pl.BlockSpec((1, HEADS, HEAD_DIM, HEAD_DIM), lambda n: (nc - 1 - n, 0, 0, 0))),
        scratch_shapes=[pltpu.VMEM((HEADS, HEAD_DIM, HEAD_DIM), F32)],
        compiler_params=_params(("arbitrary",)),
    )(do, w, kd, qg, p, gates)


def _dn_local_bwd(act, gates, u, w, kd, qg, tinv, p, sh, dsh, dvn, do, name):
    t = act.shape[0]
    cc = DN_CHUNK
    nc = t // cc

    def body(q_ref, k_ref, v_ref, g_ref, u_ref, w_ref, kd_ref, qg_ref, ti_ref, p_ref, s_ref, ds_ref,
             dvn_ref, do_ref, dq_ref, dk_ref, dv_ref, dg_ref):
        gates_v = g_ref[...]
        lower, strict = _tri(cc, "lower"), _tri(cc, "strict")
        ones = jnp.ones((cc, 128), BF16)
        rowc = _iota((cc, 1), 0)
        lane = _iota((cc, 128), 1)
        hs = range(HEADS)
        sl = [slice(h * HEAD_DIM, (h + 1) * HEAD_DIM) for h in hs]
        q, k, v, uu, ww, kd, qg, dvn, do = ([r[:, s] for s in sl] for r in (
            q_ref, k_ref, v_ref, u_ref, w_ref, kd_ref, qg_ref, dvn_ref, do_ref))
        gc_col, gc_last, beta, decay = zip(*[_dn_chunk_common(gates_v, h) for h in hs])
        gam = [jnp.exp(g) for g in gc_col]
        kb = [k[h] * beta[h] for h in hs]
        s_in = [s_ref[0, h] for h in hs]
        ds_out = [ds_ref[0, h] for h in hs]
        tinv = [ti_ref[h] for h in hs]

        a = [jnp.where(strict, _dotb(kb[h], k[h], NT) * decay[h], 0.0) for h in hs]
        vn = [uu[h] - _dotb(ww[h], s_in[h], NN) for h in hs]
        dqg = [_dotb(do[h], s_in[h], NT) for h in hs]
        dw = [-_dotb(dvn[h], s_in[h], NT) for h in hs]
        dp = [jnp.where(lower, _dotb(do[h], vn[h], NT), 0.0) for h in hs]
        dkd = [_dotb(vn[h], ds_out[h], NT) for h in hs]
        dru = [_dot3(tinv[h], dvn[h], TN) for h in hs]
        drw = [_dot3(tinv[h], dw[h], TN) for h in hs]
        da = [-jnp.where(strict, _dotb(dru[h], uu[h], NT) + _dotb(drw[h], ww[h], NT), 0.0) for h in hs]
        dad = [da[h] * decay[h] for h in hs]
        dpd = [dp[h] * decay[h] for h in hs]
        dkb = [_dotb(dad[h], k[h], NN) + gam[h] * drw[h] for h in hs]
        dk = [_dotb(dad[h], kb[h], TN) + _dotb(dpd[h], q[h], TN) + beta[h] * dkb[h]
              + jnp.exp(gc_last[h] - gc_col[h]) * dkd[h] for h in hs]
        dq = [gam[h] * dqg[h] + _dotb(dpd[h], k[h], NN) for h in hs]
        gm = [da[h] * a[h] + dp[h] * p_ref[h] for h in hs]
        colsum = [_dot_xr(gm[h], ones, TN)[:, 0:1] for h in hs]

        dgates = jnp.zeros((cc, 128), F32)
        for h in hs:
            dk_ref[:, sl[h]] = dk[h]
            dq_ref[:, sl[h]] = dq[h]
            dv_ref[:, sl[h]] = beta[h] * dru[h]
            dbeta = (jnp.sum(dkb[h] * k[h], axis=1, keepdims=True)
                     + jnp.sum(dru[h] * v[h], axis=1, keepdims=True))
            rkd = jnp.sum(dkd[h] * kd[h], axis=1, keepdims=True)
            dgc = (jnp.sum(gm[h], axis=1, keepdims=True) - colsum[h]
                   + jnp.sum(dqg[h] * qg[h], axis=1, keepdims=True)
                   + jnp.sum(drw[h] * kb[h], axis=1, keepdims=True) * gam[h] - rkd)
            tail = jnp.sum(rkd, axis=0, keepdims=True) + jnp.exp(gc_last[h]) * jnp.sum(
                jnp.sum(s_in[h] * ds_out[h], axis=1, keepdims=True), axis=0, keepdims=True)
            dgc = dgc + jnp.where(rowc == cc - 1, tail, 0.0)
            dgates = dgates + jnp.where(lane == h, dgc, 0.0) + jnp.where(lane == h + HEADS, dbeta, 0.0)
        dg_ref[...] = dgates

    row = lambda off: pl.BlockSpec((cc, WIDTH), lambda n: (n, off))
    mat = pl.BlockSpec((HEADS, cc, cc), lambda n: (0, n, 0))
    st = pl.BlockSpec((1, HEADS, HEAD_DIM, HEAD_DIM), lambda n: (n, 0, 0, 0))
    gl = pl.BlockSpec((cc, 128), lambda n: (n, 0))
    tw = jax.ShapeDtypeStruct((t, WIDTH), F32)
    return pl.pallas_call(
        body, name=name,
        out_shape=(tw, tw, tw, jax.ShapeDtypeStruct((t, 128), F32)),
        grid=(nc,),
        in_specs=[row(0), row(1), row(2), gl, row(0), row(0), row(0), row(0), mat, mat, st, st, row(0), row(0)],
        out_specs=(row(0), row(0), row(0), gl),
        compiler_params=_params(("parallel",)),
    )(act, act, act, gates, u, w, kd, qg, tinv, p, sh, dsh, dvn, do)


def _dn_post_fwd(o, gate, w, name):
    t = o.shape[0]
    tr = _tile(t, 512)

    def body(o_ref, g_ref, w_ref, y_ref):
        for h in range(HEADS):
            sl = slice(h * HEAD_DIM, (h + 1) * HEAD_DIM)
            ov, gv = o_ref[:, sl], g_ref[:, sl]
            r = lax.rsqrt(jnp.mean(ov * ov, axis=1, keepdims=True) + EPS)
            y_ref[:, sl] = (ov * r * w_ref[...] * (gv * _sigmoid(gv))).astype(BF16)

    blk = pl.BlockSpec((tr, WIDTH), lambda i: (i, 0))
    return pl.pallas_call(
        body, name=name,
        out_shape=jax.ShapeDtypeStruct((t, WIDTH), BF16),
        grid=(t // tr,),
        in_specs=[blk, blk, pl.BlockSpec((1, HEAD_DIM), lambda i: (0, 0))],
        out_specs=blk,
        compiler_params=_params(("parallel",)),
    )(o, gate, w)


def _dn_post_bwd(dy, o, gate, w, name):
    t = o.shape[0]
    tr = _tile(t, 512)

    def body(dy_ref, o_ref, g_ref, w_ref, do_ref, dg_ref, dw_ref):
        i = pl.program_id(0)

        @pl.when(i == 0)
        def _():
            dw_ref[...] = jnp.zeros_like(dw_ref)

        dw = jnp.zeros((1, HEAD_DIM), F32)
        for h in range(HEADS):
            sl = slice(h * HEAD_DIM, (h + 1) * HEAD_DIM)
            ov, gv, dyv = o_ref[:, sl], g_ref[:, sl], dy_ref[:, sl].astype(F32)
            r = lax.rsqrt(jnp.mean(ov * ov, axis=1, keepdims=True) + EPS)
            oh = ov * r
            sg = _sigmoid(gv)
            dg_ref[:, sl] = (dyv * oh * w_ref[...] * (sg * (1.0 + gv * (1.0 - sg)))).astype(BF16)
            dn = dyv * (gv * sg)
            doh = dn * w_ref[...]
            do_ref[:, sl] = r * (doh - oh * jnp.mean(doh * oh, axis=1, keepdims=True))
            dw = dw + jnp.sum(dn * oh, axis=0, keepdims=True)
        dw_ref[...] += dw

    blk = pl.BlockSpec((tr, WIDTH), lambda i: (i, 0))
    return pl.pallas_call(
        body, name=name,
        out_shape=(jax.ShapeDtypeStruct((t, WIDTH), F32), jax.ShapeDtypeStruct((t, WIDTH), BF16),
                   jax.ShapeDtypeStruct((1, HEAD_DIM), F32)),
        grid=(t // tr,),
        in_specs=[blk, blk, blk, pl.BlockSpec((1, HEAD_DIM), lambda i: (0, 0))],
        out_specs=(blk, blk, pl.BlockSpec((1, HEAD_DIM), lambda i: (0, 0))),
        compiler_params=_params(("arbitrary",)),
    )(dy, o, gate, w)


def _sb_scores(q, ks, qi, j, carry_b, uincl):
    bk = ATT_BLOCK
    scale = HEAD_DIM ** -0.5
    z = _dot(q, ks, NT) * scale
    qpos = qi * bk + _iota(z.shape, 0)
    kpos = j * bk + _iota(z.shape, 1)
    mask = kpos < qpos
    soft = _log1pexp_neg_abs(z)
    lk_full = -(jnp.maximum(z, 0.0) + soft)
    lk = jnp.where(mask, lk_full, 0.0)
    ls = jnp.minimum(z, 0.0) - soft
    incl = _dot_xr2(lk, uincl, NN)
    a = jnp.where(mask, jnp.exp(ls + (carry_b + incl - lk)), 0.0)
    return a, mask, lk_full, ls, carry_b + incl[:, 0:1]


def _sb_more(qi, carry):
    it, cb = carry[0], carry[1]
    return jnp.logical_and(it <= qi, jnp.max(cb) > SB_LOG_ZERO)


def _sb_steps(nq):
    def when():
        h, i = pl.program_id(0), pl.program_id(1)
        return (jnp.logical_and(h == 0, i == 0), jnp.logical_and(h == HEADS // 2, i == 0),
                jnp.logical_and(h == HEADS - 1, i == nq - 1))
    return when


def _sb_fwd(qkv, name, comm=None):
    t = qkv.shape[0]
    bk = ATT_BLOCK

    def body(q_ref, k_ref, v_ref, o_ref):
        qi = pl.program_id(1)
        q = q_ref[...]
        uincl = jnp.where(_tri(bk, "lower"), 1.0, 0.0).astype(BF16)

        def step(carry):
            it, cb, acc = carry
            j = qi - it
            rows = pl.ds(pl.multiple_of(j * bk, bk), bk)
            a, _, _, _, cb = _sb_scores(q, k_ref[rows, :], qi, j, cb, uincl)
            acc = acc + _dot(a.astype(BF16), v_ref[rows, :], NN)
            return it + 1, cb, acc

        init = (jnp.int32(0), jnp.zeros((bk, 1), F32), jnp.zeros((bk, HEAD_DIM), F32))
        _, _, acc = lax.while_loop(functools.partial(_sb_more, qi), step, init)
        o_ref[...] = acc

    (o,), extra = _host_call(
        body, name, comm, _sb_steps(t // bk), [jax.ShapeDtypeStruct((t, WIDTH), F32)], (HEADS, t // bk),
        [pl.BlockSpec((bk, HEAD_DIM), lambda h, i: (i, h)),
         pl.BlockSpec((t, HEAD_DIM), lambda h, i: (0, HEADS + h)),
         pl.BlockSpec((t, HEAD_DIM), lambda h, i: (0, 2 * HEADS + h))],
        [pl.BlockSpec((bk, HEAD_DIM), lambda h, i: (i, h))], [], ("parallel", "arbitrary"), (qkv, qkv, qkv))
    return o, extra


def _sb_bwd(qkv, o, do, name, comm=None):
    t = qkv.shape[0]
    bk = ATT_BLOCK
    scale = HEAD_DIM ** -0.5

    def body(q_ref, k_ref, v_ref, o_ref, do_ref, dq_ref, dk_ref, dv_ref):
        qi = pl.program_id(1)

        @pl.when(qi == 0)
        def _():
            dk_ref[...] = jnp.zeros_like(dk_ref)
            dv_ref[...] = jnp.zeros_like(dv_ref)

        q = q_ref[...]
        dov = do_ref[...]
        dob = dov.astype(BF16)
        do1, do2 = _split2(dov)
        dsum = jnp.sum(dov * o_ref[...], axis=1, keepdims=True)
        uincl = jnp.where(_tri(bk, "lower"), 1.0, 0.0).astype(BF16)

        def step(carry):
            it, cb, ce, dq = carry
            j = qi - it
            rows = pl.ds(pl.multiple_of(j * bk, bk), bk)
            ks = k_ref[rows, :]
            a, mask, lk_full, ls, cb = _sb_scores(q, ks, qi, j, cb, uincl)
            ab = a.astype(BF16)
            vs = v_ref[rows, :]
            dla = ab.astype(F32) * (_dot(do1, vs, NT) + _dot(do2, vs, NT))
            suf = _dot_xr2(dla, uincl, NN)
            e = dsum - (ce + suf)
            dz = jnp.where(mask, dla * jnp.exp(lk_full) - e * jnp.exp(ls), 0.0)
            dzb = (dz * scale).astype(BF16)
            dq = dq + _dot(dzb, ks, NN)
            dk_ref[rows, :] += _dot(dzb, q, TN)
            dv_ref[rows, :] += _dot(ab, dob, TN)
            return it + 1, cb, ce + suf[:, 0:1], dq

        zc = jnp.zeros((bk, 1), F32)
        init = (jnp.int32(0), zc, zc, jnp.zeros((bk, HEAD_DIM), F32))
        dq_ref[...] = lax.while_loop(functools.partial(_sb_more, qi), step, init)[3]

    tw = jax.ShapeDtypeStruct((t, WIDTH), F32)
    qb = pl.BlockSpec((bk, HEAD_DIM), lambda h, i: (i, h))
    full = lambda off: pl.BlockSpec((t, HEAD_DIM), lambda h, i: (0, off + h))
    return _host_call(
        body, name, comm, _sb_steps(t // bk), [tw, tw, tw], (HEADS, t // bk),
        [qb, full(HEADS), full(2 * HEADS), qb, qb], [qb, full(0), full(0)], [], ("parallel", "arbitrary"),
        (qkv, qkv, qkv, o, do))


def _merge_fwd(pd, ps, gl, name):
    t = pd.shape[0]
    tr, tc = _tile(t, 512), 512
    nj = D_MODEL // tc

    def body(pd_ref, ps_ref, gd_ref, gs_ref, o_ref):
        o_ref[...] = (_sigmoid(gd_ref[...]) * pd_ref[...] + _sigmoid(gs_ref[...]) * ps_ref[...]).astype(BF16)

    blk = lambda off: pl.BlockSpec((tr, tc), lambda i, j: (i, j + off))
    return pl.pallas_call(
        body, name=name,
        out_shape=jax.ShapeDtypeStruct((t, D_MODEL), BF16),
        grid=(t // tr, nj),
        in_specs=[blk(0), blk(0), blk(0), blk(nj)],
        out_specs=blk(0),
        compiler_params=_params(("parallel", "parallel")),
    )(pd, ps, gl, gl)


def _merge_bwd(dm, pd, ps, gl, name):
    t = pd.shape[0]
    tr, tc = _tile(t, 512), 512
    nj = D_MODEL // tc

    def body(dm_ref, pd_ref, ps_ref, gd_ref, gs_ref, dpd_ref, dps_ref, dgd_ref, dgs_ref):
        dmv = dm_ref[...]
        sd, ss = _sigmoid(gd_ref[...]), _sigmoid(gs_ref[...])
        dpd_ref[...] = (dmv * sd).astype(BF16)
        dps_ref[...] = (dmv * ss).astype(BF16)
        dgd_ref[...] = (dmv * pd_ref[...] * sd * (1.0 - sd)).astype(BF16)
        dgs_ref[...] = (dmv * ps_ref[...] * ss * (1.0 - ss)).astype(BF16)

    blk = lambda off: pl.BlockSpec((tr, tc), lambda i, j: (i, j + off))
    out = jax.ShapeDtypeStruct((t, D_MODEL), BF16)
    return pl.pallas_call(
        body, name=name,
        out_shape=(out, out, out, out),
        grid=(t // tr, nj),
        in_specs=[blk(0), blk(0), blk(0), blk(0), blk(nj)],
        out_specs=(blk(0), blk(0), blk(0), blk(0)),
        compiler_params=_params(("parallel", "parallel")),
    )(dm, pd, ps, gl, gl)


def _local_step(x, target, wts, plan=None):
    n1 = _rmsnorm_fwd(x, wts["norm1_w"], "norm1_fwd")
    qkv_pre = _matmul(n1, wts["w_dnqkv"], "nn", F32, "in_dnqkv")
    hgate = _matmul(n1, wts["w_dngate"], "nn", F32, "in_dngate")
    sbqkv = _matmul(n1, wts["w_sbqkv"], "nn", BF16, "in_sbqkv")
    gl = _matmul(n1, wts["w_gl"], "nn", F32, "in_gl")
    hab = _matmul(n1, wts["w_ab"], "nn", F32, "in_ab")

    act = _dn_pre_fwd(qkv_pre, wts["dn_conv_w"], "dn_pre_fwd")
    gates = _dn_gates_fwd(hab, wts["alog"], wts["dtb"], "dn_gates_fwd")
    u, w, kd, qg, tinv, p = _dn_local_fwd(act, gates, "dn_local_fwd")
    o_dn, sh = _dn_scan_fwd(u, w, kd, qg, p, gates, "dn_scan_fwd")
    y_dn = _dn_post_fwd(o_dn, hgate, wts["dn_norm_w"], "dn_post_fwd")

    o_sb, late = _sb_fwd(sbqkv, "sb_fwd", comm=plan.late_gather() if plan else None)
    if plan:
        wts = {**wts, **plan.late_weights(late)}

    pd = _matmul(y_dn, wts["w_proj_dn"], "nn", F32, "proj_dn")
    ps = _matmul(o_sb, wts["w_proj_sb"], "nn", F32, "proj_sb")
    mixed = _merge_fwd(pd, ps, gl, "merge_fwd")
    x1 = _matmul(mixed, wts["w_out"], "nn", F32, "out_proj", add=x)

    n2 = _rmsnorm_fwd(x1, wts["norm2_w"], "norm2_fwd")
    upre = _matmul(n2, wts["ffn_w_up"], "nn", F32, "ffn_up")
    fact = _ffn_act_fwd(upre, wts["ffn_conv_w"], "ffn_act_fwd")
    x2 = _matmul(fact, wts["ffn_w_down"], "nn", F32, "ffn_down", add=x1)

    dx2, g_normf, loss = _final_loss(x2, target, wts["norm_f_w"], "final_loss")

    dfact = _matmul(dx2, wts["ffn_w_down"], "nt", BF16, "ffn_down_dx")
    g_wdown = _matmul(fact, dx2, "tn", BF16, "ffn_down_dw")
    dgc, duc, dwg, dwu = _ffn_act_bwd(dfact, upre, wts["ffn_conv_w"], "ffn_act_bwd")
    g_fconv = jnp.concatenate([dwg, dwu], axis=1)
    dupre = _conv_bwd_data([dgc, duc], wts["ffn_conv_w"], FFN_CONV, BF16, "ffn_conv_bwd")
    dn2 = _matmul(dupre, wts["ffn_w_up"], "nt", F32, "ffn_up_dx")
    g_wup = _matmul(n2, dupre, "tn", BF16, "ffn_up_dw")
    dx1, g_norm2 = _rmsnorm_bwd(dn2, x1, wts["norm2_w"], dx2, "norm2_bwd")

    dmixed = _matmul(dx1, wts["w_out"], "nt", F32, "out_proj_dx")
    g_wout = _matmul(mixed, dx1, "tn", BF16, "out_proj_dw")
    dpd, dps, dgd, dgs = _merge_bwd(dmixed, pd, ps, gl, "merge_bwd")
    dy_dn = _matmul(dpd, wts["w_proj_dn"], "nt", F32, "proj_dn_dx")
    g_wpd = _matmul(y_dn, dpd, "tn", BF16, "proj_dn_dw")
    do_sb = _matmul(dps, wts["w_proj_sb"], "nt", F32, "proj_sb_dx")
    g_wps = _matmul(o_sb, dps, "tn", BF16, "proj_sb_dw")
    grads = dict(w_proj_dn=g_wpd, w_proj_sb=g_wps, w_out=g_wout, ffn_w_up=g_wup, ffn_w_down=g_wdown)

    (dsq, dsk, dsv), got_early = _sb_bwd(sbqkv, o_sb, do_sb, "sb_bwd",
                                         comm=plan.early_grads(grads) if plan else None)

    do_dn, dhgate, g_dnnorm = _dn_post_bwd(dy_dn, o_dn, hgate, wts["dn_norm_w"], "dn_post_bwd")
    dvn, dsh = _dn_scan_bwd(do_dn, w, kd, qg, p, gates, "dn_scan_bwd")
    dq, dk, dv, dgates = _dn_local_bwd(act, gates, u, w, kd, qg, tinv, p, sh, dsh, dvn, do_dn, "dn_local_bwd")
    dhab, g_alog, g_dtb = _dn_gates_bwd(dgates, hab, wts["alog"], wts["dtb"], "dn_gates_bwd")
    dcv, g_dnconv = _dn_pre_bwd(dq, dk, dv, qkv_pre, wts["dn_conv_w"], "dn_pre_bwd")
    dqkv_pre = _conv_bwd_data([dcv], wts["dn_conv_w"], DN_CONV, BF16, "dn_conv_bwd")

    dh = jnp.concatenate([dqkv_pre, dhgate, dsq.astype(BF16), dsk.astype(BF16), dsv.astype(BF16), dgd, dgs], axis=1)
    w_main = jnp.concatenate([wts["w_dnqkv"], wts["w_dngate"], wts["w_sbqkv"], wts["w_gl"]], axis=1)
    g_wmain = _matmul(n1, dh, "tn", BF16, "in_dw_main")
    g_wab = _matmul(n1, dhab, "tn", BF16, "in_dw_ab")
    grads.update(w_main=g_wmain, w_ab=g_wab, dn_conv_w=g_dnconv, alog=g_alog, dtb=g_dtb, dn_norm_w=g_dnnorm,
                 norm2_w=g_norm2, ffn_conv_w=g_fconv, norm_f_w=g_normf)
    got_late = []
    if plan:
        dn1, got_late = _matmul(dh, w_main, "nt", F32, "in_dx_main", comm=plan.late_grads(grads, loss))
    else:
        dn1 = _matmul(dh, w_main, "nt", F32, "in_dx_main")
    dn1 = _matmul(dhab, wts["w_ab"], "nt", F32, "in_dx_ab", add=dn1)
    grad_x, g_norm1 = _rmsnorm_bwd(dn1, x, wts["norm1_w"], dx1, "norm1_bwd")
    grads["norm1_w"] = g_norm1
    return loss, grad_x, grads, got_early, got_late


HBM_SPEC = pl.BlockSpec(memory_space=pltpu.HBM)


def _mesh_pos():
    x, y, c = lax.axis_index("x"), lax.axis_index("y"), lax.axis_index("c")
    return x, y, c, 4 * x + 2 * y + c


def _peer(k):
    x, y, c, _ = _mesh_pos()
    px = 1 - x if k & 4 else x
    py = 1 - y if k & 2 else y
    pc = 1 - c if k & 1 else c
    return (px, py, pc), 4 * px + 2 * py + pc


def _rcopy(src, dst, send, recv, a, s, peer):
    return pltpu.make_async_remote_copy(src_ref=src, dst_ref=dst, send_sem=send.at[a, s], recv_sem=recv.at[a, s],
                                        device_id=peer, device_id_type=pl.DeviceIdType.MESH)


class _Gather:
    ICI = (2, 4, 6)

    def __init__(self, shards):
        self.args = list(shards)
        self.n = len(shards)
        self.out_shape = [jax.ShapeDtypeStruct((N_DEV,) + s.shape, s.dtype) for s in shards]
        self.scratch = [pltpu.SemaphoreType.DMA((self.n, N_DEV - 1)), pltpu.SemaphoreType.DMA((self.n, N_DEV - 1)),
                        pltpu.SemaphoreType.DMA((self.n,))]

    def _first(self, ins, outs, send, recv, a):
        me = _mesh_pos()[3]
        out, got = [], []
        for s, k in enumerate((1,) + self.ICI):
            peer, pidx = _peer(k)
            out.append(_rcopy(ins[a], outs[a].at[me], send, recv, a, s, peer))
            got.append(_rcopy(ins[a], outs[a].at[pidx], send, recv, a, s, peer))
        return out, got

    def _forward(self, ins, outs, send, recv, a):
        sib = _peer(1)[0]
        out, got = [], []
        for s, k in enumerate(self.ICI):
            held = outs[a].at[_peer(k)[1]]
            out.append(_rcopy(held, held, send, recv, a, 4 + s, sib))
            other = outs[a].at[_peer(k | 1)[1]]
            got.append(_rcopy(other, other, send, recv, a, 4 + s, sib))
        return out, got

    def start(self, ins, outs, sems):
        send, recv, loc = sems
        me = _mesh_pos()[3]
        for a in range(self.n):
            pltpu.make_async_copy(ins[a], outs[a].at[me], loc.at[a]).start()
            for cp in self._first(ins, outs, send, recv, a)[0]:
                cp.start()

    def mid(self, ins, outs, sems):
        send, recv, _ = sems
        for a in range(self.n):
            arrivals = self._first(ins, outs, send, recv, a)[1]
            for s, cp in enumerate(self._forward(ins, outs, send, recv, a)[0]):
                arrivals[1 + s].wait_recv()
                cp.start()

    def finish(self, ins, outs, sems):
        send, recv, loc = sems
        me = _mesh_pos()[3]
        for a in range(self.n):
            first_out, first_got = self._first(ins, outs, send, recv, a)
            fwd_out, fwd_got = self._forward(ins, outs, send, recv, a)
            first_got[0].wait_recv()
            for cp in fwd_got:
                cp.wait_recv()
            for cp in first_out + fwd_out:
                cp.wait_send()
            pltpu.make_async_copy(ins[a], outs[a].at[me], loc.at[a]).wait()


class _Exchange:
    def __init__(self, slabs, gathered=()):
        self.args = list(slabs) + list(gathered)
        self.n_slab = len(slabs)
        self.n = len(self.args)
        self.out_shape = ([jax.ShapeDtypeStruct(s.shape, s.dtype) for s in slabs]
                          + [jax.ShapeDtypeStruct((N_DEV,) + s.shape, s.dtype) for s in gathered])
        self.scratch = [pltpu.SemaphoreType.DMA((self.n, N_DEV - 1)), pltpu.SemaphoreType.DMA((self.n, N_DEV - 1)),
                        pltpu.SemaphoreType.DMA((self.n,))]

    def _copies(self, ins, outs, send, recv, a):
        me = _mesh_pos()[3]
        out, got = [], []
        for k in range(1, N_DEV):
            peer, pidx = _peer(k)
            src = ins[a].at[pidx] if a < self.n_slab else ins[a]
            out.append(_rcopy(src, outs[a].at[me], send, recv, a, k - 1, peer))
            got.append(_rcopy(src, outs[a].at[pidx], send, recv, a, k - 1, peer))
        return out, got

    def _local(self, ins, outs, loc, a):
        me = _mesh_pos()[3]
        return pltpu.make_async_copy(ins[a].at[me] if a < self.n_slab else ins[a], outs[a].at[me], loc.at[a])

    def start(self, ins, outs, sems):
        send, recv, loc = sems
        for a in range(self.n):
            self._local(ins, outs, loc, a).start()
            for cp in self._copies(ins, outs, send, recv, a)[0]:
                cp.start()

    def mid(self, ins, outs, sems):
        pass

    def finish(self, ins, outs, sems):
        send, recv, loc = sems
        for a in range(self.n):
            out, got = self._copies(ins, outs, send, recv, a)
            for cp in got:
                cp.wait_recv()
            for cp in out:
                cp.wait_send()
            self._local(ins, outs, loc, a).wait()


def _comm_call(comm, name):
    n = comm.n

    def body(*refs):
        ins, outs, sems = refs[:n], refs[n:2 * n], refs[2 * n:]
        comm.start(ins, outs, sems)
        comm.mid(ins, outs, sems)
        comm.finish(ins, outs, sems)

    return pl.pallas_call(
        body, name=name, out_shape=comm.out_shape, in_specs=[HBM_SPEC] * n, out_specs=[HBM_SPEC] * n,
        scratch_shapes=comm.scratch,
    )(*comm.args)


def _hosted(body, comm, n_in, n_out, when):
    if comm is None:
        return body

    def wrapped(*refs):
        ins, c_ins = refs[:n_in], refs[n_in:n_in + comm.n]
        o0 = n_in + comm.n
        outs, c_outs = refs[o0:o0 + n_out], refs[o0 + n_out:o0 + n_out + comm.n]
        scratch, sems = refs[o0 + n_out + comm.n:len(refs) - 3], refs[len(refs) - 3:]
        first, middle, last = when()

        @pl.when(first)
        def _():
            comm.start(c_ins, c_outs, sems)

        body(*ins, *outs, *scratch)

        @pl.when(middle)
        def _():
            comm.mid(c_ins, c_outs, sems)

        @pl.when(last)
        def _():
            comm.finish(c_ins, c_outs, sems)

    return wrapped


def _host_call(body, name, comm, when, out_shape, grid, in_specs, out_specs, scratch_shapes, sem, args):
    n_in, n_out = len(in_specs), len(out_specs)
    if comm is None:
        res = pl.pallas_call(body, name=name, out_shape=out_shape, grid=grid, in_specs=in_specs, out_specs=out_specs,
                             scratch_shapes=scratch_shapes, compiler_params=_params(sem))(*args)
        return list(res), []
    res = pl.pallas_call(
        _hosted(body, comm, n_in, n_out, when), name=name,
        out_shape=list(out_shape) + comm.out_shape, grid=grid,
        in_specs=list(in_specs) + [HBM_SPEC] * comm.n, out_specs=list(out_specs) + [HBM_SPEC] * comm.n,
        scratch_shapes=list(scratch_shapes) + comm.scratch,
        compiler_params=_params(("arbitrary",) * len(grid)),
    )(*args, *comm.args)
    return list(res[:n_out]), list(res[n_out:])


def _adamw(parts, w, m, v, name):
    rows, cols = w.shape
    tr = rows
    for cand in (128, 176):
        if rows > cand and rows % cand == 0:
            tr = cand
            break

    def body(p_ref, w_ref, m_ref, v_ref, g_ref, d_ref, mo_ref, vo_ref):
        g = p_ref[0].astype(F32)
        for s in range(1, N_DEV):
            g = g + p_ref[s].astype(F32)
        mn = ADAM_B1 * m_ref[...] + (1.0 - ADAM_B1) * g
        vn = ADAM_B2 * v_ref[...] + (1.0 - ADAM_B2) * (g * g)
        m_hat = mn / (1.0 - ADAM_B1 ** ADAM_STEP)
        v_hat = vn / (1.0 - ADAM_B2 ** ADAM_STEP)
        g_ref[...] = g
        d_ref[...] = -ADAM_LR * (m_hat / (jnp.sqrt(v_hat) + ADAM_EPS) + ADAM_WD * w_ref[...])
        mo_ref[...] = mn
        vo_ref[...] = vn

    blk = pl.BlockSpec((tr, cols), lambda i: (i, 0))
    out = jax.ShapeDtypeStruct((rows, cols), F32)
    return pl.pallas_call(
        body, name=name,
        out_shape=(out, out, out, out),
        grid=(rows // tr,),
        in_specs=[pl.BlockSpec((N_DEV, tr, cols), lambda i: (0, i, 0)), blk, blk, blk],
        out_specs=(blk, blk, blk, blk),
        compiler_params=_params(("parallel",)),
    )(parts, w, m, v)


CONV_PACK = 8 * 1024
WEIGHT_ORDER = ("norm1_w", "w_in", "dn_conv_w", "dn_A_log", "dn_dt_bias", "dn_norm_w", "w_proj_dn", "w_proj_sb",
                "w_out", "norm2_w", "ffn_w_up", "ffn_conv_w", "ffn_w_down", "norm_f_w")


def _cols_to_slabs(g):
    r, c8 = g.shape
    return g.reshape(r, N_DEV, c8 // N_DEV).transpose(1, 0, 2)


def _slabs_to_cols(s):
    d, r, c = s.shape
    return s.transpose(1, 0, 2).reshape(r, d * c)


def kernel(x, norm1_w, w_in, dn_conv_w, dn_A_log, dn_dt_bias, dn_norm_w, w_proj_dn, w_proj_sb, w_out, norm2_w, ffn_w_up, ffn_conv_w, ffn_w_down, norm_f_w, loss_target, m_norm1_w, m_w_in, m_dn_conv_w, m_dn_A_log, m_dn_dt_bias, m_dn_norm_w, m_w_proj_dn, m_w_proj_sb, m_w_out, m_norm2_w, m_ffn_w_up, m_ffn_conv_w, m_ffn_w_down, m_norm_f_w, v_norm1_w, v_w_in, v_dn_conv_w, v_dn_A_log, v_dn_dt_bias, v_dn_norm_w, v_w_proj_dn, v_w_proj_sb, v_w_out, v_norm2_w, v_ffn_w_up, v_ffn_conv_w, v_ffn_w_down, v_norm_f_w):
    me = _mesh_pos()[3]
    w_loc = dict(norm1_w=norm1_w, w_in=w_in[0], dn_conv_w=dn_conv_w[0], dn_A_log=dn_A_log, dn_dt_bias=dn_dt_bias,
                 dn_norm_w=dn_norm_w, w_proj_dn=w_proj_dn[0], w_proj_sb=w_proj_sb[0], w_out=w_out[0],
                 norm2_w=norm2_w, ffn_w_up=ffn_w_up[0], ffn_conv_w=ffn_conv_w[0], ffn_w_down=ffn_w_down[0],
                 norm_f_w=norm_f_w[None, :])
    m_loc = dict(norm1_w=m_norm1_w, w_in=m_w_in[0], dn_conv_w=m_dn_conv_w[0], dn_A_log=m_dn_A_log,
                 dn_dt_bias=m_dn_dt_bias, dn_norm_w=m_dn_norm_w, w_proj_dn=m_w_proj_dn[0], w_proj_sb=m_w_proj_sb[0],
                 w_out=m_w_out[0], norm2_w=m_norm2_w, ffn_w_up=m_ffn_w_up[0], ffn_conv_w=m_ffn_conv_w[0],
                 ffn_w_down=m_ffn_w_down[0], norm_f_w=m_norm_f_w[None, :])
    v_loc = dict(norm1_w=v_norm1_w, w_in=v_w_in[0], dn_conv_w=v_dn_conv_w[0], dn_A_log=v_dn_A_log,
                 dn_dt_bias=v_dn_dt_bias, dn_norm_w=v_dn_norm_w, w_proj_dn=v_w_proj_dn[0], w_proj_sb=v_w_proj_sb[0],
                 w_out=v_w_out[0], norm2_w=v_norm2_w, ffn_w_up=v_ffn_w_up[0], ffn_conv_w=v_ffn_conv_w[0],
                 ffn_w_down=v_ffn_w_down[0], norm_f_w=v_norm_f_w[None, :])

    conv_flat = jnp.concatenate([w_loc["dn_conv_w"].reshape(-1), w_loc["ffn_conv_w"].reshape(-1)])
    n_dn, n_ffn = DN_CONV * 3 * WIDTH // N_DEV, FFN_CONV * 2 * D_FF // N_DEV
    conv_pack = jnp.pad(conv_flat, (0, CONV_PACK - n_dn - n_ffn)).reshape(8, 1024)
    g_in, g_conv = _comm_call(_Gather([w_loc["w_in"].astype(BF16), conv_pack]), "gather_first")
    w_in_full = _slabs_to_cols(g_in)
    g_conv = g_conv.reshape(N_DEV, CONV_PACK)
    dn_conv_full = _slabs_to_cols(g_conv[:, :n_dn].reshape(N_DEV, DN_CONV, 3 * WIDTH // N_DEV))
    ffn_conv_full = _slabs_to_cols(g_conv[:, n_dn:n_dn + n_ffn].reshape(N_DEV, FFN_CONV, 2 * D_FF // N_DEV))
    q_end = 3 * WIDTH
    ab_end = q_end + 2 * HEADS
    gate_end = ab_end + WIDTH
    sb_end = gate_end + 3 * WIDTH
    pad_lanes = lambda a: jnp.pad(a, ((0, 0), (0, 128 - a.shape[1])))
    wts = dict(
        norm1_w=norm1_w, w_dnqkv=w_in_full[:, :q_end], w_ab=pad_lanes(w_in_full[:, q_end:ab_end]),
        w_dngate=w_in_full[:, ab_end:gate_end], w_sbqkv=w_in_full[:, gate_end:sb_end], w_gl=w_in_full[:, sb_end:],
        dn_conv_w=dn_conv_full, alog=pad_lanes(dn_A_log), dtb=pad_lanes(dn_dt_bias), dn_norm_w=dn_norm_w,
        norm2_w=norm2_w, ffn_conv_w=ffn_conv_full, norm_f_w=norm_f_w[None, :])

    n_fc = FFN_CONV * 2 * D_FF
    fc_rows = -(-n_fc // D_MODEL)
    dn_rows = DN_CONV * 3 * WIDTH // D_MODEL
    late_names = ("w_proj_dn", "w_proj_sb", "w_out", "ffn_w_up", "ffn_w_down")

    class Plan:
        @staticmethod
        def late_gather():
            return _Gather([w_loc[k].astype(BF16) for k in late_names])

        @staticmethod
        def late_weights(got):
            g_pd, g_ps, g_out, g_up, g_down = got
            return dict(w_proj_dn=g_pd.reshape(WIDTH, D_MODEL), w_proj_sb=g_ps.reshape(WIDTH, D_MODEL),
                        w_out=g_out.reshape(D_MODEL, D_MODEL), ffn_w_up=_slabs_to_cols(g_up),
                        ffn_w_down=g_down.reshape(D_FF, D_MODEL))

        @staticmethod
        def early_grads(g):
            return _Exchange([g["w_proj_dn"].reshape(N_DEV, WIDTH // N_DEV, D_MODEL),
                              g["w_proj_sb"].reshape(N_DEV, WIDTH // N_DEV, D_MODEL),
                              g["w_out"].reshape(N_DEV, D_MODEL // N_DEV, D_MODEL), _cols_to_slabs(g["ffn_w_up"]),
                              g["ffn_w_down"].reshape(N_DEV, D_FF // N_DEV, D_MODEL)])

        @staticmethod
        def late_grads(g, loss):
            g_win = jnp.concatenate([g["w_main"][:, :q_end], g["w_ab"][:, :2 * HEADS], g["w_main"][:, q_end:]],
                                    axis=1)
            row3 = jnp.concatenate([g["dn_norm_w"], g["alog"], g["dtb"], jnp.pad(loss, ((0, 0), (0, 127))),
                                    jnp.zeros((1, D_MODEL - 512), F32)], axis=1)
            fconv_rows = jnp.pad(g["ffn_conv_w"].reshape(-1), (0, fc_rows * D_MODEL - n_fc)).reshape(fc_rows, D_MODEL)
            pad8 = lambda a: jnp.pad(a, ((0, -a.shape[0] % 8), (0, 0)))
            pieces = [g["norm2_w"], g["norm_f_w"], row3, g["dn_conv_w"].reshape(dn_rows, D_MODEL), fconv_rows]
            small = jnp.concatenate([pad8(a) for a in pieces], axis=0)
            assert small.shape[0] == SMALL_ROWS
            return _Exchange([_cols_to_slabs(g_win)], [small])

    loss, grad_x, g, got_early, got_late = _local_step(x[0], loss_target[0], wts, Plan)
    r_pd, r_ps, r_out, r_up, r_down = got_early
    r_in, r_small = got_late
    (r_norm1,) = _comm_call(_Exchange([], [jnp.pad(g["norm1_w"], ((0, 7), (0, 0)))]), "gather_norm1")

    parts = dict(w_in=r_in, w_proj_dn=r_pd, w_proj_sb=r_ps, w_out=r_out, ffn_w_up=r_up, ffn_w_down=r_down)
    parts["norm1_w"] = r_norm1[:, 0:1, :]
    parts["norm2_w"] = r_small[:, 0:1, :]
    parts["norm_f_w"] = r_small[:, 8:9, :]
    parts["dn_norm_w"] = r_small[:, 16:17, 0:HEAD_DIM]
    parts["dn_A_log"] = r_small[:, 16:17, 128:128 + HEADS]
    parts["dn_dt_bias"] = r_small[:, 16:17, 256:256 + HEADS]
    dnc = r_small[:, 24:24 + dn_rows, :].reshape(N_DEV, DN_CONV, 3 * WIDTH)
    parts["dn_conv_w"] = lax.dynamic_slice_in_dim(dnc, me * (3 * WIDTH // N_DEV), 3 * WIDTH // N_DEV, axis=2)
    fc0 = 24 + dn_rows + (-dn_rows % 8)
    fcc = r_small[:, fc0:fc0 + fc_rows, :].reshape(N_DEV, fc_rows * D_MODEL)[:, :n_fc]
    fcc = fcc.reshape(N_DEV, FFN_CONV, 2 * D_FF)
    parts["ffn_conv_w"] = lax.dynamic_slice_in_dim(fcc, me * (2 * D_FF // N_DEV), 2 * D_FF // N_DEV, axis=2)
    loss_total = jnp.sum(r_small[:, 16, 384])

    res = {k: _adamw(parts[k], w_loc[k], m_loc[k], v_loc[k], "adamw_" + k) for k in WEIGHT_ORDER}
    lead = ("w_in", "dn_conv_w", "w_proj_dn", "w_proj_sb", "w_out", "ffn_w_up", "ffn_conv_w", "ffn_w_down")

    def shaped(k, a):
        if k in lead:
            return a[None]
        if k == "norm_f_w":
            return a[0]
        return a

    outs = [loss_total, grad_x[None]]
    for idx in range(4):
        outs += [shaped(k, res[k][idx]) for k in WEIGHT_ORDER]
    return tuple(outs)
```

```python
import functools

import jax
import jax.numpy as jnp
from jax import lax
from jax.experimental import pallas as pl
from jax.experimental.pallas import tpu as pltpu

F32 = jnp.float32
BF16 = jnp.bfloat16

N_DEV = 8
D_MODEL = 1024
HEADS = 8
HEAD_DIM = 128
WIDTH = HEADS * HEAD_DIM
DN_CONV = 4
DN_CHUNK = 64
D_FF = 2816
FFN_CONV = 3
EPS = 1e-6
ATT_BLOCK = 256
SB_LOG_ZERO = -104.0
SMALL_ROWS = 64

ADAM_LR = 0.001
ADAM_B1 = 0.9
ADAM_B2 = 0.999
ADAM_EPS = 1e-08
ADAM_WD = 0.01
ADAM_STEP = 10

VMEM_LIMIT = 48 * 1024 * 1024


def _params(sem=None, **kw):
    return pltpu.CompilerParams(dimension_semantics=sem, vmem_limit_bytes=VMEM_LIMIT, **kw)


def _tile(n, cap):
    if n <= cap:
        return n
    best = None
    for t in range(128, cap + 1, 128):
        if n % t == 0:
            best = t
    assert best is not None, (n, cap)
    return best


def _dot(a, b, dims):
    return lax.dot_general(a, b, ((dims[0], dims[1]), ((), ())), preferred_element_type=F32)


NN = ((1,), (0,))
NT = ((1,), (1,))
TN = ((0,), (0,))


def _dotb(a, b, dims):
    return _dot(a.astype(BF16), b.astype(BF16), dims)


def _split3(x):
    h1 = x.astype(BF16)
    r1 = x - h1.astype(F32)
    h2 = r1.astype(BF16)
    r2 = r1 - h2.astype(F32)
    return h1, h2, r2.astype(BF16)


def _dot_xr(a, b_exact, dims):
    a1, a2, a3 = _split3(a)
    return _dot(a1, b_exact, dims) + _dot(a2, b_exact, dims) + _dot(a3, b_exact, dims)


def _split2(x):
    h1 = x.astype(BF16)
    return h1, (x - h1.astype(F32)).astype(BF16)


def _dot_xr2(a, b_exact, dims):
    a1, a2 = _split2(a)
    return _dot(a1, b_exact, dims) + _dot(a2, b_exact, dims)


def _dot_xl(a_exact, b, dims):
    b1, b2, b3 = _split3(b)
    return _dot(a_exact, b1, dims) + _dot(a_exact, b2, dims) + _dot(a_exact, b3, dims)


def _dot3(a, b, dims):
    a1 = a.astype(BF16)
    a2 = (a - a1.astype(F32)).astype(BF16)
    b1 = b.astype(BF16)
    b2 = (b - b1.astype(F32)).astype(BF16)
    return _dot(a1, b1, dims) + (_dot(a1, b2, dims) + _dot(a2, b1, dims))


def _sigmoid(x):
    return 1.0 / (1.0 + jnp.exp(-x))


def _log1pexp_neg_abs(x):
    return jnp.log(1.0 + jnp.exp(-jnp.abs(x)))


def _iota(shape, dim):
    return lax.broadcasted_iota(jnp.int32, shape, dim)


def _matmul(a, b, mode, out_dtype, name, add=None, comm=None):
    if mode == "nn":
        (m, k), (k2, n) = a.shape, b.shape
    elif mode == "nt":
        (m, k), (n, k2) = a.shape, b.shape
    else:
        (k, m), (k2, n) = a.shape, b.shape
    assert k == k2, (a.shape, b.shape, mode)
    tm, tn, tk = _tile(m, 1024), _tile(n, 1408), _tile(k, 1536)
    nk = k // tk
    dims = {"nn": NN, "nt": NT, "tn": TN}[mode]

    def body(*refs):
        if add is None:
            a_ref, b_ref, o_ref, acc_ref = refs
        else:
            a_ref, b_ref, add_ref, o_ref, acc_ref = refs
        kk = pl.program_id(2)

        @pl.when(kk == 0)
        def _():
            acc_ref[...] = jnp.zeros_like(acc_ref)

        acc_ref[...] += _dotb(a_ref[...], b_ref[...], dims)

        @pl.when(kk == nk - 1)
        def _():
            r = acc_ref[...]
            if add is not None:
                r = r + add_ref[...].astype(F32)
            o_ref[...] = r.astype(out_dtype)

    if mode == "nn":
        specs = [pl.BlockSpec((tm, tk), lambda i, j, l: (i, l)), pl.BlockSpec((tk, tn), lambda i, j, l: (l, j))]
    elif mode == "nt":
        specs = [pl.BlockSpec((tm, tk), lambda i, j, l: (i, l)), pl.BlockSpec((tn, tk), lambda i, j, l: (j, l))]
    else:
        specs = [pl.BlockSpec((tk, tm), lambda i, j, l: (l, i)), pl.BlockSpec((tk, tn), lambda i, j, l: (l, j))]
    args = [a, b]
    if add is not None:
        specs.append(pl.BlockSpec((tm, tn), lambda i, j, l: (i, j)))
        args.append(add)
    grid = (m // tm, n // tn, nk)

    def when():
        i, j, l = pl.program_id(0), pl.program_id(1), pl.program_id(2)
        first = jnp.logical_and(jnp.logical_and(i == 0, j == 0), l == 0)
        last = jnp.logical_and(jnp.logical_and(i == grid[0] - 1, j == grid[1] - 1), l == nk - 1)
        return first, last, last

    (out,), extra = _host_call(
        body, name, comm, when, [jax.ShapeDtypeStruct((m, n), out_dtype)], grid, specs,
        [pl.BlockSpec((tm, tn), lambda i, j, l: (i, j))], [pltpu.VMEM((tm, tn), F32)],
        ("parallel", "parallel", "arbitrary"), args)
    return out if comm is None else (out, extra)


def _rmsnorm_fwd(x, w, name):
    t, d = x.shape
    tr = _tile(t, 512)

    def body(x_ref, w_ref, o_ref):
        xv = x_ref[...]
        r = lax.rsqrt(jnp.mean(xv * xv, axis=1, keepdims=True) + EPS)
        o_ref[...] = (xv * r * w_ref[...]).astype(BF16)

    return pl.pallas_call(
        body, name=name,
        out_shape=jax.ShapeDtypeStruct((t, d), BF16),
        grid=(t // tr,),
        in_specs=[pl.BlockSpec((tr, d), lambda i: (i, 0)), pl.BlockSpec((1, d), lambda i: (0, 0))],
        out_specs=pl.BlockSpec((tr, d), lambda i: (i, 0)),
        compiler_params=_params(("parallel",)),
    )(x, w)


def _rmsnorm_bwd(dn, x, w, dres, name):
    t, d = x.shape
    tr = _tile(t, 512)

    def body(dn_ref, x_ref, w_ref, dres_ref, dx_ref, dw_ref):
        i = pl.program_id(0)
        xv = x_ref[...]
        g = dn_ref[...].astype(F32)
        r = lax.rsqrt(jnp.mean(xv * xv, axis=1, keepdims=True) + EPS)
        xh = xv * r
        dxh = g * w_ref[...]
        dx = r * (dxh - xh * jnp.mean(dxh * xh, axis=1, keepdims=True))
        dx_ref[...] = dres_ref[...] + dx

        @pl.when(i == 0)
        def _():
            dw_ref[...] = jnp.zeros_like(dw_ref)

        dw_ref[...] += jnp.sum(g * xh, axis=0, keepdims=True)

    return pl.pallas_call(
        body, name=name,
        out_shape=(jax.ShapeDtypeStruct((t, d), F32), jax.ShapeDtypeStruct((1, d), F32)),
        grid=(t // tr,),
        in_specs=[pl.BlockSpec((tr, d), lambda i: (i, 0)), pl.BlockSpec((tr, d), lambda i: (i, 0)),
                  pl.BlockSpec((1, d), lambda i: (0, 0)), pl.BlockSpec((tr, d), lambda i: (i, 0))],
        out_specs=(pl.BlockSpec((tr, d), lambda i: (i, 0)), pl.BlockSpec((1, d), lambda i: (0, 0))),
        compiler_params=_params(("arbitrary",)),
    )(dn, x, w, dres)


def _final_loss(x2, target, w, name):
    t, d = x2.shape
    tr = _tile(t, 512)

    def body(x_ref, t_ref, w_ref, dx_ref, dw_ref, loss_ref):
        i = pl.program_id(0)
        xv = x_ref[...]
        r = lax.rsqrt(jnp.mean(xv * xv, axis=1, keepdims=True) + EPS)
        xh = xv * r
        err = xh * w_ref[...] - t_ref[...]
        dy = err * (1.0 / d)
        dxh = dy * w_ref[...]
        dx_ref[...] = r * (dxh - xh * jnp.mean(dxh * xh, axis=1, keepdims=True))

        @pl.when(i == 0)
        def _():
            dw_ref[...] = jnp.zeros_like(dw_ref)
            loss_ref[...] = jnp.zeros_like(loss_ref)

        dw_ref[...] += jnp.sum(dy * xh, axis=0, keepdims=True)
        row = jnp.sum(err * err, axis=1, keepdims=True) * (0.5 / d)
        loss_ref[...] += jnp.sum(row, axis=0, keepdims=True)

    return pl.pallas_call(
        body, name=name,
        out_shape=(jax.ShapeDtypeStruct((t, d), F32), jax.ShapeDtypeStruct((1, d), F32),
                   jax.ShapeDtypeStruct((1, 1), F32)),
        grid=(t // tr,),
        in_specs=[pl.BlockSpec((tr, d), lambda i: (i, 0)), pl.BlockSpec((tr, d), lambda i: (i, 0)),
                  pl.BlockSpec((1, d), lambda i: (0, 0))],
        out_specs=(pl.BlockSpec((tr, d), lambda i: (i, 0)), pl.BlockSpec((1, d), lambda i: (0, 0)),
                   pl.BlockSpec((1, 1), lambda i: (0, 0))),
        compiler_params=_params(("arbitrary",)),
    )(x2, target, w)


def _shift_down(cur, prev, k, row):
    r = pltpu.roll(cur, k, 0)
    for m in range(k):
        r = jnp.where(row == m, prev[8 - k + m:8 - k + m + 1, :], r)
    return r


def _shift_up(cur, nxt, k, row, tr):
    r = pltpu.roll(cur, tr - k, 0)
    for m in range(k):
        r = jnp.where(row == tr - k + m, nxt[m:m + 1, :], r)
    return r


def _conv_taps(cur, prev, w, ntaps, row):
    taps = [cur if i == ntaps - 1 else _shift_down(cur, prev, ntaps - 1 - i, row) for i in range(ntaps)]
    y = w[0:1, :] * taps[0]
    for i in range(1, ntaps):
        y = y + w[i:i + 1, :] * taps[i]
    return taps, y


def _conv_bwd_data(parts, w, ntaps, out_dtype, name):
    t, chp = parts[0].shape
    npart = len(parts)
    tr, tc = _tile(t, 512), _tile(chp, 1408)
    nc = chp // tc
    nrow8 = t // 8
    last = t // tr - 1

    def body(*refs):
        cur_refs, nxt_refs = refs[:npart], refs[npart:2 * npart]
        w_ref, o_ref = refs[2 * npart], refs[2 * npart + 1]
        i, j = pl.program_id(0), pl.program_id(1)
        cur, nxt = cur_refs[0][...], nxt_refs[0][...]
        for p in range(1, npart):
            cur = jnp.where(j >= p * nc, cur_refs[p][...], cur)
            nxt = jnp.where(j >= p * nc, nxt_refs[p][...], nxt)
        nxt = jnp.where(i == last, 0.0, nxt)
        row = _iota(cur.shape, 0)
        wv = w_ref[...]
        y = wv[ntaps - 1:ntaps, :] * cur
        for k in range(1, ntaps):
            y = y + wv[ntaps - 1 - k:ntaps - k, :] * _shift_up(cur, nxt, k, row, tr)
        o_ref[...] = y.astype(out_dtype)

    col = lambda p: (lambda j: jnp.clip(j - p * nc, 0, nc - 1))
    cur_specs = [pl.BlockSpec((tr, tc), lambda i, j, c=col(p): (i, c(j))) for p in range(npart)]
    nxt_specs = [pl.BlockSpec((8, tc), lambda i, j, c=col(p): (jnp.minimum((i + 1) * (tr // 8), nrow8 - 1), c(j)))
                 for p in range(npart)]
    return pl.pallas_call(
        body, name=name,
        out_shape=jax.ShapeDtypeStruct((t, npart * chp), out_dtype),
        grid=(t // tr, npart * nc),
        in_specs=cur_specs + nxt_specs + [pl.BlockSpec((ntaps, tc), lambda i, j: (0, j))],
        out_specs=pl.BlockSpec((tr, tc), lambda i, j: (i, j)),
        compiler_params=_params(("parallel", "parallel")),
    )(*parts, *parts, w)


def _ffn_act_fwd(upre, cw, name):
    t = upre.shape[0]
    tr, tc = _tile(t, 512), _tile(D_FF, 1408)
    nj = D_FF // tc

    def body(g_ref, gp_ref, u_ref, up_ref, wg_ref, wu_ref, o_ref):
        i = pl.program_id(0)
        row = _iota((tr, tc), 0)
        gp = jnp.where(i == 0, 0.0, gp_ref[...])
        up = jnp.where(i == 0, 0.0, up_ref[...])
        _, gc = _conv_taps(g_ref[...], gp, wg_ref[...], FFN_CONV, row)
        _, uc = _conv_taps(u_ref[...], up, wu_ref[...], FFN_CONV, row)
        o_ref[...] = (gc * _sigmoid(gc) * uc).astype(BF16)

    prev = lambda off: (lambda i, j: (jnp.maximum(i * (tr // 8) - 1, 0), j + off))
    return pl.pallas_call(
        body, name=name,
        out_shape=jax.ShapeDtypeStruct((t, D_FF), BF16),
        grid=(t // tr, nj),
        in_specs=[pl.BlockSpec((tr, tc), lambda i, j: (i, j)), pl.BlockSpec((8, tc), prev(0)),
                  pl.BlockSpec((tr, tc), lambda i, j: (i, j + nj)), pl.BlockSpec((8, tc), prev(nj)),
                  pl.BlockSpec((FFN_CONV, tc), lambda i, j: (0, j)),
                  pl.BlockSpec((FFN_CONV, tc), lambda i, j: (0, j + nj))],
        out_specs=pl.BlockSpec((tr, tc), lambda i, j: (i, j)),
        compiler_params=_params(("parallel", "parallel")),
    )(upre, upre, upre, upre, cw, cw)


def _ffn_act_bwd(dact, upre, cw, name):
    t = upre.shape[0]
    tr, tc = _tile(t, 256), _tile(D_FF, 1408)
    nj = D_FF // tc

    def body(da_ref, g_ref, gp_ref, u_ref, up_ref, wg_ref, wu_ref, dg_ref, du_ref, dwg_ref, dwu_ref):
        i = pl.program_id(1)
        row = _iota((tr, tc), 0)
        gp = jnp.where(i == 0, 0.0, gp_ref[...])
        up = jnp.where(i == 0, 0.0, up_ref[...])
        gt, gc = _conv_taps(g_ref[...], gp, wg_ref[...], FFN_CONV, row)
        ut, uc = _conv_taps(u_ref[...], up, wu_ref[...], FFN_CONV, row)
        da = da_ref[...].astype(F32)
        sg = _sigmoid(gc)
        dgc = da * uc * (sg * (1.0 + gc * (1.0 - sg)))
        duc = da * (gc * sg)
        dg_ref[...] = dgc
        du_ref[...] = duc

        @pl.when(i == 0)
        def _():
            dwg_ref[...] = jnp.zeros_like(dwg_ref)
            dwu_ref[...] = jnp.zeros_like(dwu_ref)

        for k in range(FFN_CONV):
            dwg_ref[k:k + 1, :] += jnp.sum(dgc * gt[k], axis=0, keepdims=True)
            dwu_ref[k:k + 1, :] += jnp.sum(duc * ut[k], axis=0, keepdims=True)

    prev = lambda off: (lambda j, i: (jnp.maximum(i * (tr // 8) - 1, 0), j + off))
    blk = lambda off: pl.BlockSpec((tr, tc), lambda j, i: (i, j + off))
    wblk = lambda off: pl.BlockSpec((FFN_CONV, tc), lambda j, i: (0, j + off))
    dgc, duc, dwg, dwu = pl.pallas_call(
        body, name=name,
        out_shape=(jax.ShapeDtypeStruct((t, D_FF), F32), jax.ShapeDtypeStruct((t, D_FF), F32),
                   jax.ShapeDtypeStruct((FFN_CONV, D_FF), F32), jax.ShapeDtypeStruct((FFN_CONV, D_FF), F32)),
        grid=(nj, t // tr),
        in_specs=[blk(0), blk(0), pl.BlockSpec((8, tc), prev(0)), blk(nj), pl.BlockSpec((8, tc), prev(nj)),
                  wblk(0), wblk(nj)],
        out_specs=(blk(0), blk(0), wblk(0), wblk(0)),
        compiler_params=_params(("parallel", "arbitrary")),
    )(dact, upre, upre, upre, upre, cw, cw)
    return dgc, duc, dwg, dwu


def _dn_pre_fwd(qkv_pre, cw, name):
    t = qkv_pre.shape[0]
    tr = _tile(t, 512)
    scale = HEAD_DIM ** -0.5

    def body(x_ref, p_ref, w_ref, o_ref):
        i, j = pl.program_id(0), pl.program_id(1)
        row = _iota((tr, WIDTH), 0)
        prev = jnp.where(i == 0, 0.0, p_ref[...])
        _, c = _conv_taps(x_ref[...], prev, w_ref[...], DN_CONV, row)
        s = c * _sigmoid(c)
        for h in range(HEADS):
            sl = slice(h * HEAD_DIM, (h + 1) * HEAD_DIM)
            sh = s[:, sl]
            r = lax.rsqrt(jnp.sum(sh * sh, axis=1, keepdims=True) + EPS)
            o_ref[:, sl] = sh * jnp.where(j == 0, r * scale, jnp.where(j == 1, r, 1.0))

    return pl.pallas_call(
        body, name=name,
        out_shape=jax.ShapeDtypeStruct((t, 3 * WIDTH), F32),
        grid=(t // tr, 3),
        in_specs=[pl.BlockSpec((tr, WIDTH), lambda i, j: (i, j)),
                  pl.BlockSpec((8, WIDTH), lambda i, j: (jnp.maximum(i * (tr // 8) - 1, 0), j)),
                  pl.BlockSpec((DN_CONV, WIDTH), lambda i, j: (0, j))],
        out_specs=pl.BlockSpec((tr, WIDTH), lambda i, j: (i, j)),
        compiler_params=_params(("parallel", "parallel")),
    )(qkv_pre, qkv_pre, cw)


def _dn_pre_bwd(dq, dk, dv, qkv_pre, cw, name):
    t = qkv_pre.shape[0]
    tr = _tile(t, 256)
    scale = HEAD_DIM ** -0.5

    def body(dq_ref, dk_ref, dv_ref, x_ref, p_ref, w_ref, dc_ref, dw_ref):
        j, i = pl.program_id(0), pl.program_id(1)
        row = _iota((tr, WIDTH), 0)
        prev = jnp.where(i == 0, 0.0, p_ref[...])
        taps, c = _conv_taps(x_ref[...], prev, w_ref[...], DN_CONV, row)
        d = jnp.where(j == 0, dq_ref[...] * scale, jnp.where(j == 1, dk_ref[...], dv_ref[...]))
        sg = _sigmoid(c)
        s = c * sg
        dsilu = sg * (1.0 + c * (1.0 - sg))
        for h in range(HEADS):
            sl = slice(h * HEAD_DIM, (h + 1) * HEAD_DIM)
            sh, dh = s[:, sl], d[:, sl]
            r = lax.rsqrt(jnp.sum(sh * sh, axis=1, keepdims=True) + EPS)
            nh = sh * r
            ds_norm = r * (dh - nh * jnp.sum(nh * dh, axis=1, keepdims=True))
            dc_ref[:, sl] = jnp.where(j < 2, ds_norm, dh) * dsilu[:, sl]

        @pl.when(i == 0)
        def _():
            dw_ref[...] = jnp.zeros_like(dw_ref)

        dc = dc_ref[...]
        for k in range(DN_CONV):
            dw_ref[k:k + 1, :] += jnp.sum(dc * taps[k], axis=0, keepdims=True)

    dspec = lambda p: pl.BlockSpec((tr, WIDTH), lambda j, i: (jnp.where(j == p, i, 0), 0))
    return pl.pallas_call(
        body, name=name,
        out_shape=(jax.ShapeDtypeStruct((t, 3 * WIDTH), F32), jax.ShapeDtypeStruct((DN_CONV, 3 * WIDTH), F32)),
        grid=(3, t // tr),
        in_specs=[dspec(0), dspec(1), dspec(2),
                  pl.BlockSpec((tr, WIDTH), lambda j, i: (i, j)),
                  pl.BlockSpec((8, WIDTH), lambda j, i: (jnp.maximum(i * (tr // 8) - 1, 0), j)),
                  pl.BlockSpec((DN_CONV, WIDTH), lambda j, i: (0, j))],
        out_specs=(pl.BlockSpec((tr, WIDTH), lambda j, i: (i, j)),
                   pl.BlockSpec((DN_CONV, WIDTH), lambda j, i: (0, j))),
        compiler_params=_params(("parallel", "arbitrary")),
    )(dq, dk, dv, qkv_pre, qkv_pre, cw)


def _tri(n, kind):
    r, c = _iota((n, n), 0), _iota((n, n), 1)
    m = {"lower": r >= c, "strict": r > c, "upper": r <= c}[kind]
    return m


def _dn_gates_fwd(hab, alog, dtb, name):
    t = hab.shape[0]
    cc = DN_CHUNK

    def body(h_ref, al_ref, dt_ref, o_ref):
        hv = h_ref[...]
        lane = _iota(hv.shape, 1)
        xa = hv + dt_ref[...]
        sp = jnp.maximum(xa, 0.0) + _log1pexp_neg_abs(xa)
        g = jnp.where(lane < HEADS, -jnp.exp(al_ref[...]) * sp, 0.0)
        tril = jnp.where(_tri(cc, "lower"), 1.0, 0.0).astype(BF16)
        gc = _dot_xl(tril, g, NN)
        o_ref[...] = jnp.where(lane < HEADS, gc, jnp.where(lane < 2 * HEADS, _sigmoid(hv), 0.0))

    return pl.pallas_call(
        body, name=name,
        out_shape=jax.ShapeDtypeStruct((t, 128), F32),
        grid=(t // cc,),
        in_specs=[pl.BlockSpec((cc, 128), lambda i: (i, 0)), pl.BlockSpec((1, 128), lambda i: (0, 0)),
                  pl.BlockSpec((1, 128), lambda i: (0, 0))],
        out_specs=pl.BlockSpec((cc, 128), lambda i: (i, 0)),
        compiler_params=_params(("parallel",)),
    )(hab, alog, dtb)


def _dn_gates_bwd(dgates, hab, alog, dtb, name):
    t = hab.shape[0]
    cc = DN_CHUNK

    def body(d_ref, h_ref, al_ref, dt_ref, o_ref, dal_ref, ddt_ref):
        i = pl.program_id(0)
        hv = h_ref[...]
        dv = d_ref[...]
        lane = _iota(hv.shape, 1)
        triu = jnp.where(_tri(cc, "upper"), 1.0, 0.0).astype(BF16)
        dg = _dot_xl(triu, jnp.where(lane < HEADS, dv, 0.0), NN)
        xa = hv + dt_ref[...]
        sp = jnp.maximum(xa, 0.0) + _log1pexp_neg_abs(xa)
        ea = jnp.exp(al_ref[...])
        da = jnp.where(lane < HEADS, dg * (-ea) * _sigmoid(xa), 0.0)
        be = _sigmoid(hv)
        db = dv * be * (1.0 - be)
        o_ref[...] = jnp.where(lane < HEADS, da, jnp.where(lane < 2 * HEADS, db, 0.0))

        @pl.when(i == 0)
        def _():
            dal_ref[...] = jnp.zeros_like(dal_ref)
            ddt_ref[...] = jnp.zeros_like(ddt_ref)

        dal_ref[...] += jnp.sum(jnp.where(lane < HEADS, dg * (-ea) * sp, 0.0), axis=0, keepdims=True)
        ddt_ref[...] += jnp.sum(da, axis=0, keepdims=True)

    return pl.pallas_call(
        body, name=name,
        out_shape=(jax.ShapeDtypeStruct((t, 128), F32), jax.ShapeDtypeStruct((1, 128), F32),
                   jax.ShapeDtypeStruct((1, 128), F32)),
        grid=(t // cc,),
        in_specs=[pl.BlockSpec((cc, 128), lambda i: (i, 0)), pl.BlockSpec((cc, 128), lambda i: (i, 0)),
                  pl.BlockSpec((1, 128), lambda i: (0, 0)), pl.BlockSpec((1, 128), lambda i: (0, 0))],
        out_specs=(pl.BlockSpec((cc, 128), lambda i: (i, 0)), pl.BlockSpec((1, 128), lambda i: (0, 0)),
                   pl.BlockSpec((1, 128), lambda i: (0, 0))),
        compiler_params=_params(("arbitrary",)),
    )(dgates, hab, alog, dtb)


def _dn_chunk_common(gates, h):
    cc = DN_CHUNK
    lane = _iota(gates.shape, 1)
    gh = jnp.where(lane == h, gates, 0.0)
    gc_col = jnp.sum(gh, axis=1, keepdims=True)
    gc_row = _dot_xl(jnp.ones((cc, 128), BF16), gh, NT)
    beta = jnp.sum(jnp.where(lane == h + HEADS, gates, 0.0), axis=1, keepdims=True)
    lower = _tri(cc, "lower")
    decay = jnp.where(lower, jnp.exp(jnp.where(lower, gc_col - gc_row, 0.0)), 0.0)
    gc_last = gc_col[cc - 1:cc, :]
    return gc_col, gc_last, beta, decay


def _dn_local_fwd(act, gates, name):
    t = act.shape[0]
    cc = DN_CHUNK
    nc = t // cc

    def body(q_ref, k_ref, v_ref, g_ref, u_ref, w_ref, kd_ref, qg_ref, ti_ref, p_ref):
        gates = g_ref[...]
        eye = jnp.where(_iota((cc, cc), 0) == _iota((cc, cc), 1), 1.0, 0.0)
        hs = range(HEADS)
        sl = [slice(h * HEAD_DIM, (h + 1) * HEAD_DIM) for h in hs]
        q, k, v = ([r[:, s] for s in sl] for r in (q_ref, k_ref, v_ref))
        gc_col, gc_last, beta, decay = zip(*[_dn_chunk_common(gates, h) for h in hs])
        gam = [jnp.exp(g) for g in gc_col]
        kb = [k[h] * beta[h] for h in hs]
        npow = [-jnp.where(_tri(cc, "strict"), _dotb(kb[h], k[h], NT) * decay[h], 0.0) for h in hs]
        tinv = [eye + n for n in npow]
        for _ in range(5):
            npow = [_dot3(n, n, NN) for n in npow]
            tinv = [t + _dot3(t, n, NN) for t, n in zip(tinv, npow)]
        uu = [_dot3(tinv[h], v[h] * beta[h], NN) for h in hs]
        ww = [_dot3(tinv[h], kb[h] * gam[h], NN) for h in hs]
        pp = [jnp.where(_tri(cc, "lower"), _dotb(q[h], k[h], NT) * decay[h], 0.0) for h in hs]
        for h in hs:
            u_ref[:, sl[h]] = uu[h]
            w_ref[:, sl[h]] = ww[h]
            kd_ref[:, sl[h]] = k[h] * jnp.exp(gc_last[h] - gc_col[h])
            qg_ref[:, sl[h]] = q[h] * gam[h]
            ti_ref[h] = tinv[h]
            p_ref[h] = pp[h]

    row = lambda off: pl.BlockSpec((cc, WIDTH), lambda n: (n, off))
    mat = pl.BlockSpec((HEADS, cc, cc), lambda n: (0, n, 0))
    tw = jax.ShapeDtypeStruct((t, WIDTH), F32)
    hm = jax.ShapeDtypeStruct((HEADS, t, cc), F32)
    return pl.pallas_call(
        body, name=name,
        out_shape=(tw, tw, tw, tw, hm, hm),
        grid=(nc,),
        in_specs=[row(0), row(1), row(2), pl.BlockSpec((cc, 128), lambda n: (n, 0))],
        out_specs=(row(0), row(0), row(0), row(0), mat, mat),
        compiler_params=_params(("parallel",)),
    )(act, act, act, gates)


def _dn_scan_fwd(u, w, kd, qg, p, gates, name):
    t = u.shape[0]
    cc = DN_CHUNK
    nc = t // cc

    def body(u_ref, w_ref, kd_ref, qg_ref, p_ref, g_ref, o_ref, sh_ref, s_ref):
        n = pl.program_id(0)

        @pl.when(n == 0)
        def _():
            s_ref[...] = jnp.zeros_like(s_ref)

        glast = jnp.exp(g_ref[cc - 1:cc, :])
        hs = range(HEADS)
        sl = [slice(h * HEAD_DIM, (h + 1) * HEAD_DIM) for h in hs]
        s = [s_ref[h] for h in hs]
        sb = [a.astype(BF16) for a in s]
        vn = [u_ref[:, sl[h]] - _dot(w_ref[:, sl[h]].astype(BF16), sb[h], NN) for h in hs]
        vnb = [a.astype(BF16) for a in vn]
        o_state = [_dot(qg_ref[:, sl[h]].astype(BF16), sb[h], NN) for h in hs]
        o_local = [_dot(p_ref[h].astype(BF16), vnb[h], NN) for h in hs]
        s_add = [_dot(kd_ref[:, sl[h]].astype(BF16), vnb[h], TN) for h in hs]
        for h in hs:
            o_ref[:, sl[h]] = o_state[h] + o_local[h]
            sh_ref[0, h] = s[h]
            s_ref[h] = glast[:, h:h + 1] * s[h] + s_add[h]

    row = pl.BlockSpec((cc, WIDTH), lambda n: (n, 0))
    return pl.pallas_call(
        body, name=name,
        out_shape=(jax.ShapeDtypeStruct((t, WIDTH), F32),
                   jax.ShapeDtypeStruct((nc, HEADS, HEAD_DIM, HEAD_DIM), F32)),
        grid=(nc,),
        in_specs=[row, row, row, row, pl.BlockSpec((HEADS, cc, cc), lambda n: (0, n, 0)),
                  pl.BlockSpec((cc, 128), lambda n: (n, 0))],
        out_specs=(row, pl.BlockSpec((1, HEADS, HEAD_DIM, HEAD_DIM), lambda n: (n, 0, 0, 0))),
        scratch_shapes=[pltpu.VMEM((HEADS, HEAD_DIM, HEAD_DIM), F32)],
        compiler_params=_params(("arbitrary",)),
    )(u, w, kd, qg, p, gates)


def _dn_scan_bwd(do, w, kd, qg, p, gates, name):
    t = do.shape[0]
    cc = DN_CHUNK
    nc = t // cc

    def body(do_ref, w_ref, kd_ref, qg_ref, p_ref, g_ref, dvn_ref, dsh_ref, ds_ref):
        n = pl.program_id(0)

        @pl.when(n == 0)
        def _():
            ds_ref[...] = jnp.zeros_like(ds_ref)

        glast = jnp.exp(g_ref[cc - 1:cc, :])
        hs = range(HEADS)
        sl = [slice(h * HEAD_DIM, (h + 1) * HEAD_DIM) for h in hs]
        ds = [ds_ref[h] for h in hs]
        dob = [do_ref[:, sl[h]].astype(BF16) for h in hs]
        dvn = [_dot(p_ref[h].astype(BF16), dob[h], TN) + _dot(kd_ref[:, sl[h]].astype(BF16), ds[h].astype(BF16), NN)
               for h in hs]
        ds_q = [_dot(qg_ref[:, sl[h]].astype(BF16), dob[h], TN) for h in hs]
        ds_w = [_dot(w_ref[:, sl[h]].astype(BF16), dvn[h].astype(BF16), TN) for h in hs]
        for h in hs:
            dvn_ref[:, sl[h]] = dvn[h]
            dsh_ref[0, h] = ds[h]
            ds_ref[h] = ds_q[h] + glast[:, h:h + 1] * ds[h] - ds_w[h]

    row = pl.BlockSpec((cc, WIDTH), lambda n: (nc - 1 - n, 0))
    return pl.pallas_call(
        body, name=name,
        out_shape=(jax.ShapeDtypeStruct((t, WIDTH), F32),
                   jax.ShapeDtypeStruct((nc, HEADS, HEAD_DIM, HEAD_DIM), F32)),
        grid=(nc,),
        in_specs=[row, row, row, row, pl.BlockSpec((HEADS, cc, cc), lambda n: (0, nc - 1 - n, 0)),
                  pl.BlockSpec((cc, 128), lambda n: (nc - 1 - n, 0))],
        out_specs=(row, pl.BlockSpec((1, HEADS, HEAD_DIM, HEAD_DIM), lambda n: (nc - 1 - n, 0, 0, 0))),
        scratch_shapes=[pltpu.VMEM((HEADS, HEAD_DIM, HEAD_DIM), F32)],
        compiler_params=_params(("arbitrary",)),
    )(do, w, kd, qg, p, gates)


def _dn_local_bwd(act, gates, u, w, kd, qg, tinv, p, sh, dsh, dvn, do, name):
    t = act.shape[0]
    cc = DN_CHUNK
    nc = t // cc

    def body(q_ref, k_ref, v_ref, g_ref, u_ref, w_ref, kd_ref, qg_ref, ti_ref, p_ref, s_ref, ds_ref,
             dvn_ref, do_ref, dq_ref, dk_ref, dv_ref, dg_ref):
        gates_v = g_ref[...]
        lower, strict = _tri(cc, "lower"), _tri(cc, "strict")
        ones = jnp.ones((cc, 128), BF16)
        rowc = _iota((cc, 1), 0)
        lane = _iota((cc, 128), 1)
        hs = range(HEADS)
        sl = [slice(h * HEAD_DIM, (h + 1) * HEAD_DIM) for h in hs]
        q, k, v, uu, ww, kd, qg, dvn, do = ([r[:, s] for s in sl] for r in (
            q_ref, k_ref, v_ref, u_ref, w_ref, kd_ref, qg_ref, dvn_ref, do_ref))
        gc_col, gc_last, beta, decay = zip(*[_dn_chunk_common(gates_v, h) for h in hs])
        gam = [jnp.exp(g) for g in gc_col]
        kb = [k[h] * beta[h] for h in hs]
        s_in = [s_ref[0, h] for h in hs]
        ds_out = [ds_ref[0, h] for h in hs]
        tinv = [ti_ref[h] for h in hs]

        a = [jnp.where(strict, _dotb(kb[h], k[h], NT) * decay[h], 0.0) for h in hs]
        vn = [uu[h] - _dotb(ww[h], s_in[h], NN) for h in hs]
        dqg = [_dotb(do[h], s_in[h], NT) for h in hs]
        dw = [-_dotb(dvn[h], s_in[h], NT) for h in hs]
        dp = [jnp.where(lower, _dotb(do[h], vn[h], NT), 0.0) for h in hs]
        dkd = [_dotb(vn[h], ds_out[h], NT) for h in hs]
        dru = [_dot3(tinv[h], dvn[h], TN) for h in hs]
        drw = [_dot3(tinv[h], dw[h], TN) for h in hs]
        da = [-jnp.where(strict, _dotb(dru[h], uu[h], NT) + _dotb(drw[h], ww[h], NT), 0.0) for h in hs]
        dad = [da[h] * decay[h] for h in hs]
        dpd = [dp[h] * decay[h] for h in hs]
        dkb = [_dotb(dad[h], k[h], NN) + gam[h] * drw[h] for h in hs]
        dk = [_dotb(dad[h], kb[h], TN) + _dotb(dpd[h], q[h], TN) + beta[h] * dkb[h]
              + jnp.exp(gc_last[h] - gc_col[h]) * dkd[h] for h in hs]
        dq = [gam[h] * dqg[h] + _dotb(dpd[h], k[h], NN) for h in hs]
        gm = [da[h] * a[h] + dp[h] * p_ref[h] for h in hs]
        colsum = [_dot_xr(gm[h], ones, TN)[:, 0:1] for h in hs]

        dgates = jnp.zeros((cc, 128), F32)
        for h in hs:
            dk_ref[:, sl[h]] = dk[h]
            dq_ref[:, sl[h]] = dq[h]
            dv_ref[:, sl[h]] = beta[h] * dru[h]
            dbeta = (jnp.sum(dkb[h] * k[h], axis=1, keepdims=True)
                     + jnp.sum(dru[h] * v[h], axis=1, keepdims=True))
            rkd = jnp.sum(dkd[h] * kd[h], axis=1, keepdims=True)
            dgc = (jnp.sum(gm[h], axis=1, keepdims=True) - colsum[h]
                   + jnp.sum(dqg[h] * qg[h], axis=1, keepdims=True)
                   + jnp.sum(drw[h] * kb[h], axis=1, keepdims=True) * gam[h] - rkd)
            tail = jnp.sum(rkd, axis=0, keepdims=True) + jnp.exp(gc_last[h]) * jnp.sum(
                jnp.sum(s_in[h] * ds_out[h], axis=1, keepdims=True), axis=0, keepdims=True)
            dgc = dgc + jnp.where(rowc == cc - 1, tail, 0.0)
            dgates = dgates + jnp.where(lane == h, dgc, 0.0) + jnp.where(lane == h + HEADS, dbeta, 0.0)
        dg_ref[...] = dgates

    row = lambda off: pl.BlockSpec((cc, WIDTH), lambda n: (n, off))
    mat = pl.BlockSpec((HEADS, cc, cc), lambda n: (0, n, 0))
    st = pl.BlockSpec((1, HEADS, HEAD_DIM, HEAD_DIM), lambda n: (n, 0, 0, 0))
    gl = pl.BlockSpec((cc, 128), lambda n: (n, 0))
    tw = jax.ShapeDtypeStruct((t, WIDTH), F32)
    return pl.pallas_call(
        body, name=name,
        out_shape=(tw, tw, tw, jax.ShapeDtypeStruct((t, 128), F32)),
        grid=(nc,),
        in_specs=[row(0), row(1), row(2), gl, row(0), row(0), row(0), row(0), mat, mat, st, st, row(0), row(0)],
        out_specs=(row(0), row(0), row(0), gl),
        compiler_params=_params(("parallel",)),
    )(act, act, act, gates, u, w, kd, qg, tinv, p, sh, dsh, dvn, do)


def _dn_post_fwd(o, gate, w, name):
    t = o.shape[0]
    tr = _tile(t, 512)

    def body(o_ref, g_ref, w_ref, y_ref):
        for h in range(HEADS):
            sl = slice(h * HEAD_DIM, (h + 1) * HEAD_DIM)
            ov, gv = o_ref[:, sl], g_ref[:, sl]
            r = lax.rsqrt(jnp.mean(ov * ov, axis=1, keepdims=True) + EPS)
            y_ref[:, sl] = (ov * r * w_ref[...] * (gv * _sigmoid(gv))).astype(BF16)

    blk = pl.BlockSpec((tr, WIDTH), lambda i: (i, 0))
    return pl.pallas_call(
        body, name=name,
        out_shape=jax.ShapeDtypeStruct((t, WIDTH), BF16),
        grid=(t // tr,),
        in_specs=[blk, blk, pl.BlockSpec((1, HEAD_DIM), lambda i: (0, 0))],
        out_specs=blk,
        compiler_params=_params(("parallel",)),
    )(o, gate, w)


def _dn_post_bwd(dy, o, gate, w, name):
    t = o.shape[0]
    tr = _tile(t, 512)

    def body(dy_ref, o_ref, g_ref, w_ref, do_ref, dg_ref, dw_ref):
        i = pl.program_id(0)

        @pl.when(i == 0)
        def _():
            dw_ref[...] = jnp.zeros_like(dw_ref)

        dw = jnp.zeros((1, HEAD_DIM), F32)
        for h in range(HEADS):
            sl = slice(h * HEAD_DIM, (h + 1) * HEAD_DIM)
            ov, gv, dyv = o_ref[:, sl], g_ref[:, sl], dy_ref[:, sl].astype(F32)
            r = lax.rsqrt(jnp.mean(ov * ov, axis=1, keepdims=True) + EPS)
            oh = ov * r
            sg = _sigmoid(gv)
            dg_ref[:, sl] = (dyv * oh * w_ref[...] * (sg * (1.0 + gv * (1.0 - sg)))).astype(BF16)
            dn = dyv * (gv * sg)
            doh = dn * w_ref[...]
            do_ref[:, sl] = r * (doh - oh * jnp.mean(doh * oh, axis=1, keepdims=True))
            dw = dw + jnp.sum(dn * oh, axis=0, keepdims=True)
        dw_ref[...] += dw

    blk = pl.BlockSpec((tr, WIDTH), lambda i: (i, 0))
    return pl.pallas_call(
        body, name=name,
        out_shape=(jax.ShapeDtypeStruct((t, WIDTH), F32), jax.ShapeDtypeStruct((t, WIDTH), BF16),
                   jax.ShapeDtypeStruct((1, HEAD_DIM), F32)),
        grid=(t // tr,),
        in_specs=[blk, blk, blk, pl.BlockSpec((1, HEAD_DIM), lambda i: (0, 0))],
        out_specs=(blk, blk, pl.BlockSpec((1, HEAD_DIM), lambda i: (0, 0))),
        compiler_params=_params(("arbitrary",)),
    )(dy, o, gate, w)


def _sb_scores(q, ks, qi, j, carry_b, uincl):
    bk = ATT_BLOCK
    scale = HEAD_DIM ** -0.5
    z = _dot(q, ks, NT) * scale
    qpos = qi * bk + _iota(z.shape, 0)
    kpos = j * bk + _iota(z.shape, 1)
    mask = kpos < qpos
    soft = _log1pexp_neg_abs(z)
    lk_full = -(jnp.maximum(z, 0.0) + soft)
    lk = jnp.where(mask, lk_full, 0.0)
    ls = jnp.minimum(z, 0.0) - soft
    incl = _dot_xr2(lk, uincl, NN)
    a = jnp.where(mask, jnp.exp(ls + (carry_b + incl - lk)), 0.0)
    return a, mask, lk_full, ls, carry_b + incl[:, 0:1]


def _sb_more(qi, carry):
    it, cb = carry[0], carry[1]
    return jnp.logical_and(it <= qi, jnp.max(cb) > SB_LOG_ZERO)


def _sb_steps(nq):
    def when():
        h, i = pl.program_id(0), pl.program_id(1)
        return (jnp.logical_and(h == 0, i == 0), jnp.logical_and(h == HEADS // 2, i == 0),
                jnp.logical_and(h == HEADS - 1, i == nq - 1))
    return when


def _sb_fwd(qkv, name, comm=None):
    t = qkv.shape[0]
    bk = ATT_BLOCK

    def body(q_ref, k_ref, v_ref, o_ref):
        qi = pl.program_id(1)
        q = q_ref[...]
        uincl = jnp.where(_tri(bk, "lower"), 1.0, 0.0).astype(BF16)

        def step(carry):
            it, cb, acc = carry
            j = qi - it
            rows = pl.ds(pl.multiple_of(j * bk, bk), bk)
            a, _, _, _, cb = _sb_scores(q, k_ref[rows, :], qi, j, cb, uincl)
            acc = acc + _dot(a.astype(BF16), v_ref[rows, :], NN)
            return it + 1, cb, acc

        init = (jnp.int32(0), jnp.zeros((bk, 1), F32), jnp.zeros((bk, HEAD_DIM), F32))
        _, _, acc = lax.while_loop(functools.partial(_sb_more, qi), step, init)
        o_ref[...] = acc

    (o,), extra = _host_call(
        body, name, comm, _sb_steps(t // bk), [jax.ShapeDtypeStruct((t, WIDTH), F32)], (HEADS, t // bk),
        [pl.BlockSpec((bk, HEAD_DIM), lambda h, i: (i, h)),
         pl.BlockSpec((t, HEAD_DIM), lambda h, i: (0, HEADS + h)),
         pl.BlockSpec((t, HEAD_DIM), lambda h, i: (0, 2 * HEADS + h))],
        [pl.BlockSpec((bk, HEAD_DIM), lambda h, i: (i, h))], [], ("parallel", "arbitrary"), (qkv, qkv, qkv))
    return o, extra


def _sb_bwd(qkv, o, do, name, comm=None):
    t = qkv.shape[0]
    bk = ATT_BLOCK
    scale = HEAD_DIM ** -0.5

    def body(q_ref, k_ref, v_ref, o_ref, do_ref, dq_ref, dk_ref, dv_ref):
        qi = pl.program_id(1)

        @pl.when(qi == 0)
        def _():
            dk_ref[...] = jnp.zeros_like(dk_ref)
            dv_ref[...] = jnp.zeros_like(dv_ref)

        q = q_ref[...]
        dov = do_ref[...]
        dob = dov.astype(BF16)
        do1, do2 = _split2(dov)
        dsum = jnp.sum(dov * o_ref[...], axis=1, keepdims=True)
        uincl = jnp.where(_tri(bk, "lower"), 1.0, 0.0).astype(BF16)

        def step(carry):
            it, cb, ce, dq = carry
            j = qi - it
            rows = pl.ds(pl.multiple_of(j * bk, bk), bk)
            ks = k_ref[rows, :]
            a, mask, lk_full, ls, cb = _sb_scores(q, ks, qi, j, cb, uincl)
            ab = a.astype(BF16)
            vs = v_ref[rows, :]
            dla = ab.astype(F32) * (_dot(do1, vs, NT) + _dot(do2, vs, NT))
            suf = _dot_xr2(dla, uincl, NN)
            e = dsum - (ce + suf)
            dz = jnp.where(mask, dla * jnp.exp(lk_full) - e * jnp.exp(ls), 0.0)
            dzb = (dz * scale).astype(BF16)
            dq = dq + _dot(dzb, ks, NN)
            dk_ref[rows, :] += _dot(dzb, q, TN)
            dv_ref[rows, :] += _dot(ab, dob, TN)
            return it + 1, cb, ce + suf[:, 0:1], dq

        zc = jnp.zeros((bk, 1), F32)
        init = (jnp.int32(0), zc, zc, jnp.zeros((bk, HEAD_DIM), F32))
        dq_ref[...] = lax.while_loop(functools.partial(_sb_more, qi), step, init)[3]

    tw = jax.ShapeDtypeStruct((t, WIDTH), F32)
    qb = pl.BlockSpec((bk, HEAD_DIM), lambda h, i: (i, h))
    full = lambda off: pl.BlockSpec((t, HEAD_DIM), lambda h, i: (0, off + h))
    return _host_call(
        body, name, comm, _sb_steps(t // bk), [tw, tw, tw], (HEADS, t // bk),
        [qb, full(HEADS), full(2 * HEADS), qb, qb], [qb, full(0), full(0)], [], ("parallel", "arbitrary"),
        (qkv, qkv, qkv, o, do))


def _merge_fwd(pd, ps, gl, name):
    t = pd.shape[0]
    tr, tc = _tile(t, 512), 512
    nj = D_MODEL // tc

    def body(pd_ref, ps_ref, gd_ref, gs_ref, o_ref):
        o_ref[...] = (_sigmoid(gd_ref[...]) * pd_ref[...] + _sigmoid(gs_ref[...]) * ps_ref[...]).astype(BF16)

    blk = lambda off: pl.BlockSpec((tr, tc), lambda i, j: (i, j + off))
    return pl.pallas_call(
        body, name=name,
        out_shape=jax.ShapeDtypeStruct((t, D_MODEL), BF16),
        grid=(t // tr, nj),
        in_specs=[blk(0), blk(0), blk(0), blk(nj)],
        out_specs=blk(0),
        compiler_params=_params(("parallel", "parallel")),
    )(pd, ps, gl, gl)


def _merge_bwd(dm, pd, ps, gl, name):
    t = pd.shape[0]
    tr, tc = _tile(t, 512), 512
    nj = D_MODEL // tc

    def body(dm_ref, pd_ref, ps_ref, gd_ref, gs_ref, dpd_ref, dps_ref, dgd_ref, dgs_ref):
        dmv = dm_ref[...]
        sd, ss = _sigmoid(gd_ref[...]), _sigmoid(gs_ref[...])
        dpd_ref[...] = (dmv * sd).astype(BF16)
        dps_ref[...] = (dmv * ss).astype(BF16)
        dgd_ref[...] = (dmv * pd_ref[...] * sd * (1.0 - sd)).astype(BF16)
        dgs_ref[...] = (dmv * ps_ref[...] * ss * (1.0 - ss)).astype(BF16)

    blk = lambda off: pl.BlockSpec((tr, tc), lambda i, j: (i, j + off))
    out = jax.ShapeDtypeStruct((t, D_MODEL), BF16)
    return pl.pallas_call(
        body, name=name,
        out_shape=(out, out, out, out),
        grid=(t // tr, nj),
        in_specs=[blk(0), blk(0), blk(0), blk(0), blk(nj)],
        out_specs=(blk(0), blk(0), blk(0), blk(0)),
        compiler_params=_params(("parallel", "parallel")),
    )(dm, pd, ps, gl, gl)


def _local_step(x, target, wts, plan=None):
    n1 = _rmsnorm_fwd(x, wts["norm1_w"], "norm1_fwd")
    qkv_pre = _matmul(n1, wts["w_dnqkv_t"], "nt", F32, "in_dnqkv")
    hgate = _matmul(n1, wts["w_dngate_t"], "nt", F32, "in_dngate")
    sbqkv = _matmul(n1, wts["w_sbqkv_t"], "nt", BF16, "in_sbqkv")
    gl = _matmul(n1, wts["w_gl_t"], "nt", F32, "in_gl")
    hab = _matmul(n1, wts["w_ab_t"], "nt", F32, "in_ab")

    act = _dn_pre_fwd(qkv_pre, wts["dn_conv_w"], "dn_pre_fwd")
    gates = _dn_gates_fwd(hab, wts["alog"], wts["dtb"], "dn_gates_fwd")
    u, w, kd, qg, tinv, p = _dn_local_fwd(act, gates, "dn_local_fwd")
    o_dn, sh = _dn_scan_fwd(u, w, kd, qg, p, gates, "dn_scan_fwd")
    y_dn = _dn_post_fwd(o_dn, hgate, wts["dn_norm_w"], "dn_post_fwd")

    o_sb, late = _sb_fwd(sbqkv, "sb_fwd", comm=plan.late_gather() if plan else None)
    if plan:
        wts = {**wts, **plan.late_weights(late)}

    pd = _matmul(y_dn, wts["w_proj_dn"], "nn", F32, "proj_dn")
    ps = _matmul(o_sb, wts["w_proj_sb"], "nn", F32, "proj_sb")
    mixed = _merge_fwd(pd, ps, gl, "merge_fwd")
    x1 = _matmul(mixed, wts["w_out"], "nn", F32, "out_proj", add=x)

    n2 = _rmsnorm_fwd(x1, wts["norm2_w"], "norm2_fwd")
    upre = _matmul(n2, wts["ffn_w_up_t"], "nt", F32, "ffn_up")
    fact = _ffn_act_fwd(upre, wts["ffn_conv_w"], "ffn_act_fwd")
    x2 = _matmul(fact, wts["ffn_w_down"], "nn", F32, "ffn_down", add=x1)

    dx2, g_normf, loss = _final_loss(x2, target, wts["norm_f_w"], "final_loss")

    dfact = _matmul(dx2, wts["ffn_w_down"], "nt", BF16, "ffn_down_dx")
    g_wdown = _matmul(fact, dx2, "tn", BF16, "ffn_down_dw")
    dgc, duc, dwg, dwu = _ffn_act_bwd(dfact, upre, wts["ffn_conv_w"], "ffn_act_bwd")
    g_fconv = jnp.concatenate([dwg, dwu], axis=1)
    dupre = _conv_bwd_data([dgc, duc], wts["ffn_conv_w"], FFN_CONV, BF16, "ffn_conv_bwd")
    dn2 = _matmul(dupre, wts["ffn_w_up_t"], "nn", F32, "ffn_up_dx")
    g_wup = _matmul(dupre, n2, "tn", BF16, "ffn_up_dw")
    dx1, g_norm2 = _rmsnorm_bwd(dn2, x1, wts["norm2_w"], dx2, "norm2_bwd")

    dmixed = _matmul(dx1, wts["w_out"], "nt", F32, "out_proj_dx")
    g_wout = _matmul(mixed, dx1, "tn", BF16, "out_proj_dw")
    dpd, dps, dgd, dgs = _merge_bwd(dmixed, pd, ps, gl, "merge_bwd")
    dy_dn = _matmul(dpd, wts["w_proj_dn"], "nt", F32, "proj_dn_dx")
    g_wpd = _matmul(y_dn, dpd, "tn", BF16, "proj_dn_dw")
    do_sb = _matmul(dps, wts["w_proj_sb"], "nt", F32, "proj_sb_dx")
    g_wps = _matmul(o_sb, dps, "tn", BF16, "proj_sb_dw")
    grads = dict(w_proj_dn=g_wpd, w_proj_sb=g_wps, w_out=g_wout, ffn_w_up_t=g_wup, ffn_w_down=g_wdown)

    (dsq, dsk, dsv), got_early = _sb_bwd(sbqkv, o_sb, do_sb, "sb_bwd",
                                         comm=plan.early_grads(grads) if plan else None)

    do_dn, dhgate, g_dnnorm = _dn_post_bwd(dy_dn, o_dn, hgate, wts["dn_norm_w"], "dn_post_bwd")
    dvn, dsh = _dn_scan_bwd(do_dn, w, kd, qg, p, gates, "dn_scan_bwd")
    dq, dk, dv, dgates = _dn_local_bwd(act, gates, u, w, kd, qg, tinv, p, sh, dsh, dvn, do_dn, "dn_local_bwd")
    dhab, g_alog, g_dtb = _dn_gates_bwd(dgates, hab, wts["alog"], wts["dtb"], "dn_gates_bwd")
    dcv, g_dnconv = _dn_pre_bwd(dq, dk, dv, qkv_pre, wts["dn_conv_w"], "dn_pre_bwd")
    dqkv_pre = _conv_bwd_data([dcv], wts["dn_conv_w"], DN_CONV, BF16, "dn_conv_bwd")

    dh = jnp.concatenate([dqkv_pre, dhgate, dsq.astype(BF16), dsk.astype(BF16), dsv.astype(BF16), dgd, dgs], axis=1)
    w_main_t = jnp.concatenate([wts["w_dnqkv_t"], wts["w_dngate_t"], wts["w_sbqkv_t"], wts["w_gl_t"]], axis=0)
    g_wmain = _matmul(dh, n1, "tn", BF16, "in_dw_main")
    g_wab = _matmul(dhab, n1, "tn", BF16, "in_dw_ab")
    grads.update(w_main_t=g_wmain, w_ab_t=g_wab, dn_conv_w=g_dnconv, alog=g_alog, dtb=g_dtb, dn_norm_w=g_dnnorm,
                 norm2_w=g_norm2, ffn_conv_w=g_fconv, norm_f_w=g_normf)
    got_late = []
    if plan:
        dn1, got_late = _matmul(dh, w_main_t, "nn", F32, "in_dx_main", comm=plan.late_grads(grads, loss))
    else:
        dn1 = _matmul(dh, w_main_t, "nn", F32, "in_dx_main")
    dn1 = _matmul(dhab, wts["w_ab_t"], "nn", F32, "in_dx_ab", add=dn1)
    grad_x, g_norm1 = _rmsnorm_bwd(dn1, x, wts["norm1_w"], dx1, "norm1_bwd")
    grads["norm1_w"] = g_norm1
    return loss, grad_x, grads, got_early, got_late


HBM_SPEC = pl.BlockSpec(memory_space=pltpu.HBM)


def _mesh_pos():
    x, y, c = lax.axis_index("x"), lax.axis_index("y"), lax.axis_index("c")
    return x, y, c, 4 * x + 2 * y + c


def _peer(k):
    x, y, c, _ = _mesh_pos()
    px = 1 - x if k & 4 else x
    py = 1 - y if k & 2 else y
    pc = 1 - c if k & 1 else c
    return (px, py, pc), 4 * px + 2 * py + pc


def _rcopy(src, dst, send, recv, a, s, peer):
    return pltpu.make_async_remote_copy(src_ref=src, dst_ref=dst, send_sem=send.at[a, s], recv_sem=recv.at[a, s],
                                        device_id=peer, device_id_type=pl.DeviceIdType.MESH)


class _Gather:
    ICI = (2, 4, 6)

    def __init__(self, shards):
        self.args = list(shards)
        self.n = len(shards)
        self.out_shape = [jax.ShapeDtypeStruct((N_DEV,) + s.shape, s.dtype) for s in shards]
        self.scratch = [pltpu.SemaphoreType.DMA((self.n, N_DEV - 1)), pltpu.SemaphoreType.DMA((self.n, N_DEV - 1)),
                        pltpu.SemaphoreType.DMA((self.n,))]

    def _first(self, ins, outs, send, recv, a):
        me = _mesh_pos()[3]
        out, got = [], []
        for s, k in enumerate((1,) + self.ICI):
            peer, pidx = _peer(k)
            out.append(_rcopy(ins[a], outs[a].at[me], send, recv, a, s, peer))
            got.append(_rcopy(ins[a], outs[a].at[pidx], send, recv, a, s, peer))
        return out, got

    def _forward(self, ins, outs, send, recv, a):
        sib = _peer(1)[0]
        out, got = [], []
        for s, k in enumerate(self.ICI):
            held = outs[a].at[_peer(k)[1]]
            out.append(_rcopy(held, held, send, recv, a, 4 + s, sib))
            other = outs[a].at[_peer(k | 1)[1]]
            got.append(_rcopy(other, other, send, recv, a, 4 + s, sib))
        return out, got

    def start(self, ins, outs, sems):
        send, recv, loc = sems
        me = _mesh_pos()[3]
        for a in range(self.n):
            pltpu.make_async_copy(ins[a], outs[a].at[me], loc.at[a]).start()
            for cp in self._first(ins, outs, send, recv, a)[0]:
                cp.start()

    def mid(self, ins, outs, sems):
        send, recv, _ = sems
        for a in range(self.n):
            arrivals = self._first(ins, outs, send, recv, a)[1]
            for s, cp in enumerate(self._forward(ins, outs, send, recv, a)[0]):
                arrivals[1 + s].wait_recv()
                cp.start()

    def finish(self, ins, outs, sems):
        send, recv, loc = sems
        me = _mesh_pos()[3]
        for a in range(self.n):
            first_out, first_got = self._first(ins, outs, send, recv, a)
            fwd_out, fwd_got = self._forward(ins, outs, send, recv, a)
            first_got[0].wait_recv()
            for cp in fwd_got:
                cp.wait_recv()
            for cp in first_out + fwd_out:
                cp.wait_send()
            pltpu.make_async_copy(ins[a], outs[a].at[me], loc.at[a]).wait()


class _Exchange:
    def __init__(self, slabs, gathered=()):
        self.args = list(slabs) + list(gathered)
        self.n_slab = len(slabs)
        self.n = len(self.args)
        self.out_shape = ([jax.ShapeDtypeStruct(s.shape, s.dtype) for s in slabs]
                          + [jax.ShapeDtypeStruct((N_DEV,) + s.shape, s.dtype) for s in gathered])
        self.scratch = [pltpu.SemaphoreType.DMA((self.n, N_DEV - 1)), pltpu.SemaphoreType.DMA((self.n, N_DEV - 1)),
                        pltpu.SemaphoreType.DMA((self.n,))]

    def _copies(self, ins, outs, send, recv, a):
        me = _mesh_pos()[3]
        out, got = [], []
        for k in range(1, N_DEV):
            peer, pidx = _peer(k)
            src = ins[a].at[pidx] if a < self.n_slab else ins[a]
            out.append(_rcopy(src, outs[a].at[me], send, recv, a, k - 1, peer))
            got.append(_rcopy(src, outs[a].at[pidx], send, recv, a, k - 1, peer))
        return out, got

    def _local(self, ins, outs, loc, a):
        me = _mesh_pos()[3]
        return pltpu.make_async_copy(ins[a].at[me] if a < self.n_slab else ins[a], outs[a].at[me], loc.at[a])

    def start(self, ins, outs, sems):
        send, recv, loc = sems
        for a in range(self.n):
            self._local(ins, outs, loc, a).start()
            for cp in self._copies(ins, outs, send, recv, a)[0]:
                cp.start()

    def mid(self, ins, outs, sems):
        pass

    def finish(self, ins, outs, sems):
        send, recv, loc = sems
        for a in range(self.n):
            out, got = self._copies(ins, outs, send, recv, a)
            for cp in got:
                cp.wait_recv()
            for cp in out:
                cp.wait_send()
            self._local(ins, outs, loc, a).wait()


def _comm_call(comm, name):
    n = comm.n

    def body(*refs):
        ins, outs, sems = refs[:n], refs[n:2 * n], refs[2 * n:]
        comm.start(ins, outs, sems)
        comm.mid(ins, outs, sems)
        comm.finish(ins, outs, sems)

    return pl.pallas_call(
        body, name=name, out_shape=comm.out_shape, in_specs=[HBM_SPEC] * n, out_specs=[HBM_SPEC] * n,
        scratch_shapes=comm.scratch,
    )(*comm.args)


def _hosted(body, comm, n_in, n_out, when):
    if comm is None:
        return body

    def wrapped(*refs):
        ins, c_ins = refs[:n_in], refs[n_in:n_in + comm.n]
        o0 = n_in + comm.n
        outs, c_outs = refs[o0:o0 + n_out], refs[o0 + n_out:o0 + n_out + comm.n]
        scratch, sems = refs[o0 + n_out + comm.n:len(refs) - 3], refs[len(refs) - 3:]
        first, middle, last = when()

        @pl.when(first)
        def _():
            comm.start(c_ins, c_outs, sems)

        body(*ins, *outs, *scratch)

        @pl.when(middle)
        def _():
            comm.mid(c_ins, c_outs, sems)

        @pl.when(last)
        def _():
            comm.finish(c_ins, c_outs, sems)

    return wrapped


def _host_call(body, name, comm, when, out_shape, grid, in_specs, out_specs, scratch_shapes, sem, args):
    n_in, n_out = len(in_specs), len(out_specs)
    if comm is None:
        res = pl.pallas_call(body, name=name, out_shape=out_shape, grid=grid, in_specs=in_specs, out_specs=out_specs,
                             scratch_shapes=scratch_shapes, compiler_params=_params(sem))(*args)
        return list(res), []
    res = pl.pallas_call(
        _hosted(body, comm, n_in, n_out, when), name=name,
        out_shape=list(out_shape) + comm.out_shape, grid=grid,
        in_specs=list(in_specs) + [HBM_SPEC] * comm.n, out_specs=list(out_specs) + [HBM_SPEC] * comm.n,
        scratch_shapes=list(scratch_shapes) + comm.scratch,
        compiler_params=_params(("arbitrary",) * len(grid)),
    )(*args, *comm.args)
    return list(res[:n_out]), list(res[n_out:])


def _adamw(parts, w, m, v, name):
    rows, cols = w.shape
    tr, tc = rows, cols
    for cand in (128, 176):
        if rows > cand and rows % cand == 0:
            tr = cand
            break
    if tr == rows and rows > 512:
        tc = _tile(cols, 256)

    def body(p_ref, w_ref, m_ref, v_ref, g_ref, d_ref, mo_ref, vo_ref):
        g = p_ref[0].astype(F32)
        for s in range(1, N_DEV):
            g = g + p_ref[s].astype(F32)
        mn = ADAM_B1 * m_ref[...] + (1.0 - ADAM_B1) * g
        vn = ADAM_B2 * v_ref[...] + (1.0 - ADAM_B2) * (g * g)
        m_hat = mn / (1.0 - ADAM_B1 ** ADAM_STEP)
        v_hat = vn / (1.0 - ADAM_B2 ** ADAM_STEP)
        g_ref[...] = g
        d_ref[...] = -ADAM_LR * (m_hat / (jnp.sqrt(v_hat) + ADAM_EPS) + ADAM_WD * w_ref[...])
        mo_ref[...] = mn
        vo_ref[...] = vn

    blk = pl.BlockSpec((tr, tc), lambda i, j: (i, j))
    out = jax.ShapeDtypeStruct((rows, cols), F32)
    return pl.pallas_call(
        body, name=name,
        out_shape=(out, out, out, out),
        grid=(rows // tr, cols // tc),
        in_specs=[pl.BlockSpec((N_DEV, tr, tc), lambda i, j: (0, i, j)), blk, blk, blk],
        out_specs=(blk, blk, blk, blk),
        compiler_params=_params(("parallel", "parallel")),
    )(parts, w, m, v)


CONV_PACK = 8 * 1024
WEIGHT_ORDER = ("norm1_w", "w_in", "dn_conv_w", "dn_A_log", "dn_dt_bias", "dn_norm_w", "w_proj_dn", "w_proj_sb",
                "w_out", "norm2_w", "ffn_w_up", "ffn_conv_w", "ffn_w_down", "norm_f_w")


def _cols_to_slabs(g):
    r, c8 = g.shape
    return g.reshape(r, N_DEV, c8 // N_DEV).transpose(1, 0, 2)


def _slabs_to_cols(s):
    d, r, c = s.shape
    return s.transpose(1, 0, 2).reshape(r, d * c)


def kernel(x, norm1_w, w_in, dn_conv_w, dn_A_log, dn_dt_bias, dn_norm_w, w_proj_dn, w_proj_sb, w_out, norm2_w, ffn_w_up, ffn_conv_w, ffn_w_down, norm_f_w, loss_target, m_norm1_w, m_w_in, m_dn_conv_w, m_dn_A_log, m_dn_dt_bias, m_dn_norm_w, m_w_proj_dn, m_w_proj_sb, m_w_out, m_norm2_w, m_ffn_w_up, m_ffn_conv_w, m_ffn_w_down, m_norm_f_w, v_norm1_w, v_w_in, v_dn_conv_w, v_dn_A_log, v_dn_dt_bias, v_dn_norm_w, v_w_proj_dn, v_w_proj_sb, v_w_out, v_norm2_w, v_ffn_w_up, v_ffn_conv_w, v_ffn_w_down, v_norm_f_w):
    me = _mesh_pos()[3]
    tr = lambda a: jnp.transpose(a[0])
    w_loc = dict(norm1_w=norm1_w, w_in=tr(w_in), dn_conv_w=dn_conv_w[0], dn_A_log=dn_A_log, dn_dt_bias=dn_dt_bias,
                 dn_norm_w=dn_norm_w, w_proj_dn=w_proj_dn[0], w_proj_sb=w_proj_sb[0], w_out=w_out[0],
                 norm2_w=norm2_w, ffn_w_up=tr(ffn_w_up), ffn_conv_w=ffn_conv_w[0], ffn_w_down=ffn_w_down[0],
                 norm_f_w=norm_f_w[None, :])
    m_loc = dict(norm1_w=m_norm1_w, w_in=tr(m_w_in), dn_conv_w=m_dn_conv_w[0], dn_A_log=m_dn_A_log,
                 dn_dt_bias=m_dn_dt_bias, dn_norm_w=m_dn_norm_w, w_proj_dn=m_w_proj_dn[0], w_proj_sb=m_w_proj_sb[0],
                 w_out=m_w_out[0], norm2_w=m_norm2_w, ffn_w_up=tr(m_ffn_w_up), ffn_conv_w=m_ffn_conv_w[0],
                 ffn_w_down=m_ffn_w_down[0], norm_f_w=m_norm_f_w[None, :])
    v_loc = dict(norm1_w=v_norm1_w, w_in=tr(v_w_in), dn_conv_w=v_dn_conv_w[0], dn_A_log=v_dn_A_log,
                 dn_dt_bias=v_dn_dt_bias, dn_norm_w=v_dn_norm_w, w_proj_dn=v_w_proj_dn[0], w_proj_sb=v_w_proj_sb[0],
                 w_out=v_w_out[0], norm2_w=v_norm2_w, ffn_w_up=tr(v_ffn_w_up), ffn_conv_w=v_ffn_conv_w[0],
                 ffn_w_down=v_ffn_w_down[0], norm_f_w=v_norm_f_w[None, :])

    conv_flat = jnp.concatenate([w_loc["dn_conv_w"].reshape(-1), w_loc["ffn_conv_w"].reshape(-1)])
    n_dn, n_ffn = DN_CONV * 3 * WIDTH // N_DEV, FFN_CONV * 2 * D_FF // N_DEV
    conv_pack = jnp.pad(conv_flat, (0, CONV_PACK - n_dn - n_ffn)).reshape(8, 1024)
    g_in, g_conv = _comm_call(_Gather([w_loc["w_in"].astype(BF16), conv_pack]), "gather_first")
    in_width = g_in.shape[0] * g_in.shape[1]
    w_in_t = g_in.reshape(in_width, D_MODEL)
    g_conv = g_conv.reshape(N_DEV, CONV_PACK)
    dn_conv_full = _slabs_to_cols(g_conv[:, :n_dn].reshape(N_DEV, DN_CONV, 3 * WIDTH // N_DEV))
    ffn_conv_full = _slabs_to_cols(g_conv[:, n_dn:n_dn + n_ffn].reshape(N_DEV, FFN_CONV, 2 * D_FF // N_DEV))
    q_end = 3 * WIDTH
    ab_end = q_end + 2 * HEADS
    gate_end = ab_end + WIDTH
    sb_end = gate_end + 3 * WIDTH
    pad_lanes = lambda a: jnp.pad(a, ((0, 0), (0, 128 - a.shape[1])))
    wts = dict(
        norm1_w=norm1_w, w_dnqkv_t=w_in_t[:q_end], w_ab_t=jnp.pad(w_in_t[q_end:ab_end], ((0, 128 - 2 * HEADS), (0, 0))),
        w_dngate_t=w_in_t[ab_end:gate_end], w_sbqkv_t=w_in_t[gate_end:sb_end], w_gl_t=w_in_t[sb_end:],
        dn_conv_w=dn_conv_full, alog=pad_lanes(dn_A_log), dtb=pad_lanes(dn_dt_bias), dn_norm_w=dn_norm_w,
        norm2_w=norm2_w, ffn_conv_w=ffn_conv_full, norm_f_w=norm_f_w[None, :])

    n_fc = FFN_CONV * 2 * D_FF
    fc_rows = -(-n_fc // D_MODEL)
    dn_rows = DN_CONV * 3 * WIDTH // D_MODEL
    late_names = ("w_proj_dn", "w_proj_sb", "w_out", "ffn_w_up", "ffn_w_down")

    class Plan:
        @staticmethod
        def late_gather():
            return _Gather([w_loc[k].astype(BF16) for k in late_names])

        @staticmethod
        def late_weights(got):
            g_pd, g_ps, g_out, g_up, g_down = got
            return dict(w_proj_dn=g_pd.reshape(WIDTH, D_MODEL), w_proj_sb=g_ps.reshape(WIDTH, D_MODEL),
                        w_out=g_out.reshape(D_MODEL, D_MODEL), ffn_w_up_t=g_up.reshape(2 * D_FF, D_MODEL),
                        ffn_w_down=g_down.reshape(D_FF, D_MODEL))

        @staticmethod
        def early_grads(g):
            return _Exchange([g["w_proj_dn"].reshape(N_DEV, WIDTH // N_DEV, D_MODEL),
                              g["w_proj_sb"].reshape(N_DEV, WIDTH // N_DEV, D_MODEL),
                              g["w_out"].reshape(N_DEV, D_MODEL // N_DEV, D_MODEL),
                              g["ffn_w_up_t"].reshape(N_DEV, 2 * D_FF // N_DEV, D_MODEL),
                              g["ffn_w_down"].reshape(N_DEV, D_FF // N_DEV, D_MODEL)])

        @staticmethod
        def late_grads(g, loss):
            g_win_t = jnp.concatenate([g["w_main_t"][:q_end], g["w_ab_t"][:2 * HEADS], g["w_main_t"][q_end:]],
                                      axis=0)
            row3 = jnp.concatenate([g["dn_norm_w"], g["alog"], g["dtb"], jnp.pad(loss, ((0, 0), (0, 127))),
                                    jnp.zeros((1, D_MODEL - 512), F32)], axis=1)
            fconv_rows = jnp.pad(g["ffn_conv_w"].reshape(-1), (0, fc_rows * D_MODEL - n_fc)).reshape(fc_rows, D_MODEL)
            pad8 = lambda a: jnp.pad(a, ((0, -a.shape[0] % 8), (0, 0)))
            pieces = [g["norm2_w"], g["norm_f_w"], row3, g["dn_conv_w"].reshape(dn_rows, D_MODEL), fconv_rows]
            small = jnp.concatenate([pad8(a) for a in pieces], axis=0)
            assert small.shape[0] == SMALL_ROWS
            return _Exchange([g_win_t.reshape(N_DEV, in_width // N_DEV, D_MODEL)], [small])

    loss, grad_x, g, got_early, got_late = _local_step(x[0], loss_target[0], wts, Plan)
    r_pd, r_ps, r_out, r_up, r_down = got_early
    r_in, r_small = got_late
    (r_norm1,) = _comm_call(_Exchange([], [jnp.pad(g["norm1_w"], ((0, 7), (0, 0)))]), "gather_norm1")

    parts = dict(w_in=r_in, w_proj_dn=r_pd, w_proj_sb=r_ps, w_out=r_out, ffn_w_up=r_up, ffn_w_down=r_down)
    parts["norm1_w"] = r_norm1[:, 0:1, :]
    parts["norm2_w"] = r_small[:, 0:1, :]
    parts["norm_f_w"] = r_small[:, 8:9, :]
    parts["dn_norm_w"] = r_small[:, 16:17, 0:HEAD_DIM]
    parts["dn_A_log"] = r_small[:, 16:17, 128:128 + HEADS]
    parts["dn_dt_bias"] = r_small[:, 16:17, 256:256 + HEADS]
    dnc = r_small[:, 24:24 + dn_rows, :].reshape(N_DEV, DN_CONV, 3 * WIDTH)
    parts["dn_conv_w"] = lax.dynamic_slice_in_dim(dnc, me * (3 * WIDTH // N_DEV), 3 * WIDTH // N_DEV, axis=2)
    fc0 = 24 + dn_rows + (-dn_rows % 8)
    fcc = r_small[:, fc0:fc0 + fc_rows, :].reshape(N_DEV, fc_rows * D_MODEL)[:, :n_fc]
    fcc = fcc.reshape(N_DEV, FFN_CONV, 2 * D_FF)
    parts["ffn_conv_w"] = lax.dynamic_slice_in_dim(fcc, me * (2 * D_FF // N_DEV), 2 * D_FF // N_DEV, axis=2)
    loss_total = jnp.sum(r_small[:, 16, 384])

    res = {k: _adamw(parts[k], w_loc[k], m_loc[k], v_loc[k], "adamw_" + k) for k in WEIGHT_ORDER}
    lead = ("w_in", "dn_conv_w", "w_proj_dn", "w_proj_sb", "w_out", "ffn_w_up", "ffn_conv_w", "ffn_w_down")

    def shaped(k, a):
        if k in ("w_in", "ffn_w_up"):
            return jnp.transpose(a)[None]
        if k in lead:
            return a[None]
        if k == "norm_f_w":
            return a[0]
        return a

    outs = [loss_total, grad_x[None]]
    for idx in range(4):
        outs += [shaped(k, res[k][idx]) for k in WEIGHT_ORDER]
    return tuple(outs)
```

```python
import functools

import jax
import jax.numpy as jnp
from jax import lax
from jax.experimental import pallas as pl
from jax.experimental.pallas import tpu as pltpu

F32 = jnp.float32
BF16 = jnp.bfloat16

N_DEV = 8
D_MODEL = 1024
HEADS = 8
HEAD_DIM = 128
WIDTH = HEADS * HEAD_DIM
DN_CONV = 4
DN_CHUNK = 64
D_FF = 2816
FFN_CONV = 3
EPS = 1e-6
ATT_BLOCK = 256
SB_LOG_ZERO = -104.0
SB_GROUP = 2
SMALL_ROWS = 64

ADAM_LR = 0.001
ADAM_B1 = 0.9
ADAM_B2 = 0.999
ADAM_EPS = 1e-08
ADAM_WD = 0.01
ADAM_STEP = 10

VMEM_LIMIT = 48 * 1024 * 1024


def _params(sem=None, **kw):
    return pltpu.CompilerParams(dimension_semantics=sem, vmem_limit_bytes=VMEM_LIMIT, **kw)


def _tile(n, cap):
    if n <= cap:
        return n
    best = None
    for t in range(128, cap + 1, 128):
        if n % t == 0:
            best = t
    assert best is not None, (n, cap)
    return best


def _dot(a, b, dims):
    return lax.dot_general(a, b, ((dims[0], dims[1]), ((), ())), preferred_element_type=F32)


NN = ((1,), (0,))
NT = ((1,), (1,))
TN = ((0,), (0,))


def _dotb(a, b, dims):
    return _dot(a.astype(BF16), b.astype(BF16), dims)


def _split3(x):
    h1 = x.astype(BF16)
    r1 = x - h1.astype(F32)
    h2 = r1.astype(BF16)
    r2 = r1 - h2.astype(F32)
    return h1, h2, r2.astype(BF16)


def _dot_xr(a, b_exact, dims):
    a1, a2, a3 = _split3(a)
    return _dot(a1, b_exact, dims) + _dot(a2, b_exact, dims) + _dot(a3, b_exact, dims)


def _split2(x):
    h1 = x.astype(BF16)
    return h1, (x - h1.astype(F32)).astype(BF16)


def _dot_xr2(a, b_exact, dims):
    a1, a2 = _split2(a)
    return _dot(a1, b_exact, dims) + _dot(a2, b_exact, dims)


def _dot_xl(a_exact, b, dims):
    b1, b2, b3 = _split3(b)
    return _dot(a_exact, b1, dims) + _dot(a_exact, b2, dims) + _dot(a_exact, b3, dims)


def _dot3(a, b, dims):
    a1 = a.astype(BF16)
    a2 = (a - a1.astype(F32)).astype(BF16)
    b1 = b.astype(BF16)
    b2 = (b - b1.astype(F32)).astype(BF16)
    return _dot(a1, b1, dims) + (_dot(a1, b2, dims) + _dot(a2, b1, dims))


def _sigmoid(x):
    return 1.0 / (1.0 + jnp.exp(-x))


def _log1pexp_neg_abs(x):
    return jnp.log(1.0 + jnp.exp(-jnp.abs(x)))


def _iota(shape, dim):
    return lax.broadcasted_iota(jnp.int32, shape, dim)


def _matmul(a, b, mode, out_dtype, name, add=None, comm=None):
    if mode == "nn":
        (m, k), (k2, n) = a.shape, b.shape
    elif mode == "nt":
        (m, k), (n, k2) = a.shape, b.shape
    else:
        (k, m), (k2, n) = a.shape, b.shape
    assert k == k2, (a.shape, b.shape, mode)
    tm, tn, tk = _tile(m, 1024), _tile(n, 1408), _tile(k, 1536)
    nk = k // tk
    dims = {"nn": NN, "nt": NT, "tn": TN}[mode]

    def body(*refs):
        if add is None:
            a_ref, b_ref, o_ref, acc_ref = refs
        else:
            a_ref, b_ref, add_ref, o_ref, acc_ref = refs
        kk = pl.program_id(2)

        @pl.when(kk == 0)
        def _():
            acc_ref[...] = jnp.zeros_like(acc_ref)

        acc_ref[...] += _dotb(a_ref[...], b_ref[...], dims)

        @pl.when(kk == nk - 1)
        def _():
            r = acc_ref[...]
            if add is not None:
                r = r + add_ref[...].astype(F32)
            o_ref[...] = r.astype(out_dtype)

    if mode == "nn":
        specs = [pl.BlockSpec((tm, tk), lambda i, j, l: (i, l)), pl.BlockSpec((tk, tn), lambda i, j, l: (l, j))]
    elif mode == "nt":
        specs = [pl.BlockSpec((tm, tk), lambda i, j, l: (i, l)), pl.BlockSpec((tn, tk), lambda i, j, l: (j, l))]
    else:
        specs = [pl.BlockSpec((tk, tm), lambda i, j, l: (l, i)), pl.BlockSpec((tk, tn), lambda i, j, l: (l, j))]
    args = [a, b]
    if add is not None:
        specs.append(pl.BlockSpec((tm, tn), lambda i, j, l: (i, j)))
        args.append(add)
    grid = (m // tm, n // tn, nk)

    def when():
        i, j, l = pl.program_id(0), pl.program_id(1), pl.program_id(2)
        first = jnp.logical_and(jnp.logical_and(i == 0, j == 0), l == 0)
        last = jnp.logical_and(jnp.logical_and(i == grid[0] - 1, j == grid[1] - 1), l == nk - 1)
        return first, last, last

    (out,), extra = _host_call(
        body, name, comm, when, [jax.ShapeDtypeStruct((m, n), out_dtype)], grid, specs,
        [pl.BlockSpec((tm, tn), lambda i, j, l: (i, j))], [pltpu.VMEM((tm, tn), F32)],
        ("parallel", "parallel", "arbitrary"), args)
    return out if comm is None else (out, extra)


def _rmsnorm_fwd(x, w, name):
    t, d = x.shape
    tr = _tile(t, 512)

    def body(x_ref, w_ref, o_ref):
        xv = x_ref[...]
        r = lax.rsqrt(jnp.mean(xv * xv, axis=1, keepdims=True) + EPS)
        o_ref[...] = (xv * r * w_ref[...]).astype(BF16)

    return pl.pallas_call(
        body, name=name,
        out_shape=jax.ShapeDtypeStruct((t, d), BF16),
        grid=(t // tr,),
        in_specs=[pl.BlockSpec((tr, d), lambda i: (i, 0)), pl.BlockSpec((1, d), lambda i: (0, 0))],
        out_specs=pl.BlockSpec((tr, d), lambda i: (i, 0)),
        compiler_params=_params(("parallel",)),
    )(x, w)


def _rmsnorm_bwd(dn, x, w, dres, name):
    t, d = x.shape
    tr = _tile(t, 512)

    def body(dn_ref, x_ref, w_ref, dres_ref, dx_ref, dw_ref):
        i = pl.program_id(0)
        xv = x_ref[...]
        g = dn_ref[...].astype(F32)
        r = lax.rsqrt(jnp.mean(xv * xv, axis=1, keepdims=True) + EPS)
        xh = xv * r
        dxh = g * w_ref[...]
        dx = r * (dxh - xh * jnp.mean(dxh * xh, axis=1, keepdims=True))
        dx_ref[...] = dres_ref[...] + dx

        @pl.when(i == 0)
        def _():
            dw_ref[...] = jnp.zeros_like(dw_ref)

        dw_ref[...] += jnp.sum(g * xh, axis=0, keepdims=True)

    return pl.pallas_call(
        body, name=name,
        out_shape=(jax.ShapeDtypeStruct((t, d), F32), jax.ShapeDtypeStruct((1, d), F32)),
        grid=(t // tr,),
        in_specs=[pl.BlockSpec((tr, d), lambda i: (i, 0)), pl.BlockSpec((tr, d), lambda i: (i, 0)),
                  pl.BlockSpec((1, d), lambda i: (0, 0)), pl.BlockSpec((tr, d), lambda i: (i, 0))],
        out_specs=(pl.BlockSpec((tr, d), lambda i: (i, 0)), pl.BlockSpec((1, d), lambda i: (0, 0))),
        compiler_params=_params(("arbitrary",)),
    )(dn, x, w, dres)


def _final_loss(x2, target, w, name):
    t, d = x2.shape
    tr = _tile(t, 512)

    def body(x_ref, t_ref, w_ref, dx_ref, dw_ref, loss_ref):
        i = pl.program_id(0)
        xv = x_ref[...]
        r = lax.rsqrt(jnp.mean(xv * xv, axis=1, keepdims=True) + EPS)
        xh = xv * r
        err = xh * w_ref[...] - t_ref[...]
        dy = err * (1.0 / d)
        dxh = dy * w_ref[...]
        dx_ref[...] = r * (dxh - xh * jnp.mean(dxh * xh, axis=1, keepdims=True))

        @pl.when(i == 0)
        def _():
            dw_ref[...] = jnp.zeros_like(dw_ref)
            loss_ref[...] = jnp.zeros_like(loss_ref)

        dw_ref[...] += jnp.sum(dy * xh, axis=0, keepdims=True)
        row = jnp.sum(err * err, axis=1, keepdims=True) * (0.5 / d)
        loss_ref[...] += jnp.sum(row, axis=0, keepdims=True)

    return pl.pallas_call(
        body, name=name,
        out_shape=(jax.ShapeDtypeStruct((t, d), F32), jax.ShapeDtypeStruct((1, d), F32),
                   jax.ShapeDtypeStruct((1, 1), F32)),
        grid=(t // tr,),
        in_specs=[pl.BlockSpec((tr, d), lambda i: (i, 0)), pl.BlockSpec((tr, d), lambda i: (i, 0)),
                  pl.BlockSpec((1, d), lambda i: (0, 0))],
        out_specs=(pl.BlockSpec((tr, d), lambda i: (i, 0)), pl.BlockSpec((1, d), lambda i: (0, 0)),
                   pl.BlockSpec((1, 1), lambda i: (0, 0))),
        compiler_params=_params(("arbitrary",)),
    )(x2, target, w)


def _shift_down(cur, prev, k, row):
    r = pltpu.roll(cur, k, 0)
    for m in range(k):
        r = jnp.where(row == m, prev[8 - k + m:8 - k + m + 1, :], r)
    return r


def _shift_up(cur, nxt, k, row, tr):
    r = pltpu.roll(cur, tr - k, 0)
    for m in range(k):
        r = jnp.where(row == tr - k + m, nxt[m:m + 1, :], r)
    return r


def _conv_taps(cur, prev, w, ntaps, row):
    taps = [cur if i == ntaps - 1 else _shift_down(cur, prev, ntaps - 1 - i, row) for i in range(ntaps)]
    y = w[0:1, :] * taps[0]
    for i in range(1, ntaps):
        y = y + w[i:i + 1, :] * taps[i]
    return taps, y


def _conv_bwd_data(parts, w, ntaps, out_dtype, name):
    t, chp = parts[0].shape
    npart = len(parts)
    tr, tc = _tile(t, 512), _tile(chp, 1408)
    nc = chp // tc
    nrow8 = t // 8
    last = t // tr - 1

    def body(*refs):
        cur_refs, nxt_refs = refs[:npart], refs[npart:2 * npart]
        w_ref, o_ref = refs[2 * npart], refs[2 * npart + 1]
        i, j = pl.program_id(0), pl.program_id(1)
        cur, nxt = cur_refs[0][...], nxt_refs[0][...]
        for p in range(1, npart):
            cur = jnp.where(j >= p * nc, cur_refs[p][...], cur)
            nxt = jnp.where(j >= p * nc, nxt_refs[p][...], nxt)
        nxt = jnp.where(i == last, 0.0, nxt)
        row = _iota(cur.shape, 0)
        wv = w_ref[...]
        y = wv[ntaps - 1:ntaps, :] * cur
        for k in range(1, ntaps):
            y = y + wv[ntaps - 1 - k:ntaps - k, :] * _shift_up(cur, nxt, k, row, tr)
        o_ref[...] = y.astype(out_dtype)

    col = lambda p: (lambda j: jnp.clip(j - p * nc, 0, nc - 1))
    cur_specs = [pl.BlockSpec((tr, tc), lambda i, j, c=col(p): (i, c(j))) for p in range(npart)]
    nxt_specs = [pl.BlockSpec((8, tc), lambda i, j, c=col(p): (jnp.minimum((i + 1) * (tr // 8), nrow8 - 1), c(j)))
                 for p in range(npart)]
    return pl.pallas_call(
        body, name=name,
        out_shape=jax.ShapeDtypeStruct((t, npart * chp), out_dtype),
        grid=(t // tr, npart * nc),
        in_specs=cur_specs + nxt_specs + [pl.BlockSpec((ntaps, tc), lambda i, j: (0, j))],
        out_specs=pl.BlockSpec((tr, tc), lambda i, j: (i, j)),
        compiler_params=_params(("parallel", "parallel")),
    )(*parts, *parts, w)


def _ffn_act_fwd(upre, cw, name):
    t = upre.shape[0]
    tr, tc = _tile(t, 512), _tile(D_FF, 1408)
    nj = D_FF // tc

    def body(g_ref, gp_ref, u_ref, up_ref, wg_ref, wu_ref, o_ref):
        i = pl.program_id(0)
        row = _iota((tr, tc), 0)
        gp = jnp.where(i == 0, 0.0, gp_ref[...])
        up = jnp.where(i == 0, 0.0, up_ref[...])
        _, gc = _conv_taps(g_ref[...], gp, wg_ref[...], FFN_CONV, row)
        _, uc = _conv_taps(u_ref[...], up, wu_ref[...], FFN_CONV, row)
        o_ref[...] = (gc * _sigmoid(gc) * uc).astype(BF16)

    prev = lambda off: (lambda i, j: (jnp.maximum(i * (tr // 8) - 1, 0), j + off))
    return pl.pallas_call(
        body, name=name,
        out_shape=jax.ShapeDtypeStruct((t, D_FF), BF16),
        grid=(t // tr, nj),
        in_specs=[pl.BlockSpec((tr, tc), lambda i, j: (i, j)), pl.BlockSpec((8, tc), prev(0)),
                  pl.BlockSpec((tr, tc), lambda i, j: (i, j + nj)), pl.BlockSpec((8, tc), prev(nj)),
                  pl.BlockSpec((FFN_CONV, tc), lambda i, j: (0, j)),
                  pl.BlockSpec((FFN_CONV, tc), lambda i, j: (0, j + nj))],
        out_specs=pl.BlockSpec((tr, tc), lambda i, j: (i, j)),
        compiler_params=_params(("parallel", "parallel")),
    )(upre, upre, upre, upre, cw, cw)


def _ffn_act_bwd(dact, upre, cw, name):
    t = upre.shape[0]
    tr, tc = _tile(t, 256), _tile(D_FF, 1408)
    nj = D_FF // tc

    def body(da_ref, g_ref, gp_ref, u_ref, up_ref, wg_ref, wu_ref, dg_ref, du_ref, dwg_ref, dwu_ref):
        i = pl.program_id(1)
        row = _iota((tr, tc), 0)
        gp = jnp.where(i == 0, 0.0, gp_ref[...])
        up = jnp.where(i == 0, 0.0, up_ref[...])
        gt, gc = _conv_taps(g_ref[...], gp, wg_ref[...], FFN_CONV, row)
        ut, uc = _conv_taps(u_ref[...], up, wu_ref[...], FFN_CONV, row)
        da = da_ref[...].astype(F32)
        sg = _sigmoid(gc)
        dgc = da * uc * (sg * (1.0 + gc * (1.0 - sg)))
        duc = da * (gc * sg)
        dg_ref[...] = dgc
        du_ref[...] = duc

        @pl.when(i == 0)
        def _():
            dwg_ref[...] = jnp.zeros_like(dwg_ref)
            dwu_ref[...] = jnp.zeros_like(dwu_ref)

        for k in range(FFN_CONV):
            dwg_ref[k:k + 1, :] += jnp.sum(dgc * gt[k], axis=0, keepdims=True)
            dwu_ref[k:k + 1, :] += jnp.sum(duc * ut[k], axis=0, keepdims=True)

    prev = lambda off: (lambda j, i: (jnp.maximum(i * (tr // 8) - 1, 0), j + off))
    blk = lambda off: pl.BlockSpec((tr, tc), lambda j, i: (i, j + off))
    wblk = lambda off: pl.BlockSpec((FFN_CONV, tc), lambda j, i: (0, j + off))
    dgc, duc, dwg, dwu = pl.pallas_call(
        body, name=name,
        out_shape=(jax.ShapeDtypeStruct((t, D_FF), F32), jax.ShapeDtypeStruct((t, D_FF), F32),
                   jax.ShapeDtypeStruct((FFN_CONV, D_FF), F32), jax.ShapeDtypeStruct((FFN_CONV, D_FF), F32)),
        grid=(nj, t // tr),
        in_specs=[blk(0), blk(0), pl.BlockSpec((8, tc), prev(0)), blk(nj), pl.BlockSpec((8, tc), prev(nj)),
                  wblk(0), wblk(nj)],
        out_specs=(blk(0), blk(0), wblk(0), wblk(0)),
        compiler_params=_params(("parallel", "arbitrary")),
    )(dact, upre, upre, upre, upre, cw, cw)
    return dgc, duc, dwg, dwu


def _dn_pre_fwd(qkv_pre, cw, name):
    t = qkv_pre.shape[0]
    tr = _tile(t, 512)
    scale = HEAD_DIM ** -0.5

    def body(x_ref, p_ref, w_ref, o_ref):
        i, j = pl.program_id(0), pl.program_id(1)
        row = _iota((tr, WIDTH), 0)
        prev = jnp.where(i == 0, 0.0, p_ref[...])
        _, c = _conv_taps(x_ref[...], prev, w_ref[...], DN_CONV, row)
        s = c * _sigmoid(c)
        for h in range(HEADS):
            sl = slice(h * HEAD_DIM, (h + 1) * HEAD_DIM)
            sh = s[:, sl]
            r = lax.rsqrt(jnp.sum(sh * sh, axis=1, keepdims=True) + EPS)
            o_ref[:, sl] = sh * jnp.where(j == 0, r * scale, jnp.where(j == 1, r, 1.0))

    return pl.pallas_call(
        body, name=name,
        out_shape=jax.ShapeDtypeStruct((t, 3 * WIDTH), F32),
        grid=(t // tr, 3),
        in_specs=[pl.BlockSpec((tr, WIDTH), lambda i, j: (i, j)),
                  pl.BlockSpec((8, WIDTH), lambda i, j: (jnp.maximum(i * (tr // 8) - 1, 0), j)),
                  pl.BlockSpec((DN_CONV, WIDTH), lambda i, j: (0, j))],
        out_specs=pl.BlockSpec((tr, WIDTH), lambda i, j: (i, j)),
        compiler_params=_params(("parallel", "parallel")),
    )(qkv_pre, qkv_pre, cw)


def _dn_pre_bwd(dq, dk, dv, qkv_pre, cw, name):
    t = qkv_pre.shape[0]
    tr = _tile(t, 256)
    scale = HEAD_DIM ** -0.5

    def body(dq_ref, dk_ref, dv_ref, x_ref, p_ref, w_ref, dc_ref, dw_ref):
        j, i = pl.program_id(0), pl.program_id(1)
        row = _iota((tr, WIDTH), 0)
        prev = jnp.where(i == 0, 0.0, p_ref[...])
        taps, c = _conv_taps(x_ref[...], prev, w_ref[...], DN_CONV, row)
        d = jnp.where(j == 0, dq_ref[...] * scale, jnp.where(j == 1, dk_ref[...], dv_ref[...]))
        sg = _sigmoid(c)
        s = c * sg
        dsilu = sg * (1.0 + c * (1.0 - sg))
        for h in range(HEADS):
            sl = slice(h * HEAD_DIM, (h + 1) * HEAD_DIM)
            sh, dh = s[:, sl], d[:, sl]
            r = lax.rsqrt(jnp.sum(sh * sh, axis=1, keepdims=True) + EPS)
            nh = sh * r
            ds_norm = r * (dh - nh * jnp.sum(nh * dh, axis=1, keepdims=True))
            dc_ref[:, sl] = jnp.where(j < 2, ds_norm, dh) * dsilu[:, sl]

        @pl.when(i == 0)
        def _():
            dw_ref[...] = jnp.zeros_like(dw_ref)

        dc = dc_ref[...]
        for k in range(DN_CONV):
            dw_ref[k:k + 1, :] += jnp.sum(dc * taps[k], axis=0, keepdims=True)

    dspec = lambda p: pl.BlockSpec((tr, WIDTH), lambda j, i: (jnp.where(j == p, i, 0), 0))
    return pl.pallas_call(
        body, name=name,
        out_shape=(jax.ShapeDtypeStruct((t, 3 * WIDTH), F32), jax.ShapeDtypeStruct((DN_CONV, 3 * WIDTH), F32)),
        grid=(3, t // tr),
        in_specs=[dspec(0), dspec(1), dspec(2),
                  pl.BlockSpec((tr, WIDTH), lambda j, i: (i, j)),
                  pl.BlockSpec((8, WIDTH), lambda j, i: (jnp.maximum(i * (tr // 8) - 1, 0), j)),
                  pl.BlockSpec((DN_CONV, WIDTH), lambda j, i: (0, j))],
        out_specs=(pl.BlockSpec((tr, WIDTH), lambda j, i: (i, j)),
                   pl.BlockSpec((DN_CONV, WIDTH), lambda j, i: (0, j))),
        compiler_params=_params(("parallel", "arbitrary")),
    )(dq, dk, dv, qkv_pre, qkv_pre, cw)


def _tri(n, kind):
    r, c = _iota((n, n), 0), _iota((n, n), 1)
    m = {"lower": r >= c, "strict": r > c, "upper": r <= c}[kind]
    return m


def _dn_gates_fwd(hab, alog, dtb, name):
    t = hab.shape[0]
    cc = DN_CHUNK

    def body(h_ref, al_ref, dt_ref, o_ref):
        hv = h_ref[...]
        lane = _iota(hv.shape, 1)
        xa = hv + dt_ref[...]
        sp = jnp.maximum(xa, 0.0) + _log1pexp_neg_abs(xa)
        g = jnp.where(lane < HEADS, -jnp.exp(al_ref[...]) * sp, 0.0)
        tril = jnp.where(_tri(cc, "lower"), 1.0, 0.0).astype(BF16)
        gc = _dot_xl(tril, g, NN)
        o_ref[...] = jnp.where(lane < HEADS, gc, jnp.where(lane < 2 * HEADS, _sigmoid(hv), 0.0))

    return pl.pallas_call(
        body, name=name,
        out_shape=jax.ShapeDtypeStruct((t, 128), F32),
        grid=(t // cc,),
        in_specs=[pl.BlockSpec((cc, 128), lambda i: (i, 0)), pl.BlockSpec((1, 128), lambda i: (0, 0)),
                  pl.BlockSpec((1, 128), lambda i: (0, 0))],
        out_specs=pl.BlockSpec((cc, 128), lambda i: (i, 0)),
        compiler_params=_params(("parallel",)),
    )(hab, alog, dtb)


def _dn_gates_bwd(dgates, hab, alog, dtb, name):
    t = hab.shape[0]
    cc = DN_CHUNK

    def body(d_ref, h_ref, al_ref, dt_ref, o_ref, dal_ref, ddt_ref):
        i = pl.program_id(0)
        hv = h_ref[...]
        dv = d_ref[...]
        lane = _iota(hv.shape, 1)
        triu = jnp.where(_tri(cc, "upper"), 1.0, 0.0).astype(BF16)
        dg = _dot_xl(triu, jnp.where(lane < HEADS, dv, 0.0), NN)
        xa = hv + dt_ref[...]
        sp = jnp.maximum(xa, 0.0) + _log1pexp_neg_abs(xa)
        ea = jnp.exp(al_ref[...])
        da = jnp.where(lane < HEADS, dg * (-ea) * _sigmoid(xa), 0.0)
        be = _sigmoid(hv)
        db = dv * be * (1.0 - be)
        o_ref[...] = jnp.where(lane < HEADS, da, jnp.where(lane < 2 * HEADS, db, 0.0))

        @pl.when(i == 0)
        def _():
            dal_ref[...] = jnp.zeros_like(dal_ref)
            ddt_ref[...] = jnp.zeros_like(ddt_ref)

        dal_ref[...] += jnp.sum(jnp.where(lane < HEADS, dg * (-ea) * sp, 0.0), axis=0, keepdims=True)
        ddt_ref[...] += jnp.sum(da, axis=0, keepdims=True)

    return pl.pallas_call(
        body, name=name,
        out_shape=(jax.ShapeDtypeStruct((t, 128), F32), jax.ShapeDtypeStruct((1, 128), F32),
                   jax.ShapeDtypeStruct((1, 128), F32)),
        grid=(t // cc,),
        in_specs=[pl.BlockSpec((cc, 128), lambda i: (i, 0)), pl.BlockSpec((cc, 128), lambda i: (i, 0)),
                  pl.BlockSpec((1, 128), lambda i: (0, 0)), pl.BlockSpec((1, 128), lambda i: (0, 0))],
        out_specs=(pl.BlockSpec((cc, 128), lambda i: (i, 0)), pl.BlockSpec((1, 128), lambda i: (0, 0)),
                   pl.BlockSpec((1, 128), lambda i: (0, 0))),
        compiler_params=_params(("arbitrary",)),
    )(dgates, hab, alog, dtb)


def _dn_chunk_common(gates, h):
    cc = DN_CHUNK
    lane = _iota(gates.shape, 1)
    gh = jnp.where(lane == h, gates, 0.0)
    gc_col = jnp.sum(gh, axis=1, keepdims=True)
    gc_row = _dot_xl(jnp.ones((cc, 128), BF16), gh, NT)
    beta = jnp.sum(jnp.where(lane == h + HEADS, gates, 0.0), axis=1, keepdims=True)
    lower = _tri(cc, "lower")
    decay = jnp.where(lower, jnp.exp(jnp.where(lower, gc_col - gc_row, 0.0)), 0.0)
    gc_last = gc_col[cc - 1:cc, :]
    return gc_col, gc_last, beta, decay


def _dn_local_fwd(act, gates, name):
    t = act.shape[0]
    cc = DN_CHUNK
    nc = t // cc

    def body(q_ref, k_ref, v_ref, g_ref, u_ref, w_ref, kd_ref, qg_ref, ti_ref, p_ref):
        gates = g_ref[...]
        eye = jnp.where(_iota((cc, cc), 0) == _iota((cc, cc), 1), 1.0, 0.0)
        hs = range(HEADS)
        sl = [slice(h * HEAD_DIM, (h + 1) * HEAD_DIM) for h in hs]
        q, k, v = ([r[:, s] for s in sl] for r in (q_ref, k_ref, v_ref))
        gc_col, gc_last, beta, decay = zip(*[_dn_chunk_common(gates, h) for h in hs])
        gam = [jnp.exp(g) for g in gc_col]
        kb = [k[h] * beta[h] for h in hs]
        npow = [-jnp.where(_tri(cc, "strict"), _dotb(kb[h], k[h], NT) * decay[h], 0.0) for h in hs]
        tinv = [eye + n for n in npow]
        for _ in range(5):
            npow = [_dot3(n, n, NN) for n in npow]
            tinv = [t + _dot3(t, n, NN) for t, n in zip(tinv, npow)]
        uu = [_dot3(tinv[h], v[h] * beta[h], NN) for h in hs]
        ww = [_dot3(tinv[h], kb[h] * gam[h], NN) for h in hs]
        pp = [jnp.where(_tri(cc, "lower"), _dotb(q[h], k[h], NT) * decay[h], 0.0) for h in hs]
        for h in hs:
            u_ref[:, sl[h]] = uu[h]
            w_ref[:, sl[h]] = ww[h]
            kd_ref[:, sl[h]] = k[h] * jnp.exp(gc_last[h] - gc_col[h])
            qg_ref[:, sl[h]] = q[h] * gam[h]
            ti_ref[h] = tinv[h]
            p_ref[h] = pp[h]

    row = lambda off: pl.BlockSpec((cc, WIDTH), lambda n: (n, off))
    mat = pl.BlockSpec((HEADS, cc, cc), lambda n: (0, n, 0))
    tw = jax.ShapeDtypeStruct((t, WIDTH), F32)
    hm = jax.ShapeDtypeStruct((HEADS, t, cc), F32)
    return pl.pallas_call(
        body, name=name,
        out_shape=(tw, tw, tw, tw, hm, hm),
        grid=(nc,),
        in_specs=[row(0), row(1), row(2), pl.BlockSpec((cc, 128), lambda n: (n, 0))],
        out_specs=(row(0), row(0), row(0), row(0), mat, mat),
        compiler_params=_params(("parallel",)),
    )(act, act, act, gates)


def _dn_scan_fwd(u, w, kd, qg, p, gates, name):
    t = u.shape[0]
    cc = DN_CHUNK
    nc = t // cc

    def body(u_ref, w_ref, kd_ref, qg_ref, p_ref, g_ref, o_ref, sh_ref, s_ref):
        n = pl.program_id(0)

        @pl.when(n == 0)
        def _():
            s_ref[...] = jnp.zeros_like(s_ref)

        glast = jnp.exp(g_ref[cc - 1:cc, :])
        hs = range(HEADS)
        sl = [slice(h * HEAD_DIM, (h + 1) * HEAD_DIM) for h in hs]
        s = [s_ref[h] for h in hs]
        sb = [a.astype(BF16) for a in s]
        vn = [u_ref[:, sl[h]] - _dot(w_ref[:, sl[h]].astype(BF16), sb[h], NN) for h in hs]
        vnb = [a.astype(BF16) for a in vn]
        o_state = [_dot(qg_ref[:, sl[h]].astype(BF16), sb[h], NN) for h in hs]
        o_local = [_dot(p_ref[h].astype(BF16), vnb[h], NN) for h in hs]
        s_add = [_dot(kd_ref[:, sl[h]].astype(BF16), vnb[h], TN) for h in hs]
        for h in hs:
            o_ref[:, sl[h]] = o_state[h] + o_local[h]
            sh_ref[0, h] = s[h]
            s_ref[h] = glast[:, h:h + 1] * s[h] + s_add[h]

    row = pl.BlockSpec((cc, WIDTH), lambda n: (n, 0))
    return pl.pallas_call(
        body, name=name,
        out_shape=(jax.ShapeDtypeStruct((t, WIDTH), F32),
                   jax.ShapeDtypeStruct((nc, HEADS, HEAD_DIM, HEAD_DIM), F32)),
        grid=(nc,),
        in_specs=[row, row, row, row, pl.BlockSpec((HEADS, cc, cc), lambda n: (0, n, 0)),
                  pl.BlockSpec((cc, 128), lambda n: (n, 0))],
        out_specs=(row, pl.BlockSpec((1, HEADS, HEAD_DIM, HEAD_DIM), lambda n: (n, 0, 0, 0))),
        scratch_shapes=[pltpu.VMEM((HEADS, HEAD_DIM, HEAD_DIM), F32)],
        compiler_params=_params(("arbitrary",)),
    )(u, w, kd, qg, p, gates)


def _dn_scan_bwd(do, w, kd, qg, p, gates, name):
    t = do.shape[0]
    cc = DN_CHUNK
    nc = t // cc

    def body(do_ref, w_ref, kd_ref, qg_ref, p_ref, g_ref, dvn_ref, dsh_ref, ds_ref):
        n = pl.program_id(0)

        @pl.when(n == 0)
        def _():
            ds_ref[...] = jnp.zeros_like(ds_ref)

        glast = jnp.exp(g_ref[cc - 1:cc, :])
        hs = range(HEADS)
        sl = [slice(h * HEAD_DIM, (h + 1) * HEAD_DIM) for h in hs]
        ds = [ds_ref[h] for h in hs]
        dob = [do_ref[:, sl[h]].astype(BF16) for h in hs]
        dvn = [_dot(p_ref[h].astype(BF16), dob[h], TN) + _dot(kd_ref[:, sl[h]].astype(BF16), ds[h].astype(BF16), NN)
               for h in hs]
        ds_q = [_dot(qg_ref[:, sl[h]].astype(BF16), dob[h], TN) for h in hs]
        ds_w = [_dot(w_ref[:, sl[h]].astype(BF16), dvn[h].astype(BF16), TN) for h in hs]
        for h in hs:
            dvn_ref[:, sl[h]] = dvn[h]
            dsh_ref[0, h] = ds[h]
            ds_ref[h] = ds_q[h] + glast[:, h:h + 1] * ds[h] - ds_w[h]

    row = pl.BlockSpec((cc, WIDTH), lambda n: (nc - 1 - n, 0))
    return pl.pallas_call(
        body, name=name,
        out_shape=(jax.ShapeDtypeStruct((t, WIDTH), F32),
                   jax.ShapeDtypeStruct((nc, HEADS, HEAD_DIM, HEAD_DIM), F32)),
        grid=(nc,),
        in_specs=[row, row, row, row, pl.BlockSpec((HEADS, cc, cc), lambda n: (0, nc - 1 - n, 0)),
                  pl.BlockSpec((cc, 128), lambda n: (nc - 1 - n, 0))],
        out_specs=(row, pl.BlockSpec((1, HEADS, HEAD_DIM, HEAD_DIM), lambda n: (nc - 1 - n, 0, 0, 0))),
        scratch_shapes=[pltpu.VMEM((HEADS, HEAD_DIM, HEAD_DIM), F32)],
        compiler_params=_params(("arbitrary",)),
    )(do, w, kd, qg, p, gates)


def _dn_local_bwd(act, gates, u, w, kd, qg, tinv, p, sh, dsh, dvn, do, name):
    t = act.shape[0]
    cc = DN_CHUNK
    nc = t // cc

    def body(q_ref, k_ref, v_ref, g_ref, u_ref, w_ref, kd_ref, qg_ref, ti_ref, p_ref, s_ref, ds_ref,
             dvn_ref, do_ref, dq_ref, dk_ref, dv_ref, dg_ref):
        gates_v = g_ref[...]
        lower, strict = _tri(cc, "lower"), _tri(cc, "strict")
        ones = jnp.ones((cc, 128), BF16)
        rowc = _iota((cc, 1), 0)
        lane = _iota((cc, 128), 1)
        hs = range(HEADS)
        sl = [slice(h * HEAD_DIM, (h + 1) * HEAD_DIM) for h in hs]
        q, k, v, uu, ww, kd, qg, dvn, do = ([r[:, s] for s in sl] for r in (
            q_ref, k_ref, v_ref, u_ref, w_ref, kd_ref, qg_ref, dvn_ref, do_ref))
        gc_col, gc_last, beta, decay = zip(*[_dn_chunk_common(gates_v, h) for h in hs])
        gam = [jnp.exp(g) for g in gc_col]
        kb = [k[h] * beta[h] for h in hs]
        s_in = [s_ref[0, h] for h in hs]
        ds_out = [ds_ref[0, h] for h in hs]
        tinv = [ti_ref[h] for h in hs]

        a = [jnp.where(strict, _dotb(kb[h], k[h], NT) * decay[h], 0.0) for h in hs]
        vn = [uu[h] - _dotb(ww[h], s_in[h], NN) for h in hs]
        dqg = [_dotb(do[h], s_in[h], NT) for h in hs]
        dw = [-_dotb(dvn[h], s_in[h], NT) for h in hs]
        dp = [jnp.where(lower, _dotb(do[h], vn[h], NT), 0.0) for h in hs]
        dkd = [_dotb(vn[h], ds_out[h], NT) for h in hs]
        dru = [_dot3(tinv[h], dvn[h], TN) for h in hs]
        drw = [_dot3(tinv[h], dw[h], TN) for h in hs]
        da = [-jnp.where(strict, _dotb(dru[h], uu[h], NT) + _dotb(drw[h], ww[h], NT), 0.0) for h in hs]
        dad = [da[h] * decay[h] for h in hs]
        dpd = [dp[h] * decay[h] for h in hs]
        dkb = [_dotb(dad[h], k[h], NN) + gam[h] * drw[h] for h in hs]
        dk = [_dotb(dad[h], kb[h], TN) + _dotb(dpd[h], q[h], TN) + beta[h] * dkb[h]
              + jnp.exp(gc_last[h] - gc_col[h]) * dkd[h] for h in hs]
        dq = [gam[h] * dqg[h] + _dotb(dpd[h], k[h], NN) for h in hs]
        gm = [da[h] * a[h] + dp[h] * p_ref[h] for h in hs]
        colsum = [_dot_xr(gm[h], ones, TN)[:, 0:1] for h in hs]

        dgates = jnp.zeros((cc, 128), F32)
        for h in hs:
            dk_ref[:, sl[h]] = dk[h]
            dq_ref[:, sl[h]] = dq[h]
            dv_ref[:, sl[h]] = beta[h] * dru[h]
            dbeta = (jnp.sum(dkb[h] * k[h], axis=1, keepdims=True)
                     + jnp.sum(dru[h] * v[h], axis=1, keepdims=True))
            rkd = jnp.sum(dkd[h] * kd[h], axis=1, keepdims=True)
            dgc = (jnp.sum(gm[h], axis=1, keepdims=True) - colsum[h]
                   + jnp.sum(dqg[h] * qg[h], axis=1, keepdims=True)
                   + jnp.sum(drw[h] * kb[h], axis=1, keepdims=True) * gam[h] - rkd)
            tail = jnp.sum(rkd, axis=0, keepdims=True) + jnp.exp(gc_last[h]) * jnp.sum(
                jnp.sum(s_in[h] * ds_out[h], axis=1, keepdims=True), axis=0, keepdims=True)
            dgc = dgc + jnp.where(rowc == cc - 1, tail, 0.0)
            dgates = dgates + jnp.where(lane == h, dgc, 0.0) + jnp.where(lane == h + HEADS, dbeta, 0.0)
        dg_ref[...] = dgates

    row = lambda off: pl.BlockSpec((cc, WIDTH), lambda n: (n, off))
    mat = pl.BlockSpec((HEADS, cc, cc), lambda n: (0, n, 0))
    st = pl.BlockSpec((1, HEADS, HEAD_DIM, HEAD_DIM), lambda n: (n, 0, 0, 0))
    gl = pl.BlockSpec((cc, 128), lambda n: (n, 0))
    tw = jax.ShapeDtypeStruct((t, WIDTH), F32)
    return pl.pallas_call(
        body, name=name,
        out_shape=(tw, tw, tw, jax.ShapeDtypeStruct((t, 128), F32)),
        grid=(nc,),
        in_specs=[row(0), row(1), row(2), gl, row(0), row(0), row(0), row(0), mat, mat, st, st, row(0), row(0)],
        out_specs=(row(0), row(0), row(0), gl),
        compiler_params=_params(("parallel",)),
    )(act, act, act, gates, u, w, kd, qg, tinv, p, sh, dsh, dvn, do)


def _dn_post_fwd(o, gate, w, name):
    t = o.shape[0]
    tr = _tile(t, 512)

    def body(o_ref, g_ref, w_ref, y_ref):
        for h in range(HEADS):
            sl = slice(h * HEAD_DIM, (h + 1) * HEAD_DIM)
            ov, gv = o_ref[:, sl], g_ref[:, sl]
            r = lax.rsqrt(jnp.mean(ov * ov, axis=1, keepdims=True) + EPS)
            y_ref[:, sl] = (ov * r * w_ref[...] * (gv * _sigmoid(gv))).astype(BF16)

    blk = pl.BlockSpec((tr, WIDTH), lambda i: (i, 0))
    return pl.pallas_call(
        body, name=name,
        out_shape=jax.ShapeDtypeStruct((t, WIDTH), BF16),
        grid=(t // tr,),
        in_specs=[blk, blk, pl.BlockSpec((1, HEAD_DIM), lambda i: (0, 0))],
        out_specs=blk,
        compiler_params=_params(("parallel",)),
    )(o, gate, w)


def _dn_post_bwd(dy, o, gate, w, name):
    t = o.shape[0]
    tr = _tile(t, 512)

    def body(dy_ref, o_ref, g_ref, w_ref, do_ref, dg_ref, dw_ref):
        i = pl.program_id(0)

        @pl.when(i == 0)
        def _():
            dw_ref[...] = jnp.zeros_like(dw_ref)

        dw = jnp.zeros((1, HEAD_DIM), F32)
        for h in range(HEADS):
            sl = slice(h * HEAD_DIM, (h + 1) * HEAD_DIM)
            ov, gv, dyv = o_ref[:, sl], g_ref[:, sl], dy_ref[:, sl].astype(F32)
            r = lax.rsqrt(jnp.mean(ov * ov, axis=1, keepdims=True) + EPS)
            oh = ov * r
            sg = _sigmoid(gv)
            dg_ref[:, sl] = (dyv * oh * w_ref[...] * (sg * (1.0 + gv * (1.0 - sg)))).astype(BF16)
            dn = dyv * (gv * sg)
            doh = dn * w_ref[...]
            do_ref[:, sl] = r * (doh - oh * jnp.mean(doh * oh, axis=1, keepdims=True))
            dw = dw + jnp.sum(dn * oh, axis=0, keepdims=True)
        dw_ref[...] += dw

    blk = pl.BlockSpec((tr, WIDTH), lambda i: (i, 0))
    return pl.pallas_call(
        body, name=name,
        out_shape=(jax.ShapeDtypeStruct((t, WIDTH), F32), jax.ShapeDtypeStruct((t, WIDTH), BF16),
                   jax.ShapeDtypeStruct((1, HEAD_DIM), F32)),
        grid=(t // tr,),
        in_specs=[blk, blk, blk, pl.BlockSpec((1, HEAD_DIM), lambda i: (0, 0))],
        out_specs=(blk, blk, pl.BlockSpec((1, HEAD_DIM), lambda i: (0, 0))),
        compiler_params=_params(("arbitrary",)),
    )(dy, o, gate, w)


def _sb_scores(q, k_ref, qi, it, carry_b, uincl):
    bk = ATT_BLOCK
    scale = HEAD_DIM ** -0.5
    js = [qi - SB_GROUP * it - g for g in range(SB_GROUP)]
    rows = [pl.ds(pl.multiple_of(jnp.maximum(j, 0) * bk, bk), bk) for j in js]
    ks = [k_ref[r, :] for r in rows]
    z = [_dot(q, k, NT) * scale for k in ks]
    qpos = qi * bk + _iota((bk, bk), 0)
    col = _iota((bk, bk), 1)
    mask = [jnp.logical_and(j * bk + col < qpos, j >= 0) for j in js]
    soft = [_log1pexp_neg_abs(a) for a in z]
    lk_full = [-(jnp.maximum(a, 0.0) + s) for a, s in zip(z, soft)]
    lk = [jnp.where(m, a, 0.0) for m, a in zip(mask, lk_full)]
    ls = [jnp.minimum(a, 0.0) - s for a, s in zip(z, soft)]
    incl = [_dot_xr2(a, uincl, NN) for a in lk]
    weights = []
    for g in range(SB_GROUP):
        weights.append(jnp.where(mask[g], jnp.exp(ls[g] + (carry_b + incl[g] - lk[g])), 0.0))
        carry_b = carry_b + incl[g][:, 0:1]
    return rows, ks, weights, mask, lk_full, ls, carry_b


def _sb_more(qi, carry):
    it, cb = carry[0], carry[1]
    return jnp.logical_and(SB_GROUP * it <= qi, jnp.max(cb) > SB_LOG_ZERO)


def _sb_steps(nq):
    def when():
        h, i = pl.program_id(0), pl.program_id(1)
        return (jnp.logical_and(h == 0, i == 0), jnp.logical_and(h == HEADS // 2, i == 0),
                jnp.logical_and(h == HEADS - 1, i == nq - 1))
    return when


def _sb_fwd(qkv, name, comm=None):
    t = qkv.shape[0]
    bk = ATT_BLOCK

    def body(q_ref, k_ref, v_ref, o_ref):
        qi = pl.program_id(1)
        q = q_ref[...]
        uincl = jnp.where(_tri(bk, "lower"), 1.0, 0.0).astype(BF16)

        def step(carry):
            it, cb, acc = carry
            rows, _, weights, _, _, _, cb = _sb_scores(q, k_ref, qi, it, cb, uincl)
            for r, a in zip(rows, weights):
                acc = acc + _dot(a.astype(BF16), v_ref[r, :], NN)
            return it + 1, cb, acc

        init = (jnp.int32(0), jnp.zeros((bk, 1), F32), jnp.zeros((bk, HEAD_DIM), F32))
        _, _, acc = lax.while_loop(functools.partial(_sb_more, qi), step, init)
        o_ref[...] = acc

    (o,), extra = _host_call(
        body, name, comm, _sb_steps(t // bk), [jax.ShapeDtypeStruct((t, WIDTH), F32)], (HEADS, t // bk),
        [pl.BlockSpec((bk, HEAD_DIM), lambda h, i: (i, h)),
         pl.BlockSpec((t, HEAD_DIM), lambda h, i: (0, HEADS + h)),
         pl.BlockSpec((t, HEAD_DIM), lambda h, i: (0, 2 * HEADS + h))],
        [pl.BlockSpec((bk, HEAD_DIM), lambda h, i: (i, h))], [], ("parallel", "arbitrary"), (qkv, qkv, qkv))
    return o, extra


def _sb_bwd(qkv, o, do, name, comm=None):
    t = qkv.shape[0]
    bk = ATT_BLOCK
    scale = HEAD_DIM ** -0.5

    def body(q_ref, k_ref, v_ref, o_ref, do_ref, dq_ref, dk_ref, dv_ref):
        qi = pl.program_id(1)

        @pl.when(qi == 0)
        def _():
            dk_ref[...] = jnp.zeros_like(dk_ref)
            dv_ref[...] = jnp.zeros_like(dv_ref)

        q = q_ref[...]
        dov = do_ref[...]
        dob = dov.astype(BF16)
        do1, do2 = _split2(dov)
        dsum = jnp.sum(dov * o_ref[...], axis=1, keepdims=True)
        uincl = jnp.where(_tri(bk, "lower"), 1.0, 0.0).astype(BF16)

        def step(carry):
            it, cb, ce, dq = carry
            rows, ks, weights, mask, lk_full, ls, cb = _sb_scores(q, k_ref, qi, it, cb, uincl)
            ab = [a.astype(BF16) for a in weights]
            vs = [v_ref[r, :] for r in rows]
            dla = [a.astype(F32) * (_dot(do1, v, NT) + _dot(do2, v, NT)) for a, v in zip(ab, vs)]
            suf = [_dot_xr2(a, uincl, NN) for a in dla]
            for g in range(SB_GROUP):
                e = dsum - (ce + suf[g])
                ce = ce + suf[g][:, 0:1]
                dz = jnp.where(mask[g], dla[g] * jnp.exp(lk_full[g]) - e * jnp.exp(ls[g]), 0.0)
                dzb = (dz * scale).astype(BF16)
                dq = dq + _dot(dzb, ks[g], NN)
                dk_ref[rows[g], :] += _dot(dzb, q, TN)
                dv_ref[rows[g], :] += _dot(ab[g], dob, TN)
            return it + 1, cb, ce, dq

        zc = jnp.zeros((bk, 1), F32)
        init = (jnp.int32(0), zc, zc, jnp.zeros((bk, HEAD_DIM), F32))
        dq_ref[...] = lax.while_loop(functools.partial(_sb_more, qi), step, init)[3]

    tw = jax.ShapeDtypeStruct((t, WIDTH), F32)
    qb = pl.BlockSpec((bk, HEAD_DIM), lambda h, i: (i, h))
    full = lambda off: pl.BlockSpec((t, HEAD_DIM), lambda h, i: (0, off + h))
    return _host_call(
        body, name, comm, _sb_steps(t // bk), [tw, tw, tw], (HEADS, t // bk),
        [qb, full(HEADS), full(2 * HEADS), qb, qb], [qb, full(0), full(0)], [], ("parallel", "arbitrary"),
        (qkv, qkv, qkv, o, do))


def _merge_fwd(pd, ps, gl, name):
    t = pd.shape[0]
    tr, tc = _tile(t, 512), 512
    nj = D_MODEL // tc

    def body(pd_ref, ps_ref, gd_ref, gs_ref, o_ref):
        o_ref[...] = (_sigmoid(gd_ref[...]) * pd_ref[...] + _sigmoid(gs_ref[...]) * ps_ref[...]).astype(BF16)

    blk = lambda off: pl.BlockSpec((tr, tc), lambda i, j: (i, j + off))
    return pl.pallas_call(
        body, name=name,
        out_shape=jax.ShapeDtypeStruct((t, D_MODEL), BF16),
        grid=(t // tr, nj),
        in_specs=[blk(0), blk(0), blk(0), blk(nj)],
        out_specs=blk(0),
        compiler_params=_params(("parallel", "parallel")),
    )(pd, ps, gl, gl)


def _merge_bwd(dm, pd, ps, gl, name):
    t = pd.shape[0]
    tr, tc = _tile(t, 512), 512
    nj = D_MODEL // tc

    def body(dm_ref, pd_ref, ps_ref, gd_ref, gs_ref, dpd_ref, dps_ref, dgd_ref, dgs_ref):
        dmv = dm_ref[...]
        sd, ss = _sigmoid(gd_ref[...]), _sigmoid(gs_ref[...])
        dpd_ref[...] = (dmv * sd).astype(BF16)
        dps_ref[...] = (dmv * ss).astype(BF16)
        dgd_ref[...] = (dmv * pd_ref[...] * sd * (1.0 - sd)).astype(BF16)
        dgs_ref[...] = (dmv * ps_ref[...] * ss * (1.0 - ss)).astype(BF16)

    blk = lambda off: pl.BlockSpec((tr, tc), lambda i, j: (i, j + off))
    out = jax.ShapeDtypeStruct((t, D_MODEL), BF16)
    return pl.pallas_call(
        body, name=name,
        out_shape=(out, out, out, out),
        grid=(t // tr, nj),
        in_specs=[blk(0), blk(0), blk(0), blk(0), blk(nj)],
        out_specs=(blk(0), blk(0), blk(0), blk(0)),
        compiler_params=_params(("parallel", "parallel")),
    )(dm, pd, ps, gl, gl)


def _local_step(x, target, wts, plan=None):
    n1 = _rmsnorm_fwd(x, wts["norm1_w"], "norm1_fwd")
    qkv_pre = _matmul(n1, wts["w_dnqkv_t"], "nt", F32, "in_dnqkv")
    hgate = _matmul(n1, wts["w_dngate_t"], "nt", F32, "in_dngate")
    sbqkv = _matmul(n1, wts["w_sbqkv_t"], "nt", BF16, "in_sbqkv")
    gl = _matmul(n1, wts["w_gl_t"], "nt", F32, "in_gl")
    hab = _matmul(n1, wts["w_ab_t"], "nt", F32, "in_ab")

    act = _dn_pre_fwd(qkv_pre, wts["dn_conv_w"], "dn_pre_fwd")
    gates = _dn_gates_fwd(hab, wts["alog"], wts["dtb"], "dn_gates_fwd")
    u, w, kd, qg, tinv, p = _dn_local_fwd(act, gates, "dn_local_fwd")
    o_dn, sh = _dn_scan_fwd(u, w, kd, qg, p, gates, "dn_scan_fwd")
    y_dn = _dn_post_fwd(o_dn, hgate, wts["dn_norm_w"], "dn_post_fwd")

    o_sb, late = _sb_fwd(sbqkv, "sb_fwd", comm=plan.late_gather() if plan else None)
    if plan:
        wts = {**wts, **plan.late_weights(late)}

    pd = _matmul(y_dn, wts["w_proj_dn"], "nn", F32, "proj_dn")
    ps = _matmul(o_sb, wts["w_proj_sb"], "nn", F32, "proj_sb")
    mixed = _merge_fwd(pd, ps, gl, "merge_fwd")
    x1 = _matmul(mixed, wts["w_out"], "nn", F32, "out_proj", add=x)

    n2 = _rmsnorm_fwd(x1, wts["norm2_w"], "norm2_fwd")
    upre = _matmul(n2, wts["ffn_w_up_t"], "nt", F32, "ffn_up")
    fact = _ffn_act_fwd(upre, wts["ffn_conv_w"], "ffn_act_fwd")
    x2 = _matmul(fact, wts["ffn_w_down"], "nn", F32, "ffn_down", add=x1)

    dx2, g_normf, loss = _final_loss(x2, target, wts["norm_f_w"], "final_loss")

    dfact = _matmul(dx2, wts["ffn_w_down"], "nt", BF16, "ffn_down_dx")
    g_wdown = _matmul(fact, dx2, "tn", BF16, "ffn_down_dw")
    dgc, duc, dwg, dwu = _ffn_act_bwd(dfact, upre, wts["ffn_conv_w"], "ffn_act_bwd")
    g_fconv = jnp.concatenate([dwg, dwu], axis=1)
    dupre = _conv_bwd_data([dgc, duc], wts["ffn_conv_w"], FFN_CONV, BF16, "ffn_conv_bwd")
    dn2 = _matmul(dupre, wts["ffn_w_up_t"], "nn", F32, "ffn_up_dx")
    g_wup = _matmul(dupre, n2, "tn", BF16, "ffn_up_dw")
    dx1, g_norm2 = _rmsnorm_bwd(dn2, x1, wts["norm2_w"], dx2, "norm2_bwd")

    dmixed = _matmul(dx1, wts["w_out"], "nt", F32, "out_proj_dx")
    g_wout = _matmul(mixed, dx1, "tn", BF16, "out_proj_dw")
    dpd, dps, dgd, dgs = _merge_bwd(dmixed, pd, ps, gl, "merge_bwd")
    dy_dn = _matmul(dpd, wts["w_proj_dn"], "nt", F32, "proj_dn_dx")
    g_wpd = _matmul(y_dn, dpd, "tn", BF16, "proj_dn_dw")
    do_sb = _matmul(dps, wts["w_proj_sb"], "nt", F32, "proj_sb_dx")
    g_wps = _matmul(o_sb, dps, "tn", BF16, "proj_sb_dw")
    grads = dict(w_proj_dn=g_wpd, w_proj_sb=g_wps, w_out=g_wout, ffn_w_up_t=g_wup, ffn_w_down=g_wdown)

    (dsq, dsk, dsv), got_early = _sb_bwd(sbqkv, o_sb, do_sb, "sb_bwd",
                                         comm=plan.early_grads(grads) if plan else None)

    do_dn, dhgate, g_dnnorm = _dn_post_bwd(dy_dn, o_dn, hgate, wts["dn_norm_w"], "dn_post_bwd")
    dvn, dsh = _dn_scan_bwd(do_dn, w, kd, qg, p, gates, "dn_scan_bwd")
    dq, dk, dv, dgates = _dn_local_bwd(act, gates, u, w, kd, qg, tinv, p, sh, dsh, dvn, do_dn, "dn_local_bwd")
    dhab, g_alog, g_dtb = _dn_gates_bwd(dgates, hab, wts["alog"], wts["dtb"], "dn_gates_bwd")
    dcv, g_dnconv = _dn_pre_bwd(dq, dk, dv, qkv_pre, wts["dn_conv_w"], "dn_pre_bwd")
    dqkv_pre = _conv_bwd_data([dcv], wts["dn_conv_w"], DN_CONV, BF16, "dn_conv_bwd")

    dh = jnp.concatenate([dqkv_pre, dhgate, dsq.astype(BF16), dsk.astype(BF16), dsv.astype(BF16), dgd, dgs], axis=1)
    w_main_t = jnp.concatenate([wts["w_dnqkv_t"], wts["w_dngate_t"], wts["w_sbqkv_t"], wts["w_gl_t"]], axis=0)
    g_wmain = _matmul(dh, n1, "tn", BF16, "in_dw_main")
    g_wab = _matmul(dhab, n1, "tn", BF16, "in_dw_ab")
    grads.update(w_main_t=g_wmain, w_ab_t=g_wab, dn_conv_w=g_dnconv, alog=g_alog, dtb=g_dtb, dn_norm_w=g_dnnorm,
                 norm2_w=g_norm2, ffn_conv_w=g_fconv, norm_f_w=g_normf)
    got_late = []
    if plan:
        dn1, got_late = _matmul(dh, w_main_t, "nn", F32, "in_dx_main", comm=plan.late_grads(grads, loss))
    else:
        dn1 = _matmul(dh, w_main_t, "nn", F32, "in_dx_main")
    dn1 = _matmul(dhab, wts["w_ab_t"], "nn", F32, "in_dx_ab", add=dn1)
    grad_x, g_norm1 = _rmsnorm_bwd(dn1, x, wts["norm1_w"], dx1, "norm1_bwd")
    grads["norm1_w"] = g_norm1
    return loss, grad_x, grads, got_early, got_late


HBM_SPEC = pl.BlockSpec(memory_space=pltpu.HBM)


def _mesh_pos():
    x, y, c = lax.axis_index("x"), lax.axis_index("y"), lax.axis_index("c")
    return x, y, c, 4 * x + 2 * y + c


def _peer(k):
    x, y, c, _ = _mesh_pos()
    px = 1 - x if k & 4 else x
    py = 1 - y if k & 2 else y
    pc = 1 - c if k & 1 else c
    return (px, py, pc), 4 * px + 2 * py + pc


def _rcopy(src, dst, send, recv, a, s, peer):
    return pltpu.make_async_remote_copy(src_ref=src, dst_ref=dst, send_sem=send.at[a, s], recv_sem=recv.at[a, s],
                                        device_id=peer, device_id_type=pl.DeviceIdType.MESH)


class _Gather:
    ICI = (2, 4, 6)

    def __init__(self, shards):
        self.args = list(shards)
        self.n = len(shards)
        self.out_shape = [jax.ShapeDtypeStruct((N_DEV,) + s.shape, s.dtype) for s in shards]
        self.scratch = [pltpu.SemaphoreType.DMA((self.n, N_DEV - 1)), pltpu.SemaphoreType.DMA((self.n, N_DEV - 1)),
                        pltpu.SemaphoreType.DMA((self.n,))]

    def _first(self, ins, outs, send, recv, a):
        me = _mesh_pos()[3]
        out, got = [], []
        for s, k in enumerate((1,) + self.ICI):
            peer, pidx = _peer(k)
            out.append(_rcopy(ins[a], outs[a].at[me], send, recv, a, s, peer))
            got.append(_rcopy(ins[a], outs[a].at[pidx], send, recv, a, s, peer))
        return out, got

    def _forward(self, ins, outs, send, recv, a):
        sib = _peer(1)[0]
        out, got = [], []
        for s, k in enumerate(self.ICI):
            held = outs[a].at[_peer(k)[1]]
            out.append(_rcopy(held, held, send, recv, a, 4 + s, sib))
            other = outs[a].at[_peer(k | 1)[1]]
            got.append(_rcopy(other, other, send, recv, a, 4 + s, sib))
        return out, got

    def start(self, ins, outs, sems):
        send, recv, loc = sems
        me = _mesh_pos()[3]
        for a in range(self.n):
            pltpu.make_async_copy(ins[a], outs[a].at[me], loc.at[a]).start()
            for cp in self._first(ins, outs, send, recv, a)[0]:
                cp.start()

    def mid(self, ins, outs, sems):
        send, recv, _ = sems
        for a in range(self.n):
            arrivals = self._first(ins, outs, send, recv, a)[1]
            for s, cp in enumerate(self._forward(ins, outs, send, recv, a)[0]):
                arrivals[1 + s].wait_recv()
                cp.start()

    def finish(self, ins, outs, sems):
        send, recv, loc = sems
        me = _mesh_pos()[3]
        for a in range(self.n):
            first_out, first_got = self._first(ins, outs, send, recv, a)
            fwd_out, fwd_got = self._forward(ins, outs, send, recv, a)
            first_got[0].wait_recv()
            for cp in fwd_got:
                cp.wait_recv()
            for cp in first_out + fwd_out:
                cp.wait_send()
            pltpu.make_async_copy(ins[a], outs[a].at[me], loc.at[a]).wait()


class _Exchange:
    def __init__(self, slabs, gathered=()):
        self.args = list(slabs) + list(gathered)
        self.n_slab = len(slabs)
        self.n = len(self.args)
        self.out_shape = ([jax.ShapeDtypeStruct(s.shape, s.dtype) for s in slabs]
                          + [jax.ShapeDtypeStruct((N_DEV,) + s.shape, s.dtype) for s in gathered])
        self.scratch = [pltpu.SemaphoreType.DMA((self.n, N_DEV - 1)), pltpu.SemaphoreType.DMA((self.n, N_DEV - 1)),
                        pltpu.SemaphoreType.DMA((self.n,))]

    def _copies(self, ins, outs, send, recv, a):
        me = _mesh_pos()[3]
        out, got = [], []
        for k in range(1, N_DEV):
            peer, pidx = _peer(k)
            src = ins[a].at[pidx] if a < self.n_slab else ins[a]
            out.append(_rcopy(src, outs[a].at[me], send, recv, a, k - 1, peer))
            got.append(_rcopy(src, outs[a].at[pidx], send, recv, a, k - 1, peer))
        return out, got

    def _local(self, ins, outs, loc, a):
        me = _mesh_pos()[3]
        return pltpu.make_async_copy(ins[a].at[me] if a < self.n_slab else ins[a], outs[a].at[me], loc.at[a])

    def start(self, ins, outs, sems):
        send, recv, loc = sems
        for a in range(self.n):
            self._local(ins, outs, loc, a).start()
            for cp in self._copies(ins, outs, send, recv, a)[0]:
                cp.start()

    def mid(self, ins, outs, sems):
        pass

    def finish(self, ins, outs, sems):
        send, recv, loc = sems
        for a in range(self.n):
            out, got = self._copies(ins, outs, send, recv, a)
            for cp in got:
                cp.wait_recv()
            for cp in out:
                cp.wait_send()
            self._local(ins, outs, loc, a).wait()


def _comm_call(comm, name):
    n = comm.n

    def body(*refs):
        ins, outs, sems = refs[:n], refs[n:2 * n], refs[2 * n:]
        comm.start(ins, outs, sems)
        comm.mid(ins, outs, sems)
        comm.finish(ins, outs, sems)

    return pl.pallas_call(
        body, name=name, out_shape=comm.out_shape, in_specs=[HBM_SPEC] * n, out_specs=[HBM_SPEC] * n,
        scratch_shapes=comm.scratch,
    )(*comm.args)


def _hosted(body, comm, n_in, n_out, when):
    if comm is None:
        return body

    def wrapped(*refs):
        ins, c_ins = refs[:n_in], refs[n_in:n_in + comm.n]
        o0 = n_in + comm.n
        outs, c_outs = refs[o0:o0 + n_out], refs[o0 + n_out:o0 + n_out + comm.n]
        scratch, sems = refs[o0 + n_out + comm.n:len(refs) - 3], refs[len(refs) - 3:]
        first, middle, last = when()

        @pl.when(first)
        def _():
            comm.start(c_ins, c_outs, sems)

        body(*ins, *outs, *scratch)

        @pl.when(middle)
        def _():
            comm.mid(c_ins, c_outs, sems)

        @pl.when(last)
        def _():
            comm.finish(c_ins, c_outs, sems)

    return wrapped


def _host_call(body, name, comm, when, out_shape, grid, in_specs, out_specs, scratch_shapes, sem, args):
    n_in, n_out = len(in_specs), len(out_specs)
    if comm is None:
        res = pl.pallas_call(body, name=name, out_shape=out_shape, grid=grid, in_specs=in_specs, out_specs=out_specs,
                             scratch_shapes=scratch_shapes, compiler_params=_params(sem))(*args)
        return list(res), []
    res = pl.pallas_call(
        _hosted(body, comm, n_in, n_out, when), name=name,
        out_shape=list(out_shape) + comm.out_shape, grid=grid,
        in_specs=list(in_specs) + [HBM_SPEC] * comm.n, out_specs=list(out_specs) + [HBM_SPEC] * comm.n,
        scratch_shapes=list(scratch_shapes) + comm.scratch,
        compiler_params=_params(("arbitrary",) * len(grid)),
    )(*args, *comm.args)
    return list(res[:n_out]), list(res[n_out:])


def _adamw(parts, w, m, v, name):
    rows, cols = w.shape
    tr, tc = rows, cols
    for cand in (128, 176):
        if rows > cand and rows % cand == 0:
            tr = cand
            break
    if tr == rows and rows > 512:
        tc = _tile(cols, 256)

    def body(p_ref, w_ref, m_ref, v_ref, g_ref, d_ref, mo_ref, vo_ref):
        g = p_ref[0].astype(F32)
        for s in range(1, N_DEV):
            g = g + p_ref[s].astype(F32)
        mn = ADAM_B1 * m_ref[...] + (1.0 - ADAM_B1) * g
        vn = ADAM_B2 * v_ref[...] + (1.0 - ADAM_B2) * (g * g)
        m_hat = mn / (1.0 - ADAM_B1 ** ADAM_STEP)
        v_hat = vn / (1.0 - ADAM_B2 ** ADAM_STEP)
        g_ref[...] = g
        d_ref[...] = -ADAM_LR * (m_hat / (jnp.sqrt(v_hat) + ADAM_EPS) + ADAM_WD * w_ref[...])
        mo_ref[...] = mn
        vo_ref[...] = vn

    blk = pl.BlockSpec((tr, tc), lambda i, j: (i, j))
    out = jax.ShapeDtypeStruct((rows, cols), F32)
    return pl.pallas_call(
        body, name=name,
        out_shape=(out, out, out, out),
        grid=(rows // tr, cols // tc),
        in_specs=[pl.BlockSpec((N_DEV, tr, tc), lambda i, j: (0, i, j)), blk, blk, blk],
        out_specs=(blk, blk, blk, blk),
        compiler_params=_params(("parallel", "parallel")),
    )(parts, w, m, v)


CONV_PACK = 8 * 1024
WEIGHT_ORDER = ("norm1_w", "w_in", "dn_conv_w", "dn_A_log", "dn_dt_bias", "dn_norm_w", "w_proj_dn", "w_proj_sb",
                "w_out", "norm2_w", "ffn_w_up", "ffn_conv_w", "ffn_w_down", "norm_f_w")


def _cols_to_slabs(g):
    r, c8 = g.shape
    return g.reshape(r, N_DEV, c8 // N_DEV).transpose(1, 0, 2)


def _slabs_to_cols(s):
    d, r, c = s.shape
    return s.transpose(1, 0, 2).reshape(r, d * c)


def kernel(x, norm1_w, w_in, dn_conv_w, dn_A_log, dn_dt_bias, dn_norm_w, w_proj_dn, w_proj_sb, w_out, norm2_w, ffn_w_up, ffn_conv_w, ffn_w_down, norm_f_w, loss_target, m_norm1_w, m_w_in, m_dn_conv_w, m_dn_A_log, m_dn_dt_bias, m_dn_norm_w, m_w_proj_dn, m_w_proj_sb, m_w_out, m_norm2_w, m_ffn_w_up, m_ffn_conv_w, m_ffn_w_down, m_norm_f_w, v_norm1_w, v_w_in, v_dn_conv_w, v_dn_A_log, v_dn_dt_bias, v_dn_norm_w, v_w_proj_dn, v_w_proj_sb, v_w_out, v_norm2_w, v_ffn_w_up, v_ffn_conv_w, v_ffn_w_down, v_norm_f_w):
    me = _mesh_pos()[3]
    tr = lambda a: jnp.transpose(a[0])
    w_loc = dict(norm1_w=norm1_w, w_in=tr(w_in), dn_conv_w=dn_conv_w[0], dn_A_log=dn_A_log, dn_dt_bias=dn_dt_bias,
                 dn_norm_w=dn_norm_w, w_proj_dn=w_proj_dn[0], w_proj_sb=w_proj_sb[0], w_out=w_out[0],
                 norm2_w=norm2_w, ffn_w_up=tr(ffn_w_up), ffn_conv_w=ffn_conv_w[0], ffn_w_down=ffn_w_down[0],
                 norm_f_w=norm_f_w[None, :])
    m_loc = dict(norm1_w=m_norm1_w, w_in=tr(m_w_in), dn_conv_w=m_dn_conv_w[0], dn_A_log=m_dn_A_log,
                 dn_dt_bias=m_dn_dt_bias, dn_norm_w=m_dn_norm_w, w_proj_dn=m_w_proj_dn[0], w_proj_sb=m_w_proj_sb[0],
                 w_out=m_w_out[0], norm2_w=m_norm2_w, ffn_w_up=tr(m_ffn_w_up), ffn_conv_w=m_ffn_conv_w[0],
                 ffn_w_down=m_ffn_w_down[0], norm_f_w=m_norm_f_w[None, :])
    v_loc = dict(norm1_w=v_norm1_w, w_in=tr(v_w_in), dn_conv_w=v_dn_conv_w[0], dn_A_log=v_dn_A_log,
                 dn_dt_bias=v_dn_dt_bias, dn_norm_w=v_dn_norm_w, w_proj_dn=v_w_proj_dn[0], w_proj_sb=v_w_proj_sb[0],
                 w_out=v_w_out[0], norm2_w=v_norm2_w, ffn_w_up=tr(v_ffn_w_up), ffn_conv_w=v_ffn_conv_w[0],
                 ffn_w_down=v_ffn_w_down[0], norm_f_w=v_norm_f_w[None, :])

    conv_flat = jnp.concatenate([w_loc["dn_conv_w"].reshape(-1), w_loc["ffn_conv_w"].reshape(-1)])
    n_dn, n_ffn = DN_CONV * 3 * WIDTH // N_DEV, FFN_CONV * 2 * D_FF // N_DEV
    conv_pack = jnp.pad(conv_flat, (0, CONV_PACK - n_dn - n_ffn)).reshape(8, 1024)
    g_in, g_conv = _comm_call(_Gather([w_loc["w_in"].astype(BF16), conv_pack]), "gather_first")
    in_width = g_in.shape[0] * g_in.shape[1]
    w_in_t = g_in.reshape(in_width, D_MODEL)
    g_conv = g_conv.reshape(N_DEV, CONV_PACK)
    dn_conv_full = _slabs_to_cols(g_conv[:, :n_dn].reshape(N_DEV, DN_CONV, 3 * WIDTH // N_DEV))
    ffn_conv_full = _slabs_to_cols(g_conv[:, n_dn:n_dn + n_ffn].reshape(N_DEV, FFN_CONV, 2 * D_FF // N_DEV))
    q_end = 3 * WIDTH
    ab_end = q_end + 2 * HEADS
    gate_end = ab_end + WIDTH
    sb_end = gate_end + 3 * WIDTH
    pad_lanes = lambda a: jnp.pad(a, ((0, 0), (0, 128 - a.shape[1])))
    wts = dict(
        norm1_w=norm1_w, w_dnqkv_t=w_in_t[:q_end], w_ab_t=jnp.pad(w_in_t[q_end:ab_end], ((0, 128 - 2 * HEADS), (0, 0))),
        w_dngate_t=w_in_t[ab_end:gate_end], w_sbqkv_t=w_in_t[gate_end:sb_end], w_gl_t=w_in_t[sb_end:],
        dn_conv_w=dn_conv_full, alog=pad_lanes(dn_A_log), dtb=pad_lanes(dn_dt_bias), dn_norm_w=dn_norm_w,
        norm2_w=norm2_w, ffn_conv_w=ffn_conv_full, norm_f_w=norm_f_w[None, :])

    n_fc = FFN_CONV * 2 * D_FF
    fc_rows = -(-n_fc // D_MODEL)
    dn_rows = DN_CONV * 3 * WIDTH // D_MODEL
    late_names = ("w_proj_dn", "w_proj_sb", "w_out", "ffn_w_up", "ffn_w_down")

    class Plan:
        @staticmethod
        def late_gather():
            return _Gather([w_loc[k].astype(BF16) for k in late_names])

        @staticmethod
        def late_weights(got):
            g_pd, g_ps, g_out, g_up, g_down = got
            return dict(w_proj_dn=g_pd.reshape(WIDTH, D_MODEL), w_proj_sb=g_ps.reshape(WIDTH, D_MODEL),
                        w_out=g_out.reshape(D_MODEL, D_MODEL), ffn_w_up_t=g_up.reshape(2 * D_FF, D_MODEL),
                        ffn_w_down=g_down.reshape(D_FF, D_MODEL))

        @staticmethod
        def early_grads(g):
            return _Exchange([g["w_proj_dn"].reshape(N_DEV, WIDTH // N_DEV, D_MODEL),
                              g["w_proj_sb"].reshape(N_DEV, WIDTH // N_DEV, D_MODEL),
                              g["w_out"].reshape(N_DEV, D_MODEL // N_DEV, D_MODEL),
                              g["ffn_w_up_t"].reshape(N_DEV, 2 * D_FF // N_DEV, D_MODEL),
                              g["ffn_w_down"].reshape(N_DEV, D_FF // N_DEV, D_MODEL)])

        @staticmethod
        def late_grads(g, loss):
            g_win_t = jnp.concatenate([g["w_main_t"][:q_end], g["w_ab_t"][:2 * HEADS], g["w_main_t"][q_end:]],
                                      axis=0)
            row3 = jnp.concatenate([g["dn_norm_w"], g["alog"], g["dtb"], jnp.pad(loss, ((0, 0), (0, 127))),
                                    jnp.zeros((1, D_MODEL - 512), F32)], axis=1)
            fconv_rows = jnp.pad(g["ffn_conv_w"].reshape(-1), (0, fc_rows * D_MODEL - n_fc)).reshape(fc_rows, D_MODEL)
            pad8 = lambda a: jnp.pad(a, ((0, -a.shape[0] % 8), (0, 0)))
            pieces = [g["norm2_w"], g["norm_f_w"], row3, g["dn_conv_w"].reshape(dn_rows, D_MODEL), fconv_rows]
            small = jnp.concatenate([pad8(a) for a in pieces], axis=0)
            assert small.shape[0] == SMALL_ROWS
            return _Exchange([g_win_t.reshape(N_DEV, in_width // N_DEV, D_MODEL)], [small])

    loss, grad_x, g, got_early, got_late = _local_step(x[0], loss_target[0], wts, Plan)
    r_pd, r_ps, r_out, r_up, r_down = got_early
    r_in, r_small = got_late
    (r_norm1,) = _comm_call(_Exchange([], [jnp.pad(g["norm1_w"], ((0, 7), (0, 0)))]), "gather_norm1")

    parts = dict(w_in=r_in, w_proj_dn=r_pd, w_proj_sb=r_ps, w_out=r_out, ffn_w_up=r_up, ffn_w_down=r_down)
    parts["norm1_w"] = r_norm1[:, 0:1, :]
    parts["norm2_w"] = r_small[:, 0:1, :]
    parts["norm_f_w"] = r_small[:, 8:9, :]
    parts["dn_norm_w"] = r_small[:, 16:17, 0:HEAD_DIM]
    parts["dn_A_log"] = r_small[:, 16:17, 128:128 + HEADS]
    parts["dn_dt_bias"] = r_small[:, 16:17, 256:256 + HEADS]
    dnc = r_small[:, 24:24 + dn_rows, :].reshape(N_DEV, DN_CONV, 3 * WIDTH)
    parts["dn_conv_w"] = lax.dynamic_slice_in_dim(dnc, me * (3 * WIDTH // N_DEV), 3 * WIDTH // N_DEV, axis=2)
    fc0 = 24 + dn_rows + (-dn_rows % 8)
    fcc = r_small[:, fc0:fc0 + fc_rows, :].reshape(N_DEV, fc_rows * D_MODEL)[:, :n_fc]
    fcc = fcc.reshape(N_DEV, FFN_CONV, 2 * D_FF)
    parts["ffn_conv_w"] = lax.dynamic_slice_in_dim(fcc, me * (2 * D_FF // N_DEV), 2 * D_FF // N_DEV, axis=2)
    loss_total = jnp.sum(r_small[:, 16, 384])

    res = {k: _adamw(parts[k], w_loc[k], m_loc[k], v_loc[k], "adamw_" + k) for k in WEIGHT_ORDER}
    lead = ("w_in", "dn_conv_w", "w_proj_dn", "w_proj_sb", "w_out", "ffn_w_up", "ffn_conv_w", "ffn_w_down")

    def shaped(k, a):
        if k in ("w_in", "ffn_w_up"):
            return jnp.transpose(a)[None]
        if k in lead:
            return a[None]
        if k == "norm_f_w":
            return a[0]
        return a

    outs = [loss_total, grad_x[None]]
    for idx in range(4):
        outs += [shaped(k, res[k][idx]) for k in WEIGHT_ORDER]
    return tuple(outs)
```

```python
import functools

import jax
import jax.numpy as jnp
from jax import lax
from jax.experimental import pallas as pl
from jax.experimental.pallas import tpu as pltpu

F32 = jnp.float32
BF16 = jnp.bfloat16

N_DEV = 8
D_MODEL = 1024
HEADS = 8
HEAD_DIM = 128
WIDTH = HEADS * HEAD_DIM
DN_CONV = 4
DN_CHUNK = 64
D_FF = 2816
FFN_CONV = 3
EPS = 1e-6
ATT_BLOCK = 256
SB_LOG_ZERO = -104.0
SB_GROUP = 2
SB_HEADS_PER_STEP = 2
SMALL_ROWS = 64

ADAM_LR = 0.001
ADAM_B1 = 0.9
ADAM_B2 = 0.999
ADAM_EPS = 1e-08
ADAM_WD = 0.01
ADAM_STEP = 10

VMEM_LIMIT = 48 * 1024 * 1024


def _params(sem=None, **kw):
    return pltpu.CompilerParams(dimension_semantics=sem, vmem_limit_bytes=VMEM_LIMIT, **kw)


def _tile(n, cap):
    if n <= cap:
        return n
    best = None
    for t in range(128, cap + 1, 128):
        if n % t == 0:
            best = t
    assert best is not None, (n, cap)
    return best


def _dot(a, b, dims):
    return lax.dot_general(a, b, ((dims[0], dims[1]), ((), ())), preferred_element_type=F32)


NN = ((1,), (0,))
NT = ((1,), (1,))
TN = ((0,), (0,))


def _dotb(a, b, dims):
    return _dot(a.astype(BF16), b.astype(BF16), dims)


def _split3(x):
    h1 = x.astype(BF16)
    r1 = x - h1.astype(F32)
    h2 = r1.astype(BF16)
    r2 = r1 - h2.astype(F32)
    return h1, h2, r2.astype(BF16)


def _dot_xr(a, b_exact, dims):
    a1, a2, a3 = _split3(a)
    return _dot(a1, b_exact, dims) + _dot(a2, b_exact, dims) + _dot(a3, b_exact, dims)


def _split2(x):
    h1 = x.astype(BF16)
    return h1, (x - h1.astype(F32)).astype(BF16)


def _dot_xr2(a, b_exact, dims):
    a1, a2 = _split2(a)
    return _dot(a1, b_exact, dims) + _dot(a2, b_exact, dims)


def _dot_xl(a_exact, b, dims):
    b1, b2, b3 = _split3(b)
    return _dot(a_exact, b1, dims) + _dot(a_exact, b2, dims) + _dot(a_exact, b3, dims)


def _dot3(a, b, dims):
    a1 = a.astype(BF16)
    a2 = (a - a1.astype(F32)).astype(BF16)
    b1 = b.astype(BF16)
    b2 = (b - b1.astype(F32)).astype(BF16)
    return _dot(a1, b1, dims) + (_dot(a1, b2, dims) + _dot(a2, b1, dims))


def _sigmoid(x):
    return 1.0 / (1.0 + jnp.exp(-x))


def _log1pexp_neg_abs(x):
    return jnp.log(1.0 + jnp.exp(-jnp.abs(x)))


def _iota(shape, dim):
    return lax.broadcasted_iota(jnp.int32, shape, dim)


def _matmul(a, b, mode, out_dtype, name, add=None, comm=None):
    if mode == "nn":
        (m, k), (k2, n) = a.shape, b.shape
    elif mode == "nt":
        (m, k), (n, k2) = a.shape, b.shape
    else:
        (k, m), (k2, n) = a.shape, b.shape
    assert k == k2, (a.shape, b.shape, mode)
    tm, tn, tk = _tile(m, 1408), _tile(n, 1408), _tile(k, 1536)
    nk = k // tk
    dims = {"nn": NN, "nt": NT, "tn": TN}[mode]

    def body(*refs):
        if add is None:
            a_ref, b_ref, o_ref, acc_ref = refs
        else:
            a_ref, b_ref, add_ref, o_ref, acc_ref = refs
        kk = pl.program_id(2)

        @pl.when(kk == 0)
        def _():
            acc_ref[...] = jnp.zeros_like(acc_ref)

        acc_ref[...] += _dotb(a_ref[...], b_ref[...], dims)

        @pl.when(kk == nk - 1)
        def _():
            r = acc_ref[...]
            if add is not None:
                r = r + add_ref[...].astype(F32)
            o_ref[...] = r.astype(out_dtype)

    if mode == "nn":
        specs = [pl.BlockSpec((tm, tk), lambda i, j, l: (i, l)), pl.BlockSpec((tk, tn), lambda i, j, l: (l, j))]
    elif mode == "nt":
        specs = [pl.BlockSpec((tm, tk), lambda i, j, l: (i, l)), pl.BlockSpec((tn, tk), lambda i, j, l: (j, l))]
    else:
        specs = [pl.BlockSpec((tk, tm), lambda i, j, l: (l, i)), pl.BlockSpec((tk, tn), lambda i, j, l: (l, j))]
    args = [a, b]
    if add is not None:
        specs.append(pl.BlockSpec((tm, tn), lambda i, j, l: (i, j)))
        args.append(add)
    grid = (m // tm, n // tn, nk)

    def when():
        i, j, l = pl.program_id(0), pl.program_id(1), pl.program_id(2)
        first = jnp.logical_and(jnp.logical_and(i == 0, j == 0), l == 0)
        last = jnp.logical_and(jnp.logical_and(i == grid[0] - 1, j == grid[1] - 1), l == nk - 1)
        return first, last, last

    (out,), extra = _host_call(
        body, name, comm, when, [jax.ShapeDtypeStruct((m, n), out_dtype)], grid, specs,
        [pl.BlockSpec((tm, tn), lambda i, j, l: (i, j))], [pltpu.VMEM((tm, tn), F32)],
        ("parallel", "parallel", "arbitrary"), args)
    return out if comm is None else (out, extra)


def _rmsnorm_fwd(x, w, name):
    t, d = x.shape
    tr = _tile(t, 512)

    def body(x_ref, w_ref, o_ref):
        xv = x_ref[...]
        r = lax.rsqrt(jnp.mean(xv * xv, axis=1, keepdims=True) + EPS)
        o_ref[...] = (xv * r * w_ref[...]).astype(BF16)

    return pl.pallas_call(
        body, name=name,
        out_shape=jax.ShapeDtypeStruct((t, d), BF16),
        grid=(t // tr,),
        in_specs=[pl.BlockSpec((tr, d), lambda i: (i, 0)), pl.BlockSpec((1, d), lambda i: (0, 0))],
        out_specs=pl.BlockSpec((tr, d), lambda i: (i, 0)),
        compiler_params=_params(("parallel",)),
    )(x, w)


def _rmsnorm_bwd(dn, x, w, dres, name):
    t, d = x.shape
    tr = _tile(t, 512)

    def body(dn_ref, x_ref, w_ref, dres_ref, dx_ref, dw_ref):
        i = pl.program_id(0)
        xv = x_ref[...]
        g = dn_ref[...].astype(F32)
        r = lax.rsqrt(jnp.mean(xv * xv, axis=1, keepdims=True) + EPS)
        xh = xv * r
        dxh = g * w_ref[...]
        dx = r * (dxh - xh * jnp.mean(dxh * xh, axis=1, keepdims=True))
        dx_ref[...] = dres_ref[...] + dx

        @pl.when(i == 0)
        def _():
            dw_ref[...] = jnp.zeros_like(dw_ref)

        dw_ref[...] += jnp.sum(g * xh, axis=0, keepdims=True)

    return pl.pallas_call(
        body, name=name,
        out_shape=(jax.ShapeDtypeStruct((t, d), F32), jax.ShapeDtypeStruct((1, d), F32)),
        grid=(t // tr,),
        in_specs=[pl.BlockSpec((tr, d), lambda i: (i, 0)), pl.BlockSpec((tr, d), lambda i: (i, 0)),
                  pl.BlockSpec((1, d), lambda i: (0, 0)), pl.BlockSpec((tr, d), lambda i: (i, 0))],
        out_specs=(pl.BlockSpec((tr, d), lambda i: (i, 0)), pl.BlockSpec((1, d), lambda i: (0, 0))),
        compiler_params=_params(("arbitrary",)),
    )(dn, x, w, dres)


def _final_loss(x2, target, w, name):
    t, d = x2.shape
    tr = _tile(t, 512)

    def body(x_ref, t_ref, w_ref, dx_ref, dw_ref, loss_ref):
        i = pl.program_id(0)
        xv = x_ref[...]
        r = lax.rsqrt(jnp.mean(xv * xv, axis=1, keepdims=True) + EPS)
        xh = xv * r
        err = xh * w_ref[...] - t_ref[...]
        dy = err * (1.0 / d)
        dxh = dy * w_ref[...]
        dx_ref[...] = r * (dxh - xh * jnp.mean(dxh * xh, axis=1, keepdims=True))

        @pl.when(i == 0)
        def _():
            dw_ref[...] = jnp.zeros_like(dw_ref)
            loss_ref[...] = jnp.zeros_like(loss_ref)

        dw_ref[...] += jnp.sum(dy * xh, axis=0, keepdims=True)
        row = jnp.sum(err * err, axis=1, keepdims=True) * (0.5 / d)
        loss_ref[...] += jnp.sum(row, axis=0, keepdims=True)

    return pl.pallas_call(
        body, name=name,
        out_shape=(jax.ShapeDtypeStruct((t, d), F32), jax.ShapeDtypeStruct((1, d), F32),
                   jax.ShapeDtypeStruct((1, 1), F32)),
        grid=(t // tr,),
        in_specs=[pl.BlockSpec((tr, d), lambda i: (i, 0)), pl.BlockSpec((tr, d), lambda i: (i, 0)),
                  pl.BlockSpec((1, d), lambda i: (0, 0))],
        out_specs=(pl.BlockSpec((tr, d), lambda i: (i, 0)), pl.BlockSpec((1, d), lambda i: (0, 0)),
                   pl.BlockSpec((1, 1), lambda i: (0, 0))),
        compiler_params=_params(("arbitrary",)),
    )(x2, target, w)


def _shift_down(cur, prev, k, row):
    r = pltpu.roll(cur, k, 0)
    for m in range(k):
        r = jnp.where(row == m, prev[8 - k + m:8 - k + m + 1, :], r)
    return r


def _shift_up(cur, nxt, k, row, tr):
    r = pltpu.roll(cur, tr - k, 0)
    for m in range(k):
        r = jnp.where(row == tr - k + m, nxt[m:m + 1, :], r)
    return r


def _conv_taps(cur, prev, w, ntaps, row):
    taps = [cur if i == ntaps - 1 else _shift_down(cur, prev, ntaps - 1 - i, row) for i in range(ntaps)]
    y = w[0:1, :] * taps[0]
    for i in range(1, ntaps):
        y = y + w[i:i + 1, :] * taps[i]
    return taps, y


def _conv_bwd_data(parts, w, ntaps, out_dtype, name):
    t, chp = parts[0].shape
    npart = len(parts)
    tr, tc = _tile(t, 512), _tile(chp, 1408)
    nc = chp // tc
    nrow8 = t // 8
    last = t // tr - 1

    def body(*refs):
        cur_refs, nxt_refs = refs[:npart], refs[npart:2 * npart]
        w_ref, o_ref = refs[2 * npart], refs[2 * npart + 1]
        i, j = pl.program_id(0), pl.program_id(1)
        cur, nxt = cur_refs[0][...], nxt_refs[0][...]
        for p in range(1, npart):
            cur = jnp.where(j >= p * nc, cur_refs[p][...], cur)
            nxt = jnp.where(j >= p * nc, nxt_refs[p][...], nxt)
        nxt = jnp.where(i == last, 0.0, nxt)
        row = _iota(cur.shape, 0)
        wv = w_ref[...]
        y = wv[ntaps - 1:ntaps, :] * cur
        for k in range(1, ntaps):
            y = y + wv[ntaps - 1 - k:ntaps - k, :] * _shift_up(cur, nxt, k, row, tr)
        o_ref[...] = y.astype(out_dtype)

    col = lambda p: (lambda j: jnp.clip(j - p * nc, 0, nc - 1))
    cur_specs = [pl.BlockSpec((tr, tc), lambda i, j, c=col(p): (i, c(j))) for p in range(npart)]
    nxt_specs = [pl.BlockSpec((8, tc), lambda i, j, c=col(p): (jnp.minimum((i + 1) * (tr // 8), nrow8 - 1), c(j)))
                 for p in range(npart)]
    return pl.pallas_call(
        body, name=name,
        out_shape=jax.ShapeDtypeStruct((t, npart * chp), out_dtype),
        grid=(t // tr, npart * nc),
        in_specs=cur_specs + nxt_specs + [pl.BlockSpec((ntaps, tc), lambda i, j: (0, j))],
        out_specs=pl.BlockSpec((tr, tc), lambda i, j: (i, j)),
        compiler_params=_params(("parallel", "parallel")),
    )(*parts, *parts, w)


def _ffn_act_fwd(upre, cw, name):
    t = upre.shape[0]
    tr, tc = _tile(t, 512), _tile(D_FF, 1408)
    nj = D_FF // tc

    def body(g_ref, gp_ref, u_ref, up_ref, wg_ref, wu_ref, o_ref):
        i = pl.program_id(0)
        row = _iota((tr, tc), 0)
        gp = jnp.where(i == 0, 0.0, gp_ref[...])
        up = jnp.where(i == 0, 0.0, up_ref[...])
        _, gc = _conv_taps(g_ref[...], gp, wg_ref[...], FFN_CONV, row)
        _, uc = _conv_taps(u_ref[...], up, wu_ref[...], FFN_CONV, row)
        o_ref[...] = (gc * _sigmoid(gc) * uc).astype(BF16)

    prev = lambda off: (lambda i, j: (jnp.maximum(i * (tr // 8) - 1, 0), j + off))
    return pl.pallas_call(
        body, name=name,
        out_shape=jax.ShapeDtypeStruct((t, D_FF), BF16),
        grid=(t // tr, nj),
        in_specs=[pl.BlockSpec((tr, tc), lambda i, j: (i, j)), pl.BlockSpec((8, tc), prev(0)),
                  pl.BlockSpec((tr, tc), lambda i, j: (i, j + nj)), pl.BlockSpec((8, tc), prev(nj)),
                  pl.BlockSpec((FFN_CONV, tc), lambda i, j: (0, j)),
                  pl.BlockSpec((FFN_CONV, tc), lambda i, j: (0, j + nj))],
        out_specs=pl.BlockSpec((tr, tc), lambda i, j: (i, j)),
        compiler_params=_params(("parallel", "parallel")),
    )(upre, upre, upre, upre, cw, cw)


def _ffn_act_bwd(dact, upre, cw, name):
    t = upre.shape[0]
    tr, tc = _tile(t, 256), _tile(D_FF, 1408)
    nj = D_FF // tc

    def body(da_ref, g_ref, gp_ref, u_ref, up_ref, wg_ref, wu_ref, dg_ref, du_ref, dwg_ref, dwu_ref):
        i = pl.program_id(1)
        row = _iota((tr, tc), 0)
        gp = jnp.where(i == 0, 0.0, gp_ref[...])
        up = jnp.where(i == 0, 0.0, up_ref[...])
        gt, gc = _conv_taps(g_ref[...], gp, wg_ref[...], FFN_CONV, row)
        ut, uc = _conv_taps(u_ref[...], up, wu_ref[...], FFN_CONV, row)
        da = da_ref[...].astype(F32)
        sg = _sigmoid(gc)
        dgc = da * uc * (sg * (1.0 + gc * (1.0 - sg)))
        duc = da * (gc * sg)
        dg_ref[...] = dgc
        du_ref[...] = duc

        @pl.when(i == 0)
        def _():
            dwg_ref[...] = jnp.zeros_like(dwg_ref)
            dwu_ref[...] = jnp.zeros_like(dwu_ref)

        for k in range(FFN_CONV):
            dwg_ref[k:k + 1, :] += jnp.sum(dgc * gt[k], axis=0, keepdims=True)
            dwu_ref[k:k + 1, :] += jnp.sum(duc * ut[k], axis=0, keepdims=True)

    prev = lambda off: (lambda j, i: (jnp.maximum(i * (tr // 8) - 1, 0), j + off))
    blk = lambda off: pl.BlockSpec((tr, tc), lambda j, i: (i, j + off))
    wblk = lambda off: pl.BlockSpec((FFN_CONV, tc), lambda j, i: (0, j + off))
    dgc, duc, dwg, dwu = pl.pallas_call(
        body, name=name,
        out_shape=(jax.ShapeDtypeStruct((t, D_FF), F32), jax.ShapeDtypeStruct((t, D_FF), F32),
                   jax.ShapeDtypeStruct((FFN_CONV, D_FF), F32), jax.ShapeDtypeStruct((FFN_CONV, D_FF), F32)),
        grid=(nj, t // tr),
        in_specs=[blk(0), blk(0), pl.BlockSpec((8, tc), prev(0)), blk(nj), pl.BlockSpec((8, tc), prev(nj)),
                  wblk(0), wblk(nj)],
        out_specs=(blk(0), blk(0), wblk(0), wblk(0)),
        compiler_params=_params(("parallel", "arbitrary")),
    )(dact, upre, upre, upre, upre, cw, cw)
    return dgc, duc, dwg, dwu


def _dn_pre_fwd(qkv_pre, cw, name):
    t = qkv_pre.shape[0]
    tr = _tile(t, 512)
    scale = HEAD_DIM ** -0.5

    def body(x_ref, p_ref, w_ref, o_ref):
        i, j = pl.program_id(0), pl.program_id(1)
        row = _iota((tr, WIDTH), 0)
        prev = jnp.where(i == 0, 0.0, p_ref[...])
        _, c = _conv_taps(x_ref[...], prev, w_ref[...], DN_CONV, row)
        s = c * _sigmoid(c)
        for h in range(HEADS):
            sl = slice(h * HEAD_DIM, (h + 1) * HEAD_DIM)
            sh = s[:, sl]
            r = lax.rsqrt(jnp.sum(sh * sh, axis=1, keepdims=True) + EPS)
            o_ref[:, sl] = sh * jnp.where(j == 0, r * scale, jnp.where(j == 1, r, 1.0))

    return pl.pallas_call(
        body, name=name,
        out_shape=jax.ShapeDtypeStruct((t, 3 * WIDTH), F32),
        grid=(t // tr, 3),
        in_specs=[pl.BlockSpec((tr, WIDTH), lambda i, j: (i, j)),
                  pl.BlockSpec((8, WIDTH), lambda i, j: (jnp.maximum(i * (tr // 8) - 1, 0), j)),
                  pl.BlockSpec((DN_CONV, WIDTH), lambda i, j: (0, j))],
        out_specs=pl.BlockSpec((tr, WIDTH), lambda i, j: (i, j)),
        compiler_params=_params(("parallel", "parallel")),
    )(qkv_pre, qkv_pre, cw)


def _dn_pre_bwd(dq, dk, dv, qkv_pre, cw, name):
    t = qkv_pre.shape[0]
    tr = _tile(t, 256)
    scale = HEAD_DIM ** -0.5

    def body(dq_ref, dk_ref, dv_ref, x_ref, p_ref, w_ref, dc_ref, dw_ref):
        j, i = pl.program_id(0), pl.program_id(1)
        row = _iota((tr, WIDTH), 0)
        prev = jnp.where(i == 0, 0.0, p_ref[...])
        taps, c = _conv_taps(x_ref[...], prev, w_ref[...], DN_CONV, row)
        d = jnp.where(j == 0, dq_ref[...] * scale, jnp.where(j == 1, dk_ref[...], dv_ref[...]))
        sg = _sigmoid(c)
        s = c * sg
        dsilu = sg * (1.0 + c * (1.0 - sg))
        for h in range(HEADS):
            sl = slice(h * HEAD_DIM, (h + 1) * HEAD_DIM)
            sh, dh = s[:, sl], d[:, sl]
            r = lax.rsqrt(jnp.sum(sh * sh, axis=1, keepdims=True) + EPS)
            nh = sh * r
            ds_norm = r * (dh - nh * jnp.sum(nh * dh, axis=1, keepdims=True))
            dc_ref[:, sl] = jnp.where(j < 2, ds_norm, dh) * dsilu[:, sl]

        @pl.when(i == 0)
        def _():
            dw_ref[...] = jnp.zeros_like(dw_ref)

        dc = dc_ref[...]
        for k in range(DN_CONV):
            dw_ref[k:k + 1, :] += jnp.sum(dc * taps[k], axis=0, keepdims=True)

    dspec = lambda p: pl.BlockSpec((tr, WIDTH), lambda j, i: (jnp.where(j == p, i, 0), 0))
    return pl.pallas_call(
        body, name=name,
        out_shape=(jax.ShapeDtypeStruct((t, 3 * WIDTH), F32), jax.ShapeDtypeStruct((DN_CONV, 3 * WIDTH), F32)),
        grid=(3, t // tr),
        in_specs=[dspec(0), dspec(1), dspec(2),
                  pl.BlockSpec((tr, WIDTH), lambda j, i: (i, j)),
                  pl.BlockSpec((8, WIDTH), lambda j, i: (jnp.maximum(i * (tr // 8) - 1, 0), j)),
                  pl.BlockSpec((DN_CONV, WIDTH), lambda j, i: (0, j))],
        out_specs=(pl.BlockSpec((tr, WIDTH), lambda j, i: (i, j)),
                   pl.BlockSpec((DN_CONV, WIDTH), lambda j, i: (0, j))),
        compiler_params=_params(("parallel", "arbitrary")),
    )(dq, dk, dv, qkv_pre, qkv_pre, cw)


def _tri(n, kind):
    r, c = _iota((n, n), 0), _iota((n, n), 1)
    m = {"lower": r >= c, "strict": r > c, "upper": r <= c}[kind]
    return m


def _dn_gates_fwd(hab, alog, dtb, name):
    t = hab.shape[0]
    cc = DN_CHUNK

    def body(h_ref, al_ref, dt_ref, o_ref):
        hv = h_ref[...]
        lane = _iota(hv.shape, 1)
        xa = hv + dt_ref[...]
        sp = jnp.maximum(xa, 0.0) + _log1pexp_neg_abs(xa)
        g = jnp.where(lane < HEADS, -jnp.exp(al_ref[...]) * sp, 0.0)
        tril = jnp.where(_tri(cc, "lower"), 1.0, 0.0).astype(BF16)
        gc = _dot_xl(tril, g, NN)
        o_ref[...] = jnp.where(lane < HEADS, gc, jnp.where(lane < 2 * HEADS, _sigmoid(hv), 0.0))

    return pl.pallas_call(
        body, name=name,
        out_shape=jax.ShapeDtypeStruct((t, 128), F32),
        grid=(t // cc,),
        in_specs=[pl.BlockSpec((cc, 128), lambda i: (i, 0)), pl.BlockSpec((1, 128), lambda i: (0, 0)),
                  pl.BlockSpec((1, 128), lambda i: (0, 0))],
        out_specs=pl.BlockSpec((cc, 128), lambda i: (i, 0)),
        compiler_params=_params(("parallel",)),
    )(hab, alog, dtb)


def _dn_gates_bwd(dgates, hab, alog, dtb, name):
    t = hab.shape[0]
    cc = DN_CHUNK

    def body(d_ref, h_ref, al_ref, dt_ref, o_ref, dal_ref, ddt_ref):
        i = pl.program_id(0)
        hv = h_ref[...]
        dv = d_ref[...]
        lane = _iota(hv.shape, 1)
        triu = jnp.where(_tri(cc, "upper"), 1.0, 0.0).astype(BF16)
        dg = _dot_xl(triu, jnp.where(lane < HEADS, dv, 0.0), NN)
        xa = hv + dt_ref[...]
        sp = jnp.maximum(xa, 0.0) + _log1pexp_neg_abs(xa)
        ea = jnp.exp(al_ref[...])
        da = jnp.where(lane < HEADS, dg * (-ea) * _sigmoid(xa), 0.0)
        be = _sigmoid(hv)
        db = dv * be * (1.0 - be)
        o_ref[...] = jnp.where(lane < HEADS, da, jnp.where(lane < 2 * HEADS, db, 0.0))

        @pl.when(i == 0)
        def _():
            dal_ref[...] = jnp.zeros_like(dal_ref)
            ddt_ref[...] = jnp.zeros_like(ddt_ref)

        dal_ref[...] += jnp.sum(jnp.where(lane < HEADS, dg * (-ea) * sp, 0.0), axis=0, keepdims=True)
        ddt_ref[...] += jnp.sum(da, axis=0, keepdims=True)

    return pl.pallas_call(
        body, name=name,
        out_shape=(jax.ShapeDtypeStruct((t, 128), F32), jax.ShapeDtypeStruct((1, 128), F32),
                   jax.ShapeDtypeStruct((1, 128), F32)),
        grid=(t // cc,),
        in_specs=[pl.BlockSpec((cc, 128), lambda i: (i, 0)), pl.BlockSpec((cc, 128), lambda i: (i, 0)),
                  pl.BlockSpec((1, 128), lambda i: (0, 0)), pl.BlockSpec((1, 128), lambda i: (0, 0))],
        out_specs=(pl.BlockSpec((cc, 128), lambda i: (i, 0)), pl.BlockSpec((1, 128), lambda i: (0, 0)),
                   pl.BlockSpec((1, 128), lambda i: (0, 0))),
        compiler_params=_params(("arbitrary",)),
    )(dgates, hab, alog, dtb)


def _dn_chunk_common(gates, h):
    cc = DN_CHUNK
    lane = _iota(gates.shape, 1)
    gh = jnp.where(lane == h, gates, 0.0)
    gc_col = jnp.sum(gh, axis=1, keepdims=True)
    gc_row = _dot_xl(jnp.ones((cc, 128), BF16), gh, NT)
    beta = jnp.sum(jnp.where(lane == h + HEADS, gates, 0.0), axis=1, keepdims=True)
    lower = _tri(cc, "lower")
    decay = jnp.where(lower, jnp.exp(jnp.where(lower, gc_col - gc_row, 0.0)), 0.0)
    gc_last = gc_col[cc - 1:cc, :]
    return gc_col, gc_last, beta, decay


def _dn_local_fwd(act, gates, name):
    t = act.shape[0]
    cc = DN_CHUNK
    nc = t // cc

    def body(q_ref, k_ref, v_ref, g_ref, u_ref, w_ref, kd_ref, qg_ref, ti_ref, p_ref):
        gates = g_ref[...]
        eye = jnp.where(_iota((cc, cc), 0) == _iota((cc, cc), 1), 1.0, 0.0)
        hs = range(HEADS)
        sl = [slice(h * HEAD_DIM, (h + 1) * HEAD_DIM) for h in hs]
        q, k, v = ([r[:, s] for s in sl] for r in (q_ref, k_ref, v_ref))
        gc_col, gc_last, beta, decay = zip(*[_dn_chunk_common(gates, h) for h in hs])
        gam = [jnp.exp(g) for g in gc_col]
        kb = [k[h] * beta[h] for h in hs]
        npow = [-jnp.where(_tri(cc, "strict"), _dotb(kb[h], k[h], NT) * decay[h], 0.0) for h in hs]
        tinv = [eye + n for n in npow]
        for _ in range(5):
            npow = [_dot3(n, n, NN) for n in npow]
            tinv = [t + _dot3(t, n, NN) for t, n in zip(tinv, npow)]
        uu = [_dot3(tinv[h], v[h] * beta[h], NN) for h in hs]
        ww = [_dot3(tinv[h], kb[h] * gam[h], NN) for h in hs]
        pp = [jnp.where(_tri(cc, "lower"), _dotb(q[h], k[h], NT) * decay[h], 0.0) for h in hs]
        for h in hs:
            u_ref[:, sl[h]] = uu[h]
            w_ref[:, sl[h]] = ww[h]
            kd_ref[:, sl[h]] = k[h] * jnp.exp(gc_last[h] - gc_col[h])
            qg_ref[:, sl[h]] = q[h] * gam[h]
            ti_ref[h] = tinv[h]
            p_ref[h] = pp[h]

    row = lambda off: pl.BlockSpec((cc, WIDTH), lambda n: (n, off))
    mat = pl.BlockSpec((HEADS, cc, cc), lambda n: (0, n, 0))
    tw = jax.ShapeDtypeStruct((t, WIDTH), F32)
    hm = jax.ShapeDtypeStruct((HEADS, t, cc), F32)
    return pl.pallas_call(
        body, name=name,
        out_shape=(tw, tw, tw, tw, hm, hm),
        grid=(nc,),
        in_specs=[row(0), row(1), row(2), pl.BlockSpec((cc, 128), lambda n: (n, 0))],
        out_specs=(row(0), row(0), row(0), row(0), mat, mat),
        compiler_params=_params(("parallel",)),
    )(act, act, act, gates)


def _dn_scan_fwd(u, w, kd, qg, p, gates, name):
    t = u.shape[0]
    cc = DN_CHUNK
    nc = t // cc

    def body(u_ref, w_ref, kd_ref, qg_ref, p_ref, g_ref, o_ref, sh_ref, s_ref):
        n = pl.program_id(0)

        @pl.when(n == 0)
        def _():
            s_ref[...] = jnp.zeros_like(s_ref)

        glast = jnp.exp(g_ref[cc - 1:cc, :])
        hs = range(HEADS)
        sl = [slice(h * HEAD_DIM, (h + 1) * HEAD_DIM) for h in hs]
        s = [s_ref[h] for h in hs]
        sb = [a.astype(BF16) for a in s]
        vn = [u_ref[:, sl[h]] - _dot(w_ref[:, sl[h]].astype(BF16), sb[h], NN) for h in hs]
        vnb = [a.astype(BF16) for a in vn]
        o_state = [_dot(qg_ref[:, sl[h]].astype(BF16), sb[h], NN) for h in hs]
        o_local = [_dot(p_ref[h].astype(BF16), vnb[h], NN) for h in hs]
        s_add = [_dot(kd_ref[:, sl[h]].astype(BF16), vnb[h], TN) for h in hs]
        for h in hs:
            o_ref[:, sl[h]] = o_state[h] + o_local[h]
            sh_ref[0, h] = s[h]
            s_ref[h] = glast[:, h:h + 1] * s[h] + s_add[h]

    row = pl.BlockSpec((cc, WIDTH), lambda n: (n, 0))
    return pl.pallas_call(
        body, name=name,
        out_shape=(jax.ShapeDtypeStruct((t, WIDTH), F32),
                   jax.ShapeDtypeStruct((nc, HEADS, HEAD_DIM, HEAD_DIM), F32)),
        grid=(nc,),
        in_specs=[row, row, row, row, pl.BlockSpec((HEADS, cc, cc), lambda n: (0, n, 0)),
                  pl.BlockSpec((cc, 128), lambda n: (n, 0))],
        out_specs=(row, pl.BlockSpec((1, HEADS, HEAD_DIM, HEAD_DIM), lambda n: (n, 0, 0, 0))),
        scratch_shapes=[pltpu.VMEM((HEADS, HEAD_DIM, HEAD_DIM), F32)],
        compiler_params=_params(("arbitrary",)),
    )(u, w, kd, qg, p, gates)


def _dn_scan_bwd(do, w, kd, qg, p, gates, name):
    t = do.shape[0]
    cc = DN_CHUNK
    nc = t // cc

    def body(do_ref, w_ref, kd_ref, qg_ref, p_ref, g_ref, dvn_ref, dsh_ref, ds_ref):
        n = pl.program_id(0)

        @pl.when(n == 0)
        def _():
            ds_ref[...] = jnp.zeros_like(ds_ref)

        glast = jnp.exp(g_ref[cc - 1:cc, :])
        hs = range(HEADS)
        sl = [slice(h * HEAD_DIM, (h + 1) * HEAD_DIM) for h in hs]
        ds = [ds_ref[h] for h in hs]
        dob = [do_ref[:, sl[h]].astype(BF16) for h in hs]
        dvn = [_dot(p_ref[h].astype(BF16), dob[h], TN) + _dot(kd_ref[:, sl[h]].astype(BF16), ds[h].astype(BF16), NN)
               for h in hs]
        ds_q = [_dot(qg_ref[:, sl[h]].astype(BF16), dob[h], TN) for h in hs]
        ds_w = [_dot(w_ref[:, sl[h]].astype(BF16), dvn[h].astype(BF16), TN) for h in hs]
        for h in hs:
            dvn_ref[:, sl[h]] = dvn[h]
            dsh_ref[0, h] = ds[h]
            ds_ref[h] = ds_q[h] + glast[:, h:h + 1] * ds[h] - ds_w[h]

    row = pl.BlockSpec((cc, WIDTH), lambda n: (nc - 1 - n, 0))
    return pl.pallas_call(
        body, name=name,
        out_shape=(jax.ShapeDtypeStruct((t, WIDTH), F32),
                   jax.ShapeDtypeStruct((nc, HEADS, HEAD_DIM, HEAD_DIM), F32)),
        grid=(nc,),
        in_specs=[row, row, row, row, pl.BlockSpec((HEADS, cc, cc), lambda n: (0, nc - 1 - n, 0)),
                  pl.BlockSpec((cc, 128), lambda n: (nc - 1 - n, 0))],
        out_specs=(row, pl.BlockSpec((1, HEADS, HEAD_DIM, HEAD_DIM), lambda n: (nc - 1 - n, 0, 0, 0))),
        scratch_shapes=[pltpu.VMEM((HEADS, HEAD_DIM, HEAD_DIM), F32)],
        compiler_params=_params(("arbitrary",)),
    )(do, w, kd, qg, p, gates)


def _dn_local_bwd(act, gates, u, w, kd, qg, tinv, p, sh, dsh, dvn, do, name):
    t = act.shape[0]
    cc = DN_CHUNK
    nc = t // cc

    def body(q_ref, k_ref, v_ref, g_ref, u_ref, w_ref, kd_ref, qg_ref, ti_ref, p_ref, s_ref, ds_ref,
             dvn_ref, do_ref, dq_ref, dk_ref, dv_ref, dg_ref):
        gates_v = g_ref[...]
        lower, strict = _tri(cc, "lower"), _tri(cc, "strict")
        ones = jnp.ones((cc, 128), BF16)
        rowc = _iota((cc, 1), 0)
        lane = _iota((cc, 128), 1)
        hs = range(HEADS)
        sl = [slice(h * HEAD_DIM, (h + 1) * HEAD_DIM) for h in hs]
        q, k, v, uu, ww, kd, qg, dvn, do = ([r[:, s] for s in sl] for r in (
            q_ref, k_ref, v_ref, u_ref, w_ref, kd_ref, qg_ref, dvn_ref, do_ref))
        gc_col, gc_last, beta, decay = zip(*[_dn_chunk_common(gates_v, h) for h in hs])
        gam = [jnp.exp(g) for g in gc_col]
        kb = [k[h] * beta[h] for h in hs]
        s_in = [s_ref[0, h] for h in hs]
        ds_out = [ds_ref[0, h] for h in hs]
        tinv = [ti_ref[h] for h in hs]

        a = [jnp.where(strict, _dotb(kb[h], k[h], NT) * decay[h], 0.0) for h in hs]
        vn = [uu[h] - _dotb(ww[h], s_in[h], NN) for h in hs]
        dqg = [_dotb(do[h], s_in[h], NT) for h in hs]
        dw = [-_dotb(dvn[h], s_in[h], NT) for h in hs]
        dp = [jnp.where(lower, _dotb(do[h], vn[h], NT), 0.0) for h in hs]
        dkd = [_dotb(vn[h], ds_out[h], NT) for h in hs]
        dru = [_dot3(tinv[h], dvn[h], TN) for h in hs]
        drw = [_dot3(tinv[h], dw[h], TN) for h in hs]
        da = [-jnp.where(strict, _dotb(dru[h], uu[h], NT) + _dotb(drw[h], ww[h], NT), 0.0) for h in hs]
        dad = [da[h] * decay[h] for h in hs]
        dpd = [dp[h] * decay[h] for h in hs]
        dkb = [_dotb(dad[h], k[h], NN) + gam[h] * drw[h] for h in hs]
        dk = [_dotb(dad[h], kb[h], TN) + _dotb(dpd[h], q[h], TN) + beta[h] * dkb[h]
              + jnp.exp(gc_last[h] - gc_col[h]) * dkd[h] for h in hs]
        dq = [gam[h] * dqg[h] + _dotb(dpd[h], k[h], NN) for h in hs]
        gm = [da[h] * a[h] + dp[h] * p_ref[h] for h in hs]
        colsum = [_dot_xr(gm[h], ones, TN)[:, 0:1] for h in hs]

        dgates = jnp.zeros((cc, 128), F32)
        for h in hs:
            dk_ref[:, sl[h]] = dk[h]
            dq_ref[:, sl[h]] = dq[h]
            dv_ref[:, sl[h]] = beta[h] * dru[h]
            dbeta = (jnp.sum(dkb[h] * k[h], axis=1, keepdims=True)
                     + jnp.sum(dru[h] * v[h], axis=1, keepdims=True))
            rkd = jnp.sum(dkd[h] * kd[h], axis=1, keepdims=True)
            dgc = (jnp.sum(gm[h], axis=1, keepdims=True) - colsum[h]
                   + jnp.sum(dqg[h] * qg[h], axis=1, keepdims=True)
                   + jnp.sum(drw[h] * kb[h], axis=1, keepdims=True) * gam[h] - rkd)
            tail = jnp.sum(rkd, axis=0, keepdims=True) + jnp.exp(gc_last[h]) * jnp.sum(
                jnp.sum(s_in[h] * ds_out[h], axis=1, keepdims=True), axis=0, keepdims=True)
            dgc = dgc + jnp.where(rowc == cc - 1, tail, 0.0)
            dgates = dgates + jnp.where(lane == h, dgc, 0.0) + jnp.where(lane == h + HEADS, dbeta, 0.0)
        dg_ref[...] = dgates

    row = lambda off: pl.BlockSpec((cc, WIDTH), lambda n: (n, off))
    mat = pl.BlockSpec((HEADS, cc, cc), lambda n: (0, n, 0))
    st = pl.BlockSpec((1, HEADS, HEAD_DIM, HEAD_DIM), lambda n: (n, 0, 0, 0))
    gl = pl.BlockSpec((cc, 128), lambda n: (n, 0))
    tw = jax.ShapeDtypeStruct((t, WIDTH), F32)
    return pl.pallas_call(
        body, name=name,
        out_shape=(tw, tw, tw, jax.ShapeDtypeStruct((t, 128), F32)),
        grid=(nc,),
        in_specs=[row(0), row(1), row(2), gl, row(0), row(0), row(0), row(0), mat, mat, st, st, row(0), row(0)],
        out_specs=(row(0), row(0), row(0), gl),
        compiler_params=_params(("parallel",)),
    )(act, act, act, gates, u, w, kd, qg, tinv, p, sh, dsh, dvn, do)


def _dn_post_fwd(o, gate, w, name):
    t = o.shape[0]
    tr = _tile(t, 512)

    def body(o_ref, g_ref, w_ref, y_ref):
        for h in range(HEADS):
            sl = slice(h * HEAD_DIM, (h + 1) * HEAD_DIM)
            ov, gv = o_ref[:, sl], g_ref[:, sl]
            r = lax.rsqrt(jnp.mean(ov * ov, axis=1, keepdims=True) + EPS)
            y_ref[:, sl] = (ov * r * w_ref[...] * (gv * _sigmoid(gv))).astype(BF16)

    blk = pl.BlockSpec((tr, WIDTH), lambda i: (i, 0))
    return pl.pallas_call(
        body, name=name,
        out_shape=jax.ShapeDtypeStruct((t, WIDTH), BF16),
        grid=(t // tr,),
        in_specs=[blk, blk, pl.BlockSpec((1, HEAD_DIM), lambda i: (0, 0))],
        out_specs=blk,
        compiler_params=_params(("parallel",)),
    )(o, gate, w)


def _dn_post_bwd(dy, o, gate, w, name):
    t = o.shape[0]
    tr = _tile(t, 512)

    def body(dy_ref, o_ref, g_ref, w_ref, do_ref, dg_ref, dw_ref):
        i = pl.program_id(0)

        @pl.when(i == 0)
        def _():
            dw_ref[...] = jnp.zeros_like(dw_ref)

        dw = jnp.zeros((1, HEAD_DIM), F32)
        for h in range(HEADS):
            sl = slice(h * HEAD_DIM, (h + 1) * HEAD_DIM)
            ov, gv, dyv = o_ref[:, sl], g_ref[:, sl], dy_ref[:, sl].astype(F32)
            r = lax.rsqrt(jnp.mean(ov * ov, axis=1, keepdims=True) + EPS)
            oh = ov * r
            sg = _sigmoid(gv)
            dg_ref[:, sl] = (dyv * oh * w_ref[...] * (sg * (1.0 + gv * (1.0 - sg)))).astype(BF16)
            dn = dyv * (gv * sg)
            doh = dn * w_ref[...]
            do_ref[:, sl] = r * (doh - oh * jnp.mean(doh * oh, axis=1, keepdims=True))
            dw = dw + jnp.sum(dn * oh, axis=0, keepdims=True)
        dw_ref[...] += dw

    blk = pl.BlockSpec((tr, WIDTH), lambda i: (i, 0))
    return pl.pallas_call(
        body, name=name,
        out_shape=(jax.ShapeDtypeStruct((t, WIDTH), F32), jax.ShapeDtypeStruct((t, WIDTH), BF16),
                   jax.ShapeDtypeStruct((1, HEAD_DIM), F32)),
        grid=(t // tr,),
        in_specs=[blk, blk, blk, pl.BlockSpec((1, HEAD_DIM), lambda i: (0, 0))],
        out_specs=(blk, blk, pl.BlockSpec((1, HEAD_DIM), lambda i: (0, 0))),
        compiler_params=_params(("arbitrary",)),
    )(dy, o, gate, w)


def _sb_scores(qs, k_ref, qi, it, carries, uincl):
    bk = ATT_BLOCK
    scale = HEAD_DIM ** -0.5
    heads, groups = range(len(qs)), range(SB_GROUP)
    lane = [slice(e * HEAD_DIM, (e + 1) * HEAD_DIM) for e in heads]
    js = [qi - SB_GROUP * it - g for g in groups]
    rows = [pl.ds(pl.multiple_of(jnp.maximum(j, 0) * bk, bk), bk) for j in js]
    qpos = qi * bk + _iota((bk, bk), 0)
    col = _iota((bk, bk), 1)
    mask1 = [jnp.logical_and(j * bk + col < qpos, j >= 0) for j in js]
    ks = [[k_ref[r, lane[e]] for r in rows] for e in heads]
    z = [[_dot(qs[e], k, NT) * scale for k in ks[e]] for e in heads]
    soft = [[_log1pexp_neg_abs(a) for a in ze] for ze in z]
    lk_full = [[-(jnp.maximum(a, 0.0) + s) for a, s in zip(z[e], soft[e])] for e in heads]
    lk = [[jnp.where(m, a, 0.0) for m, a in zip(mask1, lk_full[e])] for e in heads]
    ls = [[jnp.minimum(a, 0.0) - s for a, s in zip(z[e], soft[e])] for e in heads]
    incl = [[_dot_xr2(a, uincl, NN) for a in lk[e]] for e in heads]
    weights, out_carries = [], []
    for e in heads:
        cb, we = carries[e], []
        for g in groups:
            we.append(jnp.where(mask1[g], jnp.exp(ls[e][g] + (cb + incl[e][g] - lk[e][g])), 0.0))
            cb = cb + incl[e][g][:, 0:1]
        weights.append(we)
        out_carries.append(cb)
    return rows, ks, weights, mask1, lk_full, ls, out_carries


def _sb_more(qi, carry):
    it, cbs = carry[0], carry[1]
    live = jnp.max(cbs[0])
    for cb in cbs[1:]:
        live = jnp.maximum(live, jnp.max(cb))
    return jnp.logical_and(SB_GROUP * it <= qi, live > SB_LOG_ZERO)


def _sb_steps(groups, nq):
    def when():
        h, i = pl.program_id(0), pl.program_id(1)
        return (jnp.logical_and(h == 0, i == 0), jnp.logical_and(h == groups // 2, i == 0),
                jnp.logical_and(h == groups - 1, i == nq - 1))
    return when


def _sb_fwd(qkv, name, comm=None):
    t = qkv.shape[0]
    bk = ATT_BLOCK

    hp, wide = SB_HEADS_PER_STEP, SB_HEADS_PER_STEP * HEAD_DIM
    lane = [slice(e * HEAD_DIM, (e + 1) * HEAD_DIM) for e in range(hp)]

    def body(q_ref, k_ref, v_ref, o_ref):
        qi = pl.program_id(1)
        qs = [q_ref[:, s] for s in lane]
        uincl = jnp.where(_tri(bk, "lower"), 1.0, 0.0).astype(BF16)

        def step(carry):
            it, cbs, accs = carry
            rows, _, weights, _, _, _, cbs = _sb_scores(qs, k_ref, qi, it, cbs, uincl)
            accs = list(accs)
            for e in range(hp):
                for r, a in zip(rows, weights[e]):
                    accs[e] = accs[e] + _dot(a.astype(BF16), v_ref[r, lane[e]], NN)
            return it + 1, tuple(cbs), tuple(accs)

        init = (jnp.int32(0), (jnp.zeros((bk, 1), F32),) * hp, (jnp.zeros((bk, HEAD_DIM), F32),) * hp)
        _, _, accs = lax.while_loop(functools.partial(_sb_more, qi), step, init)
        for e in range(hp):
            o_ref[:, lane[e]] = accs[e]

    groups = HEADS // hp
    (o,), extra = _host_call(
        body, name, comm, _sb_steps(groups, t // bk), [jax.ShapeDtypeStruct((t, WIDTH), F32)], (groups, t // bk),
        [pl.BlockSpec((bk, wide), lambda h, i: (i, h)),
         pl.BlockSpec((t, wide), lambda h, i: (0, groups + h)),
         pl.BlockSpec((t, wide), lambda h, i: (0, 2 * groups + h))],
        [pl.BlockSpec((bk, wide), lambda h, i: (i, h))], [], ("parallel", "arbitrary"), (qkv, qkv, qkv))
    return o, extra


def _sb_bwd(qkv, o, do, name, comm=None):
    t = qkv.shape[0]
    bk = ATT_BLOCK
    scale = HEAD_DIM ** -0.5
    hp, wide = SB_HEADS_PER_STEP, SB_HEADS_PER_STEP * HEAD_DIM
    lane = [slice(e * HEAD_DIM, (e + 1) * HEAD_DIM) for e in range(hp)]

    def body(q_ref, k_ref, v_ref, o_ref, do_ref, dq_ref, dk_ref, dv_ref):
        qi = pl.program_id(1)

        @pl.when(qi == 0)
        def _():
            dk_ref[...] = jnp.zeros_like(dk_ref)
            dv_ref[...] = jnp.zeros_like(dv_ref)

        heads, groups = range(hp), range(SB_GROUP)
        qs = [q_ref[:, s] for s in lane]
        dov = [do_ref[:, s] for s in lane]
        dob = [a.astype(BF16) for a in dov]
        dosplit = [_split2(a) for a in dov]
        dsum = [jnp.sum(dov[e] * o_ref[:, lane[e]], axis=1, keepdims=True) for e in heads]
        uincl = jnp.where(_tri(bk, "lower"), 1.0, 0.0).astype(BF16)

        def step(carry):
            it, cbs, ces, dqs = carry
            rows, ks, weights, mask, lk_full, ls, cbs = _sb_scores(qs, k_ref, qi, it, cbs, uincl)
            ab = [[a.astype(BF16) for a in weights[e]] for e in heads]
            vs = [[v_ref[r, lane[e]] for r in rows] for e in heads]
            dla = [[ab[e][g].astype(F32) * (_dot(dosplit[e][0], vs[e][g], NT) + _dot(dosplit[e][1], vs[e][g], NT))
                    for g in groups] for e in heads]
            suf = [[_dot_xr2(a, uincl, NN) for a in dla[e]] for e in heads]
            ces, dqs = list(ces), list(dqs)
            for e in heads:
                for g in groups:
                    err = dsum[e] - (ces[e] + suf[e][g])
                    ces[e] = ces[e] + suf[e][g][:, 0:1]
                    dz = jnp.where(mask[g], dla[e][g] * jnp.exp(lk_full[e][g]) - err * jnp.exp(ls[e][g]), 0.0)
                    dzb = (dz * scale).astype(BF16)
                    dqs[e] = dqs[e] + _dot(dzb, ks[e][g], NN)
                    dk_ref[rows[g], lane[e]] += _dot(dzb, qs[e], TN)
                    dv_ref[rows[g], lane[e]] += _dot(ab[e][g], dob[e], TN)
            return it + 1, tuple(cbs), tuple(ces), tuple(dqs)

        zc = (jnp.zeros((bk, 1), F32),) * hp
        init = (jnp.int32(0), zc, zc, (jnp.zeros((bk, HEAD_DIM), F32),) * hp)
        dqs = lax.while_loop(functools.partial(_sb_more, qi), step, init)[3]
        for e in heads:
            dq_ref[:, lane[e]] = dqs[e]

    ngroup = HEADS // hp
    tw = jax.ShapeDtypeStruct((t, WIDTH), F32)
    qb = pl.BlockSpec((bk, wide), lambda h, i: (i, h))
    full = lambda off: pl.BlockSpec((t, wide), lambda h, i: (0, off + h))
    return _host_call(
        body, name, comm, _sb_steps(ngroup, t // bk), [tw, tw, tw], (ngroup, t // bk),
        [qb, full(ngroup), full(2 * ngroup), qb, qb], [qb, full(0), full(0)], [], ("parallel", "arbitrary"),
        (qkv, qkv, qkv, o, do))


def _merge_fwd(pd, ps, gl, name):
    t = pd.shape[0]
    tr, tc = _tile(t, 512), 512
    nj = D_MODEL // tc

    def body(pd_ref, ps_ref, gd_ref, gs_ref, o_ref):
        o_ref[...] = (_sigmoid(gd_ref[...]) * pd_ref[...] + _sigmoid(gs_ref[...]) * ps_ref[...]).astype(BF16)

    blk = lambda off: pl.BlockSpec((tr, tc), lambda i, j: (i, j + off))
    return pl.pallas_call(
        body, name=name,
        out_shape=jax.ShapeDtypeStruct((t, D_MODEL), BF16),
        grid=(t // tr, nj),
        in_specs=[blk(0), blk(0), blk(0), blk(nj)],
        out_specs=blk(0),
        compiler_params=_params(("parallel", "parallel")),
    )(pd, ps, gl, gl)


def _merge_bwd(dm, pd, ps, gl, name):
    t = pd.shape[0]
    tr, tc = _tile(t, 512), 512
    nj = D_MODEL // tc

    def body(dm_ref, pd_ref, ps_ref, gd_ref, gs_ref, dpd_ref, dps_ref, dgd_ref, dgs_ref):
        dmv = dm_ref[...]
        sd, ss = _sigmoid(gd_ref[...]), _sigmoid(gs_ref[...])
        dpd_ref[...] = (dmv * sd).astype(BF16)
        dps_ref[...] = (dmv * ss).astype(BF16)
        dgd_ref[...] = (dmv * pd_ref[...] * sd * (1.0 - sd)).astype(BF16)
        dgs_ref[...] = (dmv * ps_ref[...] * ss * (1.0 - ss)).astype(BF16)

    blk = lambda off: pl.BlockSpec((tr, tc), lambda i, j: (i, j + off))
    out = jax.ShapeDtypeStruct((t, D_MODEL), BF16)
    return pl.pallas_call(
        body, name=name,
        out_shape=(out, out, out, out),
        grid=(t // tr, nj),
        in_specs=[blk(0), blk(0), blk(0), blk(0), blk(nj)],
        out_specs=(blk(0), blk(0), blk(0), blk(0)),
        compiler_params=_params(("parallel", "parallel")),
    )(dm, pd, ps, gl, gl)


def _local_step(x, target, wts, plan=None):
    n1 = _rmsnorm_fwd(x, wts["norm1_w"], "norm1_fwd")
    qkv_pre = _matmul(n1, wts["w_dnqkv_t"], "nt", F32, "in_dnqkv")
    hgate = _matmul(n1, wts["w_dngate_t"], "nt", F32, "in_dngate")
    sbqkv = _matmul(n1, wts["w_sbqkv_t"], "nt", BF16, "in_sbqkv")
    gl = _matmul(n1, wts["w_gl_t"], "nt", F32, "in_gl")
    hab = _matmul(n1, wts["w_ab_t"], "nt", F32, "in_ab")

    act = _dn_pre_fwd(qkv_pre, wts["dn_conv_w"], "dn_pre_fwd")
    gates = _dn_gates_fwd(hab, wts["alog"], wts["dtb"], "dn_gates_fwd")
    u, w, kd, qg, tinv, p = _dn_local_fwd(act, gates, "dn_local_fwd")
    o_dn, sh = _dn_scan_fwd(u, w, kd, qg, p, gates, "dn_scan_fwd")
    y_dn = _dn_post_fwd(o_dn, hgate, wts["dn_norm_w"], "dn_post_fwd")

    o_sb, late = _sb_fwd(sbqkv, "sb_fwd", comm=plan.late_gather() if plan else None)
    if plan:
        wts = {**wts, **plan.late_weights(late)}

    pd = _matmul(y_dn, wts["w_proj_dn"], "nn", F32, "proj_dn")
    ps = _matmul(o_sb, wts["w_proj_sb"], "nn", F32, "proj_sb")
    mixed = _merge_fwd(pd, ps, gl, "merge_fwd")
    x1 = _matmul(mixed, wts["w_out"], "nn", F32, "out_proj", add=x)

    n2 = _rmsnorm_fwd(x1, wts["norm2_w"], "norm2_fwd")
    upre = _matmul(n2, wts["ffn_w_up_t"], "nt", F32, "ffn_up")
    fact = _ffn_act_fwd(upre, wts["ffn_conv_w"], "ffn_act_fwd")
    x2 = _matmul(fact, wts["ffn_w_down"], "nn", F32, "ffn_down", add=x1)

    dx2, g_normf, loss = _final_loss(x2, target, wts["norm_f_w"], "final_loss")

    dfact = _matmul(dx2, wts["ffn_w_down"], "nt", BF16, "ffn_down_dx")
    g_wdown = _matmul(fact, dx2, "tn", BF16, "ffn_down_dw")
    dgc, duc, dwg, dwu = _ffn_act_bwd(dfact, upre, wts["ffn_conv_w"], "ffn_act_bwd")
    g_fconv = jnp.concatenate([dwg, dwu], axis=1)
    dupre = _conv_bwd_data([dgc, duc], wts["ffn_conv_w"], FFN_CONV, BF16, "ffn_conv_bwd")
    dn2 = _matmul(dupre, wts["ffn_w_up_t"], "nn", F32, "ffn_up_dx")
    g_wup = _matmul(dupre, n2, "tn", BF16, "ffn_up_dw")
    dx1, g_norm2 = _rmsnorm_bwd(dn2, x1, wts["norm2_w"], dx2, "norm2_bwd")

    dmixed = _matmul(dx1, wts["w_out"], "nt", F32, "out_proj_dx")
    g_wout = _matmul(mixed, dx1, "tn", BF16, "out_proj_dw")
    dpd, dps, dgd, dgs = _merge_bwd(dmixed, pd, ps, gl, "merge_bwd")
    dy_dn = _matmul(dpd, wts["w_proj_dn"], "nt", F32, "proj_dn_dx")
    g_wpd = _matmul(y_dn, dpd, "tn", BF16, "proj_dn_dw")
    do_sb = _matmul(dps, wts["w_proj_sb"], "nt", F32, "proj_sb_dx")
    g_wps = _matmul(o_sb, dps, "tn", BF16, "proj_sb_dw")
    grads = dict(w_proj_dn=g_wpd, w_proj_sb=g_wps, w_out=g_wout, ffn_w_up_t=g_wup, ffn_w_down=g_wdown)

    (dsq, dsk, dsv), got_early = _sb_bwd(sbqkv, o_sb, do_sb, "sb_bwd",
                                         comm=plan.early_grads(grads) if plan else None)

    do_dn, dhgate, g_dnnorm = _dn_post_bwd(dy_dn, o_dn, hgate, wts["dn_norm_w"], "dn_post_bwd")
    dvn, dsh = _dn_scan_bwd(do_dn, w, kd, qg, p, gates, "dn_scan_bwd")
    dq, dk, dv, dgates = _dn_local_bwd(act, gates, u, w, kd, qg, tinv, p, sh, dsh, dvn, do_dn, "dn_local_bwd")
    dhab, g_alog, g_dtb = _dn_gates_bwd(dgates, hab, wts["alog"], wts["dtb"], "dn_gates_bwd")
    dcv, g_dnconv = _dn_pre_bwd(dq, dk, dv, qkv_pre, wts["dn_conv_w"], "dn_pre_bwd")
    dqkv_pre = _conv_bwd_data([dcv], wts["dn_conv_w"], DN_CONV, BF16, "dn_conv_bwd")

    dh = jnp.concatenate([dqkv_pre, dhgate, dsq.astype(BF16), dsk.astype(BF16), dsv.astype(BF16), dgd, dgs], axis=1)
    w_main_t = jnp.concatenate([wts["w_dnqkv_t"], wts["w_dngate_t"], wts["w_sbqkv_t"], wts["w_gl_t"]], axis=0)
    g_wmain = _matmul(dh, n1, "tn", BF16, "in_dw_main")
    g_wab = _matmul(dhab, n1, "tn", BF16, "in_dw_ab")
    grads.update(w_main_t=g_wmain, w_ab_t=g_wab, dn_conv_w=g_dnconv, alog=g_alog, dtb=g_dtb, dn_norm_w=g_dnnorm,
                 norm2_w=g_norm2, ffn_conv_w=g_fconv, norm_f_w=g_normf)
    got_late = []
    if plan:
        dn1, got_late = _matmul(dh, w_main_t, "nn", F32, "in_dx_main", comm=plan.late_grads(grads, loss))
    else:
        dn1 = _matmul(dh, w_main_t, "nn", F32, "in_dx_main")
    dn1 = _matmul(dhab, wts["w_ab_t"], "nn", F32, "in_dx_ab", add=dn1)
    grad_x, g_norm1 = _rmsnorm_bwd(dn1, x, wts["norm1_w"], dx1, "norm1_bwd")
    grads["norm1_w"] = g_norm1
    return loss, grad_x, grads, got_early, got_late


HBM_SPEC = pl.BlockSpec(memory_space=pltpu.HBM)


def _mesh_pos():
    x, y, c = lax.axis_index("x"), lax.axis_index("y"), lax.axis_index("c")
    return x, y, c, 4 * x + 2 * y + c


def _peer(k):
    x, y, c, _ = _mesh_pos()
    px = 1 - x if k & 4 else x
    py = 1 - y if k & 2 else y
    pc = 1 - c if k & 1 else c
    return (px, py, pc), 4 * px + 2 * py + pc


def _rcopy(src, dst, send, recv, a, s, peer):
    return pltpu.make_async_remote_copy(src_ref=src, dst_ref=dst, send_sem=send.at[a, s], recv_sem=recv.at[a, s],
                                        device_id=peer, device_id_type=pl.DeviceIdType.MESH)


class _Gather:
    ICI = (2, 4, 6)

    def __init__(self, shards):
        self.args = list(shards)
        self.n = len(shards)
        self.out_shape = [jax.ShapeDtypeStruct((N_DEV,) + s.shape, s.dtype) for s in shards]
        self.scratch = [pltpu.SemaphoreType.DMA((self.n, N_DEV - 1)), pltpu.SemaphoreType.DMA((self.n, N_DEV - 1)),
                        pltpu.SemaphoreType.DMA((self.n,))]

    def _first(self, ins, outs, send, recv, a):
        me = _mesh_pos()[3]
        out, got = [], []
        for s, k in enumerate((1,) + self.ICI):
            peer, pidx = _peer(k)
            out.append(_rcopy(ins[a], outs[a].at[me], send, recv, a, s, peer))
            got.append(_rcopy(ins[a], outs[a].at[pidx], send, recv, a, s, peer))
        return out, got

    def _forward(self, ins, outs, send, recv, a):
        sib = _peer(1)[0]
        out, got = [], []
        for s, k in enumerate(self.ICI):
            held = outs[a].at[_peer(k)[1]]
            out.append(_rcopy(held, held, send, recv, a, 4 + s, sib))
            other = outs[a].at[_peer(k | 1)[1]]
            got.append(_rcopy(other, other, send, recv, a, 4 + s, sib))
        return out, got

    def start(self, ins, outs, sems):
        send, recv, loc = sems
        me = _mesh_pos()[3]
        for a in range(self.n):
            pltpu.make_async_copy(ins[a], outs[a].at[me], loc.at[a]).start()
            for cp in self._first(ins, outs, send, recv, a)[0]:
                cp.start()

    def mid(self, ins, outs, sems):
        send, recv, _ = sems
        for a in range(self.n):
            arrivals = self._first(ins, outs, send, recv, a)[1]
            for s, cp in enumerate(self._forward(ins, outs, send, recv, a)[0]):
                arrivals[1 + s].wait_recv()
                cp.start()

    def finish(self, ins, outs, sems):
        send, recv, loc = sems
        me = _mesh_pos()[3]
        for a in range(self.n):
            first_out, first_got = self._first(ins, outs, send, recv, a)
            fwd_out, fwd_got = self._forward(ins, outs, send, recv, a)
            first_got[0].wait_recv()
            for cp in fwd_got:
                cp.wait_recv()
            for cp in first_out + fwd_out:
                cp.wait_send()
            pltpu.make_async_copy(ins[a], outs[a].at[me], loc.at[a]).wait()


class _Exchange:
    def __init__(self, slabs, gathered=()):
        self.args = list(slabs) + list(gathered)
        self.n_slab = len(slabs)
        self.n = len(self.args)
        self.out_shape = ([jax.ShapeDtypeStruct(s.shape, s.dtype) for s in slabs]
                          + [jax.ShapeDtypeStruct((N_DEV,) + s.shape, s.dtype) for s in gathered])
        self.scratch = [pltpu.SemaphoreType.DMA((self.n, N_DEV - 1)), pltpu.SemaphoreType.DMA((self.n, N_DEV - 1)),
                        pltpu.SemaphoreType.DMA((self.n,))]

    def _copies(self, ins, outs, send, recv, a):
        me = _mesh_pos()[3]
        out, got = [], []
        for k in range(1, N_DEV):
            peer, pidx = _peer(k)
            src = ins[a].at[pidx] if a < self.n_slab else ins[a]
            out.append(_rcopy(src, outs[a].at[me], send, recv, a, k - 1, peer))
            got.append(_rcopy(src, outs[a].at[pidx], send, recv, a, k - 1, peer))
        return out, got

    def _local(self, ins, outs, loc, a):
        me = _mesh_pos()[3]
        return pltpu.make_async_copy(ins[a].at[me] if a < self.n_slab else ins[a], outs[a].at[me], loc.at[a])

    def start(self, ins, outs, sems):
        send, recv, loc = sems
        for a in range(self.n):
            self._local(ins, outs, loc, a).start()
            for cp in self._copies(ins, outs, send, recv, a)[0]:
                cp.start()

    def mid(self, ins, outs, sems):
        pass

    def finish(self, ins, outs, sems):
        send, recv, loc = sems
        for a in range(self.n):
            out, got = self._copies(ins, outs, send, recv, a)
            for cp in got:
                cp.wait_recv()
            for cp in out:
                cp.wait_send()
            self._local(ins, outs, loc, a).wait()


def _comm_call(comm, name):
    n = comm.n

    def body(*refs):
        ins, outs, sems = refs[:n], refs[n:2 * n], refs[2 * n:]
        comm.start(ins, outs, sems)
        comm.mid(ins, outs, sems)
        comm.finish(ins, outs, sems)

    return pl.pallas_call(
        body, name=name, out_shape=comm.out_shape, in_specs=[HBM_SPEC] * n, out_specs=[HBM_SPEC] * n,
        scratch_shapes=comm.scratch,
    )(*comm.args)


def _hosted(body, comm, n_in, n_out, when):
    if comm is None:
        return body

    def wrapped(*refs):
        ins, c_ins = refs[:n_in], refs[n_in:n_in + comm.n]
        o0 = n_in + comm.n
        outs, c_outs = refs[o0:o0 + n_out], refs[o0 + n_out:o0 + n_out + comm.n]
        scratch, sems = refs[o0 + n_out + comm.n:len(refs) - 3], refs[len(refs) - 3:]
        first, middle, last = when()

        @pl.when(first)
        def _():
            comm.start(c_ins, c_outs, sems)

        body(*ins, *outs, *scratch)

        @pl.when(middle)
        def _():
            comm.mid(c_ins, c_outs, sems)

        @pl.when(last)
        def _():
            comm.finish(c_ins, c_outs, sems)

    return wrapped


def _host_call(body, name, comm, when, out_shape, grid, in_specs, out_specs, scratch_shapes, sem, args):
    n_in, n_out = len(in_specs), len(out_specs)
    if comm is None:
        res = pl.pallas_call(body, name=name, out_shape=out_shape, grid=grid, in_specs=in_specs, out_specs=out_specs,
                             scratch_shapes=scratch_shapes, compiler_params=_params(sem))(*args)
        return list(res), []
    res = pl.pallas_call(
        _hosted(body, comm, n_in, n_out, when), name=name,
        out_shape=list(out_shape) + comm.out_shape, grid=grid,
        in_specs=list(in_specs) + [HBM_SPEC] * comm.n, out_specs=list(out_specs) + [HBM_SPEC] * comm.n,
        scratch_shapes=list(scratch_shapes) + comm.scratch,
        compiler_params=_params(("arbitrary",) * len(grid)),
    )(*args, *comm.args)
    return list(res[:n_out]), list(res[n_out:])


def _adamw(parts, w, m, v, name):
    rows, cols = w.shape
    tr, tc = rows, cols
    for cand in (128, 176):
        if rows > cand and rows % cand == 0:
            tr = cand
            break
    if tr == rows and rows > 512:
        tc = _tile(cols, 256)

    def body(p_ref, w_ref, m_ref, v_ref, g_ref, d_ref, mo_ref, vo_ref):
        g = p_ref[0].astype(F32)
        for s in range(1, N_DEV):
            g = g + p_ref[s].astype(F32)
        mn = ADAM_B1 * m_ref[...] + (1.0 - ADAM_B1) * g
        vn = ADAM_B2 * v_ref[...] + (1.0 - ADAM_B2) * (g * g)
        m_hat = mn / (1.0 - ADAM_B1 ** ADAM_STEP)
        v_hat = vn / (1.0 - ADAM_B2 ** ADAM_STEP)
        g_ref[...] = g
        d_ref[...] = -ADAM_LR * (m_hat / (jnp.sqrt(v_hat) + ADAM_EPS) + ADAM_WD * w_ref[...])
        mo_ref[...] = mn
        vo_ref[...] = vn

    blk = pl.BlockSpec((tr, tc), lambda i, j: (i, j))
    out = jax.ShapeDtypeStruct((rows, cols), F32)
    return pl.pallas_call(
        body, name=name,
        out_shape=(out, out, out, out),
        grid=(rows // tr, cols // tc),
        in_specs=[pl.BlockSpec((N_DEV, tr, tc), lambda i, j: (0, i, j)), blk, blk, blk],
        out_specs=(blk, blk, blk, blk),
        compiler_params=_params(("parallel", "parallel")),
    )(parts, w, m, v)


CONV_PACK = 8 * 1024
WEIGHT_ORDER = ("norm1_w", "w_in", "dn_conv_w", "dn_A_log", "dn_dt_bias", "dn_norm_w", "w_proj_dn", "w_proj_sb",
                "w_out", "norm2_w", "ffn_w_up", "ffn_conv_w", "ffn_w_down", "norm_f_w")


def _cols_to_slabs(g):
    r, c8 = g.shape
    return g.reshape(r, N_DEV, c8 // N_DEV).transpose(1, 0, 2)


def _slabs_to_cols(s):
    d, r, c = s.shape
    return s.transpose(1, 0, 2).reshape(r, d * c)


def kernel(x, norm1_w, w_in, dn_conv_w, dn_A_log, dn_dt_bias, dn_norm_w, w_proj_dn, w_proj_sb, w_out, norm2_w, ffn_w_up, ffn_conv_w, ffn_w_down, norm_f_w, loss_target, m_norm1_w, m_w_in, m_dn_conv_w, m_dn_A_log, m_dn_dt_bias, m_dn_norm_w, m_w_proj_dn, m_w_proj_sb, m_w_out, m_norm2_w, m_ffn_w_up, m_ffn_conv_w, m_ffn_w_down, m_norm_f_w, v_norm1_w, v_w_in, v_dn_conv_w, v_dn_A_log, v_dn_dt_bias, v_dn_norm_w, v_w_proj_dn, v_w_proj_sb, v_w_out, v_norm2_w, v_ffn_w_up, v_ffn_conv_w, v_ffn_w_down, v_norm_f_w):
    me = _mesh_pos()[3]
    tr = lambda a: jnp.transpose(a[0])
    w_loc = dict(norm1_w=norm1_w, w_in=tr(w_in), dn_conv_w=dn_conv_w[0], dn_A_log=dn_A_log, dn_dt_bias=dn_dt_bias,
                 dn_norm_w=dn_norm_w, w_proj_dn=w_proj_dn[0], w_proj_sb=w_proj_sb[0], w_out=w_out[0],
                 norm2_w=norm2_w, ffn_w_up=tr(ffn_w_up), ffn_conv_w=ffn_conv_w[0], ffn_w_down=ffn_w_down[0],
                 norm_f_w=norm_f_w[None, :])
    m_loc = dict(norm1_w=m_norm1_w, w_in=tr(m_w_in), dn_conv_w=m_dn_conv_w[0], dn_A_log=m_dn_A_log,
                 dn_dt_bias=m_dn_dt_bias, dn_norm_w=m_dn_norm_w, w_proj_dn=m_w_proj_dn[0], w_proj_sb=m_w_proj_sb[0],
                 w_out=m_w_out[0], norm2_w=m_norm2_w, ffn_w_up=tr(m_ffn_w_up), ffn_conv_w=m_ffn_conv_w[0],
                 ffn_w_down=m_ffn_w_down[0], norm_f_w=m_norm_f_w[None, :])
    v_loc = dict(norm1_w=v_norm1_w, w_in=tr(v_w_in), dn_conv_w=v_dn_conv_w[0], dn_A_log=v_dn_A_log,
                 dn_dt_bias=v_dn_dt_bias, dn_norm_w=v_dn_norm_w, w_proj_dn=v_w_proj_dn[0], w_proj_sb=v_w_proj_sb[0],
                 w_out=v_w_out[0], norm2_w=v_norm2_w, ffn_w_up=tr(v_ffn_w_up), ffn_conv_w=v_ffn_conv_w[0],
                 ffn_w_down=v_ffn_w_down[0], norm_f_w=v_norm_f_w[None, :])

    conv_flat = jnp.concatenate([w_loc["dn_conv_w"].reshape(-1), w_loc["ffn_conv_w"].reshape(-1)])
    n_dn, n_ffn = DN_CONV * 3 * WIDTH // N_DEV, FFN_CONV * 2 * D_FF // N_DEV
    conv_pack = jnp.pad(conv_flat, (0, CONV_PACK - n_dn - n_ffn)).reshape(8, 1024)
    g_in, g_conv = _comm_call(_Gather([w_loc["w_in"].astype(BF16), conv_pack]), "gather_first")
    in_width = g_in.shape[0] * g_in.shape[1]
    w_in_t = g_in.reshape(in_width, D_MODEL)
    g_conv = g_conv.reshape(N_DEV, CONV_PACK)
    dn_conv_full = _slabs_to_cols(g_conv[:, :n_dn].reshape(N_DEV, DN_CONV, 3 * WIDTH // N_DEV))
    ffn_conv_full = _slabs_to_cols(g_conv[:, n_dn:n_dn + n_ffn].reshape(N_DEV, FFN_CONV, 2 * D_FF // N_DEV))
    q_end = 3 * WIDTH
    ab_end = q_end + 2 * HEADS
    gate_end = ab_end + WIDTH
    sb_end = gate_end + 3 * WIDTH
    pad_lanes = lambda a: jnp.pad(a, ((0, 0), (0, 128 - a.shape[1])))
    wts = dict(
        norm1_w=norm1_w, w_dnqkv_t=w_in_t[:q_end], w_ab_t=jnp.pad(w_in_t[q_end:ab_end], ((0, 128 - 2 * HEADS), (0, 0))),
        w_dngate_t=w_in_t[ab_end:gate_end], w_sbqkv_t=w_in_t[gate_end:sb_end], w_gl_t=w_in_t[sb_end:],
        dn_conv_w=dn_conv_full, alog=pad_lanes(dn_A_log), dtb=pad_lanes(dn_dt_bias), dn_norm_w=dn_norm_w,
        norm2_w=norm2_w, ffn_conv_w=ffn_conv_full, norm_f_w=norm_f_w[None, :])

    n_fc = FFN_CONV * 2 * D_FF
    fc_rows = -(-n_fc // D_MODEL)
    dn_rows = DN_CONV * 3 * WIDTH // D_MODEL
    late_names = ("w_proj_dn", "w_proj_sb", "w_out", "ffn_w_up", "ffn_w_down")

    class Plan:
        @staticmethod
        def late_gather():
            return _Gather([w_loc[k].astype(BF16) for k in late_names])

        @staticmethod
        def late_weights(got):
            g_pd, g_ps, g_out, g_up, g_down = got
            return dict(w_proj_dn=g_pd.reshape(WIDTH, D_MODEL), w_proj_sb=g_ps.reshape(WIDTH, D_MODEL),
                        w_out=g_out.reshape(D_MODEL, D_MODEL), ffn_w_up_t=g_up.reshape(2 * D_FF, D_MODEL),
                        ffn_w_down=g_down.reshape(D_FF, D_MODEL))

        @staticmethod
        def early_grads(g):
            return _Exchange([g["w_proj_dn"].reshape(N_DEV, WIDTH // N_DEV, D_MODEL),
                              g["w_proj_sb"].reshape(N_DEV, WIDTH // N_DEV, D_MODEL),
                              g["w_out"].reshape(N_DEV, D_MODEL // N_DEV, D_MODEL),
                              g["ffn_w_up_t"].reshape(N_DEV, 2 * D_FF // N_DEV, D_MODEL),
                              g["ffn_w_down"].reshape(N_DEV, D_FF // N_DEV, D_MODEL)])

        @staticmethod
        def late_grads(g, loss):
            g_win_t = jnp.concatenate([g["w_main_t"][:q_end], g["w_ab_t"][:2 * HEADS], g["w_main_t"][q_end:]],
                                      axis=0)
            row3 = jnp.concatenate([g["dn_norm_w"], g["alog"], g["dtb"], jnp.pad(loss, ((0, 0), (0, 127))),
                                    jnp.zeros((1, D_MODEL - 512), F32)], axis=1)
            fconv_rows = jnp.pad(g["ffn_conv_w"].reshape(-1), (0, fc_rows * D_MODEL - n_fc)).reshape(fc_rows, D_MODEL)
            pad8 = lambda a: jnp.pad(a, ((0, -a.shape[0] % 8), (0, 0)))
            pieces = [g["norm2_w"], g["norm_f_w"], row3, g["dn_conv_w"].reshape(dn_rows, D_MODEL), fconv_rows]
            small = jnp.concatenate([pad8(a) for a in pieces], axis=0)
            assert small.shape[0] == SMALL_ROWS
            return _Exchange([g_win_t.reshape(N_DEV, in_width // N_DEV, D_MODEL)], [small])

    loss, grad_x, g, got_early, got_late = _local_step(x[0], loss_target[0], wts, Plan)
    r_pd, r_ps, r_out, r_up, r_down = got_early
    r_in, r_small = got_late
    (r_norm1,) = _comm_call(_Exchange([], [jnp.pad(g["norm1_w"], ((0, 7), (0, 0)))]), "gather_norm1")

    parts = dict(w_in=r_in, w_proj_dn=r_pd, w_proj_sb=r_ps, w_out=r_out, ffn_w_up=r_up, ffn_w_down=r_down)
    parts["norm1_w"] = r_norm1[:, 0:1, :]
    parts["norm2_w"] = r_small[:, 0:1, :]
    parts["norm_f_w"] = r_small[:, 8:9, :]
    parts["dn_norm_w"] = r_small[:, 16:17, 0:HEAD_DIM]
    parts["dn_A_log"] = r_small[:, 16:17, 128:128 + HEADS]
    parts["dn_dt_bias"] = r_small[:, 16:17, 256:256 + HEADS]
    dnc = r_small[:, 24:24 + dn_rows, :].reshape(N_DEV, DN_CONV, 3 * WIDTH)
    parts["dn_conv_w"] = lax.dynamic_slice_in_dim(dnc, me * (3 * WIDTH // N_DEV), 3 * WIDTH // N_DEV, axis=2)
    fc0 = 24 + dn_rows + (-dn_rows % 8)
    fcc = r_small[:, fc0:fc0 + fc_rows, :].reshape(N_DEV, fc_rows * D_MODEL)[:, :n_fc]
    fcc = fcc.reshape(N_DEV, FFN_CONV, 2 * D_FF)
    parts["ffn_conv_w"] = lax.dynamic_slice_in_dim(fcc, me * (2 * D_FF // N_DEV), 2 * D_FF // N_DEV, axis=2)
    loss_total = jnp.sum(r_small[:, 16, 384])

    res = {k: _adamw(parts[k], w_loc[k], m_loc[k], v_loc[k], "adamw_" + k) for k in WEIGHT_ORDER}
    lead = ("w_in", "dn_conv_w", "w_proj_dn", "w_proj_sb", "w_out", "ffn_w_up", "ffn_conv_w", "ffn_w_down")

    def shaped(k, a):
        if k in ("w_in", "ffn_w_up"):
            return jnp.transpose(a)[None]
        if k in lead:
            return a[None]
        if k == "norm_f_w":
            return a[0]
        return a

    outs = [loss_total, grad_x[None]]
    for idx in range(4):
        outs += [shaped(k, res[k][idx]) for k in WEIGHT_ORDER]
    return tuple(outs)
```

```python
import functools

import jax
import jax.numpy as jnp
from jax import lax
from jax.experimental import pallas as pl
from jax.experimental.pallas import tpu as pltpu

F32 = jnp.float32
BF16 = jnp.bfloat16

N_DEV = 8
D_MODEL = 1024
HEADS = 8
HEAD_DIM = 128
WIDTH = HEADS * HEAD_DIM
DN_CONV = 4
DN_CHUNK = 64
D_FF = 2816
FFN_CONV = 3
EPS = 1e-6
HALO = 16
ATT_BLOCK = 256
SB_LOG_ZERO = -104.0
SB_GROUP = 2
SB_HEADS_PER_STEP = 2
SMALL_ROWS = 64

ADAM_LR = 0.001
ADAM_B1 = 0.9
ADAM_B2 = 0.999
ADAM_EPS = 1e-08
ADAM_WD = 0.01
ADAM_STEP = 10

VMEM_LIMIT = 48 * 1024 * 1024


def _params(sem=None, **kw):
    return pltpu.CompilerParams(dimension_semantics=sem, vmem_limit_bytes=VMEM_LIMIT, **kw)


def _tile(n, cap):
    if n <= cap:
        return n
    best = None
    for t in range(128, cap + 1, 128):
        if n % t == 0:
            best = t
    assert best is not None, (n, cap)
    return best


def _dot(a, b, dims):
    return lax.dot_general(a, b, ((dims[0], dims[1]), ((), ())), preferred_element_type=F32)


NN = ((1,), (0,))
NT = ((1,), (1,))
TN = ((0,), (0,))


def _dotb(a, b, dims):
    return _dot(a.astype(BF16), b.astype(BF16), dims)


def _split3(x):
    h1 = x.astype(BF16)
    r1 = x - h1.astype(F32)
    h2 = r1.astype(BF16)
    r2 = r1 - h2.astype(F32)
    return h1, h2, r2.astype(BF16)


def _dot_xr(a, b_exact, dims):
    a1, a2, a3 = _split3(a)
    return _dot(a1, b_exact, dims) + _dot(a2, b_exact, dims) + _dot(a3, b_exact, dims)


def _split2(x):
    h1 = x.astype(BF16)
    return h1, (x - h1.astype(F32)).astype(BF16)


def _dot_xr2(a, b_exact, dims):
    a1, a2 = _split2(a)
    return _dot(a1, b_exact, dims) + _dot(a2, b_exact, dims)


def _dot_xl(a_exact, b, dims):
    b1, b2, b3 = _split3(b)
    return _dot(a_exact, b1, dims) + _dot(a_exact, b2, dims) + _dot(a_exact, b3, dims)


def _dot3(a, b, dims):
    a1 = a.astype(BF16)
    a2 = (a - a1.astype(F32)).astype(BF16)
    b1 = b.astype(BF16)
    b2 = (b - b1.astype(F32)).astype(BF16)
    return _dot(a1, b1, dims) + (_dot(a1, b2, dims) + _dot(a2, b1, dims))


def _sigmoid(x):
    return 1.0 / (1.0 + jnp.exp(-x))


def _log1pexp_neg_abs(x):
    return jnp.log(1.0 + jnp.exp(-jnp.abs(x)))


def _iota(shape, dim):
    return lax.broadcasted_iota(jnp.int32, shape, dim)


def _matmul(a, b, mode, out_dtype, name, add=None, comm=None):
    if mode == "nn":
        (m, k), (k2, n) = a.shape, b.shape
    elif mode == "nt":
        (m, k), (n, k2) = a.shape, b.shape
    else:
        (k, m), (k2, n) = a.shape, b.shape
    assert k == k2, (a.shape, b.shape, mode)
    tm, tn, tk = _tile(m, 1408), _tile(n, 1408), _tile(k, 1536)
    nk = k // tk
    dims = {"nn": NN, "nt": NT, "tn": TN}[mode]

    def body(*refs):
        if add is None:
            a_ref, b_ref, o_ref, acc_ref = refs
        else:
            a_ref, b_ref, add_ref, o_ref, acc_ref = refs
        kk = pl.program_id(2)

        @pl.when(kk == 0)
        def _():
            acc_ref[...] = jnp.zeros_like(acc_ref)

        acc_ref[...] += _dotb(a_ref[...], b_ref[...], dims)

        @pl.when(kk == nk - 1)
        def _():
            r = acc_ref[...]
            if add is not None:
                r = r + add_ref[...].astype(F32)
            o_ref[...] = r.astype(out_dtype)

    if mode == "nn":
        specs = [pl.BlockSpec((tm, tk), lambda i, j, l: (i, l)), pl.BlockSpec((tk, tn), lambda i, j, l: (l, j))]
    elif mode == "nt":
        specs = [pl.BlockSpec((tm, tk), lambda i, j, l: (i, l)), pl.BlockSpec((tn, tk), lambda i, j, l: (j, l))]
    else:
        specs = [pl.BlockSpec((tk, tm), lambda i, j, l: (l, i)), pl.BlockSpec((tk, tn), lambda i, j, l: (l, j))]
    args = [a, b]
    if add is not None:
        specs.append(pl.BlockSpec((tm, tn), lambda i, j, l: (i, j)))
        args.append(add)
    grid = (m // tm, n // tn, nk)

    def when():
        i, j, l = pl.program_id(0), pl.program_id(1), pl.program_id(2)
        first = jnp.logical_and(jnp.logical_and(i == 0, j == 0), l == 0)
        last = jnp.logical_and(jnp.logical_and(i == grid[0] - 1, j == grid[1] - 1), l == nk - 1)
        return first, last, last

    (out,), extra = _host_call(
        body, name, comm, when, [jax.ShapeDtypeStruct((m, n), out_dtype)], grid, specs,
        [pl.BlockSpec((tm, tn), lambda i, j, l: (i, j))], [pltpu.VMEM((tm, tn), F32)],
        ("parallel", "parallel", "arbitrary"), args)
    return out if comm is None else (out, extra)


def _rmsnorm_fwd(x, w, name):
    t, d = x.shape
    tr = _tile(t, 512)

    def body(x_ref, w_ref, o_ref):
        xv = x_ref[...]
        r = lax.rsqrt(jnp.mean(xv * xv, axis=1, keepdims=True) + EPS)
        o_ref[...] = (xv * r * w_ref[...]).astype(BF16)

    return pl.pallas_call(
        body, name=name,
        out_shape=jax.ShapeDtypeStruct((t, d), BF16),
        grid=(t // tr,),
        in_specs=[pl.BlockSpec((tr, d), lambda i: (i, 0)), pl.BlockSpec((1, d), lambda i: (0, 0))],
        out_specs=pl.BlockSpec((tr, d), lambda i: (i, 0)),
        compiler_params=_params(("parallel",)),
    )(x, w)


def _rmsnorm_bwd(dn, x, w, dres, name):
    t, d = x.shape
    tr = _tile(t, 512)

    def body(dn_ref, x_ref, w_ref, dres_ref, dx_ref, dw_ref):
        i = pl.program_id(0)
        xv = x_ref[...]
        g = dn_ref[...].astype(F32)
        r = lax.rsqrt(jnp.mean(xv * xv, axis=1, keepdims=True) + EPS)
        xh = xv * r
        dxh = g * w_ref[...]
        dx = r * (dxh - xh * jnp.mean(dxh * xh, axis=1, keepdims=True))
        dx_ref[...] = dres_ref[...] + dx

        @pl.when(i == 0)
        def _():
            dw_ref[...] = jnp.zeros_like(dw_ref)

        dw_ref[...] += jnp.sum(g * xh, axis=0, keepdims=True)

    return pl.pallas_call(
        body, name=name,
        out_shape=(jax.ShapeDtypeStruct((t, d), F32), jax.ShapeDtypeStruct((1, d), F32)),
        grid=(t // tr,),
        in_specs=[pl.BlockSpec((tr, d), lambda i: (i, 0)), pl.BlockSpec((tr, d), lambda i: (i, 0)),
                  pl.BlockSpec((1, d), lambda i: (0, 0)), pl.BlockSpec((tr, d), lambda i: (i, 0))],
        out_specs=(pl.BlockSpec((tr, d), lambda i: (i, 0)), pl.BlockSpec((1, d), lambda i: (0, 0))),
        compiler_params=_params(("arbitrary",)),
    )(dn, x, w, dres)


def _final_loss(x2, target, w, name):
    t, d = x2.shape
    tr = _tile(t, 512)

    def body(x_ref, t_ref, w_ref, dx_ref, dw_ref, loss_ref):
        i = pl.program_id(0)
        xv = x_ref[...]
        r = lax.rsqrt(jnp.mean(xv * xv, axis=1, keepdims=True) + EPS)
        xh = xv * r
        err = xh * w_ref[...] - t_ref[...]
        dy = err * (1.0 / d)
        dxh = dy * w_ref[...]
        dx_ref[...] = r * (dxh - xh * jnp.mean(dxh * xh, axis=1, keepdims=True))

        @pl.when(i == 0)
        def _():
            dw_ref[...] = jnp.zeros_like(dw_ref)
            loss_ref[...] = jnp.zeros_like(loss_ref)

        dw_ref[...] += jnp.sum(dy * xh, axis=0, keepdims=True)
        row = jnp.sum(err * err, axis=1, keepdims=True) * (0.5 / d)
        loss_ref[...] += jnp.sum(row, axis=0, keepdims=True)

    return pl.pallas_call(
        body, name=name,
        out_shape=(jax.ShapeDtypeStruct((t, d), F32), jax.ShapeDtypeStruct((1, d), F32),
                   jax.ShapeDtypeStruct((1, 1), F32)),
        grid=(t // tr,),
        in_specs=[pl.BlockSpec((tr, d), lambda i: (i, 0)), pl.BlockSpec((tr, d), lambda i: (i, 0)),
                  pl.BlockSpec((1, d), lambda i: (0, 0))],
        out_specs=(pl.BlockSpec((tr, d), lambda i: (i, 0)), pl.BlockSpec((1, d), lambda i: (0, 0)),
                   pl.BlockSpec((1, 1), lambda i: (0, 0))),
        compiler_params=_params(("arbitrary",)),
    )(x2, target, w)


def _shift_down(cur, prev, k, row):
    r = pltpu.roll(cur, k, 0)
    for m in range(k):
        r = jnp.where(row == m, prev[HALO - k + m:HALO - k + m + 1, :], r)
    return r


def _shift_up(cur, nxt, k, row, tr):
    r = pltpu.roll(cur, tr - k, 0)
    for m in range(k):
        r = jnp.where(row == tr - k + m, nxt[m:m + 1, :], r)
    return r


def _conv_taps(cur, prev, w, ntaps, row):
    taps = [cur if i == ntaps - 1 else _shift_down(cur, prev, ntaps - 1 - i, row) for i in range(ntaps)]
    y = w[0:1, :] * taps[0]
    for i in range(1, ntaps):
        y = y + w[i:i + 1, :] * taps[i]
    return taps, y


def _conv_bwd_data(parts, w, ntaps, out_dtype, name):
    t, chp = parts[0].shape
    npart = len(parts)
    tr, tc = _tile(t, 512), _tile(chp, 1408)
    nc = chp // tc
    nhalo = t // HALO
    last = t // tr - 1

    def body(*refs):
        cur_refs, nxt_refs = refs[:npart], refs[npart:2 * npart]
        w_ref, o_ref = refs[2 * npart], refs[2 * npart + 1]
        i, j = pl.program_id(0), pl.program_id(1)
        cur, nxt = cur_refs[0][...].astype(F32), nxt_refs[0][...].astype(F32)
        for p in range(1, npart):
            cur = jnp.where(j >= p * nc, cur_refs[p][...].astype(F32), cur)
            nxt = jnp.where(j >= p * nc, nxt_refs[p][...].astype(F32), nxt)
        nxt = jnp.where(i == last, 0.0, nxt)
        row = _iota(cur.shape, 0)
        wv = w_ref[...]
        y = wv[ntaps - 1:ntaps, :] * cur
        for k in range(1, ntaps):
            y = y + wv[ntaps - 1 - k:ntaps - k, :] * _shift_up(cur, nxt, k, row, tr)
        o_ref[...] = y.astype(out_dtype)

    col = lambda p: (lambda j: jnp.clip(j - p * nc, 0, nc - 1))
    cur_specs = [pl.BlockSpec((tr, tc), lambda i, j, c=col(p): (i, c(j))) for p in range(npart)]
    nxt_specs = [pl.BlockSpec((HALO, tc),
                              lambda i, j, c=col(p): (jnp.minimum((i + 1) * (tr // HALO), nhalo - 1), c(j)))
                 for p in range(npart)]
    return pl.pallas_call(
        body, name=name,
        out_shape=jax.ShapeDtypeStruct((t, npart * chp), out_dtype),
        grid=(t // tr, npart * nc),
        in_specs=cur_specs + nxt_specs + [pl.BlockSpec((ntaps, tc), lambda i, j: (0, j))],
        out_specs=pl.BlockSpec((tr, tc), lambda i, j: (i, j)),
        compiler_params=_params(("parallel", "parallel")),
    )(*parts, *parts, w)


def _ffn_act_fwd(upre, cw, name):
    t = upre.shape[0]
    tr, tc = _tile(t, 512), _tile(D_FF, 1408)
    nj = D_FF // tc

    def body(g_ref, gp_ref, u_ref, up_ref, wg_ref, wu_ref, o_ref):
        i = pl.program_id(0)
        row = _iota((tr, tc), 0)
        gp = jnp.where(i == 0, 0.0, gp_ref[...].astype(F32))
        up = jnp.where(i == 0, 0.0, up_ref[...].astype(F32))
        _, gc = _conv_taps(g_ref[...].astype(F32), gp, wg_ref[...], FFN_CONV, row)
        _, uc = _conv_taps(u_ref[...].astype(F32), up, wu_ref[...], FFN_CONV, row)
        o_ref[...] = (gc * _sigmoid(gc) * uc).astype(BF16)

    prev = lambda off: (lambda i, j: (jnp.maximum(i * (tr // HALO) - 1, 0), j + off))
    return pl.pallas_call(
        body, name=name,
        out_shape=jax.ShapeDtypeStruct((t, D_FF), BF16),
        grid=(t // tr, nj),
        in_specs=[pl.BlockSpec((tr, tc), lambda i, j: (i, j)), pl.BlockSpec((HALO, tc), prev(0)),
                  pl.BlockSpec((tr, tc), lambda i, j: (i, j + nj)), pl.BlockSpec((HALO, tc), prev(nj)),
                  pl.BlockSpec((FFN_CONV, tc), lambda i, j: (0, j)),
                  pl.BlockSpec((FFN_CONV, tc), lambda i, j: (0, j + nj))],
        out_specs=pl.BlockSpec((tr, tc), lambda i, j: (i, j)),
        compiler_params=_params(("parallel", "parallel")),
    )(upre, upre, upre, upre, cw, cw)


def _ffn_act_bwd(dact, upre, cw, name):
    t = upre.shape[0]
    tr, tc = _tile(t, 256), _tile(D_FF, 1408)
    nj = D_FF // tc

    def body(da_ref, g_ref, gp_ref, u_ref, up_ref, wg_ref, wu_ref, dg_ref, du_ref, dwg_ref, dwu_ref):
        i = pl.program_id(1)
        row = _iota((tr, tc), 0)
        gp = jnp.where(i == 0, 0.0, gp_ref[...].astype(F32))
        up = jnp.where(i == 0, 0.0, up_ref[...].astype(F32))
        gt, gc = _conv_taps(g_ref[...].astype(F32), gp, wg_ref[...], FFN_CONV, row)
        ut, uc = _conv_taps(u_ref[...].astype(F32), up, wu_ref[...], FFN_CONV, row)
        da = da_ref[...].astype(F32)
        sg = _sigmoid(gc)
        dgc = da * uc * (sg * (1.0 + gc * (1.0 - sg)))
        duc = da * (gc * sg)
        dg_ref[...] = dgc.astype(BF16)
        du_ref[...] = duc.astype(BF16)

        @pl.when(i == 0)
        def _():
            dwg_ref[...] = jnp.zeros_like(dwg_ref)
            dwu_ref[...] = jnp.zeros_like(dwu_ref)

        for k in range(FFN_CONV):
            dwg_ref[k:k + 1, :] += jnp.sum(dgc * gt[k], axis=0, keepdims=True)
            dwu_ref[k:k + 1, :] += jnp.sum(duc * ut[k], axis=0, keepdims=True)

    prev = lambda off: (lambda j, i: (jnp.maximum(i * (tr // HALO) - 1, 0), j + off))
    blk = lambda off: pl.BlockSpec((tr, tc), lambda j, i: (i, j + off))
    wblk = lambda off: pl.BlockSpec((FFN_CONV, tc), lambda j, i: (0, j + off))
    dgc, duc, dwg, dwu = pl.pallas_call(
        body, name=name,
        out_shape=(jax.ShapeDtypeStruct((t, D_FF), BF16), jax.ShapeDtypeStruct((t, D_FF), BF16),
                   jax.ShapeDtypeStruct((FFN_CONV, D_FF), F32), jax.ShapeDtypeStruct((FFN_CONV, D_FF), F32)),
        grid=(nj, t // tr),
        in_specs=[blk(0), blk(0), pl.BlockSpec((HALO, tc), prev(0)), blk(nj), pl.BlockSpec((HALO, tc), prev(nj)),
                  wblk(0), wblk(nj)],
        out_specs=(blk(0), blk(0), wblk(0), wblk(0)),
        compiler_params=_params(("parallel", "arbitrary")),
    )(dact, upre, upre, upre, upre, cw, cw)
    return dgc, duc, dwg, dwu


def _dn_pre_fwd(qkv_pre, cw, name):
    t = qkv_pre.shape[0]
    tr = _tile(t, 512)
    scale = HEAD_DIM ** -0.5

    def body(x_ref, p_ref, w_ref, o_ref):
        i, j = pl.program_id(0), pl.program_id(1)
        row = _iota((tr, WIDTH), 0)
        prev = jnp.where(i == 0, 0.0, p_ref[...].astype(F32))
        _, c = _conv_taps(x_ref[...].astype(F32), prev, w_ref[...], DN_CONV, row)
        s = c * _sigmoid(c)
        for h in range(HEADS):
            sl = slice(h * HEAD_DIM, (h + 1) * HEAD_DIM)
            sh = s[:, sl]
            r = lax.rsqrt(jnp.sum(sh * sh, axis=1, keepdims=True) + EPS)
            o_ref[:, sl] = sh * jnp.where(j == 0, r * scale, jnp.where(j == 1, r, 1.0))

    return pl.pallas_call(
        body, name=name,
        out_shape=jax.ShapeDtypeStruct((t, 3 * WIDTH), F32),
        grid=(t // tr, 3),
        in_specs=[pl.BlockSpec((tr, WIDTH), lambda i, j: (i, j)),
                  pl.BlockSpec((HALO, WIDTH), lambda i, j: (jnp.maximum(i * (tr // HALO) - 1, 0), j)),
                  pl.BlockSpec((DN_CONV, WIDTH), lambda i, j: (0, j))],
        out_specs=pl.BlockSpec((tr, WIDTH), lambda i, j: (i, j)),
        compiler_params=_params(("parallel", "parallel")),
    )(qkv_pre, qkv_pre, cw)


def _dn_pre_bwd(dq, dk, dv, qkv_pre, cw, name):
    t = qkv_pre.shape[0]
    tr = _tile(t, 256)
    scale = HEAD_DIM ** -0.5

    def body(dq_ref, dk_ref, dv_ref, x_ref, p_ref, w_ref, dc_ref, dw_ref):
        j, i = pl.program_id(0), pl.program_id(1)
        row = _iota((tr, WIDTH), 0)
        prev = jnp.where(i == 0, 0.0, p_ref[...].astype(F32))
        taps, c = _conv_taps(x_ref[...].astype(F32), prev, w_ref[...], DN_CONV, row)
        d = jnp.where(j == 0, dq_ref[...] * scale, jnp.where(j == 1, dk_ref[...], dv_ref[...]))
        sg = _sigmoid(c)
        s = c * sg
        dsilu = sg * (1.0 + c * (1.0 - sg))

        @pl.when(i == 0)
        def _():
            dw_ref[...] = jnp.zeros_like(dw_ref)

        for h in range(HEADS):
            sl = slice(h * HEAD_DIM, (h + 1) * HEAD_DIM)
            sh, dh = s[:, sl], d[:, sl]
            r = lax.rsqrt(jnp.sum(sh * sh, axis=1, keepdims=True) + EPS)
            nh = sh * r
            ds_norm = r * (dh - nh * jnp.sum(nh * dh, axis=1, keepdims=True))
            dc = jnp.where(j < 2, ds_norm, dh) * dsilu[:, sl]
            dc_ref[:, sl] = dc.astype(BF16)
            for k in range(DN_CONV):
                dw_ref[k:k + 1, sl] += jnp.sum(dc * taps[k][:, sl], axis=0, keepdims=True)

    dspec = lambda p: pl.BlockSpec((tr, WIDTH), lambda j, i: (jnp.where(j == p, i, 0), 0))
    return pl.pallas_call(
        body, name=name,
        out_shape=(jax.ShapeDtypeStruct((t, 3 * WIDTH), BF16), jax.ShapeDtypeStruct((DN_CONV, 3 * WIDTH), F32)),
        grid=(3, t // tr),
        in_specs=[dspec(0), dspec(1), dspec(2),
                  pl.BlockSpec((tr, WIDTH), lambda j, i: (i, j)),
                  pl.BlockSpec((HALO, WIDTH), lambda j, i: (jnp.maximum(i * (tr // HALO) - 1, 0), j)),
                  pl.BlockSpec((DN_CONV, WIDTH), lambda j, i: (0, j))],
        out_specs=(pl.BlockSpec((tr, WIDTH), lambda j, i: (i, j)),
                   pl.BlockSpec((DN_CONV, WIDTH), lambda j, i: (0, j))),
        compiler_params=_params(("parallel", "arbitrary")),
    )(dq, dk, dv, qkv_pre, qkv_pre, cw)


def _tri(n, kind):
    r, c = _iota((n, n), 0), _iota((n, n), 1)
    m = {"lower": r >= c, "strict": r > c, "upper": r <= c}[kind]
    return m


def _dn_gates_fwd(hab, alog, dtb, name):
    t = hab.shape[0]
    cc = DN_CHUNK

    def body(h_ref, al_ref, dt_ref, o_ref):
        hv = h_ref[...]
        lane = _iota(hv.shape, 1)
        xa = hv + dt_ref[...]
        sp = jnp.maximum(xa, 0.0) + _log1pexp_neg_abs(xa)
        g = jnp.where(lane < HEADS, -jnp.exp(al_ref[...]) * sp, 0.0)
        tril = jnp.where(_tri(cc, "lower"), 1.0, 0.0).astype(BF16)
        gc = _dot_xl(tril, g, NN)
        o_ref[...] = jnp.where(lane < HEADS, gc, jnp.where(lane < 2 * HEADS, _sigmoid(hv), 0.0))

    return pl.pallas_call(
        body, name=name,
        out_shape=jax.ShapeDtypeStruct((t, 128), F32),
        grid=(t // cc,),
        in_specs=[pl.BlockSpec((cc, 128), lambda i: (i, 0)), pl.BlockSpec((1, 128), lambda i: (0, 0)),
                  pl.BlockSpec((1, 128), lambda i: (0, 0))],
        out_specs=pl.BlockSpec((cc, 128), lambda i: (i, 0)),
        compiler_params=_params(("parallel",)),
    )(hab, alog, dtb)


def _dn_gates_bwd(dgates, hab, alog, dtb, name):
    t = hab.shape[0]
    cc = DN_CHUNK

    def body(d_ref, h_ref, al_ref, dt_ref, o_ref, dal_ref, ddt_ref):
        i = pl.program_id(0)
        hv = h_ref[...]
        dv = d_ref[...]
        lane = _iota(hv.shape, 1)
        triu = jnp.where(_tri(cc, "upper"), 1.0, 0.0).astype(BF16)
        dg = _dot_xl(triu, jnp.where(lane < HEADS, dv, 0.0), NN)
        xa = hv + dt_ref[...]
        sp = jnp.maximum(xa, 0.0) + _log1pexp_neg_abs(xa)
        ea = jnp.exp(al_ref[...])
        da = jnp.where(lane < HEADS, dg * (-ea) * _sigmoid(xa), 0.0)
        be = _sigmoid(hv)
        db = dv * be * (1.0 - be)
        o_ref[...] = jnp.where(lane < HEADS, da, jnp.where(lane < 2 * HEADS, db, 0.0))

        @pl.when(i == 0)
        def _():
            dal_ref[...] = jnp.zeros_like(dal_ref)
            ddt_ref[...] = jnp.zeros_like(ddt_ref)

        dal_ref[...] += jnp.sum(jnp.where(lane < HEADS, dg * (-ea) * sp, 0.0), axis=0, keepdims=True)
        ddt_ref[...] += jnp.sum(da, axis=0, keepdims=True)

    return pl.pallas_call(
        body, name=name,
        out_shape=(jax.ShapeDtypeStruct((t, 128), F32), jax.ShapeDtypeStruct((1, 128), F32),
                   jax.ShapeDtypeStruct((1, 128), F32)),
        grid=(t // cc,),
        in_specs=[pl.BlockSpec((cc, 128), lambda i: (i, 0)), pl.BlockSpec((cc, 128), lambda i: (i, 0)),
                  pl.BlockSpec((1, 128), lambda i: (0, 0)), pl.BlockSpec((1, 128), lambda i: (0, 0))],
        out_specs=(pl.BlockSpec((cc, 128), lambda i: (i, 0)), pl.BlockSpec((1, 128), lambda i: (0, 0)),
                   pl.BlockSpec((1, 128), lambda i: (0, 0))),
        compiler_params=_params(("arbitrary",)),
    )(dgates, hab, alog, dtb)


def _dn_chunk_common(gates, h):
    cc = DN_CHUNK
    lane = _iota(gates.shape, 1)
    gh = jnp.where(lane == h, gates, 0.0)
    gc_col = jnp.sum(gh, axis=1, keepdims=True)
    gc_row = _dot_xl(jnp.ones((cc, 128), BF16), gh, NT)
    beta = jnp.sum(jnp.where(lane == h + HEADS, gates, 0.0), axis=1, keepdims=True)
    lower = _tri(cc, "lower")
    decay = jnp.where(lower, jnp.exp(jnp.where(lower, gc_col - gc_row, 0.0)), 0.0)
    gc_last = gc_col[cc - 1:cc, :]
    return gc_col, gc_last, beta, decay


def _dn_local_fwd(act, gates, name):
    t = act.shape[0]
    cc = DN_CHUNK
    nc = t // cc

    def body(q_ref, k_ref, v_ref, g_ref, u_ref, w_ref, kd_ref, qg_ref, ti_ref, p_ref):
        gates = g_ref[...]
        eye = jnp.where(_iota((cc, cc), 0) == _iota((cc, cc), 1), 1.0, 0.0)
        hs = range(HEADS)
        sl = [slice(h * HEAD_DIM, (h + 1) * HEAD_DIM) for h in hs]
        q, k, v = ([r[:, s] for s in sl] for r in (q_ref, k_ref, v_ref))
        gc_col, gc_last, beta, decay = zip(*[_dn_chunk_common(gates, h) for h in hs])
        gam = [jnp.exp(g) for g in gc_col]
        kb = [k[h] * beta[h] for h in hs]
        npow = [-jnp.where(_tri(cc, "strict"), _dotb(kb[h], k[h], NT) * decay[h], 0.0) for h in hs]
        tinv = [eye + n for n in npow]
        for _ in range(5):
            npow = [_dot3(n, n, NN) for n in npow]
            tinv = [t + _dot3(t, n, NN) for t, n in zip(tinv, npow)]
        uu = [_dot3(tinv[h], v[h] * beta[h], NN) for h in hs]
        ww = [_dot3(tinv[h], kb[h] * gam[h], NN) for h in hs]
        pp = [jnp.where(_tri(cc, "lower"), _dotb(q[h], k[h], NT) * decay[h], 0.0) for h in hs]
        for h in hs:
            u_ref[:, sl[h]] = uu[h]
            w_ref[:, sl[h]] = ww[h]
            kd_ref[:, sl[h]] = k[h] * jnp.exp(gc_last[h] - gc_col[h])
            qg_ref[:, sl[h]] = q[h] * gam[h]
            ti_ref[h] = tinv[h]
            p_ref[h] = pp[h]

    row = lambda off: pl.BlockSpec((cc, WIDTH), lambda n: (n, off))
    mat = pl.BlockSpec((HEADS, cc, cc), lambda n: (0, n, 0))
    tw = jax.ShapeDtypeStruct((t, WIDTH), F32)
    hm = jax.ShapeDtypeStruct((HEADS, t, cc), F32)
    return pl.pallas_call(
        body, name=name,
        out_shape=(tw, tw, tw, tw, hm, hm),
        grid=(nc,),
        in_specs=[row(0), row(1), row(2), pl.BlockSpec((cc, 128), lambda n: (n, 0))],
        out_specs=(row(0), row(0), row(0), row(0), mat, mat),
        compiler_params=_params(("parallel",)),
    )(act, act, act, gates)


def _dn_scan_fwd(u, w, kd, qg, p, gates, name):
    t = u.shape[0]
    cc = DN_CHUNK
    nc = t // cc

    def body(u_ref, w_ref, kd_ref, qg_ref, p_ref, g_ref, o_ref, sh_ref, s_ref):
        n = pl.program_id(0)

        @pl.when(n == 0)
        def _():
            s_ref[...] = jnp.zeros_like(s_ref)

        glast = jnp.exp(g_ref[cc - 1:cc, :])
        hs = range(HEADS)
        sl = [slice(h * HEAD_DIM, (h + 1) * HEAD_DIM) for h in hs]
        s = [s_ref[h] for h in hs]
        sb = [a.astype(BF16) for a in s]
        vn = [u_ref[:, sl[h]] - _dot(w_ref[:, sl[h]].astype(BF16), sb[h], NN) for h in hs]
        vnb = [a.astype(BF16) for a in vn]
        o_state = [_dot(qg_ref[:, sl[h]].astype(BF16), sb[h], NN) for h in hs]
        o_local = [_dot(p_ref[h].astype(BF16), vnb[h], NN) for h in hs]
        s_add = [_dot(kd_ref[:, sl[h]].astype(BF16), vnb[h], TN) for h in hs]
        for h in hs:
            o_ref[:, sl[h]] = o_state[h] + o_local[h]
            sh_ref[0, h] = s[h]
            s_ref[h] = glast[:, h:h + 1] * s[h] + s_add[h]

    row = pl.BlockSpec((cc, WIDTH), lambda n: (n, 0))
    return pl.pallas_call(
        body, name=name,
        out_shape=(jax.ShapeDtypeStruct((t, WIDTH), F32),
                   jax.ShapeDtypeStruct((nc, HEADS, HEAD_DIM, HEAD_DIM), F32)),
        grid=(nc,),
        in_specs=[row, row, row, row, pl.BlockSpec((HEADS, cc, cc), lambda n: (0, n, 0)),
                  pl.BlockSpec((cc, 128), lambda n: (n, 0))],
        out_specs=(row, pl.BlockSpec((1, HEADS, HEAD_DIM, HEAD_DIM), lambda n: (n, 0, 0, 0))),
        scratch_shapes=[pltpu.VMEM((HEADS, HEAD_DIM, HEAD_DIM), F32)],
        compiler_params=_params(("arbitrary",)),
    )(u, w, kd, qg, p, gates)


def _dn_scan_bwd(do, w, kd, qg, p, gates, name):
    t = do.shape[0]
    cc = DN_CHUNK
    nc = t // cc

    def body(do_ref, w_ref, kd_ref, qg_ref, p_ref, g_ref, dvn_ref, dsh_ref, ds_ref):
        n = pl.program_id(0)

        @pl.when(n == 0)
        def _():
            ds_ref[...] = jnp.zeros_like(ds_ref)

        glast = jnp.exp(g_ref[cc - 1:cc, :])
        hs = range(HEADS)
        sl = [slice(h * HEAD_DIM, (h + 1) * HEAD_DIM) for h in hs]
        ds = [ds_ref[h] for h in hs]
        dob = [do_ref[:, sl[h]].astype(BF16) for h in hs]
        dvn = [_dot(p_ref[h].astype(BF16), dob[h], TN) + _dot(kd_ref[:, sl[h]].astype(BF16), ds[h].astype(BF16), NN)
               for h in hs]
        ds_q = [_dot(qg_ref[:, sl[h]].astype(BF16), dob[h], TN) for h in hs]
        ds_w = [_dot(w_ref[:, sl[h]].astype(BF16), dvn[h].astype(BF16), TN) for h in hs]
        for h in hs:
            dvn_ref[:, sl[h]] = dvn[h]
            dsh_ref[0, h] = ds[h]
            ds_ref[h] = ds_q[h] + glast[:, h:h + 1] * ds[h] - ds_w[h]

    row = pl.BlockSpec((cc, WIDTH), lambda n: (nc - 1 - n, 0))
    return pl.pallas_call(
        body, name=name,
        out_shape=(jax.ShapeDtypeStruct((t, WIDTH), F32),
                   jax.ShapeDtypeStruct((nc, HEADS, HEAD_DIM, HEAD_DIM), F32)),
        grid=(nc,),
        in_specs=[row, row, row, row, pl.BlockSpec((HEADS, cc, cc), lambda n: (0, nc - 1 - n, 0)),
                  pl.BlockSpec((cc, 128), lambda n: (nc - 1 - n, 0))],
        out_specs=(row, pl.BlockSpec((1, HEADS, HEAD_DIM, HEAD_DIM), lambda n: (nc - 1 - n, 0, 0, 0))),
        scratch_shapes=[pltpu.VMEM((HEADS, HEAD_DIM, HEAD_DIM), F32)],
        compiler_params=_params(("arbitrary",)),
    )(do, w, kd, qg, p, gates)


def _dn_local_bwd(act, gates, u, w, kd, qg, tinv, p, sh, dsh, dvn, do, name):
    t = act.shape[0]
    cc = DN_CHUNK
    nc = t // cc

    def body(q_ref, k_ref, v_ref, g_ref, u_ref, w_ref, kd_ref, qg_ref, ti_ref, p_ref, s_ref, ds_ref,
             dvn_ref, do_ref, dq_ref, dk_ref, dv_ref, dg_ref):
        gates_v = g_ref[...]
        lower, strict = _tri(cc, "lower"), _tri(cc, "strict")
        ones = jnp.ones((cc, 128), BF16)
        rowc = _iota((cc, 1), 0)
        lane = _iota((cc, 128), 1)
        hs = range(HEADS)
        sl = [slice(h * HEAD_DIM, (h + 1) * HEAD_DIM) for h in hs]
        q, k, v, uu, ww, kd, qg, dvn, do = ([r[:, s] for s in sl] for r in (
            q_ref, k_ref, v_ref, u_ref, w_ref, kd_ref, qg_ref, dvn_ref, do_ref))
        gc_col, gc_last, beta, decay = zip(*[_dn_chunk_common(gates_v, h) for h in hs])
        gam = [jnp.exp(g) for g in gc_col]
        kb = [k[h] * beta[h] for h in hs]
        s_in = [s_ref[0, h] for h in hs]
        ds_out = [ds_ref[0, h] for h in hs]
        tinv = [ti_ref[h] for h in hs]

        a = [jnp.where(strict, _dotb(kb[h], k[h], NT) * decay[h], 0.0) for h in hs]
        vn = [uu[h] - _dotb(ww[h], s_in[h], NN) for h in hs]
        dqg = [_dotb(do[h], s_in[h], NT) for h in hs]
        dw = [-_dotb(dvn[h], s_in[h], NT) for h in hs]
        dp = [jnp.where(lower, _dotb(do[h], vn[h], NT), 0.0) for h in hs]
        dkd = [_dotb(vn[h], ds_out[h], NT) for h in hs]
        dru = [_dot3(tinv[h], dvn[h], TN) for h in hs]
        drw = [_dot3(tinv[h], dw[h], TN) for h in hs]
        da = [-jnp.where(strict, _dotb(dru[h], uu[h], NT) + _dotb(drw[h], ww[h], NT), 0.0) for h in hs]
        dad = [da[h] * decay[h] for h in hs]
        dpd = [dp[h] * decay[h] for h in hs]
        dkb = [_dotb(dad[h], k[h], NN) + gam[h] * drw[h] for h in hs]
        dk = [_dotb(dad[h], kb[h], TN) + _dotb(dpd[h], q[h], TN) + beta[h] * dkb[h]
              + jnp.exp(gc_last[h] - gc_col[h]) * dkd[h] for h in hs]
        dq = [gam[h] * dqg[h] + _dotb(dpd[h], k[h], NN) for h in hs]
        gm = [da[h] * a[h] + dp[h] * p_ref[h] for h in hs]
        colsum = [_dot_xr(gm[h], ones, TN)[:, 0:1] for h in hs]

        dgates = jnp.zeros((cc, 128), F32)
        for h in hs:
            dk_ref[:, sl[h]] = dk[h]
            dq_ref[:, sl[h]] = dq[h]
            dv_ref[:, sl[h]] = beta[h] * dru[h]
            dbeta = (jnp.sum(dkb[h] * k[h], axis=1, keepdims=True)
                     + jnp.sum(dru[h] * v[h], axis=1, keepdims=True))
            rkd = jnp.sum(dkd[h] * kd[h], axis=1, keepdims=True)
            dgc = (jnp.sum(gm[h], axis=1, keepdims=True) - colsum[h]
                   + jnp.sum(dqg[h] * qg[h], axis=1, keepdims=True)
                   + jnp.sum(drw[h] * kb[h], axis=1, keepdims=True) * gam[h] - rkd)
            tail = jnp.sum(rkd, axis=0, keepdims=True) + jnp.exp(gc_last[h]) * jnp.sum(
                jnp.sum(s_in[h] * ds_out[h], axis=1, keepdims=True), axis=0, keepdims=True)
            dgc = dgc + jnp.where(rowc == cc - 1, tail, 0.0)
            dgates = dgates + jnp.where(lane == h, dgc, 0.0) + jnp.where(lane == h + HEADS, dbeta, 0.0)
        dg_ref[...] = dgates

    row = lambda off: pl.BlockSpec((cc, WIDTH), lambda n: (n, off))
    mat = pl.BlockSpec((HEADS, cc, cc), lambda n: (0, n, 0))
    st = pl.BlockSpec((1, HEADS, HEAD_DIM, HEAD_DIM), lambda n: (n, 0, 0, 0))
    gl = pl.BlockSpec((cc, 128), lambda n: (n, 0))
    tw = jax.ShapeDtypeStruct((t, WIDTH), F32)
    return pl.pallas_call(
        body, name=name,
        out_shape=(tw, tw, tw, jax.ShapeDtypeStruct((t, 128), F32)),
        grid=(nc,),
        in_specs=[row(0), row(1), row(2), gl, row(0), row(0), row(0), row(0), mat, mat, st, st, row(0), row(0)],
        out_specs=(row(0), row(0), row(0), gl),
        compiler_params=_params(("parallel",)),
    )(act, act, act, gates, u, w, kd, qg, tinv, p, sh, dsh, dvn, do)


def _dn_post_fwd(o, gate, w, name):
    t = o.shape[0]
    tr = _tile(t, 512)

    def body(o_ref, g_ref, w_ref, y_ref):
        for h in range(HEADS):
            sl = slice(h * HEAD_DIM, (h + 1) * HEAD_DIM)
            ov, gv = o_ref[:, sl], g_ref[:, sl].astype(F32)
            r = lax.rsqrt(jnp.mean(ov * ov, axis=1, keepdims=True) + EPS)
            y_ref[:, sl] = (ov * r * w_ref[...] * (gv * _sigmoid(gv))).astype(BF16)

    blk = pl.BlockSpec((tr, WIDTH), lambda i: (i, 0))
    return pl.pallas_call(
        body, name=name,
        out_shape=jax.ShapeDtypeStruct((t, WIDTH), BF16),
        grid=(t // tr,),
        in_specs=[blk, blk, pl.BlockSpec((1, HEAD_DIM), lambda i: (0, 0))],
        out_specs=blk,
        compiler_params=_params(("parallel",)),
    )(o, gate, w)


def _dn_post_bwd(dy, o, gate, w, name):
    t = o.shape[0]
    tr = _tile(t, 512)

    def body(dy_ref, o_ref, g_ref, w_ref, do_ref, dg_ref, dw_ref):
        i = pl.program_id(0)

        @pl.when(i == 0)
        def _():
            dw_ref[...] = jnp.zeros_like(dw_ref)

        dw = jnp.zeros((1, HEAD_DIM), F32)
        for h in range(HEADS):
            sl = slice(h * HEAD_DIM, (h + 1) * HEAD_DIM)
            ov, gv, dyv = o_ref[:, sl], g_ref[:, sl].astype(F32), dy_ref[:, sl].astype(F32)
            r = lax.rsqrt(jnp.mean(ov * ov, axis=1, keepdims=True) + EPS)
            oh = ov * r
            sg = _sigmoid(gv)
            dg_ref[:, sl] = (dyv * oh * w_ref[...] * (sg * (1.0 + gv * (1.0 - sg)))).astype(BF16)
            dn = dyv * (gv * sg)
            doh = dn * w_ref[...]
            do_ref[:, sl] = r * (doh - oh * jnp.mean(doh * oh, axis=1, keepdims=True))
            dw = dw + jnp.sum(dn * oh, axis=0, keepdims=True)
        dw_ref[...] += dw

    blk = pl.BlockSpec((tr, WIDTH), lambda i: (i, 0))
    return pl.pallas_call(
        body, name=name,
        out_shape=(jax.ShapeDtypeStruct((t, WIDTH), F32), jax.ShapeDtypeStruct((t, WIDTH), BF16),
                   jax.ShapeDtypeStruct((1, HEAD_DIM), F32)),
        grid=(t // tr,),
        in_specs=[blk, blk, blk, pl.BlockSpec((1, HEAD_DIM), lambda i: (0, 0))],
        out_specs=(blk, blk, pl.BlockSpec((1, HEAD_DIM), lambda i: (0, 0))),
        compiler_params=_params(("arbitrary",)),
    )(dy, o, gate, w)


def _sb_scores(qs, k_ref, qi, it, carries, uincl):
    bk = ATT_BLOCK
    scale = HEAD_DIM ** -0.5
    heads, groups = range(len(qs)), range(SB_GROUP)
    lane = [slice(e * HEAD_DIM, (e + 1) * HEAD_DIM) for e in heads]
    js = [qi - SB_GROUP * it - g for g in groups]
    rows = [pl.ds(pl.multiple_of(jnp.maximum(j, 0) * bk, bk), bk) for j in js]
    qpos = qi * bk + _iota((bk, bk), 0)
    col = _iota((bk, bk), 1)
    mask1 = [jnp.logical_and(j * bk + col < qpos, j >= 0) for j in js]
    ks = [[k_ref[r, lane[e]] for r in rows] for e in heads]
    z = [[_dot(qs[e], k, NT) * scale for k in ks[e]] for e in heads]
    soft = [[_log1pexp_neg_abs(a) for a in ze] for ze in z]
    lk_full = [[-(jnp.maximum(a, 0.0) + s) for a, s in zip(z[e], soft[e])] for e in heads]
    lk = [[jnp.where(m, a, 0.0) for m, a in zip(mask1, lk_full[e])] for e in heads]
    ls = [[jnp.minimum(a, 0.0) - s for a, s in zip(z[e], soft[e])] for e in heads]
    incl = [[_dot_xr2(a, uincl, NN) for a in lk[e]] for e in heads]
    weights, out_carries = [], []
    for e in heads:
        cb, we = carries[e], []
        for g in groups:
            we.append(jnp.where(mask1[g], jnp.exp(ls[e][g] + (cb + incl[e][g] - lk[e][g])), 0.0))
            cb = cb + incl[e][g][:, 0:1]
        weights.append(we)
        out_carries.append(cb)
    return rows, ks, weights, mask1, lk_full, ls, out_carries


def _sb_more(qi, carry):
    it, cbs = carry[0], carry[1]
    live = jnp.max(cbs[0])
    for cb in cbs[1:]:
        live = jnp.maximum(live, jnp.max(cb))
    return jnp.logical_and(SB_GROUP * it <= qi, live > SB_LOG_ZERO)


def _sb_steps(groups, nq):
    def when():
        h, i = pl.program_id(0), pl.program_id(1)
        return (jnp.logical_and(h == 0, i == 0), jnp.logical_and(h == groups // 2, i == 0),
                jnp.logical_and(h == groups - 1, i == nq - 1))
    return when


def _sb_fwd(qkv, name, comm=None):
    t = qkv.shape[0]
    bk = ATT_BLOCK

    hp, wide = SB_HEADS_PER_STEP, SB_HEADS_PER_STEP * HEAD_DIM
    lane = [slice(e * HEAD_DIM, (e + 1) * HEAD_DIM) for e in range(hp)]

    def body(q_ref, k_ref, v_ref, o_ref):
        qi = pl.program_id(1)
        qs = [q_ref[:, s] for s in lane]
        uincl = jnp.where(_tri(bk, "lower"), 1.0, 0.0).astype(BF16)

        def step(carry):
            it, cbs, accs = carry
            rows, _, weights, _, _, _, cbs = _sb_scores(qs, k_ref, qi, it, cbs, uincl)
            accs = list(accs)
            for e in range(hp):
                for r, a in zip(rows, weights[e]):
                    accs[e] = accs[e] + _dot(a.astype(BF16), v_ref[r, lane[e]], NN)
            return it + 1, tuple(cbs), tuple(accs)

        init = (jnp.int32(0), (jnp.zeros((bk, 1), F32),) * hp, (jnp.zeros((bk, HEAD_DIM), F32),) * hp)
        _, _, accs = lax.while_loop(functools.partial(_sb_more, qi), step, init)
        for e in range(hp):
            o_ref[:, lane[e]] = accs[e]

    groups = HEADS // hp
    (o,), extra = _host_call(
        body, name, comm, _sb_steps(groups, t // bk), [jax.ShapeDtypeStruct((t, WIDTH), F32)], (groups, t // bk),
        [pl.BlockSpec((bk, wide), lambda h, i: (i, h)),
         pl.BlockSpec((t, wide), lambda h, i: (0, groups + h)),
         pl.BlockSpec((t, wide), lambda h, i: (0, 2 * groups + h))],
        [pl.BlockSpec((bk, wide), lambda h, i: (i, h))], [], ("parallel", "arbitrary"), (qkv, qkv, qkv))
    return o, extra


def _sb_bwd(qkv, o, do, name, comm=None):
    assert do.dtype == BF16
    t = qkv.shape[0]
    bk = ATT_BLOCK
    scale = HEAD_DIM ** -0.5
    hp, wide = SB_HEADS_PER_STEP, SB_HEADS_PER_STEP * HEAD_DIM
    lane = [slice(e * HEAD_DIM, (e + 1) * HEAD_DIM) for e in range(hp)]

    def body(q_ref, k_ref, v_ref, o_ref, do_ref, dq_ref, dk_ref, dv_ref):
        qi = pl.program_id(1)

        @pl.when(qi == 0)
        def _():
            dk_ref[...] = jnp.zeros_like(dk_ref)
            dv_ref[...] = jnp.zeros_like(dv_ref)

        heads, groups = range(hp), range(SB_GROUP)
        qs = [q_ref[:, s] for s in lane]
        dob = [do_ref[:, s] for s in lane]
        dsum = [jnp.sum(dob[e].astype(F32) * o_ref[:, lane[e]], axis=1, keepdims=True) for e in heads]
        uincl = jnp.where(_tri(bk, "lower"), 1.0, 0.0).astype(BF16)

        def step(carry):
            it, cbs, ces, dqs = carry
            rows, ks, weights, mask, lk_full, ls, cbs = _sb_scores(qs, k_ref, qi, it, cbs, uincl)
            ab = [[a.astype(BF16) for a in weights[e]] for e in heads]
            vs = [[v_ref[r, lane[e]] for r in rows] for e in heads]
            dla = [[ab[e][g].astype(F32) * _dot(dob[e], vs[e][g], NT) for g in groups] for e in heads]
            suf = [[_dot_xr2(a, uincl, NN) for a in dla[e]] for e in heads]
            ces, dqs = list(ces), list(dqs)
            for e in heads:
                for g in groups:
                    err = dsum[e] - (ces[e] + suf[e][g])
                    ces[e] = ces[e] + suf[e][g][:, 0:1]
                    dz = jnp.where(mask[g], dla[e][g] * jnp.exp(lk_full[e][g]) - err * jnp.exp(ls[e][g]), 0.0)
                    dzb = (dz * scale).astype(BF16)
                    dqs[e] = dqs[e] + _dot(dzb, ks[e][g], NN)
                    dk_ref[rows[g], lane[e]] += _dot(dzb, qs[e], TN)
                    dv_ref[rows[g], lane[e]] += _dot(ab[e][g], dob[e], TN)
            return it + 1, tuple(cbs), tuple(ces), tuple(dqs)

        zc = (jnp.zeros((bk, 1), F32),) * hp
        init = (jnp.int32(0), zc, zc, (jnp.zeros((bk, HEAD_DIM), F32),) * hp)
        dqs = lax.while_loop(functools.partial(_sb_more, qi), step, init)[3]
        for e in heads:
            dq_ref[:, lane[e]] = dqs[e]

    ngroup = HEADS // hp
    tw = jax.ShapeDtypeStruct((t, WIDTH), F32)
    qb = pl.BlockSpec((bk, wide), lambda h, i: (i, h))
    full = lambda off: pl.BlockSpec((t, wide), lambda h, i: (0, off + h))
    return _host_call(
        body, name, comm, _sb_steps(ngroup, t // bk), [tw, tw, tw], (ngroup, t // bk),
        [qb, full(ngroup), full(2 * ngroup), qb, qb], [qb, full(0), full(0)], [], ("parallel", "arbitrary"),
        (qkv, qkv, qkv, o, do))


def _merge_fwd(pd, ps, gl, name):
    t = pd.shape[0]
    tr, tc = _tile(t, 512), 512
    nj = D_MODEL // tc

    def body(pd_ref, ps_ref, gd_ref, gs_ref, o_ref):
        gd, gs = gd_ref[...].astype(F32), gs_ref[...].astype(F32)
        o_ref[...] = (_sigmoid(gd) * pd_ref[...].astype(F32) + _sigmoid(gs) * ps_ref[...].astype(F32)).astype(BF16)

    blk = lambda off: pl.BlockSpec((tr, tc), lambda i, j: (i, j + off))
    return pl.pallas_call(
        body, name=name,
        out_shape=jax.ShapeDtypeStruct((t, D_MODEL), BF16),
        grid=(t // tr, nj),
        in_specs=[blk(0), blk(0), blk(0), blk(nj)],
        out_specs=blk(0),
        compiler_params=_params(("parallel", "parallel")),
    )(pd, ps, gl, gl)


def _merge_bwd(dm, pd, ps, gl, name):
    t = pd.shape[0]
    tr, tc = _tile(t, 512), 512
    nj = D_MODEL // tc

    def body(dm_ref, pd_ref, ps_ref, gd_ref, gs_ref, dpd_ref, dps_ref, dgd_ref, dgs_ref):
        dmv = dm_ref[...].astype(F32)
        sd, ss = _sigmoid(gd_ref[...].astype(F32)), _sigmoid(gs_ref[...].astype(F32))
        dpd_ref[...] = (dmv * sd).astype(BF16)
        dps_ref[...] = (dmv * ss).astype(BF16)
        dgd_ref[...] = (dmv * pd_ref[...].astype(F32) * sd * (1.0 - sd)).astype(BF16)
        dgs_ref[...] = (dmv * ps_ref[...].astype(F32) * ss * (1.0 - ss)).astype(BF16)

    blk = lambda off: pl.BlockSpec((tr, tc), lambda i, j: (i, j + off))
    out = jax.ShapeDtypeStruct((t, D_MODEL), BF16)
    return pl.pallas_call(
        body, name=name,
        out_shape=(out, out, out, out),
        grid=(t // tr, nj),
        in_specs=[blk(0), blk(0), blk(0), blk(0), blk(nj)],
        out_specs=(blk(0), blk(0), blk(0), blk(0)),
        compiler_params=_params(("parallel", "parallel")),
    )(dm, pd, ps, gl, gl)


def _local_step(x, target, wts, plan=None):
    n1 = _rmsnorm_fwd(x, wts["norm1_w"], "norm1_fwd")
    qkv_pre = _matmul(n1, wts["w_dnqkv_t"], "nt", BF16, "in_dnqkv")
    hgate = _matmul(n1, wts["w_dngate_t"], "nt", BF16, "in_dngate")
    sbqkv = _matmul(n1, wts["w_sbqkv_t"], "nt", BF16, "in_sbqkv")
    gl = _matmul(n1, wts["w_gl_t"], "nt", BF16, "in_gl")
    hab = _matmul(n1, wts["w_ab_t"], "nt", F32, "in_ab")

    act = _dn_pre_fwd(qkv_pre, wts["dn_conv_w"], "dn_pre_fwd")
    gates = _dn_gates_fwd(hab, wts["alog"], wts["dtb"], "dn_gates_fwd")
    u, w, kd, qg, tinv, p = _dn_local_fwd(act, gates, "dn_local_fwd")
    o_dn, sh = _dn_scan_fwd(u, w, kd, qg, p, gates, "dn_scan_fwd")
    y_dn = _dn_post_fwd(o_dn, hgate, wts["dn_norm_w"], "dn_post_fwd")

    o_sb, late = _sb_fwd(sbqkv, "sb_fwd", comm=plan.late_gather() if plan else None)
    if plan:
        wts = {**wts, **plan.late_weights(late)}

    pd = _matmul(y_dn, wts["w_proj_dn"], "nn", BF16, "proj_dn")
    ps = _matmul(o_sb, wts["w_proj_sb"], "nn", BF16, "proj_sb")
    mixed = _merge_fwd(pd, ps, gl, "merge_fwd")
    x1 = _matmul(mixed, wts["w_out"], "nn", F32, "out_proj", add=x)

    n2 = _rmsnorm_fwd(x1, wts["norm2_w"], "norm2_fwd")
    upre = _matmul(n2, wts["ffn_w_up_t"], "nt", BF16, "ffn_up")
    fact = _ffn_act_fwd(upre, wts["ffn_conv_w"], "ffn_act_fwd")
    x2 = _matmul(fact, wts["ffn_w_down"], "nn", F32, "ffn_down", add=x1)

    dx2, g_normf, loss = _final_loss(x2, target, wts["norm_f_w"], "final_loss")

    dfact = _matmul(dx2, wts["ffn_w_down"], "nt", BF16, "ffn_down_dx")
    g_wdown = _matmul(fact, dx2, "tn", BF16, "ffn_down_dw")
    dgc, duc, dwg, dwu = _ffn_act_bwd(dfact, upre, wts["ffn_conv_w"], "ffn_act_bwd")
    g_fconv = jnp.concatenate([dwg, dwu], axis=1)
    dupre = _conv_bwd_data([dgc, duc], wts["ffn_conv_w"], FFN_CONV, BF16, "ffn_conv_bwd")
    dn2 = _matmul(dupre, wts["ffn_w_up_t"], "nn", F32, "ffn_up_dx")
    g_wup = _matmul(dupre, n2, "tn", BF16, "ffn_up_dw")
    dx1, g_norm2 = _rmsnorm_bwd(dn2, x1, wts["norm2_w"], dx2, "norm2_bwd")

    dmixed = _matmul(dx1, wts["w_out"], "nt", BF16, "out_proj_dx")
    g_wout = _matmul(mixed, dx1, "tn", BF16, "out_proj_dw")
    dpd, dps, dgd, dgs = _merge_bwd(dmixed, pd, ps, gl, "merge_bwd")
    dy_dn = _matmul(dpd, wts["w_proj_dn"], "nt", BF16, "proj_dn_dx")
    g_wpd = _matmul(y_dn, dpd, "tn", BF16, "proj_dn_dw")
    do_sb = _matmul(dps, wts["w_proj_sb"], "nt", BF16, "proj_sb_dx")
    g_wps = _matmul(o_sb, dps, "tn", BF16, "proj_sb_dw")
    grads = dict(w_proj_dn=g_wpd, w_proj_sb=g_wps, w_out=g_wout, ffn_w_up_t=g_wup, ffn_w_down=g_wdown)

    (dsq, dsk, dsv), got_early = _sb_bwd(sbqkv, o_sb, do_sb, "sb_bwd",
                                         comm=plan.early_grads(grads) if plan else None)

    do_dn, dhgate, g_dnnorm = _dn_post_bwd(dy_dn, o_dn, hgate, wts["dn_norm_w"], "dn_post_bwd")
    dvn, dsh = _dn_scan_bwd(do_dn, w, kd, qg, p, gates, "dn_scan_bwd")
    dq, dk, dv, dgates = _dn_local_bwd(act, gates, u, w, kd, qg, tinv, p, sh, dsh, dvn, do_dn, "dn_local_bwd")
    dhab, g_alog, g_dtb = _dn_gates_bwd(dgates, hab, wts["alog"], wts["dtb"], "dn_gates_bwd")
    dcv, g_dnconv = _dn_pre_bwd(dq, dk, dv, qkv_pre, wts["dn_conv_w"], "dn_pre_bwd")
    dqkv_pre = _conv_bwd_data([dcv], wts["dn_conv_w"], DN_CONV, BF16, "dn_conv_bwd")

    dh = jnp.concatenate([dqkv_pre, dhgate, dsq.astype(BF16), dsk.astype(BF16), dsv.astype(BF16), dgd, dgs], axis=1)
    w_main_t = jnp.concatenate([wts["w_dnqkv_t"], wts["w_dngate_t"], wts["w_sbqkv_t"], wts["w_gl_t"]], axis=0)
    g_wmain = _matmul(dh, n1, "tn", BF16, "in_dw_main")
    g_wab = _matmul(dhab, n1, "tn", BF16, "in_dw_ab")
    grads.update(w_main_t=g_wmain, w_ab_t=g_wab, dn_conv_w=g_dnconv, alog=g_alog, dtb=g_dtb, dn_norm_w=g_dnnorm,
                 norm2_w=g_norm2, ffn_conv_w=g_fconv, norm_f_w=g_normf)
    got_late = []
    if plan:
        dn1, got_late = _matmul(dh, w_main_t, "nn", F32, "in_dx_main", comm=plan.late_grads(grads, loss))
    else:
        dn1 = _matmul(dh, w_main_t, "nn", F32, "in_dx_main")
    dn1 = _matmul(dhab, wts["w_ab_t"], "nn", F32, "in_dx_ab", add=dn1)
    grad_x, g_norm1 = _rmsnorm_bwd(dn1, x, wts["norm1_w"], dx1, "norm1_bwd")
    grads["norm1_w"] = g_norm1
    return loss, grad_x, grads, got_early, got_late


HBM_SPEC = pl.BlockSpec(memory_space=pltpu.HBM)


def _mesh_pos():
    x, y, c = lax.axis_index("x"), lax.axis_index("y"), lax.axis_index("c")
    return x, y, c, 4 * x + 2 * y + c


def _peer(k):
    x, y, c, _ = _mesh_pos()
    px = 1 - x if k & 4 else x
    py = 1 - y if k & 2 else y
    pc = 1 - c if k & 1 else c
    return (px, py, pc), 4 * px + 2 * py + pc


def _rcopy(src, dst, send, recv, a, s, peer):
    return pltpu.make_async_remote_copy(src_ref=src, dst_ref=dst, send_sem=send.at[a, s], recv_sem=recv.at[a, s],
                                        device_id=peer, device_id_type=pl.DeviceIdType.MESH)


class _Gather:
    ICI = (2, 4, 6)

    def __init__(self, shards):
        self.args = list(shards)
        self.n = len(shards)
        self.out_shape = [jax.ShapeDtypeStruct((N_DEV,) + s.shape, s.dtype) for s in shards]
        self.scratch = [pltpu.SemaphoreType.DMA((self.n, N_DEV - 1)), pltpu.SemaphoreType.DMA((self.n, N_DEV - 1)),
                        pltpu.SemaphoreType.DMA((self.n,))]

    def _first(self, ins, outs, send, recv, a):
        me = _mesh_pos()[3]
        out, got = [], []
        for s, k in enumerate((1,) + self.ICI):
            peer, pidx = _peer(k)
            out.append(_rcopy(ins[a], outs[a].at[me], send, recv, a, s, peer))
            got.append(_rcopy(ins[a], outs[a].at[pidx], send, recv, a, s, peer))
        return out, got

    def _forward(self, ins, outs, send, recv, a):
        sib = _peer(1)[0]
        out, got = [], []
        for s, k in enumerate(self.ICI):
            held = outs[a].at[_peer(k)[1]]
            out.append(_rcopy(held, held, send, recv, a, 4 + s, sib))
            other = outs[a].at[_peer(k | 1)[1]]
            got.append(_rcopy(other, other, send, recv, a, 4 + s, sib))
        return out, got

    def start(self, ins, outs, sems):
        send, recv, loc = sems
        me = _mesh_pos()[3]
        for a in range(self.n):
            pltpu.make_async_copy(ins[a], outs[a].at[me], loc.at[a]).start()
            for cp in self._first(ins, outs, send, recv, a)[0]:
                cp.start()

    def mid(self, ins, outs, sems):
        send, recv, _ = sems
        for a in range(self.n):
            arrivals = self._first(ins, outs, send, recv, a)[1]
            for s, cp in enumerate(self._forward(ins, outs, send, recv, a)[0]):
                arrivals[1 + s].wait_recv()
                cp.start()

    def finish(self, ins, outs, sems):
        send, recv, loc = sems
        me = _mesh_pos()[3]
        for a in range(self.n):
            first_out, first_got = self._first(ins, outs, send, recv, a)
            fwd_out, fwd_got = self._forward(ins, outs, send, recv, a)
            first_got[0].wait_recv()
            for cp in fwd_got:
                cp.wait_recv()
            for cp in first_out + fwd_out:
                cp.wait_send()
            pltpu.make_async_copy(ins[a], outs[a].at[me], loc.at[a]).wait()


class _Exchange:
    def __init__(self, slabs, gathered=()):
        self.args = list(slabs) + list(gathered)
        self.n_slab = len(slabs)
        self.n = len(self.args)
        self.out_shape = ([jax.ShapeDtypeStruct(s.shape, s.dtype) for s in slabs]
                          + [jax.ShapeDtypeStruct((N_DEV,) + s.shape, s.dtype) for s in gathered])
        self.scratch = [pltpu.SemaphoreType.DMA((self.n, N_DEV - 1)), pltpu.SemaphoreType.DMA((self.n, N_DEV - 1)),
                        pltpu.SemaphoreType.DMA((self.n,))]

    def _copies(self, ins, outs, send, recv, a):
        me = _mesh_pos()[3]
        out, got = [], []
        for k in range(1, N_DEV):
            peer, pidx = _peer(k)
            src = ins[a].at[pidx] if a < self.n_slab else ins[a]
            out.append(_rcopy(src, outs[a].at[me], send, recv, a, k - 1, peer))
            got.append(_rcopy(src, outs[a].at[pidx], send, recv, a, k - 1, peer))
        return out, got

    def _local(self, ins, outs, loc, a):
        me = _mesh_pos()[3]
        return pltpu.make_async_copy(ins[a].at[me] if a < self.n_slab else ins[a], outs[a].at[me], loc.at[a])

    def start(self, ins, outs, sems):
        send, recv, loc = sems
        for a in range(self.n):
            self._local(ins, outs, loc, a).start()
            for cp in self._copies(ins, outs, send, recv, a)[0]:
                cp.start()

    def mid(self, ins, outs, sems):
        pass

    def finish(self, ins, outs, sems):
        send, recv, loc = sems
        for a in range(self.n):
            out, got = self._copies(ins, outs, send, recv, a)
            for cp in got:
                cp.wait_recv()
            for cp in out:
                cp.wait_send()
            self._local(ins, outs, loc, a).wait()


def _comm_call(comm, name):
    n = comm.n

    def body(*refs):
        ins, outs, sems = refs[:n], refs[n:2 * n], refs[2 * n:]
        comm.start(ins, outs, sems)
        comm.mid(ins, outs, sems)
        comm.finish(ins, outs, sems)

    return pl.pallas_call(
        body, name=name, out_shape=comm.out_shape, in_specs=[HBM_SPEC] * n, out_specs=[HBM_SPEC] * n,
        scratch_shapes=comm.scratch,
    )(*comm.args)


def _hosted(body, comm, n_in, n_out, when):
    if comm is None:
        return body

    def wrapped(*refs):
        ins, c_ins = refs[:n_in], refs[n_in:n_in + comm.n]
        o0 = n_in + comm.n
        outs, c_outs = refs[o0:o0 + n_out], refs[o0 + n_out:o0 + n_out + comm.n]
        scratch, sems = refs[o0 + n_out + comm.n:len(refs) - 3], refs[len(refs) - 3:]
        first, middle, last = when()

        @pl.when(first)
        def _():
            comm.start(c_ins, c_outs, sems)

        body(*ins, *outs, *scratch)

        @pl.when(middle)
        def _():
            comm.mid(c_ins, c_outs, sems)

        @pl.when(last)
        def _():
            comm.finish(c_ins, c_outs, sems)

    return wrapped


def _host_call(body, name, comm, when, out_shape, grid, in_specs, out_specs, scratch_shapes, sem, args):
    n_in, n_out = len(in_specs), len(out_specs)
    if comm is None:
        res = pl.pallas_call(body, name=name, out_shape=out_shape, grid=grid, in_specs=in_specs, out_specs=out_specs,
                             scratch_shapes=scratch_shapes, compiler_params=_params(sem))(*args)
        return list(res), []
    res = pl.pallas_call(
        _hosted(body, comm, n_in, n_out, when), name=name,
        out_shape=list(out_shape) + comm.out_shape, grid=grid,
        in_specs=list(in_specs) + [HBM_SPEC] * comm.n, out_specs=list(out_specs) + [HBM_SPEC] * comm.n,
        scratch_shapes=list(scratch_shapes) + comm.scratch,
        compiler_params=_params(("arbitrary",) * len(grid)),
    )(*args, *comm.args)
    return list(res[:n_out]), list(res[n_out:])


def _adamw(parts, w, m, v, name):
    rows, cols = w.shape
    tr, tc = rows, cols
    for cand in (128, 176):
        if rows > cand and rows % cand == 0:
            tr = cand
            break
    if tr == rows and rows > 512:
        tc = _tile(cols, 256)

    def body(p_ref, w_ref, m_ref, v_ref, g_ref, d_ref, mo_ref, vo_ref):
        g = p_ref[0].astype(F32)
        for s in range(1, N_DEV):
            g = g + p_ref[s].astype(F32)
        mn = ADAM_B1 * m_ref[...] + (1.0 - ADAM_B1) * g
        vn = ADAM_B2 * v_ref[...] + (1.0 - ADAM_B2) * (g * g)
        m_hat = mn / (1.0 - ADAM_B1 ** ADAM_STEP)
        v_hat = vn / (1.0 - ADAM_B2 ** ADAM_STEP)
        g_ref[...] = g
        d_ref[...] = -ADAM_LR * (m_hat / (jnp.sqrt(v_hat) + ADAM_EPS) + ADAM_WD * w_ref[...])
        mo_ref[...] = mn
        vo_ref[...] = vn

    blk = pl.BlockSpec((tr, tc), lambda i, j: (i, j))
    out = jax.ShapeDtypeStruct((rows, cols), F32)
    return pl.pallas_call(
        body, name=name,
        out_shape=(out, out, out, out),
        grid=(rows // tr, cols // tc),
        in_specs=[pl.BlockSpec((N_DEV, tr, tc), lambda i, j: (0, i, j)), blk, blk, blk],
        out_specs=(blk, blk, blk, blk),
        compiler_params=_params(("parallel", "parallel")),
    )(parts, w, m, v)


CONV_PACK = 8 * 1024
WEIGHT_ORDER = ("norm1_w", "w_in", "dn_conv_w", "dn_A_log", "dn_dt_bias", "dn_norm_w", "w_proj_dn", "w_proj_sb",
                "w_out", "norm2_w", "ffn_w_up", "ffn_conv_w", "ffn_w_down", "norm_f_w")


def _cols_to_slabs(g):
    r, c8 = g.shape
    return g.reshape(r, N_DEV, c8 // N_DEV).transpose(1, 0, 2)


def _slabs_to_cols(s):
    d, r, c = s.shape
    return s.transpose(1, 0, 2).reshape(r, d * c)


def kernel(x, norm1_w, w_in, dn_conv_w, dn_A_log, dn_dt_bias, dn_norm_w, w_proj_dn, w_proj_sb, w_out, norm2_w, ffn_w_up, ffn_conv_w, ffn_w_down, norm_f_w, loss_target, m_norm1_w, m_w_in, m_dn_conv_w, m_dn_A_log, m_dn_dt_bias, m_dn_norm_w, m_w_proj_dn, m_w_proj_sb, m_w_out, m_norm2_w, m_ffn_w_up, m_ffn_conv_w, m_ffn_w_down, m_norm_f_w, v_norm1_w, v_w_in, v_dn_conv_w, v_dn_A_log, v_dn_dt_bias, v_dn_norm_w, v_w_proj_dn, v_w_proj_sb, v_w_out, v_norm2_w, v_ffn_w_up, v_ffn_conv_w, v_ffn_w_down, v_norm_f_w):
    me = _mesh_pos()[3]
    tr = lambda a: jnp.transpose(a[0])
    w_loc = dict(norm1_w=norm1_w, w_in=tr(w_in), dn_conv_w=dn_conv_w[0], dn_A_log=dn_A_log, dn_dt_bias=dn_dt_bias,
                 dn_norm_w=dn_norm_w, w_proj_dn=w_proj_dn[0], w_proj_sb=w_proj_sb[0], w_out=w_out[0],
                 norm2_w=norm2_w, ffn_w_up=tr(ffn_w_up), ffn_conv_w=ffn_conv_w[0], ffn_w_down=ffn_w_down[0],
                 norm_f_w=norm_f_w[None, :])
    m_loc = dict(norm1_w=m_norm1_w, w_in=tr(m_w_in), dn_conv_w=m_dn_conv_w[0], dn_A_log=m_dn_A_log,
                 dn_dt_bias=m_dn_dt_bias, dn_norm_w=m_dn_norm_w, w_proj_dn=m_w_proj_dn[0], w_proj_sb=m_w_proj_sb[0],
                 w_out=m_w_out[0], norm2_w=m_norm2_w, ffn_w_up=tr(m_ffn_w_up), ffn_conv_w=m_ffn_conv_w[0],
                 ffn_w_down=m_ffn_w_down[0], norm_f_w=m_norm_f_w[None, :])
    v_loc = dict(norm1_w=v_norm1_w, w_in=tr(v_w_in), dn_conv_w=v_dn_conv_w[0], dn_A_log=v_dn_A_log,
                 dn_dt_bias=v_dn_dt_bias, dn_norm_w=v_dn_norm_w, w_proj_dn=v_w_proj_dn[0], w_proj_sb=v_w_proj_sb[0],
                 w_out=v_w_out[0], norm2_w=v_norm2_w, ffn_w_up=tr(v_ffn_w_up), ffn_conv_w=v_ffn_conv_w[0],
                 ffn_w_down=v_ffn_w_down[0], norm_f_w=v_norm_f_w[None, :])

    conv_flat = jnp.concatenate([w_loc["dn_conv_w"].reshape(-1), w_loc["ffn_conv_w"].reshape(-1)])
    n_dn, n_ffn = DN_CONV * 3 * WIDTH // N_DEV, FFN_CONV * 2 * D_FF // N_DEV
    conv_pack = jnp.pad(conv_flat, (0, CONV_PACK - n_dn - n_ffn)).reshape(8, 1024)
    g_in, g_conv = _comm_call(_Gather([w_loc["w_in"].astype(BF16), conv_pack]), "gather_first")
    in_width = g_in.shape[0] * g_in.shape[1]
    w_in_t = g_in.reshape(in_width, D_MODEL)
    g_conv = g_conv.reshape(N_DEV, CONV_PACK)
    dn_conv_full = _slabs_to_cols(g_conv[:, :n_dn].reshape(N_DEV, DN_CONV, 3 * WIDTH // N_DEV))
    ffn_conv_full = _slabs_to_cols(g_conv[:, n_dn:n_dn + n_ffn].reshape(N_DEV, FFN_CONV, 2 * D_FF // N_DEV))
    q_end = 3 * WIDTH
    ab_end = q_end + 2 * HEADS
    gate_end = ab_end + WIDTH
    sb_end = gate_end + 3 * WIDTH
    pad_lanes = lambda a: jnp.pad(a, ((0, 0), (0, 128 - a.shape[1])))
    wts = dict(
        norm1_w=norm1_w, w_dnqkv_t=w_in_t[:q_end], w_ab_t=jnp.pad(w_in_t[q_end:ab_end], ((0, 128 - 2 * HEADS), (0, 0))),
        w_dngate_t=w_in_t[ab_end:gate_end], w_sbqkv_t=w_in_t[gate_end:sb_end], w_gl_t=w_in_t[sb_end:],
        dn_conv_w=dn_conv_full, alog=pad_lanes(dn_A_log), dtb=pad_lanes(dn_dt_bias), dn_norm_w=dn_norm_w,
        norm2_w=norm2_w, ffn_conv_w=ffn_conv_full, norm_f_w=norm_f_w[None, :])

    n_fc = FFN_CONV * 2 * D_FF
    fc_rows = -(-n_fc // D_MODEL)
    dn_rows = DN_CONV * 3 * WIDTH // D_MODEL
    late_names = ("w_proj_dn", "w_proj_sb", "w_out", "ffn_w_up", "ffn_w_down")

    class Plan:
        @staticmethod
        def late_gather():
            return _Gather([w_loc[k].astype(BF16) for k in late_names])

        @staticmethod
        def late_weights(got):
            g_pd, g_ps, g_out, g_up, g_down = got
            return dict(w_proj_dn=g_pd.reshape(WIDTH, D_MODEL), w_proj_sb=g_ps.reshape(WIDTH, D_MODEL),
                        w_out=g_out.reshape(D_MODEL, D_MODEL), ffn_w_up_t=g_up.reshape(2 * D_FF, D_MODEL),
                        ffn_w_down=g_down.reshape(D_FF, D_MODEL))

        @staticmethod
        def early_grads(g):
            return _Exchange([g["w_proj_dn"].reshape(N_DEV, WIDTH // N_DEV, D_MODEL),
                              g["w_proj_sb"].reshape(N_DEV, WIDTH // N_DEV, D_MODEL),
                              g["w_out"].reshape(N_DEV, D_MODEL // N_DEV, D_MODEL),
                              g["ffn_w_up_t"].reshape(N_DEV, 2 * D_FF // N_DEV, D_MODEL),
                              g["ffn_w_down"].reshape(N_DEV, D_FF // N_DEV, D_MODEL)])

        @staticmethod
        def late_grads(g, loss):
            g_win_t = jnp.concatenate([g["w_main_t"][:q_end], g["w_ab_t"][:2 * HEADS], g["w_main_t"][q_end:]],
                                      axis=0)
            row3 = jnp.concatenate([g["dn_norm_w"], g["alog"], g["dtb"], jnp.pad(loss, ((0, 0), (0, 127))),
                                    jnp.zeros((1, D_MODEL - 512), F32)], axis=1)
            fconv_rows = jnp.pad(g["ffn_conv_w"].reshape(-1), (0, fc_rows * D_MODEL - n_fc)).reshape(fc_rows, D_MODEL)
            pad8 = lambda a: jnp.pad(a, ((0, -a.shape[0] % 8), (0, 0)))
            pieces = [g["norm2_w"], g["norm_f_w"], row3, g["dn_conv_w"].reshape(dn_rows, D_MODEL), fconv_rows]
            small = jnp.concatenate([pad8(a) for a in pieces], axis=0)
            assert small.shape[0] == SMALL_ROWS
            return _Exchange([g_win_t.reshape(N_DEV, in_width // N_DEV, D_MODEL)], [small])

    loss, grad_x, g, got_early, got_late = _local_step(x[0], loss_target[0], wts, Plan)
    r_pd, r_ps, r_out, r_up, r_down = got_early
    r_in, r_small = got_late
    (r_norm1,) = _comm_call(_Exchange([], [jnp.pad(g["norm1_w"], ((0, 7), (0, 0)))]), "gather_norm1")

    parts = dict(w_in=r_in, w_proj_dn=r_pd, w_proj_sb=r_ps, w_out=r_out, ffn_w_up=r_up, ffn_w_down=r_down)
    parts["norm1_w"] = r_norm1[:, 0:1, :]
    parts["norm2_w"] = r_small[:, 0:1, :]
    parts["norm_f_w"] = r_small[:, 8:9, :]
    parts["dn_norm_w"] = r_small[:, 16:17, 0:HEAD_DIM]
    parts["dn_A_log"] = r_small[:, 16:17, 128:128 + HEADS]
    parts["dn_dt_bias"] = r_small[:, 16:17, 256:256 + HEADS]
    dnc = r_small[:, 24:24 + dn_rows, :].reshape(N_DEV, DN_CONV, 3 * WIDTH)
    parts["dn_conv_w"] = lax.dynamic_slice_in_dim(dnc, me * (3 * WIDTH // N_DEV), 3 * WIDTH // N_DEV, axis=2)
    fc0 = 24 + dn_rows + (-dn_rows % 8)
    fcc = r_small[:, fc0:fc0 + fc_rows, :].reshape(N_DEV, fc_rows * D_MODEL)[:, :n_fc]
    fcc = fcc.reshape(N_DEV, FFN_CONV, 2 * D_FF)
    parts["ffn_conv_w"] = lax.dynamic_slice_in_dim(fcc, me * (2 * D_FF // N_DEV), 2 * D_FF // N_DEV, axis=2)
    loss_total = jnp.sum(r_small[:, 16, 384])

    res = {k: _adamw(parts[k], w_loc[k], m_loc[k], v_loc[k], "adamw_" + k) for k in WEIGHT_ORDER}
    lead = ("w_in", "dn_conv_w", "w_proj_dn", "w_proj_sb", "w_out", "ffn_w_up", "ffn_conv_w", "ffn_w_down")

    def shaped(k, a):
        if k in ("w_in", "ffn_w_up"):
            return jnp.transpose(a)[None]
        if k in lead:
            return a[None]
        if k == "norm_f_w":
            return a[0]
        return a

    outs = [loss_total, grad_x[None]]
    for idx in range(4):
        outs += [shaped(k, res[k][idx]) for k in WEIGHT_ORDER]
    return tuple(outs)
```

```python
import functools

import jax
import jax.numpy as jnp
from jax import lax
from jax.experimental import pallas as pl
from jax.experimental.pallas import tpu as pltpu

F32 = jnp.float32
BF16 = jnp.bfloat16

N_DEV = 8
D_MODEL = 1024
HEADS = 8
HEAD_DIM = 128
WIDTH = HEADS * HEAD_DIM
DN_CONV = 4
DN_CHUNK = 64
D_FF = 2816
FFN_CONV = 3
EPS = 1e-6
HALO = 16
ATT_BLOCK = 256
SB_LOG_ZERO = -104.0
SB_GROUP = 2
SB_HEADS_FWD = 4
SB_HEADS_BWD = 2
SMALL_ROWS = 64

ADAM_LR = 0.001
ADAM_B1 = 0.9
ADAM_B2 = 0.999
ADAM_EPS = 1e-08
ADAM_WD = 0.01
ADAM_STEP = 10

VMEM_LIMIT = 48 * 1024 * 1024


def _params(sem=None, **kw):
    return pltpu.CompilerParams(dimension_semantics=sem, vmem_limit_bytes=VMEM_LIMIT, **kw)


def _tile(n, cap):
    if n <= cap:
        return n
    best = None
    for t in range(128, cap + 1, 128):
        if n % t == 0:
            best = t
    assert best is not None, (n, cap)
    return best


def _dot(a, b, dims):
    return lax.dot_general(a, b, ((dims[0], dims[1]), ((), ())), preferred_element_type=F32)


NN = ((1,), (0,))
NT = ((1,), (1,))
TN = ((0,), (0,))


def _dotb(a, b, dims):
    return _dot(a.astype(BF16), b.astype(BF16), dims)


def _split3(x):
    h1 = x.astype(BF16)
    r1 = x - h1.astype(F32)
    h2 = r1.astype(BF16)
    r2 = r1 - h2.astype(F32)
    return h1, h2, r2.astype(BF16)


def _dot_xr(a, b_exact, dims):
    a1, a2, a3 = _split3(a)
    return _dot(a1, b_exact, dims) + _dot(a2, b_exact, dims) + _dot(a3, b_exact, dims)


def _split2(x):
    h1 = x.astype(BF16)
    return h1, (x - h1.astype(F32)).astype(BF16)


def _dot_xr2(a, b_exact, dims):
    a1, a2 = _split2(a)
    return _dot(a1, b_exact, dims) + _dot(a2, b_exact, dims)


def _dot_xl(a_exact, b, dims):
    b1, b2, b3 = _split3(b)
    return _dot(a_exact, b1, dims) + _dot(a_exact, b2, dims) + _dot(a_exact, b3, dims)


def _dot3(a, b, dims):
    a1 = a.astype(BF16)
    a2 = (a - a1.astype(F32)).astype(BF16)
    b1 = b.astype(BF16)
    b2 = (b - b1.astype(F32)).astype(BF16)
    return _dot(a1, b1, dims) + (_dot(a1, b2, dims) + _dot(a2, b1, dims))


def _sigmoid(x):
    return 1.0 / (1.0 + jnp.exp(-x))


def _log1pexp_neg_abs(x):
    return jnp.log(1.0 + jnp.exp(-jnp.abs(x)))


def _iota(shape, dim):
    return lax.broadcasted_iota(jnp.int32, shape, dim)


def _matmul(a, b, mode, out_dtype, name, add=None, comm=None):
    if mode == "nn":
        (m, k), (k2, n) = a.shape, b.shape
    elif mode == "nt":
        (m, k), (n, k2) = a.shape, b.shape
    else:
        (k, m), (k2, n) = a.shape, b.shape
    assert k == k2, (a.shape, b.shape, mode)
    tm, tn, tk = _tile(m, 1408), _tile(n, 1408), _tile(k, 1536)
    nk = k // tk
    dims = {"nn": NN, "nt": NT, "tn": TN}[mode]

    def body(*refs):
        if add is None:
            a_ref, b_ref, o_ref, acc_ref = refs
        else:
            a_ref, b_ref, add_ref, o_ref, acc_ref = refs
        kk = pl.program_id(2)

        @pl.when(kk == 0)
        def _():
            acc_ref[...] = jnp.zeros_like(acc_ref)

        acc_ref[...] += _dotb(a_ref[...], b_ref[...], dims)

        @pl.when(kk == nk - 1)
        def _():
            r = acc_ref[...]
            if add is not None:
                r = r + add_ref[...].astype(F32)
            o_ref[...] = r.astype(out_dtype)

    if mode == "nn":
        specs = [pl.BlockSpec((tm, tk), lambda i, j, l: (i, l)), pl.BlockSpec((tk, tn), lambda i, j, l: (l, j))]
    elif mode == "nt":
        specs = [pl.BlockSpec((tm, tk), lambda i, j, l: (i, l)), pl.BlockSpec((tn, tk), lambda i, j, l: (j, l))]
    else:
        specs = [pl.BlockSpec((tk, tm), lambda i, j, l: (l, i)), pl.BlockSpec((tk, tn), lambda i, j, l: (l, j))]
    args = [a, b]
    if add is not None:
        specs.append(pl.BlockSpec((tm, tn), lambda i, j, l: (i, j)))
        args.append(add)
    grid = (m // tm, n // tn, nk)

    def when():
        i, j, l = pl.program_id(0), pl.program_id(1), pl.program_id(2)
        first = jnp.logical_and(jnp.logical_and(i == 0, j == 0), l == 0)
        last = jnp.logical_and(jnp.logical_and(i == grid[0] - 1, j == grid[1] - 1), l == nk - 1)
        return first, last, last

    (out,), extra = _host_call(
        body, name, comm, when, [jax.ShapeDtypeStruct((m, n), out_dtype)], grid, specs,
        [pl.BlockSpec((tm, tn), lambda i, j, l: (i, j))], [pltpu.VMEM((tm, tn), F32)],
        ("parallel", "parallel", "arbitrary"), args)
    return out if comm is None else (out, extra)


def _rmsnorm_fwd(x, w, name):
    t, d = x.shape
    tr = _tile(t, 512)

    def body(x_ref, w_ref, o_ref):
        xv = x_ref[...]
        r = lax.rsqrt(jnp.mean(xv * xv, axis=1, keepdims=True) + EPS)
        o_ref[...] = (xv * r * w_ref[...]).astype(BF16)

    return pl.pallas_call(
        body, name=name,
        out_shape=jax.ShapeDtypeStruct((t, d), BF16),
        grid=(t // tr,),
        in_specs=[pl.BlockSpec((tr, d), lambda i: (i, 0)), pl.BlockSpec((1, d), lambda i: (0, 0))],
        out_specs=pl.BlockSpec((tr, d), lambda i: (i, 0)),
        compiler_params=_params(("parallel",)),
    )(x, w)


def _rmsnorm_bwd(dn, x, w, dres, name):
    t, d = x.shape
    tr = _tile(t, 512)

    def body(dn_ref, x_ref, w_ref, dres_ref, dx_ref, dw_ref):
        i = pl.program_id(0)
        xv = x_ref[...]
        g = dn_ref[...].astype(F32)
        r = lax.rsqrt(jnp.mean(xv * xv, axis=1, keepdims=True) + EPS)
        xh = xv * r
        dxh = g * w_ref[...]
        dx = r * (dxh - xh * jnp.mean(dxh * xh, axis=1, keepdims=True))
        dx_ref[...] = dres_ref[...] + dx

        @pl.when(i == 0)
        def _():
            dw_ref[...] = jnp.zeros_like(dw_ref)

        dw_ref[...] += jnp.sum(g * xh, axis=0, keepdims=True)

    return pl.pallas_call(
        body, name=name,
        out_shape=(jax.ShapeDtypeStruct((t, d), F32), jax.ShapeDtypeStruct((1, d), F32)),
        grid=(t // tr,),
        in_specs=[pl.BlockSpec((tr, d), lambda i: (i, 0)), pl.BlockSpec((tr, d), lambda i: (i, 0)),
                  pl.BlockSpec((1, d), lambda i: (0, 0)), pl.BlockSpec((tr, d), lambda i: (i, 0))],
        out_specs=(pl.BlockSpec((tr, d), lambda i: (i, 0)), pl.BlockSpec((1, d), lambda i: (0, 0))),
        compiler_params=_params(("arbitrary",)),
    )(dn, x, w, dres)


def _final_loss(x2, target, w, name):
    t, d = x2.shape
    tr = _tile(t, 512)

    def body(x_ref, t_ref, w_ref, dx_ref, dw_ref, loss_ref):
        i = pl.program_id(0)
        xv = x_ref[...]
        r = lax.rsqrt(jnp.mean(xv * xv, axis=1, keepdims=True) + EPS)
        xh = xv * r
        err = xh * w_ref[...] - t_ref[...]
        dy = err * (1.0 / d)
        dxh = dy * w_ref[...]
        dx_ref[...] = r * (dxh - xh * jnp.mean(dxh * xh, axis=1, keepdims=True))

        @pl.when(i == 0)
        def _():
            dw_ref[...] = jnp.zeros_like(dw_ref)
            loss_ref[...] = jnp.zeros_like(loss_ref)

        dw_ref[...] += jnp.sum(dy * xh, axis=0, keepdims=True)
        row = jnp.sum(err * err, axis=1, keepdims=True) * (0.5 / d)
        loss_ref[...] += jnp.sum(row, axis=0, keepdims=True)

    return pl.pallas_call(
        body, name=name,
        out_shape=(jax.ShapeDtypeStruct((t, d), F32), jax.ShapeDtypeStruct((1, d), F32),
                   jax.ShapeDtypeStruct((1, 1), F32)),
        grid=(t // tr,),
        in_specs=[pl.BlockSpec((tr, d), lambda i: (i, 0)), pl.BlockSpec((tr, d), lambda i: (i, 0)),
                  pl.BlockSpec((1, d), lambda i: (0, 0))],
        out_specs=(pl.BlockSpec((tr, d), lambda i: (i, 0)), pl.BlockSpec((1, d), lambda i: (0, 0)),
                   pl.BlockSpec((1, 1), lambda i: (0, 0))),
        compiler_params=_params(("arbitrary",)),
    )(x2, target, w)


def _shift_down(cur, prev, k, row):
    r = pltpu.roll(cur, k, 0)
    top, row8 = r[0:8, :], row[0:8, :]
    for m in range(k):
        top = jnp.where(row8 == m, prev[HALO - k + m:HALO - k + m + 1, :], top)
    return jnp.concatenate([top, r[8:, :]], axis=0)


def _shift_up(cur, nxt, k, row, tr):
    r = pltpu.roll(cur, tr - k, 0)
    bottom, row8 = r[tr - 8:, :], row[0:8, :]
    for m in range(k):
        bottom = jnp.where(row8 == 8 - k + m, nxt[m:m + 1, :], bottom)
    return jnp.concatenate([r[:tr - 8, :], bottom], axis=0)


def _conv_taps(cur, prev, w, ntaps, row):
    taps = [cur if i == ntaps - 1 else _shift_down(cur, prev, ntaps - 1 - i, row) for i in range(ntaps)]
    y = w[0:1, :] * taps[0]
    for i in range(1, ntaps):
        y = y + w[i:i + 1, :] * taps[i]
    return taps, y


def _conv_bwd_data(parts, w, ntaps, out_dtype, name):
    t, chp = parts[0].shape
    npart = len(parts)
    tr, tc = _tile(t, 512), _tile(chp, 1408)
    nc = chp // tc
    nhalo = t // HALO
    last = t // tr - 1

    def body(*refs):
        cur_refs, nxt_refs = refs[:npart], refs[npart:2 * npart]
        w_ref, o_ref = refs[2 * npart], refs[2 * npart + 1]
        i, j = pl.program_id(0), pl.program_id(1)
        cur, nxt = cur_refs[0][...].astype(F32), nxt_refs[0][...].astype(F32)
        for p in range(1, npart):
            cur = jnp.where(j >= p * nc, cur_refs[p][...].astype(F32), cur)
            nxt = jnp.where(j >= p * nc, nxt_refs[p][...].astype(F32), nxt)
        nxt = jnp.where(i == last, 0.0, nxt)
        row = _iota(cur.shape, 0)
        wv = w_ref[...]
        y = wv[ntaps - 1:ntaps, :] * cur
        for k in range(1, ntaps):
            y = y + wv[ntaps - 1 - k:ntaps - k, :] * _shift_up(cur, nxt, k, row, tr)
        o_ref[...] = y.astype(out_dtype)

    col = lambda p: (lambda j: jnp.clip(j - p * nc, 0, nc - 1))
    cur_specs = [pl.BlockSpec((tr, tc), lambda i, j, c=col(p): (i, c(j))) for p in range(npart)]
    nxt_specs = [pl.BlockSpec((HALO, tc),
                              lambda i, j, c=col(p): (jnp.minimum((i + 1) * (tr // HALO), nhalo - 1), c(j)))
                 for p in range(npart)]
    return pl.pallas_call(
        body, name=name,
        out_shape=jax.ShapeDtypeStruct((t, npart * chp), out_dtype),
        grid=(t // tr, npart * nc),
        in_specs=cur_specs + nxt_specs + [pl.BlockSpec((ntaps, tc), lambda i, j: (0, j))],
        out_specs=pl.BlockSpec((tr, tc), lambda i, j: (i, j)),
        compiler_params=_params(("parallel", "parallel")),
    )(*parts, *parts, w)


def _ffn_act_fwd(upre, cw, name):
    t = upre.shape[0]
    tr, tc = _tile(t, 512), _tile(D_FF, 1408)
    nj = D_FF // tc

    def body(g_ref, gp_ref, u_ref, up_ref, wg_ref, wu_ref, o_ref):
        i = pl.program_id(0)
        row = _iota((tr, tc), 0)
        gp = jnp.where(i == 0, 0.0, gp_ref[...].astype(F32))
        up = jnp.where(i == 0, 0.0, up_ref[...].astype(F32))
        _, gc = _conv_taps(g_ref[...].astype(F32), gp, wg_ref[...], FFN_CONV, row)
        _, uc = _conv_taps(u_ref[...].astype(F32), up, wu_ref[...], FFN_CONV, row)
        o_ref[...] = (gc * _sigmoid(gc) * uc).astype(BF16)

    prev = lambda off: (lambda i, j: (jnp.maximum(i * (tr // HALO) - 1, 0), j + off))
    return pl.pallas_call(
        body, name=name,
        out_shape=jax.ShapeDtypeStruct((t, D_FF), BF16),
        grid=(t // tr, nj),
        in_specs=[pl.BlockSpec((tr, tc), lambda i, j: (i, j)), pl.BlockSpec((HALO, tc), prev(0)),
                  pl.BlockSpec((tr, tc), lambda i, j: (i, j + nj)), pl.BlockSpec((HALO, tc), prev(nj)),
                  pl.BlockSpec((FFN_CONV, tc), lambda i, j: (0, j)),
                  pl.BlockSpec((FFN_CONV, tc), lambda i, j: (0, j + nj))],
        out_specs=pl.BlockSpec((tr, tc), lambda i, j: (i, j)),
        compiler_params=_params(("parallel", "parallel")),
    )(upre, upre, upre, upre, cw, cw)


def _ffn_act_bwd(dact, upre, cw, name):
    t = upre.shape[0]
    tr, tc = _tile(t, 256), _tile(D_FF, 1408)
    nj = D_FF // tc

    def body(da_ref, g_ref, gp_ref, u_ref, up_ref, wg_ref, wu_ref, dg_ref, du_ref, dwg_ref, dwu_ref):
        i = pl.program_id(1)
        row = _iota((tr, tc), 0)
        gp = jnp.where(i == 0, 0.0, gp_ref[...].astype(F32))
        up = jnp.where(i == 0, 0.0, up_ref[...].astype(F32))
        gt, gc = _conv_taps(g_ref[...].astype(F32), gp, wg_ref[...], FFN_CONV, row)
        ut, uc = _conv_taps(u_ref[...].astype(F32), up, wu_ref[...], FFN_CONV, row)
        da = da_ref[...].astype(F32)
        sg = _sigmoid(gc)
        dgc = da * uc * (sg * (1.0 + gc * (1.0 - sg)))
        duc = da * (gc * sg)
        dg_ref[...] = dgc.astype(BF16)
        du_ref[...] = duc.astype(BF16)

        @pl.when(i == 0)
        def _():
            dwg_ref[...] = jnp.zeros_like(dwg_ref)
            dwu_ref[...] = jnp.zeros_like(dwu_ref)

        for k in range(FFN_CONV):
            dwg_ref[k:k + 1, :] += jnp.sum(dgc * gt[k], axis=0, keepdims=True)
            dwu_ref[k:k + 1, :] += jnp.sum(duc * ut[k], axis=0, keepdims=True)

    prev = lambda off: (lambda j, i: (jnp.maximum(i * (tr // HALO) - 1, 0), j + off))
    blk = lambda off: pl.BlockSpec((tr, tc), lambda j, i: (i, j + off))
    wblk = lambda off: pl.BlockSpec((FFN_CONV, tc), lambda j, i: (0, j + off))
    dgc, duc, dwg, dwu = pl.pallas_call(
        body, name=name,
        out_shape=(jax.ShapeDtypeStruct((t, D_FF), BF16), jax.ShapeDtypeStruct((t, D_FF), BF16),
                   jax.ShapeDtypeStruct((FFN_CONV, D_FF), F32), jax.ShapeDtypeStruct((FFN_CONV, D_FF), F32)),
        grid=(nj, t // tr),
        in_specs=[blk(0), blk(0), pl.BlockSpec((HALO, tc), prev(0)), blk(nj), pl.BlockSpec((HALO, tc), prev(nj)),
                  wblk(0), wblk(nj)],
        out_specs=(blk(0), blk(0), wblk(0), wblk(0)),
        compiler_params=_params(("parallel", "arbitrary")),
    )(dact, upre, upre, upre, upre, cw, cw)
    return dgc, duc, dwg, dwu


def _dn_pre_fwd(qkv_pre, cw, name):
    t = qkv_pre.shape[0]
    tr = _tile(t, 512)
    scale = HEAD_DIM ** -0.5

    def body(x_ref, p_ref, w_ref, o_ref):
        i, j = pl.program_id(0), pl.program_id(1)
        row = _iota((tr, WIDTH), 0)
        prev = jnp.where(i == 0, 0.0, p_ref[...].astype(F32))
        _, c = _conv_taps(x_ref[...].astype(F32), prev, w_ref[...], DN_CONV, row)
        s = c * _sigmoid(c)
        for h in range(HEADS):
            sl = slice(h * HEAD_DIM, (h + 1) * HEAD_DIM)
            sh = s[:, sl]
            r = lax.rsqrt(jnp.sum(sh * sh, axis=1, keepdims=True) + EPS)
            o_ref[:, sl] = sh * jnp.where(j == 0, r * scale, jnp.where(j == 1, r, 1.0))

    return pl.pallas_call(
        body, name=name,
        out_shape=jax.ShapeDtypeStruct((t, 3 * WIDTH), F32),
        grid=(t // tr, 3),
        in_specs=[pl.BlockSpec((tr, WIDTH), lambda i, j: (i, j)),
                  pl.BlockSpec((HALO, WIDTH), lambda i, j: (jnp.maximum(i * (tr // HALO) - 1, 0), j)),
                  pl.BlockSpec((DN_CONV, WIDTH), lambda i, j: (0, j))],
        out_specs=pl.BlockSpec((tr, WIDTH), lambda i, j: (i, j)),
        compiler_params=_params(("parallel", "parallel")),
    )(qkv_pre, qkv_pre, cw)


def _dn_pre_bwd(dq, dk, dv, qkv_pre, cw, name):
    t = qkv_pre.shape[0]
    tr = _tile(t, 256)
    scale = HEAD_DIM ** -0.5

    def body(dq_ref, dk_ref, dv_ref, x_ref, p_ref, w_ref, dc_ref, dw_ref):
        j, i = pl.program_id(0), pl.program_id(1)
        row = _iota((tr, WIDTH), 0)
        prev = jnp.where(i == 0, 0.0, p_ref[...].astype(F32))
        taps, c = _conv_taps(x_ref[...].astype(F32), prev, w_ref[...], DN_CONV, row)
        d = jnp.where(j == 0, dq_ref[...] * scale, jnp.where(j == 1, dk_ref[...], dv_ref[...]))
        sg = _sigmoid(c)
        s = c * sg
        dsilu = sg * (1.0 + c * (1.0 - sg))

        @pl.when(i == 0)
        def _():
            dw_ref[...] = jnp.zeros_like(dw_ref)

        ds = []
        for h in range(HEADS):
            sl = slice(h * HEAD_DIM, (h + 1) * HEAD_DIM)
            sh, dh = s[:, sl], d[:, sl]
            r = lax.rsqrt(jnp.sum(sh * sh, axis=1, keepdims=True) + EPS)
            nh = sh * r
            ds_norm = r * (dh - nh * jnp.sum(nh * dh, axis=1, keepdims=True))
            ds.append(jnp.where(j < 2, ds_norm, dh))
        dc = jnp.concatenate(ds, axis=1) * dsilu
        dc_ref[...] = dc.astype(BF16)
        for k in range(DN_CONV):
            dw_ref[k:k + 1, :] += jnp.sum(dc * taps[k], axis=0, keepdims=True)

    dspec = lambda p: pl.BlockSpec((tr, WIDTH), lambda j, i: (jnp.where(j == p, i, 0), 0))
    return pl.pallas_call(
        body, name=name,
        out_shape=(jax.ShapeDtypeStruct((t, 3 * WIDTH), BF16), jax.ShapeDtypeStruct((DN_CONV, 3 * WIDTH), F32)),
        grid=(3, t // tr),
        in_specs=[dspec(0), dspec(1), dspec(2),
                  pl.BlockSpec((tr, WIDTH), lambda j, i: (i, j)),
                  pl.BlockSpec((HALO, WIDTH), lambda j, i: (jnp.maximum(i * (tr // HALO) - 1, 0), j)),
                  pl.BlockSpec((DN_CONV, WIDTH), lambda j, i: (0, j))],
        out_specs=(pl.BlockSpec((tr, WIDTH), lambda j, i: (i, j)),
                   pl.BlockSpec((DN_CONV, WIDTH), lambda j, i: (0, j))),
        compiler_params=_params(("parallel", "arbitrary")),
    )(dq, dk, dv, qkv_pre, qkv_pre, cw)


def _tri(n, kind):
    r, c = _iota((n, n), 0), _iota((n, n), 1)
    m = {"lower": r >= c, "strict": r > c, "upper": r <= c}[kind]
    return m


def _dn_gates_fwd(hab, alog, dtb, name):
    t = hab.shape[0]
    cc = DN_CHUNK

    def body(h_ref, al_ref, dt_ref, o_ref):
        hv = h_ref[...]
        lane = _iota(hv.shape, 1)
        xa = hv + dt_ref[...]
        sp = jnp.maximum(xa, 0.0) + _log1pexp_neg_abs(xa)
        g = jnp.where(lane < HEADS, -jnp.exp(al_ref[...]) * sp, 0.0)
        tril = jnp.where(_tri(cc, "lower"), 1.0, 0.0).astype(BF16)
        gc = _dot_xl(tril, g, NN)
        o_ref[...] = jnp.where(lane < HEADS, gc, jnp.where(lane < 2 * HEADS, _sigmoid(hv), 0.0))

    return pl.pallas_call(
        body, name=name,
        out_shape=jax.ShapeDtypeStruct((t, 128), F32),
        grid=(t // cc,),
        in_specs=[pl.BlockSpec((cc, 128), lambda i: (i, 0)), pl.BlockSpec((1, 128), lambda i: (0, 0)),
                  pl.BlockSpec((1, 128), lambda i: (0, 0))],
        out_specs=pl.BlockSpec((cc, 128), lambda i: (i, 0)),
        compiler_params=_params(("parallel",)),
    )(hab, alog, dtb)


def _dn_gates_bwd(dgates, hab, alog, dtb, name):
    t = hab.shape[0]
    cc = DN_CHUNK

    def body(d_ref, h_ref, al_ref, dt_ref, o_ref, dal_ref, ddt_ref):
        i = pl.program_id(0)
        hv = h_ref[...]
        dv = d_ref[...]
        lane = _iota(hv.shape, 1)
        triu = jnp.where(_tri(cc, "upper"), 1.0, 0.0).astype(BF16)
        dg = _dot_xl(triu, jnp.where(lane < HEADS, dv, 0.0), NN)
        xa = hv + dt_ref[...]
        sp = jnp.maximum(xa, 0.0) + _log1pexp_neg_abs(xa)
        ea = jnp.exp(al_ref[...])
        da = jnp.where(lane < HEADS, dg * (-ea) * _sigmoid(xa), 0.0)
        be = _sigmoid(hv)
        db = dv * be * (1.0 - be)
        o_ref[...] = jnp.where(lane < HEADS, da, jnp.where(lane < 2 * HEADS, db, 0.0))

        @pl.when(i == 0)
        def _():
            dal_ref[...] = jnp.zeros_like(dal_ref)
            ddt_ref[...] = jnp.zeros_like(ddt_ref)

        dal_ref[...] += jnp.sum(jnp.where(lane < HEADS, dg * (-ea) * sp, 0.0), axis=0, keepdims=True)
        ddt_ref[...] += jnp.sum(da, axis=0, keepdims=True)

    return pl.pallas_call(
        body, name=name,
        out_shape=(jax.ShapeDtypeStruct((t, 128), F32), jax.ShapeDtypeStruct((1, 128), F32),
                   jax.ShapeDtypeStruct((1, 128), F32)),
        grid=(t // cc,),
        in_specs=[pl.BlockSpec((cc, 128), lambda i: (i, 0)), pl.BlockSpec((cc, 128), lambda i: (i, 0)),
                  pl.BlockSpec((1, 128), lambda i: (0, 0)), pl.BlockSpec((1, 128), lambda i: (0, 0))],
        out_specs=(pl.BlockSpec((cc, 128), lambda i: (i, 0)), pl.BlockSpec((1, 128), lambda i: (0, 0)),
                   pl.BlockSpec((1, 128), lambda i: (0, 0))),
        compiler_params=_params(("arbitrary",)),
    )(dgates, hab, alog, dtb)


def _dn_chunk_common(gates, h):
    cc = DN_CHUNK
    lane = _iota(gates.shape, 1)
    gh = jnp.where(lane == h, gates, 0.0)
    gc_col = jnp.sum(gh, axis=1, keepdims=True)
    gc_row = _dot_xl(jnp.ones((cc, 128), BF16), gh, NT)
    beta = jnp.sum(jnp.where(lane == h + HEADS, gates, 0.0), axis=1, keepdims=True)
    lower = _tri(cc, "lower")
    decay = jnp.where(lower, jnp.exp(jnp.where(lower, gc_col - gc_row, 0.0)), 0.0)
    gc_last = gc_col[cc - 1:cc, :]
    return gc_col, gc_last, beta, decay


def _dn_local_fwd(act, gates, name):
    t = act.shape[0]
    cc = DN_CHUNK
    nc = t // cc

    def body(q_ref, k_ref, v_ref, g_ref, u_ref, w_ref, kd_ref, qg_ref, ti_ref, p_ref):
        gates = g_ref[...]
        eye = jnp.where(_iota((cc, cc), 0) == _iota((cc, cc), 1), 1.0, 0.0)
        hs = range(HEADS)
        sl = [slice(h * HEAD_DIM, (h + 1) * HEAD_DIM) for h in hs]
        q, k, v = ([r[:, s] for s in sl] for r in (q_ref, k_ref, v_ref))
        gc_col, gc_last, beta, decay = zip(*[_dn_chunk_common(gates, h) for h in hs])
        gam = [jnp.exp(g) for g in gc_col]
        kb = [k[h] * beta[h] for h in hs]
        npow = [-jnp.where(_tri(cc, "strict"), _dotb(kb[h], k[h], NT) * decay[h], 0.0) for h in hs]
        tinv = [eye + n for n in npow]
        for _ in range(5):
            npow = [_dot3(n, n, NN) for n in npow]
            tinv = [t + _dot3(t, n, NN) for t, n in zip(tinv, npow)]
        uu = [_dot3(tinv[h], v[h] * beta[h], NN) for h in hs]
        ww = [_dot3(tinv[h], kb[h] * gam[h], NN) for h in hs]
        pp = [jnp.where(_tri(cc, "lower"), _dotb(q[h], k[h], NT) * decay[h], 0.0) for h in hs]
        for h in hs:
            u_ref[:, sl[h]] = uu[h]
            w_ref[:, sl[h]] = ww[h]
            kd_ref[:, sl[h]] = k[h] * jnp.exp(gc_last[h] - gc_col[h])
            qg_ref[:, sl[h]] = q[h] * gam[h]
            ti_ref[h] = tinv[h]
            p_ref[h] = pp[h]

    row = lambda off: pl.BlockSpec((cc, WIDTH), lambda n: (n, off))
    mat = pl.BlockSpec((HEADS, cc, cc), lambda n: (0, n, 0))
    tw = jax.ShapeDtypeStruct((t, WIDTH), F32)
    hm = jax.ShapeDtypeStruct((HEADS, t, cc), F32)
    return pl.pallas_call(
        body, name=name,
        out_shape=(tw, tw, tw, tw, hm, hm),
        grid=(nc,),
        in_specs=[row(0), row(1), row(2), pl.BlockSpec((cc, 128), lambda n: (n, 0))],
        out_specs=(row(0), row(0), row(0), row(0), mat, mat),
        compiler_params=_params(("parallel",)),
    )(act, act, act, gates)


def _dn_scan_fwd(u, w, kd, qg, p, gates, name):
    t = u.shape[0]
    cc = DN_CHUNK
    nc = t // cc

    def body(u_ref, w_ref, kd_ref, qg_ref, p_ref, g_ref, o_ref, sh_ref, s_ref):
        n = pl.program_id(0)

        @pl.when(n == 0)
        def _():
            s_ref[...] = jnp.zeros_like(s_ref)

        glast = jnp.exp(g_ref[cc - 1:cc, :])
        hs = range(HEADS)
        sl = [slice(h * HEAD_DIM, (h + 1) * HEAD_DIM) for h in hs]
        s = [s_ref[h] for h in hs]
        sb = [a.astype(BF16) for a in s]
        vn = [u_ref[:, sl[h]] - _dot(w_ref[:, sl[h]].astype(BF16), sb[h], NN) for h in hs]
        vnb = [a.astype(BF16) for a in vn]
        o_state = [_dot(qg_ref[:, sl[h]].astype(BF16), sb[h], NN) for h in hs]
        o_local = [_dot(p_ref[h].astype(BF16), vnb[h], NN) for h in hs]
        s_add = [_dot(kd_ref[:, sl[h]].astype(BF16), vnb[h], TN) for h in hs]
        for h in hs:
            o_ref[:, sl[h]] = o_state[h] + o_local[h]
            sh_ref[0, h] = s[h]
            s_ref[h] = glast[:, h:h + 1] * s[h] + s_add[h]

    row = pl.BlockSpec((cc, WIDTH), lambda n: (n, 0))
    return pl.pallas_call(
        body, name=name,
        out_shape=(jax.ShapeDtypeStruct((t, WIDTH), F32),
                   jax.ShapeDtypeStruct((nc, HEADS, HEAD_DIM, HEAD_DIM), F32)),
        grid=(nc,),
        in_specs=[row, row, row, row, pl.BlockSpec((HEADS, cc, cc), lambda n: (0, n, 0)),
                  pl.BlockSpec((cc, 128), lambda n: (n, 0))],
        out_specs=(row, pl.BlockSpec((1, HEADS, HEAD_DIM, HEAD_DIM), lambda n: (n, 0, 0, 0))),
        scratch_shapes=[pltpu.VMEM((HEADS, HEAD_DIM, HEAD_DIM), F32)],
        compiler_params=_params(("arbitrary",)),
    )(u, w, kd, qg, p, gates)


def _dn_scan_bwd(do, w, kd, qg, p, gates, name):
    t = do.shape[0]
    cc = DN_CHUNK
    nc = t // cc

    def body(do_ref, w_ref, kd_ref, qg_ref, p_ref, g_ref, dvn_ref, dsh_ref, ds_ref):
        n = pl.program_id(0)

        @pl.when(n == 0)
        def _():
            ds_ref[...] = jnp.zeros_like(ds_ref)

        glast = jnp.exp(g_ref[cc - 1:cc, :])
        hs = range(HEADS)
        sl = [slice(h * HEAD_DIM, (h + 1) * HEAD_DIM) for h in hs]
        ds = [ds_ref[h] for h in hs]
        dob = [do_ref[:, sl[h]].astype(BF16) for h in hs]
        dvn = [_dot(p_ref[h].astype(BF16), dob[h], TN) + _dot(kd_ref[:, sl[h]].astype(BF16), ds[h].astype(BF16), NN)
               for h in hs]
        ds_q = [_dot(qg_ref[:, sl[h]].astype(BF16), dob[h], TN) for h in hs]
        ds_w = [_dot(w_ref[:, sl[h]].astype(BF16), dvn[h].astype(BF16), TN) for h in hs]
        for h in hs:
            dvn_ref[:, sl[h]] = dvn[h]
            dsh_ref[0, h] = ds[h]
            ds_ref[h] = ds_q[h] + glast[:, h:h + 1] * ds[h] - ds_w[h]

    row = pl.BlockSpec((cc, WIDTH), lambda n: (nc - 1 - n, 0))
    return pl.pallas_call(
        body, name=name,
        out_shape=(jax.ShapeDtypeStruct((t, WIDTH), F32),
                   jax.ShapeDtypeStruct((nc, HEADS, HEAD_DIM, HEAD_DIM), F32)),
        grid=(nc,),
        in_specs=[row, row, row, row, pl.BlockSpec((HEADS, cc, cc), lambda n: (0, nc - 1 - n, 0)),
                  pl.BlockSpec((cc, 128), lambda n: (nc - 1 - n, 0))],
        out_specs=(row, pl.BlockSpec((1, HEADS, HEAD_DIM, HEAD_DIM), lambda n: (nc - 1 - n, 0, 0, 0))),
        scratch_shapes=[pltpu.VMEM((HEADS, HEAD_DIM, HEAD_DIM), F32)],
        compiler_params=_params(("arbitrary",)),
    )(do, w, kd, qg, p, gates)


def _dn_local_bwd(act, gates, u, w, kd, qg, tinv, p, sh, dsh, dvn, do, name):
    t = act.shape[0]
    cc = DN_CHUNK
    nc = t // cc

    def body(q_ref, k_ref, v_ref, g_ref, u_ref, w_ref, kd_ref, qg_ref, ti_ref, p_ref, s_ref, ds_ref,
             dvn_ref, do_ref, dq_ref, dk_ref, dv_ref, dg_ref):
        gates_v = g_ref[...]
        lower, strict = _tri(cc, "lower"), _tri(cc, "strict")
        ones = jnp.ones((cc, 128), BF16)
        rowc = _iota((cc, 1), 0)
        lane = _iota((cc, 128), 1)
        hs = range(HEADS)
        sl = [slice(h * HEAD_DIM, (h + 1) * HEAD_DIM) for h in hs]
        q, k, v, uu, ww, kd, qg, dvn, do = ([r[:, s] for s in sl] for r in (
            q_ref, k_ref, v_ref, u_ref, w_ref, kd_ref, qg_ref, dvn_ref, do_ref))
        gc_col, gc_last, beta, decay = zip(*[_dn_chunk_common(gates_v, h) for h in hs])
        gam = [jnp.exp(g) for g in gc_col]
        kb = [k[h] * beta[h] for h in hs]
        s_in = [s_ref[0, h] for h in hs]
        ds_out = [ds_ref[0, h] for h in hs]
        tinv = [ti_ref[h] for h in hs]

        a = [jnp.where(strict, _dotb(kb[h], k[h], NT) * decay[h], 0.0) for h in hs]
        vn = [uu[h] - _dotb(ww[h], s_in[h], NN) for h in hs]
        dqg = [_dotb(do[h], s_in[h], NT) for h in hs]
        dw = [-_dotb(dvn[h], s_in[h], NT) for h in hs]
        dp = [jnp.where(lower, _dotb(do[h], vn[h], NT), 0.0) for h in hs]
        dkd = [_dotb(vn[h], ds_out[h], NT) for h in hs]
        dru = [_dot3(tinv[h], dvn[h], TN) for h in hs]
        drw = [_dot3(tinv[h], dw[h], TN) for h in hs]
        da = [-jnp.where(strict, _dotb(dru[h], uu[h], NT) + _dotb(drw[h], ww[h], NT), 0.0) for h in hs]
        dad = [da[h] * decay[h] for h in hs]
        dpd = [dp[h] * decay[h] for h in hs]
        dkb = [_dotb(dad[h], k[h], NN) + gam[h] * drw[h] for h in hs]
        dk = [_dotb(dad[h], kb[h], TN) + _dotb(dpd[h], q[h], TN) + beta[h] * dkb[h]
              + jnp.exp(gc_last[h] - gc_col[h]) * dkd[h] for h in hs]
        dq = [gam[h] * dqg[h] + _dotb(dpd[h], k[h], NN) for h in hs]
        gm = [da[h] * a[h] + dp[h] * p_ref[h] for h in hs]
        colsum = [_dot_xr(gm[h], ones, TN)[:, 0:1] for h in hs]

        dgates = jnp.zeros((cc, 128), F32)
        for h in hs:
            dk_ref[:, sl[h]] = dk[h]
            dq_ref[:, sl[h]] = dq[h]
            dv_ref[:, sl[h]] = beta[h] * dru[h]
            dbeta = (jnp.sum(dkb[h] * k[h], axis=1, keepdims=True)
                     + jnp.sum(dru[h] * v[h], axis=1, keepdims=True))
            rkd = jnp.sum(dkd[h] * kd[h], axis=1, keepdims=True)
            dgc = (jnp.sum(gm[h], axis=1, keepdims=True) - colsum[h]
                   + jnp.sum(dqg[h] * qg[h], axis=1, keepdims=True)
                   + jnp.sum(drw[h] * kb[h], axis=1, keepdims=True) * gam[h] - rkd)
            tail = jnp.sum(rkd, axis=0, keepdims=True) + jnp.exp(gc_last[h]) * jnp.sum(
                jnp.sum(s_in[h] * ds_out[h], axis=1, keepdims=True), axis=0, keepdims=True)
            dgc = dgc + jnp.where(rowc == cc - 1, tail, 0.0)
            dgates = dgates + jnp.where(lane == h, dgc, 0.0) + jnp.where(lane == h + HEADS, dbeta, 0.0)
        dg_ref[...] = dgates

    row = lambda off: pl.BlockSpec((cc, WIDTH), lambda n: (n, off))
    mat = pl.BlockSpec((HEADS, cc, cc), lambda n: (0, n, 0))
    st = pl.BlockSpec((1, HEADS, HEAD_DIM, HEAD_DIM), lambda n: (n, 0, 0, 0))
    gl = pl.BlockSpec((cc, 128), lambda n: (n, 0))
    tw = jax.ShapeDtypeStruct((t, WIDTH), F32)
    return pl.pallas_call(
        body, name=name,
        out_shape=(tw, tw, tw, jax.ShapeDtypeStruct((t, 128), F32)),
        grid=(nc,),
        in_specs=[row(0), row(1), row(2), gl, row(0), row(0), row(0), row(0), mat, mat, st, st, row(0), row(0)],
        out_specs=(row(0), row(0), row(0), gl),
        compiler_params=_params(("parallel",)),
    )(act, act, act, gates, u, w, kd, qg, tinv, p, sh, dsh, dvn, do)


def _dn_post_fwd(o, gate, w, name):
    t = o.shape[0]
    tr = _tile(t, 512)

    def body(o_ref, g_ref, w_ref, y_ref):
        for h in range(HEADS):
            sl = slice(h * HEAD_DIM, (h + 1) * HEAD_DIM)
            ov, gv = o_ref[:, sl], g_ref[:, sl].astype(F32)
            r = lax.rsqrt(jnp.mean(ov * ov, axis=1, keepdims=True) + EPS)
            y_ref[:, sl] = (ov * r * w_ref[...] * (gv * _sigmoid(gv))).astype(BF16)

    blk = pl.BlockSpec((tr, WIDTH), lambda i: (i, 0))
    return pl.pallas_call(
        body, name=name,
        out_shape=jax.ShapeDtypeStruct((t, WIDTH), BF16),
        grid=(t // tr,),
        in_specs=[blk, blk, pl.BlockSpec((1, HEAD_DIM), lambda i: (0, 0))],
        out_specs=blk,
        compiler_params=_params(("parallel",)),
    )(o, gate, w)


def _dn_post_bwd(dy, o, gate, w, name):
    t = o.shape[0]
    tr = _tile(t, 512)

    def body(dy_ref, o_ref, g_ref, w_ref, do_ref, dg_ref, dw_ref):
        i = pl.program_id(0)

        @pl.when(i == 0)
        def _():
            dw_ref[...] = jnp.zeros_like(dw_ref)

        dw = jnp.zeros((1, HEAD_DIM), F32)
        for h in range(HEADS):
            sl = slice(h * HEAD_DIM, (h + 1) * HEAD_DIM)
            ov, gv, dyv = o_ref[:, sl], g_ref[:, sl].astype(F32), dy_ref[:, sl].astype(F32)
            r = lax.rsqrt(jnp.mean(ov * ov, axis=1, keepdims=True) + EPS)
            oh = ov * r
            sg = _sigmoid(gv)
            dg_ref[:, sl] = (dyv * oh * w_ref[...] * (sg * (1.0 + gv * (1.0 - sg)))).astype(BF16)
            dn = dyv * (gv * sg)
            doh = dn * w_ref[...]
            do_ref[:, sl] = r * (doh - oh * jnp.mean(doh * oh, axis=1, keepdims=True))
            dw = dw + jnp.sum(dn * oh, axis=0, keepdims=True)
        dw_ref[...] += dw

    blk = pl.BlockSpec((tr, WIDTH), lambda i: (i, 0))
    return pl.pallas_call(
        body, name=name,
        out_shape=(jax.ShapeDtypeStruct((t, WIDTH), F32), jax.ShapeDtypeStruct((t, WIDTH), BF16),
                   jax.ShapeDtypeStruct((1, HEAD_DIM), F32)),
        grid=(t // tr,),
        in_specs=[blk, blk, blk, pl.BlockSpec((1, HEAD_DIM), lambda i: (0, 0))],
        out_specs=(blk, blk, pl.BlockSpec((1, HEAD_DIM), lambda i: (0, 0))),
        compiler_params=_params(("arbitrary",)),
    )(dy, o, gate, w)


def _sb_scores(qs, k_ref, qi, it, carries, uincl):
    bk = ATT_BLOCK
    scale = HEAD_DIM ** -0.5
    heads, groups = range(len(qs)), range(SB_GROUP)
    lane = [slice(e * HEAD_DIM, (e + 1) * HEAD_DIM) for e in heads]
    js = [qi - SB_GROUP * it - g for g in groups]
    rows = [pl.ds(pl.multiple_of(jnp.maximum(j, 0) * bk, bk), bk) for j in js]
    qpos = qi * bk + _iota((bk, bk), 0)
    col = _iota((bk, bk), 1)
    mask1 = [jnp.logical_and(j * bk + col < qpos, j >= 0) for j in js]
    ks = [[k_ref[r, lane[e]] for r in rows] for e in heads]
    z = [[_dot(qs[e], k, NT) * scale for k in ks[e]] for e in heads]
    soft = [[_log1pexp_neg_abs(a) for a in ze] for ze in z]
    lk_full = [[-(jnp.maximum(a, 0.0) + s) for a, s in zip(z[e], soft[e])] for e in heads]
    lk = [[jnp.where(m, a, 0.0) for m, a in zip(mask1, lk_full[e])] for e in heads]
    ls = [[jnp.minimum(a, 0.0) - s for a, s in zip(z[e], soft[e])] for e in heads]
    incl = [[_dot_xr2(a, uincl, NN) for a in lk[e]] for e in heads]
    weights, out_carries = [], []
    for e in heads:
        cb, we = carries[e], []
        for g in groups:
            we.append(jnp.where(mask1[g], jnp.exp(ls[e][g] + (cb + incl[e][g] - lk[e][g])), 0.0))
            cb = cb + incl[e][g][:, 0:1]
        weights.append(we)
        out_carries.append(cb)
    return rows, ks, weights, mask1, lk_full, ls, out_carries


def _sb_more(qi, carry):
    it, cbs = carry[0], carry[1]
    live = jnp.max(cbs[0])
    for cb in cbs[1:]:
        live = jnp.maximum(live, jnp.max(cb))
    return jnp.logical_and(SB_GROUP * it <= qi, live > SB_LOG_ZERO)


def _sb_steps(groups, nq):
    def when():
        h, i = pl.program_id(0), pl.program_id(1)
        return (jnp.logical_and(h == 0, i == 0), jnp.logical_and(h == groups // 2, i == 0),
                jnp.logical_and(h == groups - 1, i == nq - 1))
    return when


def _sb_fwd(qkv, name, comm=None):
    t = qkv.shape[0]
    bk = ATT_BLOCK
    hp, wide = SB_HEADS_FWD, SB_HEADS_FWD * HEAD_DIM
    lane = [slice(e * HEAD_DIM, (e + 1) * HEAD_DIM) for e in range(hp)]

    def body(q_ref, k_ref, v_ref, o_ref):
        qi = pl.program_id(1)
        qs = [q_ref[:, s] for s in lane]
        uincl = jnp.where(_tri(bk, "lower"), 1.0, 0.0).astype(BF16)

        def step(carry):
            it, cbs, accs = carry
            rows, _, weights, _, _, _, cbs = _sb_scores(qs, k_ref, qi, it, cbs, uincl)
            accs = list(accs)
            for e in range(hp):
                for r, a in zip(rows, weights[e]):
                    accs[e] = accs[e] + _dot(a.astype(BF16), v_ref[r, lane[e]], NN)
            return it + 1, tuple(cbs), tuple(accs)

        init = (jnp.int32(0), (jnp.zeros((bk, 1), F32),) * hp, (jnp.zeros((bk, HEAD_DIM), F32),) * hp)
        _, _, accs = lax.while_loop(functools.partial(_sb_more, qi), step, init)
        for e in range(hp):
            o_ref[:, lane[e]] = accs[e]

    groups = HEADS // hp
    (o,), extra = _host_call(
        body, name, comm, _sb_steps(groups, t // bk), [jax.ShapeDtypeStruct((t, WIDTH), F32)], (groups, t // bk),
        [pl.BlockSpec((bk, wide), lambda h, i: (i, h)),
         pl.BlockSpec((t, wide), lambda h, i: (0, groups + h)),
         pl.BlockSpec((t, wide), lambda h, i: (0, 2 * groups + h))],
        [pl.BlockSpec((bk, wide), lambda h, i: (i, h))], [], ("parallel", "arbitrary"), (qkv, qkv, qkv))
    return o, extra


def _sb_bwd(qkv, o, do, name, comm=None):
    assert do.dtype == BF16
    t = qkv.shape[0]
    bk = ATT_BLOCK
    scale = HEAD_DIM ** -0.5
    hp, wide = SB_HEADS_BWD, SB_HEADS_BWD * HEAD_DIM
    lane = [slice(e * HEAD_DIM, (e + 1) * HEAD_DIM) for e in range(hp)]

    def body(q_ref, k_ref, v_ref, o_ref, do_ref, dq_ref, dk_ref, dv_ref):
        qi = pl.program_id(1)

        @pl.when(qi == 0)
        def _():
            dk_ref[...] = jnp.zeros_like(dk_ref)
            dv_ref[...] = jnp.zeros_like(dv_ref)

        heads, groups = range(hp), range(SB_GROUP)
        qs = [q_ref[:, s] for s in lane]
        dob = [do_ref[:, s] for s in lane]
        dsum = [jnp.sum(dob[e].astype(F32) * o_ref[:, lane[e]], axis=1, keepdims=True) for e in heads]
        uincl = jnp.where(_tri(bk, "lower"), 1.0, 0.0).astype(BF16)

        def step(carry):
            it, cbs, ces, dqs = carry
            rows, ks, weights, mask, lk_full, ls, cbs = _sb_scores(qs, k_ref, qi, it, cbs, uincl)
            ab = [[a.astype(BF16) for a in weights[e]] for e in heads]
            vs = [[v_ref[r, lane[e]] for r in rows] for e in heads]
            dla = [[ab[e][g].astype(F32) * _dot(dob[e], vs[e][g], NT) for g in groups] for e in heads]
            suf = [[_dot_xr2(a, uincl, NN) for a in dla[e]] for e in heads]
            ces, dqs = list(ces), list(dqs)
            for e in heads:
                for g in groups:
                    err = dsum[e] - (ces[e] + suf[e][g])
                    ces[e] = ces[e] + suf[e][g][:, 0:1]
                    dz = jnp.where(mask[g], dla[e][g] * jnp.exp(lk_full[e][g]) - err * jnp.exp(ls[e][g]), 0.0)
                    dzb = (dz * scale).astype(BF16)
                    dqs[e] = dqs[e] + _dot(dzb, ks[e][g], NN)
                    dk_ref[rows[g], lane[e]] += _dot(dzb, qs[e], TN)
                    dv_ref[rows[g], lane[e]] += _dot(ab[e][g], dob[e], TN)
            return it + 1, tuple(cbs), tuple(ces), tuple(dqs)

        zc = (jnp.zeros((bk, 1), F32),) * hp
        init = (jnp.int32(0), zc, zc, (jnp.zeros((bk, HEAD_DIM), F32),) * hp)
        dqs = lax.while_loop(functools.partial(_sb_more, qi), step, init)[3]
        for e in heads:
            dq_ref[:, lane[e]] = dqs[e]

    ngroup = HEADS // hp
    tw = jax.ShapeDtypeStruct((t, WIDTH), F32)
    qb = pl.BlockSpec((bk, wide), lambda h, i: (i, h))
    full = lambda off: pl.BlockSpec((t, wide), lambda h, i: (0, off + h))
    return _host_call(
        body, name, comm, _sb_steps(ngroup, t // bk), [tw, tw, tw], (ngroup, t // bk),
        [qb, full(ngroup), full(2 * ngroup), qb, qb], [qb, full(0), full(0)], [], ("parallel", "arbitrary"),
        (qkv, qkv, qkv, o, do))


def _merge_fwd(pd, ps, gl, name):
    t = pd.shape[0]
    tr, tc = _tile(t, 512), 512
    nj = D_MODEL // tc

    def body(pd_ref, ps_ref, gd_ref, gs_ref, o_ref):
        gd, gs = gd_ref[...].astype(F32), gs_ref[...].astype(F32)
        o_ref[...] = (_sigmoid(gd) * pd_ref[...].astype(F32) + _sigmoid(gs) * ps_ref[...].astype(F32)).astype(BF16)

    blk = lambda off: pl.BlockSpec((tr, tc), lambda i, j: (i, j + off))
    return pl.pallas_call(
        body, name=name,
        out_shape=jax.ShapeDtypeStruct((t, D_MODEL), BF16),
        grid=(t // tr, nj),
        in_specs=[blk(0), blk(0), blk(0), blk(nj)],
        out_specs=blk(0),
        compiler_params=_params(("parallel", "parallel")),
    )(pd, ps, gl, gl)


def _merge_bwd(dm, pd, ps, gl, name):
    t = pd.shape[0]
    tr, tc = _tile(t, 512), 512
    nj = D_MODEL // tc

    def body(dm_ref, pd_ref, ps_ref, gd_ref, gs_ref, dpd_ref, dps_ref, dgd_ref, dgs_ref):
        dmv = dm_ref[...].astype(F32)
        sd, ss = _sigmoid(gd_ref[...].astype(F32)), _sigmoid(gs_ref[...].astype(F32))
        dpd_ref[...] = (dmv * sd).astype(BF16)
        dps_ref[...] = (dmv * ss).astype(BF16)
        dgd_ref[...] = (dmv * pd_ref[...].astype(F32) * sd * (1.0 - sd)).astype(BF16)
        dgs_ref[...] = (dmv * ps_ref[...].astype(F32) * ss * (1.0 - ss)).astype(BF16)

    blk = lambda off: pl.BlockSpec((tr, tc), lambda i, j: (i, j + off))
    out = jax.ShapeDtypeStruct((t, D_MODEL), BF16)
    return pl.pallas_call(
        body, name=name,
        out_shape=(out, out, out, out),
        grid=(t // tr, nj),
        in_specs=[blk(0), blk(0), blk(0), blk(0), blk(nj)],
        out_specs=(blk(0), blk(0), blk(0), blk(0)),
        compiler_params=_params(("parallel", "parallel")),
    )(dm, pd, ps, gl, gl)


def _local_step(x, target, wts, plan=None):
    n1 = _rmsnorm_fwd(x, wts["norm1_w"], "norm1_fwd")
    qkv_pre = _matmul(n1, wts["w_dnqkv_t"], "nt", BF16, "in_dnqkv")
    hgate = _matmul(n1, wts["w_dngate_t"], "nt", BF16, "in_dngate")
    sbqkv = _matmul(n1, wts["w_sbqkv_t"], "nt", BF16, "in_sbqkv")
    gl = _matmul(n1, wts["w_gl_t"], "nt", BF16, "in_gl")
    hab = _matmul(n1, wts["w_ab_t"], "nt", F32, "in_ab")

    act = _dn_pre_fwd(qkv_pre, wts["dn_conv_w"], "dn_pre_fwd")
    gates = _dn_gates_fwd(hab, wts["alog"], wts["dtb"], "dn_gates_fwd")
    u, w, kd, qg, tinv, p = _dn_local_fwd(act, gates, "dn_local_fwd")
    o_dn, sh = _dn_scan_fwd(u, w, kd, qg, p, gates, "dn_scan_fwd")
    y_dn = _dn_post_fwd(o_dn, hgate, wts["dn_norm_w"], "dn_post_fwd")

    o_sb, late = _sb_fwd(sbqkv, "sb_fwd", comm=plan.late_gather() if plan else None)
    if plan:
        wts = {**wts, **plan.late_weights(late)}

    pd = _matmul(y_dn, wts["w_proj_dn"], "nn", BF16, "proj_dn")
    ps = _matmul(o_sb, wts["w_proj_sb"], "nn", BF16, "proj_sb")
    mixed = _merge_fwd(pd, ps, gl, "merge_fwd")
    x1 = _matmul(mixed, wts["w_out"], "nn", F32, "out_proj", add=x)

    n2 = _rmsnorm_fwd(x1, wts["norm2_w"], "norm2_fwd")
    upre = _matmul(n2, wts["ffn_w_up_t"], "nt", BF16, "ffn_up")
    fact = _ffn_act_fwd(upre, wts["ffn_conv_w"], "ffn_act_fwd")
    x2 = _matmul(fact, wts["ffn_w_down"], "nn", F32, "ffn_down", add=x1)

    dx2, g_normf, loss = _final_loss(x2, target, wts["norm_f_w"], "final_loss")

    dfact = _matmul(dx2, wts["ffn_w_down"], "nt", BF16, "ffn_down_dx")
    g_wdown = _matmul(fact, dx2, "tn", BF16, "ffn_down_dw")
    dgc, duc, dwg, dwu = _ffn_act_bwd(dfact, upre, wts["ffn_conv_w"], "ffn_act_bwd")
    g_fconv = jnp.concatenate([dwg, dwu], axis=1)
    dupre = _conv_bwd_data([dgc, duc], wts["ffn_conv_w"], FFN_CONV, BF16, "ffn_conv_bwd")
    dn2 = _matmul(dupre, wts["ffn_w_up_t"], "nn", F32, "ffn_up_dx")
    g_wup = _matmul(dupre, n2, "tn", BF16, "ffn_up_dw")
    dx1, g_norm2 = _rmsnorm_bwd(dn2, x1, wts["norm2_w"], dx2, "norm2_bwd")

    dmixed = _matmul(dx1, wts["w_out"], "nt", BF16, "out_proj_dx")
    g_wout = _matmul(mixed, dx1, "tn", BF16, "out_proj_dw")
    dpd, dps, dgd, dgs = _merge_bwd(dmixed, pd, ps, gl, "merge_bwd")
    dy_dn = _matmul(dpd, wts["w_proj_dn"], "nt", BF16, "proj_dn_dx")
    g_wpd = _matmul(y_dn, dpd, "tn", BF16, "proj_dn_dw")
    do_sb = _matmul(dps, wts["w_proj_sb"], "nt", BF16, "proj_sb_dx")
    g_wps = _matmul(o_sb, dps, "tn", BF16, "proj_sb_dw")
    grads = dict(w_proj_dn=g_wpd, w_proj_sb=g_wps, w_out=g_wout, ffn_w_up_t=g_wup, ffn_w_down=g_wdown)

    (dsq, dsk, dsv), got_early = _sb_bwd(sbqkv, o_sb, do_sb, "sb_bwd",
                                         comm=plan.early_grads(grads) if plan else None)

    do_dn, dhgate, g_dnnorm = _dn_post_bwd(dy_dn, o_dn, hgate, wts["dn_norm_w"], "dn_post_bwd")
    dvn, dsh = _dn_scan_bwd(do_dn, w, kd, qg, p, gates, "dn_scan_bwd")
    dq, dk, dv, dgates = _dn_local_bwd(act, gates, u, w, kd, qg, tinv, p, sh, dsh, dvn, do_dn, "dn_local_bwd")
    dhab, g_alog, g_dtb = _dn_gates_bwd(dgates, hab, wts["alog"], wts["dtb"], "dn_gates_bwd")
    dcv, g_dnconv = _dn_pre_bwd(dq, dk, dv, qkv_pre, wts["dn_conv_w"], "dn_pre_bwd")
    dqkv_pre = _conv_bwd_data([dcv], wts["dn_conv_w"], DN_CONV, BF16, "dn_conv_bwd")

    dh = jnp.concatenate([dqkv_pre, dhgate, dsq.astype(BF16), dsk.astype(BF16), dsv.astype(BF16), dgd, dgs], axis=1)
    w_main_t = jnp.concatenate([wts["w_dnqkv_t"], wts["w_dngate_t"], wts["w_sbqkv_t"], wts["w_gl_t"]], axis=0)
    g_wmain = _matmul(dh, n1, "tn", BF16, "in_dw_main")
    g_wab = _matmul(dhab, n1, "tn", BF16, "in_dw_ab")
    grads.update(w_main_t=g_wmain, w_ab_t=g_wab, dn_conv_w=g_dnconv, alog=g_alog, dtb=g_dtb, dn_norm_w=g_dnnorm,
                 norm2_w=g_norm2, ffn_conv_w=g_fconv, norm_f_w=g_normf)
    got_late = []
    if plan:
        dn1, got_late = _matmul(dh, w_main_t, "nn", F32, "in_dx_main", comm=plan.late_grads(grads, loss))
    else:
        dn1 = _matmul(dh, w_main_t, "nn", F32, "in_dx_main")
    dn1 = _matmul(dhab, wts["w_ab_t"], "nn", F32, "in_dx_ab", add=dn1)
    grad_x, g_norm1 = _rmsnorm_bwd(dn1, x, wts["norm1_w"], dx1, "norm1_bwd")
    grads["norm1_w"] = g_norm1
    return loss, grad_x, grads, got_early, got_late


HBM_SPEC = pl.BlockSpec(memory_space=pltpu.HBM)


def _mesh_pos():
    x, y, c = lax.axis_index("x"), lax.axis_index("y"), lax.axis_index("c")
    return x, y, c, 4 * x + 2 * y + c


def _peer(k):
    x, y, c, _ = _mesh_pos()
    px = 1 - x if k & 4 else x
    py = 1 - y if k & 2 else y
    pc = 1 - c if k & 1 else c
    return (px, py, pc), 4 * px + 2 * py + pc


def _rcopy(src, dst, send, recv, a, s, peer):
    return pltpu.make_async_remote_copy(src_ref=src, dst_ref=dst, send_sem=send.at[a, s], recv_sem=recv.at[a, s],
                                        device_id=peer, device_id_type=pl.DeviceIdType.MESH)


class _Gather:
    ICI = (2, 4, 6)

    def __init__(self, shards):
        self.args = list(shards)
        self.n = len(shards)
        self.out_shape = [jax.ShapeDtypeStruct((N_DEV,) + s.shape, s.dtype) for s in shards]
        self.scratch = [pltpu.SemaphoreType.DMA((self.n, N_DEV - 1)), pltpu.SemaphoreType.DMA((self.n, N_DEV - 1)),
                        pltpu.SemaphoreType.DMA((self.n,))]

    def _first(self, ins, outs, send, recv, a):
        me = _mesh_pos()[3]
        out, got = [], []
        for s, k in enumerate((1,) + self.ICI):
            peer, pidx = _peer(k)
            out.append(_rcopy(ins[a], outs[a].at[me], send, recv, a, s, peer))
            got.append(_rcopy(ins[a], outs[a].at[pidx], send, recv, a, s, peer))
        return out, got

    def _forward(self, ins, outs, send, recv, a):
        sib = _peer(1)[0]
        out, got = [], []
        for s, k in enumerate(self.ICI):
            held = outs[a].at[_peer(k)[1]]
            out.append(_rcopy(held, held, send, recv, a, 4 + s, sib))
            other = outs[a].at[_peer(k | 1)[1]]
            got.append(_rcopy(other, other, send, recv, a, 4 + s, sib))
        return out, got

    def start(self, ins, outs, sems):
        send, recv, loc = sems
        me = _mesh_pos()[3]
        for a in range(self.n):
            pltpu.make_async_copy(ins[a], outs[a].at[me], loc.at[a]).start()
            for cp in self._first(ins, outs, send, recv, a)[0]:
                cp.start()

    def mid(self, ins, outs, sems):
        send, recv, _ = sems
        for a in range(self.n):
            arrivals = self._first(ins, outs, send, recv, a)[1]
            for s, cp in enumerate(self._forward(ins, outs, send, recv, a)[0]):
                arrivals[1 + s].wait_recv()
                cp.start()

    def finish(self, ins, outs, sems):
        send, recv, loc = sems
        me = _mesh_pos()[3]
        for a in range(self.n):
            first_out, first_got = self._first(ins, outs, send, recv, a)
            fwd_out, fwd_got = self._forward(ins, outs, send, recv, a)
            first_got[0].wait_recv()
            for cp in fwd_got:
                cp.wait_recv()
            for cp in first_out + fwd_out:
                cp.wait_send()
            pltpu.make_async_copy(ins[a], outs[a].at[me], loc.at[a]).wait()


class _Exchange:
    def __init__(self, slabs, gathered=()):
        self.args = list(slabs) + list(gathered)
        self.n_slab = len(slabs)
        self.n = len(self.args)
        self.out_shape = ([jax.ShapeDtypeStruct(s.shape, s.dtype) for s in slabs]
                          + [jax.ShapeDtypeStruct((N_DEV,) + s.shape, s.dtype) for s in gathered])
        self.scratch = [pltpu.SemaphoreType.DMA((self.n, N_DEV - 1)), pltpu.SemaphoreType.DMA((self.n, N_DEV - 1)),
                        pltpu.SemaphoreType.DMA((self.n,))]

    def _copies(self, ins, outs, send, recv, a):
        me = _mesh_pos()[3]
        out, got = [], []
        for k in range(1, N_DEV):
            peer, pidx = _peer(k)
            src = ins[a].at[pidx] if a < self.n_slab else ins[a]
            out.append(_rcopy(src, outs[a].at[me], send, recv, a, k - 1, peer))
            got.append(_rcopy(src, outs[a].at[pidx], send, recv, a, k - 1, peer))
        return out, got

    def _local(self, ins, outs, loc, a):
        me = _mesh_pos()[3]
        return pltpu.make_async_copy(ins[a].at[me] if a < self.n_slab else ins[a], outs[a].at[me], loc.at[a])

    def start(self, ins, outs, sems):
        send, recv, loc = sems
        for a in range(self.n):
            self._local(ins, outs, loc, a).start()
            for cp in self._copies(ins, outs, send, recv, a)[0]:
                cp.start()

    def mid(self, ins, outs, sems):
        pass

    def finish(self, ins, outs, sems):
        send, recv, loc = sems
        for a in range(self.n):
            out, got = self._copies(ins, outs, send, recv, a)
            for cp in got:
                cp.wait_recv()
            for cp in out:
                cp.wait_send()
            self._local(ins, outs, loc, a).wait()


def _comm_call(comm, name):
    n = comm.n

    def body(*refs):
        ins, outs, sems = refs[:n], refs[n:2 * n], refs[2 * n:]
        comm.start(ins, outs, sems)
        comm.mid(ins, outs, sems)
        comm.finish(ins, outs, sems)

    return pl.pallas_call(
        body, name=name, out_shape=comm.out_shape, in_specs=[HBM_SPEC] * n, out_specs=[HBM_SPEC] * n,
        scratch_shapes=comm.scratch,
    )(*comm.args)


def _hosted(body, comm, n_in, n_out, when):
    if comm is None:
        return body

    def wrapped(*refs):
        ins, c_ins = refs[:n_in], refs[n_in:n_in + comm.n]
        o0 = n_in + comm.n
        outs, c_outs = refs[o0:o0 + n_out], refs[o0 + n_out:o0 + n_out + comm.n]
        scratch, sems = refs[o0 + n_out + comm.n:len(refs) - 3], refs[len(refs) - 3:]
        first, middle, last = when()

        @pl.when(first)
        def _():
            comm.start(c_ins, c_outs, sems)

        body(*ins, *outs, *scratch)

        @pl.when(middle)
        def _():
            comm.mid(c_ins, c_outs, sems)

        @pl.when(last)
        def _():
            comm.finish(c_ins, c_outs, sems)

    return wrapped


def _host_call(body, name, comm, when, out_shape, grid, in_specs, out_specs, scratch_shapes, sem, args):
    n_in, n_out = len(in_specs), len(out_specs)
    if comm is None:
        res = pl.pallas_call(body, name=name, out_shape=out_shape, grid=grid, in_specs=in_specs, out_specs=out_specs,
                             scratch_shapes=scratch_shapes, compiler_params=_params(sem))(*args)
        return list(res), []
    res = pl.pallas_call(
        _hosted(body, comm, n_in, n_out, when), name=name,
        out_shape=list(out_shape) + comm.out_shape, grid=grid,
        in_specs=list(in_specs) + [HBM_SPEC] * comm.n, out_specs=list(out_specs) + [HBM_SPEC] * comm.n,
        scratch_shapes=list(scratch_shapes) + comm.scratch,
        compiler_params=_params(("arbitrary",) * len(grid)),
    )(*args, *comm.args)
    return list(res[:n_out]), list(res[n_out:])


def _adamw(parts, w, m, v, name):
    rows, cols = w.shape
    tr, tc = rows, cols
    for cand in (128, 176):
        if rows > cand and rows % cand == 0:
            tr = cand
            break
    if tr == rows and rows > 512:
        tc = _tile(cols, 256)

    def body(p_ref, w_ref, m_ref, v_ref, g_ref, d_ref, mo_ref, vo_ref):
        g = p_ref[0].astype(F32)
        for s in range(1, N_DEV):
            g = g + p_ref[s].astype(F32)
        mn = ADAM_B1 * m_ref[...] + (1.0 - ADAM_B1) * g
        vn = ADAM_B2 * v_ref[...] + (1.0 - ADAM_B2) * (g * g)
        m_hat = mn / (1.0 - ADAM_B1 ** ADAM_STEP)
        v_hat = vn / (1.0 - ADAM_B2 ** ADAM_STEP)
        g_ref[...] = g
        d_ref[...] = -ADAM_LR * (m_hat / (jnp.sqrt(v_hat) + ADAM_EPS) + ADAM_WD * w_ref[...])
        mo_ref[...] = mn
        vo_ref[...] = vn

    blk = pl.BlockSpec((tr, tc), lambda i, j: (i, j))
    out = jax.ShapeDtypeStruct((rows, cols), F32)
    return pl.pallas_call(
        body, name=name,
        out_shape=(out, out, out, out),
        grid=(rows // tr, cols // tc),
        in_specs=[pl.BlockSpec((N_DEV, tr, tc), lambda i, j: (0, i, j)), blk, blk, blk],
        out_specs=(blk, blk, blk, blk),
        compiler_params=_params(("parallel", "parallel")),
    )(parts, w, m, v)


CONV_PACK = 8 * 1024
WEIGHT_ORDER = ("norm1_w", "w_in", "dn_conv_w", "dn_A_log", "dn_dt_bias", "dn_norm_w", "w_proj_dn", "w_proj_sb",
                "w_out", "norm2_w", "ffn_w_up", "ffn_conv_w", "ffn_w_down", "norm_f_w")


def _cols_to_slabs(g):
    r, c8 = g.shape
    return g.reshape(r, N_DEV, c8 // N_DEV).transpose(1, 0, 2)


def _slabs_to_cols(s):
    d, r, c = s.shape
    return s.transpose(1, 0, 2).reshape(r, d * c)


def kernel(x, norm1_w, w_in, dn_conv_w, dn_A_log, dn_dt_bias, dn_norm_w, w_proj_dn, w_proj_sb, w_out, norm2_w, ffn_w_up, ffn_conv_w, ffn_w_down, norm_f_w, loss_target, m_norm1_w, m_w_in, m_dn_conv_w, m_dn_A_log, m_dn_dt_bias, m_dn_norm_w, m_w_proj_dn, m_w_proj_sb, m_w_out, m_norm2_w, m_ffn_w_up, m_ffn_conv_w, m_ffn_w_down, m_norm_f_w, v_norm1_w, v_w_in, v_dn_conv_w, v_dn_A_log, v_dn_dt_bias, v_dn_norm_w, v_w_proj_dn, v_w_proj_sb, v_w_out, v_norm2_w, v_ffn_w_up, v_ffn_conv_w, v_ffn_w_down, v_norm_f_w):
    me = _mesh_pos()[3]
    tr = lambda a: jnp.transpose(a[0])
    w_loc = dict(norm1_w=norm1_w, w_in=tr(w_in), dn_conv_w=dn_conv_w[0], dn_A_log=dn_A_log, dn_dt_bias=dn_dt_bias,
                 dn_norm_w=dn_norm_w, w_proj_dn=w_proj_dn[0], w_proj_sb=w_proj_sb[0], w_out=w_out[0],
                 norm2_w=norm2_w, ffn_w_up=tr(ffn_w_up), ffn_conv_w=ffn_conv_w[0], ffn_w_down=ffn_w_down[0],
                 norm_f_w=norm_f_w[None, :])
    m_loc = dict(norm1_w=m_norm1_w, w_in=tr(m_w_in), dn_conv_w=m_dn_conv_w[0], dn_A_log=m_dn_A_log,
                 dn_dt_bias=m_dn_dt_bias, dn_norm_w=m_dn_norm_w, w_proj_dn=m_w_proj_dn[0], w_proj_sb=m_w_proj_sb[0],
                 w_out=m_w_out[0], norm2_w=m_norm2_w, ffn_w_up=tr(m_ffn_w_up), ffn_conv_w=m_ffn_conv_w[0],
                 ffn_w_down=m_ffn_w_down[0], norm_f_w=m_norm_f_w[None, :])
    v_loc = dict(norm1_w=v_norm1_w, w_in=tr(v_w_in), dn_conv_w=v_dn_conv_w[0], dn_A_log=v_dn_A_log,
                 dn_dt_bias=v_dn_dt_bias, dn_norm_w=v_dn_norm_w, w_proj_dn=v_w_proj_dn[0], w_proj_sb=v_w_proj_sb[0],
                 w_out=v_w_out[0], norm2_w=v_norm2_w, ffn_w_up=tr(v_ffn_w_up), ffn_conv_w=v_ffn_conv_w[0],
                 ffn_w_down=v_ffn_w_down[0], norm_f_w=v_norm_f_w[None, :])

    conv_flat = jnp.concatenate([w_loc["dn_conv_w"].reshape(-1), w_loc["ffn_conv_w"].reshape(-1)])
    n_dn, n_ffn = DN_CONV * 3 * WIDTH // N_DEV, FFN_CONV * 2 * D_FF // N_DEV
    conv_pack = jnp.pad(conv_flat, (0, CONV_PACK - n_dn - n_ffn)).reshape(8, 1024)
    g_in, g_conv = _comm_call(_Gather([w_loc["w_in"].astype(BF16), conv_pack]), "gather_first")
    in_width = g_in.shape[0] * g_in.shape[1]
    w_in_t = g_in.reshape(in_width, D_MODEL)
    g_conv = g_conv.reshape(N_DEV, CONV_PACK)
    dn_conv_full = _slabs_to_cols(g_conv[:, :n_dn].reshape(N_DEV, DN_CONV, 3 * WIDTH // N_DEV))
    ffn_conv_full = _slabs_to_cols(g_conv[:, n_dn:n_dn + n_ffn].reshape(N_DEV, FFN_CONV, 2 * D_FF // N_DEV))
    q_end = 3 * WIDTH
    ab_end = q_end + 2 * HEADS
    gate_end = ab_end + WIDTH
    sb_end = gate_end + 3 * WIDTH
    pad_lanes = lambda a: jnp.pad(a, ((0, 0), (0, 128 - a.shape[1])))
    wts = dict(
        norm1_w=norm1_w, w_dnqkv_t=w_in_t[:q_end], w_ab_t=jnp.pad(w_in_t[q_end:ab_end], ((0, 128 - 2 * HEADS), (0, 0))),
        w_dngate_t=w_in_t[ab_end:gate_end], w_sbqkv_t=w_in_t[gate_end:sb_end], w_gl_t=w_in_t[sb_end:],
        dn_conv_w=dn_conv_full, alog=pad_lanes(dn_A_log), dtb=pad_lanes(dn_dt_bias), dn_norm_w=dn_norm_w,
        norm2_w=norm2_w, ffn_conv_w=ffn_conv_full, norm_f_w=norm_f_w[None, :])

    n_fc = FFN_CONV * 2 * D_FF
    fc_rows = -(-n_fc // D_MODEL)
    dn_rows = DN_CONV * 3 * WIDTH // D_MODEL
    late_names = ("w_proj_dn", "w_proj_sb", "w_out", "ffn_w_up", "ffn_w_down")

    class Plan:
        @staticmethod
        def late_gather():
            return _Gather([w_loc[k].astype(BF16) for k in late_names])

        @staticmethod
        def late_weights(got):
            g_pd, g_ps, g_out, g_up, g_down = got
            return dict(w_proj_dn=g_pd.reshape(WIDTH, D_MODEL), w_proj_sb=g_ps.reshape(WIDTH, D_MODEL),
                        w_out=g_out.reshape(D_MODEL, D_MODEL), ffn_w_up_t=g_up.reshape(2 * D_FF, D_MODEL),
                        ffn_w_down=g_down.reshape(D_FF, D_MODEL))

        @staticmethod
        def early_grads(g):
            return _Exchange([g["w_proj_dn"].reshape(N_DEV, WIDTH // N_DEV, D_MODEL),
                              g["w_proj_sb"].reshape(N_DEV, WIDTH // N_DEV, D_MODEL),
                              g["w_out"].reshape(N_DEV, D_MODEL // N_DEV, D_MODEL),
                              g["ffn_w_up_t"].reshape(N_DEV, 2 * D_FF // N_DEV, D_MODEL),
                              g["ffn_w_down"].reshape(N_DEV, D_FF // N_DEV, D_MODEL)])

        @staticmethod
        def late_grads(g, loss):
            g_win_t = jnp.concatenate([g["w_main_t"][:q_end], g["w_ab_t"][:2 * HEADS], g["w_main_t"][q_end:]],
                                      axis=0)
            row3 = jnp.concatenate([g["dn_norm_w"], g["alog"], g["dtb"], jnp.pad(loss, ((0, 0), (0, 127))),
                                    jnp.zeros((1, D_MODEL - 512), F32)], axis=1)
            fconv_rows = jnp.pad(g["ffn_conv_w"].reshape(-1), (0, fc_rows * D_MODEL - n_fc)).reshape(fc_rows, D_MODEL)
            pad8 = lambda a: jnp.pad(a, ((0, -a.shape[0] % 8), (0, 0)))
            pieces = [g["norm2_w"], g["norm_f_w"], row3, g["dn_conv_w"].reshape(dn_rows, D_MODEL), fconv_rows]
            small = jnp.concatenate([pad8(a) for a in pieces], axis=0)
            assert small.shape[0] == SMALL_ROWS
            return _Exchange([g_win_t.reshape(N_DEV, in_width // N_DEV, D_MODEL)], [small])

    loss, grad_x, g, got_early, got_late = _local_step(x[0], loss_target[0], wts, Plan)
    r_pd, r_ps, r_out, r_up, r_down = got_early
    r_in, r_small = got_late
    (r_norm1,) = _comm_call(_Exchange([], [jnp.pad(g["norm1_w"], ((0, 7), (0, 0)))]), "gather_norm1")

    parts = dict(w_in=r_in, w_proj_dn=r_pd, w_proj_sb=r_ps, w_out=r_out, ffn_w_up=r_up, ffn_w_down=r_down)
    parts["norm1_w"] = r_norm1[:, 0:1, :]
    parts["norm2_w"] = r_small[:, 0:1, :]
    parts["norm_f_w"] = r_small[:, 8:9, :]
    parts["dn_norm_w"] = r_small[:, 16:17, 0:HEAD_DIM]
    parts["dn_A_log"] = r_small[:, 16:17, 128:128 + HEADS]
    parts["dn_dt_bias"] = r_small[:, 16:17, 256:256 + HEADS]
    dnc = r_small[:, 24:24 + dn_rows, :].reshape(N_DEV, DN_CONV, 3 * WIDTH)
    parts["dn_conv_w"] = lax.dynamic_slice_in_dim(dnc, me * (3 * WIDTH // N_DEV), 3 * WIDTH // N_DEV, axis=2)
    fc0 = 24 + dn_rows + (-dn_rows % 8)
    fcc = r_small[:, fc0:fc0 + fc_rows, :].reshape(N_DEV, fc_rows * D_MODEL)[:, :n_fc]
    fcc = fcc.reshape(N_DEV, FFN_CONV, 2 * D_FF)
    parts["ffn_conv_w"] = lax.dynamic_slice_in_dim(fcc, me * (2 * D_FF // N_DEV), 2 * D_FF // N_DEV, axis=2)
    loss_total = jnp.sum(r_small[:, 16, 384])

    res = {k: _adamw(parts[k], w_loc[k], m_loc[k], v_loc[k], "adamw_" + k) for k in WEIGHT_ORDER}
    lead = ("w_in", "dn_conv_w", "w_proj_dn", "w_proj_sb", "w_out", "ffn_w_up", "ffn_conv_w", "ffn_w_down")

    def shaped(k, a):
        if k in ("w_in", "ffn_w_up"):
            return jnp.transpose(a)[None]
        if k in lead:
            return a[None]
        if k == "norm_f_w":
            return a[0]
        return a

    outs = [loss_total, grad_x[None]]
    for idx in range(4):
        outs += [shaped(k, res[k][idx]) for k in WEIGHT_ORDER]
    return tuple(outs)
```

```python
import functools

import jax
import jax.numpy as jnp
from jax import lax
from jax.experimental import pallas as pl
from jax.experimental.pallas import tpu as pltpu

F32 = jnp.float32
BF16 = jnp.bfloat16

N_DEV = 8
D_MODEL = 1024
HEADS = 8
HEAD_DIM = 128
WIDTH = HEADS * HEAD_DIM
DN_CONV = 4
DN_CHUNK = 64
D_FF = 2816
FFN_CONV = 3
EPS = 1e-6
HALO = 16
CHUNK_ROWS = 256
ATT_BLOCK = 256
SB_LOG_ZERO = -104.0
SB_GROUP = 2
SB_HEADS_FWD = 4
SB_HEADS_BWD = 2
SMALL_ROWS = 64

ADAM_LR = 0.001
ADAM_B1 = 0.9
ADAM_B2 = 0.999
ADAM_EPS = 1e-08
ADAM_WD = 0.01
ADAM_STEP = 10

VMEM_LIMIT = 48 * 1024 * 1024


def _params(sem=None, **kw):
    return pltpu.CompilerParams(dimension_semantics=sem, vmem_limit_bytes=VMEM_LIMIT, **kw)


def _tile(n, cap):
    if n <= cap:
        return n
    best = None
    for t in range(128, cap + 1, 128):
        if n % t == 0:
            best = t
    assert best is not None, (n, cap)
    return best


def _dot(a, b, dims):
    return lax.dot_general(a, b, ((dims[0], dims[1]), ((), ())), preferred_element_type=F32)


NN = ((1,), (0,))
NT = ((1,), (1,))
TN = ((0,), (0,))


def _dotb(a, b, dims):
    return _dot(a.astype(BF16), b.astype(BF16), dims)


def _split3(x):
    h1 = x.astype(BF16)
    r1 = x - h1.astype(F32)
    h2 = r1.astype(BF16)
    r2 = r1 - h2.astype(F32)
    return h1, h2, r2.astype(BF16)


def _dot_xr(a, b_exact, dims):
    a1, a2, a3 = _split3(a)
    return _dot(a1, b_exact, dims) + _dot(a2, b_exact, dims) + _dot(a3, b_exact, dims)


def _split2(x):
    h1 = x.astype(BF16)
    return h1, (x - h1.astype(F32)).astype(BF16)


def _dot_xr2(a, b_exact, dims):
    a1, a2 = _split2(a)
    return _dot(a1, b_exact, dims) + _dot(a2, b_exact, dims)


def _dot_xl(a_exact, b, dims):
    b1, b2, b3 = _split3(b)
    return _dot(a_exact, b1, dims) + _dot(a_exact, b2, dims) + _dot(a_exact, b3, dims)


def _dot3(a, b, dims):
    a1 = a.astype(BF16)
    a2 = (a - a1.astype(F32)).astype(BF16)
    b1 = b.astype(BF16)
    b2 = (b - b1.astype(F32)).astype(BF16)
    return _dot(a1, b1, dims) + (_dot(a1, b2, dims) + _dot(a2, b1, dims))


def _sigmoid(x):
    return 1.0 / (1.0 + jnp.exp(-x))


def _log1pexp_neg_abs(x):
    return jnp.log(1.0 + jnp.exp(-jnp.abs(x)))


def _iota(shape, dim):
    return lax.broadcasted_iota(jnp.int32, shape, dim)


def _matmul(a, b, mode, out_dtype, name, add=None, comm=None):
    if mode == "nn":
        (m, k), (k2, n) = a.shape, b.shape
    elif mode == "nt":
        (m, k), (n, k2) = a.shape, b.shape
    else:
        (k, m), (k2, n) = a.shape, b.shape
    assert k == k2, (a.shape, b.shape, mode)
    tm, tn, tk = _tile(m, 1408), _tile(n, 1408), _tile(k, 1536)
    nk = k // tk
    dims = {"nn": NN, "nt": NT, "tn": TN}[mode]

    def body(*refs):
        if add is None:
            a_ref, b_ref, o_ref, acc_ref = refs
        else:
            a_ref, b_ref, add_ref, o_ref, acc_ref = refs
        kk = pl.program_id(2)

        @pl.when(kk == 0)
        def _():
            acc_ref[...] = jnp.zeros_like(acc_ref)

        acc_ref[...] += _dotb(a_ref[...], b_ref[...], dims)

        @pl.when(kk == nk - 1)
        def _():
            r = acc_ref[...]
            if add is not None:
                r = r + add_ref[...].astype(F32)
            o_ref[...] = r.astype(out_dtype)

    if mode == "nn":
        specs = [pl.BlockSpec((tm, tk), lambda i, j, l: (i, l)), pl.BlockSpec((tk, tn), lambda i, j, l: (l, j))]
    elif mode == "nt":
        specs = [pl.BlockSpec((tm, tk), lambda i, j, l: (i, l)), pl.BlockSpec((tn, tk), lambda i, j, l: (j, l))]
    else:
        specs = [pl.BlockSpec((tk, tm), lambda i, j, l: (l, i)), pl.BlockSpec((tk, tn), lambda i, j, l: (l, j))]
    args = [a, b]
    if add is not None:
        specs.append(pl.BlockSpec((tm, tn), lambda i, j, l: (i, j)))
        args.append(add)
    grid = (m // tm, n // tn, nk)

    def when():
        i, j, l = pl.program_id(0), pl.program_id(1), pl.program_id(2)
        first = jnp.logical_and(jnp.logical_and(i == 0, j == 0), l == 0)
        last = jnp.logical_and(jnp.logical_and(i == grid[0] - 1, j == grid[1] - 1), l == nk - 1)
        return first, last, last

    (out,), extra = _host_call(
        body, name, comm, when, [jax.ShapeDtypeStruct((m, n), out_dtype)], grid, specs,
        [pl.BlockSpec((tm, tn), lambda i, j, l: (i, j))], [pltpu.VMEM((tm, tn), F32)],
        ("parallel", "parallel", "arbitrary"), args)
    return out if comm is None else (out, extra)


def _rmsnorm_fwd(x, w, name):
    t, d = x.shape
    tr = _tile(t, 512)

    def body(x_ref, w_ref, o_ref):
        xv = x_ref[...]
        r = lax.rsqrt(jnp.mean(xv * xv, axis=1, keepdims=True) + EPS)
        o_ref[...] = (xv * r * w_ref[...]).astype(BF16)

    return pl.pallas_call(
        body, name=name,
        out_shape=jax.ShapeDtypeStruct((t, d), BF16),
        grid=(t // tr,),
        in_specs=[pl.BlockSpec((tr, d), lambda i: (i, 0)), pl.BlockSpec((1, d), lambda i: (0, 0))],
        out_specs=pl.BlockSpec((tr, d), lambda i: (i, 0)),
        compiler_params=_params(("parallel",)),
    )(x, w)


def _rmsnorm_bwd(dn, x, w, dres, name):
    t, d = x.shape
    tr = _tile(t, 512)

    def body(dn_ref, x_ref, w_ref, dres_ref, dx_ref, dw_ref):
        i = pl.program_id(0)
        xv = x_ref[...]
        g = dn_ref[...].astype(F32)
        r = lax.rsqrt(jnp.mean(xv * xv, axis=1, keepdims=True) + EPS)
        xh = xv * r
        dxh = g * w_ref[...]
        dx = r * (dxh - xh * jnp.mean(dxh * xh, axis=1, keepdims=True))
        dx_ref[...] = dres_ref[...] + dx

        @pl.when(i == 0)
        def _():
            dw_ref[...] = jnp.zeros_like(dw_ref)

        dw_ref[...] += jnp.sum(g * xh, axis=0, keepdims=True)

    return pl.pallas_call(
        body, name=name,
        out_shape=(jax.ShapeDtypeStruct((t, d), F32), jax.ShapeDtypeStruct((1, d), F32)),
        grid=(t // tr,),
        in_specs=[pl.BlockSpec((tr, d), lambda i: (i, 0)), pl.BlockSpec((tr, d), lambda i: (i, 0)),
                  pl.BlockSpec((1, d), lambda i: (0, 0)), pl.BlockSpec((tr, d), lambda i: (i, 0))],
        out_specs=(pl.BlockSpec((tr, d), lambda i: (i, 0)), pl.BlockSpec((1, d), lambda i: (0, 0))),
        compiler_params=_params(("arbitrary",)),
    )(dn, x, w, dres)


def _final_loss(x2, target, w, name):
    t, d = x2.shape
    tr = _tile(t, 512)

    def body(x_ref, t_ref, w_ref, dx_ref, dw_ref, loss_ref):
        i = pl.program_id(0)
        xv = x_ref[...]
        r = lax.rsqrt(jnp.mean(xv * xv, axis=1, keepdims=True) + EPS)
        xh = xv * r
        err = xh * w_ref[...] - t_ref[...]
        dy = err * (1.0 / d)
        dxh = dy * w_ref[...]
        dx_ref[...] = r * (dxh - xh * jnp.mean(dxh * xh, axis=1, keepdims=True))

        @pl.when(i == 0)
        def _():
            dw_ref[...] = jnp.zeros_like(dw_ref)
            loss_ref[...] = jnp.zeros_like(loss_ref)

        dw_ref[...] += jnp.sum(dy * xh, axis=0, keepdims=True)
        row = jnp.sum(err * err, axis=1, keepdims=True) * (0.5 / d)
        loss_ref[...] += jnp.sum(row, axis=0, keepdims=True)

    return pl.pallas_call(
        body, name=name,
        out_shape=(jax.ShapeDtypeStruct((t, d), F32), jax.ShapeDtypeStruct((1, d), F32),
                   jax.ShapeDtypeStruct((1, 1), F32)),
        grid=(t // tr,),
        in_specs=[pl.BlockSpec((tr, d), lambda i: (i, 0)), pl.BlockSpec((tr, d), lambda i: (i, 0)),
                  pl.BlockSpec((1, d), lambda i: (0, 0))],
        out_specs=(pl.BlockSpec((tr, d), lambda i: (i, 0)), pl.BlockSpec((1, d), lambda i: (0, 0)),
                   pl.BlockSpec((1, 1), lambda i: (0, 0))),
        compiler_params=_params(("arbitrary",)),
    )(x2, target, w)


def _shift_down(cur, prev, k, row):
    r = pltpu.roll(cur, k, 0)
    top, row8 = r[0:8, :], row[0:8, :]
    for m in range(k):
        top = jnp.where(row8 == m, prev[HALO - k + m:HALO - k + m + 1, :], top)
    return jnp.concatenate([top, r[8:, :]], axis=0)


def _shift_up(cur, nxt, k, row, tr):
    r = pltpu.roll(cur, tr - k, 0)
    bottom, row8 = r[tr - 8:, :], row[0:8, :]
    for m in range(k):
        bottom = jnp.where(row8 == 8 - k + m, nxt[m:m + 1, :], bottom)
    return jnp.concatenate([r[:tr - 8, :], bottom], axis=0)


def _fold8(a):
    out = a[0:8, :]
    for r in range(8, a.shape[0], 8):
        out = out + a[r:r + 8, :]
    return out


def _conv_taps(cur, prev, w, ntaps, row):
    taps = [cur if i == ntaps - 1 else _shift_down(cur, prev, ntaps - 1 - i, row) for i in range(ntaps)]
    y = w[0:1, :] * taps[0]
    for i in range(1, ntaps):
        y = y + w[i:i + 1, :] * taps[i]
    return taps, y


def _conv_bwd_data(parts, w, ntaps, out_dtype, name):
    t, chp = parts[0].shape
    npart = len(parts)
    tr, tc = _tile(t, 512), _tile(chp, 1408)
    nc = chp // tc
    nhalo = t // HALO
    last = t // tr - 1

    def body(*refs):
        cur_refs, nxt_refs = refs[:npart], refs[npart:2 * npart]
        w_ref, o_ref = refs[2 * npart], refs[2 * npart + 1]
        i, j = pl.program_id(0), pl.program_id(1)
        row = _iota((tr, 128), 0)
        for c0 in range(0, tc, 128):
            sl = slice(c0, c0 + 128)
            cur, nxt = cur_refs[0][:, sl].astype(F32), nxt_refs[0][:, sl].astype(F32)
            for p in range(1, npart):
                cur = jnp.where(j >= p * nc, cur_refs[p][:, sl].astype(F32), cur)
                nxt = jnp.where(j >= p * nc, nxt_refs[p][:, sl].astype(F32), nxt)
            nxt = jnp.where(i == last, 0.0, nxt)
            wv = w_ref[:, sl]
            y = wv[ntaps - 1:ntaps, :] * cur
            for k in range(1, ntaps):
                y = y + wv[ntaps - 1 - k:ntaps - k, :] * _shift_up(cur, nxt, k, row, tr)
            o_ref[:, sl] = y.astype(out_dtype)

    col = lambda p: (lambda j: jnp.clip(j - p * nc, 0, nc - 1))
    cur_specs = [pl.BlockSpec((tr, tc), lambda i, j, c=col(p): (i, c(j))) for p in range(npart)]
    nxt_specs = [pl.BlockSpec((HALO, tc),
                              lambda i, j, c=col(p): (jnp.minimum((i + 1) * (tr // HALO), nhalo - 1), c(j)))
                 for p in range(npart)]
    return pl.pallas_call(
        body, name=name,
        out_shape=jax.ShapeDtypeStruct((t, npart * chp), out_dtype),
        grid=(t // tr, npart * nc),
        in_specs=cur_specs + nxt_specs + [pl.BlockSpec((ntaps, tc), lambda i, j: (0, j))],
        out_specs=pl.BlockSpec((tr, tc), lambda i, j: (i, j)),
        compiler_params=_params(("parallel", "parallel")),
    )(*parts, *parts, w)


def _ffn_act_fwd(upre, cw, name):
    t = upre.shape[0]
    tr, tc = _tile(t, 512), _tile(D_FF, 1408)
    nj = D_FF // tc

    def body(g_ref, gp_ref, u_ref, up_ref, wg_ref, wu_ref, o_ref):
        i = pl.program_id(0)
        row = _iota((tr, 128), 0)
        for c0 in range(0, tc, 128):
            sl = slice(c0, c0 + 128)
            gp = jnp.where(i == 0, 0.0, gp_ref[:, sl].astype(F32))
            up = jnp.where(i == 0, 0.0, up_ref[:, sl].astype(F32))
            _, gc = _conv_taps(g_ref[:, sl].astype(F32), gp, wg_ref[:, sl], FFN_CONV, row)
            _, uc = _conv_taps(u_ref[:, sl].astype(F32), up, wu_ref[:, sl], FFN_CONV, row)
            o_ref[:, sl] = (gc * _sigmoid(gc) * uc).astype(BF16)

    prev = lambda off: (lambda i, j: (jnp.maximum(i * (tr // HALO) - 1, 0), j + off))
    return pl.pallas_call(
        body, name=name,
        out_shape=jax.ShapeDtypeStruct((t, D_FF), BF16),
        grid=(t // tr, nj),
        in_specs=[pl.BlockSpec((tr, tc), lambda i, j: (i, j)), pl.BlockSpec((HALO, tc), prev(0)),
                  pl.BlockSpec((tr, tc), lambda i, j: (i, j + nj)), pl.BlockSpec((HALO, tc), prev(nj)),
                  pl.BlockSpec((FFN_CONV, tc), lambda i, j: (0, j)),
                  pl.BlockSpec((FFN_CONV, tc), lambda i, j: (0, j + nj))],
        out_specs=pl.BlockSpec((tr, tc), lambda i, j: (i, j)),
        compiler_params=_params(("parallel", "parallel")),
    )(upre, upre, upre, upre, cw, cw)


def _ffn_act_bwd(dact, upre, cw, name):
    t = upre.shape[0]
    tr, tc = _tile(t, 256), _tile(D_FF, 1408)
    nj = D_FF // tc

    def body(da_ref, g_ref, gp_ref, u_ref, up_ref, wg_ref, wu_ref, dg_ref, du_ref, dwg_ref, dwu_ref):
        i = pl.program_id(1)
        row = _iota((CHUNK_ROWS, 128), 0)

        @pl.when(i == 0)
        def _():
            dwg_ref[...] = jnp.zeros_like(dwg_ref)
            dwu_ref[...] = jnp.zeros_like(dwu_ref)

        for c0 in range(0, tc, 128):
            sl = slice(c0, c0 + 128)
            wg, wu = wg_ref[:, sl], wu_ref[:, sl]
            dwg = [jnp.zeros((8, 128), F32)] * FFN_CONV
            dwu = [jnp.zeros((8, 128), F32)] * FFN_CONV
            for r0 in range(0, tr, CHUNK_ROWS):
                rows = slice(r0, r0 + CHUNK_ROWS)
                if r0 == 0:
                    gp = jnp.where(i == 0, 0.0, gp_ref[:, sl].astype(F32))
                    up = jnp.where(i == 0, 0.0, up_ref[:, sl].astype(F32))
                else:
                    gp = g_ref[r0 - HALO:r0, sl].astype(F32)
                    up = u_ref[r0 - HALO:r0, sl].astype(F32)
                gt, gc = _conv_taps(g_ref[rows, sl].astype(F32), gp, wg, FFN_CONV, row)
                ut, uc = _conv_taps(u_ref[rows, sl].astype(F32), up, wu, FFN_CONV, row)
                da = da_ref[rows, sl].astype(F32)
                sg = _sigmoid(gc)
                dgc = da * uc * (sg * (1.0 + gc * (1.0 - sg)))
                duc = da * (gc * sg)
                dg_ref[rows, sl] = dgc.astype(BF16)
                du_ref[rows, sl] = duc.astype(BF16)
                dwg = [dwg[k] + _fold8(dgc * gt[k]) for k in range(FFN_CONV)]
                dwu = [dwu[k] + _fold8(duc * ut[k]) for k in range(FFN_CONV)]
            for k in range(FFN_CONV):
                dwg_ref[k:k + 1, sl] += jnp.sum(dwg[k], axis=0, keepdims=True)
                dwu_ref[k:k + 1, sl] += jnp.sum(dwu[k], axis=0, keepdims=True)

    prev = lambda off: (lambda j, i: (jnp.maximum(i * (tr // HALO) - 1, 0), j + off))
    blk = lambda off: pl.BlockSpec((tr, tc), lambda j, i: (i, j + off))
    wblk = lambda off: pl.BlockSpec((FFN_CONV, tc), lambda j, i: (0, j + off))
    dgc, duc, dwg, dwu = pl.pallas_call(
        body, name=name,
        out_shape=(jax.ShapeDtypeStruct((t, D_FF), BF16), jax.ShapeDtypeStruct((t, D_FF), BF16),
                   jax.ShapeDtypeStruct((FFN_CONV, D_FF), F32), jax.ShapeDtypeStruct((FFN_CONV, D_FF), F32)),
        grid=(nj, t // tr),
        in_specs=[blk(0), blk(0), pl.BlockSpec((HALO, tc), prev(0)), blk(nj), pl.BlockSpec((HALO, tc), prev(nj)),
                  wblk(0), wblk(nj)],
        out_specs=(blk(0), blk(0), wblk(0), wblk(0)),
        compiler_params=_params(("parallel", "arbitrary")),
    )(dact, upre, upre, upre, upre, cw, cw)
    return dgc, duc, dwg, dwu


def _dn_pre_fwd(qkv_pre, cw, name):
    t = qkv_pre.shape[0]
    tr = _tile(t, 512)
    scale = HEAD_DIM ** -0.5

    def body(x_ref, p_ref, w_ref, o_ref):
        i, j = pl.program_id(0), pl.program_id(1)
        row = _iota((tr, HEAD_DIM), 0)
        for h in range(HEADS):
            sl = slice(h * HEAD_DIM, (h + 1) * HEAD_DIM)
            prev = jnp.where(i == 0, 0.0, p_ref[:, sl].astype(F32))
            _, c = _conv_taps(x_ref[:, sl].astype(F32), prev, w_ref[:, sl], DN_CONV, row)
            s = c * _sigmoid(c)
            r = lax.rsqrt(jnp.sum(s * s, axis=1, keepdims=True) + EPS)
            o_ref[:, sl] = s * jnp.where(j == 0, r * scale, jnp.where(j == 1, r, 1.0))

    return pl.pallas_call(
        body, name=name,
        out_shape=jax.ShapeDtypeStruct((t, 3 * WIDTH), F32),
        grid=(t // tr, 3),
        in_specs=[pl.BlockSpec((tr, WIDTH), lambda i, j: (i, j)),
                  pl.BlockSpec((HALO, WIDTH), lambda i, j: (jnp.maximum(i * (tr // HALO) - 1, 0), j)),
                  pl.BlockSpec((DN_CONV, WIDTH), lambda i, j: (0, j))],
        out_specs=pl.BlockSpec((tr, WIDTH), lambda i, j: (i, j)),
        compiler_params=_params(("parallel", "parallel")),
    )(qkv_pre, qkv_pre, cw)


def _dn_pre_bwd(dq, dk, dv, qkv_pre, cw, name):
    t = qkv_pre.shape[0]
    tr = _tile(t, 256)
    scale = HEAD_DIM ** -0.5

    def body(dq_ref, dk_ref, dv_ref, x_ref, p_ref, w_ref, dc_ref, dw_ref):
        j, i = pl.program_id(0), pl.program_id(1)
        row = _iota((CHUNK_ROWS, HEAD_DIM), 0)

        @pl.when(i == 0)
        def _():
            dw_ref[...] = jnp.zeros_like(dw_ref)

        for h in range(HEADS):
            sl = slice(h * HEAD_DIM, (h + 1) * HEAD_DIM)
            wv = w_ref[:, sl]
            dw = [jnp.zeros((8, HEAD_DIM), F32)] * DN_CONV
            for r0 in range(0, tr, CHUNK_ROWS):
                rows = slice(r0, r0 + CHUNK_ROWS)
                if r0 == 0:
                    prev = jnp.where(i == 0, 0.0, p_ref[:, sl].astype(F32))
                else:
                    prev = x_ref[r0 - HALO:r0, sl].astype(F32)
                taps, c = _conv_taps(x_ref[rows, sl].astype(F32), prev, wv, DN_CONV, row)
                d = jnp.where(j == 0, dq_ref[rows, sl] * scale, jnp.where(j == 1, dk_ref[rows, sl], dv_ref[rows, sl]))
                sg = _sigmoid(c)
                s = c * sg
                r = lax.rsqrt(jnp.sum(s * s, axis=1, keepdims=True) + EPS)
                nh = s * r
                ds_norm = r * (d - nh * jnp.sum(nh * d, axis=1, keepdims=True))
                dc = jnp.where(j < 2, ds_norm, d) * (sg * (1.0 + c * (1.0 - sg)))
                dc_ref[rows, sl] = dc.astype(BF16)
                dw = [dw[k] + _fold8(dc * taps[k]) for k in range(DN_CONV)]
            for k in range(DN_CONV):
                dw_ref[k:k + 1, sl] += jnp.sum(dw[k], axis=0, keepdims=True)

    dspec = lambda p: pl.BlockSpec((tr, WIDTH), lambda j, i: (jnp.where(j == p, i, 0), 0))
    return pl.pallas_call(
        body, name=name,
        out_shape=(jax.ShapeDtypeStruct((t, 3 * WIDTH), BF16), jax.ShapeDtypeStruct((DN_CONV, 3 * WIDTH), F32)),
        grid=(3, t // tr),
        in_specs=[dspec(0), dspec(1), dspec(2),
                  pl.BlockSpec((tr, WIDTH), lambda j, i: (i, j)),
                  pl.BlockSpec((HALO, WIDTH), lambda j, i: (jnp.maximum(i * (tr // HALO) - 1, 0), j)),
                  pl.BlockSpec((DN_CONV, WIDTH), lambda j, i: (0, j))],
        out_specs=(pl.BlockSpec((tr, WIDTH), lambda j, i: (i, j)),
                   pl.BlockSpec((DN_CONV, WIDTH), lambda j, i: (0, j))),
        compiler_params=_params(("parallel", "arbitrary")),
    )(dq, dk, dv, qkv_pre, qkv_pre, cw)


def _tri(n, kind):
    r, c = _iota((n, n), 0), _iota((n, n), 1)
    m = {"lower": r >= c, "strict": r > c, "upper": r <= c}[kind]
    return m


def _dn_gates_fwd(hab, alog, dtb, name):
    t = hab.shape[0]
    cc = DN_CHUNK

    def body(h_ref, al_ref, dt_ref, o_ref):
        hv = h_ref[...]
        lane = _iota(hv.shape, 1)
        xa = hv + dt_ref[...]
        sp = jnp.maximum(xa, 0.0) + _log1pexp_neg_abs(xa)
        g = jnp.where(lane < HEADS, -jnp.exp(al_ref[...]) * sp, 0.0)
        tril = jnp.where(_tri(cc, "lower"), 1.0, 0.0).astype(BF16)
        gc = _dot_xl(tril, g, NN)
        o_ref[...] = jnp.where(lane < HEADS, gc, jnp.where(lane < 2 * HEADS, _sigmoid(hv), 0.0))

    return pl.pallas_call(
        body, name=name,
        out_shape=jax.ShapeDtypeStruct((t, 128), F32),
        grid=(t // cc,),
        in_specs=[pl.BlockSpec((cc, 128), lambda i: (i, 0)), pl.BlockSpec((1, 128), lambda i: (0, 0)),
                  pl.BlockSpec((1, 128), lambda i: (0, 0))],
        out_specs=pl.BlockSpec((cc, 128), lambda i: (i, 0)),
        compiler_params=_params(("parallel",)),
    )(hab, alog, dtb)


def _dn_gates_bwd(dgates, hab, alog, dtb, name):
    t = hab.shape[0]
    cc = DN_CHUNK

    def body(d_ref, h_ref, al_ref, dt_ref, o_ref, dal_ref, ddt_ref):
        i = pl.program_id(0)
        hv = h_ref[...]
        dv = d_ref[...]
        lane = _iota(hv.shape, 1)
        triu = jnp.where(_tri(cc, "upper"), 1.0, 0.0).astype(BF16)
        dg = _dot_xl(triu, jnp.where(lane < HEADS, dv, 0.0), NN)
        xa = hv + dt_ref[...]
        sp = jnp.maximum(xa, 0.0) + _log1pexp_neg_abs(xa)
        ea = jnp.exp(al_ref[...])
        da = jnp.where(lane < HEADS, dg * (-ea) * _sigmoid(xa), 0.0)
        be = _sigmoid(hv)
        db = dv * be * (1.0 - be)
        o_ref[...] = jnp.where(lane < HEADS, da, jnp.where(lane < 2 * HEADS, db, 0.0))

        @pl.when(i == 0)
        def _():
            dal_ref[...] = jnp.zeros_like(dal_ref)
            ddt_ref[...] = jnp.zeros_like(ddt_ref)

        dal_ref[...] += jnp.sum(jnp.where(lane < HEADS, dg * (-ea) * sp, 0.0), axis=0, keepdims=True)
        ddt_ref[...] += jnp.sum(da, axis=0, keepdims=True)

    return pl.pallas_call(
        body, name=name,
        out_shape=(jax.ShapeDtypeStruct((t, 128), F32), jax.ShapeDtypeStruct((1, 128), F32),
                   jax.ShapeDtypeStruct((1, 128), F32)),
        grid=(t // cc,),
        in_specs=[pl.BlockSpec((cc, 128), lambda i: (i, 0)), pl.BlockSpec((cc, 128), lambda i: (i, 0)),
                  pl.BlockSpec((1, 128), lambda i: (0, 0)), pl.BlockSpec((1, 128), lambda i: (0, 0))],
        out_specs=(pl.BlockSpec((cc, 128), lambda i: (i, 0)), pl.BlockSpec((1, 128), lambda i: (0, 0)),
                   pl.BlockSpec((1, 128), lambda i: (0, 0))),
        compiler_params=_params(("arbitrary",)),
    )(dgates, hab, alog, dtb)


def _dn_chunk_common(gates, h):
    cc = DN_CHUNK
    lane = _iota(gates.shape, 1)
    gh = jnp.where(lane == h, gates, 0.0)
    gc_col = jnp.sum(gh, axis=1, keepdims=True)
    gc_row = _dot_xl(jnp.ones((cc, 128), BF16), gh, NT)
    beta = jnp.sum(jnp.where(lane == h + HEADS, gates, 0.0), axis=1, keepdims=True)
    lower = _tri(cc, "lower")
    decay = jnp.where(lower, jnp.exp(jnp.where(lower, gc_col - gc_row, 0.0)), 0.0)
    gc_last = gc_col[cc - 1:cc, :]
    return gc_col, gc_last, beta, decay


def _dn_local_fwd(act, gates, name):
    t = act.shape[0]
    cc = DN_CHUNK
    nc = t // cc

    def body(q_ref, k_ref, v_ref, g_ref, u_ref, w_ref, kd_ref, qg_ref, ti_ref, p_ref):
        gates = g_ref[...]
        eye = jnp.where(_iota((cc, cc), 0) == _iota((cc, cc), 1), 1.0, 0.0)
        hs = range(HEADS)
        sl = [slice(h * HEAD_DIM, (h + 1) * HEAD_DIM) for h in hs]
        q, k, v = ([r[:, s] for s in sl] for r in (q_ref, k_ref, v_ref))
        gc_col, gc_last, beta, decay = zip(*[_dn_chunk_common(gates, h) for h in hs])
        gam = [jnp.exp(g) for g in gc_col]
        kb = [k[h] * beta[h] for h in hs]
        npow = [-jnp.where(_tri(cc, "strict"), _dotb(kb[h], k[h], NT) * decay[h], 0.0) for h in hs]
        tinv = [eye + n for n in npow]
        for _ in range(5):
            npow = [_dot3(n, n, NN) for n in npow]
            tinv = [t + _dot3(t, n, NN) for t, n in zip(tinv, npow)]
        uu = [_dot3(tinv[h], v[h] * beta[h], NN) for h in hs]
        ww = [_dot3(tinv[h], kb[h] * gam[h], NN) for h in hs]
        pp = [jnp.where(_tri(cc, "lower"), _dotb(q[h], k[h], NT) * decay[h], 0.0) for h in hs]
        for h in hs:
            u_ref[:, sl[h]] = uu[h]
            w_ref[:, sl[h]] = ww[h]
            kd_ref[:, sl[h]] = k[h] * jnp.exp(gc_last[h] - gc_col[h])
            qg_ref[:, sl[h]] = q[h] * gam[h]
            ti_ref[h] = tinv[h]
            p_ref[h] = pp[h]

    row = lambda off: pl.BlockSpec((cc, WIDTH), lambda n: (n, off))
    mat = pl.BlockSpec((HEADS, cc, cc), lambda n: (0, n, 0))
    tw = jax.ShapeDtypeStruct((t, WIDTH), F32)
    hm = jax.ShapeDtypeStruct((HEADS, t, cc), F32)
    return pl.pallas_call(
        body, name=name,
        out_shape=(tw, tw, tw, tw, hm, hm),
        grid=(nc,),
        in_specs=[row(0), row(1), row(2), pl.BlockSpec((cc, 128), lambda n: (n, 0))],
        out_specs=(row(0), row(0), row(0), row(0), mat, mat),
        compiler_params=_params(("parallel",)),
    )(act, act, act, gates)


def _dn_scan_fwd(u, w, kd, qg, p, gates, name):
    t = u.shape[0]
    cc = DN_CHUNK
    nc = t // cc

    def body(u_ref, w_ref, kd_ref, qg_ref, p_ref, g_ref, o_ref, sh_ref, s_ref):
        n = pl.program_id(0)

        @pl.when(n == 0)
        def _():
            s_ref[...] = jnp.zeros_like(s_ref)

        glast = jnp.exp(g_ref[cc - 1:cc, :])
        hs = range(HEADS)
        sl = [slice(h * HEAD_DIM, (h + 1) * HEAD_DIM) for h in hs]
        s = [s_ref[h] for h in hs]
        sb = [a.astype(BF16) for a in s]
        vn = [u_ref[:, sl[h]] - _dot(w_ref[:, sl[h]].astype(BF16), sb[h], NN) for h in hs]
        vnb = [a.astype(BF16) for a in vn]
        o_state = [_dot(qg_ref[:, sl[h]].astype(BF16), sb[h], NN) for h in hs]
        o_local = [_dot(p_ref[h].astype(BF16), vnb[h], NN) for h in hs]
        s_add = [_dot(kd_ref[:, sl[h]].astype(BF16), vnb[h], TN) for h in hs]
        for h in hs:
            o_ref[:, sl[h]] = o_state[h] + o_local[h]
            sh_ref[0, h] = s[h]
            s_ref[h] = glast[:, h:h + 1] * s[h] + s_add[h]

    row = pl.BlockSpec((cc, WIDTH), lambda n: (n, 0))
    return pl.pallas_call(
        body, name=name,
        out_shape=(jax.ShapeDtypeStruct((t, WIDTH), F32),
                   jax.ShapeDtypeStruct((nc, HEADS, HEAD_DIM, HEAD_DIM), F32)),
        grid=(nc,),
        in_specs=[row, row, row, row, pl.BlockSpec((HEADS, cc, cc), lambda n: (0, n, 0)),
                  pl.BlockSpec((cc, 128), lambda n: (n, 0))],
        out_specs=(row, pl.BlockSpec((1, HEADS, HEAD_DIM, HEAD_DIM), lambda n: (n, 0, 0, 0))),
        scratch_shapes=[pltpu.VMEM((HEADS, HEAD_DIM, HEAD_DIM), F32)],
        compiler_params=_params(("arbitrary",)),
    )(u, w, kd, qg, p, gates)


def _dn_scan_bwd(do, w, kd, qg, p, gates, name):
    t = do.shape[0]
    cc = DN_CHUNK
    nc = t // cc

    def body(do_ref, w_ref, kd_ref, qg_ref, p_ref, g_ref, dvn_ref, dsh_ref, ds_ref):
        n = pl.program_id(0)

        @pl.when(n == 0)
        def _():
            ds_ref[...] = jnp.zeros_like(ds_ref)

        glast = jnp.exp(g_ref[cc - 1:cc, :])
        hs = range(HEADS)
        sl = [slice(h * HEAD_DIM, (h + 1) * HEAD_DIM) for h in hs]
        ds = [ds_ref[h] for h in hs]
        dob = [do_ref[:, sl[h]].astype(BF16) for h in hs]
        dvn = [_dot(p_ref[h].astype(BF16), dob[h], TN) + _dot(kd_ref[:, sl[h]].astype(BF16), ds[h].astype(BF16), NN)
               for h in hs]
        ds_q = [_dot(qg_ref[:, sl[h]].astype(BF16), dob[h], TN) for h in hs]
        ds_w = [_dot(w_ref[:, sl[h]].astype(BF16), dvn[h].astype(BF16), TN) for h in hs]
        for h in hs:
            dvn_ref[:, sl[h]] = dvn[h]
            dsh_ref[0, h] = ds[h]
            ds_ref[h] = ds_q[h] + glast[:, h:h + 1] * ds[h] - ds_w[h]

    row = pl.BlockSpec((cc, WIDTH), lambda n: (nc - 1 - n, 0))
    return pl.pallas_call(
        body, name=name,
        out_shape=(jax.ShapeDtypeStruct((t, WIDTH), F32),
                   jax.ShapeDtypeStruct((nc, HEADS, HEAD_DIM, HEAD_DIM), F32)),
        grid=(nc,),
        in_specs=[row, row, row, row, pl.BlockSpec((HEADS, cc, cc), lambda n: (0, nc - 1 - n, 0)),
                  pl.BlockSpec((cc, 128), lambda n: (nc - 1 - n, 0))],
        out_specs=(row, pl.BlockSpec((1, HEADS, HEAD_DIM, HEAD_DIM), lambda n: (nc - 1 - n, 0, 0, 0))),
        scratch_shapes=[pltpu.VMEM((HEADS, HEAD_DIM, HEAD_DIM), F32)],
        compiler_params=_params(("arbitrary",)),
    )(do, w, kd, qg, p, gates)


def _dn_local_bwd(act, gates, u, w, kd, qg, tinv, p, sh, dsh, dvn, do, name):
    t = act.shape[0]
    cc = DN_CHUNK
    nc = t // cc

    def body(q_ref, k_ref, v_ref, g_ref, u_ref, w_ref, kd_ref, qg_ref, ti_ref, p_ref, s_ref, ds_ref,
             dvn_ref, do_ref, dq_ref, dk_ref, dv_ref, dg_ref):
        gates_v = g_ref[...]
        lower, strict = _tri(cc, "lower"), _tri(cc, "strict")
        ones = jnp.ones((cc, 128), BF16)
        rowc = _iota((cc, 1), 0)
        lane = _iota((cc, 128), 1)
        hs = range(HEADS)
        sl = [slice(h * HEAD_DIM, (h + 1) * HEAD_DIM) for h in hs]
        q, k, v, uu, ww, kd, qg, dvn, do = ([r[:, s] for s in sl] for r in (
            q_ref, k_ref, v_ref, u_ref, w_ref, kd_ref, qg_ref, dvn_ref, do_ref))
        gc_col, gc_last, beta, decay = zip(*[_dn_chunk_common(gates_v, h) for h in hs])
        gam = [jnp.exp(g) for g in gc_col]
        kb = [k[h] * beta[h] for h in hs]
        s_in = [s_ref[0, h] for h in hs]
        ds_out = [ds_ref[0, h] for h in hs]
        tinv = [ti_ref[h] for h in hs]

        a = [jnp.where(strict, _dotb(kb[h], k[h], NT) * decay[h], 0.0) for h in hs]
        vn = [uu[h] - _dotb(ww[h], s_in[h], NN) for h in hs]
        dqg = [_dotb(do[h], s_in[h], NT) for h in hs]
        dw = [-_dotb(dvn[h], s_in[h], NT) for h in hs]
        dp = [jnp.where(lower, _dotb(do[h], vn[h], NT), 0.0) for h in hs]
        dkd = [_dotb(vn[h], ds_out[h], NT) for h in hs]
        dru = [_dot3(tinv[h], dvn[h], TN) for h in hs]
        drw = [_dot3(tinv[h], dw[h], TN) for h in hs]
        da = [-jnp.where(strict, _dotb(dru[h], uu[h], NT) + _dotb(drw[h], ww[h], NT), 0.0) for h in hs]
        dad = [da[h] * decay[h] for h in hs]
        dpd = [dp[h] * decay[h] for h in hs]
        dkb = [_dotb(dad[h], k[h], NN) + gam[h] * drw[h] for h in hs]
        dk = [_dotb(dad[h], kb[h], TN) + _dotb(dpd[h], q[h], TN) + beta[h] * dkb[h]
              + jnp.exp(gc_last[h] - gc_col[h]) * dkd[h] for h in hs]
        dq = [gam[h] * dqg[h] + _dotb(dpd[h], k[h], NN) for h in hs]
        gm = [da[h] * a[h] + dp[h] * p_ref[h] for h in hs]
        colsum = [_dot_xr(gm[h], ones, TN)[:, 0:1] for h in hs]

        dgates = jnp.zeros((cc, 128), F32)
        for h in hs:
            dk_ref[:, sl[h]] = dk[h]
            dq_ref[:, sl[h]] = dq[h]
            dv_ref[:, sl[h]] = beta[h] * dru[h]
            dbeta = (jnp.sum(dkb[h] * k[h], axis=1, keepdims=True)
                     + jnp.sum(dru[h] * v[h], axis=1, keepdims=True))
            rkd = jnp.sum(dkd[h] * kd[h], axis=1, keepdims=True)
            dgc = (jnp.sum(gm[h], axis=1, keepdims=True) - colsum[h]
                   + jnp.sum(dqg[h] * qg[h], axis=1, keepdims=True)
                   + jnp.sum(drw[h] * kb[h], axis=1, keepdims=True) * gam[h] - rkd)
            tail = jnp.sum(rkd, axis=0, keepdims=True) + jnp.exp(gc_last[h]) * jnp.sum(
                jnp.sum(s_in[h] * ds_out[h], axis=1, keepdims=True), axis=0, keepdims=True)
            dgc = dgc + jnp.where(rowc == cc - 1, tail, 0.0)
            dgates = dgates + jnp.where(lane == h, dgc, 0.0) + jnp.where(lane == h + HEADS, dbeta, 0.0)
        dg_ref[...] = dgates

    row = lambda off: pl.BlockSpec((cc, WIDTH), lambda n: (n, off))
    mat = pl.BlockSpec((HEADS, cc, cc), lambda n: (0, n, 0))
    st = pl.BlockSpec((1, HEADS, HEAD_DIM, HEAD_DIM), lambda n: (n, 0, 0, 0))
    gl = pl.BlockSpec((cc, 128), lambda n: (n, 0))
    tw = jax.ShapeDtypeStruct((t, WIDTH), F32)
    return pl.pallas_call(
        body, name=name,
        out_shape=(tw, tw, tw, jax.ShapeDtypeStruct((t, 128), F32)),
        grid=(nc,),
        in_specs=[row(0), row(1), row(2), gl, row(0), row(0), row(0), row(0), mat, mat, st, st, row(0), row(0)],
        out_specs=(row(0), row(0), row(0), gl),
        compiler_params=_params(("parallel",)),
    )(act, act, act, gates, u, w, kd, qg, tinv, p, sh, dsh, dvn, do)


def _dn_post_fwd(o, gate, w, name):
    t = o.shape[0]
    tr = _tile(t, 512)

    def body(o_ref, g_ref, w_ref, y_ref):
        for h in range(HEADS):
            sl = slice(h * HEAD_DIM, (h + 1) * HEAD_DIM)
            ov, gv = o_ref[:, sl], g_ref[:, sl].astype(F32)
            r = lax.rsqrt(jnp.mean(ov * ov, axis=1, keepdims=True) + EPS)
            y_ref[:, sl] = (ov * r * w_ref[...] * (gv * _sigmoid(gv))).astype(BF16)

    blk = pl.BlockSpec((tr, WIDTH), lambda i: (i, 0))
    return pl.pallas_call(
        body, name=name,
        out_shape=jax.ShapeDtypeStruct((t, WIDTH), BF16),
        grid=(t // tr,),
        in_specs=[blk, blk, pl.BlockSpec((1, HEAD_DIM), lambda i: (0, 0))],
        out_specs=blk,
        compiler_params=_params(("parallel",)),
    )(o, gate, w)


def _dn_post_bwd(dy, o, gate, w, name):
    t = o.shape[0]
    tr = _tile(t, 512)

    def body(dy_ref, o_ref, g_ref, w_ref, do_ref, dg_ref, dw_ref):
        i = pl.program_id(0)

        @pl.when(i == 0)
        def _():
            dw_ref[...] = jnp.zeros_like(dw_ref)

        dw = jnp.zeros((1, HEAD_DIM), F32)
        for h in range(HEADS):
            sl = slice(h * HEAD_DIM, (h + 1) * HEAD_DIM)
            ov, gv, dyv = o_ref[:, sl], g_ref[:, sl].astype(F32), dy_ref[:, sl].astype(F32)
            r = lax.rsqrt(jnp.mean(ov * ov, axis=1, keepdims=True) + EPS)
            oh = ov * r
            sg = _sigmoid(gv)
            dg_ref[:, sl] = (dyv * oh * w_ref[...] * (sg * (1.0 + gv * (1.0 - sg)))).astype(BF16)
            dn = dyv * (gv * sg)
            doh = dn * w_ref[...]
            do_ref[:, sl] = r * (doh - oh * jnp.mean(doh * oh, axis=1, keepdims=True))
            dw = dw + jnp.sum(dn * oh, axis=0, keepdims=True)
        dw_ref[...] += dw

    blk = pl.BlockSpec((tr, WIDTH), lambda i: (i, 0))
    return pl.pallas_call(
        body, name=name,
        out_shape=(jax.ShapeDtypeStruct((t, WIDTH), F32), jax.ShapeDtypeStruct((t, WIDTH), BF16),
                   jax.ShapeDtypeStruct((1, HEAD_DIM), F32)),
        grid=(t // tr,),
        in_specs=[blk, blk, blk, pl.BlockSpec((1, HEAD_DIM), lambda i: (0, 0))],
        out_specs=(blk, blk, pl.BlockSpec((1, HEAD_DIM), lambda i: (0, 0))),
        compiler_params=_params(("arbitrary",)),
    )(dy, o, gate, w)


def _sb_scores(qs, k_ref, qi, it, carries, uincl):
    bk = ATT_BLOCK
    scale = HEAD_DIM ** -0.5
    heads, groups = range(len(qs)), range(SB_GROUP)
    lane = [slice(e * HEAD_DIM, (e + 1) * HEAD_DIM) for e in heads]
    js = [qi - SB_GROUP * it - g for g in groups]
    rows = [pl.ds(pl.multiple_of(jnp.maximum(j, 0) * bk, bk), bk) for j in js]
    qpos = qi * bk + _iota((bk, bk), 0)
    col = _iota((bk, bk), 1)
    mask1 = [jnp.logical_and(j * bk + col < qpos, j >= 0) for j in js]
    ks = [[k_ref[r, lane[e]] for r in rows] for e in heads]
    z = [[_dot(qs[e], k, NT) * scale for k in ks[e]] for e in heads]
    soft = [[_log1pexp_neg_abs(a) for a in ze] for ze in z]
    lk_full = [[-(jnp.maximum(a, 0.0) + s) for a, s in zip(z[e], soft[e])] for e in heads]
    lk = [[jnp.where(m, a, 0.0) for m, a in zip(mask1, lk_full[e])] for e in heads]
    ls = [[jnp.minimum(a, 0.0) - s for a, s in zip(z[e], soft[e])] for e in heads]
    incl = [[_dot_xr2(a, uincl, NN) for a in lk[e]] for e in heads]
    weights, out_carries = [], []
    for e in heads:
        cb, we = carries[e], []
        for g in groups:
            we.append(jnp.where(mask1[g], jnp.exp(ls[e][g] + (cb + incl[e][g] - lk[e][g])), 0.0))
            cb = cb + incl[e][g][:, 0:1]
        weights.append(we)
        out_carries.append(cb)
    return rows, ks, weights, mask1, lk_full, ls, out_carries


def _sb_more(qi, carry):
    it, cbs = carry[0], carry[1]
    live = jnp.max(cbs[0])
    for cb in cbs[1:]:
        live = jnp.maximum(live, jnp.max(cb))
    return jnp.logical_and(SB_GROUP * it <= qi, live > SB_LOG_ZERO)


def _sb_steps(groups, nq):
    def when():
        h, i = pl.program_id(0), pl.program_id(1)
        return (jnp.logical_and(h == 0, i == 0), jnp.logical_and(h == groups // 2, i == 0),
                jnp.logical_and(h == groups - 1, i == nq - 1))
    return when


def _sb_fwd(qkv, name, comm=None):
    t = qkv.shape[0]
    bk = ATT_BLOCK
    hp, wide = SB_HEADS_FWD, SB_HEADS_FWD * HEAD_DIM
    lane = [slice(e * HEAD_DIM, (e + 1) * HEAD_DIM) for e in range(hp)]

    def body(q_ref, k_ref, v_ref, o_ref):
        qi = pl.program_id(1)
        qs = [q_ref[:, s] for s in lane]
        uincl = jnp.where(_tri(bk, "lower"), 1.0, 0.0).astype(BF16)

        def step(carry):
            it, cbs, accs = carry
            rows, _, weights, _, _, _, cbs = _sb_scores(qs, k_ref, qi, it, cbs, uincl)
            accs = list(accs)
            for e in range(hp):
                for r, a in zip(rows, weights[e]):
                    accs[e] = accs[e] + _dot(a.astype(BF16), v_ref[r, lane[e]], NN)
            return it + 1, tuple(cbs), tuple(accs)

        init = (jnp.int32(0), (jnp.zeros((bk, 1), F32),) * hp, (jnp.zeros((bk, HEAD_DIM), F32),) * hp)
        _, _, accs = lax.while_loop(functools.partial(_sb_more, qi), step, init)
        for e in range(hp):
            o_ref[:, lane[e]] = accs[e]

    groups = HEADS // hp
    (o,), extra = _host_call(
        body, name, comm, _sb_steps(groups, t // bk), [jax.ShapeDtypeStruct((t, WIDTH), F32)], (groups, t // bk),
        [pl.BlockSpec((bk, wide), lambda h, i: (i, h)),
         pl.BlockSpec((t, wide), lambda h, i: (0, groups + h)),
         pl.BlockSpec((t, wide), lambda h, i: (0, 2 * groups + h))],
        [pl.BlockSpec((bk, wide), lambda h, i: (i, h))], [], ("parallel", "arbitrary"), (qkv, qkv, qkv))
    return o, extra


def _sb_bwd(qkv, o, do, name, comm=None):
    assert do.dtype == BF16
    t = qkv.shape[0]
    bk = ATT_BLOCK
    scale = HEAD_DIM ** -0.5
    hp, wide = SB_HEADS_BWD, SB_HEADS_BWD * HEAD_DIM
    lane = [slice(e * HEAD_DIM, (e + 1) * HEAD_DIM) for e in range(hp)]

    def body(q_ref, k_ref, v_ref, o_ref, do_ref, dq_ref, dk_ref, dv_ref):
        qi = pl.program_id(1)

        @pl.when(qi == 0)
        def _():
            dk_ref[...] = jnp.zeros_like(dk_ref)
            dv_ref[...] = jnp.zeros_like(dv_ref)

        heads, groups = range(hp), range(SB_GROUP)
        qs = [q_ref[:, s] for s in lane]
        dob = [do_ref[:, s] for s in lane]
        dsum = [jnp.sum(dob[e].astype(F32) * o_ref[:, lane[e]], axis=1, keepdims=True) for e in heads]
        uincl = jnp.where(_tri(bk, "lower"), 1.0, 0.0).astype(BF16)

        def step(carry):
            it, cbs, ces, dqs = carry
            rows, ks, weights, mask, lk_full, ls, cbs = _sb_scores(qs, k_ref, qi, it, cbs, uincl)
            ab = [[a.astype(BF16) for a in weights[e]] for e in heads]
            vs = [[v_ref[r, lane[e]] for r in rows] for e in heads]
            dla = [[ab[e][g].astype(F32) * _dot(dob[e], vs[e][g], NT) for g in groups] for e in heads]
            suf = [[_dot_xr2(a, uincl, NN) for a in dla[e]] for e in heads]
            ces, dqs = list(ces), list(dqs)
            for e in heads:
                for g in groups:
                    err = dsum[e] - (ces[e] + suf[e][g])
                    ces[e] = ces[e] + suf[e][g][:, 0:1]
                    dz = jnp.where(mask[g], dla[e][g] * jnp.exp(lk_full[e][g]) - err * jnp.exp(ls[e][g]), 0.0)
                    dzb = (dz * scale).astype(BF16)
                    dqs[e] = dqs[e] + _dot(dzb, ks[e][g], NN)
                    dk_ref[rows[g], lane[e]] += _dot(dzb, qs[e], TN)
                    dv_ref[rows[g], lane[e]] += _dot(ab[e][g], dob[e], TN)
            return it + 1, tuple(cbs), tuple(ces), tuple(dqs)

        zc = (jnp.zeros((bk, 1), F32),) * hp
        init = (jnp.int32(0), zc, zc, (jnp.zeros((bk, HEAD_DIM), F32),) * hp)
        dqs = lax.while_loop(functools.partial(_sb_more, qi), step, init)[3]
        for e in heads:
            dq_ref[:, lane[e]] = dqs[e]

    ngroup = HEADS // hp
    tw = jax.ShapeDtypeStruct((t, WIDTH), F32)
    qb = pl.BlockSpec((bk, wide), lambda h, i: (i, h))
    full = lambda off: pl.BlockSpec((t, wide), lambda h, i: (0, off + h))
    return _host_call(
        body, name, comm, _sb_steps(ngroup, t // bk), [tw, tw, tw], (ngroup, t // bk),
        [qb, full(ngroup), full(2 * ngroup), qb, qb], [qb, full(0), full(0)], [], ("parallel", "arbitrary"),
        (qkv, qkv, qkv, o, do))


def _merge_fwd(pd, ps, gl, name):
    t = pd.shape[0]
    tr, tc = _tile(t, 512), 512
    nj = D_MODEL // tc

    def body(pd_ref, ps_ref, gd_ref, gs_ref, o_ref):
        gd, gs = gd_ref[...].astype(F32), gs_ref[...].astype(F32)
        o_ref[...] = (_sigmoid(gd) * pd_ref[...].astype(F32) + _sigmoid(gs) * ps_ref[...].astype(F32)).astype(BF16)

    blk = lambda off: pl.BlockSpec((tr, tc), lambda i, j: (i, j + off))
    return pl.pallas_call(
        body, name=name,
        out_shape=jax.ShapeDtypeStruct((t, D_MODEL), BF16),
        grid=(t // tr, nj),
        in_specs=[blk(0), blk(0), blk(0), blk(nj)],
        out_specs=blk(0),
        compiler_params=_params(("parallel", "parallel")),
    )(pd, ps, gl, gl)


def _merge_bwd(dm, pd, ps, gl, name):
    t = pd.shape[0]
    tr, tc = _tile(t, 512), 512
    nj = D_MODEL // tc

    def body(dm_ref, pd_ref, ps_ref, gd_ref, gs_ref, dpd_ref, dps_ref, dgd_ref, dgs_ref):
        dmv = dm_ref[...].astype(F32)
        sd, ss = _sigmoid(gd_ref[...].astype(F32)), _sigmoid(gs_ref[...].astype(F32))
        dpd_ref[...] = (dmv * sd).astype(BF16)
        dps_ref[...] = (dmv * ss).astype(BF16)
        dgd_ref[...] = (dmv * pd_ref[...].astype(F32) * sd * (1.0 - sd)).astype(BF16)
        dgs_ref[...] = (dmv * ps_ref[...].astype(F32) * ss * (1.0 - ss)).astype(BF16)

    blk = lambda off: pl.BlockSpec((tr, tc), lambda i, j: (i, j + off))
    out = jax.ShapeDtypeStruct((t, D_MODEL), BF16)
    return pl.pallas_call(
        body, name=name,
        out_shape=(out, out, out, out),
        grid=(t // tr, nj),
        in_specs=[blk(0), blk(0), blk(0), blk(0), blk(nj)],
        out_specs=(blk(0), blk(0), blk(0), blk(0)),
        compiler_params=_params(("parallel", "parallel")),
    )(dm, pd, ps, gl, gl)


def _local_step(x, target, wts, plan=None):
    n1 = _rmsnorm_fwd(x, wts["norm1_w"], "norm1_fwd")
    qkv_pre = _matmul(n1, wts["w_dnqkv_t"], "nt", BF16, "in_dnqkv")
    hgate = _matmul(n1, wts["w_dngate_t"], "nt", BF16, "in_dngate")
    sbqkv = _matmul(n1, wts["w_sbqkv_t"], "nt", BF16, "in_sbqkv")
    gl = _matmul(n1, wts["w_gl_t"], "nt", BF16, "in_gl")
    hab = _matmul(n1, wts["w_ab_t"], "nt", F32, "in_ab")

    act = _dn_pre_fwd(qkv_pre, wts["dn_conv_w"], "dn_pre_fwd")
    gates = _dn_gates_fwd(hab, wts["alog"], wts["dtb"], "dn_gates_fwd")
    u, w, kd, qg, tinv, p = _dn_local_fwd(act, gates, "dn_local_fwd")
    o_dn, sh = _dn_scan_fwd(u, w, kd, qg, p, gates, "dn_scan_fwd")
    y_dn = _dn_post_fwd(o_dn, hgate, wts["dn_norm_w"], "dn_post_fwd")

    o_sb, late = _sb_fwd(sbqkv, "sb_fwd", comm=plan.late_gather() if plan else None)
    if plan:
        wts = {**wts, **plan.late_weights(late)}

    pd = _matmul(y_dn, wts["w_proj_dn"], "nn", BF16, "proj_dn")
    ps = _matmul(o_sb, wts["w_proj_sb"], "nn", BF16, "proj_sb")
    mixed = _merge_fwd(pd, ps, gl, "merge_fwd")
    x1 = _matmul(mixed, wts["w_out"], "nn", F32, "out_proj", add=x)

    n2 = _rmsnorm_fwd(x1, wts["norm2_w"], "norm2_fwd")
    upre = _matmul(n2, wts["ffn_w_up_t"], "nt", BF16, "ffn_up")
    fact = _ffn_act_fwd(upre, wts["ffn_conv_w"], "ffn_act_fwd")
    x2 = _matmul(fact, wts["ffn_w_down"], "nn", F32, "ffn_down", add=x1)

    dx2, g_normf, loss = _final_loss(x2, target, wts["norm_f_w"], "final_loss")

    dfact = _matmul(dx2, wts["ffn_w_down"], "nt", BF16, "ffn_down_dx")
    g_wdown = _matmul(fact, dx2, "tn", BF16, "ffn_down_dw")
    dgc, duc, dwg, dwu = _ffn_act_bwd(dfact, upre, wts["ffn_conv_w"], "ffn_act_bwd")
    g_fconv = jnp.concatenate([dwg, dwu], axis=1)
    dupre = _conv_bwd_data([dgc, duc], wts["ffn_conv_w"], FFN_CONV, BF16, "ffn_conv_bwd")
    dn2 = _matmul(dupre, wts["ffn_w_up_t"], "nn", F32, "ffn_up_dx")
    g_wup = _matmul(dupre, n2, "tn", BF16, "ffn_up_dw")
    dx1, g_norm2 = _rmsnorm_bwd(dn2, x1, wts["norm2_w"], dx2, "norm2_bwd")

    dmixed = _matmul(dx1, wts["w_out"], "nt", BF16, "out_proj_dx")
    g_wout = _matmul(mixed, dx1, "tn", BF16, "out_proj_dw")
    dpd, dps, dgd, dgs = _merge_bwd(dmixed, pd, ps, gl, "merge_bwd")
    dy_dn = _matmul(dpd, wts["w_proj_dn"], "nt", BF16, "proj_dn_dx")
    g_wpd = _matmul(y_dn, dpd, "tn", BF16, "proj_dn_dw")
    do_sb = _matmul(dps, wts["w_proj_sb"], "nt", BF16, "proj_sb_dx")
    g_wps = _matmul(o_sb, dps, "tn", BF16, "proj_sb_dw")
    grads = dict(w_proj_dn=g_wpd, w_proj_sb=g_wps, w_out=g_wout, ffn_w_up_t=g_wup, ffn_w_down=g_wdown)

    (dsq, dsk, dsv), got_early = _sb_bwd(sbqkv, o_sb, do_sb, "sb_bwd",
                                         comm=plan.early_grads(grads) if plan else None)

    do_dn, dhgate, g_dnnorm = _dn_post_bwd(dy_dn, o_dn, hgate, wts["dn_norm_w"], "dn_post_bwd")
    dvn, dsh = _dn_scan_bwd(do_dn, w, kd, qg, p, gates, "dn_scan_bwd")
    dq, dk, dv, dgates = _dn_local_bwd(act, gates, u, w, kd, qg, tinv, p, sh, dsh, dvn, do_dn, "dn_local_bwd")
    dhab, g_alog, g_dtb = _dn_gates_bwd(dgates, hab, wts["alog"], wts["dtb"], "dn_gates_bwd")
    dcv, g_dnconv = _dn_pre_bwd(dq, dk, dv, qkv_pre, wts["dn_conv_w"], "dn_pre_bwd")
    dqkv_pre = _conv_bwd_data([dcv], wts["dn_conv_w"], DN_CONV, BF16, "dn_conv_bwd")

    dh = jnp.concatenate([dqkv_pre, dhgate, dsq.astype(BF16), dsk.astype(BF16), dsv.astype(BF16), dgd, dgs], axis=1)
    w_main_t = jnp.concatenate([wts["w_dnqkv_t"], wts["w_dngate_t"], wts["w_sbqkv_t"], wts["w_gl_t"]], axis=0)
    g_wmain = _matmul(dh, n1, "tn", BF16, "in_dw_main")
    g_wab = _matmul(dhab, n1, "tn", BF16, "in_dw_ab")
    grads.update(w_main_t=g_wmain, w_ab_t=g_wab, dn_conv_w=g_dnconv, alog=g_alog, dtb=g_dtb, dn_norm_w=g_dnnorm,
                 norm2_w=g_norm2, ffn_conv_w=g_fconv, norm_f_w=g_normf)
    got_late = []
    if plan:
        dn1, got_late = _matmul(dh, w_main_t, "nn", F32, "in_dx_main", comm=plan.late_grads(grads, loss))
    else:
        dn1 = _matmul(dh, w_main_t, "nn", F32, "in_dx_main")
    dn1 = _matmul(dhab, wts["w_ab_t"], "nn", F32, "in_dx_ab", add=dn1)
    grad_x, g_norm1 = _rmsnorm_bwd(dn1, x, wts["norm1_w"], dx1, "norm1_bwd")
    grads["norm1_w"] = g_norm1
    return loss, grad_x, grads, got_early, got_late


HBM_SPEC = pl.BlockSpec(memory_space=pltpu.HBM)


def _mesh_pos():
    x, y, c = lax.axis_index("x"), lax.axis_index("y"), lax.axis_index("c")
    return x, y, c, 4 * x + 2 * y + c


def _peer(k):
    x, y, c, _ = _mesh_pos()
    px = 1 - x if k & 4 else x
    py = 1 - y if k & 2 else y
    pc = 1 - c if k & 1 else c
    return (px, py, pc), 4 * px + 2 * py + pc


def _rcopy(src, dst, send, recv, a, s, peer):
    return pltpu.make_async_remote_copy(src_ref=src, dst_ref=dst, send_sem=send.at[a, s], recv_sem=recv.at[a, s],
                                        device_id=peer, device_id_type=pl.DeviceIdType.MESH)


class _Gather:
    ICI = (2, 4, 6)

    def __init__(self, shards):
        self.args = list(shards)
        self.n = len(shards)
        self.out_shape = [jax.ShapeDtypeStruct((N_DEV,) + s.shape, s.dtype) for s in shards]
        self.scratch = [pltpu.SemaphoreType.DMA((self.n, N_DEV - 1)), pltpu.SemaphoreType.DMA((self.n, N_DEV - 1)),
                        pltpu.SemaphoreType.DMA((self.n,))]

    def _first(self, ins, outs, send, recv, a):
        me = _mesh_pos()[3]
        out, got = [], []
        for s, k in enumerate((1,) + self.ICI):
            peer, pidx = _peer(k)
            out.append(_rcopy(ins[a], outs[a].at[me], send, recv, a, s, peer))
            got.append(_rcopy(ins[a], outs[a].at[pidx], send, recv, a, s, peer))
        return out, got

    def _forward(self, ins, outs, send, recv, a):
        sib = _peer(1)[0]
        out, got = [], []
        for s, k in enumerate(self.ICI):
            held = outs[a].at[_peer(k)[1]]
            out.append(_rcopy(held, held, send, recv, a, 4 + s, sib))
            other = outs[a].at[_peer(k | 1)[1]]
            got.append(_rcopy(other, other, send, recv, a, 4 + s, sib))
        return out, got

    def start(self, ins, outs, sems):
        send, recv, loc = sems
        me = _mesh_pos()[3]
        for a in range(self.n):
            pltpu.make_async_copy(ins[a], outs[a].at[me], loc.at[a]).start()
            for cp in self._first(ins, outs, send, recv, a)[0]:
                cp.start()

    def mid(self, ins, outs, sems):
        send, recv, _ = sems
        for a in range(self.n):
            arrivals = self._first(ins, outs, send, recv, a)[1]
            for s, cp in enumerate(self._forward(ins, outs, send, recv, a)[0]):
                arrivals[1 + s].wait_recv()
                cp.start()

    def finish(self, ins, outs, sems):
        send, recv, loc = sems
        me = _mesh_pos()[3]
        for a in range(self.n):
            first_out, first_got = self._first(ins, outs, send, recv, a)
            fwd_out, fwd_got = self._forward(ins, outs, send, recv, a)
            first_got[0].wait_recv()
            for cp in fwd_got:
                cp.wait_recv()
            for cp in first_out + fwd_out:
                cp.wait_send()
            pltpu.make_async_copy(ins[a], outs[a].at[me], loc.at[a]).wait()


class _Exchange:
    def __init__(self, slabs, gathered=()):
        self.args = list(slabs) + list(gathered)
        self.n_slab = len(slabs)
        self.n = len(self.args)
        self.out_shape = ([jax.ShapeDtypeStruct(s.shape, s.dtype) for s in slabs]
                          + [jax.ShapeDtypeStruct((N_DEV,) + s.shape, s.dtype) for s in gathered])
        self.scratch = [pltpu.SemaphoreType.DMA((self.n, N_DEV - 1)), pltpu.SemaphoreType.DMA((self.n, N_DEV - 1)),
                        pltpu.SemaphoreType.DMA((self.n,))]

    def _copies(self, ins, outs, send, recv, a):
        me = _mesh_pos()[3]
        out, got = [], []
        for k in range(1, N_DEV):
            peer, pidx = _peer(k)
            src = ins[a].at[pidx] if a < self.n_slab else ins[a]
            out.append(_rcopy(src, outs[a].at[me], send, recv, a, k - 1, peer))
            got.append(_rcopy(src, outs[a].at[pidx], send, recv, a, k - 1, peer))
        return out, got

    def _local(self, ins, outs, loc, a):
        me = _mesh_pos()[3]
        return pltpu.make_async_copy(ins[a].at[me] if a < self.n_slab else ins[a], outs[a].at[me], loc.at[a])

    def start(self, ins, outs, sems):
        send, recv, loc = sems
        for a in range(self.n):
            self._local(ins, outs, loc, a).start()
            for cp in self._copies(ins, outs, send, recv, a)[0]:
                cp.start()

    def mid(self, ins, outs, sems):
        pass

    def finish(self, ins, outs, sems):
        send, recv, loc = sems
        for a in range(self.n):
            out, got = self._copies(ins, outs, send, recv, a)
            for cp in got:
                cp.wait_recv()
            for cp in out:
                cp.wait_send()
            self._local(ins, outs, loc, a).wait()


def _comm_call(comm, name):
    n = comm.n

    def body(*refs):
        ins, outs, sems = refs[:n], refs[n:2 * n], refs[2 * n:]
        comm.start(ins, outs, sems)
        comm.mid(ins, outs, sems)
        comm.finish(ins, outs, sems)

    return pl.pallas_call(
        body, name=name, out_shape=comm.out_shape, in_specs=[HBM_SPEC] * n, out_specs=[HBM_SPEC] * n,
        scratch_shapes=comm.scratch,
    )(*comm.args)


def _hosted(body, comm, n_in, n_out, when):
    if comm is None:
        return body

    def wrapped(*refs):
        ins, c_ins = refs[:n_in], refs[n_in:n_in + comm.n]
        o0 = n_in + comm.n
        outs, c_outs = refs[o0:o0 + n_out], refs[o0 + n_out:o0 + n_out + comm.n]
        scratch, sems = refs[o0 + n_out + comm.n:len(refs) - 3], refs[len(refs) - 3:]
        first, middle, last = when()

        @pl.when(first)
        def _():
            comm.start(c_ins, c_outs, sems)

        body(*ins, *outs, *scratch)

        @pl.when(middle)
        def _():
            comm.mid(c_ins, c_outs, sems)

        @pl.when(last)
        def _():
            comm.finish(c_ins, c_outs, sems)

    return wrapped


def _host_call(body, name, comm, when, out_shape, grid, in_specs, out_specs, scratch_shapes, sem, args):
    n_in, n_out = len(in_specs), len(out_specs)
    if comm is None:
        res = pl.pallas_call(body, name=name, out_shape=out_shape, grid=grid, in_specs=in_specs, out_specs=out_specs,
                             scratch_shapes=scratch_shapes, compiler_params=_params(sem))(*args)
        return list(res), []
    res = pl.pallas_call(
        _hosted(body, comm, n_in, n_out, when), name=name,
        out_shape=list(out_shape) + comm.out_shape, grid=grid,
        in_specs=list(in_specs) + [HBM_SPEC] * comm.n, out_specs=list(out_specs) + [HBM_SPEC] * comm.n,
        scratch_shapes=list(scratch_shapes) + comm.scratch,
        compiler_params=_params(("arbitrary",) * len(grid)),
    )(*args, *comm.args)
    return list(res[:n_out]), list(res[n_out:])


def _adamw(parts, w, m, v, name):
    rows, cols = w.shape
    tr, tc = rows, cols
    for cand in (128, 176):
        if rows > cand and rows % cand == 0:
            tr = cand
            break
    if tr == rows and rows > 512:
        tc = _tile(cols, 256)

    def body(p_ref, w_ref, m_ref, v_ref, g_ref, d_ref, mo_ref, vo_ref):
        g = p_ref[0].astype(F32)
        for s in range(1, N_DEV):
            g = g + p_ref[s].astype(F32)
        mn = ADAM_B1 * m_ref[...] + (1.0 - ADAM_B1) * g
        vn = ADAM_B2 * v_ref[...] + (1.0 - ADAM_B2) * (g * g)
        m_hat = mn / (1.0 - ADAM_B1 ** ADAM_STEP)
        v_hat = vn / (1.0 - ADAM_B2 ** ADAM_STEP)
        g_ref[...] = g
        d_ref[...] = -ADAM_LR * (m_hat / (jnp.sqrt(v_hat) + ADAM_EPS) + ADAM_WD * w_ref[...])
        mo_ref[...] = mn
        vo_ref[...] = vn

    blk = pl.BlockSpec((tr, tc), lambda i, j: (i, j))
    out = jax.ShapeDtypeStruct((rows, cols), F32)
    return pl.pallas_call(
        body, name=name,
        out_shape=(out, out, out, out),
        grid=(rows // tr, cols // tc),
        in_specs=[pl.BlockSpec((N_DEV, tr, tc), lambda i, j: (0, i, j)), blk, blk, blk],
        out_specs=(blk, blk, blk, blk),
        compiler_params=_params(("parallel", "parallel")),
    )(parts, w, m, v)


CONV_PACK = 8 * 1024
WEIGHT_ORDER = ("norm1_w", "w_in", "dn_conv_w", "dn_A_log", "dn_dt_bias", "dn_norm_w", "w_proj_dn", "w_proj_sb",
                "w_out", "norm2_w", "ffn_w_up", "ffn_conv_w", "ffn_w_down", "norm_f_w")


def _cols_to_slabs(g):
    r, c8 = g.shape
    return g.reshape(r, N_DEV, c8 // N_DEV).transpose(1, 0, 2)


def _slabs_to_cols(s):
    d, r, c = s.shape
    return s.transpose(1, 0, 2).reshape(r, d * c)


def kernel(x, norm1_w, w_in, dn_conv_w, dn_A_log, dn_dt_bias, dn_norm_w, w_proj_dn, w_proj_sb, w_out, norm2_w, ffn_w_up, ffn_conv_w, ffn_w_down, norm_f_w, loss_target, m_norm1_w, m_w_in, m_dn_conv_w, m_dn_A_log, m_dn_dt_bias, m_dn_norm_w, m_w_proj_dn, m_w_proj_sb, m_w_out, m_norm2_w, m_ffn_w_up, m_ffn_conv_w, m_ffn_w_down, m_norm_f_w, v_norm1_w, v_w_in, v_dn_conv_w, v_dn_A_log, v_dn_dt_bias, v_dn_norm_w, v_w_proj_dn, v_w_proj_sb, v_w_out, v_norm2_w, v_ffn_w_up, v_ffn_conv_w, v_ffn_w_down, v_norm_f_w):
    me = _mesh_pos()[3]
    tr = lambda a: jnp.transpose(a[0])
    w_loc = dict(norm1_w=norm1_w, w_in=tr(w_in), dn_conv_w=dn_conv_w[0], dn_A_log=dn_A_log, dn_dt_bias=dn_dt_bias,
                 dn_norm_w=dn_norm_w, w_proj_dn=w_proj_dn[0], w_proj_sb=w_proj_sb[0], w_out=w_out[0],
                 norm2_w=norm2_w, ffn_w_up=tr(ffn_w_up), ffn_conv_w=ffn_conv_w[0], ffn_w_down=ffn_w_down[0],
                 norm_f_w=norm_f_w[None, :])
    m_loc = dict(norm1_w=m_norm1_w, w_in=tr(m_w_in), dn_conv_w=m_dn_conv_w[0], dn_A_log=m_dn_A_log,
                 dn_dt_bias=m_dn_dt_bias, dn_norm_w=m_dn_norm_w, w_proj_dn=m_w_proj_dn[0], w_proj_sb=m_w_proj_sb[0],
                 w_out=m_w_out[0], norm2_w=m_norm2_w, ffn_w_up=tr(m_ffn_w_up), ffn_conv_w=m_ffn_conv_w[0],
                 ffn_w_down=m_ffn_w_down[0], norm_f_w=m_norm_f_w[None, :])
    v_loc = dict(norm1_w=v_norm1_w, w_in=tr(v_w_in), dn_conv_w=v_dn_conv_w[0], dn_A_log=v_dn_A_log,
                 dn_dt_bias=v_dn_dt_bias, dn_norm_w=v_dn_norm_w, w_proj_dn=v_w_proj_dn[0], w_proj_sb=v_w_proj_sb[0],
                 w_out=v_w_out[0], norm2_w=v_norm2_w, ffn_w_up=tr(v_ffn_w_up), ffn_conv_w=v_ffn_conv_w[0],
                 ffn_w_down=v_ffn_w_down[0], norm_f_w=v_norm_f_w[None, :])

    conv_flat = jnp.concatenate([w_loc["dn_conv_w"].reshape(-1), w_loc["ffn_conv_w"].reshape(-1)])
    n_dn, n_ffn = DN_CONV * 3 * WIDTH // N_DEV, FFN_CONV * 2 * D_FF // N_DEV
    conv_pack = jnp.pad(conv_flat, (0, CONV_PACK - n_dn - n_ffn)).reshape(8, 1024)
    g_in, g_conv = _comm_call(_Gather([w_loc["w_in"].astype(BF16), conv_pack]), "gather_first")
    in_width = g_in.shape[0] * g_in.shape[1]
    w_in_t = g_in.reshape(in_width, D_MODEL)
    g_conv = g_conv.reshape(N_DEV, CONV_PACK)
    dn_conv_full = _slabs_to_cols(g_conv[:, :n_dn].reshape(N_DEV, DN_CONV, 3 * WIDTH // N_DEV))
    ffn_conv_full = _slabs_to_cols(g_conv[:, n_dn:n_dn + n_ffn].reshape(N_DEV, FFN_CONV, 2 * D_FF // N_DEV))
    q_end = 3 * WIDTH
    ab_end = q_end + 2 * HEADS
    gate_end = ab_end + WIDTH
    sb_end = gate_end + 3 * WIDTH
    pad_lanes = lambda a: jnp.pad(a, ((0, 0), (0, 128 - a.shape[1])))
    wts = dict(
        norm1_w=norm1_w, w_dnqkv_t=w_in_t[:q_end], w_ab_t=jnp.pad(w_in_t[q_end:ab_end], ((0, 128 - 2 * HEADS), (0, 0))),
        w_dngate_t=w_in_t[ab_end:gate_end], w_sbqkv_t=w_in_t[gate_end:sb_end], w_gl_t=w_in_t[sb_end:],
        dn_conv_w=dn_conv_full, alog=pad_lanes(dn_A_log), dtb=pad_lanes(dn_dt_bias), dn_norm_w=dn_norm_w,
        norm2_w=norm2_w, ffn_conv_w=ffn_conv_full, norm_f_w=norm_f_w[None, :])

    n_fc = FFN_CONV * 2 * D_FF
    fc_rows = -(-n_fc // D_MODEL)
    dn_rows = DN_CONV * 3 * WIDTH // D_MODEL
    late_names = ("w_proj_dn", "w_proj_sb", "w_out", "ffn_w_up", "ffn_w_down")

    class Plan:
        @staticmethod
        def late_gather():
            return _Gather([w_loc[k].astype(BF16) for k in late_names])

        @staticmethod
        def late_weights(got):
            g_pd, g_ps, g_out, g_up, g_down = got
            return dict(w_proj_dn=g_pd.reshape(WIDTH, D_MODEL), w_proj_sb=g_ps.reshape(WIDTH, D_MODEL),
                        w_out=g_out.reshape(D_MODEL, D_MODEL), ffn_w_up_t=g_up.reshape(2 * D_FF, D_MODEL),
                        ffn_w_down=g_down.reshape(D_FF, D_MODEL))

        @staticmethod
        def early_grads(g):
            return _Exchange([g["w_proj_dn"].reshape(N_DEV, WIDTH // N_DEV, D_MODEL),
                              g["w_proj_sb"].reshape(N_DEV, WIDTH // N_DEV, D_MODEL),
                              g["w_out"].reshape(N_DEV, D_MODEL // N_DEV, D_MODEL),
                              g["ffn_w_up_t"].reshape(N_DEV, 2 * D_FF // N_DEV, D_MODEL),
                              g["ffn_w_down"].reshape(N_DEV, D_FF // N_DEV, D_MODEL)])

        @staticmethod
        def late_grads(g, loss):
            g_win_t = jnp.concatenate([g["w_main_t"][:q_end], g["w_ab_t"][:2 * HEADS], g["w_main_t"][q_end:]],
                                      axis=0)
            row3 = jnp.concatenate([g["dn_norm_w"], g["alog"], g["dtb"], jnp.pad(loss, ((0, 0), (0, 127))),
                                    jnp.zeros((1, D_MODEL - 512), F32)], axis=1)
            fconv_rows = jnp.pad(g["ffn_conv_w"].reshape(-1), (0, fc_rows * D_MODEL - n_fc)).reshape(fc_rows, D_MODEL)
            pad8 = lambda a: jnp.pad(a, ((0, -a.shape[0] % 8), (0, 0)))
            pieces = [g["norm2_w"], g["norm_f_w"], row3, g["dn_conv_w"].reshape(dn_rows, D_MODEL), fconv_rows]
            small = jnp.concatenate([pad8(a) for a in pieces], axis=0)
            assert small.shape[0] == SMALL_ROWS
            return _Exchange([g_win_t.reshape(N_DEV, in_width // N_DEV, D_MODEL)], [small])

    loss, grad_x, g, got_early, got_late = _local_step(x[0], loss_target[0], wts, Plan)
    r_pd, r_ps, r_out, r_up, r_down = got_early
    r_in, r_small = got_late
    (r_norm1,) = _comm_call(_Exchange([], [jnp.pad(g["norm1_w"], ((0, 7), (0, 0)))]), "gather_norm1")

    parts = dict(w_in=r_in, w_proj_dn=r_pd, w_proj_sb=r_ps, w_out=r_out, ffn_w_up=r_up, ffn_w_down=r_down)
    parts["norm1_w"] = r_norm1[:, 0:1, :]
    parts["norm2_w"] = r_small[:, 0:1, :]
    parts["norm_f_w"] = r_small[:, 8:9, :]
    parts["dn_norm_w"] = r_small[:, 16:17, 0:HEAD_DIM]
    parts["dn_A_log"] = r_small[:, 16:17, 128:128 + HEADS]
    parts["dn_dt_bias"] = r_small[:, 16:17, 256:256 + HEADS]
    dnc = r_small[:, 24:24 + dn_rows, :].reshape(N_DEV, DN_CONV, 3 * WIDTH)
    parts["dn_conv_w"] = lax.dynamic_slice_in_dim(dnc, me * (3 * WIDTH // N_DEV), 3 * WIDTH // N_DEV, axis=2)
    fc0 = 24 + dn_rows + (-dn_rows % 8)
    fcc = r_small[:, fc0:fc0 + fc_rows, :].reshape(N_DEV, fc_rows * D_MODEL)[:, :n_fc]
    fcc = fcc.reshape(N_DEV, FFN_CONV, 2 * D_FF)
    parts["ffn_conv_w"] = lax.dynamic_slice_in_dim(fcc, me * (2 * D_FF // N_DEV), 2 * D_FF // N_DEV, axis=2)
    loss_total = jnp.sum(r_small[:, 16, 384])

    res = {k: _adamw(parts[k], w_loc[k], m_loc[k], v_loc[k], "adamw_" + k) for k in WEIGHT_ORDER}
    lead = ("w_in", "dn_conv_w", "w_proj_dn", "w_proj_sb", "w_out", "ffn_w_up", "ffn_conv_w", "ffn_w_down")

    def shaped(k, a):
        if k in ("w_in", "ffn_w_up"):
            return jnp.transpose(a)[None]
        if k in lead:
            return a[None]
        if k == "norm_f_w":
            return a[0]
        return a

    outs = [loss_total, grad_x[None]]
    for idx in range(4):
        outs += [shaped(k, res[k][idx]) for k in WEIGHT_ORDER]
    return tuple(outs)
```

```python
import functools

import jax
import jax.numpy as jnp
from jax import lax
from jax.experimental import pallas as pl
from jax.experimental.pallas import tpu as pltpu

F32 = jnp.float32
BF16 = jnp.bfloat16

N_DEV = 8
D_MODEL = 1024
HEADS = 8
HEAD_DIM = 128
WIDTH = HEADS * HEAD_DIM
DN_CONV = 4
DN_CHUNK = 64
D_FF = 2816
FFN_CONV = 3
EPS = 1e-6
HALO = 16
CHUNK_ROWS = 256
ATT_BLOCK = 256
SB_LOG_ZERO = -104.0
SB_GROUP = 2
SB_HEADS_FWD = 4
SB_HEADS_BWD = 2
SMALL_ROWS = 64

ADAM_LR = 0.001
ADAM_B1 = 0.9
ADAM_B2 = 0.999
ADAM_EPS = 1e-08
ADAM_WD = 0.01
ADAM_STEP = 10

VMEM_LIMIT = 48 * 1024 * 1024


def _params(sem=None, **kw):
    return pltpu.CompilerParams(dimension_semantics=sem, vmem_limit_bytes=VMEM_LIMIT, **kw)


def _tile(n, cap):
    if n <= cap:
        return n
    best = None
    for t in range(128, cap + 1, 128):
        if n % t == 0:
            best = t
    assert best is not None, (n, cap)
    return best


def _dot(a, b, dims):
    return lax.dot_general(a, b, ((dims[0], dims[1]), ((), ())), preferred_element_type=F32)


NN = ((1,), (0,))
NT = ((1,), (1,))
TN = ((0,), (0,))


def _dotb(a, b, dims):
    return _dot(a.astype(BF16), b.astype(BF16), dims)


def _split3(x):
    h1 = x.astype(BF16)
    r1 = x - h1.astype(F32)
    h2 = r1.astype(BF16)
    r2 = r1 - h2.astype(F32)
    return h1, h2, r2.astype(BF16)


def _dot_xr(a, b_exact, dims):
    a1, a2, a3 = _split3(a)
    return _dot(a1, b_exact, dims) + _dot(a2, b_exact, dims) + _dot(a3, b_exact, dims)


def _split2(x):
    h1 = x.astype(BF16)
    return h1, (x - h1.astype(F32)).astype(BF16)


def _dot_xr2(a, b_exact, dims):
    a1, a2 = _split2(a)
    return _dot(a1, b_exact, dims) + _dot(a2, b_exact, dims)


def _dot_xl(a_exact, b, dims):
    b1, b2, b3 = _split3(b)
    return _dot(a_exact, b1, dims) + _dot(a_exact, b2, dims) + _dot(a_exact, b3, dims)


def _dot3(a, b, dims):
    a1 = a.astype(BF16)
    a2 = (a - a1.astype(F32)).astype(BF16)
    b1 = b.astype(BF16)
    b2 = (b - b1.astype(F32)).astype(BF16)
    return _dot(a1, b1, dims) + (_dot(a1, b2, dims) + _dot(a2, b1, dims))


def _sigmoid(x):
    return 1.0 / (1.0 + jnp.exp(-x))


def _log1pexp_neg_abs(x):
    return jnp.log(1.0 + jnp.exp(-jnp.abs(x)))


def _iota(shape, dim):
    return lax.broadcasted_iota(jnp.int32, shape, dim)


def _matmul(a, b, mode, out_dtype, name, add=None, comm=None):
    if mode == "nn":
        (m, k), (k2, n) = a.shape, b.shape
    elif mode == "nt":
        (m, k), (n, k2) = a.shape, b.shape
    else:
        (k, m), (k2, n) = a.shape, b.shape
    assert k == k2, (a.shape, b.shape, mode)
    tm, tn, tk = _tile(m, 1408), _tile(n, 1408), _tile(k, 1536)
    nk = k // tk
    dims = {"nn": NN, "nt": NT, "tn": TN}[mode]

    def body(*refs):
        if add is None:
            a_ref, b_ref, o_ref, acc_ref = refs
        else:
            a_ref, b_ref, add_ref, o_ref, acc_ref = refs
        kk = pl.program_id(2)

        @pl.when(kk == 0)
        def _():
            acc_ref[...] = jnp.zeros_like(acc_ref)

        acc_ref[...] += _dotb(a_ref[...], b_ref[...], dims)

        @pl.when(kk == nk - 1)
        def _():
            r = acc_ref[...]
            if add is not None:
                r = r + add_ref[...].astype(F32)
            o_ref[...] = r.astype(out_dtype)

    if mode == "nn":
        specs = [pl.BlockSpec((tm, tk), lambda i, j, l: (i, l)), pl.BlockSpec((tk, tn), lambda i, j, l: (l, j))]
    elif mode == "nt":
        specs = [pl.BlockSpec((tm, tk), lambda i, j, l: (i, l)), pl.BlockSpec((tn, tk), lambda i, j, l: (j, l))]
    else:
        specs = [pl.BlockSpec((tk, tm), lambda i, j, l: (l, i)), pl.BlockSpec((tk, tn), lambda i, j, l: (l, j))]
    args = [a, b]
    if add is not None:
        specs.append(pl.BlockSpec((tm, tn), lambda i, j, l: (i, j)))
        args.append(add)
    grid = (m // tm, n // tn, nk)

    def when():
        i, j, l = pl.program_id(0), pl.program_id(1), pl.program_id(2)
        first = jnp.logical_and(jnp.logical_and(i == 0, j == 0), l == 0)
        last = jnp.logical_and(jnp.logical_and(i == grid[0] - 1, j == grid[1] - 1), l == nk - 1)
        return first, last, last

    (out,), extra = _host_call(
        body, name, comm, when, [jax.ShapeDtypeStruct((m, n), out_dtype)], grid, specs,
        [pl.BlockSpec((tm, tn), lambda i, j, l: (i, j))], [pltpu.VMEM((tm, tn), F32)],
        ("parallel", "parallel", "arbitrary"), args)
    return out if comm is None else (out, extra)


def _rmsnorm_fwd(x, w, name):
    t, d = x.shape
    tr = _tile(t, 512)

    def body(x_ref, w_ref, o_ref):
        xv = x_ref[...]
        r = lax.rsqrt(jnp.mean(xv * xv, axis=1, keepdims=True) + EPS)
        o_ref[...] = (xv * r * w_ref[...]).astype(BF16)

    return pl.pallas_call(
        body, name=name,
        out_shape=jax.ShapeDtypeStruct((t, d), BF16),
        grid=(t // tr,),
        in_specs=[pl.BlockSpec((tr, d), lambda i: (i, 0)), pl.BlockSpec((1, d), lambda i: (0, 0))],
        out_specs=pl.BlockSpec((tr, d), lambda i: (i, 0)),
        compiler_params=_params(("parallel",)),
    )(x, w)


def _rmsnorm_bwd(dn, x, w, dres, name):
    t, d = x.shape
    tr = _tile(t, 512)

    def body(dn_ref, x_ref, w_ref, dres_ref, dx_ref, dw_ref):
        i = pl.program_id(0)
        xv = x_ref[...]
        g = dn_ref[...].astype(F32)
        r = lax.rsqrt(jnp.mean(xv * xv, axis=1, keepdims=True) + EPS)
        xh = xv * r
        dxh = g * w_ref[...]
        dx = r * (dxh - xh * jnp.mean(dxh * xh, axis=1, keepdims=True))
        dx_ref[...] = dres_ref[...] + dx

        @pl.when(i == 0)
        def _():
            dw_ref[...] = jnp.zeros_like(dw_ref)

        dw_ref[...] += jnp.sum(g * xh, axis=0, keepdims=True)

    return pl.pallas_call(
        body, name=name,
        out_shape=(jax.ShapeDtypeStruct((t, d), F32), jax.ShapeDtypeStruct((1, d), F32)),
        grid=(t // tr,),
        in_specs=[pl.BlockSpec((tr, d), lambda i: (i, 0)), pl.BlockSpec((tr, d), lambda i: (i, 0)),
                  pl.BlockSpec((1, d), lambda i: (0, 0)), pl.BlockSpec((tr, d), lambda i: (i, 0))],
        out_specs=(pl.BlockSpec((tr, d), lambda i: (i, 0)), pl.BlockSpec((1, d), lambda i: (0, 0))),
        compiler_params=_params(("arbitrary",)),
    )(dn, x, w, dres)


def _final_loss(x2, target, w, name):
    t, d = x2.shape
    tr = _tile(t, 512)

    def body(x_ref, t_ref, w_ref, dx_ref, dw_ref, loss_ref):
        i = pl.program_id(0)
        xv = x_ref[...]
        r = lax.rsqrt(jnp.mean(xv * xv, axis=1, keepdims=True) + EPS)
        xh = xv * r
        err = xh * w_ref[...] - t_ref[...]
        dy = err * (1.0 / d)
        dxh = dy * w_ref[...]
        dx_ref[...] = r * (dxh - xh * jnp.mean(dxh * xh, axis=1, keepdims=True))

        @pl.when(i == 0)
        def _():
            dw_ref[...] = jnp.zeros_like(dw_ref)
            loss_ref[...] = jnp.zeros_like(loss_ref)

        dw_ref[...] += jnp.sum(dy * xh, axis=0, keepdims=True)
        row = jnp.sum(err * err, axis=1, keepdims=True) * (0.5 / d)
        loss_ref[...] += jnp.sum(row, axis=0, keepdims=True)

    return pl.pallas_call(
        body, name=name,
        out_shape=(jax.ShapeDtypeStruct((t, d), F32), jax.ShapeDtypeStruct((1, d), F32),
                   jax.ShapeDtypeStruct((1, 1), F32)),
        grid=(t // tr,),
        in_specs=[pl.BlockSpec((tr, d), lambda i: (i, 0)), pl.BlockSpec((tr, d), lambda i: (i, 0)),
                  pl.BlockSpec((1, d), lambda i: (0, 0))],
        out_specs=(pl.BlockSpec((tr, d), lambda i: (i, 0)), pl.BlockSpec((1, d), lambda i: (0, 0)),
                   pl.BlockSpec((1, 1), lambda i: (0, 0))),
        compiler_params=_params(("arbitrary",)),
    )(x2, target, w)


def _shift_down(cur, prev, k, row):
    r = pltpu.roll(cur, k, 0)
    top, row8 = r[0:8, :], row[0:8, :]
    for m in range(k):
        top = jnp.where(row8 == m, prev[HALO - k + m:HALO - k + m + 1, :], top)
    return jnp.concatenate([top, r[8:, :]], axis=0)


def _shift_up(cur, nxt, k, row, tr):
    r = pltpu.roll(cur, tr - k, 0)
    bottom, row8 = r[tr - 8:, :], row[0:8, :]
    for m in range(k):
        bottom = jnp.where(row8 == 8 - k + m, nxt[m:m + 1, :], bottom)
    return jnp.concatenate([r[:tr - 8, :], bottom], axis=0)


def _fold8(a):
    out = a[0:8, :]
    for r in range(8, a.shape[0], 8):
        out = out + a[r:r + 8, :]
    return out


def _conv_taps(cur, prev, w, ntaps, row):
    taps = [cur if i == ntaps - 1 else _shift_down(cur, prev, ntaps - 1 - i, row) for i in range(ntaps)]
    y = w[0:1, :] * taps[0]
    for i in range(1, ntaps):
        y = y + w[i:i + 1, :] * taps[i]
    return taps, y


def _conv_bwd_data(parts, w, ntaps, out_dtype, name):
    t, chp = parts[0].shape
    npart = len(parts)
    tr, tc = _tile(t, 512), _tile(chp, 1408)
    nc = chp // tc
    nhalo = t // HALO
    last = t // tr - 1

    def body(*refs):
        cur_refs, nxt_refs = refs[:npart], refs[npart:2 * npart]
        w_ref, o_ref = refs[2 * npart], refs[2 * npart + 1]
        i, j = pl.program_id(0), pl.program_id(1)
        row = _iota((tr, 128), 0)
        for c0 in range(0, tc, 128):
            sl = slice(c0, c0 + 128)
            cur, nxt = cur_refs[0][:, sl].astype(F32), nxt_refs[0][:, sl].astype(F32)
            for p in range(1, npart):
                cur = jnp.where(j >= p * nc, cur_refs[p][:, sl].astype(F32), cur)
                nxt = jnp.where(j >= p * nc, nxt_refs[p][:, sl].astype(F32), nxt)
            nxt = jnp.where(i == last, 0.0, nxt)
            wv = w_ref[:, sl]
            y = wv[ntaps - 1:ntaps, :] * cur
            for k in range(1, ntaps):
                y = y + wv[ntaps - 1 - k:ntaps - k, :] * _shift_up(cur, nxt, k, row, tr)
            o_ref[:, sl] = y.astype(out_dtype)

    col = lambda p: (lambda j: jnp.clip(j - p * nc, 0, nc - 1))
    cur_specs = [pl.BlockSpec((tr, tc), lambda i, j, c=col(p): (i, c(j))) for p in range(npart)]
    nxt_specs = [pl.BlockSpec((HALO, tc),
                              lambda i, j, c=col(p): (jnp.minimum((i + 1) * (tr // HALO), nhalo - 1), c(j)))
                 for p in range(npart)]
    return pl.pallas_call(
        body, name=name,
        out_shape=jax.ShapeDtypeStruct((t, npart * chp), out_dtype),
        grid=(t // tr, npart * nc),
        in_specs=cur_specs + nxt_specs + [pl.BlockSpec((ntaps, tc), lambda i, j: (0, j))],
        out_specs=pl.BlockSpec((tr, tc), lambda i, j: (i, j)),
        compiler_params=_params(("parallel", "parallel")),
    )(*parts, *parts, w)


def _ffn_act_fwd(upre, cw, name):
    t = upre.shape[0]
    tr, tc = _tile(t, 512), _tile(D_FF, 1408)
    nj = D_FF // tc

    def body(g_ref, gp_ref, u_ref, up_ref, wg_ref, wu_ref, o_ref):
        i = pl.program_id(0)
        row = _iota((tr, 128), 0)
        for c0 in range(0, tc, 128):
            sl = slice(c0, c0 + 128)
            gp = jnp.where(i == 0, 0.0, gp_ref[:, sl].astype(F32))
            up = jnp.where(i == 0, 0.0, up_ref[:, sl].astype(F32))
            _, gc = _conv_taps(g_ref[:, sl].astype(F32), gp, wg_ref[:, sl], FFN_CONV, row)
            _, uc = _conv_taps(u_ref[:, sl].astype(F32), up, wu_ref[:, sl], FFN_CONV, row)
            o_ref[:, sl] = (gc * _sigmoid(gc) * uc).astype(BF16)

    prev = lambda off: (lambda i, j: (jnp.maximum(i * (tr // HALO) - 1, 0), j + off))
    return pl.pallas_call(
        body, name=name,
        out_shape=jax.ShapeDtypeStruct((t, D_FF), BF16),
        grid=(t // tr, nj),
        in_specs=[pl.BlockSpec((tr, tc), lambda i, j: (i, j)), pl.BlockSpec((HALO, tc), prev(0)),
                  pl.BlockSpec((tr, tc), lambda i, j: (i, j + nj)), pl.BlockSpec((HALO, tc), prev(nj)),
                  pl.BlockSpec((FFN_CONV, tc), lambda i, j: (0, j)),
                  pl.BlockSpec((FFN_CONV, tc), lambda i, j: (0, j + nj))],
        out_specs=pl.BlockSpec((tr, tc), lambda i, j: (i, j)),
        compiler_params=_params(("parallel", "parallel")),
    )(upre, upre, upre, upre, cw, cw)


def _ffn_act_bwd(dact, upre, cw, name):
    t = upre.shape[0]
    tr, tc = _tile(t, 256), _tile(D_FF, 1408)
    nj = D_FF // tc

    def body(da_ref, g_ref, gp_ref, u_ref, up_ref, wg_ref, wu_ref, dg_ref, du_ref, dwg_ref, dwu_ref):
        i = pl.program_id(1)
        row = _iota((CHUNK_ROWS, 128), 0)

        @pl.when(i == 0)
        def _():
            dwg_ref[...] = jnp.zeros_like(dwg_ref)
            dwu_ref[...] = jnp.zeros_like(dwu_ref)

        for c0 in range(0, tc, 128):
            sl = slice(c0, c0 + 128)
            wg, wu = wg_ref[:, sl], wu_ref[:, sl]
            dwg = [jnp.zeros((8, 128), F32)] * FFN_CONV
            dwu = [jnp.zeros((8, 128), F32)] * FFN_CONV
            for r0 in range(0, tr, CHUNK_ROWS):
                rows = slice(r0, r0 + CHUNK_ROWS)
                if r0 == 0:
                    gp = jnp.where(i == 0, 0.0, gp_ref[:, sl].astype(F32))
                    up = jnp.where(i == 0, 0.0, up_ref[:, sl].astype(F32))
                else:
                    gp = g_ref[r0 - HALO:r0, sl].astype(F32)
                    up = u_ref[r0 - HALO:r0, sl].astype(F32)
                gt, gc = _conv_taps(g_ref[rows, sl].astype(F32), gp, wg, FFN_CONV, row)
                ut, uc = _conv_taps(u_ref[rows, sl].astype(F32), up, wu, FFN_CONV, row)
                da = da_ref[rows, sl].astype(F32)
                sg = _sigmoid(gc)
                dgc = da * uc * (sg * (1.0 + gc * (1.0 - sg)))
                duc = da * (gc * sg)
                dg_ref[rows, sl] = dgc.astype(BF16)
                du_ref[rows, sl] = duc.astype(BF16)
                dwg = [dwg[k] + _fold8(dgc * gt[k]) for k in range(FFN_CONV)]
                dwu = [dwu[k] + _fold8(duc * ut[k]) for k in range(FFN_CONV)]
            for k in range(FFN_CONV):
                dwg_ref[k:k + 1, sl] += jnp.sum(dwg[k], axis=0, keepdims=True)
                dwu_ref[k:k + 1, sl] += jnp.sum(dwu[k], axis=0, keepdims=True)

    prev = lambda off: (lambda j, i: (jnp.maximum(i * (tr // HALO) - 1, 0), j + off))
    blk = lambda off: pl.BlockSpec((tr, tc), lambda j, i: (i, j + off))
    wblk = lambda off: pl.BlockSpec((FFN_CONV, tc), lambda j, i: (0, j + off))
    dgc, duc, dwg, dwu = pl.pallas_call(
        body, name=name,
        out_shape=(jax.ShapeDtypeStruct((t, D_FF), BF16), jax.ShapeDtypeStruct((t, D_FF), BF16),
                   jax.ShapeDtypeStruct((FFN_CONV, D_FF), F32), jax.ShapeDtypeStruct((FFN_CONV, D_FF), F32)),
        grid=(nj, t // tr),
        in_specs=[blk(0), blk(0), pl.BlockSpec((HALO, tc), prev(0)), blk(nj), pl.BlockSpec((HALO, tc), prev(nj)),
                  wblk(0), wblk(nj)],
        out_specs=(blk(0), blk(0), wblk(0), wblk(0)),
        compiler_params=_params(("parallel", "arbitrary")),
    )(dact, upre, upre, upre, upre, cw, cw)
    return dgc, duc, dwg, dwu


def _dn_pre_fwd(qkv_pre, cw, name):
    t = qkv_pre.shape[0]
    tr = _tile(t, 512)
    scale = HEAD_DIM ** -0.5

    def body(x_ref, p_ref, w_ref, o_ref):
        i, j = pl.program_id(0), pl.program_id(1)
        row = _iota((tr, HEAD_DIM), 0)
        for h in range(HEADS):
            sl = slice(h * HEAD_DIM, (h + 1) * HEAD_DIM)
            prev = jnp.where(i == 0, 0.0, p_ref[:, sl].astype(F32))
            _, c = _conv_taps(x_ref[:, sl].astype(F32), prev, w_ref[:, sl], DN_CONV, row)
            s = c * _sigmoid(c)
            r = lax.rsqrt(jnp.sum(s * s, axis=1, keepdims=True) + EPS)
            o_ref[:, sl] = s * jnp.where(j == 0, r * scale, jnp.where(j == 1, r, 1.0))

    return pl.pallas_call(
        body, name=name,
        out_shape=jax.ShapeDtypeStruct((t, 3 * WIDTH), F32),
        grid=(t // tr, 3),
        in_specs=[pl.BlockSpec((tr, WIDTH), lambda i, j: (i, j)),
                  pl.BlockSpec((HALO, WIDTH), lambda i, j: (jnp.maximum(i * (tr // HALO) - 1, 0), j)),
                  pl.BlockSpec((DN_CONV, WIDTH), lambda i, j: (0, j))],
        out_specs=pl.BlockSpec((tr, WIDTH), lambda i, j: (i, j)),
        compiler_params=_params(("parallel", "parallel")),
    )(qkv_pre, qkv_pre, cw)


def _dn_pre_bwd(dq, dk, dv, qkv_pre, cw, name):
    t = qkv_pre.shape[0]
    tr = _tile(t, 256)
    scale = HEAD_DIM ** -0.5

    def body(dq_ref, dk_ref, dv_ref, x_ref, p_ref, w_ref, dc_ref, dw_ref):
        j, i = pl.program_id(0), pl.program_id(1)
        row = _iota((CHUNK_ROWS, HEAD_DIM), 0)

        @pl.when(i == 0)
        def _():
            dw_ref[...] = jnp.zeros_like(dw_ref)

        for h in range(HEADS):
            sl = slice(h * HEAD_DIM, (h + 1) * HEAD_DIM)
            wv = w_ref[:, sl]
            dw = [jnp.zeros((8, HEAD_DIM), F32)] * DN_CONV
            for r0 in range(0, tr, CHUNK_ROWS):
                rows = slice(r0, r0 + CHUNK_ROWS)
                if r0 == 0:
                    prev = jnp.where(i == 0, 0.0, p_ref[:, sl].astype(F32))
                else:
                    prev = x_ref[r0 - HALO:r0, sl].astype(F32)
                taps, c = _conv_taps(x_ref[rows, sl].astype(F32), prev, wv, DN_CONV, row)
                d = jnp.where(j == 0, dq_ref[rows, sl] * scale, jnp.where(j == 1, dk_ref[rows, sl], dv_ref[rows, sl]))
                sg = _sigmoid(c)
                s = c * sg
                r = lax.rsqrt(jnp.sum(s * s, axis=1, keepdims=True) + EPS)
                nh = s * r
                ds_norm = r * (d - nh * jnp.sum(nh * d, axis=1, keepdims=True))
                dc = jnp.where(j < 2, ds_norm, d) * (sg * (1.0 + c * (1.0 - sg)))
                dc_ref[rows, sl] = dc.astype(BF16)
                dw = [dw[k] + _fold8(dc * taps[k]) for k in range(DN_CONV)]
            for k in range(DN_CONV):
                dw_ref[k:k + 1, sl] += jnp.sum(dw[k], axis=0, keepdims=True)

    dspec = lambda p: pl.BlockSpec((tr, WIDTH), lambda j, i: (jnp.where(j == p, i, 0), 0))
    return pl.pallas_call(
        body, name=name,
        out_shape=(jax.ShapeDtypeStruct((t, 3 * WIDTH), BF16), jax.ShapeDtypeStruct((DN_CONV, 3 * WIDTH), F32)),
        grid=(3, t // tr),
        in_specs=[dspec(0), dspec(1), dspec(2),
                  pl.BlockSpec((tr, WIDTH), lambda j, i: (i, j)),
                  pl.BlockSpec((HALO, WIDTH), lambda j, i: (jnp.maximum(i * (tr // HALO) - 1, 0), j)),
                  pl.BlockSpec((DN_CONV, WIDTH), lambda j, i: (0, j))],
        out_specs=(pl.BlockSpec((tr, WIDTH), lambda j, i: (i, j)),
                   pl.BlockSpec((DN_CONV, WIDTH), lambda j, i: (0, j))),
        compiler_params=_params(("parallel", "arbitrary")),
    )(dq, dk, dv, qkv_pre, qkv_pre, cw)


def _tri(n, kind):
    r, c = _iota((n, n), 0), _iota((n, n), 1)
    m = {"lower": r >= c, "strict": r > c, "upper": r <= c}[kind]
    return m


def _dn_gates_fwd(hab, alog, dtb, name):
    t = hab.shape[0]
    cc = DN_CHUNK

    def body(h_ref, al_ref, dt_ref, o_ref):
        hv = h_ref[...]
        lane = _iota(hv.shape, 1)
        xa = hv + dt_ref[...]
        sp = jnp.maximum(xa, 0.0) + _log1pexp_neg_abs(xa)
        g = jnp.where(lane < HEADS, -jnp.exp(al_ref[...]) * sp, 0.0)
        tril = jnp.where(_tri(cc, "lower"), 1.0, 0.0).astype(BF16)
        gc = _dot_xl(tril, g, NN)
        o_ref[...] = jnp.where(lane < HEADS, gc, jnp.where(lane < 2 * HEADS, _sigmoid(hv), 0.0))

    return pl.pallas_call(
        body, name=name,
        out_shape=jax.ShapeDtypeStruct((t, 128), F32),
        grid=(t // cc,),
        in_specs=[pl.BlockSpec((cc, 128), lambda i: (i, 0)), pl.BlockSpec((1, 128), lambda i: (0, 0)),
                  pl.BlockSpec((1, 128), lambda i: (0, 0))],
        out_specs=pl.BlockSpec((cc, 128), lambda i: (i, 0)),
        compiler_params=_params(("parallel",)),
    )(hab, alog, dtb)


def _dn_gates_bwd(dgates, hab, alog, dtb, name):
    t = hab.shape[0]
    cc = DN_CHUNK

    def body(d_ref, h_ref, al_ref, dt_ref, o_ref, dal_ref, ddt_ref):
        i = pl.program_id(0)
        hv = h_ref[...]
        dv = d_ref[...]
        lane = _iota(hv.shape, 1)
        triu = jnp.where(_tri(cc, "upper"), 1.0, 0.0).astype(BF16)
        dg = _dot_xl(triu, jnp.where(lane < HEADS, dv, 0.0), NN)
        xa = hv + dt_ref[...]
        sp = jnp.maximum(xa, 0.0) + _log1pexp_neg_abs(xa)
        ea = jnp.exp(al_ref[...])
        da = jnp.where(lane < HEADS, dg * (-ea) * _sigmoid(xa), 0.0)
        be = _sigmoid(hv)
        db = dv * be * (1.0 - be)
        o_ref[...] = jnp.where(lane < HEADS, da, jnp.where(lane < 2 * HEADS, db, 0.0))

        @pl.when(i == 0)
        def _():
            dal_ref[...] = jnp.zeros_like(dal_ref)
            ddt_ref[...] = jnp.zeros_like(ddt_ref)

        dal_ref[...] += jnp.sum(jnp.where(lane < HEADS, dg * (-ea) * sp, 0.0), axis=0, keepdims=True)
        ddt_ref[...] += jnp.sum(da, axis=0, keepdims=True)

    return pl.pallas_call(
        body, name=name,
        out_shape=(jax.ShapeDtypeStruct((t, 128), F32), jax.ShapeDtypeStruct((1, 128), F32),
                   jax.ShapeDtypeStruct((1, 128), F32)),
        grid=(t // cc,),
        in_specs=[pl.BlockSpec((cc, 128), lambda i: (i, 0)), pl.BlockSpec((cc, 128), lambda i: (i, 0)),
                  pl.BlockSpec((1, 128), lambda i: (0, 0)), pl.BlockSpec((1, 128), lambda i: (0, 0))],
        out_specs=(pl.BlockSpec((cc, 128), lambda i: (i, 0)), pl.BlockSpec((1, 128), lambda i: (0, 0)),
                   pl.BlockSpec((1, 128), lambda i: (0, 0))),
        compiler_params=_params(("arbitrary",)),
    )(dgates, hab, alog, dtb)


def _dn_chunk_common(gates, h):
    cc = DN_CHUNK
    lane = _iota(gates.shape, 1)
    gh = jnp.where(lane == h, gates, 0.0)
    gc_col = jnp.sum(gh, axis=1, keepdims=True)
    gc_row = _dot_xl(jnp.ones((cc, 128), BF16), gh, NT)
    beta = jnp.sum(jnp.where(lane == h + HEADS, gates, 0.0), axis=1, keepdims=True)
    lower = _tri(cc, "lower")
    decay = jnp.where(lower, jnp.exp(jnp.where(lower, gc_col - gc_row, 0.0)), 0.0)
    gc_last = gc_col[cc - 1:cc, :]
    return gc_col, gc_last, beta, decay


def _dn_local_fwd(act, gates, name):
    t = act.shape[0]
    cc = DN_CHUNK
    nc = t // cc

    def body(q_ref, k_ref, v_ref, g_ref, u_ref, w_ref, kd_ref, qg_ref, ti_ref, p_ref):
        gates = g_ref[...]
        eye = jnp.where(_iota((cc, cc), 0) == _iota((cc, cc), 1), 1.0, 0.0)
        hs = range(HEADS)
        sl = [slice(h * HEAD_DIM, (h + 1) * HEAD_DIM) for h in hs]
        q, k, v = ([r[:, s] for s in sl] for r in (q_ref, k_ref, v_ref))
        gc_col, gc_last, beta, decay = zip(*[_dn_chunk_common(gates, h) for h in hs])
        gam = [jnp.exp(g) for g in gc_col]
        kb = [k[h] * beta[h] for h in hs]
        npow = [-jnp.where(_tri(cc, "strict"), _dotb(kb[h], k[h], NT) * decay[h], 0.0) for h in hs]
        tinv = [eye + n for n in npow]
        for _ in range(5):
            npow = [_dot3(n, n, NN) for n in npow]
            tinv = [t + _dot3(t, n, NN) for t, n in zip(tinv, npow)]
        uu = [_dot3(tinv[h], v[h] * beta[h], NN) for h in hs]
        ww = [_dot3(tinv[h], kb[h] * gam[h], NN) for h in hs]
        pp = [jnp.where(_tri(cc, "lower"), _dotb(q[h], k[h], NT) * decay[h], 0.0) for h in hs]
        for h in hs:
            u_ref[:, sl[h]] = uu[h]
            w_ref[:, sl[h]] = ww[h]
            kd_ref[:, sl[h]] = k[h] * jnp.exp(gc_last[h] - gc_col[h])
            qg_ref[:, sl[h]] = q[h] * gam[h]
            ti_ref[h] = tinv[h]
            p_ref[h] = pp[h]

    row = lambda off: pl.BlockSpec((cc, WIDTH), lambda n: (n, off))
    mat = pl.BlockSpec((HEADS, cc, cc), lambda n: (0, n, 0))
    tw = jax.ShapeDtypeStruct((t, WIDTH), F32)
    hm = jax.ShapeDtypeStruct((HEADS, t, cc), F32)
    return pl.pallas_call(
        body, name=name,
        out_shape=(tw, tw, tw, tw, hm, hm),
        grid=(nc,),
        in_specs=[row(0), row(1), row(2), pl.BlockSpec((cc, 128), lambda n: (n, 0))],
        out_specs=(row(0), row(0), row(0), row(0), mat, mat),
        compiler_params=_params(("parallel",)),
    )(act, act, act, gates)


def _dn_scan_fwd(u, w, kd, qg, p, gates, name):
    t = u.shape[0]
    cc = DN_CHUNK
    nc = t // cc

    def body(u_ref, w_ref, kd_ref, qg_ref, p_ref, g_ref, o_ref, sh_ref, s_ref):
        n = pl.program_id(0)

        @pl.when(n == 0)
        def _():
            s_ref[...] = jnp.zeros_like(s_ref)

        glast = jnp.exp(g_ref[cc - 1:cc, :])
        hs = range(HEADS)
        sl = [slice(h * HEAD_DIM, (h + 1) * HEAD_DIM) for h in hs]
        s = [s_ref[h] for h in hs]
        sb = [a.astype(BF16) for a in s]
        vn = [u_ref[:, sl[h]] - _dot(w_ref[:, sl[h]].astype(BF16), sb[h], NN) for h in hs]
        vnb = [a.astype(BF16) for a in vn]
        o_state = [_dot(qg_ref[:, sl[h]].astype(BF16), sb[h], NN) for h in hs]
        o_local = [_dot(p_ref[h].astype(BF16), vnb[h], NN) for h in hs]
        s_add = [_dot(kd_ref[:, sl[h]].astype(BF16), vnb[h], TN) for h in hs]
        for h in hs:
            o_ref[:, sl[h]] = o_state[h] + o_local[h]
            sh_ref[0, h] = s[h]
            s_ref[h] = glast[:, h:h + 1] * s[h] + s_add[h]

    row = pl.BlockSpec((cc, WIDTH), lambda n: (n, 0))
    return pl.pallas_call(
        body, name=name,
        out_shape=(jax.ShapeDtypeStruct((t, WIDTH), F32),
                   jax.ShapeDtypeStruct((nc, HEADS, HEAD_DIM, HEAD_DIM), F32)),
        grid=(nc,),
        in_specs=[row, row, row, row, pl.BlockSpec((HEADS, cc, cc), lambda n: (0, n, 0)),
                  pl.BlockSpec((cc, 128), lambda n: (n, 0))],
        out_specs=(row, pl.BlockSpec((1, HEADS, HEAD_DIM, HEAD_DIM), lambda n: (n, 0, 0, 0))),
        scratch_shapes=[pltpu.VMEM((HEADS, HEAD_DIM, HEAD_DIM), F32)],
        compiler_params=_params(("arbitrary",)),
    )(u, w, kd, qg, p, gates)


def _dn_scan_bwd(do, w, kd, qg, p, gates, name):
    t = do.shape[0]
    cc = DN_CHUNK
    nc = t // cc

    def body(do_ref, w_ref, kd_ref, qg_ref, p_ref, g_ref, dvn_ref, dsh_ref, ds_ref):
        n = pl.program_id(0)

        @pl.when(n == 0)
        def _():
            ds_ref[...] = jnp.zeros_like(ds_ref)

        glast = jnp.exp(g_ref[cc - 1:cc, :])
        hs = range(HEADS)
        sl = [slice(h * HEAD_DIM, (h + 1) * HEAD_DIM) for h in hs]
        ds = [ds_ref[h] for h in hs]
        dob = [do_ref[:, sl[h]].astype(BF16) for h in hs]
        dvn = [_dot(p_ref[h].astype(BF16), dob[h], TN) + _dot(kd_ref[:, sl[h]].astype(BF16), ds[h].astype(BF16), NN)
               for h in hs]
        ds_q = [_dot(qg_ref[:, sl[h]].astype(BF16), dob[h], TN) for h in hs]
        ds_w = [_dot(w_ref[:, sl[h]].astype(BF16), dvn[h].astype(BF16), TN) for h in hs]
        for h in hs:
            dvn_ref[:, sl[h]] = dvn[h]
            dsh_ref[0, h] = ds[h]
            ds_ref[h] = ds_q[h] + glast[:, h:h + 1] * ds[h] - ds_w[h]

    row = pl.BlockSpec((cc, WIDTH), lambda n: (nc - 1 - n, 0))
    return pl.pallas_call(
        body, name=name,
        out_shape=(jax.ShapeDtypeStruct((t, WIDTH), F32),
                   jax.ShapeDtypeStruct((nc, HEADS, HEAD_DIM, HEAD_DIM), F32)),
        grid=(nc,),
        in_specs=[row, row, row, row, pl.BlockSpec((HEADS, cc, cc), lambda n: (0, nc - 1 - n, 0)),
                  pl.BlockSpec((cc, 128), lambda n: (nc - 1 - n, 0))],
        out_specs=(row, pl.BlockSpec((1, HEADS, HEAD_DIM, HEAD_DIM), lambda n: (nc - 1 - n, 0, 0, 0))),
        scratch_shapes=[pltpu.VMEM((HEADS, HEAD_DIM, HEAD_DIM), F32)],
        compiler_params=_params(("arbitrary",)),
    )(do, w, kd, qg, p, gates)


def _dn_local_bwd(act, gates, u, w, kd, qg, tinv, p, sh, dsh, dvn, do, name):
    t = act.shape[0]
    cc = DN_CHUNK
    nc = t // cc

    def body(q_ref, k_ref, v_ref, g_ref, u_ref, w_ref, kd_ref, qg_ref, ti_ref, p_ref, s_ref, ds_ref,
             dvn_ref, do_ref, dq_ref, dk_ref, dv_ref, dg_ref):
        gates_v = g_ref[...]
        lower, strict = _tri(cc, "lower"), _tri(cc, "strict")
        ones = jnp.ones((cc, 128), BF16)
        rowc = _iota((cc, 1), 0)
        lane = _iota((cc, 128), 1)
        hs = range(HEADS)
        sl = [slice(h * HEAD_DIM, (h + 1) * HEAD_DIM) for h in hs]
        q, k, v, uu, ww, kd, qg, dvn, do = ([r[:, s] for s in sl] for r in (
            q_ref, k_ref, v_ref, u_ref, w_ref, kd_ref, qg_ref, dvn_ref, do_ref))
        gc_col, gc_last, beta, decay = zip(*[_dn_chunk_common(gates_v, h) for h in hs])
        gam = [jnp.exp(g) for g in gc_col]
        kb = [k[h] * beta[h] for h in hs]
        s_in = [s_ref[0, h] for h in hs]
        ds_out = [ds_ref[0, h] for h in hs]
        tinv = [ti_ref[h] for h in hs]

        a = [jnp.where(strict, _dotb(kb[h], k[h], NT) * decay[h], 0.0) for h in hs]
        vn = [uu[h] - _dotb(ww[h], s_in[h], NN) for h in hs]
        dqg = [_dotb(do[h], s_in[h], NT) for h in hs]
        dw = [-_dotb(dvn[h], s_in[h], NT) for h in hs]
        dp = [jnp.where(lower, _dotb(do[h], vn[h], NT), 0.0) for h in hs]
        dkd = [_dotb(vn[h], ds_out[h], NT) for h in hs]
        dru = [_dot3(tinv[h], dvn[h], TN) for h in hs]
        drw = [_dot3(tinv[h], dw[h], TN) for h in hs]
        da = [-jnp.where(strict, _dotb(dru[h], uu[h], NT) + _dotb(drw[h], ww[h], NT), 0.0) for h in hs]
        dad = [da[h] * decay[h] for h in hs]
        dpd = [dp[h] * decay[h] for h in hs]
        dkb = [_dotb(dad[h], k[h], NN) + gam[h] * drw[h] for h in hs]
        dk = [_dotb(dad[h], kb[h], TN) + _dotb(dpd[h], q[h], TN) + beta[h] * dkb[h]
              + jnp.exp(gc_last[h] - gc_col[h]) * dkd[h] for h in hs]
        dq = [gam[h] * dqg[h] + _dotb(dpd[h], k[h], NN) for h in hs]
        gm = [da[h] * a[h] + dp[h] * p_ref[h] for h in hs]
        colsum = [_dot_xr(gm[h], ones, TN)[:, 0:1] for h in hs]

        dgates = jnp.zeros((cc, 128), F32)
        for h in hs:
            dk_ref[:, sl[h]] = dk[h]
            dq_ref[:, sl[h]] = dq[h]
            dv_ref[:, sl[h]] = beta[h] * dru[h]
            dbeta = (jnp.sum(dkb[h] * k[h], axis=1, keepdims=True)
                     + jnp.sum(dru[h] * v[h], axis=1, keepdims=True))
            rkd = jnp.sum(dkd[h] * kd[h], axis=1, keepdims=True)
            dgc = (jnp.sum(gm[h], axis=1, keepdims=True) - colsum[h]
                   + jnp.sum(dqg[h] * qg[h], axis=1, keepdims=True)
                   + jnp.sum(drw[h] * kb[h], axis=1, keepdims=True) * gam[h] - rkd)
            tail = jnp.sum(rkd, axis=0, keepdims=True) + jnp.exp(gc_last[h]) * jnp.sum(
                jnp.sum(s_in[h] * ds_out[h], axis=1, keepdims=True), axis=0, keepdims=True)
            dgc = dgc + jnp.where(rowc == cc - 1, tail, 0.0)
            dgates = dgates + jnp.where(lane == h, dgc, 0.0) + jnp.where(lane == h + HEADS, dbeta, 0.0)
        dg_ref[...] = dgates

    row = lambda off: pl.BlockSpec((cc, WIDTH), lambda n: (n, off))
    mat = pl.BlockSpec((HEADS, cc, cc), lambda n: (0, n, 0))
    st = pl.BlockSpec((1, HEADS, HEAD_DIM, HEAD_DIM), lambda n: (n, 0, 0, 0))
    gl = pl.BlockSpec((cc, 128), lambda n: (n, 0))
    tw = jax.ShapeDtypeStruct((t, WIDTH), F32)
    return pl.pallas_call(
        body, name=name,
        out_shape=(tw, tw, tw, jax.ShapeDtypeStruct((t, 128), F32)),
        grid=(nc,),
        in_specs=[row(0), row(1), row(2), gl, row(0), row(0), row(0), row(0), mat, mat, st, st, row(0), row(0)],
        out_specs=(row(0), row(0), row(0), gl),
        compiler_params=_params(("parallel",)),
    )(act, act, act, gates, u, w, kd, qg, tinv, p, sh, dsh, dvn, do)


def _dn_post_fwd(o, gate, w, name):
    t = o.shape[0]
    tr = _tile(t, 512)

    def body(o_ref, g_ref, w_ref, y_ref):
        for h in range(HEADS):
            sl = slice(h * HEAD_DIM, (h + 1) * HEAD_DIM)
            ov, gv = o_ref[:, sl], g_ref[:, sl].astype(F32)
            r = lax.rsqrt(jnp.mean(ov * ov, axis=1, keepdims=True) + EPS)
            y_ref[:, sl] = (ov * r * w_ref[...] * (gv * _sigmoid(gv))).astype(BF16)

    blk = pl.BlockSpec((tr, WIDTH), lambda i: (i, 0))
    return pl.pallas_call(
        body, name=name,
        out_shape=jax.ShapeDtypeStruct((t, WIDTH), BF16),
        grid=(t // tr,),
        in_specs=[blk, blk, pl.BlockSpec((1, HEAD_DIM), lambda i: (0, 0))],
        out_specs=blk,
        compiler_params=_params(("parallel",)),
    )(o, gate, w)


def _dn_post_bwd(dy, o, gate, w, name):
    t = o.shape[0]
    tr = _tile(t, 512)

    def body(dy_ref, o_ref, g_ref, w_ref, do_ref, dg_ref, dw_ref):
        i = pl.program_id(0)

        @pl.when(i == 0)
        def _():
            dw_ref[...] = jnp.zeros_like(dw_ref)

        dw = jnp.zeros((1, HEAD_DIM), F32)
        for h in range(HEADS):
            sl = slice(h * HEAD_DIM, (h + 1) * HEAD_DIM)
            ov, gv, dyv = o_ref[:, sl], g_ref[:, sl].astype(F32), dy_ref[:, sl].astype(F32)
            r = lax.rsqrt(jnp.mean(ov * ov, axis=1, keepdims=True) + EPS)
            oh = ov * r
            sg = _sigmoid(gv)
            dg_ref[:, sl] = (dyv * oh * w_ref[...] * (sg * (1.0 + gv * (1.0 - sg)))).astype(BF16)
            dn = dyv * (gv * sg)
            doh = dn * w_ref[...]
            do_ref[:, sl] = r * (doh - oh * jnp.mean(doh * oh, axis=1, keepdims=True))
            dw = dw + jnp.sum(dn * oh, axis=0, keepdims=True)
        dw_ref[...] += dw

    blk = pl.BlockSpec((tr, WIDTH), lambda i: (i, 0))
    return pl.pallas_call(
        body, name=name,
        out_shape=(jax.ShapeDtypeStruct((t, WIDTH), F32), jax.ShapeDtypeStruct((t, WIDTH), BF16),
                   jax.ShapeDtypeStruct((1, HEAD_DIM), F32)),
        grid=(t // tr,),
        in_specs=[blk, blk, blk, pl.BlockSpec((1, HEAD_DIM), lambda i: (0, 0))],
        out_specs=(blk, blk, pl.BlockSpec((1, HEAD_DIM), lambda i: (0, 0))),
        compiler_params=_params(("arbitrary",)),
    )(dy, o, gate, w)


def _sb_scores(qs, k_ref, qi, it, carries, uincl):
    bk = ATT_BLOCK
    scale = HEAD_DIM ** -0.5
    heads, groups = range(len(qs)), range(SB_GROUP)
    lane = [slice(e * HEAD_DIM, (e + 1) * HEAD_DIM) for e in heads]
    js = [qi - SB_GROUP * it - g for g in groups]
    rows = [pl.ds(pl.multiple_of(jnp.maximum(j, 0) * bk, bk), bk) for j in js]
    qpos = qi * bk + _iota((bk, bk), 0)
    col = _iota((bk, bk), 1)
    mask1 = [jnp.logical_and(j * bk + col < qpos, j >= 0) for j in js]
    ks = [[k_ref[r, lane[e]] for r in rows] for e in heads]
    z = [[_dot(qs[e], k, NT) * scale for k in ks[e]] for e in heads]
    soft = [[_log1pexp_neg_abs(a) for a in ze] for ze in z]
    lk_full = [[-(jnp.maximum(a, 0.0) + s) for a, s in zip(z[e], soft[e])] for e in heads]
    lk = [[jnp.where(m, a, 0.0) for m, a in zip(mask1, lk_full[e])] for e in heads]
    ls = [[jnp.minimum(a, 0.0) - s for a, s in zip(z[e], soft[e])] for e in heads]
    incl = [[_dot_xr2(a, uincl, NN) for a in lk[e]] for e in heads]
    weights, out_carries = [], []
    for e in heads:
        cb, we = carries[e], []
        for g in groups:
            we.append(jnp.where(mask1[g], jnp.exp(ls[e][g] + (cb + incl[e][g] - lk[e][g])), 0.0))
            cb = cb + incl[e][g][:, 0:1]
        weights.append(we)
        out_carries.append(cb)
    return rows, ks, weights, mask1, lk_full, ls, out_carries


def _sb_more(qi, carry):
    it, cbs = carry[0], carry[1]
    live = jnp.max(cbs[0])
    for cb in cbs[1:]:
        live = jnp.maximum(live, jnp.max(cb))
    return jnp.logical_and(SB_GROUP * it <= qi, live > SB_LOG_ZERO)


def _sb_steps(groups, nq):
    def when():
        h, i = pl.program_id(0), pl.program_id(1)
        return (jnp.logical_and(h == 0, i == 0), jnp.logical_and(h == groups // 2, i == 0),
                jnp.logical_and(h == groups - 1, i == nq - 1))
    return when


def _sb_fwd(qkv, name, comm=None):
    t = qkv.shape[0]
    bk = ATT_BLOCK
    hp, wide = SB_HEADS_FWD, SB_HEADS_FWD * HEAD_DIM
    lane = [slice(e * HEAD_DIM, (e + 1) * HEAD_DIM) for e in range(hp)]

    def body(q_ref, k_ref, v_ref, o_ref):
        qi = pl.program_id(1)
        qs = [q_ref[:, s] for s in lane]
        uincl = jnp.where(_tri(bk, "lower"), 1.0, 0.0).astype(BF16)

        def step(carry):
            it, cbs, accs = carry
            rows, _, weights, _, _, _, cbs = _sb_scores(qs, k_ref, qi, it, cbs, uincl)
            accs = list(accs)
            for e in range(hp):
                for r, a in zip(rows, weights[e]):
                    accs[e] = accs[e] + _dot(a.astype(BF16), v_ref[r, lane[e]], NN)
            return it + 1, tuple(cbs), tuple(accs)

        init = (jnp.int32(0), (jnp.zeros((bk, 1), F32),) * hp, (jnp.zeros((bk, HEAD_DIM), F32),) * hp)
        _, _, accs = lax.while_loop(functools.partial(_sb_more, qi), step, init)
        for e in range(hp):
            o_ref[:, lane[e]] = accs[e]

    groups = HEADS // hp
    (o,), extra = _host_call(
        body, name, comm, _sb_steps(groups, t // bk), [jax.ShapeDtypeStruct((t, WIDTH), F32)], (groups, t // bk),
        [pl.BlockSpec((bk, wide), lambda h, i: (i, h)),
         pl.BlockSpec((t, wide), lambda h, i: (0, groups + h)),
         pl.BlockSpec((t, wide), lambda h, i: (0, 2 * groups + h))],
        [pl.BlockSpec((bk, wide), lambda h, i: (i, h))], [], ("parallel", "arbitrary"), (qkv, qkv, qkv))
    return o, extra


def _sb_bwd(qkv, o, do, name, comm=None):
    assert do.dtype == BF16
    t = qkv.shape[0]
    bk = ATT_BLOCK
    scale = HEAD_DIM ** -0.5
    hp, wide = SB_HEADS_BWD, SB_HEADS_BWD * HEAD_DIM
    lane = [slice(e * HEAD_DIM, (e + 1) * HEAD_DIM) for e in range(hp)]

    def body(q_ref, k_ref, v_ref, o_ref, do_ref, dq_ref, dk_ref, dv_ref):
        qi = pl.program_id(1)

        @pl.when(qi == 0)
        def _():
            dk_ref[...] = jnp.zeros_like(dk_ref)
            dv_ref[...] = jnp.zeros_like(dv_ref)

        heads, groups = range(hp), range(SB_GROUP)
        qs = [q_ref[:, s] for s in lane]
        dob = [do_ref[:, s] for s in lane]
        dsum = [jnp.sum(dob[e].astype(F32) * o_ref[:, lane[e]], axis=1, keepdims=True) for e in heads]
        uincl = jnp.where(_tri(bk, "lower"), 1.0, 0.0).astype(BF16)

        def step(carry):
            it, cbs, ces, dqs = carry
            rows, ks, weights, mask, lk_full, ls, cbs = _sb_scores(qs, k_ref, qi, it, cbs, uincl)
            ab = [[a.astype(BF16) for a in weights[e]] for e in heads]
            vs = [[v_ref[r, lane[e]] for r in rows] for e in heads]
            dla = [[ab[e][g].astype(F32) * _dot(dob[e], vs[e][g], NT) for g in groups] for e in heads]
            suf = [[_dot_xr2(a, uincl, NN) for a in dla[e]] for e in heads]
            ces, dqs = list(ces), list(dqs)
            for e in heads:
                for g in groups:
                    err = dsum[e] - (ces[e] + suf[e][g])
                    ces[e] = ces[e] + suf[e][g][:, 0:1]
                    dz = jnp.where(mask[g], dla[e][g] * jnp.exp(lk_full[e][g]) - err * jnp.exp(ls[e][g]), 0.0)
                    dzb = (dz * scale).astype(BF16)
                    dqs[e] = dqs[e] + _dot(dzb, ks[e][g], NN)
                    dk_ref[rows[g], lane[e]] += _dot(dzb, qs[e], TN)
                    dv_ref[rows[g], lane[e]] += _dot(ab[e][g], dob[e], TN)
            return it + 1, tuple(cbs), tuple(ces), tuple(dqs)

        zc = (jnp.zeros((bk, 1), F32),) * hp
        init = (jnp.int32(0), zc, zc, (jnp.zeros((bk, HEAD_DIM), F32),) * hp)
        dqs = lax.while_loop(functools.partial(_sb_more, qi), step, init)[3]
        for e in heads:
            dq_ref[:, lane[e]] = dqs[e]

    ngroup = HEADS // hp
    tw = jax.ShapeDtypeStruct((t, WIDTH), F32)
    qb = pl.BlockSpec((bk, wide), lambda h, i: (i, h))
    full = lambda off: pl.BlockSpec((t, wide), lambda h, i: (0, off + h))
    return _host_call(
        body, name, comm, _sb_steps(ngroup, t // bk), [tw, tw, tw], (ngroup, t // bk),
        [qb, full(ngroup), full(2 * ngroup), qb, qb], [qb, full(0), full(0)], [], ("parallel", "arbitrary"),
        (qkv, qkv, qkv, o, do))


def _merge_fwd(pd, ps, gl, name):
    t = pd.shape[0]
    tr, tc = _tile(t, 512), 512
    nj = D_MODEL // tc

    def body(pd_ref, ps_ref, gd_ref, gs_ref, o_ref):
        gd, gs = gd_ref[...].astype(F32), gs_ref[...].astype(F32)
        o_ref[...] = (_sigmoid(gd) * pd_ref[...].astype(F32) + _sigmoid(gs) * ps_ref[...].astype(F32)).astype(BF16)

    blk = lambda off: pl.BlockSpec((tr, tc), lambda i, j: (i, j + off))
    return pl.pallas_call(
        body, name=name,
        out_shape=jax.ShapeDtypeStruct((t, D_MODEL), BF16),
        grid=(t // tr, nj),
        in_specs=[blk(0), blk(0), blk(0), blk(nj)],
        out_specs=blk(0),
        compiler_params=_params(("parallel", "parallel")),
    )(pd, ps, gl, gl)


def _merge_bwd(dm, pd, ps, gl, name):
    t = pd.shape[0]
    tr, tc = _tile(t, 512), 512
    nj = D_MODEL // tc

    def body(dm_ref, pd_ref, ps_ref, gd_ref, gs_ref, dpd_ref, dps_ref, dgd_ref, dgs_ref):
        dmv = dm_ref[...].astype(F32)
        sd, ss = _sigmoid(gd_ref[...].astype(F32)), _sigmoid(gs_ref[...].astype(F32))
        dpd_ref[...] = (dmv * sd).astype(BF16)
        dps_ref[...] = (dmv * ss).astype(BF16)
        dgd_ref[...] = (dmv * pd_ref[...].astype(F32) * sd * (1.0 - sd)).astype(BF16)
        dgs_ref[...] = (dmv * ps_ref[...].astype(F32) * ss * (1.0 - ss)).astype(BF16)

    blk = lambda off: pl.BlockSpec((tr, tc), lambda i, j: (i, j + off))
    out = jax.ShapeDtypeStruct((t, D_MODEL), BF16)
    return pl.pallas_call(
        body, name=name,
        out_shape=(out, out, out, out),
        grid=(t // tr, nj),
        in_specs=[blk(0), blk(0), blk(0), blk(0), blk(nj)],
        out_specs=(blk(0), blk(0), blk(0), blk(0)),
        compiler_params=_params(("parallel", "parallel")),
    )(dm, pd, ps, gl, gl)


def _local_step(x, target, wts, plan=None):
    n1 = _rmsnorm_fwd(x, wts["norm1_w"], "norm1_fwd")
    qkv_pre = _matmul(n1, wts["w_dnqkv_t"], "nt", BF16, "in_dnqkv")
    hgate = _matmul(n1, wts["w_dngate_t"], "nt", BF16, "in_dngate")
    sbqkv = _matmul(n1, wts["w_sbqkv_t"], "nt", BF16, "in_sbqkv")
    gl = _matmul(n1, wts["w_gl_t"], "nt", BF16, "in_gl")
    hab = _matmul(n1, wts["w_ab_t"], "nt", F32, "in_ab")

    act = _dn_pre_fwd(qkv_pre, wts["dn_conv_w"], "dn_pre_fwd")
    gates = _dn_gates_fwd(hab, wts["alog"], wts["dtb"], "dn_gates_fwd")
    u, w, kd, qg, tinv, p = _dn_local_fwd(act, gates, "dn_local_fwd")
    o_dn, sh = _dn_scan_fwd(u, w, kd, qg, p, gates, "dn_scan_fwd")
    y_dn = _dn_post_fwd(o_dn, hgate, wts["dn_norm_w"], "dn_post_fwd")

    o_sb, late = _sb_fwd(sbqkv, "sb_fwd", comm=plan.late_gather() if plan else None)
    if plan:
        wts = {**wts, **plan.late_weights(late)}

    pd = _matmul(y_dn, wts["w_proj_dn"], "nn", BF16, "proj_dn")
    ps = _matmul(o_sb, wts["w_proj_sb"], "nn", BF16, "proj_sb")
    mixed = _merge_fwd(pd, ps, gl, "merge_fwd")
    x1 = _matmul(mixed, wts["w_out"], "nn", F32, "out_proj", add=x)

    n2 = _rmsnorm_fwd(x1, wts["norm2_w"], "norm2_fwd")
    upre = _matmul(n2, wts["ffn_w_up_t"], "nt", BF16, "ffn_up")
    fact = _ffn_act_fwd(upre, wts["ffn_conv_w"], "ffn_act_fwd")
    x2 = _matmul(fact, wts["ffn_w_down"], "nn", F32, "ffn_down", add=x1)

    dx2, g_normf, loss = _final_loss(x2, target, wts["norm_f_w"], "final_loss")

    dfact = _matmul(dx2, wts["ffn_w_down"], "nt", BF16, "ffn_down_dx")
    g_wdown = _matmul(fact, dx2, "tn", BF16, "ffn_down_dw")
    dgc, duc, dwg, dwu = _ffn_act_bwd(dfact, upre, wts["ffn_conv_w"], "ffn_act_bwd")
    g_fconv = jnp.concatenate([dwg, dwu], axis=1)
    dupre = _conv_bwd_data([dgc, duc], wts["ffn_conv_w"], FFN_CONV, BF16, "ffn_conv_bwd")
    dn2 = _matmul(dupre, wts["ffn_w_up_t"], "nn", F32, "ffn_up_dx")
    g_wup = _matmul(dupre, n2, "tn", BF16, "ffn_up_dw")
    dx1, g_norm2 = _rmsnorm_bwd(dn2, x1, wts["norm2_w"], dx2, "norm2_bwd")

    dmixed = _matmul(dx1, wts["w_out"], "nt", BF16, "out_proj_dx")
    g_wout = _matmul(mixed, dx1, "tn", BF16, "out_proj_dw")
    dpd, dps, dgd, dgs = _merge_bwd(dmixed, pd, ps, gl, "merge_bwd")
    dy_dn = _matmul(dpd, wts["w_proj_dn"], "nt", BF16, "proj_dn_dx")
    g_wpd = _matmul(y_dn, dpd, "tn", BF16, "proj_dn_dw")
    do_sb = _matmul(dps, wts["w_proj_sb"], "nt", BF16, "proj_sb_dx")
    g_wps = _matmul(o_sb, dps, "tn", BF16, "proj_sb_dw")
    grads = dict(w_proj_dn=g_wpd, w_proj_sb=g_wps, w_out=g_wout, ffn_w_up_t=g_wup, ffn_w_down=g_wdown)

    (dsq, dsk, dsv), got_early = _sb_bwd(sbqkv, o_sb, do_sb, "sb_bwd",
                                         comm=plan.early_grads(grads) if plan else None)

    do_dn, dhgate, g_dnnorm = _dn_post_bwd(dy_dn, o_dn, hgate, wts["dn_norm_w"], "dn_post_bwd")
    dvn, dsh = _dn_scan_bwd(do_dn, w, kd, qg, p, gates, "dn_scan_bwd")
    dq, dk, dv, dgates = _dn_local_bwd(act, gates, u, w, kd, qg, tinv, p, sh, dsh, dvn, do_dn, "dn_local_bwd")
    dhab, g_alog, g_dtb = _dn_gates_bwd(dgates, hab, wts["alog"], wts["dtb"], "dn_gates_bwd")
    dcv, g_dnconv = _dn_pre_bwd(dq, dk, dv, qkv_pre, wts["dn_conv_w"], "dn_pre_bwd")
    dqkv_pre = _conv_bwd_data([dcv], wts["dn_conv_w"], DN_CONV, BF16, "dn_conv_bwd")

    dh = jnp.concatenate([dqkv_pre, dhgate, dsq.astype(BF16), dsk.astype(BF16), dsv.astype(BF16), dgd, dgs], axis=1)
    w_main_t = jnp.concatenate([wts["w_dnqkv_t"], wts["w_dngate_t"], wts["w_sbqkv_t"], wts["w_gl_t"]], axis=0)
    g_wmain = _matmul(dh, n1, "tn", BF16, "in_dw_main")
    g_wab = _matmul(dhab, n1, "tn", BF16, "in_dw_ab")
    grads.update(w_main_t=g_wmain, w_ab_t=g_wab, dn_conv_w=g_dnconv, alog=g_alog, dtb=g_dtb, dn_norm_w=g_dnnorm,
                 norm2_w=g_norm2, ffn_conv_w=g_fconv, norm_f_w=g_normf)
    got_late = []
    if plan:
        dn1, swapped = _matmul(dhab, wts["w_ab_t"], "nn", F32, "in_dx_ab", comm=plan.sibling_swap(grads))
        dn1, got_late = _matmul(dh, w_main_t, "nn", F32, "in_dx_main", add=dn1,
                                comm=plan.late_grads(swapped, grads, loss))
    else:
        dn1 = _matmul(dhab, wts["w_ab_t"], "nn", F32, "in_dx_ab")
        dn1 = _matmul(dh, w_main_t, "nn", F32, "in_dx_main", add=dn1)
    grad_x, g_norm1 = _rmsnorm_bwd(dn1, x, wts["norm1_w"], dx1, "norm1_bwd")
    grads["norm1_w"] = g_norm1
    return loss, grad_x, grads, got_early, got_late


HBM_SPEC = pl.BlockSpec(memory_space=pltpu.HBM)


def _mesh_pos():
    x, y, c = lax.axis_index("x"), lax.axis_index("y"), lax.axis_index("c")
    return x, y, c, 4 * x + 2 * y + c


def _peer(k):
    x, y, c, _ = _mesh_pos()
    px = 1 - x if k & 4 else x
    py = 1 - y if k & 2 else y
    pc = 1 - c if k & 1 else c
    return (px, py, pc), 4 * px + 2 * py + pc


def _rcopy(src, dst, send, recv, a, s, peer):
    return pltpu.make_async_remote_copy(src_ref=src, dst_ref=dst, send_sem=send.at[a, s], recv_sem=recv.at[a, s],
                                        device_id=peer, device_id_type=pl.DeviceIdType.MESH)


class _Gather:
    ICI = (2, 4, 6)

    def __init__(self, shards):
        self.args = list(shards)
        self.n = len(shards)
        self.out_shape = [jax.ShapeDtypeStruct((N_DEV,) + s.shape, s.dtype) for s in shards]
        self.scratch = [pltpu.SemaphoreType.DMA((self.n, N_DEV - 1)), pltpu.SemaphoreType.DMA((self.n, N_DEV - 1)),
                        pltpu.SemaphoreType.DMA((self.n,))]

    def _first(self, ins, outs, send, recv, a):
        me = _mesh_pos()[3]
        out, got = [], []
        for s, k in enumerate((1,) + self.ICI):
            peer, pidx = _peer(k)
            out.append(_rcopy(ins[a], outs[a].at[me], send, recv, a, s, peer))
            got.append(_rcopy(ins[a], outs[a].at[pidx], send, recv, a, s, peer))
        return out, got

    def _forward(self, ins, outs, send, recv, a):
        sib = _peer(1)[0]
        out, got = [], []
        for s, k in enumerate(self.ICI):
            held = outs[a].at[_peer(k)[1]]
            out.append(_rcopy(held, held, send, recv, a, 4 + s, sib))
            other = outs[a].at[_peer(k | 1)[1]]
            got.append(_rcopy(other, other, send, recv, a, 4 + s, sib))
        return out, got

    def start(self, ins, outs, sems):
        send, recv, loc = sems
        me = _mesh_pos()[3]
        for a in range(self.n):
            pltpu.make_async_copy(ins[a], outs[a].at[me], loc.at[a]).start()
            for cp in self._first(ins, outs, send, recv, a)[0]:
                cp.start()

    def mid(self, ins, outs, sems):
        send, recv, _ = sems
        for a in range(self.n):
            arrivals = self._first(ins, outs, send, recv, a)[1]
            for s, cp in enumerate(self._forward(ins, outs, send, recv, a)[0]):
                arrivals[1 + s].wait_recv()
                cp.start()

    def finish(self, ins, outs, sems):
        send, recv, loc = sems
        me = _mesh_pos()[3]
        for a in range(self.n):
            first_out, first_got = self._first(ins, outs, send, recv, a)
            fwd_out, fwd_got = self._forward(ins, outs, send, recv, a)
            first_got[0].wait_recv()
            for cp in fwd_got:
                cp.wait_recv()
            for cp in first_out + fwd_out:
                cp.wait_send()
            pltpu.make_async_copy(ins[a], outs[a].at[me], loc.at[a]).wait()


class _Exchange:
    def __init__(self, slabs=(), gathered=(), chip_slabs=(), sibling_slabs=()):
        self.args = list(slabs) + list(chip_slabs) + list(sibling_slabs) + list(gathered)
        self.kind = (["dev"] * len(slabs) + ["chip"] * len(chip_slabs) + ["sib"] * len(sibling_slabs)
                     + ["all"] * len(gathered))
        self.n = len(self.args)
        half = lambda s: jax.ShapeDtypeStruct((N_DEV // 2,) + s.shape[1:], s.dtype)
        self.out_shape = ([jax.ShapeDtypeStruct(s.shape, s.dtype) for s in slabs]
                          + [half(s) for s in chip_slabs] + [half(s) for s in sibling_slabs]
                          + [jax.ShapeDtypeStruct((N_DEV,) + s.shape, s.dtype) for s in gathered])
        self.scratch = [pltpu.SemaphoreType.DMA((self.n, N_DEV - 1)), pltpu.SemaphoreType.DMA((self.n, N_DEV - 1)),
                        pltpu.SemaphoreType.DMA((self.n,))]

    def _copies(self, ins, outs, send, recv, a):
        x, y, c, me = _mesh_pos()
        kind = self.kind[a]
        out, got = [], []
        if kind == "sib":
            sib = _peer(1)[0]
            for q in range(N_DEV // 2):
                out.append(_rcopy(ins[a].at[2 * q + 1 - c], outs[a].at[q], send, recv, a, q, sib))
                got.append(_rcopy(ins[a].at[2 * q + c], outs[a].at[q], send, recv, a, q, sib))
            return out, got
        for k in ((2, 4, 6) if kind == "chip" else range(1, N_DEV)):
            peer, pidx = _peer(k)
            if kind == "chip":
                src, mine, theirs = ins[a].at[2 * peer[0] + peer[1]], 2 * x + y, 2 * peer[0] + peer[1]
            else:
                src, mine, theirs = (ins[a].at[pidx] if kind == "dev" else ins[a]), me, pidx
            out.append(_rcopy(src, outs[a].at[mine], send, recv, a, k - 1, peer))
            got.append(_rcopy(src, outs[a].at[theirs], send, recv, a, k - 1, peer))
        return out, got

    def _local(self, ins, outs, loc, a):
        x, y, _, me = _mesh_pos()
        kind = self.kind[a]
        if kind == "sib":
            return None
        if kind == "chip":
            return pltpu.make_async_copy(ins[a].at[2 * x + y], outs[a].at[2 * x + y], loc.at[a])
        return pltpu.make_async_copy(ins[a].at[me] if kind == "dev" else ins[a], outs[a].at[me], loc.at[a])

    def start(self, ins, outs, sems):
        send, recv, loc = sems
        for a in range(self.n):
            if self._local(ins, outs, loc, a) is not None:
                self._local(ins, outs, loc, a).start()
            for cp in self._copies(ins, outs, send, recv, a)[0]:
                cp.start()

    def mid(self, ins, outs, sems):
        pass

    def finish(self, ins, outs, sems):
        send, recv, loc = sems
        for a in range(self.n):
            out, got = self._copies(ins, outs, send, recv, a)
            for cp in got:
                cp.wait_recv()
            for cp in out:
                cp.wait_send()
            if self._local(ins, outs, loc, a) is not None:
                self._local(ins, outs, loc, a).wait()


def _comm_call(comm, name):
    n = comm.n

    def body(*refs):
        ins, outs, sems = refs[:n], refs[n:2 * n], refs[2 * n:]
        comm.start(ins, outs, sems)
        comm.mid(ins, outs, sems)
        comm.finish(ins, outs, sems)

    return pl.pallas_call(
        body, name=name, out_shape=comm.out_shape, in_specs=[HBM_SPEC] * n, out_specs=[HBM_SPEC] * n,
        scratch_shapes=comm.scratch,
    )(*comm.args)


def _hosted(body, comm, n_in, n_out, when):
    if comm is None:
        return body

    def wrapped(*refs):
        ins, c_ins = refs[:n_in], refs[n_in:n_in + comm.n]
        o0 = n_in + comm.n
        outs, c_outs = refs[o0:o0 + n_out], refs[o0 + n_out:o0 + n_out + comm.n]
        scratch, sems = refs[o0 + n_out + comm.n:len(refs) - 3], refs[len(refs) - 3:]
        first, middle, last = when()

        @pl.when(first)
        def _():
            comm.start(c_ins, c_outs, sems)

        body(*ins, *outs, *scratch)

        @pl.when(middle)
        def _():
            comm.mid(c_ins, c_outs, sems)

        @pl.when(last)
        def _():
            comm.finish(c_ins, c_outs, sems)

    return wrapped


def _host_call(body, name, comm, when, out_shape, grid, in_specs, out_specs, scratch_shapes, sem, args):
    n_in, n_out = len(in_specs), len(out_specs)
    if comm is None:
        res = pl.pallas_call(body, name=name, out_shape=out_shape, grid=grid, in_specs=in_specs, out_specs=out_specs,
                             scratch_shapes=scratch_shapes, compiler_params=_params(sem))(*args)
        return list(res), []
    res = pl.pallas_call(
        _hosted(body, comm, n_in, n_out, when), name=name,
        out_shape=list(out_shape) + comm.out_shape, grid=grid,
        in_specs=list(in_specs) + [HBM_SPEC] * comm.n, out_specs=list(out_specs) + [HBM_SPEC] * comm.n,
        scratch_shapes=list(scratch_shapes) + comm.scratch,
        compiler_params=_params(("arbitrary",) * len(grid)),
    )(*args, *comm.args)
    return list(res[:n_out]), list(res[n_out:])


def _add_slabs(a, b, name):
    n, rows, cols = a.shape
    tc = _tile(cols, 256)

    def body(a_ref, b_ref, o_ref):
        o_ref[...] = (a_ref[...].astype(F32) + b_ref[...].astype(F32)).astype(o_ref.dtype)

    blk = pl.BlockSpec((None, rows, tc), lambda i, j: (i, 0, j))
    return pl.pallas_call(
        body, name=name, out_shape=jax.ShapeDtypeStruct(a.shape, a.dtype), grid=(n, cols // tc),
        in_specs=[blk, blk], out_specs=blk, compiler_params=_params(("parallel", "parallel")),
    )(a, b)


def _adamw(parts, w, m, v, name):
    rows, cols = w.shape
    nparts = parts.shape[0]
    tr, tc = rows, cols
    for cand in (128, 176):
        if rows > cand and rows % cand == 0:
            tr = cand
            break
    if tr == rows and rows > 512:
        tc = _tile(cols, 256)

    def body(p_ref, w_ref, m_ref, v_ref, g_ref, d_ref, mo_ref, vo_ref):
        g = p_ref[0].astype(F32)
        for s in range(1, nparts):
            g = g + p_ref[s].astype(F32)
        mn = ADAM_B1 * m_ref[...] + (1.0 - ADAM_B1) * g
        vn = ADAM_B2 * v_ref[...] + (1.0 - ADAM_B2) * (g * g)
        m_hat = mn / (1.0 - ADAM_B1 ** ADAM_STEP)
        v_hat = vn / (1.0 - ADAM_B2 ** ADAM_STEP)
        g_ref[...] = g
        d_ref[...] = -ADAM_LR * (m_hat / (jnp.sqrt(v_hat) + ADAM_EPS) + ADAM_WD * w_ref[...])
        mo_ref[...] = mn
        vo_ref[...] = vn

    blk = pl.BlockSpec((tr, tc), lambda i, j: (i, j))
    out = jax.ShapeDtypeStruct((rows, cols), F32)
    return pl.pallas_call(
        body, name=name,
        out_shape=(out, out, out, out),
        grid=(rows // tr, cols // tc),
        in_specs=[pl.BlockSpec((nparts, tr, tc), lambda i, j: (0, i, j)), blk, blk, blk],
        out_specs=(blk, blk, blk, blk),
        compiler_params=_params(("parallel", "parallel")),
    )(parts, w, m, v)


CONV_PACK = 8 * 1024
WEIGHT_ORDER = ("norm1_w", "w_in", "dn_conv_w", "dn_A_log", "dn_dt_bias", "dn_norm_w", "w_proj_dn", "w_proj_sb",
                "w_out", "norm2_w", "ffn_w_up", "ffn_conv_w", "ffn_w_down", "norm_f_w")


def _cols_to_slabs(g):
    r, c8 = g.shape
    return g.reshape(r, N_DEV, c8 // N_DEV).transpose(1, 0, 2)


def _slabs_to_cols(s):
    d, r, c = s.shape
    return s.transpose(1, 0, 2).reshape(r, d * c)


def kernel(x, norm1_w, w_in, dn_conv_w, dn_A_log, dn_dt_bias, dn_norm_w, w_proj_dn, w_proj_sb, w_out, norm2_w, ffn_w_up, ffn_conv_w, ffn_w_down, norm_f_w, loss_target, m_norm1_w, m_w_in, m_dn_conv_w, m_dn_A_log, m_dn_dt_bias, m_dn_norm_w, m_w_proj_dn, m_w_proj_sb, m_w_out, m_norm2_w, m_ffn_w_up, m_ffn_conv_w, m_ffn_w_down, m_norm_f_w, v_norm1_w, v_w_in, v_dn_conv_w, v_dn_A_log, v_dn_dt_bias, v_dn_norm_w, v_w_proj_dn, v_w_proj_sb, v_w_out, v_norm2_w, v_ffn_w_up, v_ffn_conv_w, v_ffn_w_down, v_norm_f_w):
    me = _mesh_pos()[3]
    tr = lambda a: jnp.transpose(a[0])
    w_loc = dict(norm1_w=norm1_w, w_in=tr(w_in), dn_conv_w=dn_conv_w[0], dn_A_log=dn_A_log, dn_dt_bias=dn_dt_bias,
                 dn_norm_w=dn_norm_w, w_proj_dn=w_proj_dn[0], w_proj_sb=w_proj_sb[0], w_out=w_out[0],
                 norm2_w=norm2_w, ffn_w_up=tr(ffn_w_up), ffn_conv_w=ffn_conv_w[0], ffn_w_down=ffn_w_down[0],
                 norm_f_w=norm_f_w[None, :])
    m_loc = dict(norm1_w=m_norm1_w, w_in=tr(m_w_in), dn_conv_w=m_dn_conv_w[0], dn_A_log=m_dn_A_log,
                 dn_dt_bias=m_dn_dt_bias, dn_norm_w=m_dn_norm_w, w_proj_dn=m_w_proj_dn[0], w_proj_sb=m_w_proj_sb[0],
                 w_out=m_w_out[0], norm2_w=m_norm2_w, ffn_w_up=tr(m_ffn_w_up), ffn_conv_w=m_ffn_conv_w[0],
                 ffn_w_down=m_ffn_w_down[0], norm_f_w=m_norm_f_w[None, :])
    v_loc = dict(norm1_w=v_norm1_w, w_in=tr(v_w_in), dn_conv_w=v_dn_conv_w[0], dn_A_log=v_dn_A_log,
                 dn_dt_bias=v_dn_dt_bias, dn_norm_w=v_dn_norm_w, w_proj_dn=v_w_proj_dn[0], w_proj_sb=v_w_proj_sb[0],
                 w_out=v_w_out[0], norm2_w=v_norm2_w, ffn_w_up=tr(v_ffn_w_up), ffn_conv_w=v_ffn_conv_w[0],
                 ffn_w_down=v_ffn_w_down[0], norm_f_w=v_norm_f_w[None, :])

    conv_flat = jnp.concatenate([w_loc["dn_conv_w"].reshape(-1), w_loc["ffn_conv_w"].reshape(-1)])
    n_dn, n_ffn = DN_CONV * 3 * WIDTH // N_DEV, FFN_CONV * 2 * D_FF // N_DEV
    conv_pack = jnp.pad(conv_flat, (0, CONV_PACK - n_dn - n_ffn)).reshape(8, 1024)
    g_in, g_conv = _comm_call(_Gather([w_loc["w_in"].astype(BF16), conv_pack]), "gather_first")
    in_width = g_in.shape[0] * g_in.shape[1]
    w_in_t = g_in.reshape(in_width, D_MODEL)
    g_conv = g_conv.reshape(N_DEV, CONV_PACK)
    dn_conv_full = _slabs_to_cols(g_conv[:, :n_dn].reshape(N_DEV, DN_CONV, 3 * WIDTH // N_DEV))
    ffn_conv_full = _slabs_to_cols(g_conv[:, n_dn:n_dn + n_ffn].reshape(N_DEV, FFN_CONV, 2 * D_FF // N_DEV))
    q_end = 3 * WIDTH
    ab_end = q_end + 2 * HEADS
    gate_end = ab_end + WIDTH
    sb_end = gate_end + 3 * WIDTH
    pad_lanes = lambda a: jnp.pad(a, ((0, 0), (0, 128 - a.shape[1])))
    wts = dict(
        norm1_w=norm1_w, w_dnqkv_t=w_in_t[:q_end], w_ab_t=jnp.pad(w_in_t[q_end:ab_end], ((0, 128 - 2 * HEADS), (0, 0))),
        w_dngate_t=w_in_t[ab_end:gate_end], w_sbqkv_t=w_in_t[gate_end:sb_end], w_gl_t=w_in_t[sb_end:],
        dn_conv_w=dn_conv_full, alog=pad_lanes(dn_A_log), dtb=pad_lanes(dn_dt_bias), dn_norm_w=dn_norm_w,
        norm2_w=norm2_w, ffn_conv_w=ffn_conv_full, norm_f_w=norm_f_w[None, :])

    n_fc = FFN_CONV * 2 * D_FF
    fc_rows = -(-n_fc // D_MODEL)
    dn_rows = DN_CONV * 3 * WIDTH // D_MODEL
    late_names = ("w_proj_dn", "w_proj_sb", "w_out", "ffn_w_up", "ffn_w_down")

    class Plan:
        @staticmethod
        def late_gather():
            return _Gather([w_loc[k].astype(BF16) for k in late_names])

        @staticmethod
        def late_weights(got):
            g_pd, g_ps, g_out, g_up, g_down = got
            return dict(w_proj_dn=g_pd.reshape(WIDTH, D_MODEL), w_proj_sb=g_ps.reshape(WIDTH, D_MODEL),
                        w_out=g_out.reshape(D_MODEL, D_MODEL), ffn_w_up_t=g_up.reshape(2 * D_FF, D_MODEL),
                        ffn_w_down=g_down.reshape(D_FF, D_MODEL))

        @staticmethod
        def early_grads(g):
            return _Exchange([g["w_proj_dn"].reshape(N_DEV, WIDTH // N_DEV, D_MODEL),
                              g["w_proj_sb"].reshape(N_DEV, WIDTH // N_DEV, D_MODEL),
                              g["w_out"].reshape(N_DEV, D_MODEL // N_DEV, D_MODEL),
                              g["ffn_w_up_t"].reshape(N_DEV, 2 * D_FF // N_DEV, D_MODEL),
                              g["ffn_w_down"].reshape(N_DEV, D_FF // N_DEV, D_MODEL)])

        @staticmethod
        def _in_slabs(g):
            g_win_t = jnp.concatenate([g["w_main_t"][:q_end], g["w_ab_t"][:2 * HEADS], g["w_main_t"][q_end:]],
                                      axis=0)
            return g_win_t.reshape(N_DEV, in_width // N_DEV, D_MODEL)

        @staticmethod
        def sibling_swap(g):
            return _Exchange(sibling_slabs=[Plan._in_slabs(g)])

        @staticmethod
        def late_grads(swapped, g, loss):
            slabs = Plan._in_slabs(g)
            mine = lax.dynamic_index_in_dim(slabs.reshape((N_DEV // 2, 2) + slabs.shape[1:]), _mesh_pos()[2],
                                            axis=1, keepdims=False)
            chip_sums = _add_slabs(mine, swapped[0], "in_dw_chip_sum")
            row3 = jnp.concatenate([g["dn_norm_w"], g["alog"], g["dtb"], jnp.pad(loss, ((0, 0), (0, 127))),
                                    jnp.zeros((1, D_MODEL - 512), F32)], axis=1)
            fconv_rows = jnp.pad(g["ffn_conv_w"].reshape(-1), (0, fc_rows * D_MODEL - n_fc)).reshape(fc_rows, D_MODEL)
            pad8 = lambda a: jnp.pad(a, ((0, -a.shape[0] % 8), (0, 0)))
            pieces = [g["norm2_w"], g["norm_f_w"], row3, g["dn_conv_w"].reshape(dn_rows, D_MODEL), fconv_rows]
            small = jnp.concatenate([pad8(a) for a in pieces], axis=0)
            assert small.shape[0] == SMALL_ROWS
            return _Exchange(chip_slabs=[chip_sums], gathered=[small])

    loss, grad_x, g, got_early, got_late = _local_step(x[0], loss_target[0], wts, Plan)
    r_pd, r_ps, r_out, r_up, r_down = got_early
    r_in, r_small = got_late
    (r_norm1,) = _comm_call(_Exchange([], [jnp.pad(g["norm1_w"], ((0, 7), (0, 0)))]), "gather_norm1")

    parts = dict(w_in=r_in, w_proj_dn=r_pd, w_proj_sb=r_ps, w_out=r_out, ffn_w_up=r_up, ffn_w_down=r_down)
    parts["norm1_w"] = r_norm1[:, 0:1, :]
    parts["norm2_w"] = r_small[:, 0:1, :]
    parts["norm_f_w"] = r_small[:, 8:9, :]
    parts["dn_norm_w"] = r_small[:, 16:17, 0:HEAD_DIM]
    parts["dn_A_log"] = r_small[:, 16:17, 128:128 + HEADS]
    parts["dn_dt_bias"] = r_small[:, 16:17, 256:256 + HEADS]
    dnc = r_small[:, 24:24 + dn_rows, :].reshape(N_DEV, DN_CONV, 3 * WIDTH)
    parts["dn_conv_w"] = lax.dynamic_slice_in_dim(dnc, me * (3 * WIDTH // N_DEV), 3 * WIDTH // N_DEV, axis=2)
    fc0 = 24 + dn_rows + (-dn_rows % 8)
    fcc = r_small[:, fc0:fc0 + fc_rows, :].reshape(N_DEV, fc_rows * D_MODEL)[:, :n_fc]
    fcc = fcc.reshape(N_DEV, FFN_CONV, 2 * D_FF)
    parts["ffn_conv_w"] = lax.dynamic_slice_in_dim(fcc, me * (2 * D_FF // N_DEV), 2 * D_FF // N_DEV, axis=2)
    loss_total = jnp.sum(r_small[:, 16, 384])

    res = {k: _adamw(parts[k], w_loc[k], m_loc[k], v_loc[k], "adamw_" + k) for k in WEIGHT_ORDER}
    lead = ("w_in", "dn_conv_w", "w_proj_dn", "w_proj_sb", "w_out", "ffn_w_up", "ffn_conv_w", "ffn_w_down")

    def shaped(k, a):
        if k in ("w_in", "ffn_w_up"):
            return jnp.transpose(a)[None]
        if k in lead:
            return a[None]
        if k == "norm_f_w":
            return a[0]
        return a

    outs = [loss_total, grad_x[None]]
    for idx in range(4):
        outs += [shaped(k, res[k][idx]) for k in WEIGHT_ORDER]
    return tuple(outs)
```

```python
import functools

import jax
import jax.numpy as jnp
from jax import lax
from jax.experimental import pallas as pl
from jax.experimental.pallas import tpu as pltpu

F32 = jnp.float32
BF16 = jnp.bfloat16

N_DEV = 8
D_MODEL = 1024
HEADS = 8
HEAD_DIM = 128
WIDTH = HEADS * HEAD_DIM
DN_CONV = 4
DN_CHUNK = 64
D_FF = 2816
FFN_CONV = 3
EPS = 1e-6
HALO = 16
CHUNK_ROWS = 256
ATT_BLOCK = 256
SB_LOG_ZERO = -104.0
SB_GROUP = 2
SB_HEADS_FWD = 4
SB_HEADS_BWD = 2
SMALL_ROWS = 64

ADAM_LR = 0.001
ADAM_B1 = 0.9
ADAM_B2 = 0.999
ADAM_EPS = 1e-08
ADAM_WD = 0.01
ADAM_STEP = 10

VMEM_LIMIT = 48 * 1024 * 1024


def _params(sem=None, **kw):
    return pltpu.CompilerParams(dimension_semantics=sem, vmem_limit_bytes=VMEM_LIMIT, **kw)


def _tile(n, cap):
    if n <= cap:
        return n
    best = None
    for t in range(128, cap + 1, 128):
        if n % t == 0:
            best = t
    assert best is not None, (n, cap)
    return best


def _dot(a, b, dims):
    return lax.dot_general(a, b, ((dims[0], dims[1]), ((), ())), preferred_element_type=F32)


NN = ((1,), (0,))
NT = ((1,), (1,))
TN = ((0,), (0,))


def _dotb(a, b, dims):
    return _dot(a.astype(BF16), b.astype(BF16), dims)


def _split3(x):
    h1 = x.astype(BF16)
    r1 = x - h1.astype(F32)
    h2 = r1.astype(BF16)
    r2 = r1 - h2.astype(F32)
    return h1, h2, r2.astype(BF16)


def _dot_xr(a, b_exact, dims):
    a1, a2, a3 = _split3(a)
    return _dot(a1, b_exact, dims) + _dot(a2, b_exact, dims) + _dot(a3, b_exact, dims)


def _split2(x):
    h1 = x.astype(BF16)
    return h1, (x - h1.astype(F32)).astype(BF16)


def _dot_xr2(a, b_exact, dims):
    a1, a2 = _split2(a)
    return _dot(a1, b_exact, dims) + _dot(a2, b_exact, dims)


def _dot_xl(a_exact, b, dims):
    b1, b2, b3 = _split3(b)
    return _dot(a_exact, b1, dims) + _dot(a_exact, b2, dims) + _dot(a_exact, b3, dims)


def _dot3(a, b, dims):
    a1 = a.astype(BF16)
    a2 = (a - a1.astype(F32)).astype(BF16)
    b1 = b.astype(BF16)
    b2 = (b - b1.astype(F32)).astype(BF16)
    return _dot(a1, b1, dims) + (_dot(a1, b2, dims) + _dot(a2, b1, dims))


def _sigmoid(x):
    return 1.0 / (1.0 + jnp.exp(-x))


def _log1pexp_neg_abs(x):
    return jnp.log(1.0 + jnp.exp(-jnp.abs(x)))


def _iota(shape, dim):
    return lax.broadcasted_iota(jnp.int32, shape, dim)


def _matmul(a, b, mode, out_dtype, name, add=None, comm=None):
    if mode == "nn":
        (m, k), (k2, n) = a.shape, b.shape
    elif mode == "nt":
        (m, k), (n, k2) = a.shape, b.shape
    else:
        (k, m), (k2, n) = a.shape, b.shape
    assert k == k2, (a.shape, b.shape, mode)
    tm, tn, tk = _tile(m, 1408), _tile(n, 1408), _tile(k, 1536)
    nk = k // tk
    dims = {"nn": NN, "nt": NT, "tn": TN}[mode]

    def body(*refs):
        if add is None:
            a_ref, b_ref, o_ref, acc_ref = refs
        else:
            a_ref, b_ref, add_ref, o_ref, acc_ref = refs
        kk = pl.program_id(2)

        @pl.when(kk == 0)
        def _():
            acc_ref[...] = jnp.zeros_like(acc_ref)

        acc_ref[...] += _dotb(a_ref[...], b_ref[...], dims)

        @pl.when(kk == nk - 1)
        def _():
            r = acc_ref[...]
            if add is not None:
                r = r + add_ref[...].astype(F32)
            o_ref[...] = r.astype(out_dtype)

    if mode == "nn":
        specs = [pl.BlockSpec((tm, tk), lambda i, j, l: (i, l)), pl.BlockSpec((tk, tn), lambda i, j, l: (l, j))]
    elif mode == "nt":
        specs = [pl.BlockSpec((tm, tk), lambda i, j, l: (i, l)), pl.BlockSpec((tn, tk), lambda i, j, l: (j, l))]
    else:
        specs = [pl.BlockSpec((tk, tm), lambda i, j, l: (l, i)), pl.BlockSpec((tk, tn), lambda i, j, l: (l, j))]
    args = [a, b]
    if add is not None:
        specs.append(pl.BlockSpec((tm, tn), lambda i, j, l: (i, j)))
        args.append(add)
    grid = (m // tm, n // tn, nk)

    def when():
        i, j, l = pl.program_id(0), pl.program_id(1), pl.program_id(2)
        first = jnp.logical_and(jnp.logical_and(i == 0, j == 0), l == 0)
        last = jnp.logical_and(jnp.logical_and(i == grid[0] - 1, j == grid[1] - 1), l == nk - 1)
        return first, last, last

    (out,), extra = _host_call(
        body, name, comm, when, [jax.ShapeDtypeStruct((m, n), out_dtype)], grid, specs,
        [pl.BlockSpec((tm, tn), lambda i, j, l: (i, j))], [pltpu.VMEM((tm, tn), F32)],
        ("parallel", "parallel", "arbitrary"), args)
    return out if comm is None else (out, extra)


def _rmsnorm_fwd(x, w, name):
    t, d = x.shape
    tr = _tile(t, 512)

    def body(x_ref, w_ref, o_ref):
        xv = x_ref[...]
        r = lax.rsqrt(jnp.mean(xv * xv, axis=1, keepdims=True) + EPS)
        o_ref[...] = (xv * r * w_ref[...]).astype(BF16)

    return pl.pallas_call(
        body, name=name,
        out_shape=jax.ShapeDtypeStruct((t, d), BF16),
        grid=(t // tr,),
        in_specs=[pl.BlockSpec((tr, d), lambda i: (i, 0)), pl.BlockSpec((1, d), lambda i: (0, 0))],
        out_specs=pl.BlockSpec((tr, d), lambda i: (i, 0)),
        compiler_params=_params(("parallel",)),
    )(x, w)


def _rmsnorm_bwd(dn, x, w, dres, name):
    t, d = x.shape
    tr = _tile(t, 512)

    def body(dn_ref, x_ref, w_ref, dres_ref, dx_ref, dw_ref):
        i = pl.program_id(0)
        xv = x_ref[...]
        g = dn_ref[...].astype(F32)
        r = lax.rsqrt(jnp.mean(xv * xv, axis=1, keepdims=True) + EPS)
        xh = xv * r
        dxh = g * w_ref[...]
        dx = r * (dxh - xh * jnp.mean(dxh * xh, axis=1, keepdims=True))
        dx_ref[...] = dres_ref[...] + dx

        @pl.when(i == 0)
        def _():
            dw_ref[...] = jnp.zeros_like(dw_ref)

        dw_ref[...] += jnp.sum(g * xh, axis=0, keepdims=True)

    return pl.pallas_call(
        body, name=name,
        out_shape=(jax.ShapeDtypeStruct((t, d), F32), jax.ShapeDtypeStruct((1, d), F32)),
        grid=(t // tr,),
        in_specs=[pl.BlockSpec((tr, d), lambda i: (i, 0)), pl.BlockSpec((tr, d), lambda i: (i, 0)),
                  pl.BlockSpec((1, d), lambda i: (0, 0)), pl.BlockSpec((tr, d), lambda i: (i, 0))],
        out_specs=(pl.BlockSpec((tr, d), lambda i: (i, 0)), pl.BlockSpec((1, d), lambda i: (0, 0))),
        compiler_params=_params(("arbitrary",)),
    )(dn, x, w, dres)


def _final_loss(x2, target, w, name):
    t, d = x2.shape
    tr = _tile(t, 512)

    def body(x_ref, t_ref, w_ref, dx_ref, dw_ref, loss_ref):
        i = pl.program_id(0)
        xv = x_ref[...]
        r = lax.rsqrt(jnp.mean(xv * xv, axis=1, keepdims=True) + EPS)
        xh = xv * r
        err = xh * w_ref[...] - t_ref[...]
        dy = err * (1.0 / d)
        dxh = dy * w_ref[...]
        dx_ref[...] = r * (dxh - xh * jnp.mean(dxh * xh, axis=1, keepdims=True))

        @pl.when(i == 0)
        def _():
            dw_ref[...] = jnp.zeros_like(dw_ref)
            loss_ref[...] = jnp.zeros_like(loss_ref)

        dw_ref[...] += jnp.sum(dy * xh, axis=0, keepdims=True)
        row = jnp.sum(err * err, axis=1, keepdims=True) * (0.5 / d)
        loss_ref[...] += jnp.sum(row, axis=0, keepdims=True)

    return pl.pallas_call(
        body, name=name,
        out_shape=(jax.ShapeDtypeStruct((t, d), F32), jax.ShapeDtypeStruct((1, d), F32),
                   jax.ShapeDtypeStruct((1, 1), F32)),
        grid=(t // tr,),
        in_specs=[pl.BlockSpec((tr, d), lambda i: (i, 0)), pl.BlockSpec((tr, d), lambda i: (i, 0)),
                  pl.BlockSpec((1, d), lambda i: (0, 0))],
        out_specs=(pl.BlockSpec((tr, d), lambda i: (i, 0)), pl.BlockSpec((1, d), lambda i: (0, 0)),
                   pl.BlockSpec((1, 1), lambda i: (0, 0))),
        compiler_params=_params(("arbitrary",)),
    )(x2, target, w)


def _shift_down(cur, prev, k, row):
    r = pltpu.roll(cur, k, 0)
    top, row8 = r[0:8, :], row[0:8, :]
    for m in range(k):
        top = jnp.where(row8 == m, prev[HALO - k + m:HALO - k + m + 1, :], top)
    return jnp.concatenate([top, r[8:, :]], axis=0)


def _shift_up(cur, nxt, k, row, tr):
    r = pltpu.roll(cur, tr - k, 0)
    bottom, row8 = r[tr - 8:, :], row[0:8, :]
    for m in range(k):
        bottom = jnp.where(row8 == 8 - k + m, nxt[m:m + 1, :], bottom)
    return jnp.concatenate([r[:tr - 8, :], bottom], axis=0)


def _fold8(a):
    out = a[0:8, :]
    for r in range(8, a.shape[0], 8):
        out = out + a[r:r + 8, :]
    return out


def _conv_taps(cur, prev, w, ntaps, row):
    taps = [cur if i == ntaps - 1 else _shift_down(cur, prev, ntaps - 1 - i, row) for i in range(ntaps)]
    y = w[0:1, :] * taps[0]
    for i in range(1, ntaps):
        y = y + w[i:i + 1, :] * taps[i]
    return taps, y


def _conv_bwd_data(parts, w, ntaps, out_dtype, name):
    t, chp = parts[0].shape
    npart = len(parts)
    tr, tc = _tile(t, 512), _tile(chp, 1408)
    nc = chp // tc
    nhalo = t // HALO
    last = t // tr - 1

    def body(*refs):
        cur_refs, nxt_refs = refs[:npart], refs[npart:2 * npart]
        w_ref, o_ref = refs[2 * npart], refs[2 * npart + 1]
        i, j = pl.program_id(0), pl.program_id(1)
        row = _iota((tr, 128), 0)
        for c0 in range(0, tc, 128):
            sl = slice(c0, c0 + 128)
            cur, nxt = cur_refs[0][:, sl].astype(F32), nxt_refs[0][:, sl].astype(F32)
            for p in range(1, npart):
                cur = jnp.where(j >= p * nc, cur_refs[p][:, sl].astype(F32), cur)
                nxt = jnp.where(j >= p * nc, nxt_refs[p][:, sl].astype(F32), nxt)
            nxt = jnp.where(i == last, 0.0, nxt)
            wv = w_ref[:, sl]
            y = wv[ntaps - 1:ntaps, :] * cur
            for k in range(1, ntaps):
                y = y + wv[ntaps - 1 - k:ntaps - k, :] * _shift_up(cur, nxt, k, row, tr)
            o_ref[:, sl] = y.astype(out_dtype)

    col = lambda p: (lambda j: jnp.clip(j - p * nc, 0, nc - 1))
    cur_specs = [pl.BlockSpec((tr, tc), lambda i, j, c=col(p): (i, c(j))) for p in range(npart)]
    nxt_specs = [pl.BlockSpec((HALO, tc),
                              lambda i, j, c=col(p): (jnp.minimum((i + 1) * (tr // HALO), nhalo - 1), c(j)))
                 for p in range(npart)]
    return pl.pallas_call(
        body, name=name,
        out_shape=jax.ShapeDtypeStruct((t, npart * chp), out_dtype),
        grid=(t // tr, npart * nc),
        in_specs=cur_specs + nxt_specs + [pl.BlockSpec((ntaps, tc), lambda i, j: (0, j))],
        out_specs=pl.BlockSpec((tr, tc), lambda i, j: (i, j)),
        compiler_params=_params(("parallel", "parallel")),
    )(*parts, *parts, w)


def _ffn_act_fwd(upre, cw, name):
    t = upre.shape[0]
    tr, tc = _tile(t, 512), _tile(D_FF, 1408)
    nj = D_FF // tc

    def body(g_ref, gp_ref, u_ref, up_ref, wg_ref, wu_ref, o_ref):
        i = pl.program_id(0)
        row = _iota((tr, 128), 0)
        for c0 in range(0, tc, 128):
            sl = slice(c0, c0 + 128)
            gp = jnp.where(i == 0, 0.0, gp_ref[:, sl].astype(F32))
            up = jnp.where(i == 0, 0.0, up_ref[:, sl].astype(F32))
            _, gc = _conv_taps(g_ref[:, sl].astype(F32), gp, wg_ref[:, sl], FFN_CONV, row)
            _, uc = _conv_taps(u_ref[:, sl].astype(F32), up, wu_ref[:, sl], FFN_CONV, row)
            o_ref[:, sl] = (gc * _sigmoid(gc) * uc).astype(BF16)

    prev = lambda off: (lambda i, j: (jnp.maximum(i * (tr // HALO) - 1, 0), j + off))
    return pl.pallas_call(
        body, name=name,
        out_shape=jax.ShapeDtypeStruct((t, D_FF), BF16),
        grid=(t // tr, nj),
        in_specs=[pl.BlockSpec((tr, tc), lambda i, j: (i, j)), pl.BlockSpec((HALO, tc), prev(0)),
                  pl.BlockSpec((tr, tc), lambda i, j: (i, j + nj)), pl.BlockSpec((HALO, tc), prev(nj)),
                  pl.BlockSpec((FFN_CONV, tc), lambda i, j: (0, j)),
                  pl.BlockSpec((FFN_CONV, tc), lambda i, j: (0, j + nj))],
        out_specs=pl.BlockSpec((tr, tc), lambda i, j: (i, j)),
        compiler_params=_params(("parallel", "parallel")),
    )(upre, upre, upre, upre, cw, cw)


def _ffn_act_bwd(dact, upre, cw, name):
    t = upre.shape[0]
    tr, tc = _tile(t, 256), _tile(D_FF, 1408)
    nj = D_FF // tc

    def body(da_ref, g_ref, gp_ref, u_ref, up_ref, wg_ref, wu_ref, dg_ref, du_ref, dwg_ref, dwu_ref):
        i = pl.program_id(1)
        row = _iota((CHUNK_ROWS, 128), 0)

        @pl.when(i == 0)
        def _():
            dwg_ref[...] = jnp.zeros_like(dwg_ref)
            dwu_ref[...] = jnp.zeros_like(dwu_ref)

        for c0 in range(0, tc, 128):
            sl = slice(c0, c0 + 128)
            wg, wu = wg_ref[:, sl], wu_ref[:, sl]
            dwg = [jnp.zeros((8, 128), F32)] * FFN_CONV
            dwu = [jnp.zeros((8, 128), F32)] * FFN_CONV
            for r0 in range(0, tr, CHUNK_ROWS):
                rows = slice(r0, r0 + CHUNK_ROWS)
                if r0 == 0:
                    gp = jnp.where(i == 0, 0.0, gp_ref[:, sl].astype(F32))
                    up = jnp.where(i == 0, 0.0, up_ref[:, sl].astype(F32))
                else:
                    gp = g_ref[r0 - HALO:r0, sl].astype(F32)
                    up = u_ref[r0 - HALO:r0, sl].astype(F32)
                gt, gc = _conv_taps(g_ref[rows, sl].astype(F32), gp, wg, FFN_CONV, row)
                ut, uc = _conv_taps(u_ref[rows, sl].astype(F32), up, wu, FFN_CONV, row)
                da = da_ref[rows, sl].astype(F32)
                sg = _sigmoid(gc)
                dgc = da * uc * (sg * (1.0 + gc * (1.0 - sg)))
                duc = da * (gc * sg)
                dg_ref[rows, sl] = dgc.astype(BF16)
                du_ref[rows, sl] = duc.astype(BF16)
                dwg = [dwg[k] + _fold8(dgc * gt[k]) for k in range(FFN_CONV)]
                dwu = [dwu[k] + _fold8(duc * ut[k]) for k in range(FFN_CONV)]
            for k in range(FFN_CONV):
                dwg_ref[k:k + 1, sl] += jnp.sum(dwg[k], axis=0, keepdims=True)
                dwu_ref[k:k + 1, sl] += jnp.sum(dwu[k], axis=0, keepdims=True)

    prev = lambda off: (lambda j, i: (jnp.maximum(i * (tr // HALO) - 1, 0), j + off))
    blk = lambda off: pl.BlockSpec((tr, tc), lambda j, i: (i, j + off))
    wblk = lambda off: pl.BlockSpec((FFN_CONV, tc), lambda j, i: (0, j + off))
    dgc, duc, dwg, dwu = pl.pallas_call(
        body, name=name,
        out_shape=(jax.ShapeDtypeStruct((t, D_FF), BF16), jax.ShapeDtypeStruct((t, D_FF), BF16),
                   jax.ShapeDtypeStruct((FFN_CONV, D_FF), F32), jax.ShapeDtypeStruct((FFN_CONV, D_FF), F32)),
        grid=(nj, t // tr),
        in_specs=[blk(0), blk(0), pl.BlockSpec((HALO, tc), prev(0)), blk(nj), pl.BlockSpec((HALO, tc), prev(nj)),
                  wblk(0), wblk(nj)],
        out_specs=(blk(0), blk(0), wblk(0), wblk(0)),
        compiler_params=_params(("parallel", "arbitrary")),
    )(dact, upre, upre, upre, upre, cw, cw)
    return dgc, duc, dwg, dwu


def _dn_pre_fwd(qkv_pre, cw, name):
    t = qkv_pre.shape[0]
    tr = _tile(t, 512)
    scale = HEAD_DIM ** -0.5

    def body(x_ref, p_ref, w_ref, o_ref):
        i, j = pl.program_id(0), pl.program_id(1)
        row = _iota((tr, HEAD_DIM), 0)
        for h in range(HEADS):
            sl = slice(h * HEAD_DIM, (h + 1) * HEAD_DIM)
            prev = jnp.where(i == 0, 0.0, p_ref[:, sl].astype(F32))
            _, c = _conv_taps(x_ref[:, sl].astype(F32), prev, w_ref[:, sl], DN_CONV, row)
            s = c * _sigmoid(c)
            r = lax.rsqrt(jnp.sum(s * s, axis=1, keepdims=True) + EPS)
            o_ref[:, sl] = s * jnp.where(j == 0, r * scale, jnp.where(j == 1, r, 1.0))

    return pl.pallas_call(
        body, name=name,
        out_shape=jax.ShapeDtypeStruct((t, 3 * WIDTH), F32),
        grid=(t // tr, 3),
        in_specs=[pl.BlockSpec((tr, WIDTH), lambda i, j: (i, j)),
                  pl.BlockSpec((HALO, WIDTH), lambda i, j: (jnp.maximum(i * (tr // HALO) - 1, 0), j)),
                  pl.BlockSpec((DN_CONV, WIDTH), lambda i, j: (0, j))],
        out_specs=pl.BlockSpec((tr, WIDTH), lambda i, j: (i, j)),
        compiler_params=_params(("parallel", "parallel")),
    )(qkv_pre, qkv_pre, cw)


def _dn_pre_bwd(dq, dk, dv, qkv_pre, cw, name):
    t = qkv_pre.shape[0]
    tr = _tile(t, 256)
    scale = HEAD_DIM ** -0.5

    def body(dq_ref, dk_ref, dv_ref, x_ref, p_ref, w_ref, dc_ref, dw_ref):
        j, i = pl.program_id(0), pl.program_id(1)
        row = _iota((CHUNK_ROWS, HEAD_DIM), 0)

        @pl.when(i == 0)
        def _():
            dw_ref[...] = jnp.zeros_like(dw_ref)

        for h in range(HEADS):
            sl = slice(h * HEAD_DIM, (h + 1) * HEAD_DIM)
            wv = w_ref[:, sl]
            dw = [jnp.zeros((8, HEAD_DIM), F32)] * DN_CONV
            for r0 in range(0, tr, CHUNK_ROWS):
                rows = slice(r0, r0 + CHUNK_ROWS)
                if r0 == 0:
                    prev = jnp.where(i == 0, 0.0, p_ref[:, sl].astype(F32))
                else:
                    prev = x_ref[r0 - HALO:r0, sl].astype(F32)
                taps, c = _conv_taps(x_ref[rows, sl].astype(F32), prev, wv, DN_CONV, row)
                d = jnp.where(j == 0, dq_ref[rows, sl] * scale, jnp.where(j == 1, dk_ref[rows, sl], dv_ref[rows, sl]))
                sg = _sigmoid(c)
                s = c * sg
                r = lax.rsqrt(jnp.sum(s * s, axis=1, keepdims=True) + EPS)
                nh = s * r
                ds_norm = r * (d - nh * jnp.sum(nh * d, axis=1, keepdims=True))
                dc = jnp.where(j < 2, ds_norm, d) * (sg * (1.0 + c * (1.0 - sg)))
                dc_ref[rows, sl] = dc.astype(BF16)
                dw = [dw[k] + _fold8(dc * taps[k]) for k in range(DN_CONV)]
            for k in range(DN_CONV):
                dw_ref[k:k + 1, sl] += jnp.sum(dw[k], axis=0, keepdims=True)

    dspec = lambda p: pl.BlockSpec((tr, WIDTH), lambda j, i: (jnp.where(j == p, i, 0), 0))
    return pl.pallas_call(
        body, name=name,
        out_shape=(jax.ShapeDtypeStruct((t, 3 * WIDTH), BF16), jax.ShapeDtypeStruct((DN_CONV, 3 * WIDTH), F32)),
        grid=(3, t // tr),
        in_specs=[dspec(0), dspec(1), dspec(2),
                  pl.BlockSpec((tr, WIDTH), lambda j, i: (i, j)),
                  pl.BlockSpec((HALO, WIDTH), lambda j, i: (jnp.maximum(i * (tr // HALO) - 1, 0), j)),
                  pl.BlockSpec((DN_CONV, WIDTH), lambda j, i: (0, j))],
        out_specs=(pl.BlockSpec((tr, WIDTH), lambda j, i: (i, j)),
                   pl.BlockSpec((DN_CONV, WIDTH), lambda j, i: (0, j))),
        compiler_params=_params(("parallel", "arbitrary")),
    )(dq, dk, dv, qkv_pre, qkv_pre, cw)


def _tri(n, kind):
    r, c = _iota((n, n), 0), _iota((n, n), 1)
    m = {"lower": r >= c, "strict": r > c, "upper": r <= c}[kind]
    return m


GATE_ROWS = 4 * DN_CHUNK


def _chunk_tri(kind):
    r, c = _iota((GATE_ROWS, GATE_ROWS), 0), _iota((GATE_ROWS, GATE_ROWS), 1)
    same = (r // DN_CHUNK) == (c // DN_CHUNK)
    return jnp.where(jnp.logical_and(same, _tri(GATE_ROWS, kind)), 1.0, 0.0).astype(BF16)


def _dn_gates_fwd(hab, alog, dtb, name):
    t = hab.shape[0]
    cc = GATE_ROWS

    def body(h_ref, al_ref, dt_ref, o_ref):
        hv = h_ref[...]
        lane = _iota(hv.shape, 1)
        xa = hv + dt_ref[...]
        sp = jnp.maximum(xa, 0.0) + _log1pexp_neg_abs(xa)
        g = jnp.where(lane < HEADS, -jnp.exp(al_ref[...]) * sp, 0.0)
        gc = _dot_xl(_chunk_tri("lower"), g, NN)
        o_ref[...] = jnp.where(lane < HEADS, gc, jnp.where(lane < 2 * HEADS, _sigmoid(hv), 0.0))

    return pl.pallas_call(
        body, name=name,
        out_shape=jax.ShapeDtypeStruct((t, 128), F32),
        grid=(t // cc,),
        in_specs=[pl.BlockSpec((cc, 128), lambda i: (i, 0)), pl.BlockSpec((1, 128), lambda i: (0, 0)),
                  pl.BlockSpec((1, 128), lambda i: (0, 0))],
        out_specs=pl.BlockSpec((cc, 128), lambda i: (i, 0)),
        compiler_params=_params(("parallel",)),
    )(hab, alog, dtb)


def _dn_gates_bwd(dgates, hab, alog, dtb, name):
    t = hab.shape[0]
    cc = GATE_ROWS

    def body(d_ref, h_ref, al_ref, dt_ref, o_ref, dal_ref, ddt_ref):
        i = pl.program_id(0)
        hv = h_ref[...]
        dv = d_ref[...]
        lane = _iota(hv.shape, 1)
        dg = _dot_xl(_chunk_tri("upper"), jnp.where(lane < HEADS, dv, 0.0), NN)
        xa = hv + dt_ref[...]
        sp = jnp.maximum(xa, 0.0) + _log1pexp_neg_abs(xa)
        ea = jnp.exp(al_ref[...])
        da = jnp.where(lane < HEADS, dg * (-ea) * _sigmoid(xa), 0.0)
        be = _sigmoid(hv)
        db = dv * be * (1.0 - be)
        o_ref[...] = jnp.where(lane < HEADS, da, jnp.where(lane < 2 * HEADS, db, 0.0))

        @pl.when(i == 0)
        def _():
            dal_ref[...] = jnp.zeros_like(dal_ref)
            ddt_ref[...] = jnp.zeros_like(ddt_ref)

        dal_ref[...] += jnp.sum(jnp.where(lane < HEADS, dg * (-ea) * sp, 0.0), axis=0, keepdims=True)
        ddt_ref[...] += jnp.sum(da, axis=0, keepdims=True)

    return pl.pallas_call(
        body, name=name,
        out_shape=(jax.ShapeDtypeStruct((t, 128), F32), jax.ShapeDtypeStruct((1, 128), F32),
                   jax.ShapeDtypeStruct((1, 128), F32)),
        grid=(t // cc,),
        in_specs=[pl.BlockSpec((cc, 128), lambda i: (i, 0)), pl.BlockSpec((cc, 128), lambda i: (i, 0)),
                  pl.BlockSpec((1, 128), lambda i: (0, 0)), pl.BlockSpec((1, 128), lambda i: (0, 0))],
        out_specs=(pl.BlockSpec((cc, 128), lambda i: (i, 0)), pl.BlockSpec((1, 128), lambda i: (0, 0)),
                   pl.BlockSpec((1, 128), lambda i: (0, 0))),
        compiler_params=_params(("arbitrary",)),
    )(dgates, hab, alog, dtb)


def _dn_chunk_common(gates, h):
    cc = DN_CHUNK
    lane = _iota(gates.shape, 1)
    gh = jnp.where(lane == h, gates, 0.0)
    gc_col = jnp.sum(gh, axis=1, keepdims=True)
    gc_row = _dot_xl(jnp.ones((cc, 128), BF16), gh, NT)
    beta = jnp.sum(jnp.where(lane == h + HEADS, gates, 0.0), axis=1, keepdims=True)
    lower = _tri(cc, "lower")
    decay = jnp.where(lower, jnp.exp(jnp.where(lower, gc_col - gc_row, 0.0)), 0.0)
    gc_last = gc_col[cc - 1:cc, :]
    return gc_col, gc_last, beta, decay


def _dn_local_fwd(act, gates, name):
    t = act.shape[0]
    cc = DN_CHUNK
    nc = t // cc

    def body(q_ref, k_ref, v_ref, g_ref, u_ref, w_ref, kd_ref, qg_ref, ti_ref, p_ref):
        gates = g_ref[...]
        eye = jnp.where(_iota((cc, cc), 0) == _iota((cc, cc), 1), 1.0, 0.0)
        hs = range(HEADS)
        sl = [slice(h * HEAD_DIM, (h + 1) * HEAD_DIM) for h in hs]
        q, k, v = ([r[:, s] for s in sl] for r in (q_ref, k_ref, v_ref))
        gc_col, gc_last, beta, decay = zip(*[_dn_chunk_common(gates, h) for h in hs])
        gam = [jnp.exp(g) for g in gc_col]
        kb = [k[h] * beta[h] for h in hs]
        npow = [-jnp.where(_tri(cc, "strict"), _dotb(kb[h], k[h], NT) * decay[h], 0.0) for h in hs]
        tinv = [eye + n for n in npow]
        for _ in range(5):
            npow = [_dot3(n, n, NN) for n in npow]
            tinv = [t + _dot3(t, n, NN) for t, n in zip(tinv, npow)]
        uu = [_dot3(tinv[h], v[h] * beta[h], NN) for h in hs]
        ww = [_dot3(tinv[h], kb[h] * gam[h], NN) for h in hs]
        pp = [jnp.where(_tri(cc, "lower"), _dotb(q[h], k[h], NT) * decay[h], 0.0) for h in hs]
        for h in hs:
            u_ref[:, sl[h]] = uu[h]
            w_ref[:, sl[h]] = ww[h]
            kd_ref[:, sl[h]] = k[h] * jnp.exp(gc_last[h] - gc_col[h])
            qg_ref[:, sl[h]] = q[h] * gam[h]
            ti_ref[h] = tinv[h]
            p_ref[h] = pp[h]

    row = lambda off: pl.BlockSpec((cc, WIDTH), lambda n: (n, off))
    mat = pl.BlockSpec((HEADS, cc, cc), lambda n: (0, n, 0))
    tw = jax.ShapeDtypeStruct((t, WIDTH), F32)
    hm = jax.ShapeDtypeStruct((HEADS, t, cc), F32)
    return pl.pallas_call(
        body, name=name,
        out_shape=(tw, tw, tw, tw, hm, hm),
        grid=(nc,),
        in_specs=[row(0), row(1), row(2), pl.BlockSpec((cc, 128), lambda n: (n, 0))],
        out_specs=(row(0), row(0), row(0), row(0), mat, mat),
        compiler_params=_params(("parallel",)),
    )(act, act, act, gates)


def _dn_scan_fwd(u, w, kd, qg, p, gates, name):
    t = u.shape[0]
    cc = DN_CHUNK
    nc = t // cc

    def body(u_ref, w_ref, kd_ref, qg_ref, p_ref, g_ref, o_ref, sh_ref, s_ref):
        n = pl.program_id(0)

        @pl.when(n == 0)
        def _():
            s_ref[...] = jnp.zeros_like(s_ref)

        glast = jnp.exp(g_ref[cc - 1:cc, :])
        hs = range(HEADS)
        sl = [slice(h * HEAD_DIM, (h + 1) * HEAD_DIM) for h in hs]
        s = [s_ref[h] for h in hs]
        sb = [a.astype(BF16) for a in s]
        vn = [u_ref[:, sl[h]] - _dot(w_ref[:, sl[h]].astype(BF16), sb[h], NN) for h in hs]
        vnb = [a.astype(BF16) for a in vn]
        o_state = [_dot(qg_ref[:, sl[h]].astype(BF16), sb[h], NN) for h in hs]
        o_local = [_dot(p_ref[h].astype(BF16), vnb[h], NN) for h in hs]
        s_add = [_dot(kd_ref[:, sl[h]].astype(BF16), vnb[h], TN) for h in hs]
        for h in hs:
            o_ref[:, sl[h]] = o_state[h] + o_local[h]
            sh_ref[0, h] = s[h]
            s_ref[h] = glast[:, h:h + 1] * s[h] + s_add[h]

    row = pl.BlockSpec((cc, WIDTH), lambda n: (n, 0))
    return pl.pallas_call(
        body, name=name,
        out_shape=(jax.ShapeDtypeStruct((t, WIDTH), F32),
                   jax.ShapeDtypeStruct((nc, HEADS, HEAD_DIM, HEAD_DIM), F32)),
        grid=(nc,),
        in_specs=[row, row, row, row, pl.BlockSpec((HEADS, cc, cc), lambda n: (0, n, 0)),
                  pl.BlockSpec((cc, 128), lambda n: (n, 0))],
        out_specs=(row, pl.BlockSpec((1, HEADS, HEAD_DIM, HEAD_DIM), lambda n: (n, 0, 0, 0))),
        scratch_shapes=[pltpu.VMEM((HEADS, HEAD_DIM, HEAD_DIM), F32)],
        compiler_params=_params(("arbitrary",)),
    )(u, w, kd, qg, p, gates)


def _dn_scan_bwd(do, w, kd, qg, p, gates, name):
    t = do.shape[0]
    cc = DN_CHUNK
    nc = t // cc

    def body(do_ref, w_ref, kd_ref, qg_ref, p_ref, g_ref, dvn_ref, dsh_ref, ds_ref):
        n = pl.program_id(0)

        @pl.when(n == 0)
        def _():
            ds_ref[...] = jnp.zeros_like(ds_ref)

        glast = jnp.exp(g_ref[cc - 1:cc, :])
        hs = range(HEADS)
        sl = [slice(h * HEAD_DIM, (h + 1) * HEAD_DIM) for h in hs]
        ds = [ds_ref[h] for h in hs]
        dob = [do_ref[:, sl[h]].astype(BF16) for h in hs]
        dvn = [_dot(p_ref[h].astype(BF16), dob[h], TN) + _dot(kd_ref[:, sl[h]].astype(BF16), ds[h].astype(BF16), NN)
               for h in hs]
        ds_q = [_dot(qg_ref[:, sl[h]].astype(BF16), dob[h], TN) for h in hs]
        ds_w = [_dot(w_ref[:, sl[h]].astype(BF16), dvn[h].astype(BF16), TN) for h in hs]
        for h in hs:
            dvn_ref[:, sl[h]] = dvn[h]
            dsh_ref[0, h] = ds[h]
            ds_ref[h] = ds_q[h] + glast[:, h:h + 1] * ds[h] - ds_w[h]

    row = pl.BlockSpec((cc, WIDTH), lambda n: (nc - 1 - n, 0))
    return pl.pallas_call(
        body, name=name,
        out_shape=(jax.ShapeDtypeStruct((t, WIDTH), F32),
                   jax.ShapeDtypeStruct((nc, HEADS, HEAD_DIM, HEAD_DIM), F32)),
        grid=(nc,),
        in_specs=[row, row, row, row, pl.BlockSpec((HEADS, cc, cc), lambda n: (0, nc - 1 - n, 0)),
                  pl.BlockSpec((cc, 128), lambda n: (nc - 1 - n, 0))],
        out_specs=(row, pl.BlockSpec((1, HEADS, HEAD_DIM, HEAD_DIM), lambda n: (nc - 1 - n, 0, 0, 0))),
        scratch_shapes=[pltpu.VMEM((HEADS, HEAD_DIM, HEAD_DIM), F32)],
        compiler_params=_params(("arbitrary",)),
    )(do, w, kd, qg, p, gates)


def _dn_local_bwd(act, gates, u, w, kd, qg, tinv, p, sh, dsh, dvn, do, name):
    t = act.shape[0]
    cc = DN_CHUNK
    nc = t // cc

    def body(q_ref, k_ref, v_ref, g_ref, u_ref, w_ref, kd_ref, qg_ref, ti_ref, p_ref, s_ref, ds_ref,
             dvn_ref, do_ref, dq_ref, dk_ref, dv_ref, dg_ref):
        gates_v = g_ref[...]
        lower, strict = _tri(cc, "lower"), _tri(cc, "strict")
        ones = jnp.ones((cc, 128), BF16)
        rowc = _iota((cc, 1), 0)
        lane = _iota((cc, 128), 1)
        hs = range(HEADS)
        sl = [slice(h * HEAD_DIM, (h + 1) * HEAD_DIM) for h in hs]
        q, k, v, uu, ww, kd, qg, dvn, do = ([r[:, s] for s in sl] for r in (
            q_ref, k_ref, v_ref, u_ref, w_ref, kd_ref, qg_ref, dvn_ref, do_ref))
        gc_col, gc_last, beta, decay = zip(*[_dn_chunk_common(gates_v, h) for h in hs])
        gam = [jnp.exp(g) for g in gc_col]
        kb = [k[h] * beta[h] for h in hs]
        s_in = [s_ref[0, h] for h in hs]
        ds_out = [ds_ref[0, h] for h in hs]
        tinv = [ti_ref[h] for h in hs]

        a = [jnp.where(strict, _dotb(kb[h], k[h], NT) * decay[h], 0.0) for h in hs]
        vn = [uu[h] - _dotb(ww[h], s_in[h], NN) for h in hs]
        dqg = [_dotb(do[h], s_in[h], NT) for h in hs]
        dw = [-_dotb(dvn[h], s_in[h], NT) for h in hs]
        dp = [jnp.where(lower, _dotb(do[h], vn[h], NT), 0.0) for h in hs]
        dkd = [_dotb(vn[h], ds_out[h], NT) for h in hs]
        dru = [_dot3(tinv[h], dvn[h], TN) for h in hs]
        drw = [_dot3(tinv[h], dw[h], TN) for h in hs]
        da = [-jnp.where(strict, _dotb(dru[h], uu[h], NT) + _dotb(drw[h], ww[h], NT), 0.0) for h in hs]
        dad = [da[h] * decay[h] for h in hs]
        dpd = [dp[h] * decay[h] for h in hs]
        dkb = [_dotb(dad[h], k[h], NN) + gam[h] * drw[h] for h in hs]
        dk = [_dotb(dad[h], kb[h], TN) + _dotb(dpd[h], q[h], TN) + beta[h] * dkb[h]
              + jnp.exp(gc_last[h] - gc_col[h]) * dkd[h] for h in hs]
        dq = [gam[h] * dqg[h] + _dotb(dpd[h], k[h], NN) for h in hs]
        gm = [da[h] * a[h] + dp[h] * p_ref[h] for h in hs]
        colsum = [_dot_xr(gm[h], ones, TN)[:, 0:1] for h in hs]

        dgates = jnp.zeros((cc, 128), F32)
        for h in hs:
            dk_ref[:, sl[h]] = dk[h]
            dq_ref[:, sl[h]] = dq[h]
            dv_ref[:, sl[h]] = beta[h] * dru[h]
            dbeta = (jnp.sum(dkb[h] * k[h], axis=1, keepdims=True)
                     + jnp.sum(dru[h] * v[h], axis=1, keepdims=True))
            rkd = jnp.sum(dkd[h] * kd[h], axis=1, keepdims=True)
            dgc = (jnp.sum(gm[h], axis=1, keepdims=True) - colsum[h]
                   + jnp.sum(dqg[h] * qg[h], axis=1, keepdims=True)
                   + jnp.sum(drw[h] * kb[h], axis=1, keepdims=True) * gam[h] - rkd)
            tail = jnp.sum(rkd, axis=0, keepdims=True) + jnp.exp(gc_last[h]) * jnp.sum(
                jnp.sum(s_in[h] * ds_out[h], axis=1, keepdims=True), axis=0, keepdims=True)
            dgc = dgc + jnp.where(rowc == cc - 1, tail, 0.0)
            dgates = dgates + jnp.where(lane == h, dgc, 0.0) + jnp.where(lane == h + HEADS, dbeta, 0.0)
        dg_ref[...] = dgates

    row = lambda off: pl.BlockSpec((cc, WIDTH), lambda n: (n, off))
    mat = pl.BlockSpec((HEADS, cc, cc), lambda n: (0, n, 0))
    st = pl.BlockSpec((1, HEADS, HEAD_DIM, HEAD_DIM), lambda n: (n, 0, 0, 0))
    gl = pl.BlockSpec((cc, 128), lambda n: (n, 0))
    tw = jax.ShapeDtypeStruct((t, WIDTH), F32)
    return pl.pallas_call(
        body, name=name,
        out_shape=(tw, tw, tw, jax.ShapeDtypeStruct((t, 128), F32)),
        grid=(nc,),
        in_specs=[row(0), row(1), row(2), gl, row(0), row(0), row(0), row(0), mat, mat, st, st, row(0), row(0)],
        out_specs=(row(0), row(0), row(0), gl),
        compiler_params=_params(("parallel",)),
    )(act, act, act, gates, u, w, kd, qg, tinv, p, sh, dsh, dvn, do)


def _dn_post_fwd(o, gate, w, name):
    t = o.shape[0]
    tr = _tile(t, 512)

    def body(o_ref, g_ref, w_ref, y_ref):
        for h in range(HEADS):
            sl = slice(h * HEAD_DIM, (h + 1) * HEAD_DIM)
            ov, gv = o_ref[:, sl], g_ref[:, sl].astype(F32)
            r = lax.rsqrt(jnp.mean(ov * ov, axis=1, keepdims=True) + EPS)
            y_ref[:, sl] = (ov * r * w_ref[...] * (gv * _sigmoid(gv))).astype(BF16)

    blk = pl.BlockSpec((tr, WIDTH), lambda i: (i, 0))
    return pl.pallas_call(
        body, name=name,
        out_shape=jax.ShapeDtypeStruct((t, WIDTH), BF16),
        grid=(t // tr,),
        in_specs=[blk, blk, pl.BlockSpec((1, HEAD_DIM), lambda i: (0, 0))],
        out_specs=blk,
        compiler_params=_params(("parallel",)),
    )(o, gate, w)


def _dn_post_bwd(dy, o, gate, w, name):
    t = o.shape[0]
    tr = _tile(t, 512)

    def body(dy_ref, o_ref, g_ref, w_ref, do_ref, dg_ref, dw_ref):
        i = pl.program_id(0)

        @pl.when(i == 0)
        def _():
            dw_ref[...] = jnp.zeros_like(dw_ref)

        dw = jnp.zeros((1, HEAD_DIM), F32)
        for h in range(HEADS):
            sl = slice(h * HEAD_DIM, (h + 1) * HEAD_DIM)
            ov, gv, dyv = o_ref[:, sl], g_ref[:, sl].astype(F32), dy_ref[:, sl].astype(F32)
            r = lax.rsqrt(jnp.mean(ov * ov, axis=1, keepdims=True) + EPS)
            oh = ov * r
            sg = _sigmoid(gv)
            dg_ref[:, sl] = (dyv * oh * w_ref[...] * (sg * (1.0 + gv * (1.0 - sg)))).astype(BF16)
            dn = dyv * (gv * sg)
            doh = dn * w_ref[...]
            do_ref[:, sl] = r * (doh - oh * jnp.mean(doh * oh, axis=1, keepdims=True))
            dw = dw + jnp.sum(dn * oh, axis=0, keepdims=True)
        dw_ref[...] += dw

    blk = pl.BlockSpec((tr, WIDTH), lambda i: (i, 0))
    return pl.pallas_call(
        body, name=name,
        out_shape=(jax.ShapeDtypeStruct((t, WIDTH), F32), jax.ShapeDtypeStruct((t, WIDTH), BF16),
                   jax.ShapeDtypeStruct((1, HEAD_DIM), F32)),
        grid=(t // tr,),
        in_specs=[blk, blk, blk, pl.BlockSpec((1, HEAD_DIM), lambda i: (0, 0))],
        out_specs=(blk, blk, pl.BlockSpec((1, HEAD_DIM), lambda i: (0, 0))),
        compiler_params=_params(("arbitrary",)),
    )(dy, o, gate, w)


def _sb_scores(qs, k_ref, qi, it, carries, uincl):
    bk = ATT_BLOCK
    scale = HEAD_DIM ** -0.5
    heads, groups = range(len(qs)), range(SB_GROUP)
    lane = [slice(e * HEAD_DIM, (e + 1) * HEAD_DIM) for e in heads]
    js = [qi - SB_GROUP * it - g for g in groups]
    rows = [pl.ds(pl.multiple_of(jnp.maximum(j, 0) * bk, bk), bk) for j in js]
    qpos = qi * bk + _iota((bk, bk), 0)
    col = _iota((bk, bk), 1)
    mask1 = [jnp.logical_and(j * bk + col < qpos, j >= 0) for j in js]
    ks = [[k_ref[r, lane[e]] for r in rows] for e in heads]
    z = [[_dot(qs[e], k, NT) * scale for k in ks[e]] for e in heads]
    soft = [[_log1pexp_neg_abs(a) for a in ze] for ze in z]
    lk_full = [[-(jnp.maximum(a, 0.0) + s) for a, s in zip(z[e], soft[e])] for e in heads]
    lk = [[jnp.where(m, a, 0.0) for m, a in zip(mask1, lk_full[e])] for e in heads]
    ls = [[jnp.minimum(a, 0.0) - s for a, s in zip(z[e], soft[e])] for e in heads]
    incl = [[_dot_xr2(a, uincl, NN) for a in lk[e]] for e in heads]
    weights, out_carries = [], []
    for e in heads:
        cb, we = carries[e], []
        for g in groups:
            we.append(jnp.where(mask1[g], jnp.exp(ls[e][g] + (cb + incl[e][g] - lk[e][g])), 0.0))
            cb = cb + incl[e][g][:, 0:1]
        weights.append(we)
        out_carries.append(cb)
    return rows, ks, weights, mask1, lk_full, ls, out_carries


def _sb_more(qi, carry):
    it, cbs = carry[0], carry[1]
    live = jnp.max(cbs[0])
    for cb in cbs[1:]:
        live = jnp.maximum(live, jnp.max(cb))
    return jnp.logical_and(SB_GROUP * it <= qi, live > SB_LOG_ZERO)


def _sb_steps(groups, nq):
    def when():
        h, i = pl.program_id(0), pl.program_id(1)
        return (jnp.logical_and(h == 0, i == 0), jnp.logical_and(h == groups // 2, i == 0),
                jnp.logical_and(h == groups - 1, i == nq - 1))
    return when


def _sb_fwd(qkv, name, comm=None):
    t = qkv.shape[0]
    bk = ATT_BLOCK
    hp, wide = SB_HEADS_FWD, SB_HEADS_FWD * HEAD_DIM
    lane = [slice(e * HEAD_DIM, (e + 1) * HEAD_DIM) for e in range(hp)]

    def body(q_ref, k_ref, v_ref, o_ref):
        qi = pl.program_id(1)
        qs = [q_ref[:, s] for s in lane]
        uincl = jnp.where(_tri(bk, "lower"), 1.0, 0.0).astype(BF16)

        def step(carry):
            it, cbs, accs = carry
            rows, _, weights, _, _, _, cbs = _sb_scores(qs, k_ref, qi, it, cbs, uincl)
            accs = list(accs)
            for e in range(hp):
                for r, a in zip(rows, weights[e]):
                    accs[e] = accs[e] + _dot(a.astype(BF16), v_ref[r, lane[e]], NN)
            return it + 1, tuple(cbs), tuple(accs)

        init = (jnp.int32(0), (jnp.zeros((bk, 1), F32),) * hp, (jnp.zeros((bk, HEAD_DIM), F32),) * hp)
        _, _, accs = lax.while_loop(functools.partial(_sb_more, qi), step, init)
        for e in range(hp):
            o_ref[:, lane[e]] = accs[e]

    groups = HEADS // hp
    (o,), extra = _host_call(
        body, name, comm, _sb_steps(groups, t // bk), [jax.ShapeDtypeStruct((t, WIDTH), F32)], (groups, t // bk),
        [pl.BlockSpec((bk, wide), lambda h, i: (i, h)),
         pl.BlockSpec((t, wide), lambda h, i: (0, groups + h)),
         pl.BlockSpec((t, wide), lambda h, i: (0, 2 * groups + h))],
        [pl.BlockSpec((bk, wide), lambda h, i: (i, h))], [], ("parallel", "arbitrary"), (qkv, qkv, qkv))
    return o, extra


def _sb_bwd(qkv, o, do, name, comm=None):
    assert do.dtype == BF16
    t = qkv.shape[0]
    bk = ATT_BLOCK
    scale = HEAD_DIM ** -0.5
    hp, wide = SB_HEADS_BWD, SB_HEADS_BWD * HEAD_DIM
    lane = [slice(e * HEAD_DIM, (e + 1) * HEAD_DIM) for e in range(hp)]

    def body(q_ref, k_ref, v_ref, o_ref, do_ref, dq_ref, dk_ref, dv_ref):
        qi = pl.program_id(1)

        @pl.when(qi == 0)
        def _():
            dk_ref[...] = jnp.zeros_like(dk_ref)
            dv_ref[...] = jnp.zeros_like(dv_ref)

        heads, groups = range(hp), range(SB_GROUP)
        qs = [q_ref[:, s] for s in lane]
        dob = [do_ref[:, s] for s in lane]
        dsum = [jnp.sum(dob[e].astype(F32) * o_ref[:, lane[e]], axis=1, keepdims=True) for e in heads]
        uincl = jnp.where(_tri(bk, "lower"), 1.0, 0.0).astype(BF16)

        def step(carry):
            it, cbs, ces, dqs = carry
            rows, ks, weights, mask, lk_full, ls, cbs = _sb_scores(qs, k_ref, qi, it, cbs, uincl)
            ab = [[a.astype(BF16) for a in weights[e]] for e in heads]
            vs = [[v_ref[r, lane[e]] for r in rows] for e in heads]
            dla = [[ab[e][g].astype(F32) * _dot(dob[e], vs[e][g], NT) for g in groups] for e in heads]
            suf = [[_dot_xr2(a, uincl, NN) for a in dla[e]] for e in heads]
            ces, dqs = list(ces), list(dqs)
            for e in heads:
                for g in groups:
                    err = dsum[e] - (ces[e] + suf[e][g])
                    ces[e] = ces[e] + suf[e][g][:, 0:1]
                    dz = jnp.where(mask[g], dla[e][g] * jnp.exp(lk_full[e][g]) - err * jnp.exp(ls[e][g]), 0.0)
                    dzb = (dz * scale).astype(BF16)
                    dqs[e] = dqs[e] + _dot(dzb, ks[e][g], NN)
                    dk_ref[rows[g], lane[e]] += _dot(dzb, qs[e], TN)
                    dv_ref[rows[g], lane[e]] += _dot(ab[e][g], dob[e], TN)
            return it + 1, tuple(cbs), tuple(ces), tuple(dqs)

        zc = (jnp.zeros((bk, 1), F32),) * hp
        init = (jnp.int32(0), zc, zc, (jnp.zeros((bk, HEAD_DIM), F32),) * hp)
        dqs = lax.while_loop(functools.partial(_sb_more, qi), step, init)[3]
        for e in heads:
            dq_ref[:, lane[e]] = dqs[e].astype(BF16)

    ngroup = HEADS // hp
    tw = jax.ShapeDtypeStruct((t, WIDTH), F32)
    qb = pl.BlockSpec((bk, wide), lambda h, i: (i, h))
    full = lambda off: pl.BlockSpec((t, wide), lambda h, i: (0, off + h))
    return _host_call(
        body, name, comm, _sb_steps(ngroup, t // bk), [jax.ShapeDtypeStruct((t, WIDTH), BF16), tw, tw],
        (ngroup, t // bk),
        [qb, full(ngroup), full(2 * ngroup), qb, qb], [qb, full(0), full(0)], [], ("parallel", "arbitrary"),
        (qkv, qkv, qkv, o, do))


def _merge_fwd(pd, ps, gl, name):
    t = pd.shape[0]
    tr, tc = _tile(t, 512), 512
    nj = D_MODEL // tc

    def body(pd_ref, ps_ref, gd_ref, gs_ref, o_ref):
        gd, gs = gd_ref[...].astype(F32), gs_ref[...].astype(F32)
        o_ref[...] = (_sigmoid(gd) * pd_ref[...].astype(F32) + _sigmoid(gs) * ps_ref[...].astype(F32)).astype(BF16)

    blk = lambda off: pl.BlockSpec((tr, tc), lambda i, j: (i, j + off))
    return pl.pallas_call(
        body, name=name,
        out_shape=jax.ShapeDtypeStruct((t, D_MODEL), BF16),
        grid=(t // tr, nj),
        in_specs=[blk(0), blk(0), blk(0), blk(nj)],
        out_specs=blk(0),
        compiler_params=_params(("parallel", "parallel")),
    )(pd, ps, gl, gl)


def _merge_bwd(dm, pd, ps, gl, name):
    t = pd.shape[0]
    tr, tc = _tile(t, 512), 512
    nj = D_MODEL // tc

    def body(dm_ref, pd_ref, ps_ref, gd_ref, gs_ref, dpd_ref, dps_ref, dgd_ref, dgs_ref):
        dmv = dm_ref[...].astype(F32)
        sd, ss = _sigmoid(gd_ref[...].astype(F32)), _sigmoid(gs_ref[...].astype(F32))
        dpd_ref[...] = (dmv * sd).astype(BF16)
        dps_ref[...] = (dmv * ss).astype(BF16)
        dgd_ref[...] = (dmv * pd_ref[...].astype(F32) * sd * (1.0 - sd)).astype(BF16)
        dgs_ref[...] = (dmv * ps_ref[...].astype(F32) * ss * (1.0 - ss)).astype(BF16)

    blk = lambda off: pl.BlockSpec((tr, tc), lambda i, j: (i, j + off))
    out = jax.ShapeDtypeStruct((t, D_MODEL), BF16)
    return pl.pallas_call(
        body, name=name,
        out_shape=(out, out, out, out),
        grid=(t // tr, nj),
        in_specs=[blk(0), blk(0), blk(0), blk(0), blk(nj)],
        out_specs=(blk(0), blk(0), blk(0), blk(0)),
        compiler_params=_params(("parallel", "parallel")),
    )(dm, pd, ps, gl, gl)


def _local_step(x, target, wts, plan=None):
    n1 = _rmsnorm_fwd(x, wts["norm1_w"], "norm1_fwd")
    qkv_pre = _matmul(n1, wts["w_dnqkv_t"], "nt", BF16, "in_dnqkv")
    hgate = _matmul(n1, wts["w_dngate_t"], "nt", BF16, "in_dngate")
    sbqkv = _matmul(n1, wts["w_sbqkv_t"], "nt", BF16, "in_sbqkv")
    gl = _matmul(n1, wts["w_gl_t"], "nt", BF16, "in_gl")
    hab = _matmul(n1, wts["w_ab_t"], "nt", F32, "in_ab")

    act = _dn_pre_fwd(qkv_pre, wts["dn_conv_w"], "dn_pre_fwd")
    gates = _dn_gates_fwd(hab, wts["alog"], wts["dtb"], "dn_gates_fwd")
    u, w, kd, qg, tinv, p = _dn_local_fwd(act, gates, "dn_local_fwd")
    o_dn, sh = _dn_scan_fwd(u, w, kd, qg, p, gates, "dn_scan_fwd")
    y_dn = _dn_post_fwd(o_dn, hgate, wts["dn_norm_w"], "dn_post_fwd")

    o_sb, late = _sb_fwd(sbqkv, "sb_fwd", comm=plan.late_gather() if plan else None)
    if plan:
        wts = {**wts, **plan.late_weights(late)}

    pd = _matmul(y_dn, wts["w_proj_dn"], "nn", BF16, "proj_dn")
    ps = _matmul(o_sb, wts["w_proj_sb"], "nn", BF16, "proj_sb")
    mixed = _merge_fwd(pd, ps, gl, "merge_fwd")
    x1 = _matmul(mixed, wts["w_out"], "nn", F32, "out_proj", add=x)

    n2 = _rmsnorm_fwd(x1, wts["norm2_w"], "norm2_fwd")
    upre = _matmul(n2, wts["ffn_w_up_t"], "nt", BF16, "ffn_up")
    fact = _ffn_act_fwd(upre, wts["ffn_conv_w"], "ffn_act_fwd")
    x2 = _matmul(fact, wts["ffn_w_down"], "nn", F32, "ffn_down", add=x1)

    dx2, g_normf, loss = _final_loss(x2, target, wts["norm_f_w"], "final_loss")

    dfact = _matmul(dx2, wts["ffn_w_down"], "nt", BF16, "ffn_down_dx")
    g_wdown = _matmul(fact, dx2, "tn", BF16, "ffn_down_dw")
    dgc, duc, dwg, dwu = _ffn_act_bwd(dfact, upre, wts["ffn_conv_w"], "ffn_act_bwd")
    g_fconv = jnp.concatenate([dwg, dwu], axis=1)
    dupre = _conv_bwd_data([dgc, duc], wts["ffn_conv_w"], FFN_CONV, BF16, "ffn_conv_bwd")
    dn2 = _matmul(dupre, wts["ffn_w_up_t"], "nn", F32, "ffn_up_dx")
    g_wup = _matmul(dupre, n2, "tn", BF16, "ffn_up_dw")
    dx1, g_norm2 = _rmsnorm_bwd(dn2, x1, wts["norm2_w"], dx2, "norm2_bwd")

    dmixed = _matmul(dx1, wts["w_out"], "nt", BF16, "out_proj_dx")
    g_wout = _matmul(mixed, dx1, "tn", BF16, "out_proj_dw")
    dpd, dps, dgd, dgs = _merge_bwd(dmixed, pd, ps, gl, "merge_bwd")
    dy_dn = _matmul(dpd, wts["w_proj_dn"], "nt", BF16, "proj_dn_dx")
    g_wpd = _matmul(y_dn, dpd, "tn", BF16, "proj_dn_dw")
    do_sb = _matmul(dps, wts["w_proj_sb"], "nt", BF16, "proj_sb_dx")
    g_wps = _matmul(o_sb, dps, "tn", BF16, "proj_sb_dw")
    grads = dict(w_proj_dn=g_wpd, w_proj_sb=g_wps, w_out=g_wout, ffn_w_up_t=g_wup, ffn_w_down=g_wdown)

    (dsq, dsk, dsv), got_early = _sb_bwd(sbqkv, o_sb, do_sb, "sb_bwd",
                                         comm=plan.early_grads(grads) if plan else None)

    do_dn, dhgate, g_dnnorm = _dn_post_bwd(dy_dn, o_dn, hgate, wts["dn_norm_w"], "dn_post_bwd")
    dvn, dsh = _dn_scan_bwd(do_dn, w, kd, qg, p, gates, "dn_scan_bwd")
    dq, dk, dv, dgates = _dn_local_bwd(act, gates, u, w, kd, qg, tinv, p, sh, dsh, dvn, do_dn, "dn_local_bwd")
    dhab, g_alog, g_dtb = _dn_gates_bwd(dgates, hab, wts["alog"], wts["dtb"], "dn_gates_bwd")
    dcv, g_dnconv = _dn_pre_bwd(dq, dk, dv, qkv_pre, wts["dn_conv_w"], "dn_pre_bwd")
    dqkv_pre = _conv_bwd_data([dcv], wts["dn_conv_w"], DN_CONV, BF16, "dn_conv_bwd")

    dh = jnp.concatenate([dqkv_pre, dhgate, dsq.astype(BF16), dsk.astype(BF16), dsv.astype(BF16), dgd, dgs], axis=1)
    w_main_t = jnp.concatenate([wts["w_dnqkv_t"], wts["w_dngate_t"], wts["w_sbqkv_t"], wts["w_gl_t"]], axis=0)
    g_wmain = _matmul(dh, n1, "tn", BF16, "in_dw_main")
    g_wab = _matmul(dhab, n1, "tn", BF16, "in_dw_ab")
    grads.update(w_main_t=g_wmain, w_ab_t=g_wab, dn_conv_w=g_dnconv, alog=g_alog, dtb=g_dtb, dn_norm_w=g_dnnorm,
                 norm2_w=g_norm2, ffn_conv_w=g_fconv, norm_f_w=g_normf)
    got_late = []
    if plan:
        dn1, swapped = _matmul(dhab, wts["w_ab_t"], "nn", F32, "in_dx_ab", comm=plan.sibling_swap(grads))
        dn1, got_late = _matmul(dh, w_main_t, "nn", F32, "in_dx_main", add=dn1,
                                comm=plan.late_grads(swapped, grads, loss))
    else:
        dn1 = _matmul(dhab, wts["w_ab_t"], "nn", F32, "in_dx_ab")
        dn1 = _matmul(dh, w_main_t, "nn", F32, "in_dx_main", add=dn1)
    grad_x, g_norm1 = _rmsnorm_bwd(dn1, x, wts["norm1_w"], dx1, "norm1_bwd")
    grads["norm1_w"] = g_norm1
    return loss, grad_x, grads, got_early, got_late


HBM_SPEC = pl.BlockSpec(memory_space=pltpu.HBM)


def _mesh_pos():
    x, y, c = lax.axis_index("x"), lax.axis_index("y"), lax.axis_index("c")
    return x, y, c, 4 * x + 2 * y + c


def _peer(k):
    x, y, c, _ = _mesh_pos()
    px = 1 - x if k & 4 else x
    py = 1 - y if k & 2 else y
    pc = 1 - c if k & 1 else c
    return (px, py, pc), 4 * px + 2 * py + pc


def _rcopy(src, dst, send, recv, a, s, peer):
    return pltpu.make_async_remote_copy(src_ref=src, dst_ref=dst, send_sem=send.at[a, s], recv_sem=recv.at[a, s],
                                        device_id=peer, device_id_type=pl.DeviceIdType.MESH)


class _Gather:
    ICI = (2, 4, 6)

    def __init__(self, shards):
        self.args = list(shards)
        self.n = len(shards)
        self.out_shape = [jax.ShapeDtypeStruct((N_DEV,) + s.shape, s.dtype) for s in shards]
        self.scratch = [pltpu.SemaphoreType.DMA((self.n, N_DEV - 1)), pltpu.SemaphoreType.DMA((self.n, N_DEV - 1)),
                        pltpu.SemaphoreType.DMA((self.n,))]

    def _first(self, ins, outs, send, recv, a):
        me = _mesh_pos()[3]
        out, got = [], []
        for s, k in enumerate((1,) + self.ICI):
            peer, pidx = _peer(k)
            out.append(_rcopy(ins[a], outs[a].at[me], send, recv, a, s, peer))
            got.append(_rcopy(ins[a], outs[a].at[pidx], send, recv, a, s, peer))
        return out, got

    def _forward(self, ins, outs, send, recv, a):
        sib = _peer(1)[0]
        out, got = [], []
        for s, k in enumerate(self.ICI):
            held = outs[a].at[_peer(k)[1]]
            out.append(_rcopy(held, held, send, recv, a, 4 + s, sib))
            other = outs[a].at[_peer(k | 1)[1]]
            got.append(_rcopy(other, other, send, recv, a, 4 + s, sib))
        return out, got

    def start(self, ins, outs, sems):
        send, recv, loc = sems
        me = _mesh_pos()[3]
        for a in range(self.n):
            pltpu.make_async_copy(ins[a], outs[a].at[me], loc.at[a]).start()
            for cp in self._first(ins, outs, send, recv, a)[0]:
                cp.start()

    def mid(self, ins, outs, sems):
        send, recv, _ = sems
        for a in range(self.n):
            arrivals = self._first(ins, outs, send, recv, a)[1]
            for s, cp in enumerate(self._forward(ins, outs, send, recv, a)[0]):
                arrivals[1 + s].wait_recv()
                cp.start()

    def finish(self, ins, outs, sems):
        send, recv, loc = sems
        me = _mesh_pos()[3]
        for a in range(self.n):
            first_out, first_got = self._first(ins, outs, send, recv, a)
            fwd_out, fwd_got = self._forward(ins, outs, send, recv, a)
            first_got[0].wait_recv()
            for cp in fwd_got:
                cp.wait_recv()
            for cp in first_out + fwd_out:
                cp.wait_send()
            pltpu.make_async_copy(ins[a], outs[a].at[me], loc.at[a]).wait()


class _Exchange:
    def __init__(self, slabs=(), gathered=(), chip_slabs=(), sibling_slabs=()):
        self.args = list(slabs) + list(chip_slabs) + list(sibling_slabs) + list(gathered)
        self.kind = (["dev"] * len(slabs) + ["chip"] * len(chip_slabs) + ["sib"] * len(sibling_slabs)
                     + ["all"] * len(gathered))
        self.n = len(self.args)
        half = lambda s: jax.ShapeDtypeStruct((N_DEV // 2,) + s.shape[1:], s.dtype)
        self.out_shape = ([jax.ShapeDtypeStruct(s.shape, s.dtype) for s in slabs]
                          + [half(s) for s in chip_slabs] + [half(s) for s in sibling_slabs]
                          + [jax.ShapeDtypeStruct((N_DEV,) + s.shape, s.dtype) for s in gathered])
        self.scratch = [pltpu.SemaphoreType.DMA((self.n, N_DEV - 1)), pltpu.SemaphoreType.DMA((self.n, N_DEV - 1)),
                        pltpu.SemaphoreType.DMA((self.n,))]

    def _copies(self, ins, outs, send, recv, a):
        x, y, c, me = _mesh_pos()
        kind = self.kind[a]
        out, got = [], []
        if kind == "sib":
            sib = _peer(1)[0]
            for q in range(N_DEV // 2):
                out.append(_rcopy(ins[a].at[2 * q + 1 - c], outs[a].at[q], send, recv, a, q, sib))
                got.append(_rcopy(ins[a].at[2 * q + c], outs[a].at[q], send, recv, a, q, sib))
            return out, got
        for k in ((2, 4, 6) if kind == "chip" else range(1, N_DEV)):
            peer, pidx = _peer(k)
            if kind == "chip":
                src, mine, theirs = ins[a].at[2 * peer[0] + peer[1]], 2 * x + y, 2 * peer[0] + peer[1]
            else:
                src, mine, theirs = (ins[a].at[pidx] if kind == "dev" else ins[a]), me, pidx
            out.append(_rcopy(src, outs[a].at[mine], send, recv, a, k - 1, peer))
            got.append(_rcopy(src, outs[a].at[theirs], send, recv, a, k - 1, peer))
        return out, got

    def _local(self, ins, outs, loc, a):
        x, y, _, me = _mesh_pos()
        kind = self.kind[a]
        if kind == "sib":
            return None
        if kind == "chip":
            return pltpu.make_async_copy(ins[a].at[2 * x + y], outs[a].at[2 * x + y], loc.at[a])
        return pltpu.make_async_copy(ins[a].at[me] if kind == "dev" else ins[a], outs[a].at[me], loc.at[a])

    def start(self, ins, outs, sems):
        send, recv, loc = sems
        for a in range(self.n):
            if self._local(ins, outs, loc, a) is not None:
                self._local(ins, outs, loc, a).start()
            for cp in self._copies(ins, outs, send, recv, a)[0]:
                cp.start()

    def mid(self, ins, outs, sems):
        pass

    def finish(self, ins, outs, sems):
        send, recv, loc = sems
        for a in range(self.n):
            out, got = self._copies(ins, outs, send, recv, a)
            for cp in got:
                cp.wait_recv()
            for cp in out:
                cp.wait_send()
            if self._local(ins, outs, loc, a) is not None:
                self._local(ins, outs, loc, a).wait()


def _comm_call(comm, name):
    n = comm.n

    def body(*refs):
        ins, outs, sems = refs[:n], refs[n:2 * n], refs[2 * n:]
        comm.start(ins, outs, sems)
        comm.mid(ins, outs, sems)
        comm.finish(ins, outs, sems)

    return pl.pallas_call(
        body, name=name, out_shape=comm.out_shape, in_specs=[HBM_SPEC] * n, out_specs=[HBM_SPEC] * n,
        scratch_shapes=comm.scratch,
    )(*comm.args)


def _hosted(body, comm, n_in, n_out, when):
    if comm is None:
        return body

    def wrapped(*refs):
        ins, c_ins = refs[:n_in], refs[n_in:n_in + comm.n]
        o0 = n_in + comm.n
        outs, c_outs = refs[o0:o0 + n_out], refs[o0 + n_out:o0 + n_out + comm.n]
        scratch, sems = refs[o0 + n_out + comm.n:len(refs) - 3], refs[len(refs) - 3:]
        first, middle, last = when()

        @pl.when(first)
        def _():
            comm.start(c_ins, c_outs, sems)

        body(*ins, *outs, *scratch)

        @pl.when(middle)
        def _():
            comm.mid(c_ins, c_outs, sems)

        @pl.when(last)
        def _():
            comm.finish(c_ins, c_outs, sems)

    return wrapped


def _host_call(body, name, comm, when, out_shape, grid, in_specs, out_specs, scratch_shapes, sem, args):
    n_in, n_out = len(in_specs), len(out_specs)
    if comm is None:
        res = pl.pallas_call(body, name=name, out_shape=out_shape, grid=grid, in_specs=in_specs, out_specs=out_specs,
                             scratch_shapes=scratch_shapes, compiler_params=_params(sem))(*args)
        return list(res), []
    res = pl.pallas_call(
        _hosted(body, comm, n_in, n_out, when), name=name,
        out_shape=list(out_shape) + comm.out_shape, grid=grid,
        in_specs=list(in_specs) + [HBM_SPEC] * comm.n, out_specs=list(out_specs) + [HBM_SPEC] * comm.n,
        scratch_shapes=list(scratch_shapes) + comm.scratch,
        compiler_params=_params(("arbitrary",) * len(grid)),
    )(*args, *comm.args)
    return list(res[:n_out]), list(res[n_out:])


def _add_my_slabs(slabs, b, name):
    n, rows, cols = b.shape
    tc = _tile(cols, 256)

    def body(a_ref, b_ref, o_ref):
        o_ref[...] = (a_ref[...].astype(F32) + b_ref[...].astype(F32)).astype(o_ref.dtype)

    blk = pl.BlockSpec((None, rows, tc), lambda i, j: (i, 0, j))
    mine = pl.BlockSpec((None, rows, tc), lambda i, j: (2 * i + lax.axis_index("c"), 0, j))
    return pl.pallas_call(
        body, name=name, out_shape=jax.ShapeDtypeStruct(b.shape, b.dtype), grid=(n, cols // tc),
        in_specs=[mine, blk], out_specs=blk, compiler_params=_params(("parallel", "parallel")),
    )(slabs, b)


def _adamw(parts, w, m, v, name):
    rows, cols = w.shape
    nparts = parts.shape[0]
    tr, tc = rows, cols
    for cand in (128, 176):
        if rows > cand and rows % cand == 0:
            tr = cand
            break
    if tr == rows and rows > 512:
        tc = _tile(cols, 256)

    def body(p_ref, w_ref, m_ref, v_ref, g_ref, d_ref, mo_ref, vo_ref):
        g = p_ref[0].astype(F32)
        for s in range(1, nparts):
            g = g + p_ref[s].astype(F32)
        mn = ADAM_B1 * m_ref[...] + (1.0 - ADAM_B1) * g
        vn = ADAM_B2 * v_ref[...] + (1.0 - ADAM_B2) * (g * g)
        m_hat = mn / (1.0 - ADAM_B1 ** ADAM_STEP)
        v_hat = vn / (1.0 - ADAM_B2 ** ADAM_STEP)
        g_ref[...] = g
        d_ref[...] = -ADAM_LR * (m_hat / (jnp.sqrt(v_hat) + ADAM_EPS) + ADAM_WD * w_ref[...])
        mo_ref[...] = mn
        vo_ref[...] = vn

    blk = pl.BlockSpec((tr, tc), lambda i, j: (i, j))
    out = jax.ShapeDtypeStruct((rows, cols), F32)
    return pl.pallas_call(
        body, name=name,
        out_shape=(out, out, out, out),
        grid=(rows // tr, cols // tc),
        in_specs=[pl.BlockSpec((nparts, tr, tc), lambda i, j: (0, i, j)), blk, blk, blk],
        out_specs=(blk, blk, blk, blk),
        compiler_params=_params(("parallel", "parallel")),
    )(parts, w, m, v)


CONV_PACK = 8 * 1024
WEIGHT_ORDER = ("norm1_w", "w_in", "dn_conv_w", "dn_A_log", "dn_dt_bias", "dn_norm_w", "w_proj_dn", "w_proj_sb",
                "w_out", "norm2_w", "ffn_w_up", "ffn_conv_w", "ffn_w_down", "norm_f_w")


def _cols_to_slabs(g):
    r, c8 = g.shape
    return g.reshape(r, N_DEV, c8 // N_DEV).transpose(1, 0, 2)


def _slabs_to_cols(s):
    d, r, c = s.shape
    return s.transpose(1, 0, 2).reshape(r, d * c)


def kernel(x, norm1_w, w_in, dn_conv_w, dn_A_log, dn_dt_bias, dn_norm_w, w_proj_dn, w_proj_sb, w_out, norm2_w, ffn_w_up, ffn_conv_w, ffn_w_down, norm_f_w, loss_target, m_norm1_w, m_w_in, m_dn_conv_w, m_dn_A_log, m_dn_dt_bias, m_dn_norm_w, m_w_proj_dn, m_w_proj_sb, m_w_out, m_norm2_w, m_ffn_w_up, m_ffn_conv_w, m_ffn_w_down, m_norm_f_w, v_norm1_w, v_w_in, v_dn_conv_w, v_dn_A_log, v_dn_dt_bias, v_dn_norm_w, v_w_proj_dn, v_w_proj_sb, v_w_out, v_norm2_w, v_ffn_w_up, v_ffn_conv_w, v_ffn_w_down, v_norm_f_w):
    me = _mesh_pos()[3]
    tr = lambda a: jnp.transpose(a[0])
    w_loc = dict(norm1_w=norm1_w, w_in=tr(w_in), dn_conv_w=dn_conv_w[0], dn_A_log=dn_A_log, dn_dt_bias=dn_dt_bias,
                 dn_norm_w=dn_norm_w, w_proj_dn=w_proj_dn[0], w_proj_sb=w_proj_sb[0], w_out=w_out[0],
                 norm2_w=norm2_w, ffn_w_up=tr(ffn_w_up), ffn_conv_w=ffn_conv_w[0], ffn_w_down=ffn_w_down[0],
                 norm_f_w=norm_f_w[None, :])
    m_loc = dict(norm1_w=m_norm1_w, w_in=tr(m_w_in), dn_conv_w=m_dn_conv_w[0], dn_A_log=m_dn_A_log,
                 dn_dt_bias=m_dn_dt_bias, dn_norm_w=m_dn_norm_w, w_proj_dn=m_w_proj_dn[0], w_proj_sb=m_w_proj_sb[0],
                 w_out=m_w_out[0], norm2_w=m_norm2_w, ffn_w_up=tr(m_ffn_w_up), ffn_conv_w=m_ffn_conv_w[0],
                 ffn_w_down=m_ffn_w_down[0], norm_f_w=m_norm_f_w[None, :])
    v_loc = dict(norm1_w=v_norm1_w, w_in=tr(v_w_in), dn_conv_w=v_dn_conv_w[0], dn_A_log=v_dn_A_log,
                 dn_dt_bias=v_dn_dt_bias, dn_norm_w=v_dn_norm_w, w_proj_dn=v_w_proj_dn[0], w_proj_sb=v_w_proj_sb[0],
                 w_out=v_w_out[0], norm2_w=v_norm2_w, ffn_w_up=tr(v_ffn_w_up), ffn_conv_w=v_ffn_conv_w[0],
                 ffn_w_down=v_ffn_w_down[0], norm_f_w=v_norm_f_w[None, :])

    conv_flat = jnp.concatenate([w_loc["dn_conv_w"].reshape(-1), w_loc["ffn_conv_w"].reshape(-1)])
    n_dn, n_ffn = DN_CONV * 3 * WIDTH // N_DEV, FFN_CONV * 2 * D_FF // N_DEV
    conv_pack = jnp.pad(conv_flat, (0, CONV_PACK - n_dn - n_ffn)).reshape(8, 1024)
    g_in, g_conv = _comm_call(_Gather([w_loc["w_in"].astype(BF16), conv_pack]), "gather_first")
    in_width = g_in.shape[0] * g_in.shape[1]
    w_in_t = g_in.reshape(in_width, D_MODEL)
    g_conv = g_conv.reshape(N_DEV, CONV_PACK)
    dn_conv_full = _slabs_to_cols(g_conv[:, :n_dn].reshape(N_DEV, DN_CONV, 3 * WIDTH // N_DEV))
    ffn_conv_full = _slabs_to_cols(g_conv[:, n_dn:n_dn + n_ffn].reshape(N_DEV, FFN_CONV, 2 * D_FF // N_DEV))
    q_end = 3 * WIDTH
    ab_end = q_end + 2 * HEADS
    gate_end = ab_end + WIDTH
    sb_end = gate_end + 3 * WIDTH
    pad_lanes = lambda a: jnp.pad(a, ((0, 0), (0, 128 - a.shape[1])))
    wts = dict(
        norm1_w=norm1_w, w_dnqkv_t=w_in_t[:q_end], w_ab_t=jnp.pad(w_in_t[q_end:ab_end], ((0, 128 - 2 * HEADS), (0, 0))),
        w_dngate_t=w_in_t[ab_end:gate_end], w_sbqkv_t=w_in_t[gate_end:sb_end], w_gl_t=w_in_t[sb_end:],
        dn_conv_w=dn_conv_full, alog=pad_lanes(dn_A_log), dtb=pad_lanes(dn_dt_bias), dn_norm_w=dn_norm_w,
        norm2_w=norm2_w, ffn_conv_w=ffn_conv_full, norm_f_w=norm_f_w[None, :])

    n_fc = FFN_CONV * 2 * D_FF
    fc_rows = -(-n_fc // D_MODEL)
    dn_rows = DN_CONV * 3 * WIDTH // D_MODEL
    late_names = ("w_proj_dn", "w_proj_sb", "w_out", "ffn_w_up", "ffn_w_down")

    class Plan:
        @staticmethod
        def late_gather():
            return _Gather([w_loc[k].astype(BF16) for k in late_names])

        @staticmethod
        def late_weights(got):
            g_pd, g_ps, g_out, g_up, g_down = got
            return dict(w_proj_dn=g_pd.reshape(WIDTH, D_MODEL), w_proj_sb=g_ps.reshape(WIDTH, D_MODEL),
                        w_out=g_out.reshape(D_MODEL, D_MODEL), ffn_w_up_t=g_up.reshape(2 * D_FF, D_MODEL),
                        ffn_w_down=g_down.reshape(D_FF, D_MODEL))

        @staticmethod
        def early_grads(g):
            return _Exchange([g["w_proj_dn"].reshape(N_DEV, WIDTH // N_DEV, D_MODEL),
                              g["w_proj_sb"].reshape(N_DEV, WIDTH // N_DEV, D_MODEL),
                              g["w_out"].reshape(N_DEV, D_MODEL // N_DEV, D_MODEL),
                              g["ffn_w_up_t"].reshape(N_DEV, 2 * D_FF // N_DEV, D_MODEL),
                              g["ffn_w_down"].reshape(N_DEV, D_FF // N_DEV, D_MODEL)])

        @staticmethod
        def _in_slabs(g):
            g_win_t = jnp.concatenate([g["w_main_t"][:q_end], g["w_ab_t"][:2 * HEADS], g["w_main_t"][q_end:]],
                                      axis=0)
            return g_win_t.reshape(N_DEV, in_width // N_DEV, D_MODEL)

        @staticmethod
        def sibling_swap(g):
            return _Exchange(sibling_slabs=[Plan._in_slabs(g)])

        @staticmethod
        def late_grads(swapped, g, loss):
            chip_sums = _add_my_slabs(Plan._in_slabs(g), swapped[0], "in_dw_chip_sum")
            row3 = jnp.concatenate([g["dn_norm_w"], g["alog"], g["dtb"], jnp.pad(loss, ((0, 0), (0, 127))),
                                    jnp.zeros((1, D_MODEL - 512), F32)], axis=1)
            fconv_rows = jnp.pad(g["ffn_conv_w"].reshape(-1), (0, fc_rows * D_MODEL - n_fc)).reshape(fc_rows, D_MODEL)
            pad8 = lambda a: jnp.pad(a, ((0, -a.shape[0] % 8), (0, 0)))
            pieces = [g["norm2_w"], g["norm_f_w"], row3, g["dn_conv_w"].reshape(dn_rows, D_MODEL), fconv_rows]
            small = jnp.concatenate([pad8(a) for a in pieces], axis=0)
            assert small.shape[0] == SMALL_ROWS
            return _Exchange(chip_slabs=[chip_sums], gathered=[small])

    loss, grad_x, g, got_early, got_late = _local_step(x[0], loss_target[0], wts, Plan)
    r_pd, r_ps, r_out, r_up, r_down = got_early
    r_in, r_small = got_late
    (r_norm1,) = _comm_call(_Exchange([], [jnp.pad(g["norm1_w"], ((0, 7), (0, 0)))]), "gather_norm1")

    parts = dict(w_in=r_in, w_proj_dn=r_pd, w_proj_sb=r_ps, w_out=r_out, ffn_w_up=r_up, ffn_w_down=r_down)
    parts["norm1_w"] = r_norm1[:, 0:1, :]
    parts["norm2_w"] = r_small[:, 0:1, :]
    parts["norm_f_w"] = r_small[:, 8:9, :]
    parts["dn_norm_w"] = r_small[:, 16:17, 0:HEAD_DIM]
    parts["dn_A_log"] = r_small[:, 16:17, 128:128 + HEADS]
    parts["dn_dt_bias"] = r_small[:, 16:17, 256:256 + HEADS]
    dnc = r_small[:, 24:24 + dn_rows, :].reshape(N_DEV, DN_CONV, 3 * WIDTH)
    parts["dn_conv_w"] = lax.dynamic_slice_in_dim(dnc, me * (3 * WIDTH // N_DEV), 3 * WIDTH // N_DEV, axis=2)
    fc0 = 24 + dn_rows + (-dn_rows % 8)
    fcc = r_small[:, fc0:fc0 + fc_rows, :].reshape(N_DEV, fc_rows * D_MODEL)[:, :n_fc]
    fcc = fcc.reshape(N_DEV, FFN_CONV, 2 * D_FF)
    parts["ffn_conv_w"] = lax.dynamic_slice_in_dim(fcc, me * (2 * D_FF // N_DEV), 2 * D_FF // N_DEV, axis=2)
    loss_total = jnp.sum(r_small[:, 16, 384])

    res = {k: _adamw(parts[k], w_loc[k], m_loc[k], v_loc[k], "adamw_" + k) for k in WEIGHT_ORDER}
    lead = ("w_in", "dn_conv_w", "w_proj_dn", "w_proj_sb", "w_out", "ffn_w_up", "ffn_conv_w", "ffn_w_down")

    def shaped(k, a):
        if k in ("w_in", "ffn_w_up"):
            return jnp.transpose(a)[None]
        if k in lead:
            return a[None]
        if k == "norm_f_w":
            return a[0]
        return a

    outs = [loss_total, grad_x[None]]
    for idx in range(4):
        outs += [shaped(k, res[k][idx]) for k in WEIGHT_ORDER]
    return tuple(outs)
```

```python
import functools

import jax
import jax.numpy as jnp
from jax import lax
from jax.experimental import pallas as pl
from jax.experimental.pallas import tpu as pltpu

F32 = jnp.float32
BF16 = jnp.bfloat16

N_DEV = 8
D_MODEL = 1024
HEADS = 8
HEAD_DIM = 128
WIDTH = HEADS * HEAD_DIM
DN_CONV = 4
DN_CHUNK = 64
D_FF = 2816
FFN_CONV = 3
EPS = 1e-6
HALO = 16
CHUNK_ROWS = 256
ATT_BLOCK = 256
SB_LOG_ZERO = -104.0
SB_GROUP = 2
SB_HEADS_FWD = 4
SB_HEADS_BWD = 2
SMALL_ROWS = 64

ADAM_LR = 0.001
ADAM_B1 = 0.9
ADAM_B2 = 0.999
ADAM_EPS = 1e-08
ADAM_WD = 0.01
ADAM_STEP = 10

VMEM_LIMIT = 48 * 1024 * 1024


def _params(sem=None, **kw):
    return pltpu.CompilerParams(dimension_semantics=sem, vmem_limit_bytes=VMEM_LIMIT, **kw)


def _tile(n, cap):
    if n <= cap:
        return n
    best = None
    for t in range(128, cap + 1, 128):
        if n % t == 0:
            best = t
    assert best is not None, (n, cap)
    return best


def _dot(a, b, dims):
    return lax.dot_general(a, b, ((dims[0], dims[1]), ((), ())), preferred_element_type=F32)


NN = ((1,), (0,))
NT = ((1,), (1,))
TN = ((0,), (0,))


def _dotb(a, b, dims):
    return _dot(a.astype(BF16), b.astype(BF16), dims)


def _split3(x):
    h1 = x.astype(BF16)
    r1 = x - h1.astype(F32)
    h2 = r1.astype(BF16)
    r2 = r1 - h2.astype(F32)
    return h1, h2, r2.astype(BF16)


def _dot_xr(a, b_exact, dims):
    a1, a2, a3 = _split3(a)
    return _dot(a1, b_exact, dims) + _dot(a2, b_exact, dims) + _dot(a3, b_exact, dims)


def _split2(x):
    h1 = x.astype(BF16)
    return h1, (x - h1.astype(F32)).astype(BF16)


def _dot_xr2(a, b_exact, dims):
    a1, a2 = _split2(a)
    return _dot(a1, b_exact, dims) + _dot(a2, b_exact, dims)


def _dot_xl(a_exact, b, dims):
    b1, b2, b3 = _split3(b)
    return _dot(a_exact, b1, dims) + _dot(a_exact, b2, dims) + _dot(a_exact, b3, dims)


def _dot3(a, b, dims):
    a1 = a.astype(BF16)
    a2 = (a - a1.astype(F32)).astype(BF16)
    b1 = b.astype(BF16)
    b2 = (b - b1.astype(F32)).astype(BF16)
    return _dot(a1, b1, dims) + (_dot(a1, b2, dims) + _dot(a2, b1, dims))


def _sigmoid(x):
    return 1.0 / (1.0 + jnp.exp(-x))


def _log1pexp_neg_abs(x):
    return jnp.log(1.0 + jnp.exp(-jnp.abs(x)))


def _iota(shape, dim):
    return lax.broadcasted_iota(jnp.int32, shape, dim)


def _matmul(a, b, mode, out_dtype, name, add=None, comm=None):
    if mode == "nn":
        (m, k), (k2, n) = a.shape, b.shape
    elif mode == "nt":
        (m, k), (n, k2) = a.shape, b.shape
    else:
        (k, m), (k2, n) = a.shape, b.shape
    assert k == k2, (a.shape, b.shape, mode)
    tm, tn, tk = _tile(m, 1408), _tile(n, 1408), _tile(k, 1536)
    nk = k // tk
    dims = {"nn": NN, "nt": NT, "tn": TN}[mode]

    def body(*refs):
        if add is None:
            a_ref, b_ref, o_ref, acc_ref = refs
        else:
            a_ref, b_ref, add_ref, o_ref, acc_ref = refs
        kk = pl.program_id(2)

        @pl.when(kk == 0)
        def _():
            acc_ref[...] = jnp.zeros_like(acc_ref)

        acc_ref[...] += _dotb(a_ref[...], b_ref[...], dims)

        @pl.when(kk == nk - 1)
        def _():
            r = acc_ref[...]
            if add is not None:
                r = r + add_ref[...].astype(F32)
            o_ref[...] = r.astype(out_dtype)

    if mode == "nn":
        specs = [pl.BlockSpec((tm, tk), lambda i, j, l: (i, l)), pl.BlockSpec((tk, tn), lambda i, j, l: (l, j))]
    elif mode == "nt":
        specs = [pl.BlockSpec((tm, tk), lambda i, j, l: (i, l)), pl.BlockSpec((tn, tk), lambda i, j, l: (j, l))]
    else:
        specs = [pl.BlockSpec((tk, tm), lambda i, j, l: (l, i)), pl.BlockSpec((tk, tn), lambda i, j, l: (l, j))]
    args = [a, b]
    if add is not None:
        specs.append(pl.BlockSpec((tm, tn), lambda i, j, l: (i, j)))
        args.append(add)
    grid = (m // tm, n // tn, nk)

    def when():
        i, j, l = pl.program_id(0), pl.program_id(1), pl.program_id(2)
        first = jnp.logical_and(jnp.logical_and(i == 0, j == 0), l == 0)
        last = jnp.logical_and(jnp.logical_and(i == grid[0] - 1, j == grid[1] - 1), l == nk - 1)
        return first, last, last

    (out,), extra = _host_call(
        body, name, comm, when, [jax.ShapeDtypeStruct((m, n), out_dtype)], grid, specs,
        [pl.BlockSpec((tm, tn), lambda i, j, l: (i, j))], [pltpu.VMEM((tm, tn), F32)],
        ("parallel", "parallel", "arbitrary"), args)
    return out if comm is None else (out, extra)


PART_TILE = 512


def _matmul_kparts(a_parts, b_parts, out_dtype, name, add=None, comm=None):
    m = a_parts[0].shape[0]
    n = b_parts[0][0].shape[1]
    tm, tn, tk = _tile(m, 512), _tile(n, 1408), PART_TILE
    steps = [a.shape[1] // tk for a in a_parts]
    starts = [sum(steps[:p]) for p in range(len(a_parts))]
    nk = sum(steps)
    b_of, b_start, b_steps, p0 = [], [], [], 0
    for q, (arr, span) in enumerate(b_parts):
        b_of += [q] * span
        b_start.append(starts[p0])
        b_steps.append(sum(steps[p0:p0 + span]))
        assert arr.shape[0] == b_steps[-1] * tk, (arr.shape, b_steps[-1])
        p0 += span
    na, nb = len(a_parts), len(b_parts)

    def body(*refs):
        a_refs, b_refs = refs[:na], refs[na:na + nb]
        rest = refs[na + nb:]
        add_ref = rest[0] if add is not None else None
        o_ref, acc_ref = rest[-2], rest[-1]
        kk = pl.program_id(2)

        @pl.when(kk == 0)
        def _():
            acc_ref[...] = jnp.zeros_like(acc_ref)

        for p in range(na):
            @pl.when(jnp.logical_and(kk >= starts[p], kk < starts[p] + steps[p]))
            def _(p=p):
                acc_ref[...] += _dotb(a_refs[p][...], b_refs[b_of[p]][...], NN)

        @pl.when(kk == nk - 1)
        def _():
            r = acc_ref[...]
            if add is not None:
                r = r + add_ref[...].astype(F32)
            o_ref[...] = r.astype(out_dtype)

    clamp = lambda l, s, c: jnp.clip(l - s, 0, c - 1)
    specs = [pl.BlockSpec((tm, tk), lambda i, j, l, s=starts[p], c=steps[p]: (i, clamp(l, s, c))) for p in range(na)]
    specs += [pl.BlockSpec((tk, tn), lambda i, j, l, s=b_start[q], c=b_steps[q]: (clamp(l, s, c), j)) for q in range(nb)]
    args = list(a_parts) + [arr for arr, _ in b_parts]
    if add is not None:
        specs.append(pl.BlockSpec((tm, tn), lambda i, j, l: (i, j)))
        args.append(add)
    grid = (m // tm, n // tn, nk)

    def when():
        i, j, l = pl.program_id(0), pl.program_id(1), pl.program_id(2)
        first = jnp.logical_and(jnp.logical_and(i == 0, j == 0), l == 0)
        last = jnp.logical_and(jnp.logical_and(i == grid[0] - 1, j == grid[1] - 1), l == nk - 1)
        return first, last, last

    (out,), extra = _host_call(
        body, name, comm, when, [jax.ShapeDtypeStruct((m, n), out_dtype)], grid, specs,
        [pl.BlockSpec((tm, tn), lambda i, j, l: (i, j))], [pltpu.VMEM((tm, tn), F32)],
        ("parallel", "parallel", "arbitrary"), args)
    return out if comm is None else (out, extra)


def _matmul_mparts(a_parts, b, out_dtype, name):
    k, n = b.shape
    tm, tn, tk = PART_TILE, _tile(n, 1408), _tile(k, 1024)
    steps = [a.shape[1] // tm for a in a_parts]
    starts = [sum(steps[:p]) for p in range(len(a_parts))]
    nm, nk, na = sum(steps), k // tk, len(a_parts)

    def body(*refs):
        a_refs, b_ref, o_ref, acc_ref = refs[:na], refs[na], refs[na + 1], refs[na + 2]
        ii, kk = pl.program_id(0), pl.program_id(2)

        @pl.when(kk == 0)
        def _():
            acc_ref[...] = jnp.zeros_like(acc_ref)

        for p in range(na):
            @pl.when(jnp.logical_and(ii >= starts[p], ii < starts[p] + steps[p]))
            def _(p=p):
                acc_ref[...] += _dotb(a_refs[p][...], b_ref[...], TN)

        @pl.when(kk == nk - 1)
        def _():
            o_ref[...] = acc_ref[...].astype(out_dtype)

    def a_map(s, c):
        def index(i, j, l):
            inside = jnp.logical_and(i >= s, i < s + c)
            return jnp.where(inside, l, 0), jnp.clip(i - s, 0, c - 1)
        return index

    specs = [pl.BlockSpec((tk, tm), a_map(starts[p], steps[p])) for p in range(na)]
    specs.append(pl.BlockSpec((tk, tn), lambda i, j, l: (l, j)))
    return pl.pallas_call(
        body, name=name, out_shape=jax.ShapeDtypeStruct((nm * tm, n), out_dtype), grid=(nm, n // tn, nk),
        in_specs=specs, out_specs=pl.BlockSpec((tm, tn), lambda i, j, l: (i, j)),
        scratch_shapes=[pltpu.VMEM((tm, tn), F32)],
        compiler_params=_params(("parallel", "parallel", "arbitrary")),
    )(*a_parts, b)


def _rmsnorm_fwd(x, w, name):
    t, d = x.shape
    tr = _tile(t, 512)

    def body(x_ref, w_ref, o_ref):
        xv = x_ref[...]
        r = lax.rsqrt(jnp.mean(xv * xv, axis=1, keepdims=True) + EPS)
        o_ref[...] = (xv * r * w_ref[...]).astype(BF16)

    return pl.pallas_call(
        body, name=name,
        out_shape=jax.ShapeDtypeStruct((t, d), BF16),
        grid=(t // tr,),
        in_specs=[pl.BlockSpec((tr, d), lambda i: (i, 0)), pl.BlockSpec((1, d), lambda i: (0, 0))],
        out_specs=pl.BlockSpec((tr, d), lambda i: (i, 0)),
        compiler_params=_params(("parallel",)),
    )(x, w)


def _rmsnorm_bwd(dn, x, w, dres, name):
    t, d = x.shape
    tr = _tile(t, 512)

    def body(dn_ref, x_ref, w_ref, dres_ref, dx_ref, dw_ref):
        i = pl.program_id(0)
        xv = x_ref[...]
        g = dn_ref[...].astype(F32)
        r = lax.rsqrt(jnp.mean(xv * xv, axis=1, keepdims=True) + EPS)
        xh = xv * r
        dxh = g * w_ref[...]
        dx = r * (dxh - xh * jnp.mean(dxh * xh, axis=1, keepdims=True))
        dx_ref[...] = dres_ref[...] + dx

        @pl.when(i == 0)
        def _():
            dw_ref[...] = jnp.zeros_like(dw_ref)

        dw_ref[...] += jnp.sum(g * xh, axis=0, keepdims=True)

    return pl.pallas_call(
        body, name=name,
        out_shape=(jax.ShapeDtypeStruct((t, d), F32), jax.ShapeDtypeStruct((1, d), F32)),
        grid=(t // tr,),
        in_specs=[pl.BlockSpec((tr, d), lambda i: (i, 0)), pl.BlockSpec((tr, d), lambda i: (i, 0)),
                  pl.BlockSpec((1, d), lambda i: (0, 0)), pl.BlockSpec((tr, d), lambda i: (i, 0))],
        out_specs=(pl.BlockSpec((tr, d), lambda i: (i, 0)), pl.BlockSpec((1, d), lambda i: (0, 0))),
        compiler_params=_params(("arbitrary",)),
    )(dn, x, w, dres)


def _final_loss(x2, target, w, name):
    t, d = x2.shape
    tr = _tile(t, 512)

    def body(x_ref, t_ref, w_ref, dx_ref, dw_ref, loss_ref):
        i = pl.program_id(0)
        xv = x_ref[...]
        r = lax.rsqrt(jnp.mean(xv * xv, axis=1, keepdims=True) + EPS)
        xh = xv * r
        err = xh * w_ref[...] - t_ref[...]
        dy = err * (1.0 / d)
        dxh = dy * w_ref[...]
        dx_ref[...] = r * (dxh - xh * jnp.mean(dxh * xh, axis=1, keepdims=True))

        @pl.when(i == 0)
        def _():
            dw_ref[...] = jnp.zeros_like(dw_ref)
            loss_ref[...] = jnp.zeros_like(loss_ref)

        dw_ref[...] += jnp.sum(dy * xh, axis=0, keepdims=True)
        row = jnp.sum(err * err, axis=1, keepdims=True) * (0.5 / d)
        loss_ref[...] += jnp.sum(row, axis=0, keepdims=True)

    return pl.pallas_call(
        body, name=name,
        out_shape=(jax.ShapeDtypeStruct((t, d), F32), jax.ShapeDtypeStruct((1, d), F32),
                   jax.ShapeDtypeStruct((1, 1), F32)),
        grid=(t // tr,),
        in_specs=[pl.BlockSpec((tr, d), lambda i: (i, 0)), pl.BlockSpec((tr, d), lambda i: (i, 0)),
                  pl.BlockSpec((1, d), lambda i: (0, 0))],
        out_specs=(pl.BlockSpec((tr, d), lambda i: (i, 0)), pl.BlockSpec((1, d), lambda i: (0, 0)),
                   pl.BlockSpec((1, 1), lambda i: (0, 0))),
        compiler_params=_params(("arbitrary",)),
    )(x2, target, w)


def _shift_down(cur, prev, k, row):
    r = pltpu.roll(cur, k, 0)
    top, row8 = r[0:8, :], row[0:8, :]
    for m in range(k):
        top = jnp.where(row8 == m, prev[HALO - k + m:HALO - k + m + 1, :], top)
    return jnp.concatenate([top, r[8:, :]], axis=0)


def _shift_up(cur, nxt, k, row, tr):
    r = pltpu.roll(cur, tr - k, 0)
    bottom, row8 = r[tr - 8:, :], row[0:8, :]
    for m in range(k):
        bottom = jnp.where(row8 == 8 - k + m, nxt[m:m + 1, :], bottom)
    return jnp.concatenate([r[:tr - 8, :], bottom], axis=0)


def _fold8(a):
    out = a[0:8, :]
    for r in range(8, a.shape[0], 8):
        out = out + a[r:r + 8, :]
    return out


def _conv_taps(cur, prev, w, ntaps, row):
    taps = [cur if i == ntaps - 1 else _shift_down(cur, prev, ntaps - 1 - i, row) for i in range(ntaps)]
    y = w[0:1, :] * taps[0]
    for i in range(1, ntaps):
        y = y + w[i:i + 1, :] * taps[i]
    return taps, y


def _conv_bwd_data(parts, w, ntaps, out_dtype, name):
    t, chp = parts[0].shape
    npart = len(parts)
    tr, tc = _tile(t, 512), _tile(chp, 1408)
    nc = chp // tc
    nhalo = t // HALO
    last = t // tr - 1

    def body(*refs):
        cur_refs, nxt_refs = refs[:npart], refs[npart:2 * npart]
        w_ref, o_ref = refs[2 * npart], refs[2 * npart + 1]
        i, j = pl.program_id(0), pl.program_id(1)
        row = _iota((tr, 128), 0)
        for c0 in range(0, tc, 128):
            sl = slice(c0, c0 + 128)
            cur, nxt = cur_refs[0][:, sl].astype(F32), nxt_refs[0][:, sl].astype(F32)
            for p in range(1, npart):
                cur = jnp.where(j >= p * nc, cur_refs[p][:, sl].astype(F32), cur)
                nxt = jnp.where(j >= p * nc, nxt_refs[p][:, sl].astype(F32), nxt)
            nxt = jnp.where(i == last, 0.0, nxt)
            wv = w_ref[:, sl]
            y = wv[ntaps - 1:ntaps, :] * cur
            for k in range(1, ntaps):
                y = y + wv[ntaps - 1 - k:ntaps - k, :] * _shift_up(cur, nxt, k, row, tr)
            o_ref[:, sl] = y.astype(out_dtype)

    col = lambda p: (lambda j: jnp.clip(j - p * nc, 0, nc - 1))
    cur_specs = [pl.BlockSpec((tr, tc), lambda i, j, c=col(p): (i, c(j))) for p in range(npart)]
    nxt_specs = [pl.BlockSpec((HALO, tc),
                              lambda i, j, c=col(p): (jnp.minimum((i + 1) * (tr // HALO), nhalo - 1), c(j)))
                 for p in range(npart)]
    return pl.pallas_call(
        body, name=name,
        out_shape=jax.ShapeDtypeStruct((t, npart * chp), out_dtype),
        grid=(t // tr, npart * nc),
        in_specs=cur_specs + nxt_specs + [pl.BlockSpec((ntaps, tc), lambda i, j: (0, j))],
        out_specs=pl.BlockSpec((tr, tc), lambda i, j: (i, j)),
        compiler_params=_params(("parallel", "parallel")),
    )(*parts, *parts, w)


def _ffn_act_fwd(upre, cw, name):
    t = upre.shape[0]
    tr, tc = _tile(t, 512), _tile(D_FF, 1408)
    nj = D_FF // tc

    def body(g_ref, gp_ref, u_ref, up_ref, wg_ref, wu_ref, o_ref):
        i = pl.program_id(0)
        row = _iota((tr, 128), 0)
        for c0 in range(0, tc, 128):
            sl = slice(c0, c0 + 128)
            gp = jnp.where(i == 0, 0.0, gp_ref[:, sl].astype(F32))
            up = jnp.where(i == 0, 0.0, up_ref[:, sl].astype(F32))
            _, gc = _conv_taps(g_ref[:, sl].astype(F32), gp, wg_ref[:, sl], FFN_CONV, row)
            _, uc = _conv_taps(u_ref[:, sl].astype(F32), up, wu_ref[:, sl], FFN_CONV, row)
            o_ref[:, sl] = (gc * _sigmoid(gc) * uc).astype(BF16)

    prev = lambda off: (lambda i, j: (jnp.maximum(i * (tr // HALO) - 1, 0), j + off))
    return pl.pallas_call(
        body, name=name,
        out_shape=jax.ShapeDtypeStruct((t, D_FF), BF16),
        grid=(t // tr, nj),
        in_specs=[pl.BlockSpec((tr, tc), lambda i, j: (i, j)), pl.BlockSpec((HALO, tc), prev(0)),
                  pl.BlockSpec((tr, tc), lambda i, j: (i, j + nj)), pl.BlockSpec((HALO, tc), prev(nj)),
                  pl.BlockSpec((FFN_CONV, tc), lambda i, j: (0, j)),
                  pl.BlockSpec((FFN_CONV, tc), lambda i, j: (0, j + nj))],
        out_specs=pl.BlockSpec((tr, tc), lambda i, j: (i, j)),
        compiler_params=_params(("parallel", "parallel")),
    )(upre, upre, upre, upre, cw, cw)


def _ffn_act_bwd(dact, upre, cw, name):
    t = upre.shape[0]
    tr, tc = _tile(t, 256), _tile(D_FF, 1408)
    nj = D_FF // tc

    def body(da_ref, g_ref, gp_ref, u_ref, up_ref, wg_ref, wu_ref, dg_ref, du_ref, dwg_ref, dwu_ref):
        i = pl.program_id(1)
        row = _iota((CHUNK_ROWS, 128), 0)

        @pl.when(i == 0)
        def _():
            dwg_ref[...] = jnp.zeros_like(dwg_ref)
            dwu_ref[...] = jnp.zeros_like(dwu_ref)

        for c0 in range(0, tc, 128):
            sl = slice(c0, c0 + 128)
            wg, wu = wg_ref[:, sl], wu_ref[:, sl]
            dwg = [jnp.zeros((8, 128), F32)] * FFN_CONV
            dwu = [jnp.zeros((8, 128), F32)] * FFN_CONV
            for r0 in range(0, tr, CHUNK_ROWS):
                rows = slice(r0, r0 + CHUNK_ROWS)
                if r0 == 0:
                    gp = jnp.where(i == 0, 0.0, gp_ref[:, sl].astype(F32))
                    up = jnp.where(i == 0, 0.0, up_ref[:, sl].astype(F32))
                else:
                    gp = g_ref[r0 - HALO:r0, sl].astype(F32)
                    up = u_ref[r0 - HALO:r0, sl].astype(F32)
                gt, gc = _conv_taps(g_ref[rows, sl].astype(F32), gp, wg, FFN_CONV, row)
                ut, uc = _conv_taps(u_ref[rows, sl].astype(F32), up, wu, FFN_CONV, row)
                da = da_ref[rows, sl].astype(F32)
                sg = _sigmoid(gc)
                dgc = da * uc * (sg * (1.0 + gc * (1.0 - sg)))
                duc = da * (gc * sg)
                dg_ref[rows, sl] = dgc.astype(BF16)
                du_ref[rows, sl] = duc.astype(BF16)
                dwg = [dwg[k] + _fold8(dgc * gt[k]) for k in range(FFN_CONV)]
                dwu = [dwu[k] + _fold8(duc * ut[k]) for k in range(FFN_CONV)]
            for k in range(FFN_CONV):
                dwg_ref[k:k + 1, sl] += jnp.sum(dwg[k], axis=0, keepdims=True)
                dwu_ref[k:k + 1, sl] += jnp.sum(dwu[k], axis=0, keepdims=True)

    prev = lambda off: (lambda j, i: (jnp.maximum(i * (tr // HALO) - 1, 0), j + off))
    blk = lambda off: pl.BlockSpec((tr, tc), lambda j, i: (i, j + off))
    wblk = lambda off: pl.BlockSpec((FFN_CONV, tc), lambda j, i: (0, j + off))
    dgc, duc, dwg, dwu = pl.pallas_call(
        body, name=name,
        out_shape=(jax.ShapeDtypeStruct((t, D_FF), BF16), jax.ShapeDtypeStruct((t, D_FF), BF16),
                   jax.ShapeDtypeStruct((FFN_CONV, D_FF), F32), jax.ShapeDtypeStruct((FFN_CONV, D_FF), F32)),
        grid=(nj, t // tr),
        in_specs=[blk(0), blk(0), pl.BlockSpec((HALO, tc), prev(0)), blk(nj), pl.BlockSpec((HALO, tc), prev(nj)),
                  wblk(0), wblk(nj)],
        out_specs=(blk(0), blk(0), wblk(0), wblk(0)),
        compiler_params=_params(("parallel", "arbitrary")),
    )(dact, upre, upre, upre, upre, cw, cw)
    return dgc, duc, dwg, dwu


def _dn_pre_fwd(qkv_pre, cw, name):
    t = qkv_pre.shape[0]
    tr = _tile(t, 512)
    scale = HEAD_DIM ** -0.5

    def body(x_ref, p_ref, w_ref, o_ref):
        i, j = pl.program_id(0), pl.program_id(1)
        row = _iota((tr, HEAD_DIM), 0)
        for h in range(HEADS):
            sl = slice(h * HEAD_DIM, (h + 1) * HEAD_DIM)
            prev = jnp.where(i == 0, 0.0, p_ref[:, sl].astype(F32))
            _, c = _conv_taps(x_ref[:, sl].astype(F32), prev, w_ref[:, sl], DN_CONV, row)
            s = c * _sigmoid(c)
            r = lax.rsqrt(jnp.sum(s * s, axis=1, keepdims=True) + EPS)
            o_ref[:, sl] = s * jnp.where(j == 0, r * scale, jnp.where(j == 1, r, 1.0))

    return pl.pallas_call(
        body, name=name,
        out_shape=jax.ShapeDtypeStruct((t, 3 * WIDTH), F32),
        grid=(t // tr, 3),
        in_specs=[pl.BlockSpec((tr, WIDTH), lambda i, j: (i, j)),
                  pl.BlockSpec((HALO, WIDTH), lambda i, j: (jnp.maximum(i * (tr // HALO) - 1, 0), j)),
                  pl.BlockSpec((DN_CONV, WIDTH), lambda i, j: (0, j))],
        out_specs=pl.BlockSpec((tr, WIDTH), lambda i, j: (i, j)),
        compiler_params=_params(("parallel", "parallel")),
    )(qkv_pre, qkv_pre, cw)


def _dn_pre_bwd(dq, dk, dv, qkv_pre, cw, name):
    t = qkv_pre.shape[0]
    tr = _tile(t, 256)
    scale = HEAD_DIM ** -0.5

    def body(dq_ref, dk_ref, dv_ref, x_ref, p_ref, w_ref, dc_ref, dw_ref):
        j, i = pl.program_id(0), pl.program_id(1)
        row = _iota((CHUNK_ROWS, HEAD_DIM), 0)

        @pl.when(i == 0)
        def _():
            dw_ref[...] = jnp.zeros_like(dw_ref)

        for h in range(HEADS):
            sl = slice(h * HEAD_DIM, (h + 1) * HEAD_DIM)
            wv = w_ref[:, sl]
            dw = [jnp.zeros((8, HEAD_DIM), F32)] * DN_CONV
            for r0 in range(0, tr, CHUNK_ROWS):
                rows = slice(r0, r0 + CHUNK_ROWS)
                if r0 == 0:
                    prev = jnp.where(i == 0, 0.0, p_ref[:, sl].astype(F32))
                else:
                    prev = x_ref[r0 - HALO:r0, sl].astype(F32)
                taps, c = _conv_taps(x_ref[rows, sl].astype(F32), prev, wv, DN_CONV, row)
                d = jnp.where(j == 0, dq_ref[rows, sl] * scale, jnp.where(j == 1, dk_ref[rows, sl], dv_ref[rows, sl]))
                sg = _sigmoid(c)
                s = c * sg
                r = lax.rsqrt(jnp.sum(s * s, axis=1, keepdims=True) + EPS)
                nh = s * r
                ds_norm = r * (d - nh * jnp.sum(nh * d, axis=1, keepdims=True))
                dc = jnp.where(j < 2, ds_norm, d) * (sg * (1.0 + c * (1.0 - sg)))
                dc_ref[rows, sl] = dc.astype(BF16)
                dw = [dw[k] + _fold8(dc * taps[k]) for k in range(DN_CONV)]
            for k in range(DN_CONV):
                dw_ref[k:k + 1, sl] += jnp.sum(dw[k], axis=0, keepdims=True)

    dspec = lambda p: pl.BlockSpec((tr, WIDTH), lambda j, i: (jnp.where(j == p, i, 0), 0))
    return pl.pallas_call(
        body, name=name,
        out_shape=(jax.ShapeDtypeStruct((t, 3 * WIDTH), BF16), jax.ShapeDtypeStruct((DN_CONV, 3 * WIDTH), F32)),
        grid=(3, t // tr),
        in_specs=[dspec(0), dspec(1), dspec(2),
                  pl.BlockSpec((tr, WIDTH), lambda j, i: (i, j)),
                  pl.BlockSpec((HALO, WIDTH), lambda j, i: (jnp.maximum(i * (tr // HALO) - 1, 0), j)),
                  pl.BlockSpec((DN_CONV, WIDTH), lambda j, i: (0, j))],
        out_specs=(pl.BlockSpec((tr, WIDTH), lambda j, i: (i, j)),
                   pl.BlockSpec((DN_CONV, WIDTH), lambda j, i: (0, j))),
        compiler_params=_params(("parallel", "arbitrary")),
    )(dq, dk, dv, qkv_pre, qkv_pre, cw)


def _tri(n, kind):
    r, c = _iota((n, n), 0), _iota((n, n), 1)
    m = {"lower": r >= c, "strict": r > c, "upper": r <= c}[kind]
    return m


GATE_ROWS = 4 * DN_CHUNK


def _chunk_tri(kind):
    r, c = _iota((GATE_ROWS, GATE_ROWS), 0), _iota((GATE_ROWS, GATE_ROWS), 1)
    same = (r // DN_CHUNK) == (c // DN_CHUNK)
    return jnp.where(jnp.logical_and(same, _tri(GATE_ROWS, kind)), 1.0, 0.0).astype(BF16)


def _dn_gates_fwd(hab, alog, dtb, name):
    t = hab.shape[0]
    cc = GATE_ROWS

    def body(h_ref, al_ref, dt_ref, o_ref):
        hv = h_ref[...]
        lane = _iota(hv.shape, 1)
        xa = hv + dt_ref[...]
        sp = jnp.maximum(xa, 0.0) + _log1pexp_neg_abs(xa)
        g = jnp.where(lane < HEADS, -jnp.exp(al_ref[...]) * sp, 0.0)
        gc = _dot_xl(_chunk_tri("lower"), g, NN)
        o_ref[...] = jnp.where(lane < HEADS, gc, jnp.where(lane < 2 * HEADS, _sigmoid(hv), 0.0))

    return pl.pallas_call(
        body, name=name,
        out_shape=jax.ShapeDtypeStruct((t, 128), F32),
        grid=(t // cc,),
        in_specs=[pl.BlockSpec((cc, 128), lambda i: (i, 0)), pl.BlockSpec((1, 128), lambda i: (0, 0)),
                  pl.BlockSpec((1, 128), lambda i: (0, 0))],
        out_specs=pl.BlockSpec((cc, 128), lambda i: (i, 0)),
        compiler_params=_params(("parallel",)),
    )(hab, alog, dtb)


def _dn_gates_bwd(dgates, hab, alog, dtb, name):
    t = hab.shape[0]
    cc = GATE_ROWS

    def body(d_ref, h_ref, al_ref, dt_ref, o_ref, dal_ref, ddt_ref):
        i = pl.program_id(0)
        hv = h_ref[...]
        dv = d_ref[...]
        lane = _iota(hv.shape, 1)
        dg = _dot_xl(_chunk_tri("upper"), jnp.where(lane < HEADS, dv, 0.0), NN)
        xa = hv + dt_ref[...]
        sp = jnp.maximum(xa, 0.0) + _log1pexp_neg_abs(xa)
        ea = jnp.exp(al_ref[...])
        da = jnp.where(lane < HEADS, dg * (-ea) * _sigmoid(xa), 0.0)
        be = _sigmoid(hv)
        db = dv * be * (1.0 - be)
        o_ref[...] = jnp.where(lane < HEADS, da, jnp.where(lane < 2 * HEADS, db, 0.0))

        @pl.when(i == 0)
        def _():
            dal_ref[...] = jnp.zeros_like(dal_ref)
            ddt_ref[...] = jnp.zeros_like(ddt_ref)

        dal_ref[...] += jnp.sum(jnp.where(lane < HEADS, dg * (-ea) * sp, 0.0), axis=0, keepdims=True)
        ddt_ref[...] += jnp.sum(da, axis=0, keepdims=True)

    return pl.pallas_call(
        body, name=name,
        out_shape=(jax.ShapeDtypeStruct((t, 128), F32), jax.ShapeDtypeStruct((1, 128), F32),
                   jax.ShapeDtypeStruct((1, 128), F32)),
        grid=(t // cc,),
        in_specs=[pl.BlockSpec((cc, 128), lambda i: (i, 0)), pl.BlockSpec((cc, 128), lambda i: (i, 0)),
                  pl.BlockSpec((1, 128), lambda i: (0, 0)), pl.BlockSpec((1, 128), lambda i: (0, 0))],
        out_specs=(pl.BlockSpec((cc, 128), lambda i: (i, 0)), pl.BlockSpec((1, 128), lambda i: (0, 0)),
                   pl.BlockSpec((1, 128), lambda i: (0, 0))),
        compiler_params=_params(("arbitrary",)),
    )(dgates, hab, alog, dtb)


def _dn_chunk_common(gates, h):
    cc = DN_CHUNK
    lane = _iota(gates.shape, 1)
    gh = jnp.where(lane == h, gates, 0.0)
    gc_col = jnp.sum(gh, axis=1, keepdims=True)
    gc_row = _dot_xl(jnp.ones((cc, 128), BF16), gh, NT)
    beta = jnp.sum(jnp.where(lane == h + HEADS, gates, 0.0), axis=1, keepdims=True)
    lower = _tri(cc, "lower")
    decay = jnp.where(lower, jnp.exp(jnp.where(lower, gc_col - gc_row, 0.0)), 0.0)
    gc_last = gc_col[cc - 1:cc, :]
    return gc_col, gc_last, beta, decay


def _dn_local_fwd(act, gates, name):
    t = act.shape[0]
    cc = DN_CHUNK
    nc = t // cc

    def body(q_ref, k_ref, v_ref, g_ref, u_ref, w_ref, kd_ref, qg_ref, ti_ref, p_ref):
        gates = g_ref[...]
        eye = jnp.where(_iota((cc, cc), 0) == _iota((cc, cc), 1), 1.0, 0.0)
        hs = range(HEADS)
        sl = [slice(h * HEAD_DIM, (h + 1) * HEAD_DIM) for h in hs]
        q, k, v = ([r[:, s] for s in sl] for r in (q_ref, k_ref, v_ref))
        gc_col, gc_last, beta, decay = zip(*[_dn_chunk_common(gates, h) for h in hs])
        gam = [jnp.exp(g) for g in gc_col]
        kb = [k[h] * beta[h] for h in hs]
        npow = [-jnp.where(_tri(cc, "strict"), _dotb(kb[h], k[h], NT) * decay[h], 0.0) for h in hs]
        tinv = [eye + n for n in npow]
        for _ in range(5):
            npow = [_dot3(n, n, NN) for n in npow]
            tinv = [t + _dot3(t, n, NN) for t, n in zip(tinv, npow)]
        uu = [_dot3(tinv[h], v[h] * beta[h], NN) for h in hs]
        ww = [_dot3(tinv[h], kb[h] * gam[h], NN) for h in hs]
        pp = [jnp.where(_tri(cc, "lower"), _dotb(q[h], k[h], NT) * decay[h], 0.0) for h in hs]
        for h in hs:
            u_ref[:, sl[h]] = uu[h]
            w_ref[:, sl[h]] = ww[h]
            kd_ref[:, sl[h]] = k[h] * jnp.exp(gc_last[h] - gc_col[h])
            qg_ref[:, sl[h]] = q[h] * gam[h]
            ti_ref[h] = tinv[h]
            p_ref[h] = pp[h]

    row = lambda off: pl.BlockSpec((cc, WIDTH), lambda n: (n, off))
    mat = pl.BlockSpec((HEADS, cc, cc), lambda n: (0, n, 0))
    tw = jax.ShapeDtypeStruct((t, WIDTH), F32)
    hm = jax.ShapeDtypeStruct((HEADS, t, cc), F32)
    return pl.pallas_call(
        body, name=name,
        out_shape=(tw, tw, tw, tw, hm, hm),
        grid=(nc,),
        in_specs=[row(0), row(1), row(2), pl.BlockSpec((cc, 128), lambda n: (n, 0))],
        out_specs=(row(0), row(0), row(0), row(0), mat, mat),
        compiler_params=_params(("parallel",)),
    )(act, act, act, gates)


def _dn_scan_fwd(u, w, kd, qg, p, gates, name):
    t = u.shape[0]
    cc = DN_CHUNK
    nc = t // cc

    def body(u_ref, w_ref, kd_ref, qg_ref, p_ref, g_ref, o_ref, sh_ref, s_ref):
        n = pl.program_id(0)

        @pl.when(n == 0)
        def _():
            s_ref[...] = jnp.zeros_like(s_ref)

        glast = jnp.exp(g_ref[cc - 1:cc, :])
        hs = range(HEADS)
        sl = [slice(h * HEAD_DIM, (h + 1) * HEAD_DIM) for h in hs]
        s = [s_ref[h] for h in hs]
        sb = [a.astype(BF16) for a in s]
        vn = [u_ref[:, sl[h]] - _dot(w_ref[:, sl[h]].astype(BF16), sb[h], NN) for h in hs]
        vnb = [a.astype(BF16) for a in vn]
        o_state = [_dot(qg_ref[:, sl[h]].astype(BF16), sb[h], NN) for h in hs]
        o_local = [_dot(p_ref[h].astype(BF16), vnb[h], NN) for h in hs]
        s_add = [_dot(kd_ref[:, sl[h]].astype(BF16), vnb[h], TN) for h in hs]
        for h in hs:
            o_ref[:, sl[h]] = o_state[h] + o_local[h]
            sh_ref[0, h] = s[h]
            s_ref[h] = glast[:, h:h + 1] * s[h] + s_add[h]

    row = pl.BlockSpec((cc, WIDTH), lambda n: (n, 0))
    return pl.pallas_call(
        body, name=name,
        out_shape=(jax.ShapeDtypeStruct((t, WIDTH), F32),
                   jax.ShapeDtypeStruct((nc, HEADS, HEAD_DIM, HEAD_DIM), F32)),
        grid=(nc,),
        in_specs=[row, row, row, row, pl.BlockSpec((HEADS, cc, cc), lambda n: (0, n, 0)),
                  pl.BlockSpec((cc, 128), lambda n: (n, 0))],
        out_specs=(row, pl.BlockSpec((1, HEADS, HEAD_DIM, HEAD_DIM), lambda n: (n, 0, 0, 0))),
        scratch_shapes=[pltpu.VMEM((HEADS, HEAD_DIM, HEAD_DIM), F32)],
        compiler_params=_params(("arbitrary",)),
    )(u, w, kd, qg, p, gates)


def _dn_scan_bwd(do, w, kd, qg, p, gates, name):
    t = do.shape[0]
    cc = DN_CHUNK
    nc = t // cc

    def body(do_ref, w_ref, kd_ref, qg_ref, p_ref, g_ref, dvn_ref, dsh_ref, ds_ref):
        n = pl.program_id(0)

        @pl.when(n == 0)
        def _():
            ds_ref[...] = jnp.zeros_like(ds_ref)

        glast = jnp.exp(g_ref[cc - 1:cc, :])
        hs = range(HEADS)
        sl = [slice(h * HEAD_DIM, (h + 1) * HEAD_DIM) for h in hs]
        ds = [ds_ref[h] for h in hs]
        dob = [do_ref[:, sl[h]].astype(BF16) for h in hs]
        dvn = [_dot(p_ref[h].astype(BF16), dob[h], TN) + _dot(kd_ref[:, sl[h]].astype(BF16), ds[h].astype(BF16), NN)
               for h in hs]
        ds_q = [_dot(qg_ref[:, sl[h]].astype(BF16), dob[h], TN) for h in hs]
        ds_w = [_dot(w_ref[:, sl[h]].astype(BF16), dvn[h].astype(BF16), TN) for h in hs]
        for h in hs:
            dvn_ref[:, sl[h]] = dvn[h]
            dsh_ref[0, h] = ds[h]
            ds_ref[h] = ds_q[h] + glast[:, h:h + 1] * ds[h] - ds_w[h]

    row = pl.BlockSpec((cc, WIDTH), lambda n: (nc - 1 - n, 0))
    return pl.pallas_call(
        body, name=name,
        out_shape=(jax.ShapeDtypeStruct((t, WIDTH), F32),
                   jax.ShapeDtypeStruct((nc, HEADS, HEAD_DIM, HEAD_DIM), F32)),
        grid=(nc,),
        in_specs=[row, row, row, row, pl.BlockSpec((HEADS, cc, cc), lambda n: (0, nc - 1 - n, 0)),
                  pl.BlockSpec((cc, 128), lambda n: (nc - 1 - n, 0))],
        out_specs=(row, pl.BlockSpec((1, HEADS, HEAD_DIM, HEAD_DIM), lambda n: (nc - 1 - n, 0, 0, 0))),
        scratch_shapes=[pltpu.VMEM((HEADS, HEAD_DIM, HEAD_DIM), F32)],
        compiler_params=_params(("arbitrary",)),
    )(do, w, kd, qg, p, gates)


def _dn_local_bwd(act, gates, u, w, kd, qg, tinv, p, sh, dsh, dvn, do, name):
    t = act.shape[0]
    cc = DN_CHUNK
    nc = t // cc

    def body(q_ref, k_ref, v_ref, g_ref, u_ref, w_ref, kd_ref, qg_ref, ti_ref, p_ref, s_ref, ds_ref,
             dvn_ref, do_ref, dq_ref, dk_ref, dv_ref, dg_ref):
        gates_v = g_ref[...]
        lower, strict = _tri(cc, "lower"), _tri(cc, "strict")
        ones = jnp.ones((cc, 128), BF16)
        rowc = _iota((cc, 1), 0)
        lane = _iota((cc, 128), 1)
        hs = range(HEADS)
        sl = [slice(h * HEAD_DIM, (h + 1) * HEAD_DIM) for h in hs]
        q, k, v, uu, ww, kd, qg, dvn, do = ([r[:, s] for s in sl] for r in (
            q_ref, k_ref, v_ref, u_ref, w_ref, kd_ref, qg_ref, dvn_ref, do_ref))
        gc_col, gc_last, beta, decay = zip(*[_dn_chunk_common(gates_v, h) for h in hs])
        gam = [jnp.exp(g) for g in gc_col]
        kb = [k[h] * beta[h] for h in hs]
        s_in = [s_ref[0, h] for h in hs]
        ds_out = [ds_ref[0, h] for h in hs]
        tinv = [ti_ref[h] for h in hs]

        a = [jnp.where(strict, _dotb(kb[h], k[h], NT) * decay[h], 0.0) for h in hs]
        vn = [uu[h] - _dotb(ww[h], s_in[h], NN) for h in hs]
        dqg = [_dotb(do[h], s_in[h], NT) for h in hs]
        dw = [-_dotb(dvn[h], s_in[h], NT) for h in hs]
        dp = [jnp.where(lower, _dotb(do[h], vn[h], NT), 0.0) for h in hs]
        dkd = [_dotb(vn[h], ds_out[h], NT) for h in hs]
        dru = [_dot3(tinv[h], dvn[h], TN) for h in hs]
        drw = [_dot3(tinv[h], dw[h], TN) for h in hs]
        da = [-jnp.where(strict, _dotb(dru[h], uu[h], NT) + _dotb(drw[h], ww[h], NT), 0.0) for h in hs]
        dad = [da[h] * decay[h] for h in hs]
        dpd = [dp[h] * decay[h] for h in hs]
        dkb = [_dotb(dad[h], k[h], NN) + gam[h] * drw[h] for h in hs]
        dk = [_dotb(dad[h], kb[h], TN) + _dotb(dpd[h], q[h], TN) + beta[h] * dkb[h]
              + jnp.exp(gc_last[h] - gc_col[h]) * dkd[h] for h in hs]
        dq = [gam[h] * dqg[h] + _dotb(dpd[h], k[h], NN) for h in hs]
        gm = [da[h] * a[h] + dp[h] * p_ref[h] for h in hs]
        colsum = [_dot_xr(gm[h], ones, TN)[:, 0:1] for h in hs]

        dgates = jnp.zeros((cc, 128), F32)
        for h in hs:
            dk_ref[:, sl[h]] = dk[h]
            dq_ref[:, sl[h]] = dq[h]
            dv_ref[:, sl[h]] = beta[h] * dru[h]
            dbeta = (jnp.sum(dkb[h] * k[h], axis=1, keepdims=True)
                     + jnp.sum(dru[h] * v[h], axis=1, keepdims=True))
            rkd = jnp.sum(dkd[h] * kd[h], axis=1, keepdims=True)
            dgc = (jnp.sum(gm[h], axis=1, keepdims=True) - colsum[h]
                   + jnp.sum(dqg[h] * qg[h], axis=1, keepdims=True)
                   + jnp.sum(drw[h] * kb[h], axis=1, keepdims=True) * gam[h] - rkd)
            tail = jnp.sum(rkd, axis=0, keepdims=True) + jnp.exp(gc_last[h]) * jnp.sum(
                jnp.sum(s_in[h] * ds_out[h], axis=1, keepdims=True), axis=0, keepdims=True)
            dgc = dgc + jnp.where(rowc == cc - 1, tail, 0.0)
            dgates = dgates + jnp.where(lane == h, dgc, 0.0) + jnp.where(lane == h + HEADS, dbeta, 0.0)
        dg_ref[...] = dgates

    row = lambda off: pl.BlockSpec((cc, WIDTH), lambda n: (n, off))
    mat = pl.BlockSpec((HEADS, cc, cc), lambda n: (0, n, 0))
    st = pl.BlockSpec((1, HEADS, HEAD_DIM, HEAD_DIM), lambda n: (n, 0, 0, 0))
    gl = pl.BlockSpec((cc, 128), lambda n: (n, 0))
    tw = jax.ShapeDtypeStruct((t, WIDTH), F32)
    return pl.pallas_call(
        body, name=name,
        out_shape=(tw, tw, tw, jax.ShapeDtypeStruct((t, 128), F32)),
        grid=(nc,),
        in_specs=[row(0), row(1), row(2), gl, row(0), row(0), row(0), row(0), mat, mat, st, st, row(0), row(0)],
        out_specs=(row(0), row(0), row(0), gl),
        compiler_params=_params(("parallel",)),
    )(act, act, act, gates, u, w, kd, qg, tinv, p, sh, dsh, dvn, do)


def _dn_post_fwd(o, gate, w, name):
    t = o.shape[0]
    tr = _tile(t, 512)

    def body(o_ref, g_ref, w_ref, y_ref):
        for h in range(HEADS):
            sl = slice(h * HEAD_DIM, (h + 1) * HEAD_DIM)
            ov, gv = o_ref[:, sl], g_ref[:, sl].astype(F32)
            r = lax.rsqrt(jnp.mean(ov * ov, axis=1, keepdims=True) + EPS)
            y_ref[:, sl] = (ov * r * w_ref[...] * (gv * _sigmoid(gv))).astype(BF16)

    blk = pl.BlockSpec((tr, WIDTH), lambda i: (i, 0))
    return pl.pallas_call(
        body, name=name,
        out_shape=jax.ShapeDtypeStruct((t, WIDTH), BF16),
        grid=(t // tr,),
        in_specs=[blk, blk, pl.BlockSpec((1, HEAD_DIM), lambda i: (0, 0))],
        out_specs=blk,
        compiler_params=_params(("parallel",)),
    )(o, gate, w)


def _dn_post_bwd(dy, o, gate, w, name):
    t = o.shape[0]
    tr = _tile(t, 512)

    def body(dy_ref, o_ref, g_ref, w_ref, do_ref, dg_ref, dw_ref):
        i = pl.program_id(0)

        @pl.when(i == 0)
        def _():
            dw_ref[...] = jnp.zeros_like(dw_ref)

        dw = jnp.zeros((1, HEAD_DIM), F32)
        for h in range(HEADS):
            sl = slice(h * HEAD_DIM, (h + 1) * HEAD_DIM)
            ov, gv, dyv = o_ref[:, sl], g_ref[:, sl].astype(F32), dy_ref[:, sl].astype(F32)
            r = lax.rsqrt(jnp.mean(ov * ov, axis=1, keepdims=True) + EPS)
            oh = ov * r
            sg = _sigmoid(gv)
            dg_ref[:, sl] = (dyv * oh * w_ref[...] * (sg * (1.0 + gv * (1.0 - sg)))).astype(BF16)
            dn = dyv * (gv * sg)
            doh = dn * w_ref[...]
            do_ref[:, sl] = r * (doh - oh * jnp.mean(doh * oh, axis=1, keepdims=True))
            dw = dw + jnp.sum(dn * oh, axis=0, keepdims=True)
        dw_ref[...] += dw

    blk = pl.BlockSpec((tr, WIDTH), lambda i: (i, 0))
    return pl.pallas_call(
        body, name=name,
        out_shape=(jax.ShapeDtypeStruct((t, WIDTH), F32), jax.ShapeDtypeStruct((t, WIDTH), BF16),
                   jax.ShapeDtypeStruct((1, HEAD_DIM), F32)),
        grid=(t // tr,),
        in_specs=[blk, blk, blk, pl.BlockSpec((1, HEAD_DIM), lambda i: (0, 0))],
        out_specs=(blk, blk, pl.BlockSpec((1, HEAD_DIM), lambda i: (0, 0))),
        compiler_params=_params(("arbitrary",)),
    )(dy, o, gate, w)


def _sb_scores(qs, k_ref, qi, it, carries, uincl):
    bk = ATT_BLOCK
    scale = HEAD_DIM ** -0.5
    heads, groups = range(len(qs)), range(SB_GROUP)
    lane = [slice(e * HEAD_DIM, (e + 1) * HEAD_DIM) for e in heads]
    js = [qi - SB_GROUP * it - g for g in groups]
    rows = [pl.ds(pl.multiple_of(jnp.maximum(j, 0) * bk, bk), bk) for j in js]
    qpos = qi * bk + _iota((bk, bk), 0)
    col = _iota((bk, bk), 1)
    mask1 = [jnp.logical_and(j * bk + col < qpos, j >= 0) for j in js]
    ks = [[k_ref[r, lane[e]] for r in rows] for e in heads]
    z = [[_dot(qs[e], k, NT) * scale for k in ks[e]] for e in heads]
    soft = [[_log1pexp_neg_abs(a) for a in ze] for ze in z]
    lk_full = [[-(jnp.maximum(a, 0.0) + s) for a, s in zip(z[e], soft[e])] for e in heads]
    lk = [[jnp.where(m, a, 0.0) for m, a in zip(mask1, lk_full[e])] for e in heads]
    ls = [[jnp.minimum(a, 0.0) - s for a, s in zip(z[e], soft[e])] for e in heads]
    incl = [[_dot_xr2(a, uincl, NN) for a in lk[e]] for e in heads]
    weights, out_carries = [], []
    for e in heads:
        cb, we = carries[e], []
        for g in groups:
            we.append(jnp.where(mask1[g], jnp.exp(ls[e][g] + (cb + incl[e][g] - lk[e][g])), 0.0))
            cb = cb + incl[e][g][:, 0:1]
        weights.append(we)
        out_carries.append(cb)
    return rows, ks, weights, mask1, lk_full, ls, out_carries


def _sb_more(qi, carry):
    it, cbs = carry[0], carry[1]
    live = jnp.max(cbs[0])
    for cb in cbs[1:]:
        live = jnp.maximum(live, jnp.max(cb))
    return jnp.logical_and(SB_GROUP * it <= qi, live > SB_LOG_ZERO)


def _sb_steps(groups, nq):
    def when():
        h, i = pl.program_id(0), pl.program_id(1)
        return (jnp.logical_and(h == 0, i == 0), jnp.logical_and(h == groups // 2, i == 0),
                jnp.logical_and(h == groups - 1, i == nq - 1))
    return when


def _sb_fwd(qkv, name, comm=None):
    t = qkv.shape[0]
    bk = ATT_BLOCK
    hp, wide = SB_HEADS_FWD, SB_HEADS_FWD * HEAD_DIM
    lane = [slice(e * HEAD_DIM, (e + 1) * HEAD_DIM) for e in range(hp)]

    def body(q_ref, k_ref, v_ref, o_ref):
        qi = pl.program_id(1)
        qs = [q_ref[:, s] for s in lane]
        uincl = jnp.where(_tri(bk, "lower"), 1.0, 0.0).astype(BF16)

        def step(carry):
            it, cbs, accs = carry
            rows, _, weights, _, _, _, cbs = _sb_scores(qs, k_ref, qi, it, cbs, uincl)
            accs = list(accs)
            for e in range(hp):
                for r, a in zip(rows, weights[e]):
                    accs[e] = accs[e] + _dot(a.astype(BF16), v_ref[r, lane[e]], NN)
            return it + 1, tuple(cbs), tuple(accs)

        init = (jnp.int32(0), (jnp.zeros((bk, 1), F32),) * hp, (jnp.zeros((bk, HEAD_DIM), F32),) * hp)
        _, _, accs = lax.while_loop(functools.partial(_sb_more, qi), step, init)
        for e in range(hp):
            o_ref[:, lane[e]] = accs[e]

    groups = HEADS // hp
    (o,), extra = _host_call(
        body, name, comm, _sb_steps(groups, t // bk), [jax.ShapeDtypeStruct((t, WIDTH), F32)], (groups, t // bk),
        [pl.BlockSpec((bk, wide), lambda h, i: (i, h)),
         pl.BlockSpec((t, wide), lambda h, i: (0, groups + h)),
         pl.BlockSpec((t, wide), lambda h, i: (0, 2 * groups + h))],
        [pl.BlockSpec((bk, wide), lambda h, i: (i, h))], [], ("parallel", "arbitrary"), (qkv, qkv, qkv))
    return o, extra


def _sb_bwd(qkv, o, do, name, comm=None):
    assert do.dtype == BF16
    t = qkv.shape[0]
    bk = ATT_BLOCK
    scale = HEAD_DIM ** -0.5
    hp, wide = SB_HEADS_BWD, SB_HEADS_BWD * HEAD_DIM
    lane = [slice(e * HEAD_DIM, (e + 1) * HEAD_DIM) for e in range(hp)]

    def body(q_ref, k_ref, v_ref, o_ref, do_ref, dq_ref, dk_ref, dv_ref):
        qi = pl.program_id(1)

        @pl.when(qi == 0)
        def _():
            dk_ref[...] = jnp.zeros_like(dk_ref)
            dv_ref[...] = jnp.zeros_like(dv_ref)

        heads, groups = range(hp), range(SB_GROUP)
        qs = [q_ref[:, s] for s in lane]
        dob = [do_ref[:, s] for s in lane]
        dsum = [jnp.sum(dob[e].astype(F32) * o_ref[:, lane[e]], axis=1, keepdims=True) for e in heads]
        uincl = jnp.where(_tri(bk, "lower"), 1.0, 0.0).astype(BF16)

        def step(carry):
            it, cbs, ces, dqs = carry
            rows, ks, weights, mask, lk_full, ls, cbs = _sb_scores(qs, k_ref, qi, it, cbs, uincl)
            ab = [[a.astype(BF16) for a in weights[e]] for e in heads]
            vs = [[v_ref[r, lane[e]] for r in rows] for e in heads]
            dla = [[ab[e][g].astype(F32) * _dot(dob[e], vs[e][g], NT) for g in groups] for e in heads]
            suf = [[_dot_xr2(a, uincl, NN) for a in dla[e]] for e in heads]
            ces, dqs = list(ces), list(dqs)
            for e in heads:
                for g in groups:
                    err = dsum[e] - (ces[e] + suf[e][g])
                    ces[e] = ces[e] + suf[e][g][:, 0:1]
                    dz = jnp.where(mask[g], dla[e][g] * jnp.exp(lk_full[e][g]) - err * jnp.exp(ls[e][g]), 0.0)
                    dzb = (dz * scale).astype(BF16)
                    dqs[e] = dqs[e] + _dot(dzb, ks[e][g], NN)
                    dk_ref[rows[g], lane[e]] += _dot(dzb, qs[e], TN)
                    dv_ref[rows[g], lane[e]] += _dot(ab[e][g], dob[e], TN)
            return it + 1, tuple(cbs), tuple(ces), tuple(dqs)

        zc = (jnp.zeros((bk, 1), F32),) * hp
        init = (jnp.int32(0), zc, zc, (jnp.zeros((bk, HEAD_DIM), F32),) * hp)
        dqs = lax.while_loop(functools.partial(_sb_more, qi), step, init)[3]
        for e in heads:
            dq_ref[:, lane[e]] = dqs[e].astype(BF16)

    ngroup = HEADS // hp
    tw = jax.ShapeDtypeStruct((t, WIDTH), F32)
    qb = pl.BlockSpec((bk, wide), lambda h, i: (i, h))
    full = lambda off: pl.BlockSpec((t, wide), lambda h, i: (0, off + h))
    return _host_call(
        body, name, comm, _sb_steps(ngroup, t // bk), [jax.ShapeDtypeStruct((t, WIDTH), BF16), tw, tw],
        (ngroup, t // bk),
        [qb, full(ngroup), full(2 * ngroup), qb, qb], [qb, full(0), full(0)], [], ("parallel", "arbitrary"),
        (qkv, qkv, qkv, o, do))


def _merge_fwd(pd, ps, gl, name):
    t = pd.shape[0]
    tr, tc = _tile(t, 512), 512
    nj = D_MODEL // tc

    def body(pd_ref, ps_ref, gd_ref, gs_ref, o_ref):
        gd, gs = gd_ref[...].astype(F32), gs_ref[...].astype(F32)
        o_ref[...] = (_sigmoid(gd) * pd_ref[...].astype(F32) + _sigmoid(gs) * ps_ref[...].astype(F32)).astype(BF16)

    blk = lambda off: pl.BlockSpec((tr, tc), lambda i, j: (i, j + off))
    return pl.pallas_call(
        body, name=name,
        out_shape=jax.ShapeDtypeStruct((t, D_MODEL), BF16),
        grid=(t // tr, nj),
        in_specs=[blk(0), blk(0), blk(0), blk(nj)],
        out_specs=blk(0),
        compiler_params=_params(("parallel", "parallel")),
    )(pd, ps, gl, gl)


def _merge_bwd(dm, pd, ps, gl, name):
    t = pd.shape[0]
    tr, tc = _tile(t, 512), 512
    nj = D_MODEL // tc

    def body(dm_ref, pd_ref, ps_ref, gd_ref, gs_ref, dpd_ref, dps_ref, dgd_ref, dgs_ref):
        dmv = dm_ref[...].astype(F32)
        sd, ss = _sigmoid(gd_ref[...].astype(F32)), _sigmoid(gs_ref[...].astype(F32))
        dpd_ref[...] = (dmv * sd).astype(BF16)
        dps_ref[...] = (dmv * ss).astype(BF16)
        dgd_ref[...] = (dmv * pd_ref[...].astype(F32) * sd * (1.0 - sd)).astype(BF16)
        dgs_ref[...] = (dmv * ps_ref[...].astype(F32) * ss * (1.0 - ss)).astype(BF16)

    blk = lambda off: pl.BlockSpec((tr, tc), lambda i, j: (i, j + off))
    out = jax.ShapeDtypeStruct((t, D_MODEL), BF16)
    return pl.pallas_call(
        body, name=name,
        out_shape=(out, out, out, out),
        grid=(t // tr, nj),
        in_specs=[blk(0), blk(0), blk(0), blk(0), blk(nj)],
        out_specs=(blk(0), blk(0), blk(0), blk(0)),
        compiler_params=_params(("parallel", "parallel")),
    )(dm, pd, ps, gl, gl)


def _local_step(x, target, wts, plan=None):
    n1 = _rmsnorm_fwd(x, wts["norm1_w"], "norm1_fwd")
    qkv_pre = _matmul(n1, wts["w_dnqkv_t"], "nt", BF16, "in_dnqkv")
    hgate = _matmul(n1, wts["w_dngate_t"], "nt", BF16, "in_dngate")
    sbqkv = _matmul(n1, wts["w_sbqkv_t"], "nt", BF16, "in_sbqkv")
    gl = _matmul(n1, wts["w_gl_t"], "nt", BF16, "in_gl")
    hab = _matmul(n1, wts["w_ab_t"], "nt", F32, "in_ab")

    act = _dn_pre_fwd(qkv_pre, wts["dn_conv_w"], "dn_pre_fwd")
    gates = _dn_gates_fwd(hab, wts["alog"], wts["dtb"], "dn_gates_fwd")
    u, w, kd, qg, tinv, p = _dn_local_fwd(act, gates, "dn_local_fwd")
    o_dn, sh = _dn_scan_fwd(u, w, kd, qg, p, gates, "dn_scan_fwd")
    y_dn = _dn_post_fwd(o_dn, hgate, wts["dn_norm_w"], "dn_post_fwd")

    o_sb, late = _sb_fwd(sbqkv, "sb_fwd", comm=plan.late_gather() if plan else None)
    if plan:
        wts = {**wts, **plan.late_weights(late)}

    pd = _matmul(y_dn, wts["w_proj_dn"], "nn", BF16, "proj_dn")
    ps = _matmul(o_sb, wts["w_proj_sb"], "nn", BF16, "proj_sb")
    mixed = _merge_fwd(pd, ps, gl, "merge_fwd")
    x1 = _matmul(mixed, wts["w_out"], "nn", F32, "out_proj", add=x)

    n2 = _rmsnorm_fwd(x1, wts["norm2_w"], "norm2_fwd")
    upre = _matmul(n2, wts["ffn_w_up_t"], "nt", BF16, "ffn_up")
    fact = _ffn_act_fwd(upre, wts["ffn_conv_w"], "ffn_act_fwd")
    x2 = _matmul(fact, wts["ffn_w_down"], "nn", F32, "ffn_down", add=x1)

    dx2, g_normf, loss = _final_loss(x2, target, wts["norm_f_w"], "final_loss")

    dfact = _matmul(dx2, wts["ffn_w_down"], "nt", BF16, "ffn_down_dx")
    g_wdown = _matmul(fact, dx2, "tn", BF16, "ffn_down_dw")
    dgc, duc, dwg, dwu = _ffn_act_bwd(dfact, upre, wts["ffn_conv_w"], "ffn_act_bwd")
    g_fconv = jnp.concatenate([dwg, dwu], axis=1)
    dupre = _conv_bwd_data([dgc, duc], wts["ffn_conv_w"], FFN_CONV, BF16, "ffn_conv_bwd")
    dn2 = _matmul(dupre, wts["ffn_w_up_t"], "nn", F32, "ffn_up_dx")
    g_wup = _matmul(dupre, n2, "tn", BF16, "ffn_up_dw")
    dx1, g_norm2 = _rmsnorm_bwd(dn2, x1, wts["norm2_w"], dx2, "norm2_bwd")

    dmixed = _matmul(dx1, wts["w_out"], "nt", BF16, "out_proj_dx")
    g_wout = _matmul(mixed, dx1, "tn", BF16, "out_proj_dw")
    dpd, dps, dgd, dgs = _merge_bwd(dmixed, pd, ps, gl, "merge_bwd")
    dy_dn = _matmul(dpd, wts["w_proj_dn"], "nt", BF16, "proj_dn_dx")
    g_wpd = _matmul(y_dn, dpd, "tn", BF16, "proj_dn_dw")
    do_sb = _matmul(dps, wts["w_proj_sb"], "nt", BF16, "proj_sb_dx")
    g_wps = _matmul(o_sb, dps, "tn", BF16, "proj_sb_dw")
    grads = dict(w_proj_dn=g_wpd, w_proj_sb=g_wps, w_out=g_wout, ffn_w_up_t=g_wup, ffn_w_down=g_wdown)

    (dsq, dsk, dsv), got_early = _sb_bwd(sbqkv, o_sb, do_sb, "sb_bwd",
                                         comm=plan.early_grads(grads) if plan else None)

    do_dn, dhgate, g_dnnorm = _dn_post_bwd(dy_dn, o_dn, hgate, wts["dn_norm_w"], "dn_post_bwd")
    dvn, dsh = _dn_scan_bwd(do_dn, w, kd, qg, p, gates, "dn_scan_bwd")
    dq, dk, dv, dgates = _dn_local_bwd(act, gates, u, w, kd, qg, tinv, p, sh, dsh, dvn, do_dn, "dn_local_bwd")
    dhab, g_alog, g_dtb = _dn_gates_bwd(dgates, hab, wts["alog"], wts["dtb"], "dn_gates_bwd")
    dcv, g_dnconv = _dn_pre_bwd(dq, dk, dv, qkv_pre, wts["dn_conv_w"], "dn_pre_bwd")
    dqkv_pre = _conv_bwd_data([dcv], wts["dn_conv_w"], DN_CONV, BF16, "dn_conv_bwd")

    dh_parts = [dqkv_pre, dhgate, dsq, dsk, dsv, dgd, dgs]
    w_parts = [(wts["w_dnqkv_t"], 1), (wts["w_dngate_t"], 1), (wts["w_sbqkv_t"], 3), (wts["w_gl_t"], 2)]
    g_wmain = _matmul_mparts(dh_parts, n1, BF16, "in_dw_main")
    g_wab = _matmul(dhab, n1, "tn", BF16, "in_dw_ab")
    grads.update(w_main_t=g_wmain, w_ab_t=g_wab, dn_conv_w=g_dnconv, alog=g_alog, dtb=g_dtb, dn_norm_w=g_dnnorm,
                 norm2_w=g_norm2, ffn_conv_w=g_fconv, norm_f_w=g_normf)
    got_late = []
    if plan:
        dn1, swapped = _matmul(dhab, wts["w_ab_t"], "nn", F32, "in_dx_ab", comm=plan.sibling_swap(grads))
        dn1, got_late = _matmul_kparts(dh_parts, w_parts, F32, "in_dx_main", add=dn1,
                                       comm=plan.late_grads(swapped, grads, loss))
    else:
        dn1 = _matmul(dhab, wts["w_ab_t"], "nn", F32, "in_dx_ab")
        dn1 = _matmul_kparts(dh_parts, w_parts, F32, "in_dx_main", add=dn1)
    grad_x, g_norm1 = _rmsnorm_bwd(dn1, x, wts["norm1_w"], dx1, "norm1_bwd")
    grads["norm1_w"] = g_norm1
    return loss, grad_x, grads, got_early, got_late


HBM_SPEC = pl.BlockSpec(memory_space=pltpu.HBM)


def _mesh_pos():
    x, y, c = lax.axis_index("x"), lax.axis_index("y"), lax.axis_index("c")
    return x, y, c, 4 * x + 2 * y + c


def _peer(k):
    x, y, c, _ = _mesh_pos()
    px = 1 - x if k & 4 else x
    py = 1 - y if k & 2 else y
    pc = 1 - c if k & 1 else c
    return (px, py, pc), 4 * px + 2 * py + pc


def _rcopy(src, dst, send, recv, a, s, peer):
    return pltpu.make_async_remote_copy(src_ref=src, dst_ref=dst, send_sem=send.at[a, s], recv_sem=recv.at[a, s],
                                        device_id=peer, device_id_type=pl.DeviceIdType.MESH)


class _Gather:
    ICI = (2, 4, 6)

    def __init__(self, shards):
        self.args = list(shards)
        self.n = len(shards)
        self.out_shape = [jax.ShapeDtypeStruct((N_DEV,) + s.shape, s.dtype) for s in shards]
        self.scratch = [pltpu.SemaphoreType.DMA((self.n, N_DEV - 1)), pltpu.SemaphoreType.DMA((self.n, N_DEV - 1)),
                        pltpu.SemaphoreType.DMA((self.n,))]

    def _first(self, ins, outs, send, recv, a):
        me = _mesh_pos()[3]
        out, got = [], []
        for s, k in enumerate((1,) + self.ICI):
            peer, pidx = _peer(k)
            out.append(_rcopy(ins[a], outs[a].at[me], send, recv, a, s, peer))
            got.append(_rcopy(ins[a], outs[a].at[pidx], send, recv, a, s, peer))
        return out, got

    def _forward(self, ins, outs, send, recv, a):
        sib = _peer(1)[0]
        out, got = [], []
        for s, k in enumerate(self.ICI):
            held = outs[a].at[_peer(k)[1]]
            out.append(_rcopy(held, held, send, recv, a, 4 + s, sib))
            other = outs[a].at[_peer(k | 1)[1]]
            got.append(_rcopy(other, other, send, recv, a, 4 + s, sib))
        return out, got

    def start(self, ins, outs, sems):
        send, recv, loc = sems
        me = _mesh_pos()[3]
        for a in range(self.n):
            pltpu.make_async_copy(ins[a], outs[a].at[me], loc.at[a]).start()
            for cp in self._first(ins, outs, send, recv, a)[0]:
                cp.start()

    def mid(self, ins, outs, sems):
        send, recv, _ = sems
        for a in range(self.n):
            arrivals = self._first(ins, outs, send, recv, a)[1]
            for s, cp in enumerate(self._forward(ins, outs, send, recv, a)[0]):
                arrivals[1 + s].wait_recv()
                cp.start()

    def finish(self, ins, outs, sems):
        send, recv, loc = sems
        me = _mesh_pos()[3]
        for a in range(self.n):
            first_out, first_got = self._first(ins, outs, send, recv, a)
            fwd_out, fwd_got = self._forward(ins, outs, send, recv, a)
            first_got[0].wait_recv()
            for cp in fwd_got:
                cp.wait_recv()
            for cp in first_out + fwd_out:
                cp.wait_send()
            pltpu.make_async_copy(ins[a], outs[a].at[me], loc.at[a]).wait()


class _Exchange:
    def __init__(self, slabs=(), gathered=(), chip_slabs=(), sibling_slabs=()):
        self.args = list(slabs) + list(chip_slabs) + list(sibling_slabs) + list(gathered)
        self.kind = (["dev"] * len(slabs) + ["chip"] * len(chip_slabs) + ["sib"] * len(sibling_slabs)
                     + ["all"] * len(gathered))
        self.n = len(self.args)
        half = lambda s: jax.ShapeDtypeStruct((N_DEV // 2,) + s.shape[1:], s.dtype)
        self.out_shape = ([jax.ShapeDtypeStruct(s.shape, s.dtype) for s in slabs]
                          + [half(s) for s in chip_slabs] + [half(s) for s in sibling_slabs]
                          + [jax.ShapeDtypeStruct((N_DEV,) + s.shape, s.dtype) for s in gathered])
        self.scratch = [pltpu.SemaphoreType.DMA((self.n, N_DEV - 1)), pltpu.SemaphoreType.DMA((self.n, N_DEV - 1)),
                        pltpu.SemaphoreType.DMA((self.n,))]

    def _copies(self, ins, outs, send, recv, a):
        x, y, c, me = _mesh_pos()
        kind = self.kind[a]
        out, got = [], []
        if kind == "sib":
            sib = _peer(1)[0]
            for q in range(N_DEV // 2):
                out.append(_rcopy(ins[a].at[2 * q + 1 - c], outs[a].at[q], send, recv, a, q, sib))
                got.append(_rcopy(ins[a].at[2 * q + c], outs[a].at[q], send, recv, a, q, sib))
            return out, got
        for k in ((2, 4, 6) if kind == "chip" else range(1, N_DEV)):
            peer, pidx = _peer(k)
            if kind == "chip":
                src, mine, theirs = ins[a].at[2 * peer[0] + peer[1]], 2 * x + y, 2 * peer[0] + peer[1]
            else:
                src, mine, theirs = (ins[a].at[pidx] if kind == "dev" else ins[a]), me, pidx
            out.append(_rcopy(src, outs[a].at[mine], send, recv, a, k - 1, peer))
            got.append(_rcopy(src, outs[a].at[theirs], send, recv, a, k - 1, peer))
        return out, got

    def _local(self, ins, outs, loc, a):
        x, y, _, me = _mesh_pos()
        kind = self.kind[a]
        if kind == "sib":
            return None
        if kind == "chip":
            return pltpu.make_async_copy(ins[a].at[2 * x + y], outs[a].at[2 * x + y], loc.at[a])
        return pltpu.make_async_copy(ins[a].at[me] if kind == "dev" else ins[a], outs[a].at[me], loc.at[a])

    def start(self, ins, outs, sems):
        send, recv, loc = sems
        for a in range(self.n):
            if self._local(ins, outs, loc, a) is not None:
                self._local(ins, outs, loc, a).start()
            for cp in self._copies(ins, outs, send, recv, a)[0]:
                cp.start()

    def mid(self, ins, outs, sems):
        pass

    def finish(self, ins, outs, sems):
        send, recv, loc = sems
        for a in range(self.n):
            out, got = self._copies(ins, outs, send, recv, a)
            for cp in got:
                cp.wait_recv()
            for cp in out:
                cp.wait_send()
            if self._local(ins, outs, loc, a) is not None:
                self._local(ins, outs, loc, a).wait()


def _comm_call(comm, name):
    n = comm.n

    def body(*refs):
        ins, outs, sems = refs[:n], refs[n:2 * n], refs[2 * n:]
        comm.start(ins, outs, sems)
        comm.mid(ins, outs, sems)
        comm.finish(ins, outs, sems)

    return pl.pallas_call(
        body, name=name, out_shape=comm.out_shape, in_specs=[HBM_SPEC] * n, out_specs=[HBM_SPEC] * n,
        scratch_shapes=comm.scratch,
    )(*comm.args)


def _hosted(body, comm, n_in, n_out, when):
    if comm is None:
        return body

    def wrapped(*refs):
        ins, c_ins = refs[:n_in], refs[n_in:n_in + comm.n]
        o0 = n_in + comm.n
        outs, c_outs = refs[o0:o0 + n_out], refs[o0 + n_out:o0 + n_out + comm.n]
        scratch, sems = refs[o0 + n_out + comm.n:len(refs) - 3], refs[len(refs) - 3:]
        first, middle, last = when()

        @pl.when(first)
        def _():
            comm.start(c_ins, c_outs, sems)

        body(*ins, *outs, *scratch)

        @pl.when(middle)
        def _():
            comm.mid(c_ins, c_outs, sems)

        @pl.when(last)
        def _():
            comm.finish(c_ins, c_outs, sems)

    return wrapped


def _host_call(body, name, comm, when, out_shape, grid, in_specs, out_specs, scratch_shapes, sem, args):
    n_in, n_out = len(in_specs), len(out_specs)
    if comm is None:
        res = pl.pallas_call(body, name=name, out_shape=out_shape, grid=grid, in_specs=in_specs, out_specs=out_specs,
                             scratch_shapes=scratch_shapes, compiler_params=_params(sem))(*args)
        return list(res), []
    res = pl.pallas_call(
        _hosted(body, comm, n_in, n_out, when), name=name,
        out_shape=list(out_shape) + comm.out_shape, grid=grid,
        in_specs=list(in_specs) + [HBM_SPEC] * comm.n, out_specs=list(out_specs) + [HBM_SPEC] * comm.n,
        scratch_shapes=list(scratch_shapes) + comm.scratch,
        compiler_params=_params(("arbitrary",) * len(grid)),
    )(*args, *comm.args)
    return list(res[:n_out]), list(res[n_out:])


def _add_my_slabs(slabs, b, name):
    n, rows, cols = b.shape
    tc = _tile(cols, 256)

    def body(a_ref, b_ref, o_ref):
        o_ref[...] = (a_ref[...].astype(F32) + b_ref[...].astype(F32)).astype(o_ref.dtype)

    blk = pl.BlockSpec((None, rows, tc), lambda i, j: (i, 0, j))
    mine = pl.BlockSpec((None, rows, tc), lambda i, j: (2 * i + lax.axis_index("c"), 0, j))
    return pl.pallas_call(
        body, name=name, out_shape=jax.ShapeDtypeStruct(b.shape, b.dtype), grid=(n, cols // tc),
        in_specs=[mine, blk], out_specs=blk, compiler_params=_params(("parallel", "parallel")),
    )(slabs, b)


def _adamw(parts, w, m, v, name):
    rows, cols = w.shape
    nparts = parts.shape[0]
    tr, tc = rows, cols
    for cand in (128, 176):
        if rows > cand and rows % cand == 0:
            tr = cand
            break
    if tr == rows and rows > 512:
        tc = _tile(cols, 256)

    def body(p_ref, w_ref, m_ref, v_ref, g_ref, d_ref, mo_ref, vo_ref):
        g = p_ref[0].astype(F32)
        for s in range(1, nparts):
            g = g + p_ref[s].astype(F32)
        mn = ADAM_B1 * m_ref[...] + (1.0 - ADAM_B1) * g
        vn = ADAM_B2 * v_ref[...] + (1.0 - ADAM_B2) * (g * g)
        m_hat = mn / (1.0 - ADAM_B1 ** ADAM_STEP)
        v_hat = vn / (1.0 - ADAM_B2 ** ADAM_STEP)
        g_ref[...] = g
        d_ref[...] = -ADAM_LR * (m_hat / (jnp.sqrt(v_hat) + ADAM_EPS) + ADAM_WD * w_ref[...])
        mo_ref[...] = mn
        vo_ref[...] = vn

    blk = pl.BlockSpec((tr, tc), lambda i, j: (i, j))
    out = jax.ShapeDtypeStruct((rows, cols), F32)
    return pl.pallas_call(
        body, name=name,
        out_shape=(out, out, out, out),
        grid=(rows // tr, cols // tc),
        in_specs=[pl.BlockSpec((nparts, tr, tc), lambda i, j: (0, i, j)), blk, blk, blk],
        out_specs=(blk, blk, blk, blk),
        compiler_params=_params(("parallel", "parallel")),
    )(parts, w, m, v)


CONV_PACK = 8 * 1024
WEIGHT_ORDER = ("norm1_w", "w_in", "dn_conv_w", "dn_A_log", "dn_dt_bias", "dn_norm_w", "w_proj_dn", "w_proj_sb",
                "w_out", "norm2_w", "ffn_w_up", "ffn_conv_w", "ffn_w_down", "norm_f_w")


def _cols_to_slabs(g):
    r, c8 = g.shape
    return g.reshape(r, N_DEV, c8 // N_DEV).transpose(1, 0, 2)


def _slabs_to_cols(s):
    d, r, c = s.shape
    return s.transpose(1, 0, 2).reshape(r, d * c)


def kernel(x, norm1_w, w_in, dn_conv_w, dn_A_log, dn_dt_bias, dn_norm_w, w_proj_dn, w_proj_sb, w_out, norm2_w, ffn_w_up, ffn_conv_w, ffn_w_down, norm_f_w, loss_target, m_norm1_w, m_w_in, m_dn_conv_w, m_dn_A_log, m_dn_dt_bias, m_dn_norm_w, m_w_proj_dn, m_w_proj_sb, m_w_out, m_norm2_w, m_ffn_w_up, m_ffn_conv_w, m_ffn_w_down, m_norm_f_w, v_norm1_w, v_w_in, v_dn_conv_w, v_dn_A_log, v_dn_dt_bias, v_dn_norm_w, v_w_proj_dn, v_w_proj_sb, v_w_out, v_norm2_w, v_ffn_w_up, v_ffn_conv_w, v_ffn_w_down, v_norm_f_w):
    me = _mesh_pos()[3]
    tr = lambda a: jnp.transpose(a[0])
    w_loc = dict(norm1_w=norm1_w, w_in=tr(w_in), dn_conv_w=dn_conv_w[0], dn_A_log=dn_A_log, dn_dt_bias=dn_dt_bias,
                 dn_norm_w=dn_norm_w, w_proj_dn=w_proj_dn[0], w_proj_sb=w_proj_sb[0], w_out=w_out[0],
                 norm2_w=norm2_w, ffn_w_up=tr(ffn_w_up), ffn_conv_w=ffn_conv_w[0], ffn_w_down=ffn_w_down[0],
                 norm_f_w=norm_f_w[None, :])
    m_loc = dict(norm1_w=m_norm1_w, w_in=tr(m_w_in), dn_conv_w=m_dn_conv_w[0], dn_A_log=m_dn_A_log,
                 dn_dt_bias=m_dn_dt_bias, dn_norm_w=m_dn_norm_w, w_proj_dn=m_w_proj_dn[0], w_proj_sb=m_w_proj_sb[0],
                 w_out=m_w_out[0], norm2_w=m_norm2_w, ffn_w_up=tr(m_ffn_w_up), ffn_conv_w=m_ffn_conv_w[0],
                 ffn_w_down=m_ffn_w_down[0], norm_f_w=m_norm_f_w[None, :])
    v_loc = dict(norm1_w=v_norm1_w, w_in=tr(v_w_in), dn_conv_w=v_dn_conv_w[0], dn_A_log=v_dn_A_log,
                 dn_dt_bias=v_dn_dt_bias, dn_norm_w=v_dn_norm_w, w_proj_dn=v_w_proj_dn[0], w_proj_sb=v_w_proj_sb[0],
                 w_out=v_w_out[0], norm2_w=v_norm2_w, ffn_w_up=tr(v_ffn_w_up), ffn_conv_w=v_ffn_conv_w[0],
                 ffn_w_down=v_ffn_w_down[0], norm_f_w=v_norm_f_w[None, :])

    conv_flat = jnp.concatenate([w_loc["dn_conv_w"].reshape(-1), w_loc["ffn_conv_w"].reshape(-1)])
    n_dn, n_ffn = DN_CONV * 3 * WIDTH // N_DEV, FFN_CONV * 2 * D_FF // N_DEV
    conv_pack = jnp.pad(conv_flat, (0, CONV_PACK - n_dn - n_ffn)).reshape(8, 1024)
    g_in, g_conv = _comm_call(_Gather([w_loc["w_in"].astype(BF16), conv_pack]), "gather_first")
    in_width = g_in.shape[0] * g_in.shape[1]
    w_in_t = g_in.reshape(in_width, D_MODEL)
    g_conv = g_conv.reshape(N_DEV, CONV_PACK)
    dn_conv_full = _slabs_to_cols(g_conv[:, :n_dn].reshape(N_DEV, DN_CONV, 3 * WIDTH // N_DEV))
    ffn_conv_full = _slabs_to_cols(g_conv[:, n_dn:n_dn + n_ffn].reshape(N_DEV, FFN_CONV, 2 * D_FF // N_DEV))
    q_end = 3 * WIDTH
    ab_end = q_end + 2 * HEADS
    gate_end = ab_end + WIDTH
    sb_end = gate_end + 3 * WIDTH
    pad_lanes = lambda a: jnp.pad(a, ((0, 0), (0, 128 - a.shape[1])))
    wts = dict(
        norm1_w=norm1_w, w_dnqkv_t=w_in_t[:q_end], w_ab_t=jnp.pad(w_in_t[q_end:ab_end], ((0, 128 - 2 * HEADS), (0, 0))),
        w_dngate_t=w_in_t[ab_end:gate_end], w_sbqkv_t=w_in_t[gate_end:sb_end], w_gl_t=w_in_t[sb_end:],
        dn_conv_w=dn_conv_full, alog=pad_lanes(dn_A_log), dtb=pad_lanes(dn_dt_bias), dn_norm_w=dn_norm_w,
        norm2_w=norm2_w, ffn_conv_w=ffn_conv_full, norm_f_w=norm_f_w[None, :])

    n_fc = FFN_CONV * 2 * D_FF
    fc_rows = -(-n_fc // D_MODEL)
    dn_rows = DN_CONV * 3 * WIDTH // D_MODEL
    late_names = ("w_proj_dn", "w_proj_sb", "w_out", "ffn_w_up", "ffn_w_down")

    class Plan:
        @staticmethod
        def late_gather():
            return _Gather([w_loc[k].astype(BF16) for k in late_names])

        @staticmethod
        def late_weights(got):
            g_pd, g_ps, g_out, g_up, g_down = got
            return dict(w_proj_dn=g_pd.reshape(WIDTH, D_MODEL), w_proj_sb=g_ps.reshape(WIDTH, D_MODEL),
                        w_out=g_out.reshape(D_MODEL, D_MODEL), ffn_w_up_t=g_up.reshape(2 * D_FF, D_MODEL),
                        ffn_w_down=g_down.reshape(D_FF, D_MODEL))

        @staticmethod
        def early_grads(g):
            return _Exchange([g["w_proj_dn"].reshape(N_DEV, WIDTH // N_DEV, D_MODEL),
                              g["w_proj_sb"].reshape(N_DEV, WIDTH // N_DEV, D_MODEL),
                              g["w_out"].reshape(N_DEV, D_MODEL // N_DEV, D_MODEL),
                              g["ffn_w_up_t"].reshape(N_DEV, 2 * D_FF // N_DEV, D_MODEL),
                              g["ffn_w_down"].reshape(N_DEV, D_FF // N_DEV, D_MODEL)])

        @staticmethod
        def _in_slabs(g):
            g_win_t = jnp.concatenate([g["w_main_t"][:q_end], g["w_ab_t"][:2 * HEADS], g["w_main_t"][q_end:]],
                                      axis=0)
            return g_win_t.reshape(N_DEV, in_width // N_DEV, D_MODEL)

        @staticmethod
        def sibling_swap(g):
            return _Exchange(sibling_slabs=[Plan._in_slabs(g)])

        @staticmethod
        def late_grads(swapped, g, loss):
            chip_sums = _add_my_slabs(Plan._in_slabs(g), swapped[0], "in_dw_chip_sum")
            row3 = jnp.concatenate([g["dn_norm_w"], g["alog"], g["dtb"], jnp.pad(loss, ((0, 0), (0, 127))),
                                    jnp.zeros((1, D_MODEL - 512), F32)], axis=1)
            fconv_rows = jnp.pad(g["ffn_conv_w"].reshape(-1), (0, fc_rows * D_MODEL - n_fc)).reshape(fc_rows, D_MODEL)
            pad8 = lambda a: jnp.pad(a, ((0, -a.shape[0] % 8), (0, 0)))
            pieces = [g["norm2_w"], g["norm_f_w"], row3, g["dn_conv_w"].reshape(dn_rows, D_MODEL), fconv_rows]
            small = jnp.concatenate([pad8(a) for a in pieces], axis=0)
            assert small.shape[0] == SMALL_ROWS
            return _Exchange(chip_slabs=[chip_sums], gathered=[small])

    loss, grad_x, g, got_early, got_late = _local_step(x[0], loss_target[0], wts, Plan)
    r_pd, r_ps, r_out, r_up, r_down = got_early
    r_in, r_small = got_late
    (r_norm1,) = _comm_call(_Exchange([], [jnp.pad(g["norm1_w"], ((0, 7), (0, 0)))]), "gather_norm1")

    parts = dict(w_in=r_in, w_proj_dn=r_pd, w_proj_sb=r_ps, w_out=r_out, ffn_w_up=r_up, ffn_w_down=r_down)
    parts["norm1_w"] = r_norm1[:, 0:1, :]
    parts["norm2_w"] = r_small[:, 0:1, :]
    parts["norm_f_w"] = r_small[:, 8:9, :]
    parts["dn_norm_w"] = r_small[:, 16:17, 0:HEAD_DIM]
    parts["dn_A_log"] = r_small[:, 16:17, 128:128 + HEADS]
    parts["dn_dt_bias"] = r_small[:, 16:17, 256:256 + HEADS]
    dnc = r_small[:, 24:24 + dn_rows, :].reshape(N_DEV, DN_CONV, 3 * WIDTH)
    parts["dn_conv_w"] = lax.dynamic_slice_in_dim(dnc, me * (3 * WIDTH // N_DEV), 3 * WIDTH // N_DEV, axis=2)
    fc0 = 24 + dn_rows + (-dn_rows % 8)
    fcc = r_small[:, fc0:fc0 + fc_rows, :].reshape(N_DEV, fc_rows * D_MODEL)[:, :n_fc]
    fcc = fcc.reshape(N_DEV, FFN_CONV, 2 * D_FF)
    parts["ffn_conv_w"] = lax.dynamic_slice_in_dim(fcc, me * (2 * D_FF // N_DEV), 2 * D_FF // N_DEV, axis=2)
    loss_total = jnp.sum(r_small[:, 16, 384])

    res = {k: _adamw(parts[k], w_loc[k], m_loc[k], v_loc[k], "adamw_" + k) for k in WEIGHT_ORDER}
    lead = ("w_in", "dn_conv_w", "w_proj_dn", "w_proj_sb", "w_out", "ffn_w_up", "ffn_conv_w", "ffn_w_down")

    def shaped(k, a):
        if k in ("w_in", "ffn_w_up"):
            return jnp.transpose(a)[None]
        if k in lead:
            return a[None]
        if k == "norm_f_w":
            return a[0]
        return a

    outs = [loss_total, grad_x[None]]
    for idx in range(4):
        outs += [shaped(k, res[k][idx]) for k in WEIGHT_ORDER]
    return tuple(outs)
```

```python
import functools

import jax
import jax.numpy as jnp
from jax import lax
from jax.experimental import pallas as pl
from jax.experimental.pallas import tpu as pltpu

F32 = jnp.float32
BF16 = jnp.bfloat16

N_DEV = 8
D_MODEL = 1024
HEADS = 8
HEAD_DIM = 128
WIDTH = HEADS * HEAD_DIM
DN_CONV = 4
DN_CHUNK = 64
D_FF = 2816
FFN_CONV = 3
EPS = 1e-6
HALO = 16
CHUNK_ROWS = 256
ATT_BLOCK = 256
SB_LOG_ZERO = -104.0
SB_GROUP = 2
SB_HEADS_FWD = 4
SB_HEADS_BWD = 2
SMALL_ROWS = 64

ADAM_LR = 0.001
ADAM_B1 = 0.9
ADAM_B2 = 0.999
ADAM_EPS = 1e-08
ADAM_WD = 0.01
ADAM_STEP = 10

VMEM_LIMIT = 48 * 1024 * 1024


def _params(sem=None, **kw):
    return pltpu.CompilerParams(dimension_semantics=sem, vmem_limit_bytes=VMEM_LIMIT, **kw)


def _tile(n, cap):
    if n <= cap:
        return n
    best = None
    for t in range(128, cap + 1, 128):
        if n % t == 0:
            best = t
    assert best is not None, (n, cap)
    return best


def _dot(a, b, dims):
    return lax.dot_general(a, b, ((dims[0], dims[1]), ((), ())), preferred_element_type=F32)


NN = ((1,), (0,))
NT = ((1,), (1,))
TN = ((0,), (0,))


def _dotb(a, b, dims):
    return _dot(a.astype(BF16), b.astype(BF16), dims)


def _split3(x):
    h1 = x.astype(BF16)
    r1 = x - h1.astype(F32)
    h2 = r1.astype(BF16)
    r2 = r1 - h2.astype(F32)
    return h1, h2, r2.astype(BF16)


def _dot_xr(a, b_exact, dims):
    a1, a2, a3 = _split3(a)
    return _dot(a1, b_exact, dims) + _dot(a2, b_exact, dims) + _dot(a3, b_exact, dims)


def _split2(x):
    h1 = x.astype(BF16)
    return h1, (x - h1.astype(F32)).astype(BF16)


def _dot_xr2(a, b_exact, dims):
    a1, a2 = _split2(a)
    return _dot(a1, b_exact, dims) + _dot(a2, b_exact, dims)


def _dot_xl(a_exact, b, dims):
    b1, b2, b3 = _split3(b)
    return _dot(a_exact, b1, dims) + _dot(a_exact, b2, dims) + _dot(a_exact, b3, dims)


def _dot3(a, b, dims):
    a1 = a.astype(BF16)
    a2 = (a - a1.astype(F32)).astype(BF16)
    b1 = b.astype(BF16)
    b2 = (b - b1.astype(F32)).astype(BF16)
    return _dot(a1, b1, dims) + (_dot(a1, b2, dims) + _dot(a2, b1, dims))


def _sigmoid(x):
    return 1.0 / (1.0 + jnp.exp(-x))


def _log1pexp_neg_abs(x):
    return jnp.log(1.0 + jnp.exp(-jnp.abs(x)))


def _iota(shape, dim):
    return lax.broadcasted_iota(jnp.int32, shape, dim)


def _matmul(a, b, mode, out_dtype, name, add=None, comm=None):
    if mode == "nn":
        (m, k), (k2, n) = a.shape, b.shape
    elif mode == "nt":
        (m, k), (n, k2) = a.shape, b.shape
    else:
        (k, m), (k2, n) = a.shape, b.shape
    assert k == k2, (a.shape, b.shape, mode)
    tm, tn, tk = _tile(m, 1408), _tile(n, 1408), _tile(k, 1536)
    nk = k // tk
    dims = {"nn": NN, "nt": NT, "tn": TN}[mode]

    def body(*refs):
        if add is None:
            a_ref, b_ref, o_ref, acc_ref = refs
        else:
            a_ref, b_ref, add_ref, o_ref, acc_ref = refs
        kk = pl.program_id(2)

        @pl.when(kk == 0)
        def _():
            acc_ref[...] = jnp.zeros_like(acc_ref)

        acc_ref[...] += _dotb(a_ref[...], b_ref[...], dims)

        @pl.when(kk == nk - 1)
        def _():
            r = acc_ref[...]
            if add is not None:
                r = r + add_ref[...].astype(F32)
            o_ref[...] = r.astype(out_dtype)

    if mode == "nn":
        specs = [pl.BlockSpec((tm, tk), lambda i, j, l: (i, l)), pl.BlockSpec((tk, tn), lambda i, j, l: (l, j))]
    elif mode == "nt":
        specs = [pl.BlockSpec((tm, tk), lambda i, j, l: (i, l)), pl.BlockSpec((tn, tk), lambda i, j, l: (j, l))]
    else:
        specs = [pl.BlockSpec((tk, tm), lambda i, j, l: (l, i)), pl.BlockSpec((tk, tn), lambda i, j, l: (l, j))]
    args = [a, b]
    if add is not None:
        specs.append(pl.BlockSpec((tm, tn), lambda i, j, l: (i, j)))
        args.append(add)
    grid = (m // tm, n // tn, nk)

    def when():
        i, j, l = pl.program_id(0), pl.program_id(1), pl.program_id(2)
        first = jnp.logical_and(jnp.logical_and(i == 0, j == 0), l == 0)
        last = jnp.logical_and(jnp.logical_and(i == grid[0] - 1, j == grid[1] - 1), l == nk - 1)
        return first, last, last

    (out,), extra = _host_call(
        body, name, comm, when, [jax.ShapeDtypeStruct((m, n), out_dtype)], grid, specs,
        [pl.BlockSpec((tm, tn), lambda i, j, l: (i, j))], [pltpu.VMEM((tm, tn), F32)],
        ("parallel", "parallel", "arbitrary"), args)
    return out if comm is None else (out, extra)


def _rmsnorm_fwd(x, w, name, comm=None):
    t, d = x.shape
    tr = _tile(t, 512)
    steps = t // tr

    def body(x_ref, w_ref, o_ref):
        xv = x_ref[...]
        r = lax.rsqrt(jnp.mean(xv * xv, axis=1, keepdims=True) + EPS)
        o_ref[...] = (xv * r * w_ref[...]).astype(BF16)

    def when():
        i = pl.program_id(0)
        return i == 0, i == steps // 2, i == steps - 1

    (out,), extra = _host_call(
        body, name, comm, when, [jax.ShapeDtypeStruct((t, d), BF16)], (steps,),
        [pl.BlockSpec((tr, d), lambda i: (i, 0)), pl.BlockSpec((1, d), lambda i: (0, 0))],
        [pl.BlockSpec((tr, d), lambda i: (i, 0))], [], ("parallel",), (x, w))
    return out if comm is None else (out, extra)


def _rmsnorm_bwd(dn, x, w, dres, name):
    t, d = x.shape
    tr = _tile(t, 512)

    def body(dn_ref, x_ref, w_ref, dres_ref, dx_ref, dw_ref):
        i = pl.program_id(0)
        xv = x_ref[...]
        g = dn_ref[...].astype(F32)
        r = lax.rsqrt(jnp.mean(xv * xv, axis=1, keepdims=True) + EPS)
        xh = xv * r
        dxh = g * w_ref[...]
        dx = r * (dxh - xh * jnp.mean(dxh * xh, axis=1, keepdims=True))
        dx_ref[...] = dres_ref[...] + dx

        @pl.when(i == 0)
        def _():
            dw_ref[...] = jnp.zeros_like(dw_ref)

        dw_ref[...] += jnp.sum(g * xh, axis=0, keepdims=True)

    return pl.pallas_call(
        body, name=name,
        out_shape=(jax.ShapeDtypeStruct((t, d), F32), jax.ShapeDtypeStruct((1, d), F32)),
        grid=(t // tr,),
        in_specs=[pl.BlockSpec((tr, d), lambda i: (i, 0)), pl.BlockSpec((tr, d), lambda i: (i, 0)),
                  pl.BlockSpec((1, d), lambda i: (0, 0)), pl.BlockSpec((tr, d), lambda i: (i, 0))],
        out_specs=(pl.BlockSpec((tr, d), lambda i: (i, 0)), pl.BlockSpec((1, d), lambda i: (0, 0))),
        compiler_params=_params(("arbitrary",)),
    )(dn, x, w, dres)


def _final_loss(x2, target, w, name):
    t, d = x2.shape
    tr = _tile(t, 512)

    def body(x_ref, t_ref, w_ref, dx_ref, dw_ref, loss_ref):
        i = pl.program_id(0)
        xv = x_ref[...]
        r = lax.rsqrt(jnp.mean(xv * xv, axis=1, keepdims=True) + EPS)
        xh = xv * r
        err = xh * w_ref[...] - t_ref[...]
        dy = err * (1.0 / d)
        dxh = dy * w_ref[...]
        dx_ref[...] = r * (dxh - xh * jnp.mean(dxh * xh, axis=1, keepdims=True))

        @pl.when(i == 0)
        def _():
            dw_ref[...] = jnp.zeros_like(dw_ref)
            loss_ref[...] = jnp.zeros_like(loss_ref)

        dw_ref[...] += jnp.sum(dy * xh, axis=0, keepdims=True)
        row = jnp.sum(err * err, axis=1, keepdims=True) * (0.5 / d)
        loss_ref[...] += jnp.sum(row, axis=0, keepdims=True)

    return pl.pallas_call(
        body, name=name,
        out_shape=(jax.ShapeDtypeStruct((t, d), F32), jax.ShapeDtypeStruct((1, d), F32),
                   jax.ShapeDtypeStruct((1, 1), F32)),
        grid=(t // tr,),
        in_specs=[pl.BlockSpec((tr, d), lambda i: (i, 0)), pl.BlockSpec((tr, d), lambda i: (i, 0)),
                  pl.BlockSpec((1, d), lambda i: (0, 0))],
        out_specs=(pl.BlockSpec((tr, d), lambda i: (i, 0)), pl.BlockSpec((1, d), lambda i: (0, 0)),
                   pl.BlockSpec((1, 1), lambda i: (0, 0))),
        compiler_params=_params(("arbitrary",)),
    )(x2, target, w)


def _shift_down(cur, prev, k, row):
    r = pltpu.roll(cur, k, 0)
    top, row8 = r[0:8, :], row[0:8, :]
    for m in range(k):
        top = jnp.where(row8 == m, prev[HALO - k + m:HALO - k + m + 1, :], top)
    return jnp.concatenate([top, r[8:, :]], axis=0)


def _shift_up(cur, nxt, k, row, tr):
    r = pltpu.roll(cur, tr - k, 0)
    bottom, row8 = r[tr - 8:, :], row[0:8, :]
    for m in range(k):
        bottom = jnp.where(row8 == 8 - k + m, nxt[m:m + 1, :], bottom)
    return jnp.concatenate([r[:tr - 8, :], bottom], axis=0)


def _fold8(a):
    out = a[0:8, :]
    for r in range(8, a.shape[0], 8):
        out = out + a[r:r + 8, :]
    return out


def _conv_taps(cur, prev, w, ntaps, row):
    taps = [cur if i == ntaps - 1 else _shift_down(cur, prev, ntaps - 1 - i, row) for i in range(ntaps)]
    y = w[0:1, :] * taps[0]
    for i in range(1, ntaps):
        y = y + w[i:i + 1, :] * taps[i]
    return taps, y


def _conv_bwd_data(parts, w, ntaps, out_dtype, name):
    t, chp = parts[0].shape
    npart = len(parts)
    tr, tc = _tile(t, 512), _tile(chp, 1408)
    nc = chp // tc
    nhalo = t // HALO
    last = t // tr - 1

    def body(*refs):
        cur_refs, nxt_refs = refs[:npart], refs[npart:2 * npart]
        w_ref, o_ref = refs[2 * npart], refs[2 * npart + 1]
        i, j = pl.program_id(0), pl.program_id(1)
        row = _iota((tr, 128), 0)
        for c0 in range(0, tc, 128):
            sl = slice(c0, c0 + 128)
            cur, nxt = cur_refs[0][:, sl].astype(F32), nxt_refs[0][:, sl].astype(F32)
            for p in range(1, npart):
                cur = jnp.where(j >= p * nc, cur_refs[p][:, sl].astype(F32), cur)
                nxt = jnp.where(j >= p * nc, nxt_refs[p][:, sl].astype(F32), nxt)
            nxt = jnp.where(i == last, 0.0, nxt)
            wv = w_ref[:, sl]
            y = wv[ntaps - 1:ntaps, :] * cur
            for k in range(1, ntaps):
                y = y + wv[ntaps - 1 - k:ntaps - k, :] * _shift_up(cur, nxt, k, row, tr)
            o_ref[:, sl] = y.astype(out_dtype)

    col = lambda p: (lambda j: jnp.clip(j - p * nc, 0, nc - 1))
    cur_specs = [pl.BlockSpec((tr, tc), lambda i, j, c=col(p): (i, c(j))) for p in range(npart)]
    nxt_specs = [pl.BlockSpec((HALO, tc),
                              lambda i, j, c=col(p): (jnp.minimum((i + 1) * (tr // HALO), nhalo - 1), c(j)))
                 for p in range(npart)]
    return pl.pallas_call(
        body, name=name,
        out_shape=jax.ShapeDtypeStruct((t, npart * chp), out_dtype),
        grid=(t // tr, npart * nc),
        in_specs=cur_specs + nxt_specs + [pl.BlockSpec((ntaps, tc), lambda i, j: (0, j))],
        out_specs=pl.BlockSpec((tr, tc), lambda i, j: (i, j)),
        compiler_params=_params(("parallel", "parallel")),
    )(*parts, *parts, w)


def _ffn_act_fwd(upre, cw, name):
    t = upre.shape[0]
    tr, tc = _tile(t, 512), _tile(D_FF, 1408)
    nj = D_FF // tc

    def body(g_ref, gp_ref, u_ref, up_ref, wg_ref, wu_ref, o_ref):
        i = pl.program_id(0)
        row = _iota((tr, 128), 0)
        for c0 in range(0, tc, 128):
            sl = slice(c0, c0 + 128)
            gp = jnp.where(i == 0, 0.0, gp_ref[:, sl].astype(F32))
            up = jnp.where(i == 0, 0.0, up_ref[:, sl].astype(F32))
            _, gc = _conv_taps(g_ref[:, sl].astype(F32), gp, wg_ref[:, sl], FFN_CONV, row)
            _, uc = _conv_taps(u_ref[:, sl].astype(F32), up, wu_ref[:, sl], FFN_CONV, row)
            o_ref[:, sl] = (gc * _sigmoid(gc) * uc).astype(BF16)

    prev = lambda off: (lambda i, j: (jnp.maximum(i * (tr // HALO) - 1, 0), j + off))
    return pl.pallas_call(
        body, name=name,
        out_shape=jax.ShapeDtypeStruct((t, D_FF), BF16),
        grid=(t // tr, nj),
        in_specs=[pl.BlockSpec((tr, tc), lambda i, j: (i, j)), pl.BlockSpec((HALO, tc), prev(0)),
                  pl.BlockSpec((tr, tc), lambda i, j: (i, j + nj)), pl.BlockSpec((HALO, tc), prev(nj)),
                  pl.BlockSpec((FFN_CONV, tc), lambda i, j: (0, j)),
                  pl.BlockSpec((FFN_CONV, tc), lambda i, j: (0, j + nj))],
        out_specs=pl.BlockSpec((tr, tc), lambda i, j: (i, j)),
        compiler_params=_params(("parallel", "parallel")),
    )(upre, upre, upre, upre, cw, cw)


def _ffn_act_bwd(dact, upre, cw, name):
    t = upre.shape[0]
    tr, tc = _tile(t, 256), _tile(D_FF, 1408)
    nj = D_FF // tc

    def body(da_ref, g_ref, gp_ref, u_ref, up_ref, wg_ref, wu_ref, dg_ref, du_ref, dwg_ref, dwu_ref):
        i = pl.program_id(1)
        row = _iota((CHUNK_ROWS, 128), 0)

        @pl.when(i == 0)
        def _():
            dwg_ref[...] = jnp.zeros_like(dwg_ref)
            dwu_ref[...] = jnp.zeros_like(dwu_ref)

        for c0 in range(0, tc, 128):
            sl = slice(c0, c0 + 128)
            wg, wu = wg_ref[:, sl], wu_ref[:, sl]
            dwg = [jnp.zeros((8, 128), F32)] * FFN_CONV
            dwu = [jnp.zeros((8, 128), F32)] * FFN_CONV
            for r0 in range(0, tr, CHUNK_ROWS):
                rows = slice(r0, r0 + CHUNK_ROWS)
                if r0 == 0:
                    gp = jnp.where(i == 0, 0.0, gp_ref[:, sl].astype(F32))
                    up = jnp.where(i == 0, 0.0, up_ref[:, sl].astype(F32))
                else:
                    gp = g_ref[r0 - HALO:r0, sl].astype(F32)
                    up = u_ref[r0 - HALO:r0, sl].astype(F32)
                gt, gc = _conv_taps(g_ref[rows, sl].astype(F32), gp, wg, FFN_CONV, row)
                ut, uc = _conv_taps(u_ref[rows, sl].astype(F32), up, wu, FFN_CONV, row)
                da = da_ref[rows, sl].astype(F32)
                sg = _sigmoid(gc)
                dgc = da * uc * (sg * (1.0 + gc * (1.0 - sg)))
                duc = da * (gc * sg)
                dg_ref[rows, sl] = dgc.astype(BF16)
                du_ref[rows, sl] = duc.astype(BF16)
                dwg = [dwg[k] + _fold8(dgc * gt[k]) for k in range(FFN_CONV)]
                dwu = [dwu[k] + _fold8(duc * ut[k]) for k in range(FFN_CONV)]
            for k in range(FFN_CONV):
                dwg_ref[k:k + 1, sl] += jnp.sum(dwg[k], axis=0, keepdims=True)
                dwu_ref[k:k + 1, sl] += jnp.sum(dwu[k], axis=0, keepdims=True)

    prev = lambda off: (lambda j, i: (jnp.maximum(i * (tr // HALO) - 1, 0), j + off))
    blk = lambda off: pl.BlockSpec((tr, tc), lambda j, i: (i, j + off))
    wblk = lambda off: pl.BlockSpec((FFN_CONV, tc), lambda j, i: (0, j + off))
    dgc, duc, dwg, dwu = pl.pallas_call(
        body, name=name,
        out_shape=(jax.ShapeDtypeStruct((t, D_FF), BF16), jax.ShapeDtypeStruct((t, D_FF), BF16),
                   jax.ShapeDtypeStruct((FFN_CONV, D_FF), F32), jax.ShapeDtypeStruct((FFN_CONV, D_FF), F32)),
        grid=(nj, t // tr),
        in_specs=[blk(0), blk(0), pl.BlockSpec((HALO, tc), prev(0)), blk(nj), pl.BlockSpec((HALO, tc), prev(nj)),
                  wblk(0), wblk(nj)],
        out_specs=(blk(0), blk(0), wblk(0), wblk(0)),
        compiler_params=_params(("parallel", "arbitrary")),
    )(dact, upre, upre, upre, upre, cw, cw)
    return dgc, duc, dwg, dwu


def _dn_pre_fwd(qkv_pre, cw, name):
    t = qkv_pre.shape[0]
    tr = _tile(t, 512)
    scale = HEAD_DIM ** -0.5

    def body(x_ref, p_ref, w_ref, o_ref):
        i, j = pl.program_id(0), pl.program_id(1)
        row = _iota((tr, HEAD_DIM), 0)
        for h in range(HEADS):
            sl = slice(h * HEAD_DIM, (h + 1) * HEAD_DIM)
            prev = jnp.where(i == 0, 0.0, p_ref[:, sl].astype(F32))
            _, c = _conv_taps(x_ref[:, sl].astype(F32), prev, w_ref[:, sl], DN_CONV, row)
            s = c * _sigmoid(c)
            r = lax.rsqrt(jnp.sum(s * s, axis=1, keepdims=True) + EPS)
            o_ref[:, sl] = s * jnp.where(j == 0, r * scale, jnp.where(j == 1, r, 1.0))

    return pl.pallas_call(
        body, name=name,
        out_shape=jax.ShapeDtypeStruct((t, 3 * WIDTH), F32),
        grid=(t // tr, 3),
        in_specs=[pl.BlockSpec((tr, WIDTH), lambda i, j: (i, j)),
                  pl.BlockSpec((HALO, WIDTH), lambda i, j: (jnp.maximum(i * (tr // HALO) - 1, 0), j)),
                  pl.BlockSpec((DN_CONV, WIDTH), lambda i, j: (0, j))],
        out_specs=pl.BlockSpec((tr, WIDTH), lambda i, j: (i, j)),
        compiler_params=_params(("parallel", "parallel")),
    )(qkv_pre, qkv_pre, cw)


def _dn_pre_bwd(dq, dk, dv, qkv_pre, cw, name):
    t = qkv_pre.shape[0]
    tr = _tile(t, 256)
    scale = HEAD_DIM ** -0.5

    def body(dq_ref, dk_ref, dv_ref, x_ref, p_ref, w_ref, dc_ref, dw_ref):
        j, i = pl.program_id(0), pl.program_id(1)
        row = _iota((CHUNK_ROWS, HEAD_DIM), 0)

        @pl.when(i == 0)
        def _():
            dw_ref[...] = jnp.zeros_like(dw_ref)

        for h in range(HEADS):
            sl = slice(h * HEAD_DIM, (h + 1) * HEAD_DIM)
            wv = w_ref[:, sl]
            dw = [jnp.zeros((8, HEAD_DIM), F32)] * DN_CONV
            for r0 in range(0, tr, CHUNK_ROWS):
                rows = slice(r0, r0 + CHUNK_ROWS)
                if r0 == 0:
                    prev = jnp.where(i == 0, 0.0, p_ref[:, sl].astype(F32))
                else:
                    prev = x_ref[r0 - HALO:r0, sl].astype(F32)
                taps, c = _conv_taps(x_ref[rows, sl].astype(F32), prev, wv, DN_CONV, row)
                d = jnp.where(j == 0, dq_ref[rows, sl] * scale, jnp.where(j == 1, dk_ref[rows, sl], dv_ref[rows, sl]))
                sg = _sigmoid(c)
                s = c * sg
                r = lax.rsqrt(jnp.sum(s * s, axis=1, keepdims=True) + EPS)
                nh = s * r
                ds_norm = r * (d - nh * jnp.sum(nh * d, axis=1, keepdims=True))
                dc = jnp.where(j < 2, ds_norm, d) * (sg * (1.0 + c * (1.0 - sg)))
                dc_ref[rows, sl] = dc.astype(BF16)
                dw = [dw[k] + _fold8(dc * taps[k]) for k in range(DN_CONV)]
            for k in range(DN_CONV):
                dw_ref[k:k + 1, sl] += jnp.sum(dw[k], axis=0, keepdims=True)

    dspec = lambda p: pl.BlockSpec((tr, WIDTH), lambda j, i: (jnp.where(j == p, i, 0), 0))
    return pl.pallas_call(
        body, name=name,
        out_shape=(jax.ShapeDtypeStruct((t, 3 * WIDTH), BF16), jax.ShapeDtypeStruct((DN_CONV, 3 * WIDTH), F32)),
        grid=(3, t // tr),
        in_specs=[dspec(0), dspec(1), dspec(2),
                  pl.BlockSpec((tr, WIDTH), lambda j, i: (i, j)),
                  pl.BlockSpec((HALO, WIDTH), lambda j, i: (jnp.maximum(i * (tr // HALO) - 1, 0), j)),
                  pl.BlockSpec((DN_CONV, WIDTH), lambda j, i: (0, j))],
        out_specs=(pl.BlockSpec((tr, WIDTH), lambda j, i: (i, j)),
                   pl.BlockSpec((DN_CONV, WIDTH), lambda j, i: (0, j))),
        compiler_params=_params(("parallel", "arbitrary")),
    )(dq, dk, dv, qkv_pre, qkv_pre, cw)


def _tri(n, kind):
    r, c = _iota((n, n), 0), _iota((n, n), 1)
    m = {"lower": r >= c, "strict": r > c, "upper": r <= c}[kind]
    return m


GATE_ROWS = 4 * DN_CHUNK


def _chunk_tri(kind):
    r, c = _iota((GATE_ROWS, GATE_ROWS), 0), _iota((GATE_ROWS, GATE_ROWS), 1)
    same = (r // DN_CHUNK) == (c // DN_CHUNK)
    return jnp.where(jnp.logical_and(same, _tri(GATE_ROWS, kind)), 1.0, 0.0).astype(BF16)


def _dn_gates_fwd(hab, alog, dtb, name):
    t = hab.shape[0]
    cc = GATE_ROWS

    def body(h_ref, al_ref, dt_ref, o_ref):
        hv = h_ref[...]
        lane = _iota(hv.shape, 1)
        xa = hv + dt_ref[...]
        sp = jnp.maximum(xa, 0.0) + _log1pexp_neg_abs(xa)
        g = jnp.where(lane < HEADS, -jnp.exp(al_ref[...]) * sp, 0.0)
        gc = _dot_xl(_chunk_tri("lower"), g, NN)
        o_ref[...] = jnp.where(lane < HEADS, gc, jnp.where(lane < 2 * HEADS, _sigmoid(hv), 0.0))

    return pl.pallas_call(
        body, name=name,
        out_shape=jax.ShapeDtypeStruct((t, 128), F32),
        grid=(t // cc,),
        in_specs=[pl.BlockSpec((cc, 128), lambda i: (i, 0)), pl.BlockSpec((1, 128), lambda i: (0, 0)),
                  pl.BlockSpec((1, 128), lambda i: (0, 0))],
        out_specs=pl.BlockSpec((cc, 128), lambda i: (i, 0)),
        compiler_params=_params(("parallel",)),
    )(hab, alog, dtb)


def _dn_gates_bwd(dgates, hab, alog, dtb, name):
    t = hab.shape[0]
    cc = GATE_ROWS

    def body(d_ref, h_ref, al_ref, dt_ref, o_ref, dal_ref, ddt_ref):
        i = pl.program_id(0)
        hv = h_ref[...]
        dv = d_ref[...]
        lane = _iota(hv.shape, 1)
        dg = _dot_xl(_chunk_tri("upper"), jnp.where(lane < HEADS, dv, 0.0), NN)
        xa = hv + dt_ref[...]
        sp = jnp.maximum(xa, 0.0) + _log1pexp_neg_abs(xa)
        ea = jnp.exp(al_ref[...])
        da = jnp.where(lane < HEADS, dg * (-ea) * _sigmoid(xa), 0.0)
        be = _sigmoid(hv)
        db = dv * be * (1.0 - be)
        o_ref[...] = jnp.where(lane < HEADS, da, jnp.where(lane < 2 * HEADS, db, 0.0))

        @pl.when(i == 0)
        def _():
            dal_ref[...] = jnp.zeros_like(dal_ref)
            ddt_ref[...] = jnp.zeros_like(ddt_ref)

        dal_ref[...] += jnp.sum(jnp.where(lane < HEADS, dg * (-ea) * sp, 0.0), axis=0, keepdims=True)
        ddt_ref[...] += jnp.sum(da, axis=0, keepdims=True)

    return pl.pallas_call(
        body, name=name,
        out_shape=(jax.ShapeDtypeStruct((t, 128), F32), jax.ShapeDtypeStruct((1, 128), F32),
                   jax.ShapeDtypeStruct((1, 128), F32)),
        grid=(t // cc,),
        in_specs=[pl.BlockSpec((cc, 128), lambda i: (i, 0)), pl.BlockSpec((cc, 128), lambda i: (i, 0)),
                  pl.BlockSpec((1, 128), lambda i: (0, 0)), pl.BlockSpec((1, 128), lambda i: (0, 0))],
        out_specs=(pl.BlockSpec((cc, 128), lambda i: (i, 0)), pl.BlockSpec((1, 128), lambda i: (0, 0)),
                   pl.BlockSpec((1, 128), lambda i: (0, 0))),
        compiler_params=_params(("arbitrary",)),
    )(dgates, hab, alog, dtb)


def _dn_chunk_common(gates, h):
    cc = DN_CHUNK
    lane = _iota(gates.shape, 1)
    gh = jnp.where(lane == h, gates, 0.0)
    gc_col = jnp.sum(gh, axis=1, keepdims=True)
    gc_row = _dot_xl(jnp.ones((cc, 128), BF16), gh, NT)
    beta = jnp.sum(jnp.where(lane == h + HEADS, gates, 0.0), axis=1, keepdims=True)
    lower = _tri(cc, "lower")
    decay = jnp.where(lower, jnp.exp(jnp.where(lower, gc_col - gc_row, 0.0)), 0.0)
    gc_last = gc_col[cc - 1:cc, :]
    return gc_col, gc_last, beta, decay


def _dn_local_fwd(act, gates, name):
    t = act.shape[0]
    cc = DN_CHUNK
    nc = t // cc

    def body(q_ref, k_ref, v_ref, g_ref, u_ref, w_ref, kd_ref, qg_ref, ti_ref, p_ref):
        gates = g_ref[...]
        eye = jnp.where(_iota((cc, cc), 0) == _iota((cc, cc), 1), 1.0, 0.0)
        hs = range(HEADS)
        sl = [slice(h * HEAD_DIM, (h + 1) * HEAD_DIM) for h in hs]
        q, k, v = ([r[:, s] for s in sl] for r in (q_ref, k_ref, v_ref))
        gc_col, gc_last, beta, decay = zip(*[_dn_chunk_common(gates, h) for h in hs])
        gam = [jnp.exp(g) for g in gc_col]
        kb = [k[h] * beta[h] for h in hs]
        npow = [-jnp.where(_tri(cc, "strict"), _dotb(kb[h], k[h], NT) * decay[h], 0.0) for h in hs]
        tinv = [eye + n for n in npow]
        for _ in range(5):
            npow = [_dot3(n, n, NN) for n in npow]
            tinv = [t + _dot3(t, n, NN) for t, n in zip(tinv, npow)]
        uu = [_dot3(tinv[h], v[h] * beta[h], NN) for h in hs]
        ww = [_dot3(tinv[h], kb[h] * gam[h], NN) for h in hs]
        pp = [jnp.where(_tri(cc, "lower"), _dotb(q[h], k[h], NT) * decay[h], 0.0) for h in hs]
        for h in hs:
            u_ref[:, sl[h]] = uu[h]
            w_ref[:, sl[h]] = ww[h]
            kd_ref[:, sl[h]] = k[h] * jnp.exp(gc_last[h] - gc_col[h])
            qg_ref[:, sl[h]] = q[h] * gam[h]
            ti_ref[h] = tinv[h]
            p_ref[h] = pp[h]

    row = lambda off: pl.BlockSpec((cc, WIDTH), lambda n: (n, off))
    mat = pl.BlockSpec((HEADS, cc, cc), lambda n: (0, n, 0))
    tw = jax.ShapeDtypeStruct((t, WIDTH), F32)
    hm = jax.ShapeDtypeStruct((HEADS, t, cc), F32)
    return pl.pallas_call(
        body, name=name,
        out_shape=(tw, tw, tw, tw, hm, hm),
        grid=(nc,),
        in_specs=[row(0), row(1), row(2), pl.BlockSpec((cc, 128), lambda n: (n, 0))],
        out_specs=(row(0), row(0), row(0), row(0), mat, mat),
        compiler_params=_params(("parallel",)),
    )(act, act, act, gates)


def _dn_scan_fwd(u, w, kd, qg, p, gates, name):
    t = u.shape[0]
    cc = DN_CHUNK
    nc = t // cc

    def body(u_ref, w_ref, kd_ref, qg_ref, p_ref, g_ref, o_ref, sh_ref, s_ref):
        n = pl.program_id(0)

        @pl.when(n == 0)
        def _():
            s_ref[...] = jnp.zeros_like(s_ref)

        glast = jnp.exp(g_ref[cc - 1:cc, :])
        hs = range(HEADS)
        sl = [slice(h * HEAD_DIM, (h + 1) * HEAD_DIM) for h in hs]
        s = [s_ref[h] for h in hs]
        sb = [a.astype(BF16) for a in s]
        vn = [u_ref[:, sl[h]] - _dot(w_ref[:, sl[h]].astype(BF16), sb[h], NN) for h in hs]
        vnb = [a.astype(BF16) for a in vn]
        o_state = [_dot(qg_ref[:, sl[h]].astype(BF16), sb[h], NN) for h in hs]
        o_local = [_dot(p_ref[h].astype(BF16), vnb[h], NN) for h in hs]
        s_add = [_dot(kd_ref[:, sl[h]].astype(BF16), vnb[h], TN) for h in hs]
        for h in hs:
            o_ref[:, sl[h]] = o_state[h] + o_local[h]
            sh_ref[0, h] = s[h]
            s_ref[h] = glast[:, h:h + 1] * s[h] + s_add[h]

    row = pl.BlockSpec((cc, WIDTH), lambda n: (n, 0))
    return pl.pallas_call(
        body, name=name,
        out_shape=(jax.ShapeDtypeStruct((t, WIDTH), F32),
                   jax.ShapeDtypeStruct((nc, HEADS, HEAD_DIM, HEAD_DIM), F32)),
        grid=(nc,),
        in_specs=[row, row, row, row, pl.BlockSpec((HEADS, cc, cc), lambda n: (0, n, 0)),
                  pl.BlockSpec((cc, 128), lambda n: (n, 0))],
        out_specs=(row, pl.BlockSpec((1, HEADS, HEAD_DIM, HEAD_DIM), lambda n: (n, 0, 0, 0))),
        scratch_shapes=[pltpu.VMEM((HEADS, HEAD_DIM, HEAD_DIM), F32)],
        compiler_params=_params(("arbitrary",)),
    )(u, w, kd, qg, p, gates)


def _dn_scan_bwd(do, w, kd, qg, p, gates, name):
    t = do.shape[0]
    cc = DN_CHUNK
    nc = t // cc

    def body(do_ref, w_ref, kd_ref, qg_ref, p_ref, g_ref, dvn_ref, dsh_ref, ds_ref):
        n = pl.program_id(0)

        @pl.when(n == 0)
        def _():
            ds_ref[...] = jnp.zeros_like(ds_ref)

        glast = jnp.exp(g_ref[cc - 1:cc, :])
        hs = range(HEADS)
        sl = [slice(h * HEAD_DIM, (h + 1) * HEAD_DIM) for h in hs]
        ds = [ds_ref[h] for h in hs]
        dob = [do_ref[:, sl[h]].astype(BF16) for h in hs]
        dvn = [_dot(p_ref[h].astype(BF16), dob[h], TN) + _dot(kd_ref[:, sl[h]].astype(BF16), ds[h].astype(BF16), NN)
               for h in hs]
        ds_q = [_dot(qg_ref[:, sl[h]].astype(BF16), dob[h], TN) for h in hs]
        ds_w = [_dot(w_ref[:, sl[h]].astype(BF16), dvn[h].astype(BF16), TN) for h in hs]
        for h in hs:
            dvn_ref[:, sl[h]] = dvn[h]
            dsh_ref[0, h] = ds[h]
            ds_ref[h] = ds_q[h] + glast[:, h:h + 1] * ds[h] - ds_w[h]

    row = pl.BlockSpec((cc, WIDTH), lambda n: (nc - 1 - n, 0))
    return pl.pallas_call(
        body, name=name,
        out_shape=(jax.ShapeDtypeStruct((t, WIDTH), F32),
                   jax.ShapeDtypeStruct((nc, HEADS, HEAD_DIM, HEAD_DIM), F32)),
        grid=(nc,),
        in_specs=[row, row, row, row, pl.BlockSpec((HEADS, cc, cc), lambda n: (0, nc - 1 - n, 0)),
                  pl.BlockSpec((cc, 128), lambda n: (nc - 1 - n, 0))],
        out_specs=(row, pl.BlockSpec((1, HEADS, HEAD_DIM, HEAD_DIM), lambda n: (nc - 1 - n, 0, 0, 0))),
        scratch_shapes=[pltpu.VMEM((HEADS, HEAD_DIM, HEAD_DIM), F32)],
        compiler_params=_params(("arbitrary",)),
    )(do, w, kd, qg, p, gates)


def _dn_local_bwd(act, gates, u, w, kd, qg, tinv, p, sh, dsh, dvn, do, name):
    t = act.shape[0]
    cc = DN_CHUNK
    nc = t // cc

    def body(q_ref, k_ref, v_ref, g_ref, u_ref, w_ref, kd_ref, qg_ref, ti_ref, p_ref, s_ref, ds_ref,
             dvn_ref, do_ref, dq_ref, dk_ref, dv_ref, dg_ref):
        gates_v = g_ref[...]
        lower, strict = _tri(cc, "lower"), _tri(cc, "strict")
        ones = jnp.ones((cc, 128), BF16)
        rowc = _iota((cc, 1), 0)
        lane = _iota((cc, 128), 1)
        hs = range(HEADS)
        sl = [slice(h * HEAD_DIM, (h + 1) * HEAD_DIM) for h in hs]
        q, k, v, uu, ww, kd, qg, dvn, do = ([r[:, s] for s in sl] for r in (
            q_ref, k_ref, v_ref, u_ref, w_ref, kd_ref, qg_ref, dvn_ref, do_ref))
        gc_col, gc_last, beta, decay = zip(*[_dn_chunk_common(gates_v, h) for h in hs])
        gam = [jnp.exp(g) for g in gc_col]
        kb = [k[h] * beta[h] for h in hs]
        s_in = [s_ref[0, h] for h in hs]
        ds_out = [ds_ref[0, h] for h in hs]
        tinv = [ti_ref[h] for h in hs]

        a = [jnp.where(strict, _dotb(kb[h], k[h], NT) * decay[h], 0.0) for h in hs]
        vn = [uu[h] - _dotb(ww[h], s_in[h], NN) for h in hs]
        dqg = [_dotb(do[h], s_in[h], NT) for h in hs]
        dw = [-_dotb(dvn[h], s_in[h], NT) for h in hs]
        dp = [jnp.where(lower, _dotb(do[h], vn[h], NT), 0.0) for h in hs]
        dkd = [_dotb(vn[h], ds_out[h], NT) for h in hs]
        dru = [_dot3(tinv[h], dvn[h], TN) for h in hs]
        drw = [_dot3(tinv[h], dw[h], TN) for h in hs]
        da = [-jnp.where(strict, _dotb(dru[h], uu[h], NT) + _dotb(drw[h], ww[h], NT), 0.0) for h in hs]
        dad = [da[h] * decay[h] for h in hs]
        dpd = [dp[h] * decay[h] for h in hs]
        dkb = [_dotb(dad[h], k[h], NN) + gam[h] * drw[h] for h in hs]
        dk = [_dotb(dad[h], kb[h], TN) + _dotb(dpd[h], q[h], TN) + beta[h] * dkb[h]
              + jnp.exp(gc_last[h] - gc_col[h]) * dkd[h] for h in hs]
        dq = [gam[h] * dqg[h] + _dotb(dpd[h], k[h], NN) for h in hs]
        gm = [da[h] * a[h] + dp[h] * p_ref[h] for h in hs]
        colsum = [_dot_xr(gm[h], ones, TN)[:, 0:1] for h in hs]

        dgates = jnp.zeros((cc, 128), F32)
        for h in hs:
            dk_ref[:, sl[h]] = dk[h]
            dq_ref[:, sl[h]] = dq[h]
            dv_ref[:, sl[h]] = beta[h] * dru[h]
            dbeta = (jnp.sum(dkb[h] * k[h], axis=1, keepdims=True)
                     + jnp.sum(dru[h] * v[h], axis=1, keepdims=True))
            rkd = jnp.sum(dkd[h] * kd[h], axis=1, keepdims=True)
            dgc = (jnp.sum(gm[h], axis=1, keepdims=True) - colsum[h]
                   + jnp.sum(dqg[h] * qg[h], axis=1, keepdims=True)
                   + jnp.sum(drw[h] * kb[h], axis=1, keepdims=True) * gam[h] - rkd)
            tail = jnp.sum(rkd, axis=0, keepdims=True) + jnp.exp(gc_last[h]) * jnp.sum(
                jnp.sum(s_in[h] * ds_out[h], axis=1, keepdims=True), axis=0, keepdims=True)
            dgc = dgc + jnp.where(rowc == cc - 1, tail, 0.0)
            dgates = dgates + jnp.where(lane == h, dgc, 0.0) + jnp.where(lane == h + HEADS, dbeta, 0.0)
        dg_ref[...] = dgates

    row = lambda off: pl.BlockSpec((cc, WIDTH), lambda n: (n, off))
    mat = pl.BlockSpec((HEADS, cc, cc), lambda n: (0, n, 0))
    st = pl.BlockSpec((1, HEADS, HEAD_DIM, HEAD_DIM), lambda n: (n, 0, 0, 0))
    gl = pl.BlockSpec((cc, 128), lambda n: (n, 0))
    tw = jax.ShapeDtypeStruct((t, WIDTH), F32)
    return pl.pallas_call(
        body, name=name,
        out_shape=(tw, tw, tw, jax.ShapeDtypeStruct((t, 128), F32)),
        grid=(nc,),
        in_specs=[row(0), row(1), row(2), gl, row(0), row(0), row(0), row(0), mat, mat, st, st, row(0), row(0)],
        out_specs=(row(0), row(0), row(0), gl),
        compiler_params=_params(("parallel",)),
    )(act, act, act, gates, u, w, kd, qg, tinv, p, sh, dsh, dvn, do)


def _dn_post_fwd(o, gate, w, name):
    t = o.shape[0]
    tr = _tile(t, 512)

    def body(o_ref, g_ref, w_ref, y_ref):
        for h in range(HEADS):
            sl = slice(h * HEAD_DIM, (h + 1) * HEAD_DIM)
            ov, gv = o_ref[:, sl], g_ref[:, sl].astype(F32)
            r = lax.rsqrt(jnp.mean(ov * ov, axis=1, keepdims=True) + EPS)
            y_ref[:, sl] = (ov * r * w_ref[...] * (gv * _sigmoid(gv))).astype(BF16)

    blk = pl.BlockSpec((tr, WIDTH), lambda i: (i, 0))
    return pl.pallas_call(
        body, name=name,
        out_shape=jax.ShapeDtypeStruct((t, WIDTH), BF16),
        grid=(t // tr,),
        in_specs=[blk, blk, pl.BlockSpec((1, HEAD_DIM), lambda i: (0, 0))],
        out_specs=blk,
        compiler_params=_params(("parallel",)),
    )(o, gate, w)


def _dn_post_bwd(dy, o, gate, w, name):
    t = o.shape[0]
    tr = _tile(t, 512)

    def body(dy_ref, o_ref, g_ref, w_ref, do_ref, dg_ref, dw_ref):
        i = pl.program_id(0)

        @pl.when(i == 0)
        def _():
            dw_ref[...] = jnp.zeros_like(dw_ref)

        dw = jnp.zeros((1, HEAD_DIM), F32)
        for h in range(HEADS):
            sl = slice(h * HEAD_DIM, (h + 1) * HEAD_DIM)
            ov, gv, dyv = o_ref[:, sl], g_ref[:, sl].astype(F32), dy_ref[:, sl].astype(F32)
            r = lax.rsqrt(jnp.mean(ov * ov, axis=1, keepdims=True) + EPS)
            oh = ov * r
            sg = _sigmoid(gv)
            dg_ref[:, sl] = (dyv * oh * w_ref[...] * (sg * (1.0 + gv * (1.0 - sg)))).astype(BF16)
            dn = dyv * (gv * sg)
            doh = dn * w_ref[...]
            do_ref[:, sl] = r * (doh - oh * jnp.mean(doh * oh, axis=1, keepdims=True))
            dw = dw + jnp.sum(dn * oh, axis=0, keepdims=True)
        dw_ref[...] += dw

    blk = pl.BlockSpec((tr, WIDTH), lambda i: (i, 0))
    return pl.pallas_call(
        body, name=name,
        out_shape=(jax.ShapeDtypeStruct((t, WIDTH), F32), jax.ShapeDtypeStruct((t, WIDTH), BF16),
                   jax.ShapeDtypeStruct((1, HEAD_DIM), F32)),
        grid=(t // tr,),
        in_specs=[blk, blk, blk, pl.BlockSpec((1, HEAD_DIM), lambda i: (0, 0))],
        out_specs=(blk, blk, pl.BlockSpec((1, HEAD_DIM), lambda i: (0, 0))),
        compiler_params=_params(("arbitrary",)),
    )(dy, o, gate, w)


def _sb_scores(qs, k_ref, qi, it, carries, uincl):
    bk = ATT_BLOCK
    scale = HEAD_DIM ** -0.5
    heads, groups = range(len(qs)), range(SB_GROUP)
    lane = [slice(e * HEAD_DIM, (e + 1) * HEAD_DIM) for e in heads]
    js = [qi - SB_GROUP * it - g for g in groups]
    rows = [pl.ds(pl.multiple_of(jnp.maximum(j, 0) * bk, bk), bk) for j in js]
    qpos = qi * bk + _iota((bk, bk), 0)
    col = _iota((bk, bk), 1)
    mask1 = [jnp.logical_and(j * bk + col < qpos, j >= 0) for j in js]
    ks = [[k_ref[r, lane[e]] for r in rows] for e in heads]
    z = [[_dot(qs[e], k, NT) * scale for k in ks[e]] for e in heads]
    soft = [[_log1pexp_neg_abs(a) for a in ze] for ze in z]
    lk_full = [[-(jnp.maximum(a, 0.0) + s) for a, s in zip(z[e], soft[e])] for e in heads]
    lk = [[jnp.where(m, a, 0.0) for m, a in zip(mask1, lk_full[e])] for e in heads]
    ls = [[jnp.minimum(a, 0.0) - s for a, s in zip(z[e], soft[e])] for e in heads]
    incl = [[_dot_xr2(a, uincl, NN) for a in lk[e]] for e in heads]
    weights, out_carries = [], []
    for e in heads:
        cb, we = carries[e], []
        for g in groups:
            we.append(jnp.where(mask1[g], jnp.exp(ls[e][g] + (cb + incl[e][g] - lk[e][g])), 0.0))
            cb = cb + incl[e][g][:, 0:1]
        weights.append(we)
        out_carries.append(cb)
    return rows, ks, weights, mask1, lk_full, ls, out_carries


def _sb_more(qi, carry):
    it, cbs = carry[0], carry[1]
    live = jnp.max(cbs[0])
    for cb in cbs[1:]:
        live = jnp.maximum(live, jnp.max(cb))
    return jnp.logical_and(SB_GROUP * it <= qi, live > SB_LOG_ZERO)


def _sb_steps(groups, nq):
    def when():
        h, i = pl.program_id(0), pl.program_id(1)
        return (jnp.logical_and(h == 0, i == 0), jnp.logical_and(h == groups // 2, i == 0),
                jnp.logical_and(h == groups - 1, i == nq - 1))
    return when


def _sb_fwd(qkv, name, comm=None):
    t = qkv.shape[0]
    bk = ATT_BLOCK
    hp, wide = SB_HEADS_FWD, SB_HEADS_FWD * HEAD_DIM
    lane = [slice(e * HEAD_DIM, (e + 1) * HEAD_DIM) for e in range(hp)]

    def body(q_ref, k_ref, v_ref, o_ref):
        qi = pl.program_id(1)
        qs = [q_ref[:, s] for s in lane]
        uincl = jnp.where(_tri(bk, "lower"), 1.0, 0.0).astype(BF16)

        def step(carry):
            it, cbs, accs = carry
            rows, _, weights, _, _, _, cbs = _sb_scores(qs, k_ref, qi, it, cbs, uincl)
            accs = list(accs)
            for e in range(hp):
                for r, a in zip(rows, weights[e]):
                    accs[e] = accs[e] + _dot(a.astype(BF16), v_ref[r, lane[e]], NN)
            return it + 1, tuple(cbs), tuple(accs)

        init = (jnp.int32(0), (jnp.zeros((bk, 1), F32),) * hp, (jnp.zeros((bk, HEAD_DIM), F32),) * hp)
        _, _, accs = lax.while_loop(functools.partial(_sb_more, qi), step, init)
        for e in range(hp):
            o_ref[:, lane[e]] = accs[e]

    groups = HEADS // hp
    (o,), extra = _host_call(
        body, name, comm, _sb_steps(groups, t // bk), [jax.ShapeDtypeStruct((t, WIDTH), F32)], (groups, t // bk),
        [pl.BlockSpec((bk, wide), lambda h, i: (i, h)),
         pl.BlockSpec((t, wide), lambda h, i: (0, groups + h)),
         pl.BlockSpec((t, wide), lambda h, i: (0, 2 * groups + h))],
        [pl.BlockSpec((bk, wide), lambda h, i: (i, h))], [], ("parallel", "arbitrary"), (qkv, qkv, qkv))
    return o, extra


def _sb_bwd(qkv, o, do, name, comm=None):
    assert do.dtype == BF16
    t = qkv.shape[0]
    bk = ATT_BLOCK
    scale = HEAD_DIM ** -0.5
    hp, wide = SB_HEADS_BWD, SB_HEADS_BWD * HEAD_DIM
    lane = [slice(e * HEAD_DIM, (e + 1) * HEAD_DIM) for e in range(hp)]

    def body(q_ref, k_ref, v_ref, o_ref, do_ref, dq_ref, dk_out, dv_out, dk_ref, dv_ref):
        qi = pl.program_id(1)

        @pl.when(qi == 0)
        def _():
            dk_ref[...] = jnp.zeros_like(dk_ref)
            dv_ref[...] = jnp.zeros_like(dv_ref)

        heads, groups = range(hp), range(SB_GROUP)
        qs = [q_ref[:, s] for s in lane]
        dob = [do_ref[:, s] for s in lane]
        dsum = [jnp.sum(dob[e].astype(F32) * o_ref[:, lane[e]], axis=1, keepdims=True) for e in heads]
        uincl = jnp.where(_tri(bk, "lower"), 1.0, 0.0).astype(BF16)

        def step(carry):
            it, cbs, ces, dqs = carry
            rows, ks, weights, mask, lk_full, ls, cbs = _sb_scores(qs, k_ref, qi, it, cbs, uincl)
            ab = [[a.astype(BF16) for a in weights[e]] for e in heads]
            vs = [[v_ref[r, lane[e]] for r in rows] for e in heads]
            dla = [[ab[e][g].astype(F32) * _dot(dob[e], vs[e][g], NT) for g in groups] for e in heads]
            suf = [[_dot_xr2(a, uincl, NN) for a in dla[e]] for e in heads]
            ces, dqs = list(ces), list(dqs)
            for e in heads:
                for g in groups:
                    err = dsum[e] - (ces[e] + suf[e][g])
                    ces[e] = ces[e] + suf[e][g][:, 0:1]
                    dz = jnp.where(mask[g], dla[e][g] * jnp.exp(lk_full[e][g]) - err * jnp.exp(ls[e][g]), 0.0)
                    dzb = (dz * scale).astype(BF16)
                    dqs[e] = dqs[e] + _dot(dzb, ks[e][g], NN)
                    dk_ref[rows[g], lane[e]] += _dot(dzb, qs[e], TN)
                    dv_ref[rows[g], lane[e]] += _dot(ab[e][g], dob[e], TN)
            return it + 1, tuple(cbs), tuple(ces), tuple(dqs)

        zc = (jnp.zeros((bk, 1), F32),) * hp
        init = (jnp.int32(0), zc, zc, (jnp.zeros((bk, HEAD_DIM), F32),) * hp)
        dqs = lax.while_loop(functools.partial(_sb_more, qi), step, init)[3]
        for e in heads:
            dq_ref[:, lane[e]] = dqs[e].astype(BF16)

        @pl.when(qi == t // bk - 1)
        def _():
            dk_out[...] = dk_ref[...].astype(BF16)
            dv_out[...] = dv_ref[...].astype(BF16)

    ngroup = HEADS // hp
    tw = jax.ShapeDtypeStruct((t, WIDTH), BF16)
    qb = pl.BlockSpec((bk, wide), lambda h, i: (i, h))
    full = lambda off: pl.BlockSpec((t, wide), lambda h, i: (0, off + h))
    return _host_call(
        body, name, comm, _sb_steps(ngroup, t // bk), [tw, tw, tw], (ngroup, t // bk),
        [qb, full(ngroup), full(2 * ngroup), qb, qb], [qb, full(0), full(0)],
        [pltpu.VMEM((t, wide), F32), pltpu.VMEM((t, wide), F32)], ("parallel", "arbitrary"),
        (qkv, qkv, qkv, o, do))


def _merge_fwd(pd, ps, gl, name):
    t = pd.shape[0]
    tr, tc = _tile(t, 512), 512
    nj = D_MODEL // tc

    def body(pd_ref, ps_ref, gd_ref, gs_ref, o_ref):
        gd, gs = gd_ref[...].astype(F32), gs_ref[...].astype(F32)
        o_ref[...] = (_sigmoid(gd) * pd_ref[...].astype(F32) + _sigmoid(gs) * ps_ref[...].astype(F32)).astype(BF16)

    blk = lambda off: pl.BlockSpec((tr, tc), lambda i, j: (i, j + off))
    return pl.pallas_call(
        body, name=name,
        out_shape=jax.ShapeDtypeStruct((t, D_MODEL), BF16),
        grid=(t // tr, nj),
        in_specs=[blk(0), blk(0), blk(0), blk(nj)],
        out_specs=blk(0),
        compiler_params=_params(("parallel", "parallel")),
    )(pd, ps, gl, gl)


def _merge_bwd(dm, pd, ps, gl, name):
    t = pd.shape[0]
    tr, tc = _tile(t, 512), 512
    nj = D_MODEL // tc

    def body(dm_ref, pd_ref, ps_ref, gd_ref, gs_ref, dpd_ref, dps_ref, dgd_ref, dgs_ref):
        dmv = dm_ref[...].astype(F32)
        sd, ss = _sigmoid(gd_ref[...].astype(F32)), _sigmoid(gs_ref[...].astype(F32))
        dpd_ref[...] = (dmv * sd).astype(BF16)
        dps_ref[...] = (dmv * ss).astype(BF16)
        dgd_ref[...] = (dmv * pd_ref[...].astype(F32) * sd * (1.0 - sd)).astype(BF16)
        dgs_ref[...] = (dmv * ps_ref[...].astype(F32) * ss * (1.0 - ss)).astype(BF16)

    blk = lambda off: pl.BlockSpec((tr, tc), lambda i, j: (i, j + off))
    out = jax.ShapeDtypeStruct((t, D_MODEL), BF16)
    return pl.pallas_call(
        body, name=name,
        out_shape=(out, out, out, out),
        grid=(t // tr, nj),
        in_specs=[blk(0), blk(0), blk(0), blk(0), blk(nj)],
        out_specs=(blk(0), blk(0), blk(0), blk(0)),
        compiler_params=_params(("parallel", "parallel")),
    )(dm, pd, ps, gl, gl)


def _local_step(x, target, wts, plan=None, n1=None):
    if n1 is None:
        n1 = _rmsnorm_fwd(x, wts["norm1_w"], "norm1_fwd")
    qkv_pre = _matmul(n1, wts["w_dnqkv_t"], "nt", BF16, "in_dnqkv")
    hgate = _matmul(n1, wts["w_dngate_t"], "nt", BF16, "in_dngate")
    sbqkv = _matmul(n1, wts["w_sbqkv_t"], "nt", BF16, "in_sbqkv")
    gl = _matmul(n1, wts["w_gl_t"], "nt", BF16, "in_gl")
    hab = _matmul(n1, wts["w_ab_t"], "nt", F32, "in_ab")

    act = _dn_pre_fwd(qkv_pre, wts["dn_conv_w"], "dn_pre_fwd")
    gates = _dn_gates_fwd(hab, wts["alog"], wts["dtb"], "dn_gates_fwd")
    u, w, kd, qg, tinv, p = _dn_local_fwd(act, gates, "dn_local_fwd")
    o_dn, sh = _dn_scan_fwd(u, w, kd, qg, p, gates, "dn_scan_fwd")
    y_dn = _dn_post_fwd(o_dn, hgate, wts["dn_norm_w"], "dn_post_fwd")

    o_sb, late = _sb_fwd(sbqkv, "sb_fwd", comm=plan.late_gather() if plan else None)
    if plan:
        wts = {**wts, **plan.late_weights(late)}

    pd = _matmul(y_dn, wts["w_proj_dn"], "nn", BF16, "proj_dn")
    ps = _matmul(o_sb, wts["w_proj_sb"], "nn", BF16, "proj_sb")
    mixed = _merge_fwd(pd, ps, gl, "merge_fwd")
    x1 = _matmul(mixed, wts["w_out"], "nn", F32, "out_proj", add=x)

    n2 = _rmsnorm_fwd(x1, wts["norm2_w"], "norm2_fwd")
    upre = _matmul(n2, wts["ffn_w_up_t"], "nt", BF16, "ffn_up")
    fact = _ffn_act_fwd(upre, wts["ffn_conv_w"], "ffn_act_fwd")
    x2 = _matmul(fact, wts["ffn_w_down"], "nn", F32, "ffn_down", add=x1)

    dx2, g_normf, loss = _final_loss(x2, target, wts["norm_f_w"], "final_loss")

    dfact = _matmul(dx2, wts["ffn_w_down"], "nt", BF16, "ffn_down_dx")
    g_wdown = _matmul(fact, dx2, "tn", BF16, "ffn_down_dw")
    dgc, duc, dwg, dwu = _ffn_act_bwd(dfact, upre, wts["ffn_conv_w"], "ffn_act_bwd")
    g_fconv = jnp.concatenate([dwg, dwu], axis=1)
    dupre = _conv_bwd_data([dgc, duc], wts["ffn_conv_w"], FFN_CONV, BF16, "ffn_conv_bwd")
    dn2 = _matmul(dupre, wts["ffn_w_up_t"], "nn", F32, "ffn_up_dx")
    g_wup = _matmul(dupre, n2, "tn", BF16, "ffn_up_dw")
    dx1, g_norm2 = _rmsnorm_bwd(dn2, x1, wts["norm2_w"], dx2, "norm2_bwd")

    dmixed = _matmul(dx1, wts["w_out"], "nt", BF16, "out_proj_dx")
    g_wout = _matmul(mixed, dx1, "tn", BF16, "out_proj_dw")
    dpd, dps, dgd, dgs = _merge_bwd(dmixed, pd, ps, gl, "merge_bwd")
    dy_dn = _matmul(dpd, wts["w_proj_dn"], "nt", BF16, "proj_dn_dx")
    g_wpd = _matmul(y_dn, dpd, "tn", BF16, "proj_dn_dw")
    do_sb = _matmul(dps, wts["w_proj_sb"], "nt", BF16, "proj_sb_dx")
    g_wps = _matmul(o_sb, dps, "tn", BF16, "proj_sb_dw")
    grads = dict(w_proj_dn=g_wpd, w_proj_sb=g_wps, w_out=g_wout, ffn_w_up_t=g_wup, ffn_w_down=g_wdown)

    (dsq, dsk, dsv), got_early = _sb_bwd(sbqkv, o_sb, do_sb, "sb_bwd",
                                         comm=plan.early_grads(grads) if plan else None)

    do_dn, dhgate, g_dnnorm = _dn_post_bwd(dy_dn, o_dn, hgate, wts["dn_norm_w"], "dn_post_bwd")
    dvn, dsh = _dn_scan_bwd(do_dn, w, kd, qg, p, gates, "dn_scan_bwd")
    dq, dk, dv, dgates = _dn_local_bwd(act, gates, u, w, kd, qg, tinv, p, sh, dsh, dvn, do_dn, "dn_local_bwd")
    dhab, g_alog, g_dtb = _dn_gates_bwd(dgates, hab, wts["alog"], wts["dtb"], "dn_gates_bwd")
    dcv, g_dnconv = _dn_pre_bwd(dq, dk, dv, qkv_pre, wts["dn_conv_w"], "dn_pre_bwd")
    dqkv_pre = _conv_bwd_data([dcv], wts["dn_conv_w"], DN_CONV, BF16, "dn_conv_bwd")

    dh = jnp.concatenate([dqkv_pre, dhgate, dsq, dsk, dsv, dgd, dgs], axis=1)
    w_main_t = jnp.concatenate([wts["w_dnqkv_t"], wts["w_dngate_t"], wts["w_sbqkv_t"], wts["w_gl_t"]], axis=0)
    g_wmain = _matmul(dh, n1, "tn", BF16, "in_dw_main")
    g_wab = _matmul(dhab, n1, "tn", BF16, "in_dw_ab")
    grads.update(w_main_t=g_wmain, w_ab_t=g_wab, dn_conv_w=g_dnconv, alog=g_alog, dtb=g_dtb, dn_norm_w=g_dnnorm,
                 norm2_w=g_norm2, ffn_conv_w=g_fconv, norm_f_w=g_normf)
    got_late = []
    if plan:
        dn1, swapped = _matmul(dhab, wts["w_ab_t"], "nn", F32, "in_dx_ab", comm=plan.sibling_swap(grads))
        dn1, got_late = _matmul(dh, w_main_t, "nn", F32, "in_dx_main", add=dn1,
                                comm=plan.late_grads(swapped, grads, loss))
    else:
        dn1 = _matmul(dhab, wts["w_ab_t"], "nn", F32, "in_dx_ab")
        dn1 = _matmul(dh, w_main_t, "nn", F32, "in_dx_main", add=dn1)
    grad_x, g_norm1 = _rmsnorm_bwd(dn1, x, wts["norm1_w"], dx1, "norm1_bwd")
    grads["norm1_w"] = g_norm1
    return loss, grad_x, grads, got_early, got_late


HBM_SPEC = pl.BlockSpec(memory_space=pltpu.HBM)


def _mesh_pos():
    x, y, c = lax.axis_index("x"), lax.axis_index("y"), lax.axis_index("c")
    return x, y, c, 4 * x + 2 * y + c


def _peer(k):
    x, y, c, _ = _mesh_pos()
    px = 1 - x if k & 4 else x
    py = 1 - y if k & 2 else y
    pc = 1 - c if k & 1 else c
    return (px, py, pc), 4 * px + 2 * py + pc


def _rcopy(src, dst, send, recv, a, s, peer):
    return pltpu.make_async_remote_copy(src_ref=src, dst_ref=dst, send_sem=send.at[a, s], recv_sem=recv.at[a, s],
                                        device_id=peer, device_id_type=pl.DeviceIdType.MESH)


class _Gather:
    ICI = (2, 4, 6)

    def __init__(self, shards):
        self.args = list(shards)
        self.n = len(shards)
        self.out_shape = [jax.ShapeDtypeStruct((N_DEV,) + s.shape, s.dtype) for s in shards]
        self.scratch = [pltpu.SemaphoreType.DMA((self.n, N_DEV - 1)), pltpu.SemaphoreType.DMA((self.n, N_DEV - 1)),
                        pltpu.SemaphoreType.DMA((self.n,))]

    def _first(self, ins, outs, send, recv, a):
        me = _mesh_pos()[3]
        out, got = [], []
        for s, k in enumerate((1,) + self.ICI):
            peer, pidx = _peer(k)
            out.append(_rcopy(ins[a], outs[a].at[me], send, recv, a, s, peer))
            got.append(_rcopy(ins[a], outs[a].at[pidx], send, recv, a, s, peer))
        return out, got

    def _forward(self, ins, outs, send, recv, a):
        sib = _peer(1)[0]
        out, got = [], []
        for s, k in enumerate(self.ICI):
            held = outs[a].at[_peer(k)[1]]
            out.append(_rcopy(held, held, send, recv, a, 4 + s, sib))
            other = outs[a].at[_peer(k | 1)[1]]
            got.append(_rcopy(other, other, send, recv, a, 4 + s, sib))
        return out, got

    def start(self, ins, outs, sems):
        send, recv, loc = sems
        me = _mesh_pos()[3]
        for a in range(self.n):
            pltpu.make_async_copy(ins[a], outs[a].at[me], loc.at[a]).start()
            for cp in self._first(ins, outs, send, recv, a)[0]:
                cp.start()

    def mid(self, ins, outs, sems):
        send, recv, _ = sems
        for a in range(self.n):
            arrivals = self._first(ins, outs, send, recv, a)[1]
            for s, cp in enumerate(self._forward(ins, outs, send, recv, a)[0]):
                arrivals[1 + s].wait_recv()
                cp.start()

    def finish(self, ins, outs, sems):
        send, recv, loc = sems
        me = _mesh_pos()[3]
        for a in range(self.n):
            first_out, first_got = self._first(ins, outs, send, recv, a)
            fwd_out, fwd_got = self._forward(ins, outs, send, recv, a)
            first_got[0].wait_recv()
            for cp in fwd_got:
                cp.wait_recv()
            for cp in first_out + fwd_out:
                cp.wait_send()
            pltpu.make_async_copy(ins[a], outs[a].at[me], loc.at[a]).wait()


class _Exchange:
    def __init__(self, slabs=(), gathered=(), chip_slabs=(), sibling_slabs=()):
        self.args = list(slabs) + list(chip_slabs) + list(sibling_slabs) + list(gathered)
        self.kind = (["dev"] * len(slabs) + ["chip"] * len(chip_slabs) + ["sib"] * len(sibling_slabs)
                     + ["all"] * len(gathered))
        self.n = len(self.args)
        half = lambda s: jax.ShapeDtypeStruct((N_DEV // 2,) + s.shape[1:], s.dtype)
        self.out_shape = ([jax.ShapeDtypeStruct(s.shape, s.dtype) for s in slabs]
                          + [half(s) for s in chip_slabs] + [half(s) for s in sibling_slabs]
                          + [jax.ShapeDtypeStruct((N_DEV,) + s.shape, s.dtype) for s in gathered])
        self.scratch = [pltpu.SemaphoreType.DMA((self.n, N_DEV - 1)), pltpu.SemaphoreType.DMA((self.n, N_DEV - 1)),
                        pltpu.SemaphoreType.DMA((self.n,))]

    def _copies(self, ins, outs, send, recv, a):
        x, y, c, me = _mesh_pos()
        kind = self.kind[a]
        out, got = [], []
        if kind == "sib":
            sib = _peer(1)[0]
            for q in range(N_DEV // 2):
                out.append(_rcopy(ins[a].at[2 * q + 1 - c], outs[a].at[q], send, recv, a, q, sib))
                got.append(_rcopy(ins[a].at[2 * q + c], outs[a].at[q], send, recv, a, q, sib))
            return out, got
        for k in ((2, 4, 6) if kind == "chip" else range(1, N_DEV)):
            peer, pidx = _peer(k)
            if kind == "chip":
                src, mine, theirs = ins[a].at[2 * peer[0] + peer[1]], 2 * x + y, 2 * peer[0] + peer[1]
            else:
                src, mine, theirs = (ins[a].at[pidx] if kind == "dev" else ins[a]), me, pidx
            out.append(_rcopy(src, outs[a].at[mine], send, recv, a, k - 1, peer))
            got.append(_rcopy(src, outs[a].at[theirs], send, recv, a, k - 1, peer))
        return out, got

    def _local(self, ins, outs, loc, a):
        x, y, _, me = _mesh_pos()
        kind = self.kind[a]
        if kind == "sib":
            return None
        if kind == "chip":
            return pltpu.make_async_copy(ins[a].at[2 * x + y], outs[a].at[2 * x + y], loc.at[a])
        return pltpu.make_async_copy(ins[a].at[me] if kind == "dev" else ins[a], outs[a].at[me], loc.at[a])

    def start(self, ins, outs, sems):
        send, recv, loc = sems
        for a in range(self.n):
            if self._local(ins, outs, loc, a) is not None:
                self._local(ins, outs, loc, a).start()
            for cp in self._copies(ins, outs, send, recv, a)[0]:
                cp.start()

    def mid(self, ins, outs, sems):
        pass

    def finish(self, ins, outs, sems):
        send, recv, loc = sems
        for a in range(self.n):
            out, got = self._copies(ins, outs, send, recv, a)
            for cp in got:
                cp.wait_recv()
            for cp in out:
                cp.wait_send()
            if self._local(ins, outs, loc, a) is not None:
                self._local(ins, outs, loc, a).wait()


def _comm_call(comm, name):
    n = comm.n

    def body(*refs):
        ins, outs, sems = refs[:n], refs[n:2 * n], refs[2 * n:]
        comm.start(ins, outs, sems)
        comm.mid(ins, outs, sems)
        comm.finish(ins, outs, sems)

    return pl.pallas_call(
        body, name=name, out_shape=comm.out_shape, in_specs=[HBM_SPEC] * n, out_specs=[HBM_SPEC] * n,
        scratch_shapes=comm.scratch,
    )(*comm.args)


def _hosted(body, comm, n_in, n_out, when):
    if comm is None:
        return body

    def wrapped(*refs):
        ins, c_ins = refs[:n_in], refs[n_in:n_in + comm.n]
        o0 = n_in + comm.n
        outs, c_outs = refs[o0:o0 + n_out], refs[o0 + n_out:o0 + n_out + comm.n]
        scratch, sems = refs[o0 + n_out + comm.n:len(refs) - 3], refs[len(refs) - 3:]
        first, middle, last = when()

        @pl.when(first)
        def _():
            comm.start(c_ins, c_outs, sems)

        body(*ins, *outs, *scratch)

        @pl.when(middle)
        def _():
            comm.mid(c_ins, c_outs, sems)

        @pl.when(last)
        def _():
            comm.finish(c_ins, c_outs, sems)

    return wrapped


def _host_call(body, name, comm, when, out_shape, grid, in_specs, out_specs, scratch_shapes, sem, args):
    n_in, n_out = len(in_specs), len(out_specs)
    if comm is None:
        res = pl.pallas_call(body, name=name, out_shape=out_shape, grid=grid, in_specs=in_specs, out_specs=out_specs,
                             scratch_shapes=scratch_shapes, compiler_params=_params(sem))(*args)
        return list(res), []
    res = pl.pallas_call(
        _hosted(body, comm, n_in, n_out, when), name=name,
        out_shape=list(out_shape) + comm.out_shape, grid=grid,
        in_specs=list(in_specs) + [HBM_SPEC] * comm.n, out_specs=list(out_specs) + [HBM_SPEC] * comm.n,
        scratch_shapes=list(scratch_shapes) + comm.scratch,
        compiler_params=_params(("arbitrary",) * len(grid)),
    )(*args, *comm.args)
    return list(res[:n_out]), list(res[n_out:])


def _add_my_slabs(slabs, b, name):
    n, rows, cols = b.shape
    tc = _tile(cols, 256)

    def body(a_ref, b_ref, o_ref):
        o_ref[...] = (a_ref[...].astype(F32) + b_ref[...].astype(F32)).astype(o_ref.dtype)

    blk = pl.BlockSpec((None, rows, tc), lambda i, j: (i, 0, j))
    mine = pl.BlockSpec((None, rows, tc), lambda i, j: (2 * i + lax.axis_index("c"), 0, j))
    return pl.pallas_call(
        body, name=name, out_shape=jax.ShapeDtypeStruct(b.shape, b.dtype), grid=(n, cols // tc),
        in_specs=[mine, blk], out_specs=blk, compiler_params=_params(("parallel", "parallel")),
    )(slabs, b)


def _adamw(parts, w, m, v, name):
    rows, cols = w.shape
    nparts = parts.shape[0]
    tr, tc = rows, cols
    for cand in (128, 176):
        if rows > cand and rows % cand == 0:
            tr = cand
            break
    if tr == rows and rows > 512:
        tc = _tile(cols, 256)

    def body(p_ref, w_ref, m_ref, v_ref, g_ref, d_ref, mo_ref, vo_ref):
        g = p_ref[0].astype(F32)
        for s in range(1, nparts):
            g = g + p_ref[s].astype(F32)
        mn = ADAM_B1 * m_ref[...] + (1.0 - ADAM_B1) * g
        vn = ADAM_B2 * v_ref[...] + (1.0 - ADAM_B2) * (g * g)
        m_hat = mn / (1.0 - ADAM_B1 ** ADAM_STEP)
        v_hat = vn / (1.0 - ADAM_B2 ** ADAM_STEP)
        g_ref[...] = g
        d_ref[...] = -ADAM_LR * (m_hat / (jnp.sqrt(v_hat) + ADAM_EPS) + ADAM_WD * w_ref[...])
        mo_ref[...] = mn
        vo_ref[...] = vn

    blk = pl.BlockSpec((tr, tc), lambda i, j: (i, j))
    out = jax.ShapeDtypeStruct((rows, cols), F32)
    return pl.pallas_call(
        body, name=name,
        out_shape=(out, out, out, out),
        grid=(rows // tr, cols // tc),
        in_specs=[pl.BlockSpec((nparts, tr, tc), lambda i, j: (0, i, j)), blk, blk, blk],
        out_specs=(blk, blk, blk, blk),
        compiler_params=_params(("parallel", "parallel")),
    )(parts, w, m, v)


CONV_PACK = 8 * 1024
WEIGHT_ORDER = ("norm1_w", "w_in", "dn_conv_w", "dn_A_log", "dn_dt_bias", "dn_norm_w", "w_proj_dn", "w_proj_sb",
                "w_out", "norm2_w", "ffn_w_up", "ffn_conv_w", "ffn_w_down", "norm_f_w")


def _cols_to_slabs(g):
    r, c8 = g.shape
    return g.reshape(r, N_DEV, c8 // N_DEV).transpose(1, 0, 2)


def _slabs_to_cols(s):
    d, r, c = s.shape
    return s.transpose(1, 0, 2).reshape(r, d * c)


def kernel(x, norm1_w, w_in, dn_conv_w, dn_A_log, dn_dt_bias, dn_norm_w, w_proj_dn, w_proj_sb, w_out, norm2_w, ffn_w_up, ffn_conv_w, ffn_w_down, norm_f_w, loss_target, m_norm1_w, m_w_in, m_dn_conv_w, m_dn_A_log, m_dn_dt_bias, m_dn_norm_w, m_w_proj_dn, m_w_proj_sb, m_w_out, m_norm2_w, m_ffn_w_up, m_ffn_conv_w, m_ffn_w_down, m_norm_f_w, v_norm1_w, v_w_in, v_dn_conv_w, v_dn_A_log, v_dn_dt_bias, v_dn_norm_w, v_w_proj_dn, v_w_proj_sb, v_w_out, v_norm2_w, v_ffn_w_up, v_ffn_conv_w, v_ffn_w_down, v_norm_f_w):
    me = _mesh_pos()[3]
    tr = lambda a: jnp.transpose(a[0])
    w_loc = dict(norm1_w=norm1_w, w_in=tr(w_in), dn_conv_w=dn_conv_w[0], dn_A_log=dn_A_log, dn_dt_bias=dn_dt_bias,
                 dn_norm_w=dn_norm_w, w_proj_dn=w_proj_dn[0], w_proj_sb=w_proj_sb[0], w_out=w_out[0],
                 norm2_w=norm2_w, ffn_w_up=tr(ffn_w_up), ffn_conv_w=ffn_conv_w[0], ffn_w_down=ffn_w_down[0],
                 norm_f_w=norm_f_w[None, :])
    m_loc = dict(norm1_w=m_norm1_w, w_in=tr(m_w_in), dn_conv_w=m_dn_conv_w[0], dn_A_log=m_dn_A_log,
                 dn_dt_bias=m_dn_dt_bias, dn_norm_w=m_dn_norm_w, w_proj_dn=m_w_proj_dn[0], w_proj_sb=m_w_proj_sb[0],
                 w_out=m_w_out[0], norm2_w=m_norm2_w, ffn_w_up=tr(m_ffn_w_up), ffn_conv_w=m_ffn_conv_w[0],
                 ffn_w_down=m_ffn_w_down[0], norm_f_w=m_norm_f_w[None, :])
    v_loc = dict(norm1_w=v_norm1_w, w_in=tr(v_w_in), dn_conv_w=v_dn_conv_w[0], dn_A_log=v_dn_A_log,
                 dn_dt_bias=v_dn_dt_bias, dn_norm_w=v_dn_norm_w, w_proj_dn=v_w_proj_dn[0], w_proj_sb=v_w_proj_sb[0],
                 w_out=v_w_out[0], norm2_w=v_norm2_w, ffn_w_up=tr(v_ffn_w_up), ffn_conv_w=v_ffn_conv_w[0],
                 ffn_w_down=v_ffn_w_down[0], norm_f_w=v_norm_f_w[None, :])

    conv_flat = jnp.concatenate([w_loc["dn_conv_w"].reshape(-1), w_loc["ffn_conv_w"].reshape(-1)])
    n_dn, n_ffn = DN_CONV * 3 * WIDTH // N_DEV, FFN_CONV * 2 * D_FF // N_DEV
    conv_pack = jnp.pad(conv_flat, (0, CONV_PACK - n_dn - n_ffn)).reshape(8, 1024)
    n1, (g_in, g_conv) = _rmsnorm_fwd(x[0], norm1_w, "norm1_fwd",
                                      comm=_Gather([w_loc["w_in"].astype(BF16), conv_pack]))
    in_width = g_in.shape[0] * g_in.shape[1]
    w_in_t = g_in.reshape(in_width, D_MODEL)
    g_conv = g_conv.reshape(N_DEV, CONV_PACK)
    dn_conv_full = _slabs_to_cols(g_conv[:, :n_dn].reshape(N_DEV, DN_CONV, 3 * WIDTH // N_DEV))
    ffn_conv_full = _slabs_to_cols(g_conv[:, n_dn:n_dn + n_ffn].reshape(N_DEV, FFN_CONV, 2 * D_FF // N_DEV))
    q_end = 3 * WIDTH
    ab_end = q_end + 2 * HEADS
    gate_end = ab_end + WIDTH
    sb_end = gate_end + 3 * WIDTH
    pad_lanes = lambda a: jnp.pad(a, ((0, 0), (0, 128 - a.shape[1])))
    wts = dict(
        norm1_w=norm1_w, w_dnqkv_t=w_in_t[:q_end], w_ab_t=jnp.pad(w_in_t[q_end:ab_end], ((0, 128 - 2 * HEADS), (0, 0))),
        w_dngate_t=w_in_t[ab_end:gate_end], w_sbqkv_t=w_in_t[gate_end:sb_end], w_gl_t=w_in_t[sb_end:],
        dn_conv_w=dn_conv_full, alog=pad_lanes(dn_A_log), dtb=pad_lanes(dn_dt_bias), dn_norm_w=dn_norm_w,
        norm2_w=norm2_w, ffn_conv_w=ffn_conv_full, norm_f_w=norm_f_w[None, :])

    n_fc = FFN_CONV * 2 * D_FF
    fc_rows = -(-n_fc // D_MODEL)
    dn_rows = DN_CONV * 3 * WIDTH // D_MODEL
    late_names = ("w_proj_dn", "w_proj_sb", "w_out", "ffn_w_up", "ffn_w_down")

    class Plan:
        @staticmethod
        def late_gather():
            return _Gather([w_loc[k].astype(BF16) for k in late_names])

        @staticmethod
        def late_weights(got):
            g_pd, g_ps, g_out, g_up, g_down = got
            return dict(w_proj_dn=g_pd.reshape(WIDTH, D_MODEL), w_proj_sb=g_ps.reshape(WIDTH, D_MODEL),
                        w_out=g_out.reshape(D_MODEL, D_MODEL), ffn_w_up_t=g_up.reshape(2 * D_FF, D_MODEL),
                        ffn_w_down=g_down.reshape(D_FF, D_MODEL))

        @staticmethod
        def early_grads(g):
            return _Exchange([g["w_proj_dn"].reshape(N_DEV, WIDTH // N_DEV, D_MODEL),
                              g["w_proj_sb"].reshape(N_DEV, WIDTH // N_DEV, D_MODEL),
                              g["w_out"].reshape(N_DEV, D_MODEL // N_DEV, D_MODEL),
                              g["ffn_w_up_t"].reshape(N_DEV, 2 * D_FF // N_DEV, D_MODEL),
                              g["ffn_w_down"].reshape(N_DEV, D_FF // N_DEV, D_MODEL)])

        @staticmethod
        def _in_slabs(g):
            g_win_t = jnp.concatenate([g["w_main_t"][:q_end], g["w_ab_t"][:2 * HEADS], g["w_main_t"][q_end:]],
                                      axis=0)
            return g_win_t.reshape(N_DEV, in_width // N_DEV, D_MODEL)

        @staticmethod
        def sibling_swap(g):
            return _Exchange(sibling_slabs=[Plan._in_slabs(g)])

        @staticmethod
        def late_grads(swapped, g, loss):
            chip_sums = _add_my_slabs(Plan._in_slabs(g), swapped[0], "in_dw_chip_sum")
            row3 = jnp.concatenate([g["dn_norm_w"], g["alog"], g["dtb"], jnp.pad(loss, ((0, 0), (0, 127))),
                                    jnp.zeros((1, D_MODEL - 512), F32)], axis=1)
            fconv_rows = jnp.pad(g["ffn_conv_w"].reshape(-1), (0, fc_rows * D_MODEL - n_fc)).reshape(fc_rows, D_MODEL)
            pad8 = lambda a: jnp.pad(a, ((0, -a.shape[0] % 8), (0, 0)))
            pieces = [g["norm2_w"], g["norm_f_w"], row3, g["dn_conv_w"].reshape(dn_rows, D_MODEL), fconv_rows]
            small = jnp.concatenate([pad8(a) for a in pieces], axis=0)
            assert small.shape[0] == SMALL_ROWS
            return _Exchange(chip_slabs=[chip_sums], gathered=[small])

    loss, grad_x, g, got_early, got_late = _local_step(x[0], loss_target[0], wts, Plan, n1)
    r_pd, r_ps, r_out, r_up, r_down = got_early
    r_in, r_small = got_late
    (r_norm1,) = _comm_call(_Exchange([], [jnp.pad(g["norm1_w"], ((0, 7), (0, 0)))]), "gather_norm1")

    parts = dict(w_in=r_in, w_proj_dn=r_pd, w_proj_sb=r_ps, w_out=r_out, ffn_w_up=r_up, ffn_w_down=r_down)
    parts["norm1_w"] = r_norm1[:, 0:1, :]
    parts["norm2_w"] = r_small[:, 0:1, :]
    parts["norm_f_w"] = r_small[:, 8:9, :]
    parts["dn_norm_w"] = r_small[:, 16:17, 0:HEAD_DIM]
    parts["dn_A_log"] = r_small[:, 16:17, 128:128 + HEADS]
    parts["dn_dt_bias"] = r_small[:, 16:17, 256:256 + HEADS]
    dnc = r_small[:, 24:24 + dn_rows, :].reshape(N_DEV, DN_CONV, 3 * WIDTH)
    parts["dn_conv_w"] = lax.dynamic_slice_in_dim(dnc, me * (3 * WIDTH // N_DEV), 3 * WIDTH // N_DEV, axis=2)
    fc0 = 24 + dn_rows + (-dn_rows % 8)
    fcc = r_small[:, fc0:fc0 + fc_rows, :].reshape(N_DEV, fc_rows * D_MODEL)[:, :n_fc]
    fcc = fcc.reshape(N_DEV, FFN_CONV, 2 * D_FF)
    parts["ffn_conv_w"] = lax.dynamic_slice_in_dim(fcc, me * (2 * D_FF // N_DEV), 2 * D_FF // N_DEV, axis=2)
    loss_total = jnp.sum(r_small[:, 16, 384])

    res = {k: _adamw(parts[k], w_loc[k], m_loc[k], v_loc[k], "adamw_" + k) for k in WEIGHT_ORDER}
    lead = ("w_in", "dn_conv_w", "w_proj_dn", "w_proj_sb", "w_out", "ffn_w_up", "ffn_conv_w", "ffn_w_down")

    def shaped(k, a):
        if k in ("w_in", "ffn_w_up"):
            return jnp.transpose(a)[None]
        if k in lead:
            return a[None]
        if k == "norm_f_w":
            return a[0]
        return a

    outs = [loss_total, grad_x[None]]
    for idx in range(4):
        outs += [shaped(k, res[k][idx]) for k in WEIGHT_ORDER]
    return tuple(outs)
```

```python
import functools

import jax
import jax.numpy as jnp
from jax import lax
from jax.experimental import pallas as pl
from jax.experimental.pallas import tpu as pltpu

F32 = jnp.float32
BF16 = jnp.bfloat16

N_DEV = 8
D_MODEL = 1024
HEADS = 8
HEAD_DIM = 128
WIDTH = HEADS * HEAD_DIM
DN_CONV = 4
DN_CHUNK = 64
D_FF = 2816
FFN_CONV = 3
EPS = 1e-6
HALO = 16
CHUNK_ROWS = 256
ATT_BLOCK = 256
SB_LOG_ZERO = -104.0
SB_GROUP = 2
SB_HEADS_FWD = 4
SB_HEADS_BWD = 2
SMALL_ROWS = 64

ADAM_LR = 0.001
ADAM_B1 = 0.9
ADAM_B2 = 0.999
ADAM_EPS = 1e-08
ADAM_WD = 0.01
ADAM_STEP = 10

VMEM_LIMIT = 48 * 1024 * 1024


def _params(sem=None, **kw):
    return pltpu.CompilerParams(dimension_semantics=sem, vmem_limit_bytes=VMEM_LIMIT, **kw)


def _tile(n, cap):
    if n <= cap:
        return n
    best = None
    for t in range(128, cap + 1, 128):
        if n % t == 0:
            best = t
    assert best is not None, (n, cap)
    return best


def _dot(a, b, dims):
    return lax.dot_general(a, b, ((dims[0], dims[1]), ((), ())), preferred_element_type=F32)


NN = ((1,), (0,))
NT = ((1,), (1,))
TN = ((0,), (0,))


def _dotb(a, b, dims):
    return _dot(a.astype(BF16), b.astype(BF16), dims)


def _split3(x):
    h1 = x.astype(BF16)
    r1 = x - h1.astype(F32)
    h2 = r1.astype(BF16)
    r2 = r1 - h2.astype(F32)
    return h1, h2, r2.astype(BF16)


def _dot_xr(a, b_exact, dims):
    a1, a2, a3 = _split3(a)
    return _dot(a1, b_exact, dims) + _dot(a2, b_exact, dims) + _dot(a3, b_exact, dims)


def _split2(x):
    h1 = x.astype(BF16)
    return h1, (x - h1.astype(F32)).astype(BF16)


def _dot_xr2(a, b_exact, dims):
    a1, a2 = _split2(a)
    return _dot(a1, b_exact, dims) + _dot(a2, b_exact, dims)


def _dot_xl(a_exact, b, dims):
    b1, b2, b3 = _split3(b)
    return _dot(a_exact, b1, dims) + _dot(a_exact, b2, dims) + _dot(a_exact, b3, dims)


def _dot3(a, b, dims):
    a1 = a.astype(BF16)
    a2 = (a - a1.astype(F32)).astype(BF16)
    b1 = b.astype(BF16)
    b2 = (b - b1.astype(F32)).astype(BF16)
    return _dot(a1, b1, dims) + (_dot(a1, b2, dims) + _dot(a2, b1, dims))


def _sigmoid(x):
    return 1.0 / (1.0 + jnp.exp(-x))


def _log1pexp_neg_abs(x):
    return jnp.log(1.0 + jnp.exp(-jnp.abs(x)))


def _iota(shape, dim):
    return lax.broadcasted_iota(jnp.int32, shape, dim)


def _matmul(a, b, mode, out_dtype, name, add=None, comm=None):
    if mode == "nn":
        (m, k), (k2, n) = a.shape, b.shape
    elif mode == "nt":
        (m, k), (n, k2) = a.shape, b.shape
    else:
        (k, m), (k2, n) = a.shape, b.shape
    assert k == k2, (a.shape, b.shape, mode)
    tm, tn, tk = _tile(m, 1408), _tile(n, 1408), _tile(k, 1536)
    nk = k // tk
    dims = {"nn": NN, "nt": NT, "tn": TN}[mode]

    def body(*refs):
        if add is None:
            a_ref, b_ref, o_ref, acc_ref = refs
        else:
            a_ref, b_ref, add_ref, o_ref, acc_ref = refs
        kk = pl.program_id(2)

        @pl.when(kk == 0)
        def _():
            acc_ref[...] = jnp.zeros_like(acc_ref)

        acc_ref[...] += _dotb(a_ref[...], b_ref[...], dims)

        @pl.when(kk == nk - 1)
        def _():
            r = acc_ref[...]
            if add is not None:
                r = r + add_ref[...].astype(F32)
            o_ref[...] = r.astype(out_dtype)

    if mode == "nn":
        specs = [pl.BlockSpec((tm, tk), lambda i, j, l: (i, l)), pl.BlockSpec((tk, tn), lambda i, j, l: (l, j))]
    elif mode == "nt":
        specs = [pl.BlockSpec((tm, tk), lambda i, j, l: (i, l)), pl.BlockSpec((tn, tk), lambda i, j, l: (j, l))]
    else:
        specs = [pl.BlockSpec((tk, tm), lambda i, j, l: (l, i)), pl.BlockSpec((tk, tn), lambda i, j, l: (l, j))]
    args = [a, b]
    if add is not None:
        specs.append(pl.BlockSpec((tm, tn), lambda i, j, l: (i, j)))
        args.append(add)
    grid = (m // tm, n // tn, nk)

    def when():
        i, j, l = pl.program_id(0), pl.program_id(1), pl.program_id(2)
        first = jnp.logical_and(jnp.logical_and(i == 0, j == 0), l == 0)
        last = jnp.logical_and(jnp.logical_and(i == grid[0] - 1, j == grid[1] - 1), l == nk - 1)
        return first, last, last

    (out,), extra = _host_call(
        body, name, comm, when, [jax.ShapeDtypeStruct((m, n), out_dtype)], grid, specs,
        [pl.BlockSpec((tm, tn), lambda i, j, l: (i, j))], [pltpu.VMEM((tm, tn), F32)],
        ("parallel", "parallel", "arbitrary"), args)
    return out if comm is None else (out, extra)


def _rmsnorm_fwd(x, w, name, comm=None):
    t, d = x.shape
    tr = _tile(t, 512)
    steps = t // tr

    def body(x_ref, w_ref, o_ref):
        xv = x_ref[...]
        r = lax.rsqrt(jnp.mean(xv * xv, axis=1, keepdims=True) + EPS)
        o_ref[...] = (xv * r * w_ref[...]).astype(BF16)

    def when():
        i = pl.program_id(0)
        return i == 0, i == steps // 2, i == steps - 1

    (out,), extra = _host_call(
        body, name, comm, when, [jax.ShapeDtypeStruct((t, d), BF16)], (steps,),
        [pl.BlockSpec((tr, d), lambda i: (i, 0)), pl.BlockSpec((1, d), lambda i: (0, 0))],
        [pl.BlockSpec((tr, d), lambda i: (i, 0))], [], ("parallel",), (x, w))
    return out if comm is None else (out, extra)


def _rmsnorm_bwd(dn, x, w, dres, name):
    t, d = x.shape
    tr = _tile(t, 512)

    def body(dn_ref, x_ref, w_ref, dres_ref, dx_ref, dw_ref):
        i = pl.program_id(0)
        xv = x_ref[...]
        g = dn_ref[...].astype(F32)
        r = lax.rsqrt(jnp.mean(xv * xv, axis=1, keepdims=True) + EPS)
        xh = xv * r
        dxh = g * w_ref[...]
        dx = r * (dxh - xh * jnp.mean(dxh * xh, axis=1, keepdims=True))
        dx_ref[...] = dres_ref[...] + dx

        @pl.when(i == 0)
        def _():
            dw_ref[...] = jnp.zeros_like(dw_ref)

        dw_ref[...] += jnp.sum(g * xh, axis=0, keepdims=True)

    return pl.pallas_call(
        body, name=name,
        out_shape=(jax.ShapeDtypeStruct((t, d), F32), jax.ShapeDtypeStruct((1, d), F32)),
        grid=(t // tr,),
        in_specs=[pl.BlockSpec((tr, d), lambda i: (i, 0)), pl.BlockSpec((tr, d), lambda i: (i, 0)),
                  pl.BlockSpec((1, d), lambda i: (0, 0)), pl.BlockSpec((tr, d), lambda i: (i, 0))],
        out_specs=(pl.BlockSpec((tr, d), lambda i: (i, 0)), pl.BlockSpec((1, d), lambda i: (0, 0))),
        compiler_params=_params(("arbitrary",)),
    )(dn, x, w, dres)


def _final_loss(x2, target, w, name):
    t, d = x2.shape
    tr = _tile(t, 512)

    def body(x_ref, t_ref, w_ref, dx_ref, dw_ref, loss_ref):
        i = pl.program_id(0)
        xv = x_ref[...]
        r = lax.rsqrt(jnp.mean(xv * xv, axis=1, keepdims=True) + EPS)
        xh = xv * r
        err = xh * w_ref[...] - t_ref[...]
        dy = err * (1.0 / d)
        dxh = dy * w_ref[...]
        dx_ref[...] = r * (dxh - xh * jnp.mean(dxh * xh, axis=1, keepdims=True))

        @pl.when(i == 0)
        def _():
            dw_ref[...] = jnp.zeros_like(dw_ref)
            loss_ref[...] = jnp.zeros_like(loss_ref)

        dw_ref[...] += jnp.sum(dy * xh, axis=0, keepdims=True)
        row = jnp.sum(err * err, axis=1, keepdims=True) * (0.5 / d)
        loss_ref[...] += jnp.sum(row, axis=0, keepdims=True)

    return pl.pallas_call(
        body, name=name,
        out_shape=(jax.ShapeDtypeStruct((t, d), F32), jax.ShapeDtypeStruct((1, d), F32),
                   jax.ShapeDtypeStruct((1, 1), F32)),
        grid=(t // tr,),
        in_specs=[pl.BlockSpec((tr, d), lambda i: (i, 0)), pl.BlockSpec((tr, d), lambda i: (i, 0)),
                  pl.BlockSpec((1, d), lambda i: (0, 0))],
        out_specs=(pl.BlockSpec((tr, d), lambda i: (i, 0)), pl.BlockSpec((1, d), lambda i: (0, 0)),
                   pl.BlockSpec((1, 1), lambda i: (0, 0))),
        compiler_params=_params(("arbitrary",)),
    )(x2, target, w)


def _shift_down(cur, prev, k, row):
    r = pltpu.roll(cur, k, 0)
    top, row8 = r[0:8, :], row[0:8, :]
    for m in range(k):
        top = jnp.where(row8 == m, prev[HALO - k + m:HALO - k + m + 1, :], top)
    return jnp.concatenate([top, r[8:, :]], axis=0)


def _shift_up(cur, nxt, k, row, tr):
    r = pltpu.roll(cur, tr - k, 0)
    bottom, row8 = r[tr - 8:, :], row[0:8, :]
    for m in range(k):
        bottom = jnp.where(row8 == 8 - k + m, nxt[m:m + 1, :], bottom)
    return jnp.concatenate([r[:tr - 8, :], bottom], axis=0)


def _fold8(a):
    out = a[0:8, :]
    for r in range(8, a.shape[0], 8):
        out = out + a[r:r + 8, :]
    return out


def _conv_taps(cur, prev, w, ntaps, row):
    taps = [cur if i == ntaps - 1 else _shift_down(cur, prev, ntaps - 1 - i, row) for i in range(ntaps)]
    y = w[0:1, :] * taps[0]
    for i in range(1, ntaps):
        y = y + w[i:i + 1, :] * taps[i]
    return taps, y


def _conv_bwd_data(parts, w, ntaps, out_dtype, name):
    t, chp = parts[0].shape
    npart = len(parts)
    tr, tc = _tile(t, 512), _tile(chp, 1408)
    nc = chp // tc
    nhalo = t // HALO
    last = t // tr - 1

    def body(*refs):
        cur_refs, nxt_refs = refs[:npart], refs[npart:2 * npart]
        w_ref, o_ref = refs[2 * npart], refs[2 * npart + 1]
        i, j = pl.program_id(0), pl.program_id(1)
        row = _iota((tr, 128), 0)
        for c0 in range(0, tc, 128):
            sl = slice(c0, c0 + 128)
            cur, nxt = cur_refs[0][:, sl].astype(F32), nxt_refs[0][:, sl].astype(F32)
            for p in range(1, npart):
                cur = jnp.where(j >= p * nc, cur_refs[p][:, sl].astype(F32), cur)
                nxt = jnp.where(j >= p * nc, nxt_refs[p][:, sl].astype(F32), nxt)
            nxt = jnp.where(i == last, 0.0, nxt)
            wv = w_ref[:, sl]
            y = wv[ntaps - 1:ntaps, :] * cur
            for k in range(1, ntaps):
                y = y + wv[ntaps - 1 - k:ntaps - k, :] * _shift_up(cur, nxt, k, row, tr)
            o_ref[:, sl] = y.astype(out_dtype)

    col = lambda p: (lambda j: jnp.clip(j - p * nc, 0, nc - 1))
    cur_specs = [pl.BlockSpec((tr, tc), lambda i, j, c=col(p): (i, c(j))) for p in range(npart)]
    nxt_specs = [pl.BlockSpec((HALO, tc),
                              lambda i, j, c=col(p): (jnp.minimum((i + 1) * (tr // HALO), nhalo - 1), c(j)))
                 for p in range(npart)]
    return pl.pallas_call(
        body, name=name,
        out_shape=jax.ShapeDtypeStruct((t, npart * chp), out_dtype),
        grid=(t // tr, npart * nc),
        in_specs=cur_specs + nxt_specs + [pl.BlockSpec((ntaps, tc), lambda i, j: (0, j))],
        out_specs=pl.BlockSpec((tr, tc), lambda i, j: (i, j)),
        compiler_params=_params(("parallel", "parallel")),
    )(*parts, *parts, w)


def _ffn_act_fwd(upre, cw, name):
    t = upre.shape[0]
    tr, tc = _tile(t, 512), _tile(D_FF, 1408)
    nj = D_FF // tc

    def body(g_ref, gp_ref, u_ref, up_ref, wg_ref, wu_ref, o_ref):
        i = pl.program_id(0)
        row = _iota((tr, 128), 0)
        for c0 in range(0, tc, 128):
            sl = slice(c0, c0 + 128)
            gp = jnp.where(i == 0, 0.0, gp_ref[:, sl].astype(F32))
            up = jnp.where(i == 0, 0.0, up_ref[:, sl].astype(F32))
            _, gc = _conv_taps(g_ref[:, sl].astype(F32), gp, wg_ref[:, sl], FFN_CONV, row)
            _, uc = _conv_taps(u_ref[:, sl].astype(F32), up, wu_ref[:, sl], FFN_CONV, row)
            o_ref[:, sl] = (gc * _sigmoid(gc) * uc).astype(BF16)

    prev = lambda off: (lambda i, j: (jnp.maximum(i * (tr // HALO) - 1, 0), j + off))
    return pl.pallas_call(
        body, name=name,
        out_shape=jax.ShapeDtypeStruct((t, D_FF), BF16),
        grid=(t // tr, nj),
        in_specs=[pl.BlockSpec((tr, tc), lambda i, j: (i, j)), pl.BlockSpec((HALO, tc), prev(0)),
                  pl.BlockSpec((tr, tc), lambda i, j: (i, j + nj)), pl.BlockSpec((HALO, tc), prev(nj)),
                  pl.BlockSpec((FFN_CONV, tc), lambda i, j: (0, j)),
                  pl.BlockSpec((FFN_CONV, tc), lambda i, j: (0, j + nj))],
        out_specs=pl.BlockSpec((tr, tc), lambda i, j: (i, j)),
        compiler_params=_params(("parallel", "parallel")),
    )(upre, upre, upre, upre, cw, cw)


def _ffn_act_bwd(dact, upre, cw, name):
    t = upre.shape[0]
    tr, tc = _tile(t, 256), _tile(D_FF, 1408)
    nj = D_FF // tc

    def body(da_ref, g_ref, gp_ref, u_ref, up_ref, wg_ref, wu_ref, dg_ref, du_ref, dwg_ref, dwu_ref):
        i = pl.program_id(1)
        row = _iota((CHUNK_ROWS, 128), 0)

        @pl.when(i == 0)
        def _():
            dwg_ref[...] = jnp.zeros_like(dwg_ref)
            dwu_ref[...] = jnp.zeros_like(dwu_ref)

        for c0 in range(0, tc, 128):
            sl = slice(c0, c0 + 128)
            wg, wu = wg_ref[:, sl], wu_ref[:, sl]
            dwg = [jnp.zeros((8, 128), F32)] * FFN_CONV
            dwu = [jnp.zeros((8, 128), F32)] * FFN_CONV
            for r0 in range(0, tr, CHUNK_ROWS):
                rows = slice(r0, r0 + CHUNK_ROWS)
                if r0 == 0:
                    gp = jnp.where(i == 0, 0.0, gp_ref[:, sl].astype(F32))
                    up = jnp.where(i == 0, 0.0, up_ref[:, sl].astype(F32))
                else:
                    gp = g_ref[r0 - HALO:r0, sl].astype(F32)
                    up = u_ref[r0 - HALO:r0, sl].astype(F32)
                gt, gc = _conv_taps(g_ref[rows, sl].astype(F32), gp, wg, FFN_CONV, row)
                ut, uc = _conv_taps(u_ref[rows, sl].astype(F32), up, wu, FFN_CONV, row)
                da = da_ref[rows, sl].astype(F32)
                sg = _sigmoid(gc)
                dgc = da * uc * (sg * (1.0 + gc * (1.0 - sg)))
                duc = da * (gc * sg)
                dg_ref[rows, sl] = dgc.astype(BF16)
                du_ref[rows, sl] = duc.astype(BF16)
                dwg = [dwg[k] + _fold8(dgc * gt[k]) for k in range(FFN_CONV)]
                dwu = [dwu[k] + _fold8(duc * ut[k]) for k in range(FFN_CONV)]
            for k in range(FFN_CONV):
                dwg_ref[k:k + 1, sl] += jnp.sum(dwg[k], axis=0, keepdims=True)
                dwu_ref[k:k + 1, sl] += jnp.sum(dwu[k], axis=0, keepdims=True)

    prev = lambda off: (lambda j, i: (jnp.maximum(i * (tr // HALO) - 1, 0), j + off))
    blk = lambda off: pl.BlockSpec((tr, tc), lambda j, i: (i, j + off))
    wblk = lambda off: pl.BlockSpec((FFN_CONV, tc), lambda j, i: (0, j + off))
    dgc, duc, dwg, dwu = pl.pallas_call(
        body, name=name,
        out_shape=(jax.ShapeDtypeStruct((t, D_FF), BF16), jax.ShapeDtypeStruct((t, D_FF), BF16),
                   jax.ShapeDtypeStruct((FFN_CONV, D_FF), F32), jax.ShapeDtypeStruct((FFN_CONV, D_FF), F32)),
        grid=(nj, t // tr),
        in_specs=[blk(0), blk(0), pl.BlockSpec((HALO, tc), prev(0)), blk(nj), pl.BlockSpec((HALO, tc), prev(nj)),
                  wblk(0), wblk(nj)],
        out_specs=(blk(0), blk(0), wblk(0), wblk(0)),
        compiler_params=_params(("parallel", "arbitrary")),
    )(dact, upre, upre, upre, upre, cw, cw)
    return dgc, duc, dwg, dwu


def _dn_pre_fwd(qkv_pre, cw, name):
    t = qkv_pre.shape[0]
    tr = _tile(t, 512)
    scale = HEAD_DIM ** -0.5

    def body(x_ref, p_ref, w_ref, o_ref):
        i, j = pl.program_id(0), pl.program_id(1)
        row = _iota((tr, HEAD_DIM), 0)
        for h in range(HEADS):
            sl = slice(h * HEAD_DIM, (h + 1) * HEAD_DIM)
            prev = jnp.where(i == 0, 0.0, p_ref[:, sl].astype(F32))
            _, c = _conv_taps(x_ref[:, sl].astype(F32), prev, w_ref[:, sl], DN_CONV, row)
            s = c * _sigmoid(c)
            r = lax.rsqrt(jnp.sum(s * s, axis=1, keepdims=True) + EPS)
            o_ref[:, sl] = s * jnp.where(j == 0, r * scale, jnp.where(j == 1, r, 1.0))

    return pl.pallas_call(
        body, name=name,
        out_shape=jax.ShapeDtypeStruct((t, 3 * WIDTH), F32),
        grid=(t // tr, 3),
        in_specs=[pl.BlockSpec((tr, WIDTH), lambda i, j: (i, j)),
                  pl.BlockSpec((HALO, WIDTH), lambda i, j: (jnp.maximum(i * (tr // HALO) - 1, 0), j)),
                  pl.BlockSpec((DN_CONV, WIDTH), lambda i, j: (0, j))],
        out_specs=pl.BlockSpec((tr, WIDTH), lambda i, j: (i, j)),
        compiler_params=_params(("parallel", "parallel")),
    )(qkv_pre, qkv_pre, cw)


def _dn_pre_bwd(dq, dk, dv, qkv_pre, cw, name):
    t = qkv_pre.shape[0]
    tr = _tile(t, 256)
    scale = HEAD_DIM ** -0.5

    def body(dq_ref, dk_ref, dv_ref, x_ref, p_ref, w_ref, dc_ref, dw_ref):
        j, i = pl.program_id(0), pl.program_id(1)
        row = _iota((CHUNK_ROWS, HEAD_DIM), 0)

        @pl.when(i == 0)
        def _():
            dw_ref[...] = jnp.zeros_like(dw_ref)

        for h in range(HEADS):
            sl = slice(h * HEAD_DIM, (h + 1) * HEAD_DIM)
            wv = w_ref[:, sl]
            dw = [jnp.zeros((8, HEAD_DIM), F32)] * DN_CONV
            for r0 in range(0, tr, CHUNK_ROWS):
                rows = slice(r0, r0 + CHUNK_ROWS)
                if r0 == 0:
                    prev = jnp.where(i == 0, 0.0, p_ref[:, sl].astype(F32))
                else:
                    prev = x_ref[r0 - HALO:r0, sl].astype(F32)
                taps, c = _conv_taps(x_ref[rows, sl].astype(F32), prev, wv, DN_CONV, row)
                d = jnp.where(j == 0, dq_ref[rows, sl] * scale, jnp.where(j == 1, dk_ref[rows, sl], dv_ref[rows, sl]))
                sg = _sigmoid(c)
                s = c * sg
                r = lax.rsqrt(jnp.sum(s * s, axis=1, keepdims=True) + EPS)
                nh = s * r
                ds_norm = r * (d - nh * jnp.sum(nh * d, axis=1, keepdims=True))
                dc = jnp.where(j < 2, ds_norm, d) * (sg * (1.0 + c * (1.0 - sg)))
                dc_ref[rows, sl] = dc.astype(BF16)
                dw = [dw[k] + _fold8(dc * taps[k]) for k in range(DN_CONV)]
            for k in range(DN_CONV):
                dw_ref[k:k + 1, sl] += jnp.sum(dw[k], axis=0, keepdims=True)

    dspec = lambda p: pl.BlockSpec((tr, WIDTH), lambda j, i: (jnp.where(j == p, i, 0), 0))
    return pl.pallas_call(
        body, name=name,
        out_shape=(jax.ShapeDtypeStruct((t, 3 * WIDTH), BF16), jax.ShapeDtypeStruct((DN_CONV, 3 * WIDTH), F32)),
        grid=(3, t // tr),
        in_specs=[dspec(0), dspec(1), dspec(2),
                  pl.BlockSpec((tr, WIDTH), lambda j, i: (i, j)),
                  pl.BlockSpec((HALO, WIDTH), lambda j, i: (jnp.maximum(i * (tr // HALO) - 1, 0), j)),
                  pl.BlockSpec((DN_CONV, WIDTH), lambda j, i: (0, j))],
        out_specs=(pl.BlockSpec((tr, WIDTH), lambda j, i: (i, j)),
                   pl.BlockSpec((DN_CONV, WIDTH), lambda j, i: (0, j))),
        compiler_params=_params(("parallel", "arbitrary")),
    )(dq, dk, dv, qkv_pre, qkv_pre, cw)


def _tri(n, kind):
    r, c = _iota((n, n), 0), _iota((n, n), 1)
    m = {"lower": r >= c, "strict": r > c, "upper": r <= c}[kind]
    return m


GATE_ROWS = 4 * DN_CHUNK


def _chunk_tri(kind):
    r, c = _iota((GATE_ROWS, GATE_ROWS), 0), _iota((GATE_ROWS, GATE_ROWS), 1)
    same = (r // DN_CHUNK) == (c // DN_CHUNK)
    return jnp.where(jnp.logical_and(same, _tri(GATE_ROWS, kind)), 1.0, 0.0).astype(BF16)


def _dn_gates_fwd(hab, alog, dtb, name):
    t = hab.shape[0]
    cc = GATE_ROWS

    def body(h_ref, al_ref, dt_ref, o_ref):
        hv = h_ref[...]
        lane = _iota(hv.shape, 1)
        xa = hv + dt_ref[...]
        sp = jnp.maximum(xa, 0.0) + _log1pexp_neg_abs(xa)
        g = jnp.where(lane < HEADS, -jnp.exp(al_ref[...]) * sp, 0.0)
        gc = _dot_xl(_chunk_tri("lower"), g, NN)
        o_ref[...] = jnp.where(lane < HEADS, gc, jnp.where(lane < 2 * HEADS, _sigmoid(hv), 0.0))

    return pl.pallas_call(
        body, name=name,
        out_shape=jax.ShapeDtypeStruct((t, 128), F32),
        grid=(t // cc,),
        in_specs=[pl.BlockSpec((cc, 128), lambda i: (i, 0)), pl.BlockSpec((1, 128), lambda i: (0, 0)),
                  pl.BlockSpec((1, 128), lambda i: (0, 0))],
        out_specs=pl.BlockSpec((cc, 128), lambda i: (i, 0)),
        compiler_params=_params(("parallel",)),
    )(hab, alog, dtb)


def _dn_gates_bwd(dgates, hab, alog, dtb, name):
    t = hab.shape[0]
    cc = GATE_ROWS

    def body(d_ref, h_ref, al_ref, dt_ref, o_ref, dal_ref, ddt_ref):
        i = pl.program_id(0)
        hv = h_ref[...]
        dv = d_ref[...]
        lane = _iota(hv.shape, 1)
        dg = _dot_xl(_chunk_tri("upper"), jnp.where(lane < HEADS, dv, 0.0), NN)
        xa = hv + dt_ref[...]
        sp = jnp.maximum(xa, 0.0) + _log1pexp_neg_abs(xa)
        ea = jnp.exp(al_ref[...])
        da = jnp.where(lane < HEADS, dg * (-ea) * _sigmoid(xa), 0.0)
        be = _sigmoid(hv)
        db = dv * be * (1.0 - be)
        o_ref[...] = jnp.where(lane < HEADS, da, jnp.where(lane < 2 * HEADS, db, 0.0))

        @pl.when(i == 0)
        def _():
            dal_ref[...] = jnp.zeros_like(dal_ref)
            ddt_ref[...] = jnp.zeros_like(ddt_ref)

        dal_ref[...] += jnp.sum(jnp.where(lane < HEADS, dg * (-ea) * sp, 0.0), axis=0, keepdims=True)
        ddt_ref[...] += jnp.sum(da, axis=0, keepdims=True)

    return pl.pallas_call(
        body, name=name,
        out_shape=(jax.ShapeDtypeStruct((t, 128), F32), jax.ShapeDtypeStruct((1, 128), F32),
                   jax.ShapeDtypeStruct((1, 128), F32)),
        grid=(t // cc,),
        in_specs=[pl.BlockSpec((cc, 128), lambda i: (i, 0)), pl.BlockSpec((cc, 128), lambda i: (i, 0)),
                  pl.BlockSpec((1, 128), lambda i: (0, 0)), pl.BlockSpec((1, 128), lambda i: (0, 0))],
        out_specs=(pl.BlockSpec((cc, 128), lambda i: (i, 0)), pl.BlockSpec((1, 128), lambda i: (0, 0)),
                   pl.BlockSpec((1, 128), lambda i: (0, 0))),
        compiler_params=_params(("arbitrary",)),
    )(dgates, hab, alog, dtb)


def _dn_chunk_common(gates, h):
    cc = DN_CHUNK
    lane = _iota(gates.shape, 1)
    gh = jnp.where(lane == h, gates, 0.0)
    gc_col = jnp.sum(gh, axis=1, keepdims=True)
    gc_row = _dot_xl(jnp.ones((cc, 128), BF16), gh, NT)
    beta = jnp.sum(jnp.where(lane == h + HEADS, gates, 0.0), axis=1, keepdims=True)
    lower = _tri(cc, "lower")
    decay = jnp.where(lower, jnp.exp(jnp.where(lower, gc_col - gc_row, 0.0)), 0.0)
    gc_last = gc_col[cc - 1:cc, :]
    return gc_col, gc_last, beta, decay


def _dn_local_fwd(act, gates, name):
    t = act.shape[0]
    cc = DN_CHUNK
    nc = t // cc

    def body(q_ref, k_ref, v_ref, g_ref, u_ref, w_ref, kd_ref, qg_ref, ti_ref, p_ref):
        gates = g_ref[...]
        eye = jnp.where(_iota((cc, cc), 0) == _iota((cc, cc), 1), 1.0, 0.0)
        hs = range(HEADS)
        sl = [slice(h * HEAD_DIM, (h + 1) * HEAD_DIM) for h in hs]
        q, k, v = ([r[:, s] for s in sl] for r in (q_ref, k_ref, v_ref))
        gc_col, gc_last, beta, decay = zip(*[_dn_chunk_common(gates, h) for h in hs])
        gam = [jnp.exp(g) for g in gc_col]
        kb = [k[h] * beta[h] for h in hs]
        npow = [-jnp.where(_tri(cc, "strict"), _dotb(kb[h], k[h], NT) * decay[h], 0.0) for h in hs]
        tinv = [eye + n for n in npow]
        for _ in range(5):
            npow = [_dot3(n, n, NN) for n in npow]
            tinv = [t + _dot3(t, n, NN) for t, n in zip(tinv, npow)]
        uu = [_dot3(tinv[h], v[h] * beta[h], NN) for h in hs]
        ww = [_dot3(tinv[h], kb[h] * gam[h], NN) for h in hs]
        pp = [jnp.where(_tri(cc, "lower"), _dotb(q[h], k[h], NT) * decay[h], 0.0) for h in hs]
        for h in hs:
            u_ref[:, sl[h]] = uu[h]
            w_ref[:, sl[h]] = ww[h].astype(BF16)
            kd_ref[:, sl[h]] = (k[h] * jnp.exp(gc_last[h] - gc_col[h])).astype(BF16)
            qg_ref[:, sl[h]] = (q[h] * gam[h]).astype(BF16)
            ti_ref[h] = tinv[h]
            p_ref[h] = pp[h].astype(BF16)

    row = lambda off: pl.BlockSpec((cc, WIDTH), lambda n: (n, off))
    mat = pl.BlockSpec((HEADS, cc, cc), lambda n: (0, n, 0))
    tw, tb = jax.ShapeDtypeStruct((t, WIDTH), F32), jax.ShapeDtypeStruct((t, WIDTH), BF16)
    hm, hb = jax.ShapeDtypeStruct((HEADS, t, cc), F32), jax.ShapeDtypeStruct((HEADS, t, cc), BF16)
    return pl.pallas_call(
        body, name=name,
        out_shape=(tw, tb, tb, tb, hm, hb),
        grid=(nc,),
        in_specs=[row(0), row(1), row(2), pl.BlockSpec((cc, 128), lambda n: (n, 0))],
        out_specs=(row(0), row(0), row(0), row(0), mat, mat),
        compiler_params=_params(("parallel",)),
    )(act, act, act, gates)


def _dn_scan_fwd(u, w, kd, qg, p, gates, name):
    t = u.shape[0]
    cc = DN_CHUNK
    nc = t // cc

    def body(u_ref, w_ref, kd_ref, qg_ref, p_ref, g_ref, o_ref, sh_ref, s_ref):
        n = pl.program_id(0)

        @pl.when(n == 0)
        def _():
            s_ref[...] = jnp.zeros_like(s_ref)

        glast = jnp.exp(g_ref[cc - 1:cc, :])
        hs = range(HEADS)
        sl = [slice(h * HEAD_DIM, (h + 1) * HEAD_DIM) for h in hs]
        s = [s_ref[h] for h in hs]
        sb = [a.astype(BF16) for a in s]
        vn = [u_ref[:, sl[h]] - _dot(w_ref[:, sl[h]].astype(BF16), sb[h], NN) for h in hs]
        vnb = [a.astype(BF16) for a in vn]
        o_state = [_dot(qg_ref[:, sl[h]].astype(BF16), sb[h], NN) for h in hs]
        o_local = [_dot(p_ref[h].astype(BF16), vnb[h], NN) for h in hs]
        s_add = [_dot(kd_ref[:, sl[h]].astype(BF16), vnb[h], TN) for h in hs]
        for h in hs:
            o_ref[:, sl[h]] = o_state[h] + o_local[h]
            sh_ref[0, h] = sb[h]
            s_ref[h] = glast[:, h:h + 1] * s[h] + s_add[h]

    row = pl.BlockSpec((cc, WIDTH), lambda n: (n, 0))
    return pl.pallas_call(
        body, name=name,
        out_shape=(jax.ShapeDtypeStruct((t, WIDTH), F32),
                   jax.ShapeDtypeStruct((nc, HEADS, HEAD_DIM, HEAD_DIM), BF16)),
        grid=(nc,),
        in_specs=[row, row, row, row, pl.BlockSpec((HEADS, cc, cc), lambda n: (0, n, 0)),
                  pl.BlockSpec((cc, 128), lambda n: (n, 0))],
        out_specs=(row, pl.BlockSpec((1, HEADS, HEAD_DIM, HEAD_DIM), lambda n: (n, 0, 0, 0))),
        scratch_shapes=[pltpu.VMEM((HEADS, HEAD_DIM, HEAD_DIM), F32)],
        compiler_params=_params(("arbitrary",)),
    )(u, w, kd, qg, p, gates)


def _dn_scan_bwd(do, w, kd, qg, p, gates, name):
    t = do.shape[0]
    cc = DN_CHUNK
    nc = t // cc

    def body(do_ref, w_ref, kd_ref, qg_ref, p_ref, g_ref, dvn_ref, dsh_ref, ds_ref):
        n = pl.program_id(0)

        @pl.when(n == 0)
        def _():
            ds_ref[...] = jnp.zeros_like(ds_ref)

        glast = jnp.exp(g_ref[cc - 1:cc, :])
        hs = range(HEADS)
        sl = [slice(h * HEAD_DIM, (h + 1) * HEAD_DIM) for h in hs]
        ds = [ds_ref[h] for h in hs]
        dob = [do_ref[:, sl[h]].astype(BF16) for h in hs]
        dvn = [_dot(p_ref[h].astype(BF16), dob[h], TN) + _dot(kd_ref[:, sl[h]].astype(BF16), ds[h].astype(BF16), NN)
               for h in hs]
        ds_q = [_dot(qg_ref[:, sl[h]].astype(BF16), dob[h], TN) for h in hs]
        ds_w = [_dot(w_ref[:, sl[h]].astype(BF16), dvn[h].astype(BF16), TN) for h in hs]
        for h in hs:
            dvn_ref[:, sl[h]] = dvn[h]
            dsh_ref[0, h] = ds[h].astype(BF16)
            ds_ref[h] = ds_q[h] + glast[:, h:h + 1] * ds[h] - ds_w[h]

    row = pl.BlockSpec((cc, WIDTH), lambda n: (nc - 1 - n, 0))
    return pl.pallas_call(
        body, name=name,
        out_shape=(jax.ShapeDtypeStruct((t, WIDTH), F32),
                   jax.ShapeDtypeStruct((nc, HEADS, HEAD_DIM, HEAD_DIM), BF16)),
        grid=(nc,),
        in_specs=[row, row, row, row, pl.BlockSpec((HEADS, cc, cc), lambda n: (0, nc - 1 - n, 0)),
                  pl.BlockSpec((cc, 128), lambda n: (nc - 1 - n, 0))],
        out_specs=(row, pl.BlockSpec((1, HEADS, HEAD_DIM, HEAD_DIM), lambda n: (nc - 1 - n, 0, 0, 0))),
        scratch_shapes=[pltpu.VMEM((HEADS, HEAD_DIM, HEAD_DIM), F32)],
        compiler_params=_params(("arbitrary",)),
    )(do, w, kd, qg, p, gates)


def _dn_local_bwd(act, gates, u, w, kd, qg, tinv, p, sh, dsh, dvn, do, name):
    t = act.shape[0]
    cc = DN_CHUNK
    nc = t // cc

    def body(q_ref, k_ref, v_ref, g_ref, u_ref, w_ref, kd_ref, qg_ref, ti_ref, p_ref, s_ref, ds_ref,
             dvn_ref, do_ref, dq_ref, dk_ref, dv_ref, dg_ref):
        gates_v = g_ref[...]
        lower, strict = _tri(cc, "lower"), _tri(cc, "strict")
        ones = jnp.ones((cc, 128), BF16)
        rowc = _iota((cc, 1), 0)
        lane = _iota((cc, 128), 1)
        hs = range(HEADS)
        sl = [slice(h * HEAD_DIM, (h + 1) * HEAD_DIM) for h in hs]
        q, k, v, uu, ww, kd, qg, dvn, do = ([r[:, s] for s in sl] for r in (
            q_ref, k_ref, v_ref, u_ref, w_ref, kd_ref, qg_ref, dvn_ref, do_ref))
        gc_col, gc_last, beta, decay = zip(*[_dn_chunk_common(gates_v, h) for h in hs])
        gam = [jnp.exp(g) for g in gc_col]
        kb = [k[h] * beta[h] for h in hs]
        s_in = [s_ref[0, h] for h in hs]
        ds_out = [ds_ref[0, h] for h in hs]
        tinv = [ti_ref[h] for h in hs]

        a = [jnp.where(strict, _dotb(kb[h], k[h], NT) * decay[h], 0.0) for h in hs]
        vn = [uu[h] - _dotb(ww[h], s_in[h], NN) for h in hs]
        dqg = [_dotb(do[h], s_in[h], NT) for h in hs]
        dw = [-_dotb(dvn[h], s_in[h], NT) for h in hs]
        dp = [jnp.where(lower, _dotb(do[h], vn[h], NT), 0.0) for h in hs]
        dkd = [_dotb(vn[h], ds_out[h], NT) for h in hs]
        dru = [_dot3(tinv[h], dvn[h], TN) for h in hs]
        drw = [_dot3(tinv[h], dw[h], TN) for h in hs]
        da = [-jnp.where(strict, _dotb(dru[h], uu[h], NT) + _dotb(drw[h], ww[h], NT), 0.0) for h in hs]
        dad = [da[h] * decay[h] for h in hs]
        dpd = [dp[h] * decay[h] for h in hs]
        dkb = [_dotb(dad[h], k[h], NN) + gam[h] * drw[h] for h in hs]
        dk = [_dotb(dad[h], kb[h], TN) + _dotb(dpd[h], q[h], TN) + beta[h] * dkb[h]
              + jnp.exp(gc_last[h] - gc_col[h]) * dkd[h] for h in hs]
        dq = [gam[h] * dqg[h] + _dotb(dpd[h], k[h], NN) for h in hs]
        gm = [da[h] * a[h] + dp[h] * p_ref[h] for h in hs]
        colsum = [_dot_xr(gm[h], ones, TN)[:, 0:1] for h in hs]

        dgates = jnp.zeros((cc, 128), F32)
        for h in hs:
            dk_ref[:, sl[h]] = dk[h]
            dq_ref[:, sl[h]] = dq[h]
            dv_ref[:, sl[h]] = beta[h] * dru[h]
            dbeta = (jnp.sum(dkb[h] * k[h], axis=1, keepdims=True)
                     + jnp.sum(dru[h] * v[h], axis=1, keepdims=True))
            rkd = jnp.sum(dkd[h] * kd[h], axis=1, keepdims=True)
            dgc = (jnp.sum(gm[h], axis=1, keepdims=True) - colsum[h]
                   + jnp.sum(dqg[h] * qg[h], axis=1, keepdims=True)
                   + jnp.sum(drw[h] * kb[h], axis=1, keepdims=True) * gam[h] - rkd)
            tail = jnp.sum(rkd, axis=0, keepdims=True) + jnp.exp(gc_last[h]) * jnp.sum(
                jnp.sum(s_in[h].astype(F32) * ds_out[h].astype(F32), axis=1, keepdims=True), axis=0, keepdims=True)
            dgc = dgc + jnp.where(rowc == cc - 1, tail, 0.0)
            dgates = dgates + jnp.where(lane == h, dgc, 0.0) + jnp.where(lane == h + HEADS, dbeta, 0.0)
        dg_ref[...] = dgates

    row = lambda off: pl.BlockSpec((cc, WIDTH), lambda n: (n, off))
    mat = pl.BlockSpec((HEADS, cc, cc), lambda n: (0, n, 0))
    st = pl.BlockSpec((1, HEADS, HEAD_DIM, HEAD_DIM), lambda n: (n, 0, 0, 0))
    gl = pl.BlockSpec((cc, 128), lambda n: (n, 0))
    tw = jax.ShapeDtypeStruct((t, WIDTH), F32)
    return pl.pallas_call(
        body, name=name,
        out_shape=(tw, tw, tw, jax.ShapeDtypeStruct((t, 128), F32)),
        grid=(nc,),
        in_specs=[row(0), row(1), row(2), gl, row(0), row(0), row(0), row(0), mat, mat, st, st, row(0), row(0)],
        out_specs=(row(0), row(0), row(0), gl),
        compiler_params=_params(("parallel",)),
    )(act, act, act, gates, u, w, kd, qg, tinv, p, sh, dsh, dvn, do)


def _dn_post_fwd(o, gate, w, name):
    t = o.shape[0]
    tr = _tile(t, 512)

    def body(o_ref, g_ref, w_ref, y_ref):
        for h in range(HEADS):
            sl = slice(h * HEAD_DIM, (h + 1) * HEAD_DIM)
            ov, gv = o_ref[:, sl], g_ref[:, sl].astype(F32)
            r = lax.rsqrt(jnp.mean(ov * ov, axis=1, keepdims=True) + EPS)
            y_ref[:, sl] = (ov * r * w_ref[...] * (gv * _sigmoid(gv))).astype(BF16)

    blk = pl.BlockSpec((tr, WIDTH), lambda i: (i, 0))
    return pl.pallas_call(
        body, name=name,
        out_shape=jax.ShapeDtypeStruct((t, WIDTH), BF16),
        grid=(t // tr,),
        in_specs=[blk, blk, pl.BlockSpec((1, HEAD_DIM), lambda i: (0, 0))],
        out_specs=blk,
        compiler_params=_params(("parallel",)),
    )(o, gate, w)


def _dn_post_bwd(dy, o, gate, w, name):
    t = o.shape[0]
    tr = _tile(t, 512)

    def body(dy_ref, o_ref, g_ref, w_ref, do_ref, dg_ref, dw_ref):
        i = pl.program_id(0)

        @pl.when(i == 0)
        def _():
            dw_ref[...] = jnp.zeros_like(dw_ref)

        dw = jnp.zeros((1, HEAD_DIM), F32)
        for h in range(HEADS):
            sl = slice(h * HEAD_DIM, (h + 1) * HEAD_DIM)
            ov, gv, dyv = o_ref[:, sl], g_ref[:, sl].astype(F32), dy_ref[:, sl].astype(F32)
            r = lax.rsqrt(jnp.mean(ov * ov, axis=1, keepdims=True) + EPS)
            oh = ov * r
            sg = _sigmoid(gv)
            dg_ref[:, sl] = (dyv * oh * w_ref[...] * (sg * (1.0 + gv * (1.0 - sg)))).astype(BF16)
            dn = dyv * (gv * sg)
            doh = dn * w_ref[...]
            do_ref[:, sl] = r * (doh - oh * jnp.mean(doh * oh, axis=1, keepdims=True))
            dw = dw + jnp.sum(dn * oh, axis=0, keepdims=True)
        dw_ref[...] += dw

    blk = pl.BlockSpec((tr, WIDTH), lambda i: (i, 0))
    return pl.pallas_call(
        body, name=name,
        out_shape=(jax.ShapeDtypeStruct((t, WIDTH), F32), jax.ShapeDtypeStruct((t, WIDTH), BF16),
                   jax.ShapeDtypeStruct((1, HEAD_DIM), F32)),
        grid=(t // tr,),
        in_specs=[blk, blk, blk, pl.BlockSpec((1, HEAD_DIM), lambda i: (0, 0))],
        out_specs=(blk, blk, pl.BlockSpec((1, HEAD_DIM), lambda i: (0, 0))),
        compiler_params=_params(("arbitrary",)),
    )(dy, o, gate, w)


def _sb_scores(qs, k_ref, qi, it, carries, uincl):
    bk = ATT_BLOCK
    scale = HEAD_DIM ** -0.5
    heads, groups = range(len(qs)), range(SB_GROUP)
    lane = [slice(e * HEAD_DIM, (e + 1) * HEAD_DIM) for e in heads]
    js = [qi - SB_GROUP * it - g for g in groups]
    rows = [pl.ds(pl.multiple_of(jnp.maximum(j, 0) * bk, bk), bk) for j in js]
    qpos = qi * bk + _iota((bk, bk), 0)
    col = _iota((bk, bk), 1)
    mask1 = [jnp.logical_and(j * bk + col < qpos, j >= 0) for j in js]
    ks = [[k_ref[r, lane[e]] for r in rows] for e in heads]
    z = [[_dot(qs[e], k, NT) * scale for k in ks[e]] for e in heads]
    soft = [[_log1pexp_neg_abs(a) for a in ze] for ze in z]
    lk_full = [[-(jnp.maximum(a, 0.0) + s) for a, s in zip(z[e], soft[e])] for e in heads]
    lk = [[jnp.where(m, a, 0.0) for m, a in zip(mask1, lk_full[e])] for e in heads]
    ls = [[jnp.minimum(a, 0.0) - s for a, s in zip(z[e], soft[e])] for e in heads]
    incl = [[_dot_xr2(a, uincl, NN) for a in lk[e]] for e in heads]
    weights, out_carries = [], []
    for e in heads:
        cb, we = carries[e], []
        for g in groups:
            we.append(jnp.where(mask1[g], jnp.exp(ls[e][g] + (cb + incl[e][g] - lk[e][g])), 0.0))
            cb = cb + incl[e][g][:, 0:1]
        weights.append(we)
        out_carries.append(cb)
    return rows, ks, weights, mask1, lk_full, ls, out_carries


def _sb_more(qi, carry):
    it, cbs = carry[0], carry[1]
    live = jnp.max(cbs[0])
    for cb in cbs[1:]:
        live = jnp.maximum(live, jnp.max(cb))
    return jnp.logical_and(SB_GROUP * it <= qi, live > SB_LOG_ZERO)


def _sb_steps(groups, nq):
    def when():
        h, i = pl.program_id(0), pl.program_id(1)
        return (jnp.logical_and(h == 0, i == 0), jnp.logical_and(h == groups // 2, i == 0),
                jnp.logical_and(h == groups - 1, i == nq - 1))
    return when


def _sb_fwd(qkv, name, comm=None):
    t = qkv.shape[0]
    bk = ATT_BLOCK
    hp, wide = SB_HEADS_FWD, SB_HEADS_FWD * HEAD_DIM
    lane = [slice(e * HEAD_DIM, (e + 1) * HEAD_DIM) for e in range(hp)]

    def body(q_ref, k_ref, v_ref, o_ref):
        qi = pl.program_id(1)
        qs = [q_ref[:, s] for s in lane]
        uincl = jnp.where(_tri(bk, "lower"), 1.0, 0.0).astype(BF16)

        def step(carry):
            it, cbs, accs = carry
            rows, _, weights, _, _, _, cbs = _sb_scores(qs, k_ref, qi, it, cbs, uincl)
            accs = list(accs)
            for e in range(hp):
                for r, a in zip(rows, weights[e]):
                    accs[e] = accs[e] + _dot(a.astype(BF16), v_ref[r, lane[e]], NN)
            return it + 1, tuple(cbs), tuple(accs)

        init = (jnp.int32(0), (jnp.zeros((bk, 1), F32),) * hp, (jnp.zeros((bk, HEAD_DIM), F32),) * hp)
        _, _, accs = lax.while_loop(functools.partial(_sb_more, qi), step, init)
        for e in range(hp):
            o_ref[:, lane[e]] = accs[e]

    groups = HEADS // hp
    (o,), extra = _host_call(
        body, name, comm, _sb_steps(groups, t // bk), [jax.ShapeDtypeStruct((t, WIDTH), F32)], (groups, t // bk),
        [pl.BlockSpec((bk, wide), lambda h, i: (i, h)),
         pl.BlockSpec((t, wide), lambda h, i: (0, groups + h)),
         pl.BlockSpec((t, wide), lambda h, i: (0, 2 * groups + h))],
        [pl.BlockSpec((bk, wide), lambda h, i: (i, h))], [], ("parallel", "arbitrary"), (qkv, qkv, qkv))
    return o, extra


def _sb_bwd(qkv, o, do, name, comm=None):
    assert do.dtype == BF16
    t = qkv.shape[0]
    bk = ATT_BLOCK
    scale = HEAD_DIM ** -0.5
    hp, wide = SB_HEADS_BWD, SB_HEADS_BWD * HEAD_DIM
    lane = [slice(e * HEAD_DIM, (e + 1) * HEAD_DIM) for e in range(hp)]

    def body(q_ref, k_ref, v_ref, o_ref, do_ref, dq_ref, dk_out, dv_out, dk_ref, dv_ref):
        qi = pl.program_id(1)

        @pl.when(qi == 0)
        def _():
            dk_ref[...] = jnp.zeros_like(dk_ref)
            dv_ref[...] = jnp.zeros_like(dv_ref)

        heads, groups = range(hp), range(SB_GROUP)
        qs = [q_ref[:, s] for s in lane]
        dob = [do_ref[:, s] for s in lane]
        dsum = [jnp.sum(dob[e].astype(F32) * o_ref[:, lane[e]], axis=1, keepdims=True) for e in heads]
        uincl = jnp.where(_tri(bk, "lower"), 1.0, 0.0).astype(BF16)

        def step(carry):
            it, cbs, ces, dqs = carry
            rows, ks, weights, mask, lk_full, ls, cbs = _sb_scores(qs, k_ref, qi, it, cbs, uincl)
            ab = [[a.astype(BF16) for a in weights[e]] for e in heads]
            vs = [[v_ref[r, lane[e]] for r in rows] for e in heads]
            dla = [[ab[e][g].astype(F32) * _dot(dob[e], vs[e][g], NT) for g in groups] for e in heads]
            suf = [[_dot_xr2(a, uincl, NN) for a in dla[e]] for e in heads]
            ces, dqs = list(ces), list(dqs)
            for e in heads:
                for g in groups:
                    err = dsum[e] - (ces[e] + suf[e][g])
                    ces[e] = ces[e] + suf[e][g][:, 0:1]
                    dz = jnp.where(mask[g], dla[e][g] * jnp.exp(lk_full[e][g]) - err * jnp.exp(ls[e][g]), 0.0)
                    dzb = (dz * scale).astype(BF16)
                    dqs[e] = dqs[e] + _dot(dzb, ks[e][g], NN)
                    dk_ref[rows[g], lane[e]] += _dot(dzb, qs[e], TN)
                    dv_ref[rows[g], lane[e]] += _dot(ab[e][g], dob[e], TN)
            return it + 1, tuple(cbs), tuple(ces), tuple(dqs)

        zc = (jnp.zeros((bk, 1), F32),) * hp
        init = (jnp.int32(0), zc, zc, (jnp.zeros((bk, HEAD_DIM), F32),) * hp)
        dqs = lax.while_loop(functools.partial(_sb_more, qi), step, init)[3]
        for e in heads:
            dq_ref[:, lane[e]] = dqs[e].astype(BF16)

        @pl.when(qi == t // bk - 1)
        def _():
            dk_out[...] = dk_ref[...].astype(BF16)
            dv_out[...] = dv_ref[...].astype(BF16)

    ngroup = HEADS // hp
    tw = jax.ShapeDtypeStruct((t, WIDTH), BF16)
    qb = pl.BlockSpec((bk, wide), lambda h, i: (i, h))
    full = lambda off: pl.BlockSpec((t, wide), lambda h, i: (0, off + h))
    return _host_call(
        body, name, comm, _sb_steps(ngroup, t // bk), [tw, tw, tw], (ngroup, t // bk),
        [qb, full(ngroup), full(2 * ngroup), qb, qb], [qb, full(0), full(0)],
        [pltpu.VMEM((t, wide), F32), pltpu.VMEM((t, wide), F32)], ("parallel", "arbitrary"),
        (qkv, qkv, qkv, o, do))


def _merge_fwd(pd, ps, gl, name):
    t = pd.shape[0]
    tr, tc = _tile(t, 512), 512
    nj = D_MODEL // tc

    def body(pd_ref, ps_ref, gd_ref, gs_ref, o_ref):
        gd, gs = gd_ref[...].astype(F32), gs_ref[...].astype(F32)
        o_ref[...] = (_sigmoid(gd) * pd_ref[...].astype(F32) + _sigmoid(gs) * ps_ref[...].astype(F32)).astype(BF16)

    blk = lambda off: pl.BlockSpec((tr, tc), lambda i, j: (i, j + off))
    return pl.pallas_call(
        body, name=name,
        out_shape=jax.ShapeDtypeStruct((t, D_MODEL), BF16),
        grid=(t // tr, nj),
        in_specs=[blk(0), blk(0), blk(0), blk(nj)],
        out_specs=blk(0),
        compiler_params=_params(("parallel", "parallel")),
    )(pd, ps, gl, gl)


def _merge_bwd(dm, pd, ps, gl, name):
    t = pd.shape[0]
    tr, tc = _tile(t, 512), 512
    nj = D_MODEL // tc

    def body(dm_ref, pd_ref, ps_ref, gd_ref, gs_ref, dpd_ref, dps_ref, dgd_ref, dgs_ref):
        dmv = dm_ref[...].astype(F32)
        sd, ss = _sigmoid(gd_ref[...].astype(F32)), _sigmoid(gs_ref[...].astype(F32))
        dpd_ref[...] = (dmv * sd).astype(BF16)
        dps_ref[...] = (dmv * ss).astype(BF16)
        dgd_ref[...] = (dmv * pd_ref[...].astype(F32) * sd * (1.0 - sd)).astype(BF16)
        dgs_ref[...] = (dmv * ps_ref[...].astype(F32) * ss * (1.0 - ss)).astype(BF16)

    blk = lambda off: pl.BlockSpec((tr, tc), lambda i, j: (i, j + off))
    out = jax.ShapeDtypeStruct((t, D_MODEL), BF16)
    return pl.pallas_call(
        body, name=name,
        out_shape=(out, out, out, out),
        grid=(t // tr, nj),
        in_specs=[blk(0), blk(0), blk(0), blk(0), blk(nj)],
        out_specs=(blk(0), blk(0), blk(0), blk(0)),
        compiler_params=_params(("parallel", "parallel")),
    )(dm, pd, ps, gl, gl)


def _local_step(x, target, wts, plan=None, n1=None):
    if n1 is None:
        n1 = _rmsnorm_fwd(x, wts["norm1_w"], "norm1_fwd")
    qkv_pre = _matmul(n1, wts["w_dnqkv_t"], "nt", BF16, "in_dnqkv")
    hgate = _matmul(n1, wts["w_dngate_t"], "nt", BF16, "in_dngate")
    sbqkv = _matmul(n1, wts["w_sbqkv_t"], "nt", BF16, "in_sbqkv")
    gl = _matmul(n1, wts["w_gl_t"], "nt", BF16, "in_gl")
    hab = _matmul(n1, wts["w_ab_t"], "nt", F32, "in_ab")

    act = _dn_pre_fwd(qkv_pre, wts["dn_conv_w"], "dn_pre_fwd")
    gates = _dn_gates_fwd(hab, wts["alog"], wts["dtb"], "dn_gates_fwd")
    u, w, kd, qg, tinv, p = _dn_local_fwd(act, gates, "dn_local_fwd")
    o_dn, sh = _dn_scan_fwd(u, w, kd, qg, p, gates, "dn_scan_fwd")
    y_dn = _dn_post_fwd(o_dn, hgate, wts["dn_norm_w"], "dn_post_fwd")

    o_sb, late = _sb_fwd(sbqkv, "sb_fwd", comm=plan.late_gather() if plan else None)
    if plan:
        wts = {**wts, **plan.late_weights(late)}

    pd = _matmul(y_dn, wts["w_proj_dn"], "nn", BF16, "proj_dn")
    ps = _matmul(o_sb, wts["w_proj_sb"], "nn", BF16, "proj_sb")
    mixed = _merge_fwd(pd, ps, gl, "merge_fwd")
    x1 = _matmul(mixed, wts["w_out"], "nn", F32, "out_proj", add=x)

    n2 = _rmsnorm_fwd(x1, wts["norm2_w"], "norm2_fwd")
    upre = _matmul(n2, wts["ffn_w_up_t"], "nt", BF16, "ffn_up")
    fact = _ffn_act_fwd(upre, wts["ffn_conv_w"], "ffn_act_fwd")
    x2 = _matmul(fact, wts["ffn_w_down"], "nn", F32, "ffn_down", add=x1)

    dx2, g_normf, loss = _final_loss(x2, target, wts["norm_f_w"], "final_loss")

    dfact = _matmul(dx2, wts["ffn_w_down"], "nt", BF16, "ffn_down_dx")
    g_wdown = _matmul(fact, dx2, "tn", BF16, "ffn_down_dw")
    dgc, duc, dwg, dwu = _ffn_act_bwd(dfact, upre, wts["ffn_conv_w"], "ffn_act_bwd")
    g_fconv = jnp.concatenate([dwg, dwu], axis=1)
    dupre = _conv_bwd_data([dgc, duc], wts["ffn_conv_w"], FFN_CONV, BF16, "ffn_conv_bwd")
    dn2 = _matmul(dupre, wts["ffn_w_up_t"], "nn", F32, "ffn_up_dx")
    g_wup = _matmul(dupre, n2, "tn", BF16, "ffn_up_dw")
    dx1, g_norm2 = _rmsnorm_bwd(dn2, x1, wts["norm2_w"], dx2, "norm2_bwd")

    dmixed = _matmul(dx1, wts["w_out"], "nt", BF16, "out_proj_dx")
    g_wout = _matmul(mixed, dx1, "tn", BF16, "out_proj_dw")
    dpd, dps, dgd, dgs = _merge_bwd(dmixed, pd, ps, gl, "merge_bwd")
    dy_dn = _matmul(dpd, wts["w_proj_dn"], "nt", BF16, "proj_dn_dx")
    g_wpd = _matmul(y_dn, dpd, "tn", BF16, "proj_dn_dw")
    do_sb = _matmul(dps, wts["w_proj_sb"], "nt", BF16, "proj_sb_dx")
    g_wps = _matmul(o_sb, dps, "tn", BF16, "proj_sb_dw")
    grads = dict(w_proj_dn=g_wpd, w_proj_sb=g_wps, w_out=g_wout, ffn_w_up_t=g_wup, ffn_w_down=g_wdown)

    (dsq, dsk, dsv), got_early = _sb_bwd(sbqkv, o_sb, do_sb, "sb_bwd",
                                         comm=plan.early_grads(grads) if plan else None)

    do_dn, dhgate, g_dnnorm = _dn_post_bwd(dy_dn, o_dn, hgate, wts["dn_norm_w"], "dn_post_bwd")
    dvn, dsh = _dn_scan_bwd(do_dn, w, kd, qg, p, gates, "dn_scan_bwd")
    dq, dk, dv, dgates = _dn_local_bwd(act, gates, u, w, kd, qg, tinv, p, sh, dsh, dvn, do_dn, "dn_local_bwd")
    dhab, g_alog, g_dtb = _dn_gates_bwd(dgates, hab, wts["alog"], wts["dtb"], "dn_gates_bwd")
    dcv, g_dnconv = _dn_pre_bwd(dq, dk, dv, qkv_pre, wts["dn_conv_w"], "dn_pre_bwd")
    dqkv_pre = _conv_bwd_data([dcv], wts["dn_conv_w"], DN_CONV, BF16, "dn_conv_bwd")

    dh = jnp.concatenate([dqkv_pre, dhgate, dsq, dsk, dsv, dgd, dgs], axis=1)
    w_main_t = jnp.concatenate([wts["w_dnqkv_t"], wts["w_dngate_t"], wts["w_sbqkv_t"], wts["w_gl_t"]], axis=0)
    g_wmain = _matmul(dh, n1, "tn", BF16, "in_dw_main")
    g_wab = _matmul(dhab, n1, "tn", BF16, "in_dw_ab")
    grads.update(w_main_t=g_wmain, w_ab_t=g_wab, dn_conv_w=g_dnconv, alog=g_alog, dtb=g_dtb, dn_norm_w=g_dnnorm,
                 norm2_w=g_norm2, ffn_conv_w=g_fconv, norm_f_w=g_normf)
    got_late = []
    if plan:
        dn1, swapped = _matmul(dhab, wts["w_ab_t"], "nn", F32, "in_dx_ab", comm=plan.sibling_swap(grads))
        dn1, got_late = _matmul(dh, w_main_t, "nn", F32, "in_dx_main", add=dn1,
                                comm=plan.late_grads(swapped, grads, loss))
    else:
        dn1 = _matmul(dhab, wts["w_ab_t"], "nn", F32, "in_dx_ab")
        dn1 = _matmul(dh, w_main_t, "nn", F32, "in_dx_main", add=dn1)
    grad_x, g_norm1 = _rmsnorm_bwd(dn1, x, wts["norm1_w"], dx1, "norm1_bwd")
    grads["norm1_w"] = g_norm1
    return loss, grad_x, grads, got_early, got_late


HBM_SPEC = pl.BlockSpec(memory_space=pltpu.HBM)


def _mesh_pos():
    x, y, c = lax.axis_index("x"), lax.axis_index("y"), lax.axis_index("c")
    return x, y, c, 4 * x + 2 * y + c


def _peer(k):
    x, y, c, _ = _mesh_pos()
    px = 1 - x if k & 4 else x
    py = 1 - y if k & 2 else y
    pc = 1 - c if k & 1 else c
    return (px, py, pc), 4 * px + 2 * py + pc


def _rcopy(src, dst, send, recv, a, s, peer):
    return pltpu.make_async_remote_copy(src_ref=src, dst_ref=dst, send_sem=send.at[a, s], recv_sem=recv.at[a, s],
                                        device_id=peer, device_id_type=pl.DeviceIdType.MESH)


class _Gather:
    ICI = (2, 4, 6)

    def __init__(self, shards):
        self.args = list(shards)
        self.n = len(shards)
        self.out_shape = [jax.ShapeDtypeStruct((N_DEV,) + s.shape, s.dtype) for s in shards]
        self.scratch = [pltpu.SemaphoreType.DMA((self.n, N_DEV - 1)), pltpu.SemaphoreType.DMA((self.n, N_DEV - 1)),
                        pltpu.SemaphoreType.DMA((self.n,))]

    def _slot(self, outs, a, d):
        return outs[a].at[d]

    def _first(self, ins, outs, send, recv, a):
        me = _mesh_pos()[3]
        out, got = [], []
        for s, k in enumerate((1,) + self.ICI):
            peer, pidx = _peer(k)
            out.append(_rcopy(ins[a], self._slot(outs, a, me), send, recv, a, s, peer))
            got.append(_rcopy(ins[a], self._slot(outs, a, pidx), send, recv, a, s, peer))
        return out, got

    def _forward(self, ins, outs, send, recv, a):
        sib = _peer(1)[0]
        out, got = [], []
        for s, k in enumerate(self.ICI):
            held = self._slot(outs, a, _peer(k)[1])
            out.append(_rcopy(held, held, send, recv, a, 4 + s, sib))
            other = self._slot(outs, a, _peer(k | 1)[1])
            got.append(_rcopy(other, other, send, recv, a, 4 + s, sib))
        return out, got

    def start(self, ins, outs, sems):
        send, recv, loc = sems
        me = _mesh_pos()[3]
        for a in range(self.n):
            pltpu.make_async_copy(ins[a], self._slot(outs, a, me), loc.at[a]).start()
            for cp in self._first(ins, outs, send, recv, a)[0]:
                cp.start()

    def mid(self, ins, outs, sems):
        send, recv, _ = sems
        for a in range(self.n):
            arrivals = self._first(ins, outs, send, recv, a)[1]
            for s, cp in enumerate(self._forward(ins, outs, send, recv, a)[0]):
                arrivals[1 + s].wait_recv()
                cp.start()

    def finish(self, ins, outs, sems):
        send, recv, loc = sems
        me = _mesh_pos()[3]
        for a in range(self.n):
            first_out, first_got = self._first(ins, outs, send, recv, a)
            fwd_out, fwd_got = self._forward(ins, outs, send, recv, a)
            first_got[0].wait_recv()
            for cp in fwd_got:
                cp.wait_recv()
            for cp in first_out + fwd_out:
                cp.wait_send()
            pltpu.make_async_copy(ins[a], self._slot(outs, a, me), loc.at[a]).wait()


class _Exchange:
    def __init__(self, slabs=(), gathered=(), chip_slabs=(), sibling_slabs=()):
        self.args = list(slabs) + list(chip_slabs) + list(sibling_slabs) + list(gathered)
        self.kind = (["dev"] * len(slabs) + ["chip"] * len(chip_slabs) + ["sib"] * len(sibling_slabs)
                     + ["all"] * len(gathered))
        self.n = len(self.args)
        half = lambda s: jax.ShapeDtypeStruct((N_DEV // 2,) + s.shape[1:], s.dtype)
        self.out_shape = ([jax.ShapeDtypeStruct(s.shape, s.dtype) for s in slabs]
                          + [half(s) for s in chip_slabs] + [half(s) for s in sibling_slabs]
                          + [jax.ShapeDtypeStruct((N_DEV,) + s.shape, s.dtype) for s in gathered])
        self.scratch = [pltpu.SemaphoreType.DMA((self.n, N_DEV - 1)), pltpu.SemaphoreType.DMA((self.n, N_DEV - 1)),
                        pltpu.SemaphoreType.DMA((self.n,))]

    def _copies(self, ins, outs, send, recv, a):
        x, y, c, me = _mesh_pos()
        kind = self.kind[a]
        out, got = [], []
        if kind == "sib":
            sib = _peer(1)[0]
            for q in range(N_DEV // 2):
                out.append(_rcopy(ins[a].at[2 * q + 1 - c], outs[a].at[q], send, recv, a, q, sib))
                got.append(_rcopy(ins[a].at[2 * q + c], outs[a].at[q], send, recv, a, q, sib))
            return out, got
        for k in ((2, 4, 6) if kind == "chip" else range(1, N_DEV)):
            peer, pidx = _peer(k)
            if kind == "chip":
                src, mine, theirs = ins[a].at[2 * peer[0] + peer[1]], 2 * x + y, 2 * peer[0] + peer[1]
            else:
                src, mine, theirs = (ins[a].at[pidx] if kind == "dev" else ins[a]), me, pidx
            out.append(_rcopy(src, outs[a].at[mine], send, recv, a, k - 1, peer))
            got.append(_rcopy(src, outs[a].at[theirs], send, recv, a, k - 1, peer))
        return out, got

    def _local(self, ins, outs, loc, a):
        x, y, _, me = _mesh_pos()
        kind = self.kind[a]
        if kind == "sib":
            return None
        if kind == "chip":
            return pltpu.make_async_copy(ins[a].at[2 * x + y], outs[a].at[2 * x + y], loc.at[a])
        return pltpu.make_async_copy(ins[a].at[me] if kind == "dev" else ins[a], outs[a].at[me], loc.at[a])

    def start(self, ins, outs, sems):
        send, recv, loc = sems
        for a in range(self.n):
            if self._local(ins, outs, loc, a) is not None:
                self._local(ins, outs, loc, a).start()
            for cp in self._copies(ins, outs, send, recv, a)[0]:
                cp.start()

    def mid(self, ins, outs, sems):
        pass

    def finish(self, ins, outs, sems):
        send, recv, loc = sems
        for a in range(self.n):
            out, got = self._copies(ins, outs, send, recv, a)
            for cp in got:
                cp.wait_recv()
            for cp in out:
                cp.wait_send()
            if self._local(ins, outs, loc, a) is not None:
                self._local(ins, outs, loc, a).wait()


def _comm_call(comm, name):
    n = comm.n

    def body(*refs):
        ins, outs, sems = refs[:n], refs[n:2 * n], refs[2 * n:]
        comm.start(ins, outs, sems)
        comm.mid(ins, outs, sems)
        comm.finish(ins, outs, sems)

    return pl.pallas_call(
        body, name=name, out_shape=comm.out_shape, in_specs=[HBM_SPEC] * n, out_specs=[HBM_SPEC] * n,
        scratch_shapes=comm.scratch,
    )(*comm.args)


def _hosted(body, comm, n_in, n_out, when):
    if comm is None:
        return body

    def wrapped(*refs):
        ins, c_ins = refs[:n_in], refs[n_in:n_in + comm.n]
        o0 = n_in + comm.n
        outs, c_outs = refs[o0:o0 + n_out], refs[o0 + n_out:o0 + n_out + comm.n]
        scratch, sems = refs[o0 + n_out + comm.n:len(refs) - 3], refs[len(refs) - 3:]
        first, middle, last = when()

        @pl.when(first)
        def _():
            comm.start(c_ins, c_outs, sems)

        body(*ins, *outs, *scratch)

        @pl.when(middle)
        def _():
            comm.mid(c_ins, c_outs, sems)

        @pl.when(last)
        def _():
            comm.finish(c_ins, c_outs, sems)

    return wrapped


def _host_call(body, name, comm, when, out_shape, grid, in_specs, out_specs, scratch_shapes, sem, args):
    n_in, n_out = len(in_specs), len(out_specs)
    if comm is None:
        res = pl.pallas_call(body, name=name, out_shape=out_shape, grid=grid, in_specs=in_specs, out_specs=out_specs,
                             scratch_shapes=scratch_shapes, compiler_params=_params(sem))(*args)
        return list(res), []
    res = pl.pallas_call(
        _hosted(body, comm, n_in, n_out, when), name=name,
        out_shape=list(out_shape) + comm.out_shape, grid=grid,
        in_specs=list(in_specs) + [HBM_SPEC] * comm.n, out_specs=list(out_specs) + [HBM_SPEC] * comm.n,
        scratch_shapes=list(scratch_shapes) + comm.scratch,
        compiler_params=_params(("arbitrary",) * len(grid)),
    )(*args, *comm.args)
    return list(res[:n_out]), list(res[n_out:])


def _add_my_slabs(slabs, b, name):
    n, rows, cols = b.shape
    tc = _tile(cols, 256)

    def body(a_ref, b_ref, o_ref):
        o_ref[...] = (a_ref[...].astype(F32) + b_ref[...].astype(F32)).astype(o_ref.dtype)

    blk = pl.BlockSpec((None, rows, tc), lambda i, j: (i, 0, j))
    mine = pl.BlockSpec((None, rows, tc), lambda i, j: (2 * i + lax.axis_index("c"), 0, j))
    return pl.pallas_call(
        body, name=name, out_shape=jax.ShapeDtypeStruct(b.shape, b.dtype), grid=(n, cols // tc),
        in_specs=[mine, blk], out_specs=blk, compiler_params=_params(("parallel", "parallel")),
    )(slabs, b)


def _adamw(parts, w, m, v, name):
    rows, cols = w.shape
    nparts = parts.shape[0]
    tr, tc = rows, cols
    for cand in (128, 176):
        if rows > cand and rows % cand == 0:
            tr = cand
            break
    if tr == rows and rows > 512:
        tc = _tile(cols, 256)

    def body(p_ref, w_ref, m_ref, v_ref, g_ref, d_ref, mo_ref, vo_ref):
        g = p_ref[0].astype(F32)
        for s in range(1, nparts):
            g = g + p_ref[s].astype(F32)
        mn = ADAM_B1 * m_ref[...] + (1.0 - ADAM_B1) * g
        vn = ADAM_B2 * v_ref[...] + (1.0 - ADAM_B2) * (g * g)
        m_hat = mn / (1.0 - ADAM_B1 ** ADAM_STEP)
        v_hat = vn / (1.0 - ADAM_B2 ** ADAM_STEP)
        g_ref[...] = g
        d_ref[...] = -ADAM_LR * (m_hat / (jnp.sqrt(v_hat) + ADAM_EPS) + ADAM_WD * w_ref[...])
        mo_ref[...] = mn
        vo_ref[...] = vn

    blk = pl.BlockSpec((tr, tc), lambda i, j: (i, j))
    out = jax.ShapeDtypeStruct((rows, cols), F32)
    return pl.pallas_call(
        body, name=name,
        out_shape=(out, out, out, out),
        grid=(rows // tr, cols // tc),
        in_specs=[pl.BlockSpec((nparts, tr, tc), lambda i, j: (0, i, j)), blk, blk, blk],
        out_specs=(blk, blk, blk, blk),
        compiler_params=_params(("parallel", "parallel")),
    )(parts, w, m, v)


CONV_PACK = 8 * 1024
WEIGHT_ORDER = ("norm1_w", "w_in", "dn_conv_w", "dn_A_log", "dn_dt_bias", "dn_norm_w", "w_proj_dn", "w_proj_sb",
                "w_out", "norm2_w", "ffn_w_up", "ffn_conv_w", "ffn_w_down", "norm_f_w")


def _cols_to_slabs(g):
    r, c8 = g.shape
    return g.reshape(r, N_DEV, c8 // N_DEV).transpose(1, 0, 2)


def _slabs_to_cols(s):
    d, r, c = s.shape
    return s.transpose(1, 0, 2).reshape(r, d * c)


def kernel(x, norm1_w, w_in, dn_conv_w, dn_A_log, dn_dt_bias, dn_norm_w, w_proj_dn, w_proj_sb, w_out, norm2_w, ffn_w_up, ffn_conv_w, ffn_w_down, norm_f_w, loss_target, m_norm1_w, m_w_in, m_dn_conv_w, m_dn_A_log, m_dn_dt_bias, m_dn_norm_w, m_w_proj_dn, m_w_proj_sb, m_w_out, m_norm2_w, m_ffn_w_up, m_ffn_conv_w, m_ffn_w_down, m_norm_f_w, v_norm1_w, v_w_in, v_dn_conv_w, v_dn_A_log, v_dn_dt_bias, v_dn_norm_w, v_w_proj_dn, v_w_proj_sb, v_w_out, v_norm2_w, v_ffn_w_up, v_ffn_conv_w, v_ffn_w_down, v_norm_f_w):
    me = _mesh_pos()[3]
    tr = lambda a: jnp.transpose(a[0])
    w_loc = dict(norm1_w=norm1_w, w_in=tr(w_in), dn_conv_w=dn_conv_w[0], dn_A_log=dn_A_log, dn_dt_bias=dn_dt_bias,
                 dn_norm_w=dn_norm_w, w_proj_dn=w_proj_dn[0], w_proj_sb=w_proj_sb[0], w_out=w_out[0],
                 norm2_w=norm2_w, ffn_w_up=tr(ffn_w_up), ffn_conv_w=ffn_conv_w[0], ffn_w_down=ffn_w_down[0],
                 norm_f_w=norm_f_w[None, :])
    m_loc = dict(norm1_w=m_norm1_w, w_in=tr(m_w_in), dn_conv_w=m_dn_conv_w[0], dn_A_log=m_dn_A_log,
                 dn_dt_bias=m_dn_dt_bias, dn_norm_w=m_dn_norm_w, w_proj_dn=m_w_proj_dn[0], w_proj_sb=m_w_proj_sb[0],
                 w_out=m_w_out[0], norm2_w=m_norm2_w, ffn_w_up=tr(m_ffn_w_up), ffn_conv_w=m_ffn_conv_w[0],
                 ffn_w_down=m_ffn_w_down[0], norm_f_w=m_norm_f_w[None, :])
    v_loc = dict(norm1_w=v_norm1_w, w_in=tr(v_w_in), dn_conv_w=v_dn_conv_w[0], dn_A_log=v_dn_A_log,
                 dn_dt_bias=v_dn_dt_bias, dn_norm_w=v_dn_norm_w, w_proj_dn=v_w_proj_dn[0], w_proj_sb=v_w_proj_sb[0],
                 w_out=v_w_out[0], norm2_w=v_norm2_w, ffn_w_up=tr(v_ffn_w_up), ffn_conv_w=v_ffn_conv_w[0],
                 ffn_w_down=v_ffn_w_down[0], norm_f_w=v_norm_f_w[None, :])

    conv_flat = jnp.concatenate([w_loc["dn_conv_w"].reshape(-1), w_loc["ffn_conv_w"].reshape(-1)])
    n_dn, n_ffn = DN_CONV * 3 * WIDTH // N_DEV, FFN_CONV * 2 * D_FF // N_DEV
    conv_pack = jnp.pad(conv_flat, (0, CONV_PACK - n_dn - n_ffn)).reshape(8, 1024)
    n1, (g_in, g_conv) = _rmsnorm_fwd(x[0], norm1_w, "norm1_fwd",
                                      comm=_Gather([w_loc["w_in"].astype(BF16), conv_pack]))
    in_width = g_in.shape[0] * g_in.shape[1]
    w_in_t = g_in.reshape(in_width, D_MODEL)
    g_conv = g_conv.reshape(N_DEV, CONV_PACK)
    dn_conv_full = _slabs_to_cols(g_conv[:, :n_dn].reshape(N_DEV, DN_CONV, 3 * WIDTH // N_DEV))
    ffn_conv_full = _slabs_to_cols(g_conv[:, n_dn:n_dn + n_ffn].reshape(N_DEV, FFN_CONV, 2 * D_FF // N_DEV))
    q_end = 3 * WIDTH
    ab_end = q_end + 2 * HEADS
    gate_end = ab_end + WIDTH
    sb_end = gate_end + 3 * WIDTH
    pad_lanes = lambda a: jnp.pad(a, ((0, 0), (0, 128 - a.shape[1])))
    wts = dict(
        norm1_w=norm1_w, w_dnqkv_t=w_in_t[:q_end], w_ab_t=jnp.pad(w_in_t[q_end:ab_end], ((0, 128 - 2 * HEADS), (0, 0))),
        w_dngate_t=w_in_t[ab_end:gate_end], w_sbqkv_t=w_in_t[gate_end:sb_end], w_gl_t=w_in_t[sb_end:],
        dn_conv_w=dn_conv_full, alog=pad_lanes(dn_A_log), dtb=pad_lanes(dn_dt_bias), dn_norm_w=dn_norm_w,
        norm2_w=norm2_w, ffn_conv_w=ffn_conv_full, norm_f_w=norm_f_w[None, :])

    n_fc = FFN_CONV * 2 * D_FF
    fc_rows = -(-n_fc // D_MODEL)
    dn_rows = DN_CONV * 3 * WIDTH // D_MODEL
    late_names = ("w_proj_dn", "w_proj_sb", "w_out", "ffn_w_up", "ffn_w_down")

    class Plan:
        @staticmethod
        def late_gather():
            return _Gather([w_loc[k].astype(BF16) for k in late_names])

        @staticmethod
        def late_weights(got):
            g_pd, g_ps, g_out, g_up, g_down = got
            return dict(w_proj_dn=g_pd.reshape(WIDTH, D_MODEL), w_proj_sb=g_ps.reshape(WIDTH, D_MODEL),
                        w_out=g_out.reshape(D_MODEL, D_MODEL), ffn_w_up_t=g_up.reshape(2 * D_FF, D_MODEL),
                        ffn_w_down=g_down.reshape(D_FF, D_MODEL))

        @staticmethod
        def early_grads(g):
            return _Exchange([g["w_proj_dn"].reshape(N_DEV, WIDTH // N_DEV, D_MODEL),
                              g["w_proj_sb"].reshape(N_DEV, WIDTH // N_DEV, D_MODEL),
                              g["w_out"].reshape(N_DEV, D_MODEL // N_DEV, D_MODEL),
                              g["ffn_w_up_t"].reshape(N_DEV, 2 * D_FF // N_DEV, D_MODEL),
                              g["ffn_w_down"].reshape(N_DEV, D_FF // N_DEV, D_MODEL)])

        @staticmethod
        def _in_slabs(g):
            g_win_t = jnp.concatenate([g["w_main_t"][:q_end], g["w_ab_t"][:2 * HEADS], g["w_main_t"][q_end:]],
                                      axis=0)
            return g_win_t.reshape(N_DEV, in_width // N_DEV, D_MODEL)

        @staticmethod
        def sibling_swap(g):
            return _Exchange(sibling_slabs=[Plan._in_slabs(g)])

        @staticmethod
        def late_grads(swapped, g, loss):
            chip_sums = _add_my_slabs(Plan._in_slabs(g), swapped[0], "in_dw_chip_sum")
            row3 = jnp.concatenate([g["dn_norm_w"], g["alog"], g["dtb"], jnp.pad(loss, ((0, 0), (0, 127))),
                                    jnp.zeros((1, D_MODEL - 512), F32)], axis=1)
            fconv_rows = jnp.pad(g["ffn_conv_w"].reshape(-1), (0, fc_rows * D_MODEL - n_fc)).reshape(fc_rows, D_MODEL)
            pad8 = lambda a: jnp.pad(a, ((0, -a.shape[0] % 8), (0, 0)))
            pieces = [g["norm2_w"], g["norm_f_w"], row3, g["dn_conv_w"].reshape(dn_rows, D_MODEL), fconv_rows]
            small = jnp.concatenate([pad8(a) for a in pieces], axis=0)
            assert small.shape[0] == SMALL_ROWS
            return _Exchange(chip_slabs=[chip_sums], gathered=[small])

    loss, grad_x, g, got_early, got_late = _local_step(x[0], loss_target[0], wts, Plan, n1)
    r_pd, r_ps, r_out, r_up, r_down = got_early
    r_in, r_small = got_late
    (r_norm1,) = _comm_call(_Exchange([], [jnp.pad(g["norm1_w"], ((0, 7), (0, 0)))]), "gather_norm1")

    parts = dict(w_in=r_in, w_proj_dn=r_pd, w_proj_sb=r_ps, w_out=r_out, ffn_w_up=r_up, ffn_w_down=r_down)
    parts["norm1_w"] = r_norm1[:, 0:1, :]
    parts["norm2_w"] = r_small[:, 0:1, :]
    parts["norm_f_w"] = r_small[:, 8:9, :]
    parts["dn_norm_w"] = r_small[:, 16:17, 0:HEAD_DIM]
    parts["dn_A_log"] = r_small[:, 16:17, 128:128 + HEADS]
    parts["dn_dt_bias"] = r_small[:, 16:17, 256:256 + HEADS]
    dnc = r_small[:, 24:24 + dn_rows, :].reshape(N_DEV, DN_CONV, 3 * WIDTH)
    parts["dn_conv_w"] = lax.dynamic_slice_in_dim(dnc, me * (3 * WIDTH // N_DEV), 3 * WIDTH // N_DEV, axis=2)
    fc0 = 24 + dn_rows + (-dn_rows % 8)
    fcc = r_small[:, fc0:fc0 + fc_rows, :].reshape(N_DEV, fc_rows * D_MODEL)[:, :n_fc]
    fcc = fcc.reshape(N_DEV, FFN_CONV, 2 * D_FF)
    parts["ffn_conv_w"] = lax.dynamic_slice_in_dim(fcc, me * (2 * D_FF // N_DEV), 2 * D_FF // N_DEV, axis=2)
    loss_total = jnp.sum(r_small[:, 16, 384])

    res = {k: _adamw(parts[k], w_loc[k], m_loc[k], v_loc[k], "adamw_" + k) for k in WEIGHT_ORDER}
    lead = ("w_in", "dn_conv_w", "w_proj_dn", "w_proj_sb", "w_out", "ffn_w_up", "ffn_conv_w", "ffn_w_down")

    def shaped(k, a):
        if k in ("w_in", "ffn_w_up"):
            return jnp.transpose(a)[None]
        if k in lead:
            return a[None]
        if k == "norm_f_w":
            return a[0]
        return a

    outs = [loss_total, grad_x[None]]
    for idx in range(4):
        outs += [shaped(k, res[k][idx]) for k in WEIGHT_ORDER]
    return tuple(outs)
```

```python
import functools

import jax
import jax.numpy as jnp
from jax import lax
from jax.experimental import pallas as pl
from jax.experimental.pallas import tpu as pltpu

F32 = jnp.float32
BF16 = jnp.bfloat16

N_DEV = 8
D_MODEL = 1024
HEADS = 8
HEAD_DIM = 128
WIDTH = HEADS * HEAD_DIM
DN_CONV = 4
DN_CHUNK = 64
D_FF = 2816
FFN_CONV = 3
EPS = 1e-6
HALO = 16
CHUNK_ROWS = 256
ATT_BLOCK = 256
SB_LOG_ZERO = -104.0
SB_GROUP = 2
SB_HEADS_FWD = 4
SB_HEADS_BWD = 2
SMALL_ROWS = 64

ADAM_LR = 0.001
ADAM_B1 = 0.9
ADAM_B2 = 0.999
ADAM_EPS = 1e-08
ADAM_WD = 0.01
ADAM_STEP = 10

VMEM_LIMIT = 48 * 1024 * 1024


def _params(sem=None, **kw):
    return pltpu.CompilerParams(dimension_semantics=sem, vmem_limit_bytes=VMEM_LIMIT, **kw)


def _tile(n, cap):
    if n <= cap:
        return n
    best = None
    for t in range(128, cap + 1, 128):
        if n % t == 0:
            best = t
    assert best is not None, (n, cap)
    return best


def _dot(a, b, dims):
    return lax.dot_general(a, b, ((dims[0], dims[1]), ((), ())), preferred_element_type=F32)


NN = ((1,), (0,))
NT = ((1,), (1,))
TN = ((0,), (0,))


def _dotb(a, b, dims):
    return _dot(a.astype(BF16), b.astype(BF16), dims)


def _split3(x):
    h1 = x.astype(BF16)
    r1 = x - h1.astype(F32)
    h2 = r1.astype(BF16)
    r2 = r1 - h2.astype(F32)
    return h1, h2, r2.astype(BF16)


def _dot_xr(a, b_exact, dims):
    a1, a2, a3 = _split3(a)
    return _dot(a1, b_exact, dims) + _dot(a2, b_exact, dims) + _dot(a3, b_exact, dims)


def _split2(x):
    h1 = x.astype(BF16)
    return h1, (x - h1.astype(F32)).astype(BF16)


def _dot_xr2(a, b_exact, dims):
    a1, a2 = _split2(a)
    return _dot(a1, b_exact, dims) + _dot(a2, b_exact, dims)


def _dot_xl(a_exact, b, dims):
    b1, b2, b3 = _split3(b)
    return _dot(a_exact, b1, dims) + _dot(a_exact, b2, dims) + _dot(a_exact, b3, dims)


def _dot3(a, b, dims):
    a1 = a.astype(BF16)
    a2 = (a - a1.astype(F32)).astype(BF16)
    b1 = b.astype(BF16)
    b2 = (b - b1.astype(F32)).astype(BF16)
    return _dot(a1, b1, dims) + (_dot(a1, b2, dims) + _dot(a2, b1, dims))


def _sigmoid(x):
    return 1.0 / (1.0 + jnp.exp(-x))


def _log1pexp_neg_abs(x):
    return jnp.log(1.0 + jnp.exp(-jnp.abs(x)))


def _iota(shape, dim):
    return lax.broadcasted_iota(jnp.int32, shape, dim)


def _matmul(a, b, mode, out_dtype, name, add=None, comm=None):
    if mode == "nn":
        (m, k), (k2, n) = a.shape, b.shape
    elif mode == "nt":
        (m, k), (n, k2) = a.shape, b.shape
    else:
        (k, m), (k2, n) = a.shape, b.shape
    assert k == k2, (a.shape, b.shape, mode)
    tm, tn, tk = _tile(m, 1408), _tile(n, 1408), _tile(k, 1536)
    nk = k // tk
    dims = {"nn": NN, "nt": NT, "tn": TN}[mode]

    def body(*refs):
        if add is None:
            a_ref, b_ref, o_ref, acc_ref = refs
        else:
            a_ref, b_ref, add_ref, o_ref, acc_ref = refs
        kk = pl.program_id(2)

        @pl.when(kk == 0)
        def _():
            acc_ref[...] = jnp.zeros_like(acc_ref)

        acc_ref[...] += _dotb(a_ref[...], b_ref[...], dims)

        @pl.when(kk == nk - 1)
        def _():
            r = acc_ref[...]
            if add is not None:
                r = r + add_ref[...].astype(F32)
            o_ref[...] = r.astype(out_dtype)

    if mode == "nn":
        specs = [pl.BlockSpec((tm, tk), lambda i, j, l: (i, l)), pl.BlockSpec((tk, tn), lambda i, j, l: (l, j))]
    elif mode == "nt":
        specs = [pl.BlockSpec((tm, tk), lambda i, j, l: (i, l)), pl.BlockSpec((tn, tk), lambda i, j, l: (j, l))]
    else:
        specs = [pl.BlockSpec((tk, tm), lambda i, j, l: (l, i)), pl.BlockSpec((tk, tn), lambda i, j, l: (l, j))]
    args = [a, b]
    if add is not None:
        specs.append(pl.BlockSpec((tm, tn), lambda i, j, l: (i, j)))
        args.append(add)
    grid = (m // tm, n // tn, nk)

    def when():
        i, j, l = pl.program_id(0), pl.program_id(1), pl.program_id(2)
        first = jnp.logical_and(jnp.logical_and(i == 0, j == 0), l == 0)
        last = jnp.logical_and(jnp.logical_and(i == grid[0] - 1, j == grid[1] - 1), l == nk - 1)
        return first, last, last

    (out,), extra = _host_call(
        body, name, comm, when, [jax.ShapeDtypeStruct((m, n), out_dtype)], grid, specs,
        [pl.BlockSpec((tm, tn), lambda i, j, l: (i, j))], [pltpu.VMEM((tm, tn), F32)],
        ("parallel", "parallel", "arbitrary"), args)
    return out if comm is None else (out, extra)


def _rmsnorm_fwd(x, w, name, comm=None):
    t, d = x.shape
    tr = _tile(t, 512)
    steps = t // tr

    def body(x_ref, w_ref, o_ref):
        xv = x_ref[...]
        r = lax.rsqrt(jnp.mean(xv * xv, axis=1, keepdims=True) + EPS)
        o_ref[...] = (xv * r * w_ref[...]).astype(BF16)

    def when():
        i = pl.program_id(0)
        return i == 0, i == steps // 2, i == steps - 1

    (out,), extra = _host_call(
        body, name, comm, when, [jax.ShapeDtypeStruct((t, d), BF16)], (steps,),
        [pl.BlockSpec((tr, d), lambda i: (i, 0)), pl.BlockSpec((1, d), lambda i: (0, 0))],
        [pl.BlockSpec((tr, d), lambda i: (i, 0))], [], ("parallel",), (x, w))
    return out if comm is None else (out, extra)


def _rmsnorm_bwd(dn, x, w, dres, name):
    t, d = x.shape
    tr = _tile(t, 512)

    def body(dn_ref, x_ref, w_ref, dres_ref, dx_ref, dw_ref):
        i = pl.program_id(0)
        xv = x_ref[...]
        g = dn_ref[...].astype(F32)
        r = lax.rsqrt(jnp.mean(xv * xv, axis=1, keepdims=True) + EPS)
        xh = xv * r
        dxh = g * w_ref[...]
        dx = r * (dxh - xh * jnp.mean(dxh * xh, axis=1, keepdims=True))
        dx_ref[...] = dres_ref[...] + dx

        @pl.when(i == 0)
        def _():
            dw_ref[...] = jnp.zeros_like(dw_ref)

        dw_ref[...] += jnp.sum(g * xh, axis=0, keepdims=True)

    return pl.pallas_call(
        body, name=name,
        out_shape=(jax.ShapeDtypeStruct((t, d), F32), jax.ShapeDtypeStruct((1, d), F32)),
        grid=(t // tr,),
        in_specs=[pl.BlockSpec((tr, d), lambda i: (i, 0)), pl.BlockSpec((tr, d), lambda i: (i, 0)),
                  pl.BlockSpec((1, d), lambda i: (0, 0)), pl.BlockSpec((tr, d), lambda i: (i, 0))],
        out_specs=(pl.BlockSpec((tr, d), lambda i: (i, 0)), pl.BlockSpec((1, d), lambda i: (0, 0))),
        compiler_params=_params(("arbitrary",)),
    )(dn, x, w, dres)


def _final_loss(x2, target, w, name):
    t, d = x2.shape
    tr = _tile(t, 512)

    def body(x_ref, t_ref, w_ref, dx_ref, dw_ref, loss_ref):
        i = pl.program_id(0)
        xv = x_ref[...]
        r = lax.rsqrt(jnp.mean(xv * xv, axis=1, keepdims=True) + EPS)
        xh = xv * r
        err = xh * w_ref[...] - t_ref[...]
        dy = err * (1.0 / d)
        dxh = dy * w_ref[...]
        dx_ref[...] = r * (dxh - xh * jnp.mean(dxh * xh, axis=1, keepdims=True))

        @pl.when(i == 0)
        def _():
            dw_ref[...] = jnp.zeros_like(dw_ref)
            loss_ref[...] = jnp.zeros_like(loss_ref)

        dw_ref[...] += jnp.sum(dy * xh, axis=0, keepdims=True)
        row = jnp.sum(err * err, axis=1, keepdims=True) * (0.5 / d)
        loss_ref[...] += jnp.sum(row, axis=0, keepdims=True)

    return pl.pallas_call(
        body, name=name,
        out_shape=(jax.ShapeDtypeStruct((t, d), F32), jax.ShapeDtypeStruct((1, d), F32),
                   jax.ShapeDtypeStruct((1, 1), F32)),
        grid=(t // tr,),
        in_specs=[pl.BlockSpec((tr, d), lambda i: (i, 0)), pl.BlockSpec((tr, d), lambda i: (i, 0)),
                  pl.BlockSpec((1, d), lambda i: (0, 0))],
        out_specs=(pl.BlockSpec((tr, d), lambda i: (i, 0)), pl.BlockSpec((1, d), lambda i: (0, 0)),
                   pl.BlockSpec((1, 1), lambda i: (0, 0))),
        compiler_params=_params(("arbitrary",)),
    )(x2, target, w)


def _shift_down(cur, prev, k, row):
    r = pltpu.roll(cur, k, 0)
    top, row8 = r[0:8, :], row[0:8, :]
    for m in range(k):
        top = jnp.where(row8 == m, prev[HALO - k + m:HALO - k + m + 1, :], top)
    return jnp.concatenate([top, r[8:, :]], axis=0)


def _shift_up(cur, nxt, k, row, tr):
    r = pltpu.roll(cur, tr - k, 0)
    bottom, row8 = r[tr - 8:, :], row[0:8, :]
    for m in range(k):
        bottom = jnp.where(row8 == 8 - k + m, nxt[m:m + 1, :], bottom)
    return jnp.concatenate([r[:tr - 8, :], bottom], axis=0)


def _fold8(a):
    out = a[0:8, :]
    for r in range(8, a.shape[0], 8):
        out = out + a[r:r + 8, :]
    return out


def _conv_taps(cur, prev, w, ntaps, row):
    taps = [cur if i == ntaps - 1 else _shift_down(cur, prev, ntaps - 1 - i, row) for i in range(ntaps)]
    y = w[0:1, :] * taps[0]
    for i in range(1, ntaps):
        y = y + w[i:i + 1, :] * taps[i]
    return taps, y


def _conv_bwd_data(parts, w, ntaps, out_dtype, name):
    t, chp = parts[0].shape
    npart = len(parts)
    tr, tc = _tile(t, 512), _tile(chp, 1408)
    nc = chp // tc
    nhalo = t // HALO
    last = t // tr - 1

    def body(*refs):
        cur_refs, nxt_refs = refs[:npart], refs[npart:2 * npart]
        w_ref, o_ref = refs[2 * npart], refs[2 * npart + 1]
        i, j = pl.program_id(0), pl.program_id(1)
        row = _iota((tr, 128), 0)
        for c0 in range(0, tc, 128):
            sl = slice(c0, c0 + 128)
            cur, nxt = cur_refs[0][:, sl].astype(F32), nxt_refs[0][:, sl].astype(F32)
            for p in range(1, npart):
                cur = jnp.where(j >= p * nc, cur_refs[p][:, sl].astype(F32), cur)
                nxt = jnp.where(j >= p * nc, nxt_refs[p][:, sl].astype(F32), nxt)
            nxt = jnp.where(i == last, 0.0, nxt)
            wv = w_ref[:, sl]
            y = wv[ntaps - 1:ntaps, :] * cur
            for k in range(1, ntaps):
                y = y + wv[ntaps - 1 - k:ntaps - k, :] * _shift_up(cur, nxt, k, row, tr)
            o_ref[:, sl] = y.astype(out_dtype)

    col = lambda p: (lambda j: jnp.clip(j - p * nc, 0, nc - 1))
    cur_specs = [pl.BlockSpec((tr, tc), lambda i, j, c=col(p): (i, c(j))) for p in range(npart)]
    nxt_specs = [pl.BlockSpec((HALO, tc),
                              lambda i, j, c=col(p): (jnp.minimum((i + 1) * (tr // HALO), nhalo - 1), c(j)))
                 for p in range(npart)]
    return pl.pallas_call(
        body, name=name,
        out_shape=jax.ShapeDtypeStruct((t, npart * chp), out_dtype),
        grid=(t // tr, npart * nc),
        in_specs=cur_specs + nxt_specs + [pl.BlockSpec((ntaps, tc), lambda i, j: (0, j))],
        out_specs=pl.BlockSpec((tr, tc), lambda i, j: (i, j)),
        compiler_params=_params(("parallel", "parallel")),
    )(*parts, *parts, w)


def _ffn_act_fwd(upre, cw, name):
    t = upre.shape[0]
    tr, tc = _tile(t, 512), _tile(D_FF, 1408)
    nj = D_FF // tc

    def body(g_ref, gp_ref, u_ref, up_ref, wg_ref, wu_ref, o_ref):
        i = pl.program_id(0)
        row = _iota((tr, 128), 0)
        for c0 in range(0, tc, 128):
            sl = slice(c0, c0 + 128)
            gp = jnp.where(i == 0, 0.0, gp_ref[:, sl].astype(F32))
            up = jnp.where(i == 0, 0.0, up_ref[:, sl].astype(F32))
            _, gc = _conv_taps(g_ref[:, sl].astype(F32), gp, wg_ref[:, sl], FFN_CONV, row)
            _, uc = _conv_taps(u_ref[:, sl].astype(F32), up, wu_ref[:, sl], FFN_CONV, row)
            o_ref[:, sl] = (gc * _sigmoid(gc) * uc).astype(BF16)

    prev = lambda off: (lambda i, j: (jnp.maximum(i * (tr // HALO) - 1, 0), j + off))
    return pl.pallas_call(
        body, name=name,
        out_shape=jax.ShapeDtypeStruct((t, D_FF), BF16),
        grid=(t // tr, nj),
        in_specs=[pl.BlockSpec((tr, tc), lambda i, j: (i, j)), pl.BlockSpec((HALO, tc), prev(0)),
                  pl.BlockSpec((tr, tc), lambda i, j: (i, j + nj)), pl.BlockSpec((HALO, tc), prev(nj)),
                  pl.BlockSpec((FFN_CONV, tc), lambda i, j: (0, j)),
                  pl.BlockSpec((FFN_CONV, tc), lambda i, j: (0, j + nj))],
        out_specs=pl.BlockSpec((tr, tc), lambda i, j: (i, j)),
        compiler_params=_params(("parallel", "parallel")),
    )(upre, upre, upre, upre, cw, cw)


def _ffn_act_bwd(dact, upre, cw, name):
    t = upre.shape[0]
    tr, tc = _tile(t, 512), _tile(D_FF, 1408)
    nj = D_FF // tc

    def body(da_ref, g_ref, gp_ref, u_ref, up_ref, wg_ref, wu_ref, dg_ref, du_ref, dwg_ref, dwu_ref):
        i = pl.program_id(1)
        row = _iota((CHUNK_ROWS, 128), 0)

        @pl.when(i == 0)
        def _():
            dwg_ref[...] = jnp.zeros_like(dwg_ref)
            dwu_ref[...] = jnp.zeros_like(dwu_ref)

        for c0 in range(0, tc, 128):
            sl = slice(c0, c0 + 128)
            wg, wu = wg_ref[:, sl], wu_ref[:, sl]
            dwg = [jnp.zeros((8, 128), F32)] * FFN_CONV
            dwu = [jnp.zeros((8, 128), F32)] * FFN_CONV
            for r0 in range(0, tr, CHUNK_ROWS):
                rows = slice(r0, r0 + CHUNK_ROWS)
                if r0 == 0:
                    gp = jnp.where(i == 0, 0.0, gp_ref[:, sl].astype(F32))
                    up = jnp.where(i == 0, 0.0, up_ref[:, sl].astype(F32))
                else:
                    gp = g_ref[r0 - HALO:r0, sl].astype(F32)
                    up = u_ref[r0 - HALO:r0, sl].astype(F32)
                gt, gc = _conv_taps(g_ref[rows, sl].astype(F32), gp, wg, FFN_CONV, row)
                ut, uc = _conv_taps(u_ref[rows, sl].astype(F32), up, wu, FFN_CONV, row)
                da = da_ref[rows, sl].astype(F32)
                sg = _sigmoid(gc)
                dgc = da * uc * (sg * (1.0 + gc * (1.0 - sg)))
                duc = da * (gc * sg)
                dg_ref[rows, sl] = dgc.astype(BF16)
                du_ref[rows, sl] = duc.astype(BF16)
                dwg = [dwg[k] + _fold8(dgc * gt[k]) for k in range(FFN_CONV)]
                dwu = [dwu[k] + _fold8(duc * ut[k]) for k in range(FFN_CONV)]
            for k in range(FFN_CONV):
                dwg_ref[k:k + 1, sl] += jnp.sum(dwg[k], axis=0, keepdims=True)
                dwu_ref[k:k + 1, sl] += jnp.sum(dwu[k], axis=0, keepdims=True)

    prev = lambda off: (lambda j, i: (jnp.maximum(i * (tr // HALO) - 1, 0), j + off))
    blk = lambda off: pl.BlockSpec((tr, tc), lambda j, i: (i, j + off))
    wblk = lambda off: pl.BlockSpec((FFN_CONV, tc), lambda j, i: (0, j + off))
    dgc, duc, dwg, dwu = pl.pallas_call(
        body, name=name,
        out_shape=(jax.ShapeDtypeStruct((t, D_FF), BF16), jax.ShapeDtypeStruct((t, D_FF), BF16),
                   jax.ShapeDtypeStruct((FFN_CONV, D_FF), F32), jax.ShapeDtypeStruct((FFN_CONV, D_FF), F32)),
        grid=(nj, t // tr),
        in_specs=[blk(0), blk(0), pl.BlockSpec((HALO, tc), prev(0)), blk(nj), pl.BlockSpec((HALO, tc), prev(nj)),
                  wblk(0), wblk(nj)],
        out_specs=(blk(0), blk(0), wblk(0), wblk(0)),
        compiler_params=_params(("parallel", "arbitrary")),
    )(dact, upre, upre, upre, upre, cw, cw)
    return dgc, duc, dwg, dwu


def _dn_pre_fwd(qkv_pre, cw, name):
    t = qkv_pre.shape[0]
    tr = _tile(t, 512)
    scale = HEAD_DIM ** -0.5

    def body(x_ref, p_ref, w_ref, o_ref):
        i, j = pl.program_id(0), pl.program_id(1)
        row = _iota((tr, HEAD_DIM), 0)
        for h in range(HEADS):
            sl = slice(h * HEAD_DIM, (h + 1) * HEAD_DIM)
            prev = jnp.where(i == 0, 0.0, p_ref[:, sl].astype(F32))
            _, c = _conv_taps(x_ref[:, sl].astype(F32), prev, w_ref[:, sl], DN_CONV, row)
            s = c * _sigmoid(c)
            r = lax.rsqrt(jnp.sum(s * s, axis=1, keepdims=True) + EPS)
            o_ref[:, sl] = s * jnp.where(j == 0, r * scale, jnp.where(j == 1, r, 1.0))

    return pl.pallas_call(
        body, name=name,
        out_shape=jax.ShapeDtypeStruct((t, 3 * WIDTH), F32),
        grid=(t // tr, 3),
        in_specs=[pl.BlockSpec((tr, WIDTH), lambda i, j: (i, j)),
                  pl.BlockSpec((HALO, WIDTH), lambda i, j: (jnp.maximum(i * (tr // HALO) - 1, 0), j)),
                  pl.BlockSpec((DN_CONV, WIDTH), lambda i, j: (0, j))],
        out_specs=pl.BlockSpec((tr, WIDTH), lambda i, j: (i, j)),
        compiler_params=_params(("parallel", "parallel")),
    )(qkv_pre, qkv_pre, cw)


def _dn_pre_bwd(dq, dk, dv, qkv_pre, cw, name):
    t = qkv_pre.shape[0]
    tr = _tile(t, 512)
    scale = HEAD_DIM ** -0.5

    def body(dq_ref, dk_ref, dv_ref, x_ref, p_ref, w_ref, dc_ref, dw_ref):
        j, i = pl.program_id(0), pl.program_id(1)
        row = _iota((CHUNK_ROWS, HEAD_DIM), 0)

        @pl.when(i == 0)
        def _():
            dw_ref[...] = jnp.zeros_like(dw_ref)

        for h in range(HEADS):
            sl = slice(h * HEAD_DIM, (h + 1) * HEAD_DIM)
            wv = w_ref[:, sl]
            dw = [jnp.zeros((8, HEAD_DIM), F32)] * DN_CONV
            for r0 in range(0, tr, CHUNK_ROWS):
                rows = slice(r0, r0 + CHUNK_ROWS)
                if r0 == 0:
                    prev = jnp.where(i == 0, 0.0, p_ref[:, sl].astype(F32))
                else:
                    prev = x_ref[r0 - HALO:r0, sl].astype(F32)
                taps, c = _conv_taps(x_ref[rows, sl].astype(F32), prev, wv, DN_CONV, row)
                d = jnp.where(j == 0, dq_ref[rows, sl] * scale, jnp.where(j == 1, dk_ref[rows, sl], dv_ref[rows, sl]))
                sg = _sigmoid(c)
                s = c * sg
                r = lax.rsqrt(jnp.sum(s * s, axis=1, keepdims=True) + EPS)
                nh = s * r
                ds_norm = r * (d - nh * jnp.sum(nh * d, axis=1, keepdims=True))
                dc = jnp.where(j < 2, ds_norm, d) * (sg * (1.0 + c * (1.0 - sg)))
                dc_ref[rows, sl] = dc.astype(BF16)
                dw = [dw[k] + _fold8(dc * taps[k]) for k in range(DN_CONV)]
            for k in range(DN_CONV):
                dw_ref[k:k + 1, sl] += jnp.sum(dw[k], axis=0, keepdims=True)

    dspec = lambda p: pl.BlockSpec((tr, WIDTH), lambda j, i: (jnp.where(j == p, i, 0), 0))
    return pl.pallas_call(
        body, name=name,
        out_shape=(jax.ShapeDtypeStruct((t, 3 * WIDTH), BF16), jax.ShapeDtypeStruct((DN_CONV, 3 * WIDTH), F32)),
        grid=(3, t // tr),
        in_specs=[dspec(0), dspec(1), dspec(2),
                  pl.BlockSpec((tr, WIDTH), lambda j, i: (i, j)),
                  pl.BlockSpec((HALO, WIDTH), lambda j, i: (jnp.maximum(i * (tr // HALO) - 1, 0), j)),
                  pl.BlockSpec((DN_CONV, WIDTH), lambda j, i: (0, j))],
        out_specs=(pl.BlockSpec((tr, WIDTH), lambda j, i: (i, j)),
                   pl.BlockSpec((DN_CONV, WIDTH), lambda j, i: (0, j))),
        compiler_params=_params(("parallel", "arbitrary")),
    )(dq, dk, dv, qkv_pre, qkv_pre, cw)


def _tri(n, kind):
    r, c = _iota((n, n), 0), _iota((n, n), 1)
    m = {"lower": r >= c, "strict": r > c, "upper": r <= c}[kind]
    return m


GATE_ROWS = 4 * DN_CHUNK


def _chunk_tri(kind):
    r, c = _iota((GATE_ROWS, GATE_ROWS), 0), _iota((GATE_ROWS, GATE_ROWS), 1)
    same = (r // DN_CHUNK) == (c // DN_CHUNK)
    return jnp.where(jnp.logical_and(same, _tri(GATE_ROWS, kind)), 1.0, 0.0).astype(BF16)


def _dn_gates_fwd(hab, alog, dtb, name):
    t = hab.shape[0]
    cc = GATE_ROWS

    def body(h_ref, al_ref, dt_ref, o_ref):
        hv = h_ref[...]
        lane = _iota(hv.shape, 1)
        xa = hv + dt_ref[...]
        sp = jnp.maximum(xa, 0.0) + _log1pexp_neg_abs(xa)
        g = jnp.where(lane < HEADS, -jnp.exp(al_ref[...]) * sp, 0.0)
        gc = _dot_xl(_chunk_tri("lower"), g, NN)
        o_ref[...] = jnp.where(lane < HEADS, gc, jnp.where(lane < 2 * HEADS, _sigmoid(hv), 0.0))

    return pl.pallas_call(
        body, name=name,
        out_shape=jax.ShapeDtypeStruct((t, 128), F32),
        grid=(t // cc,),
        in_specs=[pl.BlockSpec((cc, 128), lambda i: (i, 0)), pl.BlockSpec((1, 128), lambda i: (0, 0)),
                  pl.BlockSpec((1, 128), lambda i: (0, 0))],
        out_specs=pl.BlockSpec((cc, 128), lambda i: (i, 0)),
        compiler_params=_params(("parallel",)),
    )(hab, alog, dtb)


def _dn_gates_bwd(dgates, hab, alog, dtb, name):
    t = hab.shape[0]
    cc = GATE_ROWS

    def body(d_ref, h_ref, al_ref, dt_ref, o_ref, dal_ref, ddt_ref):
        i = pl.program_id(0)
        hv = h_ref[...]
        dv = d_ref[...]
        lane = _iota(hv.shape, 1)
        dg = _dot_xl(_chunk_tri("upper"), jnp.where(lane < HEADS, dv, 0.0), NN)
        xa = hv + dt_ref[...]
        sp = jnp.maximum(xa, 0.0) + _log1pexp_neg_abs(xa)
        ea = jnp.exp(al_ref[...])
        da = jnp.where(lane < HEADS, dg * (-ea) * _sigmoid(xa), 0.0)
        be = _sigmoid(hv)
        db = dv * be * (1.0 - be)
        o_ref[...] = jnp.where(lane < HEADS, da, jnp.where(lane < 2 * HEADS, db, 0.0))

        @pl.when(i == 0)
        def _():
            dal_ref[...] = jnp.zeros_like(dal_ref)
            ddt_ref[...] = jnp.zeros_like(ddt_ref)

        dal_ref[...] += jnp.sum(jnp.where(lane < HEADS, dg * (-ea) * sp, 0.0), axis=0, keepdims=True)
        ddt_ref[...] += jnp.sum(da, axis=0, keepdims=True)

    return pl.pallas_call(
        body, name=name,
        out_shape=(jax.ShapeDtypeStruct((t, 128), F32), jax.ShapeDtypeStruct((1, 128), F32),
                   jax.ShapeDtypeStruct((1, 128), F32)),
        grid=(t // cc,),
        in_specs=[pl.BlockSpec((cc, 128), lambda i: (i, 0)), pl.BlockSpec((cc, 128), lambda i: (i, 0)),
                  pl.BlockSpec((1, 128), lambda i: (0, 0)), pl.BlockSpec((1, 128), lambda i: (0, 0))],
        out_specs=(pl.BlockSpec((cc, 128), lambda i: (i, 0)), pl.BlockSpec((1, 128), lambda i: (0, 0)),
                   pl.BlockSpec((1, 128), lambda i: (0, 0))),
        compiler_params=_params(("arbitrary",)),
    )(dgates, hab, alog, dtb)


def _dn_chunk_common(gates, h):
    cc = DN_CHUNK
    lane = _iota(gates.shape, 1)
    gh = jnp.where(lane == h, gates, 0.0)
    gc_col = jnp.sum(gh, axis=1, keepdims=True)
    gc_row = _dot_xl(jnp.ones((cc, 128), BF16), gh, NT)
    beta = jnp.sum(jnp.where(lane == h + HEADS, gates, 0.0), axis=1, keepdims=True)
    lower = _tri(cc, "lower")
    decay = jnp.where(lower, jnp.exp(jnp.where(lower, gc_col - gc_row, 0.0)), 0.0)
    gc_last = gc_col[cc - 1:cc, :]
    return gc_col, gc_last, beta, decay


def _dn_local_fwd(act, gates, name):
    t = act.shape[0]
    cc = DN_CHUNK
    nc = t // cc

    def body(q_ref, k_ref, v_ref, g_ref, u_ref, w_ref, kd_ref, qg_ref, ti_ref, p_ref):
        gates = g_ref[...]
        eye = jnp.where(_iota((cc, cc), 0) == _iota((cc, cc), 1), 1.0, 0.0)
        hs = range(HEADS)
        sl = [slice(h * HEAD_DIM, (h + 1) * HEAD_DIM) for h in hs]
        q, k, v = ([r[:, s] for s in sl] for r in (q_ref, k_ref, v_ref))
        gc_col, gc_last, beta, decay = zip(*[_dn_chunk_common(gates, h) for h in hs])
        gam = [jnp.exp(g) for g in gc_col]
        kb = [k[h] * beta[h] for h in hs]
        npow = [-jnp.where(_tri(cc, "strict"), _dotb(kb[h], k[h], NT) * decay[h], 0.0) for h in hs]
        tinv = [eye + n for n in npow]
        for _ in range(5):
            npow = [_dot3(n, n, NN) for n in npow]
            tinv = [t + _dot3(t, n, NN) for t, n in zip(tinv, npow)]
        uu = [_dot3(tinv[h], v[h] * beta[h], NN) for h in hs]
        ww = [_dot3(tinv[h], kb[h] * gam[h], NN) for h in hs]
        pp = [jnp.where(_tri(cc, "lower"), _dotb(q[h], k[h], NT) * decay[h], 0.0) for h in hs]
        for h in hs:
            u_ref[:, sl[h]] = uu[h]
            w_ref[:, sl[h]] = ww[h].astype(BF16)
            kd_ref[:, sl[h]] = k[h] * jnp.exp(gc_last[h] - gc_col[h])
            qg_ref[:, sl[h]] = q[h] * gam[h]
            ti_ref[h] = tinv[h]
            p_ref[h] = pp[h]

    row = lambda off: pl.BlockSpec((cc, WIDTH), lambda n: (n, off))
    mat = pl.BlockSpec((HEADS, cc, cc), lambda n: (0, n, 0))
    tw, tb = jax.ShapeDtypeStruct((t, WIDTH), F32), jax.ShapeDtypeStruct((t, WIDTH), BF16)
    hm = jax.ShapeDtypeStruct((HEADS, t, cc), F32)
    return pl.pallas_call(
        body, name=name,
        out_shape=(tw, tb, tw, tw, hm, hm),
        grid=(nc,),
        in_specs=[row(0), row(1), row(2), pl.BlockSpec((cc, 128), lambda n: (n, 0))],
        out_specs=(row(0), row(0), row(0), row(0), mat, mat),
        compiler_params=_params(("parallel",)),
    )(act, act, act, gates)


def _dn_scan_fwd(u, w, kd, qg, p, gates, name):
    t = u.shape[0]
    cc = DN_CHUNK
    nc = t // cc

    def body(u_ref, w_ref, kd_ref, qg_ref, p_ref, g_ref, o_ref, sh_ref, s_ref):
        n = pl.program_id(0)

        @pl.when(n == 0)
        def _():
            s_ref[...] = jnp.zeros_like(s_ref)

        glast = jnp.exp(g_ref[cc - 1:cc, :])
        hs = range(HEADS)
        sl = [slice(h * HEAD_DIM, (h + 1) * HEAD_DIM) for h in hs]
        s = [s_ref[h] for h in hs]
        sb = [a.astype(BF16) for a in s]
        vn = [u_ref[:, sl[h]] - _dot(w_ref[:, sl[h]].astype(BF16), sb[h], NN) for h in hs]
        vnb = [a.astype(BF16) for a in vn]
        o_state = [_dot(qg_ref[:, sl[h]].astype(BF16), sb[h], NN) for h in hs]
        o_local = [_dot(p_ref[h].astype(BF16), vnb[h], NN) for h in hs]
        s_add = [_dot(kd_ref[:, sl[h]].astype(BF16), vnb[h], TN) for h in hs]
        for h in hs:
            o_ref[:, sl[h]] = o_state[h] + o_local[h]
            sh_ref[0, h] = s[h]
            s_ref[h] = glast[:, h:h + 1] * s[h] + s_add[h]

    row = pl.BlockSpec((cc, WIDTH), lambda n: (n, 0))
    return pl.pallas_call(
        body, name=name,
        out_shape=(jax.ShapeDtypeStruct((t, WIDTH), F32),
                   jax.ShapeDtypeStruct((nc, HEADS, HEAD_DIM, HEAD_DIM), F32)),
        grid=(nc,),
        in_specs=[row, row, row, row, pl.BlockSpec((HEADS, cc, cc), lambda n: (0, n, 0)),
                  pl.BlockSpec((cc, 128), lambda n: (n, 0))],
        out_specs=(row, pl.BlockSpec((1, HEADS, HEAD_DIM, HEAD_DIM), lambda n: (n, 0, 0, 0))),
        scratch_shapes=[pltpu.VMEM((HEADS, HEAD_DIM, HEAD_DIM), F32)],
        compiler_params=_params(("arbitrary",)),
    )(u, w, kd, qg, p, gates)


def _dn_scan_bwd(do, w, kd, qg, p, gates, name):
    t = do.shape[0]
    cc = DN_CHUNK
    nc = t // cc

    def body(do_ref, w_ref, kd_ref, qg_ref, p_ref, g_ref, dvn_ref, dsh_ref, ds_ref):
        n = pl.program_id(0)

        @pl.when(n == 0)
        def _():
            ds_ref[...] = jnp.zeros_like(ds_ref)

        glast = jnp.exp(g_ref[cc - 1:cc, :])
        hs = range(HEADS)
        sl = [slice(h * HEAD_DIM, (h + 1) * HEAD_DIM) for h in hs]
        ds = [ds_ref[h] for h in hs]
        dob = [do_ref[:, sl[h]].astype(BF16) for h in hs]
        dvn = [_dot(p_ref[h].astype(BF16), dob[h], TN) + _dot(kd_ref[:, sl[h]].astype(BF16), ds[h].astype(BF16), NN)
               for h in hs]
        ds_q = [_dot(qg_ref[:, sl[h]].astype(BF16), dob[h], TN) for h in hs]
        ds_w = [_dot(w_ref[:, sl[h]].astype(BF16), dvn[h].astype(BF16), TN) for h in hs]
        for h in hs:
            dvn_ref[:, sl[h]] = dvn[h]
            dsh_ref[0, h] = ds[h]
            ds_ref[h] = ds_q[h] + glast[:, h:h + 1] * ds[h] - ds_w[h]

    row = pl.BlockSpec((cc, WIDTH), lambda n: (nc - 1 - n, 0))
    return pl.pallas_call(
        body, name=name,
        out_shape=(jax.ShapeDtypeStruct((t, WIDTH), F32),
                   jax.ShapeDtypeStruct((nc, HEADS, HEAD_DIM, HEAD_DIM), F32)),
        grid=(nc,),
        in_specs=[row, row, row, row, pl.BlockSpec((HEADS, cc, cc), lambda n: (0, nc - 1 - n, 0)),
                  pl.BlockSpec((cc, 128), lambda n: (nc - 1 - n, 0))],
        out_specs=(row, pl.BlockSpec((1, HEADS, HEAD_DIM, HEAD_DIM), lambda n: (nc - 1 - n, 0, 0, 0))),
        scratch_shapes=[pltpu.VMEM((HEADS, HEAD_DIM, HEAD_DIM), F32)],
        compiler_params=_params(("arbitrary",)),
    )(do, w, kd, qg, p, gates)


def _dn_local_bwd(act, gates, u, w, kd, qg, tinv, p, sh, dsh, dvn, do, name):
    t = act.shape[0]
    cc = DN_CHUNK
    nc = t // cc

    def body(q_ref, k_ref, v_ref, g_ref, u_ref, w_ref, kd_ref, qg_ref, ti_ref, p_ref, s_ref, ds_ref,
             dvn_ref, do_ref, dq_ref, dk_ref, dv_ref, dg_ref):
        gates_v = g_ref[...]
        lower, strict = _tri(cc, "lower"), _tri(cc, "strict")
        ones = jnp.ones((cc, 128), BF16)
        rowc = _iota((cc, 1), 0)
        lane = _iota((cc, 128), 1)
        hs = range(HEADS)
        sl = [slice(h * HEAD_DIM, (h + 1) * HEAD_DIM) for h in hs]
        q, k, v, uu, ww, kd, qg, dvn, do = ([r[:, s] for s in sl] for r in (
            q_ref, k_ref, v_ref, u_ref, w_ref, kd_ref, qg_ref, dvn_ref, do_ref))
        gc_col, gc_last, beta, decay = zip(*[_dn_chunk_common(gates_v, h) for h in hs])
        gam = [jnp.exp(g) for g in gc_col]
        kb = [k[h] * beta[h] for h in hs]
        s_in = [s_ref[0, h] for h in hs]
        ds_out = [ds_ref[0, h] for h in hs]
        tinv = [ti_ref[h] for h in hs]

        a = [jnp.where(strict, _dotb(kb[h], k[h], NT) * decay[h], 0.0) for h in hs]
        vn = [uu[h] - _dotb(ww[h], s_in[h], NN) for h in hs]
        dqg = [_dotb(do[h], s_in[h], NT) for h in hs]
        dw = [-_dotb(dvn[h], s_in[h], NT) for h in hs]
        dp = [jnp.where(lower, _dotb(do[h], vn[h], NT), 0.0) for h in hs]
        dkd = [_dotb(vn[h], ds_out[h], NT) for h in hs]
        dru = [_dot3(tinv[h], dvn[h], TN) for h in hs]
        drw = [_dot3(tinv[h], dw[h], TN) for h in hs]
        da = [-jnp.where(strict, _dotb(dru[h], uu[h], NT) + _dotb(drw[h], ww[h], NT), 0.0) for h in hs]
        dad = [da[h] * decay[h] for h in hs]
        dpd = [dp[h] * decay[h] for h in hs]
        dkb = [_dotb(dad[h], k[h], NN) + gam[h] * drw[h] for h in hs]
        dk = [_dotb(dad[h], kb[h], TN) + _dotb(dpd[h], q[h], TN) + beta[h] * dkb[h]
              + jnp.exp(gc_last[h] - gc_col[h]) * dkd[h] for h in hs]
        dq = [gam[h] * dqg[h] + _dotb(dpd[h], k[h], NN) for h in hs]
        gm = [da[h] * a[h] + dp[h] * p_ref[h] for h in hs]
        colsum = [_dot_xr(gm[h], ones, TN)[:, 0:1] for h in hs]

        dgates = jnp.zeros((cc, 128), F32)
        for h in hs:
            dk_ref[:, sl[h]] = dk[h]
            dq_ref[:, sl[h]] = dq[h]
            dv_ref[:, sl[h]] = beta[h] * dru[h]
            dbeta = (jnp.sum(dkb[h] * k[h], axis=1, keepdims=True)
                     + jnp.sum(dru[h] * v[h], axis=1, keepdims=True))
            rkd = jnp.sum(dkd[h] * kd[h], axis=1, keepdims=True)
            dgc = (jnp.sum(gm[h], axis=1, keepdims=True) - colsum[h]
                   + jnp.sum(dqg[h] * qg[h], axis=1, keepdims=True)
                   + jnp.sum(drw[h] * kb[h], axis=1, keepdims=True) * gam[h] - rkd)
            tail = jnp.sum(rkd, axis=0, keepdims=True) + jnp.exp(gc_last[h]) * jnp.sum(
                jnp.sum(s_in[h].astype(F32) * ds_out[h].astype(F32), axis=1, keepdims=True), axis=0, keepdims=True)
            dgc = dgc + jnp.where(rowc == cc - 1, tail, 0.0)
            dgates = dgates + jnp.where(lane == h, dgc, 0.0) + jnp.where(lane == h + HEADS, dbeta, 0.0)
        dg_ref[...] = dgates

    row = lambda off: pl.BlockSpec((cc, WIDTH), lambda n: (n, off))
    mat = pl.BlockSpec((HEADS, cc, cc), lambda n: (0, n, 0))
    st = pl.BlockSpec((1, HEADS, HEAD_DIM, HEAD_DIM), lambda n: (n, 0, 0, 0))
    gl = pl.BlockSpec((cc, 128), lambda n: (n, 0))
    tw = jax.ShapeDtypeStruct((t, WIDTH), F32)
    return pl.pallas_call(
        body, name=name,
        out_shape=(tw, tw, tw, jax.ShapeDtypeStruct((t, 128), F32)),
        grid=(nc,),
        in_specs=[row(0), row(1), row(2), gl, row(0), row(0), row(0), row(0), mat, mat, st, st, row(0), row(0)],
        out_specs=(row(0), row(0), row(0), gl),
        compiler_params=_params(("parallel",)),
    )(act, act, act, gates, u, w, kd, qg, tinv, p, sh, dsh, dvn, do)


def _dn_post_fwd(o, gate, w, name):
    t = o.shape[0]
    tr = _tile(t, 512)

    def body(o_ref, g_ref, w_ref, y_ref):
        for h in range(HEADS):
            sl = slice(h * HEAD_DIM, (h + 1) * HEAD_DIM)
            ov, gv = o_ref[:, sl], g_ref[:, sl].astype(F32)
            r = lax.rsqrt(jnp.mean(ov * ov, axis=1, keepdims=True) + EPS)
            y_ref[:, sl] = (ov * r * w_ref[...] * (gv * _sigmoid(gv))).astype(BF16)

    blk = pl.BlockSpec((tr, WIDTH), lambda i: (i, 0))
    return pl.pallas_call(
        body, name=name,
        out_shape=jax.ShapeDtypeStruct((t, WIDTH), BF16),
        grid=(t // tr,),
        in_specs=[blk, blk, pl.BlockSpec((1, HEAD_DIM), lambda i: (0, 0))],
        out_specs=blk,
        compiler_params=_params(("parallel",)),
    )(o, gate, w)


def _dn_post_bwd(dy, o, gate, w, name):
    t = o.shape[0]
    tr = _tile(t, 512)

    def body(dy_ref, o_ref, g_ref, w_ref, do_ref, dg_ref, dw_ref):
        i = pl.program_id(0)

        @pl.when(i == 0)
        def _():
            dw_ref[...] = jnp.zeros_like(dw_ref)

        dw = jnp.zeros((1, HEAD_DIM), F32)
        for h in range(HEADS):
            sl = slice(h * HEAD_DIM, (h + 1) * HEAD_DIM)
            ov, gv, dyv = o_ref[:, sl], g_ref[:, sl].astype(F32), dy_ref[:, sl].astype(F32)
            r = lax.rsqrt(jnp.mean(ov * ov, axis=1, keepdims=True) + EPS)
            oh = ov * r
            sg = _sigmoid(gv)
            dg_ref[:, sl] = (dyv * oh * w_ref[...] * (sg * (1.0 + gv * (1.0 - sg)))).astype(BF16)
            dn = dyv * (gv * sg)
            doh = dn * w_ref[...]
            do_ref[:, sl] = r * (doh - oh * jnp.mean(doh * oh, axis=1, keepdims=True))
            dw = dw + jnp.sum(dn * oh, axis=0, keepdims=True)
        dw_ref[...] += dw

    blk = pl.BlockSpec((tr, WIDTH), lambda i: (i, 0))
    return pl.pallas_call(
        body, name=name,
        out_shape=(jax.ShapeDtypeStruct((t, WIDTH), F32), jax.ShapeDtypeStruct((t, WIDTH), BF16),
                   jax.ShapeDtypeStruct((1, HEAD_DIM), F32)),
        grid=(t // tr,),
        in_specs=[blk, blk, blk, pl.BlockSpec((1, HEAD_DIM), lambda i: (0, 0))],
        out_specs=(blk, blk, pl.BlockSpec((1, HEAD_DIM), lambda i: (0, 0))),
        compiler_params=_params(("arbitrary",)),
    )(dy, o, gate, w)


def _sb_scores(qs, k_ref, qi, it, carries, uincl):
    bk = ATT_BLOCK
    scale = HEAD_DIM ** -0.5
    heads, groups = range(len(qs)), range(SB_GROUP)
    lane = [slice(e * HEAD_DIM, (e + 1) * HEAD_DIM) for e in heads]
    js = [qi - SB_GROUP * it - g for g in groups]
    rows = [pl.ds(pl.multiple_of(jnp.maximum(j, 0) * bk, bk), bk) for j in js]
    qpos = qi * bk + _iota((bk, bk), 0)
    col = _iota((bk, bk), 1)
    mask1 = [jnp.logical_and(j * bk + col < qpos, j >= 0) for j in js]
    ks = [[k_ref[r, lane[e]] for r in rows] for e in heads]
    z = [[_dot(qs[e], k, NT) * scale for k in ks[e]] for e in heads]
    soft = [[_log1pexp_neg_abs(a) for a in ze] for ze in z]
    lk_full = [[-(jnp.maximum(a, 0.0) + s) for a, s in zip(z[e], soft[e])] for e in heads]
    lk = [[jnp.where(m, a, 0.0) for m, a in zip(mask1, lk_full[e])] for e in heads]
    ls = [[jnp.minimum(a, 0.0) - s for a, s in zip(z[e], soft[e])] for e in heads]
    incl = [[_dot_xr2(a, uincl, NN) for a in lk[e]] for e in heads]
    weights, out_carries = [], []
    for e in heads:
        cb, we = carries[e], []
        for g in groups:
            we.append(jnp.where(mask1[g], jnp.exp(ls[e][g] + (cb + incl[e][g] - lk[e][g])), 0.0))
            cb = cb + incl[e][g][:, 0:1]
        weights.append(we)
        out_carries.append(cb)
    return rows, ks, weights, mask1, lk_full, ls, out_carries


def _sb_more(qi, carry):
    it, cbs = carry[0], carry[1]
    live = jnp.max(cbs[0])
    for cb in cbs[1:]:
        live = jnp.maximum(live, jnp.max(cb))
    return jnp.logical_and(SB_GROUP * it <= qi, live > SB_LOG_ZERO)


def _sb_steps(groups, nq):
    def when():
        h, i = pl.program_id(0), pl.program_id(1)
        return (jnp.logical_and(h == 0, i == 0), jnp.logical_and(h == groups // 2, i == 0),
                jnp.logical_and(h == groups - 1, i == nq - 1))
    return when


def _sb_fwd(qkv, name, comm=None):
    t = qkv.shape[0]
    bk = ATT_BLOCK
    hp, wide = SB_HEADS_FWD, SB_HEADS_FWD * HEAD_DIM
    lane = [slice(e * HEAD_DIM, (e + 1) * HEAD_DIM) for e in range(hp)]

    def body(q_ref, k_ref, v_ref, o_ref):
        qi = pl.program_id(1)
        qs = [q_ref[:, s] for s in lane]
        uincl = jnp.where(_tri(bk, "lower"), 1.0, 0.0).astype(BF16)

        def step(carry):
            it, cbs, accs = carry
            rows, _, weights, _, _, _, cbs = _sb_scores(qs, k_ref, qi, it, cbs, uincl)
            accs = list(accs)
            for e in range(hp):
                for r, a in zip(rows, weights[e]):
                    accs[e] = accs[e] + _dot(a.astype(BF16), v_ref[r, lane[e]], NN)
            return it + 1, tuple(cbs), tuple(accs)

        init = (jnp.int32(0), (jnp.zeros((bk, 1), F32),) * hp, (jnp.zeros((bk, HEAD_DIM), F32),) * hp)
        _, _, accs = lax.while_loop(functools.partial(_sb_more, qi), step, init)
        for e in range(hp):
            o_ref[:, lane[e]] = accs[e]

    groups = HEADS // hp
    (o,), extra = _host_call(
        body, name, comm, _sb_steps(groups, t // bk), [jax.ShapeDtypeStruct((t, WIDTH), F32)], (groups, t // bk),
        [pl.BlockSpec((bk, wide), lambda h, i: (i, h)),
         pl.BlockSpec((t, wide), lambda h, i: (0, groups + h)),
         pl.BlockSpec((t, wide), lambda h, i: (0, 2 * groups + h))],
        [pl.BlockSpec((bk, wide), lambda h, i: (i, h))], [], ("parallel", "arbitrary"), (qkv, qkv, qkv))
    return o, extra


def _sb_bwd(qkv, o, do, name, comm=None):
    assert do.dtype == BF16
    t = qkv.shape[0]
    bk = ATT_BLOCK
    scale = HEAD_DIM ** -0.5
    hp, wide = SB_HEADS_BWD, SB_HEADS_BWD * HEAD_DIM
    lane = [slice(e * HEAD_DIM, (e + 1) * HEAD_DIM) for e in range(hp)]

    def body(q_ref, k_ref, v_ref, o_ref, do_ref, dq_ref, dk_out, dv_out, dk_ref, dv_ref):
        qi = pl.program_id(1)

        @pl.when(qi == 0)
        def _():
            dk_ref[...] = jnp.zeros_like(dk_ref)
            dv_ref[...] = jnp.zeros_like(dv_ref)

        heads, groups = range(hp), range(SB_GROUP)
        qs = [q_ref[:, s] for s in lane]
        dob = [do_ref[:, s] for s in lane]
        dsum = [jnp.sum(dob[e].astype(F32) * o_ref[:, lane[e]], axis=1, keepdims=True) for e in heads]
        uincl = jnp.where(_tri(bk, "lower"), 1.0, 0.0).astype(BF16)

        def step(carry):
            it, cbs, ces, dqs = carry
            rows, ks, weights, mask, lk_full, ls, cbs = _sb_scores(qs, k_ref, qi, it, cbs, uincl)
            ab = [[a.astype(BF16) for a in weights[e]] for e in heads]
            vs = [[v_ref[r, lane[e]] for r in rows] for e in heads]
            dla = [[ab[e][g].astype(F32) * _dot(dob[e], vs[e][g], NT) for g in groups] for e in heads]
            suf = [[_dot_xr2(a, uincl, NN) for a in dla[e]] for e in heads]
            ces, dqs = list(ces), list(dqs)
            for e in heads:
                for g in groups:
                    err = dsum[e] - (ces[e] + suf[e][g])
                    ces[e] = ces[e] + suf[e][g][:, 0:1]
                    dz = jnp.where(mask[g], dla[e][g] * jnp.exp(lk_full[e][g]) - err * jnp.exp(ls[e][g]), 0.0)
                    dzb = (dz * scale).astype(BF16)
                    dqs[e] = dqs[e] + _dot(dzb, ks[e][g], NN)
                    dk_ref[rows[g], lane[e]] += _dot(dzb, qs[e], TN)
                    dv_ref[rows[g], lane[e]] += _dot(ab[e][g], dob[e], TN)
            return it + 1, tuple(cbs), tuple(ces), tuple(dqs)

        zc = (jnp.zeros((bk, 1), F32),) * hp
        init = (jnp.int32(0), zc, zc, (jnp.zeros((bk, HEAD_DIM), F32),) * hp)
        dqs = lax.while_loop(functools.partial(_sb_more, qi), step, init)[3]
        for e in heads:
            dq_ref[:, lane[e]] = dqs[e].astype(BF16)

        @pl.when(qi == t // bk - 1)
        def _():
            dk_out[...] = dk_ref[...].astype(BF16)
            dv_out[...] = dv_ref[...].astype(BF16)

    ngroup = HEADS // hp
    tw = jax.ShapeDtypeStruct((t, WIDTH), BF16)
    qb = pl.BlockSpec((bk, wide), lambda h, i: (i, h))
    full = lambda off: pl.BlockSpec((t, wide), lambda h, i: (0, off + h))
    return _host_call(
        body, name, comm, _sb_steps(ngroup, t // bk), [tw, tw, tw], (ngroup, t // bk),
        [qb, full(ngroup), full(2 * ngroup), qb, qb], [qb, full(0), full(0)],
        [pltpu.VMEM((t, wide), F32), pltpu.VMEM((t, wide), F32)], ("parallel", "arbitrary"),
        (qkv, qkv, qkv, o, do))


def _merge_fwd(pd, ps, gl, name):
    t = pd.shape[0]
    tr, tc = _tile(t, 512), 512
    nj = D_MODEL // tc

    def body(pd_ref, ps_ref, gd_ref, gs_ref, o_ref):
        gd, gs = gd_ref[...].astype(F32), gs_ref[...].astype(F32)
        o_ref[...] = (_sigmoid(gd) * pd_ref[...].astype(F32) + _sigmoid(gs) * ps_ref[...].astype(F32)).astype(BF16)

    blk = lambda off: pl.BlockSpec((tr, tc), lambda i, j: (i, j + off))
    return pl.pallas_call(
        body, name=name,
        out_shape=jax.ShapeDtypeStruct((t, D_MODEL), BF16),
        grid=(t // tr, nj),
        in_specs=[blk(0), blk(0), blk(0), blk(nj)],
        out_specs=blk(0),
        compiler_params=_params(("parallel", "parallel")),
    )(pd, ps, gl, gl)


def _merge_bwd(dm, pd, ps, gl, name):
    t = pd.shape[0]
    tr, tc = _tile(t, 512), 512
    nj = D_MODEL // tc

    def body(dm_ref, pd_ref, ps_ref, gd_ref, gs_ref, dpd_ref, dps_ref, dgd_ref, dgs_ref):
        dmv = dm_ref[...].astype(F32)
        sd, ss = _sigmoid(gd_ref[...].astype(F32)), _sigmoid(gs_ref[...].astype(F32))
        dpd_ref[...] = (dmv * sd).astype(BF16)
        dps_ref[...] = (dmv * ss).astype(BF16)
        dgd_ref[...] = (dmv * pd_ref[...].astype(F32) * sd * (1.0 - sd)).astype(BF16)
        dgs_ref[...] = (dmv * ps_ref[...].astype(F32) * ss * (1.0 - ss)).astype(BF16)

    blk = lambda off: pl.BlockSpec((tr, tc), lambda i, j: (i, j + off))
    out = jax.ShapeDtypeStruct((t, D_MODEL), BF16)
    return pl.pallas_call(
        body, name=name,
        out_shape=(out, out, out, out),
        grid=(t // tr, nj),
        in_specs=[blk(0), blk(0), blk(0), blk(0), blk(nj)],
        out_specs=(blk(0), blk(0), blk(0), blk(0)),
        compiler_params=_params(("parallel", "parallel")),
    )(dm, pd, ps, gl, gl)


def _local_step(x, target, wts, plan=None, n1=None):
    if n1 is None:
        n1 = _rmsnorm_fwd(x, wts["norm1_w"], "norm1_fwd")
    qkv_pre = _matmul(n1, wts["w_dnqkv_t"], "nt", BF16, "in_dnqkv")
    hgate = _matmul(n1, wts["w_dngate_t"], "nt", BF16, "in_dngate")
    sbqkv = _matmul(n1, wts["w_sbqkv_t"], "nt", BF16, "in_sbqkv")
    gl = _matmul(n1, wts["w_gl_t"], "nt", BF16, "in_gl")
    hab = _matmul(n1, wts["w_ab_t"], "nt", F32, "in_ab")

    act = _dn_pre_fwd(qkv_pre, wts["dn_conv_w"], "dn_pre_fwd")
    gates = _dn_gates_fwd(hab, wts["alog"], wts["dtb"], "dn_gates_fwd")
    u, w, kd, qg, tinv, p = _dn_local_fwd(act, gates, "dn_local_fwd")
    o_dn, sh = _dn_scan_fwd(u, w, kd, qg, p, gates, "dn_scan_fwd")
    y_dn = _dn_post_fwd(o_dn, hgate, wts["dn_norm_w"], "dn_post_fwd")

    o_sb, late = _sb_fwd(sbqkv, "sb_fwd", comm=plan.late_gather() if plan else None)
    if plan:
        wts = {**wts, **plan.late_weights(late)}

    pd = _matmul(y_dn, wts["w_proj_dn"], "nn", BF16, "proj_dn")
    ps = _matmul(o_sb, wts["w_proj_sb"], "nn", BF16, "proj_sb")
    mixed = _merge_fwd(pd, ps, gl, "merge_fwd")
    x1 = _matmul(mixed, wts["w_out"], "nn", F32, "out_proj", add=x)

    n2 = _rmsnorm_fwd(x1, wts["norm2_w"], "norm2_fwd")
    upre = _matmul(n2, wts["ffn_w_up_t"], "nt", BF16, "ffn_up")
    fact = _ffn_act_fwd(upre, wts["ffn_conv_w"], "ffn_act_fwd")
    x2 = _matmul(fact, wts["ffn_w_down"], "nn", F32, "ffn_down", add=x1)

    dx2, g_normf, loss = _final_loss(x2, target, wts["norm_f_w"], "final_loss")

    dfact = _matmul(dx2, wts["ffn_w_down"], "nt", BF16, "ffn_down_dx")
    g_wdown = _matmul(fact, dx2, "tn", BF16, "ffn_down_dw")
    dgc, duc, dwg, dwu = _ffn_act_bwd(dfact, upre, wts["ffn_conv_w"], "ffn_act_bwd")
    g_fconv = jnp.concatenate([dwg, dwu], axis=1)
    dupre = _conv_bwd_data([dgc, duc], wts["ffn_conv_w"], FFN_CONV, BF16, "ffn_conv_bwd")
    dn2 = _matmul(dupre, wts["ffn_w_up_t"], "nn", F32, "ffn_up_dx")
    g_wup = _matmul(dupre, n2, "tn", BF16, "ffn_up_dw")
    dx1, g_norm2 = _rmsnorm_bwd(dn2, x1, wts["norm2_w"], dx2, "norm2_bwd")

    dmixed = _matmul(dx1, wts["w_out"], "nt", BF16, "out_proj_dx")
    g_wout = _matmul(mixed, dx1, "tn", BF16, "out_proj_dw")
    dpd, dps, dgd, dgs = _merge_bwd(dmixed, pd, ps, gl, "merge_bwd")
    dy_dn = _matmul(dpd, wts["w_proj_dn"], "nt", BF16, "proj_dn_dx")
    g_wpd = _matmul(y_dn, dpd, "tn", BF16, "proj_dn_dw")
    do_sb = _matmul(dps, wts["w_proj_sb"], "nt", BF16, "proj_sb_dx")
    g_wps = _matmul(o_sb, dps, "tn", BF16, "proj_sb_dw")
    grads = dict(w_proj_dn=g_wpd, w_proj_sb=g_wps, w_out=g_wout, ffn_w_up_t=g_wup, ffn_w_down=g_wdown)

    (dsq, dsk, dsv), got_early = _sb_bwd(sbqkv, o_sb, do_sb, "sb_bwd",
                                         comm=plan.early_grads(grads) if plan else None)

    do_dn, dhgate, g_dnnorm = _dn_post_bwd(dy_dn, o_dn, hgate, wts["dn_norm_w"], "dn_post_bwd")
    dvn, dsh = _dn_scan_bwd(do_dn, w, kd, qg, p, gates, "dn_scan_bwd")
    dq, dk, dv, dgates = _dn_local_bwd(act, gates, u, w, kd, qg, tinv, p, sh, dsh, dvn, do_dn, "dn_local_bwd")
    dhab, g_alog, g_dtb = _dn_gates_bwd(dgates, hab, wts["alog"], wts["dtb"], "dn_gates_bwd")
    dcv, g_dnconv = _dn_pre_bwd(dq, dk, dv, qkv_pre, wts["dn_conv_w"], "dn_pre_bwd")
    dqkv_pre = _conv_bwd_data([dcv], wts["dn_conv_w"], DN_CONV, BF16, "dn_conv_bwd")

    dh = jnp.concatenate([dqkv_pre, dhgate, dsq, dsk, dsv, dgd, dgs], axis=1)
    w_main_t = jnp.concatenate([wts["w_dnqkv_t"], wts["w_dngate_t"], wts["w_sbqkv_t"], wts["w_gl_t"]], axis=0)
    g_wmain = _matmul(dh, n1, "tn", BF16, "in_dw_main")
    g_wab = _matmul(dhab, n1, "tn", BF16, "in_dw_ab")
    grads.update(w_main_t=g_wmain, w_ab_t=g_wab, dn_conv_w=g_dnconv, alog=g_alog, dtb=g_dtb, dn_norm_w=g_dnnorm,
                 norm2_w=g_norm2, ffn_conv_w=g_fconv, norm_f_w=g_normf)
    got_late = []
    if plan:
        dn1, swapped = _matmul(dhab, wts["w_ab_t"], "nn", F32, "in_dx_ab", comm=plan.sibling_swap(grads))
        dn1, got_late = _matmul(dh, w_main_t, "nn", F32, "in_dx_main", add=dn1,
                                comm=plan.late_grads(swapped, grads, loss))
    else:
        dn1 = _matmul(dhab, wts["w_ab_t"], "nn", F32, "in_dx_ab")
        dn1 = _matmul(dh, w_main_t, "nn", F32, "in_dx_main", add=dn1)
    grad_x, g_norm1 = _rmsnorm_bwd(dn1, x, wts["norm1_w"], dx1, "norm1_bwd")
    grads["norm1_w"] = g_norm1
    return loss, grad_x, grads, got_early, got_late


HBM_SPEC = pl.BlockSpec(memory_space=pltpu.HBM)


def _mesh_pos():
    x, y, c = lax.axis_index("x"), lax.axis_index("y"), lax.axis_index("c")
    return x, y, c, 4 * x + 2 * y + c


def _peer(k):
    x, y, c, _ = _mesh_pos()
    px = 1 - x if k & 4 else x
    py = 1 - y if k & 2 else y
    pc = 1 - c if k & 1 else c
    return (px, py, pc), 4 * px + 2 * py + pc


def _rcopy(src, dst, send, recv, a, s, peer):
    return pltpu.make_async_remote_copy(src_ref=src, dst_ref=dst, send_sem=send.at[a, s], recv_sem=recv.at[a, s],
                                        device_id=peer, device_id_type=pl.DeviceIdType.MESH)


class _Gather:
    ICI = (2, 4, 6)

    def __init__(self, shards):
        self.args = list(shards)
        self.n = len(shards)
        self.out_shape = [jax.ShapeDtypeStruct((N_DEV,) + s.shape, s.dtype) for s in shards]
        self.scratch = [pltpu.SemaphoreType.DMA((self.n, N_DEV - 1)), pltpu.SemaphoreType.DMA((self.n, N_DEV - 1)),
                        pltpu.SemaphoreType.DMA((self.n,))]

    def _slot(self, outs, a, d):
        return outs[a].at[d]

    def _first(self, ins, outs, send, recv, a):
        me = _mesh_pos()[3]
        out, got = [], []
        for s, k in enumerate((1,) + self.ICI):
            peer, pidx = _peer(k)
            out.append(_rcopy(ins[a], self._slot(outs, a, me), send, recv, a, s, peer))
            got.append(_rcopy(ins[a], self._slot(outs, a, pidx), send, recv, a, s, peer))
        return out, got

    def _forward(self, ins, outs, send, recv, a):
        sib = _peer(1)[0]
        out, got = [], []
        for s, k in enumerate(self.ICI):
            held = self._slot(outs, a, _peer(k)[1])
            out.append(_rcopy(held, held, send, recv, a, 4 + s, sib))
            other = self._slot(outs, a, _peer(k | 1)[1])
            got.append(_rcopy(other, other, send, recv, a, 4 + s, sib))
        return out, got

    def start(self, ins, outs, sems):
        send, recv, loc = sems
        me = _mesh_pos()[3]
        for a in range(self.n):
            pltpu.make_async_copy(ins[a], self._slot(outs, a, me), loc.at[a]).start()
            for cp in self._first(ins, outs, send, recv, a)[0]:
                cp.start()

    def mid(self, ins, outs, sems):
        send, recv, _ = sems
        for a in range(self.n):
            arrivals = self._first(ins, outs, send, recv, a)[1]
            for s, cp in enumerate(self._forward(ins, outs, send, recv, a)[0]):
                arrivals[1 + s].wait_recv()
                cp.start()

    def finish(self, ins, outs, sems):
        send, recv, loc = sems
        me = _mesh_pos()[3]
        for a in range(self.n):
            first_out, first_got = self._first(ins, outs, send, recv, a)
            fwd_out, fwd_got = self._forward(ins, outs, send, recv, a)
            first_got[0].wait_recv()
            for cp in fwd_got:
                cp.wait_recv()
            for cp in first_out + fwd_out:
                cp.wait_send()
            pltpu.make_async_copy(ins[a], self._slot(outs, a, me), loc.at[a]).wait()


class _Exchange:
    def __init__(self, slabs=(), gathered=(), chip_slabs=(), sibling_slabs=()):
        self.args = list(slabs) + list(chip_slabs) + list(sibling_slabs) + list(gathered)
        self.kind = (["dev"] * len(slabs) + ["chip"] * len(chip_slabs) + ["sib"] * len(sibling_slabs)
                     + ["all"] * len(gathered))
        self.n = len(self.args)
        half = lambda s: jax.ShapeDtypeStruct((N_DEV // 2,) + s.shape[1:], s.dtype)
        self.out_shape = ([jax.ShapeDtypeStruct(s.shape, s.dtype) for s in slabs]
                          + [half(s) for s in chip_slabs] + [half(s) for s in sibling_slabs]
                          + [jax.ShapeDtypeStruct((N_DEV,) + s.shape, s.dtype) for s in gathered])
        self.scratch = [pltpu.SemaphoreType.DMA((self.n, N_DEV - 1)), pltpu.SemaphoreType.DMA((self.n, N_DEV - 1)),
                        pltpu.SemaphoreType.DMA((self.n,))]

    def _copies(self, ins, outs, send, recv, a):
        x, y, c, me = _mesh_pos()
        kind = self.kind[a]
        out, got = [], []
        if kind == "sib":
            sib = _peer(1)[0]
            for q in range(N_DEV // 2):
                out.append(_rcopy(ins[a].at[2 * q + 1 - c], outs[a].at[q], send, recv, a, q, sib))
                got.append(_rcopy(ins[a].at[2 * q + c], outs[a].at[q], send, recv, a, q, sib))
            return out, got
        for k in ((2, 4, 6) if kind == "chip" else range(1, N_DEV)):
            peer, pidx = _peer(k)
            if kind == "chip":
                src, mine, theirs = ins[a].at[2 * peer[0] + peer[1]], 2 * x + y, 2 * peer[0] + peer[1]
            else:
                src, mine, theirs = (ins[a].at[pidx] if kind == "dev" else ins[a]), me, pidx
            out.append(_rcopy(src, outs[a].at[mine], send, recv, a, k - 1, peer))
            got.append(_rcopy(src, outs[a].at[theirs], send, recv, a, k - 1, peer))
        return out, got

    def _local(self, ins, outs, loc, a):
        x, y, _, me = _mesh_pos()
        kind = self.kind[a]
        if kind == "sib":
            return None
        if kind == "chip":
            return pltpu.make_async_copy(ins[a].at[2 * x + y], outs[a].at[2 * x + y], loc.at[a])
        return pltpu.make_async_copy(ins[a].at[me] if kind == "dev" else ins[a], outs[a].at[me], loc.at[a])

    def start(self, ins, outs, sems):
        send, recv, loc = sems
        for a in range(self.n):
            if self._local(ins, outs, loc, a) is not None:
                self._local(ins, outs, loc, a).start()
            for cp in self._copies(ins, outs, send, recv, a)[0]:
                cp.start()

    def mid(self, ins, outs, sems):
        pass

    def finish(self, ins, outs, sems):
        send, recv, loc = sems
        for a in range(self.n):
            out, got = self._copies(ins, outs, send, recv, a)
            for cp in got:
                cp.wait_recv()
            for cp in out:
                cp.wait_send()
            if self._local(ins, outs, loc, a) is not None:
                self._local(ins, outs, loc, a).wait()


def _comm_call(comm, name):
    n = comm.n

    def body(*refs):
        ins, outs, sems = refs[:n], refs[n:2 * n], refs[2 * n:]
        comm.start(ins, outs, sems)
        comm.mid(ins, outs, sems)
        comm.finish(ins, outs, sems)

    return pl.pallas_call(
        body, name=name, out_shape=comm.out_shape, in_specs=[HBM_SPEC] * n, out_specs=[HBM_SPEC] * n,
        scratch_shapes=comm.scratch,
    )(*comm.args)


def _hosted(body, comm, n_in, n_out, when):
    if comm is None:
        return body

    def wrapped(*refs):
        ins, c_ins = refs[:n_in], refs[n_in:n_in + comm.n]
        o0 = n_in + comm.n
        outs, c_outs = refs[o0:o0 + n_out], refs[o0 + n_out:o0 + n_out + comm.n]
        scratch, sems = refs[o0 + n_out + comm.n:len(refs) - 3], refs[len(refs) - 3:]
        first, middle, last = when()

        @pl.when(first)
        def _():
            comm.start(c_ins, c_outs, sems)

        body(*ins, *outs, *scratch)

        @pl.when(middle)
        def _():
            comm.mid(c_ins, c_outs, sems)

        @pl.when(last)
        def _():
            comm.finish(c_ins, c_outs, sems)

    return wrapped


def _host_call(body, name, comm, when, out_shape, grid, in_specs, out_specs, scratch_shapes, sem, args):
    n_in, n_out = len(in_specs), len(out_specs)
    if comm is None:
        res = pl.pallas_call(body, name=name, out_shape=out_shape, grid=grid, in_specs=in_specs, out_specs=out_specs,
                             scratch_shapes=scratch_shapes, compiler_params=_params(sem))(*args)
        return list(res), []
    res = pl.pallas_call(
        _hosted(body, comm, n_in, n_out, when), name=name,
        out_shape=list(out_shape) + comm.out_shape, grid=grid,
        in_specs=list(in_specs) + [HBM_SPEC] * comm.n, out_specs=list(out_specs) + [HBM_SPEC] * comm.n,
        scratch_shapes=list(scratch_shapes) + comm.scratch,
        compiler_params=_params(("arbitrary",) * len(grid)),
    )(*args, *comm.args)
    return list(res[:n_out]), list(res[n_out:])


def _add_my_slabs(slabs, b, name):
    n, rows, cols = b.shape
    tc = _tile(cols, 256)

    def body(a_ref, b_ref, o_ref):
        o_ref[...] = (a_ref[...].astype(F32) + b_ref[...].astype(F32)).astype(o_ref.dtype)

    blk = pl.BlockSpec((None, rows, tc), lambda i, j: (i, 0, j))
    mine = pl.BlockSpec((None, rows, tc), lambda i, j: (2 * i + lax.axis_index("c"), 0, j))
    return pl.pallas_call(
        body, name=name, out_shape=jax.ShapeDtypeStruct(b.shape, b.dtype), grid=(n, cols // tc),
        in_specs=[mine, blk], out_specs=blk, compiler_params=_params(("parallel", "parallel")),
    )(slabs, b)


def _adamw(parts, w, m, v, name):
    rows, cols = w.shape
    nparts = parts.shape[0]
    tr, tc = rows, cols
    for cand in (128, 176):
        if rows > cand and rows % cand == 0:
            tr = cand
            break
    if tr == rows and rows > 512:
        tc = _tile(cols, 256)

    def body(p_ref, w_ref, m_ref, v_ref, g_ref, d_ref, mo_ref, vo_ref):
        g = p_ref[0].astype(F32)
        for s in range(1, nparts):
            g = g + p_ref[s].astype(F32)
        mn = ADAM_B1 * m_ref[...] + (1.0 - ADAM_B1) * g
        vn = ADAM_B2 * v_ref[...] + (1.0 - ADAM_B2) * (g * g)
        m_hat = mn / (1.0 - ADAM_B1 ** ADAM_STEP)
        v_hat = vn / (1.0 - ADAM_B2 ** ADAM_STEP)
        g_ref[...] = g
        d_ref[...] = -ADAM_LR * (m_hat / (jnp.sqrt(v_hat) + ADAM_EPS) + ADAM_WD * w_ref[...])
        mo_ref[...] = mn
        vo_ref[...] = vn

    blk = pl.BlockSpec((tr, tc), lambda i, j: (i, j))
    out = jax.ShapeDtypeStruct((rows, cols), F32)
    return pl.pallas_call(
        body, name=name,
        out_shape=(out, out, out, out),
        grid=(rows // tr, cols // tc),
        in_specs=[pl.BlockSpec((nparts, tr, tc), lambda i, j: (0, i, j)), blk, blk, blk],
        out_specs=(blk, blk, blk, blk),
        compiler_params=_params(("parallel", "parallel")),
    )(parts, w, m, v)


CONV_PACK = 8 * 1024
WEIGHT_ORDER = ("norm1_w", "w_in", "dn_conv_w", "dn_A_log", "dn_dt_bias", "dn_norm_w", "w_proj_dn", "w_proj_sb",
                "w_out", "norm2_w", "ffn_w_up", "ffn_conv_w", "ffn_w_down", "norm_f_w")


def _cols_to_slabs(g):
    r, c8 = g.shape
    return g.reshape(r, N_DEV, c8 // N_DEV).transpose(1, 0, 2)


def _slabs_to_cols(s):
    d, r, c = s.shape
    return s.transpose(1, 0, 2).reshape(r, d * c)


def kernel(x, norm1_w, w_in, dn_conv_w, dn_A_log, dn_dt_bias, dn_norm_w, w_proj_dn, w_proj_sb, w_out, norm2_w, ffn_w_up, ffn_conv_w, ffn_w_down, norm_f_w, loss_target, m_norm1_w, m_w_in, m_dn_conv_w, m_dn_A_log, m_dn_dt_bias, m_dn_norm_w, m_w_proj_dn, m_w_proj_sb, m_w_out, m_norm2_w, m_ffn_w_up, m_ffn_conv_w, m_ffn_w_down, m_norm_f_w, v_norm1_w, v_w_in, v_dn_conv_w, v_dn_A_log, v_dn_dt_bias, v_dn_norm_w, v_w_proj_dn, v_w_proj_sb, v_w_out, v_norm2_w, v_ffn_w_up, v_ffn_conv_w, v_ffn_w_down, v_norm_f_w):
    me = _mesh_pos()[3]
    tr = lambda a: jnp.transpose(a[0])
    w_loc = dict(norm1_w=norm1_w, w_in=tr(w_in), dn_conv_w=dn_conv_w[0], dn_A_log=dn_A_log, dn_dt_bias=dn_dt_bias,
                 dn_norm_w=dn_norm_w, w_proj_dn=w_proj_dn[0], w_proj_sb=w_proj_sb[0], w_out=w_out[0],
                 norm2_w=norm2_w, ffn_w_up=tr(ffn_w_up), ffn_conv_w=ffn_conv_w[0], ffn_w_down=ffn_w_down[0],
                 norm_f_w=norm_f_w[None, :])
    m_loc = dict(norm1_w=m_norm1_w, w_in=tr(m_w_in), dn_conv_w=m_dn_conv_w[0], dn_A_log=m_dn_A_log,
                 dn_dt_bias=m_dn_dt_bias, dn_norm_w=m_dn_norm_w, w_proj_dn=m_w_proj_dn[0], w_proj_sb=m_w_proj_sb[0],
                 w_out=m_w_out[0], norm2_w=m_norm2_w, ffn_w_up=tr(m_ffn_w_up), ffn_conv_w=m_ffn_conv_w[0],
                 ffn_w_down=m_ffn_w_down[0], norm_f_w=m_norm_f_w[None, :])
    v_loc = dict(norm1_w=v_norm1_w, w_in=tr(v_w_in), dn_conv_w=v_dn_conv_w[0], dn_A_log=v_dn_A_log,
                 dn_dt_bias=v_dn_dt_bias, dn_norm_w=v_dn_norm_w, w_proj_dn=v_w_proj_dn[0], w_proj_sb=v_w_proj_sb[0],
                 w_out=v_w_out[0], norm2_w=v_norm2_w, ffn_w_up=tr(v_ffn_w_up), ffn_conv_w=v_ffn_conv_w[0],
                 ffn_w_down=v_ffn_w_down[0], norm_f_w=v_norm_f_w[None, :])

    conv_flat = jnp.concatenate([w_loc["dn_conv_w"].reshape(-1), w_loc["ffn_conv_w"].reshape(-1)])
    n_dn, n_ffn = DN_CONV * 3 * WIDTH // N_DEV, FFN_CONV * 2 * D_FF // N_DEV
    conv_pack = jnp.pad(conv_flat, (0, CONV_PACK - n_dn - n_ffn)).reshape(8, 1024)
    n1, (g_in, g_conv) = _rmsnorm_fwd(x[0], norm1_w, "norm1_fwd",
                                      comm=_Gather([w_loc["w_in"].astype(BF16), conv_pack]))
    in_width = g_in.shape[0] * g_in.shape[1]
    w_in_t = g_in.reshape(in_width, D_MODEL)
    g_conv = g_conv.reshape(N_DEV, CONV_PACK)
    dn_conv_full = _slabs_to_cols(g_conv[:, :n_dn].reshape(N_DEV, DN_CONV, 3 * WIDTH // N_DEV))
    ffn_conv_full = _slabs_to_cols(g_conv[:, n_dn:n_dn + n_ffn].reshape(N_DEV, FFN_CONV, 2 * D_FF // N_DEV))
    q_end = 3 * WIDTH
    ab_end = q_end + 2 * HEADS
    gate_end = ab_end + WIDTH
    sb_end = gate_end + 3 * WIDTH
    pad_lanes = lambda a: jnp.pad(a, ((0, 0), (0, 128 - a.shape[1])))
    wts = dict(
        norm1_w=norm1_w, w_dnqkv_t=w_in_t[:q_end], w_ab_t=jnp.pad(w_in_t[q_end:ab_end], ((0, 128 - 2 * HEADS), (0, 0))),
        w_dngate_t=w_in_t[ab_end:gate_end], w_sbqkv_t=w_in_t[gate_end:sb_end], w_gl_t=w_in_t[sb_end:],
        dn_conv_w=dn_conv_full, alog=pad_lanes(dn_A_log), dtb=pad_lanes(dn_dt_bias), dn_norm_w=dn_norm_w,
        norm2_w=norm2_w, ffn_conv_w=ffn_conv_full, norm_f_w=norm_f_w[None, :])

    n_fc = FFN_CONV * 2 * D_FF
    fc_rows = -(-n_fc // D_MODEL)
    dn_rows = DN_CONV * 3 * WIDTH // D_MODEL
    late_names = ("w_proj_dn", "w_proj_sb", "w_out", "ffn_w_up", "ffn_w_down")

    class Plan:
        @staticmethod
        def late_gather():
            return _Gather([w_loc[k].astype(BF16) for k in late_names])

        @staticmethod
        def late_weights(got):
            g_pd, g_ps, g_out, g_up, g_down = got
            return dict(w_proj_dn=g_pd.reshape(WIDTH, D_MODEL), w_proj_sb=g_ps.reshape(WIDTH, D_MODEL),
                        w_out=g_out.reshape(D_MODEL, D_MODEL), ffn_w_up_t=g_up.reshape(2 * D_FF, D_MODEL),
                        ffn_w_down=g_down.reshape(D_FF, D_MODEL))

        @staticmethod
        def early_grads(g):
            return _Exchange([g["w_proj_dn"].reshape(N_DEV, WIDTH // N_DEV, D_MODEL),
                              g["w_proj_sb"].reshape(N_DEV, WIDTH // N_DEV, D_MODEL),
                              g["w_out"].reshape(N_DEV, D_MODEL // N_DEV, D_MODEL),
                              g["ffn_w_up_t"].reshape(N_DEV, 2 * D_FF // N_DEV, D_MODEL),
                              g["ffn_w_down"].reshape(N_DEV, D_FF // N_DEV, D_MODEL)])

        @staticmethod
        def _in_slabs(g):
            g_win_t = jnp.concatenate([g["w_main_t"][:q_end], g["w_ab_t"][:2 * HEADS], g["w_main_t"][q_end:]],
                                      axis=0)
            return g_win_t.reshape(N_DEV, in_width // N_DEV, D_MODEL)

        @staticmethod
        def sibling_swap(g):
            return _Exchange(sibling_slabs=[Plan._in_slabs(g)])

        @staticmethod
        def late_grads(swapped, g, loss):
            chip_sums = _add_my_slabs(Plan._in_slabs(g), swapped[0], "in_dw_chip_sum")
            row3 = jnp.concatenate([g["dn_norm_w"], g["alog"], g["dtb"], jnp.pad(loss, ((0, 0), (0, 127))),
                                    jnp.zeros((1, D_MODEL - 512), F32)], axis=1)
            fconv_rows = jnp.pad(g["ffn_conv_w"].reshape(-1), (0, fc_rows * D_MODEL - n_fc)).reshape(fc_rows, D_MODEL)
            pad8 = lambda a: jnp.pad(a, ((0, -a.shape[0] % 8), (0, 0)))
            pieces = [g["norm2_w"], g["norm_f_w"], row3, g["dn_conv_w"].reshape(dn_rows, D_MODEL), fconv_rows]
            small = jnp.concatenate([pad8(a) for a in pieces], axis=0)
            assert small.shape[0] == SMALL_ROWS
            return _Exchange(chip_slabs=[chip_sums], gathered=[small])

    loss, grad_x, g, got_early, got_late = _local_step(x[0], loss_target[0], wts, Plan, n1)
    r_pd, r_ps, r_out, r_up, r_down = got_early
    r_in, r_small = got_late
    (r_norm1,) = _comm_call(_Exchange([], [jnp.pad(g["norm1_w"], ((0, 7), (0, 0)))]), "gather_norm1")

    parts = dict(w_in=r_in, w_proj_dn=r_pd, w_proj_sb=r_ps, w_out=r_out, ffn_w_up=r_up, ffn_w_down=r_down)
    parts["norm1_w"] = r_norm1[:, 0:1, :]
    parts["norm2_w"] = r_small[:, 0:1, :]
    parts["norm_f_w"] = r_small[:, 8:9, :]
    parts["dn_norm_w"] = r_small[:, 16:17, 0:HEAD_DIM]
    parts["dn_A_log"] = r_small[:, 16:17, 128:128 + HEADS]
    parts["dn_dt_bias"] = r_small[:, 16:17, 256:256 + HEADS]
    dnc = r_small[:, 24:24 + dn_rows, :].reshape(N_DEV, DN_CONV, 3 * WIDTH)
    parts["dn_conv_w"] = lax.dynamic_slice_in_dim(dnc, me * (3 * WIDTH // N_DEV), 3 * WIDTH // N_DEV, axis=2)
    fc0 = 24 + dn_rows + (-dn_rows % 8)
    fcc = r_small[:, fc0:fc0 + fc_rows, :].reshape(N_DEV, fc_rows * D_MODEL)[:, :n_fc]
    fcc = fcc.reshape(N_DEV, FFN_CONV, 2 * D_FF)
    parts["ffn_conv_w"] = lax.dynamic_slice_in_dim(fcc, me * (2 * D_FF // N_DEV), 2 * D_FF // N_DEV, axis=2)
    loss_total = jnp.sum(r_small[:, 16, 384])

    res = {k: _adamw(parts[k], w_loc[k], m_loc[k], v_loc[k], "adamw_" + k) for k in WEIGHT_ORDER}
    lead = ("w_in", "dn_conv_w", "w_proj_dn", "w_proj_sb", "w_out", "ffn_w_up", "ffn_conv_w", "ffn_w_down")

    def shaped(k, a):
        if k in ("w_in", "ffn_w_up"):
            return jnp.transpose(a)[None]
        if k in lead:
            return a[None]
        if k == "norm_f_w":
            return a[0]
        return a

    outs = [loss_total, grad_x[None]]
    for idx in range(4):
        outs += [shaped(k, res[k][idx]) for k in WEIGHT_ORDER]
    return tuple(outs)
```

```python
import functools

import jax
import jax.numpy as jnp
from jax import lax
from jax.experimental import pallas as pl
from jax.experimental.pallas import tpu as pltpu

F32 = jnp.float32
BF16 = jnp.bfloat16

N_DEV = 8
D_MODEL = 1024
HEADS = 8
HEAD_DIM = 128
WIDTH = HEADS * HEAD_DIM
DN_CONV = 4
DN_CHUNK = 64
D_FF = 2816
FFN_CONV = 3
EPS = 1e-6
HALO = 16
CHUNK_ROWS = 256
SCAN_CHUNKS = 2
ATT_BLOCK = 256
SB_LOG_ZERO = -104.0
SB_GROUP = 2
SB_HEADS_FWD = 4
SB_HEADS_BWD = 2
SMALL_ROWS = 64

ADAM_LR = 0.001
ADAM_B1 = 0.9
ADAM_B2 = 0.999
ADAM_EPS = 1e-08
ADAM_WD = 0.01
ADAM_STEP = 10

VMEM_LIMIT = 48 * 1024 * 1024


def _params(sem=None, **kw):
    return pltpu.CompilerParams(dimension_semantics=sem, vmem_limit_bytes=VMEM_LIMIT, **kw)


def _tile(n, cap):
    if n <= cap:
        return n
    best = None
    for t in range(128, cap + 1, 128):
        if n % t == 0:
            best = t
    assert best is not None, (n, cap)
    return best


def _dot(a, b, dims):
    return lax.dot_general(a, b, ((dims[0], dims[1]), ((), ())), preferred_element_type=F32)


NN = ((1,), (0,))
NT = ((1,), (1,))
TN = ((0,), (0,))


def _dotb(a, b, dims):
    return _dot(a.astype(BF16), b.astype(BF16), dims)


def _split3(x):
    h1 = x.astype(BF16)
    r1 = x - h1.astype(F32)
    h2 = r1.astype(BF16)
    r2 = r1 - h2.astype(F32)
    return h1, h2, r2.astype(BF16)


def _dot_xr(a, b_exact, dims):
    a1, a2, a3 = _split3(a)
    return _dot(a1, b_exact, dims) + _dot(a2, b_exact, dims) + _dot(a3, b_exact, dims)


def _split2(x):
    h1 = x.astype(BF16)
    return h1, (x - h1.astype(F32)).astype(BF16)


def _dot_xr2(a, b_exact, dims):
    a1, a2 = _split2(a)
    return _dot(a1, b_exact, dims) + _dot(a2, b_exact, dims)


def _dot_xl(a_exact, b, dims):
    b1, b2, b3 = _split3(b)
    return _dot(a_exact, b1, dims) + _dot(a_exact, b2, dims) + _dot(a_exact, b3, dims)


def _dot3(a, b, dims):
    a1 = a.astype(BF16)
    a2 = (a - a1.astype(F32)).astype(BF16)
    b1 = b.astype(BF16)
    b2 = (b - b1.astype(F32)).astype(BF16)
    return _dot(a1, b1, dims) + (_dot(a1, b2, dims) + _dot(a2, b1, dims))


def _sigmoid(x):
    return 1.0 / (1.0 + jnp.exp(-x))


def _log1pexp_neg_abs(x):
    return jnp.log(1.0 + jnp.exp(-jnp.abs(x)))


def _iota(shape, dim):
    return lax.broadcasted_iota(jnp.int32, shape, dim)


def _matmul(a, b, mode, out_dtype, name, add=None, comm=None):
    if mode == "nn":
        (m, k), (k2, n) = a.shape, b.shape
    elif mode == "nt":
        (m, k), (n, k2) = a.shape, b.shape
    else:
        (k, m), (k2, n) = a.shape, b.shape
    assert k == k2, (a.shape, b.shape, mode)
    tm, tn, tk = _tile(m, 1408), _tile(n, 1408), _tile(k, 1536)
    nk = k // tk
    dims = {"nn": NN, "nt": NT, "tn": TN}[mode]

    def body(*refs):
        if add is None:
            a_ref, b_ref, o_ref, acc_ref = refs
        else:
            a_ref, b_ref, add_ref, o_ref, acc_ref = refs
        kk = pl.program_id(2)

        @pl.when(kk == 0)
        def _():
            acc_ref[...] = jnp.zeros_like(acc_ref)

        acc_ref[...] += _dotb(a_ref[...], b_ref[...], dims)

        @pl.when(kk == nk - 1)
        def _():
            r = acc_ref[...]
            if add is not None:
                r = r + add_ref[...].astype(F32)
            o_ref[...] = r.astype(out_dtype)

    if mode == "nn":
        specs = [pl.BlockSpec((tm, tk), lambda i, j, l: (i, l)), pl.BlockSpec((tk, tn), lambda i, j, l: (l, j))]
    elif mode == "nt":
        specs = [pl.BlockSpec((tm, tk), lambda i, j, l: (i, l)), pl.BlockSpec((tn, tk), lambda i, j, l: (j, l))]
    else:
        specs = [pl.BlockSpec((tk, tm), lambda i, j, l: (l, i)), pl.BlockSpec((tk, tn), lambda i, j, l: (l, j))]
    args = [a, b]
    if add is not None:
        specs.append(pl.BlockSpec((tm, tn), lambda i, j, l: (i, j)))
        args.append(add)
    grid = (m // tm, n // tn, nk)

    def when():
        i, j, l = pl.program_id(0), pl.program_id(1), pl.program_id(2)
        first = jnp.logical_and(jnp.logical_and(i == 0, j == 0), l == 0)
        last = jnp.logical_and(jnp.logical_and(i == grid[0] - 1, j == grid[1] - 1), l == nk - 1)
        return first, last, last

    (out,), extra = _host_call(
        body, name, comm, when, [jax.ShapeDtypeStruct((m, n), out_dtype)], grid, specs,
        [pl.BlockSpec((tm, tn), lambda i, j, l: (i, j))], [pltpu.VMEM((tm, tn), F32)],
        ("parallel", "parallel", "arbitrary"), args)
    return out if comm is None else (out, extra)


def _rmsnorm_fwd(x, w, name, comm=None):
    t, d = x.shape
    tr = _tile(t, 512)
    steps = t // tr

    def body(x_ref, w_ref, o_ref):
        xv = x_ref[...]
        r = lax.rsqrt(jnp.mean(xv * xv, axis=1, keepdims=True) + EPS)
        o_ref[...] = (xv * r * w_ref[...]).astype(BF16)

    def when():
        i = pl.program_id(0)
        return i == 0, i == steps // 2, i == steps - 1

    (out,), extra = _host_call(
        body, name, comm, when, [jax.ShapeDtypeStruct((t, d), BF16)], (steps,),
        [pl.BlockSpec((tr, d), lambda i: (i, 0)), pl.BlockSpec((1, d), lambda i: (0, 0))],
        [pl.BlockSpec((tr, d), lambda i: (i, 0))], [], ("parallel",), (x, w))
    return out if comm is None else (out, extra)


def _rmsnorm_bwd(dn, x, w, dres, name):
    t, d = x.shape
    tr = _tile(t, 512)

    def body(dn_ref, x_ref, w_ref, dres_ref, dx_ref, dw_ref):
        i = pl.program_id(0)
        xv = x_ref[...]
        g = dn_ref[...].astype(F32)
        r = lax.rsqrt(jnp.mean(xv * xv, axis=1, keepdims=True) + EPS)
        xh = xv * r
        dxh = g * w_ref[...]
        dx = r * (dxh - xh * jnp.mean(dxh * xh, axis=1, keepdims=True))
        dx_ref[...] = dres_ref[...] + dx

        @pl.when(i == 0)
        def _():
            dw_ref[...] = jnp.zeros_like(dw_ref)

        dw_ref[...] += jnp.sum(g * xh, axis=0, keepdims=True)

    return pl.pallas_call(
        body, name=name,
        out_shape=(jax.ShapeDtypeStruct((t, d), F32), jax.ShapeDtypeStruct((1, d), F32)),
        grid=(t // tr,),
        in_specs=[pl.BlockSpec((tr, d), lambda i: (i, 0)), pl.BlockSpec((tr, d), lambda i: (i, 0)),
                  pl.BlockSpec((1, d), lambda i: (0, 0)), pl.BlockSpec((tr, d), lambda i: (i, 0))],
        out_specs=(pl.BlockSpec((tr, d), lambda i: (i, 0)), pl.BlockSpec((1, d), lambda i: (0, 0))),
        compiler_params=_params(("arbitrary",)),
    )(dn, x, w, dres)


def _final_loss(x2, target, w, name):
    t, d = x2.shape
    tr = _tile(t, 512)

    def body(x_ref, t_ref, w_ref, dx_ref, dw_ref, loss_ref):
        i = pl.program_id(0)
        xv = x_ref[...]
        r = lax.rsqrt(jnp.mean(xv * xv, axis=1, keepdims=True) + EPS)
        xh = xv * r
        err = xh * w_ref[...] - t_ref[...]
        dy = err * (1.0 / d)
        dxh = dy * w_ref[...]
        dx_ref[...] = r * (dxh - xh * jnp.mean(dxh * xh, axis=1, keepdims=True))

        @pl.when(i == 0)
        def _():
            dw_ref[...] = jnp.zeros_like(dw_ref)
            loss_ref[...] = jnp.zeros_like(loss_ref)

        dw_ref[...] += jnp.sum(dy * xh, axis=0, keepdims=True)
        row = jnp.sum(err * err, axis=1, keepdims=True) * (0.5 / d)
        loss_ref[...] += jnp.sum(row, axis=0, keepdims=True)

    return pl.pallas_call(
        body, name=name,
        out_shape=(jax.ShapeDtypeStruct((t, d), F32), jax.ShapeDtypeStruct((1, d), F32),
                   jax.ShapeDtypeStruct((1, 1), F32)),
        grid=(t // tr,),
        in_specs=[pl.BlockSpec((tr, d), lambda i: (i, 0)), pl.BlockSpec((tr, d), lambda i: (i, 0)),
                  pl.BlockSpec((1, d), lambda i: (0, 0))],
        out_specs=(pl.BlockSpec((tr, d), lambda i: (i, 0)), pl.BlockSpec((1, d), lambda i: (0, 0)),
                   pl.BlockSpec((1, 1), lambda i: (0, 0))),
        compiler_params=_params(("arbitrary",)),
    )(x2, target, w)


def _shift_down(cur, prev, k, row):
    r = pltpu.roll(cur, k, 0)
    top, row8 = r[0:8, :], row[0:8, :]
    for m in range(k):
        top = jnp.where(row8 == m, prev[HALO - k + m:HALO - k + m + 1, :], top)
    return jnp.concatenate([top, r[8:, :]], axis=0)


def _shift_up(cur, nxt, k, row, tr):
    r = pltpu.roll(cur, tr - k, 0)
    bottom, row8 = r[tr - 8:, :], row[0:8, :]
    for m in range(k):
        bottom = jnp.where(row8 == 8 - k + m, nxt[m:m + 1, :], bottom)
    return jnp.concatenate([r[:tr - 8, :], bottom], axis=0)


def _fold8(a):
    out = a[0:8, :]
    for r in range(8, a.shape[0], 8):
        out = out + a[r:r + 8, :]
    return out


def _conv_taps(cur, prev, w, ntaps, row):
    taps = [cur if i == ntaps - 1 else _shift_down(cur, prev, ntaps - 1 - i, row) for i in range(ntaps)]
    y = w[0:1, :] * taps[0]
    for i in range(1, ntaps):
        y = y + w[i:i + 1, :] * taps[i]
    return taps, y


def _conv_bwd_data(parts, w, ntaps, out_dtype, name):
    t, chp = parts[0].shape
    npart = len(parts)
    tr, tc = _tile(t, 512), _tile(chp, 1408)
    nc = chp // tc
    nhalo = t // HALO
    last = t // tr - 1

    def body(*refs):
        cur_refs, nxt_refs = refs[:npart], refs[npart:2 * npart]
        w_ref, o_ref = refs[2 * npart], refs[2 * npart + 1]
        i, j = pl.program_id(0), pl.program_id(1)
        row = _iota((tr, 128), 0)
        for c0 in range(0, tc, 128):
            sl = slice(c0, c0 + 128)
            cur, nxt = cur_refs[0][:, sl].astype(F32), nxt_refs[0][:, sl].astype(F32)
            for p in range(1, npart):
                cur = jnp.where(j >= p * nc, cur_refs[p][:, sl].astype(F32), cur)
                nxt = jnp.where(j >= p * nc, nxt_refs[p][:, sl].astype(F32), nxt)
            nxt = jnp.where(i == last, 0.0, nxt)
            wv = w_ref[:, sl]
            y = wv[ntaps - 1:ntaps, :] * cur
            for k in range(1, ntaps):
                y = y + wv[ntaps - 1 - k:ntaps - k, :] * _shift_up(cur, nxt, k, row, tr)
            o_ref[:, sl] = y.astype(out_dtype)

    col = lambda p: (lambda j: jnp.clip(j - p * nc, 0, nc - 1))
    cur_specs = [pl.BlockSpec((tr, tc), lambda i, j, c=col(p): (i, c(j))) for p in range(npart)]
    nxt_specs = [pl.BlockSpec((HALO, tc),
                              lambda i, j, c=col(p): (jnp.minimum((i + 1) * (tr // HALO), nhalo - 1), c(j)))
                 for p in range(npart)]
    return pl.pallas_call(
        body, name=name,
        out_shape=jax.ShapeDtypeStruct((t, npart * chp), out_dtype),
        grid=(t // tr, npart * nc),
        in_specs=cur_specs + nxt_specs + [pl.BlockSpec((ntaps, tc), lambda i, j: (0, j))],
        out_specs=pl.BlockSpec((tr, tc), lambda i, j: (i, j)),
        compiler_params=_params(("parallel", "parallel")),
    )(*parts, *parts, w)


def _ffn_act_fwd(upre, cw, name):
    t = upre.shape[0]
    tr, tc = _tile(t, 512), _tile(D_FF, 1408)
    nj = D_FF // tc

    def body(g_ref, gp_ref, u_ref, up_ref, wg_ref, wu_ref, o_ref):
        i = pl.program_id(0)
        row = _iota((tr, 128), 0)
        for c0 in range(0, tc, 128):
            sl = slice(c0, c0 + 128)
            gp = jnp.where(i == 0, 0.0, gp_ref[:, sl].astype(F32))
            up = jnp.where(i == 0, 0.0, up_ref[:, sl].astype(F32))
            _, gc = _conv_taps(g_ref[:, sl].astype(F32), gp, wg_ref[:, sl], FFN_CONV, row)
            _, uc = _conv_taps(u_ref[:, sl].astype(F32), up, wu_ref[:, sl], FFN_CONV, row)
            o_ref[:, sl] = (gc * _sigmoid(gc) * uc).astype(BF16)

    prev = lambda off: (lambda i, j: (jnp.maximum(i * (tr // HALO) - 1, 0), j + off))
    return pl.pallas_call(
        body, name=name,
        out_shape=jax.ShapeDtypeStruct((t, D_FF), BF16),
        grid=(t // tr, nj),
        in_specs=[pl.BlockSpec((tr, tc), lambda i, j: (i, j)), pl.BlockSpec((HALO, tc), prev(0)),
                  pl.BlockSpec((tr, tc), lambda i, j: (i, j + nj)), pl.BlockSpec((HALO, tc), prev(nj)),
                  pl.BlockSpec((FFN_CONV, tc), lambda i, j: (0, j)),
                  pl.BlockSpec((FFN_CONV, tc), lambda i, j: (0, j + nj))],
        out_specs=pl.BlockSpec((tr, tc), lambda i, j: (i, j)),
        compiler_params=_params(("parallel", "parallel")),
    )(upre, upre, upre, upre, cw, cw)


def _ffn_act_bwd(dact, upre, cw, name):
    t = upre.shape[0]
    tr, tc = _tile(t, 512), _tile(D_FF, 1408)
    nj = D_FF // tc

    def body(da_ref, g_ref, gp_ref, u_ref, up_ref, wg_ref, wu_ref, dg_ref, du_ref, dwg_ref, dwu_ref):
        i = pl.program_id(1)
        row = _iota((CHUNK_ROWS, 128), 0)

        @pl.when(i == 0)
        def _():
            dwg_ref[...] = jnp.zeros_like(dwg_ref)
            dwu_ref[...] = jnp.zeros_like(dwu_ref)

        for c0 in range(0, tc, 128):
            sl = slice(c0, c0 + 128)
            wg, wu = wg_ref[:, sl], wu_ref[:, sl]
            dwg = [jnp.zeros((8, 128), F32)] * FFN_CONV
            dwu = [jnp.zeros((8, 128), F32)] * FFN_CONV
            for r0 in range(0, tr, CHUNK_ROWS):
                rows = slice(r0, r0 + CHUNK_ROWS)
                if r0 == 0:
                    gp = jnp.where(i == 0, 0.0, gp_ref[:, sl].astype(F32))
                    up = jnp.where(i == 0, 0.0, up_ref[:, sl].astype(F32))
                else:
                    gp = g_ref[r0 - HALO:r0, sl].astype(F32)
                    up = u_ref[r0 - HALO:r0, sl].astype(F32)
                gt, gc = _conv_taps(g_ref[rows, sl].astype(F32), gp, wg, FFN_CONV, row)
                ut, uc = _conv_taps(u_ref[rows, sl].astype(F32), up, wu, FFN_CONV, row)
                da = da_ref[rows, sl].astype(F32)
                sg = _sigmoid(gc)
                dgc = da * uc * (sg * (1.0 + gc * (1.0 - sg)))
                duc = da * (gc * sg)
                dg_ref[rows, sl] = dgc.astype(BF16)
                du_ref[rows, sl] = duc.astype(BF16)
                dwg = [dwg[k] + _fold8(dgc * gt[k]) for k in range(FFN_CONV)]
                dwu = [dwu[k] + _fold8(duc * ut[k]) for k in range(FFN_CONV)]
            for k in range(FFN_CONV):
                dwg_ref[k:k + 1, sl] += jnp.sum(dwg[k], axis=0, keepdims=True)
                dwu_ref[k:k + 1, sl] += jnp.sum(dwu[k], axis=0, keepdims=True)

    prev = lambda off: (lambda j, i: (jnp.maximum(i * (tr // HALO) - 1, 0), j + off))
    blk = lambda off: pl.BlockSpec((tr, tc), lambda j, i: (i, j + off))
    wblk = lambda off: pl.BlockSpec((FFN_CONV, tc), lambda j, i: (0, j + off))
    dgc, duc, dwg, dwu = pl.pallas_call(
        body, name=name,
        out_shape=(jax.ShapeDtypeStruct((t, D_FF), BF16), jax.ShapeDtypeStruct((t, D_FF), BF16),
                   jax.ShapeDtypeStruct((FFN_CONV, D_FF), F32), jax.ShapeDtypeStruct((FFN_CONV, D_FF), F32)),
        grid=(nj, t // tr),
        in_specs=[blk(0), blk(0), pl.BlockSpec((HALO, tc), prev(0)), blk(nj), pl.BlockSpec((HALO, tc), prev(nj)),
                  wblk(0), wblk(nj)],
        out_specs=(blk(0), blk(0), wblk(0), wblk(0)),
        compiler_params=_params(("parallel", "arbitrary")),
    )(dact, upre, upre, upre, upre, cw, cw)
    return dgc, duc, dwg, dwu


def _dn_pre_fwd(qkv_pre, cw, name):
    t = qkv_pre.shape[0]
    tr = _tile(t, 512)
    scale = HEAD_DIM ** -0.5

    def body(x_ref, p_ref, w_ref, o_ref):
        i, j = pl.program_id(0), pl.program_id(1)
        row = _iota((tr, HEAD_DIM), 0)
        for h in range(HEADS):
            sl = slice(h * HEAD_DIM, (h + 1) * HEAD_DIM)
            prev = jnp.where(i == 0, 0.0, p_ref[:, sl].astype(F32))
            _, c = _conv_taps(x_ref[:, sl].astype(F32), prev, w_ref[:, sl], DN_CONV, row)
            s = c * _sigmoid(c)
            r = lax.rsqrt(jnp.sum(s * s, axis=1, keepdims=True) + EPS)
            o_ref[:, sl] = s * jnp.where(j == 0, r * scale, jnp.where(j == 1, r, 1.0))

    return pl.pallas_call(
        body, name=name,
        out_shape=jax.ShapeDtypeStruct((t, 3 * WIDTH), F32),
        grid=(t // tr, 3),
        in_specs=[pl.BlockSpec((tr, WIDTH), lambda i, j: (i, j)),
                  pl.BlockSpec((HALO, WIDTH), lambda i, j: (jnp.maximum(i * (tr // HALO) - 1, 0), j)),
                  pl.BlockSpec((DN_CONV, WIDTH), lambda i, j: (0, j))],
        out_specs=pl.BlockSpec((tr, WIDTH), lambda i, j: (i, j)),
        compiler_params=_params(("parallel", "parallel")),
    )(qkv_pre, qkv_pre, cw)


def _dn_pre_bwd(dq, dk, dv, qkv_pre, cw, name):
    t = qkv_pre.shape[0]
    tr = _tile(t, 512)
    scale = HEAD_DIM ** -0.5

    def body(dq_ref, dk_ref, dv_ref, x_ref, p_ref, w_ref, dc_ref, dw_ref):
        j, i = pl.program_id(0), pl.program_id(1)
        row = _iota((CHUNK_ROWS, HEAD_DIM), 0)

        @pl.when(i == 0)
        def _():
            dw_ref[...] = jnp.zeros_like(dw_ref)

        for h in range(HEADS):
            sl = slice(h * HEAD_DIM, (h + 1) * HEAD_DIM)
            wv = w_ref[:, sl]
            dw = [jnp.zeros((8, HEAD_DIM), F32)] * DN_CONV
            for r0 in range(0, tr, CHUNK_ROWS):
                rows = slice(r0, r0 + CHUNK_ROWS)
                if r0 == 0:
                    prev = jnp.where(i == 0, 0.0, p_ref[:, sl].astype(F32))
                else:
                    prev = x_ref[r0 - HALO:r0, sl].astype(F32)
                taps, c = _conv_taps(x_ref[rows, sl].astype(F32), prev, wv, DN_CONV, row)
                d = jnp.where(j == 0, dq_ref[rows, sl] * scale, jnp.where(j == 1, dk_ref[rows, sl], dv_ref[rows, sl]))
                sg = _sigmoid(c)
                s = c * sg
                r = lax.rsqrt(jnp.sum(s * s, axis=1, keepdims=True) + EPS)
                nh = s * r
                ds_norm = r * (d - nh * jnp.sum(nh * d, axis=1, keepdims=True))
                dc = jnp.where(j < 2, ds_norm, d) * (sg * (1.0 + c * (1.0 - sg)))
                dc_ref[rows, sl] = dc.astype(BF16)
                dw = [dw[k] + _fold8(dc * taps[k]) for k in range(DN_CONV)]
            for k in range(DN_CONV):
                dw_ref[k:k + 1, sl] += jnp.sum(dw[k], axis=0, keepdims=True)

    dspec = lambda p: pl.BlockSpec((tr, WIDTH), lambda j, i: (jnp.where(j == p, i, 0), 0))
    return pl.pallas_call(
        body, name=name,
        out_shape=(jax.ShapeDtypeStruct((t, 3 * WIDTH), BF16), jax.ShapeDtypeStruct((DN_CONV, 3 * WIDTH), F32)),
        grid=(3, t // tr),
        in_specs=[dspec(0), dspec(1), dspec(2),
                  pl.BlockSpec((tr, WIDTH), lambda j, i: (i, j)),
                  pl.BlockSpec((HALO, WIDTH), lambda j, i: (jnp.maximum(i * (tr // HALO) - 1, 0), j)),
                  pl.BlockSpec((DN_CONV, WIDTH), lambda j, i: (0, j))],
        out_specs=(pl.BlockSpec((tr, WIDTH), lambda j, i: (i, j)),
                   pl.BlockSpec((DN_CONV, WIDTH), lambda j, i: (0, j))),
        compiler_params=_params(("parallel", "arbitrary")),
    )(dq, dk, dv, qkv_pre, qkv_pre, cw)


def _tri(n, kind):
    r, c = _iota((n, n), 0), _iota((n, n), 1)
    m = {"lower": r >= c, "strict": r > c, "upper": r <= c}[kind]
    return m


GATE_ROWS = 4 * DN_CHUNK


def _chunk_tri(kind):
    r, c = _iota((GATE_ROWS, GATE_ROWS), 0), _iota((GATE_ROWS, GATE_ROWS), 1)
    same = (r // DN_CHUNK) == (c // DN_CHUNK)
    return jnp.where(jnp.logical_and(same, _tri(GATE_ROWS, kind)), 1.0, 0.0).astype(BF16)


def _dn_gates_fwd(hab, alog, dtb, name):
    t = hab.shape[0]
    cc = GATE_ROWS

    def body(h_ref, al_ref, dt_ref, o_ref):
        hv = h_ref[...]
        lane = _iota(hv.shape, 1)
        xa = hv + dt_ref[...]
        sp = jnp.maximum(xa, 0.0) + _log1pexp_neg_abs(xa)
        g = jnp.where(lane < HEADS, -jnp.exp(al_ref[...]) * sp, 0.0)
        gc = _dot_xl(_chunk_tri("lower"), g, NN)
        o_ref[...] = jnp.where(lane < HEADS, gc, jnp.where(lane < 2 * HEADS, _sigmoid(hv), 0.0))

    return pl.pallas_call(
        body, name=name,
        out_shape=jax.ShapeDtypeStruct((t, 128), F32),
        grid=(t // cc,),
        in_specs=[pl.BlockSpec((cc, 128), lambda i: (i, 0)), pl.BlockSpec((1, 128), lambda i: (0, 0)),
                  pl.BlockSpec((1, 128), lambda i: (0, 0))],
        out_specs=pl.BlockSpec((cc, 128), lambda i: (i, 0)),
        compiler_params=_params(("parallel",)),
    )(hab, alog, dtb)


def _dn_gates_bwd(dgates, hab, alog, dtb, name):
    t = hab.shape[0]
    cc = GATE_ROWS

    def body(d_ref, h_ref, al_ref, dt_ref, o_ref, dal_ref, ddt_ref):
        i = pl.program_id(0)
        hv = h_ref[...]
        dv = d_ref[...]
        lane = _iota(hv.shape, 1)
        dg = _dot_xl(_chunk_tri("upper"), jnp.where(lane < HEADS, dv, 0.0), NN)
        xa = hv + dt_ref[...]
        sp = jnp.maximum(xa, 0.0) + _log1pexp_neg_abs(xa)
        ea = jnp.exp(al_ref[...])
        da = jnp.where(lane < HEADS, dg * (-ea) * _sigmoid(xa), 0.0)
        be = _sigmoid(hv)
        db = dv * be * (1.0 - be)
        o_ref[...] = jnp.where(lane < HEADS, da, jnp.where(lane < 2 * HEADS, db, 0.0))

        @pl.when(i == 0)
        def _():
            dal_ref[...] = jnp.zeros_like(dal_ref)
            ddt_ref[...] = jnp.zeros_like(ddt_ref)

        dal_ref[...] += jnp.sum(jnp.where(lane < HEADS, dg * (-ea) * sp, 0.0), axis=0, keepdims=True)
        ddt_ref[...] += jnp.sum(da, axis=0, keepdims=True)

    return pl.pallas_call(
        body, name=name,
        out_shape=(jax.ShapeDtypeStruct((t, 128), F32), jax.ShapeDtypeStruct((1, 128), F32),
                   jax.ShapeDtypeStruct((1, 128), F32)),
        grid=(t // cc,),
        in_specs=[pl.BlockSpec((cc, 128), lambda i: (i, 0)), pl.BlockSpec((cc, 128), lambda i: (i, 0)),
                  pl.BlockSpec((1, 128), lambda i: (0, 0)), pl.BlockSpec((1, 128), lambda i: (0, 0))],
        out_specs=(pl.BlockSpec((cc, 128), lambda i: (i, 0)), pl.BlockSpec((1, 128), lambda i: (0, 0)),
                   pl.BlockSpec((1, 128), lambda i: (0, 0))),
        compiler_params=_params(("arbitrary",)),
    )(dgates, hab, alog, dtb)


def _dn_chunk_common(gates, h):
    cc = DN_CHUNK
    lane = _iota(gates.shape, 1)
    gh = jnp.where(lane == h, gates, 0.0)
    gc_col = jnp.sum(gh, axis=1, keepdims=True)
    gc_row = _dot_xl(jnp.ones((cc, 128), BF16), gh, NT)
    beta = jnp.sum(jnp.where(lane == h + HEADS, gates, 0.0), axis=1, keepdims=True)
    lower = _tri(cc, "lower")
    decay = jnp.where(lower, jnp.exp(jnp.where(lower, gc_col - gc_row, 0.0)), 0.0)
    gc_last = gc_col[cc - 1:cc, :]
    return gc_col, gc_last, beta, decay


def _dn_local_fwd(act, gates, name):
    t = act.shape[0]
    cc = DN_CHUNK
    nc = t // cc

    def body(q_ref, k_ref, v_ref, g_ref, u_ref, w_ref, kd_ref, qg_ref, ti_ref, p_ref):
        gates = g_ref[...]
        eye = jnp.where(_iota((cc, cc), 0) == _iota((cc, cc), 1), 1.0, 0.0)
        hs = range(HEADS)
        sl = [slice(h * HEAD_DIM, (h + 1) * HEAD_DIM) for h in hs]
        q, k, v = ([r[:, s] for s in sl] for r in (q_ref, k_ref, v_ref))
        gc_col, gc_last, beta, decay = zip(*[_dn_chunk_common(gates, h) for h in hs])
        gam = [jnp.exp(g) for g in gc_col]
        kb = [k[h] * beta[h] for h in hs]
        npow = [-jnp.where(_tri(cc, "strict"), _dotb(kb[h], k[h], NT) * decay[h], 0.0) for h in hs]
        tinv = [eye + n for n in npow]
        for _ in range(5):
            npow = [_dot3(n, n, NN) for n in npow]
            tinv = [t + _dot3(t, n, NN) for t, n in zip(tinv, npow)]
        uu = [_dot3(tinv[h], v[h] * beta[h], NN) for h in hs]
        ww = [_dot3(tinv[h], kb[h] * gam[h], NN) for h in hs]
        pp = [jnp.where(_tri(cc, "lower"), _dotb(q[h], k[h], NT) * decay[h], 0.0) for h in hs]
        for h in hs:
            u_ref[:, sl[h]] = uu[h]
            w_ref[:, sl[h]] = ww[h].astype(BF16)
            kd_ref[:, sl[h]] = (k[h] * jnp.exp(gc_last[h] - gc_col[h])).astype(BF16)
            qg_ref[:, sl[h]] = (q[h] * gam[h]).astype(BF16)
            ti_ref[h] = tinv[h]
            p_ref[h] = pp[h].astype(BF16)

    row = lambda off: pl.BlockSpec((cc, WIDTH), lambda n: (n, off))
    mat = pl.BlockSpec((HEADS, cc, cc), lambda n: (0, n, 0))
    tw, tb = jax.ShapeDtypeStruct((t, WIDTH), F32), jax.ShapeDtypeStruct((t, WIDTH), BF16)
    hm, hb = jax.ShapeDtypeStruct((HEADS, t, cc), F32), jax.ShapeDtypeStruct((HEADS, t, cc), BF16)
    return pl.pallas_call(
        body, name=name,
        out_shape=(tw, tb, tb, tb, hm, hb),
        grid=(nc,),
        in_specs=[row(0), row(1), row(2), pl.BlockSpec((cc, 128), lambda n: (n, 0))],
        out_specs=(row(0), row(0), row(0), row(0), mat, mat),
        compiler_params=_params(("parallel",)),
    )(act, act, act, gates)


def _dn_scan_fwd(u, w, kd, qg, p, gates, name):
    t = u.shape[0]
    cc = DN_CHUNK
    nc = t // cc
    per = SCAN_CHUNKS

    def body(u_ref, w_ref, kd_ref, qg_ref, p_ref, g_ref, o_ref, sh_ref, s_ref):
        n = pl.program_id(0)

        @pl.when(n == 0)
        def _():
            s_ref[...] = jnp.zeros_like(s_ref)

        hs = range(HEADS)
        sl = [slice(h * HEAD_DIM, (h + 1) * HEAD_DIM) for h in hs]
        s = [s_ref[h] for h in hs]
        for c in range(per):
            r = slice(c * cc, (c + 1) * cc)
            glast = jnp.exp(g_ref[(c + 1) * cc - 1:(c + 1) * cc, :])
            sb = [a.astype(BF16) for a in s]
            vn = [u_ref[r, sl[h]] - _dot(w_ref[r, sl[h]].astype(BF16), sb[h], NN) for h in hs]
            vnb = [a.astype(BF16) for a in vn]
            o_state = [_dot(qg_ref[r, sl[h]].astype(BF16), sb[h], NN) for h in hs]
            o_local = [_dot(p_ref[h, r, :].astype(BF16), vnb[h], NN) for h in hs]
            s_add = [_dot(kd_ref[r, sl[h]].astype(BF16), vnb[h], TN) for h in hs]
            for h in hs:
                o_ref[r, sl[h]] = o_state[h] + o_local[h]
                sh_ref[c, h] = sb[h]
            s = [glast[:, h:h + 1] * s[h] + s_add[h] for h in hs]
        for h in hs:
            s_ref[h] = s[h]

    row = pl.BlockSpec((per * cc, WIDTH), lambda n: (n, 0))
    return pl.pallas_call(
        body, name=name,
        out_shape=(jax.ShapeDtypeStruct((t, WIDTH), F32),
                   jax.ShapeDtypeStruct((nc, HEADS, HEAD_DIM, HEAD_DIM), BF16)),
        grid=(nc // per,),
        in_specs=[row, row, row, row, pl.BlockSpec((HEADS, per * cc, cc), lambda n: (0, n, 0)),
                  pl.BlockSpec((per * cc, 128), lambda n: (n, 0))],
        out_specs=(row, pl.BlockSpec((per, HEADS, HEAD_DIM, HEAD_DIM), lambda n: (n, 0, 0, 0))),
        scratch_shapes=[pltpu.VMEM((HEADS, HEAD_DIM, HEAD_DIM), F32)],
        compiler_params=_params(("arbitrary",)),
    )(u, w, kd, qg, p, gates)


def _dn_scan_bwd(do, w, kd, qg, p, gates, name):
    t = do.shape[0]
    cc = DN_CHUNK
    nc = t // cc
    per = SCAN_CHUNKS
    nb = nc // per

    def body(do_ref, w_ref, kd_ref, qg_ref, p_ref, g_ref, dvn_ref, dsh_ref, ds_ref):
        n = pl.program_id(0)

        @pl.when(n == 0)
        def _():
            ds_ref[...] = jnp.zeros_like(ds_ref)

        hs = range(HEADS)
        sl = [slice(h * HEAD_DIM, (h + 1) * HEAD_DIM) for h in hs]
        ds = [ds_ref[h] for h in hs]
        for c in reversed(range(per)):
            r = slice(c * cc, (c + 1) * cc)
            glast = jnp.exp(g_ref[(c + 1) * cc - 1:(c + 1) * cc, :])
            dob = [do_ref[r, sl[h]].astype(BF16) for h in hs]
            dvn = [_dot(p_ref[h, r, :].astype(BF16), dob[h], TN)
                   + _dot(kd_ref[r, sl[h]].astype(BF16), ds[h].astype(BF16), NN) for h in hs]
            ds_q = [_dot(qg_ref[r, sl[h]].astype(BF16), dob[h], TN) for h in hs]
            ds_w = [_dot(w_ref[r, sl[h]].astype(BF16), dvn[h].astype(BF16), TN) for h in hs]
            for h in hs:
                dvn_ref[r, sl[h]] = dvn[h]
                dsh_ref[c, h] = ds[h].astype(BF16)
            ds = [ds_q[h] + glast[:, h:h + 1] * ds[h] - ds_w[h] for h in hs]
        for h in hs:
            ds_ref[h] = ds[h]

    row = pl.BlockSpec((per * cc, WIDTH), lambda n: (nb - 1 - n, 0))
    return pl.pallas_call(
        body, name=name,
        out_shape=(jax.ShapeDtypeStruct((t, WIDTH), F32),
                   jax.ShapeDtypeStruct((nc, HEADS, HEAD_DIM, HEAD_DIM), BF16)),
        grid=(nb,),
        in_specs=[row, row, row, row, pl.BlockSpec((HEADS, per * cc, cc), lambda n: (0, nb - 1 - n, 0)),
                  pl.BlockSpec((per * cc, 128), lambda n: (nb - 1 - n, 0))],
        out_specs=(row, pl.BlockSpec((per, HEADS, HEAD_DIM, HEAD_DIM), lambda n: (nb - 1 - n, 0, 0, 0))),
        scratch_shapes=[pltpu.VMEM((HEADS, HEAD_DIM, HEAD_DIM), F32)],
        compiler_params=_params(("arbitrary",)),
    )(do, w, kd, qg, p, gates)


def _dn_local_bwd(act, gates, u, w, kd, qg, tinv, p, sh, dsh, dvn, do, name):
    t = act.shape[0]
    cc = DN_CHUNK
    nc = t // cc

    def body(q_ref, k_ref, v_ref, g_ref, u_ref, w_ref, kd_ref, qg_ref, ti_ref, p_ref, s_ref, ds_ref,
             dvn_ref, do_ref, dq_ref, dk_ref, dv_ref, dg_ref):
        gates_v = g_ref[...]
        lower, strict = _tri(cc, "lower"), _tri(cc, "strict")
        ones = jnp.ones((cc, 128), BF16)
        rowc = _iota((cc, 1), 0)
        lane = _iota((cc, 128), 1)
        hs = range(HEADS)
        sl = [slice(h * HEAD_DIM, (h + 1) * HEAD_DIM) for h in hs]
        q, k, v, uu, ww, kd, qg, dvn, do = ([r[:, s] for s in sl] for r in (
            q_ref, k_ref, v_ref, u_ref, w_ref, kd_ref, qg_ref, dvn_ref, do_ref))
        gc_col, gc_last, beta, decay = zip(*[_dn_chunk_common(gates_v, h) for h in hs])
        gam = [jnp.exp(g) for g in gc_col]
        kb = [k[h] * beta[h] for h in hs]
        s_in = [s_ref[0, h] for h in hs]
        ds_out = [ds_ref[0, h] for h in hs]
        tinv = [ti_ref[h] for h in hs]

        a = [jnp.where(strict, _dotb(kb[h], k[h], NT) * decay[h], 0.0) for h in hs]
        vn = [uu[h] - _dotb(ww[h], s_in[h], NN) for h in hs]
        dqg = [_dotb(do[h], s_in[h], NT) for h in hs]
        dw = [-_dotb(dvn[h], s_in[h], NT) for h in hs]
        dp = [jnp.where(lower, _dotb(do[h], vn[h], NT), 0.0) for h in hs]
        dkd = [_dotb(vn[h], ds_out[h], NT) for h in hs]
        dru = [_dot3(tinv[h], dvn[h], TN) for h in hs]
        drw = [_dot3(tinv[h], dw[h], TN) for h in hs]
        da = [-jnp.where(strict, _dotb(dru[h], uu[h], NT) + _dotb(drw[h], ww[h], NT), 0.0) for h in hs]
        dad = [da[h] * decay[h] for h in hs]
        dpd = [dp[h] * decay[h] for h in hs]
        dkb = [_dotb(dad[h], k[h], NN) + gam[h] * drw[h] for h in hs]
        dk = [_dotb(dad[h], kb[h], TN) + _dotb(dpd[h], q[h], TN) + beta[h] * dkb[h]
              + jnp.exp(gc_last[h] - gc_col[h]) * dkd[h] for h in hs]
        dq = [gam[h] * dqg[h] + _dotb(dpd[h], k[h], NN) for h in hs]
        gm = [da[h] * a[h] + dp[h] * p_ref[h] for h in hs]
        colsum = [_dot_xr(gm[h], ones, TN)[:, 0:1] for h in hs]

        dgates = jnp.zeros((cc, 128), F32)
        for h in hs:
            dk_ref[:, sl[h]] = dk[h]
            dq_ref[:, sl[h]] = dq[h]
            dv_ref[:, sl[h]] = beta[h] * dru[h]
            dbeta = (jnp.sum(dkb[h] * k[h], axis=1, keepdims=True)
                     + jnp.sum(dru[h] * v[h], axis=1, keepdims=True))
            rkd = jnp.sum(dkd[h] * kd[h], axis=1, keepdims=True)
            dgc = (jnp.sum(gm[h], axis=1, keepdims=True) - colsum[h]
                   + jnp.sum(dqg[h] * qg[h], axis=1, keepdims=True)
                   + jnp.sum(drw[h] * kb[h], axis=1, keepdims=True) * gam[h] - rkd)
            tail = jnp.sum(rkd, axis=0, keepdims=True) + jnp.exp(gc_last[h]) * jnp.sum(
                jnp.sum(s_in[h].astype(F32) * ds_out[h].astype(F32), axis=1, keepdims=True), axis=0, keepdims=True)
            dgc = dgc + jnp.where(rowc == cc - 1, tail, 0.0)
            dgates = dgates + jnp.where(lane == h, dgc, 0.0) + jnp.where(lane == h + HEADS, dbeta, 0.0)
        dg_ref[...] = dgates

    row = lambda off: pl.BlockSpec((cc, WIDTH), lambda n: (n, off))
    mat = pl.BlockSpec((HEADS, cc, cc), lambda n: (0, n, 0))
    st = pl.BlockSpec((1, HEADS, HEAD_DIM, HEAD_DIM), lambda n: (n, 0, 0, 0))
    gl = pl.BlockSpec((cc, 128), lambda n: (n, 0))
    tw = jax.ShapeDtypeStruct((t, WIDTH), F32)
    return pl.pallas_call(
        body, name=name,
        out_shape=(tw, tw, tw, jax.ShapeDtypeStruct((t, 128), F32)),
        grid=(nc,),
        in_specs=[row(0), row(1), row(2), gl, row(0), row(0), row(0), row(0), mat, mat, st, st, row(0), row(0)],
        out_specs=(row(0), row(0), row(0), gl),
        compiler_params=_params(("parallel",)),
    )(act, act, act, gates, u, w, kd, qg, tinv, p, sh, dsh, dvn, do)


def _dn_post_fwd(o, gate, w, name):
    t = o.shape[0]
    tr = _tile(t, 512)

    def body(o_ref, g_ref, w_ref, y_ref):
        for h in range(HEADS):
            sl = slice(h * HEAD_DIM, (h + 1) * HEAD_DIM)
            ov, gv = o_ref[:, sl], g_ref[:, sl].astype(F32)
            r = lax.rsqrt(jnp.mean(ov * ov, axis=1, keepdims=True) + EPS)
            y_ref[:, sl] = (ov * r * w_ref[...] * (gv * _sigmoid(gv))).astype(BF16)

    blk = pl.BlockSpec((tr, WIDTH), lambda i: (i, 0))
    return pl.pallas_call(
        body, name=name,
        out_shape=jax.ShapeDtypeStruct((t, WIDTH), BF16),
        grid=(t // tr,),
        in_specs=[blk, blk, pl.BlockSpec((1, HEAD_DIM), lambda i: (0, 0))],
        out_specs=blk,
        compiler_params=_params(("parallel",)),
    )(o, gate, w)


def _dn_post_bwd(dy, o, gate, w, name):
    t = o.shape[0]
    tr = _tile(t, 512)

    def body(dy_ref, o_ref, g_ref, w_ref, do_ref, dg_ref, dw_ref):
        i = pl.program_id(0)

        @pl.when(i == 0)
        def _():
            dw_ref[...] = jnp.zeros_like(dw_ref)

        dw = jnp.zeros((1, HEAD_DIM), F32)
        for h in range(HEADS):
            sl = slice(h * HEAD_DIM, (h + 1) * HEAD_DIM)
            ov, gv, dyv = o_ref[:, sl], g_ref[:, sl].astype(F32), dy_ref[:, sl].astype(F32)
            r = lax.rsqrt(jnp.mean(ov * ov, axis=1, keepdims=True) + EPS)
            oh = ov * r
            sg = _sigmoid(gv)
            dg_ref[:, sl] = (dyv * oh * w_ref[...] * (sg * (1.0 + gv * (1.0 - sg)))).astype(BF16)
            dn = dyv * (gv * sg)
            doh = dn * w_ref[...]
            do_ref[:, sl] = r * (doh - oh * jnp.mean(doh * oh, axis=1, keepdims=True))
            dw = dw + jnp.sum(dn * oh, axis=0, keepdims=True)
        dw_ref[...] += dw

    blk = pl.BlockSpec((tr, WIDTH), lambda i: (i, 0))
    return pl.pallas_call(
        body, name=name,
        out_shape=(jax.ShapeDtypeStruct((t, WIDTH), F32), jax.ShapeDtypeStruct((t, WIDTH), BF16),
                   jax.ShapeDtypeStruct((1, HEAD_DIM), F32)),
        grid=(t // tr,),
        in_specs=[blk, blk, blk, pl.BlockSpec((1, HEAD_DIM), lambda i: (0, 0))],
        out_specs=(blk, blk, pl.BlockSpec((1, HEAD_DIM), lambda i: (0, 0))),
        compiler_params=_params(("arbitrary",)),
    )(dy, o, gate, w)


def _sb_scores(qs, k_ref, qi, it, carries, uincl):
    bk = ATT_BLOCK
    scale = HEAD_DIM ** -0.5
    heads, groups = range(len(qs)), range(SB_GROUP)
    lane = [slice(e * HEAD_DIM, (e + 1) * HEAD_DIM) for e in heads]
    js = [qi - SB_GROUP * it - g for g in groups]
    rows = [pl.ds(pl.multiple_of(jnp.maximum(j, 0) * bk, bk), bk) for j in js]
    qpos = qi * bk + _iota((bk, bk), 0)
    col = _iota((bk, bk), 1)
    mask1 = [jnp.logical_and(j * bk + col < qpos, j >= 0) for j in js]
    ks = [[k_ref[r, lane[e]] for r in rows] for e in heads]
    z = [[_dot(qs[e], k, NT) * scale for k in ks[e]] for e in heads]
    soft = [[_log1pexp_neg_abs(a) for a in ze] for ze in z]
    lk_full = [[-(jnp.maximum(a, 0.0) + s) for a, s in zip(z[e], soft[e])] for e in heads]
    lk = [[jnp.where(m, a, 0.0) for m, a in zip(mask1, lk_full[e])] for e in heads]
    ls = [[jnp.minimum(a, 0.0) - s for a, s in zip(z[e], soft[e])] for e in heads]
    incl = [[_dot_xr2(a, uincl, NN) for a in lk[e]] for e in heads]
    weights, out_carries = [], []
    for e in heads:
        cb, we = carries[e], []
        for g in groups:
            we.append(jnp.where(mask1[g], jnp.exp(ls[e][g] + (cb + incl[e][g] - lk[e][g])), 0.0))
            cb = cb + incl[e][g][:, 0:1]
        weights.append(we)
        out_carries.append(cb)
    return rows, ks, weights, mask1, lk_full, ls, out_carries


def _sb_more(qi, carry):
    it, cbs = carry[0], carry[1]
    live = jnp.max(cbs[0])
    for cb in cbs[1:]:
        live = jnp.maximum(live, jnp.max(cb))
    return jnp.logical_and(SB_GROUP * it <= qi, live > SB_LOG_ZERO)


def _sb_steps(groups, nq):
    def when():
        h, i = pl.program_id(0), pl.program_id(1)
        return (jnp.logical_and(h == 0, i == 0), jnp.logical_and(h == groups // 2, i == 0),
                jnp.logical_and(h == groups - 1, i == nq - 1))
    return when


def _sb_fwd(qkv, name, comm=None):
    t = qkv.shape[0]
    bk = ATT_BLOCK
    hp, wide = SB_HEADS_FWD, SB_HEADS_FWD * HEAD_DIM
    lane = [slice(e * HEAD_DIM, (e + 1) * HEAD_DIM) for e in range(hp)]

    def body(q_ref, k_ref, v_ref, o_ref):
        qi = pl.program_id(1)
        qs = [q_ref[:, s] for s in lane]
        uincl = jnp.where(_tri(bk, "lower"), 1.0, 0.0).astype(BF16)

        def step(carry):
            it, cbs, accs = carry
            rows, _, weights, _, _, _, cbs = _sb_scores(qs, k_ref, qi, it, cbs, uincl)
            accs = list(accs)
            for e in range(hp):
                for r, a in zip(rows, weights[e]):
                    accs[e] = accs[e] + _dot(a.astype(BF16), v_ref[r, lane[e]], NN)
            return it + 1, tuple(cbs), tuple(accs)

        init = (jnp.int32(0), (jnp.zeros((bk, 1), F32),) * hp, (jnp.zeros((bk, HEAD_DIM), F32),) * hp)
        _, _, accs = lax.while_loop(functools.partial(_sb_more, qi), step, init)
        for e in range(hp):
            o_ref[:, lane[e]] = accs[e]

    groups = HEADS // hp
    (o,), extra = _host_call(
        body, name, comm, _sb_steps(groups, t // bk), [jax.ShapeDtypeStruct((t, WIDTH), F32)], (groups, t // bk),
        [pl.BlockSpec((bk, wide), lambda h, i: (i, h)),
         pl.BlockSpec((t, wide), lambda h, i: (0, groups + h)),
         pl.BlockSpec((t, wide), lambda h, i: (0, 2 * groups + h))],
        [pl.BlockSpec((bk, wide), lambda h, i: (i, h))], [], ("parallel", "arbitrary"), (qkv, qkv, qkv))
    return o, extra


def _sb_bwd(qkv, o, do, name, comm=None):
    assert do.dtype == BF16
    t = qkv.shape[0]
    bk = ATT_BLOCK
    scale = HEAD_DIM ** -0.5
    hp, wide = SB_HEADS_BWD, SB_HEADS_BWD * HEAD_DIM
    lane = [slice(e * HEAD_DIM, (e + 1) * HEAD_DIM) for e in range(hp)]

    def body(q_ref, k_ref, v_ref, o_ref, do_ref, dq_ref, dk_out, dv_out, dk_ref, dv_ref):
        qi = pl.program_id(1)

        @pl.when(qi == 0)
        def _():
            dk_ref[...] = jnp.zeros_like(dk_ref)
            dv_ref[...] = jnp.zeros_like(dv_ref)

        heads, groups = range(hp), range(SB_GROUP)
        qs = [q_ref[:, s] for s in lane]
        dob = [do_ref[:, s] for s in lane]
        dsum = [jnp.sum(dob[e].astype(F32) * o_ref[:, lane[e]], axis=1, keepdims=True) for e in heads]
        uincl = jnp.where(_tri(bk, "lower"), 1.0, 0.0).astype(BF16)

        def step(carry):
            it, cbs, ces, dqs = carry
            rows, ks, weights, mask, lk_full, ls, cbs = _sb_scores(qs, k_ref, qi, it, cbs, uincl)
            ab = [[a.astype(BF16) for a in weights[e]] for e in heads]
            vs = [[v_ref[r, lane[e]] for r in rows] for e in heads]
            dla = [[ab[e][g].astype(F32) * _dot(dob[e], vs[e][g], NT) for g in groups] for e in heads]
            suf = [[_dot_xr2(a, uincl, NN) for a in dla[e]] for e in heads]
            ces, dqs = list(ces), list(dqs)
            for e in heads:
                for g in groups:
                    err = dsum[e] - (ces[e] + suf[e][g])
                    ces[e] = ces[e] + suf[e][g][:, 0:1]
                    dz = jnp.where(mask[g], dla[e][g] * jnp.exp(lk_full[e][g]) - err * jnp.exp(ls[e][g]), 0.0)
                    dzb = (dz * scale).astype(BF16)
                    dqs[e] = dqs[e] + _dot(dzb, ks[e][g], NN)
                    dk_ref[rows[g], lane[e]] += _dot(dzb, qs[e], TN)
                    dv_ref[rows[g], lane[e]] += _dot(ab[e][g], dob[e], TN)
            return it + 1, tuple(cbs), tuple(ces), tuple(dqs)

        zc = (jnp.zeros((bk, 1), F32),) * hp
        init = (jnp.int32(0), zc, zc, (jnp.zeros((bk, HEAD_DIM), F32),) * hp)
        dqs = lax.while_loop(functools.partial(_sb_more, qi), step, init)[3]
        for e in heads:
            dq_ref[:, lane[e]] = dqs[e].astype(BF16)

        @pl.when(qi == t // bk - 1)
        def _():
            dk_out[...] = dk_ref[...].astype(BF16)
            dv_out[...] = dv_ref[...].astype(BF16)

    ngroup = HEADS // hp
    tw = jax.ShapeDtypeStruct((t, WIDTH), BF16)
    qb = pl.BlockSpec((bk, wide), lambda h, i: (i, h))
    full = lambda off: pl.BlockSpec((t, wide), lambda h, i: (0, off + h))
    return _host_call(
        body, name, comm, _sb_steps(ngroup, t // bk), [tw, tw, tw], (ngroup, t // bk),
        [qb, full(ngroup), full(2 * ngroup), qb, qb], [qb, full(0), full(0)],
        [pltpu.VMEM((t, wide), F32), pltpu.VMEM((t, wide), F32)], ("parallel", "arbitrary"),
        (qkv, qkv, qkv, o, do))


def _merge_fwd(pd, ps, gl, name):
    t = pd.shape[0]
    tr, tc = _tile(t, 512), 512
    nj = D_MODEL // tc

    def body(pd_ref, ps_ref, gd_ref, gs_ref, o_ref):
        gd, gs = gd_ref[...].astype(F32), gs_ref[...].astype(F32)
        o_ref[...] = (_sigmoid(gd) * pd_ref[...].astype(F32) + _sigmoid(gs) * ps_ref[...].astype(F32)).astype(BF16)

    blk = lambda off: pl.BlockSpec((tr, tc), lambda i, j: (i, j + off))
    return pl.pallas_call(
        body, name=name,
        out_shape=jax.ShapeDtypeStruct((t, D_MODEL), BF16),
        grid=(t // tr, nj),
        in_specs=[blk(0), blk(0), blk(0), blk(nj)],
        out_specs=blk(0),
        compiler_params=_params(("parallel", "parallel")),
    )(pd, ps, gl, gl)


def _merge_bwd(dm, pd, ps, gl, name):
    t = pd.shape[0]
    tr, tc = _tile(t, 512), 512
    nj = D_MODEL // tc

    def body(dm_ref, pd_ref, ps_ref, gd_ref, gs_ref, dpd_ref, dps_ref, dgd_ref, dgs_ref):
        dmv = dm_ref[...].astype(F32)
        sd, ss = _sigmoid(gd_ref[...].astype(F32)), _sigmoid(gs_ref[...].astype(F32))
        dpd_ref[...] = (dmv * sd).astype(BF16)
        dps_ref[...] = (dmv * ss).astype(BF16)
        dgd_ref[...] = (dmv * pd_ref[...].astype(F32) * sd * (1.0 - sd)).astype(BF16)
        dgs_ref[...] = (dmv * ps_ref[...].astype(F32) * ss * (1.0 - ss)).astype(BF16)

    blk = lambda off: pl.BlockSpec((tr, tc), lambda i, j: (i, j + off))
    out = jax.ShapeDtypeStruct((t, D_MODEL), BF16)
    return pl.pallas_call(
        body, name=name,
        out_shape=(out, out, out, out),
        grid=(t // tr, nj),
        in_specs=[blk(0), blk(0), blk(0), blk(0), blk(nj)],
        out_specs=(blk(0), blk(0), blk(0), blk(0)),
        compiler_params=_params(("parallel", "parallel")),
    )(dm, pd, ps, gl, gl)


def _local_step(x, target, wts, plan=None, n1=None):
    if n1 is None:
        n1 = _rmsnorm_fwd(x, wts["norm1_w"], "norm1_fwd")
    qkv_pre = _matmul(n1, wts["w_dnqkv_t"], "nt", BF16, "in_dnqkv")
    hgate = _matmul(n1, wts["w_dngate_t"], "nt", BF16, "in_dngate")
    sbqkv = _matmul(n1, wts["w_sbqkv_t"], "nt", BF16, "in_sbqkv")
    gl = _matmul(n1, wts["w_gl_t"], "nt", BF16, "in_gl")
    hab = _matmul(n1, wts["w_ab_t"], "nt", F32, "in_ab")

    act = _dn_pre_fwd(qkv_pre, wts["dn_conv_w"], "dn_pre_fwd")
    gates = _dn_gates_fwd(hab, wts["alog"], wts["dtb"], "dn_gates_fwd")
    u, w, kd, qg, tinv, p = _dn_local_fwd(act, gates, "dn_local_fwd")
    o_dn, sh = _dn_scan_fwd(u, w, kd, qg, p, gates, "dn_scan_fwd")
    y_dn = _dn_post_fwd(o_dn, hgate, wts["dn_norm_w"], "dn_post_fwd")

    o_sb, late = _sb_fwd(sbqkv, "sb_fwd", comm=plan.late_gather() if plan else None)
    if plan:
        wts = {**wts, **plan.late_weights(late)}

    pd = _matmul(y_dn, wts["w_proj_dn"], "nn", BF16, "proj_dn")
    ps = _matmul(o_sb, wts["w_proj_sb"], "nn", BF16, "proj_sb")
    mixed = _merge_fwd(pd, ps, gl, "merge_fwd")
    x1 = _matmul(mixed, wts["w_out"], "nn", F32, "out_proj", add=x)

    n2 = _rmsnorm_fwd(x1, wts["norm2_w"], "norm2_fwd")
    upre = _matmul(n2, wts["ffn_w_up_t"], "nt", BF16, "ffn_up")
    fact = _ffn_act_fwd(upre, wts["ffn_conv_w"], "ffn_act_fwd")
    x2 = _matmul(fact, wts["ffn_w_down"], "nn", F32, "ffn_down", add=x1)

    dx2, g_normf, loss = _final_loss(x2, target, wts["norm_f_w"], "final_loss")

    dfact = _matmul(dx2, wts["ffn_w_down"], "nt", BF16, "ffn_down_dx")
    g_wdown = _matmul(fact, dx2, "tn", BF16, "ffn_down_dw")
    dgc, duc, dwg, dwu = _ffn_act_bwd(dfact, upre, wts["ffn_conv_w"], "ffn_act_bwd")
    g_fconv = jnp.concatenate([dwg, dwu], axis=1)
    dupre = _conv_bwd_data([dgc, duc], wts["ffn_conv_w"], FFN_CONV, BF16, "ffn_conv_bwd")
    dn2 = _matmul(dupre, wts["ffn_w_up_t"], "nn", F32, "ffn_up_dx")
    g_wup = _matmul(dupre, n2, "tn", BF16, "ffn_up_dw")
    dx1, g_norm2 = _rmsnorm_bwd(dn2, x1, wts["norm2_w"], dx2, "norm2_bwd")

    dmixed = _matmul(dx1, wts["w_out"], "nt", BF16, "out_proj_dx")
    g_wout = _matmul(mixed, dx1, "tn", BF16, "out_proj_dw")
    dpd, dps, dgd, dgs = _merge_bwd(dmixed, pd, ps, gl, "merge_bwd")
    dy_dn = _matmul(dpd, wts["w_proj_dn"], "nt", BF16, "proj_dn_dx")
    g_wpd = _matmul(y_dn, dpd, "tn", BF16, "proj_dn_dw")
    do_sb = _matmul(dps, wts["w_proj_sb"], "nt", BF16, "proj_sb_dx")
    g_wps = _matmul(o_sb, dps, "tn", BF16, "proj_sb_dw")
    grads = dict(w_proj_dn=g_wpd, w_proj_sb=g_wps, w_out=g_wout, ffn_w_up_t=g_wup, ffn_w_down=g_wdown)

    (dsq, dsk, dsv), got_early = _sb_bwd(sbqkv, o_sb, do_sb, "sb_bwd",
                                         comm=plan.early_grads(grads) if plan else None)

    do_dn, dhgate, g_dnnorm = _dn_post_bwd(dy_dn, o_dn, hgate, wts["dn_norm_w"], "dn_post_bwd")
    dvn, dsh = _dn_scan_bwd(do_dn, w, kd, qg, p, gates, "dn_scan_bwd")
    dq, dk, dv, dgates = _dn_local_bwd(act, gates, u, w, kd, qg, tinv, p, sh, dsh, dvn, do_dn, "dn_local_bwd")
    dhab, g_alog, g_dtb = _dn_gates_bwd(dgates, hab, wts["alog"], wts["dtb"], "dn_gates_bwd")
    dcv, g_dnconv = _dn_pre_bwd(dq, dk, dv, qkv_pre, wts["dn_conv_w"], "dn_pre_bwd")
    dqkv_pre = _conv_bwd_data([dcv], wts["dn_conv_w"], DN_CONV, BF16, "dn_conv_bwd")

    dh = jnp.concatenate([dqkv_pre, dhgate, dsq, dsk, dsv, dgd, dgs], axis=1)
    w_main_t = jnp.concatenate([wts["w_dnqkv_t"], wts["w_dngate_t"], wts["w_sbqkv_t"], wts["w_gl_t"]], axis=0)
    g_wmain = _matmul(dh, n1, "tn", BF16, "in_dw_main")
    g_wab = _matmul(dhab, n1, "tn", BF16, "in_dw_ab")
    grads.update(w_main_t=g_wmain, w_ab_t=g_wab, dn_conv_w=g_dnconv, alog=g_alog, dtb=g_dtb, dn_norm_w=g_dnnorm,
                 norm2_w=g_norm2, ffn_conv_w=g_fconv, norm_f_w=g_normf)
    got_late = []
    if plan:
        dn1, swapped = _matmul(dhab, wts["w_ab_t"], "nn", F32, "in_dx_ab", comm=plan.sibling_swap(grads))
        dn1, got_late = _matmul(dh, w_main_t, "nn", F32, "in_dx_main", add=dn1,
                                comm=plan.late_grads(swapped, grads, loss))
    else:
        dn1 = _matmul(dhab, wts["w_ab_t"], "nn", F32, "in_dx_ab")
        dn1 = _matmul(dh, w_main_t, "nn", F32, "in_dx_main", add=dn1)
    grad_x, g_norm1 = _rmsnorm_bwd(dn1, x, wts["norm1_w"], dx1, "norm1_bwd")
    grads["norm1_w"] = g_norm1
    return loss, grad_x, grads, got_early, got_late


HBM_SPEC = pl.BlockSpec(memory_space=pltpu.HBM)


def _mesh_pos():
    x, y, c = lax.axis_index("x"), lax.axis_index("y"), lax.axis_index("c")
    return x, y, c, 4 * x + 2 * y + c


def _peer(k):
    x, y, c, _ = _mesh_pos()
    px = 1 - x if k & 4 else x
    py = 1 - y if k & 2 else y
    pc = 1 - c if k & 1 else c
    return (px, py, pc), 4 * px + 2 * py + pc


def _rcopy(src, dst, send, recv, a, s, peer):
    return pltpu.make_async_remote_copy(src_ref=src, dst_ref=dst, send_sem=send.at[a, s], recv_sem=recv.at[a, s],
                                        device_id=peer, device_id_type=pl.DeviceIdType.MESH)


class _Gather:
    ICI = (2, 4, 6)

    def __init__(self, shards):
        self.args = list(shards)
        self.n = len(shards)
        self.out_shape = [jax.ShapeDtypeStruct((N_DEV,) + s.shape, s.dtype) for s in shards]
        self.scratch = [pltpu.SemaphoreType.DMA((self.n, N_DEV - 1)), pltpu.SemaphoreType.DMA((self.n, N_DEV - 1)),
                        pltpu.SemaphoreType.DMA((self.n,))]

    def _slot(self, outs, a, d):
        return outs[a].at[d]

    def _first(self, ins, outs, send, recv, a):
        me = _mesh_pos()[3]
        out, got = [], []
        for s, k in enumerate((1,) + self.ICI):
            peer, pidx = _peer(k)
            out.append(_rcopy(ins[a], self._slot(outs, a, me), send, recv, a, s, peer))
            got.append(_rcopy(ins[a], self._slot(outs, a, pidx), send, recv, a, s, peer))
        return out, got

    def _forward(self, ins, outs, send, recv, a):
        sib = _peer(1)[0]
        out, got = [], []
        for s, k in enumerate(self.ICI):
            held = self._slot(outs, a, _peer(k)[1])
            out.append(_rcopy(held, held, send, recv, a, 4 + s, sib))
            other = self._slot(outs, a, _peer(k | 1)[1])
            got.append(_rcopy(other, other, send, recv, a, 4 + s, sib))
        return out, got

    def start(self, ins, outs, sems):
        send, recv, loc = sems
        me = _mesh_pos()[3]
        for a in range(self.n):
            pltpu.make_async_copy(ins[a], self._slot(outs, a, me), loc.at[a]).start()
            for cp in self._first(ins, outs, send, recv, a)[0]:
                cp.start()

    def mid(self, ins, outs, sems):
        send, recv, _ = sems
        for a in range(self.n):
            arrivals = self._first(ins, outs, send, recv, a)[1]
            for s, cp in enumerate(self._forward(ins, outs, send, recv, a)[0]):
                arrivals[1 + s].wait_recv()
                cp.start()

    def finish(self, ins, outs, sems):
        send, recv, loc = sems
        me = _mesh_pos()[3]
        for a in range(self.n):
            first_out, first_got = self._first(ins, outs, send, recv, a)
            fwd_out, fwd_got = self._forward(ins, outs, send, recv, a)
            first_got[0].wait_recv()
            for cp in fwd_got:
                cp.wait_recv()
            for cp in first_out + fwd_out:
                cp.wait_send()
            pltpu.make_async_copy(ins[a], self._slot(outs, a, me), loc.at[a]).wait()


class _Exchange:
    def __init__(self, slabs=(), gathered=(), chip_slabs=(), sibling_slabs=()):
        self.args = list(slabs) + list(chip_slabs) + list(sibling_slabs) + list(gathered)
        self.kind = (["dev"] * len(slabs) + ["chip"] * len(chip_slabs) + ["sib"] * len(sibling_slabs)
                     + ["all"] * len(gathered))
        self.n = len(self.args)
        half = lambda s: jax.ShapeDtypeStruct((N_DEV // 2,) + s.shape[1:], s.dtype)
        self.out_shape = ([jax.ShapeDtypeStruct(s.shape, s.dtype) for s in slabs]
                          + [half(s) for s in chip_slabs] + [half(s) for s in sibling_slabs]
                          + [jax.ShapeDtypeStruct((N_DEV,) + s.shape, s.dtype) for s in gathered])
        self.scratch = [pltpu.SemaphoreType.DMA((self.n, N_DEV - 1)), pltpu.SemaphoreType.DMA((self.n, N_DEV - 1)),
                        pltpu.SemaphoreType.DMA((self.n,))]

    def _copies(self, ins, outs, send, recv, a):
        x, y, c, me = _mesh_pos()
        kind = self.kind[a]
        out, got = [], []
        if kind == "sib":
            sib = _peer(1)[0]
            for q in range(N_DEV // 2):
                out.append(_rcopy(ins[a].at[2 * q + 1 - c], outs[a].at[q], send, recv, a, q, sib))
                got.append(_rcopy(ins[a].at[2 * q + c], outs[a].at[q], send, recv, a, q, sib))
            return out, got
        for k in ((2, 4, 6) if kind == "chip" else range(1, N_DEV)):
            peer, pidx = _peer(k)
            if kind == "chip":
                src, mine, theirs = ins[a].at[2 * peer[0] + peer[1]], 2 * x + y, 2 * peer[0] + peer[1]
            else:
                src, mine, theirs = (ins[a].at[pidx] if kind == "dev" else ins[a]), me, pidx
            out.append(_rcopy(src, outs[a].at[mine], send, recv, a, k - 1, peer))
            got.append(_rcopy(src, outs[a].at[theirs], send, recv, a, k - 1, peer))
        return out, got

    def _local(self, ins, outs, loc, a):
        x, y, _, me = _mesh_pos()
        kind = self.kind[a]
        if kind == "sib":
            return None
        if kind == "chip":
            return pltpu.make_async_copy(ins[a].at[2 * x + y], outs[a].at[2 * x + y], loc.at[a])
        return pltpu.make_async_copy(ins[a].at[me] if kind == "dev" else ins[a], outs[a].at[me], loc.at[a])

    def start(self, ins, outs, sems):
        send, recv, loc = sems
        for a in range(self.n):
            if self._local(ins, outs, loc, a) is not None:
                self._local(ins, outs, loc, a).start()
            for cp in self._copies(ins, outs, send, recv, a)[0]:
                cp.start()

    def mid(self, ins, outs, sems):
        pass

    def finish(self, ins, outs, sems):
        send, recv, loc = sems
        for a in range(self.n):
            out, got = self._copies(ins, outs, send, recv, a)
            for cp in got:
                cp.wait_recv()
            for cp in out:
                cp.wait_send()
            if self._local(ins, outs, loc, a) is not None:
                self._local(ins, outs, loc, a).wait()


def _comm_call(comm, name):
    n = comm.n

    def body(*refs):
        ins, outs, sems = refs[:n], refs[n:2 * n], refs[2 * n:]
        comm.start(ins, outs, sems)
        comm.mid(ins, outs, sems)
        comm.finish(ins, outs, sems)

    return pl.pallas_call(
        body, name=name, out_shape=comm.out_shape, in_specs=[HBM_SPEC] * n, out_specs=[HBM_SPEC] * n,
        scratch_shapes=comm.scratch,
    )(*comm.args)


def _hosted(body, comm, n_in, n_out, when):
    if comm is None:
        return body

    def wrapped(*refs):
        ins, c_ins = refs[:n_in], refs[n_in:n_in + comm.n]
        o0 = n_in + comm.n
        outs, c_outs = refs[o0:o0 + n_out], refs[o0 + n_out:o0 + n_out + comm.n]
        scratch, sems = refs[o0 + n_out + comm.n:len(refs) - 3], refs[len(refs) - 3:]
        first, middle, last = when()

        @pl.when(first)
        def _():
            comm.start(c_ins, c_outs, sems)

        body(*ins, *outs, *scratch)

        @pl.when(middle)
        def _():
            comm.mid(c_ins, c_outs, sems)

        @pl.when(last)
        def _():
            comm.finish(c_ins, c_outs, sems)

    return wrapped


def _host_call(body, name, comm, when, out_shape, grid, in_specs, out_specs, scratch_shapes, sem, args):
    n_in, n_out = len(in_specs), len(out_specs)
    if comm is None:
        res = pl.pallas_call(body, name=name, out_shape=out_shape, grid=grid, in_specs=in_specs, out_specs=out_specs,
                             scratch_shapes=scratch_shapes, compiler_params=_params(sem))(*args)
        return list(res), []
    res = pl.pallas_call(
        _hosted(body, comm, n_in, n_out, when), name=name,
        out_shape=list(out_shape) + comm.out_shape, grid=grid,
        in_specs=list(in_specs) + [HBM_SPEC] * comm.n, out_specs=list(out_specs) + [HBM_SPEC] * comm.n,
        scratch_shapes=list(scratch_shapes) + comm.scratch,
        compiler_params=_params(("arbitrary",) * len(grid)),
    )(*args, *comm.args)
    return list(res[:n_out]), list(res[n_out:])


def _add_my_slabs(slabs, b, name):
    n, rows, cols = b.shape
    tc = _tile(cols, 256)

    def body(a_ref, b_ref, o_ref):
        o_ref[...] = (a_ref[...].astype(F32) + b_ref[...].astype(F32)).astype(o_ref.dtype)

    blk = pl.BlockSpec((None, rows, tc), lambda i, j: (i, 0, j))
    mine = pl.BlockSpec((None, rows, tc), lambda i, j: (2 * i + lax.axis_index("c"), 0, j))
    return pl.pallas_call(
        body, name=name, out_shape=jax.ShapeDtypeStruct(b.shape, b.dtype), grid=(n, cols // tc),
        in_specs=[mine, blk], out_specs=blk, compiler_params=_params(("parallel", "parallel")),
    )(slabs, b)


def _adamw(parts, w, m, v, name):
    rows, cols = w.shape
    nparts = parts.shape[0]
    tr, tc = rows, cols
    for cand in (128, 176):
        if rows > cand and rows % cand == 0:
            tr = cand
            break
    if tr == rows and rows > 512:
        tc = _tile(cols, 256)

    def body(p_ref, w_ref, m_ref, v_ref, g_ref, d_ref, mo_ref, vo_ref):
        g = p_ref[0].astype(F32)
        for s in range(1, nparts):
            g = g + p_ref[s].astype(F32)
        mn = ADAM_B1 * m_ref[...] + (1.0 - ADAM_B1) * g
        vn = ADAM_B2 * v_ref[...] + (1.0 - ADAM_B2) * (g * g)
        m_hat = mn / (1.0 - ADAM_B1 ** ADAM_STEP)
        v_hat = vn / (1.0 - ADAM_B2 ** ADAM_STEP)
        g_ref[...] = g
        d_ref[...] = -ADAM_LR * (m_hat / (jnp.sqrt(v_hat) + ADAM_EPS) + ADAM_WD * w_ref[...])
        mo_ref[...] = mn
        vo_ref[...] = vn

    blk = pl.BlockSpec((tr, tc), lambda i, j: (i, j))
    out = jax.ShapeDtypeStruct((rows, cols), F32)
    return pl.pallas_call(
        body, name=name,
        out_shape=(out, out, out, out),
        grid=(rows // tr, cols // tc),
        in_specs=[pl.BlockSpec((nparts, tr, tc), lambda i, j: (0, i, j)), blk, blk, blk],
        out_specs=(blk, blk, blk, blk),
        compiler_params=_params(("parallel", "parallel")),
    )(parts, w, m, v)


CONV_PACK = 8 * 1024
WEIGHT_ORDER = ("norm1_w", "w_in", "dn_conv_w", "dn_A_log", "dn_dt_bias", "dn_norm_w", "w_proj_dn", "w_proj_sb",
                "w_out", "norm2_w", "ffn_w_up", "ffn_conv_w", "ffn_w_down", "norm_f_w")


def _cols_to_slabs(g):
    r, c8 = g.shape
    return g.reshape(r, N_DEV, c8 // N_DEV).transpose(1, 0, 2)


def _slabs_to_cols(s):
    d, r, c = s.shape
    return s.transpose(1, 0, 2).reshape(r, d * c)


def kernel(x, norm1_w, w_in, dn_conv_w, dn_A_log, dn_dt_bias, dn_norm_w, w_proj_dn, w_proj_sb, w_out, norm2_w, ffn_w_up, ffn_conv_w, ffn_w_down, norm_f_w, loss_target, m_norm1_w, m_w_in, m_dn_conv_w, m_dn_A_log, m_dn_dt_bias, m_dn_norm_w, m_w_proj_dn, m_w_proj_sb, m_w_out, m_norm2_w, m_ffn_w_up, m_ffn_conv_w, m_ffn_w_down, m_norm_f_w, v_norm1_w, v_w_in, v_dn_conv_w, v_dn_A_log, v_dn_dt_bias, v_dn_norm_w, v_w_proj_dn, v_w_proj_sb, v_w_out, v_norm2_w, v_ffn_w_up, v_ffn_conv_w, v_ffn_w_down, v_norm_f_w):
    me = _mesh_pos()[3]
    tr = lambda a: jnp.transpose(a[0])
    w_loc = dict(norm1_w=norm1_w, w_in=tr(w_in), dn_conv_w=dn_conv_w[0], dn_A_log=dn_A_log, dn_dt_bias=dn_dt_bias,
                 dn_norm_w=dn_norm_w, w_proj_dn=w_proj_dn[0], w_proj_sb=w_proj_sb[0], w_out=w_out[0],
                 norm2_w=norm2_w, ffn_w_up=tr(ffn_w_up), ffn_conv_w=ffn_conv_w[0], ffn_w_down=ffn_w_down[0],
                 norm_f_w=norm_f_w[None, :])
    m_loc = dict(norm1_w=m_norm1_w, w_in=tr(m_w_in), dn_conv_w=m_dn_conv_w[0], dn_A_log=m_dn_A_log,
                 dn_dt_bias=m_dn_dt_bias, dn_norm_w=m_dn_norm_w, w_proj_dn=m_w_proj_dn[0], w_proj_sb=m_w_proj_sb[0],
                 w_out=m_w_out[0], norm2_w=m_norm2_w, ffn_w_up=tr(m_ffn_w_up), ffn_conv_w=m_ffn_conv_w[0],
                 ffn_w_down=m_ffn_w_down[0], norm_f_w=m_norm_f_w[None, :])
    v_loc = dict(norm1_w=v_norm1_w, w_in=tr(v_w_in), dn_conv_w=v_dn_conv_w[0], dn_A_log=v_dn_A_log,
                 dn_dt_bias=v_dn_dt_bias, dn_norm_w=v_dn_norm_w, w_proj_dn=v_w_proj_dn[0], w_proj_sb=v_w_proj_sb[0],
                 w_out=v_w_out[0], norm2_w=v_norm2_w, ffn_w_up=tr(v_ffn_w_up), ffn_conv_w=v_ffn_conv_w[0],
                 ffn_w_down=v_ffn_w_down[0], norm_f_w=v_norm_f_w[None, :])

    conv_flat = jnp.concatenate([w_loc["dn_conv_w"].reshape(-1), w_loc["ffn_conv_w"].reshape(-1)])
    n_dn, n_ffn = DN_CONV * 3 * WIDTH // N_DEV, FFN_CONV * 2 * D_FF // N_DEV
    conv_pack = jnp.pad(conv_flat, (0, CONV_PACK - n_dn - n_ffn)).reshape(8, 1024)
    n1, (g_in, g_conv) = _rmsnorm_fwd(x[0], norm1_w, "norm1_fwd",
                                      comm=_Gather([w_loc["w_in"].astype(BF16), conv_pack]))
    in_width = g_in.shape[0] * g_in.shape[1]
    w_in_t = g_in.reshape(in_width, D_MODEL)
    g_conv = g_conv.reshape(N_DEV, CONV_PACK)
    dn_conv_full = _slabs_to_cols(g_conv[:, :n_dn].reshape(N_DEV, DN_CONV, 3 * WIDTH // N_DEV))
    ffn_conv_full = _slabs_to_cols(g_conv[:, n_dn:n_dn + n_ffn].reshape(N_DEV, FFN_CONV, 2 * D_FF // N_DEV))
    q_end = 3 * WIDTH
    ab_end = q_end + 2 * HEADS
    gate_end = ab_end + WIDTH
    sb_end = gate_end + 3 * WIDTH
    pad_lanes = lambda a: jnp.pad(a, ((0, 0), (0, 128 - a.shape[1])))
    wts = dict(
        norm1_w=norm1_w, w_dnqkv_t=w_in_t[:q_end], w_ab_t=jnp.pad(w_in_t[q_end:ab_end], ((0, 128 - 2 * HEADS), (0, 0))),
        w_dngate_t=w_in_t[ab_end:gate_end], w_sbqkv_t=w_in_t[gate_end:sb_end], w_gl_t=w_in_t[sb_end:],
        dn_conv_w=dn_conv_full, alog=pad_lanes(dn_A_log), dtb=pad_lanes(dn_dt_bias), dn_norm_w=dn_norm_w,
        norm2_w=norm2_w, ffn_conv_w=ffn_conv_full, norm_f_w=norm_f_w[None, :])

    n_fc = FFN_CONV * 2 * D_FF
    fc_rows = -(-n_fc // D_MODEL)
    dn_rows = DN_CONV * 3 * WIDTH // D_MODEL
    late_names = ("w_proj_dn", "w_proj_sb", "w_out", "ffn_w_up", "ffn_w_down")

    class Plan:
        @staticmethod
        def late_gather():
            return _Gather([w_loc[k].astype(BF16) for k in late_names])

        @staticmethod
        def late_weights(got):
            g_pd, g_ps, g_out, g_up, g_down = got
            return dict(w_proj_dn=g_pd.reshape(WIDTH, D_MODEL), w_proj_sb=g_ps.reshape(WIDTH, D_MODEL),
                        w_out=g_out.reshape(D_MODEL, D_MODEL), ffn_w_up_t=g_up.reshape(2 * D_FF, D_MODEL),
                        ffn_w_down=g_down.reshape(D_FF, D_MODEL))

        @staticmethod
        def early_grads(g):
            return _Exchange([g["w_proj_dn"].reshape(N_DEV, WIDTH // N_DEV, D_MODEL),
                              g["w_proj_sb"].reshape(N_DEV, WIDTH // N_DEV, D_MODEL),
                              g["w_out"].reshape(N_DEV, D_MODEL // N_DEV, D_MODEL),
                              g["ffn_w_up_t"].reshape(N_DEV, 2 * D_FF // N_DEV, D_MODEL),
                              g["ffn_w_down"].reshape(N_DEV, D_FF // N_DEV, D_MODEL)])

        @staticmethod
        def _in_slabs(g):
            g_win_t = jnp.concatenate([g["w_main_t"][:q_end], g["w_ab_t"][:2 * HEADS], g["w_main_t"][q_end:]],
                                      axis=0)
            return g_win_t.reshape(N_DEV, in_width // N_DEV, D_MODEL)

        @staticmethod
        def sibling_swap(g):
            return _Exchange(sibling_slabs=[Plan._in_slabs(g)])

        @staticmethod
        def late_grads(swapped, g, loss):
            chip_sums = _add_my_slabs(Plan._in_slabs(g), swapped[0], "in_dw_chip_sum")
            row3 = jnp.concatenate([g["dn_norm_w"], g["alog"], g["dtb"], jnp.pad(loss, ((0, 0), (0, 127))),
                                    jnp.zeros((1, D_MODEL - 512), F32)], axis=1)
            fconv_rows = jnp.pad(g["ffn_conv_w"].reshape(-1), (0, fc_rows * D_MODEL - n_fc)).reshape(fc_rows, D_MODEL)
            pad8 = lambda a: jnp.pad(a, ((0, -a.shape[0] % 8), (0, 0)))
            pieces = [g["norm2_w"], g["norm_f_w"], row3, g["dn_conv_w"].reshape(dn_rows, D_MODEL), fconv_rows]
            small = jnp.concatenate([pad8(a) for a in pieces], axis=0)
            assert small.shape[0] == SMALL_ROWS
            return _Exchange(chip_slabs=[chip_sums], gathered=[small])

    loss, grad_x, g, got_early, got_late = _local_step(x[0], loss_target[0], wts, Plan, n1)
    r_pd, r_ps, r_out, r_up, r_down = got_early
    r_in, r_small = got_late
    (r_norm1,) = _comm_call(_Exchange([], [jnp.pad(g["norm1_w"], ((0, 7), (0, 0)))]), "gather_norm1")

    parts = dict(w_in=r_in, w_proj_dn=r_pd, w_proj_sb=r_ps, w_out=r_out, ffn_w_up=r_up, ffn_w_down=r_down)
    parts["norm1_w"] = r_norm1[:, 0:1, :]
    parts["norm2_w"] = r_small[:, 0:1, :]
    parts["norm_f_w"] = r_small[:, 8:9, :]
    parts["dn_norm_w"] = r_small[:, 16:17, 0:HEAD_DIM]
    parts["dn_A_log"] = r_small[:, 16:17, 128:128 + HEADS]
    parts["dn_dt_bias"] = r_small[:, 16:17, 256:256 + HEADS]
    dnc = r_small[:, 24:24 + dn_rows, :].reshape(N_DEV, DN_CONV, 3 * WIDTH)
    parts["dn_conv_w"] = lax.dynamic_slice_in_dim(dnc, me * (3 * WIDTH // N_DEV), 3 * WIDTH // N_DEV, axis=2)
    fc0 = 24 + dn_rows + (-dn_rows % 8)
    fcc = r_small[:, fc0:fc0 + fc_rows, :].reshape(N_DEV, fc_rows * D_MODEL)[:, :n_fc]
    fcc = fcc.reshape(N_DEV, FFN_CONV, 2 * D_FF)
    parts["ffn_conv_w"] = lax.dynamic_slice_in_dim(fcc, me * (2 * D_FF // N_DEV), 2 * D_FF // N_DEV, axis=2)
    loss_total = jnp.sum(r_small[:, 16, 384])

    res = {k: _adamw(parts[k], w_loc[k], m_loc[k], v_loc[k], "adamw_" + k) for k in WEIGHT_ORDER}
    lead = ("w_in", "dn_conv_w", "w_proj_dn", "w_proj_sb", "w_out", "ffn_w_up", "ffn_conv_w", "ffn_w_down")

    def shaped(k, a):
        if k in ("w_in", "ffn_w_up"):
            return jnp.transpose(a)[None]
        if k in lead:
            return a[None]
        if k == "norm_f_w":
            return a[0]
        return a

    outs = [loss_total, grad_x[None]]
    for idx in range(4):
        outs += [shaped(k, res[k][idx]) for k in WEIGHT_ORDER]
    return tuple(outs)
```

```python
import functools

import jax
import jax.numpy as jnp
from jax import lax
from jax.experimental import pallas as pl
from jax.experimental.pallas import tpu as pltpu

F32 = jnp.float32
BF16 = jnp.bfloat16

N_DEV = 8
D_MODEL = 1024
HEADS = 8
HEAD_DIM = 128
WIDTH = HEADS * HEAD_DIM
DN_CONV = 4
DN_CHUNK = 64
D_FF = 2816
FFN_CONV = 3
EPS = 1e-6
HALO = 16
CHUNK_ROWS = 256
SCAN_CHUNKS = 4
ATT_BLOCK = 256
SB_LOG_ZERO = -104.0
SB_GROUP = 2
SB_HEADS_FWD = 4
SB_HEADS_BWD = 2
SMALL_ROWS = 64

ADAM_LR = 0.001
ADAM_B1 = 0.9
ADAM_B2 = 0.999
ADAM_EPS = 1e-08
ADAM_WD = 0.01
ADAM_STEP = 10

VMEM_LIMIT = 48 * 1024 * 1024


def _params(sem=None, **kw):
    return pltpu.CompilerParams(dimension_semantics=sem, vmem_limit_bytes=VMEM_LIMIT, **kw)


def _tile(n, cap):
    if n <= cap:
        return n
    best = None
    for t in range(128, cap + 1, 128):
        if n % t == 0:
            best = t
    assert best is not None, (n, cap)
    return best


def _dot(a, b, dims):
    return lax.dot_general(a, b, ((dims[0], dims[1]), ((), ())), preferred_element_type=F32)


NN = ((1,), (0,))
NT = ((1,), (1,))
TN = ((0,), (0,))


def _dotb(a, b, dims):
    return _dot(a.astype(BF16), b.astype(BF16), dims)


def _split3(x):
    h1 = x.astype(BF16)
    r1 = x - h1.astype(F32)
    h2 = r1.astype(BF16)
    r2 = r1 - h2.astype(F32)
    return h1, h2, r2.astype(BF16)


def _dot_xr(a, b_exact, dims):
    a1, a2, a3 = _split3(a)
    return _dot(a1, b_exact, dims) + _dot(a2, b_exact, dims) + _dot(a3, b_exact, dims)


def _split2(x):
    h1 = x.astype(BF16)
    return h1, (x - h1.astype(F32)).astype(BF16)


def _dot_xr2(a, b_exact, dims):
    a1, a2 = _split2(a)
    return _dot(a1, b_exact, dims) + _dot(a2, b_exact, dims)


def _dot_xl(a_exact, b, dims):
    b1, b2, b3 = _split3(b)
    return _dot(a_exact, b1, dims) + _dot(a_exact, b2, dims) + _dot(a_exact, b3, dims)


def _dot3(a, b, dims):
    a1 = a.astype(BF16)
    a2 = (a - a1.astype(F32)).astype(BF16)
    b1 = b.astype(BF16)
    b2 = (b - b1.astype(F32)).astype(BF16)
    return _dot(a1, b1, dims) + (_dot(a1, b2, dims) + _dot(a2, b1, dims))


def _sigmoid(x):
    return 1.0 / (1.0 + jnp.exp(-x))


def _log1pexp_neg_abs(x):
    return jnp.log(1.0 + jnp.exp(-jnp.abs(x)))


def _iota(shape, dim):
    return lax.broadcasted_iota(jnp.int32, shape, dim)


def _matmul(a, b, mode, out_dtype, name, add=None, comm=None):
    if mode == "nn":
        (m, k), (k2, n) = a.shape, b.shape
    elif mode == "nt":
        (m, k), (n, k2) = a.shape, b.shape
    else:
        (k, m), (k2, n) = a.shape, b.shape
    assert k == k2, (a.shape, b.shape, mode)
    tm, tn, tk = _tile(m, 1408), _tile(n, 1408), _tile(k, 1536)
    nk = k // tk
    dims = {"nn": NN, "nt": NT, "tn": TN}[mode]

    def body(*refs):
        if add is None:
            a_ref, b_ref, o_ref, acc_ref = refs
        else:
            a_ref, b_ref, add_ref, o_ref, acc_ref = refs
        kk = pl.program_id(2)

        @pl.when(kk == 0)
        def _():
            acc_ref[...] = jnp.zeros_like(acc_ref)

        acc_ref[...] += _dotb(a_ref[...], b_ref[...], dims)

        @pl.when(kk == nk - 1)
        def _():
            r = acc_ref[...]
            if add is not None:
                r = r + add_ref[...].astype(F32)
            o_ref[...] = r.astype(out_dtype)

    if mode == "nn":
        specs = [pl.BlockSpec((tm, tk), lambda i, j, l: (i, l)), pl.BlockSpec((tk, tn), lambda i, j, l: (l, j))]
    elif mode == "nt":
        specs = [pl.BlockSpec((tm, tk), lambda i, j, l: (i, l)), pl.BlockSpec((tn, tk), lambda i, j, l: (j, l))]
    else:
        specs = [pl.BlockSpec((tk, tm), lambda i, j, l: (l, i)), pl.BlockSpec((tk, tn), lambda i, j, l: (l, j))]
    args = [a, b]
    if add is not None:
        specs.append(pl.BlockSpec((tm, tn), lambda i, j, l: (i, j)))
        args.append(add)
    grid = (m // tm, n // tn, nk)

    def when():
        i, j, l = pl.program_id(0), pl.program_id(1), pl.program_id(2)
        first = jnp.logical_and(jnp.logical_and(i == 0, j == 0), l == 0)
        last = jnp.logical_and(jnp.logical_and(i == grid[0] - 1, j == grid[1] - 1), l == nk - 1)
        return first, last, last

    (out,), extra = _host_call(
        body, name, comm, when, [jax.ShapeDtypeStruct((m, n), out_dtype)], grid, specs,
        [pl.BlockSpec((tm, tn), lambda i, j, l: (i, j))], [pltpu.VMEM((tm, tn), F32)],
        ("parallel", "parallel", "arbitrary"), args)
    return out if comm is None else (out, extra)


def _rmsnorm_fwd(x, w, name, comm=None):
    t, d = x.shape
    tr = _tile(t, 512)
    steps = t // tr

    def body(x_ref, w_ref, o_ref):
        xv = x_ref[...]
        r = lax.rsqrt(jnp.mean(xv * xv, axis=1, keepdims=True) + EPS)
        o_ref[...] = (xv * r * w_ref[...]).astype(BF16)

    def when():
        i = pl.program_id(0)
        return i == 0, i == steps // 2, i == steps - 1

    (out,), extra = _host_call(
        body, name, comm, when, [jax.ShapeDtypeStruct((t, d), BF16)], (steps,),
        [pl.BlockSpec((tr, d), lambda i: (i, 0)), pl.BlockSpec((1, d), lambda i: (0, 0))],
        [pl.BlockSpec((tr, d), lambda i: (i, 0))], [], ("parallel",), (x, w))
    return out if comm is None else (out, extra)


def _rmsnorm_bwd(dn, x, w, dres, name):
    t, d = x.shape
    tr = _tile(t, 512)

    def body(dn_ref, x_ref, w_ref, dres_ref, dx_ref, dw_ref):
        i = pl.program_id(0)
        xv = x_ref[...]
        g = dn_ref[...].astype(F32)
        r = lax.rsqrt(jnp.mean(xv * xv, axis=1, keepdims=True) + EPS)
        xh = xv * r
        dxh = g * w_ref[...]
        dx = r * (dxh - xh * jnp.mean(dxh * xh, axis=1, keepdims=True))
        dx_ref[...] = dres_ref[...] + dx

        @pl.when(i == 0)
        def _():
            dw_ref[...] = jnp.zeros_like(dw_ref)

        dw_ref[...] += jnp.sum(g * xh, axis=0, keepdims=True)

    return pl.pallas_call(
        body, name=name,
        out_shape=(jax.ShapeDtypeStruct((t, d), F32), jax.ShapeDtypeStruct((1, d), F32)),
        grid=(t // tr,),
        in_specs=[pl.BlockSpec((tr, d), lambda i: (i, 0)), pl.BlockSpec((tr, d), lambda i: (i, 0)),
                  pl.BlockSpec((1, d), lambda i: (0, 0)), pl.BlockSpec((tr, d), lambda i: (i, 0))],
        out_specs=(pl.BlockSpec((tr, d), lambda i: (i, 0)), pl.BlockSpec((1, d), lambda i: (0, 0))),
        compiler_params=_params(("arbitrary",)),
    )(dn, x, w, dres)


def _final_loss(x2, target, w, name):
    t, d = x2.shape
    tr = _tile(t, 512)

    def body(x_ref, t_ref, w_ref, dx_ref, dw_ref, loss_ref):
        i = pl.program_id(0)
        xv = x_ref[...]
        r = lax.rsqrt(jnp.mean(xv * xv, axis=1, keepdims=True) + EPS)
        xh = xv * r
        err = xh * w_ref[...] - t_ref[...]
        dy = err * (1.0 / d)
        dxh = dy * w_ref[...]
        dx_ref[...] = r * (dxh - xh * jnp.mean(dxh * xh, axis=1, keepdims=True))

        @pl.when(i == 0)
        def _():
            dw_ref[...] = jnp.zeros_like(dw_ref)
            loss_ref[...] = jnp.zeros_like(loss_ref)

        dw_ref[...] += jnp.sum(dy * xh, axis=0, keepdims=True)
        row = jnp.sum(err * err, axis=1, keepdims=True) * (0.5 / d)
        loss_ref[...] += jnp.sum(row, axis=0, keepdims=True)

    return pl.pallas_call(
        body, name=name,
        out_shape=(jax.ShapeDtypeStruct((t, d), F32), jax.ShapeDtypeStruct((1, d), F32),
                   jax.ShapeDtypeStruct((1, 1), F32)),
        grid=(t // tr,),
        in_specs=[pl.BlockSpec((tr, d), lambda i: (i, 0)), pl.BlockSpec((tr, d), lambda i: (i, 0)),
                  pl.BlockSpec((1, d), lambda i: (0, 0))],
        out_specs=(pl.BlockSpec((tr, d), lambda i: (i, 0)), pl.BlockSpec((1, d), lambda i: (0, 0)),
                   pl.BlockSpec((1, 1), lambda i: (0, 0))),
        compiler_params=_params(("arbitrary",)),
    )(x2, target, w)


def _shift_down(cur, prev, k, row):
    r = pltpu.roll(cur, k, 0)
    top, row8 = r[0:8, :], row[0:8, :]
    for m in range(k):
        top = jnp.where(row8 == m, prev[HALO - k + m:HALO - k + m + 1, :], top)
    return jnp.concatenate([top, r[8:, :]], axis=0)


def _shift_up(cur, nxt, k, row, tr):
    r = pltpu.roll(cur, tr - k, 0)
    bottom, row8 = r[tr - 8:, :], row[0:8, :]
    for m in range(k):
        bottom = jnp.where(row8 == 8 - k + m, nxt[m:m + 1, :], bottom)
    return jnp.concatenate([r[:tr - 8, :], bottom], axis=0)


def _fold8(a):
    out = a[0:8, :]
    for r in range(8, a.shape[0], 8):
        out = out + a[r:r + 8, :]
    return out


def _conv_taps(cur, prev, w, ntaps, row):
    taps = [cur if i == ntaps - 1 else _shift_down(cur, prev, ntaps - 1 - i, row) for i in range(ntaps)]
    y = w[0:1, :] * taps[0]
    for i in range(1, ntaps):
        y = y + w[i:i + 1, :] * taps[i]
    return taps, y


def _conv_bwd_data(parts, w, ntaps, out_dtype, name):
    t, chp = parts[0].shape
    npart = len(parts)
    tr, tc = _tile(t, 512), _tile(chp, 1408)
    nc = chp // tc
    nhalo = t // HALO
    last = t // tr - 1

    def body(*refs):
        cur_refs, nxt_refs = refs[:npart], refs[npart:2 * npart]
        w_ref, o_ref = refs[2 * npart], refs[2 * npart + 1]
        i, j = pl.program_id(0), pl.program_id(1)
        row = _iota((tr, 128), 0)
        for c0 in range(0, tc, 128):
            sl = slice(c0, c0 + 128)
            cur, nxt = cur_refs[0][:, sl].astype(F32), nxt_refs[0][:, sl].astype(F32)
            for p in range(1, npart):
                cur = jnp.where(j >= p * nc, cur_refs[p][:, sl].astype(F32), cur)
                nxt = jnp.where(j >= p * nc, nxt_refs[p][:, sl].astype(F32), nxt)
            nxt = jnp.where(i == last, 0.0, nxt)
            wv = w_ref[:, sl]
            y = wv[ntaps - 1:ntaps, :] * cur
            for k in range(1, ntaps):
                y = y + wv[ntaps - 1 - k:ntaps - k, :] * _shift_up(cur, nxt, k, row, tr)
            o_ref[:, sl] = y.astype(out_dtype)

    col = lambda p: (lambda j: jnp.clip(j - p * nc, 0, nc - 1))
    cur_specs = [pl.BlockSpec((tr, tc), lambda i, j, c=col(p): (i, c(j))) for p in range(npart)]
    nxt_specs = [pl.BlockSpec((HALO, tc),
                              lambda i, j, c=col(p): (jnp.minimum((i + 1) * (tr // HALO), nhalo - 1), c(j)))
                 for p in range(npart)]
    return pl.pallas_call(
        body, name=name,
        out_shape=jax.ShapeDtypeStruct((t, npart * chp), out_dtype),
        grid=(t // tr, npart * nc),
        in_specs=cur_specs + nxt_specs + [pl.BlockSpec((ntaps, tc), lambda i, j: (0, j))],
        out_specs=pl.BlockSpec((tr, tc), lambda i, j: (i, j)),
        compiler_params=_params(("parallel", "parallel")),
    )(*parts, *parts, w)


def _ffn_act_fwd(upre, cw, name):
    t = upre.shape[0]
    tr, tc = _tile(t, 512), _tile(D_FF, 1408)
    nj = D_FF // tc

    def body(g_ref, gp_ref, u_ref, up_ref, wg_ref, wu_ref, o_ref):
        i = pl.program_id(0)
        row = _iota((tr, 128), 0)
        for c0 in range(0, tc, 128):
            sl = slice(c0, c0 + 128)
            gp = jnp.where(i == 0, 0.0, gp_ref[:, sl].astype(F32))
            up = jnp.where(i == 0, 0.0, up_ref[:, sl].astype(F32))
            _, gc = _conv_taps(g_ref[:, sl].astype(F32), gp, wg_ref[:, sl], FFN_CONV, row)
            _, uc = _conv_taps(u_ref[:, sl].astype(F32), up, wu_ref[:, sl], FFN_CONV, row)
            o_ref[:, sl] = (gc * _sigmoid(gc) * uc).astype(BF16)

    prev = lambda off: (lambda i, j: (jnp.maximum(i * (tr // HALO) - 1, 0), j + off))
    return pl.pallas_call(
        body, name=name,
        out_shape=jax.ShapeDtypeStruct((t, D_FF), BF16),
        grid=(t // tr, nj),
        in_specs=[pl.BlockSpec((tr, tc), lambda i, j: (i, j)), pl.BlockSpec((HALO, tc), prev(0)),
                  pl.BlockSpec((tr, tc), lambda i, j: (i, j + nj)), pl.BlockSpec((HALO, tc), prev(nj)),
                  pl.BlockSpec((FFN_CONV, tc), lambda i, j: (0, j)),
                  pl.BlockSpec((FFN_CONV, tc), lambda i, j: (0, j + nj))],
        out_specs=pl.BlockSpec((tr, tc), lambda i, j: (i, j)),
        compiler_params=_params(("parallel", "parallel")),
    )(upre, upre, upre, upre, cw, cw)


def _ffn_act_bwd(dact, upre, cw, name):
    t = upre.shape[0]
    tr, tc = _tile(t, 512), _tile(D_FF, 1408)
    nj = D_FF // tc

    def body(da_ref, g_ref, gp_ref, u_ref, up_ref, wg_ref, wu_ref, dg_ref, du_ref, dwg_ref, dwu_ref):
        i = pl.program_id(1)
        row = _iota((CHUNK_ROWS, 128), 0)

        @pl.when(i == 0)
        def _():
            dwg_ref[...] = jnp.zeros_like(dwg_ref)
            dwu_ref[...] = jnp.zeros_like(dwu_ref)

        for c0 in range(0, tc, 128):
            sl = slice(c0, c0 + 128)
            wg, wu = wg_ref[:, sl], wu_ref[:, sl]
            dwg = [jnp.zeros((8, 128), F32)] * FFN_CONV
            dwu = [jnp.zeros((8, 128), F32)] * FFN_CONV
            for r0 in range(0, tr, CHUNK_ROWS):
                rows = slice(r0, r0 + CHUNK_ROWS)
                if r0 == 0:
                    gp = jnp.where(i == 0, 0.0, gp_ref[:, sl].astype(F32))
                    up = jnp.where(i == 0, 0.0, up_ref[:, sl].astype(F32))
                else:
                    gp = g_ref[r0 - HALO:r0, sl].astype(F32)
                    up = u_ref[r0 - HALO:r0, sl].astype(F32)
                gt, gc = _conv_taps(g_ref[rows, sl].astype(F32), gp, wg, FFN_CONV, row)
                ut, uc = _conv_taps(u_ref[rows, sl].astype(F32), up, wu, FFN_CONV, row)
                da = da_ref[rows, sl].astype(F32)
                sg = _sigmoid(gc)
                dgc = da * uc * (sg * (1.0 + gc * (1.0 - sg)))
                duc = da * (gc * sg)
                dg_ref[rows, sl] = dgc.astype(BF16)
                du_ref[rows, sl] = duc.astype(BF16)
                dwg = [dwg[k] + _fold8(dgc * gt[k]) for k in range(FFN_CONV)]
                dwu = [dwu[k] + _fold8(duc * ut[k]) for k in range(FFN_CONV)]
            for k in range(FFN_CONV):
                dwg_ref[k:k + 1, sl] += jnp.sum(dwg[k], axis=0, keepdims=True)
                dwu_ref[k:k + 1, sl] += jnp.sum(dwu[k], axis=0, keepdims=True)

    prev = lambda off: (lambda j, i: (jnp.maximum(i * (tr // HALO) - 1, 0), j + off))
    blk = lambda off: pl.BlockSpec((tr, tc), lambda j, i: (i, j + off))
    wblk = lambda off: pl.BlockSpec((FFN_CONV, tc), lambda j, i: (0, j + off))
    dgc, duc, dwg, dwu = pl.pallas_call(
        body, name=name,
        out_shape=(jax.ShapeDtypeStruct((t, D_FF), BF16), jax.ShapeDtypeStruct((t, D_FF), BF16),
                   jax.ShapeDtypeStruct((FFN_CONV, D_FF), F32), jax.ShapeDtypeStruct((FFN_CONV, D_FF), F32)),
        grid=(nj, t // tr),
        in_specs=[blk(0), blk(0), pl.BlockSpec((HALO, tc), prev(0)), blk(nj), pl.BlockSpec((HALO, tc), prev(nj)),
                  wblk(0), wblk(nj)],
        out_specs=(blk(0), blk(0), wblk(0), wblk(0)),
        compiler_params=_params(("parallel", "arbitrary")),
    )(dact, upre, upre, upre, upre, cw, cw)
    return dgc, duc, dwg, dwu


def _dn_pre_fwd(qkv_pre, cw, name):
    t = qkv_pre.shape[0]
    tr = _tile(t, 512)
    scale = HEAD_DIM ** -0.5

    def body(x_ref, p_ref, w_ref, o_ref):
        i, j = pl.program_id(0), pl.program_id(1)
        row = _iota((tr, HEAD_DIM), 0)
        for h in range(HEADS):
            sl = slice(h * HEAD_DIM, (h + 1) * HEAD_DIM)
            prev = jnp.where(i == 0, 0.0, p_ref[:, sl].astype(F32))
            _, c = _conv_taps(x_ref[:, sl].astype(F32), prev, w_ref[:, sl], DN_CONV, row)
            s = c * _sigmoid(c)
            r = lax.rsqrt(jnp.sum(s * s, axis=1, keepdims=True) + EPS)
            o_ref[:, sl] = s * jnp.where(j == 0, r * scale, jnp.where(j == 1, r, 1.0))

    return pl.pallas_call(
        body, name=name,
        out_shape=jax.ShapeDtypeStruct((t, 3 * WIDTH), F32),
        grid=(t // tr, 3),
        in_specs=[pl.BlockSpec((tr, WIDTH), lambda i, j: (i, j)),
                  pl.BlockSpec((HALO, WIDTH), lambda i, j: (jnp.maximum(i * (tr // HALO) - 1, 0), j)),
                  pl.BlockSpec((DN_CONV, WIDTH), lambda i, j: (0, j))],
        out_specs=pl.BlockSpec((tr, WIDTH), lambda i, j: (i, j)),
        compiler_params=_params(("parallel", "parallel")),
    )(qkv_pre, qkv_pre, cw)


def _dn_pre_bwd(dq, dk, dv, qkv_pre, cw, name):
    t = qkv_pre.shape[0]
    tr = _tile(t, 512)
    scale = HEAD_DIM ** -0.5

    def body(dq_ref, dk_ref, dv_ref, x_ref, p_ref, w_ref, dc_ref, dw_ref):
        j, i = pl.program_id(0), pl.program_id(1)
        row = _iota((CHUNK_ROWS, HEAD_DIM), 0)

        @pl.when(i == 0)
        def _():
            dw_ref[...] = jnp.zeros_like(dw_ref)

        for h in range(HEADS):
            sl = slice(h * HEAD_DIM, (h + 1) * HEAD_DIM)
            wv = w_ref[:, sl]
            dw = [jnp.zeros((8, HEAD_DIM), F32)] * DN_CONV
            for r0 in range(0, tr, CHUNK_ROWS):
                rows = slice(r0, r0 + CHUNK_ROWS)
                if r0 == 0:
                    prev = jnp.where(i == 0, 0.0, p_ref[:, sl].astype(F32))
                else:
                    prev = x_ref[r0 - HALO:r0, sl].astype(F32)
                taps, c = _conv_taps(x_ref[rows, sl].astype(F32), prev, wv, DN_CONV, row)
                d = jnp.where(j == 0, dq_ref[rows, sl] * scale, jnp.where(j == 1, dk_ref[rows, sl], dv_ref[rows, sl]))
                sg = _sigmoid(c)
                s = c * sg
                r = lax.rsqrt(jnp.sum(s * s, axis=1, keepdims=True) + EPS)
                nh = s * r
                ds_norm = r * (d - nh * jnp.sum(nh * d, axis=1, keepdims=True))
                dc = jnp.where(j < 2, ds_norm, d) * (sg * (1.0 + c * (1.0 - sg)))
                dc_ref[rows, sl] = dc.astype(BF16)
                dw = [dw[k] + _fold8(dc * taps[k]) for k in range(DN_CONV)]
            for k in range(DN_CONV):
                dw_ref[k:k + 1, sl] += jnp.sum(dw[k], axis=0, keepdims=True)

    dspec = lambda p: pl.BlockSpec((tr, WIDTH), lambda j, i: (jnp.where(j == p, i, 0), 0))
    return pl.pallas_call(
        body, name=name,
        out_shape=(jax.ShapeDtypeStruct((t, 3 * WIDTH), BF16), jax.ShapeDtypeStruct((DN_CONV, 3 * WIDTH), F32)),
        grid=(3, t // tr),
        in_specs=[dspec(0), dspec(1), dspec(2),
                  pl.BlockSpec((tr, WIDTH), lambda j, i: (i, j)),
                  pl.BlockSpec((HALO, WIDTH), lambda j, i: (jnp.maximum(i * (tr // HALO) - 1, 0), j)),
                  pl.BlockSpec((DN_CONV, WIDTH), lambda j, i: (0, j))],
        out_specs=(pl.BlockSpec((tr, WIDTH), lambda j, i: (i, j)),
                   pl.BlockSpec((DN_CONV, WIDTH), lambda j, i: (0, j))),
        compiler_params=_params(("parallel", "arbitrary")),
    )(dq, dk, dv, qkv_pre, qkv_pre, cw)


def _tri(n, kind):
    r, c = _iota((n, n), 0), _iota((n, n), 1)
    m = {"lower": r >= c, "strict": r > c, "upper": r <= c}[kind]
    return m


GATE_ROWS = 4 * DN_CHUNK


def _chunk_tri(kind):
    r, c = _iota((GATE_ROWS, GATE_ROWS), 0), _iota((GATE_ROWS, GATE_ROWS), 1)
    same = (r // DN_CHUNK) == (c // DN_CHUNK)
    return jnp.where(jnp.logical_and(same, _tri(GATE_ROWS, kind)), 1.0, 0.0).astype(BF16)


def _dn_gates_fwd(hab, alog, dtb, name):
    t = hab.shape[0]
    cc = GATE_ROWS

    def body(h_ref, al_ref, dt_ref, o_ref):
        hv = h_ref[...]
        lane = _iota(hv.shape, 1)
        xa = hv + dt_ref[...]
        sp = jnp.maximum(xa, 0.0) + _log1pexp_neg_abs(xa)
        g = jnp.where(lane < HEADS, -jnp.exp(al_ref[...]) * sp, 0.0)
        gc = _dot_xl(_chunk_tri("lower"), g, NN)
        o_ref[...] = jnp.where(lane < HEADS, gc, jnp.where(lane < 2 * HEADS, _sigmoid(hv), 0.0))

    return pl.pallas_call(
        body, name=name,
        out_shape=jax.ShapeDtypeStruct((t, 128), F32),
        grid=(t // cc,),
        in_specs=[pl.BlockSpec((cc, 128), lambda i: (i, 0)), pl.BlockSpec((1, 128), lambda i: (0, 0)),
                  pl.BlockSpec((1, 128), lambda i: (0, 0))],
        out_specs=pl.BlockSpec((cc, 128), lambda i: (i, 0)),
        compiler_params=_params(("parallel",)),
    )(hab, alog, dtb)


def _dn_gates_bwd(dgates, hab, alog, dtb, name):
    t = hab.shape[0]
    cc = GATE_ROWS

    def body(d_ref, h_ref, al_ref, dt_ref, o_ref, dal_ref, ddt_ref):
        i = pl.program_id(0)
        hv = h_ref[...]
        dv = d_ref[...]
        lane = _iota(hv.shape, 1)
        dg = _dot_xl(_chunk_tri("upper"), jnp.where(lane < HEADS, dv, 0.0), NN)
        xa = hv + dt_ref[...]
        sp = jnp.maximum(xa, 0.0) + _log1pexp_neg_abs(xa)
        ea = jnp.exp(al_ref[...])
        da = jnp.where(lane < HEADS, dg * (-ea) * _sigmoid(xa), 0.0)
        be = _sigmoid(hv)
        db = dv * be * (1.0 - be)
        o_ref[...] = jnp.where(lane < HEADS, da, jnp.where(lane < 2 * HEADS, db, 0.0))

        @pl.when(i == 0)
        def _():
            dal_ref[...] = jnp.zeros_like(dal_ref)
            ddt_ref[...] = jnp.zeros_like(ddt_ref)

        dal_ref[...] += jnp.sum(jnp.where(lane < HEADS, dg * (-ea) * sp, 0.0), axis=0, keepdims=True)
        ddt_ref[...] += jnp.sum(da, axis=0, keepdims=True)

    return pl.pallas_call(
        body, name=name,
        out_shape=(jax.ShapeDtypeStruct((t, 128), F32), jax.ShapeDtypeStruct((1, 128), F32),
                   jax.ShapeDtypeStruct((1, 128), F32)),
        grid=(t // cc,),
        in_specs=[pl.BlockSpec((cc, 128), lambda i: (i, 0)), pl.BlockSpec((cc, 128), lambda i: (i, 0)),
                  pl.BlockSpec((1, 128), lambda i: (0, 0)), pl.BlockSpec((1, 128), lambda i: (0, 0))],
        out_specs=(pl.BlockSpec((cc, 128), lambda i: (i, 0)), pl.BlockSpec((1, 128), lambda i: (0, 0)),
                   pl.BlockSpec((1, 128), lambda i: (0, 0))),
        compiler_params=_params(("arbitrary",)),
    )(dgates, hab, alog, dtb)


def _dn_chunk_common(gates, h):
    cc = DN_CHUNK
    lane = _iota(gates.shape, 1)
    gh = jnp.where(lane == h, gates, 0.0)
    gc_col = jnp.sum(gh, axis=1, keepdims=True)
    gc_row = _dot_xl(jnp.ones((cc, 128), BF16), gh, NT)
    beta = jnp.sum(jnp.where(lane == h + HEADS, gates, 0.0), axis=1, keepdims=True)
    lower = _tri(cc, "lower")
    decay = jnp.where(lower, jnp.exp(jnp.where(lower, gc_col - gc_row, 0.0)), 0.0)
    gc_last = gc_col[cc - 1:cc, :]
    return gc_col, gc_last, beta, decay


def _dn_local_fwd(act, gates, name):
    t = act.shape[0]
    cc = DN_CHUNK
    nc = t // cc

    def body(q_ref, k_ref, v_ref, g_ref, u_ref, w_ref, kd_ref, qg_ref, ti_ref, p_ref):
        gates = g_ref[...]
        eye = jnp.where(_iota((cc, cc), 0) == _iota((cc, cc), 1), 1.0, 0.0)
        hs = range(HEADS)
        sl = [slice(h * HEAD_DIM, (h + 1) * HEAD_DIM) for h in hs]
        q, k, v = ([r[:, s] for s in sl] for r in (q_ref, k_ref, v_ref))
        gc_col, gc_last, beta, decay = zip(*[_dn_chunk_common(gates, h) for h in hs])
        gam = [jnp.exp(g) for g in gc_col]
        kb = [k[h] * beta[h] for h in hs]
        npow = [-jnp.where(_tri(cc, "strict"), _dotb(kb[h], k[h], NT) * decay[h], 0.0) for h in hs]
        tinv = [eye + n for n in npow]
        for _ in range(5):
            npow = [_dot3(n, n, NN) for n in npow]
            tinv = [t + _dot3(t, n, NN) for t, n in zip(tinv, npow)]
        uu = [_dot3(tinv[h], v[h] * beta[h], NN) for h in hs]
        ww = [_dot3(tinv[h], kb[h] * gam[h], NN) for h in hs]
        pp = [jnp.where(_tri(cc, "lower"), _dotb(q[h], k[h], NT) * decay[h], 0.0) for h in hs]
        for h in hs:
            u_ref[:, sl[h]] = uu[h]
            w_ref[:, sl[h]] = ww[h].astype(BF16)
            kd_ref[:, sl[h]] = (k[h] * jnp.exp(gc_last[h] - gc_col[h])).astype(BF16)
            qg_ref[:, sl[h]] = (q[h] * gam[h]).astype(BF16)
            ti_ref[h] = tinv[h]
            p_ref[h] = pp[h].astype(BF16)

    row = lambda off: pl.BlockSpec((cc, WIDTH), lambda n: (n, off))
    mat = pl.BlockSpec((HEADS, cc, cc), lambda n: (0, n, 0))
    tw, tb = jax.ShapeDtypeStruct((t, WIDTH), F32), jax.ShapeDtypeStruct((t, WIDTH), BF16)
    hm, hb = jax.ShapeDtypeStruct((HEADS, t, cc), F32), jax.ShapeDtypeStruct((HEADS, t, cc), BF16)
    return pl.pallas_call(
        body, name=name,
        out_shape=(tw, tb, tb, tb, hm, hb),
        grid=(nc,),
        in_specs=[row(0), row(1), row(2), pl.BlockSpec((cc, 128), lambda n: (n, 0))],
        out_specs=(row(0), row(0), row(0), row(0), mat, mat),
        compiler_params=_params(("parallel",)),
    )(act, act, act, gates)


def _dn_scan_fwd(u, w, kd, qg, p, gates, name):
    t = u.shape[0]
    cc = DN_CHUNK
    nc = t // cc
    per = SCAN_CHUNKS

    def body(u_ref, w_ref, kd_ref, qg_ref, p_ref, g_ref, o_ref, sh_ref, s_ref):
        n = pl.program_id(0)

        @pl.when(n == 0)
        def _():
            s_ref[...] = jnp.zeros_like(s_ref)

        hs = range(HEADS)
        sl = [slice(h * HEAD_DIM, (h + 1) * HEAD_DIM) for h in hs]
        s = [s_ref[h] for h in hs]
        for c in range(per):
            r = slice(c * cc, (c + 1) * cc)
            glast = jnp.exp(g_ref[(c + 1) * cc - 1:(c + 1) * cc, :])
            sb = [a.astype(BF16) for a in s]
            vn = [u_ref[r, sl[h]] - _dot(w_ref[r, sl[h]].astype(BF16), sb[h], NN) for h in hs]
            vnb = [a.astype(BF16) for a in vn]
            o_state = [_dot(qg_ref[r, sl[h]].astype(BF16), sb[h], NN) for h in hs]
            o_local = [_dot(p_ref[h, r, :].astype(BF16), vnb[h], NN) for h in hs]
            s_add = [_dot(kd_ref[r, sl[h]].astype(BF16), vnb[h], TN) for h in hs]
            for h in hs:
                o_ref[r, sl[h]] = o_state[h] + o_local[h]
                sh_ref[c, h] = sb[h]
            s = [glast[:, h:h + 1] * s[h] + s_add[h] for h in hs]
        for h in hs:
            s_ref[h] = s[h]

    row = pl.BlockSpec((per * cc, WIDTH), lambda n: (n, 0))
    return pl.pallas_call(
        body, name=name,
        out_shape=(jax.ShapeDtypeStruct((t, WIDTH), F32),
                   jax.ShapeDtypeStruct((nc, HEADS, HEAD_DIM, HEAD_DIM), BF16)),
        grid=(nc // per,),
        in_specs=[row, row, row, row, pl.BlockSpec((HEADS, per * cc, cc), lambda n: (0, n, 0)),
                  pl.BlockSpec((per * cc, 128), lambda n: (n, 0))],
        out_specs=(row, pl.BlockSpec((per, HEADS, HEAD_DIM, HEAD_DIM), lambda n: (n, 0, 0, 0))),
        scratch_shapes=[pltpu.VMEM((HEADS, HEAD_DIM, HEAD_DIM), F32)],
        compiler_params=_params(("arbitrary",)),
    )(u, w, kd, qg, p, gates)


def _dn_scan_bwd(do, w, kd, qg, p, gates, name):
    t = do.shape[0]
    cc = DN_CHUNK
    nc = t // cc
    per = SCAN_CHUNKS
    nb = nc // per

    def body(do_ref, w_ref, kd_ref, qg_ref, p_ref, g_ref, dvn_ref, dsh_ref, ds_ref):
        n = pl.program_id(0)

        @pl.when(n == 0)
        def _():
            ds_ref[...] = jnp.zeros_like(ds_ref)

        hs = range(HEADS)
        sl = [slice(h * HEAD_DIM, (h + 1) * HEAD_DIM) for h in hs]
        ds = [ds_ref[h] for h in hs]
        for c in reversed(range(per)):
            r = slice(c * cc, (c + 1) * cc)
            glast = jnp.exp(g_ref[(c + 1) * cc - 1:(c + 1) * cc, :])
            dob = [do_ref[r, sl[h]].astype(BF16) for h in hs]
            dvn = [_dot(p_ref[h, r, :].astype(BF16), dob[h], TN)
                   + _dot(kd_ref[r, sl[h]].astype(BF16), ds[h].astype(BF16), NN) for h in hs]
            ds_q = [_dot(qg_ref[r, sl[h]].astype(BF16), dob[h], TN) for h in hs]
            ds_w = [_dot(w_ref[r, sl[h]].astype(BF16), dvn[h].astype(BF16), TN) for h in hs]
            for h in hs:
                dvn_ref[r, sl[h]] = dvn[h]
                dsh_ref[c, h] = ds[h].astype(BF16)
            ds = [ds_q[h] + glast[:, h:h + 1] * ds[h] - ds_w[h] for h in hs]
        for h in hs:
            ds_ref[h] = ds[h]

    row = pl.BlockSpec((per * cc, WIDTH), lambda n: (nb - 1 - n, 0))
    return pl.pallas_call(
        body, name=name,
        out_shape=(jax.ShapeDtypeStruct((t, WIDTH), F32),
                   jax.ShapeDtypeStruct((nc, HEADS, HEAD_DIM, HEAD_DIM), BF16)),
        grid=(nb,),
        in_specs=[row, row, row, row, pl.BlockSpec((HEADS, per * cc, cc), lambda n: (0, nb - 1 - n, 0)),
                  pl.BlockSpec((per * cc, 128), lambda n: (nb - 1 - n, 0))],
        out_specs=(row, pl.BlockSpec((per, HEADS, HEAD_DIM, HEAD_DIM), lambda n: (nb - 1 - n, 0, 0, 0))),
        scratch_shapes=[pltpu.VMEM((HEADS, HEAD_DIM, HEAD_DIM), F32)],
        compiler_params=_params(("arbitrary",)),
    )(do, w, kd, qg, p, gates)


def _dn_local_bwd(act, gates, u, w, kd, qg, tinv, p, sh, dsh, dvn, do, name):
    t = act.shape[0]
    cc = DN_CHUNK
    nc = t // cc

    def body(q_ref, k_ref, v_ref, g_ref, u_ref, w_ref, kd_ref, qg_ref, ti_ref, p_ref, s_ref, ds_ref,
             dvn_ref, do_ref, dq_ref, dk_ref, dv_ref, dg_ref):
        gates_v = g_ref[...]
        lower, strict = _tri(cc, "lower"), _tri(cc, "strict")
        ones = jnp.ones((cc, 128), BF16)
        rowc = _iota((cc, 1), 0)
        lane = _iota((cc, 128), 1)
        hs = range(HEADS)
        sl = [slice(h * HEAD_DIM, (h + 1) * HEAD_DIM) for h in hs]
        q, k, v, uu, ww, kd, qg, dvn, do = ([r[:, s] for s in sl] for r in (
            q_ref, k_ref, v_ref, u_ref, w_ref, kd_ref, qg_ref, dvn_ref, do_ref))
        gc_col, gc_last, beta, decay = zip(*[_dn_chunk_common(gates_v, h) for h in hs])
        gam = [jnp.exp(g) for g in gc_col]
        kb = [k[h] * beta[h] for h in hs]
        s_in = [s_ref[0, h] for h in hs]
        ds_out = [ds_ref[0, h] for h in hs]
        tinv = [ti_ref[h] for h in hs]

        a = [jnp.where(strict, _dotb(kb[h], k[h], NT) * decay[h], 0.0) for h in hs]
        vn = [uu[h] - _dotb(ww[h], s_in[h], NN) for h in hs]
        dqg = [_dotb(do[h], s_in[h], NT) for h in hs]
        dw = [-_dotb(dvn[h], s_in[h], NT) for h in hs]
        dp = [jnp.where(lower, _dotb(do[h], vn[h], NT), 0.0) for h in hs]
        dkd = [_dotb(vn[h], ds_out[h], NT) for h in hs]
        dru = [_dot3(tinv[h], dvn[h], TN) for h in hs]
        drw = [_dot3(tinv[h], dw[h], TN) for h in hs]
        da = [-jnp.where(strict, _dotb(dru[h], uu[h], NT) + _dotb(drw[h], ww[h], NT), 0.0) for h in hs]
        dad = [da[h] * decay[h] for h in hs]
        dpd = [dp[h] * decay[h] for h in hs]
        dkb = [_dotb(dad[h], k[h], NN) + gam[h] * drw[h] for h in hs]
        dk = [_dotb(dad[h], kb[h], TN) + _dotb(dpd[h], q[h], TN) + beta[h] * dkb[h]
              + jnp.exp(gc_last[h] - gc_col[h]) * dkd[h] for h in hs]
        dq = [gam[h] * dqg[h] + _dotb(dpd[h], k[h], NN) for h in hs]
        gm = [da[h] * a[h] + dp[h] * p_ref[h] for h in hs]
        colsum = [_dot_xr(gm[h], ones, TN)[:, 0:1] for h in hs]

        dgates = jnp.zeros((cc, 128), F32)
        for h in hs:
            dk_ref[:, sl[h]] = dk[h]
            dq_ref[:, sl[h]] = dq[h]
            dv_ref[:, sl[h]] = beta[h] * dru[h]
            dbeta = (jnp.sum(dkb[h] * k[h], axis=1, keepdims=True)
                     + jnp.sum(dru[h] * v[h], axis=1, keepdims=True))
            rkd = jnp.sum(dkd[h] * kd[h], axis=1, keepdims=True)
            dgc = (jnp.sum(gm[h], axis=1, keepdims=True) - colsum[h]
                   + jnp.sum(dqg[h] * qg[h], axis=1, keepdims=True)
                   + jnp.sum(drw[h] * kb[h], axis=1, keepdims=True) * gam[h] - rkd)
            tail = jnp.sum(rkd, axis=0, keepdims=True) + jnp.exp(gc_last[h]) * jnp.sum(
                jnp.sum(s_in[h].astype(F32) * ds_out[h].astype(F32), axis=1, keepdims=True), axis=0, keepdims=True)
            dgc = dgc + jnp.where(rowc == cc - 1, tail, 0.0)
            dgates = dgates + jnp.where(lane == h, dgc, 0.0) + jnp.where(lane == h + HEADS, dbeta, 0.0)
        dg_ref[...] = dgates

    row = lambda off: pl.BlockSpec((cc, WIDTH), lambda n: (n, off))
    mat = pl.BlockSpec((HEADS, cc, cc), lambda n: (0, n, 0))
    st = pl.BlockSpec((1, HEADS, HEAD_DIM, HEAD_DIM), lambda n: (n, 0, 0, 0))
    gl = pl.BlockSpec((cc, 128), lambda n: (n, 0))
    tw = jax.ShapeDtypeStruct((t, WIDTH), F32)
    return pl.pallas_call(
        body, name=name,
        out_shape=(tw, tw, tw, jax.ShapeDtypeStruct((t, 128), F32)),
        grid=(nc,),
        in_specs=[row(0), row(1), row(2), gl, row(0), row(0), row(0), row(0), mat, mat, st, st, row(0), row(0)],
        out_specs=(row(0), row(0), row(0), gl),
        compiler_params=_params(("parallel",)),
    )(act, act, act, gates, u, w, kd, qg, tinv, p, sh, dsh, dvn, do)


def _dn_post_fwd(o, gate, w, name):
    t = o.shape[0]
    tr = _tile(t, 512)

    def body(o_ref, g_ref, w_ref, y_ref):
        for h in range(HEADS):
            sl = slice(h * HEAD_DIM, (h + 1) * HEAD_DIM)
            ov, gv = o_ref[:, sl], g_ref[:, sl].astype(F32)
            r = lax.rsqrt(jnp.mean(ov * ov, axis=1, keepdims=True) + EPS)
            y_ref[:, sl] = (ov * r * w_ref[...] * (gv * _sigmoid(gv))).astype(BF16)

    blk = pl.BlockSpec((tr, WIDTH), lambda i: (i, 0))
    return pl.pallas_call(
        body, name=name,
        out_shape=jax.ShapeDtypeStruct((t, WIDTH), BF16),
        grid=(t // tr,),
        in_specs=[blk, blk, pl.BlockSpec((1, HEAD_DIM), lambda i: (0, 0))],
        out_specs=blk,
        compiler_params=_params(("parallel",)),
    )(o, gate, w)


def _dn_post_bwd(dy, o, gate, w, name):
    t = o.shape[0]
    tr = _tile(t, 512)

    def body(dy_ref, o_ref, g_ref, w_ref, do_ref, dg_ref, dw_ref):
        i = pl.program_id(0)

        @pl.when(i == 0)
        def _():
            dw_ref[...] = jnp.zeros_like(dw_ref)

        dw = jnp.zeros((1, HEAD_DIM), F32)
        for h in range(HEADS):
            sl = slice(h * HEAD_DIM, (h + 1) * HEAD_DIM)
            ov, gv, dyv = o_ref[:, sl], g_ref[:, sl].astype(F32), dy_ref[:, sl].astype(F32)
            r = lax.rsqrt(jnp.mean(ov * ov, axis=1, keepdims=True) + EPS)
            oh = ov * r
            sg = _sigmoid(gv)
            dg_ref[:, sl] = (dyv * oh * w_ref[...] * (sg * (1.0 + gv * (1.0 - sg)))).astype(BF16)
            dn = dyv * (gv * sg)
            doh = dn * w_ref[...]
            do_ref[:, sl] = r * (doh - oh * jnp.mean(doh * oh, axis=1, keepdims=True))
            dw = dw + jnp.sum(dn * oh, axis=0, keepdims=True)
        dw_ref[...] += dw

    blk = pl.BlockSpec((tr, WIDTH), lambda i: (i, 0))
    return pl.pallas_call(
        body, name=name,
        out_shape=(jax.ShapeDtypeStruct((t, WIDTH), F32), jax.ShapeDtypeStruct((t, WIDTH), BF16),
                   jax.ShapeDtypeStruct((1, HEAD_DIM), F32)),
        grid=(t // tr,),
        in_specs=[blk, blk, blk, pl.BlockSpec((1, HEAD_DIM), lambda i: (0, 0))],
        out_specs=(blk, blk, pl.BlockSpec((1, HEAD_DIM), lambda i: (0, 0))),
        compiler_params=_params(("arbitrary",)),
    )(dy, o, gate, w)


def _sb_scores(qs, k_ref, qi, it, carries, uincl):
    bk = ATT_BLOCK
    scale = HEAD_DIM ** -0.5
    heads, groups = range(len(qs)), range(SB_GROUP)
    lane = [slice(e * HEAD_DIM, (e + 1) * HEAD_DIM) for e in heads]
    js = [qi - SB_GROUP * it - g for g in groups]
    rows = [pl.ds(pl.multiple_of(jnp.maximum(j, 0) * bk, bk), bk) for j in js]
    qpos = qi * bk + _iota((bk, bk), 0)
    col = _iota((bk, bk), 1)
    mask1 = [jnp.logical_and(j * bk + col < qpos, j >= 0) for j in js]
    ks = [[k_ref[r, lane[e]] for r in rows] for e in heads]
    z = [[_dot(qs[e], k, NT) * scale for k in ks[e]] for e in heads]
    soft = [[_log1pexp_neg_abs(a) for a in ze] for ze in z]
    lk_full = [[-(jnp.maximum(a, 0.0) + s) for a, s in zip(z[e], soft[e])] for e in heads]
    lk = [[jnp.where(m, a, 0.0) for m, a in zip(mask1, lk_full[e])] for e in heads]
    ls = [[jnp.minimum(a, 0.0) - s for a, s in zip(z[e], soft[e])] for e in heads]
    incl = [[_dot_xr2(a, uincl, NN) for a in lk[e]] for e in heads]
    weights, out_carries = [], []
    for e in heads:
        cb, we = carries[e], []
        for g in groups:
            we.append(jnp.where(mask1[g], jnp.exp(ls[e][g] + (cb + incl[e][g] - lk[e][g])), 0.0))
            cb = cb + incl[e][g][:, 0:1]
        weights.append(we)
        out_carries.append(cb)
    return rows, ks, weights, mask1, lk_full, ls, out_carries


def _sb_more(qi, carry):
    it, cbs = carry[0], carry[1]
    live = jnp.max(cbs[0])
    for cb in cbs[1:]:
        live = jnp.maximum(live, jnp.max(cb))
    return jnp.logical_and(SB_GROUP * it <= qi, live > SB_LOG_ZERO)


def _sb_steps(groups, nq):
    def when():
        h, i = pl.program_id(0), pl.program_id(1)
        return (jnp.logical_and(h == 0, i == 0), jnp.logical_and(h == groups // 2, i == 0),
                jnp.logical_and(h == groups - 1, i == nq - 1))
    return when


def _sb_fwd(qkv, name, comm=None):
    t = qkv.shape[0]
    bk = ATT_BLOCK
    hp, wide = SB_HEADS_FWD, SB_HEADS_FWD * HEAD_DIM
    lane = [slice(e * HEAD_DIM, (e + 1) * HEAD_DIM) for e in range(hp)]

    def body(q_ref, k_ref, v_ref, o_ref):
        qi = pl.program_id(1)
        qs = [q_ref[:, s] for s in lane]
        uincl = jnp.where(_tri(bk, "lower"), 1.0, 0.0).astype(BF16)

        def step(carry):
            it, cbs, accs = carry
            rows, _, weights, _, _, _, cbs = _sb_scores(qs, k_ref, qi, it, cbs, uincl)
            accs = list(accs)
            for e in range(hp):
                for r, a in zip(rows, weights[e]):
                    accs[e] = accs[e] + _dot(a.astype(BF16), v_ref[r, lane[e]], NN)
            return it + 1, tuple(cbs), tuple(accs)

        init = (jnp.int32(0), (jnp.zeros((bk, 1), F32),) * hp, (jnp.zeros((bk, HEAD_DIM), F32),) * hp)
        _, _, accs = lax.while_loop(functools.partial(_sb_more, qi), step, init)
        for e in range(hp):
            o_ref[:, lane[e]] = accs[e]

    groups = HEADS // hp
    (o,), extra = _host_call(
        body, name, comm, _sb_steps(groups, t // bk), [jax.ShapeDtypeStruct((t, WIDTH), F32)], (groups, t // bk),
        [pl.BlockSpec((bk, wide), lambda h, i: (i, h)),
         pl.BlockSpec((t, wide), lambda h, i: (0, groups + h)),
         pl.BlockSpec((t, wide), lambda h, i: (0, 2 * groups + h))],
        [pl.BlockSpec((bk, wide), lambda h, i: (i, h))], [], ("parallel", "arbitrary"), (qkv, qkv, qkv))
    return o, extra


def _sb_bwd(qkv, o, do, name, comm=None):
    assert do.dtype == BF16
    t = qkv.shape[0]
    bk = ATT_BLOCK
    scale = HEAD_DIM ** -0.5
    hp, wide = SB_HEADS_BWD, SB_HEADS_BWD * HEAD_DIM
    lane = [slice(e * HEAD_DIM, (e + 1) * HEAD_DIM) for e in range(hp)]

    def body(q_ref, k_ref, v_ref, o_ref, do_ref, dq_ref, dk_out, dv_out, dk_ref, dv_ref):
        qi = pl.program_id(1)

        @pl.when(qi == 0)
        def _():
            dk_ref[...] = jnp.zeros_like(dk_ref)
            dv_ref[...] = jnp.zeros_like(dv_ref)

        heads, groups = range(hp), range(SB_GROUP)
        qs = [q_ref[:, s] for s in lane]
        dob = [do_ref[:, s] for s in lane]
        dsum = [jnp.sum(dob[e].astype(F32) * o_ref[:, lane[e]], axis=1, keepdims=True) for e in heads]
        uincl = jnp.where(_tri(bk, "lower"), 1.0, 0.0).astype(BF16)

        def step(carry):
            it, cbs, ces, dqs = carry
            rows, ks, weights, mask, lk_full, ls, cbs = _sb_scores(qs, k_ref, qi, it, cbs, uincl)
            ab = [[a.astype(BF16) for a in weights[e]] for e in heads]
            vs = [[v_ref[r, lane[e]] for r in rows] for e in heads]
            dla = [[ab[e][g].astype(F32) * _dot(dob[e], vs[e][g], NT) for g in groups] for e in heads]
            suf = [[_dot_xr2(a, uincl, NN) for a in dla[e]] for e in heads]
            ces, dqs = list(ces), list(dqs)
            for e in heads:
                for g in groups:
                    err = dsum[e] - (ces[e] + suf[e][g])
                    ces[e] = ces[e] + suf[e][g][:, 0:1]
                    dz = jnp.where(mask[g], dla[e][g] * jnp.exp(lk_full[e][g]) - err * jnp.exp(ls[e][g]), 0.0)
                    dzb = (dz * scale).astype(BF16)
                    dqs[e] = dqs[e] + _dot(dzb, ks[e][g], NN)
                    dk_ref[rows[g], lane[e]] += _dot(dzb, qs[e], TN)
                    dv_ref[rows[g], lane[e]] += _dot(ab[e][g], dob[e], TN)
            return it + 1, tuple(cbs), tuple(ces), tuple(dqs)

        zc = (jnp.zeros((bk, 1), F32),) * hp
        init = (jnp.int32(0), zc, zc, (jnp.zeros((bk, HEAD_DIM), F32),) * hp)
        dqs = lax.while_loop(functools.partial(_sb_more, qi), step, init)[3]
        for e in heads:
            dq_ref[:, lane[e]] = dqs[e].astype(BF16)

        @pl.when(qi == t // bk - 1)
        def _():
            dk_out[...] = dk_ref[...].astype(BF16)
            dv_out[...] = dv_ref[...].astype(BF16)

    ngroup = HEADS // hp
    tw = jax.ShapeDtypeStruct((t, WIDTH), BF16)
    qb = pl.BlockSpec((bk, wide), lambda h, i: (i, h))
    full = lambda off: pl.BlockSpec((t, wide), lambda h, i: (0, off + h))
    return _host_call(
        body, name, comm, _sb_steps(ngroup, t // bk), [tw, tw, tw], (ngroup, t // bk),
        [qb, full(ngroup), full(2 * ngroup), qb, qb], [qb, full(0), full(0)],
        [pltpu.VMEM((t, wide), F32), pltpu.VMEM((t, wide), F32)], ("parallel", "arbitrary"),
        (qkv, qkv, qkv, o, do))


def _merge_fwd(pd, ps, gl, name):
    t = pd.shape[0]
    tr, tc = _tile(t, 512), 512
    nj = D_MODEL // tc

    def body(pd_ref, ps_ref, gd_ref, gs_ref, o_ref):
        gd, gs = gd_ref[...].astype(F32), gs_ref[...].astype(F32)
        o_ref[...] = (_sigmoid(gd) * pd_ref[...].astype(F32) + _sigmoid(gs) * ps_ref[...].astype(F32)).astype(BF16)

    blk = lambda off: pl.BlockSpec((tr, tc), lambda i, j: (i, j + off))
    return pl.pallas_call(
        body, name=name,
        out_shape=jax.ShapeDtypeStruct((t, D_MODEL), BF16),
        grid=(t // tr, nj),
        in_specs=[blk(0), blk(0), blk(0), blk(nj)],
        out_specs=blk(0),
        compiler_params=_params(("parallel", "parallel")),
    )(pd, ps, gl, gl)


def _merge_bwd(dm, pd, ps, gl, name):
    t = pd.shape[0]
    tr, tc = _tile(t, 512), 512
    nj = D_MODEL // tc

    def body(dm_ref, pd_ref, ps_ref, gd_ref, gs_ref, dpd_ref, dps_ref, dgd_ref, dgs_ref):
        dmv = dm_ref[...].astype(F32)
        sd, ss = _sigmoid(gd_ref[...].astype(F32)), _sigmoid(gs_ref[...].astype(F32))
        dpd_ref[...] = (dmv * sd).astype(BF16)
        dps_ref[...] = (dmv * ss).astype(BF16)
        dgd_ref[...] = (dmv * pd_ref[...].astype(F32) * sd * (1.0 - sd)).astype(BF16)
        dgs_ref[...] = (dmv * ps_ref[...].astype(F32) * ss * (1.0 - ss)).astype(BF16)

    blk = lambda off: pl.BlockSpec((tr, tc), lambda i, j: (i, j + off))
    out = jax.ShapeDtypeStruct((t, D_MODEL), BF16)
    return pl.pallas_call(
        body, name=name,
        out_shape=(out, out, out, out),
        grid=(t // tr, nj),
        in_specs=[blk(0), blk(0), blk(0), blk(0), blk(nj)],
        out_specs=(blk(0), blk(0), blk(0), blk(0)),
        compiler_params=_params(("parallel", "parallel")),
    )(dm, pd, ps, gl, gl)


def _local_step(x, target, wts, plan=None, n1=None):
    if n1 is None:
        n1 = _rmsnorm_fwd(x, wts["norm1_w"], "norm1_fwd")
    qkv_pre = _matmul(n1, wts["w_dnqkv_t"], "nt", BF16, "in_dnqkv")
    hgate = _matmul(n1, wts["w_dngate_t"], "nt", BF16, "in_dngate")
    sbqkv = _matmul(n1, wts["w_sbqkv_t"], "nt", BF16, "in_sbqkv")
    gl = _matmul(n1, wts["w_gl_t"], "nt", BF16, "in_gl")
    hab = _matmul(n1, wts["w_ab_t"], "nt", F32, "in_ab")

    act = _dn_pre_fwd(qkv_pre, wts["dn_conv_w"], "dn_pre_fwd")
    gates = _dn_gates_fwd(hab, wts["alog"], wts["dtb"], "dn_gates_fwd")
    u, w, kd, qg, tinv, p = _dn_local_fwd(act, gates, "dn_local_fwd")
    o_dn, sh = _dn_scan_fwd(u, w, kd, qg, p, gates, "dn_scan_fwd")
    y_dn = _dn_post_fwd(o_dn, hgate, wts["dn_norm_w"], "dn_post_fwd")

    o_sb, late = _sb_fwd(sbqkv, "sb_fwd", comm=plan.late_gather() if plan else None)
    if plan:
        wts = {**wts, **plan.late_weights(late)}

    pd = _matmul(y_dn, wts["w_proj_dn"], "nn", BF16, "proj_dn")
    ps = _matmul(o_sb, wts["w_proj_sb"], "nn", BF16, "proj_sb")
    mixed = _merge_fwd(pd, ps, gl, "merge_fwd")
    x1 = _matmul(mixed, wts["w_out"], "nn", F32, "out_proj", add=x)

    n2 = _rmsnorm_fwd(x1, wts["norm2_w"], "norm2_fwd")
    upre = _matmul(n2, wts["ffn_w_up_t"], "nt", BF16, "ffn_up")
    fact = _ffn_act_fwd(upre, wts["ffn_conv_w"], "ffn_act_fwd")
    x2 = _matmul(fact, wts["ffn_w_down"], "nn", F32, "ffn_down", add=x1)

    dx2, g_normf, loss = _final_loss(x2, target, wts["norm_f_w"], "final_loss")

    dfact = _matmul(dx2, wts["ffn_w_down"], "nt", BF16, "ffn_down_dx")
    g_wdown = _matmul(fact, dx2, "tn", BF16, "ffn_down_dw")
    dgc, duc, dwg, dwu = _ffn_act_bwd(dfact, upre, wts["ffn_conv_w"], "ffn_act_bwd")
    g_fconv = jnp.concatenate([dwg, dwu], axis=1)
    dupre = _conv_bwd_data([dgc, duc], wts["ffn_conv_w"], FFN_CONV, BF16, "ffn_conv_bwd")
    dn2 = _matmul(dupre, wts["ffn_w_up_t"], "nn", F32, "ffn_up_dx")
    g_wup = _matmul(dupre, n2, "tn", BF16, "ffn_up_dw")
    dx1, g_norm2 = _rmsnorm_bwd(dn2, x1, wts["norm2_w"], dx2, "norm2_bwd")

    dmixed = _matmul(dx1, wts["w_out"], "nt", BF16, "out_proj_dx")
    g_wout = _matmul(mixed, dx1, "tn", BF16, "out_proj_dw")
    dpd, dps, dgd, dgs = _merge_bwd(dmixed, pd, ps, gl, "merge_bwd")
    dy_dn = _matmul(dpd, wts["w_proj_dn"], "nt", BF16, "proj_dn_dx")
    g_wpd = _matmul(y_dn, dpd, "tn", BF16, "proj_dn_dw")
    do_sb = _matmul(dps, wts["w_proj_sb"], "nt", BF16, "proj_sb_dx")
    g_wps = _matmul(o_sb, dps, "tn", BF16, "proj_sb_dw")
    grads = dict(w_proj_dn=g_wpd, w_proj_sb=g_wps, w_out=g_wout, ffn_w_up_t=g_wup, ffn_w_down=g_wdown)

    (dsq, dsk, dsv), got_early = _sb_bwd(sbqkv, o_sb, do_sb, "sb_bwd",
                                         comm=plan.early_grads(grads) if plan else None)

    do_dn, dhgate, g_dnnorm = _dn_post_bwd(dy_dn, o_dn, hgate, wts["dn_norm_w"], "dn_post_bwd")
    dvn, dsh = _dn_scan_bwd(do_dn, w, kd, qg, p, gates, "dn_scan_bwd")
    dq, dk, dv, dgates = _dn_local_bwd(act, gates, u, w, kd, qg, tinv, p, sh, dsh, dvn, do_dn, "dn_local_bwd")
    dhab, g_alog, g_dtb = _dn_gates_bwd(dgates, hab, wts["alog"], wts["dtb"], "dn_gates_bwd")
    dcv, g_dnconv = _dn_pre_bwd(dq, dk, dv, qkv_pre, wts["dn_conv_w"], "dn_pre_bwd")
    dqkv_pre = _conv_bwd_data([dcv], wts["dn_conv_w"], DN_CONV, BF16, "dn_conv_bwd")

    dh = jnp.concatenate([dqkv_pre, dhgate, dsq, dsk, dsv, dgd, dgs], axis=1)
    w_main_t = jnp.concatenate([wts["w_dnqkv_t"], wts["w_dngate_t"], wts["w_sbqkv_t"], wts["w_gl_t"]], axis=0)
    g_wmain = _matmul(dh, n1, "tn", BF16, "in_dw_main")
    g_wab = _matmul(dhab, n1, "tn", BF16, "in_dw_ab")
    grads.update(w_main_t=g_wmain, w_ab_t=g_wab, dn_conv_w=g_dnconv, alog=g_alog, dtb=g_dtb, dn_norm_w=g_dnnorm,
                 norm2_w=g_norm2, ffn_conv_w=g_fconv, norm_f_w=g_normf)
    got_late = []
    if plan:
        dn1, swapped = _matmul(dhab, wts["w_ab_t"], "nn", F32, "in_dx_ab", comm=plan.sibling_swap(grads))
        dn1, got_late = _matmul(dh, w_main_t, "nn", F32, "in_dx_main", add=dn1,
                                comm=plan.late_grads(swapped, grads, loss))
    else:
        dn1 = _matmul(dhab, wts["w_ab_t"], "nn", F32, "in_dx_ab")
        dn1 = _matmul(dh, w_main_t, "nn", F32, "in_dx_main", add=dn1)
    grad_x, g_norm1 = _rmsnorm_bwd(dn1, x, wts["norm1_w"], dx1, "norm1_bwd")
    grads["norm1_w"] = g_norm1
    return loss, grad_x, grads, got_early, got_late


HBM_SPEC = pl.BlockSpec(memory_space=pltpu.HBM)


def _mesh_pos():
    x, y, c = lax.axis_index("x"), lax.axis_index("y"), lax.axis_index("c")
    return x, y, c, 4 * x + 2 * y + c


def _peer(k):
    x, y, c, _ = _mesh_pos()
    px = 1 - x if k & 4 else x
    py = 1 - y if k & 2 else y
    pc = 1 - c if k & 1 else c
    return (px, py, pc), 4 * px + 2 * py + pc


def _rcopy(src, dst, send, recv, a, s, peer):
    return pltpu.make_async_remote_copy(src_ref=src, dst_ref=dst, send_sem=send.at[a, s], recv_sem=recv.at[a, s],
                                        device_id=peer, device_id_type=pl.DeviceIdType.MESH)


class _Gather:
    ICI = (2, 4, 6)

    def __init__(self, shards):
        self.args = list(shards)
        self.n = len(shards)
        self.out_shape = [jax.ShapeDtypeStruct((N_DEV,) + s.shape, s.dtype) for s in shards]
        self.scratch = [pltpu.SemaphoreType.DMA((self.n, N_DEV - 1)), pltpu.SemaphoreType.DMA((self.n, N_DEV - 1)),
                        pltpu.SemaphoreType.DMA((self.n,))]

    def _slot(self, outs, a, d):
        return outs[a].at[d]

    def _first(self, ins, outs, send, recv, a):
        me = _mesh_pos()[3]
        out, got = [], []
        for s, k in enumerate((1,) + self.ICI):
            peer, pidx = _peer(k)
            out.append(_rcopy(ins[a], self._slot(outs, a, me), send, recv, a, s, peer))
            got.append(_rcopy(ins[a], self._slot(outs, a, pidx), send, recv, a, s, peer))
        return out, got

    def _forward(self, ins, outs, send, recv, a):
        sib = _peer(1)[0]
        out, got = [], []
        for s, k in enumerate(self.ICI):
            held = self._slot(outs, a, _peer(k)[1])
            out.append(_rcopy(held, held, send, recv, a, 4 + s, sib))
            other = self._slot(outs, a, _peer(k | 1)[1])
            got.append(_rcopy(other, other, send, recv, a, 4 + s, sib))
        return out, got

    def start(self, ins, outs, sems):
        send, recv, loc = sems
        me = _mesh_pos()[3]
        for a in range(self.n):
            pltpu.make_async_copy(ins[a], self._slot(outs, a, me), loc.at[a]).start()
            for cp in self._first(ins, outs, send, recv, a)[0]:
                cp.start()

    def mid(self, ins, outs, sems):
        send, recv, _ = sems
        for a in range(self.n):
            arrivals = self._first(ins, outs, send, recv, a)[1]
            for s, cp in enumerate(self._forward(ins, outs, send, recv, a)[0]):
                arrivals[1 + s].wait_recv()
                cp.start()

    def finish(self, ins, outs, sems):
        send, recv, loc = sems
        me = _mesh_pos()[3]
        for a in range(self.n):
            first_out, first_got = self._first(ins, outs, send, recv, a)
            fwd_out, fwd_got = self._forward(ins, outs, send, recv, a)
            first_got[0].wait_recv()
            for cp in fwd_got:
                cp.wait_recv()
            for cp in first_out + fwd_out:
                cp.wait_send()
            pltpu.make_async_copy(ins[a], self._slot(outs, a, me), loc.at[a]).wait()


class _Exchange:
    def __init__(self, slabs=(), gathered=(), chip_slabs=(), sibling_slabs=()):
        self.args = list(slabs) + list(chip_slabs) + list(sibling_slabs) + list(gathered)
        self.kind = (["dev"] * len(slabs) + ["chip"] * len(chip_slabs) + ["sib"] * len(sibling_slabs)
                     + ["all"] * len(gathered))
        self.n = len(self.args)
        half = lambda s: jax.ShapeDtypeStruct((N_DEV // 2,) + s.shape[1:], s.dtype)
        self.out_shape = ([jax.ShapeDtypeStruct(s.shape, s.dtype) for s in slabs]
                          + [half(s) for s in chip_slabs] + [half(s) for s in sibling_slabs]
                          + [jax.ShapeDtypeStruct((N_DEV,) + s.shape, s.dtype) for s in gathered])
        self.scratch = [pltpu.SemaphoreType.DMA((self.n, N_DEV - 1)), pltpu.SemaphoreType.DMA((self.n, N_DEV - 1)),
                        pltpu.SemaphoreType.DMA((self.n,))]

    def _copies(self, ins, outs, send, recv, a):
        x, y, c, me = _mesh_pos()
        kind = self.kind[a]
        out, got = [], []
        if kind == "sib":
            sib = _peer(1)[0]
            for q in range(N_DEV // 2):
                out.append(_rcopy(ins[a].at[2 * q + 1 - c], outs[a].at[q], send, recv, a, q, sib))
                got.append(_rcopy(ins[a].at[2 * q + c], outs[a].at[q], send, recv, a, q, sib))
            return out, got
        for k in ((2, 4, 6) if kind == "chip" else range(1, N_DEV)):
            peer, pidx = _peer(k)
            if kind == "chip":
                src, mine, theirs = ins[a].at[2 * peer[0] + peer[1]], 2 * x + y, 2 * peer[0] + peer[1]
            else:
                src, mine, theirs = (ins[a].at[pidx] if kind == "dev" else ins[a]), me, pidx
            out.append(_rcopy(src, outs[a].at[mine], send, recv, a, k - 1, peer))
            got.append(_rcopy(src, outs[a].at[theirs], send, recv, a, k - 1, peer))
        return out, got

    def _local(self, ins, outs, loc, a):
        x, y, _, me = _mesh_pos()
        kind = self.kind[a]
        if kind == "sib":
            return None
        if kind == "chip":
            return pltpu.make_async_copy(ins[a].at[2 * x + y], outs[a].at[2 * x + y], loc.at[a])
        return pltpu.make_async_copy(ins[a].at[me] if kind == "dev" else ins[a], outs[a].at[me], loc.at[a])

    def start(self, ins, outs, sems):
        send, recv, loc = sems
        for a in range(self.n):
            if self._local(ins, outs, loc, a) is not None:
                self._local(ins, outs, loc, a).start()
            for cp in self._copies(ins, outs, send, recv, a)[0]:
                cp.start()

    def mid(self, ins, outs, sems):
        pass

    def finish(self, ins, outs, sems):
        send, recv, loc = sems
        for a in range(self.n):
            out, got = self._copies(ins, outs, send, recv, a)
            for cp in got:
                cp.wait_recv()
            for cp in out:
                cp.wait_send()
            if self._local(ins, outs, loc, a) is not None:
                self._local(ins, outs, loc, a).wait()


def _comm_call(comm, name):
    n = comm.n

    def body(*refs):
        ins, outs, sems = refs[:n], refs[n:2 * n], refs[2 * n:]
        comm.start(ins, outs, sems)
        comm.mid(ins, outs, sems)
        comm.finish(ins, outs, sems)

    return pl.pallas_call(
        body, name=name, out_shape=comm.out_shape, in_specs=[HBM_SPEC] * n, out_specs=[HBM_SPEC] * n,
        scratch_shapes=comm.scratch,
    )(*comm.args)


def _hosted(body, comm, n_in, n_out, when):
    if comm is None:
        return body

    def wrapped(*refs):
        ins, c_ins = refs[:n_in], refs[n_in:n_in + comm.n]
        o0 = n_in + comm.n
        outs, c_outs = refs[o0:o0 + n_out], refs[o0 + n_out:o0 + n_out + comm.n]
        scratch, sems = refs[o0 + n_out + comm.n:len(refs) - 3], refs[len(refs) - 3:]
        first, middle, last = when()

        @pl.when(first)
        def _():
            comm.start(c_ins, c_outs, sems)

        body(*ins, *outs, *scratch)

        @pl.when(middle)
        def _():
            comm.mid(c_ins, c_outs, sems)

        @pl.when(last)
        def _():
            comm.finish(c_ins, c_outs, sems)

    return wrapped


def _host_call(body, name, comm, when, out_shape, grid, in_specs, out_specs, scratch_shapes, sem, args):
    n_in, n_out = len(in_specs), len(out_specs)
    if comm is None:
        res = pl.pallas_call(body, name=name, out_shape=out_shape, grid=grid, in_specs=in_specs, out_specs=out_specs,
                             scratch_shapes=scratch_shapes, compiler_params=_params(sem))(*args)
        return list(res), []
    res = pl.pallas_call(
        _hosted(body, comm, n_in, n_out, when), name=name,
        out_shape=list(out_shape) + comm.out_shape, grid=grid,
        in_specs=list(in_specs) + [HBM_SPEC] * comm.n, out_specs=list(out_specs) + [HBM_SPEC] * comm.n,
        scratch_shapes=list(scratch_shapes) + comm.scratch,
        compiler_params=_params(("arbitrary",) * len(grid)),
    )(*args, *comm.args)
    return list(res[:n_out]), list(res[n_out:])


def _add_my_slabs(slabs, b, name):
    n, rows, cols = b.shape
    tc = _tile(cols, 256)

    def body(a_ref, b_ref, o_ref):
        o_ref[...] = (a_ref[...].astype(F32) + b_ref[...].astype(F32)).astype(o_ref.dtype)

    blk = pl.BlockSpec((None, rows, tc), lambda i, j: (i, 0, j))
    mine = pl.BlockSpec((None, rows, tc), lambda i, j: (2 * i + lax.axis_index("c"), 0, j))
    return pl.pallas_call(
        body, name=name, out_shape=jax.ShapeDtypeStruct(b.shape, b.dtype), grid=(n, cols // tc),
        in_specs=[mine, blk], out_specs=blk, compiler_params=_params(("parallel", "parallel")),
    )(slabs, b)


def _adamw(parts, w, m, v, name):
    rows, cols = w.shape
    nparts = parts.shape[0]
    tr, tc = rows, cols
    for cand in (128, 176):
        if rows > cand and rows % cand == 0:
            tr = cand
            break
    if tr == rows and rows > 512:
        tc = _tile(cols, 256)

    def body(p_ref, w_ref, m_ref, v_ref, g_ref, d_ref, mo_ref, vo_ref):
        g = p_ref[0].astype(F32)
        for s in range(1, nparts):
            g = g + p_ref[s].astype(F32)
        mn = ADAM_B1 * m_ref[...] + (1.0 - ADAM_B1) * g
        vn = ADAM_B2 * v_ref[...] + (1.0 - ADAM_B2) * (g * g)
        m_hat = mn / (1.0 - ADAM_B1 ** ADAM_STEP)
        v_hat = vn / (1.0 - ADAM_B2 ** ADAM_STEP)
        g_ref[...] = g
        d_ref[...] = -ADAM_LR * (m_hat / (jnp.sqrt(v_hat) + ADAM_EPS) + ADAM_WD * w_ref[...])
        mo_ref[...] = mn
        vo_ref[...] = vn

    blk = pl.BlockSpec((tr, tc), lambda i, j: (i, j))
    out = jax.ShapeDtypeStruct((rows, cols), F32)
    return pl.pallas_call(
        body, name=name,
        out_shape=(out, out, out, out),
        grid=(rows // tr, cols // tc),
        in_specs=[pl.BlockSpec((nparts, tr, tc), lambda i, j: (0, i, j)), blk, blk, blk],
        out_specs=(blk, blk, blk, blk),
        compiler_params=_params(("parallel", "parallel")),
    )(parts, w, m, v)


CONV_PACK = 8 * 1024
WEIGHT_ORDER = ("norm1_w", "w_in", "dn_conv_w", "dn_A_log", "dn_dt_bias", "dn_norm_w", "w_proj_dn", "w_proj_sb",
                "w_out", "norm2_w", "ffn_w_up", "ffn_conv_w", "ffn_w_down", "norm_f_w")


def _cols_to_slabs(g):
    r, c8 = g.shape
    return g.reshape(r, N_DEV, c8 // N_DEV).transpose(1, 0, 2)


def _slabs_to_cols(s):
    d, r, c = s.shape
    return s.transpose(1, 0, 2).reshape(r, d * c)


def kernel(x, norm1_w, w_in, dn_conv_w, dn_A_log, dn_dt_bias, dn_norm_w, w_proj_dn, w_proj_sb, w_out, norm2_w, ffn_w_up, ffn_conv_w, ffn_w_down, norm_f_w, loss_target, m_norm1_w, m_w_in, m_dn_conv_w, m_dn_A_log, m_dn_dt_bias, m_dn_norm_w, m_w_proj_dn, m_w_proj_sb, m_w_out, m_norm2_w, m_ffn_w_up, m_ffn_conv_w, m_ffn_w_down, m_norm_f_w, v_norm1_w, v_w_in, v_dn_conv_w, v_dn_A_log, v_dn_dt_bias, v_dn_norm_w, v_w_proj_dn, v_w_proj_sb, v_w_out, v_norm2_w, v_ffn_w_up, v_ffn_conv_w, v_ffn_w_down, v_norm_f_w):
    me = _mesh_pos()[3]
    tr = lambda a: jnp.transpose(a[0])
    w_loc = dict(norm1_w=norm1_w, w_in=tr(w_in), dn_conv_w=dn_conv_w[0], dn_A_log=dn_A_log, dn_dt_bias=dn_dt_bias,
                 dn_norm_w=dn_norm_w, w_proj_dn=w_proj_dn[0], w_proj_sb=w_proj_sb[0], w_out=w_out[0],
                 norm2_w=norm2_w, ffn_w_up=tr(ffn_w_up), ffn_conv_w=ffn_conv_w[0], ffn_w_down=ffn_w_down[0],
                 norm_f_w=norm_f_w[None, :])
    m_loc = dict(norm1_w=m_norm1_w, w_in=tr(m_w_in), dn_conv_w=m_dn_conv_w[0], dn_A_log=m_dn_A_log,
                 dn_dt_bias=m_dn_dt_bias, dn_norm_w=m_dn_norm_w, w_proj_dn=m_w_proj_dn[0], w_proj_sb=m_w_proj_sb[0],
                 w_out=m_w_out[0], norm2_w=m_norm2_w, ffn_w_up=tr(m_ffn_w_up), ffn_conv_w=m_ffn_conv_w[0],
                 ffn_w_down=m_ffn_w_down[0], norm_f_w=m_norm_f_w[None, :])
    v_loc = dict(norm1_w=v_norm1_w, w_in=tr(v_w_in), dn_conv_w=v_dn_conv_w[0], dn_A_log=v_dn_A_log,
                 dn_dt_bias=v_dn_dt_bias, dn_norm_w=v_dn_norm_w, w_proj_dn=v_w_proj_dn[0], w_proj_sb=v_w_proj_sb[0],
                 w_out=v_w_out[0], norm2_w=v_norm2_w, ffn_w_up=tr(v_ffn_w_up), ffn_conv_w=v_ffn_conv_w[0],
                 ffn_w_down=v_ffn_w_down[0], norm_f_w=v_norm_f_w[None, :])

    conv_flat = jnp.concatenate([w_loc["dn_conv_w"].reshape(-1), w_loc["ffn_conv_w"].reshape(-1)])
    n_dn, n_ffn = DN_CONV * 3 * WIDTH // N_DEV, FFN_CONV * 2 * D_FF // N_DEV
    conv_pack = jnp.pad(conv_flat, (0, CONV_PACK - n_dn - n_ffn)).reshape(8, 1024)
    n1, (g_in, g_conv) = _rmsnorm_fwd(x[0], norm1_w, "norm1_fwd",
                                      comm=_Gather([w_loc["w_in"].astype(BF16), conv_pack]))
    in_width = g_in.shape[0] * g_in.shape[1]
    w_in_t = g_in.reshape(in_width, D_MODEL)
    g_conv = g_conv.reshape(N_DEV, CONV_PACK)
    dn_conv_full = _slabs_to_cols(g_conv[:, :n_dn].reshape(N_DEV, DN_CONV, 3 * WIDTH // N_DEV))
    ffn_conv_full = _slabs_to_cols(g_conv[:, n_dn:n_dn + n_ffn].reshape(N_DEV, FFN_CONV, 2 * D_FF // N_DEV))
    q_end = 3 * WIDTH
    ab_end = q_end + 2 * HEADS
    gate_end = ab_end + WIDTH
    sb_end = gate_end + 3 * WIDTH
    pad_lanes = lambda a: jnp.pad(a, ((0, 0), (0, 128 - a.shape[1])))
    wts = dict(
        norm1_w=norm1_w, w_dnqkv_t=w_in_t[:q_end], w_ab_t=jnp.pad(w_in_t[q_end:ab_end], ((0, 128 - 2 * HEADS), (0, 0))),
        w_dngate_t=w_in_t[ab_end:gate_end], w_sbqkv_t=w_in_t[gate_end:sb_end], w_gl_t=w_in_t[sb_end:],
        dn_conv_w=dn_conv_full, alog=pad_lanes(dn_A_log), dtb=pad_lanes(dn_dt_bias), dn_norm_w=dn_norm_w,
        norm2_w=norm2_w, ffn_conv_w=ffn_conv_full, norm_f_w=norm_f_w[None, :])

    n_fc = FFN_CONV * 2 * D_FF
    fc_rows = -(-n_fc // D_MODEL)
    dn_rows = DN_CONV * 3 * WIDTH // D_MODEL
    late_names = ("w_proj_dn", "w_proj_sb", "w_out", "ffn_w_up", "ffn_w_down")

    class Plan:
        @staticmethod
        def late_gather():
            return _Gather([w_loc[k].astype(BF16) for k in late_names])

        @staticmethod
        def late_weights(got):
            g_pd, g_ps, g_out, g_up, g_down = got
            return dict(w_proj_dn=g_pd.reshape(WIDTH, D_MODEL), w_proj_sb=g_ps.reshape(WIDTH, D_MODEL),
                        w_out=g_out.reshape(D_MODEL, D_MODEL), ffn_w_up_t=g_up.reshape(2 * D_FF, D_MODEL),
                        ffn_w_down=g_down.reshape(D_FF, D_MODEL))

        @staticmethod
        def early_grads(g):
            return _Exchange([g["w_proj_dn"].reshape(N_DEV, WIDTH // N_DEV, D_MODEL),
                              g["w_proj_sb"].reshape(N_DEV, WIDTH // N_DEV, D_MODEL),
                              g["w_out"].reshape(N_DEV, D_MODEL // N_DEV, D_MODEL),
                              g["ffn_w_up_t"].reshape(N_DEV, 2 * D_FF // N_DEV, D_MODEL),
                              g["ffn_w_down"].reshape(N_DEV, D_FF // N_DEV, D_MODEL)])

        @staticmethod
        def _in_slabs(g):
            g_win_t = jnp.concatenate([g["w_main_t"][:q_end], g["w_ab_t"][:2 * HEADS], g["w_main_t"][q_end:]],
                                      axis=0)
            return g_win_t.reshape(N_DEV, in_width // N_DEV, D_MODEL)

        @staticmethod
        def sibling_swap(g):
            return _Exchange(sibling_slabs=[Plan._in_slabs(g)])

        @staticmethod
        def late_grads(swapped, g, loss):
            chip_sums = _add_my_slabs(Plan._in_slabs(g), swapped[0], "in_dw_chip_sum")
            row3 = jnp.concatenate([g["dn_norm_w"], g["alog"], g["dtb"], jnp.pad(loss, ((0, 0), (0, 127))),
                                    jnp.zeros((1, D_MODEL - 512), F32)], axis=1)
            fconv_rows = jnp.pad(g["ffn_conv_w"].reshape(-1), (0, fc_rows * D_MODEL - n_fc)).reshape(fc_rows, D_MODEL)
            pad8 = lambda a: jnp.pad(a, ((0, -a.shape[0] % 8), (0, 0)))
            pieces = [g["norm2_w"], g["norm_f_w"], row3, g["dn_conv_w"].reshape(dn_rows, D_MODEL), fconv_rows]
            small = jnp.concatenate([pad8(a) for a in pieces], axis=0)
            assert small.shape[0] == SMALL_ROWS
            return _Exchange(chip_slabs=[chip_sums], gathered=[small])

    loss, grad_x, g, got_early, got_late = _local_step(x[0], loss_target[0], wts, Plan, n1)
    r_pd, r_ps, r_out, r_up, r_down = got_early
    r_in, r_small = got_late
    (r_norm1,) = _comm_call(_Exchange([], [jnp.pad(g["norm1_w"], ((0, 7), (0, 0)))]), "gather_norm1")

    parts = dict(w_in=r_in, w_proj_dn=r_pd, w_proj_sb=r_ps, w_out=r_out, ffn_w_up=r_up, ffn_w_down=r_down)
    parts["norm1_w"] = r_norm1[:, 0:1, :]
    parts["norm2_w"] = r_small[:, 0:1, :]
    parts["norm_f_w"] = r_small[:, 8:9, :]
    parts["dn_norm_w"] = r_small[:, 16:17, 0:HEAD_DIM]
    parts["dn_A_log"] = r_small[:, 16:17, 128:128 + HEADS]
    parts["dn_dt_bias"] = r_small[:, 16:17, 256:256 + HEADS]
    dnc = r_small[:, 24:24 + dn_rows, :].reshape(N_DEV, DN_CONV, 3 * WIDTH)
    parts["dn_conv_w"] = lax.dynamic_slice_in_dim(dnc, me * (3 * WIDTH // N_DEV), 3 * WIDTH // N_DEV, axis=2)
    fc0 = 24 + dn_rows + (-dn_rows % 8)
    fcc = r_small[:, fc0:fc0 + fc_rows, :].reshape(N_DEV, fc_rows * D_MODEL)[:, :n_fc]
    fcc = fcc.reshape(N_DEV, FFN_CONV, 2 * D_FF)
    parts["ffn_conv_w"] = lax.dynamic_slice_in_dim(fcc, me * (2 * D_FF // N_DEV), 2 * D_FF // N_DEV, axis=2)
    loss_total = jnp.sum(r_small[:, 16, 384])

    res = {k: _adamw(parts[k], w_loc[k], m_loc[k], v_loc[k], "adamw_" + k) for k in WEIGHT_ORDER}
    lead = ("w_in", "dn_conv_w", "w_proj_dn", "w_proj_sb", "w_out", "ffn_w_up", "ffn_conv_w", "ffn_w_down")

    def shaped(k, a):
        if k in ("w_in", "ffn_w_up"):
            return jnp.transpose(a)[None]
        if k in lead:
            return a[None]
        if k == "norm_f_w":
            return a[0]
        return a

    outs = [loss_total, grad_x[None]]
    for idx in range(4):
        outs += [shaped(k, res[k][idx]) for k in WEIGHT_ORDER]
    return tuple(outs)
```

```python
import functools

import jax
import jax.numpy as jnp
from jax import lax
from jax.experimental import pallas as pl
from jax.experimental.pallas import tpu as pltpu

F32 = jnp.float32
BF16 = jnp.bfloat16

N_DEV = 8
D_MODEL = 1024
HEADS = 8
HEAD_DIM = 128
WIDTH = HEADS * HEAD_DIM
DN_CONV = 4
DN_CHUNK = 64
D_FF = 2816
FFN_CONV = 3
EPS = 1e-6
HALO = 16
CHUNK_ROWS = 256
SCAN_CHUNKS = 4
LOCAL_CHUNKS = 2
ATT_BLOCK = 256
SB_LOG_ZERO = -104.0
SB_GROUP = 2
SB_HEADS_FWD = 4
SB_HEADS_BWD = 2
SMALL_ROWS = 64

ADAM_LR = 0.001
ADAM_B1 = 0.9
ADAM_B2 = 0.999
ADAM_EPS = 1e-08
ADAM_WD = 0.01
ADAM_STEP = 10

VMEM_LIMIT = 48 * 1024 * 1024


def _params(sem=None, **kw):
    return pltpu.CompilerParams(dimension_semantics=sem, vmem_limit_bytes=VMEM_LIMIT, **kw)


def _tile(n, cap):
    if n <= cap:
        return n
    best = None
    for t in range(128, cap + 1, 128):
        if n % t == 0:
            best = t
    assert best is not None, (n, cap)
    return best


def _dot(a, b, dims):
    return lax.dot_general(a, b, ((dims[0], dims[1]), ((), ())), preferred_element_type=F32)


NN = ((1,), (0,))
NT = ((1,), (1,))
TN = ((0,), (0,))


def _dotb(a, b, dims):
    return _dot(a.astype(BF16), b.astype(BF16), dims)


def _split3(x):
    h1 = x.astype(BF16)
    r1 = x - h1.astype(F32)
    h2 = r1.astype(BF16)
    r2 = r1 - h2.astype(F32)
    return h1, h2, r2.astype(BF16)


def _dot_xr(a, b_exact, dims):
    a1, a2, a3 = _split3(a)
    return _dot(a1, b_exact, dims) + _dot(a2, b_exact, dims) + _dot(a3, b_exact, dims)


def _split2(x):
    h1 = x.astype(BF16)
    return h1, (x - h1.astype(F32)).astype(BF16)


def _dot_xr2(a, b_exact, dims):
    a1, a2 = _split2(a)
    return _dot(a1, b_exact, dims) + _dot(a2, b_exact, dims)


def _dot_xl(a_exact, b, dims):
    b1, b2, b3 = _split3(b)
    return _dot(a_exact, b1, dims) + _dot(a_exact, b2, dims) + _dot(a_exact, b3, dims)


def _dot3(a, b, dims):
    a1 = a.astype(BF16)
    a2 = (a - a1.astype(F32)).astype(BF16)
    b1 = b.astype(BF16)
    b2 = (b - b1.astype(F32)).astype(BF16)
    return _dot(a1, b1, dims) + (_dot(a1, b2, dims) + _dot(a2, b1, dims))


def _sigmoid(x):
    return 1.0 / (1.0 + jnp.exp(-x))


def _log1pexp_neg_abs(x):
    return jnp.log(1.0 + jnp.exp(-jnp.abs(x)))


def _iota(shape, dim):
    return lax.broadcasted_iota(jnp.int32, shape, dim)


def _matmul(a, b, mode, out_dtype, name, add=None, comm=None):
    if mode == "nn":
        (m, k), (k2, n) = a.shape, b.shape
    elif mode == "nt":
        (m, k), (n, k2) = a.shape, b.shape
    else:
        (k, m), (k2, n) = a.shape, b.shape
    assert k == k2, (a.shape, b.shape, mode)
    tm, tn, tk = _tile(m, 1408), _tile(n, 1408), _tile(k, 1536)
    nk = k // tk
    dims = {"nn": NN, "nt": NT, "tn": TN}[mode]

    def body(*refs):
        if add is None:
            a_ref, b_ref, o_ref, acc_ref = refs
        else:
            a_ref, b_ref, add_ref, o_ref, acc_ref = refs
        kk = pl.program_id(2)

        @pl.when(kk == 0)
        def _():
            acc_ref[...] = jnp.zeros_like(acc_ref)

        acc_ref[...] += _dotb(a_ref[...], b_ref[...], dims)

        @pl.when(kk == nk - 1)
        def _():
            r = acc_ref[...]
            if add is not None:
                r = r + add_ref[...].astype(F32)
            o_ref[...] = r.astype(out_dtype)

    if mode == "nn":
        specs = [pl.BlockSpec((tm, tk), lambda i, j, l: (i, l)), pl.BlockSpec((tk, tn), lambda i, j, l: (l, j))]
    elif mode == "nt":
        specs = [pl.BlockSpec((tm, tk), lambda i, j, l: (i, l)), pl.BlockSpec((tn, tk), lambda i, j, l: (j, l))]
    else:
        specs = [pl.BlockSpec((tk, tm), lambda i, j, l: (l, i)), pl.BlockSpec((tk, tn), lambda i, j, l: (l, j))]
    args = [a, b]
    if add is not None:
        specs.append(pl.BlockSpec((tm, tn), lambda i, j, l: (i, j)))
        args.append(add)
    grid = (m // tm, n // tn, nk)

    def when():
        i, j, l = pl.program_id(0), pl.program_id(1), pl.program_id(2)
        first = jnp.logical_and(jnp.logical_and(i == 0, j == 0), l == 0)
        last = jnp.logical_and(jnp.logical_and(i == grid[0] - 1, j == grid[1] - 1), l == nk - 1)
        return first, last, last

    (out,), extra = _host_call(
        body, name, comm, when, [jax.ShapeDtypeStruct((m, n), out_dtype)], grid, specs,
        [pl.BlockSpec((tm, tn), lambda i, j, l: (i, j))], [pltpu.VMEM((tm, tn), F32)],
        ("parallel", "parallel", "arbitrary"), args)
    return out if comm is None else (out, extra)


def _rmsnorm_fwd(x, w, name, comm=None):
    t, d = x.shape
    tr = _tile(t, 512)
    steps = t // tr

    def body(x_ref, w_ref, o_ref):
        xv = x_ref[...]
        r = lax.rsqrt(jnp.mean(xv * xv, axis=1, keepdims=True) + EPS)
        o_ref[...] = (xv * r * w_ref[...]).astype(BF16)

    def when():
        i = pl.program_id(0)
        return i == 0, i == steps // 2, i == steps - 1

    (out,), extra = _host_call(
        body, name, comm, when, [jax.ShapeDtypeStruct((t, d), BF16)], (steps,),
        [pl.BlockSpec((tr, d), lambda i: (i, 0)), pl.BlockSpec((1, d), lambda i: (0, 0))],
        [pl.BlockSpec((tr, d), lambda i: (i, 0))], [], ("parallel",), (x, w))
    return out if comm is None else (out, extra)


def _rmsnorm_bwd(dn, x, w, dres, name):
    t, d = x.shape
    tr = _tile(t, 512)

    def body(dn_ref, x_ref, w_ref, dres_ref, dx_ref, dw_ref):
        i = pl.program_id(0)
        xv = x_ref[...]
        g = dn_ref[...].astype(F32)
        r = lax.rsqrt(jnp.mean(xv * xv, axis=1, keepdims=True) + EPS)
        xh = xv * r
        dxh = g * w_ref[...]
        dx = r * (dxh - xh * jnp.mean(dxh * xh, axis=1, keepdims=True))
        dx_ref[...] = dres_ref[...] + dx

        @pl.when(i == 0)
        def _():
            dw_ref[...] = jnp.zeros_like(dw_ref)

        dw_ref[...] += jnp.sum(g * xh, axis=0, keepdims=True)

    return pl.pallas_call(
        body, name=name,
        out_shape=(jax.ShapeDtypeStruct((t, d), F32), jax.ShapeDtypeStruct((1, d), F32)),
        grid=(t // tr,),
        in_specs=[pl.BlockSpec((tr, d), lambda i: (i, 0)), pl.BlockSpec((tr, d), lambda i: (i, 0)),
                  pl.BlockSpec((1, d), lambda i: (0, 0)), pl.BlockSpec((tr, d), lambda i: (i, 0))],
        out_specs=(pl.BlockSpec((tr, d), lambda i: (i, 0)), pl.BlockSpec((1, d), lambda i: (0, 0))),
        compiler_params=_params(("arbitrary",)),
    )(dn, x, w, dres)


def _final_loss(x2, target, w, name):
    t, d = x2.shape
    tr = _tile(t, 512)

    def body(x_ref, t_ref, w_ref, dx_ref, dw_ref, loss_ref):
        i = pl.program_id(0)
        xv = x_ref[...]
        r = lax.rsqrt(jnp.mean(xv * xv, axis=1, keepdims=True) + EPS)
        xh = xv * r
        err = xh * w_ref[...] - t_ref[...]
        dy = err * (1.0 / d)
        dxh = dy * w_ref[...]
        dx_ref[...] = r * (dxh - xh * jnp.mean(dxh * xh, axis=1, keepdims=True))

        @pl.when(i == 0)
        def _():
            dw_ref[...] = jnp.zeros_like(dw_ref)
            loss_ref[...] = jnp.zeros_like(loss_ref)

        dw_ref[...] += jnp.sum(dy * xh, axis=0, keepdims=True)
        row = jnp.sum(err * err, axis=1, keepdims=True) * (0.5 / d)
        loss_ref[...] += jnp.sum(row, axis=0, keepdims=True)

    return pl.pallas_call(
        body, name=name,
        out_shape=(jax.ShapeDtypeStruct((t, d), F32), jax.ShapeDtypeStruct((1, d), F32),
                   jax.ShapeDtypeStruct((1, 1), F32)),
        grid=(t // tr,),
        in_specs=[pl.BlockSpec((tr, d), lambda i: (i, 0)), pl.BlockSpec((tr, d), lambda i: (i, 0)),
                  pl.BlockSpec((1, d), lambda i: (0, 0))],
        out_specs=(pl.BlockSpec((tr, d), lambda i: (i, 0)), pl.BlockSpec((1, d), lambda i: (0, 0)),
                   pl.BlockSpec((1, 1), lambda i: (0, 0))),
        compiler_params=_params(("arbitrary",)),
    )(x2, target, w)


def _shift_down(cur, prev, k, row):
    r = pltpu.roll(cur, k, 0)
    top, row8 = r[0:8, :], row[0:8, :]
    for m in range(k):
        top = jnp.where(row8 == m, prev[HALO - k + m:HALO - k + m + 1, :], top)
    return jnp.concatenate([top, r[8:, :]], axis=0)


def _shift_up(cur, nxt, k, row, tr):
    r = pltpu.roll(cur, tr - k, 0)
    bottom, row8 = r[tr - 8:, :], row[0:8, :]
    for m in range(k):
        bottom = jnp.where(row8 == 8 - k + m, nxt[m:m + 1, :], bottom)
    return jnp.concatenate([r[:tr - 8, :], bottom], axis=0)


def _fold8(a):
    out = a[0:8, :]
    for r in range(8, a.shape[0], 8):
        out = out + a[r:r + 8, :]
    return out


def _conv_taps(cur, prev, w, ntaps, row):
    taps = [cur if i == ntaps - 1 else _shift_down(cur, prev, ntaps - 1 - i, row) for i in range(ntaps)]
    y = w[0:1, :] * taps[0]
    for i in range(1, ntaps):
        y = y + w[i:i + 1, :] * taps[i]
    return taps, y


def _conv_bwd_data(parts, w, ntaps, out_dtype, name):
    t, chp = parts[0].shape
    npart = len(parts)
    tr, tc = _tile(t, 512), _tile(chp, 1408)
    nc = chp // tc
    nhalo = t // HALO
    last = t // tr - 1

    def body(*refs):
        cur_refs, nxt_refs = refs[:npart], refs[npart:2 * npart]
        w_ref, o_ref = refs[2 * npart], refs[2 * npart + 1]
        i, j = pl.program_id(0), pl.program_id(1)
        row = _iota((tr, 128), 0)
        for c0 in range(0, tc, 128):
            sl = slice(c0, c0 + 128)
            cur, nxt = cur_refs[0][:, sl].astype(F32), nxt_refs[0][:, sl].astype(F32)
            for p in range(1, npart):
                cur = jnp.where(j >= p * nc, cur_refs[p][:, sl].astype(F32), cur)
                nxt = jnp.where(j >= p * nc, nxt_refs[p][:, sl].astype(F32), nxt)
            nxt = jnp.where(i == last, 0.0, nxt)
            wv = w_ref[:, sl]
            y = wv[ntaps - 1:ntaps, :] * cur
            for k in range(1, ntaps):
                y = y + wv[ntaps - 1 - k:ntaps - k, :] * _shift_up(cur, nxt, k, row, tr)
            o_ref[:, sl] = y.astype(out_dtype)

    col = lambda p: (lambda j: jnp.clip(j - p * nc, 0, nc - 1))
    cur_specs = [pl.BlockSpec((tr, tc), lambda i, j, c=col(p): (i, c(j))) for p in range(npart)]
    nxt_specs = [pl.BlockSpec((HALO, tc),
                              lambda i, j, c=col(p): (jnp.minimum((i + 1) * (tr // HALO), nhalo - 1), c(j)))
                 for p in range(npart)]
    return pl.pallas_call(
        body, name=name,
        out_shape=jax.ShapeDtypeStruct((t, npart * chp), out_dtype),
        grid=(t // tr, npart * nc),
        in_specs=cur_specs + nxt_specs + [pl.BlockSpec((ntaps, tc), lambda i, j: (0, j))],
        out_specs=pl.BlockSpec((tr, tc), lambda i, j: (i, j)),
        compiler_params=_params(("parallel", "parallel")),
    )(*parts, *parts, w)


def _ffn_act_fwd(upre, cw, name):
    t = upre.shape[0]
    tr, tc = _tile(t, 512), _tile(D_FF, 1408)
    nj = D_FF // tc

    def body(g_ref, gp_ref, u_ref, up_ref, wg_ref, wu_ref, o_ref):
        i = pl.program_id(0)
        row = _iota((tr, 128), 0)
        for c0 in range(0, tc, 128):
            sl = slice(c0, c0 + 128)
            gp = jnp.where(i == 0, 0.0, gp_ref[:, sl].astype(F32))
            up = jnp.where(i == 0, 0.0, up_ref[:, sl].astype(F32))
            _, gc = _conv_taps(g_ref[:, sl].astype(F32), gp, wg_ref[:, sl], FFN_CONV, row)
            _, uc = _conv_taps(u_ref[:, sl].astype(F32), up, wu_ref[:, sl], FFN_CONV, row)
            o_ref[:, sl] = (gc * _sigmoid(gc) * uc).astype(BF16)

    prev = lambda off: (lambda i, j: (jnp.maximum(i * (tr // HALO) - 1, 0), j + off))
    return pl.pallas_call(
        body, name=name,
        out_shape=jax.ShapeDtypeStruct((t, D_FF), BF16),
        grid=(t // tr, nj),
        in_specs=[pl.BlockSpec((tr, tc), lambda i, j: (i, j)), pl.BlockSpec((HALO, tc), prev(0)),
                  pl.BlockSpec((tr, tc), lambda i, j: (i, j + nj)), pl.BlockSpec((HALO, tc), prev(nj)),
                  pl.BlockSpec((FFN_CONV, tc), lambda i, j: (0, j)),
                  pl.BlockSpec((FFN_CONV, tc), lambda i, j: (0, j + nj))],
        out_specs=pl.BlockSpec((tr, tc), lambda i, j: (i, j)),
        compiler_params=_params(("parallel", "parallel")),
    )(upre, upre, upre, upre, cw, cw)


def _ffn_act_bwd(dact, upre, cw, name):
    t = upre.shape[0]
    tr, tc = _tile(t, 512), _tile(D_FF, 1408)
    nj = D_FF // tc

    def body(da_ref, g_ref, gp_ref, u_ref, up_ref, wg_ref, wu_ref, dg_ref, du_ref, dwg_ref, dwu_ref):
        i = pl.program_id(1)
        row = _iota((CHUNK_ROWS, 128), 0)

        @pl.when(i == 0)
        def _():
            dwg_ref[...] = jnp.zeros_like(dwg_ref)
            dwu_ref[...] = jnp.zeros_like(dwu_ref)

        for c0 in range(0, tc, 128):
            sl = slice(c0, c0 + 128)
            wg, wu = wg_ref[:, sl], wu_ref[:, sl]
            dwg = [jnp.zeros((8, 128), F32)] * FFN_CONV
            dwu = [jnp.zeros((8, 128), F32)] * FFN_CONV
            for r0 in range(0, tr, CHUNK_ROWS):
                rows = slice(r0, r0 + CHUNK_ROWS)
                if r0 == 0:
                    gp = jnp.where(i == 0, 0.0, gp_ref[:, sl].astype(F32))
                    up = jnp.where(i == 0, 0.0, up_ref[:, sl].astype(F32))
                else:
                    gp = g_ref[r0 - HALO:r0, sl].astype(F32)
                    up = u_ref[r0 - HALO:r0, sl].astype(F32)
                gt, gc = _conv_taps(g_ref[rows, sl].astype(F32), gp, wg, FFN_CONV, row)
                ut, uc = _conv_taps(u_ref[rows, sl].astype(F32), up, wu, FFN_CONV, row)
                da = da_ref[rows, sl].astype(F32)
                sg = _sigmoid(gc)
                dgc = da * uc * (sg * (1.0 + gc * (1.0 - sg)))
                duc = da * (gc * sg)
                dg_ref[rows, sl] = dgc.astype(BF16)
                du_ref[rows, sl] = duc.astype(BF16)
                dwg = [dwg[k] + _fold8(dgc * gt[k]) for k in range(FFN_CONV)]
                dwu = [dwu[k] + _fold8(duc * ut[k]) for k in range(FFN_CONV)]
            for k in range(FFN_CONV):
                dwg_ref[k:k + 1, sl] += jnp.sum(dwg[k], axis=0, keepdims=True)
                dwu_ref[k:k + 1, sl] += jnp.sum(dwu[k], axis=0, keepdims=True)

    prev = lambda off: (lambda j, i: (jnp.maximum(i * (tr // HALO) - 1, 0), j + off))
    blk = lambda off: pl.BlockSpec((tr, tc), lambda j, i: (i, j + off))
    wblk = lambda off: pl.BlockSpec((FFN_CONV, tc), lambda j, i: (0, j + off))
    dgc, duc, dwg, dwu = pl.pallas_call(
        body, name=name,
        out_shape=(jax.ShapeDtypeStruct((t, D_FF), BF16), jax.ShapeDtypeStruct((t, D_FF), BF16),
                   jax.ShapeDtypeStruct((FFN_CONV, D_FF), F32), jax.ShapeDtypeStruct((FFN_CONV, D_FF), F32)),
        grid=(nj, t // tr),
        in_specs=[blk(0), blk(0), pl.BlockSpec((HALO, tc), prev(0)), blk(nj), pl.BlockSpec((HALO, tc), prev(nj)),
                  wblk(0), wblk(nj)],
        out_specs=(blk(0), blk(0), wblk(0), wblk(0)),
        compiler_params=_params(("parallel", "arbitrary")),
    )(dact, upre, upre, upre, upre, cw, cw)
    return dgc, duc, dwg, dwu


def _dn_pre_fwd(qkv_pre, cw, name):
    t = qkv_pre.shape[0]
    tr = _tile(t, 512)
    scale = HEAD_DIM ** -0.5

    def body(x_ref, p_ref, w_ref, o_ref):
        i, j = pl.program_id(0), pl.program_id(1)
        row = _iota((tr, HEAD_DIM), 0)
        for h in range(HEADS):
            sl = slice(h * HEAD_DIM, (h + 1) * HEAD_DIM)
            prev = jnp.where(i == 0, 0.0, p_ref[:, sl].astype(F32))
            _, c = _conv_taps(x_ref[:, sl].astype(F32), prev, w_ref[:, sl], DN_CONV, row)
            s = c * _sigmoid(c)
            r = lax.rsqrt(jnp.sum(s * s, axis=1, keepdims=True) + EPS)
            o_ref[:, sl] = s * jnp.where(j == 0, r * scale, jnp.where(j == 1, r, 1.0))

    return pl.pallas_call(
        body, name=name,
        out_shape=jax.ShapeDtypeStruct((t, 3 * WIDTH), F32),
        grid=(t // tr, 3),
        in_specs=[pl.BlockSpec((tr, WIDTH), lambda i, j: (i, j)),
                  pl.BlockSpec((HALO, WIDTH), lambda i, j: (jnp.maximum(i * (tr // HALO) - 1, 0), j)),
                  pl.BlockSpec((DN_CONV, WIDTH), lambda i, j: (0, j))],
        out_specs=pl.BlockSpec((tr, WIDTH), lambda i, j: (i, j)),
        compiler_params=_params(("parallel", "parallel")),
    )(qkv_pre, qkv_pre, cw)


def _dn_pre_bwd(dq, dk, dv, qkv_pre, cw, name):
    t = qkv_pre.shape[0]
    tr = _tile(t, 512)
    scale = HEAD_DIM ** -0.5

    def body(dq_ref, dk_ref, dv_ref, x_ref, p_ref, w_ref, dc_ref, dw_ref):
        j, i = pl.program_id(0), pl.program_id(1)
        row = _iota((CHUNK_ROWS, HEAD_DIM), 0)

        @pl.when(i == 0)
        def _():
            dw_ref[...] = jnp.zeros_like(dw_ref)

        for h in range(HEADS):
            sl = slice(h * HEAD_DIM, (h + 1) * HEAD_DIM)
            wv = w_ref[:, sl]
            dw = [jnp.zeros((8, HEAD_DIM), F32)] * DN_CONV
            for r0 in range(0, tr, CHUNK_ROWS):
                rows = slice(r0, r0 + CHUNK_ROWS)
                if r0 == 0:
                    prev = jnp.where(i == 0, 0.0, p_ref[:, sl].astype(F32))
                else:
                    prev = x_ref[r0 - HALO:r0, sl].astype(F32)
                taps, c = _conv_taps(x_ref[rows, sl].astype(F32), prev, wv, DN_CONV, row)
                d = jnp.where(j == 0, dq_ref[rows, sl] * scale, jnp.where(j == 1, dk_ref[rows, sl], dv_ref[rows, sl]))
                sg = _sigmoid(c)
                s = c * sg
                r = lax.rsqrt(jnp.sum(s * s, axis=1, keepdims=True) + EPS)
                nh = s * r
                ds_norm = r * (d - nh * jnp.sum(nh * d, axis=1, keepdims=True))
                dc = jnp.where(j < 2, ds_norm, d) * (sg * (1.0 + c * (1.0 - sg)))
                dc_ref[rows, sl] = dc.astype(BF16)
                dw = [dw[k] + _fold8(dc * taps[k]) for k in range(DN_CONV)]
            for k in range(DN_CONV):
                dw_ref[k:k + 1, sl] += jnp.sum(dw[k], axis=0, keepdims=True)

    dspec = lambda p: pl.BlockSpec((tr, WIDTH), lambda j, i: (jnp.where(j == p, i, 0), 0))
    return pl.pallas_call(
        body, name=name,
        out_shape=(jax.ShapeDtypeStruct((t, 3 * WIDTH), BF16), jax.ShapeDtypeStruct((DN_CONV, 3 * WIDTH), F32)),
        grid=(3, t // tr),
        in_specs=[dspec(0), dspec(1), dspec(2),
                  pl.BlockSpec((tr, WIDTH), lambda j, i: (i, j)),
                  pl.BlockSpec((HALO, WIDTH), lambda j, i: (jnp.maximum(i * (tr // HALO) - 1, 0), j)),
                  pl.BlockSpec((DN_CONV, WIDTH), lambda j, i: (0, j))],
        out_specs=(pl.BlockSpec((tr, WIDTH), lambda j, i: (i, j)),
                   pl.BlockSpec((DN_CONV, WIDTH), lambda j, i: (0, j))),
        compiler_params=_params(("parallel", "arbitrary")),
    )(dq, dk, dv, qkv_pre, qkv_pre, cw)


def _tri(n, kind):
    r, c = _iota((n, n), 0), _iota((n, n), 1)
    m = {"lower": r >= c, "strict": r > c, "upper": r <= c}[kind]
    return m


GATE_ROWS = 4 * DN_CHUNK


def _chunk_tri(kind):
    r, c = _iota((GATE_ROWS, GATE_ROWS), 0), _iota((GATE_ROWS, GATE_ROWS), 1)
    same = (r // DN_CHUNK) == (c // DN_CHUNK)
    return jnp.where(jnp.logical_and(same, _tri(GATE_ROWS, kind)), 1.0, 0.0).astype(BF16)


def _dn_gates_fwd(hab, alog, dtb, name):
    t = hab.shape[0]
    cc = GATE_ROWS

    def body(h_ref, al_ref, dt_ref, o_ref):
        hv = h_ref[...]
        lane = _iota(hv.shape, 1)
        xa = hv + dt_ref[...]
        sp = jnp.maximum(xa, 0.0) + _log1pexp_neg_abs(xa)
        g = jnp.where(lane < HEADS, -jnp.exp(al_ref[...]) * sp, 0.0)
        gc = _dot_xl(_chunk_tri("lower"), g, NN)
        o_ref[...] = jnp.where(lane < HEADS, gc, jnp.where(lane < 2 * HEADS, _sigmoid(hv), 0.0))

    return pl.pallas_call(
        body, name=name,
        out_shape=jax.ShapeDtypeStruct((t, 128), F32),
        grid=(t // cc,),
        in_specs=[pl.BlockSpec((cc, 128), lambda i: (i, 0)), pl.BlockSpec((1, 128), lambda i: (0, 0)),
                  pl.BlockSpec((1, 128), lambda i: (0, 0))],
        out_specs=pl.BlockSpec((cc, 128), lambda i: (i, 0)),
        compiler_params=_params(("parallel",)),
    )(hab, alog, dtb)


def _dn_gates_bwd(dgates, hab, alog, dtb, name):
    t = hab.shape[0]
    cc = GATE_ROWS

    def body(d_ref, h_ref, al_ref, dt_ref, o_ref, dal_ref, ddt_ref):
        i = pl.program_id(0)
        hv = h_ref[...]
        dv = d_ref[...]
        lane = _iota(hv.shape, 1)
        dg = _dot_xl(_chunk_tri("upper"), jnp.where(lane < HEADS, dv, 0.0), NN)
        xa = hv + dt_ref[...]
        sp = jnp.maximum(xa, 0.0) + _log1pexp_neg_abs(xa)
        ea = jnp.exp(al_ref[...])
        da = jnp.where(lane < HEADS, dg * (-ea) * _sigmoid(xa), 0.0)
        be = _sigmoid(hv)
        db = dv * be * (1.0 - be)
        o_ref[...] = jnp.where(lane < HEADS, da, jnp.where(lane < 2 * HEADS, db, 0.0))

        @pl.when(i == 0)
        def _():
            dal_ref[...] = jnp.zeros_like(dal_ref)
            ddt_ref[...] = jnp.zeros_like(ddt_ref)

        dal_ref[...] += jnp.sum(jnp.where(lane < HEADS, dg * (-ea) * sp, 0.0), axis=0, keepdims=True)
        ddt_ref[...] += jnp.sum(da, axis=0, keepdims=True)

    return pl.pallas_call(
        body, name=name,
        out_shape=(jax.ShapeDtypeStruct((t, 128), F32), jax.ShapeDtypeStruct((1, 128), F32),
                   jax.ShapeDtypeStruct((1, 128), F32)),
        grid=(t // cc,),
        in_specs=[pl.BlockSpec((cc, 128), lambda i: (i, 0)), pl.BlockSpec((cc, 128), lambda i: (i, 0)),
                  pl.BlockSpec((1, 128), lambda i: (0, 0)), pl.BlockSpec((1, 128), lambda i: (0, 0))],
        out_specs=(pl.BlockSpec((cc, 128), lambda i: (i, 0)), pl.BlockSpec((1, 128), lambda i: (0, 0)),
                   pl.BlockSpec((1, 128), lambda i: (0, 0))),
        compiler_params=_params(("arbitrary",)),
    )(dgates, hab, alog, dtb)


def _dn_chunk_common(gates, h):
    cc = DN_CHUNK
    lane = _iota(gates.shape, 1)
    gh = jnp.where(lane == h, gates, 0.0)
    gc_col = jnp.sum(gh, axis=1, keepdims=True)
    gc_row = _dot_xl(jnp.ones((cc, 128), BF16), gh, NT)
    beta = jnp.sum(jnp.where(lane == h + HEADS, gates, 0.0), axis=1, keepdims=True)
    lower = _tri(cc, "lower")
    decay = jnp.where(lower, jnp.exp(jnp.where(lower, gc_col - gc_row, 0.0)), 0.0)
    gc_last = gc_col[cc - 1:cc, :]
    return gc_col, gc_last, beta, decay


def _dn_local_fwd(act, gates, name):
    t = act.shape[0]
    cc = DN_CHUNK
    nc = t // cc

    per = LOCAL_CHUNKS

    def body(q_ref, k_ref, v_ref, g_ref, u_ref, w_ref, kd_ref, qg_ref, ti_ref, p_ref):
        eye = jnp.where(_iota((cc, cc), 0) == _iota((cc, cc), 1), 1.0, 0.0)
        units = [(c, h) for c in range(per) for h in range(HEADS)]
        us = range(len(units))
        rows = [slice(c * cc, (c + 1) * cc) for c, _ in units]
        sl = [slice(h * HEAD_DIM, (h + 1) * HEAD_DIM) for _, h in units]
        gates = [g_ref[c * cc:(c + 1) * cc, :] for c in range(per)]
        q, k, v = ([r[rows[i], sl[i]] for i in us] for r in (q_ref, k_ref, v_ref))
        gc_col, gc_last, beta, decay = zip(*[_dn_chunk_common(gates[c], h) for c, h in units])
        gam = [jnp.exp(g) for g in gc_col]
        kb = [k[i] * beta[i] for i in us]
        npow = [-jnp.where(_tri(cc, "strict"), _dotb(kb[i], k[i], NT) * decay[i], 0.0) for i in us]
        tinv = [eye + n for n in npow]
        for _ in range(5):
            npow = [_dot3(n, n, NN) for n in npow]
            tinv = [t + _dot3(t, n, NN) for t, n in zip(tinv, npow)]
        uu = [_dot3(tinv[i], v[i] * beta[i], NN) for i in us]
        ww = [_dot3(tinv[i], kb[i] * gam[i], NN) for i in us]
        pp = [jnp.where(_tri(cc, "lower"), _dotb(q[i], k[i], NT) * decay[i], 0.0) for i in us]
        for i, (_, h) in enumerate(units):
            u_ref[rows[i], sl[i]] = uu[i]
            w_ref[rows[i], sl[i]] = ww[i].astype(BF16)
            kd_ref[rows[i], sl[i]] = (k[i] * jnp.exp(gc_last[i] - gc_col[i])).astype(BF16)
            qg_ref[rows[i], sl[i]] = (q[i] * gam[i]).astype(BF16)
            ti_ref[h, rows[i], :] = tinv[i]
            p_ref[h, rows[i], :] = pp[i].astype(BF16)

    row = lambda off: pl.BlockSpec((per * cc, WIDTH), lambda n: (n, off))
    mat = pl.BlockSpec((HEADS, per * cc, cc), lambda n: (0, n, 0))
    tw, tb = jax.ShapeDtypeStruct((t, WIDTH), F32), jax.ShapeDtypeStruct((t, WIDTH), BF16)
    hm, hb = jax.ShapeDtypeStruct((HEADS, t, cc), F32), jax.ShapeDtypeStruct((HEADS, t, cc), BF16)
    return pl.pallas_call(
        body, name=name,
        out_shape=(tw, tb, tb, tb, hm, hb),
        grid=(nc // per,),
        in_specs=[row(0), row(1), row(2), pl.BlockSpec((per * cc, 128), lambda n: (n, 0))],
        out_specs=(row(0), row(0), row(0), row(0), mat, mat),
        compiler_params=_params(("parallel",)),
    )(act, act, act, gates)


def _dn_scan_fwd(u, w, kd, qg, p, gates, name):
    t = u.shape[0]
    cc = DN_CHUNK
    nc = t // cc
    per = SCAN_CHUNKS

    def body(u_ref, w_ref, kd_ref, qg_ref, p_ref, g_ref, o_ref, sh_ref, s_ref):
        n = pl.program_id(0)

        @pl.when(n == 0)
        def _():
            s_ref[...] = jnp.zeros_like(s_ref)

        hs = range(HEADS)
        sl = [slice(h * HEAD_DIM, (h + 1) * HEAD_DIM) for h in hs]
        s = [s_ref[h] for h in hs]
        for c in range(per):
            r = slice(c * cc, (c + 1) * cc)
            glast = jnp.exp(g_ref[(c + 1) * cc - 1:(c + 1) * cc, :])
            sb = [a.astype(BF16) for a in s]
            vn = [u_ref[r, sl[h]] - _dot(w_ref[r, sl[h]].astype(BF16), sb[h], NN) for h in hs]
            vnb = [a.astype(BF16) for a in vn]
            o_state = [_dot(qg_ref[r, sl[h]].astype(BF16), sb[h], NN) for h in hs]
            o_local = [_dot(p_ref[h, r, :].astype(BF16), vnb[h], NN) for h in hs]
            s_add = [_dot(kd_ref[r, sl[h]].astype(BF16), vnb[h], TN) for h in hs]
            for h in hs:
                o_ref[r, sl[h]] = o_state[h] + o_local[h]
                sh_ref[c, h] = sb[h]
            s = [glast[:, h:h + 1] * s[h] + s_add[h] for h in hs]
        for h in hs:
            s_ref[h] = s[h]

    row = pl.BlockSpec((per * cc, WIDTH), lambda n: (n, 0))
    return pl.pallas_call(
        body, name=name,
        out_shape=(jax.ShapeDtypeStruct((t, WIDTH), F32),
                   jax.ShapeDtypeStruct((nc, HEADS, HEAD_DIM, HEAD_DIM), BF16)),
        grid=(nc // per,),
        in_specs=[row, row, row, row, pl.BlockSpec((HEADS, per * cc, cc), lambda n: (0, n, 0)),
                  pl.BlockSpec((per * cc, 128), lambda n: (n, 0))],
        out_specs=(row, pl.BlockSpec((per, HEADS, HEAD_DIM, HEAD_DIM), lambda n: (n, 0, 0, 0))),
        scratch_shapes=[pltpu.VMEM((HEADS, HEAD_DIM, HEAD_DIM), F32)],
        compiler_params=_params(("arbitrary",)),
    )(u, w, kd, qg, p, gates)


def _dn_scan_bwd(do, w, kd, qg, p, gates, name):
    t = do.shape[0]
    cc = DN_CHUNK
    nc = t // cc
    per = SCAN_CHUNKS
    nb = nc // per

    def body(do_ref, w_ref, kd_ref, qg_ref, p_ref, g_ref, dvn_ref, dsh_ref, ds_ref):
        n = pl.program_id(0)

        @pl.when(n == 0)
        def _():
            ds_ref[...] = jnp.zeros_like(ds_ref)

        hs = range(HEADS)
        sl = [slice(h * HEAD_DIM, (h + 1) * HEAD_DIM) for h in hs]
        ds = [ds_ref[h] for h in hs]
        for c in reversed(range(per)):
            r = slice(c * cc, (c + 1) * cc)
            glast = jnp.exp(g_ref[(c + 1) * cc - 1:(c + 1) * cc, :])
            dob = [do_ref[r, sl[h]].astype(BF16) for h in hs]
            dvn = [_dot(p_ref[h, r, :].astype(BF16), dob[h], TN)
                   + _dot(kd_ref[r, sl[h]].astype(BF16), ds[h].astype(BF16), NN) for h in hs]
            ds_q = [_dot(qg_ref[r, sl[h]].astype(BF16), dob[h], TN) for h in hs]
            ds_w = [_dot(w_ref[r, sl[h]].astype(BF16), dvn[h].astype(BF16), TN) for h in hs]
            for h in hs:
                dvn_ref[r, sl[h]] = dvn[h]
                dsh_ref[c, h] = ds[h].astype(BF16)
            ds = [ds_q[h] + glast[:, h:h + 1] * ds[h] - ds_w[h] for h in hs]
        for h in hs:
            ds_ref[h] = ds[h]

    row = pl.BlockSpec((per * cc, WIDTH), lambda n: (nb - 1 - n, 0))
    return pl.pallas_call(
        body, name=name,
        out_shape=(jax.ShapeDtypeStruct((t, WIDTH), F32),
                   jax.ShapeDtypeStruct((nc, HEADS, HEAD_DIM, HEAD_DIM), BF16)),
        grid=(nb,),
        in_specs=[row, row, row, row, pl.BlockSpec((HEADS, per * cc, cc), lambda n: (0, nb - 1 - n, 0)),
                  pl.BlockSpec((per * cc, 128), lambda n: (nb - 1 - n, 0))],
        out_specs=(row, pl.BlockSpec((per, HEADS, HEAD_DIM, HEAD_DIM), lambda n: (nb - 1 - n, 0, 0, 0))),
        scratch_shapes=[pltpu.VMEM((HEADS, HEAD_DIM, HEAD_DIM), F32)],
        compiler_params=_params(("arbitrary",)),
    )(do, w, kd, qg, p, gates)


def _dn_local_bwd(act, gates, u, w, kd, qg, tinv, p, sh, dsh, dvn, do, name):
    t = act.shape[0]
    cc = DN_CHUNK
    nc = t // cc
    per = LOCAL_CHUNKS

    def body(q_ref, k_ref, v_ref, g_ref, u_ref, w_ref, kd_ref, qg_ref, ti_ref, p_ref, s_ref, ds_ref,
             dvn_ref, do_ref, dq_ref, dk_ref, dv_ref, dg_ref):
        lower, strict = _tri(cc, "lower"), _tri(cc, "strict")
        ones = jnp.ones((cc, 128), BF16)
        rowc = _iota((cc, 1), 0)
        lane = _iota((cc, 128), 1)
        units = [(c, h) for c in range(per) for h in range(HEADS)]
        hs = range(len(units))
        rows = [slice(c * cc, (c + 1) * cc) for c, _ in units]
        sl = [slice(h * HEAD_DIM, (h + 1) * HEAD_DIM) for _, h in units]
        gates_v = [g_ref[c * cc:(c + 1) * cc, :] for c in range(per)]
        q, k, v, uu, ww, kd, qg, dvn, do = ([r[rows[i], sl[i]] for i in hs] for r in (
            q_ref, k_ref, v_ref, u_ref, w_ref, kd_ref, qg_ref, dvn_ref, do_ref))
        gc_col, gc_last, beta, decay = zip(*[_dn_chunk_common(gates_v[c], h) for c, h in units])
        gam = [jnp.exp(g) for g in gc_col]
        kb = [k[h] * beta[h] for h in hs]
        s_in = [s_ref[c, h] for c, h in units]
        ds_out = [ds_ref[c, h] for c, h in units]
        tinv = [ti_ref[h, rows[i], :] for i, (_, h) in enumerate(units)]
        pmat = [p_ref[h, rows[i], :] for i, (_, h) in enumerate(units)]

        a = [jnp.where(strict, _dotb(kb[h], k[h], NT) * decay[h], 0.0) for h in hs]
        vn = [uu[h] - _dotb(ww[h], s_in[h], NN) for h in hs]
        dqg = [_dotb(do[h], s_in[h], NT) for h in hs]
        dw = [-_dotb(dvn[h], s_in[h], NT) for h in hs]
        dp = [jnp.where(lower, _dotb(do[h], vn[h], NT), 0.0) for h in hs]
        dkd = [_dotb(vn[h], ds_out[h], NT) for h in hs]
        dru = [_dot3(tinv[h], dvn[h], TN) for h in hs]
        drw = [_dot3(tinv[h], dw[h], TN) for h in hs]
        da = [-jnp.where(strict, _dotb(dru[h], uu[h], NT) + _dotb(drw[h], ww[h], NT), 0.0) for h in hs]
        dad = [da[h] * decay[h] for h in hs]
        dpd = [dp[h] * decay[h] for h in hs]
        dkb = [_dotb(dad[h], k[h], NN) + gam[h] * drw[h] for h in hs]
        dk = [_dotb(dad[h], kb[h], TN) + _dotb(dpd[h], q[h], TN) + beta[h] * dkb[h]
              + jnp.exp(gc_last[h] - gc_col[h]) * dkd[h] for h in hs]
        dq = [gam[h] * dqg[h] + _dotb(dpd[h], k[h], NN) for h in hs]
        gm = [da[h] * a[h] + dp[h] * pmat[h] for h in hs]
        colsum = [_dot_xr(gm[h], ones, TN)[:, 0:1] for h in hs]

        dgates = [jnp.zeros((cc, 128), F32)] * per
        for h, (c, head) in enumerate(units):
            dk_ref[rows[h], sl[h]] = dk[h]
            dq_ref[rows[h], sl[h]] = dq[h]
            dv_ref[rows[h], sl[h]] = beta[h] * dru[h]
            dbeta = (jnp.sum(dkb[h] * k[h], axis=1, keepdims=True)
                     + jnp.sum(dru[h] * v[h], axis=1, keepdims=True))
            rkd = jnp.sum(dkd[h] * kd[h], axis=1, keepdims=True)
            dgc = (jnp.sum(gm[h], axis=1, keepdims=True) - colsum[h]
                   + jnp.sum(dqg[h] * qg[h], axis=1, keepdims=True)
                   + jnp.sum(drw[h] * kb[h], axis=1, keepdims=True) * gam[h] - rkd)
            tail = jnp.sum(rkd, axis=0, keepdims=True) + jnp.exp(gc_last[h]) * jnp.sum(
                jnp.sum(s_in[h].astype(F32) * ds_out[h].astype(F32), axis=1, keepdims=True), axis=0, keepdims=True)
            dgc = dgc + jnp.where(rowc == cc - 1, tail, 0.0)
            dgates[c] = dgates[c] + jnp.where(lane == head, dgc, 0.0) + jnp.where(lane == head + HEADS, dbeta, 0.0)
        for c in range(per):
            dg_ref[c * cc:(c + 1) * cc, :] = dgates[c]

    row = lambda off: pl.BlockSpec((per * cc, WIDTH), lambda n: (n, off))
    mat = pl.BlockSpec((HEADS, per * cc, cc), lambda n: (0, n, 0))
    st = pl.BlockSpec((per, HEADS, HEAD_DIM, HEAD_DIM), lambda n: (n, 0, 0, 0))
    gl = pl.BlockSpec((per * cc, 128), lambda n: (n, 0))
    tw = jax.ShapeDtypeStruct((t, WIDTH), F32)
    return pl.pallas_call(
        body, name=name,
        out_shape=(tw, tw, tw, jax.ShapeDtypeStruct((t, 128), F32)),
        grid=(nc // per,),
        in_specs=[row(0), row(1), row(2), gl, row(0), row(0), row(0), row(0), mat, mat, st, st, row(0), row(0)],
        out_specs=(row(0), row(0), row(0), gl),
        compiler_params=_params(("parallel",)),
    )(act, act, act, gates, u, w, kd, qg, tinv, p, sh, dsh, dvn, do)


def _dn_post_fwd(o, gate, w, name):
    t = o.shape[0]
    tr = _tile(t, 512)

    def body(o_ref, g_ref, w_ref, y_ref):
        for h in range(HEADS):
            sl = slice(h * HEAD_DIM, (h + 1) * HEAD_DIM)
            ov, gv = o_ref[:, sl], g_ref[:, sl].astype(F32)
            r = lax.rsqrt(jnp.mean(ov * ov, axis=1, keepdims=True) + EPS)
            y_ref[:, sl] = (ov * r * w_ref[...] * (gv * _sigmoid(gv))).astype(BF16)

    blk = pl.BlockSpec((tr, WIDTH), lambda i: (i, 0))
    return pl.pallas_call(
        body, name=name,
        out_shape=jax.ShapeDtypeStruct((t, WIDTH), BF16),
        grid=(t // tr,),
        in_specs=[blk, blk, pl.BlockSpec((1, HEAD_DIM), lambda i: (0, 0))],
        out_specs=blk,
        compiler_params=_params(("parallel",)),
    )(o, gate, w)


def _dn_post_bwd(dy, o, gate, w, name):
    t = o.shape[0]
    tr = _tile(t, 512)

    def body(dy_ref, o_ref, g_ref, w_ref, do_ref, dg_ref, dw_ref):
        i = pl.program_id(0)

        @pl.when(i == 0)
        def _():
            dw_ref[...] = jnp.zeros_like(dw_ref)

        dw = jnp.zeros((1, HEAD_DIM), F32)
        for h in range(HEADS):
            sl = slice(h * HEAD_DIM, (h + 1) * HEAD_DIM)
            ov, gv, dyv = o_ref[:, sl], g_ref[:, sl].astype(F32), dy_ref[:, sl].astype(F32)
            r = lax.rsqrt(jnp.mean(ov * ov, axis=1, keepdims=True) + EPS)
            oh = ov * r
            sg = _sigmoid(gv)
            dg_ref[:, sl] = (dyv * oh * w_ref[...] * (sg * (1.0 + gv * (1.0 - sg)))).astype(BF16)
            dn = dyv * (gv * sg)
            doh = dn * w_ref[...]
            do_ref[:, sl] = r * (doh - oh * jnp.mean(doh * oh, axis=1, keepdims=True))
            dw = dw + jnp.sum(dn * oh, axis=0, keepdims=True)
        dw_ref[...] += dw

    blk = pl.BlockSpec((tr, WIDTH), lambda i: (i, 0))
    return pl.pallas_call(
        body, name=name,
        out_shape=(jax.ShapeDtypeStruct((t, WIDTH), F32), jax.ShapeDtypeStruct((t, WIDTH), BF16),
                   jax.ShapeDtypeStruct((1, HEAD_DIM), F32)),
        grid=(t // tr,),
        in_specs=[blk, blk, blk, pl.BlockSpec((1, HEAD_DIM), lambda i: (0, 0))],
        out_specs=(blk, blk, pl.BlockSpec((1, HEAD_DIM), lambda i: (0, 0))),
        compiler_params=_params(("arbitrary",)),
    )(dy, o, gate, w)


def _sb_scores(qs, k_ref, qi, it, carries, uincl):
    bk = ATT_BLOCK
    scale = HEAD_DIM ** -0.5
    heads, groups = range(len(qs)), range(SB_GROUP)
    lane = [slice(e * HEAD_DIM, (e + 1) * HEAD_DIM) for e in heads]
    js = [qi - SB_GROUP * it - g for g in groups]
    rows = [pl.ds(pl.multiple_of(jnp.maximum(j, 0) * bk, bk), bk) for j in js]
    qpos = qi * bk + _iota((bk, bk), 0)
    col = _iota((bk, bk), 1)
    mask1 = [jnp.logical_and(j * bk + col < qpos, j >= 0) for j in js]
    ks = [[k_ref[r, lane[e]] for r in rows] for e in heads]
    z = [[_dot(qs[e], k, NT) * scale for k in ks[e]] for e in heads]
    soft = [[_log1pexp_neg_abs(a) for a in ze] for ze in z]
    lk_full = [[-(jnp.maximum(a, 0.0) + s) for a, s in zip(z[e], soft[e])] for e in heads]
    lk = [[jnp.where(m, a, 0.0) for m, a in zip(mask1, lk_full[e])] for e in heads]
    ls = [[jnp.minimum(a, 0.0) - s for a, s in zip(z[e], soft[e])] for e in heads]
    incl = [[_dot_xr2(a, uincl, NN) for a in lk[e]] for e in heads]
    weights, out_carries = [], []
    for e in heads:
        cb, we = carries[e], []
        for g in groups:
            we.append(jnp.where(mask1[g], jnp.exp(ls[e][g] + (cb + incl[e][g] - lk[e][g])), 0.0))
            cb = cb + incl[e][g][:, 0:1]
        weights.append(we)
        out_carries.append(cb)
    return rows, ks, weights, mask1, lk_full, ls, out_carries


def _sb_more(qi, carry):
    it, cbs = carry[0], carry[1]
    live = jnp.max(cbs[0])
    for cb in cbs[1:]:
        live = jnp.maximum(live, jnp.max(cb))
    return jnp.logical_and(SB_GROUP * it <= qi, live > SB_LOG_ZERO)


def _sb_steps(groups, nq):
    def when():
        h, i = pl.program_id(0), pl.program_id(1)
        return (jnp.logical_and(h == 0, i == 0), jnp.logical_and(h == groups // 2, i == 0),
                jnp.logical_and(h == groups - 1, i == nq - 1))
    return when


def _sb_fwd(qkv, name, comm=None):
    t = qkv.shape[0]
    bk = ATT_BLOCK
    hp, wide = SB_HEADS_FWD, SB_HEADS_FWD * HEAD_DIM
    lane = [slice(e * HEAD_DIM, (e + 1) * HEAD_DIM) for e in range(hp)]

    def body(q_ref, k_ref, v_ref, o_ref):
        qi = pl.program_id(1)
        qs = [q_ref[:, s] for s in lane]
        uincl = jnp.where(_tri(bk, "lower"), 1.0, 0.0).astype(BF16)

        def step(carry):
            it, cbs, accs = carry
            rows, _, weights, _, _, _, cbs = _sb_scores(qs, k_ref, qi, it, cbs, uincl)
            accs = list(accs)
            for e in range(hp):
                for r, a in zip(rows, weights[e]):
                    accs[e] = accs[e] + _dot(a.astype(BF16), v_ref[r, lane[e]], NN)
            return it + 1, tuple(cbs), tuple(accs)

        init = (jnp.int32(0), (jnp.zeros((bk, 1), F32),) * hp, (jnp.zeros((bk, HEAD_DIM), F32),) * hp)
        _, _, accs = lax.while_loop(functools.partial(_sb_more, qi), step, init)
        for e in range(hp):
            o_ref[:, lane[e]] = accs[e]

    groups = HEADS // hp
    (o,), extra = _host_call(
        body, name, comm, _sb_steps(groups, t // bk), [jax.ShapeDtypeStruct((t, WIDTH), F32)], (groups, t // bk),
        [pl.BlockSpec((bk, wide), lambda h, i: (i, h)),
         pl.BlockSpec((t, wide), lambda h, i: (0, groups + h)),
         pl.BlockSpec((t, wide), lambda h, i: (0, 2 * groups + h))],
        [pl.BlockSpec((bk, wide), lambda h, i: (i, h))], [], ("parallel", "arbitrary"), (qkv, qkv, qkv))
    return o, extra


def _sb_bwd(qkv, o, do, name, comm=None):
    assert do.dtype == BF16
    t = qkv.shape[0]
    bk = ATT_BLOCK
    scale = HEAD_DIM ** -0.5
    hp, wide = SB_HEADS_BWD, SB_HEADS_BWD * HEAD_DIM
    lane = [slice(e * HEAD_DIM, (e + 1) * HEAD_DIM) for e in range(hp)]

    def body(q_ref, k_ref, v_ref, o_ref, do_ref, dq_ref, dk_out, dv_out, dk_ref, dv_ref):
        qi = pl.program_id(1)

        @pl.when(qi == 0)
        def _():
            dk_ref[...] = jnp.zeros_like(dk_ref)
            dv_ref[...] = jnp.zeros_like(dv_ref)

        heads, groups = range(hp), range(SB_GROUP)
        qs = [q_ref[:, s] for s in lane]
        dob = [do_ref[:, s] for s in lane]
        dsum = [jnp.sum(dob[e].astype(F32) * o_ref[:, lane[e]], axis=1, keepdims=True) for e in heads]
        uincl = jnp.where(_tri(bk, "lower"), 1.0, 0.0).astype(BF16)

        def step(carry):
            it, cbs, ces, dqs = carry
            rows, ks, weights, mask, lk_full, ls, cbs = _sb_scores(qs, k_ref, qi, it, cbs, uincl)
            ab = [[a.astype(BF16) for a in weights[e]] for e in heads]
            vs = [[v_ref[r, lane[e]] for r in rows] for e in heads]
            dla = [[ab[e][g].astype(F32) * _dot(dob[e], vs[e][g], NT) for g in groups] for e in heads]
            suf = [[_dot_xr2(a, uincl, NN) for a in dla[e]] for e in heads]
            ces, dqs = list(ces), list(dqs)
            for e in heads:
                for g in groups:
                    err = dsum[e] - (ces[e] + suf[e][g])
                    ces[e] = ces[e] + suf[e][g][:, 0:1]
                    dz = jnp.where(mask[g], dla[e][g] * jnp.exp(lk_full[e][g]) - err * jnp.exp(ls[e][g]), 0.0)
                    dzb = (dz * scale).astype(BF16)
                    dqs[e] = dqs[e] + _dot(dzb, ks[e][g], NN)
                    dk_ref[rows[g], lane[e]] += _dot(dzb, qs[e], TN)
                    dv_ref[rows[g], lane[e]] += _dot(ab[e][g], dob[e], TN)
            return it + 1, tuple(cbs), tuple(ces), tuple(dqs)

        zc = (jnp.zeros((bk, 1), F32),) * hp
        init = (jnp.int32(0), zc, zc, (jnp.zeros((bk, HEAD_DIM), F32),) * hp)
        dqs = lax.while_loop(functools.partial(_sb_more, qi), step, init)[3]
        for e in heads:
            dq_ref[:, lane[e]] = dqs[e].astype(BF16)

        @pl.when(qi == t // bk - 1)
        def _():
            dk_out[...] = dk_ref[...].astype(BF16)
            dv_out[...] = dv_ref[...].astype(BF16)

    ngroup = HEADS // hp
    tw = jax.ShapeDtypeStruct((t, WIDTH), BF16)
    qb = pl.BlockSpec((bk, wide), lambda h, i: (i, h))
    full = lambda off: pl.BlockSpec((t, wide), lambda h, i: (0, off + h))
    return _host_call(
        body, name, comm, _sb_steps(ngroup, t // bk), [tw, tw, tw], (ngroup, t // bk),
        [qb, full(ngroup), full(2 * ngroup), qb, qb], [qb, full(0), full(0)],
        [pltpu.VMEM((t, wide), F32), pltpu.VMEM((t, wide), F32)], ("parallel", "arbitrary"),
        (qkv, qkv, qkv, o, do))


def _merge_fwd(pd, ps, gl, name):
    t = pd.shape[0]
    tr, tc = _tile(t, 512), 512
    nj = D_MODEL // tc

    def body(pd_ref, ps_ref, gd_ref, gs_ref, o_ref):
        gd, gs = gd_ref[...].astype(F32), gs_ref[...].astype(F32)
        o_ref[...] = (_sigmoid(gd) * pd_ref[...].astype(F32) + _sigmoid(gs) * ps_ref[...].astype(F32)).astype(BF16)

    blk = lambda off: pl.BlockSpec((tr, tc), lambda i, j: (i, j + off))
    return pl.pallas_call(
        body, name=name,
        out_shape=jax.ShapeDtypeStruct((t, D_MODEL), BF16),
        grid=(t // tr, nj),
        in_specs=[blk(0), blk(0), blk(0), blk(nj)],
        out_specs=blk(0),
        compiler_params=_params(("parallel", "parallel")),
    )(pd, ps, gl, gl)


def _merge_bwd(dm, pd, ps, gl, name):
    t = pd.shape[0]
    tr, tc = _tile(t, 512), 512
    nj = D_MODEL // tc

    def body(dm_ref, pd_ref, ps_ref, gd_ref, gs_ref, dpd_ref, dps_ref, dgd_ref, dgs_ref):
        dmv = dm_ref[...].astype(F32)
        sd, ss = _sigmoid(gd_ref[...].astype(F32)), _sigmoid(gs_ref[...].astype(F32))
        dpd_ref[...] = (dmv * sd).astype(BF16)
        dps_ref[...] = (dmv * ss).astype(BF16)
        dgd_ref[...] = (dmv * pd_ref[...].astype(F32) * sd * (1.0 - sd)).astype(BF16)
        dgs_ref[...] = (dmv * ps_ref[...].astype(F32) * ss * (1.0 - ss)).astype(BF16)

    blk = lambda off: pl.BlockSpec((tr, tc), lambda i, j: (i, j + off))
    out = jax.ShapeDtypeStruct((t, D_MODEL), BF16)
    return pl.pallas_call(
        body, name=name,
        out_shape=(out, out, out, out),
        grid=(t // tr, nj),
        in_specs=[blk(0), blk(0), blk(0), blk(0), blk(nj)],
        out_specs=(blk(0), blk(0), blk(0), blk(0)),
        compiler_params=_params(("parallel", "parallel")),
    )(dm, pd, ps, gl, gl)


def _local_step(x, target, wts, plan=None, n1=None):
    if n1 is None:
        n1 = _rmsnorm_fwd(x, wts["norm1_w"], "norm1_fwd")
    qkv_pre = _matmul(n1, wts["w_dnqkv_t"], "nt", BF16, "in_dnqkv")
    hgate = _matmul(n1, wts["w_dngate_t"], "nt", BF16, "in_dngate")
    sbqkv = _matmul(n1, wts["w_sbqkv_t"], "nt", BF16, "in_sbqkv")
    gl = _matmul(n1, wts["w_gl_t"], "nt", BF16, "in_gl")
    hab = _matmul(n1, wts["w_ab_t"], "nt", F32, "in_ab")

    act = _dn_pre_fwd(qkv_pre, wts["dn_conv_w"], "dn_pre_fwd")
    gates = _dn_gates_fwd(hab, wts["alog"], wts["dtb"], "dn_gates_fwd")
    u, w, kd, qg, tinv, p = _dn_local_fwd(act, gates, "dn_local_fwd")
    o_dn, sh = _dn_scan_fwd(u, w, kd, qg, p, gates, "dn_scan_fwd")
    y_dn = _dn_post_fwd(o_dn, hgate, wts["dn_norm_w"], "dn_post_fwd")

    o_sb, late = _sb_fwd(sbqkv, "sb_fwd", comm=plan.late_gather() if plan else None)
    if plan:
        wts = {**wts, **plan.late_weights(late)}

    pd = _matmul(y_dn, wts["w_proj_dn"], "nn", BF16, "proj_dn")
    ps = _matmul(o_sb, wts["w_proj_sb"], "nn", BF16, "proj_sb")
    mixed = _merge_fwd(pd, ps, gl, "merge_fwd")
    x1 = _matmul(mixed, wts["w_out"], "nn", F32, "out_proj", add=x)

    n2 = _rmsnorm_fwd(x1, wts["norm2_w"], "norm2_fwd")
    upre = _matmul(n2, wts["ffn_w_up_t"], "nt", BF16, "ffn_up")
    fact = _ffn_act_fwd(upre, wts["ffn_conv_w"], "ffn_act_fwd")
    x2 = _matmul(fact, wts["ffn_w_down"], "nn", F32, "ffn_down", add=x1)

    dx2, g_normf, loss = _final_loss(x2, target, wts["norm_f_w"], "final_loss")

    dfact = _matmul(dx2, wts["ffn_w_down"], "nt", BF16, "ffn_down_dx")
    g_wdown = _matmul(fact, dx2, "tn", BF16, "ffn_down_dw")
    dgc, duc, dwg, dwu = _ffn_act_bwd(dfact, upre, wts["ffn_conv_w"], "ffn_act_bwd")
    g_fconv = jnp.concatenate([dwg, dwu], axis=1)
    dupre = _conv_bwd_data([dgc, duc], wts["ffn_conv_w"], FFN_CONV, BF16, "ffn_conv_bwd")
    dn2 = _matmul(dupre, wts["ffn_w_up_t"], "nn", F32, "ffn_up_dx")
    g_wup = _matmul(dupre, n2, "tn", BF16, "ffn_up_dw")
    dx1, g_norm2 = _rmsnorm_bwd(dn2, x1, wts["norm2_w"], dx2, "norm2_bwd")

    dmixed = _matmul(dx1, wts["w_out"], "nt", BF16, "out_proj_dx")
    g_wout = _matmul(mixed, dx1, "tn", BF16, "out_proj_dw")
    dpd, dps, dgd, dgs = _merge_bwd(dmixed, pd, ps, gl, "merge_bwd")
    dy_dn = _matmul(dpd, wts["w_proj_dn"], "nt", BF16, "proj_dn_dx")
    g_wpd = _matmul(y_dn, dpd, "tn", BF16, "proj_dn_dw")
    do_sb = _matmul(dps, wts["w_proj_sb"], "nt", BF16, "proj_sb_dx")
    g_wps = _matmul(o_sb, dps, "tn", BF16, "proj_sb_dw")
    grads = dict(w_proj_dn=g_wpd, w_proj_sb=g_wps, w_out=g_wout, ffn_w_up_t=g_wup, ffn_w_down=g_wdown)

    (dsq, dsk, dsv), got_early = _sb_bwd(sbqkv, o_sb, do_sb, "sb_bwd",
                                         comm=plan.early_grads(grads) if plan else None)

    do_dn, dhgate, g_dnnorm = _dn_post_bwd(dy_dn, o_dn, hgate, wts["dn_norm_w"], "dn_post_bwd")
    dvn, dsh = _dn_scan_bwd(do_dn, w, kd, qg, p, gates, "dn_scan_bwd")
    dq, dk, dv, dgates = _dn_local_bwd(act, gates, u, w, kd, qg, tinv, p, sh, dsh, dvn, do_dn, "dn_local_bwd")
    dhab, g_alog, g_dtb = _dn_gates_bwd(dgates, hab, wts["alog"], wts["dtb"], "dn_gates_bwd")
    dcv, g_dnconv = _dn_pre_bwd(dq, dk, dv, qkv_pre, wts["dn_conv_w"], "dn_pre_bwd")
    dqkv_pre = _conv_bwd_data([dcv], wts["dn_conv_w"], DN_CONV, BF16, "dn_conv_bwd")

    dh = jnp.concatenate([dqkv_pre, dhgate, dsq, dsk, dsv, dgd, dgs], axis=1)
    w_main_t = jnp.concatenate([wts["w_dnqkv_t"], wts["w_dngate_t"], wts["w_sbqkv_t"], wts["w_gl_t"]], axis=0)
    g_wmain = _matmul(dh, n1, "tn", BF16, "in_dw_main")
    g_wab = _matmul(dhab, n1, "tn", BF16, "in_dw_ab")
    grads.update(w_main_t=g_wmain, w_ab_t=g_wab, dn_conv_w=g_dnconv, alog=g_alog, dtb=g_dtb, dn_norm_w=g_dnnorm,
                 norm2_w=g_norm2, ffn_conv_w=g_fconv, norm_f_w=g_normf)
    got_late = []
    if plan:
        dn1, swapped = _matmul(dhab, wts["w_ab_t"], "nn", F32, "in_dx_ab", comm=plan.sibling_swap(grads))
        dn1, got_late = _matmul(dh, w_main_t, "nn", F32, "in_dx_main", add=dn1,
                                comm=plan.late_grads(swapped, grads, loss))
    else:
        dn1 = _matmul(dhab, wts["w_ab_t"], "nn", F32, "in_dx_ab")
        dn1 = _matmul(dh, w_main_t, "nn", F32, "in_dx_main", add=dn1)
    grad_x, g_norm1 = _rmsnorm_bwd(dn1, x, wts["norm1_w"], dx1, "norm1_bwd")
    grads["norm1_w"] = g_norm1
    return loss, grad_x, grads, got_early, got_late


HBM_SPEC = pl.BlockSpec(memory_space=pltpu.HBM)


def _mesh_pos():
    x, y, c = lax.axis_index("x"), lax.axis_index("y"), lax.axis_index("c")
    return x, y, c, 4 * x + 2 * y + c


def _peer(k):
    x, y, c, _ = _mesh_pos()
    px = 1 - x if k & 4 else x
    py = 1 - y if k & 2 else y
    pc = 1 - c if k & 1 else c
    return (px, py, pc), 4 * px + 2 * py + pc


def _rcopy(src, dst, send, recv, a, s, peer):
    return pltpu.make_async_remote_copy(src_ref=src, dst_ref=dst, send_sem=send.at[a, s], recv_sem=recv.at[a, s],
                                        device_id=peer, device_id_type=pl.DeviceIdType.MESH)


class _Gather:
    ICI = (2, 4, 6)

    def __init__(self, shards):
        self.args = list(shards)
        self.n = len(shards)
        self.out_shape = [jax.ShapeDtypeStruct((N_DEV,) + s.shape, s.dtype) for s in shards]
        self.scratch = [pltpu.SemaphoreType.DMA((self.n, N_DEV - 1)), pltpu.SemaphoreType.DMA((self.n, N_DEV - 1)),
                        pltpu.SemaphoreType.DMA((self.n,))]

    def _slot(self, outs, a, d):
        return outs[a].at[d]

    def _first(self, ins, outs, send, recv, a):
        me = _mesh_pos()[3]
        out, got = [], []
        for s, k in enumerate((1,) + self.ICI):
            peer, pidx = _peer(k)
            out.append(_rcopy(ins[a], self._slot(outs, a, me), send, recv, a, s, peer))
            got.append(_rcopy(ins[a], self._slot(outs, a, pidx), send, recv, a, s, peer))
        return out, got

    def _forward(self, ins, outs, send, recv, a):
        sib = _peer(1)[0]
        out, got = [], []
        for s, k in enumerate(self.ICI):
            held = self._slot(outs, a, _peer(k)[1])
            out.append(_rcopy(held, held, send, recv, a, 4 + s, sib))
            other = self._slot(outs, a, _peer(k | 1)[1])
            got.append(_rcopy(other, other, send, recv, a, 4 + s, sib))
        return out, got

    def start(self, ins, outs, sems):
        send, recv, loc = sems
        me = _mesh_pos()[3]
        for a in range(self.n):
            pltpu.make_async_copy(ins[a], self._slot(outs, a, me), loc.at[a]).start()
            for cp in self._first(ins, outs, send, recv, a)[0]:
                cp.start()

    def mid(self, ins, outs, sems):
        send, recv, _ = sems
        for a in range(self.n):
            arrivals = self._first(ins, outs, send, recv, a)[1]
            for s, cp in enumerate(self._forward(ins, outs, send, recv, a)[0]):
                arrivals[1 + s].wait_recv()
                cp.start()

    def finish(self, ins, outs, sems):
        send, recv, loc = sems
        me = _mesh_pos()[3]
        for a in range(self.n):
            first_out, first_got = self._first(ins, outs, send, recv, a)
            fwd_out, fwd_got = self._forward(ins, outs, send, recv, a)
            first_got[0].wait_recv()
            for cp in fwd_got:
                cp.wait_recv()
            for cp in first_out + fwd_out:
                cp.wait_send()
            pltpu.make_async_copy(ins[a], self._slot(outs, a, me), loc.at[a]).wait()


class _Exchange:
    def __init__(self, slabs=(), gathered=(), chip_slabs=(), sibling_slabs=()):
        self.args = list(slabs) + list(chip_slabs) + list(sibling_slabs) + list(gathered)
        self.kind = (["dev"] * len(slabs) + ["chip"] * len(chip_slabs) + ["sib"] * len(sibling_slabs)
                     + ["all"] * len(gathered))
        self.n = len(self.args)
        half = lambda s: jax.ShapeDtypeStruct((N_DEV // 2,) + s.shape[1:], s.dtype)
        self.out_shape = ([jax.ShapeDtypeStruct(s.shape, s.dtype) for s in slabs]
                          + [half(s) for s in chip_slabs] + [half(s) for s in sibling_slabs]
                          + [jax.ShapeDtypeStruct((N_DEV,) + s.shape, s.dtype) for s in gathered])
        self.scratch = [pltpu.SemaphoreType.DMA((self.n, N_DEV - 1)), pltpu.SemaphoreType.DMA((self.n, N_DEV - 1)),
                        pltpu.SemaphoreType.DMA((self.n,))]

    def _copies(self, ins, outs, send, recv, a):
        x, y, c, me = _mesh_pos()
        kind = self.kind[a]
        out, got = [], []
        if kind == "sib":
            sib = _peer(1)[0]
            for q in range(N_DEV // 2):
                out.append(_rcopy(ins[a].at[2 * q + 1 - c], outs[a].at[q], send, recv, a, q, sib))
                got.append(_rcopy(ins[a].at[2 * q + c], outs[a].at[q], send, recv, a, q, sib))
            return out, got
        for k in ((2, 4, 6) if kind == "chip" else range(1, N_DEV)):
            peer, pidx = _peer(k)
            if kind == "chip":
                src, mine, theirs = ins[a].at[2 * peer[0] + peer[1]], 2 * x + y, 2 * peer[0] + peer[1]
            else:
                src, mine, theirs = (ins[a].at[pidx] if kind == "dev" else ins[a]), me, pidx
            out.append(_rcopy(src, outs[a].at[mine], send, recv, a, k - 1, peer))
            got.append(_rcopy(src, outs[a].at[theirs], send, recv, a, k - 1, peer))
        return out, got

    def _local(self, ins, outs, loc, a):
        x, y, _, me = _mesh_pos()
        kind = self.kind[a]
        if kind == "sib":
            return None
        if kind == "chip":
            return pltpu.make_async_copy(ins[a].at[2 * x + y], outs[a].at[2 * x + y], loc.at[a])
        return pltpu.make_async_copy(ins[a].at[me] if kind == "dev" else ins[a], outs[a].at[me], loc.at[a])

    def start(self, ins, outs, sems):
        send, recv, loc = sems
        for a in range(self.n):
            if self._local(ins, outs, loc, a) is not None:
                self._local(ins, outs, loc, a).start()
            for cp in self._copies(ins, outs, send, recv, a)[0]:
                cp.start()

    def mid(self, ins, outs, sems):
        pass

    def finish(self, ins, outs, sems):
        send, recv, loc = sems
        for a in range(self.n):
            out, got = self._copies(ins, outs, send, recv, a)
            for cp in got:
                cp.wait_recv()
            for cp in out:
                cp.wait_send()
            if self._local(ins, outs, loc, a) is not None:
                self._local(ins, outs, loc, a).wait()


def _comm_call(comm, name):
    n = comm.n

    def body(*refs):
        ins, outs, sems = refs[:n], refs[n:2 * n], refs[2 * n:]
        comm.start(ins, outs, sems)
        comm.mid(ins, outs, sems)
        comm.finish(ins, outs, sems)

    return pl.pallas_call(
        body, name=name, out_shape=comm.out_shape, in_specs=[HBM_SPEC] * n, out_specs=[HBM_SPEC] * n,
        scratch_shapes=comm.scratch,
    )(*comm.args)


def _hosted(body, comm, n_in, n_out, when):
    if comm is None:
        return body

    def wrapped(*refs):
        ins, c_ins = refs[:n_in], refs[n_in:n_in + comm.n]
        o0 = n_in + comm.n
        outs, c_outs = refs[o0:o0 + n_out], refs[o0 + n_out:o0 + n_out + comm.n]
        scratch, sems = refs[o0 + n_out + comm.n:len(refs) - 3], refs[len(refs) - 3:]
        first, middle, last = when()

        @pl.when(first)
        def _():
            comm.start(c_ins, c_outs, sems)

        body(*ins, *outs, *scratch)

        @pl.when(middle)
        def _():
            comm.mid(c_ins, c_outs, sems)

        @pl.when(last)
        def _():
            comm.finish(c_ins, c_outs, sems)

    return wrapped


def _host_call(body, name, comm, when, out_shape, grid, in_specs, out_specs, scratch_shapes, sem, args):
    n_in, n_out = len(in_specs), len(out_specs)
    if comm is None:
        res = pl.pallas_call(body, name=name, out_shape=out_shape, grid=grid, in_specs=in_specs, out_specs=out_specs,
                             scratch_shapes=scratch_shapes, compiler_params=_params(sem))(*args)
        return list(res), []
    res = pl.pallas_call(
        _hosted(body, comm, n_in, n_out, when), name=name,
        out_shape=list(out_shape) + comm.out_shape, grid=grid,
        in_specs=list(in_specs) + [HBM_SPEC] * comm.n, out_specs=list(out_specs) + [HBM_SPEC] * comm.n,
        scratch_shapes=list(scratch_shapes) + comm.scratch,
        compiler_params=_params(("arbitrary",) * len(grid)),
    )(*args, *comm.args)
    return list(res[:n_out]), list(res[n_out:])


def _add_my_slabs(slabs, b, name):
    n, rows, cols = b.shape
    tc = _tile(cols, 256)

    def body(a_ref, b_ref, o_ref):
        o_ref[...] = (a_ref[...].astype(F32) + b_ref[...].astype(F32)).astype(o_ref.dtype)

    blk = pl.BlockSpec((None, rows, tc), lambda i, j: (i, 0, j))
    mine = pl.BlockSpec((None, rows, tc), lambda i, j: (2 * i + lax.axis_index("c"), 0, j))
    return pl.pallas_call(
        body, name=name, out_shape=jax.ShapeDtypeStruct(b.shape, b.dtype), grid=(n, cols // tc),
        in_specs=[mine, blk], out_specs=blk, compiler_params=_params(("parallel", "parallel")),
    )(slabs, b)


def _adamw(parts, w, m, v, name):
    rows, cols = w.shape
    nparts = parts.shape[0]
    tr, tc = rows, cols
    for cand in (128, 176):
        if rows > cand and rows % cand == 0:
            tr = cand
            break
    if tr == rows and rows > 512:
        tc = _tile(cols, 256)

    def body(p_ref, w_ref, m_ref, v_ref, g_ref, d_ref, mo_ref, vo_ref):
        g = p_ref[0].astype(F32)
        for s in range(1, nparts):
            g = g + p_ref[s].astype(F32)
        mn = ADAM_B1 * m_ref[...] + (1.0 - ADAM_B1) * g
        vn = ADAM_B2 * v_ref[...] + (1.0 - ADAM_B2) * (g * g)
        m_hat = mn / (1.0 - ADAM_B1 ** ADAM_STEP)
        v_hat = vn / (1.0 - ADAM_B2 ** ADAM_STEP)
        g_ref[...] = g
        d_ref[...] = -ADAM_LR * (m_hat / (jnp.sqrt(v_hat) + ADAM_EPS) + ADAM_WD * w_ref[...])
        mo_ref[...] = mn
        vo_ref[...] = vn

    blk = pl.BlockSpec((tr, tc), lambda i, j: (i, j))
    out = jax.ShapeDtypeStruct((rows, cols), F32)
    return pl.pallas_call(
        body, name=name,
        out_shape=(out, out, out, out),
        grid=(rows // tr, cols // tc),
        in_specs=[pl.BlockSpec((nparts, tr, tc), lambda i, j: (0, i, j)), blk, blk, blk],
        out_specs=(blk, blk, blk, blk),
        compiler_params=_params(("parallel", "parallel")),
    )(parts, w, m, v)


CONV_PACK = 8 * 1024
WEIGHT_ORDER = ("norm1_w", "w_in", "dn_conv_w", "dn_A_log", "dn_dt_bias", "dn_norm_w", "w_proj_dn", "w_proj_sb",
                "w_out", "norm2_w", "ffn_w_up", "ffn_conv_w", "ffn_w_down", "norm_f_w")


def _cols_to_slabs(g):
    r, c8 = g.shape
    return g.reshape(r, N_DEV, c8 // N_DEV).transpose(1, 0, 2)


def _slabs_to_cols(s):
    d, r, c = s.shape
    return s.transpose(1, 0, 2).reshape(r, d * c)


def kernel(x, norm1_w, w_in, dn_conv_w, dn_A_log, dn_dt_bias, dn_norm_w, w_proj_dn, w_proj_sb, w_out, norm2_w, ffn_w_up, ffn_conv_w, ffn_w_down, norm_f_w, loss_target, m_norm1_w, m_w_in, m_dn_conv_w, m_dn_A_log, m_dn_dt_bias, m_dn_norm_w, m_w_proj_dn, m_w_proj_sb, m_w_out, m_norm2_w, m_ffn_w_up, m_ffn_conv_w, m_ffn_w_down, m_norm_f_w, v_norm1_w, v_w_in, v_dn_conv_w, v_dn_A_log, v_dn_dt_bias, v_dn_norm_w, v_w_proj_dn, v_w_proj_sb, v_w_out, v_norm2_w, v_ffn_w_up, v_ffn_conv_w, v_ffn_w_down, v_norm_f_w):
    me = _mesh_pos()[3]
    tr = lambda a: jnp.transpose(a[0])
    w_loc = dict(norm1_w=norm1_w, w_in=tr(w_in), dn_conv_w=dn_conv_w[0], dn_A_log=dn_A_log, dn_dt_bias=dn_dt_bias,
                 dn_norm_w=dn_norm_w, w_proj_dn=w_proj_dn[0], w_proj_sb=w_proj_sb[0], w_out=w_out[0],
                 norm2_w=norm2_w, ffn_w_up=tr(ffn_w_up), ffn_conv_w=ffn_conv_w[0], ffn_w_down=ffn_w_down[0],
                 norm_f_w=norm_f_w[None, :])
    m_loc = dict(norm1_w=m_norm1_w, w_in=tr(m_w_in), dn_conv_w=m_dn_conv_w[0], dn_A_log=m_dn_A_log,
                 dn_dt_bias=m_dn_dt_bias, dn_norm_w=m_dn_norm_w, w_proj_dn=m_w_proj_dn[0], w_proj_sb=m_w_proj_sb[0],
                 w_out=m_w_out[0], norm2_w=m_norm2_w, ffn_w_up=tr(m_ffn_w_up), ffn_conv_w=m_ffn_conv_w[0],
                 ffn_w_down=m_ffn_w_down[0], norm_f_w=m_norm_f_w[None, :])
    v_loc = dict(norm1_w=v_norm1_w, w_in=tr(v_w_in), dn_conv_w=v_dn_conv_w[0], dn_A_log=v_dn_A_log,
                 dn_dt_bias=v_dn_dt_bias, dn_norm_w=v_dn_norm_w, w_proj_dn=v_w_proj_dn[0], w_proj_sb=v_w_proj_sb[0],
                 w_out=v_w_out[0], norm2_w=v_norm2_w, ffn_w_up=tr(v_ffn_w_up), ffn_conv_w=v_ffn_conv_w[0],
                 ffn_w_down=v_ffn_w_down[0], norm_f_w=v_norm_f_w[None, :])

    conv_flat = jnp.concatenate([w_loc["dn_conv_w"].reshape(-1), w_loc["ffn_conv_w"].reshape(-1)])
    n_dn, n_ffn = DN_CONV * 3 * WIDTH // N_DEV, FFN_CONV * 2 * D_FF // N_DEV
    conv_pack = jnp.pad(conv_flat, (0, CONV_PACK - n_dn - n_ffn)).reshape(8, 1024)
    n1, (g_in, g_conv) = _rmsnorm_fwd(x[0], norm1_w, "norm1_fwd",
                                      comm=_Gather([w_loc["w_in"].astype(BF16), conv_pack]))
    in_width = g_in.shape[0] * g_in.shape[1]
    w_in_t = g_in.reshape(in_width, D_MODEL)
    g_conv = g_conv.reshape(N_DEV, CONV_PACK)
    dn_conv_full = _slabs_to_cols(g_conv[:, :n_dn].reshape(N_DEV, DN_CONV, 3 * WIDTH // N_DEV))
    ffn_conv_full = _slabs_to_cols(g_conv[:, n_dn:n_dn + n_ffn].reshape(N_DEV, FFN_CONV, 2 * D_FF // N_DEV))
    q_end = 3 * WIDTH
    ab_end = q_end + 2 * HEADS
    gate_end = ab_end + WIDTH
    sb_end = gate_end + 3 * WIDTH
    pad_lanes = lambda a: jnp.pad(a, ((0, 0), (0, 128 - a.shape[1])))
    wts = dict(
        norm1_w=norm1_w, w_dnqkv_t=w_in_t[:q_end], w_ab_t=jnp.pad(w_in_t[q_end:ab_end], ((0, 128 - 2 * HEADS), (0, 0))),
        w_dngate_t=w_in_t[ab_end:gate_end], w_sbqkv_t=w_in_t[gate_end:sb_end], w_gl_t=w_in_t[sb_end:],
        dn_conv_w=dn_conv_full, alog=pad_lanes(dn_A_log), dtb=pad_lanes(dn_dt_bias), dn_norm_w=dn_norm_w,
        norm2_w=norm2_w, ffn_conv_w=ffn_conv_full, norm_f_w=norm_f_w[None, :])

    n_fc = FFN_CONV * 2 * D_FF
    fc_rows = -(-n_fc // D_MODEL)
    dn_rows = DN_CONV * 3 * WIDTH // D_MODEL
    late_names = ("w_proj_dn", "w_proj_sb", "w_out", "ffn_w_up", "ffn_w_down")

    class Plan:
        @staticmethod
        def late_gather():
            return _Gather([w_loc[k].astype(BF16) for k in late_names])

        @staticmethod
        def late_weights(got):
            g_pd, g_ps, g_out, g_up, g_down = got
            return dict(w_proj_dn=g_pd.reshape(WIDTH, D_MODEL), w_proj_sb=g_ps.reshape(WIDTH, D_MODEL),
                        w_out=g_out.reshape(D_MODEL, D_MODEL), ffn_w_up_t=g_up.reshape(2 * D_FF, D_MODEL),
                        ffn_w_down=g_down.reshape(D_FF, D_MODEL))

        @staticmethod
        def early_grads(g):
            return _Exchange([g["w_proj_dn"].reshape(N_DEV, WIDTH // N_DEV, D_MODEL),
                              g["w_proj_sb"].reshape(N_DEV, WIDTH // N_DEV, D_MODEL),
                              g["w_out"].reshape(N_DEV, D_MODEL // N_DEV, D_MODEL),
                              g["ffn_w_up_t"].reshape(N_DEV, 2 * D_FF // N_DEV, D_MODEL),
                              g["ffn_w_down"].reshape(N_DEV, D_FF // N_DEV, D_MODEL)])

        @staticmethod
        def _in_slabs(g):
            g_win_t = jnp.concatenate([g["w_main_t"][:q_end], g["w_ab_t"][:2 * HEADS], g["w_main_t"][q_end:]],
                                      axis=0)
            return g_win_t.reshape(N_DEV, in_width // N_DEV, D_MODEL)

        @staticmethod
        def sibling_swap(g):
            return _Exchange(sibling_slabs=[Plan._in_slabs(g)])

        @staticmethod
        def late_grads(swapped, g, loss):
            chip_sums = _add_my_slabs(Plan._in_slabs(g), swapped[0], "in_dw_chip_sum")
            row3 = jnp.concatenate([g["dn_norm_w"], g["alog"], g["dtb"], jnp.pad(loss, ((0, 0), (0, 127))),
                                    jnp.zeros((1, D_MODEL - 512), F32)], axis=1)
            fconv_rows = jnp.pad(g["ffn_conv_w"].reshape(-1), (0, fc_rows * D_MODEL - n_fc)).reshape(fc_rows, D_MODEL)
            pad8 = lambda a: jnp.pad(a, ((0, -a.shape[0] % 8), (0, 0)))
            pieces = [g["norm2_w"], g["norm_f_w"], row3, g["dn_conv_w"].reshape(dn_rows, D_MODEL), fconv_rows]
            small = jnp.concatenate([pad8(a) for a in pieces], axis=0)
            assert small.shape[0] == SMALL_ROWS
            return _Exchange(chip_slabs=[chip_sums], gathered=[small])

    loss, grad_x, g, got_early, got_late = _local_step(x[0], loss_target[0], wts, Plan, n1)
    r_pd, r_ps, r_out, r_up, r_down = got_early
    r_in, r_small = got_late
    (r_norm1,) = _comm_call(_Exchange([], [jnp.pad(g["norm1_w"], ((0, 7), (0, 0)))]), "gather_norm1")

    parts = dict(w_in=r_in, w_proj_dn=r_pd, w_proj_sb=r_ps, w_out=r_out, ffn_w_up=r_up, ffn_w_down=r_down)
    parts["norm1_w"] = r_norm1[:, 0:1, :]
    parts["norm2_w"] = r_small[:, 0:1, :]
    parts["norm_f_w"] = r_small[:, 8:9, :]
    parts["dn_norm_w"] = r_small[:, 16:17, 0:HEAD_DIM]
    parts["dn_A_log"] = r_small[:, 16:17, 128:128 + HEADS]
    parts["dn_dt_bias"] = r_small[:, 16:17, 256:256 + HEADS]
    dnc = r_small[:, 24:24 + dn_rows, :].reshape(N_DEV, DN_CONV, 3 * WIDTH)
    parts["dn_conv_w"] = lax.dynamic_slice_in_dim(dnc, me * (3 * WIDTH // N_DEV), 3 * WIDTH // N_DEV, axis=2)
    fc0 = 24 + dn_rows + (-dn_rows % 8)
    fcc = r_small[:, fc0:fc0 + fc_rows, :].reshape(N_DEV, fc_rows * D_MODEL)[:, :n_fc]
    fcc = fcc.reshape(N_DEV, FFN_CONV, 2 * D_FF)
    parts["ffn_conv_w"] = lax.dynamic_slice_in_dim(fcc, me * (2 * D_FF // N_DEV), 2 * D_FF // N_DEV, axis=2)
    loss_total = jnp.sum(r_small[:, 16, 384])

    res = {k: _adamw(parts[k], w_loc[k], m_loc[k], v_loc[k], "adamw_" + k) for k in WEIGHT_ORDER}
    lead = ("w_in", "dn_conv_w", "w_proj_dn", "w_proj_sb", "w_out", "ffn_w_up", "ffn_conv_w", "ffn_w_down")

    def shaped(k, a):
        if k in ("w_in", "ffn_w_up"):
            return jnp.transpose(a)[None]
        if k in lead:
            return a[None]
        if k == "norm_f_w":
            return a[0]
        return a

    outs = [loss_total, grad_x[None]]
    for idx in range(4):
        outs += [shaped(k, res[k][idx]) for k in WEIGHT_ORDER]
    return tuple(outs)
```

```python
import functools

import jax
import jax.numpy as jnp
from jax import lax
from jax.experimental import pallas as pl
from jax.experimental.pallas import tpu as pltpu

F32 = jnp.float32
BF16 = jnp.bfloat16

N_DEV = 8
D_MODEL = 1024
HEADS = 8
HEAD_DIM = 128
WIDTH = HEADS * HEAD_DIM
DN_CONV = 4
DN_CHUNK = 64
D_FF = 2816
FFN_CONV = 3
EPS = 1e-6
HALO = 16
CHUNK_ROWS = 256
SCAN_CHUNKS = 4
LOCAL_CHUNKS = 4
ATT_BLOCK = 256
SB_LOG_ZERO = -104.0
SB_GROUP = 2
SB_HEADS_FWD = 4
SB_HEADS_BWD = 2
SMALL_ROWS = 64

ADAM_LR = 0.001
ADAM_B1 = 0.9
ADAM_B2 = 0.999
ADAM_EPS = 1e-08
ADAM_WD = 0.01
ADAM_STEP = 10

VMEM_LIMIT = 48 * 1024 * 1024


def _params(sem=None, **kw):
    return pltpu.CompilerParams(dimension_semantics=sem, vmem_limit_bytes=VMEM_LIMIT, **kw)


def _tile(n, cap):
    if n <= cap:
        return n
    best = None
    for t in range(128, cap + 1, 128):
        if n % t == 0:
            best = t
    assert best is not None, (n, cap)
    return best


def _dot(a, b, dims):
    return lax.dot_general(a, b, ((dims[0], dims[1]), ((), ())), preferred_element_type=F32)


NN = ((1,), (0,))
NT = ((1,), (1,))
TN = ((0,), (0,))


def _dotb(a, b, dims):
    return _dot(a.astype(BF16), b.astype(BF16), dims)


def _split3(x):
    h1 = x.astype(BF16)
    r1 = x - h1.astype(F32)
    h2 = r1.astype(BF16)
    r2 = r1 - h2.astype(F32)
    return h1, h2, r2.astype(BF16)


def _dot_xr(a, b_exact, dims):
    a1, a2, a3 = _split3(a)
    return _dot(a1, b_exact, dims) + _dot(a2, b_exact, dims) + _dot(a3, b_exact, dims)


def _split2(x):
    h1 = x.astype(BF16)
    return h1, (x - h1.astype(F32)).astype(BF16)


def _dot_xr2(a, b_exact, dims):
    a1, a2 = _split2(a)
    return _dot(a1, b_exact, dims) + _dot(a2, b_exact, dims)


def _dot_xl(a_exact, b, dims):
    b1, b2, b3 = _split3(b)
    return _dot(a_exact, b1, dims) + _dot(a_exact, b2, dims) + _dot(a_exact, b3, dims)


def _dot3(a, b, dims):
    a1 = a.astype(BF16)
    a2 = (a - a1.astype(F32)).astype(BF16)
    b1 = b.astype(BF16)
    b2 = (b - b1.astype(F32)).astype(BF16)
    return _dot(a1, b1, dims) + (_dot(a1, b2, dims) + _dot(a2, b1, dims))


def _sigmoid(x):
    return 1.0 / (1.0 + jnp.exp(-x))


def _log1pexp_neg_abs(x):
    return jnp.log(1.0 + jnp.exp(-jnp.abs(x)))


def _iota(shape, dim):
    return lax.broadcasted_iota(jnp.int32, shape, dim)


def _matmul(a, b, mode, out_dtype, name, add=None, comm=None):
    if mode == "nn":
        (m, k), (k2, n) = a.shape, b.shape
    elif mode == "nt":
        (m, k), (n, k2) = a.shape, b.shape
    else:
        (k, m), (k2, n) = a.shape, b.shape
    assert k == k2, (a.shape, b.shape, mode)
    tm, tn, tk = _tile(m, 1408), _tile(n, 1408), _tile(k, 1536)
    nk = k // tk
    dims = {"nn": NN, "nt": NT, "tn": TN}[mode]

    def body(*refs):
        if add is None:
            a_ref, b_ref, o_ref, acc_ref = refs
        else:
            a_ref, b_ref, add_ref, o_ref, acc_ref = refs
        kk = pl.program_id(2)

        @pl.when(kk == 0)
        def _():
            acc_ref[...] = jnp.zeros_like(acc_ref)

        acc_ref[...] += _dotb(a_ref[...], b_ref[...], dims)

        @pl.when(kk == nk - 1)
        def _():
            r = acc_ref[...]
            if add is not None:
                r = r + add_ref[...].astype(F32)
            o_ref[...] = r.astype(out_dtype)

    if mode == "nn":
        specs = [pl.BlockSpec((tm, tk), lambda i, j, l: (i, l)), pl.BlockSpec((tk, tn), lambda i, j, l: (l, j))]
    elif mode == "nt":
        specs = [pl.BlockSpec((tm, tk), lambda i, j, l: (i, l)), pl.BlockSpec((tn, tk), lambda i, j, l: (j, l))]
    else:
        specs = [pl.BlockSpec((tk, tm), lambda i, j, l: (l, i)), pl.BlockSpec((tk, tn), lambda i, j, l: (l, j))]
    args = [a, b]
    if add is not None:
        specs.append(pl.BlockSpec((tm, tn), lambda i, j, l: (i, j)))
        args.append(add)
    grid = (m // tm, n // tn, nk)

    def when():
        i, j, l = pl.program_id(0), pl.program_id(1), pl.program_id(2)
        first = jnp.logical_and(jnp.logical_and(i == 0, j == 0), l == 0)
        last = jnp.logical_and(jnp.logical_and(i == grid[0] - 1, j == grid[1] - 1), l == nk - 1)
        return first, last, last

    (out,), extra = _host_call(
        body, name, comm, when, [jax.ShapeDtypeStruct((m, n), out_dtype)], grid, specs,
        [pl.BlockSpec((tm, tn), lambda i, j, l: (i, j))], [pltpu.VMEM((tm, tn), F32)],
        ("parallel", "parallel", "arbitrary"), args)
    return out if comm is None else (out, extra)


def _rmsnorm_fwd(x, w, name, comm=None):
    t, d = x.shape
    tr = _tile(t, 512)
    steps = t // tr

    def body(x_ref, w_ref, o_ref):
        xv = x_ref[...]
        r = lax.rsqrt(jnp.mean(xv * xv, axis=1, keepdims=True) + EPS)
        o_ref[...] = (xv * r * w_ref[...]).astype(BF16)

    def when():
        i = pl.program_id(0)
        return i == 0, i == steps // 2, i == steps - 1

    (out,), extra = _host_call(
        body, name, comm, when, [jax.ShapeDtypeStruct((t, d), BF16)], (steps,),
        [pl.BlockSpec((tr, d), lambda i: (i, 0)), pl.BlockSpec((1, d), lambda i: (0, 0))],
        [pl.BlockSpec((tr, d), lambda i: (i, 0))], [], ("parallel",), (x, w))
    return out if comm is None else (out, extra)


def _rmsnorm_bwd(dn, x, w, dres, name):
    t, d = x.shape
    tr = _tile(t, 512)

    def body(dn_ref, x_ref, w_ref, dres_ref, dx_ref, dw_ref):
        i = pl.program_id(0)
        xv = x_ref[...]
        g = dn_ref[...].astype(F32)
        r = lax.rsqrt(jnp.mean(xv * xv, axis=1, keepdims=True) + EPS)
        xh = xv * r
        dxh = g * w_ref[...]
        dx = r * (dxh - xh * jnp.mean(dxh * xh, axis=1, keepdims=True))
        dx_ref[...] = dres_ref[...] + dx

        @pl.when(i == 0)
        def _():
            dw_ref[...] = jnp.zeros_like(dw_ref)

        dw_ref[...] += jnp.sum(g * xh, axis=0, keepdims=True)

    return pl.pallas_call(
        body, name=name,
        out_shape=(jax.ShapeDtypeStruct((t, d), F32), jax.ShapeDtypeStruct((1, d), F32)),
        grid=(t // tr,),
        in_specs=[pl.BlockSpec((tr, d), lambda i: (i, 0)), pl.BlockSpec((tr, d), lambda i: (i, 0)),
                  pl.BlockSpec((1, d), lambda i: (0, 0)), pl.BlockSpec((tr, d), lambda i: (i, 0))],
        out_specs=(pl.BlockSpec((tr, d), lambda i: (i, 0)), pl.BlockSpec((1, d), lambda i: (0, 0))),
        compiler_params=_params(("arbitrary",)),
    )(dn, x, w, dres)


def _final_loss(x2, target, w, name):
    t, d = x2.shape
    tr = _tile(t, 512)

    def body(x_ref, t_ref, w_ref, dx_ref, dw_ref, loss_ref):
        i = pl.program_id(0)
        xv = x_ref[...]
        r = lax.rsqrt(jnp.mean(xv * xv, axis=1, keepdims=True) + EPS)
        xh = xv * r
        err = xh * w_ref[...] - t_ref[...]
        dy = err * (1.0 / d)
        dxh = dy * w_ref[...]
        dx_ref[...] = r * (dxh - xh * jnp.mean(dxh * xh, axis=1, keepdims=True))

        @pl.when(i == 0)
        def _():
            dw_ref[...] = jnp.zeros_like(dw_ref)
            loss_ref[...] = jnp.zeros_like(loss_ref)

        dw_ref[...] += jnp.sum(dy * xh, axis=0, keepdims=True)
        row = jnp.sum(err * err, axis=1, keepdims=True) * (0.5 / d)
        loss_ref[...] += jnp.sum(row, axis=0, keepdims=True)

    return pl.pallas_call(
        body, name=name,
        out_shape=(jax.ShapeDtypeStruct((t, d), F32), jax.ShapeDtypeStruct((1, d), F32),
                   jax.ShapeDtypeStruct((1, 1), F32)),
        grid=(t // tr,),
        in_specs=[pl.BlockSpec((tr, d), lambda i: (i, 0)), pl.BlockSpec((tr, d), lambda i: (i, 0)),
                  pl.BlockSpec((1, d), lambda i: (0, 0))],
        out_specs=(pl.BlockSpec((tr, d), lambda i: (i, 0)), pl.BlockSpec((1, d), lambda i: (0, 0)),
                   pl.BlockSpec((1, 1), lambda i: (0, 0))),
        compiler_params=_params(("arbitrary",)),
    )(x2, target, w)


def _shift_down(cur, prev, k, row):
    r = pltpu.roll(cur, k, 0)
    top, row8 = r[0:8, :], row[0:8, :]
    for m in range(k):
        top = jnp.where(row8 == m, prev[HALO - k + m:HALO - k + m + 1, :], top)
    return jnp.concatenate([top, r[8:, :]], axis=0)


def _shift_up(cur, nxt, k, row, tr):
    r = pltpu.roll(cur, tr - k, 0)
    bottom, row8 = r[tr - 8:, :], row[0:8, :]
    for m in range(k):
        bottom = jnp.where(row8 == 8 - k + m, nxt[m:m + 1, :], bottom)
    return jnp.concatenate([r[:tr - 8, :], bottom], axis=0)


def _fold8(a):
    out = a[0:8, :]
    for r in range(8, a.shape[0], 8):
        out = out + a[r:r + 8, :]
    return out


def _conv_taps(cur, prev, w, ntaps, row):
    taps = [cur if i == ntaps - 1 else _shift_down(cur, prev, ntaps - 1 - i, row) for i in range(ntaps)]
    y = w[0:1, :] * taps[0]
    for i in range(1, ntaps):
        y = y + w[i:i + 1, :] * taps[i]
    return taps, y


def _conv_bwd_data(parts, w, ntaps, out_dtype, name):
    t, chp = parts[0].shape
    npart = len(parts)
    tr, tc = _tile(t, 512), _tile(chp, 1408)
    nc = chp // tc
    nhalo = t // HALO
    last = t // tr - 1

    def body(*refs):
        cur_refs, nxt_refs = refs[:npart], refs[npart:2 * npart]
        w_ref, o_ref = refs[2 * npart], refs[2 * npart + 1]
        i, j = pl.program_id(0), pl.program_id(1)
        row = _iota((tr, 128), 0)
        for c0 in range(0, tc, 128):
            sl = slice(c0, c0 + 128)
            cur, nxt = cur_refs[0][:, sl].astype(F32), nxt_refs[0][:, sl].astype(F32)
            for p in range(1, npart):
                cur = jnp.where(j >= p * nc, cur_refs[p][:, sl].astype(F32), cur)
                nxt = jnp.where(j >= p * nc, nxt_refs[p][:, sl].astype(F32), nxt)
            nxt = jnp.where(i == last, 0.0, nxt)
            wv = w_ref[:, sl]
            y = wv[ntaps - 1:ntaps, :] * cur
            for k in range(1, ntaps):
                y = y + wv[ntaps - 1 - k:ntaps - k, :] * _shift_up(cur, nxt, k, row, tr)
            o_ref[:, sl] = y.astype(out_dtype)

    col = lambda p: (lambda j: jnp.clip(j - p * nc, 0, nc - 1))
    cur_specs = [pl.BlockSpec((tr, tc), lambda i, j, c=col(p): (i, c(j))) for p in range(npart)]
    nxt_specs = [pl.BlockSpec((HALO, tc),
                              lambda i, j, c=col(p): (jnp.minimum((i + 1) * (tr // HALO), nhalo - 1), c(j)))
                 for p in range(npart)]
    return pl.pallas_call(
        body, name=name,
        out_shape=jax.ShapeDtypeStruct((t, npart * chp), out_dtype),
        grid=(t // tr, npart * nc),
        in_specs=cur_specs + nxt_specs + [pl.BlockSpec((ntaps, tc), lambda i, j: (0, j))],
        out_specs=pl.BlockSpec((tr, tc), lambda i, j: (i, j)),
        compiler_params=_params(("parallel", "parallel")),
    )(*parts, *parts, w)


def _ffn_act_fwd(upre, cw, name):
    t = upre.shape[0]
    tr, tc = _tile(t, 512), _tile(D_FF, 1408)
    nj = D_FF // tc

    def body(g_ref, gp_ref, u_ref, up_ref, wg_ref, wu_ref, o_ref):
        i = pl.program_id(0)
        row = _iota((tr, 128), 0)
        for c0 in range(0, tc, 128):
            sl = slice(c0, c0 + 128)
            gp = jnp.where(i == 0, 0.0, gp_ref[:, sl].astype(F32))
            up = jnp.where(i == 0, 0.0, up_ref[:, sl].astype(F32))
            _, gc = _conv_taps(g_ref[:, sl].astype(F32), gp, wg_ref[:, sl], FFN_CONV, row)
            _, uc = _conv_taps(u_ref[:, sl].astype(F32), up, wu_ref[:, sl], FFN_CONV, row)
            o_ref[:, sl] = (gc * _sigmoid(gc) * uc).astype(BF16)

    prev = lambda off: (lambda i, j: (jnp.maximum(i * (tr // HALO) - 1, 0), j + off))
    return pl.pallas_call(
        body, name=name,
        out_shape=jax.ShapeDtypeStruct((t, D_FF), BF16),
        grid=(t // tr, nj),
        in_specs=[pl.BlockSpec((tr, tc), lambda i, j: (i, j)), pl.BlockSpec((HALO, tc), prev(0)),
                  pl.BlockSpec((tr, tc), lambda i, j: (i, j + nj)), pl.BlockSpec((HALO, tc), prev(nj)),
                  pl.BlockSpec((FFN_CONV, tc), lambda i, j: (0, j)),
                  pl.BlockSpec((FFN_CONV, tc), lambda i, j: (0, j + nj))],
        out_specs=pl.BlockSpec((tr, tc), lambda i, j: (i, j)),
        compiler_params=_params(("parallel", "parallel")),
    )(upre, upre, upre, upre, cw, cw)


def _ffn_act_bwd(dact, upre, cw, name):
    t = upre.shape[0]
    tr, tc = _tile(t, 512), _tile(D_FF, 1408)
    nj = D_FF // tc

    def body(da_ref, g_ref, gp_ref, u_ref, up_ref, wg_ref, wu_ref, dg_ref, du_ref, dwg_ref, dwu_ref):
        i = pl.program_id(1)
        row = _iota((CHUNK_ROWS, 128), 0)

        @pl.when(i == 0)
        def _():
            dwg_ref[...] = jnp.zeros_like(dwg_ref)
            dwu_ref[...] = jnp.zeros_like(dwu_ref)

        for c0 in range(0, tc, 128):
            sl = slice(c0, c0 + 128)
            wg, wu = wg_ref[:, sl], wu_ref[:, sl]
            dwg = [jnp.zeros((8, 128), F32)] * FFN_CONV
            dwu = [jnp.zeros((8, 128), F32)] * FFN_CONV
            for r0 in range(0, tr, CHUNK_ROWS):
                rows = slice(r0, r0 + CHUNK_ROWS)
                if r0 == 0:
                    gp = jnp.where(i == 0, 0.0, gp_ref[:, sl].astype(F32))
                    up = jnp.where(i == 0, 0.0, up_ref[:, sl].astype(F32))
                else:
                    gp = g_ref[r0 - HALO:r0, sl].astype(F32)
                    up = u_ref[r0 - HALO:r0, sl].astype(F32)
                gt, gc = _conv_taps(g_ref[rows, sl].astype(F32), gp, wg, FFN_CONV, row)
                ut, uc = _conv_taps(u_ref[rows, sl].astype(F32), up, wu, FFN_CONV, row)
                da = da_ref[rows, sl].astype(F32)
                sg = _sigmoid(gc)
                dgc = da * uc * (sg * (1.0 + gc * (1.0 - sg)))
                duc = da * (gc * sg)
                dg_ref[rows, sl] = dgc.astype(BF16)
                du_ref[rows, sl] = duc.astype(BF16)
                dwg = [dwg[k] + _fold8(dgc * gt[k]) for k in range(FFN_CONV)]
                dwu = [dwu[k] + _fold8(duc * ut[k]) for k in range(FFN_CONV)]
            for k in range(FFN_CONV):
                dwg_ref[k:k + 1, sl] += jnp.sum(dwg[k], axis=0, keepdims=True)
                dwu_ref[k:k + 1, sl] += jnp.sum(dwu[k], axis=0, keepdims=True)

    prev = lambda off: (lambda j, i: (jnp.maximum(i * (tr // HALO) - 1, 0), j + off))
    blk = lambda off: pl.BlockSpec((tr, tc), lambda j, i: (i, j + off))
    wblk = lambda off: pl.BlockSpec((FFN_CONV, tc), lambda j, i: (0, j + off))
    dgc, duc, dwg, dwu = pl.pallas_call(
        body, name=name,
        out_shape=(jax.ShapeDtypeStruct((t, D_FF), BF16), jax.ShapeDtypeStruct((t, D_FF), BF16),
                   jax.ShapeDtypeStruct((FFN_CONV, D_FF), F32), jax.ShapeDtypeStruct((FFN_CONV, D_FF), F32)),
        grid=(nj, t // tr),
        in_specs=[blk(0), blk(0), pl.BlockSpec((HALO, tc), prev(0)), blk(nj), pl.BlockSpec((HALO, tc), prev(nj)),
                  wblk(0), wblk(nj)],
        out_specs=(blk(0), blk(0), wblk(0), wblk(0)),
        compiler_params=_params(("parallel", "arbitrary")),
    )(dact, upre, upre, upre, upre, cw, cw)
    return dgc, duc, dwg, dwu


def _dn_pre_fwd(qkv_pre, cw, name):
    t = qkv_pre.shape[0]
    tr = _tile(t, 512)
    scale = HEAD_DIM ** -0.5

    def body(x_ref, p_ref, w_ref, o_ref):
        i, j = pl.program_id(0), pl.program_id(1)
        row = _iota((tr, HEAD_DIM), 0)
        for h in range(HEADS):
            sl = slice(h * HEAD_DIM, (h + 1) * HEAD_DIM)
            prev = jnp.where(i == 0, 0.0, p_ref[:, sl].astype(F32))
            _, c = _conv_taps(x_ref[:, sl].astype(F32), prev, w_ref[:, sl], DN_CONV, row)
            s = c * _sigmoid(c)
            r = lax.rsqrt(jnp.sum(s * s, axis=1, keepdims=True) + EPS)
            o_ref[:, sl] = s * jnp.where(j == 0, r * scale, jnp.where(j == 1, r, 1.0))

    return pl.pallas_call(
        body, name=name,
        out_shape=jax.ShapeDtypeStruct((t, 3 * WIDTH), F32),
        grid=(t // tr, 3),
        in_specs=[pl.BlockSpec((tr, WIDTH), lambda i, j: (i, j)),
                  pl.BlockSpec((HALO, WIDTH), lambda i, j: (jnp.maximum(i * (tr // HALO) - 1, 0), j)),
                  pl.BlockSpec((DN_CONV, WIDTH), lambda i, j: (0, j))],
        out_specs=pl.BlockSpec((tr, WIDTH), lambda i, j: (i, j)),
        compiler_params=_params(("parallel", "parallel")),
    )(qkv_pre, qkv_pre, cw)


def _dn_pre_bwd(dq, dk, dv, qkv_pre, cw, name):
    t = qkv_pre.shape[0]
    tr = _tile(t, 512)
    scale = HEAD_DIM ** -0.5

    def body(dq_ref, dk_ref, dv_ref, x_ref, p_ref, w_ref, dc_ref, dw_ref):
        j, i = pl.program_id(0), pl.program_id(1)
        row = _iota((CHUNK_ROWS, HEAD_DIM), 0)

        @pl.when(i == 0)
        def _():
            dw_ref[...] = jnp.zeros_like(dw_ref)

        for h in range(HEADS):
            sl = slice(h * HEAD_DIM, (h + 1) * HEAD_DIM)
            wv = w_ref[:, sl]
            dw = [jnp.zeros((8, HEAD_DIM), F32)] * DN_CONV
            for r0 in range(0, tr, CHUNK_ROWS):
                rows = slice(r0, r0 + CHUNK_ROWS)
                if r0 == 0:
                    prev = jnp.where(i == 0, 0.0, p_ref[:, sl].astype(F32))
                else:
                    prev = x_ref[r0 - HALO:r0, sl].astype(F32)
                taps, c = _conv_taps(x_ref[rows, sl].astype(F32), prev, wv, DN_CONV, row)
                d = jnp.where(j == 0, dq_ref[rows, sl] * scale, jnp.where(j == 1, dk_ref[rows, sl], dv_ref[rows, sl]))
                sg = _sigmoid(c)
                s = c * sg
                r = lax.rsqrt(jnp.sum(s * s, axis=1, keepdims=True) + EPS)
                nh = s * r
                ds_norm = r * (d - nh * jnp.sum(nh * d, axis=1, keepdims=True))
                dc = jnp.where(j < 2, ds_norm, d) * (sg * (1.0 + c * (1.0 - sg)))
                dc_ref[rows, sl] = dc.astype(BF16)
                dw = [dw[k] + _fold8(dc * taps[k]) for k in range(DN_CONV)]
            for k in range(DN_CONV):
                dw_ref[k:k + 1, sl] += jnp.sum(dw[k], axis=0, keepdims=True)

    dspec = lambda p: pl.BlockSpec((tr, WIDTH), lambda j, i: (jnp.where(j == p, i, 0), 0))
    return pl.pallas_call(
        body, name=name,
        out_shape=(jax.ShapeDtypeStruct((t, 3 * WIDTH), BF16), jax.ShapeDtypeStruct((DN_CONV, 3 * WIDTH), F32)),
        grid=(3, t // tr),
        in_specs=[dspec(0), dspec(1), dspec(2),
                  pl.BlockSpec((tr, WIDTH), lambda j, i: (i, j)),
                  pl.BlockSpec((HALO, WIDTH), lambda j, i: (jnp.maximum(i * (tr // HALO) - 1, 0), j)),
                  pl.BlockSpec((DN_CONV, WIDTH), lambda j, i: (0, j))],
        out_specs=(pl.BlockSpec((tr, WIDTH), lambda j, i: (i, j)),
                   pl.BlockSpec((DN_CONV, WIDTH), lambda j, i: (0, j))),
        compiler_params=_params(("parallel", "arbitrary")),
    )(dq, dk, dv, qkv_pre, qkv_pre, cw)


def _tri(n, kind):
    r, c = _iota((n, n), 0), _iota((n, n), 1)
    m = {"lower": r >= c, "strict": r > c, "upper": r <= c}[kind]
    return m


GATE_ROWS = 4 * DN_CHUNK


def _chunk_tri(kind):
    r, c = _iota((GATE_ROWS, GATE_ROWS), 0), _iota((GATE_ROWS, GATE_ROWS), 1)
    same = (r // DN_CHUNK) == (c // DN_CHUNK)
    return jnp.where(jnp.logical_and(same, _tri(GATE_ROWS, kind)), 1.0, 0.0).astype(BF16)


def _dn_gates_fwd(hab, alog, dtb, name):
    t = hab.shape[0]
    cc = GATE_ROWS

    def body(h_ref, al_ref, dt_ref, o_ref):
        hv = h_ref[...]
        lane = _iota(hv.shape, 1)
        xa = hv + dt_ref[...]
        sp = jnp.maximum(xa, 0.0) + _log1pexp_neg_abs(xa)
        g = jnp.where(lane < HEADS, -jnp.exp(al_ref[...]) * sp, 0.0)
        gc = _dot_xl(_chunk_tri("lower"), g, NN)
        o_ref[...] = jnp.where(lane < HEADS, gc, jnp.where(lane < 2 * HEADS, _sigmoid(hv), 0.0))

    return pl.pallas_call(
        body, name=name,
        out_shape=jax.ShapeDtypeStruct((t, 128), F32),
        grid=(t // cc,),
        in_specs=[pl.BlockSpec((cc, 128), lambda i: (i, 0)), pl.BlockSpec((1, 128), lambda i: (0, 0)),
                  pl.BlockSpec((1, 128), lambda i: (0, 0))],
        out_specs=pl.BlockSpec((cc, 128), lambda i: (i, 0)),
        compiler_params=_params(("parallel",)),
    )(hab, alog, dtb)


def _dn_gates_bwd(dgates, hab, alog, dtb, name):
    t = hab.shape[0]
    cc = GATE_ROWS

    def body(d_ref, h_ref, al_ref, dt_ref, o_ref, dal_ref, ddt_ref):
        i = pl.program_id(0)
        hv = h_ref[...]
        dv = d_ref[...]
        lane = _iota(hv.shape, 1)
        dg = _dot_xl(_chunk_tri("upper"), jnp.where(lane < HEADS, dv, 0.0), NN)
        xa = hv + dt_ref[...]
        sp = jnp.maximum(xa, 0.0) + _log1pexp_neg_abs(xa)
        ea = jnp.exp(al_ref[...])
        da = jnp.where(lane < HEADS, dg * (-ea) * _sigmoid(xa), 0.0)
        be = _sigmoid(hv)
        db = dv * be * (1.0 - be)
        o_ref[...] = jnp.where(lane < HEADS, da, jnp.where(lane < 2 * HEADS, db, 0.0))

        @pl.when(i == 0)
        def _():
            dal_ref[...] = jnp.zeros_like(dal_ref)
            ddt_ref[...] = jnp.zeros_like(ddt_ref)

        dal_ref[...] += jnp.sum(jnp.where(lane < HEADS, dg * (-ea) * sp, 0.0), axis=0, keepdims=True)
        ddt_ref[...] += jnp.sum(da, axis=0, keepdims=True)

    return pl.pallas_call(
        body, name=name,
        out_shape=(jax.ShapeDtypeStruct((t, 128), F32), jax.ShapeDtypeStruct((1, 128), F32),
                   jax.ShapeDtypeStruct((1, 128), F32)),
        grid=(t // cc,),
        in_specs=[pl.BlockSpec((cc, 128), lambda i: (i, 0)), pl.BlockSpec((cc, 128), lambda i: (i, 0)),
                  pl.BlockSpec((1, 128), lambda i: (0, 0)), pl.BlockSpec((1, 128), lambda i: (0, 0))],
        out_specs=(pl.BlockSpec((cc, 128), lambda i: (i, 0)), pl.BlockSpec((1, 128), lambda i: (0, 0)),
                   pl.BlockSpec((1, 128), lambda i: (0, 0))),
        compiler_params=_params(("arbitrary",)),
    )(dgates, hab, alog, dtb)


def _dn_chunk_common(gates, h):
    cc = DN_CHUNK
    lane = _iota(gates.shape, 1)
    gh = jnp.where(lane == h, gates, 0.0)
    gc_col = jnp.sum(gh, axis=1, keepdims=True)
    gc_row = _dot_xl(jnp.ones((cc, 128), BF16), gh, NT)
    beta = jnp.sum(jnp.where(lane == h + HEADS, gates, 0.0), axis=1, keepdims=True)
    lower = _tri(cc, "lower")
    decay = jnp.where(lower, jnp.exp(jnp.where(lower, gc_col - gc_row, 0.0)), 0.0)
    gc_last = gc_col[cc - 1:cc, :]
    return gc_col, gc_last, beta, decay


def _dn_local_fwd(act, gates, name):
    t = act.shape[0]
    cc = DN_CHUNK
    nc = t // cc

    per = LOCAL_CHUNKS

    def body(q_ref, k_ref, v_ref, g_ref, u_ref, w_ref, kd_ref, qg_ref, ti_ref, p_ref):
        eye = jnp.where(_iota((cc, cc), 0) == _iota((cc, cc), 1), 1.0, 0.0)
        units = [(c, h) for c in range(per) for h in range(HEADS)]
        us = range(len(units))
        rows = [slice(c * cc, (c + 1) * cc) for c, _ in units]
        sl = [slice(h * HEAD_DIM, (h + 1) * HEAD_DIM) for _, h in units]
        gates = [g_ref[c * cc:(c + 1) * cc, :] for c in range(per)]
        q, k, v = ([r[rows[i], sl[i]] for i in us] for r in (q_ref, k_ref, v_ref))
        gc_col, gc_last, beta, decay = zip(*[_dn_chunk_common(gates[c], h) for c, h in units])
        gam = [jnp.exp(g) for g in gc_col]
        kb = [k[i] * beta[i] for i in us]
        npow = [-jnp.where(_tri(cc, "strict"), _dotb(kb[i], k[i], NT) * decay[i], 0.0) for i in us]
        tinv = [eye + n for n in npow]
        for _ in range(5):
            npow = [_dot3(n, n, NN) for n in npow]
            tinv = [t + _dot3(t, n, NN) for t, n in zip(tinv, npow)]
        uu = [_dot3(tinv[i], v[i] * beta[i], NN) for i in us]
        ww = [_dot3(tinv[i], kb[i] * gam[i], NN) for i in us]
        pp = [jnp.where(_tri(cc, "lower"), _dotb(q[i], k[i], NT) * decay[i], 0.0) for i in us]
        for i, (_, h) in enumerate(units):
            u_ref[rows[i], sl[i]] = uu[i]
            w_ref[rows[i], sl[i]] = ww[i].astype(BF16)
            kd_ref[rows[i], sl[i]] = (k[i] * jnp.exp(gc_last[i] - gc_col[i])).astype(BF16)
            qg_ref[rows[i], sl[i]] = (q[i] * gam[i]).astype(BF16)
            ti_ref[h, rows[i], :] = tinv[i]
            p_ref[h, rows[i], :] = pp[i].astype(BF16)

    row = lambda off: pl.BlockSpec((per * cc, WIDTH), lambda n: (n, off))
    mat = pl.BlockSpec((HEADS, per * cc, cc), lambda n: (0, n, 0))
    tw, tb = jax.ShapeDtypeStruct((t, WIDTH), F32), jax.ShapeDtypeStruct((t, WIDTH), BF16)
    hm, hb = jax.ShapeDtypeStruct((HEADS, t, cc), F32), jax.ShapeDtypeStruct((HEADS, t, cc), BF16)
    return pl.pallas_call(
        body, name=name,
        out_shape=(tw, tb, tb, tb, hm, hb),
        grid=(nc // per,),
        in_specs=[row(0), row(1), row(2), pl.BlockSpec((per * cc, 128), lambda n: (n, 0))],
        out_specs=(row(0), row(0), row(0), row(0), mat, mat),
        compiler_params=_params(("parallel",)),
    )(act, act, act, gates)


def _dn_scan_fwd(u, w, kd, qg, p, gates, name):
    t = u.shape[0]
    cc = DN_CHUNK
    nc = t // cc
    per = SCAN_CHUNKS

    def body(u_ref, w_ref, kd_ref, qg_ref, p_ref, g_ref, o_ref, sh_ref, s_ref):
        n = pl.program_id(0)

        @pl.when(n == 0)
        def _():
            s_ref[...] = jnp.zeros_like(s_ref)

        hs = range(HEADS)
        sl = [slice(h * HEAD_DIM, (h + 1) * HEAD_DIM) for h in hs]
        s = [s_ref[h] for h in hs]
        for c in range(per):
            r = slice(c * cc, (c + 1) * cc)
            glast = jnp.exp(g_ref[(c + 1) * cc - 1:(c + 1) * cc, :])
            sb = [a.astype(BF16) for a in s]
            vn = [u_ref[r, sl[h]] - _dot(w_ref[r, sl[h]].astype(BF16), sb[h], NN) for h in hs]
            vnb = [a.astype(BF16) for a in vn]
            o_state = [_dot(qg_ref[r, sl[h]].astype(BF16), sb[h], NN) for h in hs]
            o_local = [_dot(p_ref[h, r, :].astype(BF16), vnb[h], NN) for h in hs]
            s_add = [_dot(kd_ref[r, sl[h]].astype(BF16), vnb[h], TN) for h in hs]
            for h in hs:
                o_ref[r, sl[h]] = o_state[h] + o_local[h]
                sh_ref[c, h] = sb[h]
            s = [glast[:, h:h + 1] * s[h] + s_add[h] for h in hs]
        for h in hs:
            s_ref[h] = s[h]

    row = pl.BlockSpec((per * cc, WIDTH), lambda n: (n, 0))
    return pl.pallas_call(
        body, name=name,
        out_shape=(jax.ShapeDtypeStruct((t, WIDTH), F32),
                   jax.ShapeDtypeStruct((nc, HEADS, HEAD_DIM, HEAD_DIM), BF16)),
        grid=(nc // per,),
        in_specs=[row, row, row, row, pl.BlockSpec((HEADS, per * cc, cc), lambda n: (0, n, 0)),
                  pl.BlockSpec((per * cc, 128), lambda n: (n, 0))],
        out_specs=(row, pl.BlockSpec((per, HEADS, HEAD_DIM, HEAD_DIM), lambda n: (n, 0, 0, 0))),
        scratch_shapes=[pltpu.VMEM((HEADS, HEAD_DIM, HEAD_DIM), F32)],
        compiler_params=_params(("arbitrary",)),
    )(u, w, kd, qg, p, gates)


def _dn_scan_bwd(do, w, kd, qg, p, gates, name):
    t = do.shape[0]
    cc = DN_CHUNK
    nc = t // cc
    per = SCAN_CHUNKS
    nb = nc // per

    def body(do_ref, w_ref, kd_ref, qg_ref, p_ref, g_ref, dvn_ref, dsh_ref, ds_ref):
        n = pl.program_id(0)

        @pl.when(n == 0)
        def _():
            ds_ref[...] = jnp.zeros_like(ds_ref)

        hs = range(HEADS)
        sl = [slice(h * HEAD_DIM, (h + 1) * HEAD_DIM) for h in hs]
        ds = [ds_ref[h] for h in hs]
        for c in reversed(range(per)):
            r = slice(c * cc, (c + 1) * cc)
            glast = jnp.exp(g_ref[(c + 1) * cc - 1:(c + 1) * cc, :])
            dob = [do_ref[r, sl[h]].astype(BF16) for h in hs]
            dvn = [_dot(p_ref[h, r, :].astype(BF16), dob[h], TN)
                   + _dot(kd_ref[r, sl[h]].astype(BF16), ds[h].astype(BF16), NN) for h in hs]
            ds_q = [_dot(qg_ref[r, sl[h]].astype(BF16), dob[h], TN) for h in hs]
            ds_w = [_dot(w_ref[r, sl[h]].astype(BF16), dvn[h].astype(BF16), TN) for h in hs]
            for h in hs:
                dvn_ref[r, sl[h]] = dvn[h]
                dsh_ref[c, h] = ds[h].astype(BF16)
            ds = [ds_q[h] + glast[:, h:h + 1] * ds[h] - ds_w[h] for h in hs]
        for h in hs:
            ds_ref[h] = ds[h]

    row = pl.BlockSpec((per * cc, WIDTH), lambda n: (nb - 1 - n, 0))
    return pl.pallas_call(
        body, name=name,
        out_shape=(jax.ShapeDtypeStruct((t, WIDTH), F32),
                   jax.ShapeDtypeStruct((nc, HEADS, HEAD_DIM, HEAD_DIM), BF16)),
        grid=(nb,),
        in_specs=[row, row, row, row, pl.BlockSpec((HEADS, per * cc, cc), lambda n: (0, nb - 1 - n, 0)),
                  pl.BlockSpec((per * cc, 128), lambda n: (nb - 1 - n, 0))],
        out_specs=(row, pl.BlockSpec((per, HEADS, HEAD_DIM, HEAD_DIM), lambda n: (nb - 1 - n, 0, 0, 0))),
        scratch_shapes=[pltpu.VMEM((HEADS, HEAD_DIM, HEAD_DIM), F32)],
        compiler_params=_params(("arbitrary",)),
    )(do, w, kd, qg, p, gates)


def _dn_local_bwd(act, gates, u, w, kd, qg, tinv, p, sh, dsh, dvn, do, name):
    t = act.shape[0]
    cc = DN_CHUNK
    nc = t // cc
    per = LOCAL_CHUNKS

    def body(q_ref, k_ref, v_ref, g_ref, u_ref, w_ref, kd_ref, qg_ref, ti_ref, p_ref, s_ref, ds_ref,
             dvn_ref, do_ref, dq_ref, dk_ref, dv_ref, dg_ref):
        lower, strict = _tri(cc, "lower"), _tri(cc, "strict")
        ones = jnp.ones((cc, 128), BF16)
        rowc = _iota((cc, 1), 0)
        lane = _iota((cc, 128), 1)
        units = [(c, h) for c in range(per) for h in range(HEADS)]
        hs = range(len(units))
        rows = [slice(c * cc, (c + 1) * cc) for c, _ in units]
        sl = [slice(h * HEAD_DIM, (h + 1) * HEAD_DIM) for _, h in units]
        gates_v = [g_ref[c * cc:(c + 1) * cc, :] for c in range(per)]
        q, k, v, uu, ww, kd, qg, dvn, do = ([r[rows[i], sl[i]] for i in hs] for r in (
            q_ref, k_ref, v_ref, u_ref, w_ref, kd_ref, qg_ref, dvn_ref, do_ref))
        gc_col, gc_last, beta, decay = zip(*[_dn_chunk_common(gates_v[c], h) for c, h in units])
        gam = [jnp.exp(g) for g in gc_col]
        kb = [k[h] * beta[h] for h in hs]
        s_in = [s_ref[c, h] for c, h in units]
        ds_out = [ds_ref[c, h] for c, h in units]
        tinv = [ti_ref[h, rows[i], :] for i, (_, h) in enumerate(units)]
        pmat = [p_ref[h, rows[i], :] for i, (_, h) in enumerate(units)]

        a = [jnp.where(strict, _dotb(kb[h], k[h], NT) * decay[h], 0.0) for h in hs]
        vn = [uu[h] - _dotb(ww[h], s_in[h], NN) for h in hs]
        dqg = [_dotb(do[h], s_in[h], NT) for h in hs]
        dw = [-_dotb(dvn[h], s_in[h], NT) for h in hs]
        dp = [jnp.where(lower, _dotb(do[h], vn[h], NT), 0.0) for h in hs]
        dkd = [_dotb(vn[h], ds_out[h], NT) for h in hs]
        dru = [_dot3(tinv[h], dvn[h], TN) for h in hs]
        drw = [_dot3(tinv[h], dw[h], TN) for h in hs]
        da = [-jnp.where(strict, _dotb(dru[h], uu[h], NT) + _dotb(drw[h], ww[h], NT), 0.0) for h in hs]
        dad = [da[h] * decay[h] for h in hs]
        dpd = [dp[h] * decay[h] for h in hs]
        dkb = [_dotb(dad[h], k[h], NN) + gam[h] * drw[h] for h in hs]
        dk = [_dotb(dad[h], kb[h], TN) + _dotb(dpd[h], q[h], TN) + beta[h] * dkb[h]
              + jnp.exp(gc_last[h] - gc_col[h]) * dkd[h] for h in hs]
        dq = [gam[h] * dqg[h] + _dotb(dpd[h], k[h], NN) for h in hs]
        gm = [da[h] * a[h] + dp[h] * pmat[h] for h in hs]
        colsum = [_dot_xr(gm[h], ones, TN)[:, 0:1] for h in hs]

        dgates = [jnp.zeros((cc, 128), F32)] * per
        for h, (c, head) in enumerate(units):
            dk_ref[rows[h], sl[h]] = dk[h]
            dq_ref[rows[h], sl[h]] = dq[h]
            dv_ref[rows[h], sl[h]] = beta[h] * dru[h]
            dbeta = (jnp.sum(dkb[h] * k[h], axis=1, keepdims=True)
                     + jnp.sum(dru[h] * v[h], axis=1, keepdims=True))
            rkd = jnp.sum(dkd[h] * kd[h], axis=1, keepdims=True)
            dgc = (jnp.sum(gm[h], axis=1, keepdims=True) - colsum[h]
                   + jnp.sum(dqg[h] * qg[h], axis=1, keepdims=True)
                   + jnp.sum(drw[h] * kb[h], axis=1, keepdims=True) * gam[h] - rkd)
            tail = jnp.sum(rkd, axis=0, keepdims=True) + jnp.exp(gc_last[h]) * jnp.sum(
                jnp.sum(s_in[h].astype(F32) * ds_out[h].astype(F32), axis=1, keepdims=True), axis=0, keepdims=True)
            dgc = dgc + jnp.where(rowc == cc - 1, tail, 0.0)
            dgates[c] = dgates[c] + jnp.where(lane == head, dgc, 0.0) + jnp.where(lane == head + HEADS, dbeta, 0.0)
        for c in range(per):
            dg_ref[c * cc:(c + 1) * cc, :] = dgates[c]

    row = lambda off: pl.BlockSpec((per * cc, WIDTH), lambda n: (n, off))
    mat = pl.BlockSpec((HEADS, per * cc, cc), lambda n: (0, n, 0))
    st = pl.BlockSpec((per, HEADS, HEAD_DIM, HEAD_DIM), lambda n: (n, 0, 0, 0))
    gl = pl.BlockSpec((per * cc, 128), lambda n: (n, 0))
    tw = jax.ShapeDtypeStruct((t, WIDTH), F32)
    return pl.pallas_call(
        body, name=name,
        out_shape=(tw, tw, tw, jax.ShapeDtypeStruct((t, 128), F32)),
        grid=(nc // per,),
        in_specs=[row(0), row(1), row(2), gl, row(0), row(0), row(0), row(0), mat, mat, st, st, row(0), row(0)],
        out_specs=(row(0), row(0), row(0), gl),
        compiler_params=_params(("parallel",)),
    )(act, act, act, gates, u, w, kd, qg, tinv, p, sh, dsh, dvn, do)


def _dn_post_fwd(o, gate, w, name):
    t = o.shape[0]
    tr = _tile(t, 512)

    def body(o_ref, g_ref, w_ref, y_ref):
        for h in range(HEADS):
            sl = slice(h * HEAD_DIM, (h + 1) * HEAD_DIM)
            ov, gv = o_ref[:, sl], g_ref[:, sl].astype(F32)
            r = lax.rsqrt(jnp.mean(ov * ov, axis=1, keepdims=True) + EPS)
            y_ref[:, sl] = (ov * r * w_ref[...] * (gv * _sigmoid(gv))).astype(BF16)

    blk = pl.BlockSpec((tr, WIDTH), lambda i: (i, 0))
    return pl.pallas_call(
        body, name=name,
        out_shape=jax.ShapeDtypeStruct((t, WIDTH), BF16),
        grid=(t // tr,),
        in_specs=[blk, blk, pl.BlockSpec((1, HEAD_DIM), lambda i: (0, 0))],
        out_specs=blk,
        compiler_params=_params(("parallel",)),
    )(o, gate, w)


def _dn_post_bwd(dy, o, gate, w, name):
    t = o.shape[0]
    tr = _tile(t, 512)

    def body(dy_ref, o_ref, g_ref, w_ref, do_ref, dg_ref, dw_ref):
        i = pl.program_id(0)

        @pl.when(i == 0)
        def _():
            dw_ref[...] = jnp.zeros_like(dw_ref)

        dw = jnp.zeros((1, HEAD_DIM), F32)
        for h in range(HEADS):
            sl = slice(h * HEAD_DIM, (h + 1) * HEAD_DIM)
            ov, gv, dyv = o_ref[:, sl], g_ref[:, sl].astype(F32), dy_ref[:, sl].astype(F32)
            r = lax.rsqrt(jnp.mean(ov * ov, axis=1, keepdims=True) + EPS)
            oh = ov * r
            sg = _sigmoid(gv)
            dg_ref[:, sl] = (dyv * oh * w_ref[...] * (sg * (1.0 + gv * (1.0 - sg)))).astype(BF16)
            dn = dyv * (gv * sg)
            doh = dn * w_ref[...]
            do_ref[:, sl] = r * (doh - oh * jnp.mean(doh * oh, axis=1, keepdims=True))
            dw = dw + jnp.sum(dn * oh, axis=0, keepdims=True)
        dw_ref[...] += dw

    blk = pl.BlockSpec((tr, WIDTH), lambda i: (i, 0))
    return pl.pallas_call(
        body, name=name,
        out_shape=(jax.ShapeDtypeStruct((t, WIDTH), F32), jax.ShapeDtypeStruct((t, WIDTH), BF16),
                   jax.ShapeDtypeStruct((1, HEAD_DIM), F32)),
        grid=(t // tr,),
        in_specs=[blk, blk, blk, pl.BlockSpec((1, HEAD_DIM), lambda i: (0, 0))],
        out_specs=(blk, blk, pl.BlockSpec((1, HEAD_DIM), lambda i: (0, 0))),
        compiler_params=_params(("arbitrary",)),
    )(dy, o, gate, w)


def _sb_scores(qs, k_ref, qi, it, carries, uincl):
    bk = ATT_BLOCK
    scale = HEAD_DIM ** -0.5
    heads, groups = range(len(qs)), range(SB_GROUP)
    lane = [slice(e * HEAD_DIM, (e + 1) * HEAD_DIM) for e in heads]
    js = [qi - SB_GROUP * it - g for g in groups]
    rows = [pl.ds(pl.multiple_of(jnp.maximum(j, 0) * bk, bk), bk) for j in js]
    qpos = qi * bk + _iota((bk, bk), 0)
    col = _iota((bk, bk), 1)
    mask1 = [jnp.logical_and(j * bk + col < qpos, j >= 0) for j in js]
    ks = [[k_ref[r, lane[e]] for r in rows] for e in heads]
    z = [[_dot(qs[e], k, NT) * scale for k in ks[e]] for e in heads]
    soft = [[_log1pexp_neg_abs(a) for a in ze] for ze in z]
    lk_full = [[-(jnp.maximum(a, 0.0) + s) for a, s in zip(z[e], soft[e])] for e in heads]
    lk = [[jnp.where(m, a, 0.0) for m, a in zip(mask1, lk_full[e])] for e in heads]
    ls = [[jnp.minimum(a, 0.0) - s for a, s in zip(z[e], soft[e])] for e in heads]
    incl = [[_dot_xr2(a, uincl, NN) for a in lk[e]] for e in heads]
    weights, out_carries = [], []
    for e in heads:
        cb, we = carries[e], []
        for g in groups:
            we.append(jnp.where(mask1[g], jnp.exp(ls[e][g] + (cb + incl[e][g] - lk[e][g])), 0.0))
            cb = cb + incl[e][g][:, 0:1]
        weights.append(we)
        out_carries.append(cb)
    return rows, ks, weights, mask1, lk_full, ls, out_carries


def _sb_more(qi, carry):
    it, cbs = carry[0], carry[1]
    live = jnp.max(cbs[0])
    for cb in cbs[1:]:
        live = jnp.maximum(live, jnp.max(cb))
    return jnp.logical_and(SB_GROUP * it <= qi, live > SB_LOG_ZERO)


def _sb_steps(groups, nq):
    def when():
        h, i = pl.program_id(0), pl.program_id(1)
        return (jnp.logical_and(h == 0, i == 0), jnp.logical_and(h == groups // 2, i == 0),
                jnp.logical_and(h == groups - 1, i == nq - 1))
    return when


def _sb_fwd(qkv, name, comm=None):
    t = qkv.shape[0]
    bk = ATT_BLOCK
    hp, wide = SB_HEADS_FWD, SB_HEADS_FWD * HEAD_DIM
    lane = [slice(e * HEAD_DIM, (e + 1) * HEAD_DIM) for e in range(hp)]

    def body(q_ref, k_ref, v_ref, o_ref):
        qi = pl.program_id(1)
        qs = [q_ref[:, s] for s in lane]
        uincl = jnp.where(_tri(bk, "lower"), 1.0, 0.0).astype(BF16)

        def step(carry):
            it, cbs, accs = carry
            rows, _, weights, _, _, _, cbs = _sb_scores(qs, k_ref, qi, it, cbs, uincl)
            accs = list(accs)
            for e in range(hp):
                for r, a in zip(rows, weights[e]):
                    accs[e] = accs[e] + _dot(a.astype(BF16), v_ref[r, lane[e]], NN)
            return it + 1, tuple(cbs), tuple(accs)

        init = (jnp.int32(0), (jnp.zeros((bk, 1), F32),) * hp, (jnp.zeros((bk, HEAD_DIM), F32),) * hp)
        _, _, accs = lax.while_loop(functools.partial(_sb_more, qi), step, init)
        for e in range(hp):
            o_ref[:, lane[e]] = accs[e]

    groups = HEADS // hp
    (o,), extra = _host_call(
        body, name, comm, _sb_steps(groups, t // bk), [jax.ShapeDtypeStruct((t, WIDTH), F32)], (groups, t // bk),
        [pl.BlockSpec((bk, wide), lambda h, i: (i, h)),
         pl.BlockSpec((t, wide), lambda h, i: (0, groups + h)),
         pl.BlockSpec((t, wide), lambda h, i: (0, 2 * groups + h))],
        [pl.BlockSpec((bk, wide), lambda h, i: (i, h))], [], ("parallel", "arbitrary"), (qkv, qkv, qkv))
    return o, extra


def _sb_bwd(qkv, o, do, name, comm=None):
    assert do.dtype == BF16
    t = qkv.shape[0]
    bk = ATT_BLOCK
    scale = HEAD_DIM ** -0.5
    hp, wide = SB_HEADS_BWD, SB_HEADS_BWD * HEAD_DIM
    lane = [slice(e * HEAD_DIM, (e + 1) * HEAD_DIM) for e in range(hp)]

    def body(q_ref, k_ref, v_ref, o_ref, do_ref, dq_ref, dk_out, dv_out, dk_ref, dv_ref):
        qi = pl.program_id(1)

        @pl.when(qi == 0)
        def _():
            dk_ref[...] = jnp.zeros_like(dk_ref)
            dv_ref[...] = jnp.zeros_like(dv_ref)

        heads, groups = range(hp), range(SB_GROUP)
        qs = [q_ref[:, s] for s in lane]
        dob = [do_ref[:, s] for s in lane]
        dsum = [jnp.sum(dob[e].astype(F32) * o_ref[:, lane[e]], axis=1, keepdims=True) for e in heads]
        uincl = jnp.where(_tri(bk, "lower"), 1.0, 0.0).astype(BF16)

        def step(carry):
            it, cbs, ces, dqs = carry
            rows, ks, weights, mask, lk_full, ls, cbs = _sb_scores(qs, k_ref, qi, it, cbs, uincl)
            ab = [[a.astype(BF16) for a in weights[e]] for e in heads]
            vs = [[v_ref[r, lane[e]] for r in rows] for e in heads]
            dla = [[ab[e][g].astype(F32) * _dot(dob[e], vs[e][g], NT) for g in groups] for e in heads]
            suf = [[_dot_xr2(a, uincl, NN) for a in dla[e]] for e in heads]
            ces, dqs = list(ces), list(dqs)
            for e in heads:
                for g in groups:
                    err = dsum[e] - (ces[e] + suf[e][g])
                    ces[e] = ces[e] + suf[e][g][:, 0:1]
                    dz = jnp.where(mask[g], dla[e][g] * jnp.exp(lk_full[e][g]) - err * jnp.exp(ls[e][g]), 0.0)
                    dzb = (dz * scale).astype(BF16)
                    dqs[e] = dqs[e] + _dot(dzb, ks[e][g], NN)
                    dk_ref[rows[g], lane[e]] += _dot(dzb, qs[e], TN)
                    dv_ref[rows[g], lane[e]] += _dot(ab[e][g], dob[e], TN)
            return it + 1, tuple(cbs), tuple(ces), tuple(dqs)

        zc = (jnp.zeros((bk, 1), F32),) * hp
        init = (jnp.int32(0), zc, zc, (jnp.zeros((bk, HEAD_DIM), F32),) * hp)
        dqs = lax.while_loop(functools.partial(_sb_more, qi), step, init)[3]
        for e in heads:
            dq_ref[:, lane[e]] = dqs[e].astype(BF16)

        @pl.when(qi == t // bk - 1)
        def _():
            dk_out[...] = dk_ref[...].astype(BF16)
            dv_out[...] = dv_ref[...].astype(BF16)

    ngroup = HEADS // hp
    tw = jax.ShapeDtypeStruct((t, WIDTH), BF16)
    qb = pl.BlockSpec((bk, wide), lambda h, i: (i, h))
    full = lambda off: pl.BlockSpec((t, wide), lambda h, i: (0, off + h))
    return _host_call(
        body, name, comm, _sb_steps(ngroup, t // bk), [tw, tw, tw], (ngroup, t // bk),
        [qb, full(ngroup), full(2 * ngroup), qb, qb], [qb, full(0), full(0)],
        [pltpu.VMEM((t, wide), F32), pltpu.VMEM((t, wide), F32)], ("parallel", "arbitrary"),
        (qkv, qkv, qkv, o, do))


def _merge_fwd(pd, ps, gl, name):
    t = pd.shape[0]
    tr, tc = _tile(t, 512), 512
    nj = D_MODEL // tc

    def body(pd_ref, ps_ref, gd_ref, gs_ref, o_ref):
        gd, gs = gd_ref[...].astype(F32), gs_ref[...].astype(F32)
        o_ref[...] = (_sigmoid(gd) * pd_ref[...].astype(F32) + _sigmoid(gs) * ps_ref[...].astype(F32)).astype(BF16)

    blk = lambda off: pl.BlockSpec((tr, tc), lambda i, j: (i, j + off))
    return pl.pallas_call(
        body, name=name,
        out_shape=jax.ShapeDtypeStruct((t, D_MODEL), BF16),
        grid=(t // tr, nj),
        in_specs=[blk(0), blk(0), blk(0), blk(nj)],
        out_specs=blk(0),
        compiler_params=_params(("parallel", "parallel")),
    )(pd, ps, gl, gl)


def _merge_bwd(dm, pd, ps, gl, name):
    t = pd.shape[0]
    tr, tc = _tile(t, 512), 512
    nj = D_MODEL // tc

    def body(dm_ref, pd_ref, ps_ref, gd_ref, gs_ref, dpd_ref, dps_ref, dgd_ref, dgs_ref):
        dmv = dm_ref[...].astype(F32)
        sd, ss = _sigmoid(gd_ref[...].astype(F32)), _sigmoid(gs_ref[...].astype(F32))
        dpd_ref[...] = (dmv * sd).astype(BF16)
        dps_ref[...] = (dmv * ss).astype(BF16)
        dgd_ref[...] = (dmv * pd_ref[...].astype(F32) * sd * (1.0 - sd)).astype(BF16)
        dgs_ref[...] = (dmv * ps_ref[...].astype(F32) * ss * (1.0 - ss)).astype(BF16)

    blk = lambda off: pl.BlockSpec((tr, tc), lambda i, j: (i, j + off))
    out = jax.ShapeDtypeStruct((t, D_MODEL), BF16)
    return pl.pallas_call(
        body, name=name,
        out_shape=(out, out, out, out),
        grid=(t // tr, nj),
        in_specs=[blk(0), blk(0), blk(0), blk(0), blk(nj)],
        out_specs=(blk(0), blk(0), blk(0), blk(0)),
        compiler_params=_params(("parallel", "parallel")),
    )(dm, pd, ps, gl, gl)


def _local_step(x, target, wts, plan=None, n1=None):
    if n1 is None:
        n1 = _rmsnorm_fwd(x, wts["norm1_w"], "norm1_fwd")
    qkv_pre = _matmul(n1, wts["w_dnqkv_t"], "nt", BF16, "in_dnqkv")
    hgate = _matmul(n1, wts["w_dngate_t"], "nt", BF16, "in_dngate")
    sbqkv = _matmul(n1, wts["w_sbqkv_t"], "nt", BF16, "in_sbqkv")
    gl = _matmul(n1, wts["w_gl_t"], "nt", BF16, "in_gl")
    hab = _matmul(n1, wts["w_ab_t"], "nt", F32, "in_ab")

    act = _dn_pre_fwd(qkv_pre, wts["dn_conv_w"], "dn_pre_fwd")
    gates = _dn_gates_fwd(hab, wts["alog"], wts["dtb"], "dn_gates_fwd")
    u, w, kd, qg, tinv, p = _dn_local_fwd(act, gates, "dn_local_fwd")
    o_dn, sh = _dn_scan_fwd(u, w, kd, qg, p, gates, "dn_scan_fwd")
    y_dn = _dn_post_fwd(o_dn, hgate, wts["dn_norm_w"], "dn_post_fwd")

    o_sb, late = _sb_fwd(sbqkv, "sb_fwd", comm=plan.late_gather() if plan else None)
    if plan:
        wts = {**wts, **plan.late_weights(late)}

    pd = _matmul(y_dn, wts["w_proj_dn"], "nn", BF16, "proj_dn")
    ps = _matmul(o_sb, wts["w_proj_sb"], "nn", BF16, "proj_sb")
    mixed = _merge_fwd(pd, ps, gl, "merge_fwd")
    x1 = _matmul(mixed, wts["w_out"], "nn", F32, "out_proj", add=x)

    n2 = _rmsnorm_fwd(x1, wts["norm2_w"], "norm2_fwd")
    upre = _matmul(n2, wts["ffn_w_up_t"], "nt", BF16, "ffn_up")
    fact = _ffn_act_fwd(upre, wts["ffn_conv_w"], "ffn_act_fwd")
    x2 = _matmul(fact, wts["ffn_w_down"], "nn", F32, "ffn_down", add=x1)

    dx2, g_normf, loss = _final_loss(x2, target, wts["norm_f_w"], "final_loss")

    dfact = _matmul(dx2, wts["ffn_w_down"], "nt", BF16, "ffn_down_dx")
    g_wdown = _matmul(fact, dx2, "tn", BF16, "ffn_down_dw")
    dgc, duc, dwg, dwu = _ffn_act_bwd(dfact, upre, wts["ffn_conv_w"], "ffn_act_bwd")
    g_fconv = jnp.concatenate([dwg, dwu], axis=1)
    dupre = _conv_bwd_data([dgc, duc], wts["ffn_conv_w"], FFN_CONV, BF16, "ffn_conv_bwd")
    dn2 = _matmul(dupre, wts["ffn_w_up_t"], "nn", F32, "ffn_up_dx")
    g_wup = _matmul(dupre, n2, "tn", BF16, "ffn_up_dw")
    dx1, g_norm2 = _rmsnorm_bwd(dn2, x1, wts["norm2_w"], dx2, "norm2_bwd")

    dmixed = _matmul(dx1, wts["w_out"], "nt", BF16, "out_proj_dx")
    g_wout = _matmul(mixed, dx1, "tn", BF16, "out_proj_dw")
    dpd, dps, dgd, dgs = _merge_bwd(dmixed, pd, ps, gl, "merge_bwd")
    dy_dn = _matmul(dpd, wts["w_proj_dn"], "nt", BF16, "proj_dn_dx")
    g_wpd = _matmul(y_dn, dpd, "tn", BF16, "proj_dn_dw")
    do_sb = _matmul(dps, wts["w_proj_sb"], "nt", BF16, "proj_sb_dx")
    g_wps = _matmul(o_sb, dps, "tn", BF16, "proj_sb_dw")
    grads = dict(w_proj_dn=g_wpd, w_proj_sb=g_wps, w_out=g_wout, ffn_w_up_t=g_wup, ffn_w_down=g_wdown)

    (dsq, dsk, dsv), got_early = _sb_bwd(sbqkv, o_sb, do_sb, "sb_bwd",
                                         comm=plan.early_grads(grads) if plan else None)

    do_dn, dhgate, g_dnnorm = _dn_post_bwd(dy_dn, o_dn, hgate, wts["dn_norm_w"], "dn_post_bwd")
    dvn, dsh = _dn_scan_bwd(do_dn, w, kd, qg, p, gates, "dn_scan_bwd")
    dq, dk, dv, dgates = _dn_local_bwd(act, gates, u, w, kd, qg, tinv, p, sh, dsh, dvn, do_dn, "dn_local_bwd")
    dhab, g_alog, g_dtb = _dn_gates_bwd(dgates, hab, wts["alog"], wts["dtb"], "dn_gates_bwd")
    dcv, g_dnconv = _dn_pre_bwd(dq, dk, dv, qkv_pre, wts["dn_conv_w"], "dn_pre_bwd")
    dqkv_pre = _conv_bwd_data([dcv], wts["dn_conv_w"], DN_CONV, BF16, "dn_conv_bwd")

    dh = jnp.concatenate([dqkv_pre, dhgate, dsq, dsk, dsv, dgd, dgs], axis=1)
    w_main_t = jnp.concatenate([wts["w_dnqkv_t"], wts["w_dngate_t"], wts["w_sbqkv_t"], wts["w_gl_t"]], axis=0)
    g_wmain = _matmul(dh, n1, "tn", BF16, "in_dw_main")
    g_wab = _matmul(dhab, n1, "tn", BF16, "in_dw_ab")
    grads.update(w_main_t=g_wmain, w_ab_t=g_wab, dn_conv_w=g_dnconv, alog=g_alog, dtb=g_dtb, dn_norm_w=g_dnnorm,
                 norm2_w=g_norm2, ffn_conv_w=g_fconv, norm_f_w=g_normf)
    got_late = []
    if plan:
        dn1, swapped = _matmul(dhab, wts["w_ab_t"], "nn", F32, "in_dx_ab", comm=plan.sibling_swap(grads))
        dn1, got_late = _matmul(dh, w_main_t, "nn", F32, "in_dx_main", add=dn1,
                                comm=plan.late_grads(swapped, grads, loss))
    else:
        dn1 = _matmul(dhab, wts["w_ab_t"], "nn", F32, "in_dx_ab")
        dn1 = _matmul(dh, w_main_t, "nn", F32, "in_dx_main", add=dn1)
    grad_x, g_norm1 = _rmsnorm_bwd(dn1, x, wts["norm1_w"], dx1, "norm1_bwd")
    grads["norm1_w"] = g_norm1
    return loss, grad_x, grads, got_early, got_late


HBM_SPEC = pl.BlockSpec(memory_space=pltpu.HBM)


def _mesh_pos():
    x, y, c = lax.axis_index("x"), lax.axis_index("y"), lax.axis_index("c")
    return x, y, c, 4 * x + 2 * y + c


def _peer(k):
    x, y, c, _ = _mesh_pos()
    px = 1 - x if k & 4 else x
    py = 1 - y if k & 2 else y
    pc = 1 - c if k & 1 else c
    return (px, py, pc), 4 * px + 2 * py + pc


def _rcopy(src, dst, send, recv, a, s, peer):
    return pltpu.make_async_remote_copy(src_ref=src, dst_ref=dst, send_sem=send.at[a, s], recv_sem=recv.at[a, s],
                                        device_id=peer, device_id_type=pl.DeviceIdType.MESH)


class _Gather:
    ICI = (2, 4, 6)

    def __init__(self, shards):
        self.args = list(shards)
        self.n = len(shards)
        self.out_shape = [jax.ShapeDtypeStruct((N_DEV,) + s.shape, s.dtype) for s in shards]
        self.scratch = [pltpu.SemaphoreType.DMA((self.n, N_DEV - 1)), pltpu.SemaphoreType.DMA((self.n, N_DEV - 1)),
                        pltpu.SemaphoreType.DMA((self.n,))]

    def _slot(self, outs, a, d):
        return outs[a].at[d]

    def _first(self, ins, outs, send, recv, a):
        me = _mesh_pos()[3]
        out, got = [], []
        for s, k in enumerate((1,) + self.ICI):
            peer, pidx = _peer(k)
            out.append(_rcopy(ins[a], self._slot(outs, a, me), send, recv, a, s, peer))
            got.append(_rcopy(ins[a], self._slot(outs, a, pidx), send, recv, a, s, peer))
        return out, got

    def _forward(self, ins, outs, send, recv, a):
        sib = _peer(1)[0]
        out, got = [], []
        for s, k in enumerate(self.ICI):
            held = self._slot(outs, a, _peer(k)[1])
            out.append(_rcopy(held, held, send, recv, a, 4 + s, sib))
            other = self._slot(outs, a, _peer(k | 1)[1])
            got.append(_rcopy(other, other, send, recv, a, 4 + s, sib))
        return out, got

    def start(self, ins, outs, sems):
        send, recv, loc = sems
        me = _mesh_pos()[3]
        for a in range(self.n):
            pltpu.make_async_copy(ins[a], self._slot(outs, a, me), loc.at[a]).start()
            for cp in self._first(ins, outs, send, recv, a)[0]:
                cp.start()

    def mid(self, ins, outs, sems):
        send, recv, _ = sems
        for a in range(self.n):
            arrivals = self._first(ins, outs, send, recv, a)[1]
            for s, cp in enumerate(self._forward(ins, outs, send, recv, a)[0]):
                arrivals[1 + s].wait_recv()
                cp.start()

    def finish(self, ins, outs, sems):
        send, recv, loc = sems
        me = _mesh_pos()[3]
        for a in range(self.n):
            first_out, first_got = self._first(ins, outs, send, recv, a)
            fwd_out, fwd_got = self._forward(ins, outs, send, recv, a)
            first_got[0].wait_recv()
            for cp in fwd_got:
                cp.wait_recv()
            for cp in first_out + fwd_out:
                cp.wait_send()
            pltpu.make_async_copy(ins[a], self._slot(outs, a, me), loc.at[a]).wait()


class _Exchange:
    def __init__(self, slabs=(), gathered=(), chip_slabs=(), sibling_slabs=()):
        self.args = list(slabs) + list(chip_slabs) + list(sibling_slabs) + list(gathered)
        self.kind = (["dev"] * len(slabs) + ["chip"] * len(chip_slabs) + ["sib"] * len(sibling_slabs)
                     + ["all"] * len(gathered))
        self.n = len(self.args)
        half = lambda s: jax.ShapeDtypeStruct((N_DEV // 2,) + s.shape[1:], s.dtype)
        self.out_shape = ([jax.ShapeDtypeStruct(s.shape, s.dtype) for s in slabs]
                          + [half(s) for s in chip_slabs] + [half(s) for s in sibling_slabs]
                          + [jax.ShapeDtypeStruct((N_DEV,) + s.shape, s.dtype) for s in gathered])
        self.scratch = [pltpu.SemaphoreType.DMA((self.n, N_DEV - 1)), pltpu.SemaphoreType.DMA((self.n, N_DEV - 1)),
                        pltpu.SemaphoreType.DMA((self.n,))]

    def _copies(self, ins, outs, send, recv, a):
        x, y, c, me = _mesh_pos()
        kind = self.kind[a]
        out, got = [], []
        if kind == "sib":
            sib = _peer(1)[0]
            for q in range(N_DEV // 2):
                out.append(_rcopy(ins[a].at[2 * q + 1 - c], outs[a].at[q], send, recv, a, q, sib))
                got.append(_rcopy(ins[a].at[2 * q + c], outs[a].at[q], send, recv, a, q, sib))
            return out, got
        for k in ((2, 4, 6) if kind == "chip" else range(1, N_DEV)):
            peer, pidx = _peer(k)
            if kind == "chip":
                src, mine, theirs = ins[a].at[2 * peer[0] + peer[1]], 2 * x + y, 2 * peer[0] + peer[1]
            else:
                src, mine, theirs = (ins[a].at[pidx] if kind == "dev" else ins[a]), me, pidx
            out.append(_rcopy(src, outs[a].at[mine], send, recv, a, k - 1, peer))
            got.append(_rcopy(src, outs[a].at[theirs], send, recv, a, k - 1, peer))
        return out, got

    def _local(self, ins, outs, loc, a):
        x, y, _, me = _mesh_pos()
        kind = self.kind[a]
        if kind == "sib":
            return None
        if kind == "chip":
            return pltpu.make_async_copy(ins[a].at[2 * x + y], outs[a].at[2 * x + y], loc.at[a])
        return pltpu.make_async_copy(ins[a].at[me] if kind == "dev" else ins[a], outs[a].at[me], loc.at[a])

    def start(self, ins, outs, sems):
        send, recv, loc = sems
        for a in range(self.n):
            if self._local(ins, outs, loc, a) is not None:
                self._local(ins, outs, loc, a).start()
            for cp in self._copies(ins, outs, send, recv, a)[0]:
                cp.start()

    def mid(self, ins, outs, sems):
        pass

    def finish(self, ins, outs, sems):
        send, recv, loc = sems
        for a in range(self.n):
            out, got = self._copies(ins, outs, send, recv, a)
            for cp in got:
                cp.wait_recv()
            for cp in out:
                cp.wait_send()
            if self._local(ins, outs, loc, a) is not None:
                self._local(ins, outs, loc, a).wait()


def _comm_call(comm, name):
    n = comm.n

    def body(*refs):
        ins, outs, sems = refs[:n], refs[n:2 * n], refs[2 * n:]
        comm.start(ins, outs, sems)
        comm.mid(ins, outs, sems)
        comm.finish(ins, outs, sems)

    return pl.pallas_call(
        body, name=name, out_shape=comm.out_shape, in_specs=[HBM_SPEC] * n, out_specs=[HBM_SPEC] * n,
        scratch_shapes=comm.scratch,
    )(*comm.args)


def _hosted(body, comm, n_in, n_out, when):
    if comm is None:
        return body

    def wrapped(*refs):
        ins, c_ins = refs[:n_in], refs[n_in:n_in + comm.n]
        o0 = n_in + comm.n
        outs, c_outs = refs[o0:o0 + n_out], refs[o0 + n_out:o0 + n_out + comm.n]
        scratch, sems = refs[o0 + n_out + comm.n:len(refs) - 3], refs[len(refs) - 3:]
        first, middle, last = when()

        @pl.when(first)
        def _():
            comm.start(c_ins, c_outs, sems)

        body(*ins, *outs, *scratch)

        @pl.when(middle)
        def _():
            comm.mid(c_ins, c_outs, sems)

        @pl.when(last)
        def _():
            comm.finish(c_ins, c_outs, sems)

    return wrapped


def _host_call(body, name, comm, when, out_shape, grid, in_specs, out_specs, scratch_shapes, sem, args):
    n_in, n_out = len(in_specs), len(out_specs)
    if comm is None:
        res = pl.pallas_call(body, name=name, out_shape=out_shape, grid=grid, in_specs=in_specs, out_specs=out_specs,
                             scratch_shapes=scratch_shapes, compiler_params=_params(sem))(*args)
        return list(res), []
    res = pl.pallas_call(
        _hosted(body, comm, n_in, n_out, when), name=name,
        out_shape=list(out_shape) + comm.out_shape, grid=grid,
        in_specs=list(in_specs) + [HBM_SPEC] * comm.n, out_specs=list(out_specs) + [HBM_SPEC] * comm.n,
        scratch_shapes=list(scratch_shapes) + comm.scratch,
        compiler_params=_params(("arbitrary",) * len(grid)),
    )(*args, *comm.args)
    return list(res[:n_out]), list(res[n_out:])


def _add_my_slabs(slabs, b, name):
    n, rows, cols = b.shape
    tc = _tile(cols, 256)

    def body(a_ref, b_ref, o_ref):
        o_ref[...] = (a_ref[...].astype(F32) + b_ref[...].astype(F32)).astype(o_ref.dtype)

    blk = pl.BlockSpec((None, rows, tc), lambda i, j: (i, 0, j))
    mine = pl.BlockSpec((None, rows, tc), lambda i, j: (2 * i + lax.axis_index("c"), 0, j))
    return pl.pallas_call(
        body, name=name, out_shape=jax.ShapeDtypeStruct(b.shape, b.dtype), grid=(n, cols // tc),
        in_specs=[mine, blk], out_specs=blk, compiler_params=_params(("parallel", "parallel")),
    )(slabs, b)


def _adamw(parts, w, m, v, name):
    rows, cols = w.shape
    nparts = parts.shape[0]
    tr, tc = rows, cols
    for cand in (128, 176):
        if rows > cand and rows % cand == 0:
            tr = cand
            break
    if tr == rows and rows > 512:
        tc = _tile(cols, 256)

    def body(p_ref, w_ref, m_ref, v_ref, g_ref, d_ref, mo_ref, vo_ref):
        g = p_ref[0].astype(F32)
        for s in range(1, nparts):
            g = g + p_ref[s].astype(F32)
        mn = ADAM_B1 * m_ref[...] + (1.0 - ADAM_B1) * g
        vn = ADAM_B2 * v_ref[...] + (1.0 - ADAM_B2) * (g * g)
        m_hat = mn / (1.0 - ADAM_B1 ** ADAM_STEP)
        v_hat = vn / (1.0 - ADAM_B2 ** ADAM_STEP)
        g_ref[...] = g
        d_ref[...] = -ADAM_LR * (m_hat / (jnp.sqrt(v_hat) + ADAM_EPS) + ADAM_WD * w_ref[...])
        mo_ref[...] = mn
        vo_ref[...] = vn

    blk = pl.BlockSpec((tr, tc), lambda i, j: (i, j))
    out = jax.ShapeDtypeStruct((rows, cols), F32)
    return pl.pallas_call(
        body, name=name,
        out_shape=(out, out, out, out),
        grid=(rows // tr, cols // tc),
        in_specs=[pl.BlockSpec((nparts, tr, tc), lambda i, j: (0, i, j)), blk, blk, blk],
        out_specs=(blk, blk, blk, blk),
        compiler_params=_params(("parallel", "parallel")),
    )(parts, w, m, v)


CONV_PACK = 8 * 1024
WEIGHT_ORDER = ("norm1_w", "w_in", "dn_conv_w", "dn_A_log", "dn_dt_bias", "dn_norm_w", "w_proj_dn", "w_proj_sb",
                "w_out", "norm2_w", "ffn_w_up", "ffn_conv_w", "ffn_w_down", "norm_f_w")


def _cols_to_slabs(g):
    r, c8 = g.shape
    return g.reshape(r, N_DEV, c8 // N_DEV).transpose(1, 0, 2)


def _slabs_to_cols(s):
    d, r, c = s.shape
    return s.transpose(1, 0, 2).reshape(r, d * c)


def kernel(x, norm1_w, w_in, dn_conv_w, dn_A_log, dn_dt_bias, dn_norm_w, w_proj_dn, w_proj_sb, w_out, norm2_w, ffn_w_up, ffn_conv_w, ffn_w_down, norm_f_w, loss_target, m_norm1_w, m_w_in, m_dn_conv_w, m_dn_A_log, m_dn_dt_bias, m_dn_norm_w, m_w_proj_dn, m_w_proj_sb, m_w_out, m_norm2_w, m_ffn_w_up, m_ffn_conv_w, m_ffn_w_down, m_norm_f_w, v_norm1_w, v_w_in, v_dn_conv_w, v_dn_A_log, v_dn_dt_bias, v_dn_norm_w, v_w_proj_dn, v_w_proj_sb, v_w_out, v_norm2_w, v_ffn_w_up, v_ffn_conv_w, v_ffn_w_down, v_norm_f_w):
    me = _mesh_pos()[3]
    tr = lambda a: jnp.transpose(a[0])
    w_loc = dict(norm1_w=norm1_w, w_in=tr(w_in), dn_conv_w=dn_conv_w[0], dn_A_log=dn_A_log, dn_dt_bias=dn_dt_bias,
                 dn_norm_w=dn_norm_w, w_proj_dn=w_proj_dn[0], w_proj_sb=w_proj_sb[0], w_out=w_out[0],
                 norm2_w=norm2_w, ffn_w_up=tr(ffn_w_up), ffn_conv_w=ffn_conv_w[0], ffn_w_down=ffn_w_down[0],
                 norm_f_w=norm_f_w[None, :])
    m_loc = dict(norm1_w=m_norm1_w, w_in=tr(m_w_in), dn_conv_w=m_dn_conv_w[0], dn_A_log=m_dn_A_log,
                 dn_dt_bias=m_dn_dt_bias, dn_norm_w=m_dn_norm_w, w_proj_dn=m_w_proj_dn[0], w_proj_sb=m_w_proj_sb[0],
                 w_out=m_w_out[0], norm2_w=m_norm2_w, ffn_w_up=tr(m_ffn_w_up), ffn_conv_w=m_ffn_conv_w[0],
                 ffn_w_down=m_ffn_w_down[0], norm_f_w=m_norm_f_w[None, :])
    v_loc = dict(norm1_w=v_norm1_w, w_in=tr(v_w_in), dn_conv_w=v_dn_conv_w[0], dn_A_log=v_dn_A_log,
                 dn_dt_bias=v_dn_dt_bias, dn_norm_w=v_dn_norm_w, w_proj_dn=v_w_proj_dn[0], w_proj_sb=v_w_proj_sb[0],
                 w_out=v_w_out[0], norm2_w=v_norm2_w, ffn_w_up=tr(v_ffn_w_up), ffn_conv_w=v_ffn_conv_w[0],
                 ffn_w_down=v_ffn_w_down[0], norm_f_w=v_norm_f_w[None, :])

    conv_flat = jnp.concatenate([w_loc["dn_conv_w"].reshape(-1), w_loc["ffn_conv_w"].reshape(-1)])
    n_dn, n_ffn = DN_CONV * 3 * WIDTH // N_DEV, FFN_CONV * 2 * D_FF // N_DEV
    conv_pack = jnp.pad(conv_flat, (0, CONV_PACK - n_dn - n_ffn)).reshape(8, 1024)
    n1, (g_in, g_conv) = _rmsnorm_fwd(x[0], norm1_w, "norm1_fwd",
                                      comm=_Gather([w_loc["w_in"].astype(BF16), conv_pack]))
    in_width = g_in.shape[0] * g_in.shape[1]
    w_in_t = g_in.reshape(in_width, D_MODEL)
    g_conv = g_conv.reshape(N_DEV, CONV_PACK)
    dn_conv_full = _slabs_to_cols(g_conv[:, :n_dn].reshape(N_DEV, DN_CONV, 3 * WIDTH // N_DEV))
    ffn_conv_full = _slabs_to_cols(g_conv[:, n_dn:n_dn + n_ffn].reshape(N_DEV, FFN_CONV, 2 * D_FF // N_DEV))
    q_end = 3 * WIDTH
    ab_end = q_end + 2 * HEADS
    gate_end = ab_end + WIDTH
    sb_end = gate_end + 3 * WIDTH
    pad_lanes = lambda a: jnp.pad(a, ((0, 0), (0, 128 - a.shape[1])))
    wts = dict(
        norm1_w=norm1_w, w_dnqkv_t=w_in_t[:q_end], w_ab_t=jnp.pad(w_in_t[q_end:ab_end], ((0, 128 - 2 * HEADS), (0, 0))),
        w_dngate_t=w_in_t[ab_end:gate_end], w_sbqkv_t=w_in_t[gate_end:sb_end], w_gl_t=w_in_t[sb_end:],
        dn_conv_w=dn_conv_full, alog=pad_lanes(dn_A_log), dtb=pad_lanes(dn_dt_bias), dn_norm_w=dn_norm_w,
        norm2_w=norm2_w, ffn_conv_w=ffn_conv_full, norm_f_w=norm_f_w[None, :])

    n_fc = FFN_CONV * 2 * D_FF
    fc_rows = -(-n_fc // D_MODEL)
    dn_rows = DN_CONV * 3 * WIDTH // D_MODEL
    late_names = ("w_proj_dn", "w_proj_sb", "w_out", "ffn_w_up", "ffn_w_down")

    class Plan:
        @staticmethod
        def late_gather():
            return _Gather([w_loc[k].astype(BF16) for k in late_names])

        @staticmethod
        def late_weights(got):
            g_pd, g_ps, g_out, g_up, g_down = got
            return dict(w_proj_dn=g_pd.reshape(WIDTH, D_MODEL), w_proj_sb=g_ps.reshape(WIDTH, D_MODEL),
                        w_out=g_out.reshape(D_MODEL, D_MODEL), ffn_w_up_t=g_up.reshape(2 * D_FF, D_MODEL),
                        ffn_w_down=g_down.reshape(D_FF, D_MODEL))

        @staticmethod
        def early_grads(g):
            return _Exchange([g["w_proj_dn"].reshape(N_DEV, WIDTH // N_DEV, D_MODEL),
                              g["w_proj_sb"].reshape(N_DEV, WIDTH // N_DEV, D_MODEL),
                              g["w_out"].reshape(N_DEV, D_MODEL // N_DEV, D_MODEL),
                              g["ffn_w_up_t"].reshape(N_DEV, 2 * D_FF // N_DEV, D_MODEL),
                              g["ffn_w_down"].reshape(N_DEV, D_FF // N_DEV, D_MODEL)])

        @staticmethod
        def _in_slabs(g):
            g_win_t = jnp.concatenate([g["w_main_t"][:q_end], g["w_ab_t"][:2 * HEADS], g["w_main_t"][q_end:]],
                                      axis=0)
            return g_win_t.reshape(N_DEV, in_width // N_DEV, D_MODEL)

        @staticmethod
        def sibling_swap(g):
            return _Exchange(sibling_slabs=[Plan._in_slabs(g)])

        @staticmethod
        def late_grads(swapped, g, loss):
            chip_sums = _add_my_slabs(Plan._in_slabs(g), swapped[0], "in_dw_chip_sum")
            row3 = jnp.concatenate([g["dn_norm_w"], g["alog"], g["dtb"], jnp.pad(loss, ((0, 0), (0, 127))),
                                    jnp.zeros((1, D_MODEL - 512), F32)], axis=1)
            fconv_rows = jnp.pad(g["ffn_conv_w"].reshape(-1), (0, fc_rows * D_MODEL - n_fc)).reshape(fc_rows, D_MODEL)
            pad8 = lambda a: jnp.pad(a, ((0, -a.shape[0] % 8), (0, 0)))
            pieces = [g["norm2_w"], g["norm_f_w"], row3, g["dn_conv_w"].reshape(dn_rows, D_MODEL), fconv_rows]
            small = jnp.concatenate([pad8(a) for a in pieces], axis=0)
            assert small.shape[0] == SMALL_ROWS
            return _Exchange(chip_slabs=[chip_sums], gathered=[small])

    loss, grad_x, g, got_early, got_late = _local_step(x[0], loss_target[0], wts, Plan, n1)
    r_pd, r_ps, r_out, r_up, r_down = got_early
    r_in, r_small = got_late
    (r_norm1,) = _comm_call(_Exchange([], [jnp.pad(g["norm1_w"], ((0, 7), (0, 0)))]), "gather_norm1")

    parts = dict(w_in=r_in, w_proj_dn=r_pd, w_proj_sb=r_ps, w_out=r_out, ffn_w_up=r_up, ffn_w_down=r_down)
    parts["norm1_w"] = r_norm1[:, 0:1, :]
    parts["norm2_w"] = r_small[:, 0:1, :]
    parts["norm_f_w"] = r_small[:, 8:9, :]
    parts["dn_norm_w"] = r_small[:, 16:17, 0:HEAD_DIM]
    parts["dn_A_log"] = r_small[:, 16:17, 128:128 + HEADS]
    parts["dn_dt_bias"] = r_small[:, 16:17, 256:256 + HEADS]
    dnc = r_small[:, 24:24 + dn_rows, :].reshape(N_DEV, DN_CONV, 3 * WIDTH)
    parts["dn_conv_w"] = lax.dynamic_slice_in_dim(dnc, me * (3 * WIDTH // N_DEV), 3 * WIDTH // N_DEV, axis=2)
    fc0 = 24 + dn_rows + (-dn_rows % 8)
    fcc = r_small[:, fc0:fc0 + fc_rows, :].reshape(N_DEV, fc_rows * D_MODEL)[:, :n_fc]
    fcc = fcc.reshape(N_DEV, FFN_CONV, 2 * D_FF)
    parts["ffn_conv_w"] = lax.dynamic_slice_in_dim(fcc, me * (2 * D_FF // N_DEV), 2 * D_FF // N_DEV, axis=2)
    loss_total = jnp.sum(r_small[:, 16, 384])

    res = {k: _adamw(parts[k], w_loc[k], m_loc[k], v_loc[k], "adamw_" + k) for k in WEIGHT_ORDER}
    lead = ("w_in", "dn_conv_w", "w_proj_dn", "w_proj_sb", "w_out", "ffn_w_up", "ffn_conv_w", "ffn_w_down")

    def shaped(k, a):
        if k in ("w_in", "ffn_w_up"):
            return jnp.transpose(a)[None]
        if k in lead:
            return a[None]
        if k == "norm_f_w":
            return a[0]
        return a

    outs = [loss_total, grad_x[None]]
    for idx in range(4):
        outs += [shaped(k, res[k][idx]) for k in WEIGHT_ORDER]
    return tuple(outs)
```

```python
import functools

import jax
import jax.numpy as jnp
from jax import lax
from jax.experimental import pallas as pl
from jax.experimental.pallas import tpu as pltpu

F32 = jnp.float32
BF16 = jnp.bfloat16

N_DEV = 8
D_MODEL = 1024
HEADS = 8
HEAD_DIM = 128
WIDTH = HEADS * HEAD_DIM
DN_CONV = 4
DN_CHUNK = 64
D_FF = 2816
FFN_CONV = 3
EPS = 1e-6
HALO = 16
CHUNK_ROWS = 256
SCAN_CHUNKS = 8
LOCAL_CHUNKS = 4
ATT_BLOCK = 256
SB_LOG_ZERO = -104.0
SB_GROUP = 2
SB_HEADS_FWD = 4
SB_HEADS_BWD = 2
SMALL_ROWS = 64

ADAM_LR = 0.001
ADAM_B1 = 0.9
ADAM_B2 = 0.999
ADAM_EPS = 1e-08
ADAM_WD = 0.01
ADAM_STEP = 10

VMEM_LIMIT = 48 * 1024 * 1024


def _params(sem=None, **kw):
    return pltpu.CompilerParams(dimension_semantics=sem, vmem_limit_bytes=VMEM_LIMIT, **kw)


def _tile(n, cap):
    if n <= cap:
        return n
    best = None
    for t in range(128, cap + 1, 128):
        if n % t == 0:
            best = t
    assert best is not None, (n, cap)
    return best


def _dot(a, b, dims):
    return lax.dot_general(a, b, ((dims[0], dims[1]), ((), ())), preferred_element_type=F32)


NN = ((1,), (0,))
NT = ((1,), (1,))
TN = ((0,), (0,))


def _dotb(a, b, dims):
    return _dot(a.astype(BF16), b.astype(BF16), dims)


def _split3(x):
    h1 = x.astype(BF16)
    r1 = x - h1.astype(F32)
    h2 = r1.astype(BF16)
    r2 = r1 - h2.astype(F32)
    return h1, h2, r2.astype(BF16)


def _dot_xr(a, b_exact, dims):
    a1, a2, a3 = _split3(a)
    return _dot(a1, b_exact, dims) + _dot(a2, b_exact, dims) + _dot(a3, b_exact, dims)


def _split2(x):
    h1 = x.astype(BF16)
    return h1, (x - h1.astype(F32)).astype(BF16)


def _dot_xr2(a, b_exact, dims):
    a1, a2 = _split2(a)
    return _dot(a1, b_exact, dims) + _dot(a2, b_exact, dims)


def _dot_xl(a_exact, b, dims):
    b1, b2, b3 = _split3(b)
    return _dot(a_exact, b1, dims) + _dot(a_exact, b2, dims) + _dot(a_exact, b3, dims)


def _dot3(a, b, dims):
    a1 = a.astype(BF16)
    a2 = (a - a1.astype(F32)).astype(BF16)
    b1 = b.astype(BF16)
    b2 = (b - b1.astype(F32)).astype(BF16)
    return _dot(a1, b1, dims) + (_dot(a1, b2, dims) + _dot(a2, b1, dims))


def _sigmoid(x):
    return 1.0 / (1.0 + jnp.exp(-x))


def _log1pexp_neg_abs(x):
    return jnp.log(1.0 + jnp.exp(-jnp.abs(x)))


def _iota(shape, dim):
    return lax.broadcasted_iota(jnp.int32, shape, dim)


def _matmul(a, b, mode, out_dtype, name, add=None, comm=None):
    if mode == "nn":
        (m, k), (k2, n) = a.shape, b.shape
    elif mode == "nt":
        (m, k), (n, k2) = a.shape, b.shape
    else:
        (k, m), (k2, n) = a.shape, b.shape
    assert k == k2, (a.shape, b.shape, mode)
    tm, tn, tk = _tile(m, 1408), _tile(n, 1408), _tile(k, 1536)
    nk = k // tk
    dims = {"nn": NN, "nt": NT, "tn": TN}[mode]

    def body(*refs):
        if add is None:
            a_ref, b_ref, o_ref, acc_ref = refs
        else:
            a_ref, b_ref, add_ref, o_ref, acc_ref = refs
        kk = pl.program_id(2)

        @pl.when(kk == 0)
        def _():
            acc_ref[...] = jnp.zeros_like(acc_ref)

        acc_ref[...] += _dotb(a_ref[...], b_ref[...], dims)

        @pl.when(kk == nk - 1)
        def _():
            r = acc_ref[...]
            if add is not None:
                r = r + add_ref[...].astype(F32)
            o_ref[...] = r.astype(out_dtype)

    if mode == "nn":
        specs = [pl.BlockSpec((tm, tk), lambda i, j, l: (i, l)), pl.BlockSpec((tk, tn), lambda i, j, l: (l, j))]
    elif mode == "nt":
        specs = [pl.BlockSpec((tm, tk), lambda i, j, l: (i, l)), pl.BlockSpec((tn, tk), lambda i, j, l: (j, l))]
    else:
        specs = [pl.BlockSpec((tk, tm), lambda i, j, l: (l, i)), pl.BlockSpec((tk, tn), lambda i, j, l: (l, j))]
    args = [a, b]
    if add is not None:
        specs.append(pl.BlockSpec((tm, tn), lambda i, j, l: (i, j)))
        args.append(add)
    grid = (m // tm, n // tn, nk)

    def when():
        i, j, l = pl.program_id(0), pl.program_id(1), pl.program_id(2)
        first = jnp.logical_and(jnp.logical_and(i == 0, j == 0), l == 0)
        last = jnp.logical_and(jnp.logical_and(i == grid[0] - 1, j == grid[1] - 1), l == nk - 1)
        return first, last, last

    (out,), extra = _host_call(
        body, name, comm, when, [jax.ShapeDtypeStruct((m, n), out_dtype)], grid, specs,
        [pl.BlockSpec((tm, tn), lambda i, j, l: (i, j))], [pltpu.VMEM((tm, tn), F32)],
        ("parallel", "parallel", "arbitrary"), args)
    return out if comm is None else (out, extra)


def _rmsnorm_fwd(x, w, name, comm=None):
    t, d = x.shape
    tr = _tile(t, 512)
    steps = t // tr

    def body(x_ref, w_ref, o_ref):
        xv = x_ref[...]
        r = lax.rsqrt(jnp.mean(xv * xv, axis=1, keepdims=True) + EPS)
        o_ref[...] = (xv * r * w_ref[...]).astype(BF16)

    def when():
        i = pl.program_id(0)
        return i == 0, i == steps // 2, i == steps - 1

    (out,), extra = _host_call(
        body, name, comm, when, [jax.ShapeDtypeStruct((t, d), BF16)], (steps,),
        [pl.BlockSpec((tr, d), lambda i: (i, 0)), pl.BlockSpec((1, d), lambda i: (0, 0))],
        [pl.BlockSpec((tr, d), lambda i: (i, 0))], [], ("parallel",), (x, w))
    return out if comm is None else (out, extra)


def _rmsnorm_bwd(dn, x, w, dres, name):
    t, d = x.shape
    tr = _tile(t, 512)

    def body(dn_ref, x_ref, w_ref, dres_ref, dx_ref, dw_ref):
        i = pl.program_id(0)
        xv = x_ref[...]
        g = dn_ref[...].astype(F32)
        r = lax.rsqrt(jnp.mean(xv * xv, axis=1, keepdims=True) + EPS)
        xh = xv * r
        dxh = g * w_ref[...]
        dx = r * (dxh - xh * jnp.mean(dxh * xh, axis=1, keepdims=True))
        dx_ref[...] = dres_ref[...] + dx

        @pl.when(i == 0)
        def _():
            dw_ref[...] = jnp.zeros_like(dw_ref)

        dw_ref[...] += jnp.sum(g * xh, axis=0, keepdims=True)

    return pl.pallas_call(
        body, name=name,
        out_shape=(jax.ShapeDtypeStruct((t, d), F32), jax.ShapeDtypeStruct((1, d), F32)),
        grid=(t // tr,),
        in_specs=[pl.BlockSpec((tr, d), lambda i: (i, 0)), pl.BlockSpec((tr, d), lambda i: (i, 0)),
                  pl.BlockSpec((1, d), lambda i: (0, 0)), pl.BlockSpec((tr, d), lambda i: (i, 0))],
        out_specs=(pl.BlockSpec((tr, d), lambda i: (i, 0)), pl.BlockSpec((1, d), lambda i: (0, 0))),
        compiler_params=_params(("arbitrary",)),
    )(dn, x, w, dres)


def _final_loss(x2, target, w, name):
    t, d = x2.shape
    tr = _tile(t, 512)

    def body(x_ref, t_ref, w_ref, dx_ref, dw_ref, loss_ref):
        i = pl.program_id(0)
        xv = x_ref[...]
        r = lax.rsqrt(jnp.mean(xv * xv, axis=1, keepdims=True) + EPS)
        xh = xv * r
        err = xh * w_ref[...] - t_ref[...]
        dy = err * (1.0 / d)
        dxh = dy * w_ref[...]
        dx_ref[...] = r * (dxh - xh * jnp.mean(dxh * xh, axis=1, keepdims=True))

        @pl.when(i == 0)
        def _():
            dw_ref[...] = jnp.zeros_like(dw_ref)
            loss_ref[...] = jnp.zeros_like(loss_ref)

        dw_ref[...] += jnp.sum(dy * xh, axis=0, keepdims=True)
        row = jnp.sum(err * err, axis=1, keepdims=True) * (0.5 / d)
        loss_ref[...] += jnp.sum(row, axis=0, keepdims=True)

    return pl.pallas_call(
        body, name=name,
        out_shape=(jax.ShapeDtypeStruct((t, d), F32), jax.ShapeDtypeStruct((1, d), F32),
                   jax.ShapeDtypeStruct((1, 1), F32)),
        grid=(t // tr,),
        in_specs=[pl.BlockSpec((tr, d), lambda i: (i, 0)), pl.BlockSpec((tr, d), lambda i: (i, 0)),
                  pl.BlockSpec((1, d), lambda i: (0, 0))],
        out_specs=(pl.BlockSpec((tr, d), lambda i: (i, 0)), pl.BlockSpec((1, d), lambda i: (0, 0)),
                   pl.BlockSpec((1, 1), lambda i: (0, 0))),
        compiler_params=_params(("arbitrary",)),
    )(x2, target, w)


def _shift_down(cur, prev, k, row):
    r = pltpu.roll(cur, k, 0)
    top, row8 = r[0:8, :], row[0:8, :]
    for m in range(k):
        top = jnp.where(row8 == m, prev[HALO - k + m:HALO - k + m + 1, :], top)
    return jnp.concatenate([top, r[8:, :]], axis=0)


def _shift_up(cur, nxt, k, row, tr):
    r = pltpu.roll(cur, tr - k, 0)
    bottom, row8 = r[tr - 8:, :], row[0:8, :]
    for m in range(k):
        bottom = jnp.where(row8 == 8 - k + m, nxt[m:m + 1, :], bottom)
    return jnp.concatenate([r[:tr - 8, :], bottom], axis=0)


def _fold8(a):
    out = a[0:8, :]
    for r in range(8, a.shape[0], 8):
        out = out + a[r:r + 8, :]
    return out


def _conv_taps(cur, prev, w, ntaps, row):
    taps = [cur if i == ntaps - 1 else _shift_down(cur, prev, ntaps - 1 - i, row) for i in range(ntaps)]
    y = w[0:1, :] * taps[0]
    for i in range(1, ntaps):
        y = y + w[i:i + 1, :] * taps[i]
    return taps, y


def _conv_bwd_data(parts, w, ntaps, out_dtype, name):
    t, chp = parts[0].shape
    npart = len(parts)
    tr, tc = _tile(t, 512), _tile(chp, 1408)
    nc = chp // tc
    nhalo = t // HALO
    last = t // tr - 1

    def body(*refs):
        cur_refs, nxt_refs = refs[:npart], refs[npart:2 * npart]
        w_ref, o_ref = refs[2 * npart], refs[2 * npart + 1]
        i, j = pl.program_id(0), pl.program_id(1)
        row = _iota((tr, 128), 0)
        for c0 in range(0, tc, 128):
            sl = slice(c0, c0 + 128)
            cur, nxt = cur_refs[0][:, sl].astype(F32), nxt_refs[0][:, sl].astype(F32)
            for p in range(1, npart):
                cur = jnp.where(j >= p * nc, cur_refs[p][:, sl].astype(F32), cur)
                nxt = jnp.where(j >= p * nc, nxt_refs[p][:, sl].astype(F32), nxt)
            nxt = jnp.where(i == last, 0.0, nxt)
            wv = w_ref[:, sl]
            y = wv[ntaps - 1:ntaps, :] * cur
            for k in range(1, ntaps):
                y = y + wv[ntaps - 1 - k:ntaps - k, :] * _shift_up(cur, nxt, k, row, tr)
            o_ref[:, sl] = y.astype(out_dtype)

    col = lambda p: (lambda j: jnp.clip(j - p * nc, 0, nc - 1))
    cur_specs = [pl.BlockSpec((tr, tc), lambda i, j, c=col(p): (i, c(j))) for p in range(npart)]
    nxt_specs = [pl.BlockSpec((HALO, tc),
                              lambda i, j, c=col(p): (jnp.minimum((i + 1) * (tr // HALO), nhalo - 1), c(j)))
                 for p in range(npart)]
    return pl.pallas_call(
        body, name=name,
        out_shape=jax.ShapeDtypeStruct((t, npart * chp), out_dtype),
        grid=(t // tr, npart * nc),
        in_specs=cur_specs + nxt_specs + [pl.BlockSpec((ntaps, tc), lambda i, j: (0, j))],
        out_specs=pl.BlockSpec((tr, tc), lambda i, j: (i, j)),
        compiler_params=_params(("parallel", "parallel")),
    )(*parts, *parts, w)


def _ffn_act_fwd(upre, cw, name):
    t = upre.shape[0]
    tr, tc = _tile(t, 512), _tile(D_FF, 1408)
    nj = D_FF // tc

    def body(g_ref, gp_ref, u_ref, up_ref, wg_ref, wu_ref, o_ref):
        i = pl.program_id(0)
        row = _iota((tr, 128), 0)
        for c0 in range(0, tc, 128):
            sl = slice(c0, c0 + 128)
            gp = jnp.where(i == 0, 0.0, gp_ref[:, sl].astype(F32))
            up = jnp.where(i == 0, 0.0, up_ref[:, sl].astype(F32))
            _, gc = _conv_taps(g_ref[:, sl].astype(F32), gp, wg_ref[:, sl], FFN_CONV, row)
            _, uc = _conv_taps(u_ref[:, sl].astype(F32), up, wu_ref[:, sl], FFN_CONV, row)
            o_ref[:, sl] = (gc * _sigmoid(gc) * uc).astype(BF16)

    prev = lambda off: (lambda i, j: (jnp.maximum(i * (tr // HALO) - 1, 0), j + off))
    return pl.pallas_call(
        body, name=name,
        out_shape=jax.ShapeDtypeStruct((t, D_FF), BF16),
        grid=(t // tr, nj),
        in_specs=[pl.BlockSpec((tr, tc), lambda i, j: (i, j)), pl.BlockSpec((HALO, tc), prev(0)),
                  pl.BlockSpec((tr, tc), lambda i, j: (i, j + nj)), pl.BlockSpec((HALO, tc), prev(nj)),
                  pl.BlockSpec((FFN_CONV, tc), lambda i, j: (0, j)),
                  pl.BlockSpec((FFN_CONV, tc), lambda i, j: (0, j + nj))],
        out_specs=pl.BlockSpec((tr, tc), lambda i, j: (i, j)),
        compiler_params=_params(("parallel", "parallel")),
    )(upre, upre, upre, upre, cw, cw)


def _ffn_act_bwd(dact, upre, cw, name):
    t = upre.shape[0]
    tr, tc = _tile(t, 512), _tile(D_FF, 1408)
    nj = D_FF // tc

    def body(da_ref, g_ref, gp_ref, u_ref, up_ref, wg_ref, wu_ref, dg_ref, du_ref, dwg_ref, dwu_ref):
        i = pl.program_id(1)
        row = _iota((CHUNK_ROWS, 128), 0)

        @pl.when(i == 0)
        def _():
            dwg_ref[...] = jnp.zeros_like(dwg_ref)
            dwu_ref[...] = jnp.zeros_like(dwu_ref)

        for c0 in range(0, tc, 128):
            sl = slice(c0, c0 + 128)
            wg, wu = wg_ref[:, sl], wu_ref[:, sl]
            dwg = [jnp.zeros((8, 128), F32)] * FFN_CONV
            dwu = [jnp.zeros((8, 128), F32)] * FFN_CONV
            for r0 in range(0, tr, CHUNK_ROWS):
                rows = slice(r0, r0 + CHUNK_ROWS)
                if r0 == 0:
                    gp = jnp.where(i == 0, 0.0, gp_ref[:, sl].astype(F32))
                    up = jnp.where(i == 0, 0.0, up_ref[:, sl].astype(F32))
                else:
                    gp = g_ref[r0 - HALO:r0, sl].astype(F32)
                    up = u_ref[r0 - HALO:r0, sl].astype(F32)
                gt, gc = _conv_taps(g_ref[rows, sl].astype(F32), gp, wg, FFN_CONV, row)
                ut, uc = _conv_taps(u_ref[rows, sl].astype(F32), up, wu, FFN_CONV, row)
                da = da_ref[rows, sl].astype(F32)
                sg = _sigmoid(gc)
                dgc = da * uc * (sg * (1.0 + gc * (1.0 - sg)))
                duc = da * (gc * sg)
                dg_ref[rows, sl] = dgc.astype(BF16)
                du_ref[rows, sl] = duc.astype(BF16)
                dwg = [dwg[k] + _fold8(dgc * gt[k]) for k in range(FFN_CONV)]
                dwu = [dwu[k] + _fold8(duc * ut[k]) for k in range(FFN_CONV)]
            for k in range(FFN_CONV):
                dwg_ref[k:k + 1, sl] += jnp.sum(dwg[k], axis=0, keepdims=True)
                dwu_ref[k:k + 1, sl] += jnp.sum(dwu[k], axis=0, keepdims=True)

    prev = lambda off: (lambda j, i: (jnp.maximum(i * (tr // HALO) - 1, 0), j + off))
    blk = lambda off: pl.BlockSpec((tr, tc), lambda j, i: (i, j + off))
    wblk = lambda off: pl.BlockSpec((FFN_CONV, tc), lambda j, i: (0, j + off))
    dgc, duc, dwg, dwu = pl.pallas_call(
        body, name=name,
        out_shape=(jax.ShapeDtypeStruct((t, D_FF), BF16), jax.ShapeDtypeStruct((t, D_FF), BF16),
                   jax.ShapeDtypeStruct((FFN_CONV, D_FF), F32), jax.ShapeDtypeStruct((FFN_CONV, D_FF), F32)),
        grid=(nj, t // tr),
        in_specs=[blk(0), blk(0), pl.BlockSpec((HALO, tc), prev(0)), blk(nj), pl.BlockSpec((HALO, tc), prev(nj)),
                  wblk(0), wblk(nj)],
        out_specs=(blk(0), blk(0), wblk(0), wblk(0)),
        compiler_params=_params(("parallel", "arbitrary")),
    )(dact, upre, upre, upre, upre, cw, cw)
    return dgc, duc, dwg, dwu


def _dn_pre_fwd(qkv_pre, cw, name):
    t = qkv_pre.shape[0]
    tr = _tile(t, 512)
    scale = HEAD_DIM ** -0.5

    def body(x_ref, p_ref, w_ref, o_ref):
        i, j = pl.program_id(0), pl.program_id(1)
        row = _iota((tr, HEAD_DIM), 0)
        for h in range(HEADS):
            sl = slice(h * HEAD_DIM, (h + 1) * HEAD_DIM)
            prev = jnp.where(i == 0, 0.0, p_ref[:, sl].astype(F32))
            _, c = _conv_taps(x_ref[:, sl].astype(F32), prev, w_ref[:, sl], DN_CONV, row)
            s = c * _sigmoid(c)
            r = lax.rsqrt(jnp.sum(s * s, axis=1, keepdims=True) + EPS)
            o_ref[:, sl] = s * jnp.where(j == 0, r * scale, jnp.where(j == 1, r, 1.0))

    return pl.pallas_call(
        body, name=name,
        out_shape=jax.ShapeDtypeStruct((t, 3 * WIDTH), F32),
        grid=(t // tr, 3),
        in_specs=[pl.BlockSpec((tr, WIDTH), lambda i, j: (i, j)),
                  pl.BlockSpec((HALO, WIDTH), lambda i, j: (jnp.maximum(i * (tr // HALO) - 1, 0), j)),
                  pl.BlockSpec((DN_CONV, WIDTH), lambda i, j: (0, j))],
        out_specs=pl.BlockSpec((tr, WIDTH), lambda i, j: (i, j)),
        compiler_params=_params(("parallel", "parallel")),
    )(qkv_pre, qkv_pre, cw)


def _dn_pre_bwd(dq, dk, dv, qkv_pre, cw, name):
    t = qkv_pre.shape[0]
    tr = _tile(t, 512)
    scale = HEAD_DIM ** -0.5

    def body(dq_ref, dk_ref, dv_ref, x_ref, p_ref, w_ref, dc_ref, dw_ref):
        j, i = pl.program_id(0), pl.program_id(1)
        row = _iota((CHUNK_ROWS, HEAD_DIM), 0)

        @pl.when(i == 0)
        def _():
            dw_ref[...] = jnp.zeros_like(dw_ref)

        for h in range(HEADS):
            sl = slice(h * HEAD_DIM, (h + 1) * HEAD_DIM)
            wv = w_ref[:, sl]
            dw = [jnp.zeros((8, HEAD_DIM), F32)] * DN_CONV
            for r0 in range(0, tr, CHUNK_ROWS):
                rows = slice(r0, r0 + CHUNK_ROWS)
                if r0 == 0:
                    prev = jnp.where(i == 0, 0.0, p_ref[:, sl].astype(F32))
                else:
                    prev = x_ref[r0 - HALO:r0, sl].astype(F32)
                taps, c = _conv_taps(x_ref[rows, sl].astype(F32), prev, wv, DN_CONV, row)
                d = jnp.where(j == 0, dq_ref[rows, sl] * scale, jnp.where(j == 1, dk_ref[rows, sl], dv_ref[rows, sl]))
                sg = _sigmoid(c)
                s = c * sg
                r = lax.rsqrt(jnp.sum(s * s, axis=1, keepdims=True) + EPS)
                nh = s * r
                ds_norm = r * (d - nh * jnp.sum(nh * d, axis=1, keepdims=True))
                dc = jnp.where(j < 2, ds_norm, d) * (sg * (1.0 + c * (1.0 - sg)))
                dc_ref[rows, sl] = dc.astype(BF16)
                dw = [dw[k] + _fold8(dc * taps[k]) for k in range(DN_CONV)]
            for k in range(DN_CONV):
                dw_ref[k:k + 1, sl] += jnp.sum(dw[k], axis=0, keepdims=True)

    dspec = lambda p: pl.BlockSpec((tr, WIDTH), lambda j, i: (jnp.where(j == p, i, 0), 0))
    return pl.pallas_call(
        body, name=name,
        out_shape=(jax.ShapeDtypeStruct((t, 3 * WIDTH), BF16), jax.ShapeDtypeStruct((DN_CONV, 3 * WIDTH), F32)),
        grid=(3, t // tr),
        in_specs=[dspec(0), dspec(1), dspec(2),
                  pl.BlockSpec((tr, WIDTH), lambda j, i: (i, j)),
                  pl.BlockSpec((HALO, WIDTH), lambda j, i: (jnp.maximum(i * (tr // HALO) - 1, 0), j)),
                  pl.BlockSpec((DN_CONV, WIDTH), lambda j, i: (0, j))],
        out_specs=(pl.BlockSpec((tr, WIDTH), lambda j, i: (i, j)),
                   pl.BlockSpec((DN_CONV, WIDTH), lambda j, i: (0, j))),
        compiler_params=_params(("parallel", "arbitrary")),
    )(dq, dk, dv, qkv_pre, qkv_pre, cw)


def _tri(n, kind):
    r, c = _iota((n, n), 0), _iota((n, n), 1)
    m = {"lower": r >= c, "strict": r > c, "upper": r <= c}[kind]
    return m


GATE_ROWS = 4 * DN_CHUNK


def _chunk_tri(kind):
    r, c = _iota((GATE_ROWS, GATE_ROWS), 0), _iota((GATE_ROWS, GATE_ROWS), 1)
    same = (r // DN_CHUNK) == (c // DN_CHUNK)
    return jnp.where(jnp.logical_and(same, _tri(GATE_ROWS, kind)), 1.0, 0.0).astype(BF16)


def _dn_gates_fwd(hab, alog, dtb, name):
    t = hab.shape[0]
    cc = GATE_ROWS

    def body(h_ref, al_ref, dt_ref, o_ref):
        hv = h_ref[...]
        lane = _iota(hv.shape, 1)
        xa = hv + dt_ref[...]
        sp = jnp.maximum(xa, 0.0) + _log1pexp_neg_abs(xa)
        g = jnp.where(lane < HEADS, -jnp.exp(al_ref[...]) * sp, 0.0)
        gc = _dot_xl(_chunk_tri("lower"), g, NN)
        o_ref[...] = jnp.where(lane < HEADS, gc, jnp.where(lane < 2 * HEADS, _sigmoid(hv), 0.0))

    return pl.pallas_call(
        body, name=name,
        out_shape=jax.ShapeDtypeStruct((t, 128), F32),
        grid=(t // cc,),
        in_specs=[pl.BlockSpec((cc, 128), lambda i: (i, 0)), pl.BlockSpec((1, 128), lambda i: (0, 0)),
                  pl.BlockSpec((1, 128), lambda i: (0, 0))],
        out_specs=pl.BlockSpec((cc, 128), lambda i: (i, 0)),
        compiler_params=_params(("parallel",)),
    )(hab, alog, dtb)


def _dn_gates_bwd(dgates, hab, alog, dtb, name):
    t = hab.shape[0]
    cc = GATE_ROWS

    def body(d_ref, h_ref, al_ref, dt_ref, o_ref, dal_ref, ddt_ref):
        i = pl.program_id(0)
        hv = h_ref[...]
        dv = d_ref[...]
        lane = _iota(hv.shape, 1)
        dg = _dot_xl(_chunk_tri("upper"), jnp.where(lane < HEADS, dv, 0.0), NN)
        xa = hv + dt_ref[...]
        sp = jnp.maximum(xa, 0.0) + _log1pexp_neg_abs(xa)
        ea = jnp.exp(al_ref[...])
        da = jnp.where(lane < HEADS, dg * (-ea) * _sigmoid(xa), 0.0)
        be = _sigmoid(hv)
        db = dv * be * (1.0 - be)
        o_ref[...] = jnp.where(lane < HEADS, da, jnp.where(lane < 2 * HEADS, db, 0.0))

        @pl.when(i == 0)
        def _():
            dal_ref[...] = jnp.zeros_like(dal_ref)
            ddt_ref[...] = jnp.zeros_like(ddt_ref)

        dal_ref[...] += jnp.sum(jnp.where(lane < HEADS, dg * (-ea) * sp, 0.0), axis=0, keepdims=True)
        ddt_ref[...] += jnp.sum(da, axis=0, keepdims=True)

    return pl.pallas_call(
        body, name=name,
        out_shape=(jax.ShapeDtypeStruct((t, 128), F32), jax.ShapeDtypeStruct((1, 128), F32),
                   jax.ShapeDtypeStruct((1, 128), F32)),
        grid=(t // cc,),
        in_specs=[pl.BlockSpec((cc, 128), lambda i: (i, 0)), pl.BlockSpec((cc, 128), lambda i: (i, 0)),
                  pl.BlockSpec((1, 128), lambda i: (0, 0)), pl.BlockSpec((1, 128), lambda i: (0, 0))],
        out_specs=(pl.BlockSpec((cc, 128), lambda i: (i, 0)), pl.BlockSpec((1, 128), lambda i: (0, 0)),
                   pl.BlockSpec((1, 128), lambda i: (0, 0))),
        compiler_params=_params(("arbitrary",)),
    )(dgates, hab, alog, dtb)


def _dn_chunk_common(gates, h):
    cc = DN_CHUNK
    lane = _iota(gates.shape, 1)
    gh = jnp.where(lane == h, gates, 0.0)
    gc_col = jnp.sum(gh, axis=1, keepdims=True)
    gc_row = _dot_xl(jnp.ones((cc, 128), BF16), gh, NT)
    beta = jnp.sum(jnp.where(lane == h + HEADS, gates, 0.0), axis=1, keepdims=True)
    lower = _tri(cc, "lower")
    decay = jnp.where(lower, jnp.exp(jnp.where(lower, gc_col - gc_row, 0.0)), 0.0)
    gc_last = gc_col[cc - 1:cc, :]
    return gc_col, gc_last, beta, decay


def _dn_local_fwd(act, gates, name):
    t = act.shape[0]
    cc = DN_CHUNK
    nc = t // cc

    per = LOCAL_CHUNKS

    def body(q_ref, k_ref, v_ref, g_ref, u_ref, w_ref, kd_ref, qg_ref, ti_ref, p_ref):
        eye = jnp.where(_iota((cc, cc), 0) == _iota((cc, cc), 1), 1.0, 0.0)
        units = [(c, h) for c in range(per) for h in range(HEADS)]
        us = range(len(units))
        rows = [slice(c * cc, (c + 1) * cc) for c, _ in units]
        sl = [slice(h * HEAD_DIM, (h + 1) * HEAD_DIM) for _, h in units]
        gates = [g_ref[c * cc:(c + 1) * cc, :] for c in range(per)]
        q, k, v = ([r[rows[i], sl[i]] for i in us] for r in (q_ref, k_ref, v_ref))
        gc_col, gc_last, beta, decay = zip(*[_dn_chunk_common(gates[c], h) for c, h in units])
        gam = [jnp.exp(g) for g in gc_col]
        kb = [k[i] * beta[i] for i in us]
        npow = [-jnp.where(_tri(cc, "strict"), _dotb(kb[i], k[i], NT) * decay[i], 0.0) for i in us]
        tinv = [eye + n for n in npow]
        for _ in range(5):
            npow = [_dot3(n, n, NN) for n in npow]
            tinv = [t + _dot3(t, n, NN) for t, n in zip(tinv, npow)]
        uu = [_dot3(tinv[i], v[i] * beta[i], NN) for i in us]
        ww = [_dot3(tinv[i], kb[i] * gam[i], NN) for i in us]
        pp = [jnp.where(_tri(cc, "lower"), _dotb(q[i], k[i], NT) * decay[i], 0.0) for i in us]
        for i, (_, h) in enumerate(units):
            u_ref[rows[i], sl[i]] = uu[i]
            w_ref[rows[i], sl[i]] = ww[i].astype(BF16)
            kd_ref[rows[i], sl[i]] = (k[i] * jnp.exp(gc_last[i] - gc_col[i])).astype(BF16)
            qg_ref[rows[i], sl[i]] = (q[i] * gam[i]).astype(BF16)
            ti_ref[h, rows[i], :] = tinv[i]
            p_ref[h, rows[i], :] = pp[i].astype(BF16)

    row = lambda off: pl.BlockSpec((per * cc, WIDTH), lambda n: (n, off))
    mat = pl.BlockSpec((HEADS, per * cc, cc), lambda n: (0, n, 0))
    tw, tb = jax.ShapeDtypeStruct((t, WIDTH), F32), jax.ShapeDtypeStruct((t, WIDTH), BF16)
    hm, hb = jax.ShapeDtypeStruct((HEADS, t, cc), F32), jax.ShapeDtypeStruct((HEADS, t, cc), BF16)
    return pl.pallas_call(
        body, name=name,
        out_shape=(tw, tb, tb, tb, hm, hb),
        grid=(nc // per,),
        in_specs=[row(0), row(1), row(2), pl.BlockSpec((per * cc, 128), lambda n: (n, 0))],
        out_specs=(row(0), row(0), row(0), row(0), mat, mat),
        compiler_params=_params(("parallel",)),
    )(act, act, act, gates)


def _dn_scan_fwd(u, w, kd, qg, p, gates, name):
    t = u.shape[0]
    cc = DN_CHUNK
    nc = t // cc
    per = min(SCAN_CHUNKS, nc)

    def body(u_ref, w_ref, kd_ref, qg_ref, p_ref, g_ref, o_ref, sh_ref, s_ref):
        n = pl.program_id(0)

        @pl.when(n == 0)
        def _():
            s_ref[...] = jnp.zeros_like(s_ref)

        hs = range(HEADS)
        sl = [slice(h * HEAD_DIM, (h + 1) * HEAD_DIM) for h in hs]
        s = [s_ref[h] for h in hs]
        for c in range(per):
            r = slice(c * cc, (c + 1) * cc)
            glast = jnp.exp(g_ref[(c + 1) * cc - 1:(c + 1) * cc, :])
            sb = [a.astype(BF16) for a in s]
            vn = [u_ref[r, sl[h]] - _dot(w_ref[r, sl[h]].astype(BF16), sb[h], NN) for h in hs]
            vnb = [a.astype(BF16) for a in vn]
            o_state = [_dot(qg_ref[r, sl[h]].astype(BF16), sb[h], NN) for h in hs]
            o_local = [_dot(p_ref[h, r, :].astype(BF16), vnb[h], NN) for h in hs]
            s_add = [_dot(kd_ref[r, sl[h]].astype(BF16), vnb[h], TN) for h in hs]
            for h in hs:
                o_ref[r, sl[h]] = o_state[h] + o_local[h]
                sh_ref[c, h] = sb[h]
            s = [glast[:, h:h + 1] * s[h] + s_add[h] for h in hs]
        for h in hs:
            s_ref[h] = s[h]

    row = pl.BlockSpec((per * cc, WIDTH), lambda n: (n, 0))
    return pl.pallas_call(
        body, name=name,
        out_shape=(jax.ShapeDtypeStruct((t, WIDTH), F32),
                   jax.ShapeDtypeStruct((nc, HEADS, HEAD_DIM, HEAD_DIM), BF16)),
        grid=(nc // per,),
        in_specs=[row, row, row, row, pl.BlockSpec((HEADS, per * cc, cc), lambda n: (0, n, 0)),
                  pl.BlockSpec((per * cc, 128), lambda n: (n, 0))],
        out_specs=(row, pl.BlockSpec((per, HEADS, HEAD_DIM, HEAD_DIM), lambda n: (n, 0, 0, 0))),
        scratch_shapes=[pltpu.VMEM((HEADS, HEAD_DIM, HEAD_DIM), F32)],
        compiler_params=_params(("arbitrary",)),
    )(u, w, kd, qg, p, gates)


def _dn_scan_bwd(do, w, kd, qg, p, gates, name):
    t = do.shape[0]
    cc = DN_CHUNK
    nc = t // cc
    per = min(SCAN_CHUNKS, nc)
    nb = nc // per

    def body(do_ref, w_ref, kd_ref, qg_ref, p_ref, g_ref, dvn_ref, dsh_ref, ds_ref):
        n = pl.program_id(0)

        @pl.when(n == 0)
        def _():
            ds_ref[...] = jnp.zeros_like(ds_ref)

        hs = range(HEADS)
        sl = [slice(h * HEAD_DIM, (h + 1) * HEAD_DIM) for h in hs]
        ds = [ds_ref[h] for h in hs]
        for c in reversed(range(per)):
            r = slice(c * cc, (c + 1) * cc)
            glast = jnp.exp(g_ref[(c + 1) * cc - 1:(c + 1) * cc, :])
            dob = [do_ref[r, sl[h]].astype(BF16) for h in hs]
            dvn = [_dot(p_ref[h, r, :].astype(BF16), dob[h], TN)
                   + _dot(kd_ref[r, sl[h]].astype(BF16), ds[h].astype(BF16), NN) for h in hs]
            ds_q = [_dot(qg_ref[r, sl[h]].astype(BF16), dob[h], TN) for h in hs]
            ds_w = [_dot(w_ref[r, sl[h]].astype(BF16), dvn[h].astype(BF16), TN) for h in hs]
            for h in hs:
                dvn_ref[r, sl[h]] = dvn[h]
                dsh_ref[c, h] = ds[h].astype(BF16)
            ds = [ds_q[h] + glast[:, h:h + 1] * ds[h] - ds_w[h] for h in hs]
        for h in hs:
            ds_ref[h] = ds[h]

    row = pl.BlockSpec((per * cc, WIDTH), lambda n: (nb - 1 - n, 0))
    return pl.pallas_call(
        body, name=name,
        out_shape=(jax.ShapeDtypeStruct((t, WIDTH), F32),
                   jax.ShapeDtypeStruct((nc, HEADS, HEAD_DIM, HEAD_DIM), BF16)),
        grid=(nb,),
        in_specs=[row, row, row, row, pl.BlockSpec((HEADS, per * cc, cc), lambda n: (0, nb - 1 - n, 0)),
                  pl.BlockSpec((per * cc, 128), lambda n: (nb - 1 - n, 0))],
        out_specs=(row, pl.BlockSpec((per, HEADS, HEAD_DIM, HEAD_DIM), lambda n: (nb - 1 - n, 0, 0, 0))),
        scratch_shapes=[pltpu.VMEM((HEADS, HEAD_DIM, HEAD_DIM), F32)],
        compiler_params=_params(("arbitrary",)),
    )(do, w, kd, qg, p, gates)


def _dn_local_bwd(act, gates, u, w, kd, qg, tinv, p, sh, dsh, dvn, do, name):
    t = act.shape[0]
    cc = DN_CHUNK
    nc = t // cc
    per = LOCAL_CHUNKS

    def body(q_ref, k_ref, v_ref, g_ref, u_ref, w_ref, kd_ref, qg_ref, ti_ref, p_ref, s_ref, ds_ref,
             dvn_ref, do_ref, dq_ref, dk_ref, dv_ref, dg_ref):
        lower, strict = _tri(cc, "lower"), _tri(cc, "strict")
        ones = jnp.ones((cc, 128), BF16)
        rowc = _iota((cc, 1), 0)
        lane = _iota((cc, 128), 1)
        units = [(c, h) for c in range(per) for h in range(HEADS)]
        hs = range(len(units))
        rows = [slice(c * cc, (c + 1) * cc) for c, _ in units]
        sl = [slice(h * HEAD_DIM, (h + 1) * HEAD_DIM) for _, h in units]
        gates_v = [g_ref[c * cc:(c + 1) * cc, :] for c in range(per)]
        q, k, v, uu, ww, kd, qg, dvn, do = ([r[rows[i], sl[i]] for i in hs] for r in (
            q_ref, k_ref, v_ref, u_ref, w_ref, kd_ref, qg_ref, dvn_ref, do_ref))
        gc_col, gc_last, beta, decay = zip(*[_dn_chunk_common(gates_v[c], h) for c, h in units])
        gam = [jnp.exp(g) for g in gc_col]
        kb = [k[h] * beta[h] for h in hs]
        s_in = [s_ref[c, h] for c, h in units]
        ds_out = [ds_ref[c, h] for c, h in units]
        tinv = [ti_ref[h, rows[i], :] for i, (_, h) in enumerate(units)]
        pmat = [p_ref[h, rows[i], :] for i, (_, h) in enumerate(units)]

        a = [jnp.where(strict, _dotb(kb[h], k[h], NT) * decay[h], 0.0) for h in hs]
        vn = [uu[h] - _dotb(ww[h], s_in[h], NN) for h in hs]
        dqg = [_dotb(do[h], s_in[h], NT) for h in hs]
        dw = [-_dotb(dvn[h], s_in[h], NT) for h in hs]
        dp = [jnp.where(lower, _dotb(do[h], vn[h], NT), 0.0) for h in hs]
        dkd = [_dotb(vn[h], ds_out[h], NT) for h in hs]
        dru = [_dot3(tinv[h], dvn[h], TN) for h in hs]
        drw = [_dot3(tinv[h], dw[h], TN) for h in hs]
        da = [-jnp.where(strict, _dotb(dru[h], uu[h], NT) + _dotb(drw[h], ww[h], NT), 0.0) for h in hs]
        dad = [da[h] * decay[h] for h in hs]
        dpd = [dp[h] * decay[h] for h in hs]
        dkb = [_dotb(dad[h], k[h], NN) + gam[h] * drw[h] for h in hs]
        dk = [_dotb(dad[h], kb[h], TN) + _dotb(dpd[h], q[h], TN) + beta[h] * dkb[h]
              + jnp.exp(gc_last[h] - gc_col[h]) * dkd[h] for h in hs]
        dq = [gam[h] * dqg[h] + _dotb(dpd[h], k[h], NN) for h in hs]
        gm = [da[h] * a[h] + dp[h] * pmat[h] for h in hs]
        colsum = [_dot_xr(gm[h], ones, TN)[:, 0:1] for h in hs]

        dgates = [jnp.zeros((cc, 128), F32)] * per
        for h, (c, head) in enumerate(units):
            dk_ref[rows[h], sl[h]] = dk[h]
            dq_ref[rows[h], sl[h]] = dq[h]
            dv_ref[rows[h], sl[h]] = beta[h] * dru[h]
            dbeta = (jnp.sum(dkb[h] * k[h], axis=1, keepdims=True)
                     + jnp.sum(dru[h] * v[h], axis=1, keepdims=True))
            rkd = jnp.sum(dkd[h] * kd[h], axis=1, keepdims=True)
            dgc = (jnp.sum(gm[h], axis=1, keepdims=True) - colsum[h]
                   + jnp.sum(dqg[h] * qg[h], axis=1, keepdims=True)
                   + jnp.sum(drw[h] * kb[h], axis=1, keepdims=True) * gam[h] - rkd)
            tail = jnp.sum(rkd, axis=0, keepdims=True) + jnp.exp(gc_last[h]) * jnp.sum(
                jnp.sum(s_in[h].astype(F32) * ds_out[h].astype(F32), axis=1, keepdims=True), axis=0, keepdims=True)
            dgc = dgc + jnp.where(rowc == cc - 1, tail, 0.0)
            dgates[c] = dgates[c] + jnp.where(lane == head, dgc, 0.0) + jnp.where(lane == head + HEADS, dbeta, 0.0)
        for c in range(per):
            dg_ref[c * cc:(c + 1) * cc, :] = dgates[c]

    row = lambda off: pl.BlockSpec((per * cc, WIDTH), lambda n: (n, off))
    mat = pl.BlockSpec((HEADS, per * cc, cc), lambda n: (0, n, 0))
    st = pl.BlockSpec((per, HEADS, HEAD_DIM, HEAD_DIM), lambda n: (n, 0, 0, 0))
    gl = pl.BlockSpec((per * cc, 128), lambda n: (n, 0))
    tw = jax.ShapeDtypeStruct((t, WIDTH), F32)
    return pl.pallas_call(
        body, name=name,
        out_shape=(tw, tw, tw, jax.ShapeDtypeStruct((t, 128), F32)),
        grid=(nc // per,),
        in_specs=[row(0), row(1), row(2), gl, row(0), row(0), row(0), row(0), mat, mat, st, st, row(0), row(0)],
        out_specs=(row(0), row(0), row(0), gl),
        compiler_params=_params(("parallel",)),
    )(act, act, act, gates, u, w, kd, qg, tinv, p, sh, dsh, dvn, do)


def _dn_post_fwd(o, gate, w, name):
    t = o.shape[0]
    tr = _tile(t, 512)

    def body(o_ref, g_ref, w_ref, y_ref):
        for h in range(HEADS):
            sl = slice(h * HEAD_DIM, (h + 1) * HEAD_DIM)
            ov, gv = o_ref[:, sl], g_ref[:, sl].astype(F32)
            r = lax.rsqrt(jnp.mean(ov * ov, axis=1, keepdims=True) + EPS)
            y_ref[:, sl] = (ov * r * w_ref[...] * (gv * _sigmoid(gv))).astype(BF16)

    blk = pl.BlockSpec((tr, WIDTH), lambda i: (i, 0))
    return pl.pallas_call(
        body, name=name,
        out_shape=jax.ShapeDtypeStruct((t, WIDTH), BF16),
        grid=(t // tr,),
        in_specs=[blk, blk, pl.BlockSpec((1, HEAD_DIM), lambda i: (0, 0))],
        out_specs=blk,
        compiler_params=_params(("parallel",)),
    )(o, gate, w)


def _dn_post_bwd(dy, o, gate, w, name):
    t = o.shape[0]
    tr = _tile(t, 512)

    def body(dy_ref, o_ref, g_ref, w_ref, do_ref, dg_ref, dw_ref):
        i = pl.program_id(0)

        @pl.when(i == 0)
        def _():
            dw_ref[...] = jnp.zeros_like(dw_ref)

        dw = jnp.zeros((1, HEAD_DIM), F32)
        for h in range(HEADS):
            sl = slice(h * HEAD_DIM, (h + 1) * HEAD_DIM)
            ov, gv, dyv = o_ref[:, sl], g_ref[:, sl].astype(F32), dy_ref[:, sl].astype(F32)
            r = lax.rsqrt(jnp.mean(ov * ov, axis=1, keepdims=True) + EPS)
            oh = ov * r
            sg = _sigmoid(gv)
            dg_ref[:, sl] = (dyv * oh * w_ref[...] * (sg * (1.0 + gv * (1.0 - sg)))).astype(BF16)
            dn = dyv * (gv * sg)
            doh = dn * w_ref[...]
            do_ref[:, sl] = r * (doh - oh * jnp.mean(doh * oh, axis=1, keepdims=True))
            dw = dw + jnp.sum(dn * oh, axis=0, keepdims=True)
        dw_ref[...] += dw

    blk = pl.BlockSpec((tr, WIDTH), lambda i: (i, 0))
    return pl.pallas_call(
        body, name=name,
        out_shape=(jax.ShapeDtypeStruct((t, WIDTH), F32), jax.ShapeDtypeStruct((t, WIDTH), BF16),
                   jax.ShapeDtypeStruct((1, HEAD_DIM), F32)),
        grid=(t // tr,),
        in_specs=[blk, blk, blk, pl.BlockSpec((1, HEAD_DIM), lambda i: (0, 0))],
        out_specs=(blk, blk, pl.BlockSpec((1, HEAD_DIM), lambda i: (0, 0))),
        compiler_params=_params(("arbitrary",)),
    )(dy, o, gate, w)


def _sb_scores(qs, k_ref, qi, it, carries, uincl):
    bk = ATT_BLOCK
    scale = HEAD_DIM ** -0.5
    heads, groups = range(len(qs)), range(SB_GROUP)
    lane = [slice(e * HEAD_DIM, (e + 1) * HEAD_DIM) for e in heads]
    js = [qi - SB_GROUP * it - g for g in groups]
    rows = [pl.ds(pl.multiple_of(jnp.maximum(j, 0) * bk, bk), bk) for j in js]
    qpos = qi * bk + _iota((bk, bk), 0)
    col = _iota((bk, bk), 1)
    mask1 = [jnp.logical_and(j * bk + col < qpos, j >= 0) for j in js]
    ks = [[k_ref[r, lane[e]] for r in rows] for e in heads]
    z = [[_dot(qs[e], k, NT) * scale for k in ks[e]] for e in heads]
    soft = [[_log1pexp_neg_abs(a) for a in ze] for ze in z]
    lk_full = [[-(jnp.maximum(a, 0.0) + s) for a, s in zip(z[e], soft[e])] for e in heads]
    lk = [[jnp.where(m, a, 0.0) for m, a in zip(mask1, lk_full[e])] for e in heads]
    ls = [[jnp.minimum(a, 0.0) - s for a, s in zip(z[e], soft[e])] for e in heads]
    incl = [[_dot_xr2(a, uincl, NN) for a in lk[e]] for e in heads]
    weights, out_carries = [], []
    for e in heads:
        cb, we = carries[e], []
        for g in groups:
            we.append(jnp.where(mask1[g], jnp.exp(ls[e][g] + (cb + incl[e][g] - lk[e][g])), 0.0))
            cb = cb + incl[e][g][:, 0:1]
        weights.append(we)
        out_carries.append(cb)
    return rows, ks, weights, mask1, lk_full, ls, out_carries


def _sb_more(qi, carry):
    it, cbs = carry[0], carry[1]
    live = jnp.max(cbs[0])
    for cb in cbs[1:]:
        live = jnp.maximum(live, jnp.max(cb))
    return jnp.logical_and(SB_GROUP * it <= qi, live > SB_LOG_ZERO)


def _sb_steps(groups, nq):
    def when():
        h, i = pl.program_id(0), pl.program_id(1)
        return (jnp.logical_and(h == 0, i == 0), jnp.logical_and(h == groups // 2, i == 0),
                jnp.logical_and(h == groups - 1, i == nq - 1))
    return when


def _sb_fwd(qkv, name, comm=None):
    t = qkv.shape[0]
    bk = ATT_BLOCK
    hp, wide = SB_HEADS_FWD, SB_HEADS_FWD * HEAD_DIM
    lane = [slice(e * HEAD_DIM, (e + 1) * HEAD_DIM) for e in range(hp)]

    def body(q_ref, k_ref, v_ref, o_ref):
        qi = pl.program_id(1)
        qs = [q_ref[:, s] for s in lane]
        uincl = jnp.where(_tri(bk, "lower"), 1.0, 0.0).astype(BF16)

        def step(carry):
            it, cbs, accs = carry
            rows, _, weights, _, _, _, cbs = _sb_scores(qs, k_ref, qi, it, cbs, uincl)
            accs = list(accs)
            for e in range(hp):
                for r, a in zip(rows, weights[e]):
                    accs[e] = accs[e] + _dot(a.astype(BF16), v_ref[r, lane[e]], NN)
            return it + 1, tuple(cbs), tuple(accs)

        init = (jnp.int32(0), (jnp.zeros((bk, 1), F32),) * hp, (jnp.zeros((bk, HEAD_DIM), F32),) * hp)
        _, _, accs = lax.while_loop(functools.partial(_sb_more, qi), step, init)
        for e in range(hp):
            o_ref[:, lane[e]] = accs[e]

    groups = HEADS // hp
    (o,), extra = _host_call(
        body, name, comm, _sb_steps(groups, t // bk), [jax.ShapeDtypeStruct((t, WIDTH), F32)], (groups, t // bk),
        [pl.BlockSpec((bk, wide), lambda h, i: (i, h)),
         pl.BlockSpec((t, wide), lambda h, i: (0, groups + h)),
         pl.BlockSpec((t, wide), lambda h, i: (0, 2 * groups + h))],
        [pl.BlockSpec((bk, wide), lambda h, i: (i, h))], [], ("parallel", "arbitrary"), (qkv, qkv, qkv))
    return o, extra


def _sb_bwd(qkv, o, do, name, comm=None):
    assert do.dtype == BF16
    t = qkv.shape[0]
    bk = ATT_BLOCK
    scale = HEAD_DIM ** -0.5
    hp, wide = SB_HEADS_BWD, SB_HEADS_BWD * HEAD_DIM
    lane = [slice(e * HEAD_DIM, (e + 1) * HEAD_DIM) for e in range(hp)]

    def body(q_ref, k_ref, v_ref, o_ref, do_ref, dq_ref, dk_out, dv_out, dk_ref, dv_ref):
        qi = pl.program_id(1)

        @pl.when(qi == 0)
        def _():
            dk_ref[...] = jnp.zeros_like(dk_ref)
            dv_ref[...] = jnp.zeros_like(dv_ref)

        heads, groups = range(hp), range(SB_GROUP)
        qs = [q_ref[:, s] for s in lane]
        dob = [do_ref[:, s] for s in lane]
        dsum = [jnp.sum(dob[e].astype(F32) * o_ref[:, lane[e]], axis=1, keepdims=True) for e in heads]
        uincl = jnp.where(_tri(bk, "lower"), 1.0, 0.0).astype(BF16)

        def step(carry):
            it, cbs, ces, dqs = carry
            rows, ks, weights, mask, lk_full, ls, cbs = _sb_scores(qs, k_ref, qi, it, cbs, uincl)
            ab = [[a.astype(BF16) for a in weights[e]] for e in heads]
            vs = [[v_ref[r, lane[e]] for r in rows] for e in heads]
            dla = [[ab[e][g].astype(F32) * _dot(dob[e], vs[e][g], NT) for g in groups] for e in heads]
            suf = [[_dot_xr2(a, uincl, NN) for a in dla[e]] for e in heads]
            ces, dqs = list(ces), list(dqs)
            for e in heads:
                for g in groups:
                    err = dsum[e] - (ces[e] + suf[e][g])
                    ces[e] = ces[e] + suf[e][g][:, 0:1]
                    dz = jnp.where(mask[g], dla[e][g] * jnp.exp(lk_full[e][g]) - err * jnp.exp(ls[e][g]), 0.0)
                    dzb = (dz * scale).astype(BF16)
                    dqs[e] = dqs[e] + _dot(dzb, ks[e][g], NN)
                    dk_ref[rows[g], lane[e]] += _dot(dzb, qs[e], TN)
                    dv_ref[rows[g], lane[e]] += _dot(ab[e][g], dob[e], TN)
            return it + 1, tuple(cbs), tuple(ces), tuple(dqs)

        zc = (jnp.zeros((bk, 1), F32),) * hp
        init = (jnp.int32(0), zc, zc, (jnp.zeros((bk, HEAD_DIM), F32),) * hp)
        dqs = lax.while_loop(functools.partial(_sb_more, qi), step, init)[3]
        for e in heads:
            dq_ref[:, lane[e]] = dqs[e].astype(BF16)

        @pl.when(qi == t // bk - 1)
        def _():
            dk_out[...] = dk_ref[...].astype(BF16)
            dv_out[...] = dv_ref[...].astype(BF16)

    ngroup = HEADS // hp
    tw = jax.ShapeDtypeStruct((t, WIDTH), BF16)
    qb = pl.BlockSpec((bk, wide), lambda h, i: (i, h))
    full = lambda off: pl.BlockSpec((t, wide), lambda h, i: (0, off + h))
    return _host_call(
        body, name, comm, _sb_steps(ngroup, t // bk), [tw, tw, tw], (ngroup, t // bk),
        [qb, full(ngroup), full(2 * ngroup), qb, qb], [qb, full(0), full(0)],
        [pltpu.VMEM((t, wide), F32), pltpu.VMEM((t, wide), F32)], ("parallel", "arbitrary"),
        (qkv, qkv, qkv, o, do))


def _merge_fwd(pd, ps, gl, name):
    t = pd.shape[0]
    tr, tc = _tile(t, 512), D_MODEL
    nj = D_MODEL // tc

    def body(pd_ref, ps_ref, gd_ref, gs_ref, o_ref):
        gd, gs = gd_ref[...].astype(F32), gs_ref[...].astype(F32)
        o_ref[...] = (_sigmoid(gd) * pd_ref[...].astype(F32) + _sigmoid(gs) * ps_ref[...].astype(F32)).astype(BF16)

    blk = lambda off: pl.BlockSpec((tr, tc), lambda i, j: (i, j + off))
    return pl.pallas_call(
        body, name=name,
        out_shape=jax.ShapeDtypeStruct((t, D_MODEL), BF16),
        grid=(t // tr, nj),
        in_specs=[blk(0), blk(0), blk(0), blk(nj)],
        out_specs=blk(0),
        compiler_params=_params(("parallel", "parallel")),
    )(pd, ps, gl, gl)


def _merge_bwd(dm, pd, ps, gl, name):
    t = pd.shape[0]
    tr, tc = _tile(t, 512), D_MODEL
    nj = D_MODEL // tc

    def body(dm_ref, pd_ref, ps_ref, gd_ref, gs_ref, dpd_ref, dps_ref, dgd_ref, dgs_ref):
        dmv = dm_ref[...].astype(F32)
        sd, ss = _sigmoid(gd_ref[...].astype(F32)), _sigmoid(gs_ref[...].astype(F32))
        dpd_ref[...] = (dmv * sd).astype(BF16)
        dps_ref[...] = (dmv * ss).astype(BF16)
        dgd_ref[...] = (dmv * pd_ref[...].astype(F32) * sd * (1.0 - sd)).astype(BF16)
        dgs_ref[...] = (dmv * ps_ref[...].astype(F32) * ss * (1.0 - ss)).astype(BF16)

    blk = lambda off: pl.BlockSpec((tr, tc), lambda i, j: (i, j + off))
    out = jax.ShapeDtypeStruct((t, D_MODEL), BF16)
    return pl.pallas_call(
        body, name=name,
        out_shape=(out, out, out, out),
        grid=(t // tr, nj),
        in_specs=[blk(0), blk(0), blk(0), blk(0), blk(nj)],
        out_specs=(blk(0), blk(0), blk(0), blk(0)),
        compiler_params=_params(("parallel", "parallel")),
    )(dm, pd, ps, gl, gl)


def _local_step(x, target, wts, plan=None, n1=None):
    if n1 is None:
        n1 = _rmsnorm_fwd(x, wts["norm1_w"], "norm1_fwd")
    qkv_pre = _matmul(n1, wts["w_dnqkv_t"], "nt", BF16, "in_dnqkv")
    hgate = _matmul(n1, wts["w_dngate_t"], "nt", BF16, "in_dngate")
    sbqkv = _matmul(n1, wts["w_sbqkv_t"], "nt", BF16, "in_sbqkv")
    gl = _matmul(n1, wts["w_gl_t"], "nt", BF16, "in_gl")
    hab = _matmul(n1, wts["w_ab_t"], "nt", F32, "in_ab")

    act = _dn_pre_fwd(qkv_pre, wts["dn_conv_w"], "dn_pre_fwd")
    gates = _dn_gates_fwd(hab, wts["alog"], wts["dtb"], "dn_gates_fwd")
    u, w, kd, qg, tinv, p = _dn_local_fwd(act, gates, "dn_local_fwd")
    o_dn, sh = _dn_scan_fwd(u, w, kd, qg, p, gates, "dn_scan_fwd")
    y_dn = _dn_post_fwd(o_dn, hgate, wts["dn_norm_w"], "dn_post_fwd")

    o_sb, late = _sb_fwd(sbqkv, "sb_fwd", comm=plan.late_gather() if plan else None)
    if plan:
        wts = {**wts, **plan.late_weights(late)}

    pd = _matmul(y_dn, wts["w_proj_dn"], "nn", BF16, "proj_dn")
    ps = _matmul(o_sb, wts["w_proj_sb"], "nn", BF16, "proj_sb")
    mixed = _merge_fwd(pd, ps, gl, "merge_fwd")
    x1 = _matmul(mixed, wts["w_out"], "nn", F32, "out_proj", add=x)

    n2 = _rmsnorm_fwd(x1, wts["norm2_w"], "norm2_fwd")
    upre = _matmul(n2, wts["ffn_w_up_t"], "nt", BF16, "ffn_up")
    fact = _ffn_act_fwd(upre, wts["ffn_conv_w"], "ffn_act_fwd")
    x2 = _matmul(fact, wts["ffn_w_down"], "nn", F32, "ffn_down", add=x1)

    dx2, g_normf, loss = _final_loss(x2, target, wts["norm_f_w"], "final_loss")

    dfact = _matmul(dx2, wts["ffn_w_down"], "nt", BF16, "ffn_down_dx")
    g_wdown = _matmul(fact, dx2, "tn", BF16, "ffn_down_dw")
    dgc, duc, dwg, dwu = _ffn_act_bwd(dfact, upre, wts["ffn_conv_w"], "ffn_act_bwd")
    g_fconv = jnp.concatenate([dwg, dwu], axis=1)
    dupre = _conv_bwd_data([dgc, duc], wts["ffn_conv_w"], FFN_CONV, BF16, "ffn_conv_bwd")
    dn2 = _matmul(dupre, wts["ffn_w_up_t"], "nn", F32, "ffn_up_dx")
    g_wup = _matmul(dupre, n2, "tn", BF16, "ffn_up_dw")
    dx1, g_norm2 = _rmsnorm_bwd(dn2, x1, wts["norm2_w"], dx2, "norm2_bwd")

    dmixed = _matmul(dx1, wts["w_out"], "nt", BF16, "out_proj_dx")
    g_wout = _matmul(mixed, dx1, "tn", BF16, "out_proj_dw")
    dpd, dps, dgd, dgs = _merge_bwd(dmixed, pd, ps, gl, "merge_bwd")
    dy_dn = _matmul(dpd, wts["w_proj_dn"], "nt", BF16, "proj_dn_dx")
    g_wpd = _matmul(y_dn, dpd, "tn", BF16, "proj_dn_dw")
    do_sb = _matmul(dps, wts["w_proj_sb"], "nt", BF16, "proj_sb_dx")
    g_wps = _matmul(o_sb, dps, "tn", BF16, "proj_sb_dw")
    grads = dict(w_proj_dn=g_wpd, w_proj_sb=g_wps, w_out=g_wout, ffn_w_up_t=g_wup, ffn_w_down=g_wdown)

    (dsq, dsk, dsv), got_early = _sb_bwd(sbqkv, o_sb, do_sb, "sb_bwd",
                                         comm=plan.early_grads(grads) if plan else None)

    do_dn, dhgate, g_dnnorm = _dn_post_bwd(dy_dn, o_dn, hgate, wts["dn_norm_w"], "dn_post_bwd")
    dvn, dsh = _dn_scan_bwd(do_dn, w, kd, qg, p, gates, "dn_scan_bwd")
    dq, dk, dv, dgates = _dn_local_bwd(act, gates, u, w, kd, qg, tinv, p, sh, dsh, dvn, do_dn, "dn_local_bwd")
    dhab, g_alog, g_dtb = _dn_gates_bwd(dgates, hab, wts["alog"], wts["dtb"], "dn_gates_bwd")
    dcv, g_dnconv = _dn_pre_bwd(dq, dk, dv, qkv_pre, wts["dn_conv_w"], "dn_pre_bwd")
    dqkv_pre = _conv_bwd_data([dcv], wts["dn_conv_w"], DN_CONV, BF16, "dn_conv_bwd")

    dh = jnp.concatenate([dqkv_pre, dhgate, dsq, dsk, dsv, dgd, dgs], axis=1)
    w_main_t = jnp.concatenate([wts["w_dnqkv_t"], wts["w_dngate_t"], wts["w_sbqkv_t"], wts["w_gl_t"]], axis=0)
    g_wmain = _matmul(dh, n1, "tn", BF16, "in_dw_main")
    g_wab = _matmul(dhab, n1, "tn", BF16, "in_dw_ab")
    grads.update(w_main_t=g_wmain, w_ab_t=g_wab, dn_conv_w=g_dnconv, alog=g_alog, dtb=g_dtb, dn_norm_w=g_dnnorm,
                 norm2_w=g_norm2, ffn_conv_w=g_fconv, norm_f_w=g_normf)
    got_late = []
    if plan:
        dn1, swapped = _matmul(dhab, wts["w_ab_t"], "nn", F32, "in_dx_ab", comm=plan.sibling_swap(grads))
        dn1, got_late = _matmul(dh, w_main_t, "nn", F32, "in_dx_main", add=dn1,
                                comm=plan.late_grads(swapped, grads, loss))
    else:
        dn1 = _matmul(dhab, wts["w_ab_t"], "nn", F32, "in_dx_ab")
        dn1 = _matmul(dh, w_main_t, "nn", F32, "in_dx_main", add=dn1)
    grad_x, g_norm1 = _rmsnorm_bwd(dn1, x, wts["norm1_w"], dx1, "norm1_bwd")
    grads["norm1_w"] = g_norm1
    return loss, grad_x, grads, got_early, got_late


HBM_SPEC = pl.BlockSpec(memory_space=pltpu.HBM)


def _mesh_pos():
    x, y, c = lax.axis_index("x"), lax.axis_index("y"), lax.axis_index("c")
    return x, y, c, 4 * x + 2 * y + c


def _peer(k):
    x, y, c, _ = _mesh_pos()
    px = 1 - x if k & 4 else x
    py = 1 - y if k & 2 else y
    pc = 1 - c if k & 1 else c
    return (px, py, pc), 4 * px + 2 * py + pc


def _rcopy(src, dst, send, recv, a, s, peer):
    return pltpu.make_async_remote_copy(src_ref=src, dst_ref=dst, send_sem=send.at[a, s], recv_sem=recv.at[a, s],
                                        device_id=peer, device_id_type=pl.DeviceIdType.MESH)


class _Gather:
    ICI = (2, 4, 6)

    def __init__(self, shards):
        self.args = list(shards)
        self.n = len(shards)
        self.out_shape = [jax.ShapeDtypeStruct((N_DEV,) + s.shape, s.dtype) for s in shards]
        self.scratch = [pltpu.SemaphoreType.DMA((self.n, N_DEV - 1)), pltpu.SemaphoreType.DMA((self.n, N_DEV - 1)),
                        pltpu.SemaphoreType.DMA((self.n,))]

    def _slot(self, outs, a, d):
        return outs[a].at[d]

    def _first(self, ins, outs, send, recv, a):
        me = _mesh_pos()[3]
        out, got = [], []
        for s, k in enumerate((1,) + self.ICI):
            peer, pidx = _peer(k)
            out.append(_rcopy(ins[a], self._slot(outs, a, me), send, recv, a, s, peer))
            got.append(_rcopy(ins[a], self._slot(outs, a, pidx), send, recv, a, s, peer))
        return out, got

    def _forward(self, ins, outs, send, recv, a):
        sib = _peer(1)[0]
        out, got = [], []
        for s, k in enumerate(self.ICI):
            held = self._slot(outs, a, _peer(k)[1])
            out.append(_rcopy(held, held, send, recv, a, 4 + s, sib))
            other = self._slot(outs, a, _peer(k | 1)[1])
            got.append(_rcopy(other, other, send, recv, a, 4 + s, sib))
        return out, got

    def start(self, ins, outs, sems):
        send, recv, loc = sems
        me = _mesh_pos()[3]
        for a in range(self.n):
            pltpu.make_async_copy(ins[a], self._slot(outs, a, me), loc.at[a]).start()
            for cp in self._first(ins, outs, send, recv, a)[0]:
                cp.start()

    def mid(self, ins, outs, sems):
        send, recv, _ = sems
        for a in range(self.n):
            arrivals = self._first(ins, outs, send, recv, a)[1]
            for s, cp in enumerate(self._forward(ins, outs, send, recv, a)[0]):
                arrivals[1 + s].wait_recv()
                cp.start()

    def finish(self, ins, outs, sems):
        send, recv, loc = sems
        me = _mesh_pos()[3]
        for a in range(self.n):
            first_out, first_got = self._first(ins, outs, send, recv, a)
            fwd_out, fwd_got = self._forward(ins, outs, send, recv, a)
            first_got[0].wait_recv()
            for cp in fwd_got:
                cp.wait_recv()
            for cp in first_out + fwd_out:
                cp.wait_send()
            pltpu.make_async_copy(ins[a], self._slot(outs, a, me), loc.at[a]).wait()


class _Exchange:
    def __init__(self, slabs=(), gathered=(), chip_slabs=(), sibling_slabs=()):
        self.args = list(slabs) + list(chip_slabs) + list(sibling_slabs) + list(gathered)
        self.kind = (["dev"] * len(slabs) + ["chip"] * len(chip_slabs) + ["sib"] * len(sibling_slabs)
                     + ["all"] * len(gathered))
        self.n = len(self.args)
        half = lambda s: jax.ShapeDtypeStruct((N_DEV // 2,) + s.shape[1:], s.dtype)
        self.out_shape = ([jax.ShapeDtypeStruct(s.shape, s.dtype) for s in slabs]
                          + [half(s) for s in chip_slabs] + [half(s) for s in sibling_slabs]
                          + [jax.ShapeDtypeStruct((N_DEV,) + s.shape, s.dtype) for s in gathered])
        self.scratch = [pltpu.SemaphoreType.DMA((self.n, N_DEV - 1)), pltpu.SemaphoreType.DMA((self.n, N_DEV - 1)),
                        pltpu.SemaphoreType.DMA((self.n,))]

    def _copies(self, ins, outs, send, recv, a):
        x, y, c, me = _mesh_pos()
        kind = self.kind[a]
        out, got = [], []
        if kind == "sib":
            sib = _peer(1)[0]
            for q in range(N_DEV // 2):
                out.append(_rcopy(ins[a].at[2 * q + 1 - c], outs[a].at[q], send, recv, a, q, sib))
                got.append(_rcopy(ins[a].at[2 * q + c], outs[a].at[q], send, recv, a, q, sib))
            return out, got
        for k in ((2, 4, 6) if kind == "chip" else range(1, N_DEV)):
            peer, pidx = _peer(k)
            if kind == "chip":
                src, mine, theirs = ins[a].at[2 * peer[0] + peer[1]], 2 * x + y, 2 * peer[0] + peer[1]
            else:
                src, mine, theirs = (ins[a].at[pidx] if kind == "dev" else ins[a]), me, pidx
            out.append(_rcopy(src, outs[a].at[mine], send, recv, a, k - 1, peer))
            got.append(_rcopy(src, outs[a].at[theirs], send, recv, a, k - 1, peer))
        return out, got

    def _local(self, ins, outs, loc, a):
        x, y, _, me = _mesh_pos()
        kind = self.kind[a]
        if kind == "sib":
            return None
        if kind == "chip":
            return pltpu.make_async_copy(ins[a].at[2 * x + y], outs[a].at[2 * x + y], loc.at[a])
        return pltpu.make_async_copy(ins[a].at[me] if kind == "dev" else ins[a], outs[a].at[me], loc.at[a])

    def start(self, ins, outs, sems):
        send, recv, loc = sems
        for a in range(self.n):
            if self._local(ins, outs, loc, a) is not None:
                self._local(ins, outs, loc, a).start()
            for cp in self._copies(ins, outs, send, recv, a)[0]:
                cp.start()

    def mid(self, ins, outs, sems):
        pass

    def finish(self, ins, outs, sems):
        send, recv, loc = sems
        for a in range(self.n):
            out, got = self._copies(ins, outs, send, recv, a)
            for cp in got:
                cp.wait_recv()
            for cp in out:
                cp.wait_send()
            if self._local(ins, outs, loc, a) is not None:
                self._local(ins, outs, loc, a).wait()


def _comm_call(comm, name):
    n = comm.n

    def body(*refs):
        ins, outs, sems = refs[:n], refs[n:2 * n], refs[2 * n:]
        comm.start(ins, outs, sems)
        comm.mid(ins, outs, sems)
        comm.finish(ins, outs, sems)

    return pl.pallas_call(
        body, name=name, out_shape=comm.out_shape, in_specs=[HBM_SPEC] * n, out_specs=[HBM_SPEC] * n,
        scratch_shapes=comm.scratch,
    )(*comm.args)


def _hosted(body, comm, n_in, n_out, when):
    if comm is None:
        return body

    def wrapped(*refs):
        ins, c_ins = refs[:n_in], refs[n_in:n_in + comm.n]
        o0 = n_in + comm.n
        outs, c_outs = refs[o0:o0 + n_out], refs[o0 + n_out:o0 + n_out + comm.n]
        scratch, sems = refs[o0 + n_out + comm.n:len(refs) - 3], refs[len(refs) - 3:]
        first, middle, last = when()

        @pl.when(first)
        def _():
            comm.start(c_ins, c_outs, sems)

        body(*ins, *outs, *scratch)

        @pl.when(middle)
        def _():
            comm.mid(c_ins, c_outs, sems)

        @pl.when(last)
        def _():
            comm.finish(c_ins, c_outs, sems)

    return wrapped


def _host_call(body, name, comm, when, out_shape, grid, in_specs, out_specs, scratch_shapes, sem, args):
    n_in, n_out = len(in_specs), len(out_specs)
    if comm is None:
        res = pl.pallas_call(body, name=name, out_shape=out_shape, grid=grid, in_specs=in_specs, out_specs=out_specs,
                             scratch_shapes=scratch_shapes, compiler_params=_params(sem))(*args)
        return list(res), []
    res = pl.pallas_call(
        _hosted(body, comm, n_in, n_out, when), name=name,
        out_shape=list(out_shape) + comm.out_shape, grid=grid,
        in_specs=list(in_specs) + [HBM_SPEC] * comm.n, out_specs=list(out_specs) + [HBM_SPEC] * comm.n,
        scratch_shapes=list(scratch_shapes) + comm.scratch,
        compiler_params=_params(("arbitrary",) * len(grid)),
    )(*args, *comm.args)
    return list(res[:n_out]), list(res[n_out:])


def _add_my_slabs(slabs, b, name):
    n, rows, cols = b.shape
    tc = _tile(cols, 256)

    def body(a_ref, b_ref, o_ref):
        o_ref[...] = (a_ref[...].astype(F32) + b_ref[...].astype(F32)).astype(o_ref.dtype)

    blk = pl.BlockSpec((None, rows, tc), lambda i, j: (i, 0, j))
    mine = pl.BlockSpec((None, rows, tc), lambda i, j: (2 * i + lax.axis_index("c"), 0, j))
    return pl.pallas_call(
        body, name=name, out_shape=jax.ShapeDtypeStruct(b.shape, b.dtype), grid=(n, cols // tc),
        in_specs=[mine, blk], out_specs=blk, compiler_params=_params(("parallel", "parallel")),
    )(slabs, b)


def _adamw(parts, w, m, v, name):
    rows, cols = w.shape
    nparts = parts.shape[0]
    tr, tc = rows, cols
    for cand in (128, 176):
        if rows > cand and rows % cand == 0:
            tr = cand
            break
    if tr == rows and rows > 512:
        tc = _tile(cols, 256)

    def body(p_ref, w_ref, m_ref, v_ref, g_ref, d_ref, mo_ref, vo_ref):
        g = p_ref[0].astype(F32)
        for s in range(1, nparts):
            g = g + p_ref[s].astype(F32)
        mn = ADAM_B1 * m_ref[...] + (1.0 - ADAM_B1) * g
        vn = ADAM_B2 * v_ref[...] + (1.0 - ADAM_B2) * (g * g)
        m_hat = mn / (1.0 - ADAM_B1 ** ADAM_STEP)
        v_hat = vn / (1.0 - ADAM_B2 ** ADAM_STEP)
        g_ref[...] = g
        d_ref[...] = -ADAM_LR * (m_hat / (jnp.sqrt(v_hat) + ADAM_EPS) + ADAM_WD * w_ref[...])
        mo_ref[...] = mn
        vo_ref[...] = vn

    blk = pl.BlockSpec((tr, tc), lambda i, j: (i, j))
    out = jax.ShapeDtypeStruct((rows, cols), F32)
    return pl.pallas_call(
        body, name=name,
        out_shape=(out, out, out, out),
        grid=(rows // tr, cols // tc),
        in_specs=[pl.BlockSpec((nparts, tr, tc), lambda i, j: (0, i, j)), blk, blk, blk],
        out_specs=(blk, blk, blk, blk),
        compiler_params=_params(("parallel", "parallel")),
    )(parts, w, m, v)


CONV_PACK = 8 * 1024
WEIGHT_ORDER = ("norm1_w", "w_in", "dn_conv_w", "dn_A_log", "dn_dt_bias", "dn_norm_w", "w_proj_dn", "w_proj_sb",
                "w_out", "norm2_w", "ffn_w_up", "ffn_conv_w", "ffn_w_down", "norm_f_w")


def _cols_to_slabs(g):
    r, c8 = g.shape
    return g.reshape(r, N_DEV, c8 // N_DEV).transpose(1, 0, 2)


def _slabs_to_cols(s):
    d, r, c = s.shape
    return s.transpose(1, 0, 2).reshape(r, d * c)


def kernel(x, norm1_w, w_in, dn_conv_w, dn_A_log, dn_dt_bias, dn_norm_w, w_proj_dn, w_proj_sb, w_out, norm2_w, ffn_w_up, ffn_conv_w, ffn_w_down, norm_f_w, loss_target, m_norm1_w, m_w_in, m_dn_conv_w, m_dn_A_log, m_dn_dt_bias, m_dn_norm_w, m_w_proj_dn, m_w_proj_sb, m_w_out, m_norm2_w, m_ffn_w_up, m_ffn_conv_w, m_ffn_w_down, m_norm_f_w, v_norm1_w, v_w_in, v_dn_conv_w, v_dn_A_log, v_dn_dt_bias, v_dn_norm_w, v_w_proj_dn, v_w_proj_sb, v_w_out, v_norm2_w, v_ffn_w_up, v_ffn_conv_w, v_ffn_w_down, v_norm_f_w):
    me = _mesh_pos()[3]
    tr = lambda a: jnp.transpose(a[0])
    w_loc = dict(norm1_w=norm1_w, w_in=tr(w_in), dn_conv_w=dn_conv_w[0], dn_A_log=dn_A_log, dn_dt_bias=dn_dt_bias,
                 dn_norm_w=dn_norm_w, w_proj_dn=w_proj_dn[0], w_proj_sb=w_proj_sb[0], w_out=w_out[0],
                 norm2_w=norm2_w, ffn_w_up=tr(ffn_w_up), ffn_conv_w=ffn_conv_w[0], ffn_w_down=ffn_w_down[0],
                 norm_f_w=norm_f_w[None, :])
    m_loc = dict(norm1_w=m_norm1_w, w_in=tr(m_w_in), dn_conv_w=m_dn_conv_w[0], dn_A_log=m_dn_A_log,
                 dn_dt_bias=m_dn_dt_bias, dn_norm_w=m_dn_norm_w, w_proj_dn=m_w_proj_dn[0], w_proj_sb=m_w_proj_sb[0],
                 w_out=m_w_out[0], norm2_w=m_norm2_w, ffn_w_up=tr(m_ffn_w_up), ffn_conv_w=m_ffn_conv_w[0],
                 ffn_w_down=m_ffn_w_down[0], norm_f_w=m_norm_f_w[None, :])
    v_loc = dict(norm1_w=v_norm1_w, w_in=tr(v_w_in), dn_conv_w=v_dn_conv_w[0], dn_A_log=v_dn_A_log,
                 dn_dt_bias=v_dn_dt_bias, dn_norm_w=v_dn_norm_w, w_proj_dn=v_w_proj_dn[0], w_proj_sb=v_w_proj_sb[0],
                 w_out=v_w_out[0], norm2_w=v_norm2_w, ffn_w_up=tr(v_ffn_w_up), ffn_conv_w=v_ffn_conv_w[0],
                 ffn_w_down=v_ffn_w_down[0], norm_f_w=v_norm_f_w[None, :])

    conv_flat = jnp.concatenate([w_loc["dn_conv_w"].reshape(-1), w_loc["ffn_conv_w"].reshape(-1)])
    n_dn, n_ffn = DN_CONV * 3 * WIDTH // N_DEV, FFN_CONV * 2 * D_FF // N_DEV
    conv_pack = jnp.pad(conv_flat, (0, CONV_PACK - n_dn - n_ffn)).reshape(8, 1024)
    n1, (g_in, g_conv) = _rmsnorm_fwd(x[0], norm1_w, "norm1_fwd",
                                      comm=_Gather([w_loc["w_in"].astype(BF16), conv_pack]))
    in_width = g_in.shape[0] * g_in.shape[1]
    w_in_t = g_in.reshape(in_width, D_MODEL)
    g_conv = g_conv.reshape(N_DEV, CONV_PACK)
    dn_conv_full = _slabs_to_cols(g_conv[:, :n_dn].reshape(N_DEV, DN_CONV, 3 * WIDTH // N_DEV))
    ffn_conv_full = _slabs_to_cols(g_conv[:, n_dn:n_dn + n_ffn].reshape(N_DEV, FFN_CONV, 2 * D_FF // N_DEV))
    q_end = 3 * WIDTH
    ab_end = q_end + 2 * HEADS
    gate_end = ab_end + WIDTH
    sb_end = gate_end + 3 * WIDTH
    pad_lanes = lambda a: jnp.pad(a, ((0, 0), (0, 128 - a.shape[1])))
    wts = dict(
        norm1_w=norm1_w, w_dnqkv_t=w_in_t[:q_end], w_ab_t=jnp.pad(w_in_t[q_end:ab_end], ((0, 128 - 2 * HEADS), (0, 0))),
        w_dngate_t=w_in_t[ab_end:gate_end], w_sbqkv_t=w_in_t[gate_end:sb_end], w_gl_t=w_in_t[sb_end:],
        dn_conv_w=dn_conv_full, alog=pad_lanes(dn_A_log), dtb=pad_lanes(dn_dt_bias), dn_norm_w=dn_norm_w,
        norm2_w=norm2_w, ffn_conv_w=ffn_conv_full, norm_f_w=norm_f_w[None, :])

    n_fc = FFN_CONV * 2 * D_FF
    fc_rows = -(-n_fc // D_MODEL)
    dn_rows = DN_CONV * 3 * WIDTH // D_MODEL
    late_names = ("w_proj_dn", "w_proj_sb", "w_out", "ffn_w_up", "ffn_w_down")

    class Plan:
        @staticmethod
        def late_gather():
            return _Gather([w_loc[k].astype(BF16) for k in late_names])

        @staticmethod
        def late_weights(got):
            g_pd, g_ps, g_out, g_up, g_down = got
            return dict(w_proj_dn=g_pd.reshape(WIDTH, D_MODEL), w_proj_sb=g_ps.reshape(WIDTH, D_MODEL),
                        w_out=g_out.reshape(D_MODEL, D_MODEL), ffn_w_up_t=g_up.reshape(2 * D_FF, D_MODEL),
                        ffn_w_down=g_down.reshape(D_FF, D_MODEL))

        @staticmethod
        def early_grads(g):
            return _Exchange([g["w_proj_dn"].reshape(N_DEV, WIDTH // N_DEV, D_MODEL),
                              g["w_proj_sb"].reshape(N_DEV, WIDTH // N_DEV, D_MODEL),
                              g["w_out"].reshape(N_DEV, D_MODEL // N_DEV, D_MODEL),
                              g["ffn_w_up_t"].reshape(N_DEV, 2 * D_FF // N_DEV, D_MODEL),
                              g["ffn_w_down"].reshape(N_DEV, D_FF // N_DEV, D_MODEL)])

        @staticmethod
        def _in_slabs(g):
            g_win_t = jnp.concatenate([g["w_main_t"][:q_end], g["w_ab_t"][:2 * HEADS], g["w_main_t"][q_end:]],
                                      axis=0)
            return g_win_t.reshape(N_DEV, in_width // N_DEV, D_MODEL)

        @staticmethod
        def sibling_swap(g):
            return _Exchange(sibling_slabs=[Plan._in_slabs(g)])

        @staticmethod
        def late_grads(swapped, g, loss):
            chip_sums = _add_my_slabs(Plan._in_slabs(g), swapped[0], "in_dw_chip_sum")
            row3 = jnp.concatenate([g["dn_norm_w"], g["alog"], g["dtb"], jnp.pad(loss, ((0, 0), (0, 127))),
                                    jnp.zeros((1, D_MODEL - 512), F32)], axis=1)
            fconv_rows = jnp.pad(g["ffn_conv_w"].reshape(-1), (0, fc_rows * D_MODEL - n_fc)).reshape(fc_rows, D_MODEL)
            pad8 = lambda a: jnp.pad(a, ((0, -a.shape[0] % 8), (0, 0)))
            pieces = [g["norm2_w"], g["norm_f_w"], row3, g["dn_conv_w"].reshape(dn_rows, D_MODEL), fconv_rows]
            small = jnp.concatenate([pad8(a) for a in pieces], axis=0)
            assert small.shape[0] == SMALL_ROWS
            return _Exchange(chip_slabs=[chip_sums], gathered=[small])

    loss, grad_x, g, got_early, got_late = _local_step(x[0], loss_target[0], wts, Plan, n1)
    r_pd, r_ps, r_out, r_up, r_down = got_early
    r_in, r_small = got_late
    (r_norm1,) = _comm_call(_Exchange([], [jnp.pad(g["norm1_w"], ((0, 7), (0, 0)))]), "gather_norm1")

    parts = dict(w_in=r_in, w_proj_dn=r_pd, w_proj_sb=r_ps, w_out=r_out, ffn_w_up=r_up, ffn_w_down=r_down)
    parts["norm1_w"] = r_norm1[:, 0:1, :]
    parts["norm2_w"] = r_small[:, 0:1, :]
    parts["norm_f_w"] = r_small[:, 8:9, :]
    parts["dn_norm_w"] = r_small[:, 16:17, 0:HEAD_DIM]
    parts["dn_A_log"] = r_small[:, 16:17, 128:128 + HEADS]
    parts["dn_dt_bias"] = r_small[:, 16:17, 256:256 + HEADS]
    dnc = r_small[:, 24:24 + dn_rows, :].reshape(N_DEV, DN_CONV, 3 * WIDTH)
    parts["dn_conv_w"] = lax.dynamic_slice_in_dim(dnc, me * (3 * WIDTH // N_DEV), 3 * WIDTH // N_DEV, axis=2)
    fc0 = 24 + dn_rows + (-dn_rows % 8)
    fcc = r_small[:, fc0:fc0 + fc_rows, :].reshape(N_DEV, fc_rows * D_MODEL)[:, :n_fc]
    fcc = fcc.reshape(N_DEV, FFN_CONV, 2 * D_FF)
    parts["ffn_conv_w"] = lax.dynamic_slice_in_dim(fcc, me * (2 * D_FF // N_DEV), 2 * D_FF // N_DEV, axis=2)
    loss_total = jnp.sum(r_small[:, 16, 384])

    res = {k: _adamw(parts[k], w_loc[k], m_loc[k], v_loc[k], "adamw_" + k) for k in WEIGHT_ORDER}
    lead = ("w_in", "dn_conv_w", "w_proj_dn", "w_proj_sb", "w_out", "ffn_w_up", "ffn_conv_w", "ffn_w_down")

    def shaped(k, a):
        if k in ("w_in", "ffn_w_up"):
            return jnp.transpose(a)[None]
        if k in lead:
            return a[None]
        if k == "norm_f_w":
            return a[0]
        return a

    outs = [loss_total, grad_x[None]]
    for idx in range(4):
        outs += [shaped(k, res[k][idx]) for k in WEIGHT_ORDER]
    return tuple(outs)
```
